```python
import jax, jax.numpy as jnp
from jax import lax
import numpy as np

D_MODEL = 1024
BATCH = 8
SEQ = 2048
DEPTH = 2

CHUNK = 64
N_EVEN = (DEPTH + 1) // 2
N_ODD = DEPTH // 2
GROUP_WIDTH = D_MODEL // 2
D_FF = 4 * D_MODEL
NORM_EPS = 1e-6

GLA_HEADS = 4
GLA_DV = GROUP_WIDTH // GLA_HEADS
GLA_DK = GLA_DV // 2
GLA_KW = GLA_HEADS * GLA_DK
GLA_RANK = 16
GLA_GATE_TAU = 16.0

FOX_HEAD_DIM = 64
FOX_HEADS = GROUP_WIDTH // FOX_HEAD_DIM
FOX_BLOCK = 128

CA_HEAD_DIM = 64
CA_HEADS = GROUP_WIDTH // CA_HEAD_DIM
CA_LEFT_CHUNKS = 8
CA_BAND = (CA_LEFT_CHUNKS + 1) * CHUNK
REL_CLIP = 128

LRU_WIDTH = GROUP_WIDTH
LRU_BLOCKS = 8
LRU_BLOCK_DIM = LRU_WIDTH // LRU_BLOCKS
CONV_WIDTH = 4
LRU_C = 8.0

EVEN_SIZES = (GLA_KW, GLA_KW, GROUP_WIDTH, GROUP_WIDTH, GLA_RANK,
              GROUP_WIDTH, GROUP_WIDTH, GROUP_WIDTH, FOX_HEADS)
EVEN_IN = sum(EVEN_SIZES)
ODD_SIZES = (GROUP_WIDTH, GROUP_WIDTH, GROUP_WIDTH, LRU_WIDTH, LRU_WIDTH)
ODD_IN = sum(ODD_SIZES)

kernel_name = 'hybrid_gla_fox_chunkattn_rglru_trunk'


def _split(a, sizes):
    return jnp.split(a, [int(s) for s in np.cumsum(sizes)[:-1]], axis=-1)


def rmsnorm(x, w):
    xf = x.astype(jnp.float32)
    y = xf * lax.rsqrt(jnp.mean(xf * xf, axis=-1, keepdims=True) + NORM_EPS)
    return (y * w.astype(jnp.float32)).astype(x.dtype)


def gla_mixer(q, k, v, r, a_low, w_a_up, b_a, norm_w):
    B, T, _ = q.shape
    nc = T // CHUNK
    f32 = jnp.float32
    qc = q.reshape(B, nc, CHUNK, GLA_HEADS, GLA_DK).astype(f32) * (GLA_DK ** -0.5)
    kc = k.reshape(B, nc, CHUNK, GLA_HEADS, GLA_DK).astype(f32)
    vc = v.reshape(B, nc, CHUNK, GLA_HEADS, GLA_DV).astype(f32)
    log_a = jax.nn.log_sigmoid((a_low @ w_a_up + b_a).astype(f32)) / GLA_GATE_TAU
    log_a = log_a.reshape(B, nc, CHUNK, GLA_HEADS, GLA_DK)
    cum = jnp.cumsum(log_a, axis=2)
    total = cum[:, :, -1]
    k_dec = kc * jnp.exp(total[:, :, None] - cum)
    inc = jnp.einsum('bcshk,bcshv->bchkv', k_dec, vc)

    def step(state, inp):
        dec, add = inp
        state = dec[..., None] * state + add
        return state, state

    init = jnp.zeros((B, GLA_HEADS, GLA_DK, GLA_DV), f32)
    _, states = lax.scan(step, init, (jnp.moveaxis(jnp.exp(total), 1, 0), jnp.moveaxis(inc, 1, 0)))
    states = jnp.moveaxis(states, 0, 1)
    o = jnp.einsum('bcthk,bchkv->bcthv', qc, states).reshape(B, T, GLA_HEADS, GLA_DV)
    o = o * lax.rsqrt(jnp.mean(o * o, axis=-1, keepdims=True) + NORM_EPS)
    o = o.reshape(B, T, GROUP_WIDTH) * norm_w.astype(f32)
    return (o * jax.nn.silu(r.astype(f32))).astype(q.dtype)


def fox_mixer(q, k, v, f_logit):
    B, T, _ = q.shape
    f32 = jnp.float32
    qh = q.reshape(B, T, FOX_HEADS, FOX_HEAD_DIM)
    kh = k.reshape(B, T, FOX_HEADS, FOX_HEAD_DIM)
    vh = v.reshape(B, T, FOX_HEADS, FOX_HEAD_DIM)
    cum = jnp.cumsum(jax.nn.log_sigmoid(f_logit.astype(f32)), axis=1).transpose(0, 2, 1)
    scale = FOX_HEAD_DIM ** -0.5
    neg = jnp.finfo(f32).min
    outs = []
    for blk in range(T // FOX_BLOCK):
        q0 = blk * FOX_BLOCK
        q1 = q0 + FOX_BLOCK
        s = jnp.einsum('bqhd,bkhd->bhqk', qh[:, q0:q1], kh[:, :q1]).astype(f32) * scale
        s = s + (cum[:, :, q0:q1, None] - cum[:, :, None, :q1])
        mask = (q0 + jnp.arange(FOX_BLOCK))[:, None] >= jnp.arange(q1)[None, :]
        p = jax.nn.softmax(jnp.where(mask, s, neg), axis=-1)
        outs.append(jnp.einsum('bhqk,bkhd->bqhd', p.astype(v.dtype), vh[:, :q1]))
    return jnp.concatenate(outs, axis=1).reshape(B, T, GROUP_WIDTH)


def chunk_rel_attention(q, k, v, rel_bias):
    B, T, _ = q.shape
    nc = T // CHUNK
    f32 = jnp.float32
    pad = CA_LEFT_CHUNKS * CHUNK
    qc = q.reshape(B, nc, CHUNK, CA_HEADS, CA_HEAD_DIM)
    kp = jnp.pad(k, ((0, 0), (pad, 0), (0, 0))).reshape(B, nc + CA_LEFT_CHUNKS, CHUNK, CA_HEADS, CA_HEAD_DIM)
    vp = jnp.pad(v, ((0, 0), (pad, 0), (0, 0))).reshape(B, nc + CA_LEFT_CHUNKS, CHUNK, CA_HEADS, CA_HEAD_DIM)
    k_band = jnp.concatenate([kp[:, j:j + nc] for j in range(CA_LEFT_CHUNKS + 1)], axis=2)
    v_band = jnp.concatenate([vp[:, j:j + nc] for j in range(CA_LEFT_CHUNKS + 1)], axis=2)
    s = jnp.einsum('bcqhd,bckhd->bchqk', qc, k_band).astype(f32) * (CA_HEAD_DIM ** -0.5)
    qi = jnp.arange(CHUNK)
    kj = jnp.arange(CA_BAND)
    rel = jnp.clip(pad + qi[:, None] - kj[None, :], -REL_CLIP, REL_CLIP) + REL_CLIP
    bias = rel_bias.astype(f32)[:, rel]
    key_pos = jnp.arange(nc)[:, None] * CHUNK - pad + kj[None, :]
    valid = (key_pos >= 0)[None, :, None, None, :]
    s = jnp.where(valid, s + bias[None, None], jnp.finfo(f32).min)
    p = jax.nn.softmax(s, axis=-1)
    o = jnp.einsum('bchqk,bckhd->bcqhd', p.astype(v.dtype), v_band)
    return o.reshape(B, T, GROUP_WIDTH)


def rglru_mixer(gate_in, x_in, conv_w, conv_b, w_a, b_a, w_x, b_x, lam):
    B, T, W = x_in.shape
    f32 = jnp.float32
    xc = lax.conv_general_dilated(x_in, conv_w[:, None, :], window_strides=(1,),
                                  padding=[(CONV_WIDTH - 1, 0)],
                                  dimension_numbers=('NWC', 'WIO', 'NWC'),
                                  feature_group_count=W) + conv_b
    xb = xc.reshape(B, T, LRU_BLOCKS, LRU_BLOCK_DIM)
    r = jax.nn.sigmoid((jnp.einsum('btnd,nde->btne', xb, w_a).reshape(B, T, W) + b_a).astype(f32))
    i = jax.nn.sigmoid((jnp.einsum('btnd,nde->btne', xb, w_x).reshape(B, T, W) + b_x).astype(f32))
    log_a = LRU_C * r * jax.nn.log_sigmoid(lam.astype(f32))
    a = jnp.exp(log_a)
    b = jnp.sqrt(-jnp.expm1(2.0 * log_a)) * (i * xc.astype(f32))

    def combine(left, right):
        a_l, b_l = left
        a_r, b_r = right
        return a_l * a_r, a_r * b_l + b_r

    _, h = lax.associative_scan(combine, (a, b), axis=1)
    return (h * jax.nn.gelu(gate_in.astype(f32))).astype(x_in.dtype)


def sq_relu_mlp(h, w_up, w_down):
    return jnp.square(jax.nn.relu(h @ w_up)) @ w_down


def _fwd_setup_inputs(seed: int = 0) -> dict:
    key = jax.random.key(seed)
    ks = jax.random.split(key, 24)
    nrm = jax.random.normal
    f32 = jnp.float32
    x = nrm(ks[0], (BATCH, SEQ, D_MODEL), f32)
    norm_w = 1.0 + 0.05 * nrm(ks[1], (DEPTH, 4, D_MODEL), f32)
    w_in_even = nrm(ks[2], (N_EVEN, D_MODEL, EVEN_IN), f32) * D_MODEL ** -0.5
    gla_w_a_up = nrm(ks[3], (N_EVEN, GLA_RANK, GLA_KW), f32) * GLA_RANK ** -0.5
    gla_b_a = jax.random.uniform(ks[4], (N_EVEN, GLA_KW), f32, 0.0, 2.0)
    gla_norm_w = 1.0 + 0.05 * nrm(ks[5], (N_EVEN, GROUP_WIDTH), f32)
    fox_b_f = jax.random.uniform(ks[6], (N_EVEN, FOX_HEADS), f32, 1.0, 4.0)
    w_out_even = nrm(ks[7], (N_EVEN, 2 * GROUP_WIDTH, D_MODEL), f32) * (2 * GROUP_WIDTH) ** -0.5
    w_in_odd = nrm(ks[8], (N_ODD, D_MODEL, ODD_IN), f32) * D_MODEL ** -0.5
    rel_bias = 0.5 * nrm(ks[9], (N_ODD, CA_HEADS, 2 * REL_CLIP + 1), f32)
    conv_w = nrm(ks[10], (N_ODD, CONV_WIDTH, LRU_WIDTH), f32) * CONV_WIDTH ** -0.5
    conv_b = 0.01 * nrm(ks[11], (N_ODD, LRU_WIDTH), f32)
    lru_w_a = nrm(ks[12], (N_ODD, LRU_BLOCKS, LRU_BLOCK_DIM, LRU_BLOCK_DIM), f32) * LRU_BLOCK_DIM ** -0.5
    lru_b_a = 0.1 * nrm(ks[13], (N_ODD, LRU_WIDTH), f32)
    lru_w_x = nrm(ks[14], (N_ODD, LRU_BLOCKS, LRU_BLOCK_DIM, LRU_BLOCK_DIM), f32) * LRU_BLOCK_DIM ** -0.5
    lru_b_x = 0.1 * nrm(ks[15], (N_ODD, LRU_WIDTH), f32)
    a_c = jax.random.uniform(ks[16], (N_ODD, LRU_WIDTH), f32, 0.9, 0.999)
    a_base = a_c ** (1.0 / LRU_C)
    lru_lambda = jnp.log(a_base) - jnp.log1p(-a_base)
    w_out_odd = nrm(ks[17], (N_ODD, 2 * GROUP_WIDTH, D_MODEL), f32) * (2 * GROUP_WIDTH) ** -0.5
    w_mlp_up = nrm(ks[18], (DEPTH, D_MODEL, D_FF), f32) * D_MODEL ** -0.5
    w_mlp_down = nrm(ks[19], (DEPTH, D_FF, D_MODEL), f32) * D_FF ** -0.5
    return {'x': x, 'norm_w': norm_w, 'w_in_even': w_in_even, 'gla_w_a_up': gla_w_a_up,
            'gla_b_a': gla_b_a, 'gla_norm_w': gla_norm_w, 'fox_b_f': fox_b_f,
            'w_out_even': w_out_even, 'w_in_odd': w_in_odd, 'rel_bias': rel_bias,
            'conv_w': conv_w, 'conv_b': conv_b, 'lru_w_a': lru_w_a, 'lru_b_a': lru_b_a,
            'lru_w_x': lru_w_x, 'lru_b_x': lru_b_x, 'lru_lambda': lru_lambda,
            'w_out_odd': w_out_odd, 'w_mlp_up': w_mlp_up, 'w_mlp_down': w_mlp_down}


def _fwd_reference(x, norm_w, w_in_even, gla_w_a_up, gla_b_a, gla_norm_w, fox_b_f, w_out_even,
              w_in_odd, rel_bias, conv_w, conv_b, lru_w_a, lru_b_a, lru_w_x, lru_b_x,
              lru_lambda, w_out_odd, w_mlp_up, w_mlp_down):
    for layer in range(DEPTH):
        j = layer // 2
        h = rmsnorm(x, norm_w[layer, 0])
        if layer % 2 == 0:
            proj = h @ w_in_even[j]
            g_q, g_k, g_v, g_r, g_a, f_q, f_k, f_v, f_f = _split(proj, EVEN_SIZES)
            out_a = gla_mixer(g_q, g_k, g_v, g_r, g_a, gla_w_a_up[j], gla_b_a[j], gla_norm_w[j])
            out_b = fox_mixer(f_q, f_k, f_v, f_f + fox_b_f[j])
            mix = jnp.concatenate([out_a, out_b], axis=-1) @ w_out_even[j]
        else:
            proj = h @ w_in_odd[j]
            c_q, c_k, c_v, d_gate, d_in = _split(proj, ODD_SIZES)
            out_c = chunk_rel_attention(c_q, c_k, c_v, rel_bias[j])
            out_d = rglru_mixer(d_gate, d_in, conv_w[j], conv_b[j], lru_w_a[j], lru_b_a[j],
                                lru_w_x[j], lru_b_x[j], lru_lambda[j])
            mix = jnp.concatenate([out_c, out_d], axis=-1) @ w_out_odd[j]
        x = x + rmsnorm(mix, norm_w[layer, 1])
        h = rmsnorm(x, norm_w[layer, 2])
        x = x + rmsnorm(sq_relu_mlp(h, w_mlp_up[layer], w_mlp_down[layer]), norm_w[layer, 3])
    return x


import jax as _jax
import jax.numpy as _jnp

TWIN_FORMAT = 'train_step'
FWD_PARAMS = ['x', 'norm_w', 'w_in_even', 'gla_w_a_up', 'gla_b_a', 'gla_norm_w', 'fox_b_f', 'w_out_even', 'w_in_odd', 'rel_bias', 'conv_w', 'conv_b', 'lru_w_a', 'lru_b_a', 'lru_w_x', 'lru_b_x', 'lru_lambda', 'w_out_odd', 'w_mlp_up', 'w_mlp_down']
TWIN_WEIGHTS = ['norm_w', 'w_in_even', 'gla_w_a_up', 'gla_b_a', 'gla_norm_w', 'fox_b_f', 'w_out_even', 'w_in_odd', 'rel_bias', 'conv_w', 'conv_b', 'lru_w_a', 'lru_b_a', 'lru_w_x', 'lru_b_x', 'lru_lambda', 'w_out_odd', 'w_mlp_up', 'w_mlp_down']
TWIN_DIFF_INPUT = 'x'
TWIN_INPUTS = ['x', 'norm_w', 'w_in_even', 'gla_w_a_up', 'gla_b_a', 'gla_norm_w', 'fox_b_f', 'w_out_even', 'w_in_odd', 'rel_bias', 'conv_w', 'conv_b', 'lru_w_a', 'lru_b_a', 'lru_w_x', 'lru_b_x', 'lru_lambda', 'w_out_odd', 'w_mlp_up', 'w_mlp_down', 'loss_target', 'm_norm_w', 'm_w_in_even', 'm_gla_w_a_up', 'm_gla_b_a', 'm_gla_norm_w', 'm_fox_b_f', 'm_w_out_even', 'm_w_in_odd', 'm_rel_bias', 'm_conv_w', 'm_conv_b', 'm_lru_w_a', 'm_lru_b_a', 'm_lru_w_x', 'm_lru_b_x', 'm_lru_lambda', 'm_w_out_odd', 'm_w_mlp_up', 'm_w_mlp_down', 'v_norm_w', 'v_w_in_even', 'v_gla_w_a_up', 'v_gla_b_a', 'v_gla_norm_w', 'v_fox_b_f', 'v_w_out_even', 'v_w_in_odd', 'v_rel_bias', 'v_conv_w', 'v_conv_b', 'v_lru_w_a', 'v_lru_b_a', 'v_lru_w_x', 'v_lru_b_x', 'v_lru_lambda', 'v_w_out_odd', 'v_w_mlp_up', 'v_w_mlp_down']
TWIN_OUTPUTS = ['loss', 'grad_x', 'grad_norm_w', 'grad_w_in_even', 'grad_gla_w_a_up', 'grad_gla_b_a', 'grad_gla_norm_w', 'grad_fox_b_f', 'grad_w_out_even', 'grad_w_in_odd', 'grad_rel_bias', 'grad_conv_w', 'grad_conv_b', 'grad_lru_w_a', 'grad_lru_b_a', 'grad_lru_w_x', 'grad_lru_b_x', 'grad_lru_lambda', 'grad_w_out_odd', 'grad_w_mlp_up', 'grad_w_mlp_down', 'delta_norm_w', 'delta_w_in_even', 'delta_gla_w_a_up', 'delta_gla_b_a', 'delta_gla_norm_w', 'delta_fox_b_f', 'delta_w_out_even', 'delta_w_in_odd', 'delta_rel_bias', 'delta_conv_w', 'delta_conv_b', 'delta_lru_w_a', 'delta_lru_b_a', 'delta_lru_w_x', 'delta_lru_b_x', 'delta_lru_lambda', 'delta_w_out_odd', 'delta_w_mlp_up', 'delta_w_mlp_down', 'new_m_norm_w', 'new_m_w_in_even', 'new_m_gla_w_a_up', 'new_m_gla_b_a', 'new_m_gla_norm_w', 'new_m_fox_b_f', 'new_m_w_out_even', 'new_m_w_in_odd', 'new_m_rel_bias', 'new_m_conv_w', 'new_m_conv_b', 'new_m_lru_w_a', 'new_m_lru_b_a', 'new_m_lru_w_x', 'new_m_lru_b_x', 'new_m_lru_lambda', 'new_m_w_out_odd', 'new_m_w_mlp_up', 'new_m_w_mlp_down', 'new_v_norm_w', 'new_v_w_in_even', 'new_v_gla_w_a_up', 'new_v_gla_b_a', 'new_v_gla_norm_w', 'new_v_fox_b_f', 'new_v_w_out_even', 'new_v_w_in_odd', 'new_v_rel_bias', 'new_v_conv_w', 'new_v_conv_b', 'new_v_lru_w_a', 'new_v_lru_b_a', 'new_v_lru_w_x', 'new_v_lru_b_x', 'new_v_lru_lambda', 'new_v_w_out_odd', 'new_v_w_mlp_up', 'new_v_w_mlp_down']
TWIN_LEAF_KINDS = {'loss': 'loss', 'grad_x': 'grad_x', 'grad_norm_w': 'grad_w', 'grad_w_in_even': 'grad_w', 'grad_gla_w_a_up': 'grad_w', 'grad_gla_b_a': 'grad_w', 'grad_gla_norm_w': 'grad_w', 'grad_fox_b_f': 'grad_w', 'grad_w_out_even': 'grad_w', 'grad_w_in_odd': 'grad_w', 'grad_rel_bias': 'grad_w', 'grad_conv_w': 'grad_w', 'grad_conv_b': 'grad_w', 'grad_lru_w_a': 'grad_w', 'grad_lru_b_a': 'grad_w', 'grad_lru_w_x': 'grad_w', 'grad_lru_b_x': 'grad_w', 'grad_lru_lambda': 'grad_w', 'grad_w_out_odd': 'grad_w', 'grad_w_mlp_up': 'grad_w', 'grad_w_mlp_down': 'grad_w', 'delta_norm_w': 'delta_w', 'delta_w_in_even': 'delta_w', 'delta_gla_w_a_up': 'delta_w', 'delta_gla_b_a': 'delta_w', 'delta_gla_norm_w': 'delta_w', 'delta_fox_b_f': 'delta_w', 'delta_w_out_even': 'delta_w', 'delta_w_in_odd': 'delta_w', 'delta_rel_bias': 'delta_w', 'delta_conv_w': 'delta_w', 'delta_conv_b': 'delta_w', 'delta_lru_w_a': 'delta_w', 'delta_lru_b_a': 'delta_w', 'delta_lru_w_x': 'delta_w', 'delta_lru_b_x': 'delta_w', 'delta_lru_lambda': 'delta_w', 'delta_w_out_odd': 'delta_w', 'delta_w_mlp_up': 'delta_w', 'delta_w_mlp_down': 'delta_w', 'new_m_norm_w': 'new_m', 'new_m_w_in_even': 'new_m', 'new_m_gla_w_a_up': 'new_m', 'new_m_gla_b_a': 'new_m', 'new_m_gla_norm_w': 'new_m', 'new_m_fox_b_f': 'new_m', 'new_m_w_out_even': 'new_m', 'new_m_w_in_odd': 'new_m', 'new_m_rel_bias': 'new_m', 'new_m_conv_w': 'new_m', 'new_m_conv_b': 'new_m', 'new_m_lru_w_a': 'new_m', 'new_m_lru_b_a': 'new_m', 'new_m_lru_w_x': 'new_m', 'new_m_lru_b_x': 'new_m', 'new_m_lru_lambda': 'new_m', 'new_m_w_out_odd': 'new_m', 'new_m_w_mlp_up': 'new_m', 'new_m_w_mlp_down': 'new_m', 'new_v_norm_w': 'new_v', 'new_v_w_in_even': 'new_v', 'new_v_gla_w_a_up': 'new_v', 'new_v_gla_b_a': 'new_v', 'new_v_gla_norm_w': 'new_v', 'new_v_fox_b_f': 'new_v', 'new_v_w_out_even': 'new_v', 'new_v_w_in_odd': 'new_v', 'new_v_rel_bias': 'new_v', 'new_v_conv_w': 'new_v', 'new_v_conv_b': 'new_v', 'new_v_lru_w_a': 'new_v', 'new_v_lru_b_a': 'new_v', 'new_v_lru_w_x': 'new_v', 'new_v_lru_b_x': 'new_v', 'new_v_lru_lambda': 'new_v', 'new_v_w_out_odd': 'new_v', 'new_v_w_mlp_up': 'new_v', 'new_v_w_mlp_down': 'new_v'}


def _forward(args):
    return _fwd_reference(*[args[k] for k in FWD_PARAMS])


def _output_shape():
    out = _jax.eval_shape(lambda: _forward(_fwd_setup_inputs(0)))
    return out.shape, out.dtype

N_MICROBATCH = 1
ADAM_LR = 0.001
ADAM_B1 = 0.9
ADAM_B2 = 0.999
ADAM_EPS = 1e-08
ADAM_WD = 0.01
ADAM_STEP = 10
PER_EXAMPLE_BATCH_AXIS = {'x': 0, 'loss_target': 0}
SHARED_INPUTS = []
_WEIGHT_DTYPES = {'norm_w': _jnp.float32, 'w_in_even': _jnp.float32, 'gla_w_a_up': _jnp.float32, 'gla_b_a': _jnp.float32, 'gla_norm_w': _jnp.float32, 'fox_b_f': _jnp.float32, 'w_out_even': _jnp.float32, 'w_in_odd': _jnp.float32, 'rel_bias': _jnp.float32, 'conv_w': _jnp.float32, 'conv_b': _jnp.float32, 'lru_w_a': _jnp.float32, 'lru_b_a': _jnp.float32, 'lru_w_x': _jnp.float32, 'lru_b_x': _jnp.float32, 'lru_lambda': _jnp.float32, 'w_out_odd': _jnp.float32, 'w_mlp_up': _jnp.float32, 'w_mlp_down': _jnp.float32}
MOMENT_SCALE = {'norm_w': 1.210168e+01, 'w_in_even': 6.243704e-01, 'gla_w_a_up': 1.356324e-01, 'gla_b_a': 6.926683e-01, 'gla_norm_w': 1.285305e+00, 'fox_b_f': 2.247212e+00, 'w_out_even': 7.793762e-01, 'w_in_odd': 3.517211e+00, 'rel_bias': 5.233232e-02, 'conv_w': 6.975562e+00, 'conv_b': 2.534843e+01, 'lru_w_a': 8.503149e-01, 'lru_b_a': 6.411991e-01, 'lru_w_x': 1.639309e+00, 'lru_b_x': 2.245190e+00, 'lru_lambda': 1.268988e+00, 'w_out_odd': 4.793870e+00, 'w_mlp_up': 7.815453e-01, 'w_mlp_down': 5.940915e+00}


def _to_microbatches(a, axis):
    t = _jnp.moveaxis(a, axis, 0)
    t = t.reshape((N_MICROBATCH, t.shape[0] // N_MICROBATCH) + t.shape[1:])
    return _jnp.moveaxis(t, 1, axis + 1)


def setup_inputs(seed: int = 0) -> dict:
    inp = _fwd_setup_inputs(seed)
    key = _jax.random.fold_in(_jax.random.key(seed), 7919)
    shape, _ = _output_shape()
    out = dict(inp)
    out["loss_target"] = _jax.random.normal(_jax.random.fold_in(key, 0), shape, _jnp.float32)
    for i, name in enumerate(TWIN_WEIGHTS):
        w = inp[name].astype(_jnp.float32)
        if MOMENT_SCALE is None:
            s = _jnp.sqrt(_jnp.mean(_jnp.square(w)) + 1e-30)
        else:
            s = MOMENT_SCALE[name]
        km, kv = _jax.random.split(_jax.random.fold_in(key, i + 1))
        out[name] = w
        out["m_" + name] = s * _jax.random.normal(km, w.shape, _jnp.float32)
        out["v_" + name] = (s * s) * _jax.random.uniform(kv, w.shape, _jnp.float32, 0.5, 1.5)
    if N_MICROBATCH > 1:
        for name, axis in PER_EXAMPLE_BATCH_AXIS.items():
            out[name] = _to_microbatches(out[name], axis)
    return {'x': out['x'], 'norm_w': out['norm_w'], 'w_in_even': out['w_in_even'], 'gla_w_a_up': out['gla_w_a_up'], 'gla_b_a': out['gla_b_a'], 'gla_norm_w': out['gla_norm_w'], 'fox_b_f': out['fox_b_f'], 'w_out_even': out['w_out_even'], 'w_in_odd': out['w_in_odd'], 'rel_bias': out['rel_bias'], 'conv_w': out['conv_w'], 'conv_b': out['conv_b'], 'lru_w_a': out['lru_w_a'], 'lru_b_a': out['lru_b_a'], 'lru_w_x': out['lru_w_x'], 'lru_b_x': out['lru_b_x'], 'lru_lambda': out['lru_lambda'], 'w_out_odd': out['w_out_odd'], 'w_mlp_up': out['w_mlp_up'], 'w_mlp_down': out['w_mlp_down'], 'loss_target': out['loss_target'], 'm_norm_w': out['m_norm_w'], 'm_w_in_even': out['m_w_in_even'], 'm_gla_w_a_up': out['m_gla_w_a_up'], 'm_gla_b_a': out['m_gla_b_a'], 'm_gla_norm_w': out['m_gla_norm_w'], 'm_fox_b_f': out['m_fox_b_f'], 'm_w_out_even': out['m_w_out_even'], 'm_w_in_odd': out['m_w_in_odd'], 'm_rel_bias': out['m_rel_bias'], 'm_conv_w': out['m_conv_w'], 'm_conv_b': out['m_conv_b'], 'm_lru_w_a': out['m_lru_w_a'], 'm_lru_b_a': out['m_lru_b_a'], 'm_lru_w_x': out['m_lru_w_x'], 'm_lru_b_x': out['m_lru_b_x'], 'm_lru_lambda': out['m_lru_lambda'], 'm_w_out_odd': out['m_w_out_odd'], 'm_w_mlp_up': out['m_w_mlp_up'], 'm_w_mlp_down': out['m_w_mlp_down'], 'v_norm_w': out['v_norm_w'], 'v_w_in_even': out['v_w_in_even'], 'v_gla_w_a_up': out['v_gla_w_a_up'], 'v_gla_b_a': out['v_gla_b_a'], 'v_gla_norm_w': out['v_gla_norm_w'], 'v_fox_b_f': out['v_fox_b_f'], 'v_w_out_even': out['v_w_out_even'], 'v_w_in_odd': out['v_w_in_odd'], 'v_rel_bias': out['v_rel_bias'], 'v_conv_w': out['v_conv_w'], 'v_conv_b': out['v_conv_b'], 'v_lru_w_a': out['v_lru_w_a'], 'v_lru_b_a': out['v_lru_b_a'], 'v_lru_w_x': out['v_lru_w_x'], 'v_lru_b_x': out['v_lru_b_x'], 'v_lru_lambda': out['v_lru_lambda'], 'v_w_out_odd': out['v_w_out_odd'], 'v_w_mlp_up': out['v_w_mlp_up'], 'v_w_mlp_down': out['v_w_mlp_down']}


def _loss(weights, diff, rest, loss_target):
    with _jax.named_scope("forward"):
        args = {**rest, TWIN_DIFF_INPUT: diff, **{k: w.astype(_WEIGHT_DTYPES[k]) for k, w in weights.items()}}
        y = _forward(args)
    with _jax.named_scope("loss_head"):
        err = _jnp.square(y.astype(_jnp.float32) - loss_target)
        return 0.5 * _jnp.sum(_jnp.mean(err, axis=-1)) if err.ndim else 0.5 * err


def _adamw(w, g, m, v):
    m = ADAM_B1 * m + (1.0 - ADAM_B1) * g
    v = ADAM_B2 * v + (1.0 - ADAM_B2) * _jnp.square(g)
    m_hat = m / (1.0 - ADAM_B1 ** ADAM_STEP)
    v_hat = v / (1.0 - ADAM_B2 ** ADAM_STEP)
    delta = -ADAM_LR * (m_hat / (_jnp.sqrt(v_hat) + ADAM_EPS) + ADAM_WD * w)
    return delta, m, v


def reference(x, norm_w, w_in_even, gla_w_a_up, gla_b_a, gla_norm_w, fox_b_f, w_out_even, w_in_odd, rel_bias, conv_w, conv_b, lru_w_a, lru_b_a, lru_w_x, lru_b_x, lru_lambda, w_out_odd, w_mlp_up, w_mlp_down, loss_target, m_norm_w, m_w_in_even, m_gla_w_a_up, m_gla_b_a, m_gla_norm_w, m_fox_b_f, m_w_out_even, m_w_in_odd, m_rel_bias, m_conv_w, m_conv_b, m_lru_w_a, m_lru_b_a, m_lru_w_x, m_lru_b_x, m_lru_lambda, m_w_out_odd, m_w_mlp_up, m_w_mlp_down, v_norm_w, v_w_in_even, v_gla_w_a_up, v_gla_b_a, v_gla_norm_w, v_fox_b_f, v_w_out_even, v_w_in_odd, v_rel_bias, v_conv_w, v_conv_b, v_lru_w_a, v_lru_b_a, v_lru_w_x, v_lru_b_x, v_lru_lambda, v_w_out_odd, v_w_mlp_up, v_w_mlp_down):
    given = dict(x=x, norm_w=norm_w, w_in_even=w_in_even, gla_w_a_up=gla_w_a_up, gla_b_a=gla_b_a, gla_norm_w=gla_norm_w, fox_b_f=fox_b_f, w_out_even=w_out_even, w_in_odd=w_in_odd, rel_bias=rel_bias, conv_w=conv_w, conv_b=conv_b, lru_w_a=lru_w_a, lru_b_a=lru_b_a, lru_w_x=lru_w_x, lru_b_x=lru_b_x, lru_lambda=lru_lambda, w_out_odd=w_out_odd, w_mlp_up=w_mlp_up, w_mlp_down=w_mlp_down, loss_target=loss_target, m_norm_w=m_norm_w, m_w_in_even=m_w_in_even, m_gla_w_a_up=m_gla_w_a_up, m_gla_b_a=m_gla_b_a, m_gla_norm_w=m_gla_norm_w, m_fox_b_f=m_fox_b_f, m_w_out_even=m_w_out_even, m_w_in_odd=m_w_in_odd, m_rel_bias=m_rel_bias, m_conv_w=m_conv_w, m_conv_b=m_conv_b, m_lru_w_a=m_lru_w_a, m_lru_b_a=m_lru_b_a, m_lru_w_x=m_lru_w_x, m_lru_b_x=m_lru_b_x, m_lru_lambda=m_lru_lambda, m_w_out_odd=m_w_out_odd, m_w_mlp_up=m_w_mlp_up, m_w_mlp_down=m_w_mlp_down, v_norm_w=v_norm_w, v_w_in_even=v_w_in_even, v_gla_w_a_up=v_gla_w_a_up, v_gla_b_a=v_gla_b_a, v_gla_norm_w=v_gla_norm_w, v_fox_b_f=v_fox_b_f, v_w_out_even=v_w_out_even, v_w_in_odd=v_w_in_odd, v_rel_bias=v_rel_bias, v_conv_w=v_conv_w, v_conv_b=v_conv_b, v_lru_w_a=v_lru_w_a, v_lru_b_a=v_lru_b_a, v_lru_w_x=v_lru_w_x, v_lru_b_x=v_lru_b_x, v_lru_lambda=v_lru_lambda, v_w_out_odd=v_w_out_odd, v_w_mlp_up=v_w_mlp_up, v_w_mlp_down=v_w_mlp_down)
    weights = {n: given[n] for n in TWIN_WEIGHTS}
    shared = {n: given[n] for n in SHARED_INPUTS}
    per_example = {n: given[n] for n in ['x']}
    grad_fn = _jax.value_and_grad(_loss, argnums=(0, 1))

    def one_microbatch(ex, loss_target):
        ex = dict(ex)
        diff = ex.pop(TWIN_DIFF_INPUT)
        return grad_fn(weights, diff, {**shared, **ex}, loss_target)

    if N_MICROBATCH == 1:
        loss, (grad_w, grad_x) = one_microbatch(per_example, given["loss_target"])
    else:
        def body(carry, xs):
            loss_sum, grad_sum = carry
            l_k, (gw_k, gx_k) = one_microbatch(xs[0], xs[1])
            with _jax.named_scope("update"):
                return (loss_sum + l_k, _jax.tree.map(_jnp.add, grad_sum, gw_k)), gx_k

        init = (_jnp.zeros((), _jnp.float32), _jax.tree.map(_jnp.zeros_like, weights))
        (loss, grad_w), grad_x = _jax.lax.scan(body, init, (per_example, given["loss_target"]))
    with _jax.named_scope("update"):
        delta_w, new_m, new_v = {}, {}, {}
        for n in TWIN_WEIGHTS:
            delta_w[n], new_m[n], new_v[n] = _adamw(weights[n], grad_w[n], given["m_" + n], given["v_" + n])
    return (loss, grad_x, *[grad_w[n] for n in TWIN_WEIGHTS], *[delta_w[n] for n in TWIN_WEIGHTS],
            *[new_m[n] for n in TWIN_WEIGHTS], *[new_v[n] for n in TWIN_WEIGHTS])
```

```python
import functools

import jax
import jax.numpy as jnp
from jax import lax
from jax.experimental import pallas as pl
from jax.experimental.pallas import tpu as pltpu

F32 = jnp.float32
MXU_DTYPE = jnp.bfloat16
ACT_DTYPE = jnp.bfloat16
WIRE_DTYPE = jnp.bfloat16

V7X_VMEM_BYTES = 64 * 1024 * 1024
VMEM_LIMIT = (V7X_VMEM_BYTES * 7) // 8
LANES = 128

D_MODEL = 1024
SEQ = 2048
DEPTH = 2
CHUNK = 64
GROUP_WIDTH = D_MODEL // 2
D_FF = 4 * D_MODEL
NORM_EPS = 1e-6
GLA_HEADS = 4
GLA_DV = GROUP_WIDTH // GLA_HEADS
GLA_DK = GLA_DV // 2
GLA_KW = GLA_HEADS * GLA_DK
GLA_RANK = 16
GLA_GATE_TAU = 16.0
HEAD_DIM = 64
ATT_HEADS = GROUP_WIDTH // HEAD_DIM
CA_LEFT = 8 * CHUNK
REL_CLIP = 128
LRU_BLOCK_DIM = 64
CONV_WIDTH = 4
LRU_C = 8.0
N_DEV = 8

ADAM_LR = 0.001
ADAM_B1 = 0.9
ADAM_B2 = 0.999
ADAM_EPS = 1e-08
ADAM_WD = 0.01
ADAM_STEP = 10

NEG = float(jnp.finfo(jnp.float32).min)
MESH = pl.DeviceIdType.MESH


def _params(*sem):
    return pltpu.CompilerParams(dimension_semantics=sem, vmem_limit_bytes=VMEM_LIMIT)


def _dot(a, b, ca=1, cb=0):
    return lax.dot_general(a.astype(MXU_DTYPE), b.astype(MXU_DTYPE), (((ca,), (cb,)), ((), ())),
                           preferred_element_type=F32)


def _dot_exact(a, b):
    return lax.dot_general(a, b, (((1,), (0,)), ((), ())), precision=lax.Precision.HIGHEST,
                           preferred_element_type=F32)


def _log_sigmoid(x):
    return jnp.minimum(x, 0.0) - jnp.log1p(jnp.exp(-jnp.abs(x)))


def _iota(shape, axis):
    return lax.broadcasted_iota(jnp.int32, shape, axis)


def _mm(a, b, *, nt=False, out_dtype, tm, tn, a_sqrelu=False, drelu_of=None, name):
    m, k = a.shape
    n = b.shape[0] if nt else b.shape[1]
    tm, tn = min(tm, m), min(tn, n)
    assert m % tm == 0 and n % tn == 0 and (b.shape[1] if nt else b.shape[0]) == k

    def body(*refs):
        a_ref, b_ref = refs[0], refs[1]
        o_ref = refs[-1]
        av = a_ref[...]
        if a_sqrelu:
            av = jnp.square(jnp.maximum(av.astype(F32), 0.0))
        acc = _dot(av, b_ref[...], 1, 1 if nt else 0)
        if drelu_of is not None:
            acc = acc * (2.0 * jnp.maximum(refs[2][...].astype(F32), 0.0))
        o_ref[...] = acc.astype(out_dtype)

    in_specs = [pl.BlockSpec((tm, k), lambda i, j: (i, 0)),
                pl.BlockSpec((tn, k), lambda i, j: (j, 0)) if nt else pl.BlockSpec((k, tn), lambda i, j: (0, j))]
    args = [a, b]
    if drelu_of is not None:
        in_specs.append(pl.BlockSpec((tm, tn), lambda i, j: (i, j)))
        args.append(drelu_of)
    return pl.pallas_call(
        body, grid=(m // tm, n // tn), in_specs=in_specs,
        out_specs=pl.BlockSpec((tm, tn), lambda i, j: (i, j)),
        out_shape=jax.ShapeDtypeStruct((m, n), out_dtype),
        compiler_params=_params("parallel", "parallel"), name=name)(*args)


ROW_TILE = 256


def _norm_fwd(x, w, *, out_dtype, res=None, name):
    t, d = x.shape

    def body(*refs):
        x_ref, w_ref, o_ref = refs[0], refs[1], refs[-1]
        xv = x_ref[...]
        y = xv * lax.rsqrt(jnp.mean(xv * xv, axis=-1, keepdims=True) + NORM_EPS) * w_ref[...]
        if res is not None:
            y = refs[2][...] + y
        o_ref[...] = y.astype(out_dtype)

    row = pl.BlockSpec((ROW_TILE, d), lambda i: (i, 0))
    in_specs = [row, pl.BlockSpec((1, d), lambda i: (0, 0))] + ([row] if res is not None else [])
    args = [x, w] + ([res] if res is not None else [])
    return pl.pallas_call(body, grid=(t // ROW_TILE,), in_specs=in_specs, out_specs=row,
                          out_shape=jax.ShapeDtypeStruct((t, d), out_dtype),
                          compiler_params=_params("parallel"), name=name)(*args)


def _norm_bwd(dy, x, w, *, out_dtype, add=None, name):
    t, d = x.shape

    def body(*refs):
        dy_ref, x_ref, w_ref = refs[0], refs[1], refs[2]
        dx_ref, dw_ref = refs[-2], refs[-1]
        xv = x_ref[...]
        rstd = lax.rsqrt(jnp.mean(xv * xv, axis=-1, keepdims=True) + NORM_EPS)
        xhat = xv * rstd
        dyv = dy_ref[...].astype(F32)
        g = dyv * w_ref[...]
        dx = rstd * (g - xhat * jnp.mean(g * xhat, axis=-1, keepdims=True))
        if add is not None:
            dx = dx + refs[3][...]
        dx_ref[...] = dx.astype(out_dtype)

        @pl.when(pl.program_id(0) == 0)
        def _():
            dw_ref[...] = jnp.zeros_like(dw_ref)

        dw_ref[...] += jnp.sum(dyv * xhat, axis=0, keepdims=True)

    row = pl.BlockSpec((ROW_TILE, d), lambda i: (i, 0))
    vec = pl.BlockSpec((1, d), lambda i: (0, 0))
    in_specs = [row, row, vec] + ([row] if add is not None else [])
    args = [dy, x, w] + ([add] if add is not None else [])
    return pl.pallas_call(body, grid=(t // ROW_TILE,), in_specs=in_specs, out_specs=[row, vec],
                          out_shape=[jax.ShapeDtypeStruct((t, d), out_dtype), jax.ShapeDtypeStruct((1, d), F32)],
                          compiler_params=_params("arbitrary"), name=name)(*args)


def _loss_fwd_bwd(y, target):
    t, d = y.shape

    def body(y_ref, t_ref, l_ref, dy_ref):
        diff = y_ref[...] - t_ref[...]
        dy_ref[...] = diff * (1.0 / d)

        @pl.when(pl.program_id(0) == 0)
        def _():
            l_ref[...] = jnp.zeros_like(l_ref)

        l_ref[...] += 0.5 * jnp.sum(jnp.mean(diff * diff, axis=-1, keepdims=True), axis=0, keepdims=True)

    row = pl.BlockSpec((ROW_TILE, d), lambda i: (i, 0))
    return pl.pallas_call(body, grid=(t // ROW_TILE,), in_specs=[row, row],
                          out_specs=[pl.BlockSpec((8, LANES), lambda i: (0, 0)), row],
                          out_shape=[jax.ShapeDtypeStruct((8, LANES), F32), jax.ShapeDtypeStruct((t, d), F32)],
                          compiler_params=_params("arbitrary"), name="loss")(y, target)


def _gla_specs(t):
    return [pl.BlockSpec((t, GLA_KW), lambda i: (0, 0)),
            pl.BlockSpec((t, GLA_KW), lambda i: (0, 1)),
            pl.BlockSpec((t, GROUP_WIDTH), lambda i: (0, 1)),
            pl.BlockSpec((t, GROUP_WIDTH), lambda i: (0, 0)),
            pl.BlockSpec((t, LANES), lambda i: (0, 4)),
            pl.BlockSpec((LANES, GLA_KW), lambda i: (0, 0)),
            pl.BlockSpec((1, GLA_KW), lambda i: (0, 0)),
            pl.BlockSpec((1, GROUP_WIDTH), lambda i: (0, 0))]


def _gla_fwd(pmm, pel, w_up, b_a, gnorm_w):
    t = pmm.shape[0]
    nc = t // CHUNK
    scale = GLA_DK ** -0.5

    def body(q_ref, k_ref, v_ref, r_ref, a_ref, wup_ref, ba_ref, gw_ref, o_ref, st_ref, la_scr, s_scr):
        z = _dot(a_ref[...], wup_ref[...]) + ba_ref[...]
        la_scr[...] = _log_sigmoid(z) * (1.0 / GLA_GATE_TAU)
        s_scr[...] = jnp.zeros_like(s_scr)
        tri = (_iota((CHUNK, CHUNK), 1) <= _iota((CHUNK, CHUNK), 0)).astype(F32)

        def chunk(c, carry):
            rows = pl.ds(pl.multiple_of(c * CHUNK, CHUNK), CHUNK)
            cum = _dot_exact(tri, la_scr[rows, :])
            tot = cum[CHUNK - 1:CHUNK, :]
            kd = k_ref[rows, :].astype(F32) * jnp.exp(tot - cum)
            decay = jnp.exp(tot)
            qs = q_ref[rows, :].astype(F32) * scale
            vv = v_ref[rows, :].astype(F32)
            rr = r_ref[rows, :]
            gate = rr * jax.nn.sigmoid(rr) * gw_ref[...]
            for h in range(GLA_HEADS):
                ks = slice(h * GLA_DK, (h + 1) * GLA_DK)
                vs = slice(h * GLA_DV, (h + 1) * GLA_DV)
                inc_t = _dot(vv[:, vs].T, kd[:, ks])
                s_t = s_scr[vs, :] * decay[:, ks] + inc_t
                s_scr[vs, :] = s_t
                st_ref[c, vs, :] = s_t
                o = _dot(qs[:, ks], s_t, 1, 1)
                y = o * lax.rsqrt(jnp.mean(o * o, axis=-1, keepdims=True) + NORM_EPS)
                o_ref[rows, vs] = (y * gate[:, vs]).astype(o_ref.dtype)
            return carry

        lax.fori_loop(0, nc, chunk, 0)

    return pl.pallas_call(
        body, grid=(1,), in_specs=_gla_specs(t),
        out_specs=[pl.BlockSpec((t, GROUP_WIDTH), lambda i: (0, 0)),
                   pl.BlockSpec((nc, GLA_HEADS * GLA_DV, GLA_DK), lambda i: (0, 0, 0))],
        out_shape=[jax.ShapeDtypeStruct((t, GROUP_WIDTH), ACT_DTYPE),
                   jax.ShapeDtypeStruct((nc, GLA_HEADS * GLA_DV, GLA_DK), F32)],
        scratch_shapes=[pltpu.VMEM((t, GLA_KW), F32), pltpu.VMEM((GLA_HEADS * GLA_DV, GLA_DK), F32)],
        compiler_params=_params("arbitrary"), name="gla_fwd")(pmm, pmm, pmm, pel, pel, w_up, b_a, gnorm_w)


def _gla_bwd(pmm, pel, w_up, b_a, gnorm_w, states, dmix):
    t = pmm.shape[0]
    nc = t // CHUNK
    scale = GLA_DK ** -0.5

    def body(q_ref, k_ref, v_ref, r_ref, a_ref, wup_ref, ba_ref, gw_ref, st_ref, do_ref,
             dq_ref, dk_ref, dv_ref, dr_ref, da_ref, dwup_ref, dba_ref, dgw_ref, la_scr, dz_scr, ds_scr):
        z = _dot(a_ref[...], wup_ref[...]) + ba_ref[...]
        la_scr[...] = _log_sigmoid(z) * (1.0 / GLA_GATE_TAU)
        ds_scr[...] = jnp.zeros_like(ds_scr)
        dgw_ref[...] = jnp.zeros_like(dgw_ref)
        row_i, col_i = _iota((CHUNK, CHUNK), 0), _iota((CHUNK, CHUNK), 1)
        tri = (col_i <= row_i).astype(F32)
        tri_strict = (col_i < row_i).astype(F32)

        def chunk(n, carry):
            c = nc - 1 - n
            rows = pl.ds(pl.multiple_of(c * CHUNK, CHUNK), CHUNK)
            cum = _dot_exact(tri, la_scr[rows, :])
            tot = cum[CHUNK - 1:CHUNK, :]
            e = jnp.exp(tot - cum)
            kd = k_ref[rows, :].astype(F32) * e
            decay = jnp.exp(tot)
            qs = q_ref[rows, :].astype(F32) * scale
            vv = v_ref[rows, :].astype(F32)
            rr = r_ref[rows, :]
            sig = jax.nn.sigmoid(rr)
            silu = rr * sig
            dsilu = sig * (1.0 + rr * (1.0 - sig))
            dout = do_ref[rows, :]
            gw = gw_ref[...]
            c_prev = jnp.maximum(c - 1, 0)
            has_prev = (c > 0).astype(F32)
            zc = _dot(a_ref[rows, :], wup_ref[...]) + ba_ref[...]
            dz_scale = jax.nn.sigmoid(-zc) * (1.0 / GLA_GATE_TAU)
            for h in range(GLA_HEADS):
                ks = slice(h * GLA_DK, (h + 1) * GLA_DK)
                vs = slice(h * GLA_DV, (h + 1) * GLA_DV)
                s_t = st_ref[c, vs, :]
                s_prev = st_ref[c_prev, vs, :] * has_prev
                o = _dot(qs[:, ks], s_t, 1, 1)
                rstd = lax.rsqrt(jnp.mean(o * o, axis=-1, keepdims=True) + NORM_EPS)
                y = o * rstd
                dg = dout[:, vs]
                dgw_ref[:, vs] += jnp.sum(dg * y * silu[:, vs], axis=0, keepdims=True)
                dr_ref[rows, vs] = (dg * y * gw[:, vs] * dsilu[:, vs]).astype(dr_ref.dtype)
                dy = dg * gw[:, vs] * silu[:, vs]
                d_o = rstd * (dy - y * jnp.mean(dy * y, axis=-1, keepdims=True))
                dq_ref[rows, ks] = (_dot(d_o, s_t) * scale).astype(dq_ref.dtype)
                ds_t = ds_scr[vs, :] + _dot(d_o.T, qs[:, ks])
                dv_ref[rows, vs] = _dot(kd[:, ks], ds_t, 1, 1).astype(dv_ref.dtype)
                dkd = _dot(vv[:, vs], ds_t)
                ddecay = jnp.sum(ds_t * s_prev, axis=0, keepdims=True)
                ds_scr[vs, :] = ds_t * decay[:, ks]
                dla = ddecay * decay[:, ks] + _dot_exact(tri_strict, dkd * kd[:, ks])
                dz_scr[rows, ks] = dla * dz_scale[:, ks]
                dk_ref[rows, ks] = (dkd * e[:, ks]).astype(dk_ref.dtype)
            return carry

        lax.fori_loop(0, nc, chunk, 0)
        dz = dz_scr[...]
        da_ref[...] = _dot(dz, wup_ref[...], 1, 1).astype(da_ref.dtype)
        dwup_ref[...] = _dot(a_ref[...].T, dz)
        dba_ref[...] = jnp.sum(dz, axis=0, keepdims=True)

    in_specs = _gla_specs(t) + [
        pl.BlockSpec((nc, GLA_HEADS * GLA_DV, GLA_DK), lambda i: (0, 0, 0)),
        pl.BlockSpec((t, GROUP_WIDTH), lambda i: (0, 0))]
    full = lambda r, c: pl.BlockSpec((r, c), lambda i: (0, 0))
    return pl.pallas_call(
        body, grid=(1,), in_specs=in_specs,
        out_specs=[full(t, GLA_KW), full(t, GLA_KW), full(t, GROUP_WIDTH), full(t, GROUP_WIDTH), full(t, LANES),
                   full(LANES, GLA_KW), full(1, GLA_KW), full(1, GROUP_WIDTH)],
        out_shape=[jax.ShapeDtypeStruct((t, GLA_KW), ACT_DTYPE), jax.ShapeDtypeStruct((t, GLA_KW), ACT_DTYPE),
                   jax.ShapeDtypeStruct((t, GROUP_WIDTH), ACT_DTYPE), jax.ShapeDtypeStruct((t, GROUP_WIDTH), ACT_DTYPE),
                   jax.ShapeDtypeStruct((t, LANES), ACT_DTYPE), jax.ShapeDtypeStruct((LANES, GLA_KW), F32),
                   jax.ShapeDtypeStruct((1, GLA_KW), F32), jax.ShapeDtypeStruct((1, GROUP_WIDTH), F32)],
        scratch_shapes=[pltpu.VMEM((t, GLA_KW), F32), pltpu.VMEM((t, GLA_KW), F32),
                        pltpu.VMEM((GLA_HEADS * GLA_DV, GLA_DK), F32)],
        compiler_params=_params("arbitrary"), name="gla_bwd")(
            pmm, pmm, pmm, pel, pel, w_up, b_a, gnorm_w, states, dmix)


CUM_BLOCK = 256


def _fox_gate_fwd(pel, b_f):
    t = pel.shape[0]

    def body(f_ref, b_ref, cum_ref, cum_t_ref):
        tri = (_iota((CUM_BLOCK, CUM_BLOCK), 1) <= _iota((CUM_BLOCK, CUM_BLOCK), 0)).astype(F32)
        carry = jnp.zeros((1, LANES), F32)
        for blk in range(t // CUM_BLOCK):
            rows = slice(blk * CUM_BLOCK, (blk + 1) * CUM_BLOCK)
            cum = _dot_exact(tri, _log_sigmoid(f_ref[rows, :] + b_ref[...])) + carry
            cum_ref[rows, :] = cum
            carry = cum[CUM_BLOCK - 1:CUM_BLOCK, :]
        cum_t_ref[...] = cum_ref[...].T

    return pl.pallas_call(
        body, grid=(1,),
        in_specs=[pl.BlockSpec((t, LANES), lambda i: (0, 5)), pl.BlockSpec((1, LANES), lambda i: (0, 0))],
        out_specs=[pl.BlockSpec((t, LANES), lambda i: (0, 0)), pl.BlockSpec((LANES, t), lambda i: (0, 0))],
        out_shape=[jax.ShapeDtypeStruct((t, LANES), F32), jax.ShapeDtypeStruct((LANES, t), F32)],
        compiler_params=_params("arbitrary"), name="fox_gate_fwd")(pel, b_f)


def _fox_gate_bwd(pel, b_f, dcum_t):
    t = pel.shape[0]

    def body(f_ref, b_ref, dct_ref, df_ref, db_ref, dc_scr):
        dc_scr[...] = dct_ref[...].T
        tri_up = (_iota((CUM_BLOCK, CUM_BLOCK), 1) >= _iota((CUM_BLOCK, CUM_BLOCK), 0)).astype(F32)
        carry = jnp.zeros((1, LANES), F32)
        db = jnp.zeros((1, LANES), F32)
        for blk in reversed(range(t // CUM_BLOCK)):
            rows = slice(blk * CUM_BLOCK, (blk + 1) * CUM_BLOCK)
            dls = _dot_exact(tri_up, dc_scr[rows, :]) + carry
            carry = dls[0:1, :]
            df = dls * jax.nn.sigmoid(-(f_ref[rows, :] + b_ref[...]))
            df_ref[rows, :] = df.astype(df_ref.dtype)
            db = db + jnp.sum(df, axis=0, keepdims=True)
        db_ref[...] = db

    return pl.pallas_call(
        body, grid=(1,),
        in_specs=[pl.BlockSpec((t, LANES), lambda i: (0, 5)), pl.BlockSpec((1, LANES), lambda i: (0, 0)),
                  pl.BlockSpec((LANES, t), lambda i: (0, 0))],
        out_specs=[pl.BlockSpec((t, LANES), lambda i: (0, 0)), pl.BlockSpec((1, LANES), lambda i: (0, 0))],
        out_shape=[jax.ShapeDtypeStruct((t, LANES), ACT_DTYPE), jax.ShapeDtypeStruct((1, LANES), F32)],
        scratch_shapes=[pltpu.VMEM((t, LANES), F32)],
        compiler_params=_params("arbitrary"), name="fox_gate_bwd")(pel, b_f, dcum_t)


FOX_Q_BLOCK = 256


def _fox_scores(q_ref, k_ref, cum_ref, cum_t_ref, h, mask):
    hs = slice(h * HEAD_DIM, (h + 1) * HEAD_DIM)
    s = _dot(q_ref[:, hs], k_ref[:, hs], 1, 1) * (HEAD_DIM ** -0.5)
    s = s + (cum_ref[:, h:h + 1] - cum_t_ref[h:h + 1, :])
    return jnp.where(mask, s, NEG)


def _fox_fwd(pmm, cum, cum_t):
    t = pmm.shape[0]
    bq = FOX_Q_BLOCK

    def body(q_ref, k_ref, v_ref, cum_ref, cum_t_ref, o_ref, lse_ref):
        i = pl.program_id(0)
        mask = _iota((bq, t), 1) <= i * bq + _iota((bq, t), 0)
        lse_ref[...] = jnp.zeros_like(lse_ref)
        for h in range(ATT_HEADS):
            hs = slice(h * HEAD_DIM, (h + 1) * HEAD_DIM)
            s = _fox_scores(q_ref, k_ref, cum_ref, cum_t_ref, h, mask)
            m = jnp.max(s, axis=-1, keepdims=True)
            p = jnp.exp(s - m)
            l = jnp.sum(p, axis=-1, keepdims=True)
            o_ref[:, hs] = (_dot(p, v_ref[:, hs]) / l).astype(o_ref.dtype)
            lse_ref[:, h:h + 1] = m + jnp.log(l)

    return pl.pallas_call(
        body, grid=(t // bq,),
        in_specs=[pl.BlockSpec((bq, GROUP_WIDTH), lambda i: (i, 2)), pl.BlockSpec((t, GROUP_WIDTH), lambda i: (0, 3)),
                  pl.BlockSpec((t, GROUP_WIDTH), lambda i: (0, 4)), pl.BlockSpec((bq, LANES), lambda i: (i, 0)),
                  pl.BlockSpec((8, t), lambda i: (0, 0))],
        out_specs=[pl.BlockSpec((bq, GROUP_WIDTH), lambda i: (i, 0)), pl.BlockSpec((bq, LANES), lambda i: (i, 0))],
        out_shape=[jax.ShapeDtypeStruct((t, GROUP_WIDTH), ACT_DTYPE), jax.ShapeDtypeStruct((t, LANES), F32)],
        compiler_params=_params("parallel"), name="fox_fwd")(pmm, pmm, pmm, cum, cum_t)


def _fox_bwd(pmm, cum, cum_t, lse, dmix):
    t = pmm.shape[0]
    bq = FOX_Q_BLOCK
    scale = HEAD_DIM ** -0.5

    def body(q_ref, k_ref, v_ref, cum_ref, cum_t_ref, lse_ref, do_ref, dq_ref, dk_ref, dv_ref, dct_ref):
        i = pl.program_id(0)

        @pl.when(i == 0)
        def _():
            dk_ref[...] = jnp.zeros_like(dk_ref)
            dv_ref[...] = jnp.zeros_like(dv_ref)
            dct_ref[...] = jnp.zeros_like(dct_ref)

        mask = _iota((bq, t), 1) <= i * bq + _iota((bq, t), 0)
        for h in range(ATT_HEADS):
            hs = slice(h * HEAD_DIM, (h + 1) * HEAD_DIM)
            s = _fox_scores(q_ref, k_ref, cum_ref, cum_t_ref, h, mask)
            p = jnp.exp(s - lse_ref[:, h:h + 1])
            do = do_ref[:, hs]
            dp = _dot(do, v_ref[:, hs], 1, 1)
            ds = p * (dp - jnp.sum(p * dp, axis=-1, keepdims=True))
            dq_ref[:, hs] = (_dot(ds, k_ref[:, hs]) * scale).astype(dq_ref.dtype)
            dk_ref[:, hs] += _dot(ds, q_ref[:, hs], 0, 0) * scale
            dv_ref[:, hs] += _dot(p, do, 0, 0)
            dct_ref[h:h + 1, :] += -jnp.sum(ds, axis=0, keepdims=True)

    whole = pl.BlockSpec((t, GROUP_WIDTH), lambda i: (0, 0))
    return pl.pallas_call(
        body, grid=(t // bq,),
        in_specs=[pl.BlockSpec((bq, GROUP_WIDTH), lambda i: (i, 2)), pl.BlockSpec((t, GROUP_WIDTH), lambda i: (0, 3)),
                  pl.BlockSpec((t, GROUP_WIDTH), lambda i: (0, 4)), pl.BlockSpec((bq, LANES), lambda i: (i, 0)),
                  pl.BlockSpec((8, t), lambda i: (0, 0)), pl.BlockSpec((bq, LANES), lambda i: (i, 0)),
                  pl.BlockSpec((bq, GROUP_WIDTH), lambda i: (i, 1))],
        out_specs=[pl.BlockSpec((bq, GROUP_WIDTH), lambda i: (i, 0)), whole, whole,
                   pl.BlockSpec((LANES, t), lambda i: (0, 0))],
        out_shape=[jax.ShapeDtypeStruct((t, GROUP_WIDTH), ACT_DTYPE), jax.ShapeDtypeStruct((t, GROUP_WIDTH), F32),
                   jax.ShapeDtypeStruct((t, GROUP_WIDTH), F32), jax.ShapeDtypeStruct((LANES, t), F32)],
        compiler_params=_params("arbitrary"), name="fox_bwd")(pmm, pmm, pmm, cum, cum_t, lse, dmix)


CA_Q_BLOCK = 4 * CHUNK
CA_WINDOW = CA_Q_BLOCK + CA_LEFT
CA_BASE = 1024


def _ca_bias_base(rel_bias):
    n = rel_bias.shape[0]
    flat = CA_Q_BLOCK + CA_LEFT - REL_CLIP
    tail = CA_BASE - flat - (2 * REL_CLIP + 1)
    return jnp.concatenate([jnp.broadcast_to(rel_bias[:, 2 * REL_CLIP:], (n, flat)), rel_bias[:, ::-1],
                            jnp.broadcast_to(rel_bias[:, :1], (n, tail))], axis=1)


def _ca_bias_base_grad(dbase):
    flat = CA_Q_BLOCK + CA_LEFT - REL_CLIP
    mid = dbase[:, flat:flat + 2 * REL_CLIP + 1][:, ::-1]
    lo = jnp.sum(dbase[:, flat + 2 * REL_CLIP + 1:], axis=1, keepdims=True)
    hi = jnp.sum(dbase[:, :flat], axis=1, keepdims=True)
    pad = jnp.zeros((dbase.shape[0], 2 * REL_CLIP - 1), F32)
    return mid + jnp.concatenate([lo, pad, hi], axis=1)


def _ca_mask(i):
    r, j = _iota((CA_Q_BLOCK, CA_WINDOW), 0), _iota((CA_Q_BLOCK, CA_WINDOW), 1)
    rc, jc = r // CHUNK, j // CHUNK
    return (jc >= rc) & (jc <= rc + CA_LEFT // CHUNK) & (i * CA_Q_BLOCK + j >= CA_LEFT)


def _ca_scores(q_ref, kp_ref, base_ref, win, h, mask):
    hs = slice(h * HEAD_DIM, (h + 1) * HEAD_DIM)
    s = _dot(q_ref[:, hs], kp_ref[win, hs], 1, 1) * (HEAD_DIM ** -0.5)
    rows = jnp.broadcast_to(base_ref[h:h + 1, :], (CA_Q_BLOCK, CA_BASE))
    bias = pltpu.roll(rows, CA_BASE - CA_Q_BLOCK, 1, stride=1, stride_axis=0)[:, :CA_WINDOW]
    return jnp.where(mask, s + bias, NEG)


def _ca_fwd(pmm, kp, vp, base):
    t = pmm.shape[0]

    def body(q_ref, kp_ref, vp_ref, base_ref, o_ref, lse_ref):
        i = pl.program_id(0)
        win = pl.ds(pl.multiple_of(i * CA_Q_BLOCK, CA_Q_BLOCK), CA_WINDOW)
        mask = _ca_mask(i)
        lse_ref[...] = jnp.zeros_like(lse_ref)
        for h in range(ATT_HEADS):
            hs = slice(h * HEAD_DIM, (h + 1) * HEAD_DIM)
            s = _ca_scores(q_ref, kp_ref, base_ref, win, h, mask)
            m = jnp.max(s, axis=-1, keepdims=True)
            p = jnp.exp(s - m)
            l = jnp.sum(p, axis=-1, keepdims=True)
            o_ref[:, hs] = (_dot(p, vp_ref[win, hs]) / l).astype(o_ref.dtype)
            lse_ref[:, h:h + 1] = m + jnp.log(l)

    padded = pl.BlockSpec((t + CA_LEFT, GROUP_WIDTH), lambda i: (0, 0))
    return pl.pallas_call(
        body, grid=(t // CA_Q_BLOCK,),
        in_specs=[pl.BlockSpec((CA_Q_BLOCK, GROUP_WIDTH), lambda i: (i, 0)), padded, padded,
                  pl.BlockSpec((ATT_HEADS, CA_BASE), lambda i: (0, 0))],
        out_specs=[pl.BlockSpec((CA_Q_BLOCK, GROUP_WIDTH), lambda i: (i, 0)),
                   pl.BlockSpec((CA_Q_BLOCK, LANES), lambda i: (i, 0))],
        out_shape=[jax.ShapeDtypeStruct((t, GROUP_WIDTH), ACT_DTYPE), jax.ShapeDtypeStruct((t, LANES), F32)],
        compiler_params=_params("parallel"), name="ca_fwd")(pmm, kp, vp, base)


def _ca_bwd(pmm, kp, vp, base, lse, dmix):
    t = pmm.shape[0]
    scale = HEAD_DIM ** -0.5

    def body(q_ref, kp_ref, vp_ref, base_ref, lse_ref, do_ref, dq_ref, dkp_ref, dvp_ref, dbase_ref):
        i = pl.program_id(0)

        @pl.when(i == 0)
        def _():
            dkp_ref[...] = jnp.zeros_like(dkp_ref)
            dvp_ref[...] = jnp.zeros_like(dvp_ref)
            dbase_ref[...] = jnp.zeros_like(dbase_ref)

        win = pl.ds(pl.multiple_of(i * CA_Q_BLOCK, CA_Q_BLOCK), CA_WINDOW)
        mask = _ca_mask(i)
        flip = (_iota((CA_Q_BLOCK, CA_Q_BLOCK), 0) + _iota((CA_Q_BLOCK, CA_Q_BLOCK), 1) == CA_Q_BLOCK - 1).astype(F32)
        for h in range(ATT_HEADS):
            hs = slice(h * HEAD_DIM, (h + 1) * HEAD_DIM)
            s = _ca_scores(q_ref, kp_ref, base_ref, win, h, mask)
            p = jnp.exp(s - lse_ref[:, h:h + 1])
            do = do_ref[:, hs]
            dp = _dot(do, vp_ref[win, hs], 1, 1)
            ds = p * (dp - jnp.sum(p * dp, axis=-1, keepdims=True))
            dq_ref[:, hs] = (_dot(ds, kp_ref[win, hs]) * scale).astype(dq_ref.dtype)
            dkp_ref[win, hs] += _dot(ds, q_ref[:, hs], 0, 0) * scale
            dvp_ref[win, hs] += _dot(p, do, 0, 0)
            rev = jnp.concatenate([_dot(flip, ds), jnp.zeros((CA_Q_BLOCK, CA_BASE - CA_WINDOW), F32)], axis=1)
            lined = pltpu.roll(rev, 1, 1, stride=1, stride_axis=0)
            dbase_ref[h:h + 1, :] += jnp.sum(lined, axis=0, keepdims=True)

    padded = pl.BlockSpec((t + CA_LEFT, GROUP_WIDTH), lambda i: (0, 0))
    return pl.pallas_call(
        body, grid=(t // CA_Q_BLOCK,),
        in_specs=[pl.BlockSpec((CA_Q_BLOCK, GROUP_WIDTH), lambda i: (i, 0)), padded, padded,
                  pl.BlockSpec((ATT_HEADS, CA_BASE), lambda i: (0, 0)),
                  pl.BlockSpec((CA_Q_BLOCK, LANES), lambda i: (i, 0)),
                  pl.BlockSpec((CA_Q_BLOCK, GROUP_WIDTH), lambda i: (i, 0))],
        out_specs=[pl.BlockSpec((CA_Q_BLOCK, GROUP_WIDTH), lambda i: (i, 0)), padded, padded,
                   pl.BlockSpec((ATT_HEADS, CA_BASE), lambda i: (0, 0))],
        out_shape=[jax.ShapeDtypeStruct((t, GROUP_WIDTH), ACT_DTYPE),
                   jax.ShapeDtypeStruct((t + CA_LEFT, GROUP_WIDTH), F32),
                   jax.ShapeDtypeStruct((t + CA_LEFT, GROUP_WIDTH), F32),
                   jax.ShapeDtypeStruct((ATT_HEADS, CA_BASE), F32)],
        compiler_params=_params("arbitrary"), name="ca_bwd")(pmm, kp, vp, base, lse, dmix)


GELU_C = 0.7978845608028654
GELU_A = 0.044715


def _shift_down(v, k, fill):
    return jnp.where(_iota(v.shape, 0) >= k, pltpu.roll(v, k, 0), fill)


def _shift_up(v, k, fill):
    t = v.shape[0]
    return jnp.where(_iota(v.shape, 0) < t - k, pltpu.roll(v, t - k, 0), fill)


def _linear_scan(a, b, shift):
    k = 1
    while k < a.shape[0]:
        b = a * shift(b, k, 0.0) + b
        a = a * shift(a, k, 1.0)
        k *= 2
    return b


def _neg_expm1(y):
    series = -y * (1.0 + y * (0.5 + y * (1.0 / 6.0 + y * (1.0 / 24.0 + y * (1.0 / 120.0)))))
    return jnp.where(y > -0.1, series, 1.0 - jnp.exp(y))


def _lru_forward(x, g_in, cw, cb, wa, ba, wx, bx, lam):
    xs = [_shift_down(x, CONV_WIDTH - 1 - j, 0.0) for j in range(CONV_WIDTH - 1)] + [x]
    xc = cb + sum(cw[j:j + 1, :] * xs[j] for j in range(CONV_WIDTH))
    r = jax.nn.sigmoid(_dot(xc, wa) + ba)
    i = jax.nn.sigmoid(_dot(xc, wx) + bx)
    lsl = _log_sigmoid(lam)
    la = LRU_C * r * lsl
    a = jnp.exp(la)
    s = jnp.sqrt(_neg_expm1(2.0 * la))
    h = _linear_scan(a, s * (i * xc), _shift_down)
    u = GELU_C * (g_in + GELU_A * g_in * g_in * g_in)
    th = jnp.tanh(u)
    gelu = 0.5 * g_in * (1.0 + th)
    return xs, xc, r, i, lsl, a, s, h, th, gelu


def _lru_specs(t):
    col = lambda off: pl.BlockSpec((t, LANES), lambda j: (0, j + off))
    vec = pl.BlockSpec((1, LANES), lambda j: (0, j))
    mat = pl.BlockSpec((None, LANES, LANES), lambda j: (j, 0, 0))
    return [col(0), col(GROUP_WIDTH // LANES), pl.BlockSpec((CONV_WIDTH, LANES), lambda j: (0, j)),
            vec, mat, vec, mat, vec, vec]


def _lru_fwd(pel, conv_w, conv_b, wa, ba, wx, bx, lam):
    t = pel.shape[0]

    def body(g_ref, x_ref, cw_ref, cb_ref, wa_ref, ba_ref, wx_ref, bx_ref, lam_ref, o_ref):
        res = _lru_forward(x_ref[...], g_ref[...], cw_ref[...], cb_ref[...], wa_ref[...], ba_ref[...],
                           wx_ref[...], bx_ref[...], lam_ref[...])
        o_ref[...] = (res[7] * res[9]).astype(o_ref.dtype)

    return pl.pallas_call(
        body, grid=(GROUP_WIDTH // LANES,), in_specs=_lru_specs(t),
        out_specs=pl.BlockSpec((t, LANES), lambda j: (0, j)),
        out_shape=jax.ShapeDtypeStruct((t, GROUP_WIDTH), ACT_DTYPE),
        compiler_params=_params("parallel"), name="lru_fwd")(pel, pel, conv_w, conv_b, wa, ba, wx, bx, lam)


def _lru_bwd(pel, conv_w, conv_b, wa, ba, wx, bx, lam, dmix):
    t = pel.shape[0]

    def body(g_ref, x_ref, cw_ref, cb_ref, wa_ref, ba_ref, wx_ref, bx_ref, lam_ref, do_ref,
             dg_ref, dx_ref, dcw_ref, dcb_ref, dwa_ref, dba_ref, dwx_ref, dbx_ref, dlam_ref):
        g_in, cw, lam = g_ref[...], cw_ref[...], lam_ref[...]
        xs, xc, r, i, lsl, a, s, h, th, gelu = _lru_forward(
            x_ref[...], g_in, cw, cb_ref[...], wa_ref[...], ba_ref[...], wx_ref[...], bx_ref[...], lam)
        dout = do_ref[...]
        dgelu = 0.5 * (1.0 + th) + 0.5 * g_in * (1.0 - th * th) * GELU_C * (1.0 + 3.0 * GELU_A * g_in * g_in)
        dg_ref[...] = (dout * h * dgelu).astype(dg_ref.dtype)
        gsum = _linear_scan(_shift_up(a, 1, 0.0), dout * gelu, _shift_up)
        da = gsum * _shift_down(h, 1, 0.0)
        di = gsum * s * xc
        dla = da * a - gsum * (i * xc) * (a * a / s)
        dlam_ref[...] = jnp.sum(dla * (LRU_C * r), axis=0, keepdims=True) * jax.nn.sigmoid(-lam)
        dpr = dla * (LRU_C * lsl) * r * (1.0 - r)
        dpi = di * i * (1.0 - i)
        dxc = gsum * s * i + _dot(dpr, wa_ref[...], 1, 1) + _dot(dpi, wx_ref[...], 1, 1)
        xct = xc.T
        dwa_ref[...] = _dot(xct, dpr)
        dwx_ref[...] = _dot(xct, dpi)
        dba_ref[...] = jnp.sum(dpr, axis=0, keepdims=True)
        dbx_ref[...] = jnp.sum(dpi, axis=0, keepdims=True)
        dcb_ref[...] = jnp.sum(dxc, axis=0, keepdims=True)
        for j in range(CONV_WIDTH):
            dcw_ref[j:j + 1, :] = jnp.sum(dxc * xs[j], axis=0, keepdims=True)
        dx = cw[CONV_WIDTH - 1:CONV_WIDTH, :] * dxc
        for j in range(CONV_WIDTH - 1):
            dx = dx + cw[j:j + 1, :] * _shift_up(dxc, CONV_WIDTH - 1 - j, 0.0)
        dx_ref[...] = dx.astype(dx_ref.dtype)

    col = pl.BlockSpec((t, LANES), lambda j: (0, j))
    vec = pl.BlockSpec((1, LANES), lambda j: (0, j))
    mat = pl.BlockSpec((None, LANES, LANES), lambda j: (j, 0, 0))
    nb = GROUP_WIDTH // LANES
    vshape = jax.ShapeDtypeStruct((1, GROUP_WIDTH), F32)
    mshape = jax.ShapeDtypeStruct((nb, LANES, LANES), F32)
    return pl.pallas_call(
        body, grid=(nb,),
        in_specs=_lru_specs(t) + [pl.BlockSpec((t, LANES), lambda j: (0, j + nb))],
        out_specs=[col, col, pl.BlockSpec((CONV_WIDTH, LANES), lambda j: (0, j)), vec, mat, vec, mat, vec, vec],
        out_shape=[jax.ShapeDtypeStruct((t, GROUP_WIDTH), ACT_DTYPE), jax.ShapeDtypeStruct((t, GROUP_WIDTH), ACT_DTYPE),
                   jax.ShapeDtypeStruct((CONV_WIDTH, GROUP_WIDTH), F32), vshape, mshape, vshape, mshape, vshape, vshape],
        compiler_params=_params("parallel"), name="lru_bwd")(
            pel, pel, conv_w, conv_b, wa, ba, wx, bx, lam, dmix)


def _block_diag_pairs(w):
    z = jnp.zeros((LRU_BLOCK_DIM, LRU_BLOCK_DIM), w.dtype)
    return jnp.stack([jnp.block([[w[2 * j], z], [z, w[2 * j + 1]]]) for j in range(w.shape[0] // 2)])


def _block_diag_pairs_grad(dw):
    b = LRU_BLOCK_DIM
    return jnp.stack([dw[n // 2, (n % 2) * b:(n % 2 + 1) * b, (n % 2) * b:(n % 2 + 1) * b] for n in range(2 * dw.shape[0])])


ANY = pl.BlockSpec(memory_space=pl.ANY)


def _all_gather(x, name):
    def body(x_ref, out_ref, send_sems, recv_sems, local_sem):
        x, y, c = lax.axis_index("x"), lax.axis_index("y"), lax.axis_index("c")
        me, sibling = (x, y, c), (x, y, 1 - c)
        chips = [(1 - x, y), (x, 1 - y), (1 - x, 1 - y)]

        def rows(px, py, pc):
            return out_ref.at[4 * px + 2 * py + pc]

        def copy(k, block, to, src=None):
            return pltpu.make_async_remote_copy(
                src_ref=rows(*block) if src is None else src, dst_ref=rows(*block),
                send_sem=send_sems.at[k], recv_sem=recv_sems.at[k], device_id=to, device_id_type=MESH)

        mine = pltpu.make_async_copy(x_ref, rows(*me), local_sem)
        mine.start()
        first = [copy(0, me, sibling, src=x_ref)]
        first += [copy(1 + j, me, (*chip, c), src=x_ref) for j, chip in enumerate(chips)]
        for cp in first:
            cp.start()
        passed = [copy(4 + j, (*chip, c), sibling) for j, chip in enumerate(chips)]
        for j, chip in enumerate(chips):
            copy(1 + j, (*chip, c), me).wait_recv()
            passed[j].start()
        copy(0, sibling, me).wait_recv()
        for j, chip in enumerate(chips):
            copy(4 + j, (*chip, 1 - c), me).wait_recv()
        for cp in first + passed:
            cp.wait_send()
        mine.wait()

    return pl.pallas_call(
        body, out_shape=jax.ShapeDtypeStruct((N_DEV,) + x.shape, x.dtype), in_specs=[ANY], out_specs=ANY,
        scratch_shapes=[pltpu.SemaphoreType.DMA((7,)), pltpu.SemaphoreType.DMA((7,)), pltpu.SemaphoreType.DMA(())],
        name=name)(x)


CHIP_POSITIONS = [(0, 0), (0, 1), (1, 0), (1, 1)]


def _sibling_exchange(g):
    def body(g_ref, own_ref, got_ref, send_sems, recv_sems, local_sems):
        x, y, c = lax.axis_index("x"), lax.axis_index("y"), lax.axis_index("c")
        local, remote = [], []
        for k, (px, py) in enumerate(CHIP_POSITIONS):
            local.append(pltpu.make_async_copy(g_ref.at[4 * px + 2 * py + c], own_ref.at[k], local_sems.at[k]))
            remote.append(pltpu.make_async_remote_copy(
                src_ref=g_ref.at[4 * px + 2 * py + (1 - c)], dst_ref=got_ref.at[k],
                send_sem=send_sems.at[k], recv_sem=recv_sems.at[k], device_id=(x, y, 1 - c), device_id_type=MESH))
        for cp in remote + local:
            cp.start()
        for cp in remote:
            cp.wait_recv()
        for cp in remote:
            cp.wait_send()
        for cp in local:
            cp.wait()

    shape = jax.ShapeDtypeStruct((4,) + g.shape[1:], g.dtype)
    return pl.pallas_call(
        body, out_shape=[shape, shape], in_specs=[ANY], out_specs=[ANY, ANY],
        scratch_shapes=[pltpu.SemaphoreType.DMA((4,)), pltpu.SemaphoreType.DMA((4,)), pltpu.SemaphoreType.DMA((4,))],
        name="rs_sibling_exchange")(g)


def _chip_exchange(s):
    def body(s_ref, out_ref, send_sems, recv_sems, local_sem):
        x, y, c = lax.axis_index("x"), lax.axis_index("y"), lax.axis_index("c")
        my = 2 * x + y
        mine = pltpu.make_async_copy(s_ref.at[my], out_ref.at[my], local_sem)
        mine.start()
        chips = [(1 - x, y), (x, 1 - y), (1 - x, 1 - y)]
        sends = [pltpu.make_async_remote_copy(
            src_ref=s_ref.at[2 * px + py], dst_ref=out_ref.at[my], send_sem=send_sems.at[j], recv_sem=recv_sems.at[j],
            device_id=(px, py, c), device_id_type=MESH) for j, (px, py) in enumerate(chips)]
        for cp in sends:
            cp.start()
        for j, (px, py) in enumerate(chips):
            pltpu.make_async_remote_copy(
                src_ref=s_ref.at[my], dst_ref=out_ref.at[2 * px + py], send_sem=send_sems.at[j],
                recv_sem=recv_sems.at[j], device_id=(px, py, c), device_id_type=MESH).wait_recv()
        for cp in sends:
            cp.wait_send()
        mine.wait()

    return pl.pallas_call(
        body, out_shape=jax.ShapeDtypeStruct(s.shape, s.dtype), in_specs=[ANY], out_specs=ANY,
        scratch_shapes=[pltpu.SemaphoreType.DMA((3,)), pltpu.SemaphoreType.DMA((3,)), pltpu.SemaphoreType.DMA(())],
        name="rs_chip_exchange")(s)


def _pair_sum(a, b):
    p, r, c = a.shape

    def body(a_ref, b_ref, o_ref):
        o_ref[...] = (a_ref[...].astype(F32) + b_ref[...].astype(F32)).astype(o_ref.dtype)

    blk = pl.BlockSpec((1, ROW_TILE, c), lambda k, i: (k, i, 0))
    return pl.pallas_call(body, grid=(p, r // ROW_TILE), in_specs=[blk, blk], out_specs=blk,
                          out_shape=jax.ShapeDtypeStruct(a.shape, a.dtype),
                          compiler_params=_params("parallel", "parallel"), name="rs_pair_sum")(a, b)


def _adamw(parts, w, m, v, name):
    p, r, c = parts.shape
    tile = min(ROW_TILE, r)
    assert r % tile == 0

    def body(p_ref, w_ref, m_ref, v_ref, g_ref, d_ref, nm_ref, nv_ref):
        g = p_ref[0].astype(F32)
        for k in range(1, p):
            g = g + p_ref[k].astype(F32)
        nm = ADAM_B1 * m_ref[...] + (1.0 - ADAM_B1) * g
        nv = ADAM_B2 * v_ref[...] + (1.0 - ADAM_B2) * jnp.square(g)
        m_hat = nm / (1.0 - ADAM_B1 ** ADAM_STEP)
        v_hat = nv / (1.0 - ADAM_B2 ** ADAM_STEP)
        g_ref[...] = g
        d_ref[...] = -ADAM_LR * (m_hat / (jnp.sqrt(v_hat) + ADAM_EPS) + ADAM_WD * w_ref[...])
        nm_ref[...] = nm
        nv_ref[...] = nv

    blk = pl.BlockSpec((tile, c), lambda i: (i, 0))
    out = jax.ShapeDtypeStruct((r, c), F32)
    return pl.pallas_call(body, grid=(r // tile,),
                          in_specs=[pl.BlockSpec((p, tile, c), lambda i: (0, i, 0)), blk, blk, blk],
                          out_specs=[blk, blk, blk, blk], out_shape=[out, out, out, out],
                          compiler_params=_params("parallel"), name=name)(parts, w, m, v)


SLAB_COLS = 1024
SHARDED = {"norm_w": 2, "w_in_even": 2, "gla_w_a_up": 2, "w_out_even": 1, "w_in_odd": 2, "conv_w": 2, "conv_b": 1,
           "lru_b_a": 1, "lru_b_x": 1, "lru_lambda": 1, "w_out_odd": 1, "w_mlp_up": 2, "w_mlp_down": 1}
REPLICATED = ["gla_b_a", "gla_norm_w", "fox_b_f", "rel_bias", "lru_w_a", "lru_w_x"]
WEIGHTS = ["norm_w", "w_in_even", "gla_w_a_up", "gla_b_a", "gla_norm_w", "fox_b_f", "w_out_even", "w_in_odd",
           "rel_bias", "conv_w", "conv_b", "lru_w_a", "lru_b_a", "lru_w_x", "lru_b_x", "lru_lambda", "w_out_odd",
           "w_mlp_up", "w_mlp_down"]
MATRICES = ("w_in_even", "w_out_even", "w_in_odd", "w_out_odd", "w_mlp_up", "w_mlp_down")
SHARD_SLAB_ROWS = 3072
BIG_SLAB_ROWS = 3024
SMALL_SLAB_ROWS = 8
REPL_SLAB_ROWS = 72


def _rows_of(shape):
    n = 1
    for s in shape:
        n *= s
    return -(-n // SLAB_COLS), n


def _pack(arrays, total_rows, lead=()):
    parts, used = [], 0
    for a in arrays:
        rows, n = _rows_of(a.shape[len(lead):])
        flat = a.reshape(lead + (n,))
        flat = jnp.pad(flat, [(0, 0)] * len(lead) + [(0, rows * SLAB_COLS - n)])
        parts.append(flat.reshape(lead + (rows, SLAB_COLS)))
        used += rows
    parts.append(jnp.zeros(lead + (total_rows - used, SLAB_COLS), parts[0].dtype))
    return jnp.concatenate(parts, axis=len(lead))


def _unpack(slab, shapes, lead=()):
    out, row = [], 0
    for shape in shapes:
        rows, n = _rows_of(shape)
        seg = lax.slice_in_dim(slab, row, row + rows, axis=len(lead))
        out.append(seg.reshape(lead + (rows * SLAB_COLS,))[..., :n].reshape(lead + tuple(shape)))
        row += rows
    return out


def _join_shards(blocks, axis):
    moved = jnp.moveaxis(blocks, 0, axis)
    shape = moved.shape
    return moved.reshape(shape[:axis] + (shape[axis] * shape[axis + 1],) + shape[axis + 2:])


def _split_shards(full, axis):
    shape = full.shape
    cut = full.reshape(shape[:axis] + (N_DEV, shape[axis] // N_DEV) + shape[axis + 1:])
    return jnp.moveaxis(cut, axis, 0)


EVEN_SPLITS = (0, 256, 512, 1024, 1536, 1552, 2064, 2576, 3088, 3096)


def _even_in_split(w):
    c = [w[:, EVEN_SPLITS[k]:EVEN_SPLITS[k + 1]] for k in range(9)]
    gq, gk, gv, gr, ga, fq, fk, fv, ff = c
    padcols = lambda a: jnp.pad(a, ((0, 0), (0, LANES - a.shape[1])))
    return jnp.concatenate([gq, gk, gv, fq, fk, fv], axis=1), jnp.concatenate([gr, padcols(ga), padcols(ff)], axis=1)


def _even_in_merge(dmm, dele):
    return jnp.concatenate([dmm[:, :1024], dele[:, :512], dele[:, 512:512 + GLA_RANK], dmm[:, 1024:2560],
                            dele[:, 640:640 + ATT_HEADS]], axis=1)


def _local_step(x, target, w):
    t = x.shape[0]
    nw = w["norm_w"]
    nrm = lambda l, k: nw[l, k][None, :]
    g = {}
    dnorm = {}

    def mlp_fwd(xin, layer):
        h = _norm_fwd(xin, nrm(layer, 2), out_dtype=ACT_DTYPE, name=f"norm_mlp_{layer}")
        u = _mm(h, w["w_mlp_up"][layer], out_dtype=ACT_DTYPE, tm=1024, tn=512, name=f"mlp_up_{layer}")
        yv = _mm(u, w["w_mlp_down"][layer], out_dtype=F32, tm=512, tn=512, a_sqrelu=True, name=f"mlp_down_{layer}")
        xout = _norm_fwd(yv, nrm(layer, 3), out_dtype=F32, res=xin, name=f"norm_mlp_out_{layer}")
        return xout, (xin, h, u, yv)

    def mlp_bwd(dxout, saved, layer):
        xin, h, u, yv = saved
        dy, dnorm[(layer, 3)] = _norm_bwd(dxout, yv, nrm(layer, 3), out_dtype=ACT_DTYPE, name=f"norm_mlp_out_bwd_{layer}")
        du = _mm(dy, w["w_mlp_down"][layer], nt=True, out_dtype=ACT_DTYPE, tm=512, tn=512, drelu_of=u,
                 name=f"mlp_down_dx_{layer}")
        dw_down = _mm(u.T, dy, out_dtype=F32, tm=512, tn=512, a_sqrelu=True, name=f"mlp_down_dw_{layer}")
        dw_up = _mm(h.T, du, out_dtype=F32, tm=512, tn=512, name=f"mlp_up_dw_{layer}")
        dh = _mm(du, w["w_mlp_up"][layer], nt=True, out_dtype=F32, tm=512, tn=512, name=f"mlp_up_dx_{layer}")
        dxin, dnorm[(layer, 2)] = _norm_bwd(dh, xin, nrm(layer, 2), out_dtype=F32, add=dxout, name=f"norm_mlp_bwd_{layer}")
        return dxin, dw_up, dw_down

    wmm_e, wel_e = _even_in_split(w["w_in_even"][0])
    w_up_pad = jnp.pad(w["gla_w_a_up"][0], ((0, LANES - GLA_RANK), (0, 0)))
    b_f_pad = jnp.pad(w["fox_b_f"], ((0, 0), (0, LANES - ATT_HEADS)))
    h0 = _norm_fwd(x, nrm(0, 0), out_dtype=ACT_DTYPE, name="norm_in_0")
    pmm0 = _mm(h0, wmm_e, out_dtype=ACT_DTYPE, tm=1024, tn=512, name="in_even_mm")
    pel0 = _mm(h0, wel_e, out_dtype=F32, tm=1024, tn=768, name="in_even_el")
    out_a, states = _gla_fwd(pmm0, pel0, w_up_pad, w["gla_b_a"], w["gla_norm_w"])
    cum, cum_t = _fox_gate_fwd(pel0, b_f_pad)
    out_b, lse_b = _fox_fwd(pmm0, cum, cum_t)
    mix_in0 = jnp.concatenate([out_a, out_b], axis=1)
    mix0 = _mm(mix_in0, w["w_out_even"][0], out_dtype=F32, tm=1024, tn=512, name="out_even")
    x1 = _norm_fwd(mix0, nrm(0, 1), out_dtype=F32, res=x, name="norm_mix_0")
    x2, mlp0 = mlp_fwd(x1, 0)

    w_in_o = w["w_in_odd"][0]
    n_mm_o = 3 * GROUP_WIDTH
    wa_bd, wx_bd = _block_diag_pairs(w["lru_w_a"][0]), _block_diag_pairs(w["lru_w_x"][0])
    base = _ca_bias_base(w["rel_bias"][0])
    h1 = _norm_fwd(x2, nrm(1, 0), out_dtype=ACT_DTYPE, name="norm_in_1")
    pmm1 = _mm(h1, w_in_o[:, :n_mm_o], out_dtype=ACT_DTYPE, tm=1024, tn=512, name="in_odd_mm")
    pel1 = _mm(h1, w_in_o[:, n_mm_o:], out_dtype=F32, tm=1024, tn=512, name="in_odd_el")
    kp = jnp.pad(pmm1[:, GROUP_WIDTH:2 * GROUP_WIDTH], ((CA_LEFT, 0), (0, 0)))
    vp = jnp.pad(pmm1[:, 2 * GROUP_WIDTH:], ((CA_LEFT, 0), (0, 0)))
    out_c, lse_c = _ca_fwd(pmm1, kp, vp, base)
    lru_args = (pel1, w["conv_w"][0], w["conv_b"], wa_bd, w["lru_b_a"], wx_bd, w["lru_b_x"], w["lru_lambda"])
    out_d = _lru_fwd(*lru_args)
    mix_in1 = jnp.concatenate([out_c, out_d], axis=1)
    mix1 = _mm(mix_in1, w["w_out_odd"][0], out_dtype=F32, tm=1024, tn=512, name="out_odd")
    x3 = _norm_fwd(mix1, nrm(1, 1), out_dtype=F32, res=x2, name="norm_mix_1")
    x4, mlp1 = mlp_fwd(x3, 1)

    loss, dx4 = _loss_fwd_bwd(x4, target)

    dx3, dw_up1, dw_down1 = mlp_bwd(dx4, mlp1, 1)
    dmix1, dnorm[(1, 1)] = _norm_bwd(dx3, mix1, nrm(1, 1), out_dtype=ACT_DTYPE, name="norm_mix_bwd_1")
    g["w_out_odd"] = _mm(mix_in1.T, dmix1, out_dtype=F32, tm=512, tn=512, name="out_odd_dw")[None]
    dmix_in1 = _mm(dmix1, w["w_out_odd"][0], nt=True, out_dtype=F32, tm=512, tn=512, name="out_odd_dx")
    dq_c, dkp, dvp, dbase = _ca_bwd(pmm1, kp, vp, base, lse_c, dmix_in1)
    (dgate, dxin, g_conv_w, g_conv_b, dwa_bd, g_lru_b_a, dwx_bd, g_lru_b_x, g_lru_lambda) = _lru_bwd(*lru_args, dmix_in1)
    dp1 = jnp.concatenate([dq_c, dkp[CA_LEFT:].astype(ACT_DTYPE), dvp[CA_LEFT:].astype(ACT_DTYPE), dgate, dxin], axis=1)
    g["w_in_odd"] = _mm(h1.T, dp1, out_dtype=F32, tm=512, tn=512, name="in_odd_dw")[None]
    dh1 = _mm(dp1, w_in_o, nt=True, out_dtype=F32, tm=512, tn=512, name="in_odd_dx")
    dx2, dnorm[(1, 0)] = _norm_bwd(dh1, x2, nrm(1, 0), out_dtype=F32, add=dx3, name="norm_in_bwd_1")
    g["rel_bias"] = _ca_bias_base_grad(dbase)[None]
    g["conv_w"], g["conv_b"] = g_conv_w[None], g_conv_b
    g["lru_w_a"], g["lru_w_x"] = _block_diag_pairs_grad(dwa_bd)[None], _block_diag_pairs_grad(dwx_bd)[None]
    g["lru_b_a"], g["lru_b_x"], g["lru_lambda"] = g_lru_b_a, g_lru_b_x, g_lru_lambda

    dx1, dw_up0, dw_down0 = mlp_bwd(dx2, mlp0, 0)
    dmix0, dnorm[(0, 1)] = _norm_bwd(dx1, mix0, nrm(0, 1), out_dtype=ACT_DTYPE, name="norm_mix_bwd_0")
    g["w_out_even"] = _mm(mix_in0.T, dmix0, out_dtype=F32, tm=512, tn=512, name="out_even_dw")[None]
    dmix_in0 = _mm(dmix0, w["w_out_even"][0], nt=True, out_dtype=F32, tm=512, tn=512, name="out_even_dx")
    dq_a, dk_a, dv_a, dr_a, da_a, dw_up_pad, g_gla_b_a, g_gla_norm_w = _gla_bwd(
        pmm0, pel0, w_up_pad, w["gla_b_a"], w["gla_norm_w"], states, dmix_in0)
    dq_b, dk_b, dv_b, dcum_t = _fox_bwd(pmm0, cum, cum_t, lse_b, dmix_in0)
    df_b, db_f = _fox_gate_bwd(pel0, b_f_pad, dcum_t)
    dp0 = jnp.concatenate([dq_a, dk_a, dv_a, dq_b, dk_b.astype(ACT_DTYPE), dv_b.astype(ACT_DTYPE), dr_a, da_a, df_b],
                          axis=1)
    w_perm = jnp.concatenate([wmm_e, wel_e], axis=1)
    n_mm_e = wmm_e.shape[1]
    dw_perm = _mm(h0.T, dp0, out_dtype=F32, tm=512, tn=dp0.shape[1] // 2, name="in_even_dw")
    dh0 = _mm(dp0, w_perm, nt=True, out_dtype=F32, tm=512, tn=512, name="in_even_dx")
    dx0, dnorm[(0, 0)] = _norm_bwd(dh0, x, nrm(0, 0), out_dtype=F32, add=dx1, name="norm_in_bwd_0")
    g["w_in_even"] = _even_in_merge(dw_perm[:, :n_mm_e], dw_perm[:, n_mm_e:])[None]
    g["gla_w_a_up"] = dw_up_pad[:GLA_RANK][None]
    g["gla_b_a"], g["gla_norm_w"], g["fox_b_f"] = g_gla_b_a, g_gla_norm_w, db_f[:, :ATT_HEADS]
    g["w_mlp_up"], g["w_mlp_down"] = jnp.stack([dw_up0, dw_up1]), jnp.stack([dw_down0, dw_down1])
    g["norm_w"] = jnp.stack([jnp.concatenate([dnorm[(l, k)] for k in range(4)], axis=0) for l in range(DEPTH)])
    return loss, dx0, g


def kernel(x, norm_w, w_in_even, gla_w_a_up, gla_b_a, gla_norm_w, fox_b_f, w_out_even, w_in_odd, rel_bias, conv_w, conv_b, lru_w_a, lru_b_a, lru_w_x, lru_b_x, lru_lambda, w_out_odd, w_mlp_up, w_mlp_down, loss_target, m_norm_w, m_w_in_even, m_gla_w_a_up, m_gla_b_a, m_gla_norm_w, m_fox_b_f, m_w_out_even, m_w_in_odd, m_rel_bias, m_conv_w, m_conv_b, m_lru_w_a, m_lru_b_a, m_lru_w_x, m_lru_b_x, m_lru_lambda, m_w_out_odd, m_w_mlp_up, m_w_mlp_down, v_norm_w, v_w_in_even, v_gla_w_a_up, v_gla_b_a, v_gla_norm_w, v_fox_b_f, v_w_out_even, v_w_in_odd, v_rel_bias, v_conv_w, v_conv_b, v_lru_w_a, v_lru_b_a, v_lru_w_x, v_lru_b_x, v_lru_lambda, v_w_out_odd, v_w_mlp_up, v_w_mlp_down):
    wts = dict(zip(WEIGHTS, (norm_w, w_in_even, gla_w_a_up, gla_b_a, gla_norm_w, fox_b_f, w_out_even, w_in_odd, rel_bias,
                             conv_w, conv_b, lru_w_a, lru_b_a, lru_w_x, lru_b_x, lru_lambda, w_out_odd, w_mlp_up,
                             w_mlp_down)))
    mom = dict(zip(WEIGHTS, (m_norm_w, m_w_in_even, m_gla_w_a_up, m_gla_b_a, m_gla_norm_w, m_fox_b_f, m_w_out_even,
                             m_w_in_odd, m_rel_bias, m_conv_w, m_conv_b, m_lru_w_a, m_lru_b_a, m_lru_w_x, m_lru_b_x,
                             m_lru_lambda, m_w_out_odd, m_w_mlp_up, m_w_mlp_down)))
    var = dict(zip(WEIGHTS, (v_norm_w, v_w_in_even, v_gla_w_a_up, v_gla_b_a, v_gla_norm_w, v_fox_b_f, v_w_out_even,
                             v_w_in_odd, v_rel_bias, v_conv_w, v_conv_b, v_lru_w_a, v_lru_b_a, v_lru_w_x, v_lru_b_x,
                             v_lru_lambda, v_w_out_odd, v_w_mlp_up, v_w_mlp_down)))
    sharded = list(SHARDED)
    shard_shapes = [wts[n].shape for n in sharded]
    repl_shapes = [wts[n].shape for n in REPLICATED]

    big = [n for n in sharded if n in MATRICES]
    small = [n for n in sharded if n not in MATRICES]
    big_slab = _pack([wts[n].astype(WIRE_DTYPE) for n in big], BIG_SLAB_ROWS)
    small_slab = _pack([wts[n] for n in small], SMALL_SLAB_ROWS)
    big_blocks = _unpack(_all_gather(big_slab, "matrices_all_gather"), [wts[n].shape for n in big], lead=(N_DEV,))
    small_blocks = _unpack(_all_gather(small_slab, "vectors_all_gather"), [wts[n].shape for n in small], lead=(N_DEV,))
    full = {n: _join_shards(b, SHARDED[n]) for n, b in zip(big + small, big_blocks + small_blocks)}
    for n in REPLICATED:
        full[n] = wts[n]
    w_slab = _pack([wts[n] for n in sharded], SHARD_SLAB_ROWS)

    loss_blk, dx, grads = _local_step(x[0], loss_target[0], full)
    loss = lax.psum(loss_blk[0, 0], ("x", "y", "c"))

    g_slab = _pack([_split_shards(grads[n], SHARDED[n]) for n in sharded], SHARD_SLAB_ROWS, lead=(N_DEV,))
    own, got = _sibling_exchange(g_slab.astype(WIRE_DTYPE))
    chip_parts = _chip_exchange(_pair_sum(own, got))
    repl_parts = _all_gather(_pack([grads[n] for n in REPLICATED], REPL_SLAB_ROWS), "replicated_grads_all_gather")

    outs = {}
    sh = _adamw(chip_parts, w_slab, _pack([mom[n] for n in sharded], SHARD_SLAB_ROWS),
                _pack([var[n] for n in sharded], SHARD_SLAB_ROWS), "adamw_sharded")
    rp = _adamw(repl_parts, _pack([wts[n] for n in REPLICATED], REPL_SLAB_ROWS),
                _pack([mom[n] for n in REPLICATED], REPL_SLAB_ROWS),
                _pack([var[n] for n in REPLICATED], REPL_SLAB_ROWS), "adamw_replicated")
    for kind, s_slab, r_slab in zip(("grad", "delta", "new_m", "new_v"), sh, rp):
        vals = dict(zip(sharded, _unpack(s_slab, shard_shapes)))
        vals.update(zip(REPLICATED, _unpack(r_slab, repl_shapes)))
        outs[kind] = [vals[n] for n in WEIGHTS]
    return (loss, dx[None], *outs["grad"], *outs["delta"], *outs["new_m"], *outs["new_v"])
```

```python
import functools

import jax
import jax.numpy as jnp
from jax import lax
from jax.experimental import pallas as pl
from jax.experimental.pallas import tpu as pltpu

F32 = jnp.float32
MXU_DTYPE = jnp.bfloat16
ACT_DTYPE = jnp.bfloat16
WIRE_DTYPE = jnp.bfloat16

V7X_VMEM_BYTES = 64 * 1024 * 1024
VMEM_LIMIT = (V7X_VMEM_BYTES * 7) // 8
LANES = 128

D_MODEL = 1024
SEQ = 2048
DEPTH = 2
CHUNK = 64
GROUP_WIDTH = D_MODEL // 2
D_FF = 4 * D_MODEL
NORM_EPS = 1e-6
GLA_HEADS = 4
GLA_DV = GROUP_WIDTH // GLA_HEADS
GLA_DK = GLA_DV // 2
GLA_KW = GLA_HEADS * GLA_DK
GLA_RANK = 16
GLA_GATE_TAU = 16.0
HEAD_DIM = 64
ATT_HEADS = GROUP_WIDTH // HEAD_DIM
CA_LEFT = 8 * CHUNK
REL_CLIP = 128
LRU_BLOCK_DIM = 64
CONV_WIDTH = 4
LRU_C = 8.0
N_DEV = 8

ADAM_LR = 0.001
ADAM_B1 = 0.9
ADAM_B2 = 0.999
ADAM_EPS = 1e-08
ADAM_WD = 0.01
ADAM_STEP = 10

NEG = float(jnp.finfo(jnp.float32).min)
MESH = pl.DeviceIdType.MESH


def _params(*sem):
    return pltpu.CompilerParams(dimension_semantics=sem, vmem_limit_bytes=VMEM_LIMIT)


def _dot(a, b, ca=1, cb=0):
    return lax.dot_general(a.astype(MXU_DTYPE), b.astype(MXU_DTYPE), (((ca,), (cb,)), ((), ())),
                           preferred_element_type=F32)


def _dot_exact(a, b):
    return lax.dot_general(a, b, (((1,), (0,)), ((), ())), precision=lax.Precision.HIGHEST,
                           preferred_element_type=F32)


def _log_sigmoid(x):
    return jnp.minimum(x, 0.0) - jnp.log1p(jnp.exp(-jnp.abs(x)))


def _iota(shape, axis):
    return lax.broadcasted_iota(jnp.int32, shape, axis)


def _mm(a, b, *, nt=False, out_dtype, tm, tn, a_sqrelu=False, drelu_of=None, b_blocked=False, out_blocked=False,
        name):
    m, k = a.shape
    if b_blocked:
        assert not nt and b.shape[1] == k and b.shape[2] == tn
        n = b.shape[0] * tn
    else:
        n = b.shape[0] if nt else b.shape[1]
        assert (b.shape[1] if nt else b.shape[0]) == k
    tm, tn = min(tm, m), min(tn, n)
    assert m % tm == 0 and n % tn == 0

    def body(*refs):
        a_ref, b_ref = refs[0], refs[1]
        o_ref = refs[-1]
        av = a_ref[...]
        if a_sqrelu:
            av = jnp.square(jnp.maximum(av.astype(F32), 0.0))
        acc = _dot(av, b_ref[...], 1, 1 if nt else 0)
        if drelu_of is not None:
            acc = acc * (2.0 * jnp.maximum(refs[2][...].astype(F32), 0.0))
        o_ref[...] = acc.astype(out_dtype)

    if b_blocked:
        b_spec = pl.BlockSpec((None, k, tn), lambda i, j: (j, 0, 0))
    elif nt:
        b_spec = pl.BlockSpec((tn, k), lambda i, j: (j, 0))
    else:
        b_spec = pl.BlockSpec((k, tn), lambda i, j: (0, j))
    in_specs = [pl.BlockSpec((tm, k), lambda i, j: (i, 0)), b_spec]
    args = [a, b]
    if drelu_of is not None:
        in_specs.append(pl.BlockSpec((tm, tn), lambda i, j: (i, j)))
        args.append(drelu_of)
    if out_blocked:
        out_spec = pl.BlockSpec((None, tm, tn), lambda i, j: (j, i, 0))
        out_shape = jax.ShapeDtypeStruct((n // tn, m, tn), out_dtype)
    else:
        out_spec = pl.BlockSpec((tm, tn), lambda i, j: (i, j))
        out_shape = jax.ShapeDtypeStruct((m, n), out_dtype)
    return pl.pallas_call(
        body, grid=(m // tm, n // tn), in_specs=in_specs, out_specs=out_spec, out_shape=out_shape,
        compiler_params=_params("parallel", "parallel"), name=name)(*args)


def _mm_nt_blocked(a, b, *, out_dtype, tm, tn, name):
    m = a.shape[0]
    p, n, kp = b.shape
    assert a.shape[1] == p * kp and m % tm == 0 and n % tn == 0

    def body(a_ref, b_ref, o_ref, acc_ref):
        @pl.when(pl.program_id(2) == 0)
        def _():
            acc_ref[...] = jnp.zeros_like(acc_ref)

        acc_ref[...] += _dot(a_ref[...], b_ref[...], 1, 1)

        @pl.when(pl.program_id(2) == p - 1)
        def _():
            o_ref[...] = acc_ref[...].astype(out_dtype)

    return pl.pallas_call(
        body, grid=(m // tm, n // tn, p),
        in_specs=[pl.BlockSpec((tm, kp), lambda i, j, q: (i, q)), pl.BlockSpec((None, tn, kp), lambda i, j, q: (q, j, 0))],
        out_specs=pl.BlockSpec((tm, tn), lambda i, j, q: (i, j)),
        out_shape=jax.ShapeDtypeStruct((m, n), out_dtype),
        scratch_shapes=[pltpu.VMEM((tm, tn), F32)],
        compiler_params=_params("parallel", "parallel", "arbitrary"), name=name)(a, b)


ROW_TILE = 256


def _norm_fwd(x, w, *, out_dtype, res=None, name):
    t, d = x.shape

    def body(*refs):
        x_ref, w_ref, o_ref = refs[0], refs[1], refs[-1]
        xv = x_ref[...]
        y = xv * lax.rsqrt(jnp.mean(xv * xv, axis=-1, keepdims=True) + NORM_EPS) * w_ref[...]
        if res is not None:
            y = refs[2][...] + y
        o_ref[...] = y.astype(out_dtype)

    row = pl.BlockSpec((ROW_TILE, d), lambda i: (i, 0))
    in_specs = [row, pl.BlockSpec((1, d), lambda i: (0, 0))] + ([row] if res is not None else [])
    args = [x, w] + ([res] if res is not None else [])
    return pl.pallas_call(body, grid=(t // ROW_TILE,), in_specs=in_specs, out_specs=row,
                          out_shape=jax.ShapeDtypeStruct((t, d), out_dtype),
                          compiler_params=_params("parallel"), name=name)(*args)


def _norm_bwd(dy, x, w, *, out_dtype, add=None, name):
    t, d = x.shape

    def body(*refs):
        dy_ref, x_ref, w_ref = refs[0], refs[1], refs[2]
        dx_ref, dw_ref = refs[-2], refs[-1]
        xv = x_ref[...]
        rstd = lax.rsqrt(jnp.mean(xv * xv, axis=-1, keepdims=True) + NORM_EPS)
        xhat = xv * rstd
        dyv = dy_ref[...].astype(F32)
        g = dyv * w_ref[...]
        dx = rstd * (g - xhat * jnp.mean(g * xhat, axis=-1, keepdims=True))
        if add is not None:
            dx = dx + refs[3][...]
        dx_ref[...] = dx.astype(out_dtype)

        @pl.when(pl.program_id(0) == 0)
        def _():
            dw_ref[...] = jnp.zeros_like(dw_ref)

        dw_ref[...] += jnp.sum(dyv * xhat, axis=0, keepdims=True)

    row = pl.BlockSpec((ROW_TILE, d), lambda i: (i, 0))
    vec = pl.BlockSpec((1, d), lambda i: (0, 0))
    in_specs = [row, row, vec] + ([row] if add is not None else [])
    args = [dy, x, w] + ([add] if add is not None else [])
    return pl.pallas_call(body, grid=(t // ROW_TILE,), in_specs=in_specs, out_specs=[row, vec],
                          out_shape=[jax.ShapeDtypeStruct((t, d), out_dtype), jax.ShapeDtypeStruct((1, d), F32)],
                          compiler_params=_params("arbitrary"), name=name)(*args)


def _loss_fwd_bwd(y, target):
    t, d = y.shape

    def body(y_ref, t_ref, l_ref, dy_ref):
        diff = y_ref[...] - t_ref[...]
        dy_ref[...] = diff * (1.0 / d)

        @pl.when(pl.program_id(0) == 0)
        def _():
            l_ref[...] = jnp.zeros_like(l_ref)

        l_ref[...] += 0.5 * jnp.sum(jnp.mean(diff * diff, axis=-1, keepdims=True), axis=0, keepdims=True)

    row = pl.BlockSpec((ROW_TILE, d), lambda i: (i, 0))
    return pl.pallas_call(body, grid=(t // ROW_TILE,), in_specs=[row, row],
                          out_specs=[pl.BlockSpec((8, LANES), lambda i: (0, 0)), row],
                          out_shape=[jax.ShapeDtypeStruct((8, LANES), F32), jax.ShapeDtypeStruct((t, d), F32)],
                          compiler_params=_params("arbitrary"), name="loss")(y, target)


def _gla_specs(t):
    return [pl.BlockSpec((t, GLA_KW), lambda i: (0, 0)),
            pl.BlockSpec((t, GLA_KW), lambda i: (0, 1)),
            pl.BlockSpec((t, GROUP_WIDTH), lambda i: (0, 1)),
            pl.BlockSpec((t, GROUP_WIDTH), lambda i: (0, 0)),
            pl.BlockSpec((t, LANES), lambda i: (0, 4)),
            pl.BlockSpec((LANES, GLA_KW), lambda i: (0, 0)),
            pl.BlockSpec((1, GLA_KW), lambda i: (0, 0)),
            pl.BlockSpec((1, GROUP_WIDTH), lambda i: (0, 0))]


def _gla_fwd(pmm, pel, w_up, b_a, gnorm_w):
    t = pmm.shape[0]
    nc = t // CHUNK
    scale = GLA_DK ** -0.5

    def body(q_ref, k_ref, v_ref, r_ref, a_ref, wup_ref, ba_ref, gw_ref, o_ref, st_ref, la_scr, s_scr):
        z = _dot(a_ref[...], wup_ref[...]) + ba_ref[...]
        la_scr[...] = _log_sigmoid(z) * (1.0 / GLA_GATE_TAU)
        s_scr[...] = jnp.zeros_like(s_scr)
        tri = (_iota((CHUNK, CHUNK), 1) <= _iota((CHUNK, CHUNK), 0)).astype(F32)

        def chunk(c, carry):
            rows = pl.ds(pl.multiple_of(c * CHUNK, CHUNK), CHUNK)
            cum = _dot_exact(tri, la_scr[rows, :])
            tot = cum[CHUNK - 1:CHUNK, :]
            kd = k_ref[rows, :].astype(F32) * jnp.exp(tot - cum)
            decay = jnp.exp(tot)
            qs = q_ref[rows, :].astype(F32) * scale
            vv = v_ref[rows, :].astype(F32)
            rr = r_ref[rows, :]
            gate = rr * jax.nn.sigmoid(rr) * gw_ref[...]
            for h in range(GLA_HEADS):
                ks = slice(h * GLA_DK, (h + 1) * GLA_DK)
                vs = slice(h * GLA_DV, (h + 1) * GLA_DV)
                inc_t = _dot(vv[:, vs].T, kd[:, ks])
                s_t = s_scr[vs, :] * decay[:, ks] + inc_t
                s_scr[vs, :] = s_t
                st_ref[c, vs, :] = s_t
                o = _dot(qs[:, ks], s_t, 1, 1)
                y = o * lax.rsqrt(jnp.mean(o * o, axis=-1, keepdims=True) + NORM_EPS)
                o_ref[rows, vs] = (y * gate[:, vs]).astype(o_ref.dtype)
            return carry

        lax.fori_loop(0, nc, chunk, 0)

    return pl.pallas_call(
        body, grid=(1,), in_specs=_gla_specs(t),
        out_specs=[pl.BlockSpec((t, GROUP_WIDTH), lambda i: (0, 0)),
                   pl.BlockSpec((nc, GLA_HEADS * GLA_DV, GLA_DK), lambda i: (0, 0, 0))],
        out_shape=[jax.ShapeDtypeStruct((t, GROUP_WIDTH), ACT_DTYPE),
                   jax.ShapeDtypeStruct((nc, GLA_HEADS * GLA_DV, GLA_DK), F32)],
        scratch_shapes=[pltpu.VMEM((t, GLA_KW), F32), pltpu.VMEM((GLA_HEADS * GLA_DV, GLA_DK), F32)],
        compiler_params=_params("arbitrary"), name="gla_fwd")(pmm, pmm, pmm, pel, pel, w_up, b_a, gnorm_w)


def _gla_bwd(pmm, pel, w_up, b_a, gnorm_w, states, dmix):
    t = pmm.shape[0]
    nc = t // CHUNK
    scale = GLA_DK ** -0.5

    def body(q_ref, k_ref, v_ref, r_ref, a_ref, wup_ref, ba_ref, gw_ref, st_ref, do_ref,
             dq_ref, dk_ref, dv_ref, dr_ref, da_ref, dwup_ref, dba_ref, dgw_ref, la_scr, dz_scr, ds_scr):
        z = _dot(a_ref[...], wup_ref[...]) + ba_ref[...]
        la_scr[...] = _log_sigmoid(z) * (1.0 / GLA_GATE_TAU)
        ds_scr[...] = jnp.zeros_like(ds_scr)
        dgw_ref[...] = jnp.zeros_like(dgw_ref)
        row_i, col_i = _iota((CHUNK, CHUNK), 0), _iota((CHUNK, CHUNK), 1)
        tri = (col_i <= row_i).astype(F32)
        tri_strict = (col_i < row_i).astype(F32)

        def chunk(n, carry):
            c = nc - 1 - n
            rows = pl.ds(pl.multiple_of(c * CHUNK, CHUNK), CHUNK)
            cum = _dot_exact(tri, la_scr[rows, :])
            tot = cum[CHUNK - 1:CHUNK, :]
            e = jnp.exp(tot - cum)
            kd = k_ref[rows, :].astype(F32) * e
            decay = jnp.exp(tot)
            qs = q_ref[rows, :].astype(F32) * scale
            vv = v_ref[rows, :].astype(F32)
            rr = r_ref[rows, :]
            sig = jax.nn.sigmoid(rr)
            silu = rr * sig
            dsilu = sig * (1.0 + rr * (1.0 - sig))
            dout = do_ref[rows, :]
            gw = gw_ref[...]
            c_prev = jnp.maximum(c - 1, 0)
            has_prev = (c > 0).astype(F32)
            zc = _dot(a_ref[rows, :], wup_ref[...]) + ba_ref[...]
            dz_scale = jax.nn.sigmoid(-zc) * (1.0 / GLA_GATE_TAU)
            for h in range(GLA_HEADS):
                ks = slice(h * GLA_DK, (h + 1) * GLA_DK)
                vs = slice(h * GLA_DV, (h + 1) * GLA_DV)
                s_t = st_ref[c, vs, :]
                s_prev = st_ref[c_prev, vs, :] * has_prev
                o = _dot(qs[:, ks], s_t, 1, 1)
                rstd = lax.rsqrt(jnp.mean(o * o, axis=-1, keepdims=True) + NORM_EPS)
                y = o * rstd
                dg = dout[:, vs]
                dgw_ref[:, vs] += jnp.sum(dg * y * silu[:, vs], axis=0, keepdims=True)
                dr_ref[rows, vs] = (dg * y * gw[:, vs] * dsilu[:, vs]).astype(dr_ref.dtype)
                dy = dg * gw[:, vs] * silu[:, vs]
                d_o = rstd * (dy - y * jnp.mean(dy * y, axis=-1, keepdims=True))
                dq_ref[rows, ks] = (_dot(d_o, s_t) * scale).astype(dq_ref.dtype)
                ds_t = ds_scr[vs, :] + _dot(d_o.T, qs[:, ks])
                dv_ref[rows, vs] = _dot(kd[:, ks], ds_t, 1, 1).astype(dv_ref.dtype)
                dkd = _dot(vv[:, vs], ds_t)
                ddecay = jnp.sum(ds_t * s_prev, axis=0, keepdims=True)
                ds_scr[vs, :] = ds_t * decay[:, ks]
                dla = ddecay * decay[:, ks] + _dot_exact(tri_strict, dkd * kd[:, ks])
                dz_scr[rows, ks] = dla * dz_scale[:, ks]
                dk_ref[rows, ks] = (dkd * e[:, ks]).astype(dk_ref.dtype)
            return carry

        lax.fori_loop(0, nc, chunk, 0)
        dz = dz_scr[...]
        da_ref[...] = _dot(dz, wup_ref[...], 1, 1).astype(da_ref.dtype)
        dwup_ref[...] = _dot(a_ref[...].T, dz)
        dba_ref[...] = jnp.sum(dz, axis=0, keepdims=True)

    in_specs = _gla_specs(t) + [
        pl.BlockSpec((nc, GLA_HEADS * GLA_DV, GLA_DK), lambda i: (0, 0, 0)),
        pl.BlockSpec((t, GROUP_WIDTH), lambda i: (0, 0))]
    full = lambda r, c: pl.BlockSpec((r, c), lambda i: (0, 0))
    return pl.pallas_call(
        body, grid=(1,), in_specs=in_specs,
        out_specs=[full(t, GLA_KW), full(t, GLA_KW), full(t, GROUP_WIDTH), full(t, GROUP_WIDTH), full(t, LANES),
                   full(LANES, GLA_KW), full(1, GLA_KW), full(1, GROUP_WIDTH)],
        out_shape=[jax.ShapeDtypeStruct((t, GLA_KW), ACT_DTYPE), jax.ShapeDtypeStruct((t, GLA_KW), ACT_DTYPE),
                   jax.ShapeDtypeStruct((t, GROUP_WIDTH), ACT_DTYPE), jax.ShapeDtypeStruct((t, GROUP_WIDTH), ACT_DTYPE),
                   jax.ShapeDtypeStruct((t, LANES), ACT_DTYPE), jax.ShapeDtypeStruct((LANES, GLA_KW), F32),
                   jax.ShapeDtypeStruct((1, GLA_KW), F32), jax.ShapeDtypeStruct((1, GROUP_WIDTH), F32)],
        scratch_shapes=[pltpu.VMEM((t, GLA_KW), F32), pltpu.VMEM((t, GLA_KW), F32),
                        pltpu.VMEM((GLA_HEADS * GLA_DV, GLA_DK), F32)],
        compiler_params=_params("arbitrary"), name="gla_bwd")(
            pmm, pmm, pmm, pel, pel, w_up, b_a, gnorm_w, states, dmix)


CUM_BLOCK = 256


def _fox_gate_fwd(pel, b_f):
    t = pel.shape[0]

    def body(f_ref, b_ref, cum_ref, cum_t_ref):
        tri = (_iota((CUM_BLOCK, CUM_BLOCK), 1) <= _iota((CUM_BLOCK, CUM_BLOCK), 0)).astype(F32)
        carry = jnp.zeros((1, LANES), F32)
        for blk in range(t // CUM_BLOCK):
            rows = slice(blk * CUM_BLOCK, (blk + 1) * CUM_BLOCK)
            cum = _dot_exact(tri, _log_sigmoid(f_ref[rows, :] + b_ref[...])) + carry
            cum_ref[rows, :] = cum
            carry = cum[CUM_BLOCK - 1:CUM_BLOCK, :]
        cum_t_ref[...] = cum_ref[...].T

    return pl.pallas_call(
        body, grid=(1,),
        in_specs=[pl.BlockSpec((t, LANES), lambda i: (0, 5)), pl.BlockSpec((1, LANES), lambda i: (0, 0))],
        out_specs=[pl.BlockSpec((t, LANES), lambda i: (0, 0)), pl.BlockSpec((LANES, t), lambda i: (0, 0))],
        out_shape=[jax.ShapeDtypeStruct((t, LANES), F32), jax.ShapeDtypeStruct((LANES, t), F32)],
        compiler_params=_params("arbitrary"), name="fox_gate_fwd")(pel, b_f)


def _fox_gate_bwd(pel, b_f, dcum_t):
    t = pel.shape[0]

    def body(f_ref, b_ref, dct_ref, df_ref, db_ref, dc_scr):
        dc_scr[...] = dct_ref[...].T
        tri_up = (_iota((CUM_BLOCK, CUM_BLOCK), 1) >= _iota((CUM_BLOCK, CUM_BLOCK), 0)).astype(F32)
        carry = jnp.zeros((1, LANES), F32)
        db = jnp.zeros((1, LANES), F32)
        for blk in reversed(range(t // CUM_BLOCK)):
            rows = slice(blk * CUM_BLOCK, (blk + 1) * CUM_BLOCK)
            dls = _dot_exact(tri_up, dc_scr[rows, :]) + carry
            carry = dls[0:1, :]
            df = dls * jax.nn.sigmoid(-(f_ref[rows, :] + b_ref[...]))
            df_ref[rows, :] = df.astype(df_ref.dtype)
            db = db + jnp.sum(df, axis=0, keepdims=True)
        db_ref[...] = db

    return pl.pallas_call(
        body, grid=(1,),
        in_specs=[pl.BlockSpec((t, LANES), lambda i: (0, 5)), pl.BlockSpec((1, LANES), lambda i: (0, 0)),
                  pl.BlockSpec((LANES, t), lambda i: (0, 0))],
        out_specs=[pl.BlockSpec((t, LANES), lambda i: (0, 0)), pl.BlockSpec((1, LANES), lambda i: (0, 0))],
        out_shape=[jax.ShapeDtypeStruct((t, LANES), ACT_DTYPE), jax.ShapeDtypeStruct((1, LANES), F32)],
        scratch_shapes=[pltpu.VMEM((t, LANES), F32)],
        compiler_params=_params("arbitrary"), name="fox_gate_bwd")(pel, b_f, dcum_t)


FOX_Q_BLOCK = 256


def _fox_scores(q_ref, k_ref, cum_ref, cum_t_ref, h, mask):
    hs = slice(h * HEAD_DIM, (h + 1) * HEAD_DIM)
    s = _dot(q_ref[:, hs], k_ref[:, hs], 1, 1) * (HEAD_DIM ** -0.5)
    s = s + (cum_ref[:, h:h + 1] - cum_t_ref[h:h + 1, :])
    return jnp.where(mask, s, NEG)


def _fox_fwd(pmm, cum, cum_t):
    t = pmm.shape[0]
    bq = FOX_Q_BLOCK

    def body(q_ref, k_ref, v_ref, cum_ref, cum_t_ref, o_ref, lse_ref):
        i = pl.program_id(0)
        mask = _iota((bq, t), 1) <= i * bq + _iota((bq, t), 0)
        lse_ref[...] = jnp.zeros_like(lse_ref)
        for h in range(ATT_HEADS):
            hs = slice(h * HEAD_DIM, (h + 1) * HEAD_DIM)
            s = _fox_scores(q_ref, k_ref, cum_ref, cum_t_ref, h, mask)
            m = jnp.max(s, axis=-1, keepdims=True)
            p = jnp.exp(s - m)
            l = jnp.sum(p, axis=-1, keepdims=True)
            o_ref[:, hs] = (_dot(p, v_ref[:, hs]) / l).astype(o_ref.dtype)
            lse_ref[:, h:h + 1] = m + jnp.log(l)

    return pl.pallas_call(
        body, grid=(t // bq,),
        in_specs=[pl.BlockSpec((bq, GROUP_WIDTH), lambda i: (i, 2)), pl.BlockSpec((t, GROUP_WIDTH), lambda i: (0, 3)),
                  pl.BlockSpec((t, GROUP_WIDTH), lambda i: (0, 4)), pl.BlockSpec((bq, LANES), lambda i: (i, 0)),
                  pl.BlockSpec((8, t), lambda i: (0, 0))],
        out_specs=[pl.BlockSpec((bq, GROUP_WIDTH), lambda i: (i, 0)), pl.BlockSpec((bq, LANES), lambda i: (i, 0))],
        out_shape=[jax.ShapeDtypeStruct((t, GROUP_WIDTH), ACT_DTYPE), jax.ShapeDtypeStruct((t, LANES), F32)],
        compiler_params=_params("parallel"), name="fox_fwd")(pmm, pmm, pmm, cum, cum_t)


def _fox_bwd(pmm, cum, cum_t, lse, dmix):
    t = pmm.shape[0]
    bq = FOX_Q_BLOCK
    scale = HEAD_DIM ** -0.5

    def body(q_ref, k_ref, v_ref, cum_ref, cum_t_ref, lse_ref, do_ref, dq_ref, dk_ref, dv_ref, dct_ref):
        i = pl.program_id(0)

        @pl.when(i == 0)
        def _():
            dk_ref[...] = jnp.zeros_like(dk_ref)
            dv_ref[...] = jnp.zeros_like(dv_ref)
            dct_ref[...] = jnp.zeros_like(dct_ref)

        mask = _iota((bq, t), 1) <= i * bq + _iota((bq, t), 0)
        for h in range(ATT_HEADS):
            hs = slice(h * HEAD_DIM, (h + 1) * HEAD_DIM)
            s = _fox_scores(q_ref, k_ref, cum_ref, cum_t_ref, h, mask)
            p = jnp.exp(s - lse_ref[:, h:h + 1])
            do = do_ref[:, hs]
            dp = _dot(do, v_ref[:, hs], 1, 1)
            ds = p * (dp - jnp.sum(p * dp, axis=-1, keepdims=True))
            dq_ref[:, hs] = (_dot(ds, k_ref[:, hs]) * scale).astype(dq_ref.dtype)
            dk_ref[:, hs] += _dot(ds, q_ref[:, hs], 0, 0) * scale
            dv_ref[:, hs] += _dot(p, do, 0, 0)
            dct_ref[h:h + 1, :] += -jnp.sum(ds, axis=0, keepdims=True)

    whole = pl.BlockSpec((t, GROUP_WIDTH), lambda i: (0, 0))
    return pl.pallas_call(
        body, grid=(t // bq,),
        in_specs=[pl.BlockSpec((bq, GROUP_WIDTH), lambda i: (i, 2)), pl.BlockSpec((t, GROUP_WIDTH), lambda i: (0, 3)),
                  pl.BlockSpec((t, GROUP_WIDTH), lambda i: (0, 4)), pl.BlockSpec((bq, LANES), lambda i: (i, 0)),
                  pl.BlockSpec((8, t), lambda i: (0, 0)), pl.BlockSpec((bq, LANES), lambda i: (i, 0)),
                  pl.BlockSpec((bq, GROUP_WIDTH), lambda i: (i, 1))],
        out_specs=[pl.BlockSpec((bq, GROUP_WIDTH), lambda i: (i, 0)), whole, whole,
                   pl.BlockSpec((LANES, t), lambda i: (0, 0))],
        out_shape=[jax.ShapeDtypeStruct((t, GROUP_WIDTH), ACT_DTYPE), jax.ShapeDtypeStruct((t, GROUP_WIDTH), F32),
                   jax.ShapeDtypeStruct((t, GROUP_WIDTH), F32), jax.ShapeDtypeStruct((LANES, t), F32)],
        compiler_params=_params("arbitrary"), name="fox_bwd")(pmm, pmm, pmm, cum, cum_t, lse, dmix)


CA_Q_BLOCK = 4 * CHUNK
CA_WINDOW = CA_Q_BLOCK + CA_LEFT
CA_BASE = 1024


def _ca_bias_base(rel_bias):
    n = rel_bias.shape[0]
    flat = CA_Q_BLOCK + CA_LEFT - REL_CLIP
    tail = CA_BASE - flat - (2 * REL_CLIP + 1)
    return jnp.concatenate([jnp.broadcast_to(rel_bias[:, 2 * REL_CLIP:], (n, flat)), rel_bias[:, ::-1],
                            jnp.broadcast_to(rel_bias[:, :1], (n, tail))], axis=1)


def _ca_bias_base_grad(dbase):
    flat = CA_Q_BLOCK + CA_LEFT - REL_CLIP
    mid = dbase[:, flat:flat + 2 * REL_CLIP + 1][:, ::-1]
    lo = jnp.sum(dbase[:, flat + 2 * REL_CLIP + 1:], axis=1, keepdims=True)
    hi = jnp.sum(dbase[:, :flat], axis=1, keepdims=True)
    pad = jnp.zeros((dbase.shape[0], 2 * REL_CLIP - 1), F32)
    return mid + jnp.concatenate([lo, pad, hi], axis=1)


def _ca_mask(i):
    r, j = _iota((CA_Q_BLOCK, CA_WINDOW), 0), _iota((CA_Q_BLOCK, CA_WINDOW), 1)
    rc, jc = r // CHUNK, j // CHUNK
    return (jc >= rc) & (jc <= rc + CA_LEFT // CHUNK) & (i * CA_Q_BLOCK + j >= CA_LEFT)


def _ca_scores(q_ref, kp_ref, base_ref, win, h, mask):
    hs = slice(h * HEAD_DIM, (h + 1) * HEAD_DIM)
    s = _dot(q_ref[:, hs], kp_ref[win, hs], 1, 1) * (HEAD_DIM ** -0.5)
    rows = jnp.broadcast_to(base_ref[h:h + 1, :], (CA_Q_BLOCK, CA_BASE))
    bias = pltpu.roll(rows, CA_BASE - CA_Q_BLOCK, 1, stride=1, stride_axis=0)[:, :CA_WINDOW]
    return jnp.where(mask, s + bias, NEG)


def _ca_fwd(pmm, kp, vp, base):
    t = pmm.shape[0]

    def body(q_ref, kp_ref, vp_ref, base_ref, o_ref, lse_ref):
        i = pl.program_id(0)
        win = pl.ds(pl.multiple_of(i * CA_Q_BLOCK, CA_Q_BLOCK), CA_WINDOW)
        mask = _ca_mask(i)
        lse_ref[...] = jnp.zeros_like(lse_ref)
        for h in range(ATT_HEADS):
            hs = slice(h * HEAD_DIM, (h + 1) * HEAD_DIM)
            s = _ca_scores(q_ref, kp_ref, base_ref, win, h, mask)
            m = jnp.max(s, axis=-1, keepdims=True)
            p = jnp.exp(s - m)
            l = jnp.sum(p, axis=-1, keepdims=True)
            o_ref[:, hs] = (_dot(p, vp_ref[win, hs]) / l).astype(o_ref.dtype)
            lse_ref[:, h:h + 1] = m + jnp.log(l)

    padded = pl.BlockSpec((t + CA_LEFT, GROUP_WIDTH), lambda i: (0, 0))
    return pl.pallas_call(
        body, grid=(t // CA_Q_BLOCK,),
        in_specs=[pl.BlockSpec((CA_Q_BLOCK, GROUP_WIDTH), lambda i: (i, 0)), padded, padded,
                  pl.BlockSpec((ATT_HEADS, CA_BASE), lambda i: (0, 0))],
        out_specs=[pl.BlockSpec((CA_Q_BLOCK, GROUP_WIDTH), lambda i: (i, 0)),
                   pl.BlockSpec((CA_Q_BLOCK, LANES), lambda i: (i, 0))],
        out_shape=[jax.ShapeDtypeStruct((t, GROUP_WIDTH), ACT_DTYPE), jax.ShapeDtypeStruct((t, LANES), F32)],
        compiler_params=_params("parallel"), name="ca_fwd")(pmm, kp, vp, base)


def _ca_bwd(pmm, kp, vp, base, lse, dmix):
    t = pmm.shape[0]
    scale = HEAD_DIM ** -0.5

    def body(q_ref, kp_ref, vp_ref, base_ref, lse_ref, do_ref, dq_ref, dkp_ref, dvp_ref, dbase_ref):
        i = pl.program_id(0)

        @pl.when(i == 0)
        def _():
            dkp_ref[...] = jnp.zeros_like(dkp_ref)
            dvp_ref[...] = jnp.zeros_like(dvp_ref)
            dbase_ref[...] = jnp.zeros_like(dbase_ref)

        win = pl.ds(pl.multiple_of(i * CA_Q_BLOCK, CA_Q_BLOCK), CA_WINDOW)
        mask = _ca_mask(i)
        flip = (_iota((CA_Q_BLOCK, CA_Q_BLOCK), 0) + _iota((CA_Q_BLOCK, CA_Q_BLOCK), 1) == CA_Q_BLOCK - 1).astype(F32)
        for h in range(ATT_HEADS):
            hs = slice(h * HEAD_DIM, (h + 1) * HEAD_DIM)
            s = _ca_scores(q_ref, kp_ref, base_ref, win, h, mask)
            p = jnp.exp(s - lse_ref[:, h:h + 1])
            do = do_ref[:, hs]
            dp = _dot(do, vp_ref[win, hs], 1, 1)
            ds = p * (dp - jnp.sum(p * dp, axis=-1, keepdims=True))
            dq_ref[:, hs] = (_dot(ds, kp_ref[win, hs]) * scale).astype(dq_ref.dtype)
            dkp_ref[win, hs] += _dot(ds, q_ref[:, hs], 0, 0) * scale
            dvp_ref[win, hs] += _dot(p, do, 0, 0)
            rev = jnp.concatenate([_dot(flip, ds), jnp.zeros((CA_Q_BLOCK, CA_BASE - CA_WINDOW), F32)], axis=1)
            lined = pltpu.roll(rev, 1, 1, stride=1, stride_axis=0)
            dbase_ref[h:h + 1, :] += jnp.sum(lined, axis=0, keepdims=True)

    padded = pl.BlockSpec((t + CA_LEFT, GROUP_WIDTH), lambda i: (0, 0))
    return pl.pallas_call(
        body, grid=(t // CA_Q_BLOCK,),
        in_specs=[pl.BlockSpec((CA_Q_BLOCK, GROUP_WIDTH), lambda i: (i, 0)), padded, padded,
                  pl.BlockSpec((ATT_HEADS, CA_BASE), lambda i: (0, 0)),
                  pl.BlockSpec((CA_Q_BLOCK, LANES), lambda i: (i, 0)),
                  pl.BlockSpec((CA_Q_BLOCK, GROUP_WIDTH), lambda i: (i, 0))],
        out_specs=[pl.BlockSpec((CA_Q_BLOCK, GROUP_WIDTH), lambda i: (i, 0)), padded, padded,
                   pl.BlockSpec((ATT_HEADS, CA_BASE), lambda i: (0, 0))],
        out_shape=[jax.ShapeDtypeStruct((t, GROUP_WIDTH), ACT_DTYPE),
                   jax.ShapeDtypeStruct((t + CA_LEFT, GROUP_WIDTH), F32),
                   jax.ShapeDtypeStruct((t + CA_LEFT, GROUP_WIDTH), F32),
                   jax.ShapeDtypeStruct((ATT_HEADS, CA_BASE), F32)],
        compiler_params=_params("arbitrary"), name="ca_bwd")(pmm, kp, vp, base, lse, dmix)


GELU_C = 0.7978845608028654
GELU_A = 0.044715


def _shift_down(v, k, fill):
    return jnp.where(_iota(v.shape, 0) >= k, pltpu.roll(v, k, 0), fill)


def _shift_up(v, k, fill):
    t = v.shape[0]
    return jnp.where(_iota(v.shape, 0) < t - k, pltpu.roll(v, t - k, 0), fill)


def _linear_scan(a, b, shift):
    k = 1
    while k < a.shape[0]:
        b = a * shift(b, k, 0.0) + b
        a = a * shift(a, k, 1.0)
        k *= 2
    return b


def _neg_expm1(y):
    series = -y * (1.0 + y * (0.5 + y * (1.0 / 6.0 + y * (1.0 / 24.0 + y * (1.0 / 120.0)))))
    return jnp.where(y > -0.1, series, 1.0 - jnp.exp(y))


def _lru_forward(x, g_in, cw, cb, wa, ba, wx, bx, lam):
    xs = [_shift_down(x, CONV_WIDTH - 1 - j, 0.0) for j in range(CONV_WIDTH - 1)] + [x]
    xc = cb + sum(cw[j:j + 1, :] * xs[j] for j in range(CONV_WIDTH))
    r = jax.nn.sigmoid(_dot(xc, wa) + ba)
    i = jax.nn.sigmoid(_dot(xc, wx) + bx)
    lsl = _log_sigmoid(lam)
    la = LRU_C * r * lsl
    a = jnp.exp(la)
    s = jnp.sqrt(_neg_expm1(2.0 * la))
    h = _linear_scan(a, s * (i * xc), _shift_down)
    u = GELU_C * (g_in + GELU_A * g_in * g_in * g_in)
    th = jnp.tanh(u)
    gelu = 0.5 * g_in * (1.0 + th)
    return xs, xc, r, i, lsl, a, s, h, th, gelu


def _lru_specs(t):
    col = lambda off: pl.BlockSpec((t, LANES), lambda j: (0, j + off))
    vec = pl.BlockSpec((1, LANES), lambda j: (0, j))
    mat = pl.BlockSpec((None, LANES, LANES), lambda j: (j, 0, 0))
    return [col(0), col(GROUP_WIDTH // LANES), pl.BlockSpec((CONV_WIDTH, LANES), lambda j: (0, j)),
            vec, mat, vec, mat, vec, vec]


def _lru_fwd(pel, conv_w, conv_b, wa, ba, wx, bx, lam):
    t = pel.shape[0]

    def body(g_ref, x_ref, cw_ref, cb_ref, wa_ref, ba_ref, wx_ref, bx_ref, lam_ref, o_ref):
        res = _lru_forward(x_ref[...], g_ref[...], cw_ref[...], cb_ref[...], wa_ref[...], ba_ref[...],
                           wx_ref[...], bx_ref[...], lam_ref[...])
        o_ref[...] = (res[7] * res[9]).astype(o_ref.dtype)

    return pl.pallas_call(
        body, grid=(GROUP_WIDTH // LANES,), in_specs=_lru_specs(t),
        out_specs=pl.BlockSpec((t, LANES), lambda j: (0, j)),
        out_shape=jax.ShapeDtypeStruct((t, GROUP_WIDTH), ACT_DTYPE),
        compiler_params=_params("parallel"), name="lru_fwd")(pel, pel, conv_w, conv_b, wa, ba, wx, bx, lam)


def _lru_bwd(pel, conv_w, conv_b, wa, ba, wx, bx, lam, dmix):
    t = pel.shape[0]

    def body(g_ref, x_ref, cw_ref, cb_ref, wa_ref, ba_ref, wx_ref, bx_ref, lam_ref, do_ref,
             dg_ref, dx_ref, dcw_ref, dcb_ref, dwa_ref, dba_ref, dwx_ref, dbx_ref, dlam_ref):
        g_in, cw, lam = g_ref[...], cw_ref[...], lam_ref[...]
        xs, xc, r, i, lsl, a, s, h, th, gelu = _lru_forward(
            x_ref[...], g_in, cw, cb_ref[...], wa_ref[...], ba_ref[...], wx_ref[...], bx_ref[...], lam)
        dout = do_ref[...]
        dgelu = 0.5 * (1.0 + th) + 0.5 * g_in * (1.0 - th * th) * GELU_C * (1.0 + 3.0 * GELU_A * g_in * g_in)
        dg_ref[...] = (dout * h * dgelu).astype(dg_ref.dtype)
        gsum = _linear_scan(_shift_up(a, 1, 0.0), dout * gelu, _shift_up)
        da = gsum * _shift_down(h, 1, 0.0)
        di = gsum * s * xc
        dla = da * a - gsum * (i * xc) * (a * a / s)
        dlam_ref[...] = jnp.sum(dla * (LRU_C * r), axis=0, keepdims=True) * jax.nn.sigmoid(-lam)
        dpr = dla * (LRU_C * lsl) * r * (1.0 - r)
        dpi = di * i * (1.0 - i)
        dxc = gsum * s * i + _dot(dpr, wa_ref[...], 1, 1) + _dot(dpi, wx_ref[...], 1, 1)
        xct = xc.T
        dwa_ref[...] = _dot(xct, dpr)
        dwx_ref[...] = _dot(xct, dpi)
        dba_ref[...] = jnp.sum(dpr, axis=0, keepdims=True)
        dbx_ref[...] = jnp.sum(dpi, axis=0, keepdims=True)
        dcb_ref[...] = jnp.sum(dxc, axis=0, keepdims=True)
        for j in range(CONV_WIDTH):
            dcw_ref[j:j + 1, :] = jnp.sum(dxc * xs[j], axis=0, keepdims=True)
        dx = cw[CONV_WIDTH - 1:CONV_WIDTH, :] * dxc
        for j in range(CONV_WIDTH - 1):
            dx = dx + cw[j:j + 1, :] * _shift_up(dxc, CONV_WIDTH - 1 - j, 0.0)
        dx_ref[...] = dx.astype(dx_ref.dtype)

    col = pl.BlockSpec((t, LANES), lambda j: (0, j))
    vec = pl.BlockSpec((1, LANES), lambda j: (0, j))
    mat = pl.BlockSpec((None, LANES, LANES), lambda j: (j, 0, 0))
    nb = GROUP_WIDTH // LANES
    vshape = jax.ShapeDtypeStruct((1, GROUP_WIDTH), F32)
    mshape = jax.ShapeDtypeStruct((nb, LANES, LANES), F32)
    return pl.pallas_call(
        body, grid=(nb,),
        in_specs=_lru_specs(t) + [pl.BlockSpec((t, LANES), lambda j: (0, j + nb))],
        out_specs=[col, col, pl.BlockSpec((CONV_WIDTH, LANES), lambda j: (0, j)), vec, mat, vec, mat, vec, vec],
        out_shape=[jax.ShapeDtypeStruct((t, GROUP_WIDTH), ACT_DTYPE), jax.ShapeDtypeStruct((t, GROUP_WIDTH), ACT_DTYPE),
                   jax.ShapeDtypeStruct((CONV_WIDTH, GROUP_WIDTH), F32), vshape, mshape, vshape, mshape, vshape, vshape],
        compiler_params=_params("parallel"), name="lru_bwd")(
            pel, pel, conv_w, conv_b, wa, ba, wx, bx, lam, dmix)


def _block_diag_pairs(w):
    z = jnp.zeros((LRU_BLOCK_DIM, LRU_BLOCK_DIM), w.dtype)
    return jnp.stack([jnp.block([[w[2 * j], z], [z, w[2 * j + 1]]]) for j in range(w.shape[0] // 2)])


def _block_diag_pairs_grad(dw):
    b = LRU_BLOCK_DIM
    return jnp.stack([dw[n // 2, (n % 2) * b:(n % 2 + 1) * b, (n % 2) * b:(n % 2 + 1) * b] for n in range(2 * dw.shape[0])])


ANY = pl.BlockSpec(memory_space=pl.ANY)


def _all_gather(xs, name):
    n = len(xs)

    def body(*refs):
        x_refs, out_refs = refs[:n], refs[n:2 * n]
        send_sems, recv_sems, local_sems = refs[2 * n:]
        x, y, c = lax.axis_index("x"), lax.axis_index("y"), lax.axis_index("c")
        me, sibling = (x, y, c), (x, y, 1 - c)
        chips = [(1 - x, y), (x, 1 - y), (1 - x, 1 - y)]

        def rows(a, px, py, pc):
            return out_refs[a].at[4 * px + 2 * py + pc]

        def copy(a, k, block, to, src=None):
            return pltpu.make_async_remote_copy(
                src_ref=rows(a, *block) if src is None else src, dst_ref=rows(a, *block),
                send_sem=send_sems.at[7 * a + k], recv_sem=recv_sems.at[7 * a + k], device_id=to, device_id_type=MESH)

        mine = [pltpu.make_async_copy(x_refs[a], rows(a, *me), local_sems.at[a]) for a in range(n)]
        first = []
        for a in range(n):
            first.append(copy(a, 0, me, sibling, src=x_refs[a]))
            first += [copy(a, 1 + j, me, (*chip, c), src=x_refs[a]) for j, chip in enumerate(chips)]
        for cp in first + mine:
            cp.start()
        passed = []
        for j, chip in enumerate(chips):
            for a in range(n):
                copy(a, 1 + j, (*chip, c), me).wait_recv()
                passed.append(copy(a, 4 + j, (*chip, c), sibling))
                passed[-1].start()
        for a in range(n):
            copy(a, 0, sibling, me).wait_recv()
            for j, chip in enumerate(chips):
                copy(a, 4 + j, (*chip, 1 - c), me).wait_recv()
        for cp in first + passed:
            cp.wait_send()
        for cp in mine:
            cp.wait()

    return pl.pallas_call(
        body, out_shape=[jax.ShapeDtypeStruct((N_DEV,) + x.shape, x.dtype) for x in xs],
        in_specs=[ANY] * n, out_specs=[ANY] * n,
        scratch_shapes=[pltpu.SemaphoreType.DMA((7 * n,)), pltpu.SemaphoreType.DMA((7 * n,)),
                        pltpu.SemaphoreType.DMA((n,))],
        name=name)(*xs)


N_CHIPS = 4


def _sibling_exchange(gs):
    n = len(gs)

    def body(*refs):
        g_refs, got_refs = refs[:n], refs[n:2 * n]
        send_sems, recv_sems = refs[2 * n:]
        x, y, c = lax.axis_index("x"), lax.axis_index("y"), lax.axis_index("c")
        copies = [pltpu.make_async_remote_copy(
            src_ref=g_refs[a].at[2 * k + (1 - c)], dst_ref=got_refs[a].at[k], send_sem=send_sems.at[N_CHIPS * a + k],
            recv_sem=recv_sems.at[N_CHIPS * a + k], device_id=(x, y, 1 - c), device_id_type=MESH)
            for a in range(n) for k in range(N_CHIPS)]
        for cp in copies:
            cp.start()
        for cp in copies:
            cp.wait_recv()
        for cp in copies:
            cp.wait_send()

    return pl.pallas_call(
        body, out_shape=[jax.ShapeDtypeStruct((N_CHIPS,) + g.shape[1:], g.dtype) for g in gs],
        in_specs=[ANY] * n, out_specs=[ANY] * n,
        scratch_shapes=[pltpu.SemaphoreType.DMA((N_CHIPS * n,)), pltpu.SemaphoreType.DMA((N_CHIPS * n,))],
        name="rs_sibling_exchange")(*gs)


def _chip_exchange(ss):
    n = len(ss)

    def body(*refs):
        s_refs, out_refs = refs[:n], refs[n:2 * n]
        send_sems, recv_sems = refs[2 * n:]
        x, y, c = lax.axis_index("x"), lax.axis_index("y"), lax.axis_index("c")
        chips = [(1 - x, y), (x, 1 - y), (1 - x, 1 - y)]
        copies = [pltpu.make_async_remote_copy(
            src_ref=s_refs[a].at[2 * px + py], dst_ref=out_refs[a].at[j], send_sem=send_sems.at[3 * a + j],
            recv_sem=recv_sems.at[3 * a + j], device_id=(px, py, c), device_id_type=MESH)
            for a in range(n) for j, (px, py) in enumerate(chips)]
        for cp in copies:
            cp.start()
        for cp in copies:
            cp.wait_recv()
        for cp in copies:
            cp.wait_send()

    return pl.pallas_call(
        body, out_shape=[jax.ShapeDtypeStruct((3,) + s.shape[1:], s.dtype) for s in ss],
        in_specs=[ANY] * n, out_specs=[ANY] * n,
        scratch_shapes=[pltpu.SemaphoreType.DMA((3 * n,)), pltpu.SemaphoreType.DMA((3 * n,))],
        name="rs_chip_exchange")(*ss)


def _row_tile(r):
    return ROW_TILE if r % ROW_TILE == 0 else r


def _pair_sum(g, got, place, name):
    _, r, c = g.shape
    tile = _row_tile(r)

    def body(place_ref, a_ref, b_ref, o_ref):
        o_ref[...] = (a_ref[...].astype(F32) + b_ref[...].astype(F32)).astype(o_ref.dtype)

    blk = pl.BlockSpec((1, tile, c), lambda k, i, place_ref: (k, i, 0))
    return pl.pallas_call(
        body,
        grid_spec=pltpu.PrefetchScalarGridSpec(
            num_scalar_prefetch=1, grid=(N_CHIPS, r // tile),
            in_specs=[pl.BlockSpec((1, tile, c), lambda k, i, place_ref: (2 * k + place_ref[0], i, 0)), blk],
            out_specs=blk),
        out_shape=jax.ShapeDtypeStruct(got.shape, got.dtype),
        compiler_params=_params("parallel", "parallel"), name=name)(place, g, got)


def _adamw_update(g, w_ref, m_ref, v_ref, g_ref, d_ref, nm_ref, nv_ref):
    nm = ADAM_B1 * m_ref[...] + (1.0 - ADAM_B1) * g
    nv = ADAM_B2 * v_ref[...] + (1.0 - ADAM_B2) * jnp.square(g)
    m_hat = nm / (1.0 - ADAM_B1 ** ADAM_STEP)
    v_hat = nv / (1.0 - ADAM_B2 ** ADAM_STEP)
    g_ref[...] = g
    d_ref[...] = -ADAM_LR * (m_hat / (jnp.sqrt(v_hat) + ADAM_EPS) + ADAM_WD * w_ref[...])
    nm_ref[...] = nm
    nv_ref[...] = nv


def _adamw_sharded(s, recv, w, m, v, place, name):
    r, c = w.shape
    tile = _row_tile(r)

    def body(place_ref, s_ref, r_ref, w_ref, m_ref, v_ref, *outs):
        g = s_ref[0].astype(F32) + r_ref[0].astype(F32) + r_ref[1].astype(F32) + r_ref[2].astype(F32)
        _adamw_update(g, w_ref, m_ref, v_ref, *outs)

    blk = pl.BlockSpec((tile, c), lambda i, place_ref: (i, 0))
    out = jax.ShapeDtypeStruct((r, c), F32)
    return pl.pallas_call(
        body,
        grid_spec=pltpu.PrefetchScalarGridSpec(
            num_scalar_prefetch=1, grid=(r // tile,),
            in_specs=[pl.BlockSpec((1, tile, c), lambda i, place_ref: (place_ref[1], i, 0)),
                      pl.BlockSpec((3, tile, c), lambda i, place_ref: (0, i, 0)), blk, blk, blk],
            out_specs=[blk, blk, blk, blk]),
        out_shape=[out, out, out, out], compiler_params=_params("parallel"), name=name)(place, s, recv, w, m, v)


def _adamw_replicated(parts, w, m, v, name):
    p, r, c = parts.shape
    tile = _row_tile(r)

    def body(p_ref, w_ref, m_ref, v_ref, *outs):
        g = p_ref[0].astype(F32)
        for k in range(1, p):
            g = g + p_ref[k].astype(F32)
        _adamw_update(g, w_ref, m_ref, v_ref, *outs)

    blk = pl.BlockSpec((tile, c), lambda i: (i, 0))
    out = jax.ShapeDtypeStruct((r, c), F32)
    return pl.pallas_call(body, grid=(r // tile,),
                          in_specs=[pl.BlockSpec((p, tile, c), lambda i: (0, i, 0)), blk, blk, blk],
                          out_specs=[blk, blk, blk, blk], out_shape=[out, out, out, out],
                          compiler_params=_params("parallel"), name=name)(parts, w, m, v)


SLAB_COLS = 1024
SHARDED = {"norm_w": 2, "w_in_even": 2, "gla_w_a_up": 2, "w_out_even": 1, "w_in_odd": 2, "conv_w": 2, "conv_b": 1,
           "lru_b_a": 1, "lru_b_x": 1, "lru_lambda": 1, "w_out_odd": 1, "w_mlp_up": 2, "w_mlp_down": 1}
REPLICATED = ["gla_b_a", "gla_norm_w", "fox_b_f", "rel_bias", "lru_w_a", "lru_w_x"]
WEIGHTS = ["norm_w", "w_in_even", "gla_w_a_up", "gla_b_a", "gla_norm_w", "fox_b_f", "w_out_even", "w_in_odd",
           "rel_bias", "conv_w", "conv_b", "lru_w_a", "lru_b_a", "lru_w_x", "lru_b_x", "lru_lambda", "w_out_odd",
           "w_mlp_up", "w_mlp_down"]
MATRICES = ("w_in_even", "w_out_even", "w_in_odd", "w_out_odd", "w_mlp_up", "w_mlp_down")
VECTORS = tuple(n for n in SHARDED if n not in MATRICES)
VEC_SLAB_ROWS = 16
REPL_SLAB_ROWS = 72
MATRIX_BLOCKS = (("w_in_even", 0), ("w_out_even", 0), ("w_in_odd", 0), ("w_out_odd", 0),
                 ("w_mlp_up", 0), ("w_mlp_up", 1), ("w_mlp_down", 0), ("w_mlp_down", 1))


def _rows_of(shape):
    n = 1
    for s in shape:
        n *= s
    return -(-n // SLAB_COLS), n


def _pack(arrays, total_rows, lead=()):
    parts, used = [], 0
    for a in arrays:
        rows, n = _rows_of(a.shape[len(lead):])
        flat = a.reshape(lead + (n,))
        flat = jnp.pad(flat, [(0, 0)] * len(lead) + [(0, rows * SLAB_COLS - n)])
        parts.append(flat.reshape(lead + (rows, SLAB_COLS)))
        used += rows
    parts.append(jnp.zeros(lead + (total_rows - used, SLAB_COLS), parts[0].dtype))
    return jnp.concatenate(parts, axis=len(lead))


def _unpack(slab, shapes, lead=()):
    out, row = [], 0
    for shape in shapes:
        rows, n = _rows_of(shape)
        seg = lax.slice_in_dim(slab, row, row + rows, axis=len(lead))
        out.append(seg.reshape(lead + (rows * SLAB_COLS,))[..., :n].reshape(lead + tuple(shape)))
        row += rows
    return out


def _join_shards(blocks, axis):
    moved = jnp.moveaxis(blocks, 0, axis)
    shape = moved.shape
    return moved.reshape(shape[:axis] + (shape[axis] * shape[axis + 1],) + shape[axis + 2:])


def _split_shards(full, axis):
    shape = full.shape
    cut = full.reshape(shape[:axis] + (N_DEV, shape[axis] // N_DEV) + shape[axis + 1:])
    return jnp.moveaxis(cut, axis, 0)


EVEN_SPLITS = (0, 256, 512, 1024, 1536, 1552, 2064, 2576, 3088, 3096)


def _even_in_split(w):
    c = [w[:, EVEN_SPLITS[k]:EVEN_SPLITS[k + 1]] for k in range(9)]
    gq, gk, gv, gr, ga, fq, fk, fv, ff = c
    padcols = lambda a: jnp.pad(a, ((0, 0), (0, LANES - a.shape[1])))
    return jnp.concatenate([gq, gk, gv, fq, fk, fv], axis=1), jnp.concatenate([gr, padcols(ga), padcols(ff)], axis=1)


def _even_in_merge(dmm, dele):
    return jnp.concatenate([dmm[:, :1024], dele[:, :512], dele[:, 512:512 + GLA_RANK], dmm[:, 1024:2560],
                            dele[:, 640:640 + ATT_HEADS]], axis=1)


def _column_shards(full):
    r, c = full.shape
    return jnp.moveaxis(full.reshape(r, N_DEV, c // N_DEV), 1, 0)


def _local_step(x, target, w):
    t = x.shape[0]
    nw = w["norm_w"]
    nrm = lambda l, k: nw[l, k][None, :]
    g = {}
    dnorm = {}

    def mlp_fwd(xin, layer):
        h = _norm_fwd(xin, nrm(layer, 2), out_dtype=ACT_DTYPE, name=f"norm_mlp_{layer}")
        u = _mm(h, w["w_mlp_up"][layer], out_dtype=ACT_DTYPE, tm=1024, tn=D_FF // N_DEV, b_blocked=True,
                name=f"mlp_up_{layer}")
        yv = _mm(u, w["w_mlp_down"][layer], out_dtype=F32, tm=512, tn=512, a_sqrelu=True, name=f"mlp_down_{layer}")
        xout = _norm_fwd(yv, nrm(layer, 3), out_dtype=F32, res=xin, name=f"norm_mlp_out_{layer}")
        return xout, (xin, h, u, yv)

    def mlp_bwd(dxout, saved, layer):
        xin, h, u, yv = saved
        dy, dnorm[(layer, 3)] = _norm_bwd(dxout, yv, nrm(layer, 3), out_dtype=ACT_DTYPE, name=f"norm_mlp_out_bwd_{layer}")
        du = _mm(dy, w["w_mlp_down"][layer], nt=True, out_dtype=ACT_DTYPE, tm=512, tn=512, drelu_of=u,
                 name=f"mlp_down_dx_{layer}")
        dw_down = _mm(u.T, dy, out_dtype=WIRE_DTYPE, tm=512, tn=512, a_sqrelu=True, name=f"mlp_down_dw_{layer}")
        g[("w_mlp_down", layer)] = dw_down.reshape(N_DEV, D_FF // N_DEV, D_MODEL)
        g[("w_mlp_up", layer)] = _mm(h.T, du, out_dtype=WIRE_DTYPE, tm=512, tn=D_FF // N_DEV, out_blocked=True,
                                     name=f"mlp_up_dw_{layer}")
        dh = _mm_nt_blocked(du, w["w_mlp_up"][layer], out_dtype=F32, tm=512, tn=512, name=f"mlp_up_dx_{layer}")
        dxin, dnorm[(layer, 2)] = _norm_bwd(dh, xin, nrm(layer, 2), out_dtype=F32, add=dxout, name=f"norm_mlp_bwd_{layer}")
        return dxin

    wmm_e, wel_e = _even_in_split(w["w_in_even"])
    w_up_pad = jnp.pad(w["gla_w_a_up"][0], ((0, LANES - GLA_RANK), (0, 0)))
    b_f_pad = jnp.pad(w["fox_b_f"], ((0, 0), (0, LANES - ATT_HEADS)))
    h0 = _norm_fwd(x, nrm(0, 0), out_dtype=ACT_DTYPE, name="norm_in_0")
    pmm0 = _mm(h0, wmm_e, out_dtype=ACT_DTYPE, tm=1024, tn=512, name="in_even_mm")
    pel0 = _mm(h0, wel_e, out_dtype=F32, tm=1024, tn=768, name="in_even_el")
    out_a, states = _gla_fwd(pmm0, pel0, w_up_pad, w["gla_b_a"], w["gla_norm_w"])
    cum, cum_t = _fox_gate_fwd(pel0, b_f_pad)
    out_b, lse_b = _fox_fwd(pmm0, cum, cum_t)
    mix_in0 = jnp.concatenate([out_a, out_b], axis=1)
    mix0 = _mm(mix_in0, w["w_out_even"], out_dtype=F32, tm=1024, tn=512, name="out_even")
    x1 = _norm_fwd(mix0, nrm(0, 1), out_dtype=F32, res=x, name="norm_mix_0")
    x2, mlp0 = mlp_fwd(x1, 0)

    w_in_o = w["w_in_odd"]
    n_mm_o = 3 * GROUP_WIDTH
    wa_bd, wx_bd = _block_diag_pairs(w["lru_w_a"][0]), _block_diag_pairs(w["lru_w_x"][0])
    base = _ca_bias_base(w["rel_bias"][0])
    h1 = _norm_fwd(x2, nrm(1, 0), out_dtype=ACT_DTYPE, name="norm_in_1")
    pmm1 = _mm(h1, w_in_o[:, :n_mm_o], out_dtype=ACT_DTYPE, tm=1024, tn=512, name="in_odd_mm")
    pel1 = _mm(h1, w_in_o[:, n_mm_o:], out_dtype=F32, tm=1024, tn=512, name="in_odd_el")
    kp = jnp.pad(pmm1[:, GROUP_WIDTH:2 * GROUP_WIDTH], ((CA_LEFT, 0), (0, 0)))
    vp = jnp.pad(pmm1[:, 2 * GROUP_WIDTH:], ((CA_LEFT, 0), (0, 0)))
    out_c, lse_c = _ca_fwd(pmm1, kp, vp, base)
    lru_args = (pel1, w["conv_w"][0], w["conv_b"], wa_bd, w["lru_b_a"], wx_bd, w["lru_b_x"], w["lru_lambda"])
    out_d = _lru_fwd(*lru_args)
    mix_in1 = jnp.concatenate([out_c, out_d], axis=1)
    mix1 = _mm(mix_in1, w["w_out_odd"], out_dtype=F32, tm=1024, tn=512, name="out_odd")
    x3 = _norm_fwd(mix1, nrm(1, 1), out_dtype=F32, res=x2, name="norm_mix_1")
    x4, mlp1 = mlp_fwd(x3, 1)

    loss, dx4 = _loss_fwd_bwd(x4, target)

    dx3 = mlp_bwd(dx4, mlp1, 1)
    dmix1, dnorm[(1, 1)] = _norm_bwd(dx3, mix1, nrm(1, 1), out_dtype=ACT_DTYPE, name="norm_mix_bwd_1")
    g[("w_out_odd", 0)] = _mm(mix_in1.T, dmix1, out_dtype=WIRE_DTYPE, tm=512, tn=512, name="out_odd_dw").reshape(
        N_DEV, D_MODEL // N_DEV, D_MODEL)
    dmix_in1 = _mm(dmix1, w["w_out_odd"], nt=True, out_dtype=F32, tm=512, tn=512, name="out_odd_dx")
    dq_c, dkp, dvp, dbase = _ca_bwd(pmm1, kp, vp, base, lse_c, dmix_in1)
    (dgate, dxin, g_conv_w, g_conv_b, dwa_bd, g_lru_b_a, dwx_bd, g_lru_b_x, g_lru_lambda) = _lru_bwd(*lru_args, dmix_in1)
    dp1 = jnp.concatenate([dq_c, dkp[CA_LEFT:].astype(ACT_DTYPE), dvp[CA_LEFT:].astype(ACT_DTYPE), dgate, dxin], axis=1)
    g[("w_in_odd", 0)] = _column_shards(_mm(h1.T, dp1, out_dtype=WIRE_DTYPE, tm=512, tn=512, name="in_odd_dw"))
    dh1 = _mm(dp1, w_in_o, nt=True, out_dtype=F32, tm=512, tn=512, name="in_odd_dx")
    dx2, dnorm[(1, 0)] = _norm_bwd(dh1, x2, nrm(1, 0), out_dtype=F32, add=dx3, name="norm_in_bwd_1")
    g["rel_bias"] = _ca_bias_base_grad(dbase)[None]
    g["conv_w"], g["conv_b"] = g_conv_w[None], g_conv_b
    g["lru_w_a"], g["lru_w_x"] = _block_diag_pairs_grad(dwa_bd)[None], _block_diag_pairs_grad(dwx_bd)[None]
    g["lru_b_a"], g["lru_b_x"], g["lru_lambda"] = g_lru_b_a, g_lru_b_x, g_lru_lambda

    dx1 = mlp_bwd(dx2, mlp0, 0)
    dmix0, dnorm[(0, 1)] = _norm_bwd(dx1, mix0, nrm(0, 1), out_dtype=ACT_DTYPE, name="norm_mix_bwd_0")
    g[("w_out_even", 0)] = _mm(mix_in0.T, dmix0, out_dtype=WIRE_DTYPE, tm=512, tn=512, name="out_even_dw").reshape(
        N_DEV, D_MODEL // N_DEV, D_MODEL)
    dmix_in0 = _mm(dmix0, w["w_out_even"], nt=True, out_dtype=F32, tm=512, tn=512, name="out_even_dx")
    dq_a, dk_a, dv_a, dr_a, da_a, dw_up_pad, g_gla_b_a, g_gla_norm_w = _gla_bwd(
        pmm0, pel0, w_up_pad, w["gla_b_a"], w["gla_norm_w"], states, dmix_in0)
    dq_b, dk_b, dv_b, dcum_t = _fox_bwd(pmm0, cum, cum_t, lse_b, dmix_in0)
    df_b, db_f = _fox_gate_bwd(pel0, b_f_pad, dcum_t)
    dp0 = jnp.concatenate([dq_a, dk_a, dv_a, dq_b, dk_b.astype(ACT_DTYPE), dv_b.astype(ACT_DTYPE), dr_a, da_a, df_b],
                          axis=1)
    w_perm = jnp.concatenate([wmm_e, wel_e], axis=1)
    n_mm_e = wmm_e.shape[1]
    dw_perm = _mm(h0.T, dp0, out_dtype=WIRE_DTYPE, tm=512, tn=dp0.shape[1] // 2, name="in_even_dw")
    dh0 = _mm(dp0, w_perm, nt=True, out_dtype=F32, tm=512, tn=512, name="in_even_dx")
    dx0, dnorm[(0, 0)] = _norm_bwd(dh0, x, nrm(0, 0), out_dtype=F32, add=dx1, name="norm_in_bwd_0")
    g[("w_in_even", 0)] = _column_shards(_even_in_merge(dw_perm[:, :n_mm_e], dw_perm[:, n_mm_e:]))
    g["gla_w_a_up"] = dw_up_pad[:GLA_RANK][None]
    g["gla_b_a"], g["gla_norm_w"], g["fox_b_f"] = g_gla_b_a, g_gla_norm_w, db_f[:, :ATT_HEADS]
    g["norm_w"] = jnp.stack([jnp.concatenate([dnorm[(l, k)] for k in range(4)], axis=0) for l in range(DEPTH)])
    return loss, dx0, g


def kernel(x, norm_w, w_in_even, gla_w_a_up, gla_b_a, gla_norm_w, fox_b_f, w_out_even, w_in_odd, rel_bias, conv_w, conv_b, lru_w_a, lru_b_a, lru_w_x, lru_b_x, lru_lambda, w_out_odd, w_mlp_up, w_mlp_down, loss_target, m_norm_w, m_w_in_even, m_gla_w_a_up, m_gla_b_a, m_gla_norm_w, m_fox_b_f, m_w_out_even, m_w_in_odd, m_rel_bias, m_conv_w, m_conv_b, m_lru_w_a, m_lru_b_a, m_lru_w_x, m_lru_b_x, m_lru_lambda, m_w_out_odd, m_w_mlp_up, m_w_mlp_down, v_norm_w, v_w_in_even, v_gla_w_a_up, v_gla_b_a, v_gla_norm_w, v_fox_b_f, v_w_out_even, v_w_in_odd, v_rel_bias, v_conv_w, v_conv_b, v_lru_w_a, v_lru_b_a, v_lru_w_x, v_lru_b_x, v_lru_lambda, v_w_out_odd, v_w_mlp_up, v_w_mlp_down):
    wts = dict(zip(WEIGHTS, (norm_w, w_in_even, gla_w_a_up, gla_b_a, gla_norm_w, fox_b_f, w_out_even, w_in_odd, rel_bias,
                             conv_w, conv_b, lru_w_a, lru_b_a, lru_w_x, lru_b_x, lru_lambda, w_out_odd, w_mlp_up,
                             w_mlp_down)))
    mom = dict(zip(WEIGHTS, (m_norm_w, m_w_in_even, m_gla_w_a_up, m_gla_b_a, m_gla_norm_w, m_fox_b_f, m_w_out_even,
                             m_w_in_odd, m_rel_bias, m_conv_w, m_conv_b, m_lru_w_a, m_lru_b_a, m_lru_w_x, m_lru_b_x,
                             m_lru_lambda, m_w_out_odd, m_w_mlp_up, m_w_mlp_down)))
    var = dict(zip(WEIGHTS, (v_norm_w, v_w_in_even, v_gla_w_a_up, v_gla_b_a, v_gla_norm_w, v_fox_b_f, v_w_out_even,
                             v_w_in_odd, v_rel_bias, v_conv_w, v_conv_b, v_lru_w_a, v_lru_b_a, v_lru_w_x, v_lru_b_x,
                             v_lru_lambda, v_w_out_odd, v_w_mlp_up, v_w_mlp_down)))
    vec_shapes = [wts[n].shape for n in VECTORS]
    repl_shapes = [wts[n].shape for n in REPLICATED]
    place = jnp.stack([lax.axis_index("c"), 2 * lax.axis_index("x") + lax.axis_index("y")]).astype(jnp.int32)

    vec_slab = _pack([wts[n] for n in VECTORS], VEC_SLAB_ROWS)
    gathered = _all_gather([wts[n][l].astype(WIRE_DTYPE) for n, l in MATRIX_BLOCKS] + [vec_slab], "weights_all_gather")
    mats = dict(zip(MATRIX_BLOCKS, gathered))
    full = {n: _join_shards(b, SHARDED[n]) for n, b in zip(VECTORS, _unpack(gathered[-1], vec_shapes, lead=(N_DEV,)))}
    for n in ("w_in_even", "w_in_odd"):
        full[n] = _join_shards(mats[(n, 0)], 1)
    for n in ("w_out_even", "w_out_odd"):
        full[n] = mats[(n, 0)].reshape(D_MODEL, D_MODEL)
    full["w_mlp_up"] = [mats[("w_mlp_up", l)] for l in range(DEPTH)]
    full["w_mlp_down"] = [mats[("w_mlp_down", l)].reshape(D_FF, D_MODEL) for l in range(DEPTH)]
    for n in REPLICATED:
        full[n] = wts[n]

    loss_blk, dx, grads = _local_step(x[0], loss_target[0], full)
    loss = lax.psum(loss_blk[0, 0], ("x", "y", "c"))

    vec_grads = _pack([_split_shards(grads[n], SHARDED[n]) for n in VECTORS], VEC_SLAB_ROWS, lead=(N_DEV,))
    keys = list(MATRIX_BLOCKS) + ["vectors"]
    g_list = [grads[k] for k in MATRIX_BLOCKS] + [vec_grads.astype(WIRE_DTYPE)]
    got = _sibling_exchange(g_list)
    sums = [_pair_sum(g, gt, place, f"rs_pair_sum_{i}") for i, (g, gt) in enumerate(zip(g_list, got))]
    recv = _chip_exchange(sums)
    repl_parts = _all_gather([_pack([grads[n] for n in REPLICATED], REPL_SLAB_ROWS)], "replicated_grads_all_gather")[0]

    def local2d(d, k):
        return _pack([d[n] for n in VECTORS], VEC_SLAB_ROWS) if k == "vectors" else d[k[0]][k[1]]

    upd = {k: _adamw_sharded(s, r, local2d(wts, k), local2d(mom, k), local2d(var, k), place, f"adamw_{i}")
           for i, (k, s, r) in enumerate(zip(keys, sums, recv))}
    rp = _adamw_replicated(repl_parts, _pack([wts[n] for n in REPLICATED], REPL_SLAB_ROWS),
                           _pack([mom[n] for n in REPLICATED], REPL_SLAB_ROWS),
                           _pack([var[n] for n in REPLICATED], REPL_SLAB_ROWS), "adamw_replicated")
    outs = []
    for kind in range(4):
        vals = dict(zip(VECTORS, _unpack(upd["vectors"][kind], vec_shapes)))
        vals.update(zip(REPLICATED, _unpack(rp[kind], repl_shapes)))
        for n in MATRICES:
            vals[n] = jnp.stack([upd[(n, l)][kind] for l in range(wts[n].shape[0])])
        outs += [vals[n] for n in WEIGHTS]
    return (loss, dx[None], *outs)
```

```python
import functools
from typing import Callable, NamedTuple

import jax
import jax.numpy as jnp
from jax import lax
from jax.experimental import pallas as pl
from jax.experimental.pallas import tpu as pltpu

F32 = jnp.float32
MXU_DTYPE = jnp.bfloat16
ACT_DTYPE = jnp.bfloat16
WIRE_DTYPE = jnp.bfloat16

V7X_VMEM_BYTES = 64 * 1024 * 1024
VMEM_LIMIT = (V7X_VMEM_BYTES * 7) // 8
LANES = 128

D_MODEL = 1024
SEQ = 2048
DEPTH = 2
CHUNK = 64
GROUP_WIDTH = D_MODEL // 2
D_FF = 4 * D_MODEL
NORM_EPS = 1e-6
GLA_HEADS = 4
GLA_DV = GROUP_WIDTH // GLA_HEADS
GLA_DK = GLA_DV // 2
GLA_KW = GLA_HEADS * GLA_DK
GLA_RANK = 16
GLA_GATE_TAU = 16.0
HEAD_DIM = 64
ATT_HEADS = GROUP_WIDTH // HEAD_DIM
CA_LEFT = 8 * CHUNK
REL_CLIP = 128
LRU_BLOCK_DIM = 64
CONV_WIDTH = 4
LRU_C = 8.0
N_DEV = 8

ADAM_LR = 0.001
ADAM_B1 = 0.9
ADAM_B2 = 0.999
ADAM_EPS = 1e-08
ADAM_WD = 0.01
ADAM_STEP = 10

NEG = float(jnp.finfo(jnp.float32).min)
MESH = pl.DeviceIdType.MESH


def _params(*sem):
    return pltpu.CompilerParams(dimension_semantics=sem, vmem_limit_bytes=VMEM_LIMIT)


def _dot(a, b, ca=1, cb=0):
    return lax.dot_general(a.astype(MXU_DTYPE), b.astype(MXU_DTYPE), (((ca,), (cb,)), ((), ())),
                           preferred_element_type=F32)


def _dot_exact(a, b):
    return lax.dot_general(a, b, (((1,), (0,)), ((), ())), precision=lax.Precision.HIGHEST,
                           preferred_element_type=F32)


def _log_sigmoid(x):
    return jnp.minimum(x, 0.0) - jnp.log1p(jnp.exp(-jnp.abs(x)))


def _iota(shape, axis):
    return lax.broadcasted_iota(jnp.int32, shape, axis)


ANY = pl.BlockSpec(memory_space=pl.ANY)
N_CHIPS = 4


class _Plan(NamedTuple):
    ins: list
    outs: list
    sems: list
    start: Callable
    finish: Callable


def _place():
    x, y, c = lax.axis_index("x"), lax.axis_index("y"), lax.axis_index("c")
    return x, y, c, [(1 - x, y), (x, 1 - y), (1 - x, 1 - y)]


def _gather_plan(xs):
    n = len(xs)

    def parts(x_refs, out_refs, sems):
        send_sems, recv_sems, local_sems = sems
        x, y, c, chips = _place()
        me, sibling = (x, y, c), (x, y, 1 - c)

        def rows(a, px, py, pc):
            return out_refs[a].at[4 * px + 2 * py + pc]

        def copy(a, k, block, to, src=None):
            return pltpu.make_async_remote_copy(
                src_ref=rows(a, *block) if src is None else src, dst_ref=rows(a, *block),
                send_sem=send_sems.at[7 * a + k], recv_sem=recv_sems.at[7 * a + k], device_id=to, device_id_type=MESH)

        mine = [pltpu.make_async_copy(x_refs[a], rows(a, *me), local_sems.at[a]) for a in range(n)]
        first = []
        for a in range(n):
            first.append(copy(a, 0, me, sibling, src=x_refs[a]))
            first += [copy(a, 1 + j, me, (*chip, c), src=x_refs[a]) for j, chip in enumerate(chips)]
        return c, me, sibling, chips, copy, mine, first

    def start(x_refs, out_refs, sems):
        *_, mine, first = parts(x_refs, out_refs, sems)
        for cp in first + mine:
            cp.start()

    def finish(x_refs, out_refs, sems):
        c, me, sibling, chips, copy, mine, first = parts(x_refs, out_refs, sems)
        passed = []
        for j, chip in enumerate(chips):
            for a in range(n):
                copy(a, 1 + j, (*chip, c), me).wait_recv()
                passed.append(copy(a, 4 + j, (*chip, c), sibling))
                passed[-1].start()
        for a in range(n):
            copy(a, 0, sibling, me).wait_recv()
            for j, chip in enumerate(chips):
                copy(a, 4 + j, (*chip, 1 - c), me).wait_recv()
        for cp in first + passed:
            cp.wait_send()
        for cp in mine:
            cp.wait()

    return _Plan(list(xs), [jax.ShapeDtypeStruct((N_DEV,) + x.shape, x.dtype) for x in xs],
                 [pltpu.SemaphoreType.DMA((7 * n,)), pltpu.SemaphoreType.DMA((7 * n,)), pltpu.SemaphoreType.DMA((n,))],
                 start, finish)


def _exchange_plan(copies_of, ins, outs, per_array):
    n = len(ins)

    def start(in_refs, out_refs, sems):
        for cp in copies_of(in_refs, out_refs, sems):
            cp.start()

    def finish(in_refs, out_refs, sems):
        copies = copies_of(in_refs, out_refs, sems)
        for cp in copies:
            cp.wait_recv()
        for cp in copies:
            cp.wait_send()

    return _Plan(list(ins), outs, [pltpu.SemaphoreType.DMA((per_array * n,)), pltpu.SemaphoreType.DMA((per_array * n,))],
                 start, finish)


def _sibling_plan(gs):
    def copies_of(g_refs, got_refs, sems):
        x, y, c, _ = _place()
        return [pltpu.make_async_remote_copy(
            src_ref=g_refs[a].at[2 * k + (1 - c)], dst_ref=got_refs[a].at[k], send_sem=sems[0].at[N_CHIPS * a + k],
            recv_sem=sems[1].at[N_CHIPS * a + k], device_id=(x, y, 1 - c), device_id_type=MESH)
            for a in range(len(gs)) for k in range(N_CHIPS)]

    return _exchange_plan(copies_of, gs, [jax.ShapeDtypeStruct((N_CHIPS,) + g.shape[1:], g.dtype) for g in gs], N_CHIPS)


def _chip_plan(ss):
    def copies_of(s_refs, out_refs, sems):
        x, y, c, chips = _place()
        return [pltpu.make_async_remote_copy(
            src_ref=s_refs[a].at[2 * px + py], dst_ref=out_refs[a].at[j], send_sem=sems[0].at[3 * a + j],
            recv_sem=sems[1].at[3 * a + j], device_id=(px, py, c), device_id_type=MESH)
            for a in range(len(ss)) for j, (px, py) in enumerate(chips)]

    return _exchange_plan(copies_of, ss, [jax.ShapeDtypeStruct((3,) + s.shape[1:], s.dtype) for s in ss], 3)


def _run_plan(plan, name):
    n_in, n_out = len(plan.ins), len(plan.outs)

    def body(*refs):
        args = refs[:n_in], refs[n_in:n_in + n_out], refs[n_in + n_out:]
        plan.start(*args)
        plan.finish(*args)

    return pl.pallas_call(body, out_shape=plan.outs, in_specs=[ANY] * n_in, out_specs=[ANY] * n_out,
                          scratch_shapes=plan.sems, name=name)(*plan.ins)


def _pcall(body, ride, *, grid, in_specs, out_specs, out_shape, scratch_shapes=(), semantics, name):
    if ride is None:
        return pl.pallas_call(body, grid=grid, in_specs=in_specs, out_specs=out_specs, out_shape=out_shape,
                              scratch_shapes=list(scratch_shapes), compiler_params=_params(*semantics), name=name)
    single = not isinstance(out_shape, (list, tuple))
    out_specs_l, out_shape_l = ([out_specs], [out_shape]) if single else (list(out_specs), list(out_shape))
    n_in, n_out, n_scr = len(in_specs), len(out_shape_l), len(scratch_shapes)
    r_in, r_out = len(ride.ins), len(ride.outs)

    def riding(*refs):
        cuts = [n_in, r_in, n_out, r_out, n_scr]
        groups, at = [], 0
        for width in cuts:
            groups.append(refs[at:at + width])
            at += width
        ins, r_ins, outs, r_outs, scr = groups
        sems = refs[at:]
        first = functools.reduce(jnp.logical_and, [pl.program_id(d) == 0 for d in range(len(grid))])
        last = functools.reduce(jnp.logical_and, [pl.program_id(d) == grid[d] - 1 for d in range(len(grid))])

        @pl.when(first)
        def _():
            ride.start(r_ins, r_outs, sems)

        body(*ins, *outs, *scr)

        @pl.when(last)
        def _():
            ride.finish(r_ins, r_outs, sems)

    call = pl.pallas_call(
        riding, grid=grid, in_specs=list(in_specs) + [ANY] * r_in, out_specs=out_specs_l + [ANY] * r_out,
        out_shape=out_shape_l + list(ride.outs), scratch_shapes=list(scratch_shapes) + list(ride.sems),
        compiler_params=_params(*(["arbitrary"] * len(grid))), name=name)

    def run(*args):
        res = call(*args, *ride.ins)
        return (res[0] if single else list(res[:n_out])), list(res[n_out:])

    return run


def _mm(a, b, *, nt=False, out_dtype, tm, tn, a_sqrelu=False, drelu_of=None, b_blocked=False, out_blocked=False,
        name, ride=None):
    m, k = a.shape
    if b_blocked:
        assert not nt and b.shape[1] == k and b.shape[2] == tn
        n = b.shape[0] * tn
    else:
        n = b.shape[0] if nt else b.shape[1]
        assert (b.shape[1] if nt else b.shape[0]) == k
    tm, tn = min(tm, m), min(tn, n)
    assert m % tm == 0 and n % tn == 0

    def body(*refs):
        a_ref, b_ref = refs[0], refs[1]
        o_ref = refs[-1]
        av = a_ref[...]
        if a_sqrelu:
            av = jnp.square(jnp.maximum(av.astype(F32), 0.0))
        acc = _dot(av, b_ref[...], 1, 1 if nt else 0)
        if drelu_of is not None:
            acc = acc * (2.0 * jnp.maximum(refs[2][...].astype(F32), 0.0))
        o_ref[...] = acc.astype(out_dtype)

    if b_blocked:
        b_spec = pl.BlockSpec((None, k, tn), lambda i, j: (j, 0, 0))
    elif nt:
        b_spec = pl.BlockSpec((tn, k), lambda i, j: (j, 0))
    else:
        b_spec = pl.BlockSpec((k, tn), lambda i, j: (0, j))
    in_specs = [pl.BlockSpec((tm, k), lambda i, j: (i, 0)), b_spec]
    args = [a, b]
    if drelu_of is not None:
        in_specs.append(pl.BlockSpec((tm, tn), lambda i, j: (i, j)))
        args.append(drelu_of)
    if out_blocked:
        out_spec = pl.BlockSpec((None, tm, tn), lambda i, j: (j, i, 0))
        out_shape = jax.ShapeDtypeStruct((n // tn, m, tn), out_dtype)
    else:
        out_spec = pl.BlockSpec((tm, tn), lambda i, j: (i, j))
        out_shape = jax.ShapeDtypeStruct((m, n), out_dtype)
    return _pcall(body, ride, grid=(m // tm, n // tn), in_specs=in_specs, out_specs=out_spec, out_shape=out_shape,
                  semantics=("parallel", "parallel"), name=name)(*args)


def _mm_nt_blocked(a, b, *, out_dtype, tm, tn, name):
    m = a.shape[0]
    p, n, kp = b.shape
    assert a.shape[1] == p * kp and m % tm == 0 and n % tn == 0

    def body(a_ref, b_ref, o_ref, acc_ref):
        @pl.when(pl.program_id(2) == 0)
        def _():
            acc_ref[...] = jnp.zeros_like(acc_ref)

        acc_ref[...] += _dot(a_ref[...], b_ref[...], 1, 1)

        @pl.when(pl.program_id(2) == p - 1)
        def _():
            o_ref[...] = acc_ref[...].astype(out_dtype)

    return pl.pallas_call(
        body, grid=(m // tm, n // tn, p),
        in_specs=[pl.BlockSpec((tm, kp), lambda i, j, q: (i, q)), pl.BlockSpec((None, tn, kp), lambda i, j, q: (q, j, 0))],
        out_specs=pl.BlockSpec((tm, tn), lambda i, j, q: (i, j)),
        out_shape=jax.ShapeDtypeStruct((m, n), out_dtype),
        scratch_shapes=[pltpu.VMEM((tm, tn), F32)],
        compiler_params=_params("parallel", "parallel", "arbitrary"), name=name)(a, b)


ROW_TILE = 256


def _norm_fwd(x, w, *, out_dtype, res=None, name):
    t, d = x.shape

    def body(*refs):
        x_ref, w_ref, o_ref = refs[0], refs[1], refs[-1]
        xv = x_ref[...]
        y = xv * lax.rsqrt(jnp.mean(xv * xv, axis=-1, keepdims=True) + NORM_EPS) * w_ref[...]
        if res is not None:
            y = refs[2][...] + y
        o_ref[...] = y.astype(out_dtype)

    row = pl.BlockSpec((ROW_TILE, d), lambda i: (i, 0))
    in_specs = [row, pl.BlockSpec((1, d), lambda i: (0, 0))] + ([row] if res is not None else [])
    args = [x, w] + ([res] if res is not None else [])
    return pl.pallas_call(body, grid=(t // ROW_TILE,), in_specs=in_specs, out_specs=row,
                          out_shape=jax.ShapeDtypeStruct((t, d), out_dtype),
                          compiler_params=_params("parallel"), name=name)(*args)


def _norm_bwd(dy, x, w, *, out_dtype, add=None, name):
    t, d = x.shape

    def body(*refs):
        dy_ref, x_ref, w_ref = refs[0], refs[1], refs[2]
        dx_ref, dw_ref = refs[-2], refs[-1]
        xv = x_ref[...]
        rstd = lax.rsqrt(jnp.mean(xv * xv, axis=-1, keepdims=True) + NORM_EPS)
        xhat = xv * rstd
        dyv = dy_ref[...].astype(F32)
        g = dyv * w_ref[...]
        dx = rstd * (g - xhat * jnp.mean(g * xhat, axis=-1, keepdims=True))
        if add is not None:
            dx = dx + refs[3][...]
        dx_ref[...] = dx.astype(out_dtype)

        @pl.when(pl.program_id(0) == 0)
        def _():
            dw_ref[...] = jnp.zeros_like(dw_ref)

        dw_ref[...] += jnp.sum(dyv * xhat, axis=0, keepdims=True)

    row = pl.BlockSpec((ROW_TILE, d), lambda i: (i, 0))
    vec = pl.BlockSpec((1, d), lambda i: (0, 0))
    in_specs = [row, row, vec] + ([row] if add is not None else [])
    args = [dy, x, w] + ([add] if add is not None else [])
    return pl.pallas_call(body, grid=(t // ROW_TILE,), in_specs=in_specs, out_specs=[row, vec],
                          out_shape=[jax.ShapeDtypeStruct((t, d), out_dtype), jax.ShapeDtypeStruct((1, d), F32)],
                          compiler_params=_params("arbitrary"), name=name)(*args)


def _loss_fwd_bwd(y, target):
    t, d = y.shape

    def body(y_ref, t_ref, l_ref, dy_ref):
        diff = y_ref[...] - t_ref[...]
        dy_ref[...] = diff * (1.0 / d)

        @pl.when(pl.program_id(0) == 0)
        def _():
            l_ref[...] = jnp.zeros_like(l_ref)

        l_ref[...] += 0.5 * jnp.sum(jnp.mean(diff * diff, axis=-1, keepdims=True), axis=0, keepdims=True)

    row = pl.BlockSpec((ROW_TILE, d), lambda i: (i, 0))
    return pl.pallas_call(body, grid=(t // ROW_TILE,), in_specs=[row, row],
                          out_specs=[pl.BlockSpec((8, LANES), lambda i: (0, 0)), row],
                          out_shape=[jax.ShapeDtypeStruct((8, LANES), F32), jax.ShapeDtypeStruct((t, d), F32)],
                          compiler_params=_params("arbitrary"), name="loss")(y, target)


def _gla_specs(t):
    return [pl.BlockSpec((t, GLA_KW), lambda i: (0, 0)),
            pl.BlockSpec((t, GLA_KW), lambda i: (0, 1)),
            pl.BlockSpec((t, GROUP_WIDTH), lambda i: (0, 1)),
            pl.BlockSpec((t, GROUP_WIDTH), lambda i: (0, 0)),
            pl.BlockSpec((t, LANES), lambda i: (0, 4)),
            pl.BlockSpec((LANES, GLA_KW), lambda i: (0, 0)),
            pl.BlockSpec((1, GLA_KW), lambda i: (0, 0)),
            pl.BlockSpec((1, GROUP_WIDTH), lambda i: (0, 0))]


def _gla_fwd(pmm, pel, w_up, b_a, gnorm_w, ride=None):
    t = pmm.shape[0]
    nc = t // CHUNK
    scale = GLA_DK ** -0.5

    def body(q_ref, k_ref, v_ref, r_ref, a_ref, wup_ref, ba_ref, gw_ref, o_ref, st_ref, la_scr, s_scr):
        z = _dot(a_ref[...], wup_ref[...]) + ba_ref[...]
        la_scr[...] = _log_sigmoid(z) * (1.0 / GLA_GATE_TAU)
        s_scr[...] = jnp.zeros_like(s_scr)
        tri = (_iota((CHUNK, CHUNK), 1) <= _iota((CHUNK, CHUNK), 0)).astype(F32)

        def chunk(c, carry):
            rows = pl.ds(pl.multiple_of(c * CHUNK, CHUNK), CHUNK)
            cum = _dot_exact(tri, la_scr[rows, :])
            tot = cum[CHUNK - 1:CHUNK, :]
            kd = k_ref[rows, :].astype(F32) * jnp.exp(tot - cum)
            decay = jnp.exp(tot)
            qs = q_ref[rows, :].astype(F32) * scale
            vv = v_ref[rows, :].astype(F32)
            rr = r_ref[rows, :]
            gate = rr * jax.nn.sigmoid(rr) * gw_ref[...]
            for h in range(GLA_HEADS):
                ks = slice(h * GLA_DK, (h + 1) * GLA_DK)
                vs = slice(h * GLA_DV, (h + 1) * GLA_DV)
                inc_t = _dot(vv[:, vs].T, kd[:, ks])
                s_t = s_scr[vs, :] * decay[:, ks] + inc_t
                s_scr[vs, :] = s_t
                st_ref[c, vs, :] = s_t
                o = _dot(qs[:, ks], s_t, 1, 1)
                y = o * lax.rsqrt(jnp.mean(o * o, axis=-1, keepdims=True) + NORM_EPS)
                o_ref[rows, vs] = (y * gate[:, vs]).astype(o_ref.dtype)
            return carry

        lax.fori_loop(0, nc, chunk, 0)

    return _pcall(
        body, ride, grid=(1,), in_specs=_gla_specs(t),
        out_specs=[pl.BlockSpec((t, GROUP_WIDTH), lambda i: (0, 0)),
                   pl.BlockSpec((nc, GLA_HEADS * GLA_DV, GLA_DK), lambda i: (0, 0, 0))],
        out_shape=[jax.ShapeDtypeStruct((t, GROUP_WIDTH), ACT_DTYPE),
                   jax.ShapeDtypeStruct((nc, GLA_HEADS * GLA_DV, GLA_DK), F32)],
        scratch_shapes=[pltpu.VMEM((t, GLA_KW), F32), pltpu.VMEM((GLA_HEADS * GLA_DV, GLA_DK), F32)],
        semantics=("arbitrary",), name="gla_fwd")(pmm, pmm, pmm, pel, pel, w_up, b_a, gnorm_w)


def _gla_bwd(pmm, pel, w_up, b_a, gnorm_w, states, dmix, ride=None):
    t = pmm.shape[0]
    nc = t // CHUNK
    scale = GLA_DK ** -0.5

    def body(q_ref, k_ref, v_ref, r_ref, a_ref, wup_ref, ba_ref, gw_ref, st_ref, do_ref,
             dq_ref, dk_ref, dv_ref, dr_ref, da_ref, dwup_ref, dba_ref, dgw_ref, la_scr, dz_scr, ds_scr):
        z = _dot(a_ref[...], wup_ref[...]) + ba_ref[...]
        la_scr[...] = _log_sigmoid(z) * (1.0 / GLA_GATE_TAU)
        ds_scr[...] = jnp.zeros_like(ds_scr)
        dgw_ref[...] = jnp.zeros_like(dgw_ref)
        row_i, col_i = _iota((CHUNK, CHUNK), 0), _iota((CHUNK, CHUNK), 1)
        tri = (col_i <= row_i).astype(F32)
        tri_strict = (col_i < row_i).astype(F32)

        def chunk(n, carry):
            c = nc - 1 - n
            rows = pl.ds(pl.multiple_of(c * CHUNK, CHUNK), CHUNK)
            cum = _dot_exact(tri, la_scr[rows, :])
            tot = cum[CHUNK - 1:CHUNK, :]
            e = jnp.exp(tot - cum)
            kd = k_ref[rows, :].astype(F32) * e
            decay = jnp.exp(tot)
            qs = q_ref[rows, :].astype(F32) * scale
            vv = v_ref[rows, :].astype(F32)
            rr = r_ref[rows, :]
            sig = jax.nn.sigmoid(rr)
            silu = rr * sig
            dsilu = sig * (1.0 + rr * (1.0 - sig))
            dout = do_ref[rows, :]
            gw = gw_ref[...]
            c_prev = jnp.maximum(c - 1, 0)
            has_prev = (c > 0).astype(F32)
            zc = _dot(a_ref[rows, :], wup_ref[...]) + ba_ref[...]
            dz_scale = jax.nn.sigmoid(-zc) * (1.0 / GLA_GATE_TAU)
            for h in range(GLA_HEADS):
                ks = slice(h * GLA_DK, (h + 1) * GLA_DK)
                vs = slice(h * GLA_DV, (h + 1) * GLA_DV)
                s_t = st_ref[c, vs, :]
                s_prev = st_ref[c_prev, vs, :] * has_prev
                o = _dot(qs[:, ks], s_t, 1, 1)
                rstd = lax.rsqrt(jnp.mean(o * o, axis=-1, keepdims=True) + NORM_EPS)
                y = o * rstd
                dg = dout[:, vs]
                dgw_ref[:, vs] += jnp.sum(dg * y * silu[:, vs], axis=0, keepdims=True)
                dr_ref[rows, vs] = (dg * y * gw[:, vs] * dsilu[:, vs]).astype(dr_ref.dtype)
                dy = dg * gw[:, vs] * silu[:, vs]
                d_o = rstd * (dy - y * jnp.mean(dy * y, axis=-1, keepdims=True))
                dq_ref[rows, ks] = (_dot(d_o, s_t) * scale).astype(dq_ref.dtype)
                ds_t = ds_scr[vs, :] + _dot(d_o.T, qs[:, ks])
                dv_ref[rows, vs] = _dot(kd[:, ks], ds_t, 1, 1).astype(dv_ref.dtype)
                dkd = _dot(vv[:, vs], ds_t)
                ddecay = jnp.sum(ds_t * s_prev, axis=0, keepdims=True)
                ds_scr[vs, :] = ds_t * decay[:, ks]
                dla = ddecay * decay[:, ks] + _dot_exact(tri_strict, dkd * kd[:, ks])
                dz_scr[rows, ks] = dla * dz_scale[:, ks]
                dk_ref[rows, ks] = (dkd * e[:, ks]).astype(dk_ref.dtype)
            return carry

        lax.fori_loop(0, nc, chunk, 0)
        dz = dz_scr[...]
        da_ref[...] = _dot(dz, wup_ref[...], 1, 1).astype(da_ref.dtype)
        dwup_ref[...] = _dot(a_ref[...].T, dz)
        dba_ref[...] = jnp.sum(dz, axis=0, keepdims=True)

    in_specs = _gla_specs(t) + [
        pl.BlockSpec((nc, GLA_HEADS * GLA_DV, GLA_DK), lambda i: (0, 0, 0)),
        pl.BlockSpec((t, GROUP_WIDTH), lambda i: (0, 0))]
    full = lambda r, c: pl.BlockSpec((r, c), lambda i: (0, 0))
    return _pcall(
        body, ride, grid=(1,), in_specs=in_specs,
        out_specs=[full(t, GLA_KW), full(t, GLA_KW), full(t, GROUP_WIDTH), full(t, GROUP_WIDTH), full(t, LANES),
                   full(LANES, GLA_KW), full(1, GLA_KW), full(1, GROUP_WIDTH)],
        out_shape=[jax.ShapeDtypeStruct((t, GLA_KW), ACT_DTYPE), jax.ShapeDtypeStruct((t, GLA_KW), ACT_DTYPE),
                   jax.ShapeDtypeStruct((t, GROUP_WIDTH), ACT_DTYPE), jax.ShapeDtypeStruct((t, GROUP_WIDTH), ACT_DTYPE),
                   jax.ShapeDtypeStruct((t, LANES), ACT_DTYPE), jax.ShapeDtypeStruct((LANES, GLA_KW), F32),
                   jax.ShapeDtypeStruct((1, GLA_KW), F32), jax.ShapeDtypeStruct((1, GROUP_WIDTH), F32)],
        scratch_shapes=[pltpu.VMEM((t, GLA_KW), F32), pltpu.VMEM((t, GLA_KW), F32),
                        pltpu.VMEM((GLA_HEADS * GLA_DV, GLA_DK), F32)],
        semantics=("arbitrary",), name="gla_bwd")(
            pmm, pmm, pmm, pel, pel, w_up, b_a, gnorm_w, states, dmix)


CUM_BLOCK = 256


def _fox_gate_fwd(pel, b_f):
    t = pel.shape[0]

    def body(f_ref, b_ref, cum_ref, cum_t_ref):
        tri = (_iota((CUM_BLOCK, CUM_BLOCK), 1) <= _iota((CUM_BLOCK, CUM_BLOCK), 0)).astype(F32)
        carry = jnp.zeros((1, LANES), F32)
        for blk in range(t // CUM_BLOCK):
            rows = slice(blk * CUM_BLOCK, (blk + 1) * CUM_BLOCK)
            cum = _dot_exact(tri, _log_sigmoid(f_ref[rows, :] + b_ref[...])) + carry
            cum_ref[rows, :] = cum
            carry = cum[CUM_BLOCK - 1:CUM_BLOCK, :]
        cum_t_ref[...] = cum_ref[...].T

    return pl.pallas_call(
        body, grid=(1,),
        in_specs=[pl.BlockSpec((t, LANES), lambda i: (0, 5)), pl.BlockSpec((1, LANES), lambda i: (0, 0))],
        out_specs=[pl.BlockSpec((t, LANES), lambda i: (0, 0)), pl.BlockSpec((LANES, t), lambda i: (0, 0))],
        out_shape=[jax.ShapeDtypeStruct((t, LANES), F32), jax.ShapeDtypeStruct((LANES, t), F32)],
        compiler_params=_params("arbitrary"), name="fox_gate_fwd")(pel, b_f)


def _fox_gate_bwd(pel, b_f, dcum_t):
    t = pel.shape[0]

    def body(f_ref, b_ref, dct_ref, df_ref, db_ref, dc_scr):
        dc_scr[...] = dct_ref[...].T
        tri_up = (_iota((CUM_BLOCK, CUM_BLOCK), 1) >= _iota((CUM_BLOCK, CUM_BLOCK), 0)).astype(F32)
        carry = jnp.zeros((1, LANES), F32)
        db = jnp.zeros((1, LANES), F32)
        for blk in reversed(range(t // CUM_BLOCK)):
            rows = slice(blk * CUM_BLOCK, (blk + 1) * CUM_BLOCK)
            dls = _dot_exact(tri_up, dc_scr[rows, :]) + carry
            carry = dls[0:1, :]
            df = dls * jax.nn.sigmoid(-(f_ref[rows, :] + b_ref[...]))
            df_ref[rows, :] = df.astype(df_ref.dtype)
            db = db + jnp.sum(df, axis=0, keepdims=True)
        db_ref[...] = db

    return pl.pallas_call(
        body, grid=(1,),
        in_specs=[pl.BlockSpec((t, LANES), lambda i: (0, 5)), pl.BlockSpec((1, LANES), lambda i: (0, 0)),
                  pl.BlockSpec((LANES, t), lambda i: (0, 0))],
        out_specs=[pl.BlockSpec((t, LANES), lambda i: (0, 0)), pl.BlockSpec((1, LANES), lambda i: (0, 0))],
        out_shape=[jax.ShapeDtypeStruct((t, LANES), ACT_DTYPE), jax.ShapeDtypeStruct((1, LANES), F32)],
        scratch_shapes=[pltpu.VMEM((t, LANES), F32)],
        compiler_params=_params("arbitrary"), name="fox_gate_bwd")(pel, b_f, dcum_t)


FOX_Q_BLOCK = 256


def _fox_scores(q_ref, k_ref, cum_ref, cum_t_ref, h, mask):
    hs = slice(h * HEAD_DIM, (h + 1) * HEAD_DIM)
    s = _dot(q_ref[:, hs], k_ref[:, hs], 1, 1) * (HEAD_DIM ** -0.5)
    s = s + (cum_ref[:, h:h + 1] - cum_t_ref[h:h + 1, :])
    return jnp.where(mask, s, NEG)


def _fox_fwd(pmm, cum, cum_t, ride=None):
    t = pmm.shape[0]
    bq = FOX_Q_BLOCK

    def body(q_ref, k_ref, v_ref, cum_ref, cum_t_ref, o_ref, lse_ref):
        i = pl.program_id(0)
        mask = _iota((bq, t), 1) <= i * bq + _iota((bq, t), 0)
        lse_ref[...] = jnp.zeros_like(lse_ref)
        for h in range(ATT_HEADS):
            hs = slice(h * HEAD_DIM, (h + 1) * HEAD_DIM)
            s = _fox_scores(q_ref, k_ref, cum_ref, cum_t_ref, h, mask)
            m = jnp.max(s, axis=-1, keepdims=True)
            p = jnp.exp(s - m)
            l = jnp.sum(p, axis=-1, keepdims=True)
            o_ref[:, hs] = (_dot(p, v_ref[:, hs]) / l).astype(o_ref.dtype)
            lse_ref[:, h:h + 1] = m + jnp.log(l)

    return _pcall(
        body, ride, grid=(t // bq,),
        in_specs=[pl.BlockSpec((bq, GROUP_WIDTH), lambda i: (i, 2)), pl.BlockSpec((t, GROUP_WIDTH), lambda i: (0, 3)),
                  pl.BlockSpec((t, GROUP_WIDTH), lambda i: (0, 4)), pl.BlockSpec((bq, LANES), lambda i: (i, 0)),
                  pl.BlockSpec((8, t), lambda i: (0, 0))],
        out_specs=[pl.BlockSpec((bq, GROUP_WIDTH), lambda i: (i, 0)), pl.BlockSpec((bq, LANES), lambda i: (i, 0))],
        out_shape=[jax.ShapeDtypeStruct((t, GROUP_WIDTH), ACT_DTYPE), jax.ShapeDtypeStruct((t, LANES), F32)],
        semantics=("parallel",), name="fox_fwd")(pmm, pmm, pmm, cum, cum_t)


def _fox_bwd(pmm, cum, cum_t, lse, dmix, ride=None):
    t = pmm.shape[0]
    bq = FOX_Q_BLOCK
    scale = HEAD_DIM ** -0.5

    def body(q_ref, k_ref, v_ref, cum_ref, cum_t_ref, lse_ref, do_ref, dq_ref, dk_ref, dv_ref, dct_ref):
        i = pl.program_id(0)

        @pl.when(i == 0)
        def _():
            dk_ref[...] = jnp.zeros_like(dk_ref)
            dv_ref[...] = jnp.zeros_like(dv_ref)
            dct_ref[...] = jnp.zeros_like(dct_ref)

        mask = _iota((bq, t), 1) <= i * bq + _iota((bq, t), 0)
        for h in range(ATT_HEADS):
            hs = slice(h * HEAD_DIM, (h + 1) * HEAD_DIM)
            s = _fox_scores(q_ref, k_ref, cum_ref, cum_t_ref, h, mask)
            p = jnp.exp(s - lse_ref[:, h:h + 1])
            do = do_ref[:, hs]
            dp = _dot(do, v_ref[:, hs], 1, 1)
            ds = p * (dp - jnp.sum(p * dp, axis=-1, keepdims=True))
            dq_ref[:, hs] = (_dot(ds, k_ref[:, hs]) * scale).astype(dq_ref.dtype)
            dk_ref[:, hs] += _dot(ds, q_ref[:, hs], 0, 0) * scale
            dv_ref[:, hs] += _dot(p, do, 0, 0)
            dct_ref[h:h + 1, :] += -jnp.sum(ds, axis=0, keepdims=True)

    whole = pl.BlockSpec((t, GROUP_WIDTH), lambda i: (0, 0))
    return _pcall(
        body, ride, grid=(t // bq,),
        in_specs=[pl.BlockSpec((bq, GROUP_WIDTH), lambda i: (i, 2)), pl.BlockSpec((t, GROUP_WIDTH), lambda i: (0, 3)),
                  pl.BlockSpec((t, GROUP_WIDTH), lambda i: (0, 4)), pl.BlockSpec((bq, LANES), lambda i: (i, 0)),
                  pl.BlockSpec((8, t), lambda i: (0, 0)), pl.BlockSpec((bq, LANES), lambda i: (i, 0)),
                  pl.BlockSpec((bq, GROUP_WIDTH), lambda i: (i, 1))],
        out_specs=[pl.BlockSpec((bq, GROUP_WIDTH), lambda i: (i, 0)), whole, whole,
                   pl.BlockSpec((LANES, t), lambda i: (0, 0))],
        out_shape=[jax.ShapeDtypeStruct((t, GROUP_WIDTH), ACT_DTYPE), jax.ShapeDtypeStruct((t, GROUP_WIDTH), F32),
                   jax.ShapeDtypeStruct((t, GROUP_WIDTH), F32), jax.ShapeDtypeStruct((LANES, t), F32)],
        semantics=("arbitrary",), name="fox_bwd")(pmm, pmm, pmm, cum, cum_t, lse, dmix)


CA_Q_BLOCK = 4 * CHUNK
CA_WINDOW = CA_Q_BLOCK + CA_LEFT
CA_BASE = 1024


def _ca_bias_base(rel_bias):
    n = rel_bias.shape[0]
    flat = CA_Q_BLOCK + CA_LEFT - REL_CLIP
    tail = CA_BASE - flat - (2 * REL_CLIP + 1)
    return jnp.concatenate([jnp.broadcast_to(rel_bias[:, 2 * REL_CLIP:], (n, flat)), rel_bias[:, ::-1],
                            jnp.broadcast_to(rel_bias[:, :1], (n, tail))], axis=1)


def _ca_bias_base_grad(dbase):
    flat = CA_Q_BLOCK + CA_LEFT - REL_CLIP
    mid = dbase[:, flat:flat + 2 * REL_CLIP + 1][:, ::-1]
    lo = jnp.sum(dbase[:, flat + 2 * REL_CLIP + 1:], axis=1, keepdims=True)
    hi = jnp.sum(dbase[:, :flat], axis=1, keepdims=True)
    pad = jnp.zeros((dbase.shape[0], 2 * REL_CLIP - 1), F32)
    return mid + jnp.concatenate([lo, pad, hi], axis=1)


def _ca_mask(i):
    r, j = _iota((CA_Q_BLOCK, CA_WINDOW), 0), _iota((CA_Q_BLOCK, CA_WINDOW), 1)
    rc, jc = r // CHUNK, j // CHUNK
    return (jc >= rc) & (jc <= rc + CA_LEFT // CHUNK) & (i * CA_Q_BLOCK + j >= CA_LEFT)


def _ca_scores(q_ref, kp_ref, base_ref, win, h, mask):
    hs = slice(h * HEAD_DIM, (h + 1) * HEAD_DIM)
    s = _dot(q_ref[:, hs], kp_ref[win, hs], 1, 1) * (HEAD_DIM ** -0.5)
    rows = jnp.broadcast_to(base_ref[h:h + 1, :], (CA_Q_BLOCK, CA_BASE))
    bias = pltpu.roll(rows, CA_BASE - CA_Q_BLOCK, 1, stride=1, stride_axis=0)[:, :CA_WINDOW]
    return jnp.where(mask, s + bias, NEG)


def _ca_fwd(pmm, kp, vp, base, ride=None):
    t = pmm.shape[0]

    def body(q_ref, kp_ref, vp_ref, base_ref, o_ref, lse_ref):
        i = pl.program_id(0)
        win = pl.ds(pl.multiple_of(i * CA_Q_BLOCK, CA_Q_BLOCK), CA_WINDOW)
        mask = _ca_mask(i)
        lse_ref[...] = jnp.zeros_like(lse_ref)
        for h in range(ATT_HEADS):
            hs = slice(h * HEAD_DIM, (h + 1) * HEAD_DIM)
            s = _ca_scores(q_ref, kp_ref, base_ref, win, h, mask)
            m = jnp.max(s, axis=-1, keepdims=True)
            p = jnp.exp(s - m)
            l = jnp.sum(p, axis=-1, keepdims=True)
            o_ref[:, hs] = (_dot(p, vp_ref[win, hs]) / l).astype(o_ref.dtype)
            lse_ref[:, h:h + 1] = m + jnp.log(l)

    padded = pl.BlockSpec((t + CA_LEFT, GROUP_WIDTH), lambda i: (0, 0))
    return _pcall(
        body, ride, grid=(t // CA_Q_BLOCK,),
        in_specs=[pl.BlockSpec((CA_Q_BLOCK, GROUP_WIDTH), lambda i: (i, 0)), padded, padded,
                  pl.BlockSpec((ATT_HEADS, CA_BASE), lambda i: (0, 0))],
        out_specs=[pl.BlockSpec((CA_Q_BLOCK, GROUP_WIDTH), lambda i: (i, 0)),
                   pl.BlockSpec((CA_Q_BLOCK, LANES), lambda i: (i, 0))],
        out_shape=[jax.ShapeDtypeStruct((t, GROUP_WIDTH), ACT_DTYPE), jax.ShapeDtypeStruct((t, LANES), F32)],
        semantics=("parallel",), name="ca_fwd")(pmm, kp, vp, base)


def _ca_bwd(pmm, kp, vp, base, lse, dmix, ride=None):
    t = pmm.shape[0]
    scale = HEAD_DIM ** -0.5

    def body(q_ref, kp_ref, vp_ref, base_ref, lse_ref, do_ref, dq_ref, dkp_ref, dvp_ref, dbase_ref):
        i = pl.program_id(0)

        @pl.when(i == 0)
        def _():
            dkp_ref[...] = jnp.zeros_like(dkp_ref)
            dvp_ref[...] = jnp.zeros_like(dvp_ref)
            dbase_ref[...] = jnp.zeros_like(dbase_ref)

        win = pl.ds(pl.multiple_of(i * CA_Q_BLOCK, CA_Q_BLOCK), CA_WINDOW)
        mask = _ca_mask(i)
        flip = (_iota((CA_Q_BLOCK, CA_Q_BLOCK), 0) + _iota((CA_Q_BLOCK, CA_Q_BLOCK), 1) == CA_Q_BLOCK - 1).astype(F32)
        for h in range(ATT_HEADS):
            hs = slice(h * HEAD_DIM, (h + 1) * HEAD_DIM)
            s = _ca_scores(q_ref, kp_ref, base_ref, win, h, mask)
            p = jnp.exp(s - lse_ref[:, h:h + 1])
            do = do_ref[:, hs]
            dp = _dot(do, vp_ref[win, hs], 1, 1)
            ds = p * (dp - jnp.sum(p * dp, axis=-1, keepdims=True))
            dq_ref[:, hs] = (_dot(ds, kp_ref[win, hs]) * scale).astype(dq_ref.dtype)
            dkp_ref[win, hs] += _dot(ds, q_ref[:, hs], 0, 0) * scale
            dvp_ref[win, hs] += _dot(p, do, 0, 0)
            rev = jnp.concatenate([_dot(flip, ds), jnp.zeros((CA_Q_BLOCK, CA_BASE - CA_WINDOW), F32)], axis=1)
            lined = pltpu.roll(rev, 1, 1, stride=1, stride_axis=0)
            dbase_ref[h:h + 1, :] += jnp.sum(lined, axis=0, keepdims=True)

    padded = pl.BlockSpec((t + CA_LEFT, GROUP_WIDTH), lambda i: (0, 0))
    return _pcall(
        body, ride, grid=(t // CA_Q_BLOCK,),
        in_specs=[pl.BlockSpec((CA_Q_BLOCK, GROUP_WIDTH), lambda i: (i, 0)), padded, padded,
                  pl.BlockSpec((ATT_HEADS, CA_BASE), lambda i: (0, 0)),
                  pl.BlockSpec((CA_Q_BLOCK, LANES), lambda i: (i, 0)),
                  pl.BlockSpec((CA_Q_BLOCK, GROUP_WIDTH), lambda i: (i, 0))],
        out_specs=[pl.BlockSpec((CA_Q_BLOCK, GROUP_WIDTH), lambda i: (i, 0)), padded, padded,
                   pl.BlockSpec((ATT_HEADS, CA_BASE), lambda i: (0, 0))],
        out_shape=[jax.ShapeDtypeStruct((t, GROUP_WIDTH), ACT_DTYPE),
                   jax.ShapeDtypeStruct((t + CA_LEFT, GROUP_WIDTH), F32),
                   jax.ShapeDtypeStruct((t + CA_LEFT, GROUP_WIDTH), F32),
                   jax.ShapeDtypeStruct((ATT_HEADS, CA_BASE), F32)],
        semantics=("arbitrary",), name="ca_bwd")(pmm, kp, vp, base, lse, dmix)


GELU_C = 0.7978845608028654
GELU_A = 0.044715


def _shift_down(v, k, fill):
    return jnp.where(_iota(v.shape, 0) >= k, pltpu.roll(v, k, 0), fill)


def _shift_up(v, k, fill):
    t = v.shape[0]
    return jnp.where(_iota(v.shape, 0) < t - k, pltpu.roll(v, t - k, 0), fill)


def _linear_scan(a, b, shift):
    k = 1
    while k < a.shape[0]:
        b = a * shift(b, k, 0.0) + b
        a = a * shift(a, k, 1.0)
        k *= 2
    return b


def _neg_expm1(y):
    series = -y * (1.0 + y * (0.5 + y * (1.0 / 6.0 + y * (1.0 / 24.0 + y * (1.0 / 120.0)))))
    return jnp.where(y > -0.1, series, 1.0 - jnp.exp(y))


def _lru_forward(x, g_in, cw, cb, wa, ba, wx, bx, lam):
    xs = [_shift_down(x, CONV_WIDTH - 1 - j, 0.0) for j in range(CONV_WIDTH - 1)] + [x]
    xc = cb + sum(cw[j:j + 1, :] * xs[j] for j in range(CONV_WIDTH))
    r = jax.nn.sigmoid(_dot(xc, wa) + ba)
    i = jax.nn.sigmoid(_dot(xc, wx) + bx)
    lsl = _log_sigmoid(lam)
    la = LRU_C * r * lsl
    a = jnp.exp(la)
    s = jnp.sqrt(_neg_expm1(2.0 * la))
    h = _linear_scan(a, s * (i * xc), _shift_down)
    u = GELU_C * (g_in + GELU_A * g_in * g_in * g_in)
    th = jnp.tanh(u)
    gelu = 0.5 * g_in * (1.0 + th)
    return xs, xc, r, i, lsl, a, s, h, th, gelu


def _lru_specs(t):
    col = lambda off: pl.BlockSpec((t, LANES), lambda j: (0, j + off))
    vec = pl.BlockSpec((1, LANES), lambda j: (0, j))
    mat = pl.BlockSpec((None, LANES, LANES), lambda j: (j, 0, 0))
    return [col(0), col(GROUP_WIDTH // LANES), pl.BlockSpec((CONV_WIDTH, LANES), lambda j: (0, j)),
            vec, mat, vec, mat, vec, vec]


def _lru_fwd(pel, conv_w, conv_b, wa, ba, wx, bx, lam, ride=None):
    t = pel.shape[0]

    def body(g_ref, x_ref, cw_ref, cb_ref, wa_ref, ba_ref, wx_ref, bx_ref, lam_ref, o_ref):
        res = _lru_forward(x_ref[...], g_ref[...], cw_ref[...], cb_ref[...], wa_ref[...], ba_ref[...],
                           wx_ref[...], bx_ref[...], lam_ref[...])
        o_ref[...] = (res[7] * res[9]).astype(o_ref.dtype)

    return _pcall(
        body, ride, grid=(GROUP_WIDTH // LANES,), in_specs=_lru_specs(t),
        out_specs=pl.BlockSpec((t, LANES), lambda j: (0, j)),
        out_shape=jax.ShapeDtypeStruct((t, GROUP_WIDTH), ACT_DTYPE),
        semantics=("parallel",), name="lru_fwd")(pel, pel, conv_w, conv_b, wa, ba, wx, bx, lam)


def _lru_bwd(pel, conv_w, conv_b, wa, ba, wx, bx, lam, dmix, ride=None):
    t = pel.shape[0]

    def body(g_ref, x_ref, cw_ref, cb_ref, wa_ref, ba_ref, wx_ref, bx_ref, lam_ref, do_ref,
             dg_ref, dx_ref, dcw_ref, dcb_ref, dwa_ref, dba_ref, dwx_ref, dbx_ref, dlam_ref):
        g_in, cw, lam = g_ref[...], cw_ref[...], lam_ref[...]
        xs, xc, r, i, lsl, a, s, h, th, gelu = _lru_forward(
            x_ref[...], g_in, cw, cb_ref[...], wa_ref[...], ba_ref[...], wx_ref[...], bx_ref[...], lam)
        dout = do_ref[...]
        dgelu = 0.5 * (1.0 + th) + 0.5 * g_in * (1.0 - th * th) * GELU_C * (1.0 + 3.0 * GELU_A * g_in * g_in)
        dg_ref[...] = (dout * h * dgelu).astype(dg_ref.dtype)
        gsum = _linear_scan(_shift_up(a, 1, 0.0), dout * gelu, _shift_up)
        da = gsum * _shift_down(h, 1, 0.0)
        di = gsum * s * xc
        dla = da * a - gsum * (i * xc) * (a * a / s)
        dlam_ref[...] = jnp.sum(dla * (LRU_C * r), axis=0, keepdims=True) * jax.nn.sigmoid(-lam)
        dpr = dla * (LRU_C * lsl) * r * (1.0 - r)
        dpi = di * i * (1.0 - i)
        dxc = gsum * s * i + _dot(dpr, wa_ref[...], 1, 1) + _dot(dpi, wx_ref[...], 1, 1)
        xct = xc.T
        dwa_ref[...] = _dot(xct, dpr)
        dwx_ref[...] = _dot(xct, dpi)
        dba_ref[...] = jnp.sum(dpr, axis=0, keepdims=True)
        dbx_ref[...] = jnp.sum(dpi, axis=0, keepdims=True)
        dcb_ref[...] = jnp.sum(dxc, axis=0, keepdims=True)
        for j in range(CONV_WIDTH):
            dcw_ref[j:j + 1, :] = jnp.sum(dxc * xs[j], axis=0, keepdims=True)
        dx = cw[CONV_WIDTH - 1:CONV_WIDTH, :] * dxc
        for j in range(CONV_WIDTH - 1):
            dx = dx + cw[j:j + 1, :] * _shift_up(dxc, CONV_WIDTH - 1 - j, 0.0)
        dx_ref[...] = dx.astype(dx_ref.dtype)

    col = pl.BlockSpec((t, LANES), lambda j: (0, j))
    vec = pl.BlockSpec((1, LANES), lambda j: (0, j))
    mat = pl.BlockSpec((None, LANES, LANES), lambda j: (j, 0, 0))
    nb = GROUP_WIDTH // LANES
    vshape = jax.ShapeDtypeStruct((1, GROUP_WIDTH), F32)
    mshape = jax.ShapeDtypeStruct((nb, LANES, LANES), F32)
    return _pcall(
        body, ride, grid=(nb,),
        in_specs=_lru_specs(t) + [pl.BlockSpec((t, LANES), lambda j: (0, j + nb))],
        out_specs=[col, col, pl.BlockSpec((CONV_WIDTH, LANES), lambda j: (0, j)), vec, mat, vec, mat, vec, vec],
        out_shape=[jax.ShapeDtypeStruct((t, GROUP_WIDTH), ACT_DTYPE), jax.ShapeDtypeStruct((t, GROUP_WIDTH), ACT_DTYPE),
                   jax.ShapeDtypeStruct((CONV_WIDTH, GROUP_WIDTH), F32), vshape, mshape, vshape, mshape, vshape, vshape],
        semantics=("parallel",), name="lru_bwd")(
            pel, pel, conv_w, conv_b, wa, ba, wx, bx, lam, dmix)


def _block_diag_pairs(w):
    z = jnp.zeros((LRU_BLOCK_DIM, LRU_BLOCK_DIM), w.dtype)
    return jnp.stack([jnp.block([[w[2 * j], z], [z, w[2 * j + 1]]]) for j in range(w.shape[0] // 2)])


def _block_diag_pairs_grad(dw):
    b = LRU_BLOCK_DIM
    return jnp.stack([dw[n // 2, (n % 2) * b:(n % 2 + 1) * b, (n % 2) * b:(n % 2 + 1) * b] for n in range(2 * dw.shape[0])])


def _row_tile(r):
    return ROW_TILE if r % ROW_TILE == 0 else r


def _pair_sum(g, got, place, name):
    _, r, c = g.shape
    tile = _row_tile(r)

    def body(place_ref, a_ref, b_ref, o_ref):
        o_ref[...] = (a_ref[...].astype(F32) + b_ref[...].astype(F32)).astype(o_ref.dtype)

    blk = pl.BlockSpec((1, tile, c), lambda k, i, place_ref: (k, i, 0))
    return pl.pallas_call(
        body,
        grid_spec=pltpu.PrefetchScalarGridSpec(
            num_scalar_prefetch=1, grid=(N_CHIPS, r // tile),
            in_specs=[pl.BlockSpec((1, tile, c), lambda k, i, place_ref: (2 * k + place_ref[0], i, 0)), blk],
            out_specs=blk),
        out_shape=jax.ShapeDtypeStruct(got.shape, got.dtype),
        compiler_params=_params("parallel", "parallel"), name=name)(place, g, got)


def _adamw_update(g, w_ref, m_ref, v_ref, g_ref, d_ref, nm_ref, nv_ref):
    nm = ADAM_B1 * m_ref[...] + (1.0 - ADAM_B1) * g
    nv = ADAM_B2 * v_ref[...] + (1.0 - ADAM_B2) * jnp.square(g)
    m_hat = nm / (1.0 - ADAM_B1 ** ADAM_STEP)
    v_hat = nv / (1.0 - ADAM_B2 ** ADAM_STEP)
    g_ref[...] = g
    d_ref[...] = -ADAM_LR * (m_hat / (jnp.sqrt(v_hat) + ADAM_EPS) + ADAM_WD * w_ref[...])
    nm_ref[...] = nm
    nv_ref[...] = nv


def _adamw_sharded(s, recv, w, m, v, place, name):
    r, c = w.shape
    tile = _row_tile(r)

    def body(place_ref, s_ref, r_ref, w_ref, m_ref, v_ref, *outs):
        g = s_ref[0].astype(F32) + r_ref[0].astype(F32) + r_ref[1].astype(F32) + r_ref[2].astype(F32)
        _adamw_update(g, w_ref, m_ref, v_ref, *outs)

    blk = pl.BlockSpec((tile, c), lambda i, place_ref: (i, 0))
    out = jax.ShapeDtypeStruct((r, c), F32)
    return pl.pallas_call(
        body,
        grid_spec=pltpu.PrefetchScalarGridSpec(
            num_scalar_prefetch=1, grid=(r // tile,),
            in_specs=[pl.BlockSpec((1, tile, c), lambda i, place_ref: (place_ref[1], i, 0)),
                      pl.BlockSpec((3, tile, c), lambda i, place_ref: (0, i, 0)), blk, blk, blk],
            out_specs=[blk, blk, blk, blk]),
        out_shape=[out, out, out, out], compiler_params=_params("parallel"), name=name)(place, s, recv, w, m, v)


def _adamw_replicated(parts, w, m, v, name):
    p, r, c = parts.shape
    tile = _row_tile(r)

    def body(p_ref, w_ref, m_ref, v_ref, *outs):
        g = p_ref[0].astype(F32)
        for k in range(1, p):
            g = g + p_ref[k].astype(F32)
        _adamw_update(g, w_ref, m_ref, v_ref, *outs)

    blk = pl.BlockSpec((tile, c), lambda i: (i, 0))
    out = jax.ShapeDtypeStruct((r, c), F32)
    return pl.pallas_call(body, grid=(r // tile,),
                          in_specs=[pl.BlockSpec((p, tile, c), lambda i: (0, i, 0)), blk, blk, blk],
                          out_specs=[blk, blk, blk, blk], out_shape=[out, out, out, out],
                          compiler_params=_params("parallel"), name=name)(parts, w, m, v)


SLAB_COLS = 1024
SHARDED = {"norm_w": 2, "w_in_even": 2, "gla_w_a_up": 2, "w_out_even": 1, "w_in_odd": 2, "conv_w": 2, "conv_b": 1,
           "lru_b_a": 1, "lru_b_x": 1, "lru_lambda": 1, "w_out_odd": 1, "w_mlp_up": 2, "w_mlp_down": 1}
REPLICATED = ["gla_b_a", "gla_norm_w", "fox_b_f", "rel_bias", "lru_w_a", "lru_w_x"]
WEIGHTS = ["norm_w", "w_in_even", "gla_w_a_up", "gla_b_a", "gla_norm_w", "fox_b_f", "w_out_even", "w_in_odd",
           "rel_bias", "conv_w", "conv_b", "lru_w_a", "lru_b_a", "lru_w_x", "lru_b_x", "lru_lambda", "w_out_odd",
           "w_mlp_up", "w_mlp_down"]
MATRICES = ("w_in_even", "w_out_even", "w_in_odd", "w_out_odd", "w_mlp_up", "w_mlp_down")
VECTORS = tuple(n for n in SHARDED if n not in MATRICES)
VEC_SLAB_ROWS = 16
REPL_SLAB_ROWS = 72
MATRIX_BLOCKS = (("w_in_even", 0), ("w_out_even", 0), ("w_in_odd", 0), ("w_out_odd", 0),
                 ("w_mlp_up", 0), ("w_mlp_up", 1), ("w_mlp_down", 0), ("w_mlp_down", 1))


def _rows_of(shape):
    n = 1
    for s in shape:
        n *= s
    return -(-n // SLAB_COLS), n


def _pack(arrays, total_rows, lead=()):
    parts, used = [], 0
    for a in arrays:
        rows, n = _rows_of(a.shape[len(lead):])
        flat = a.reshape(lead + (n,))
        flat = jnp.pad(flat, [(0, 0)] * len(lead) + [(0, rows * SLAB_COLS - n)])
        parts.append(flat.reshape(lead + (rows, SLAB_COLS)))
        used += rows
    parts.append(jnp.zeros(lead + (total_rows - used, SLAB_COLS), parts[0].dtype))
    return jnp.concatenate(parts, axis=len(lead))


def _unpack(slab, shapes, lead=()):
    out, row = [], 0
    for shape in shapes:
        rows, n = _rows_of(shape)
        seg = lax.slice_in_dim(slab, row, row + rows, axis=len(lead))
        out.append(seg.reshape(lead + (rows * SLAB_COLS,))[..., :n].reshape(lead + tuple(shape)))
        row += rows
    return out


def _join_shards(blocks, axis):
    moved = jnp.moveaxis(blocks, 0, axis)
    shape = moved.shape
    return moved.reshape(shape[:axis] + (shape[axis] * shape[axis + 1],) + shape[axis + 2:])


def _split_shards(full, axis):
    shape = full.shape
    cut = full.reshape(shape[:axis] + (N_DEV, shape[axis] // N_DEV) + shape[axis + 1:])
    return jnp.moveaxis(cut, axis, 0)


EVEN_SPLITS = (0, 256, 512, 1024, 1536, 1552, 2064, 2576, 3088, 3096)


def _even_in_split(w):
    c = [w[:, EVEN_SPLITS[k]:EVEN_SPLITS[k + 1]] for k in range(9)]
    gq, gk, gv, gr, ga, fq, fk, fv, ff = c
    padcols = lambda a: jnp.pad(a, ((0, 0), (0, LANES - a.shape[1])))
    return jnp.concatenate([gq, gk, gv, fq, fk, fv], axis=1), jnp.concatenate([gr, padcols(ga), padcols(ff)], axis=1)


def _even_in_merge(dmm, dele):
    return jnp.concatenate([dmm[:, :1024], dele[:, :512], dele[:, 512:512 + GLA_RANK], dmm[:, 1024:2560],
                            dele[:, 640:640 + ATT_HEADS]], axis=1)


def _column_shards(full):
    r, c = full.shape
    return jnp.moveaxis(full.reshape(r, N_DEV, c // N_DEV), 1, 0)


def _forward_backward(x, target, shard, vec_slab, vec_shapes, w, place):
    w = dict(w)
    g, dnorm, sums, recv = {}, {}, {}, {}
    nrm = lambda l, k: w["norm_w"][l, k][None, :]
    gather = lambda *keys: _gather_plan([shard[k] for k in keys])
    blocks = lambda r, c: (N_DEV, r // N_DEV, c)

    def pair_sum(key):
        sums[key] = _pair_sum(g[key], got[key], place, f"rs_pair_sum_{key[0]}_{key[1]}")

    got = {}

    def mlp_fwd(xin, layer, ride_up, ride_down):
        h = _norm_fwd(xin, nrm(layer, 2), out_dtype=ACT_DTYPE, name=f"norm_mlp_{layer}")
        u = _mm(h, w["w_mlp_up"][layer], out_dtype=ACT_DTYPE, tm=1024, tn=D_FF // N_DEV, b_blocked=True,
                name=f"mlp_up_{layer}", ride=ride_up)
        u, rode_up = u if ride_up is not None else (u, None)
        yv = _mm(u, w["w_mlp_down"][layer], out_dtype=F32, tm=512, tn=512, a_sqrelu=True,
                 name=f"mlp_down_{layer}", ride=ride_down)
        yv, rode_down = yv if ride_down is not None else (yv, None)
        xout = _norm_fwd(yv, nrm(layer, 3), out_dtype=F32, res=xin, name=f"norm_mlp_out_{layer}")
        return xout, (xin, h, u, yv), rode_up, rode_down

    def mlp_bwd(dxout, saved, layer, ride):
        xin, h, u, yv = saved
        k_up, k_down = ("w_mlp_up", layer), ("w_mlp_down", layer)
        dy, dnorm[(layer, 3)] = _norm_bwd(dxout, yv, nrm(layer, 3), out_dtype=ACT_DTYPE, name=f"norm_mlp_out_bwd_{layer}")
        du = _mm(dy, w["w_mlp_down"][layer], nt=True, out_dtype=ACT_DTYPE, tm=512, tn=512, drelu_of=u,
                 name=f"mlp_down_dx_{layer}", ride=ride)
        rode = None
        if ride is not None:
            du, rode = du
        g[k_down] = _mm(u.T, dy, out_dtype=WIRE_DTYPE, tm=512, tn=512, a_sqrelu=True,
                        name=f"mlp_down_dw_{layer}").reshape(blocks(D_FF, D_MODEL))
        g[k_up], (got[k_down],) = _mm(h.T, du, out_dtype=WIRE_DTYPE, tm=512, tn=D_FF // N_DEV, out_blocked=True,
                                      name=f"mlp_up_dw_{layer}", ride=_sibling_plan([g[k_down]]))
        w_up = jnp.moveaxis(w["w_mlp_up"][layer], 0, 1).reshape(D_MODEL, D_FF)
        dh, (got[k_up],) = _mm(du, w_up, nt=True, out_dtype=F32, tm=512, tn=512, name=f"mlp_up_dx_{layer}",
                               ride=_sibling_plan([g[k_up]]))
        pair_sum(k_down)
        pair_sum(k_up)
        dxin, dnorm[(layer, 2)] = _norm_bwd(dh, xin, nrm(layer, 2), out_dtype=F32, add=dxout, name=f"norm_mlp_bwd_{layer}")
        return dxin, rode

    first = _run_plan(_gather_plan([shard[("w_in_even", 0)], vec_slab]), "weights_all_gather_first")
    w["w_in_even"] = _join_shards(first[0], 1)
    for n, b in zip(VECTORS, _unpack(first[1], vec_shapes, lead=(N_DEV,))):
        w[n] = _join_shards(b, SHARDED[n])
    w["w_mlp_up"], w["w_mlp_down"] = [None] * DEPTH, [None] * DEPTH

    wmm_e, wel_e = _even_in_split(w["w_in_even"])
    w_up_pad = jnp.pad(w["gla_w_a_up"][0], ((0, LANES - GLA_RANK), (0, 0)))
    b_f_pad = jnp.pad(w["fox_b_f"], ((0, 0), (0, LANES - ATT_HEADS)))
    h0 = _norm_fwd(x, nrm(0, 0), out_dtype=ACT_DTYPE, name="norm_in_0")
    pmm0, (w_out_even,) = _mm(h0, wmm_e, out_dtype=ACT_DTYPE, tm=1024, tn=512, name="in_even_mm",
                              ride=gather(("w_out_even", 0)))
    pel0 = _mm(h0, wel_e, out_dtype=F32, tm=1024, tn=768, name="in_even_el")
    (out_a, states), (w["w_mlp_up"][0],) = _gla_fwd(pmm0, pel0, w_up_pad, w["gla_b_a"], w["gla_norm_w"],
                                                    ride=gather(("w_mlp_up", 0)))
    cum, cum_t = _fox_gate_fwd(pel0, b_f_pad)
    (out_b, lse_b), (w_mlp_down0, w_in_odd) = _fox_fwd(pmm0, cum, cum_t, ride=gather(("w_mlp_down", 0), ("w_in_odd", 0)))
    w["w_out_even"] = w_out_even.reshape(D_MODEL, D_MODEL)
    w["w_mlp_down"][0] = w_mlp_down0.reshape(D_FF, D_MODEL)
    w["w_in_odd"] = _join_shards(w_in_odd, 1)
    mix_in0 = jnp.concatenate([out_a, out_b], axis=1)
    mix0 = _mm(mix_in0, w["w_out_even"], out_dtype=F32, tm=1024, tn=512, name="out_even")
    x1 = _norm_fwd(mix0, nrm(0, 1), out_dtype=F32, res=x, name="norm_mix_0")
    x2, mlp0, (w_out_odd,), (w["w_mlp_up"][1],) = mlp_fwd(x1, 0, gather(("w_out_odd", 0)), gather(("w_mlp_up", 1)))
    w["w_out_odd"] = w_out_odd.reshape(D_MODEL, D_MODEL)

    w_in_o = w["w_in_odd"]
    n_mm_o = 3 * GROUP_WIDTH
    wa_bd, wx_bd = _block_diag_pairs(w["lru_w_a"][0]), _block_diag_pairs(w["lru_w_x"][0])
    base = _ca_bias_base(w["rel_bias"][0])
    h1 = _norm_fwd(x2, nrm(1, 0), out_dtype=ACT_DTYPE, name="norm_in_1")
    pmm1 = _mm(h1, w_in_o[:, :n_mm_o], out_dtype=ACT_DTYPE, tm=1024, tn=512, name="in_odd_mm")
    pel1 = _mm(h1, w_in_o[:, n_mm_o:], out_dtype=F32, tm=1024, tn=512, name="in_odd_el")
    kp = jnp.pad(pmm1[:, GROUP_WIDTH:2 * GROUP_WIDTH], ((CA_LEFT, 0), (0, 0)))
    vp = jnp.pad(pmm1[:, 2 * GROUP_WIDTH:], ((CA_LEFT, 0), (0, 0)))
    (out_c, lse_c), (w_mlp_down1,) = _ca_fwd(pmm1, kp, vp, base, ride=gather(("w_mlp_down", 1)))
    w["w_mlp_down"][1] = w_mlp_down1.reshape(D_FF, D_MODEL)
    lru_args = (pel1, w["conv_w"][0], w["conv_b"], wa_bd, w["lru_b_a"], wx_bd, w["lru_b_x"], w["lru_lambda"])
    out_d = _lru_fwd(*lru_args)
    mix_in1 = jnp.concatenate([out_c, out_d], axis=1)
    mix1 = _mm(mix_in1, w["w_out_odd"], out_dtype=F32, tm=1024, tn=512, name="out_odd")
    x3 = _norm_fwd(mix1, nrm(1, 1), out_dtype=F32, res=x2, name="norm_mix_1")
    x4, mlp1, _, _ = mlp_fwd(x3, 1, None, None)

    loss, dx4 = _loss_fwd_bwd(x4, target)

    k_oo, k_io, k_oe, k_ie = ("w_out_odd", 0), ("w_in_odd", 0), ("w_out_even", 0), ("w_in_even", 0)
    mlp_keys = lambda l: [("w_mlp_down", l), ("w_mlp_up", l)]
    dx3, _ = mlp_bwd(dx4, mlp1, 1, None)
    dmix1, dnorm[(1, 1)] = _norm_bwd(dx3, mix1, nrm(1, 1), out_dtype=ACT_DTYPE, name="norm_mix_bwd_1")
    g[k_oo] = _mm(mix_in1.T, dmix1, out_dtype=WIRE_DTYPE, tm=512, tn=512, name="out_odd_dw").reshape(
        blocks(D_MODEL, D_MODEL))
    dmix_in1, (got[k_oo],) = _mm(dmix1, w["w_out_odd"], nt=True, out_dtype=F32, tm=512, tn=512, name="out_odd_dx",
                                 ride=_sibling_plan([g[k_oo]]))
    (dq_c, dkp, dvp, dbase), rode = _ca_bwd(pmm1, kp, vp, base, lse_c, dmix_in1,
                                            ride=_chip_plan([sums[k] for k in mlp_keys(1)]))
    recv.update(zip(mlp_keys(1), rode))
    pair_sum(k_oo)
    (dgate, dxin, g_conv_w, g_conv_b, dwa_bd, g_lru_b_a, dwx_bd, g_lru_b_x, g_lru_lambda), (recv[k_oo],) = _lru_bwd(
        *lru_args, dmix_in1, ride=_chip_plan([sums[k_oo]]))
    dp1 = jnp.concatenate([dq_c, dkp[CA_LEFT:].astype(ACT_DTYPE), dvp[CA_LEFT:].astype(ACT_DTYPE), dgate, dxin], axis=1)
    g[k_io] = _column_shards(_mm(h1.T, dp1, out_dtype=WIRE_DTYPE, tm=512, tn=512, name="in_odd_dw"))
    dh1, (got[k_io],) = _mm(dp1, w_in_o, nt=True, out_dtype=F32, tm=512, tn=512, name="in_odd_dx",
                            ride=_sibling_plan([g[k_io]]))
    pair_sum(k_io)
    dx2, dnorm[(1, 0)] = _norm_bwd(dh1, x2, nrm(1, 0), out_dtype=F32, add=dx3, name="norm_in_bwd_1")
    g["rel_bias"] = _ca_bias_base_grad(dbase)[None]
    g["conv_w"], g["conv_b"] = g_conv_w[None], g_conv_b
    g["lru_w_a"], g["lru_w_x"] = _block_diag_pairs_grad(dwa_bd)[None], _block_diag_pairs_grad(dwx_bd)[None]
    g["lru_b_a"], g["lru_b_x"], g["lru_lambda"] = g_lru_b_a, g_lru_b_x, g_lru_lambda

    dx1, (recv[k_io],) = mlp_bwd(dx2, mlp0, 0, _chip_plan([sums[k_io]]))
    dmix0, dnorm[(0, 1)] = _norm_bwd(dx1, mix0, nrm(0, 1), out_dtype=ACT_DTYPE, name="norm_mix_bwd_0")
    g[k_oe] = _mm(mix_in0.T, dmix0, out_dtype=WIRE_DTYPE, tm=512, tn=512, name="out_even_dw").reshape(
        blocks(D_MODEL, D_MODEL))
    dmix_in0, (got[k_oe],) = _mm(dmix0, w["w_out_even"], nt=True, out_dtype=F32, tm=512, tn=512, name="out_even_dx",
                                 ride=_sibling_plan([g[k_oe]]))
    (dq_a, dk_a, dv_a, dr_a, da_a, dw_up_pad, g_gla_b_a, g_gla_norm_w), rode = _gla_bwd(
        pmm0, pel0, w_up_pad, w["gla_b_a"], w["gla_norm_w"], states, dmix_in0,
        ride=_chip_plan([sums[k] for k in mlp_keys(0)]))
    recv.update(zip(mlp_keys(0), rode))
    pair_sum(k_oe)
    (dq_b, dk_b, dv_b, dcum_t), (recv[k_oe],) = _fox_bwd(pmm0, cum, cum_t, lse_b, dmix_in0, ride=_chip_plan([sums[k_oe]]))
    df_b, db_f = _fox_gate_bwd(pel0, b_f_pad, dcum_t)
    g["gla_w_a_up"] = dw_up_pad[:GLA_RANK][None]
    g["gla_b_a"], g["gla_norm_w"], g["fox_b_f"] = g_gla_b_a, g_gla_norm_w, db_f[:, :ATT_HEADS]
    dp0 = jnp.concatenate([dq_a, dk_a, dv_a, dq_b, dk_b.astype(ACT_DTYPE), dv_b.astype(ACT_DTYPE), dr_a, da_a, df_b],
                          axis=1)
    w_perm = jnp.concatenate([wmm_e, wel_e], axis=1)
    n_mm_e = wmm_e.shape[1]
    dw_perm, (repl_parts,) = _mm(h0.T, dp0, out_dtype=WIRE_DTYPE, tm=512, tn=dp0.shape[1] // 2, name="in_even_dw",
                                 ride=_gather_plan([_pack([g[n] for n in REPLICATED], REPL_SLAB_ROWS)]))
    g[k_ie] = _column_shards(_even_in_merge(dw_perm[:, :n_mm_e], dw_perm[:, n_mm_e:]))
    dh0, (got[k_ie],) = _mm(dp0, w_perm, nt=True, out_dtype=F32, tm=512, tn=512, name="in_even_dx",
                            ride=_sibling_plan([g[k_ie]]))
    pair_sum(k_ie)
    dx0, dnorm[(0, 0)] = _norm_bwd(dh0, x, nrm(0, 0), out_dtype=F32, add=dx1, name="norm_in_bwd_0")

    g["norm_w"] = jnp.stack([jnp.concatenate([dnorm[(l, k)] for k in range(4)], axis=0) for l in range(DEPTH)])
    k_vec = ("vectors", 0)
    vec_grads = _pack([_split_shards(g[n], SHARDED[n]) for n in VECTORS], VEC_SLAB_ROWS, lead=(N_DEV,))
    g[k_vec] = vec_grads.astype(WIRE_DTYPE)
    (got[k_vec],) = _run_plan(_sibling_plan([g[k_vec]]), "rs_sibling_exchange_last")
    pair_sum(k_vec)
    recv[k_ie], recv[k_vec] = _run_plan(_chip_plan([sums[k_ie], sums[k_vec]]), "rs_chip_exchange_last")
    return loss, dx0, sums, recv, repl_parts


def kernel(x, norm_w, w_in_even, gla_w_a_up, gla_b_a, gla_norm_w, fox_b_f, w_out_even, w_in_odd, rel_bias, conv_w, conv_b, lru_w_a, lru_b_a, lru_w_x, lru_b_x, lru_lambda, w_out_odd, w_mlp_up, w_mlp_down, loss_target, m_norm_w, m_w_in_even, m_gla_w_a_up, m_gla_b_a, m_gla_norm_w, m_fox_b_f, m_w_out_even, m_w_in_odd, m_rel_bias, m_conv_w, m_conv_b, m_lru_w_a, m_lru_b_a, m_lru_w_x, m_lru_b_x, m_lru_lambda, m_w_out_odd, m_w_mlp_up, m_w_mlp_down, v_norm_w, v_w_in_even, v_gla_w_a_up, v_gla_b_a, v_gla_norm_w, v_fox_b_f, v_w_out_even, v_w_in_odd, v_rel_bias, v_conv_w, v_conv_b, v_lru_w_a, v_lru_b_a, v_lru_w_x, v_lru_b_x, v_lru_lambda, v_w_out_odd, v_w_mlp_up, v_w_mlp_down):
    wts = dict(zip(WEIGHTS, (norm_w, w_in_even, gla_w_a_up, gla_b_a, gla_norm_w, fox_b_f, w_out_even, w_in_odd, rel_bias,
                             conv_w, conv_b, lru_w_a, lru_b_a, lru_w_x, lru_b_x, lru_lambda, w_out_odd, w_mlp_up,
                             w_mlp_down)))
    mom = dict(zip(WEIGHTS, (m_norm_w, m_w_in_even, m_gla_w_a_up, m_gla_b_a, m_gla_norm_w, m_fox_b_f, m_w_out_even,
                             m_w_in_odd, m_rel_bias, m_conv_w, m_conv_b, m_lru_w_a, m_lru_b_a, m_lru_w_x, m_lru_b_x,
                             m_lru_lambda, m_w_out_odd, m_w_mlp_up, m_w_mlp_down)))
    var = dict(zip(WEIGHTS, (v_norm_w, v_w_in_even, v_gla_w_a_up, v_gla_b_a, v_gla_norm_w, v_fox_b_f, v_w_out_even,
                             v_w_in_odd, v_rel_bias, v_conv_w, v_conv_b, v_lru_w_a, v_lru_b_a, v_lru_w_x, v_lru_b_x,
                             v_lru_lambda, v_w_out_odd, v_w_mlp_up, v_w_mlp_down)))
    vec_shapes = [wts[n].shape for n in VECTORS]
    repl_shapes = [wts[n].shape for n in REPLICATED]
    place = jnp.stack([lax.axis_index("c"), 2 * lax.axis_index("x") + lax.axis_index("y")]).astype(jnp.int32)

    shard = {(n, l): wts[n][l].astype(WIRE_DTYPE) for n, l in MATRIX_BLOCKS}
    vec_slab = _pack([wts[n] for n in VECTORS], VEC_SLAB_ROWS)
    loss_blk, dx, sums, recv, repl_parts = _forward_backward(
        x[0], loss_target[0], shard, vec_slab, vec_shapes, {n: wts[n] for n in REPLICATED}, place)
    loss = lax.psum(loss_blk[0, 0], ("x", "y", "c"))

    k_vec = ("vectors", 0)

    def local2d(d, k):
        return _pack([d[n] for n in VECTORS], VEC_SLAB_ROWS) if k == k_vec else d[k[0]][k[1]]

    upd = {k: _adamw_sharded(sums[k], recv[k], local2d(wts, k), local2d(mom, k), local2d(var, k), place,
                             f"adamw_{k[0]}_{k[1]}") for k in list(MATRIX_BLOCKS) + [k_vec]}
    rp = _adamw_replicated(repl_parts, _pack([wts[n] for n in REPLICATED], REPL_SLAB_ROWS),
                           _pack([mom[n] for n in REPLICATED], REPL_SLAB_ROWS),
                           _pack([var[n] for n in REPLICATED], REPL_SLAB_ROWS), "adamw_replicated")
    outs = []
    for kind in range(4):
        vals = dict(zip(VECTORS, _unpack(upd[k_vec][kind], vec_shapes)))
        vals.update(zip(REPLICATED, _unpack(rp[kind], repl_shapes)))
        for n in MATRICES:
            vals[n] = jnp.stack([upd[(n, l)][kind] for l in range(wts[n].shape[0])])
        outs += [vals[n] for n in WEIGHTS]
    return (loss, dx[None], *outs)
```

```python
import functools
from typing import Callable, NamedTuple

import jax
import jax.numpy as jnp
from jax import lax
from jax.experimental import pallas as pl
from jax.experimental.pallas import tpu as pltpu

F32 = jnp.float32
MXU_DTYPE = jnp.bfloat16
ACT_DTYPE = jnp.bfloat16
WIRE_DTYPE = jnp.bfloat16

V7X_VMEM_BYTES = 64 * 1024 * 1024
VMEM_LIMIT = (V7X_VMEM_BYTES * 7) // 8
LANES = 128

D_MODEL = 1024
SEQ = 2048
DEPTH = 2
CHUNK = 64
GROUP_WIDTH = D_MODEL // 2
D_FF = 4 * D_MODEL
NORM_EPS = 1e-6
GLA_HEADS = 4
GLA_DV = GROUP_WIDTH // GLA_HEADS
GLA_DK = GLA_DV // 2
GLA_KW = GLA_HEADS * GLA_DK
GLA_RANK = 16
GLA_GATE_TAU = 16.0
HEAD_DIM = 64
ATT_HEADS = GROUP_WIDTH // HEAD_DIM
CA_LEFT = 8 * CHUNK
REL_CLIP = 128
LRU_BLOCK_DIM = 64
CONV_WIDTH = 4
LRU_C = 8.0
N_DEV = 8

ADAM_LR = 0.001
ADAM_B1 = 0.9
ADAM_B2 = 0.999
ADAM_EPS = 1e-08
ADAM_WD = 0.01
ADAM_STEP = 10

NEG = float(jnp.finfo(jnp.float32).min)
MESH = pl.DeviceIdType.MESH


def _params(*sem):
    return pltpu.CompilerParams(dimension_semantics=sem, vmem_limit_bytes=VMEM_LIMIT)


def _dot(a, b, ca=1, cb=0):
    return lax.dot_general(a.astype(MXU_DTYPE), b.astype(MXU_DTYPE), (((ca,), (cb,)), ((), ())),
                           preferred_element_type=F32)


def _dot_exact(a, b):
    return lax.dot_general(a, b, (((1,), (0,)), ((), ())), precision=lax.Precision.HIGHEST,
                           preferred_element_type=F32)


def _log_sigmoid(x):
    return jnp.minimum(x, 0.0) - jnp.log1p(jnp.exp(-jnp.abs(x)))


def _iota(shape, axis):
    return lax.broadcasted_iota(jnp.int32, shape, axis)


ANY = pl.BlockSpec(memory_space=pl.ANY)
N_CHIPS = 4


class _Plan(NamedTuple):
    ins: list
    outs: list
    sems: list
    start: Callable
    finish: Callable


def _place():
    x, y, c = lax.axis_index("x"), lax.axis_index("y"), lax.axis_index("c")
    return x, y, c, [(1 - x, y), (x, 1 - y), (1 - x, 1 - y)]


def _gather_plan(xs):
    n = len(xs)

    def parts(x_refs, out_refs, sems):
        send_sems, recv_sems, local_sems = sems
        x, y, c, chips = _place()
        me, sibling = (x, y, c), (x, y, 1 - c)

        def rows(a, px, py, pc):
            return out_refs[a].at[4 * px + 2 * py + pc]

        def copy(a, k, block, to, src=None):
            return pltpu.make_async_remote_copy(
                src_ref=rows(a, *block) if src is None else src, dst_ref=rows(a, *block),
                send_sem=send_sems.at[7 * a + k], recv_sem=recv_sems.at[7 * a + k], device_id=to, device_id_type=MESH)

        mine = [pltpu.make_async_copy(x_refs[a], rows(a, *me), local_sems.at[a]) for a in range(n)]
        first = []
        for a in range(n):
            first.append(copy(a, 0, me, sibling, src=x_refs[a]))
            first += [copy(a, 1 + j, me, (*chip, c), src=x_refs[a]) for j, chip in enumerate(chips)]
        return c, me, sibling, chips, copy, mine, first

    def start(x_refs, out_refs, sems):
        *_, mine, first = parts(x_refs, out_refs, sems)
        for cp in first + mine:
            cp.start()

    def finish(x_refs, out_refs, sems):
        c, me, sibling, chips, copy, mine, first = parts(x_refs, out_refs, sems)
        passed = []
        for j, chip in enumerate(chips):
            for a in range(n):
                copy(a, 1 + j, (*chip, c), me).wait_recv()
                passed.append(copy(a, 4 + j, (*chip, c), sibling))
                passed[-1].start()
        for a in range(n):
            copy(a, 0, sibling, me).wait_recv()
            for j, chip in enumerate(chips):
                copy(a, 4 + j, (*chip, 1 - c), me).wait_recv()
        for cp in first + passed:
            cp.wait_send()
        for cp in mine:
            cp.wait()

    return _Plan(list(xs), [jax.ShapeDtypeStruct((N_DEV,) + x.shape, x.dtype) for x in xs],
                 [pltpu.SemaphoreType.DMA((7 * n,)), pltpu.SemaphoreType.DMA((7 * n,)), pltpu.SemaphoreType.DMA((n,))],
                 start, finish)


def _exchange_plan(copies_of, ins, outs, per_array):
    n = len(ins)

    def start(in_refs, out_refs, sems):
        for cp in copies_of(in_refs, out_refs, sems):
            cp.start()

    def finish(in_refs, out_refs, sems):
        copies = copies_of(in_refs, out_refs, sems)
        for cp in copies:
            cp.wait_recv()
        for cp in copies:
            cp.wait_send()

    return _Plan(list(ins), outs, [pltpu.SemaphoreType.DMA((per_array * n,)), pltpu.SemaphoreType.DMA((per_array * n,))],
                 start, finish)


def _sibling_plan(gs):
    def copies_of(g_refs, got_refs, sems):
        x, y, c, _ = _place()
        return [pltpu.make_async_remote_copy(
            src_ref=g_refs[a].at[2 * k + (1 - c)], dst_ref=got_refs[a].at[k], send_sem=sems[0].at[N_CHIPS * a + k],
            recv_sem=sems[1].at[N_CHIPS * a + k], device_id=(x, y, 1 - c), device_id_type=MESH)
            for a in range(len(gs)) for k in range(N_CHIPS)]

    return _exchange_plan(copies_of, gs, [jax.ShapeDtypeStruct((N_CHIPS,) + g.shape[1:], g.dtype) for g in gs], N_CHIPS)


def _chip_plan(ss):
    def copies_of(s_refs, out_refs, sems):
        x, y, c, chips = _place()
        return [pltpu.make_async_remote_copy(
            src_ref=s_refs[a].at[2 * px + py], dst_ref=out_refs[a].at[j], send_sem=sems[0].at[3 * a + j],
            recv_sem=sems[1].at[3 * a + j], device_id=(px, py, c), device_id_type=MESH)
            for a in range(len(ss)) for j, (px, py) in enumerate(chips)]

    return _exchange_plan(copies_of, ss, [jax.ShapeDtypeStruct((3,) + s.shape[1:], s.dtype) for s in ss], 3)


def _run_plan(plan, name):
    n_in, n_out = len(plan.ins), len(plan.outs)

    def body(*refs):
        args = refs[:n_in], refs[n_in:n_in + n_out], refs[n_in + n_out:]
        plan.start(*args)
        plan.finish(*args)

    return pl.pallas_call(body, out_shape=plan.outs, in_specs=[ANY] * n_in, out_specs=[ANY] * n_out,
                          scratch_shapes=plan.sems, name=name)(*plan.ins)


def _pcall(body, ride, *, grid, in_specs, out_specs, out_shape, scratch_shapes=(), semantics, name):
    if ride is None:
        return pl.pallas_call(body, grid=grid, in_specs=in_specs, out_specs=out_specs, out_shape=out_shape,
                              scratch_shapes=list(scratch_shapes), compiler_params=_params(*semantics), name=name)
    single = not isinstance(out_shape, (list, tuple))
    out_specs_l, out_shape_l = ([out_specs], [out_shape]) if single else (list(out_specs), list(out_shape))
    n_in, n_out, n_scr = len(in_specs), len(out_shape_l), len(scratch_shapes)
    r_in, r_out = len(ride.ins), len(ride.outs)

    def riding(*refs):
        cuts = [n_in, r_in, n_out, r_out, n_scr]
        groups, at = [], 0
        for width in cuts:
            groups.append(refs[at:at + width])
            at += width
        ins, r_ins, outs, r_outs, scr = groups
        sems = refs[at:]
        first = functools.reduce(jnp.logical_and, [pl.program_id(d) == 0 for d in range(len(grid))])
        last = functools.reduce(jnp.logical_and, [pl.program_id(d) == grid[d] - 1 for d in range(len(grid))])

        @pl.when(first)
        def _():
            ride.start(r_ins, r_outs, sems)

        body(*ins, *outs, *scr)

        @pl.when(last)
        def _():
            ride.finish(r_ins, r_outs, sems)

    call = pl.pallas_call(
        riding, grid=grid, in_specs=list(in_specs) + [ANY] * r_in, out_specs=out_specs_l + [ANY] * r_out,
        out_shape=out_shape_l + list(ride.outs), scratch_shapes=list(scratch_shapes) + list(ride.sems),
        compiler_params=_params(*(["arbitrary"] * len(grid))), name=name)

    def run(*args):
        res = call(*args, *ride.ins)
        return (res[0] if single else list(res[:n_out])), list(res[n_out:])

    return run


def _mm(a, b, *, nt=False, ta=False, out_dtype, tm, tn, a_sqrelu=False, drelu_of=None, b_blocked=False,
        out_blocked=False, name, ride=None):
    k, m = a.shape if ta else a.shape[::-1]
    if b_blocked:
        assert not nt and b.shape[1] == k and b.shape[2] == tn
        n = b.shape[0] * tn
    else:
        n = b.shape[0] if nt else b.shape[1]
        assert (b.shape[1] if nt else b.shape[0]) == k
    tm, tn = min(tm, m), min(tn, n)
    assert m % tm == 0 and n % tn == 0

    def body(*refs):
        a_ref, b_ref = refs[0], refs[1]
        o_ref = refs[-1]
        av = a_ref[...]
        if a_sqrelu:
            av = jnp.square(jnp.maximum(av.astype(F32), 0.0))
        acc = _dot(av, b_ref[...], 0 if ta else 1, 1 if nt else 0)
        if drelu_of is not None:
            acc = acc * (2.0 * jnp.maximum(refs[2][...].astype(F32), 0.0))
        o_ref[...] = acc.astype(out_dtype)

    if b_blocked:
        b_spec = pl.BlockSpec((None, k, tn), lambda i, j: (j, 0, 0))
    elif nt:
        b_spec = pl.BlockSpec((tn, k), lambda i, j: (j, 0))
    else:
        b_spec = pl.BlockSpec((k, tn), lambda i, j: (0, j))
    a_spec = pl.BlockSpec((k, tm), lambda i, j: (0, i)) if ta else pl.BlockSpec((tm, k), lambda i, j: (i, 0))
    in_specs = [a_spec, b_spec]
    args = [a, b]
    if drelu_of is not None:
        in_specs.append(pl.BlockSpec((tm, tn), lambda i, j: (i, j)))
        args.append(drelu_of)
    if out_blocked:
        out_spec = pl.BlockSpec((None, tm, tn), lambda i, j: (j, i, 0))
        out_shape = jax.ShapeDtypeStruct((n // tn, m, tn), out_dtype)
    else:
        out_spec = pl.BlockSpec((tm, tn), lambda i, j: (i, j))
        out_shape = jax.ShapeDtypeStruct((m, n), out_dtype)
    return _pcall(body, ride, grid=(m // tm, n // tn), in_specs=in_specs, out_specs=out_spec, out_shape=out_shape,
                  semantics=("parallel", "parallel"), name=name)(*args)


def _mm_nt_blocked(a, b, *, out_dtype, tm, tn, name):
    m = a.shape[0]
    p, n, kp = b.shape
    assert a.shape[1] == p * kp and m % tm == 0 and n % tn == 0

    def body(a_ref, b_ref, o_ref, acc_ref):
        @pl.when(pl.program_id(2) == 0)
        def _():
            acc_ref[...] = jnp.zeros_like(acc_ref)

        acc_ref[...] += _dot(a_ref[...], b_ref[...], 1, 1)

        @pl.when(pl.program_id(2) == p - 1)
        def _():
            o_ref[...] = acc_ref[...].astype(out_dtype)

    return pl.pallas_call(
        body, grid=(m // tm, n // tn, p),
        in_specs=[pl.BlockSpec((tm, kp), lambda i, j, q: (i, q)), pl.BlockSpec((None, tn, kp), lambda i, j, q: (q, j, 0))],
        out_specs=pl.BlockSpec((tm, tn), lambda i, j, q: (i, j)),
        out_shape=jax.ShapeDtypeStruct((m, n), out_dtype),
        scratch_shapes=[pltpu.VMEM((tm, tn), F32)],
        compiler_params=_params("parallel", "parallel", "arbitrary"), name=name)(a, b)


ROW_TILE = 256


def _norm_fwd(x, w, *, out_dtype, res=None, name):
    t, d = x.shape

    def body(*refs):
        x_ref, w_ref, o_ref = refs[0], refs[1], refs[-1]
        xv = x_ref[...]
        y = xv * lax.rsqrt(jnp.mean(xv * xv, axis=-1, keepdims=True) + NORM_EPS) * w_ref[...]
        if res is not None:
            y = refs[2][...] + y
        o_ref[...] = y.astype(out_dtype)

    row = pl.BlockSpec((ROW_TILE, d), lambda i: (i, 0))
    in_specs = [row, pl.BlockSpec((1, d), lambda i: (0, 0))] + ([row] if res is not None else [])
    args = [x, w] + ([res] if res is not None else [])
    return pl.pallas_call(body, grid=(t // ROW_TILE,), in_specs=in_specs, out_specs=row,
                          out_shape=jax.ShapeDtypeStruct((t, d), out_dtype),
                          compiler_params=_params("parallel"), name=name)(*args)


def _norm_bwd(dy, x, w, *, out_dtype, add=None, name):
    t, d = x.shape

    def body(*refs):
        dy_ref, x_ref, w_ref = refs[0], refs[1], refs[2]
        dx_ref, dw_ref = refs[-2], refs[-1]
        xv = x_ref[...]
        rstd = lax.rsqrt(jnp.mean(xv * xv, axis=-1, keepdims=True) + NORM_EPS)
        xhat = xv * rstd
        dyv = dy_ref[...].astype(F32)
        g = dyv * w_ref[...]
        dx = rstd * (g - xhat * jnp.mean(g * xhat, axis=-1, keepdims=True))
        if add is not None:
            dx = dx + refs[3][...]
        dx_ref[...] = dx.astype(out_dtype)

        @pl.when(pl.program_id(0) == 0)
        def _():
            dw_ref[...] = jnp.zeros_like(dw_ref)

        dw_ref[...] += jnp.sum(dyv * xhat, axis=0, keepdims=True)

    row = pl.BlockSpec((ROW_TILE, d), lambda i: (i, 0))
    vec = pl.BlockSpec((1, d), lambda i: (0, 0))
    in_specs = [row, row, vec] + ([row] if add is not None else [])
    args = [dy, x, w] + ([add] if add is not None else [])
    return pl.pallas_call(body, grid=(t // ROW_TILE,), in_specs=in_specs, out_specs=[row, vec],
                          out_shape=[jax.ShapeDtypeStruct((t, d), out_dtype), jax.ShapeDtypeStruct((1, d), F32)],
                          compiler_params=_params("arbitrary"), name=name)(*args)


def _loss_fwd_bwd(y, target):
    t, d = y.shape

    def body(y_ref, t_ref, l_ref, dy_ref):
        diff = y_ref[...] - t_ref[...]
        dy_ref[...] = diff * (1.0 / d)

        @pl.when(pl.program_id(0) == 0)
        def _():
            l_ref[...] = jnp.zeros_like(l_ref)

        l_ref[...] += 0.5 * jnp.sum(jnp.mean(diff * diff, axis=-1, keepdims=True), axis=0, keepdims=True)

    row = pl.BlockSpec((ROW_TILE, d), lambda i: (i, 0))
    return pl.pallas_call(body, grid=(t // ROW_TILE,), in_specs=[row, row],
                          out_specs=[pl.BlockSpec((8, LANES), lambda i: (0, 0)), row],
                          out_shape=[jax.ShapeDtypeStruct((8, LANES), F32), jax.ShapeDtypeStruct((t, d), F32)],
                          compiler_params=_params("arbitrary"), name="loss")(y, target)


def _gla_specs(t):
    return [pl.BlockSpec((t, GLA_KW), lambda i: (0, 0)),
            pl.BlockSpec((t, GLA_KW), lambda i: (0, 1)),
            pl.BlockSpec((t, GROUP_WIDTH), lambda i: (0, 1)),
            pl.BlockSpec((t, GROUP_WIDTH), lambda i: (0, 0)),
            pl.BlockSpec((t, LANES), lambda i: (0, 4)),
            pl.BlockSpec((LANES, GLA_KW), lambda i: (0, 0)),
            pl.BlockSpec((1, GLA_KW), lambda i: (0, 0)),
            pl.BlockSpec((1, GROUP_WIDTH), lambda i: (0, 0))]


def _gla_fwd(pmm, pel, w_up, b_a, gnorm_w, ride=None):
    t = pmm.shape[0]
    nc = t // CHUNK
    scale = GLA_DK ** -0.5

    def body(q_ref, k_ref, v_ref, r_ref, a_ref, wup_ref, ba_ref, gw_ref, o_ref, st_ref, la_scr, s_scr):
        z = _dot(a_ref[...], wup_ref[...]) + ba_ref[...]
        la_scr[...] = _log_sigmoid(z) * (1.0 / GLA_GATE_TAU)
        s_scr[...] = jnp.zeros_like(s_scr)
        tri = (_iota((CHUNK, CHUNK), 1) <= _iota((CHUNK, CHUNK), 0)).astype(F32)

        def chunk(c, carry):
            rows = pl.ds(pl.multiple_of(c * CHUNK, CHUNK), CHUNK)
            cum = _dot_exact(tri, la_scr[rows, :])
            tot = cum[CHUNK - 1:CHUNK, :]
            kd = k_ref[rows, :].astype(F32) * jnp.exp(tot - cum)
            decay = jnp.exp(tot)
            qs = q_ref[rows, :].astype(F32) * scale
            vv = v_ref[rows, :].astype(F32)
            rr = r_ref[rows, :]
            gate = rr * jax.nn.sigmoid(rr) * gw_ref[...]
            for h in range(GLA_HEADS):
                ks = slice(h * GLA_DK, (h + 1) * GLA_DK)
                vs = slice(h * GLA_DV, (h + 1) * GLA_DV)
                inc_t = _dot(vv[:, vs].T, kd[:, ks])
                s_t = s_scr[vs, :] * decay[:, ks] + inc_t
                s_scr[vs, :] = s_t
                st_ref[c, vs, :] = s_t
                o = _dot(qs[:, ks], s_t, 1, 1)
                y = o * lax.rsqrt(jnp.mean(o * o, axis=-1, keepdims=True) + NORM_EPS)
                o_ref[rows, vs] = (y * gate[:, vs]).astype(o_ref.dtype)
            return carry

        lax.fori_loop(0, nc, chunk, 0)

    return _pcall(
        body, ride, grid=(1,), in_specs=_gla_specs(t),
        out_specs=[pl.BlockSpec((t, GROUP_WIDTH), lambda i: (0, 0)),
                   pl.BlockSpec((nc, GLA_HEADS * GLA_DV, GLA_DK), lambda i: (0, 0, 0))],
        out_shape=[jax.ShapeDtypeStruct((t, GROUP_WIDTH), ACT_DTYPE),
                   jax.ShapeDtypeStruct((nc, GLA_HEADS * GLA_DV, GLA_DK), F32)],
        scratch_shapes=[pltpu.VMEM((t, GLA_KW), F32), pltpu.VMEM((GLA_HEADS * GLA_DV, GLA_DK), F32)],
        semantics=("arbitrary",), name="gla_fwd")(pmm, pmm, pmm, pel, pel, w_up, b_a, gnorm_w)


def _gla_bwd(pmm, pel, w_up, b_a, gnorm_w, states, dmix, ride=None):
    t = pmm.shape[0]
    nc = t // CHUNK
    scale = GLA_DK ** -0.5

    def body(q_ref, k_ref, v_ref, r_ref, a_ref, wup_ref, ba_ref, gw_ref, st_ref, do_ref,
             dq_ref, dk_ref, dv_ref, dr_ref, da_ref, dwup_ref, dba_ref, dgw_ref, la_scr, dz_scr, ds_scr):
        z = _dot(a_ref[...], wup_ref[...]) + ba_ref[...]
        la_scr[...] = _log_sigmoid(z) * (1.0 / GLA_GATE_TAU)
        ds_scr[...] = jnp.zeros_like(ds_scr)
        dgw_ref[...] = jnp.zeros_like(dgw_ref)
        row_i, col_i = _iota((CHUNK, CHUNK), 0), _iota((CHUNK, CHUNK), 1)
        tri = (col_i <= row_i).astype(F32)
        tri_strict = (col_i < row_i).astype(F32)

        def chunk(n, carry):
            c = nc - 1 - n
            rows = pl.ds(pl.multiple_of(c * CHUNK, CHUNK), CHUNK)
            cum = _dot_exact(tri, la_scr[rows, :])
            tot = cum[CHUNK - 1:CHUNK, :]
            e = jnp.exp(tot - cum)
            kd = k_ref[rows, :].astype(F32) * e
            decay = jnp.exp(tot)
            qs = q_ref[rows, :].astype(F32) * scale
            vv = v_ref[rows, :].astype(F32)
            rr = r_ref[rows, :]
            sig = jax.nn.sigmoid(rr)
            silu = rr * sig
            dsilu = sig * (1.0 + rr * (1.0 - sig))
            dout = do_ref[rows, :]
            gw = gw_ref[...]
            c_prev = jnp.maximum(c - 1, 0)
            has_prev = (c > 0).astype(F32)
            zc = _dot(a_ref[rows, :], wup_ref[...]) + ba_ref[...]
            dz_scale = jax.nn.sigmoid(-zc) * (1.0 / GLA_GATE_TAU)
            for h in range(GLA_HEADS):
                ks = slice(h * GLA_DK, (h + 1) * GLA_DK)
                vs = slice(h * GLA_DV, (h + 1) * GLA_DV)
                s_t = st_ref[c, vs, :]
                s_prev = st_ref[c_prev, vs, :] * has_prev
                o = _dot(qs[:, ks], s_t, 1, 1)
                rstd = lax.rsqrt(jnp.mean(o * o, axis=-1, keepdims=True) + NORM_EPS)
                y = o * rstd
                dg = dout[:, vs]
                dgw_ref[:, vs] += jnp.sum(dg * y * silu[:, vs], axis=0, keepdims=True)
                dr_ref[rows, vs] = (dg * y * gw[:, vs] * dsilu[:, vs]).astype(dr_ref.dtype)
                dy = dg * gw[:, vs] * silu[:, vs]
                d_o = rstd * (dy - y * jnp.mean(dy * y, axis=-1, keepdims=True))
                dq_ref[rows, ks] = (_dot(d_o, s_t) * scale).astype(dq_ref.dtype)
                ds_t = ds_scr[vs, :] + _dot(d_o.T, qs[:, ks])
                dv_ref[rows, vs] = _dot(kd[:, ks], ds_t, 1, 1).astype(dv_ref.dtype)
                dkd = _dot(vv[:, vs], ds_t)
                ddecay = jnp.sum(ds_t * s_prev, axis=0, keepdims=True)
                ds_scr[vs, :] = ds_t * decay[:, ks]
                dla = ddecay * decay[:, ks] + _dot_exact(tri_strict, dkd * kd[:, ks])
                dz_scr[rows, ks] = dla * dz_scale[:, ks]
                dk_ref[rows, ks] = (dkd * e[:, ks]).astype(dk_ref.dtype)
            return carry

        lax.fori_loop(0, nc, chunk, 0)
        dz = dz_scr[...]
        da_ref[...] = _dot(dz, wup_ref[...], 1, 1).astype(da_ref.dtype)
        dwup_ref[...] = _dot(a_ref[...].T, dz)
        dba_ref[...] = jnp.sum(dz, axis=0, keepdims=True)

    in_specs = _gla_specs(t) + [
        pl.BlockSpec((nc, GLA_HEADS * GLA_DV, GLA_DK), lambda i: (0, 0, 0)),
        pl.BlockSpec((t, GROUP_WIDTH), lambda i: (0, 0))]
    full = lambda r, c: pl.BlockSpec((r, c), lambda i: (0, 0))
    return _pcall(
        body, ride, grid=(1,), in_specs=in_specs,
        out_specs=[full(t, GLA_KW), full(t, GLA_KW), full(t, GROUP_WIDTH), full(t, GROUP_WIDTH), full(t, LANES),
                   full(LANES, GLA_KW), full(1, GLA_KW), full(1, GROUP_WIDTH)],
        out_shape=[jax.ShapeDtypeStruct((t, GLA_KW), ACT_DTYPE), jax.ShapeDtypeStruct((t, GLA_KW), ACT_DTYPE),
                   jax.ShapeDtypeStruct((t, GROUP_WIDTH), ACT_DTYPE), jax.ShapeDtypeStruct((t, GROUP_WIDTH), ACT_DTYPE),
                   jax.ShapeDtypeStruct((t, LANES), ACT_DTYPE), jax.ShapeDtypeStruct((LANES, GLA_KW), F32),
                   jax.ShapeDtypeStruct((1, GLA_KW), F32), jax.ShapeDtypeStruct((1, GROUP_WIDTH), F32)],
        scratch_shapes=[pltpu.VMEM((t, GLA_KW), F32), pltpu.VMEM((t, GLA_KW), F32),
                        pltpu.VMEM((GLA_HEADS * GLA_DV, GLA_DK), F32)],
        semantics=("arbitrary",), name="gla_bwd")(
            pmm, pmm, pmm, pel, pel, w_up, b_a, gnorm_w, states, dmix)


CUM_BLOCK = 256


def _fox_gate_fwd(pel, b_f):
    t = pel.shape[0]

    def body(f_ref, b_ref, cum_ref, cum_t_ref):
        tri = (_iota((CUM_BLOCK, CUM_BLOCK), 1) <= _iota((CUM_BLOCK, CUM_BLOCK), 0)).astype(F32)
        carry = jnp.zeros((1, LANES), F32)
        for blk in range(t // CUM_BLOCK):
            rows = slice(blk * CUM_BLOCK, (blk + 1) * CUM_BLOCK)
            cum = _dot_exact(tri, _log_sigmoid(f_ref[rows, :] + b_ref[...])) + carry
            cum_ref[rows, :] = cum
            carry = cum[CUM_BLOCK - 1:CUM_BLOCK, :]
        cum_t_ref[...] = cum_ref[...].T

    return pl.pallas_call(
        body, grid=(1,),
        in_specs=[pl.BlockSpec((t, LANES), lambda i: (0, 5)), pl.BlockSpec((1, LANES), lambda i: (0, 0))],
        out_specs=[pl.BlockSpec((t, LANES), lambda i: (0, 0)), pl.BlockSpec((LANES, t), lambda i: (0, 0))],
        out_shape=[jax.ShapeDtypeStruct((t, LANES), F32), jax.ShapeDtypeStruct((LANES, t), F32)],
        compiler_params=_params("arbitrary"), name="fox_gate_fwd")(pel, b_f)


def _fox_gate_bwd(pel, b_f, dcum_t):
    t = pel.shape[0]

    def body(f_ref, b_ref, dct_ref, df_ref, db_ref, dc_scr):
        dc_scr[...] = dct_ref[...].T
        tri_up = (_iota((CUM_BLOCK, CUM_BLOCK), 1) >= _iota((CUM_BLOCK, CUM_BLOCK), 0)).astype(F32)
        carry = jnp.zeros((1, LANES), F32)
        db = jnp.zeros((1, LANES), F32)
        for blk in reversed(range(t // CUM_BLOCK)):
            rows = slice(blk * CUM_BLOCK, (blk + 1) * CUM_BLOCK)
            dls = _dot_exact(tri_up, dc_scr[rows, :]) + carry
            carry = dls[0:1, :]
            df = dls * jax.nn.sigmoid(-(f_ref[rows, :] + b_ref[...]))
            df_ref[rows, :] = df.astype(df_ref.dtype)
            db = db + jnp.sum(df, axis=0, keepdims=True)
        db_ref[...] = db

    return pl.pallas_call(
        body, grid=(1,),
        in_specs=[pl.BlockSpec((t, LANES), lambda i: (0, 5)), pl.BlockSpec((1, LANES), lambda i: (0, 0)),
                  pl.BlockSpec((LANES, t), lambda i: (0, 0))],
        out_specs=[pl.BlockSpec((t, LANES), lambda i: (0, 0)), pl.BlockSpec((1, LANES), lambda i: (0, 0))],
        out_shape=[jax.ShapeDtypeStruct((t, LANES), ACT_DTYPE), jax.ShapeDtypeStruct((1, LANES), F32)],
        scratch_shapes=[pltpu.VMEM((t, LANES), F32)],
        compiler_params=_params("arbitrary"), name="fox_gate_bwd")(pel, b_f, dcum_t)


FOX_Q_BLOCK = 256


FOX_KEY_STEP = 512


def _fox_by_extent(i, t, run):
    per = FOX_KEY_STEP // FOX_Q_BLOCK
    for kx in range(t // FOX_KEY_STEP):
        pl.when(i // per == kx)(functools.partial(run, (kx + 1) * FOX_KEY_STEP))


def _fox_scores(q_ref, k_ref, cum_ref, cum_t_ref, h, i, n):
    hs = slice(h * HEAD_DIM, (h + 1) * HEAD_DIM)
    s = _dot(q_ref[:, hs], k_ref[:n, hs], 1, 1) * (HEAD_DIM ** -0.5)
    s = s + (cum_ref[:, h:h + 1] - cum_t_ref[h:h + 1, :n])
    mask = _iota((FOX_Q_BLOCK, n), 1) <= i * FOX_Q_BLOCK + _iota((FOX_Q_BLOCK, n), 0)
    return jnp.where(mask, s, NEG)


def _fox_fwd(pmm, cum, cum_t, ride=None):
    t = pmm.shape[0]
    bq = FOX_Q_BLOCK

    def body(q_ref, k_ref, v_ref, cum_ref, cum_t_ref, o_ref, lse_ref):
        i = pl.program_id(0)
        lse_ref[...] = jnp.zeros_like(lse_ref)

        def run(n):
            for h in range(ATT_HEADS):
                hs = slice(h * HEAD_DIM, (h + 1) * HEAD_DIM)
                s = _fox_scores(q_ref, k_ref, cum_ref, cum_t_ref, h, i, n)
                m = jnp.max(s, axis=-1, keepdims=True)
                p = jnp.exp(s - m)
                l = jnp.sum(p, axis=-1, keepdims=True)
                o_ref[:, hs] = (_dot(p, v_ref[:n, hs]) / l).astype(o_ref.dtype)
                lse_ref[:, h:h + 1] = m + jnp.log(l)

        _fox_by_extent(i, t, run)

    return _pcall(
        body, ride, grid=(t // bq,),
        in_specs=[pl.BlockSpec((bq, GROUP_WIDTH), lambda i: (i, 2)), pl.BlockSpec((t, GROUP_WIDTH), lambda i: (0, 3)),
                  pl.BlockSpec((t, GROUP_WIDTH), lambda i: (0, 4)), pl.BlockSpec((bq, LANES), lambda i: (i, 0)),
                  pl.BlockSpec((8, t), lambda i: (0, 0))],
        out_specs=[pl.BlockSpec((bq, GROUP_WIDTH), lambda i: (i, 0)), pl.BlockSpec((bq, LANES), lambda i: (i, 0))],
        out_shape=[jax.ShapeDtypeStruct((t, GROUP_WIDTH), ACT_DTYPE), jax.ShapeDtypeStruct((t, LANES), F32)],
        semantics=("parallel",), name="fox_fwd")(pmm, pmm, pmm, cum, cum_t)


def _fox_bwd(pmm, cum, cum_t, lse, dmix, ride=None):
    t = pmm.shape[0]
    bq = FOX_Q_BLOCK
    scale = HEAD_DIM ** -0.5

    def body(q_ref, k_ref, v_ref, cum_ref, cum_t_ref, lse_ref, do_ref, dq_ref, dk_ref, dv_ref, dct_ref):
        i = pl.program_id(0)

        @pl.when(i == 0)
        def _():
            dk_ref[...] = jnp.zeros_like(dk_ref)
            dv_ref[...] = jnp.zeros_like(dv_ref)
            dct_ref[...] = jnp.zeros_like(dct_ref)

        def run(n):
            for h in range(ATT_HEADS):
                hs = slice(h * HEAD_DIM, (h + 1) * HEAD_DIM)
                s = _fox_scores(q_ref, k_ref, cum_ref, cum_t_ref, h, i, n)
                p = jnp.exp(s - lse_ref[:, h:h + 1])
                do = do_ref[:, hs]
                dp = _dot(do, v_ref[:n, hs], 1, 1)
                ds = p * (dp - jnp.sum(p * dp, axis=-1, keepdims=True))
                dq_ref[:, hs] = (_dot(ds, k_ref[:n, hs]) * scale).astype(dq_ref.dtype)
                dk_ref[:n, hs] += _dot(ds, q_ref[:, hs], 0, 0) * scale
                dv_ref[:n, hs] += _dot(p, do, 0, 0)
                dct_ref[h:h + 1, :n] += -jnp.sum(ds, axis=0, keepdims=True)

        _fox_by_extent(i, t, run)

    whole = pl.BlockSpec((t, GROUP_WIDTH), lambda i: (0, 0))
    return _pcall(
        body, ride, grid=(t // bq,),
        in_specs=[pl.BlockSpec((bq, GROUP_WIDTH), lambda i: (i, 2)), pl.BlockSpec((t, GROUP_WIDTH), lambda i: (0, 3)),
                  pl.BlockSpec((t, GROUP_WIDTH), lambda i: (0, 4)), pl.BlockSpec((bq, LANES), lambda i: (i, 0)),
                  pl.BlockSpec((8, t), lambda i: (0, 0)), pl.BlockSpec((bq, LANES), lambda i: (i, 0)),
                  pl.BlockSpec((bq, GROUP_WIDTH), lambda i: (i, 1))],
        out_specs=[pl.BlockSpec((bq, GROUP_WIDTH), lambda i: (i, 0)), whole, whole,
                   pl.BlockSpec((LANES, t), lambda i: (0, 0))],
        out_shape=[jax.ShapeDtypeStruct((t, GROUP_WIDTH), ACT_DTYPE), jax.ShapeDtypeStruct((t, GROUP_WIDTH), F32),
                   jax.ShapeDtypeStruct((t, GROUP_WIDTH), F32), jax.ShapeDtypeStruct((LANES, t), F32)],
        semantics=("arbitrary",), name="fox_bwd")(pmm, pmm, pmm, cum, cum_t, lse, dmix)


CA_Q_BLOCK = 4 * CHUNK
CA_WINDOW = CA_Q_BLOCK + CA_LEFT
CA_BASE = 1024


def _ca_bias_base(rel_bias):
    n = rel_bias.shape[0]
    flat = CA_Q_BLOCK + CA_LEFT - REL_CLIP
    tail = CA_BASE - flat - (2 * REL_CLIP + 1)
    return jnp.concatenate([jnp.broadcast_to(rel_bias[:, 2 * REL_CLIP:], (n, flat)), rel_bias[:, ::-1],
                            jnp.broadcast_to(rel_bias[:, :1], (n, tail))], axis=1)


def _ca_bias_base_grad(dbase):
    flat = CA_Q_BLOCK + CA_LEFT - REL_CLIP
    mid = dbase[:, flat:flat + 2 * REL_CLIP + 1][:, ::-1]
    lo = jnp.sum(dbase[:, flat + 2 * REL_CLIP + 1:], axis=1, keepdims=True)
    hi = jnp.sum(dbase[:, :flat], axis=1, keepdims=True)
    pad = jnp.zeros((dbase.shape[0], 2 * REL_CLIP - 1), F32)
    return mid + jnp.concatenate([lo, pad, hi], axis=1)


def _ca_mask(i):
    r, j = _iota((CA_Q_BLOCK, CA_WINDOW), 0), _iota((CA_Q_BLOCK, CA_WINDOW), 1)
    rc, jc = r // CHUNK, j // CHUNK
    return (jc >= rc) & (jc <= rc + CA_LEFT // CHUNK) & (i * CA_Q_BLOCK + j >= CA_LEFT)


def _ca_scores(q_ref, kp_ref, base_ref, win, h, mask):
    hs = slice(h * HEAD_DIM, (h + 1) * HEAD_DIM)
    s = _dot(q_ref[:, hs], kp_ref[win, hs], 1, 1) * (HEAD_DIM ** -0.5)
    rows = jnp.broadcast_to(base_ref[h:h + 1, :], (CA_Q_BLOCK, CA_BASE))
    bias = pltpu.roll(rows, CA_BASE - CA_Q_BLOCK, 1, stride=1, stride_axis=0)[:, :CA_WINDOW]
    return jnp.where(mask, s + bias, NEG)


def _ca_fwd(pmm, kp, vp, base, ride=None):
    t = pmm.shape[0]

    def body(q_ref, kp_ref, vp_ref, base_ref, o_ref, lse_ref):
        i = pl.program_id(0)
        win = pl.ds(pl.multiple_of(i * CA_Q_BLOCK, CA_Q_BLOCK), CA_WINDOW)
        mask = _ca_mask(i)
        lse_ref[...] = jnp.zeros_like(lse_ref)
        for h in range(ATT_HEADS):
            hs = slice(h * HEAD_DIM, (h + 1) * HEAD_DIM)
            s = _ca_scores(q_ref, kp_ref, base_ref, win, h, mask)
            m = jnp.max(s, axis=-1, keepdims=True)
            p = jnp.exp(s - m)
            l = jnp.sum(p, axis=-1, keepdims=True)
            o_ref[:, hs] = (_dot(p, vp_ref[win, hs]) / l).astype(o_ref.dtype)
            lse_ref[:, h:h + 1] = m + jnp.log(l)

    padded = pl.BlockSpec((t + CA_LEFT, GROUP_WIDTH), lambda i: (0, 0))
    return _pcall(
        body, ride, grid=(t // CA_Q_BLOCK,),
        in_specs=[pl.BlockSpec((CA_Q_BLOCK, GROUP_WIDTH), lambda i: (i, 0)), padded, padded,
                  pl.BlockSpec((ATT_HEADS, CA_BASE), lambda i: (0, 0))],
        out_specs=[pl.BlockSpec((CA_Q_BLOCK, GROUP_WIDTH), lambda i: (i, 0)),
                   pl.BlockSpec((CA_Q_BLOCK, LANES), lambda i: (i, 0))],
        out_shape=[jax.ShapeDtypeStruct((t, GROUP_WIDTH), ACT_DTYPE), jax.ShapeDtypeStruct((t, LANES), F32)],
        semantics=("parallel",), name="ca_fwd")(pmm, kp, vp, base)


def _ca_bwd(pmm, kp, vp, base, lse, dmix, ride=None):
    t = pmm.shape[0]
    scale = HEAD_DIM ** -0.5

    def body(q_ref, kp_ref, vp_ref, base_ref, lse_ref, do_ref, dq_ref, dkp_ref, dvp_ref, dbase_ref):
        i = pl.program_id(0)

        @pl.when(i == 0)
        def _():
            dkp_ref[...] = jnp.zeros_like(dkp_ref)
            dvp_ref[...] = jnp.zeros_like(dvp_ref)
            dbase_ref[...] = jnp.zeros_like(dbase_ref)

        win = pl.ds(pl.multiple_of(i * CA_Q_BLOCK, CA_Q_BLOCK), CA_WINDOW)
        mask = _ca_mask(i)
        flip = (_iota((CA_Q_BLOCK, CA_Q_BLOCK), 0) + _iota((CA_Q_BLOCK, CA_Q_BLOCK), 1) == CA_Q_BLOCK - 1).astype(F32)
        for h in range(ATT_HEADS):
            hs = slice(h * HEAD_DIM, (h + 1) * HEAD_DIM)
            s = _ca_scores(q_ref, kp_ref, base_ref, win, h, mask)
            p = jnp.exp(s - lse_ref[:, h:h + 1])
            do = do_ref[:, hs]
            dp = _dot(do, vp_ref[win, hs], 1, 1)
            ds = p * (dp - jnp.sum(p * dp, axis=-1, keepdims=True))
            dq_ref[:, hs] = (_dot(ds, kp_ref[win, hs]) * scale).astype(dq_ref.dtype)
            dkp_ref[win, hs] += _dot(ds, q_ref[:, hs], 0, 0) * scale
            dvp_ref[win, hs] += _dot(p, do, 0, 0)
            rev = jnp.concatenate([_dot(flip, ds), jnp.zeros((CA_Q_BLOCK, CA_BASE - CA_WINDOW), F32)], axis=1)
            lined = pltpu.roll(rev, 1, 1, stride=1, stride_axis=0)
            dbase_ref[h:h + 1, :] += jnp.sum(lined, axis=0, keepdims=True)

    padded = pl.BlockSpec((t + CA_LEFT, GROUP_WIDTH), lambda i: (0, 0))
    return _pcall(
        body, ride, grid=(t // CA_Q_BLOCK,),
        in_specs=[pl.BlockSpec((CA_Q_BLOCK, GROUP_WIDTH), lambda i: (i, 0)), padded, padded,
                  pl.BlockSpec((ATT_HEADS, CA_BASE), lambda i: (0, 0)),
                  pl.BlockSpec((CA_Q_BLOCK, LANES), lambda i: (i, 0)),
                  pl.BlockSpec((CA_Q_BLOCK, GROUP_WIDTH), lambda i: (i, 0))],
        out_specs=[pl.BlockSpec((CA_Q_BLOCK, GROUP_WIDTH), lambda i: (i, 0)), padded, padded,
                   pl.BlockSpec((ATT_HEADS, CA_BASE), lambda i: (0, 0))],
        out_shape=[jax.ShapeDtypeStruct((t, GROUP_WIDTH), ACT_DTYPE),
                   jax.ShapeDtypeStruct((t + CA_LEFT, GROUP_WIDTH), F32),
                   jax.ShapeDtypeStruct((t + CA_LEFT, GROUP_WIDTH), F32),
                   jax.ShapeDtypeStruct((ATT_HEADS, CA_BASE), F32)],
        semantics=("arbitrary",), name="ca_bwd")(pmm, kp, vp, base, lse, dmix)


GELU_C = 0.7978845608028654
GELU_A = 0.044715


def _shift_down(v, k, fill):
    return jnp.where(_iota(v.shape, 0) >= k, pltpu.roll(v, k, 0), fill)


def _shift_up(v, k, fill):
    t = v.shape[0]
    return jnp.where(_iota(v.shape, 0) < t - k, pltpu.roll(v, t - k, 0), fill)


def _linear_scan(a, b, shift):
    k = 1
    while k < a.shape[0]:
        b = a * shift(b, k, 0.0) + b
        a = a * shift(a, k, 1.0)
        k *= 2
    return b


def _neg_expm1(y):
    series = -y * (1.0 + y * (0.5 + y * (1.0 / 6.0 + y * (1.0 / 24.0 + y * (1.0 / 120.0)))))
    return jnp.where(y > -0.1, series, 1.0 - jnp.exp(y))


def _lru_forward(x, g_in, cw, cb, wa, ba, wx, bx, lam):
    xs = [_shift_down(x, CONV_WIDTH - 1 - j, 0.0) for j in range(CONV_WIDTH - 1)] + [x]
    xc = cb + sum(cw[j:j + 1, :] * xs[j] for j in range(CONV_WIDTH))
    r = jax.nn.sigmoid(_dot(xc, wa) + ba)
    i = jax.nn.sigmoid(_dot(xc, wx) + bx)
    lsl = _log_sigmoid(lam)
    la = LRU_C * r * lsl
    a = jnp.exp(la)
    s = jnp.sqrt(_neg_expm1(2.0 * la))
    h = _linear_scan(a, s * (i * xc), _shift_down)
    u = GELU_C * (g_in + GELU_A * g_in * g_in * g_in)
    th = jnp.tanh(u)
    gelu = 0.5 * g_in * (1.0 + th)
    return xs, xc, r, i, lsl, a, s, h, th, gelu


def _lru_specs(t):
    col = lambda off: pl.BlockSpec((t, LANES), lambda j: (0, j + off))
    vec = pl.BlockSpec((1, LANES), lambda j: (0, j))
    mat = pl.BlockSpec((None, LANES, LANES), lambda j: (j, 0, 0))
    return [col(0), col(GROUP_WIDTH // LANES), pl.BlockSpec((CONV_WIDTH, LANES), lambda j: (0, j)),
            vec, mat, vec, mat, vec, vec]


def _lru_fwd(pel, conv_w, conv_b, wa, ba, wx, bx, lam, ride=None):
    t = pel.shape[0]

    def body(g_ref, x_ref, cw_ref, cb_ref, wa_ref, ba_ref, wx_ref, bx_ref, lam_ref, o_ref):
        res = _lru_forward(x_ref[...], g_ref[...], cw_ref[...], cb_ref[...], wa_ref[...], ba_ref[...],
                           wx_ref[...], bx_ref[...], lam_ref[...])
        o_ref[...] = (res[7] * res[9]).astype(o_ref.dtype)

    return _pcall(
        body, ride, grid=(GROUP_WIDTH // LANES,), in_specs=_lru_specs(t),
        out_specs=pl.BlockSpec((t, LANES), lambda j: (0, j)),
        out_shape=jax.ShapeDtypeStruct((t, GROUP_WIDTH), ACT_DTYPE),
        semantics=("parallel",), name="lru_fwd")(pel, pel, conv_w, conv_b, wa, ba, wx, bx, lam)


def _lru_bwd(pel, conv_w, conv_b, wa, ba, wx, bx, lam, dmix, ride=None):
    t = pel.shape[0]

    def body(g_ref, x_ref, cw_ref, cb_ref, wa_ref, ba_ref, wx_ref, bx_ref, lam_ref, do_ref,
             dg_ref, dx_ref, dcw_ref, dcb_ref, dwa_ref, dba_ref, dwx_ref, dbx_ref, dlam_ref):
        g_in, cw, lam = g_ref[...], cw_ref[...], lam_ref[...]
        xs, xc, r, i, lsl, a, s, h, th, gelu = _lru_forward(
            x_ref[...], g_in, cw, cb_ref[...], wa_ref[...], ba_ref[...], wx_ref[...], bx_ref[...], lam)
        dout = do_ref[...]
        dgelu = 0.5 * (1.0 + th) + 0.5 * g_in * (1.0 - th * th) * GELU_C * (1.0 + 3.0 * GELU_A * g_in * g_in)
        dg_ref[...] = (dout * h * dgelu).astype(dg_ref.dtype)
        gsum = _linear_scan(_shift_up(a, 1, 0.0), dout * gelu, _shift_up)
        da = gsum * _shift_down(h, 1, 0.0)
        di = gsum * s * xc
        dla = da * a - gsum * (i * xc) * (a * a / s)
        dlam_ref[...] = jnp.sum(dla * (LRU_C * r), axis=0, keepdims=True) * jax.nn.sigmoid(-lam)
        dpr = dla * (LRU_C * lsl) * r * (1.0 - r)
        dpi = di * i * (1.0 - i)
        dxc = gsum * s * i + _dot(dpr, wa_ref[...], 1, 1) + _dot(dpi, wx_ref[...], 1, 1)
        xct = xc.T
        dwa_ref[...] = _dot(xct, dpr)
        dwx_ref[...] = _dot(xct, dpi)
        dba_ref[...] = jnp.sum(dpr, axis=0, keepdims=True)
        dbx_ref[...] = jnp.sum(dpi, axis=0, keepdims=True)
        dcb_ref[...] = jnp.sum(dxc, axis=0, keepdims=True)
        for j in range(CONV_WIDTH):
            dcw_ref[j:j + 1, :] = jnp.sum(dxc * xs[j], axis=0, keepdims=True)
        dx = cw[CONV_WIDTH - 1:CONV_WIDTH, :] * dxc
        for j in range(CONV_WIDTH - 1):
            dx = dx + cw[j:j + 1, :] * _shift_up(dxc, CONV_WIDTH - 1 - j, 0.0)
        dx_ref[...] = dx.astype(dx_ref.dtype)

    col = pl.BlockSpec((t, LANES), lambda j: (0, j))
    vec = pl.BlockSpec((1, LANES), lambda j: (0, j))
    mat = pl.BlockSpec((None, LANES, LANES), lambda j: (j, 0, 0))
    nb = GROUP_WIDTH // LANES
    vshape = jax.ShapeDtypeStruct((1, GROUP_WIDTH), F32)
    mshape = jax.ShapeDtypeStruct((nb, LANES, LANES), F32)
    return _pcall(
        body, ride, grid=(nb,),
        in_specs=_lru_specs(t) + [pl.BlockSpec((t, LANES), lambda j: (0, j + nb))],
        out_specs=[col, col, pl.BlockSpec((CONV_WIDTH, LANES), lambda j: (0, j)), vec, mat, vec, mat, vec, vec],
        out_shape=[jax.ShapeDtypeStruct((t, GROUP_WIDTH), ACT_DTYPE), jax.ShapeDtypeStruct((t, GROUP_WIDTH), ACT_DTYPE),
                   jax.ShapeDtypeStruct((CONV_WIDTH, GROUP_WIDTH), F32), vshape, mshape, vshape, mshape, vshape, vshape],
        semantics=("parallel",), name="lru_bwd")(
            pel, pel, conv_w, conv_b, wa, ba, wx, bx, lam, dmix)


def _block_diag_pairs(w):
    z = jnp.zeros((LRU_BLOCK_DIM, LRU_BLOCK_DIM), w.dtype)
    return jnp.stack([jnp.block([[w[2 * j], z], [z, w[2 * j + 1]]]) for j in range(w.shape[0] // 2)])


def _block_diag_pairs_grad(dw):
    b = LRU_BLOCK_DIM
    return jnp.stack([dw[n // 2, (n % 2) * b:(n % 2 + 1) * b, (n % 2) * b:(n % 2 + 1) * b] for n in range(2 * dw.shape[0])])


def _row_tile(r):
    return ROW_TILE if r % ROW_TILE == 0 else r


def _pair_sum(g, got, place, name):
    _, r, c = g.shape
    tile = _row_tile(r)

    def body(place_ref, a_ref, b_ref, o_ref):
        o_ref[...] = (a_ref[...].astype(F32) + b_ref[...].astype(F32)).astype(o_ref.dtype)

    blk = pl.BlockSpec((1, tile, c), lambda k, i, place_ref: (k, i, 0))
    return pl.pallas_call(
        body,
        grid_spec=pltpu.PrefetchScalarGridSpec(
            num_scalar_prefetch=1, grid=(N_CHIPS, r // tile),
            in_specs=[pl.BlockSpec((1, tile, c), lambda k, i, place_ref: (2 * k + place_ref[0], i, 0)), blk],
            out_specs=blk),
        out_shape=jax.ShapeDtypeStruct(got.shape, got.dtype),
        compiler_params=_params("parallel", "parallel"), name=name)(place, g, got)


def _adamw_update(g, w_ref, m_ref, v_ref, g_ref, d_ref, nm_ref, nv_ref):
    nm = ADAM_B1 * m_ref[...] + (1.0 - ADAM_B1) * g
    nv = ADAM_B2 * v_ref[...] + (1.0 - ADAM_B2) * jnp.square(g)
    m_hat = nm / (1.0 - ADAM_B1 ** ADAM_STEP)
    v_hat = nv / (1.0 - ADAM_B2 ** ADAM_STEP)
    g_ref[...] = g
    d_ref[...] = -ADAM_LR * (m_hat / (jnp.sqrt(v_hat) + ADAM_EPS) + ADAM_WD * w_ref[...])
    nm_ref[...] = nm
    nv_ref[...] = nv


def _adamw_sharded(parts, w, m, v, place, name):
    n_layers, r, c = w.shape
    tile = _row_tile(r)
    nb = r // tile

    def body(place_ref, *refs):
        layer = pl.program_id(0)
        g = None
        for l in range(n_layers):
            s_ref, r_ref = refs[2 * l], refs[2 * l + 1]
            g_l = s_ref[0].astype(F32) + r_ref[0].astype(F32) + r_ref[1].astype(F32) + r_ref[2].astype(F32)
            g = g_l if g is None else jnp.where(layer == l, g_l, g)
        _adamw_update(g, *refs[2 * n_layers:])

    def part_specs(l):
        rows = lambda q, i: jnp.where(q < l, 0, jnp.where(q > l, nb - 1, i))
        return [pl.BlockSpec((1, tile, c), lambda q, i, place_ref: (place_ref[1], rows(q, i), 0)),
                pl.BlockSpec((3, tile, c), lambda q, i, place_ref: (0, rows(q, i), 0))]

    in_specs, args = [], []
    for l, (s, recv) in enumerate(parts):
        in_specs += part_specs(l)
        args += [s, recv]
    blk = pl.BlockSpec((None, tile, c), lambda q, i, place_ref: (q, i, 0))
    out = jax.ShapeDtypeStruct((n_layers, r, c), F32)
    return pl.pallas_call(
        body,
        grid_spec=pltpu.PrefetchScalarGridSpec(
            num_scalar_prefetch=1, grid=(n_layers, nb), in_specs=in_specs + [blk, blk, blk],
            out_specs=[blk, blk, blk, blk]),
        out_shape=[out, out, out, out], compiler_params=_params("arbitrary", "arbitrary"), name=name)(
            place, *args, w, m, v)


def _adamw_replicated(parts, w, m, v, name):
    p, r, c = parts.shape
    tile = _row_tile(r)

    def body(p_ref, w_ref, m_ref, v_ref, *outs):
        g = p_ref[0].astype(F32)
        for k in range(1, p):
            g = g + p_ref[k].astype(F32)
        _adamw_update(g, w_ref, m_ref, v_ref, *outs)

    blk = pl.BlockSpec((tile, c), lambda i: (i, 0))
    out = jax.ShapeDtypeStruct((r, c), F32)
    return pl.pallas_call(body, grid=(r // tile,),
                          in_specs=[pl.BlockSpec((p, tile, c), lambda i: (0, i, 0)), blk, blk, blk],
                          out_specs=[blk, blk, blk, blk], out_shape=[out, out, out, out],
                          compiler_params=_params("parallel"), name=name)(parts, w, m, v)


SLAB_COLS = 1024
SHARDED = {"norm_w": 2, "w_in_even": 2, "gla_w_a_up": 2, "w_out_even": 1, "w_in_odd": 2, "conv_w": 2, "conv_b": 1,
           "lru_b_a": 1, "lru_b_x": 1, "lru_lambda": 1, "w_out_odd": 1, "w_mlp_up": 2, "w_mlp_down": 1}
REPLICATED = ["gla_b_a", "gla_norm_w", "fox_b_f", "rel_bias", "lru_w_a", "lru_w_x"]
WEIGHTS = ["norm_w", "w_in_even", "gla_w_a_up", "gla_b_a", "gla_norm_w", "fox_b_f", "w_out_even", "w_in_odd",
           "rel_bias", "conv_w", "conv_b", "lru_w_a", "lru_b_a", "lru_w_x", "lru_b_x", "lru_lambda", "w_out_odd",
           "w_mlp_up", "w_mlp_down"]
MATRICES = ("w_in_even", "w_out_even", "w_in_odd", "w_out_odd", "w_mlp_up", "w_mlp_down")
VECTORS = tuple(n for n in SHARDED if n not in MATRICES)
VEC_SLAB_ROWS = 16
REPL_SLAB_ROWS = 72
MATRIX_BLOCKS = (("w_in_even", 0), ("w_out_even", 0), ("w_in_odd", 0), ("w_out_odd", 0),
                 ("w_mlp_up", 0), ("w_mlp_up", 1), ("w_mlp_down", 0), ("w_mlp_down", 1))


def _rows_of(shape):
    n = 1
    for s in shape:
        n *= s
    return -(-n // SLAB_COLS), n


def _pack(arrays, total_rows, lead=()):
    parts, used = [], 0
    for a in arrays:
        rows, n = _rows_of(a.shape[len(lead):])
        flat = a.reshape(lead + (n,))
        flat = jnp.pad(flat, [(0, 0)] * len(lead) + [(0, rows * SLAB_COLS - n)])
        parts.append(flat.reshape(lead + (rows, SLAB_COLS)))
        used += rows
    parts.append(jnp.zeros(lead + (total_rows - used, SLAB_COLS), parts[0].dtype))
    return jnp.concatenate(parts, axis=len(lead))


def _unpack(slab, shapes, lead=()):
    out, row = [], 0
    for shape in shapes:
        rows, n = _rows_of(shape)
        seg = lax.slice_in_dim(slab, row, row + rows, axis=len(lead))
        out.append(seg.reshape(lead + (rows * SLAB_COLS,))[..., :n].reshape(lead + tuple(shape)))
        row += rows
    return out


def _join_shards(blocks, axis):
    moved = jnp.moveaxis(blocks, 0, axis)
    shape = moved.shape
    return moved.reshape(shape[:axis] + (shape[axis] * shape[axis + 1],) + shape[axis + 2:])


def _split_shards(full, axis):
    shape = full.shape
    cut = full.reshape(shape[:axis] + (N_DEV, shape[axis] // N_DEV) + shape[axis + 1:])
    return jnp.moveaxis(cut, axis, 0)


EVEN_SPLITS = (0, 256, 512, 1024, 1536, 1552, 2064, 2576, 3088, 3096)


def _even_in_split(w):
    c = [w[:, EVEN_SPLITS[k]:EVEN_SPLITS[k + 1]] for k in range(9)]
    gq, gk, gv, gr, ga, fq, fk, fv, ff = c
    padcols = lambda a: jnp.pad(a, ((0, 0), (0, LANES - a.shape[1])))
    return jnp.concatenate([gq, gk, gv, fq, fk, fv], axis=1), jnp.concatenate([gr, padcols(ga), padcols(ff)], axis=1)


def _even_in_merge(dmm, dele):
    return jnp.concatenate([dmm[:, :1024], dele[:, :512], dele[:, 512:512 + GLA_RANK], dmm[:, 1024:2560],
                            dele[:, 640:640 + ATT_HEADS]], axis=1)


def _column_shards(full):
    r, c = full.shape
    return jnp.moveaxis(full.reshape(r, N_DEV, c // N_DEV), 1, 0)


def _forward_backward(x, target, shard, vec_slab, vec_shapes, w, place):
    w = dict(w)
    g, dnorm, sums, recv = {}, {}, {}, {}
    nrm = lambda l, k: w["norm_w"][l, k][None, :]
    gather = lambda *keys: _gather_plan([shard[k] for k in keys])
    blocks = lambda r, c: (N_DEV, r // N_DEV, c)

    def pair_sum(key):
        sums[key] = _pair_sum(g[key], got[key], place, f"rs_pair_sum_{key[0]}_{key[1]}")

    got = {}

    def mlp_fwd(xin, layer, ride_up, ride_down):
        h = _norm_fwd(xin, nrm(layer, 2), out_dtype=ACT_DTYPE, name=f"norm_mlp_{layer}")
        u = _mm(h, w["w_mlp_up"][layer], out_dtype=ACT_DTYPE, tm=1024, tn=D_FF // N_DEV, b_blocked=True,
                name=f"mlp_up_{layer}", ride=ride_up)
        u, rode_up = u if ride_up is not None else (u, None)
        yv = _mm(u, w["w_mlp_down"][layer], out_dtype=F32, tm=512, tn=512, a_sqrelu=True,
                 name=f"mlp_down_{layer}", ride=ride_down)
        yv, rode_down = yv if ride_down is not None else (yv, None)
        xout = _norm_fwd(yv, nrm(layer, 3), out_dtype=F32, res=xin, name=f"norm_mlp_out_{layer}")
        return xout, (xin, h, u, yv), rode_up, rode_down

    def mlp_bwd(dxout, saved, layer, ride):
        xin, h, u, yv = saved
        k_up, k_down = ("w_mlp_up", layer), ("w_mlp_down", layer)
        dy, dnorm[(layer, 3)] = _norm_bwd(dxout, yv, nrm(layer, 3), out_dtype=ACT_DTYPE, name=f"norm_mlp_out_bwd_{layer}")
        du = _mm(dy, w["w_mlp_down"][layer], nt=True, out_dtype=ACT_DTYPE, tm=512, tn=512, drelu_of=u,
                 name=f"mlp_down_dx_{layer}", ride=ride)
        rode = None
        if ride is not None:
            du, rode = du
        g[k_down] = _mm(u, dy, ta=True, out_dtype=WIRE_DTYPE, tm=512, tn=512, a_sqrelu=True,
                        name=f"mlp_down_dw_{layer}").reshape(blocks(D_FF, D_MODEL))
        g[k_up], (got[k_down],) = _mm(h, du, ta=True, out_dtype=WIRE_DTYPE, tm=512, tn=D_FF // N_DEV, out_blocked=True,
                                      name=f"mlp_up_dw_{layer}", ride=_sibling_plan([g[k_down]]))
        w_up = jnp.moveaxis(w["w_mlp_up"][layer], 0, 1).reshape(D_MODEL, D_FF)
        dh, (got[k_up],) = _mm(du, w_up, nt=True, out_dtype=F32, tm=512, tn=512, name=f"mlp_up_dx_{layer}",
                               ride=_sibling_plan([g[k_up]]))
        pair_sum(k_down)
        pair_sum(k_up)
        dxin, dnorm[(layer, 2)] = _norm_bwd(dh, xin, nrm(layer, 2), out_dtype=F32, add=dxout, name=f"norm_mlp_bwd_{layer}")
        return dxin, rode

    first = _run_plan(_gather_plan([shard[("w_in_even", 0)], vec_slab]), "weights_all_gather_first")
    w["w_in_even"] = _join_shards(first[0], 1)
    for n, b in zip(VECTORS, _unpack(first[1], vec_shapes, lead=(N_DEV,))):
        w[n] = _join_shards(b, SHARDED[n])
    w["w_mlp_up"], w["w_mlp_down"] = [None] * DEPTH, [None] * DEPTH

    wmm_e, wel_e = _even_in_split(w["w_in_even"])
    w_up_pad = jnp.pad(w["gla_w_a_up"][0], ((0, LANES - GLA_RANK), (0, 0)))
    b_f_pad = jnp.pad(w["fox_b_f"], ((0, 0), (0, LANES - ATT_HEADS)))
    h0 = _norm_fwd(x, nrm(0, 0), out_dtype=ACT_DTYPE, name="norm_in_0")
    pmm0, (w_out_even,) = _mm(h0, wmm_e, out_dtype=ACT_DTYPE, tm=1024, tn=512, name="in_even_mm",
                              ride=gather(("w_out_even", 0)))
    pel0 = _mm(h0, wel_e, out_dtype=F32, tm=1024, tn=768, name="in_even_el")
    (out_a, states), (w["w_mlp_up"][0],) = _gla_fwd(pmm0, pel0, w_up_pad, w["gla_b_a"], w["gla_norm_w"],
                                                    ride=gather(("w_mlp_up", 0)))
    cum, cum_t = _fox_gate_fwd(pel0, b_f_pad)
    (out_b, lse_b), (w_mlp_down0, w_in_odd) = _fox_fwd(pmm0, cum, cum_t, ride=gather(("w_mlp_down", 0), ("w_in_odd", 0)))
    w["w_out_even"] = w_out_even.reshape(D_MODEL, D_MODEL)
    w["w_mlp_down"][0] = w_mlp_down0.reshape(D_FF, D_MODEL)
    w["w_in_odd"] = _join_shards(w_in_odd, 1)
    mix_in0 = jnp.concatenate([out_a, out_b], axis=1)
    mix0 = _mm(mix_in0, w["w_out_even"], out_dtype=F32, tm=1024, tn=512, name="out_even")
    x1 = _norm_fwd(mix0, nrm(0, 1), out_dtype=F32, res=x, name="norm_mix_0")
    x2, mlp0, (w_out_odd,), (w["w_mlp_up"][1],) = mlp_fwd(x1, 0, gather(("w_out_odd", 0)), gather(("w_mlp_up", 1)))
    w["w_out_odd"] = w_out_odd.reshape(D_MODEL, D_MODEL)

    w_in_o = w["w_in_odd"]
    n_mm_o = 3 * GROUP_WIDTH
    wa_bd, wx_bd = _block_diag_pairs(w["lru_w_a"][0]), _block_diag_pairs(w["lru_w_x"][0])
    base = _ca_bias_base(w["rel_bias"][0])
    h1 = _norm_fwd(x2, nrm(1, 0), out_dtype=ACT_DTYPE, name="norm_in_1")
    pmm1 = _mm(h1, w_in_o[:, :n_mm_o], out_dtype=ACT_DTYPE, tm=1024, tn=512, name="in_odd_mm")
    pel1 = _mm(h1, w_in_o[:, n_mm_o:], out_dtype=F32, tm=1024, tn=512, name="in_odd_el")
    kp = jnp.pad(pmm1[:, GROUP_WIDTH:2 * GROUP_WIDTH], ((CA_LEFT, 0), (0, 0)))
    vp = jnp.pad(pmm1[:, 2 * GROUP_WIDTH:], ((CA_LEFT, 0), (0, 0)))
    (out_c, lse_c), (w_mlp_down1,) = _ca_fwd(pmm1, kp, vp, base, ride=gather(("w_mlp_down", 1)))
    w["w_mlp_down"][1] = w_mlp_down1.reshape(D_FF, D_MODEL)
    lru_args = (pel1, w["conv_w"][0], w["conv_b"], wa_bd, w["lru_b_a"], wx_bd, w["lru_b_x"], w["lru_lambda"])
    out_d = _lru_fwd(*lru_args)
    mix_in1 = jnp.concatenate([out_c, out_d], axis=1)
    mix1 = _mm(mix_in1, w["w_out_odd"], out_dtype=F32, tm=1024, tn=512, name="out_odd")
    x3 = _norm_fwd(mix1, nrm(1, 1), out_dtype=F32, res=x2, name="norm_mix_1")
    x4, mlp1, _, _ = mlp_fwd(x3, 1, None, None)

    loss, dx4 = _loss_fwd_bwd(x4, target)

    k_oo, k_io, k_oe, k_ie = ("w_out_odd", 0), ("w_in_odd", 0), ("w_out_even", 0), ("w_in_even", 0)
    mlp_keys = lambda l: [("w_mlp_down", l), ("w_mlp_up", l)]
    dx3, _ = mlp_bwd(dx4, mlp1, 1, None)
    dmix1, dnorm[(1, 1)] = _norm_bwd(dx3, mix1, nrm(1, 1), out_dtype=ACT_DTYPE, name="norm_mix_bwd_1")
    g[k_oo] = _mm(mix_in1, dmix1, ta=True, out_dtype=WIRE_DTYPE, tm=512, tn=512, name="out_odd_dw").reshape(
        blocks(D_MODEL, D_MODEL))
    dmix_in1, (got[k_oo],) = _mm(dmix1, w["w_out_odd"], nt=True, out_dtype=F32, tm=512, tn=512, name="out_odd_dx",
                                 ride=_sibling_plan([g[k_oo]]))
    (dq_c, dkp, dvp, dbase), rode = _ca_bwd(pmm1, kp, vp, base, lse_c, dmix_in1,
                                            ride=_chip_plan([sums[k] for k in mlp_keys(1)]))
    recv.update(zip(mlp_keys(1), rode))
    pair_sum(k_oo)
    (dgate, dxin, g_conv_w, g_conv_b, dwa_bd, g_lru_b_a, dwx_bd, g_lru_b_x, g_lru_lambda), (recv[k_oo],) = _lru_bwd(
        *lru_args, dmix_in1, ride=_chip_plan([sums[k_oo]]))
    dp1 = jnp.concatenate([dq_c, dkp[CA_LEFT:].astype(ACT_DTYPE), dvp[CA_LEFT:].astype(ACT_DTYPE), dgate, dxin], axis=1)
    g[k_io] = _column_shards(_mm(h1, dp1, ta=True, out_dtype=WIRE_DTYPE, tm=512, tn=512, name="in_odd_dw"))
    dh1, (got[k_io],) = _mm(dp1, w_in_o, nt=True, out_dtype=F32, tm=512, tn=512, name="in_odd_dx",
                            ride=_sibling_plan([g[k_io]]))
    pair_sum(k_io)
    dx2, dnorm[(1, 0)] = _norm_bwd(dh1, x2, nrm(1, 0), out_dtype=F32, add=dx3, name="norm_in_bwd_1")
    g["rel_bias"] = _ca_bias_base_grad(dbase)[None]
    g["conv_w"], g["conv_b"] = g_conv_w[None], g_conv_b
    g["lru_w_a"], g["lru_w_x"] = _block_diag_pairs_grad(dwa_bd)[None], _block_diag_pairs_grad(dwx_bd)[None]
    g["lru_b_a"], g["lru_b_x"], g["lru_lambda"] = g_lru_b_a, g_lru_b_x, g_lru_lambda

    dx1, (recv[k_io],) = mlp_bwd(dx2, mlp0, 0, _chip_plan([sums[k_io]]))
    dmix0, dnorm[(0, 1)] = _norm_bwd(dx1, mix0, nrm(0, 1), out_dtype=ACT_DTYPE, name="norm_mix_bwd_0")
    g[k_oe] = _mm(mix_in0, dmix0, ta=True, out_dtype=WIRE_DTYPE, tm=512, tn=512, name="out_even_dw").reshape(
        blocks(D_MODEL, D_MODEL))
    dmix_in0, (got[k_oe],) = _mm(dmix0, w["w_out_even"], nt=True, out_dtype=F32, tm=512, tn=512, name="out_even_dx",
                                 ride=_sibling_plan([g[k_oe]]))
    (dq_a, dk_a, dv_a, dr_a, da_a, dw_up_pad, g_gla_b_a, g_gla_norm_w), rode = _gla_bwd(
        pmm0, pel0, w_up_pad, w["gla_b_a"], w["gla_norm_w"], states, dmix_in0,
        ride=_chip_plan([sums[k] for k in mlp_keys(0)]))
    recv.update(zip(mlp_keys(0), rode))
    pair_sum(k_oe)
    (dq_b, dk_b, dv_b, dcum_t), (recv[k_oe],) = _fox_bwd(pmm0, cum, cum_t, lse_b, dmix_in0, ride=_chip_plan([sums[k_oe]]))
    df_b, db_f = _fox_gate_bwd(pel0, b_f_pad, dcum_t)
    g["gla_w_a_up"] = dw_up_pad[:GLA_RANK][None]
    g["gla_b_a"], g["gla_norm_w"], g["fox_b_f"] = g_gla_b_a, g_gla_norm_w, db_f[:, :ATT_HEADS]
    dp0 = jnp.concatenate([dq_a, dk_a, dv_a, dq_b, dk_b.astype(ACT_DTYPE), dv_b.astype(ACT_DTYPE), dr_a, da_a, df_b],
                          axis=1)
    w_perm = jnp.concatenate([wmm_e, wel_e], axis=1)
    n_mm_e = wmm_e.shape[1]
    dw_perm, (repl_parts,) = _mm(h0, dp0, ta=True, out_dtype=WIRE_DTYPE, tm=512, tn=dp0.shape[1] // 2, name="in_even_dw",
                                 ride=_gather_plan([_pack([g[n] for n in REPLICATED], REPL_SLAB_ROWS)]))
    g[k_ie] = _column_shards(_even_in_merge(dw_perm[:, :n_mm_e], dw_perm[:, n_mm_e:]))
    dh0, (got[k_ie],) = _mm(dp0, w_perm, nt=True, out_dtype=F32, tm=512, tn=512, name="in_even_dx",
                            ride=_sibling_plan([g[k_ie]]))
    pair_sum(k_ie)
    dx0, dnorm[(0, 0)] = _norm_bwd(dh0, x, nrm(0, 0), out_dtype=F32, add=dx1, name="norm_in_bwd_0")

    g["norm_w"] = jnp.stack([jnp.concatenate([dnorm[(l, k)] for k in range(4)], axis=0) for l in range(DEPTH)])
    k_vec = ("vectors", 0)
    vec_grads = _pack([_split_shards(g[n], SHARDED[n]) for n in VECTORS], VEC_SLAB_ROWS, lead=(N_DEV,))
    g[k_vec] = vec_grads.astype(WIRE_DTYPE)
    (got[k_vec],) = _run_plan(_sibling_plan([g[k_vec]]), "rs_sibling_exchange_last")
    pair_sum(k_vec)
    recv[k_ie], recv[k_vec] = _run_plan(_chip_plan([sums[k_ie], sums[k_vec]]), "rs_chip_exchange_last")
    return loss, dx0, sums, recv, repl_parts


def kernel(x, norm_w, w_in_even, gla_w_a_up, gla_b_a, gla_norm_w, fox_b_f, w_out_even, w_in_odd, rel_bias, conv_w, conv_b, lru_w_a, lru_b_a, lru_w_x, lru_b_x, lru_lambda, w_out_odd, w_mlp_up, w_mlp_down, loss_target, m_norm_w, m_w_in_even, m_gla_w_a_up, m_gla_b_a, m_gla_norm_w, m_fox_b_f, m_w_out_even, m_w_in_odd, m_rel_bias, m_conv_w, m_conv_b, m_lru_w_a, m_lru_b_a, m_lru_w_x, m_lru_b_x, m_lru_lambda, m_w_out_odd, m_w_mlp_up, m_w_mlp_down, v_norm_w, v_w_in_even, v_gla_w_a_up, v_gla_b_a, v_gla_norm_w, v_fox_b_f, v_w_out_even, v_w_in_odd, v_rel_bias, v_conv_w, v_conv_b, v_lru_w_a, v_lru_b_a, v_lru_w_x, v_lru_b_x, v_lru_lambda, v_w_out_odd, v_w_mlp_up, v_w_mlp_down):
    wts = dict(zip(WEIGHTS, (norm_w, w_in_even, gla_w_a_up, gla_b_a, gla_norm_w, fox_b_f, w_out_even, w_in_odd, rel_bias,
                             conv_w, conv_b, lru_w_a, lru_b_a, lru_w_x, lru_b_x, lru_lambda, w_out_odd, w_mlp_up,
                             w_mlp_down)))
    mom = dict(zip(WEIGHTS, (m_norm_w, m_w_in_even, m_gla_w_a_up, m_gla_b_a, m_gla_norm_w, m_fox_b_f, m_w_out_even,
                             m_w_in_odd, m_rel_bias, m_conv_w, m_conv_b, m_lru_w_a, m_lru_b_a, m_lru_w_x, m_lru_b_x,
                             m_lru_lambda, m_w_out_odd, m_w_mlp_up, m_w_mlp_down)))
    var = dict(zip(WEIGHTS, (v_norm_w, v_w_in_even, v_gla_w_a_up, v_gla_b_a, v_gla_norm_w, v_fox_b_f, v_w_out_even,
                             v_w_in_odd, v_rel_bias, v_conv_w, v_conv_b, v_lru_w_a, v_lru_b_a, v_lru_w_x, v_lru_b_x,
                             v_lru_lambda, v_w_out_odd, v_w_mlp_up, v_w_mlp_down)))
    vec_shapes = [wts[n].shape for n in VECTORS]
    repl_shapes = [wts[n].shape for n in REPLICATED]
    place = jnp.stack([lax.axis_index("c"), 2 * lax.axis_index("x") + lax.axis_index("y")]).astype(jnp.int32)

    shard = {(n, l): wts[n][l].astype(WIRE_DTYPE) for n, l in MATRIX_BLOCKS}
    vec_slab = _pack([wts[n] for n in VECTORS], VEC_SLAB_ROWS)
    loss_blk, dx, sums, recv, repl_parts = _forward_backward(
        x[0], loss_target[0], shard, vec_slab, vec_shapes, {n: wts[n] for n in REPLICATED}, place)
    loss = lax.psum(loss_blk[0, 0], ("x", "y", "c"))

    k_vec = ("vectors", 0)

    vec_of = lambda d: _pack([d[n] for n in VECTORS], VEC_SLAB_ROWS)[None]
    upd = {n: _adamw_sharded([(sums[(n, l)], recv[(n, l)]) for l in range(wts[n].shape[0])], wts[n], mom[n], var[n],
                             place, f"adamw_{n}") for n in MATRICES}
    vec_upd = _adamw_sharded([(sums[k_vec], recv[k_vec])], vec_of(wts), vec_of(mom), vec_of(var), place, "adamw_vectors")
    rp = _adamw_replicated(repl_parts, _pack([wts[n] for n in REPLICATED], REPL_SLAB_ROWS),
                           _pack([mom[n] for n in REPLICATED], REPL_SLAB_ROWS),
                           _pack([var[n] for n in REPLICATED], REPL_SLAB_ROWS), "adamw_replicated")
    outs = []
    for kind in range(4):
        vals = dict(zip(VECTORS, _unpack(vec_upd[kind][0], vec_shapes)))
        vals.update(zip(REPLICATED, _unpack(rp[kind], repl_shapes)))
        vals.update((n, upd[n][kind]) for n in MATRICES)
        outs += [vals[n] for n in WEIGHTS]
    return (loss, dx[None], *outs)
```

```python
import functools
from typing import Callable, NamedTuple

import jax
import jax.numpy as jnp
from jax import lax
from jax.experimental import pallas as pl
from jax.experimental.pallas import tpu as pltpu

F32 = jnp.float32
MXU_DTYPE = jnp.bfloat16
ACT_DTYPE = jnp.bfloat16
WIRE_DTYPE = jnp.bfloat16

V7X_VMEM_BYTES = 64 * 1024 * 1024
VMEM_LIMIT = (V7X_VMEM_BYTES * 7) // 8
LANES = 128

D_MODEL = 1024
SEQ = 2048
DEPTH = 2
CHUNK = 64
GROUP_WIDTH = D_MODEL // 2
D_FF = 4 * D_MODEL
NORM_EPS = 1e-6
GLA_HEADS = 4
GLA_DV = GROUP_WIDTH // GLA_HEADS
GLA_DK = GLA_DV // 2
GLA_KW = GLA_HEADS * GLA_DK
GLA_RANK = 16
GLA_GATE_TAU = 16.0
HEAD_DIM = 64
ATT_HEADS = GROUP_WIDTH // HEAD_DIM
CA_LEFT = 8 * CHUNK
REL_CLIP = 128
LRU_BLOCK_DIM = 64
CONV_WIDTH = 4
LRU_C = 8.0
N_DEV = 8

ADAM_LR = 0.001
ADAM_B1 = 0.9
ADAM_B2 = 0.999
ADAM_EPS = 1e-08
ADAM_WD = 0.01
ADAM_STEP = 10

NEG = float(jnp.finfo(jnp.float32).min)
MESH = pl.DeviceIdType.MESH


def _params(*sem):
    return pltpu.CompilerParams(dimension_semantics=sem, vmem_limit_bytes=VMEM_LIMIT)


def _dot(a, b, ca=1, cb=0):
    return lax.dot_general(a.astype(MXU_DTYPE), b.astype(MXU_DTYPE), (((ca,), (cb,)), ((), ())),
                           preferred_element_type=F32)


def _dot_exact(a, b):
    return lax.dot_general(a, b, (((1,), (0,)), ((), ())), precision=lax.Precision.HIGHEST,
                           preferred_element_type=F32)


def _log_sigmoid(x):
    return jnp.minimum(x, 0.0) - jnp.log1p(jnp.exp(-jnp.abs(x)))


def _iota(shape, axis):
    return lax.broadcasted_iota(jnp.int32, shape, axis)


ANY = pl.BlockSpec(memory_space=pl.ANY)
N_CHIPS = 4


class _Plan(NamedTuple):
    ins: list
    outs: list
    sems: list
    start: Callable
    finish: Callable


def _place():
    x, y, c = lax.axis_index("x"), lax.axis_index("y"), lax.axis_index("c")
    return x, y, c, [(1 - x, y), (x, 1 - y), (1 - x, 1 - y)]


def _gather_plan(xs):
    n = len(xs)

    def parts(x_refs, out_refs, sems):
        send_sems, recv_sems, local_sems = sems
        x, y, c, chips = _place()
        me, sibling = (x, y, c), (x, y, 1 - c)

        def rows(a, px, py, pc):
            return out_refs[a].at[4 * px + 2 * py + pc]

        def copy(a, k, block, to, src=None):
            return pltpu.make_async_remote_copy(
                src_ref=rows(a, *block) if src is None else src, dst_ref=rows(a, *block),
                send_sem=send_sems.at[7 * a + k], recv_sem=recv_sems.at[7 * a + k], device_id=to, device_id_type=MESH)

        mine = [pltpu.make_async_copy(x_refs[a], rows(a, *me), local_sems.at[a]) for a in range(n)]
        first = []
        for a in range(n):
            first.append(copy(a, 0, me, sibling, src=x_refs[a]))
            first += [copy(a, 1 + j, me, (*chip, c), src=x_refs[a]) for j, chip in enumerate(chips)]
        return c, me, sibling, chips, copy, mine, first

    def start(x_refs, out_refs, sems):
        *_, mine, first = parts(x_refs, out_refs, sems)
        for cp in first + mine:
            cp.start()

    def finish(x_refs, out_refs, sems):
        c, me, sibling, chips, copy, mine, first = parts(x_refs, out_refs, sems)
        passed = []
        for j, chip in enumerate(chips):
            for a in range(n):
                copy(a, 1 + j, (*chip, c), me).wait_recv()
                passed.append(copy(a, 4 + j, (*chip, c), sibling))
                passed[-1].start()
        for a in range(n):
            copy(a, 0, sibling, me).wait_recv()
            for j, chip in enumerate(chips):
                copy(a, 4 + j, (*chip, 1 - c), me).wait_recv()
        for cp in first + passed:
            cp.wait_send()
        for cp in mine:
            cp.wait()

    return _Plan(list(xs), [jax.ShapeDtypeStruct((N_DEV,) + x.shape, x.dtype) for x in xs],
                 [pltpu.SemaphoreType.DMA((7 * n,)), pltpu.SemaphoreType.DMA((7 * n,)), pltpu.SemaphoreType.DMA((n,))],
                 start, finish)


def _exchange_plan(copies_of, ins, outs, per_array):
    n = len(ins)

    def start(in_refs, out_refs, sems):
        for cp in copies_of(in_refs, out_refs, sems):
            cp.start()

    def finish(in_refs, out_refs, sems):
        copies = copies_of(in_refs, out_refs, sems)
        for cp in copies:
            cp.wait_recv()
        for cp in copies:
            cp.wait_send()

    return _Plan(list(ins), outs, [pltpu.SemaphoreType.DMA((per_array * n,)), pltpu.SemaphoreType.DMA((per_array * n,))],
                 start, finish)


def _sibling_plan(gs):
    def copies_of(g_refs, got_refs, sems):
        x, y, c, _ = _place()
        return [pltpu.make_async_remote_copy(
            src_ref=g_refs[a].at[2 * k + (1 - c)], dst_ref=got_refs[a].at[k], send_sem=sems[0].at[N_CHIPS * a + k],
            recv_sem=sems[1].at[N_CHIPS * a + k], device_id=(x, y, 1 - c), device_id_type=MESH)
            for a in range(len(gs)) for k in range(N_CHIPS)]

    return _exchange_plan(copies_of, gs, [jax.ShapeDtypeStruct((N_CHIPS,) + g.shape[1:], g.dtype) for g in gs], N_CHIPS)


def _chip_plan(ss):
    def copies_of(s_refs, out_refs, sems):
        x, y, c, chips = _place()
        return [pltpu.make_async_remote_copy(
            src_ref=s_refs[a].at[2 * px + py], dst_ref=out_refs[a].at[j], send_sem=sems[0].at[3 * a + j],
            recv_sem=sems[1].at[3 * a + j], device_id=(px, py, c), device_id_type=MESH)
            for a in range(len(ss)) for j, (px, py) in enumerate(chips)]

    return _exchange_plan(copies_of, ss, [jax.ShapeDtypeStruct((3,) + s.shape[1:], s.dtype) for s in ss], 3)


def _run_plan(plan, name):
    n_in, n_out = len(plan.ins), len(plan.outs)

    def body(*refs):
        args = refs[:n_in], refs[n_in:n_in + n_out], refs[n_in + n_out:]
        plan.start(*args)
        plan.finish(*args)

    return pl.pallas_call(body, out_shape=plan.outs, in_specs=[ANY] * n_in, out_specs=[ANY] * n_out,
                          scratch_shapes=plan.sems, name=name)(*plan.ins)


def _pcall(body, ride, *, grid, in_specs, out_specs, out_shape, scratch_shapes=(), semantics, name):
    if ride is None:
        return pl.pallas_call(body, grid=grid, in_specs=in_specs, out_specs=out_specs, out_shape=out_shape,
                              scratch_shapes=list(scratch_shapes), compiler_params=_params(*semantics), name=name)
    single = not isinstance(out_shape, (list, tuple))
    out_specs_l, out_shape_l = ([out_specs], [out_shape]) if single else (list(out_specs), list(out_shape))
    n_in, n_out, n_scr = len(in_specs), len(out_shape_l), len(scratch_shapes)
    r_in, r_out = len(ride.ins), len(ride.outs)

    def riding(*refs):
        cuts = [n_in, r_in, n_out, r_out, n_scr]
        groups, at = [], 0
        for width in cuts:
            groups.append(refs[at:at + width])
            at += width
        ins, r_ins, outs, r_outs, scr = groups
        sems = refs[at:]
        first = functools.reduce(jnp.logical_and, [pl.program_id(d) == 0 for d in range(len(grid))])
        last = functools.reduce(jnp.logical_and, [pl.program_id(d) == grid[d] - 1 for d in range(len(grid))])

        @pl.when(first)
        def _():
            ride.start(r_ins, r_outs, sems)

        body(*ins, *outs, *scr)

        @pl.when(last)
        def _():
            ride.finish(r_ins, r_outs, sems)

    call = pl.pallas_call(
        riding, grid=grid, in_specs=list(in_specs) + [ANY] * r_in, out_specs=out_specs_l + [ANY] * r_out,
        out_shape=out_shape_l + list(ride.outs), scratch_shapes=list(scratch_shapes) + list(ride.sems),
        compiler_params=_params(*(["arbitrary"] * len(grid))), name=name)

    def run(*args):
        res = call(*args, *ride.ins)
        return (res[0] if single else list(res[:n_out])), list(res[n_out:])

    return run


def _mm(a, b, *, nt=False, ta=False, out_dtype, tm, tn, a_sqrelu=False, drelu_of=None, b_blocked=False,
        out_blocked=False, name, ride=None):
    k, m = a.shape if ta else a.shape[::-1]
    if b_blocked:
        assert not nt and b.shape[1] == k and b.shape[2] == tn
        n = b.shape[0] * tn
    else:
        n = b.shape[0] if nt else b.shape[1]
        assert (b.shape[1] if nt else b.shape[0]) == k
    tm, tn = min(tm, m), min(tn, n)
    assert m % tm == 0 and n % tn == 0

    def body(*refs):
        a_ref, b_ref = refs[0], refs[1]
        o_ref = refs[-1]
        av = a_ref[...]
        if a_sqrelu:
            av = jnp.square(jnp.maximum(av.astype(F32), 0.0))
        acc = _dot(av, b_ref[...], 0 if ta else 1, 1 if nt else 0)
        if drelu_of is not None:
            acc = acc * (2.0 * jnp.maximum(refs[2][...].astype(F32), 0.0))
        o_ref[...] = acc.astype(out_dtype)

    if b_blocked:
        b_spec = pl.BlockSpec((None, k, tn), lambda i, j: (j, 0, 0))
    elif nt:
        b_spec = pl.BlockSpec((tn, k), lambda i, j: (j, 0))
    else:
        b_spec = pl.BlockSpec((k, tn), lambda i, j: (0, j))
    a_spec = pl.BlockSpec((k, tm), lambda i, j: (0, i)) if ta else pl.BlockSpec((tm, k), lambda i, j: (i, 0))
    in_specs = [a_spec, b_spec]
    args = [a, b]
    if drelu_of is not None:
        in_specs.append(pl.BlockSpec((tm, tn), lambda i, j: (i, j)))
        args.append(drelu_of)
    if out_blocked:
        out_spec = pl.BlockSpec((None, tm, tn), lambda i, j: (j, i, 0))
        out_shape = jax.ShapeDtypeStruct((n // tn, m, tn), out_dtype)
    else:
        out_spec = pl.BlockSpec((tm, tn), lambda i, j: (i, j))
        out_shape = jax.ShapeDtypeStruct((m, n), out_dtype)
    return _pcall(body, ride, grid=(m // tm, n // tn), in_specs=in_specs, out_specs=out_spec, out_shape=out_shape,
                  semantics=("parallel", "parallel"), name=name)(*args)


def _mm_nt_blocked(a, b, *, out_dtype, tm, tn, name):
    m = a.shape[0]
    p, n, kp = b.shape
    assert a.shape[1] == p * kp and m % tm == 0 and n % tn == 0

    def body(a_ref, b_ref, o_ref, acc_ref):
        @pl.when(pl.program_id(2) == 0)
        def _():
            acc_ref[...] = jnp.zeros_like(acc_ref)

        acc_ref[...] += _dot(a_ref[...], b_ref[...], 1, 1)

        @pl.when(pl.program_id(2) == p - 1)
        def _():
            o_ref[...] = acc_ref[...].astype(out_dtype)

    return pl.pallas_call(
        body, grid=(m // tm, n // tn, p),
        in_specs=[pl.BlockSpec((tm, kp), lambda i, j, q: (i, q)), pl.BlockSpec((None, tn, kp), lambda i, j, q: (q, j, 0))],
        out_specs=pl.BlockSpec((tm, tn), lambda i, j, q: (i, j)),
        out_shape=jax.ShapeDtypeStruct((m, n), out_dtype),
        scratch_shapes=[pltpu.VMEM((tm, tn), F32)],
        compiler_params=_params("parallel", "parallel", "arbitrary"), name=name)(a, b)


ROW_TILE = 256


def _norm_fwd(x, w, *, out_dtype, res=None, name):
    t, d = x.shape

    def body(*refs):
        x_ref, w_ref, o_ref = refs[0], refs[1], refs[-1]
        xv = x_ref[...]
        y = xv * lax.rsqrt(jnp.mean(xv * xv, axis=-1, keepdims=True) + NORM_EPS) * w_ref[...]
        if res is not None:
            y = refs[2][...] + y
        o_ref[...] = y.astype(out_dtype)

    row = pl.BlockSpec((ROW_TILE, d), lambda i: (i, 0))
    in_specs = [row, pl.BlockSpec((1, d), lambda i: (0, 0))] + ([row] if res is not None else [])
    args = [x, w] + ([res] if res is not None else [])
    return pl.pallas_call(body, grid=(t // ROW_TILE,), in_specs=in_specs, out_specs=row,
                          out_shape=jax.ShapeDtypeStruct((t, d), out_dtype),
                          compiler_params=_params("parallel"), name=name)(*args)


def _norm_bwd(dy, x, w, *, out_dtype, add=None, name):
    t, d = x.shape

    def body(*refs):
        dy_ref, x_ref, w_ref = refs[0], refs[1], refs[2]
        dx_ref, dw_ref = refs[-2], refs[-1]
        xv = x_ref[...]
        rstd = lax.rsqrt(jnp.mean(xv * xv, axis=-1, keepdims=True) + NORM_EPS)
        xhat = xv * rstd
        dyv = dy_ref[...].astype(F32)
        g = dyv * w_ref[...]
        dx = rstd * (g - xhat * jnp.mean(g * xhat, axis=-1, keepdims=True))
        if add is not None:
            dx = dx + refs[3][...]
        dx_ref[...] = dx.astype(out_dtype)

        @pl.when(pl.program_id(0) == 0)
        def _():
            dw_ref[...] = jnp.zeros_like(dw_ref)

        dw_ref[...] += jnp.sum(dyv * xhat, axis=0, keepdims=True)

    row = pl.BlockSpec((ROW_TILE, d), lambda i: (i, 0))
    vec = pl.BlockSpec((1, d), lambda i: (0, 0))
    in_specs = [row, row, vec] + ([row] if add is not None else [])
    args = [dy, x, w] + ([add] if add is not None else [])
    return pl.pallas_call(body, grid=(t // ROW_TILE,), in_specs=in_specs, out_specs=[row, vec],
                          out_shape=[jax.ShapeDtypeStruct((t, d), out_dtype), jax.ShapeDtypeStruct((1, d), F32)],
                          compiler_params=_params("arbitrary"), name=name)(*args)


def _loss_fwd_bwd(y, target):
    t, d = y.shape

    def body(y_ref, t_ref, l_ref, dy_ref):
        diff = y_ref[...] - t_ref[...]
        dy_ref[...] = diff * (1.0 / d)

        @pl.when(pl.program_id(0) == 0)
        def _():
            l_ref[...] = jnp.zeros_like(l_ref)

        l_ref[...] += 0.5 * jnp.sum(jnp.mean(diff * diff, axis=-1, keepdims=True), axis=0, keepdims=True)

    row = pl.BlockSpec((ROW_TILE, d), lambda i: (i, 0))
    return pl.pallas_call(body, grid=(t // ROW_TILE,), in_specs=[row, row],
                          out_specs=[pl.BlockSpec((8, LANES), lambda i: (0, 0)), row],
                          out_shape=[jax.ShapeDtypeStruct((8, LANES), F32), jax.ShapeDtypeStruct((t, d), F32)],
                          compiler_params=_params("arbitrary"), name="loss")(y, target)


def _gla_specs(t):
    return [pl.BlockSpec((t, GLA_KW), lambda i: (0, 0)),
            pl.BlockSpec((t, GLA_KW), lambda i: (0, 1)),
            pl.BlockSpec((t, GROUP_WIDTH), lambda i: (0, 1)),
            pl.BlockSpec((t, GROUP_WIDTH), lambda i: (0, 0)),
            pl.BlockSpec((t, LANES), lambda i: (0, 4)),
            pl.BlockSpec((LANES, GLA_KW), lambda i: (0, 0)),
            pl.BlockSpec((1, GLA_KW), lambda i: (0, 0)),
            pl.BlockSpec((1, GROUP_WIDTH), lambda i: (0, 0))]


def _gla_fwd(pmm, pel, w_up, b_a, gnorm_w, ride=None):
    t = pmm.shape[0]
    nc = t // CHUNK
    scale = GLA_DK ** -0.5

    def body(q_ref, k_ref, v_ref, r_ref, a_ref, wup_ref, ba_ref, gw_ref, o_ref, st_ref, la_scr, s_scr):
        z = _dot(a_ref[...], wup_ref[...]) + ba_ref[...]
        la_scr[...] = _log_sigmoid(z) * (1.0 / GLA_GATE_TAU)
        s_scr[...] = jnp.zeros_like(s_scr)
        tri = (_iota((CHUNK, CHUNK), 1) <= _iota((CHUNK, CHUNK), 0)).astype(F32)

        def chunk(c, carry):
            rows = pl.ds(pl.multiple_of(c * CHUNK, CHUNK), CHUNK)
            cum = _dot_exact(tri, la_scr[rows, :])
            tot = cum[CHUNK - 1:CHUNK, :]
            kd = k_ref[rows, :].astype(F32) * jnp.exp(tot - cum)
            decay = jnp.exp(tot)
            qs = q_ref[rows, :].astype(F32) * scale
            vv = v_ref[rows, :].astype(F32)
            rr = r_ref[rows, :]
            gate = rr * jax.nn.sigmoid(rr) * gw_ref[...]
            for h in range(GLA_HEADS):
                ks = slice(h * GLA_DK, (h + 1) * GLA_DK)
                vs = slice(h * GLA_DV, (h + 1) * GLA_DV)
                inc_t = _dot(vv[:, vs].T, kd[:, ks])
                s_t = s_scr[vs, :] * decay[:, ks] + inc_t
                s_scr[vs, :] = s_t
                st_ref[c, vs, :] = s_t
                o = _dot(qs[:, ks], s_t, 1, 1)
                y = o * lax.rsqrt(jnp.mean(o * o, axis=-1, keepdims=True) + NORM_EPS)
                o_ref[rows, vs] = (y * gate[:, vs]).astype(o_ref.dtype)
            return carry

        lax.fori_loop(0, nc, chunk, 0)

    return _pcall(
        body, ride, grid=(1,), in_specs=_gla_specs(t),
        out_specs=[pl.BlockSpec((t, GROUP_WIDTH), lambda i: (0, 0)),
                   pl.BlockSpec((nc, GLA_HEADS * GLA_DV, GLA_DK), lambda i: (0, 0, 0))],
        out_shape=[jax.ShapeDtypeStruct((t, GROUP_WIDTH), ACT_DTYPE),
                   jax.ShapeDtypeStruct((nc, GLA_HEADS * GLA_DV, GLA_DK), F32)],
        scratch_shapes=[pltpu.VMEM((t, GLA_KW), F32), pltpu.VMEM((GLA_HEADS * GLA_DV, GLA_DK), F32)],
        semantics=("arbitrary",), name="gla_fwd")(pmm, pmm, pmm, pel, pel, w_up, b_a, gnorm_w)


def _gla_bwd(pmm, pel, w_up, b_a, gnorm_w, states, dmix, ride=None):
    t = pmm.shape[0]
    nc = t // CHUNK
    scale = GLA_DK ** -0.5

    def body(q_ref, k_ref, v_ref, r_ref, a_ref, wup_ref, ba_ref, gw_ref, st_ref, do_ref,
             dq_ref, dk_ref, dv_ref, dr_ref, da_ref, dwup_ref, dba_ref, dgw_ref, la_scr, dz_scr, ds_scr):
        z = _dot(a_ref[...], wup_ref[...]) + ba_ref[...]
        la_scr[...] = _log_sigmoid(z) * (1.0 / GLA_GATE_TAU)
        ds_scr[...] = jnp.zeros_like(ds_scr)
        dgw_ref[...] = jnp.zeros_like(dgw_ref)
        row_i, col_i = _iota((CHUNK, CHUNK), 0), _iota((CHUNK, CHUNK), 1)
        tri = (col_i <= row_i).astype(F32)
        tri_strict = (col_i < row_i).astype(F32)

        def chunk(n, carry):
            c = nc - 1 - n
            rows = pl.ds(pl.multiple_of(c * CHUNK, CHUNK), CHUNK)
            cum = _dot_exact(tri, la_scr[rows, :])
            tot = cum[CHUNK - 1:CHUNK, :]
            e = jnp.exp(tot - cum)
            kd = k_ref[rows, :].astype(F32) * e
            decay = jnp.exp(tot)
            qs = q_ref[rows, :].astype(F32) * scale
            vv = v_ref[rows, :].astype(F32)
            rr = r_ref[rows, :]
            sig = jax.nn.sigmoid(rr)
            silu = rr * sig
            dsilu = sig * (1.0 + rr * (1.0 - sig))
            dout = do_ref[rows, :]
            gw = gw_ref[...]
            c_prev = jnp.maximum(c - 1, 0)
            has_prev = (c > 0).astype(F32)
            zc = _dot(a_ref[rows, :], wup_ref[...]) + ba_ref[...]
            dz_scale = jax.nn.sigmoid(-zc) * (1.0 / GLA_GATE_TAU)
            for h in range(GLA_HEADS):
                ks = slice(h * GLA_DK, (h + 1) * GLA_DK)
                vs = slice(h * GLA_DV, (h + 1) * GLA_DV)
                s_t = st_ref[c, vs, :]
                s_prev = st_ref[c_prev, vs, :] * has_prev
                o = _dot(qs[:, ks], s_t, 1, 1)
                rstd = lax.rsqrt(jnp.mean(o * o, axis=-1, keepdims=True) + NORM_EPS)
                y = o * rstd
                dg = dout[:, vs]
                dgw_ref[:, vs] += jnp.sum(dg * y * silu[:, vs], axis=0, keepdims=True)
                dr_ref[rows, vs] = (dg * y * gw[:, vs] * dsilu[:, vs]).astype(dr_ref.dtype)
                dy = dg * gw[:, vs] * silu[:, vs]
                d_o = rstd * (dy - y * jnp.mean(dy * y, axis=-1, keepdims=True))
                dq_ref[rows, ks] = (_dot(d_o, s_t) * scale).astype(dq_ref.dtype)
                ds_t = ds_scr[vs, :] + _dot(d_o.T, qs[:, ks])
                dv_ref[rows, vs] = _dot(kd[:, ks], ds_t, 1, 1).astype(dv_ref.dtype)
                dkd = _dot(vv[:, vs], ds_t)
                ddecay = jnp.sum(ds_t * s_prev, axis=0, keepdims=True)
                ds_scr[vs, :] = ds_t * decay[:, ks]
                dla = ddecay * decay[:, ks] + _dot_exact(tri_strict, dkd * kd[:, ks])
                dz_scr[rows, ks] = dla * dz_scale[:, ks]
                dk_ref[rows, ks] = (dkd * e[:, ks]).astype(dk_ref.dtype)
            return carry

        lax.fori_loop(0, nc, chunk, 0)
        dz = dz_scr[...]
        da_ref[...] = _dot(dz, wup_ref[...], 1, 1).astype(da_ref.dtype)
        dwup_ref[...] = _dot(a_ref[...].T, dz)
        dba_ref[...] = jnp.sum(dz, axis=0, keepdims=True)

    in_specs = _gla_specs(t) + [
        pl.BlockSpec((nc, GLA_HEADS * GLA_DV, GLA_DK), lambda i: (0, 0, 0)),
        pl.BlockSpec((t, GROUP_WIDTH), lambda i: (0, 0))]
    full = lambda r, c: pl.BlockSpec((r, c), lambda i: (0, 0))
    return _pcall(
        body, ride, grid=(1,), in_specs=in_specs,
        out_specs=[full(t, GLA_KW), full(t, GLA_KW), full(t, GROUP_WIDTH), full(t, GROUP_WIDTH), full(t, LANES),
                   full(LANES, GLA_KW), full(1, GLA_KW), full(1, GROUP_WIDTH)],
        out_shape=[jax.ShapeDtypeStruct((t, GLA_KW), ACT_DTYPE), jax.ShapeDtypeStruct((t, GLA_KW), ACT_DTYPE),
                   jax.ShapeDtypeStruct((t, GROUP_WIDTH), ACT_DTYPE), jax.ShapeDtypeStruct((t, GROUP_WIDTH), ACT_DTYPE),
                   jax.ShapeDtypeStruct((t, LANES), ACT_DTYPE), jax.ShapeDtypeStruct((LANES, GLA_KW), F32),
                   jax.ShapeDtypeStruct((1, GLA_KW), F32), jax.ShapeDtypeStruct((1, GROUP_WIDTH), F32)],
        scratch_shapes=[pltpu.VMEM((t, GLA_KW), F32), pltpu.VMEM((t, GLA_KW), F32),
                        pltpu.VMEM((GLA_HEADS * GLA_DV, GLA_DK), F32)],
        semantics=("arbitrary",), name="gla_bwd")(
            pmm, pmm, pmm, pel, pel, w_up, b_a, gnorm_w, states, dmix)


CUM_BLOCK = 256


def _fox_gate_fwd(pel, b_f):
    t = pel.shape[0]
    nb = t // CUM_BLOCK

    def body(f_ref, b_ref, cum_ref, cum_t_ref):
        tri = (_iota((CUM_BLOCK, CUM_BLOCK), 1) <= _iota((CUM_BLOCK, CUM_BLOCK), 0)).astype(F32)
        carry = jnp.zeros((1, LANES), F32)
        for blk in range(nb):
            rows = slice(blk * CUM_BLOCK, (blk + 1) * CUM_BLOCK)
            cum = _dot_exact(tri, _log_sigmoid(f_ref[rows, :] + b_ref[...])) + carry
            cum_ref[rows, :] = cum
            cum_t_ref[blk] = cum.T[:ATT_HEADS, :]
            carry = cum[CUM_BLOCK - 1:CUM_BLOCK, :]

    return pl.pallas_call(
        body, grid=(1,),
        in_specs=[pl.BlockSpec((t, LANES), lambda i: (0, 5)), pl.BlockSpec((1, LANES), lambda i: (0, 0))],
        out_specs=[pl.BlockSpec((t, LANES), lambda i: (0, 0)),
                   pl.BlockSpec((nb, ATT_HEADS, CUM_BLOCK), lambda i: (0, 0, 0))],
        out_shape=[jax.ShapeDtypeStruct((t, LANES), F32), jax.ShapeDtypeStruct((nb, ATT_HEADS, CUM_BLOCK), F32)],
        compiler_params=_params("arbitrary"), name="fox_gate_fwd")(pel, b_f)


def _fox_gate_bwd(pel, b_f, dcum_t, dcum_q):
    t = pel.shape[0]
    nb = t // CUM_BLOCK

    def body(f_ref, b_ref, dct_ref, dcq_ref, df_ref, db_ref):
        tri_up = (_iota((CUM_BLOCK, CUM_BLOCK), 1) >= _iota((CUM_BLOCK, CUM_BLOCK), 0)).astype(F32)
        carry = jnp.zeros((1, LANES), F32)
        db = jnp.zeros((1, LANES), F32)
        for blk in reversed(range(nb)):
            rows = slice(blk * CUM_BLOCK, (blk + 1) * CUM_BLOCK)
            dls = _dot_exact(tri_up, dct_ref[blk].T + dcq_ref[rows, :]) + carry
            carry = dls[0:1, :]
            df = dls * jax.nn.sigmoid(-(f_ref[rows, :] + b_ref[...]))
            df_ref[rows, :] = df.astype(df_ref.dtype)
            db = db + jnp.sum(df, axis=0, keepdims=True)
        db_ref[...] = db

    return pl.pallas_call(
        body, grid=(1,),
        in_specs=[pl.BlockSpec((t, LANES), lambda i: (0, 5)), pl.BlockSpec((1, LANES), lambda i: (0, 0)),
                  pl.BlockSpec((nb, LANES, CUM_BLOCK), lambda i: (0, 0, 0)), pl.BlockSpec((t, LANES), lambda i: (0, 0))],
        out_specs=[pl.BlockSpec((t, LANES), lambda i: (0, 0)), pl.BlockSpec((1, LANES), lambda i: (0, 0))],
        out_shape=[jax.ShapeDtypeStruct((t, LANES), ACT_DTYPE), jax.ShapeDtypeStruct((1, LANES), F32)],
        compiler_params=_params("arbitrary"), name="fox_gate_bwd")(pel, b_f, dcum_t, dcum_q)


FOX_Q_BLOCK = 256


assert FOX_Q_BLOCK == CUM_BLOCK


def _fox_scores(q, k_ref, cq, cum_t_ref, h, i, kb):
    hs = slice(h * HEAD_DIM, (h + 1) * HEAD_DIM)
    keys = pl.ds(pl.multiple_of(kb * FOX_Q_BLOCK, FOX_Q_BLOCK), FOX_Q_BLOCK)
    s = _dot(q, k_ref[keys, hs], 1, 1) * (HEAD_DIM ** -0.5) + (cq - cum_t_ref[kb, h:h + 1, :])
    shape = (FOX_Q_BLOCK, FOX_Q_BLOCK)
    causal = kb * FOX_Q_BLOCK + _iota(shape, 1) <= i * FOX_Q_BLOCK + _iota(shape, 0)
    return jnp.where(causal, s, NEG), keys


def _fox_specs(t):
    bq, nb = FOX_Q_BLOCK, t // FOX_Q_BLOCK
    return [pl.BlockSpec((bq, GROUP_WIDTH), lambda i: (i, 2)), pl.BlockSpec((t, GROUP_WIDTH), lambda i: (0, 3)),
            pl.BlockSpec((t, GROUP_WIDTH), lambda i: (0, 4)), pl.BlockSpec((bq, LANES), lambda i: (i, 0)),
            pl.BlockSpec((nb, ATT_HEADS, bq), lambda i: (0, 0, 0))]


def _fox_fwd(pmm, cum, cum_t, ride=None):
    t = pmm.shape[0]
    bq = FOX_Q_BLOCK

    def body(q_ref, k_ref, v_ref, cum_ref, cum_t_ref, o_ref, lse_ref):
        i = pl.program_id(0)
        lse_ref[...] = jnp.zeros_like(lse_ref)
        for h in range(ATT_HEADS):
            hs = slice(h * HEAD_DIM, (h + 1) * HEAD_DIM)
            q, cq = q_ref[:, hs], cum_ref[:, h:h + 1]

            def key_block(kb, carry):
                m, l, acc = carry
                s, keys = _fox_scores(q, k_ref, cq, cum_t_ref, h, i, kb)
                m_new = jnp.maximum(m, jnp.max(s, axis=-1, keepdims=True))
                alpha = jnp.exp(m - m_new)
                p = jnp.exp(s - m_new)
                return (m_new, alpha * l + jnp.sum(p, axis=-1, keepdims=True),
                        alpha * acc + _dot(p, v_ref[keys, hs]))

            m, l, acc = lax.fori_loop(0, i + 1, key_block, (jnp.full((bq, 1), NEG, F32), jnp.zeros((bq, 1), F32),
                                                            jnp.zeros((bq, HEAD_DIM), F32)))
            o_ref[:, hs] = (acc / l).astype(o_ref.dtype)
            lse_ref[:, h:h + 1] = m + jnp.log(l)

    return _pcall(
        body, ride, grid=(t // bq,), in_specs=_fox_specs(t),
        out_specs=[pl.BlockSpec((bq, GROUP_WIDTH), lambda i: (i, 0)), pl.BlockSpec((bq, LANES), lambda i: (i, 0))],
        out_shape=[jax.ShapeDtypeStruct((t, GROUP_WIDTH), ACT_DTYPE), jax.ShapeDtypeStruct((t, LANES), F32)],
        semantics=("parallel",), name="fox_fwd")(pmm, pmm, pmm, cum, cum_t)


def _fox_bwd(pmm, cum, cum_t, lse, out_b, dmix, ride=None):
    t = pmm.shape[0]
    bq, nb = FOX_Q_BLOCK, t // FOX_Q_BLOCK
    scale = HEAD_DIM ** -0.5

    def body(q_ref, k_ref, v_ref, cum_ref, cum_t_ref, lse_ref, o_ref, do_ref, dq_ref, dk_ref, dv_ref, dct_ref, dcq_ref):
        i = pl.program_id(0)

        @pl.when(i == 0)
        def _():
            dk_ref[...] = jnp.zeros_like(dk_ref)
            dv_ref[...] = jnp.zeros_like(dv_ref)
            dct_ref[...] = jnp.zeros_like(dct_ref)

        dcq_ref[...] = jnp.zeros_like(dcq_ref)
        for h in range(ATT_HEADS):
            hs = slice(h * HEAD_DIM, (h + 1) * HEAD_DIM)
            q, cq, do, lse_h = q_ref[:, hs], cum_ref[:, h:h + 1], do_ref[:, hs], lse_ref[:, h:h + 1]
            delta = jnp.sum(do * o_ref[:, hs].astype(F32), axis=-1, keepdims=True)

            def key_block(kb, carry):
                dq, row_sum = carry
                s, keys = _fox_scores(q, k_ref, cq, cum_t_ref, h, i, kb)
                p = jnp.exp(s - lse_h)
                ds = p * (_dot(do, v_ref[keys, hs], 1, 1) - delta)
                dk_ref[keys, hs] += _dot(ds, q, 0, 0) * scale
                dv_ref[keys, hs] += _dot(p, do, 0, 0)
                dct_ref[kb, h:h + 1, :] += -jnp.sum(ds, axis=0, keepdims=True)
                return dq + _dot(ds, k_ref[keys, hs]), row_sum + jnp.sum(ds, axis=1, keepdims=True)

            dq, row_sum = lax.fori_loop(0, i + 1, key_block, (jnp.zeros((bq, HEAD_DIM), F32), jnp.zeros((bq, 1), F32)))
            dq_ref[:, hs] = (dq * scale).astype(dq_ref.dtype)
            dcq_ref[:, h:h + 1] = row_sum

    whole = pl.BlockSpec((t, GROUP_WIDTH), lambda i: (0, 0))
    return _pcall(
        body, ride, grid=(t // bq,),
        in_specs=_fox_specs(t) + [pl.BlockSpec((bq, LANES), lambda i: (i, 0)),
                                  pl.BlockSpec((bq, GROUP_WIDTH), lambda i: (i, 0)),
                                  pl.BlockSpec((bq, GROUP_WIDTH), lambda i: (i, 1))],
        out_specs=[pl.BlockSpec((bq, GROUP_WIDTH), lambda i: (i, 0)), whole, whole,
                   pl.BlockSpec((nb, LANES, bq), lambda i: (0, 0, 0)), pl.BlockSpec((bq, LANES), lambda i: (i, 0))],
        out_shape=[jax.ShapeDtypeStruct((t, GROUP_WIDTH), ACT_DTYPE), jax.ShapeDtypeStruct((t, GROUP_WIDTH), F32),
                   jax.ShapeDtypeStruct((t, GROUP_WIDTH), F32), jax.ShapeDtypeStruct((nb, LANES, bq), F32),
                   jax.ShapeDtypeStruct((t, LANES), F32)],
        semantics=("arbitrary",), name="fox_bwd")(pmm, pmm, pmm, cum, cum_t, lse, out_b, dmix)


CA_Q_BLOCK = 4 * CHUNK
CA_WINDOW = CA_Q_BLOCK + CA_LEFT
CA_BASE = 1024


def _ca_bias_base(rel_bias):
    n = rel_bias.shape[0]
    flat = CA_Q_BLOCK + CA_LEFT - REL_CLIP
    tail = CA_BASE - flat - (2 * REL_CLIP + 1)
    return jnp.concatenate([jnp.broadcast_to(rel_bias[:, 2 * REL_CLIP:], (n, flat)), rel_bias[:, ::-1],
                            jnp.broadcast_to(rel_bias[:, :1], (n, tail))], axis=1)


def _ca_bias_base_grad(dbase):
    flat = CA_Q_BLOCK + CA_LEFT - REL_CLIP
    mid = dbase[:, flat:flat + 2 * REL_CLIP + 1][:, ::-1]
    lo = jnp.sum(dbase[:, flat + 2 * REL_CLIP + 1:], axis=1, keepdims=True)
    hi = jnp.sum(dbase[:, :flat], axis=1, keepdims=True)
    pad = jnp.zeros((dbase.shape[0], 2 * REL_CLIP - 1), F32)
    return mid + jnp.concatenate([lo, pad, hi], axis=1)


def _ca_mask(i):
    r, j = _iota((CA_Q_BLOCK, CA_WINDOW), 0), _iota((CA_Q_BLOCK, CA_WINDOW), 1)
    rc, jc = r // CHUNK, j // CHUNK
    return (jc >= rc) & (jc <= rc + CA_LEFT // CHUNK) & (i * CA_Q_BLOCK + j >= CA_LEFT)


def _ca_scores(q_ref, kp_ref, base_ref, win, h, mask):
    hs = slice(h * HEAD_DIM, (h + 1) * HEAD_DIM)
    s = _dot(q_ref[:, hs], kp_ref[win, hs], 1, 1) * (HEAD_DIM ** -0.5)
    rows = jnp.broadcast_to(base_ref[h:h + 1, :], (CA_Q_BLOCK, CA_BASE))
    bias = pltpu.roll(rows, CA_BASE - CA_Q_BLOCK, 1, stride=1, stride_axis=0)[:, :CA_WINDOW]
    return jnp.where(mask, s + bias, NEG)


def _ca_fwd(pmm, kp, vp, base, ride=None):
    t = pmm.shape[0]

    def body(q_ref, kp_ref, vp_ref, base_ref, o_ref, lse_ref):
        i = pl.program_id(0)
        win = pl.ds(pl.multiple_of(i * CA_Q_BLOCK, CA_Q_BLOCK), CA_WINDOW)
        mask = _ca_mask(i)
        lse_ref[...] = jnp.zeros_like(lse_ref)
        for h in range(ATT_HEADS):
            hs = slice(h * HEAD_DIM, (h + 1) * HEAD_DIM)
            s = _ca_scores(q_ref, kp_ref, base_ref, win, h, mask)
            m = jnp.max(s, axis=-1, keepdims=True)
            p = jnp.exp(s - m)
            l = jnp.sum(p, axis=-1, keepdims=True)
            o_ref[:, hs] = (_dot(p, vp_ref[win, hs]) / l).astype(o_ref.dtype)
            lse_ref[:, h:h + 1] = m + jnp.log(l)

    padded = pl.BlockSpec((t + CA_LEFT, GROUP_WIDTH), lambda i: (0, 0))
    return _pcall(
        body, ride, grid=(t // CA_Q_BLOCK,),
        in_specs=[pl.BlockSpec((CA_Q_BLOCK, GROUP_WIDTH), lambda i: (i, 0)), padded, padded,
                  pl.BlockSpec((ATT_HEADS, CA_BASE), lambda i: (0, 0))],
        out_specs=[pl.BlockSpec((CA_Q_BLOCK, GROUP_WIDTH), lambda i: (i, 0)),
                   pl.BlockSpec((CA_Q_BLOCK, LANES), lambda i: (i, 0))],
        out_shape=[jax.ShapeDtypeStruct((t, GROUP_WIDTH), ACT_DTYPE), jax.ShapeDtypeStruct((t, LANES), F32)],
        semantics=("parallel",), name="ca_fwd")(pmm, kp, vp, base)


def _ca_bwd(pmm, kp, vp, base, lse, dmix, ride=None):
    t = pmm.shape[0]
    scale = HEAD_DIM ** -0.5

    def body(q_ref, kp_ref, vp_ref, base_ref, lse_ref, do_ref, dq_ref, dkp_ref, dvp_ref, dbase_ref):
        i = pl.program_id(0)

        @pl.when(i == 0)
        def _():
            dkp_ref[...] = jnp.zeros_like(dkp_ref)
            dvp_ref[...] = jnp.zeros_like(dvp_ref)
            dbase_ref[...] = jnp.zeros_like(dbase_ref)

        win = pl.ds(pl.multiple_of(i * CA_Q_BLOCK, CA_Q_BLOCK), CA_WINDOW)
        mask = _ca_mask(i)
        flip = (_iota((CA_Q_BLOCK, CA_Q_BLOCK), 0) + _iota((CA_Q_BLOCK, CA_Q_BLOCK), 1) == CA_Q_BLOCK - 1).astype(F32)
        for h in range(ATT_HEADS):
            hs = slice(h * HEAD_DIM, (h + 1) * HEAD_DIM)
            s = _ca_scores(q_ref, kp_ref, base_ref, win, h, mask)
            p = jnp.exp(s - lse_ref[:, h:h + 1])
            do = do_ref[:, hs]
            dp = _dot(do, vp_ref[win, hs], 1, 1)
            ds = p * (dp - jnp.sum(p * dp, axis=-1, keepdims=True))
            dq_ref[:, hs] = (_dot(ds, kp_ref[win, hs]) * scale).astype(dq_ref.dtype)
            dkp_ref[win, hs] += _dot(ds, q_ref[:, hs], 0, 0) * scale
            dvp_ref[win, hs] += _dot(p, do, 0, 0)
            rev = jnp.concatenate([_dot(flip, ds), jnp.zeros((CA_Q_BLOCK, CA_BASE - CA_WINDOW), F32)], axis=1)
            lined = pltpu.roll(rev, 1, 1, stride=1, stride_axis=0)
            dbase_ref[h:h + 1, :] += jnp.sum(lined, axis=0, keepdims=True)

    padded = pl.BlockSpec((t + CA_LEFT, GROUP_WIDTH), lambda i: (0, 0))
    return _pcall(
        body, ride, grid=(t // CA_Q_BLOCK,),
        in_specs=[pl.BlockSpec((CA_Q_BLOCK, GROUP_WIDTH), lambda i: (i, 0)), padded, padded,
                  pl.BlockSpec((ATT_HEADS, CA_BASE), lambda i: (0, 0)),
                  pl.BlockSpec((CA_Q_BLOCK, LANES), lambda i: (i, 0)),
                  pl.BlockSpec((CA_Q_BLOCK, GROUP_WIDTH), lambda i: (i, 0))],
        out_specs=[pl.BlockSpec((CA_Q_BLOCK, GROUP_WIDTH), lambda i: (i, 0)), padded, padded,
                   pl.BlockSpec((ATT_HEADS, CA_BASE), lambda i: (0, 0))],
        out_shape=[jax.ShapeDtypeStruct((t, GROUP_WIDTH), ACT_DTYPE),
                   jax.ShapeDtypeStruct((t + CA_LEFT, GROUP_WIDTH), F32),
                   jax.ShapeDtypeStruct((t + CA_LEFT, GROUP_WIDTH), F32),
                   jax.ShapeDtypeStruct((ATT_HEADS, CA_BASE), F32)],
        semantics=("arbitrary",), name="ca_bwd")(pmm, kp, vp, base, lse, dmix)


GELU_C = 0.7978845608028654
GELU_A = 0.044715


def _shift_down(v, k, fill):
    return jnp.where(_iota(v.shape, 0) >= k, pltpu.roll(v, k, 0), fill)


def _shift_up(v, k, fill):
    t = v.shape[0]
    return jnp.where(_iota(v.shape, 0) < t - k, pltpu.roll(v, t - k, 0), fill)


def _linear_scan(a, b, shift):
    k = 1
    while k < a.shape[0]:
        b = a * shift(b, k, 0.0) + b
        a = a * shift(a, k, 1.0)
        k *= 2
    return b


def _neg_expm1(y):
    series = -y * (1.0 + y * (0.5 + y * (1.0 / 6.0 + y * (1.0 / 24.0 + y * (1.0 / 120.0)))))
    return jnp.where(y > -0.1, series, 1.0 - jnp.exp(y))


def _lru_forward(x, g_in, cw, cb, wa, ba, wx, bx, lam):
    xs = [_shift_down(x, CONV_WIDTH - 1 - j, 0.0) for j in range(CONV_WIDTH - 1)] + [x]
    xc = cb + sum(cw[j:j + 1, :] * xs[j] for j in range(CONV_WIDTH))
    r = jax.nn.sigmoid(_dot(xc, wa) + ba)
    i = jax.nn.sigmoid(_dot(xc, wx) + bx)
    lsl = _log_sigmoid(lam)
    la = LRU_C * r * lsl
    a = jnp.exp(la)
    s = jnp.sqrt(_neg_expm1(2.0 * la))
    h = _linear_scan(a, s * (i * xc), _shift_down)
    u = GELU_C * (g_in + GELU_A * g_in * g_in * g_in)
    th = jnp.tanh(u)
    gelu = 0.5 * g_in * (1.0 + th)
    return xs, xc, r, i, lsl, a, s, h, th, gelu


def _lru_specs(t):
    col = lambda off: pl.BlockSpec((t, LANES), lambda j: (0, j + off))
    vec = pl.BlockSpec((1, LANES), lambda j: (0, j))
    mat = pl.BlockSpec((None, LANES, LANES), lambda j: (j, 0, 0))
    return [col(0), col(GROUP_WIDTH // LANES), pl.BlockSpec((CONV_WIDTH, LANES), lambda j: (0, j)),
            vec, mat, vec, mat, vec, vec]


def _lru_fwd(pel, conv_w, conv_b, wa, ba, wx, bx, lam, ride=None):
    t = pel.shape[0]

    def body(g_ref, x_ref, cw_ref, cb_ref, wa_ref, ba_ref, wx_ref, bx_ref, lam_ref, o_ref):
        res = _lru_forward(x_ref[...], g_ref[...], cw_ref[...], cb_ref[...], wa_ref[...], ba_ref[...],
                           wx_ref[...], bx_ref[...], lam_ref[...])
        o_ref[...] = (res[7] * res[9]).astype(o_ref.dtype)

    return _pcall(
        body, ride, grid=(GROUP_WIDTH // LANES,), in_specs=_lru_specs(t),
        out_specs=pl.BlockSpec((t, LANES), lambda j: (0, j)),
        out_shape=jax.ShapeDtypeStruct((t, GROUP_WIDTH), ACT_DTYPE),
        semantics=("parallel",), name="lru_fwd")(pel, pel, conv_w, conv_b, wa, ba, wx, bx, lam)


def _lru_bwd(pel, conv_w, conv_b, wa, ba, wx, bx, lam, dmix, ride=None):
    t = pel.shape[0]

    def body(g_ref, x_ref, cw_ref, cb_ref, wa_ref, ba_ref, wx_ref, bx_ref, lam_ref, do_ref,
             dg_ref, dx_ref, dcw_ref, dcb_ref, dwa_ref, dba_ref, dwx_ref, dbx_ref, dlam_ref):
        g_in, cw, lam = g_ref[...], cw_ref[...], lam_ref[...]
        xs, xc, r, i, lsl, a, s, h, th, gelu = _lru_forward(
            x_ref[...], g_in, cw, cb_ref[...], wa_ref[...], ba_ref[...], wx_ref[...], bx_ref[...], lam)
        dout = do_ref[...]
        dgelu = 0.5 * (1.0 + th) + 0.5 * g_in * (1.0 - th * th) * GELU_C * (1.0 + 3.0 * GELU_A * g_in * g_in)
        dg_ref[...] = (dout * h * dgelu).astype(dg_ref.dtype)
        gsum = _linear_scan(_shift_up(a, 1, 0.0), dout * gelu, _shift_up)
        da = gsum * _shift_down(h, 1, 0.0)
        di = gsum * s * xc
        dla = da * a - gsum * (i * xc) * (a * a / s)
        dlam_ref[...] = jnp.sum(dla * (LRU_C * r), axis=0, keepdims=True) * jax.nn.sigmoid(-lam)
        dpr = dla * (LRU_C * lsl) * r * (1.0 - r)
        dpi = di * i * (1.0 - i)
        dxc = gsum * s * i + _dot(dpr, wa_ref[...], 1, 1) + _dot(dpi, wx_ref[...], 1, 1)
        xct = xc.T
        dwa_ref[...] = _dot(xct, dpr)
        dwx_ref[...] = _dot(xct, dpi)
        dba_ref[...] = jnp.sum(dpr, axis=0, keepdims=True)
        dbx_ref[...] = jnp.sum(dpi, axis=0, keepdims=True)
        dcb_ref[...] = jnp.sum(dxc, axis=0, keepdims=True)
        for j in range(CONV_WIDTH):
            dcw_ref[j:j + 1, :] = jnp.sum(dxc * xs[j], axis=0, keepdims=True)
        dx = cw[CONV_WIDTH - 1:CONV_WIDTH, :] * dxc
        for j in range(CONV_WIDTH - 1):
            dx = dx + cw[j:j + 1, :] * _shift_up(dxc, CONV_WIDTH - 1 - j, 0.0)
        dx_ref[...] = dx.astype(dx_ref.dtype)

    col = pl.BlockSpec((t, LANES), lambda j: (0, j))
    vec = pl.BlockSpec((1, LANES), lambda j: (0, j))
    mat = pl.BlockSpec((None, LANES, LANES), lambda j: (j, 0, 0))
    nb = GROUP_WIDTH // LANES
    vshape = jax.ShapeDtypeStruct((1, GROUP_WIDTH), F32)
    mshape = jax.ShapeDtypeStruct((nb, LANES, LANES), F32)
    return _pcall(
        body, ride, grid=(nb,),
        in_specs=_lru_specs(t) + [pl.BlockSpec((t, LANES), lambda j: (0, j + nb))],
        out_specs=[col, col, pl.BlockSpec((CONV_WIDTH, LANES), lambda j: (0, j)), vec, mat, vec, mat, vec, vec],
        out_shape=[jax.ShapeDtypeStruct((t, GROUP_WIDTH), ACT_DTYPE), jax.ShapeDtypeStruct((t, GROUP_WIDTH), ACT_DTYPE),
                   jax.ShapeDtypeStruct((CONV_WIDTH, GROUP_WIDTH), F32), vshape, mshape, vshape, mshape, vshape, vshape],
        semantics=("parallel",), name="lru_bwd")(
            pel, pel, conv_w, conv_b, wa, ba, wx, bx, lam, dmix)


def _block_diag_pairs(w):
    z = jnp.zeros((LRU_BLOCK_DIM, LRU_BLOCK_DIM), w.dtype)
    return jnp.stack([jnp.block([[w[2 * j], z], [z, w[2 * j + 1]]]) for j in range(w.shape[0] // 2)])


def _block_diag_pairs_grad(dw):
    b = LRU_BLOCK_DIM
    return jnp.stack([dw[n // 2, (n % 2) * b:(n % 2 + 1) * b, (n % 2) * b:(n % 2 + 1) * b] for n in range(2 * dw.shape[0])])


def _row_tile(r):
    return ROW_TILE if r % ROW_TILE == 0 else r


def _pair_sum(g, got, place, name):
    _, r, c = g.shape
    tile = _row_tile(r)

    def body(place_ref, a_ref, b_ref, o_ref):
        o_ref[...] = (a_ref[...].astype(F32) + b_ref[...].astype(F32)).astype(o_ref.dtype)

    blk = pl.BlockSpec((1, tile, c), lambda k, i, place_ref: (k, i, 0))
    return pl.pallas_call(
        body,
        grid_spec=pltpu.PrefetchScalarGridSpec(
            num_scalar_prefetch=1, grid=(N_CHIPS, r // tile),
            in_specs=[pl.BlockSpec((1, tile, c), lambda k, i, place_ref: (2 * k + place_ref[0], i, 0)), blk],
            out_specs=blk),
        out_shape=jax.ShapeDtypeStruct(got.shape, got.dtype),
        compiler_params=_params("parallel", "parallel"), name=name)(place, g, got)


def _adamw_update(g, w_ref, m_ref, v_ref, g_ref, d_ref, nm_ref, nv_ref):
    nm = ADAM_B1 * m_ref[...] + (1.0 - ADAM_B1) * g
    nv = ADAM_B2 * v_ref[...] + (1.0 - ADAM_B2) * jnp.square(g)
    m_hat = nm / (1.0 - ADAM_B1 ** ADAM_STEP)
    v_hat = nv / (1.0 - ADAM_B2 ** ADAM_STEP)
    g_ref[...] = g
    d_ref[...] = -ADAM_LR * (m_hat / (jnp.sqrt(v_hat) + ADAM_EPS) + ADAM_WD * w_ref[...])
    nm_ref[...] = nm
    nv_ref[...] = nv


def _adamw_sharded(parts, w, m, v, place, name):
    n_layers, r, c = w.shape
    tile = _row_tile(r)
    nb = r // tile

    def body(place_ref, *refs):
        layer = pl.program_id(0)
        g = None
        for l in range(n_layers):
            s_ref, r_ref = refs[2 * l], refs[2 * l + 1]
            g_l = s_ref[0].astype(F32) + r_ref[0].astype(F32) + r_ref[1].astype(F32) + r_ref[2].astype(F32)
            g = g_l if g is None else jnp.where(layer == l, g_l, g)
        _adamw_update(g, *refs[2 * n_layers:])

    def part_specs(l):
        rows = lambda q, i: jnp.where(q < l, 0, jnp.where(q > l, nb - 1, i))
        return [pl.BlockSpec((1, tile, c), lambda q, i, place_ref: (place_ref[1], rows(q, i), 0)),
                pl.BlockSpec((3, tile, c), lambda q, i, place_ref: (0, rows(q, i), 0))]

    in_specs, args = [], []
    for l, (s, recv) in enumerate(parts):
        in_specs += part_specs(l)
        args += [s, recv]
    blk = pl.BlockSpec((None, tile, c), lambda q, i, place_ref: (q, i, 0))
    out = jax.ShapeDtypeStruct((n_layers, r, c), F32)
    return pl.pallas_call(
        body,
        grid_spec=pltpu.PrefetchScalarGridSpec(
            num_scalar_prefetch=1, grid=(n_layers, nb), in_specs=in_specs + [blk, blk, blk],
            out_specs=[blk, blk, blk, blk]),
        out_shape=[out, out, out, out], compiler_params=_params("arbitrary", "arbitrary"), name=name)(
            place, *args, w, m, v)


def _adamw_replicated(parts, w, m, v, name):
    p, r, c = parts.shape
    tile = _row_tile(r)

    def body(p_ref, w_ref, m_ref, v_ref, *outs):
        g = p_ref[0].astype(F32)
        for k in range(1, p):
            g = g + p_ref[k].astype(F32)
        _adamw_update(g, w_ref, m_ref, v_ref, *outs)

    blk = pl.BlockSpec((tile, c), lambda i: (i, 0))
    out = jax.ShapeDtypeStruct((r, c), F32)
    return pl.pallas_call(body, grid=(r // tile,),
                          in_specs=[pl.BlockSpec((p, tile, c), lambda i: (0, i, 0)), blk, blk, blk],
                          out_specs=[blk, blk, blk, blk], out_shape=[out, out, out, out],
                          compiler_params=_params("parallel"), name=name)(parts, w, m, v)


SLAB_COLS = 1024
SHARDED = {"norm_w": 2, "w_in_even": 2, "gla_w_a_up": 2, "w_out_even": 1, "w_in_odd": 2, "conv_w": 2, "conv_b": 1,
           "lru_b_a": 1, "lru_b_x": 1, "lru_lambda": 1, "w_out_odd": 1, "w_mlp_up": 2, "w_mlp_down": 1}
REPLICATED = ["gla_b_a", "gla_norm_w", "fox_b_f", "rel_bias", "lru_w_a", "lru_w_x"]
WEIGHTS = ["norm_w", "w_in_even", "gla_w_a_up", "gla_b_a", "gla_norm_w", "fox_b_f", "w_out_even", "w_in_odd",
           "rel_bias", "conv_w", "conv_b", "lru_w_a", "lru_b_a", "lru_w_x", "lru_b_x", "lru_lambda", "w_out_odd",
           "w_mlp_up", "w_mlp_down"]
MATRICES = ("w_in_even", "w_out_even", "w_in_odd", "w_out_odd", "w_mlp_up", "w_mlp_down")
VECTORS = tuple(n for n in SHARDED if n not in MATRICES)
VEC_SLAB_ROWS = 16
REPL_SLAB_ROWS = 72
MATRIX_BLOCKS = (("w_in_even", 0), ("w_out_even", 0), ("w_in_odd", 0), ("w_out_odd", 0),
                 ("w_mlp_up", 0), ("w_mlp_up", 1), ("w_mlp_down", 0), ("w_mlp_down", 1))


def _rows_of(shape):
    n = 1
    for s in shape:
        n *= s
    return -(-n // SLAB_COLS), n


def _pack(arrays, total_rows, lead=()):
    parts, used = [], 0
    for a in arrays:
        rows, n = _rows_of(a.shape[len(lead):])
        flat = a.reshape(lead + (n,))
        flat = jnp.pad(flat, [(0, 0)] * len(lead) + [(0, rows * SLAB_COLS - n)])
        parts.append(flat.reshape(lead + (rows, SLAB_COLS)))
        used += rows
    parts.append(jnp.zeros(lead + (total_rows - used, SLAB_COLS), parts[0].dtype))
    return jnp.concatenate(parts, axis=len(lead))


def _unpack(slab, shapes, lead=()):
    out, row = [], 0
    for shape in shapes:
        rows, n = _rows_of(shape)
        seg = lax.slice_in_dim(slab, row, row + rows, axis=len(lead))
        out.append(seg.reshape(lead + (rows * SLAB_COLS,))[..., :n].reshape(lead + tuple(shape)))
        row += rows
    return out


def _join_shards(blocks, axis):
    moved = jnp.moveaxis(blocks, 0, axis)
    shape = moved.shape
    return moved.reshape(shape[:axis] + (shape[axis] * shape[axis + 1],) + shape[axis + 2:])


def _split_shards(full, axis):
    shape = full.shape
    cut = full.reshape(shape[:axis] + (N_DEV, shape[axis] // N_DEV) + shape[axis + 1:])
    return jnp.moveaxis(cut, axis, 0)


EVEN_SPLITS = (0, 256, 512, 1024, 1536, 1552, 2064, 2576, 3088, 3096)


def _even_in_split(w):
    c = [w[:, EVEN_SPLITS[k]:EVEN_SPLITS[k + 1]] for k in range(9)]
    gq, gk, gv, gr, ga, fq, fk, fv, ff = c
    padcols = lambda a: jnp.pad(a, ((0, 0), (0, LANES - a.shape[1])))
    return jnp.concatenate([gq, gk, gv, fq, fk, fv], axis=1), jnp.concatenate([gr, padcols(ga), padcols(ff)], axis=1)


def _even_in_merge(dmm, dele):
    return jnp.concatenate([dmm[:, :1024], dele[:, :512], dele[:, 512:512 + GLA_RANK], dmm[:, 1024:2560],
                            dele[:, 640:640 + ATT_HEADS]], axis=1)


def _column_shards(full):
    r, c = full.shape
    return jnp.moveaxis(full.reshape(r, N_DEV, c // N_DEV), 1, 0)


def _forward_backward(x, target, shard, vec_slab, vec_shapes, w, place):
    w = dict(w)
    g, dnorm, sums, recv = {}, {}, {}, {}
    nrm = lambda l, k: w["norm_w"][l, k][None, :]
    gather = lambda *keys: _gather_plan([shard[k] for k in keys])
    blocks = lambda r, c: (N_DEV, r // N_DEV, c)

    def pair_sum(key):
        sums[key] = _pair_sum(g[key], got[key], place, f"rs_pair_sum_{key[0]}_{key[1]}")

    got = {}

    def mlp_fwd(xin, layer, ride_up, ride_down):
        h = _norm_fwd(xin, nrm(layer, 2), out_dtype=ACT_DTYPE, name=f"norm_mlp_{layer}")
        u = _mm(h, w["w_mlp_up"][layer], out_dtype=ACT_DTYPE, tm=1024, tn=D_FF // N_DEV, b_blocked=True,
                name=f"mlp_up_{layer}", ride=ride_up)
        u, rode_up = u if ride_up is not None else (u, None)
        yv = _mm(u, w["w_mlp_down"][layer], out_dtype=F32, tm=512, tn=512, a_sqrelu=True,
                 name=f"mlp_down_{layer}", ride=ride_down)
        yv, rode_down = yv if ride_down is not None else (yv, None)
        xout = _norm_fwd(yv, nrm(layer, 3), out_dtype=F32, res=xin, name=f"norm_mlp_out_{layer}")
        return xout, (xin, h, u, yv), rode_up, rode_down

    def mlp_bwd(dxout, saved, layer, ride):
        xin, h, u, yv = saved
        k_up, k_down = ("w_mlp_up", layer), ("w_mlp_down", layer)
        dy, dnorm[(layer, 3)] = _norm_bwd(dxout, yv, nrm(layer, 3), out_dtype=ACT_DTYPE, name=f"norm_mlp_out_bwd_{layer}")
        du = _mm(dy, w["w_mlp_down"][layer], nt=True, out_dtype=ACT_DTYPE, tm=512, tn=512, drelu_of=u,
                 name=f"mlp_down_dx_{layer}", ride=ride)
        rode = None
        if ride is not None:
            du, rode = du
        g[k_down] = _mm(u, dy, ta=True, out_dtype=WIRE_DTYPE, tm=512, tn=512, a_sqrelu=True,
                        name=f"mlp_down_dw_{layer}").reshape(blocks(D_FF, D_MODEL))
        g[k_up], (got[k_down],) = _mm(h, du, ta=True, out_dtype=WIRE_DTYPE, tm=512, tn=D_FF // N_DEV, out_blocked=True,
                                      name=f"mlp_up_dw_{layer}", ride=_sibling_plan([g[k_down]]))
        w_up = jnp.moveaxis(w["w_mlp_up"][layer], 0, 1).reshape(D_MODEL, D_FF)
        dh, (got[k_up],) = _mm(du, w_up, nt=True, out_dtype=F32, tm=512, tn=512, name=f"mlp_up_dx_{layer}",
                               ride=_sibling_plan([g[k_up]]))
        pair_sum(k_down)
        pair_sum(k_up)
        dxin, dnorm[(layer, 2)] = _norm_bwd(dh, xin, nrm(layer, 2), out_dtype=F32, add=dxout, name=f"norm_mlp_bwd_{layer}")
        return dxin, rode

    first = _run_plan(_gather_plan([shard[("w_in_even", 0)], vec_slab]), "weights_all_gather_first")
    w["w_in_even"] = _join_shards(first[0], 1)
    for n, b in zip(VECTORS, _unpack(first[1], vec_shapes, lead=(N_DEV,))):
        w[n] = _join_shards(b, SHARDED[n])
    w["w_mlp_up"], w["w_mlp_down"] = [None] * DEPTH, [None] * DEPTH

    wmm_e, wel_e = _even_in_split(w["w_in_even"])
    w_up_pad = jnp.pad(w["gla_w_a_up"][0], ((0, LANES - GLA_RANK), (0, 0)))
    b_f_pad = jnp.pad(w["fox_b_f"], ((0, 0), (0, LANES - ATT_HEADS)))
    h0 = _norm_fwd(x, nrm(0, 0), out_dtype=ACT_DTYPE, name="norm_in_0")
    pmm0, (w_out_even,) = _mm(h0, wmm_e, out_dtype=ACT_DTYPE, tm=1024, tn=512, name="in_even_mm",
                              ride=gather(("w_out_even", 0)))
    pel0 = _mm(h0, wel_e, out_dtype=F32, tm=1024, tn=768, name="in_even_el")
    (out_a, states), (w["w_mlp_up"][0],) = _gla_fwd(pmm0, pel0, w_up_pad, w["gla_b_a"], w["gla_norm_w"],
                                                    ride=gather(("w_mlp_up", 0)))
    cum, cum_t = _fox_gate_fwd(pel0, b_f_pad)
    (out_b, lse_b), (w_mlp_down0, w_in_odd) = _fox_fwd(pmm0, cum, cum_t, ride=gather(("w_mlp_down", 0), ("w_in_odd", 0)))
    w["w_out_even"] = w_out_even.reshape(D_MODEL, D_MODEL)
    w["w_mlp_down"][0] = w_mlp_down0.reshape(D_FF, D_MODEL)
    w["w_in_odd"] = _join_shards(w_in_odd, 1)
    mix_in0 = jnp.concatenate([out_a, out_b], axis=1)
    mix0 = _mm(mix_in0, w["w_out_even"], out_dtype=F32, tm=1024, tn=512, name="out_even")
    x1 = _norm_fwd(mix0, nrm(0, 1), out_dtype=F32, res=x, name="norm_mix_0")
    x2, mlp0, (w_out_odd,), (w["w_mlp_up"][1],) = mlp_fwd(x1, 0, gather(("w_out_odd", 0)), gather(("w_mlp_up", 1)))
    w["w_out_odd"] = w_out_odd.reshape(D_MODEL, D_MODEL)

    w_in_o = w["w_in_odd"]
    n_mm_o = 3 * GROUP_WIDTH
    wa_bd, wx_bd = _block_diag_pairs(w["lru_w_a"][0]), _block_diag_pairs(w["lru_w_x"][0])
    base = _ca_bias_base(w["rel_bias"][0])
    h1 = _norm_fwd(x2, nrm(1, 0), out_dtype=ACT_DTYPE, name="norm_in_1")
    pmm1 = _mm(h1, w_in_o[:, :n_mm_o], out_dtype=ACT_DTYPE, tm=1024, tn=512, name="in_odd_mm")
    pel1 = _mm(h1, w_in_o[:, n_mm_o:], out_dtype=F32, tm=1024, tn=512, name="in_odd_el")
    kp = jnp.pad(pmm1[:, GROUP_WIDTH:2 * GROUP_WIDTH], ((CA_LEFT, 0), (0, 0)))
    vp = jnp.pad(pmm1[:, 2 * GROUP_WIDTH:], ((CA_LEFT, 0), (0, 0)))
    (out_c, lse_c), (w_mlp_down1,) = _ca_fwd(pmm1, kp, vp, base, ride=gather(("w_mlp_down", 1)))
    w["w_mlp_down"][1] = w_mlp_down1.reshape(D_FF, D_MODEL)
    lru_args = (pel1, w["conv_w"][0], w["conv_b"], wa_bd, w["lru_b_a"], wx_bd, w["lru_b_x"], w["lru_lambda"])
    out_d = _lru_fwd(*lru_args)
    mix_in1 = jnp.concatenate([out_c, out_d], axis=1)
    mix1 = _mm(mix_in1, w["w_out_odd"], out_dtype=F32, tm=1024, tn=512, name="out_odd")
    x3 = _norm_fwd(mix1, nrm(1, 1), out_dtype=F32, res=x2, name="norm_mix_1")
    x4, mlp1, _, _ = mlp_fwd(x3, 1, None, None)

    loss, dx4 = _loss_fwd_bwd(x4, target)

    k_oo, k_io, k_oe, k_ie = ("w_out_odd", 0), ("w_in_odd", 0), ("w_out_even", 0), ("w_in_even", 0)
    mlp_keys = lambda l: [("w_mlp_down", l), ("w_mlp_up", l)]
    dx3, _ = mlp_bwd(dx4, mlp1, 1, None)
    dmix1, dnorm[(1, 1)] = _norm_bwd(dx3, mix1, nrm(1, 1), out_dtype=ACT_DTYPE, name="norm_mix_bwd_1")
    g[k_oo] = _mm(mix_in1, dmix1, ta=True, out_dtype=WIRE_DTYPE, tm=512, tn=512, name="out_odd_dw").reshape(
        blocks(D_MODEL, D_MODEL))
    dmix_in1, (got[k_oo],) = _mm(dmix1, w["w_out_odd"], nt=True, out_dtype=F32, tm=512, tn=512, name="out_odd_dx",
                                 ride=_sibling_plan([g[k_oo]]))
    (dq_c, dkp, dvp, dbase), rode = _ca_bwd(pmm1, kp, vp, base, lse_c, dmix_in1,
                                            ride=_chip_plan([sums[k] for k in mlp_keys(1)]))
    recv.update(zip(mlp_keys(1), rode))
    pair_sum(k_oo)
    (dgate, dxin, g_conv_w, g_conv_b, dwa_bd, g_lru_b_a, dwx_bd, g_lru_b_x, g_lru_lambda), (recv[k_oo],) = _lru_bwd(
        *lru_args, dmix_in1, ride=_chip_plan([sums[k_oo]]))
    dp1 = jnp.concatenate([dq_c, dkp[CA_LEFT:].astype(ACT_DTYPE), dvp[CA_LEFT:].astype(ACT_DTYPE), dgate, dxin], axis=1)
    g[k_io] = _column_shards(_mm(h1, dp1, ta=True, out_dtype=WIRE_DTYPE, tm=512, tn=512, name="in_odd_dw"))
    dh1, (got[k_io],) = _mm(dp1, w_in_o, nt=True, out_dtype=F32, tm=512, tn=512, name="in_odd_dx",
                            ride=_sibling_plan([g[k_io]]))
    pair_sum(k_io)
    dx2, dnorm[(1, 0)] = _norm_bwd(dh1, x2, nrm(1, 0), out_dtype=F32, add=dx3, name="norm_in_bwd_1")
    g["rel_bias"] = _ca_bias_base_grad(dbase)[None]
    g["conv_w"], g["conv_b"] = g_conv_w[None], g_conv_b
    g["lru_w_a"], g["lru_w_x"] = _block_diag_pairs_grad(dwa_bd)[None], _block_diag_pairs_grad(dwx_bd)[None]
    g["lru_b_a"], g["lru_b_x"], g["lru_lambda"] = g_lru_b_a, g_lru_b_x, g_lru_lambda

    dx1, (recv[k_io],) = mlp_bwd(dx2, mlp0, 0, _chip_plan([sums[k_io]]))
    dmix0, dnorm[(0, 1)] = _norm_bwd(dx1, mix0, nrm(0, 1), out_dtype=ACT_DTYPE, name="norm_mix_bwd_0")
    g[k_oe] = _mm(mix_in0, dmix0, ta=True, out_dtype=WIRE_DTYPE, tm=512, tn=512, name="out_even_dw").reshape(
        blocks(D_MODEL, D_MODEL))
    dmix_in0, (got[k_oe],) = _mm(dmix0, w["w_out_even"], nt=True, out_dtype=F32, tm=512, tn=512, name="out_even_dx",
                                 ride=_sibling_plan([g[k_oe]]))
    (dq_a, dk_a, dv_a, dr_a, da_a, dw_up_pad, g_gla_b_a, g_gla_norm_w), rode = _gla_bwd(
        pmm0, pel0, w_up_pad, w["gla_b_a"], w["gla_norm_w"], states, dmix_in0,
        ride=_chip_plan([sums[k] for k in mlp_keys(0)]))
    recv.update(zip(mlp_keys(0), rode))
    pair_sum(k_oe)
    (dq_b, dk_b, dv_b, dcum_t, dcum_q), (recv[k_oe],) = _fox_bwd(pmm0, cum, cum_t, lse_b, out_b, dmix_in0,
                                                                 ride=_chip_plan([sums[k_oe]]))
    df_b, db_f = _fox_gate_bwd(pel0, b_f_pad, dcum_t, dcum_q)
    g["gla_w_a_up"] = dw_up_pad[:GLA_RANK][None]
    g["gla_b_a"], g["gla_norm_w"], g["fox_b_f"] = g_gla_b_a, g_gla_norm_w, db_f[:, :ATT_HEADS]
    dp0 = jnp.concatenate([dq_a, dk_a, dv_a, dq_b, dk_b.astype(ACT_DTYPE), dv_b.astype(ACT_DTYPE), dr_a, da_a, df_b],
                          axis=1)
    w_perm = jnp.concatenate([wmm_e, wel_e], axis=1)
    n_mm_e = wmm_e.shape[1]
    dw_perm, (repl_parts,) = _mm(h0, dp0, ta=True, out_dtype=WIRE_DTYPE, tm=512, tn=dp0.shape[1] // 2, name="in_even_dw",
                                 ride=_gather_plan([_pack([g[n] for n in REPLICATED], REPL_SLAB_ROWS)]))
    g[k_ie] = _column_shards(_even_in_merge(dw_perm[:, :n_mm_e], dw_perm[:, n_mm_e:]))
    dh0, (got[k_ie],) = _mm(dp0, w_perm, nt=True, out_dtype=F32, tm=512, tn=512, name="in_even_dx",
                            ride=_sibling_plan([g[k_ie]]))
    pair_sum(k_ie)
    dx0, dnorm[(0, 0)] = _norm_bwd(dh0, x, nrm(0, 0), out_dtype=F32, add=dx1, name="norm_in_bwd_0")

    g["norm_w"] = jnp.stack([jnp.concatenate([dnorm[(l, k)] for k in range(4)], axis=0) for l in range(DEPTH)])
    k_vec = ("vectors", 0)
    vec_grads = _pack([_split_shards(g[n], SHARDED[n]) for n in VECTORS], VEC_SLAB_ROWS, lead=(N_DEV,))
    g[k_vec] = vec_grads.astype(WIRE_DTYPE)
    (got[k_vec],) = _run_plan(_sibling_plan([g[k_vec]]), "rs_sibling_exchange_last")
    pair_sum(k_vec)
    recv[k_ie], recv[k_vec] = _run_plan(_chip_plan([sums[k_ie], sums[k_vec]]), "rs_chip_exchange_last")
    return loss, dx0, sums, recv, repl_parts


def kernel(x, norm_w, w_in_even, gla_w_a_up, gla_b_a, gla_norm_w, fox_b_f, w_out_even, w_in_odd, rel_bias, conv_w, conv_b, lru_w_a, lru_b_a, lru_w_x, lru_b_x, lru_lambda, w_out_odd, w_mlp_up, w_mlp_down, loss_target, m_norm_w, m_w_in_even, m_gla_w_a_up, m_gla_b_a, m_gla_norm_w, m_fox_b_f, m_w_out_even, m_w_in_odd, m_rel_bias, m_conv_w, m_conv_b, m_lru_w_a, m_lru_b_a, m_lru_w_x, m_lru_b_x, m_lru_lambda, m_w_out_odd, m_w_mlp_up, m_w_mlp_down, v_norm_w, v_w_in_even, v_gla_w_a_up, v_gla_b_a, v_gla_norm_w, v_fox_b_f, v_w_out_even, v_w_in_odd, v_rel_bias, v_conv_w, v_conv_b, v_lru_w_a, v_lru_b_a, v_lru_w_x, v_lru_b_x, v_lru_lambda, v_w_out_odd, v_w_mlp_up, v_w_mlp_down):
    wts = dict(zip(WEIGHTS, (norm_w, w_in_even, gla_w_a_up, gla_b_a, gla_norm_w, fox_b_f, w_out_even, w_in_odd, rel_bias,
                             conv_w, conv_b, lru_w_a, lru_b_a, lru_w_x, lru_b_x, lru_lambda, w_out_odd, w_mlp_up,
                             w_mlp_down)))
    mom = dict(zip(WEIGHTS, (m_norm_w, m_w_in_even, m_gla_w_a_up, m_gla_b_a, m_gla_norm_w, m_fox_b_f, m_w_out_even,
                             m_w_in_odd, m_rel_bias, m_conv_w, m_conv_b, m_lru_w_a, m_lru_b_a, m_lru_w_x, m_lru_b_x,
                             m_lru_lambda, m_w_out_odd, m_w_mlp_up, m_w_mlp_down)))
    var = dict(zip(WEIGHTS, (v_norm_w, v_w_in_even, v_gla_w_a_up, v_gla_b_a, v_gla_norm_w, v_fox_b_f, v_w_out_even,
                             v_w_in_odd, v_rel_bias, v_conv_w, v_conv_b, v_lru_w_a, v_lru_b_a, v_lru_w_x, v_lru_b_x,
                             v_lru_lambda, v_w_out_odd, v_w_mlp_up, v_w_mlp_down)))
    vec_shapes = [wts[n].shape for n in VECTORS]
    repl_shapes = [wts[n].shape for n in REPLICATED]
    place = jnp.stack([lax.axis_index("c"), 2 * lax.axis_index("x") + lax.axis_index("y")]).astype(jnp.int32)

    shard = {(n, l): wts[n][l].astype(WIRE_DTYPE) for n, l in MATRIX_BLOCKS}
    vec_slab = _pack([wts[n] for n in VECTORS], VEC_SLAB_ROWS)
    loss_blk, dx, sums, recv, repl_parts = _forward_backward(
        x[0], loss_target[0], shard, vec_slab, vec_shapes, {n: wts[n] for n in REPLICATED}, place)
    loss = lax.psum(loss_blk[0, 0], ("x", "y", "c"))

    k_vec = ("vectors", 0)

    vec_of = lambda d: _pack([d[n] for n in VECTORS], VEC_SLAB_ROWS)[None]
    upd = {n: _adamw_sharded([(sums[(n, l)], recv[(n, l)]) for l in range(wts[n].shape[0])], wts[n], mom[n], var[n],
                             place, f"adamw_{n}") for n in MATRICES}
    vec_upd = _adamw_sharded([(sums[k_vec], recv[k_vec])], vec_of(wts), vec_of(mom), vec_of(var), place, "adamw_vectors")
    rp = _adamw_replicated(repl_parts, _pack([wts[n] for n in REPLICATED], REPL_SLAB_ROWS),
                           _pack([mom[n] for n in REPLICATED], REPL_SLAB_ROWS),
                           _pack([var[n] for n in REPLICATED], REPL_SLAB_ROWS), "adamw_replicated")
    outs = []
    for kind in range(4):
        vals = dict(zip(VECTORS, _unpack(vec_upd[kind][0], vec_shapes)))
        vals.update(zip(REPLICATED, _unpack(rp[kind], repl_shapes)))
        vals.update((n, upd[n][kind]) for n in MATRICES)
        outs += [vals[n] for n in WEIGHTS]
    return (loss, dx[None], *outs)
```

```python
import functools
from typing import Callable, NamedTuple

import jax
import jax.numpy as jnp
from jax import lax
from jax.experimental import pallas as pl
from jax.experimental.pallas import tpu as pltpu

F32 = jnp.float32
MXU_DTYPE = jnp.bfloat16
ACT_DTYPE = jnp.bfloat16
WIRE_DTYPE = jnp.bfloat16

V7X_VMEM_BYTES = 64 * 1024 * 1024
VMEM_LIMIT = (V7X_VMEM_BYTES * 7) // 8
LANES = 128

D_MODEL = 1024
SEQ = 2048
DEPTH = 2
CHUNK = 64
GROUP_WIDTH = D_MODEL // 2
D_FF = 4 * D_MODEL
NORM_EPS = 1e-6
GLA_HEADS = 4
GLA_DV = GROUP_WIDTH // GLA_HEADS
GLA_DK = GLA_DV // 2
GLA_KW = GLA_HEADS * GLA_DK
GLA_RANK = 16
GLA_GATE_TAU = 16.0
HEAD_DIM = 64
ATT_HEADS = GROUP_WIDTH // HEAD_DIM
CA_LEFT = 8 * CHUNK
REL_CLIP = 128
LRU_BLOCK_DIM = 64
CONV_WIDTH = 4
LRU_C = 8.0
N_DEV = 8

ADAM_LR = 0.001
ADAM_B1 = 0.9
ADAM_B2 = 0.999
ADAM_EPS = 1e-08
ADAM_WD = 0.01
ADAM_STEP = 10

NEG = float(jnp.finfo(jnp.float32).min)
MESH = pl.DeviceIdType.MESH


def _params(*sem):
    return pltpu.CompilerParams(dimension_semantics=sem, vmem_limit_bytes=VMEM_LIMIT)


def _dot(a, b, ca=1, cb=0):
    return lax.dot_general(a.astype(MXU_DTYPE), b.astype(MXU_DTYPE), (((ca,), (cb,)), ((), ())),
                           preferred_element_type=F32)


def _dot_exact(a, b):
    return lax.dot_general(a, b, (((1,), (0,)), ((), ())), precision=lax.Precision.HIGHEST,
                           preferred_element_type=F32)


def _log_sigmoid(x):
    return jnp.minimum(x, 0.0) - jnp.log1p(jnp.exp(-jnp.abs(x)))


def _iota(shape, axis):
    return lax.broadcasted_iota(jnp.int32, shape, axis)


ANY = pl.BlockSpec(memory_space=pl.ANY)
N_CHIPS = 4


class _Plan(NamedTuple):
    ins: list
    outs: list
    sems: list
    start: Callable
    finish: Callable


def _place():
    x, y, c = lax.axis_index("x"), lax.axis_index("y"), lax.axis_index("c")
    return x, y, c, [(1 - x, y), (x, 1 - y), (1 - x, 1 - y)]


def _gather_plan(xs):
    n = len(xs)

    def parts(x_refs, out_refs, sems):
        send_sems, recv_sems, local_sems = sems
        x, y, c, chips = _place()
        me, sibling = (x, y, c), (x, y, 1 - c)

        def rows(a, px, py, pc):
            return out_refs[a].at[4 * px + 2 * py + pc]

        def copy(a, k, block, to, src=None):
            return pltpu.make_async_remote_copy(
                src_ref=rows(a, *block) if src is None else src, dst_ref=rows(a, *block),
                send_sem=send_sems.at[7 * a + k], recv_sem=recv_sems.at[7 * a + k], device_id=to, device_id_type=MESH)

        mine = [pltpu.make_async_copy(x_refs[a], rows(a, *me), local_sems.at[a]) for a in range(n)]
        first = []
        for a in range(n):
            first.append(copy(a, 0, me, sibling, src=x_refs[a]))
            first += [copy(a, 1 + j, me, (*chip, c), src=x_refs[a]) for j, chip in enumerate(chips)]
        return c, me, sibling, chips, copy, mine, first

    def start(x_refs, out_refs, sems):
        *_, mine, first = parts(x_refs, out_refs, sems)
        for cp in first + mine:
            cp.start()

    def finish(x_refs, out_refs, sems):
        c, me, sibling, chips, copy, mine, first = parts(x_refs, out_refs, sems)
        passed = []
        for j, chip in enumerate(chips):
            for a in range(n):
                copy(a, 1 + j, (*chip, c), me).wait_recv()
                passed.append(copy(a, 4 + j, (*chip, c), sibling))
                passed[-1].start()
        for a in range(n):
            copy(a, 0, sibling, me).wait_recv()
            for j, chip in enumerate(chips):
                copy(a, 4 + j, (*chip, 1 - c), me).wait_recv()
        for cp in first + passed:
            cp.wait_send()
        for cp in mine:
            cp.wait()

    return _Plan(list(xs), [jax.ShapeDtypeStruct((N_DEV,) + x.shape, x.dtype) for x in xs],
                 [pltpu.SemaphoreType.DMA((7 * n,)), pltpu.SemaphoreType.DMA((7 * n,)), pltpu.SemaphoreType.DMA((n,))],
                 start, finish)


def _exchange_plan(copies_of, ins, outs, per_array):
    n = len(ins)

    def start(in_refs, out_refs, sems):
        for cp in copies_of(in_refs, out_refs, sems):
            cp.start()

    def finish(in_refs, out_refs, sems):
        copies = copies_of(in_refs, out_refs, sems)
        for cp in copies:
            cp.wait_recv()
        for cp in copies:
            cp.wait_send()

    return _Plan(list(ins), outs, [pltpu.SemaphoreType.DMA((per_array * n,)), pltpu.SemaphoreType.DMA((per_array * n,))],
                 start, finish)


def _sibling_plan(gs):
    def copies_of(g_refs, got_refs, sems):
        x, y, c, _ = _place()
        return [pltpu.make_async_remote_copy(
            src_ref=g_refs[a].at[2 * k + (1 - c)], dst_ref=got_refs[a].at[k], send_sem=sems[0].at[N_CHIPS * a + k],
            recv_sem=sems[1].at[N_CHIPS * a + k], device_id=(x, y, 1 - c), device_id_type=MESH)
            for a in range(len(gs)) for k in range(N_CHIPS)]

    return _exchange_plan(copies_of, gs, [jax.ShapeDtypeStruct((N_CHIPS,) + g.shape[1:], g.dtype) for g in gs], N_CHIPS)


def _chip_plan(ss):
    def copies_of(s_refs, out_refs, sems):
        x, y, c, chips = _place()
        return [pltpu.make_async_remote_copy(
            src_ref=s_refs[a].at[2 * px + py], dst_ref=out_refs[a].at[j], send_sem=sems[0].at[3 * a + j],
            recv_sem=sems[1].at[3 * a + j], device_id=(px, py, c), device_id_type=MESH)
            for a in range(len(ss)) for j, (px, py) in enumerate(chips)]

    return _exchange_plan(copies_of, ss, [jax.ShapeDtypeStruct((3,) + s.shape[1:], s.dtype) for s in ss], 3)


def _run_plan(plan, name):
    n_in, n_out = len(plan.ins), len(plan.outs)

    def body(*refs):
        args = refs[:n_in], refs[n_in:n_in + n_out], refs[n_in + n_out:]
        plan.start(*args)
        plan.finish(*args)

    return pl.pallas_call(body, out_shape=plan.outs, in_specs=[ANY] * n_in, out_specs=[ANY] * n_out,
                          scratch_shapes=plan.sems, name=name)(*plan.ins)


def _pcall(body, ride, *, grid, in_specs, out_specs, out_shape, scratch_shapes=(), semantics, name):
    if ride is None:
        return pl.pallas_call(body, grid=grid, in_specs=in_specs, out_specs=out_specs, out_shape=out_shape,
                              scratch_shapes=list(scratch_shapes), compiler_params=_params(*semantics), name=name)
    single = not isinstance(out_shape, (list, tuple))
    out_specs_l, out_shape_l = ([out_specs], [out_shape]) if single else (list(out_specs), list(out_shape))
    n_in, n_out, n_scr = len(in_specs), len(out_shape_l), len(scratch_shapes)
    r_in, r_out = len(ride.ins), len(ride.outs)

    def riding(*refs):
        cuts = [n_in, r_in, n_out, r_out, n_scr]
        groups, at = [], 0
        for width in cuts:
            groups.append(refs[at:at + width])
            at += width
        ins, r_ins, outs, r_outs, scr = groups
        sems = refs[at:]
        first = functools.reduce(jnp.logical_and, [pl.program_id(d) == 0 for d in range(len(grid))])
        last = functools.reduce(jnp.logical_and, [pl.program_id(d) == grid[d] - 1 for d in range(len(grid))])

        @pl.when(first)
        def _():
            ride.start(r_ins, r_outs, sems)

        body(*ins, *outs, *scr)

        @pl.when(last)
        def _():
            ride.finish(r_ins, r_outs, sems)

    call = pl.pallas_call(
        riding, grid=grid, in_specs=list(in_specs) + [ANY] * r_in, out_specs=out_specs_l + [ANY] * r_out,
        out_shape=out_shape_l + list(ride.outs), scratch_shapes=list(scratch_shapes) + list(ride.sems),
        compiler_params=_params(*(["arbitrary"] * len(grid))), name=name)

    def run(*args):
        res = call(*args, *ride.ins)
        return (res[0] if single else list(res[:n_out])), list(res[n_out:])

    return run


def _mm(a, b, *, nt=False, ta=False, out_dtype, tm, tn, a_sqrelu=False, drelu_of=None, b_blocked=False,
        out_blocked=False, name, ride=None):
    k, m = a.shape if ta else a.shape[::-1]
    if b_blocked:
        assert not nt and b.shape[1] == k and b.shape[2] == tn
        n = b.shape[0] * tn
    else:
        n = b.shape[0] if nt else b.shape[1]
        assert (b.shape[1] if nt else b.shape[0]) == k
    tm, tn = min(tm, m), min(tn, n)
    assert m % tm == 0 and n % tn == 0

    def body(*refs):
        a_ref, b_ref = refs[0], refs[1]
        o_ref = refs[-1]
        av = a_ref[...]
        if a_sqrelu:
            av = jnp.square(jnp.maximum(av.astype(F32), 0.0))
        acc = _dot(av, b_ref[...], 0 if ta else 1, 1 if nt else 0)
        if drelu_of is not None:
            acc = acc * (2.0 * jnp.maximum(refs[2][...].astype(F32), 0.0))
        o_ref[...] = acc.astype(out_dtype)

    if b_blocked:
        b_spec = pl.BlockSpec((None, k, tn), lambda i, j: (j, 0, 0))
    elif nt:
        b_spec = pl.BlockSpec((tn, k), lambda i, j: (j, 0))
    else:
        b_spec = pl.BlockSpec((k, tn), lambda i, j: (0, j))
    a_spec = pl.BlockSpec((k, tm), lambda i, j: (0, i)) if ta else pl.BlockSpec((tm, k), lambda i, j: (i, 0))
    in_specs = [a_spec, b_spec]
    args = [a, b]
    if drelu_of is not None:
        in_specs.append(pl.BlockSpec((tm, tn), lambda i, j: (i, j)))
        args.append(drelu_of)
    if out_blocked:
        out_spec = pl.BlockSpec((None, tm, tn), lambda i, j: (j, i, 0))
        out_shape = jax.ShapeDtypeStruct((n // tn, m, tn), out_dtype)
    else:
        out_spec = pl.BlockSpec((tm, tn), lambda i, j: (i, j))
        out_shape = jax.ShapeDtypeStruct((m, n), out_dtype)
    return _pcall(body, ride, grid=(m // tm, n // tn), in_specs=in_specs, out_specs=out_spec, out_shape=out_shape,
                  semantics=("parallel", "parallel"), name=name)(*args)


def _mm_nt_blocked(a, b, *, out_dtype, tm, tn, name):
    m = a.shape[0]
    p, n, kp = b.shape
    assert a.shape[1] == p * kp and m % tm == 0 and n % tn == 0

    def body(a_ref, b_ref, o_ref, acc_ref):
        @pl.when(pl.program_id(2) == 0)
        def _():
            acc_ref[...] = jnp.zeros_like(acc_ref)

        acc_ref[...] += _dot(a_ref[...], b_ref[...], 1, 1)

        @pl.when(pl.program_id(2) == p - 1)
        def _():
            o_ref[...] = acc_ref[...].astype(out_dtype)

    return pl.pallas_call(
        body, grid=(m // tm, n // tn, p),
        in_specs=[pl.BlockSpec((tm, kp), lambda i, j, q: (i, q)), pl.BlockSpec((None, tn, kp), lambda i, j, q: (q, j, 0))],
        out_specs=pl.BlockSpec((tm, tn), lambda i, j, q: (i, j)),
        out_shape=jax.ShapeDtypeStruct((m, n), out_dtype),
        scratch_shapes=[pltpu.VMEM((tm, tn), F32)],
        compiler_params=_params("parallel", "parallel", "arbitrary"), name=name)(a, b)


ROW_TILE = 256
TM_FWD, TM_DX, TM_DW, TN = 2048, 1024, 1024, 512


def _norm_fwd(x, w, *, out_dtype, res=None, name):
    t, d = x.shape

    def body(*refs):
        x_ref, w_ref, o_ref = refs[0], refs[1], refs[-1]
        xv = x_ref[...]
        y = xv * lax.rsqrt(jnp.mean(xv * xv, axis=-1, keepdims=True) + NORM_EPS) * w_ref[...]
        if res is not None:
            y = refs[2][...] + y
        o_ref[...] = y.astype(out_dtype)

    row = pl.BlockSpec((ROW_TILE, d), lambda i: (i, 0))
    in_specs = [row, pl.BlockSpec((1, d), lambda i: (0, 0))] + ([row] if res is not None else [])
    args = [x, w] + ([res] if res is not None else [])
    return pl.pallas_call(body, grid=(t // ROW_TILE,), in_specs=in_specs, out_specs=row,
                          out_shape=jax.ShapeDtypeStruct((t, d), out_dtype),
                          compiler_params=_params("parallel"), name=name)(*args)


def _norm_bwd(dy, x, w, *, out_dtype, add=None, name):
    t, d = x.shape

    def body(*refs):
        dy_ref, x_ref, w_ref = refs[0], refs[1], refs[2]
        dx_ref, dw_ref = refs[-2], refs[-1]
        xv = x_ref[...]
        rstd = lax.rsqrt(jnp.mean(xv * xv, axis=-1, keepdims=True) + NORM_EPS)
        xhat = xv * rstd
        dyv = dy_ref[...].astype(F32)
        g = dyv * w_ref[...]
        dx = rstd * (g - xhat * jnp.mean(g * xhat, axis=-1, keepdims=True))
        if add is not None:
            dx = dx + refs[3][...]
        dx_ref[...] = dx.astype(out_dtype)

        @pl.when(pl.program_id(0) == 0)
        def _():
            dw_ref[...] = jnp.zeros_like(dw_ref)

        dw_ref[...] += jnp.sum(dyv * xhat, axis=0, keepdims=True)

    row = pl.BlockSpec((ROW_TILE, d), lambda i: (i, 0))
    vec = pl.BlockSpec((1, d), lambda i: (0, 0))
    in_specs = [row, row, vec] + ([row] if add is not None else [])
    args = [dy, x, w] + ([add] if add is not None else [])
    return pl.pallas_call(body, grid=(t // ROW_TILE,), in_specs=in_specs, out_specs=[row, vec],
                          out_shape=[jax.ShapeDtypeStruct((t, d), out_dtype), jax.ShapeDtypeStruct((1, d), F32)],
                          compiler_params=_params("arbitrary"), name=name)(*args)


def _loss_fwd_bwd(y, target):
    t, d = y.shape

    def body(y_ref, t_ref, l_ref, dy_ref):
        diff = y_ref[...] - t_ref[...]
        dy_ref[...] = diff * (1.0 / d)

        @pl.when(pl.program_id(0) == 0)
        def _():
            l_ref[...] = jnp.zeros_like(l_ref)

        l_ref[...] += 0.5 * jnp.sum(jnp.mean(diff * diff, axis=-1, keepdims=True), axis=0, keepdims=True)

    row = pl.BlockSpec((ROW_TILE, d), lambda i: (i, 0))
    return pl.pallas_call(body, grid=(t // ROW_TILE,), in_specs=[row, row],
                          out_specs=[pl.BlockSpec((8, LANES), lambda i: (0, 0)), row],
                          out_shape=[jax.ShapeDtypeStruct((8, LANES), F32), jax.ShapeDtypeStruct((t, d), F32)],
                          compiler_params=_params("arbitrary"), name="loss")(y, target)


def _gla_specs(t):
    return [pl.BlockSpec((t, GLA_KW), lambda i: (0, 0)),
            pl.BlockSpec((t, GLA_KW), lambda i: (0, 1)),
            pl.BlockSpec((t, GROUP_WIDTH), lambda i: (0, 1)),
            pl.BlockSpec((t, GROUP_WIDTH), lambda i: (0, 0)),
            pl.BlockSpec((t, LANES), lambda i: (0, 4)),
            pl.BlockSpec((LANES, GLA_KW), lambda i: (0, 0)),
            pl.BlockSpec((1, GLA_KW), lambda i: (0, 0)),
            pl.BlockSpec((1, GROUP_WIDTH), lambda i: (0, 0))]


def _gla_fwd(pmm, pel, w_up, b_a, gnorm_w, ride=None):
    t = pmm.shape[0]
    nc = t // CHUNK
    scale = GLA_DK ** -0.5

    def body(q_ref, k_ref, v_ref, r_ref, a_ref, wup_ref, ba_ref, gw_ref, o_ref, st_ref, la_scr, s_scr):
        z = _dot(a_ref[...], wup_ref[...]) + ba_ref[...]
        la_scr[...] = _log_sigmoid(z) * (1.0 / GLA_GATE_TAU)
        s_scr[...] = jnp.zeros_like(s_scr)
        tri = (_iota((CHUNK, CHUNK), 1) <= _iota((CHUNK, CHUNK), 0)).astype(F32)

        def chunk(c, carry):
            rows = pl.ds(pl.multiple_of(c * CHUNK, CHUNK), CHUNK)
            cum = _dot_exact(tri, la_scr[rows, :])
            tot = cum[CHUNK - 1:CHUNK, :]
            kd = k_ref[rows, :].astype(F32) * jnp.exp(tot - cum)
            decay = jnp.exp(tot)
            qs = q_ref[rows, :].astype(F32) * scale
            vv = v_ref[rows, :].astype(F32)
            rr = r_ref[rows, :]
            gate = rr * jax.nn.sigmoid(rr) * gw_ref[...]
            for h in range(GLA_HEADS):
                ks = slice(h * GLA_DK, (h + 1) * GLA_DK)
                vs = slice(h * GLA_DV, (h + 1) * GLA_DV)
                inc_t = _dot(vv[:, vs].T, kd[:, ks])
                s_t = s_scr[vs, :] * decay[:, ks] + inc_t
                s_scr[vs, :] = s_t
                st_ref[c, vs, :] = s_t
                o = _dot(qs[:, ks], s_t, 1, 1)
                y = o * lax.rsqrt(jnp.mean(o * o, axis=-1, keepdims=True) + NORM_EPS)
                o_ref[rows, vs] = (y * gate[:, vs]).astype(o_ref.dtype)
            return carry

        lax.fori_loop(0, nc, chunk, 0)

    return _pcall(
        body, ride, grid=(1,), in_specs=_gla_specs(t),
        out_specs=[pl.BlockSpec((t, GROUP_WIDTH), lambda i: (0, 0)),
                   pl.BlockSpec((nc, GLA_HEADS * GLA_DV, GLA_DK), lambda i: (0, 0, 0))],
        out_shape=[jax.ShapeDtypeStruct((t, GROUP_WIDTH), ACT_DTYPE),
                   jax.ShapeDtypeStruct((nc, GLA_HEADS * GLA_DV, GLA_DK), F32)],
        scratch_shapes=[pltpu.VMEM((t, GLA_KW), F32), pltpu.VMEM((GLA_HEADS * GLA_DV, GLA_DK), F32)],
        semantics=("arbitrary",), name="gla_fwd")(pmm, pmm, pmm, pel, pel, w_up, b_a, gnorm_w)


def _gla_bwd(pmm, pel, w_up, b_a, gnorm_w, states, dmix, ride=None):
    t = pmm.shape[0]
    nc = t // CHUNK
    scale = GLA_DK ** -0.5

    def body(q_ref, k_ref, v_ref, r_ref, a_ref, wup_ref, ba_ref, gw_ref, st_ref, do_ref,
             dq_ref, dk_ref, dv_ref, dr_ref, da_ref, dwup_ref, dba_ref, dgw_ref, la_scr, dz_scr, ds_scr):
        z = _dot(a_ref[...], wup_ref[...]) + ba_ref[...]
        la_scr[...] = _log_sigmoid(z) * (1.0 / GLA_GATE_TAU)
        ds_scr[...] = jnp.zeros_like(ds_scr)
        dgw_ref[...] = jnp.zeros_like(dgw_ref)
        row_i, col_i = _iota((CHUNK, CHUNK), 0), _iota((CHUNK, CHUNK), 1)
        tri = (col_i <= row_i).astype(F32)
        tri_strict = (col_i < row_i).astype(F32)

        def chunk(n, carry):
            c = nc - 1 - n
            rows = pl.ds(pl.multiple_of(c * CHUNK, CHUNK), CHUNK)
            cum = _dot_exact(tri, la_scr[rows, :])
            tot = cum[CHUNK - 1:CHUNK, :]
            e = jnp.exp(tot - cum)
            kd = k_ref[rows, :].astype(F32) * e
            decay = jnp.exp(tot)
            qs = q_ref[rows, :].astype(F32) * scale
            vv = v_ref[rows, :].astype(F32)
            rr = r_ref[rows, :]
            sig = jax.nn.sigmoid(rr)
            silu = rr * sig
            dsilu = sig * (1.0 + rr * (1.0 - sig))
            dout = do_ref[rows, :]
            gw = gw_ref[...]
            c_prev = jnp.maximum(c - 1, 0)
            has_prev = (c > 0).astype(F32)
            zc = _dot(a_ref[rows, :], wup_ref[...]) + ba_ref[...]
            dz_scale = jax.nn.sigmoid(-zc) * (1.0 / GLA_GATE_TAU)
            for h in range(GLA_HEADS):
                ks = slice(h * GLA_DK, (h + 1) * GLA_DK)
                vs = slice(h * GLA_DV, (h + 1) * GLA_DV)
                s_t = st_ref[c, vs, :]
                s_prev = st_ref[c_prev, vs, :] * has_prev
                o = _dot(qs[:, ks], s_t, 1, 1)
                rstd = lax.rsqrt(jnp.mean(o * o, axis=-1, keepdims=True) + NORM_EPS)
                y = o * rstd
                dg = dout[:, vs]
                dgw_ref[:, vs] += jnp.sum(dg * y * silu[:, vs], axis=0, keepdims=True)
                dr_ref[rows, vs] = (dg * y * gw[:, vs] * dsilu[:, vs]).astype(dr_ref.dtype)
                dy = dg * gw[:, vs] * silu[:, vs]
                d_o = rstd * (dy - y * jnp.mean(dy * y, axis=-1, keepdims=True))
                dq_ref[rows, ks] = (_dot(d_o, s_t) * scale).astype(dq_ref.dtype)
                ds_t = ds_scr[vs, :] + _dot(d_o.T, qs[:, ks])
                dv_ref[rows, vs] = _dot(kd[:, ks], ds_t, 1, 1).astype(dv_ref.dtype)
                dkd = _dot(vv[:, vs], ds_t)
                ddecay = jnp.sum(ds_t * s_prev, axis=0, keepdims=True)
                ds_scr[vs, :] = ds_t * decay[:, ks]
                dla = ddecay * decay[:, ks] + _dot_exact(tri_strict, dkd * kd[:, ks])
                dz_scr[rows, ks] = dla * dz_scale[:, ks]
                dk_ref[rows, ks] = (dkd * e[:, ks]).astype(dk_ref.dtype)
            return carry

        lax.fori_loop(0, nc, chunk, 0)
        dz = dz_scr[...]
        da_ref[...] = _dot(dz, wup_ref[...], 1, 1).astype(da_ref.dtype)
        dwup_ref[...] = _dot(a_ref[...].T, dz)
        dba_ref[...] = jnp.sum(dz, axis=0, keepdims=True)

    in_specs = _gla_specs(t) + [
        pl.BlockSpec((nc, GLA_HEADS * GLA_DV, GLA_DK), lambda i: (0, 0, 0)),
        pl.BlockSpec((t, GROUP_WIDTH), lambda i: (0, 0))]
    full = lambda r, c: pl.BlockSpec((r, c), lambda i: (0, 0))
    return _pcall(
        body, ride, grid=(1,), in_specs=in_specs,
        out_specs=[full(t, GLA_KW), full(t, GLA_KW), full(t, GROUP_WIDTH), full(t, GROUP_WIDTH), full(t, LANES),
                   full(LANES, GLA_KW), full(1, GLA_KW), full(1, GROUP_WIDTH)],
        out_shape=[jax.ShapeDtypeStruct((t, GLA_KW), ACT_DTYPE), jax.ShapeDtypeStruct((t, GLA_KW), ACT_DTYPE),
                   jax.ShapeDtypeStruct((t, GROUP_WIDTH), ACT_DTYPE), jax.ShapeDtypeStruct((t, GROUP_WIDTH), ACT_DTYPE),
                   jax.ShapeDtypeStruct((t, LANES), ACT_DTYPE), jax.ShapeDtypeStruct((LANES, GLA_KW), F32),
                   jax.ShapeDtypeStruct((1, GLA_KW), F32), jax.ShapeDtypeStruct((1, GROUP_WIDTH), F32)],
        scratch_shapes=[pltpu.VMEM((t, GLA_KW), F32), pltpu.VMEM((t, GLA_KW), F32),
                        pltpu.VMEM((GLA_HEADS * GLA_DV, GLA_DK), F32)],
        semantics=("arbitrary",), name="gla_bwd")(
            pmm, pmm, pmm, pel, pel, w_up, b_a, gnorm_w, states, dmix)


CUM_BLOCK = 256


def _fox_gate_fwd(pel, b_f):
    t = pel.shape[0]
    nb = t // CUM_BLOCK

    def body(f_ref, b_ref, cum_ref, cum_t_ref):
        tri = (_iota((CUM_BLOCK, CUM_BLOCK), 1) <= _iota((CUM_BLOCK, CUM_BLOCK), 0)).astype(F32)
        carry = jnp.zeros((1, LANES), F32)
        for blk in range(nb):
            rows = slice(blk * CUM_BLOCK, (blk + 1) * CUM_BLOCK)
            cum = _dot_exact(tri, _log_sigmoid(f_ref[rows, :] + b_ref[...])) + carry
            cum_ref[rows, :] = cum
            cum_t_ref[blk] = cum.T[:ATT_HEADS, :]
            carry = cum[CUM_BLOCK - 1:CUM_BLOCK, :]

    return pl.pallas_call(
        body, grid=(1,),
        in_specs=[pl.BlockSpec((t, LANES), lambda i: (0, 5)), pl.BlockSpec((1, LANES), lambda i: (0, 0))],
        out_specs=[pl.BlockSpec((t, LANES), lambda i: (0, 0)),
                   pl.BlockSpec((nb, ATT_HEADS, CUM_BLOCK), lambda i: (0, 0, 0))],
        out_shape=[jax.ShapeDtypeStruct((t, LANES), F32), jax.ShapeDtypeStruct((nb, ATT_HEADS, CUM_BLOCK), F32)],
        compiler_params=_params("arbitrary"), name="fox_gate_fwd")(pel, b_f)


def _fox_gate_bwd(pel, b_f, dcum_t, dcum_q):
    t = pel.shape[0]
    nb = t // CUM_BLOCK

    def body(f_ref, b_ref, dct_ref, dcq_ref, df_ref, db_ref):
        tri_up = (_iota((CUM_BLOCK, CUM_BLOCK), 1) >= _iota((CUM_BLOCK, CUM_BLOCK), 0)).astype(F32)
        carry = jnp.zeros((1, LANES), F32)
        db = jnp.zeros((1, LANES), F32)
        for blk in reversed(range(nb)):
            rows = slice(blk * CUM_BLOCK, (blk + 1) * CUM_BLOCK)
            dls = _dot_exact(tri_up, dct_ref[blk].T + dcq_ref[rows, :]) + carry
            carry = dls[0:1, :]
            df = dls * jax.nn.sigmoid(-(f_ref[rows, :] + b_ref[...]))
            df_ref[rows, :] = df.astype(df_ref.dtype)
            db = db + jnp.sum(df, axis=0, keepdims=True)
        db_ref[...] = db

    return pl.pallas_call(
        body, grid=(1,),
        in_specs=[pl.BlockSpec((t, LANES), lambda i: (0, 5)), pl.BlockSpec((1, LANES), lambda i: (0, 0)),
                  pl.BlockSpec((nb, LANES, CUM_BLOCK), lambda i: (0, 0, 0)), pl.BlockSpec((t, LANES), lambda i: (0, 0))],
        out_specs=[pl.BlockSpec((t, LANES), lambda i: (0, 0)), pl.BlockSpec((1, LANES), lambda i: (0, 0))],
        out_shape=[jax.ShapeDtypeStruct((t, LANES), ACT_DTYPE), jax.ShapeDtypeStruct((1, LANES), F32)],
        compiler_params=_params("arbitrary"), name="fox_gate_bwd")(pel, b_f, dcum_t, dcum_q)


FOX_Q_BLOCK = 256


assert FOX_Q_BLOCK == CUM_BLOCK


def _fox_scores(q_ref, k_ref, cum_ref, cum_t_ref, h, i):
    hs = slice(h * HEAD_DIM, (h + 1) * HEAD_DIM)
    nb = cum_t_ref.shape[0]
    key_gate = jnp.concatenate([cum_t_ref[kb, h:h + 1, :] for kb in range(nb)], axis=1)
    s = _dot(q_ref[:, hs], k_ref[:, hs], 1, 1) * (HEAD_DIM ** -0.5) + (cum_ref[:, h:h + 1] - key_gate)
    shape = (FOX_Q_BLOCK, nb * FOX_Q_BLOCK)
    return jnp.where(_iota(shape, 1) <= i * FOX_Q_BLOCK + _iota(shape, 0), s, NEG)


def _fox_specs(t):
    bq, nb = FOX_Q_BLOCK, t // FOX_Q_BLOCK
    return [pl.BlockSpec((bq, GROUP_WIDTH), lambda i: (i, 2)), pl.BlockSpec((t, GROUP_WIDTH), lambda i: (0, 3)),
            pl.BlockSpec((t, GROUP_WIDTH), lambda i: (0, 4)), pl.BlockSpec((bq, LANES), lambda i: (i, 0)),
            pl.BlockSpec((nb, ATT_HEADS, bq), lambda i: (0, 0, 0))]


def _fox_fwd(pmm, cum, cum_t, ride=None):
    t = pmm.shape[0]
    bq = FOX_Q_BLOCK

    def body(q_ref, k_ref, v_ref, cum_ref, cum_t_ref, o_ref, lse_ref):
        i = pl.program_id(0)
        lse_ref[...] = jnp.zeros_like(lse_ref)
        for h in range(ATT_HEADS):
            hs = slice(h * HEAD_DIM, (h + 1) * HEAD_DIM)
            s = _fox_scores(q_ref, k_ref, cum_ref, cum_t_ref, h, i)
            m = jnp.max(s, axis=-1, keepdims=True)
            p = jnp.exp(s - m)
            l = jnp.sum(p, axis=-1, keepdims=True)
            o_ref[:, hs] = (_dot(p, v_ref[:, hs]) / l).astype(o_ref.dtype)
            lse_ref[:, h:h + 1] = m + jnp.log(l)

    return _pcall(
        body, ride, grid=(t // bq,), in_specs=_fox_specs(t),
        out_specs=[pl.BlockSpec((bq, GROUP_WIDTH), lambda i: (i, 0)), pl.BlockSpec((bq, LANES), lambda i: (i, 0))],
        out_shape=[jax.ShapeDtypeStruct((t, GROUP_WIDTH), ACT_DTYPE), jax.ShapeDtypeStruct((t, LANES), F32)],
        semantics=("parallel",), name="fox_fwd")(pmm, pmm, pmm, cum, cum_t)


def _fox_bwd(pmm, cum, cum_t, lse, dmix, ride=None):
    t = pmm.shape[0]
    bq, nb = FOX_Q_BLOCK, t // FOX_Q_BLOCK
    scale = HEAD_DIM ** -0.5

    def body(q_ref, k_ref, v_ref, cum_ref, cum_t_ref, lse_ref, do_ref, dq_ref, dk_ref, dv_ref, dct_ref, dcq_ref):
        i = pl.program_id(0)

        @pl.when(i == 0)
        def _():
            dk_ref[...] = jnp.zeros_like(dk_ref)
            dv_ref[...] = jnp.zeros_like(dv_ref)
            dct_ref[...] = jnp.zeros_like(dct_ref)

        dcq_ref[...] = jnp.zeros_like(dcq_ref)
        for h in range(ATT_HEADS):
            hs = slice(h * HEAD_DIM, (h + 1) * HEAD_DIM)
            s = _fox_scores(q_ref, k_ref, cum_ref, cum_t_ref, h, i)
            p = jnp.exp(s - lse_ref[:, h:h + 1])
            do = do_ref[:, hs]
            dp = _dot(do, v_ref[:, hs], 1, 1)
            ds = p * (dp - jnp.sum(p * dp, axis=-1, keepdims=True))
            dq_ref[:, hs] = (_dot(ds, k_ref[:, hs]) * scale).astype(dq_ref.dtype)
            dk_ref[:, hs] += _dot(ds, q_ref[:, hs], 0, 0) * scale
            dv_ref[:, hs] += _dot(p, do, 0, 0)
            key_side = -jnp.sum(ds, axis=0, keepdims=True)
            for kb in range(nb):
                dct_ref[kb, h:h + 1, :] += key_side[:, kb * bq:(kb + 1) * bq]
            dcq_ref[:, h:h + 1] = jnp.sum(ds, axis=1, keepdims=True)

    whole = pl.BlockSpec((t, GROUP_WIDTH), lambda i: (0, 0))
    return _pcall(
        body, ride, grid=(t // bq,),
        in_specs=_fox_specs(t) + [pl.BlockSpec((bq, LANES), lambda i: (i, 0)),
                                  pl.BlockSpec((bq, GROUP_WIDTH), lambda i: (i, 1))],
        out_specs=[pl.BlockSpec((bq, GROUP_WIDTH), lambda i: (i, 0)), whole, whole,
                   pl.BlockSpec((nb, LANES, bq), lambda i: (0, 0, 0)), pl.BlockSpec((bq, LANES), lambda i: (i, 0))],
        out_shape=[jax.ShapeDtypeStruct((t, GROUP_WIDTH), ACT_DTYPE), jax.ShapeDtypeStruct((t, GROUP_WIDTH), F32),
                   jax.ShapeDtypeStruct((t, GROUP_WIDTH), F32), jax.ShapeDtypeStruct((nb, LANES, bq), F32),
                   jax.ShapeDtypeStruct((t, LANES), F32)],
        semantics=("arbitrary",), name="fox_bwd")(pmm, pmm, pmm, cum, cum_t, lse, dmix)


CA_Q_BLOCK = 4 * CHUNK
CA_WINDOW = CA_Q_BLOCK + CA_LEFT
CA_BASE = 1024


def _ca_bias_base(rel_bias):
    n = rel_bias.shape[0]
    flat = CA_Q_BLOCK + CA_LEFT - REL_CLIP
    tail = CA_BASE - flat - (2 * REL_CLIP + 1)
    return jnp.concatenate([jnp.broadcast_to(rel_bias[:, 2 * REL_CLIP:], (n, flat)), rel_bias[:, ::-1],
                            jnp.broadcast_to(rel_bias[:, :1], (n, tail))], axis=1)


def _ca_bias_base_grad(dbase):
    flat = CA_Q_BLOCK + CA_LEFT - REL_CLIP
    mid = dbase[:, flat:flat + 2 * REL_CLIP + 1][:, ::-1]
    lo = jnp.sum(dbase[:, flat + 2 * REL_CLIP + 1:], axis=1, keepdims=True)
    hi = jnp.sum(dbase[:, :flat], axis=1, keepdims=True)
    pad = jnp.zeros((dbase.shape[0], 2 * REL_CLIP - 1), F32)
    return mid + jnp.concatenate([lo, pad, hi], axis=1)


def _ca_mask(i):
    r, j = _iota((CA_Q_BLOCK, CA_WINDOW), 0), _iota((CA_Q_BLOCK, CA_WINDOW), 1)
    rc, jc = r // CHUNK, j // CHUNK
    return (jc >= rc) & (jc <= rc + CA_LEFT // CHUNK) & (i * CA_Q_BLOCK + j >= CA_LEFT)


def _ca_scores(q_ref, kp_ref, base_ref, win, h, mask):
    hs = slice(h * HEAD_DIM, (h + 1) * HEAD_DIM)
    s = _dot(q_ref[:, hs], kp_ref[win, hs], 1, 1) * (HEAD_DIM ** -0.5)
    rows = jnp.broadcast_to(base_ref[h:h + 1, :], (CA_Q_BLOCK, CA_BASE))
    bias = pltpu.roll(rows, CA_BASE - CA_Q_BLOCK, 1, stride=1, stride_axis=0)[:, :CA_WINDOW]
    return jnp.where(mask, s + bias, NEG)


def _ca_fwd(pmm, kp, vp, base, ride=None):
    t = pmm.shape[0]

    def body(q_ref, kp_ref, vp_ref, base_ref, o_ref, lse_ref):
        i = pl.program_id(0)
        win = pl.ds(pl.multiple_of(i * CA_Q_BLOCK, CA_Q_BLOCK), CA_WINDOW)
        mask = _ca_mask(i)
        lse_ref[...] = jnp.zeros_like(lse_ref)
        for h in range(ATT_HEADS):
            hs = slice(h * HEAD_DIM, (h + 1) * HEAD_DIM)
            s = _ca_scores(q_ref, kp_ref, base_ref, win, h, mask)
            m = jnp.max(s, axis=-1, keepdims=True)
            p = jnp.exp(s - m)
            l = jnp.sum(p, axis=-1, keepdims=True)
            o_ref[:, hs] = (_dot(p, vp_ref[win, hs]) / l).astype(o_ref.dtype)
            lse_ref[:, h:h + 1] = m + jnp.log(l)

    padded = pl.BlockSpec((t + CA_LEFT, GROUP_WIDTH), lambda i: (0, 0))
    return _pcall(
        body, ride, grid=(t // CA_Q_BLOCK,),
        in_specs=[pl.BlockSpec((CA_Q_BLOCK, GROUP_WIDTH), lambda i: (i, 0)), padded, padded,
                  pl.BlockSpec((ATT_HEADS, CA_BASE), lambda i: (0, 0))],
        out_specs=[pl.BlockSpec((CA_Q_BLOCK, GROUP_WIDTH), lambda i: (i, 0)),
                   pl.BlockSpec((CA_Q_BLOCK, LANES), lambda i: (i, 0))],
        out_shape=[jax.ShapeDtypeStruct((t, GROUP_WIDTH), ACT_DTYPE), jax.ShapeDtypeStruct((t, LANES), F32)],
        semantics=("parallel",), name="ca_fwd")(pmm, kp, vp, base)


def _ca_bwd(pmm, kp, vp, base, lse, dmix, ride=None):
    t = pmm.shape[0]
    scale = HEAD_DIM ** -0.5

    def body(q_ref, kp_ref, vp_ref, base_ref, lse_ref, do_ref, dq_ref, dkp_ref, dvp_ref, dbase_ref):
        i = pl.program_id(0)

        @pl.when(i == 0)
        def _():
            dkp_ref[...] = jnp.zeros_like(dkp_ref)
            dvp_ref[...] = jnp.zeros_like(dvp_ref)
            dbase_ref[...] = jnp.zeros_like(dbase_ref)

        win = pl.ds(pl.multiple_of(i * CA_Q_BLOCK, CA_Q_BLOCK), CA_WINDOW)
        mask = _ca_mask(i)
        flip = (_iota((CA_Q_BLOCK, CA_Q_BLOCK), 0) + _iota((CA_Q_BLOCK, CA_Q_BLOCK), 1) == CA_Q_BLOCK - 1).astype(F32)
        for h in range(ATT_HEADS):
            hs = slice(h * HEAD_DIM, (h + 1) * HEAD_DIM)
            s = _ca_scores(q_ref, kp_ref, base_ref, win, h, mask)
            p = jnp.exp(s - lse_ref[:, h:h + 1])
            do = do_ref[:, hs]
            dp = _dot(do, vp_ref[win, hs], 1, 1)
            ds = p * (dp - jnp.sum(p * dp, axis=-1, keepdims=True))
            dq_ref[:, hs] = (_dot(ds, kp_ref[win, hs]) * scale).astype(dq_ref.dtype)
            dkp_ref[win, hs] += _dot(ds, q_ref[:, hs], 0, 0) * scale
            dvp_ref[win, hs] += _dot(p, do, 0, 0)
            rev = jnp.concatenate([_dot(flip, ds), jnp.zeros((CA_Q_BLOCK, CA_BASE - CA_WINDOW), F32)], axis=1)
            lined = pltpu.roll(rev, 1, 1, stride=1, stride_axis=0)
            dbase_ref[h:h + 1, :] += jnp.sum(lined, axis=0, keepdims=True)

    padded = pl.BlockSpec((t + CA_LEFT, GROUP_WIDTH), lambda i: (0, 0))
    return _pcall(
        body, ride, grid=(t // CA_Q_BLOCK,),
        in_specs=[pl.BlockSpec((CA_Q_BLOCK, GROUP_WIDTH), lambda i: (i, 0)), padded, padded,
                  pl.BlockSpec((ATT_HEADS, CA_BASE), lambda i: (0, 0)),
                  pl.BlockSpec((CA_Q_BLOCK, LANES), lambda i: (i, 0)),
                  pl.BlockSpec((CA_Q_BLOCK, GROUP_WIDTH), lambda i: (i, 0))],
        out_specs=[pl.BlockSpec((CA_Q_BLOCK, GROUP_WIDTH), lambda i: (i, 0)), padded, padded,
                   pl.BlockSpec((ATT_HEADS, CA_BASE), lambda i: (0, 0))],
        out_shape=[jax.ShapeDtypeStruct((t, GROUP_WIDTH), ACT_DTYPE),
                   jax.ShapeDtypeStruct((t + CA_LEFT, GROUP_WIDTH), F32),
                   jax.ShapeDtypeStruct((t + CA_LEFT, GROUP_WIDTH), F32),
                   jax.ShapeDtypeStruct((ATT_HEADS, CA_BASE), F32)],
        semantics=("arbitrary",), name="ca_bwd")(pmm, kp, vp, base, lse, dmix)


GELU_C = 0.7978845608028654
GELU_A = 0.044715


def _shift_down(v, k, fill):
    return jnp.where(_iota(v.shape, 0) >= k, pltpu.roll(v, k, 0), fill)


def _shift_up(v, k, fill):
    t = v.shape[0]
    return jnp.where(_iota(v.shape, 0) < t - k, pltpu.roll(v, t - k, 0), fill)


def _linear_scan(a, b, shift):
    k = 1
    while k < a.shape[0]:
        b = a * shift(b, k, 0.0) + b
        a = a * shift(a, k, 1.0)
        k *= 2
    return b


def _neg_expm1(y):
    series = -y * (1.0 + y * (0.5 + y * (1.0 / 6.0 + y * (1.0 / 24.0 + y * (1.0 / 120.0)))))
    return jnp.where(y > -0.1, series, 1.0 - jnp.exp(y))


def _lru_forward(x, g_in, cw, cb, wa, ba, wx, bx, lam):
    xs = [_shift_down(x, CONV_WIDTH - 1 - j, 0.0) for j in range(CONV_WIDTH - 1)] + [x]
    xc = cb + sum(cw[j:j + 1, :] * xs[j] for j in range(CONV_WIDTH))
    r = jax.nn.sigmoid(_dot(xc, wa) + ba)
    i = jax.nn.sigmoid(_dot(xc, wx) + bx)
    lsl = _log_sigmoid(lam)
    la = LRU_C * r * lsl
    a = jnp.exp(la)
    s = jnp.sqrt(_neg_expm1(2.0 * la))
    h = _linear_scan(a, s * (i * xc), _shift_down)
    u = GELU_C * (g_in + GELU_A * g_in * g_in * g_in)
    th = jnp.tanh(u)
    gelu = 0.5 * g_in * (1.0 + th)
    return xs, xc, r, i, lsl, a, s, h, th, gelu


def _lru_specs(t):
    col = lambda off: pl.BlockSpec((t, LANES), lambda j: (0, j + off))
    vec = pl.BlockSpec((1, LANES), lambda j: (0, j))
    mat = pl.BlockSpec((None, LANES, LANES), lambda j: (j, 0, 0))
    return [col(0), col(GROUP_WIDTH // LANES), pl.BlockSpec((CONV_WIDTH, LANES), lambda j: (0, j)),
            vec, mat, vec, mat, vec, vec]


def _lru_fwd(pel, conv_w, conv_b, wa, ba, wx, bx, lam, ride=None):
    t = pel.shape[0]

    def body(g_ref, x_ref, cw_ref, cb_ref, wa_ref, ba_ref, wx_ref, bx_ref, lam_ref, o_ref):
        res = _lru_forward(x_ref[...], g_ref[...], cw_ref[...], cb_ref[...], wa_ref[...], ba_ref[...],
                           wx_ref[...], bx_ref[...], lam_ref[...])
        o_ref[...] = (res[7] * res[9]).astype(o_ref.dtype)

    return _pcall(
        body, ride, grid=(GROUP_WIDTH // LANES,), in_specs=_lru_specs(t),
        out_specs=pl.BlockSpec((t, LANES), lambda j: (0, j)),
        out_shape=jax.ShapeDtypeStruct((t, GROUP_WIDTH), ACT_DTYPE),
        semantics=("parallel",), name="lru_fwd")(pel, pel, conv_w, conv_b, wa, ba, wx, bx, lam)


def _lru_bwd(pel, conv_w, conv_b, wa, ba, wx, bx, lam, dmix, ride=None):
    t = pel.shape[0]

    def body(g_ref, x_ref, cw_ref, cb_ref, wa_ref, ba_ref, wx_ref, bx_ref, lam_ref, do_ref,
             dg_ref, dx_ref, dcw_ref, dcb_ref, dwa_ref, dba_ref, dwx_ref, dbx_ref, dlam_ref):
        g_in, cw, lam = g_ref[...], cw_ref[...], lam_ref[...]
        xs, xc, r, i, lsl, a, s, h, th, gelu = _lru_forward(
            x_ref[...], g_in, cw, cb_ref[...], wa_ref[...], ba_ref[...], wx_ref[...], bx_ref[...], lam)
        dout = do_ref[...]
        dgelu = 0.5 * (1.0 + th) + 0.5 * g_in * (1.0 - th * th) * GELU_C * (1.0 + 3.0 * GELU_A * g_in * g_in)
        dg_ref[...] = (dout * h * dgelu).astype(dg_ref.dtype)
        gsum = _linear_scan(_shift_up(a, 1, 0.0), dout * gelu, _shift_up)
        da = gsum * _shift_down(h, 1, 0.0)
        di = gsum * s * xc
        dla = da * a - gsum * (i * xc) * (a * a / s)
        dlam_ref[...] = jnp.sum(dla * (LRU_C * r), axis=0, keepdims=True) * jax.nn.sigmoid(-lam)
        dpr = dla * (LRU_C * lsl) * r * (1.0 - r)
        dpi = di * i * (1.0 - i)
        dxc = gsum * s * i + _dot(dpr, wa_ref[...], 1, 1) + _dot(dpi, wx_ref[...], 1, 1)
        xct = xc.T
        dwa_ref[...] = _dot(xct, dpr)
        dwx_ref[...] = _dot(xct, dpi)
        dba_ref[...] = jnp.sum(dpr, axis=0, keepdims=True)
        dbx_ref[...] = jnp.sum(dpi, axis=0, keepdims=True)
        dcb_ref[...] = jnp.sum(dxc, axis=0, keepdims=True)
        for j in range(CONV_WIDTH):
            dcw_ref[j:j + 1, :] = jnp.sum(dxc * xs[j], axis=0, keepdims=True)
        dx = cw[CONV_WIDTH - 1:CONV_WIDTH, :] * dxc
        for j in range(CONV_WIDTH - 1):
            dx = dx + cw[j:j + 1, :] * _shift_up(dxc, CONV_WIDTH - 1 - j, 0.0)
        dx_ref[...] = dx.astype(dx_ref.dtype)

    col = pl.BlockSpec((t, LANES), lambda j: (0, j))
    vec = pl.BlockSpec((1, LANES), lambda j: (0, j))
    mat = pl.BlockSpec((None, LANES, LANES), lambda j: (j, 0, 0))
    nb = GROUP_WIDTH // LANES
    vshape = jax.ShapeDtypeStruct((1, GROUP_WIDTH), F32)
    mshape = jax.ShapeDtypeStruct((nb, LANES, LANES), F32)
    return _pcall(
        body, ride, grid=(nb,),
        in_specs=_lru_specs(t) + [pl.BlockSpec((t, LANES), lambda j: (0, j + nb))],
        out_specs=[col, col, pl.BlockSpec((CONV_WIDTH, LANES), lambda j: (0, j)), vec, mat, vec, mat, vec, vec],
        out_shape=[jax.ShapeDtypeStruct((t, GROUP_WIDTH), ACT_DTYPE), jax.ShapeDtypeStruct((t, GROUP_WIDTH), ACT_DTYPE),
                   jax.ShapeDtypeStruct((CONV_WIDTH, GROUP_WIDTH), F32), vshape, mshape, vshape, mshape, vshape, vshape],
        semantics=("parallel",), name="lru_bwd")(
            pel, pel, conv_w, conv_b, wa, ba, wx, bx, lam, dmix)


def _block_diag_pairs(w):
    z = jnp.zeros((LRU_BLOCK_DIM, LRU_BLOCK_DIM), w.dtype)
    return jnp.stack([jnp.block([[w[2 * j], z], [z, w[2 * j + 1]]]) for j in range(w.shape[0] // 2)])


def _block_diag_pairs_grad(dw):
    b = LRU_BLOCK_DIM
    return jnp.stack([dw[n // 2, (n % 2) * b:(n % 2 + 1) * b, (n % 2) * b:(n % 2 + 1) * b] for n in range(2 * dw.shape[0])])


def _row_tile(r):
    return ROW_TILE if r % ROW_TILE == 0 else r


def _pair_sum(g, got, place, name):
    _, r, c = g.shape
    tile = _row_tile(r)

    def body(place_ref, a_ref, b_ref, o_ref):
        o_ref[...] = (a_ref[...].astype(F32) + b_ref[...].astype(F32)).astype(o_ref.dtype)

    blk = pl.BlockSpec((1, tile, c), lambda k, i, place_ref: (k, i, 0))
    return pl.pallas_call(
        body,
        grid_spec=pltpu.PrefetchScalarGridSpec(
            num_scalar_prefetch=1, grid=(N_CHIPS, r // tile),
            in_specs=[pl.BlockSpec((1, tile, c), lambda k, i, place_ref: (2 * k + place_ref[0], i, 0)), blk],
            out_specs=blk),
        out_shape=jax.ShapeDtypeStruct(got.shape, got.dtype),
        compiler_params=_params("parallel", "parallel"), name=name)(place, g, got)


def _adamw_update(g, w_ref, m_ref, v_ref, g_ref, d_ref, nm_ref, nv_ref):
    nm = ADAM_B1 * m_ref[...] + (1.0 - ADAM_B1) * g
    nv = ADAM_B2 * v_ref[...] + (1.0 - ADAM_B2) * jnp.square(g)
    m_hat = nm / (1.0 - ADAM_B1 ** ADAM_STEP)
    v_hat = nv / (1.0 - ADAM_B2 ** ADAM_STEP)
    g_ref[...] = g
    d_ref[...] = -ADAM_LR * (m_hat / (jnp.sqrt(v_hat) + ADAM_EPS) + ADAM_WD * w_ref[...])
    nm_ref[...] = nm
    nv_ref[...] = nv


def _adamw_sharded(parts, w, m, v, place, name):
    n_layers, r, c = w.shape
    tile = _row_tile(r)
    nb = r // tile

    def body(place_ref, *refs):
        layer = pl.program_id(0)
        g = None
        for l in range(n_layers):
            s_ref, r_ref = refs[2 * l], refs[2 * l + 1]
            g_l = s_ref[0].astype(F32) + r_ref[0].astype(F32) + r_ref[1].astype(F32) + r_ref[2].astype(F32)
            g = g_l if g is None else jnp.where(layer == l, g_l, g)
        _adamw_update(g, *refs[2 * n_layers:])

    def part_specs(l):
        rows = lambda q, i: jnp.where(q < l, 0, jnp.where(q > l, nb - 1, i))
        return [pl.BlockSpec((1, tile, c), lambda q, i, place_ref: (place_ref[1], rows(q, i), 0)),
                pl.BlockSpec((3, tile, c), lambda q, i, place_ref: (0, rows(q, i), 0))]

    in_specs, args = [], []
    for l, (s, recv) in enumerate(parts):
        in_specs += part_specs(l)
        args += [s, recv]
    blk = pl.BlockSpec((None, tile, c), lambda q, i, place_ref: (q, i, 0))
    out = jax.ShapeDtypeStruct((n_layers, r, c), F32)
    return pl.pallas_call(
        body,
        grid_spec=pltpu.PrefetchScalarGridSpec(
            num_scalar_prefetch=1, grid=(n_layers, nb), in_specs=in_specs + [blk, blk, blk],
            out_specs=[blk, blk, blk, blk]),
        out_shape=[out, out, out, out], compiler_params=_params("arbitrary", "arbitrary"), name=name)(
            place, *args, w, m, v)


def _adamw_replicated(parts, w, m, v, name):
    p, r, c = parts.shape
    tile = _row_tile(r)

    def body(p_ref, w_ref, m_ref, v_ref, *outs):
        g = p_ref[0].astype(F32)
        for k in range(1, p):
            g = g + p_ref[k].astype(F32)
        _adamw_update(g, w_ref, m_ref, v_ref, *outs)

    blk = pl.BlockSpec((tile, c), lambda i: (i, 0))
    out = jax.ShapeDtypeStruct((r, c), F32)
    return pl.pallas_call(body, grid=(r // tile,),
                          in_specs=[pl.BlockSpec((p, tile, c), lambda i: (0, i, 0)), blk, blk, blk],
                          out_specs=[blk, blk, blk, blk], out_shape=[out, out, out, out],
                          compiler_params=_params("parallel"), name=name)(parts, w, m, v)


SLAB_COLS = 1024
SHARDED = {"norm_w": 2, "w_in_even": 2, "gla_w_a_up": 2, "w_out_even": 1, "w_in_odd": 2, "conv_w": 2, "conv_b": 1,
           "lru_b_a": 1, "lru_b_x": 1, "lru_lambda": 1, "w_out_odd": 1, "w_mlp_up": 2, "w_mlp_down": 1}
REPLICATED = ["gla_b_a", "gla_norm_w", "fox_b_f", "rel_bias", "lru_w_a", "lru_w_x"]
WEIGHTS = ["norm_w", "w_in_even", "gla_w_a_up", "gla_b_a", "gla_norm_w", "fox_b_f", "w_out_even", "w_in_odd",
           "rel_bias", "conv_w", "conv_b", "lru_w_a", "lru_b_a", "lru_w_x", "lru_b_x", "lru_lambda", "w_out_odd",
           "w_mlp_up", "w_mlp_down"]
MATRICES = ("w_in_even", "w_out_even", "w_in_odd", "w_out_odd", "w_mlp_up", "w_mlp_down")
VECTORS = tuple(n for n in SHARDED if n not in MATRICES)
VEC_SLAB_ROWS = 16
REPL_SLAB_ROWS = 72
MATRIX_BLOCKS = (("w_in_even", 0), ("w_out_even", 0), ("w_in_odd", 0), ("w_out_odd", 0),
                 ("w_mlp_up", 0), ("w_mlp_up", 1), ("w_mlp_down", 0), ("w_mlp_down", 1))


def _rows_of(shape):
    n = 1
    for s in shape:
        n *= s
    return -(-n // SLAB_COLS), n


def _pack(arrays, total_rows, lead=()):
    parts, used = [], 0
    for a in arrays:
        rows, n = _rows_of(a.shape[len(lead):])
        flat = a.reshape(lead + (n,))
        flat = jnp.pad(flat, [(0, 0)] * len(lead) + [(0, rows * SLAB_COLS - n)])
        parts.append(flat.reshape(lead + (rows, SLAB_COLS)))
        used += rows
    parts.append(jnp.zeros(lead + (total_rows - used, SLAB_COLS), parts[0].dtype))
    return jnp.concatenate(parts, axis=len(lead))


def _unpack(slab, shapes, lead=()):
    out, row = [], 0
    for shape in shapes:
        rows, n = _rows_of(shape)
        seg = lax.slice_in_dim(slab, row, row + rows, axis=len(lead))
        out.append(seg.reshape(lead + (rows * SLAB_COLS,))[..., :n].reshape(lead + tuple(shape)))
        row += rows
    return out


def _join_shards(blocks, axis):
    moved = jnp.moveaxis(blocks, 0, axis)
    shape = moved.shape
    return moved.reshape(shape[:axis] + (shape[axis] * shape[axis + 1],) + shape[axis + 2:])


def _split_shards(full, axis):
    shape = full.shape
    cut = full.reshape(shape[:axis] + (N_DEV, shape[axis] // N_DEV) + shape[axis + 1:])
    return jnp.moveaxis(cut, axis, 0)


EVEN_SPLITS = (0, 256, 512, 1024, 1536, 1552, 2064, 2576, 3088, 3096)


def _even_in_split(w):
    c = [w[:, EVEN_SPLITS[k]:EVEN_SPLITS[k + 1]] for k in range(9)]
    gq, gk, gv, gr, ga, fq, fk, fv, ff = c
    padcols = lambda a: jnp.pad(a, ((0, 0), (0, LANES - a.shape[1])))
    return jnp.concatenate([gq, gk, gv, fq, fk, fv], axis=1), jnp.concatenate([gr, padcols(ga), padcols(ff)], axis=1)


def _even_in_merge(dmm, dele):
    return jnp.concatenate([dmm[:, :1024], dele[:, :512], dele[:, 512:512 + GLA_RANK], dmm[:, 1024:2560],
                            dele[:, 640:640 + ATT_HEADS]], axis=1)


def _column_shards(full):
    r, c = full.shape
    return jnp.moveaxis(full.reshape(r, N_DEV, c // N_DEV), 1, 0)


def _forward_backward(x, target, shard, vec_slab, vec_shapes, w, place):
    w = dict(w)
    g, dnorm, sums, recv = {}, {}, {}, {}
    nrm = lambda l, k: w["norm_w"][l, k][None, :]
    gather = lambda *keys: _gather_plan([shard[k] for k in keys])
    blocks = lambda r, c: (N_DEV, r // N_DEV, c)

    def pair_sum(key):
        sums[key] = _pair_sum(g[key], got[key], place, f"rs_pair_sum_{key[0]}_{key[1]}")

    got = {}

    def mlp_fwd(xin, layer, ride_up, ride_down):
        h = _norm_fwd(xin, nrm(layer, 2), out_dtype=ACT_DTYPE, name=f"norm_mlp_{layer}")
        u = _mm(h, w["w_mlp_up"][layer], out_dtype=ACT_DTYPE, tm=TM_FWD, tn=D_FF // N_DEV, b_blocked=True,
                name=f"mlp_up_{layer}", ride=ride_up)
        u, rode_up = u if ride_up is not None else (u, None)
        yv = _mm(u, w["w_mlp_down"][layer], out_dtype=F32, tm=TM_DX, tn=TN, a_sqrelu=True,
                 name=f"mlp_down_{layer}", ride=ride_down)
        yv, rode_down = yv if ride_down is not None else (yv, None)
        xout = _norm_fwd(yv, nrm(layer, 3), out_dtype=F32, res=xin, name=f"norm_mlp_out_{layer}")
        return xout, (xin, h, u, yv), rode_up, rode_down

    def mlp_bwd(dxout, saved, layer, ride):
        xin, h, u, yv = saved
        k_up, k_down = ("w_mlp_up", layer), ("w_mlp_down", layer)
        dy, dnorm[(layer, 3)] = _norm_bwd(dxout, yv, nrm(layer, 3), out_dtype=ACT_DTYPE, name=f"norm_mlp_out_bwd_{layer}")
        du = _mm(dy, w["w_mlp_down"][layer], nt=True, out_dtype=ACT_DTYPE, tm=TM_DX, tn=TN, drelu_of=u,
                 name=f"mlp_down_dx_{layer}", ride=ride)
        rode = None
        if ride is not None:
            du, rode = du
        g[k_down] = _mm(u, dy, ta=True, out_dtype=WIRE_DTYPE, tm=TM_DW, tn=TN, a_sqrelu=True,
                        name=f"mlp_down_dw_{layer}").reshape(blocks(D_FF, D_MODEL))
        g[k_up] = _mm(h, du, ta=True, out_dtype=WIRE_DTYPE, tm=TM_DW, tn=D_FF // N_DEV, out_blocked=True,
                      name=f"mlp_up_dw_{layer}")
        w_up = jnp.moveaxis(w["w_mlp_up"][layer], 0, 1).reshape(D_MODEL, D_FF)
        dh, (got[k_down], got[k_up]) = _mm(du, w_up, nt=True, out_dtype=F32, tm=TM_DX, tn=TN, name=f"mlp_up_dx_{layer}",
                                           ride=_sibling_plan([g[k_down], g[k_up]]))
        pair_sum(k_down)
        pair_sum(k_up)
        dxin, dnorm[(layer, 2)] = _norm_bwd(dh, xin, nrm(layer, 2), out_dtype=F32, add=dxout, name=f"norm_mlp_bwd_{layer}")
        return dxin, rode

    first = _run_plan(_gather_plan([shard[("w_in_even", 0)], vec_slab]), "weights_all_gather_first")
    w["w_in_even"] = _join_shards(first[0], 1)
    for n, b in zip(VECTORS, _unpack(first[1], vec_shapes, lead=(N_DEV,))):
        w[n] = _join_shards(b, SHARDED[n])
    w["w_mlp_up"], w["w_mlp_down"] = [None] * DEPTH, [None] * DEPTH

    wmm_e, wel_e = _even_in_split(w["w_in_even"])
    w_up_pad = jnp.pad(w["gla_w_a_up"][0], ((0, LANES - GLA_RANK), (0, 0)))
    b_f_pad = jnp.pad(w["fox_b_f"], ((0, 0), (0, LANES - ATT_HEADS)))
    h0 = _norm_fwd(x, nrm(0, 0), out_dtype=ACT_DTYPE, name="norm_in_0")
    pmm0, (w_out_even,) = _mm(h0, wmm_e, out_dtype=ACT_DTYPE, tm=TM_FWD, tn=TN, name="in_even_mm",
                              ride=gather(("w_out_even", 0)))
    pel0 = _mm(h0, wel_e, out_dtype=F32, tm=TM_FWD, tn=768, name="in_even_el")
    (out_a, states), (w["w_mlp_up"][0],) = _gla_fwd(pmm0, pel0, w_up_pad, w["gla_b_a"], w["gla_norm_w"],
                                                    ride=gather(("w_mlp_up", 0)))
    cum, cum_t = _fox_gate_fwd(pel0, b_f_pad)
    (out_b, lse_b), (w_mlp_down0, w_in_odd) = _fox_fwd(pmm0, cum, cum_t, ride=gather(("w_mlp_down", 0), ("w_in_odd", 0)))
    w["w_out_even"] = w_out_even.reshape(D_MODEL, D_MODEL)
    w["w_mlp_down"][0] = w_mlp_down0.reshape(D_FF, D_MODEL)
    w["w_in_odd"] = _join_shards(w_in_odd, 1)
    mix_in0 = jnp.concatenate([out_a, out_b], axis=1)
    mix0 = _mm(mix_in0, w["w_out_even"], out_dtype=F32, tm=TM_FWD, tn=TN, name="out_even")
    x1 = _norm_fwd(mix0, nrm(0, 1), out_dtype=F32, res=x, name="norm_mix_0")
    x2, mlp0, (w_out_odd,), (w["w_mlp_up"][1],) = mlp_fwd(x1, 0, gather(("w_out_odd", 0)), gather(("w_mlp_up", 1)))
    w["w_out_odd"] = w_out_odd.reshape(D_MODEL, D_MODEL)

    w_in_o = w["w_in_odd"]
    n_mm_o = 3 * GROUP_WIDTH
    wa_bd, wx_bd = _block_diag_pairs(w["lru_w_a"][0]), _block_diag_pairs(w["lru_w_x"][0])
    base = _ca_bias_base(w["rel_bias"][0])
    h1 = _norm_fwd(x2, nrm(1, 0), out_dtype=ACT_DTYPE, name="norm_in_1")
    pmm1 = _mm(h1, w_in_o[:, :n_mm_o], out_dtype=ACT_DTYPE, tm=TM_FWD, tn=TN, name="in_odd_mm")
    pel1 = _mm(h1, w_in_o[:, n_mm_o:], out_dtype=F32, tm=TM_FWD, tn=TN, name="in_odd_el")
    kp = jnp.pad(pmm1[:, GROUP_WIDTH:2 * GROUP_WIDTH], ((CA_LEFT, 0), (0, 0)))
    vp = jnp.pad(pmm1[:, 2 * GROUP_WIDTH:], ((CA_LEFT, 0), (0, 0)))
    (out_c, lse_c), (w_mlp_down1,) = _ca_fwd(pmm1, kp, vp, base, ride=gather(("w_mlp_down", 1)))
    w["w_mlp_down"][1] = w_mlp_down1.reshape(D_FF, D_MODEL)
    lru_args = (pel1, w["conv_w"][0], w["conv_b"], wa_bd, w["lru_b_a"], wx_bd, w["lru_b_x"], w["lru_lambda"])
    out_d = _lru_fwd(*lru_args)
    mix_in1 = jnp.concatenate([out_c, out_d], axis=1)
    mix1 = _mm(mix_in1, w["w_out_odd"], out_dtype=F32, tm=TM_FWD, tn=TN, name="out_odd")
    x3 = _norm_fwd(mix1, nrm(1, 1), out_dtype=F32, res=x2, name="norm_mix_1")
    x4, mlp1, _, _ = mlp_fwd(x3, 1, None, None)

    loss, dx4 = _loss_fwd_bwd(x4, target)

    k_oo, k_io, k_oe, k_ie = ("w_out_odd", 0), ("w_in_odd", 0), ("w_out_even", 0), ("w_in_even", 0)
    mlp_keys = lambda l: [("w_mlp_down", l), ("w_mlp_up", l)]
    dx3, _ = mlp_bwd(dx4, mlp1, 1, None)
    dmix1, dnorm[(1, 1)] = _norm_bwd(dx3, mix1, nrm(1, 1), out_dtype=ACT_DTYPE, name="norm_mix_bwd_1")
    g[k_oo] = _mm(mix_in1, dmix1, ta=True, out_dtype=WIRE_DTYPE, tm=TM_DW, tn=TN, name="out_odd_dw").reshape(
        blocks(D_MODEL, D_MODEL))
    dmix_in1, (got[k_oo],) = _mm(dmix1, w["w_out_odd"], nt=True, out_dtype=F32, tm=TM_DX, tn=TN, name="out_odd_dx",
                                 ride=_sibling_plan([g[k_oo]]))
    (dq_c, dkp, dvp, dbase), rode = _ca_bwd(pmm1, kp, vp, base, lse_c, dmix_in1,
                                            ride=_chip_plan([sums[k] for k in mlp_keys(1)]))
    recv.update(zip(mlp_keys(1), rode))
    pair_sum(k_oo)
    (dgate, dxin, g_conv_w, g_conv_b, dwa_bd, g_lru_b_a, dwx_bd, g_lru_b_x, g_lru_lambda), (recv[k_oo],) = _lru_bwd(
        *lru_args, dmix_in1, ride=_chip_plan([sums[k_oo]]))
    dp1 = jnp.concatenate([dq_c, dkp[CA_LEFT:].astype(ACT_DTYPE), dvp[CA_LEFT:].astype(ACT_DTYPE), dgate, dxin], axis=1)
    g[k_io] = _column_shards(_mm(h1, dp1, ta=True, out_dtype=WIRE_DTYPE, tm=TM_DW, tn=TN, name="in_odd_dw"))
    dh1, (got[k_io],) = _mm(dp1, w_in_o, nt=True, out_dtype=F32, tm=TM_DX, tn=TN, name="in_odd_dx",
                            ride=_sibling_plan([g[k_io]]))
    pair_sum(k_io)
    dx2, dnorm[(1, 0)] = _norm_bwd(dh1, x2, nrm(1, 0), out_dtype=F32, add=dx3, name="norm_in_bwd_1")
    g["rel_bias"] = _ca_bias_base_grad(dbase)[None]
    g["conv_w"], g["conv_b"] = g_conv_w[None], g_conv_b
    g["lru_w_a"], g["lru_w_x"] = _block_diag_pairs_grad(dwa_bd)[None], _block_diag_pairs_grad(dwx_bd)[None]
    g["lru_b_a"], g["lru_b_x"], g["lru_lambda"] = g_lru_b_a, g_lru_b_x, g_lru_lambda

    dx1, (recv[k_io],) = mlp_bwd(dx2, mlp0, 0, _chip_plan([sums[k_io]]))
    dmix0, dnorm[(0, 1)] = _norm_bwd(dx1, mix0, nrm(0, 1), out_dtype=ACT_DTYPE, name="norm_mix_bwd_0")
    g[k_oe] = _mm(mix_in0, dmix0, ta=True, out_dtype=WIRE_DTYPE, tm=TM_DW, tn=TN, name="out_even_dw").reshape(
        blocks(D_MODEL, D_MODEL))
    dmix_in0, (got[k_oe],) = _mm(dmix0, w["w_out_even"], nt=True, out_dtype=F32, tm=TM_DX, tn=TN, name="out_even_dx",
                                 ride=_sibling_plan([g[k_oe]]))
    (dq_a, dk_a, dv_a, dr_a, da_a, dw_up_pad, g_gla_b_a, g_gla_norm_w), rode = _gla_bwd(
        pmm0, pel0, w_up_pad, w["gla_b_a"], w["gla_norm_w"], states, dmix_in0,
        ride=_chip_plan([sums[k] for k in mlp_keys(0)]))
    recv.update(zip(mlp_keys(0), rode))
    pair_sum(k_oe)
    (dq_b, dk_b, dv_b, dcum_t, dcum_q), (recv[k_oe],) = _fox_bwd(pmm0, cum, cum_t, lse_b, dmix_in0,
                                                                 ride=_chip_plan([sums[k_oe]]))
    df_b, db_f = _fox_gate_bwd(pel0, b_f_pad, dcum_t, dcum_q)
    g["gla_w_a_up"] = dw_up_pad[:GLA_RANK][None]
    g["gla_b_a"], g["gla_norm_w"], g["fox_b_f"] = g_gla_b_a, g_gla_norm_w, db_f[:, :ATT_HEADS]
    dp0 = jnp.concatenate([dq_a, dk_a, dv_a, dq_b, dk_b.astype(ACT_DTYPE), dv_b.astype(ACT_DTYPE), dr_a, da_a, df_b],
                          axis=1)
    w_perm = jnp.concatenate([wmm_e, wel_e], axis=1)
    n_mm_e = wmm_e.shape[1]
    dw_perm, (repl_parts,) = _mm(h0, dp0, ta=True, out_dtype=WIRE_DTYPE, tm=TM_DW, tn=dp0.shape[1] // 2, name="in_even_dw",
                                 ride=_gather_plan([_pack([g[n] for n in REPLICATED], REPL_SLAB_ROWS)]))
    g[k_ie] = _column_shards(_even_in_merge(dw_perm[:, :n_mm_e], dw_perm[:, n_mm_e:]))
    dh0, (got[k_ie],) = _mm(dp0, w_perm, nt=True, out_dtype=F32, tm=TM_DX, tn=TN, name="in_even_dx",
                            ride=_sibling_plan([g[k_ie]]))
    pair_sum(k_ie)
    dx0, dnorm[(0, 0)] = _norm_bwd(dh0, x, nrm(0, 0), out_dtype=F32, add=dx1, name="norm_in_bwd_0")

    g["norm_w"] = jnp.stack([jnp.concatenate([dnorm[(l, k)] for k in range(4)], axis=0) for l in range(DEPTH)])
    k_vec = ("vectors", 0)
    vec_grads = _pack([_split_shards(g[n], SHARDED[n]) for n in VECTORS], VEC_SLAB_ROWS, lead=(N_DEV,))
    g[k_vec] = vec_grads.astype(WIRE_DTYPE)
    (got[k_vec],) = _run_plan(_sibling_plan([g[k_vec]]), "rs_sibling_exchange_last")
    pair_sum(k_vec)
    recv[k_ie], recv[k_vec] = _run_plan(_chip_plan([sums[k_ie], sums[k_vec]]), "rs_chip_exchange_last")
    return loss, dx0, sums, recv, repl_parts


def kernel(x, norm_w, w_in_even, gla_w_a_up, gla_b_a, gla_norm_w, fox_b_f, w_out_even, w_in_odd, rel_bias, conv_w, conv_b, lru_w_a, lru_b_a, lru_w_x, lru_b_x, lru_lambda, w_out_odd, w_mlp_up, w_mlp_down, loss_target, m_norm_w, m_w_in_even, m_gla_w_a_up, m_gla_b_a, m_gla_norm_w, m_fox_b_f, m_w_out_even, m_w_in_odd, m_rel_bias, m_conv_w, m_conv_b, m_lru_w_a, m_lru_b_a, m_lru_w_x, m_lru_b_x, m_lru_lambda, m_w_out_odd, m_w_mlp_up, m_w_mlp_down, v_norm_w, v_w_in_even, v_gla_w_a_up, v_gla_b_a, v_gla_norm_w, v_fox_b_f, v_w_out_even, v_w_in_odd, v_rel_bias, v_conv_w, v_conv_b, v_lru_w_a, v_lru_b_a, v_lru_w_x, v_lru_b_x, v_lru_lambda, v_w_out_odd, v_w_mlp_up, v_w_mlp_down):
    wts = dict(zip(WEIGHTS, (norm_w, w_in_even, gla_w_a_up, gla_b_a, gla_norm_w, fox_b_f, w_out_even, w_in_odd, rel_bias,
                             conv_w, conv_b, lru_w_a, lru_b_a, lru_w_x, lru_b_x, lru_lambda, w_out_odd, w_mlp_up,
                             w_mlp_down)))
    mom = dict(zip(WEIGHTS, (m_norm_w, m_w_in_even, m_gla_w_a_up, m_gla_b_a, m_gla_norm_w, m_fox_b_f, m_w_out_even,
                             m_w_in_odd, m_rel_bias, m_conv_w, m_conv_b, m_lru_w_a, m_lru_b_a, m_lru_w_x, m_lru_b_x,
                             m_lru_lambda, m_w_out_odd, m_w_mlp_up, m_w_mlp_down)))
    var = dict(zip(WEIGHTS, (v_norm_w, v_w_in_even, v_gla_w_a_up, v_gla_b_a, v_gla_norm_w, v_fox_b_f, v_w_out_even,
                             v_w_in_odd, v_rel_bias, v_conv_w, v_conv_b, v_lru_w_a, v_lru_b_a, v_lru_w_x, v_lru_b_x,
                             v_lru_lambda, v_w_out_odd, v_w_mlp_up, v_w_mlp_down)))
    vec_shapes = [wts[n].shape for n in VECTORS]
    repl_shapes = [wts[n].shape for n in REPLICATED]
    place = jnp.stack([lax.axis_index("c"), 2 * lax.axis_index("x") + lax.axis_index("y")]).astype(jnp.int32)

    shard = {(n, l): wts[n][l].astype(WIRE_DTYPE) for n, l in MATRIX_BLOCKS}
    vec_slab = _pack([wts[n] for n in VECTORS], VEC_SLAB_ROWS)
    loss_blk, dx, sums, recv, repl_parts = _forward_backward(
        x[0], loss_target[0], shard, vec_slab, vec_shapes, {n: wts[n] for n in REPLICATED}, place)
    loss = lax.psum(loss_blk[0, 0], ("x", "y", "c"))

    k_vec = ("vectors", 0)

    vec_of = lambda d: _pack([d[n] for n in VECTORS], VEC_SLAB_ROWS)[None]
    upd = {n: _adamw_sharded([(sums[(n, l)], recv[(n, l)]) for l in range(wts[n].shape[0])], wts[n], mom[n], var[n],
                             place, f"adamw_{n}") for n in MATRICES}
    vec_upd = _adamw_sharded([(sums[k_vec], recv[k_vec])], vec_of(wts), vec_of(mom), vec_of(var), place, "adamw_vectors")
    rp = _adamw_replicated(repl_parts, _pack([wts[n] for n in REPLICATED], REPL_SLAB_ROWS),
                           _pack([mom[n] for n in REPLICATED], REPL_SLAB_ROWS),
                           _pack([var[n] for n in REPLICATED], REPL_SLAB_ROWS), "adamw_replicated")
    outs = []
    for kind in range(4):
        vals = dict(zip(VECTORS, _unpack(vec_upd[kind][0], vec_shapes)))
        vals.update(zip(REPLICATED, _unpack(rp[kind], repl_shapes)))
        vals.update((n, upd[n][kind]) for n in MATRICES)
        outs += [vals[n] for n in WEIGHTS]
    return (loss, dx[None], *outs)
```

```python
import functools
from typing import Callable, NamedTuple

import jax
import jax.numpy as jnp
from jax import lax
from jax.experimental import pallas as pl
from jax.experimental.pallas import tpu as pltpu

F32 = jnp.float32
MXU_DTYPE = jnp.bfloat16
ACT_DTYPE = jnp.bfloat16
WIRE_DTYPE = jnp.bfloat16

V7X_VMEM_BYTES = 64 * 1024 * 1024
VMEM_LIMIT = (V7X_VMEM_BYTES * 7) // 8
LANES = 128

D_MODEL = 1024
SEQ = 2048
DEPTH = 2
CHUNK = 64
GROUP_WIDTH = D_MODEL // 2
D_FF = 4 * D_MODEL
NORM_EPS = 1e-6
GLA_HEADS = 4
GLA_DV = GROUP_WIDTH // GLA_HEADS
GLA_DK = GLA_DV // 2
GLA_KW = GLA_HEADS * GLA_DK
GLA_RANK = 16
GLA_GATE_TAU = 16.0
HEAD_DIM = 64
ATT_HEADS = GROUP_WIDTH // HEAD_DIM
CA_LEFT = 8 * CHUNK
REL_CLIP = 128
LRU_BLOCK_DIM = 64
CONV_WIDTH = 4
LRU_C = 8.0
N_DEV = 8

ADAM_LR = 0.001
ADAM_B1 = 0.9
ADAM_B2 = 0.999
ADAM_EPS = 1e-08
ADAM_WD = 0.01
ADAM_STEP = 10

NEG = float(jnp.finfo(jnp.float32).min)
MESH = pl.DeviceIdType.MESH


def _params(*sem):
    return pltpu.CompilerParams(dimension_semantics=sem, vmem_limit_bytes=VMEM_LIMIT)


def _dot(a, b, ca=1, cb=0):
    return lax.dot_general(a.astype(MXU_DTYPE), b.astype(MXU_DTYPE), (((ca,), (cb,)), ((), ())),
                           preferred_element_type=F32)


def _dot_exact(a, b):
    return lax.dot_general(a, b, (((1,), (0,)), ((), ())), precision=lax.Precision.HIGHEST,
                           preferred_element_type=F32)


def _log_sigmoid(x):
    return jnp.minimum(x, 0.0) - jnp.log1p(jnp.exp(-jnp.abs(x)))


def _iota(shape, axis):
    return lax.broadcasted_iota(jnp.int32, shape, axis)


ANY = pl.BlockSpec(memory_space=pl.ANY)
N_CHIPS = 4


class _Plan(NamedTuple):
    ins: list
    outs: list
    sems: list
    start: Callable
    finish: Callable


def _place():
    x, y, c = lax.axis_index("x"), lax.axis_index("y"), lax.axis_index("c")
    return x, y, c, [(1 - x, y), (x, 1 - y), (1 - x, 1 - y)]


def _gather_plan(xs):
    n = len(xs)

    def parts(x_refs, out_refs, sems):
        send_sems, recv_sems, local_sems = sems
        x, y, c, chips = _place()
        me, sibling = (x, y, c), (x, y, 1 - c)

        def rows(a, px, py, pc):
            return out_refs[a].at[4 * px + 2 * py + pc]

        def copy(a, k, block, to, src=None):
            return pltpu.make_async_remote_copy(
                src_ref=rows(a, *block) if src is None else src, dst_ref=rows(a, *block),
                send_sem=send_sems.at[7 * a + k], recv_sem=recv_sems.at[7 * a + k], device_id=to, device_id_type=MESH)

        mine = [pltpu.make_async_copy(x_refs[a], rows(a, *me), local_sems.at[a]) for a in range(n)]
        first = []
        for a in range(n):
            first.append(copy(a, 0, me, sibling, src=x_refs[a]))
            first += [copy(a, 1 + j, me, (*chip, c), src=x_refs[a]) for j, chip in enumerate(chips)]
        return c, me, sibling, chips, copy, mine, first

    def start(x_refs, out_refs, sems):
        *_, mine, first = parts(x_refs, out_refs, sems)
        for cp in first + mine:
            cp.start()

    def finish(x_refs, out_refs, sems):
        c, me, sibling, chips, copy, mine, first = parts(x_refs, out_refs, sems)
        passed = []
        for j, chip in enumerate(chips):
            for a in range(n):
                copy(a, 1 + j, (*chip, c), me).wait_recv()
                passed.append(copy(a, 4 + j, (*chip, c), sibling))
                passed[-1].start()
        for a in range(n):
            copy(a, 0, sibling, me).wait_recv()
            for j, chip in enumerate(chips):
                copy(a, 4 + j, (*chip, 1 - c), me).wait_recv()
        for cp in first + passed:
            cp.wait_send()
        for cp in mine:
            cp.wait()

    return _Plan(list(xs), [jax.ShapeDtypeStruct((N_DEV,) + x.shape, x.dtype) for x in xs],
                 [pltpu.SemaphoreType.DMA((7 * n,)), pltpu.SemaphoreType.DMA((7 * n,)), pltpu.SemaphoreType.DMA((n,))],
                 start, finish)


def _exchange_plan(copies_of, ins, outs, per_array):
    n = len(ins)

    def start(in_refs, out_refs, sems):
        for cp in copies_of(in_refs, out_refs, sems):
            cp.start()

    def finish(in_refs, out_refs, sems):
        copies = copies_of(in_refs, out_refs, sems)
        for cp in copies:
            cp.wait_recv()
        for cp in copies:
            cp.wait_send()

    return _Plan(list(ins), outs, [pltpu.SemaphoreType.DMA((per_array * n,)), pltpu.SemaphoreType.DMA((per_array * n,))],
                 start, finish)


def _sibling_plan(gs):
    def copies_of(g_refs, got_refs, sems):
        x, y, c, _ = _place()
        return [pltpu.make_async_remote_copy(
            src_ref=g_refs[a].at[2 * k + (1 - c)], dst_ref=got_refs[a].at[k], send_sem=sems[0].at[N_CHIPS * a + k],
            recv_sem=sems[1].at[N_CHIPS * a + k], device_id=(x, y, 1 - c), device_id_type=MESH)
            for a in range(len(gs)) for k in range(N_CHIPS)]

    return _exchange_plan(copies_of, gs, [jax.ShapeDtypeStruct((N_CHIPS,) + g.shape[1:], g.dtype) for g in gs], N_CHIPS)


def _chip_plan(ss):
    def copies_of(s_refs, out_refs, sems):
        x, y, c, chips = _place()
        return [pltpu.make_async_remote_copy(
            src_ref=s_refs[a].at[2 * px + py], dst_ref=out_refs[a].at[j], send_sem=sems[0].at[3 * a + j],
            recv_sem=sems[1].at[3 * a + j], device_id=(px, py, c), device_id_type=MESH)
            for a in range(len(ss)) for j, (px, py) in enumerate(chips)]

    return _exchange_plan(copies_of, ss, [jax.ShapeDtypeStruct((3,) + s.shape[1:], s.dtype) for s in ss], 3)


def _run_plan(plan, name):
    n_in, n_out = len(plan.ins), len(plan.outs)

    def body(*refs):
        args = refs[:n_in], refs[n_in:n_in + n_out], refs[n_in + n_out:]
        plan.start(*args)
        plan.finish(*args)

    return pl.pallas_call(body, out_shape=plan.outs, in_specs=[ANY] * n_in, out_specs=[ANY] * n_out,
                          scratch_shapes=plan.sems, name=name)(*plan.ins)


def _pcall(body, ride, *, grid, in_specs, out_specs, out_shape, scratch_shapes=(), semantics, name):
    if ride is None:
        return pl.pallas_call(body, grid=grid, in_specs=in_specs, out_specs=out_specs, out_shape=out_shape,
                              scratch_shapes=list(scratch_shapes), compiler_params=_params(*semantics), name=name)
    single = not isinstance(out_shape, (list, tuple))
    out_specs_l, out_shape_l = ([out_specs], [out_shape]) if single else (list(out_specs), list(out_shape))
    n_in, n_out, n_scr = len(in_specs), len(out_shape_l), len(scratch_shapes)
    r_in, r_out = len(ride.ins), len(ride.outs)

    def riding(*refs):
        cuts = [n_in, r_in, n_out, r_out, n_scr]
        groups, at = [], 0
        for width in cuts:
            groups.append(refs[at:at + width])
            at += width
        ins, r_ins, outs, r_outs, scr = groups
        sems = refs[at:]
        first = functools.reduce(jnp.logical_and, [pl.program_id(d) == 0 for d in range(len(grid))])
        last = functools.reduce(jnp.logical_and, [pl.program_id(d) == grid[d] - 1 for d in range(len(grid))])

        @pl.when(first)
        def _():
            ride.start(r_ins, r_outs, sems)

        body(*ins, *outs, *scr)

        @pl.when(last)
        def _():
            ride.finish(r_ins, r_outs, sems)

    call = pl.pallas_call(
        riding, grid=grid, in_specs=list(in_specs) + [ANY] * r_in, out_specs=out_specs_l + [ANY] * r_out,
        out_shape=out_shape_l + list(ride.outs), scratch_shapes=list(scratch_shapes) + list(ride.sems),
        compiler_params=_params(*(["arbitrary"] * len(grid))), name=name)

    def run(*args):
        res = call(*args, *ride.ins)
        return (res[0] if single else list(res[:n_out])), list(res[n_out:])

    return run


def _mm(a, b, *, nt=False, ta=False, out_dtype, tm, tn, a_sqrelu=False, drelu_of=None, b_blocked=False,
        out_blocked=False, name, ride=None):
    k, m = a.shape if ta else a.shape[::-1]
    if b_blocked:
        assert not nt and b.shape[1] == k and b.shape[2] == tn
        n = b.shape[0] * tn
    else:
        n = b.shape[0] if nt else b.shape[1]
        assert (b.shape[1] if nt else b.shape[0]) == k
    tm, tn = min(tm, m), min(tn, n)
    assert m % tm == 0 and n % tn == 0

    def body(*refs):
        a_ref, b_ref = refs[0], refs[1]
        o_ref = refs[-1]
        av = a_ref[...]
        if a_sqrelu:
            av = jnp.square(jnp.maximum(av.astype(F32), 0.0))
        acc = _dot(av, b_ref[...], 0 if ta else 1, 1 if nt else 0)
        if drelu_of is not None:
            acc = acc * (2.0 * jnp.maximum(refs[2][...].astype(F32), 0.0))
        o_ref[...] = acc.astype(out_dtype)

    if b_blocked:
        b_spec = pl.BlockSpec((None, k, tn), lambda i, j: (j, 0, 0))
    elif nt:
        b_spec = pl.BlockSpec((tn, k), lambda i, j: (j, 0))
    else:
        b_spec = pl.BlockSpec((k, tn), lambda i, j: (0, j))
    a_spec = pl.BlockSpec((k, tm), lambda i, j: (0, i)) if ta else pl.BlockSpec((tm, k), lambda i, j: (i, 0))
    in_specs = [a_spec, b_spec]
    args = [a, b]
    if drelu_of is not None:
        in_specs.append(pl.BlockSpec((tm, tn), lambda i, j: (i, j)))
        args.append(drelu_of)
    if out_blocked:
        out_spec = pl.BlockSpec((None, tm, tn), lambda i, j: (j, i, 0))
        out_shape = jax.ShapeDtypeStruct((n // tn, m, tn), out_dtype)
    else:
        out_spec = pl.BlockSpec((tm, tn), lambda i, j: (i, j))
        out_shape = jax.ShapeDtypeStruct((m, n), out_dtype)
    return _pcall(body, ride, grid=(m // tm, n // tn), in_specs=in_specs, out_specs=out_spec, out_shape=out_shape,
                  semantics=("parallel", "parallel"), name=name)(*args)


def _mm_nt_blocked(a, b, *, out_dtype, tm, tn, name):
    m = a.shape[0]
    p, n, kp = b.shape
    assert a.shape[1] == p * kp and m % tm == 0 and n % tn == 0

    def body(a_ref, b_ref, o_ref, acc_ref):
        @pl.when(pl.program_id(2) == 0)
        def _():
            acc_ref[...] = jnp.zeros_like(acc_ref)

        acc_ref[...] += _dot(a_ref[...], b_ref[...], 1, 1)

        @pl.when(pl.program_id(2) == p - 1)
        def _():
            o_ref[...] = acc_ref[...].astype(out_dtype)

    return pl.pallas_call(
        body, grid=(m // tm, n // tn, p),
        in_specs=[pl.BlockSpec((tm, kp), lambda i, j, q: (i, q)), pl.BlockSpec((None, tn, kp), lambda i, j, q: (q, j, 0))],
        out_specs=pl.BlockSpec((tm, tn), lambda i, j, q: (i, j)),
        out_shape=jax.ShapeDtypeStruct((m, n), out_dtype),
        scratch_shapes=[pltpu.VMEM((tm, tn), F32)],
        compiler_params=_params("parallel", "parallel", "arbitrary"), name=name)(a, b)


ROW_TILE = 512
TM_FWD, TM_DX, TM_DW, TN = 2048, 1024, 1024, 512


def _norm_fwd(x, w, *, out_dtype, res=None, name):
    t, d = x.shape

    def body(*refs):
        x_ref, w_ref, o_ref = refs[0], refs[1], refs[-1]
        xv = x_ref[...]
        y = xv * lax.rsqrt(jnp.mean(xv * xv, axis=-1, keepdims=True) + NORM_EPS) * w_ref[...]
        if res is not None:
            y = refs[2][...] + y
        o_ref[...] = y.astype(out_dtype)

    row = pl.BlockSpec((ROW_TILE, d), lambda i: (i, 0))
    in_specs = [row, pl.BlockSpec((1, d), lambda i: (0, 0))] + ([row] if res is not None else [])
    args = [x, w] + ([res] if res is not None else [])
    return pl.pallas_call(body, grid=(t // ROW_TILE,), in_specs=in_specs, out_specs=row,
                          out_shape=jax.ShapeDtypeStruct((t, d), out_dtype),
                          compiler_params=_params("parallel"), name=name)(*args)


def _norm_bwd(dy, x, w, *, out_dtype, add=None, name):
    t, d = x.shape

    def body(*refs):
        dy_ref, x_ref, w_ref = refs[0], refs[1], refs[2]
        dx_ref, dw_ref = refs[-2], refs[-1]
        xv = x_ref[...]
        rstd = lax.rsqrt(jnp.mean(xv * xv, axis=-1, keepdims=True) + NORM_EPS)
        xhat = xv * rstd
        dyv = dy_ref[...].astype(F32)
        g = dyv * w_ref[...]
        dx = rstd * (g - xhat * jnp.mean(g * xhat, axis=-1, keepdims=True))
        if add is not None:
            dx = dx + refs[3][...]
        dx_ref[...] = dx.astype(out_dtype)

        @pl.when(pl.program_id(0) == 0)
        def _():
            dw_ref[...] = jnp.zeros_like(dw_ref)

        dw_ref[...] += jnp.sum(dyv * xhat, axis=0, keepdims=True)

    row = pl.BlockSpec((ROW_TILE, d), lambda i: (i, 0))
    vec = pl.BlockSpec((1, d), lambda i: (0, 0))
    in_specs = [row, row, vec] + ([row] if add is not None else [])
    args = [dy, x, w] + ([add] if add is not None else [])
    return pl.pallas_call(body, grid=(t // ROW_TILE,), in_specs=in_specs, out_specs=[row, vec],
                          out_shape=[jax.ShapeDtypeStruct((t, d), out_dtype), jax.ShapeDtypeStruct((1, d), F32)],
                          compiler_params=_params("arbitrary"), name=name)(*args)


def _loss_fwd_bwd(y, target):
    t, d = y.shape

    def body(y_ref, t_ref, l_ref, dy_ref):
        diff = y_ref[...] - t_ref[...]
        dy_ref[...] = diff * (1.0 / d)

        @pl.when(pl.program_id(0) == 0)
        def _():
            l_ref[...] = jnp.zeros_like(l_ref)

        l_ref[...] += 0.5 * jnp.sum(jnp.mean(diff * diff, axis=-1, keepdims=True), axis=0, keepdims=True)

    row = pl.BlockSpec((ROW_TILE, d), lambda i: (i, 0))
    return pl.pallas_call(body, grid=(t // ROW_TILE,), in_specs=[row, row],
                          out_specs=[pl.BlockSpec((8, LANES), lambda i: (0, 0)), row],
                          out_shape=[jax.ShapeDtypeStruct((8, LANES), F32), jax.ShapeDtypeStruct((t, d), F32)],
                          compiler_params=_params("arbitrary"), name="loss")(y, target)


GLA_UNROLL = (8, 4)


def _gla_specs(t):
    return [pl.BlockSpec((t, GLA_KW), lambda i: (0, 0)),
            pl.BlockSpec((t, GLA_KW), lambda i: (0, 1)),
            pl.BlockSpec((t, GROUP_WIDTH), lambda i: (0, 1)),
            pl.BlockSpec((t, GROUP_WIDTH), lambda i: (0, 0)),
            pl.BlockSpec((t, LANES), lambda i: (0, 4)),
            pl.BlockSpec((LANES, GLA_KW), lambda i: (0, 0)),
            pl.BlockSpec((1, GLA_KW), lambda i: (0, 0)),
            pl.BlockSpec((1, GROUP_WIDTH), lambda i: (0, 0))]


def _gla_fwd(pmm, pel, w_up, b_a, gnorm_w, ride=None):
    t = pmm.shape[0]
    nc = t // CHUNK
    scale = GLA_DK ** -0.5

    def body(q_ref, k_ref, v_ref, r_ref, a_ref, wup_ref, ba_ref, gw_ref, o_ref, st_ref, la_scr, s_scr):
        z = _dot(a_ref[...], wup_ref[...]) + ba_ref[...]
        la_scr[...] = _log_sigmoid(z) * (1.0 / GLA_GATE_TAU)
        s_scr[...] = jnp.zeros_like(s_scr)
        tri = (_iota((CHUNK, CHUNK), 1) <= _iota((CHUNK, CHUNK), 0)).astype(F32)

        def chunk(c, carry):
            rows = pl.ds(pl.multiple_of(c * CHUNK, CHUNK), CHUNK)
            cum = _dot_exact(tri, la_scr[rows, :])
            tot = cum[CHUNK - 1:CHUNK, :]
            kd = k_ref[rows, :].astype(F32) * jnp.exp(tot - cum)
            decay = jnp.exp(tot)
            qs = q_ref[rows, :].astype(F32) * scale
            vv = v_ref[rows, :].astype(F32)
            rr = r_ref[rows, :]
            gate = rr * jax.nn.sigmoid(rr) * gw_ref[...]
            for h in range(GLA_HEADS):
                ks = slice(h * GLA_DK, (h + 1) * GLA_DK)
                vs = slice(h * GLA_DV, (h + 1) * GLA_DV)
                inc_t = _dot(vv[:, vs].T, kd[:, ks])
                s_t = s_scr[vs, :] * decay[:, ks] + inc_t
                s_scr[vs, :] = s_t
                st_ref[c, vs, :] = s_t
                o = _dot(qs[:, ks], s_t, 1, 1)
                y = o * lax.rsqrt(jnp.mean(o * o, axis=-1, keepdims=True) + NORM_EPS)
                o_ref[rows, vs] = (y * gate[:, vs]).astype(o_ref.dtype)
            return carry

        lax.fori_loop(0, nc, chunk, 0, unroll=GLA_UNROLL[0])

    return _pcall(
        body, ride, grid=(1,), in_specs=_gla_specs(t),
        out_specs=[pl.BlockSpec((t, GROUP_WIDTH), lambda i: (0, 0)),
                   pl.BlockSpec((nc, GLA_HEADS * GLA_DV, GLA_DK), lambda i: (0, 0, 0))],
        out_shape=[jax.ShapeDtypeStruct((t, GROUP_WIDTH), ACT_DTYPE),
                   jax.ShapeDtypeStruct((nc, GLA_HEADS * GLA_DV, GLA_DK), F32)],
        scratch_shapes=[pltpu.VMEM((t, GLA_KW), F32), pltpu.VMEM((GLA_HEADS * GLA_DV, GLA_DK), F32)],
        semantics=("arbitrary",), name="gla_fwd")(pmm, pmm, pmm, pel, pel, w_up, b_a, gnorm_w)


def _gla_bwd(pmm, pel, w_up, b_a, gnorm_w, states, dmix, ride=None):
    t = pmm.shape[0]
    nc = t // CHUNK
    scale = GLA_DK ** -0.5

    def body(q_ref, k_ref, v_ref, r_ref, a_ref, wup_ref, ba_ref, gw_ref, st_ref, do_ref,
             dq_ref, dk_ref, dv_ref, dr_ref, da_ref, dwup_ref, dba_ref, dgw_ref, la_scr, dz_scr, ds_scr):
        z = _dot(a_ref[...], wup_ref[...]) + ba_ref[...]
        la_scr[...] = _log_sigmoid(z) * (1.0 / GLA_GATE_TAU)
        ds_scr[...] = jnp.zeros_like(ds_scr)
        dgw_ref[...] = jnp.zeros_like(dgw_ref)
        row_i, col_i = _iota((CHUNK, CHUNK), 0), _iota((CHUNK, CHUNK), 1)
        tri = (col_i <= row_i).astype(F32)
        tri_strict = (col_i < row_i).astype(F32)

        def chunk(n, carry):
            c = nc - 1 - n
            rows = pl.ds(pl.multiple_of(c * CHUNK, CHUNK), CHUNK)
            cum = _dot_exact(tri, la_scr[rows, :])
            tot = cum[CHUNK - 1:CHUNK, :]
            e = jnp.exp(tot - cum)
            kd = k_ref[rows, :].astype(F32) * e
            decay = jnp.exp(tot)
            qs = q_ref[rows, :].astype(F32) * scale
            vv = v_ref[rows, :].astype(F32)
            rr = r_ref[rows, :]
            sig = jax.nn.sigmoid(rr)
            silu = rr * sig
            dsilu = sig * (1.0 + rr * (1.0 - sig))
            dout = do_ref[rows, :]
            gw = gw_ref[...]
            c_prev = jnp.maximum(c - 1, 0)
            has_prev = (c > 0).astype(F32)
            zc = _dot(a_ref[rows, :], wup_ref[...]) + ba_ref[...]
            dz_scale = jax.nn.sigmoid(-zc) * (1.0 / GLA_GATE_TAU)
            for h in range(GLA_HEADS):
                ks = slice(h * GLA_DK, (h + 1) * GLA_DK)
                vs = slice(h * GLA_DV, (h + 1) * GLA_DV)
                s_t = st_ref[c, vs, :]
                s_prev = st_ref[c_prev, vs, :] * has_prev
                o = _dot(qs[:, ks], s_t, 1, 1)
                rstd = lax.rsqrt(jnp.mean(o * o, axis=-1, keepdims=True) + NORM_EPS)
                y = o * rstd
                dg = dout[:, vs]
                dgw_ref[:, vs] += jnp.sum(dg * y * silu[:, vs], axis=0, keepdims=True)
                dr_ref[rows, vs] = (dg * y * gw[:, vs] * dsilu[:, vs]).astype(dr_ref.dtype)
                dy = dg * gw[:, vs] * silu[:, vs]
                d_o = rstd * (dy - y * jnp.mean(dy * y, axis=-1, keepdims=True))
                dq_ref[rows, ks] = (_dot(d_o, s_t) * scale).astype(dq_ref.dtype)
                ds_t = ds_scr[vs, :] + _dot(d_o.T, qs[:, ks])
                dv_ref[rows, vs] = _dot(kd[:, ks], ds_t, 1, 1).astype(dv_ref.dtype)
                dkd = _dot(vv[:, vs], ds_t)
                ddecay = jnp.sum(ds_t * s_prev, axis=0, keepdims=True)
                ds_scr[vs, :] = ds_t * decay[:, ks]
                dla = ddecay * decay[:, ks] + _dot_exact(tri_strict, dkd * kd[:, ks])
                dz_scr[rows, ks] = dla * dz_scale[:, ks]
                dk_ref[rows, ks] = (dkd * e[:, ks]).astype(dk_ref.dtype)
            return carry

        lax.fori_loop(0, nc, chunk, 0, unroll=GLA_UNROLL[1])
        dz = dz_scr[...]
        da_ref[...] = _dot(dz, wup_ref[...], 1, 1).astype(da_ref.dtype)
        dwup_ref[...] = _dot(a_ref[...].T, dz)
        dba_ref[...] = jnp.sum(dz, axis=0, keepdims=True)

    in_specs = _gla_specs(t) + [
        pl.BlockSpec((nc, GLA_HEADS * GLA_DV, GLA_DK), lambda i: (0, 0, 0)),
        pl.BlockSpec((t, GROUP_WIDTH), lambda i: (0, 0))]
    full = lambda r, c: pl.BlockSpec((r, c), lambda i: (0, 0))
    return _pcall(
        body, ride, grid=(1,), in_specs=in_specs,
        out_specs=[full(t, GLA_KW), full(t, GLA_KW), full(t, GROUP_WIDTH), full(t, GROUP_WIDTH), full(t, LANES),
                   full(LANES, GLA_KW), full(1, GLA_KW), full(1, GROUP_WIDTH)],
        out_shape=[jax.ShapeDtypeStruct((t, GLA_KW), ACT_DTYPE), jax.ShapeDtypeStruct((t, GLA_KW), ACT_DTYPE),
                   jax.ShapeDtypeStruct((t, GROUP_WIDTH), ACT_DTYPE), jax.ShapeDtypeStruct((t, GROUP_WIDTH), ACT_DTYPE),
                   jax.ShapeDtypeStruct((t, LANES), ACT_DTYPE), jax.ShapeDtypeStruct((LANES, GLA_KW), F32),
                   jax.ShapeDtypeStruct((1, GLA_KW), F32), jax.ShapeDtypeStruct((1, GROUP_WIDTH), F32)],
        scratch_shapes=[pltpu.VMEM((t, GLA_KW), F32), pltpu.VMEM((t, GLA_KW), F32),
                        pltpu.VMEM((GLA_HEADS * GLA_DV, GLA_DK), F32)],
        semantics=("arbitrary",), name="gla_bwd")(
            pmm, pmm, pmm, pel, pel, w_up, b_a, gnorm_w, states, dmix)


CUM_BLOCK = 256


def _fox_gate_fwd(pel, b_f):
    t = pel.shape[0]
    nb = t // CUM_BLOCK

    def body(f_ref, b_ref, cum_ref, cum_t_ref):
        tri = (_iota((CUM_BLOCK, CUM_BLOCK), 1) <= _iota((CUM_BLOCK, CUM_BLOCK), 0)).astype(F32)
        carry = jnp.zeros((1, LANES), F32)
        for blk in range(nb):
            rows = slice(blk * CUM_BLOCK, (blk + 1) * CUM_BLOCK)
            cum = _dot_exact(tri, _log_sigmoid(f_ref[rows, :] + b_ref[...])) + carry
            cum_ref[rows, :] = cum
            cum_t_ref[blk] = cum.T[:ATT_HEADS, :]
            carry = cum[CUM_BLOCK - 1:CUM_BLOCK, :]

    return pl.pallas_call(
        body, grid=(1,),
        in_specs=[pl.BlockSpec((t, LANES), lambda i: (0, 5)), pl.BlockSpec((1, LANES), lambda i: (0, 0))],
        out_specs=[pl.BlockSpec((t, LANES), lambda i: (0, 0)),
                   pl.BlockSpec((nb, ATT_HEADS, CUM_BLOCK), lambda i: (0, 0, 0))],
        out_shape=[jax.ShapeDtypeStruct((t, LANES), F32), jax.ShapeDtypeStruct((nb, ATT_HEADS, CUM_BLOCK), F32)],
        compiler_params=_params("arbitrary"), name="fox_gate_fwd")(pel, b_f)


def _fox_gate_bwd(pel, b_f, dcum_t, dcum_q):
    t = pel.shape[0]
    nb = t // CUM_BLOCK

    def body(f_ref, b_ref, dct_ref, dcq_ref, df_ref, db_ref):
        tri_up = (_iota((CUM_BLOCK, CUM_BLOCK), 1) >= _iota((CUM_BLOCK, CUM_BLOCK), 0)).astype(F32)
        carry = jnp.zeros((1, LANES), F32)
        db = jnp.zeros((1, LANES), F32)
        for blk in reversed(range(nb)):
            rows = slice(blk * CUM_BLOCK, (blk + 1) * CUM_BLOCK)
            dls = _dot_exact(tri_up, dct_ref[blk].T + dcq_ref[rows, :]) + carry
            carry = dls[0:1, :]
            df = dls * jax.nn.sigmoid(-(f_ref[rows, :] + b_ref[...]))
            df_ref[rows, :] = df.astype(df_ref.dtype)
            db = db + jnp.sum(df, axis=0, keepdims=True)
        db_ref[...] = db

    return pl.pallas_call(
        body, grid=(1,),
        in_specs=[pl.BlockSpec((t, LANES), lambda i: (0, 5)), pl.BlockSpec((1, LANES), lambda i: (0, 0)),
                  pl.BlockSpec((nb, LANES, CUM_BLOCK), lambda i: (0, 0, 0)), pl.BlockSpec((t, LANES), lambda i: (0, 0))],
        out_specs=[pl.BlockSpec((t, LANES), lambda i: (0, 0)), pl.BlockSpec((1, LANES), lambda i: (0, 0))],
        out_shape=[jax.ShapeDtypeStruct((t, LANES), ACT_DTYPE), jax.ShapeDtypeStruct((1, LANES), F32)],
        compiler_params=_params("arbitrary"), name="fox_gate_bwd")(pel, b_f, dcum_t, dcum_q)


FOX_Q_BLOCK = 256


assert FOX_Q_BLOCK == CUM_BLOCK


def _fox_scores(q_ref, k_ref, cum_ref, cum_t_ref, h, i):
    hs = slice(h * HEAD_DIM, (h + 1) * HEAD_DIM)
    nb = cum_t_ref.shape[0]
    key_gate = jnp.concatenate([cum_t_ref[kb, h:h + 1, :] for kb in range(nb)], axis=1)
    s = _dot(q_ref[:, hs], k_ref[:, hs], 1, 1) * (HEAD_DIM ** -0.5) + (cum_ref[:, h:h + 1] - key_gate)
    shape = (FOX_Q_BLOCK, nb * FOX_Q_BLOCK)
    return jnp.where(_iota(shape, 1) <= i * FOX_Q_BLOCK + _iota(shape, 0), s, NEG)


def _fox_specs(t):
    bq, nb = FOX_Q_BLOCK, t // FOX_Q_BLOCK
    return [pl.BlockSpec((bq, GROUP_WIDTH), lambda i: (i, 2)), pl.BlockSpec((t, GROUP_WIDTH), lambda i: (0, 3)),
            pl.BlockSpec((t, GROUP_WIDTH), lambda i: (0, 4)), pl.BlockSpec((bq, LANES), lambda i: (i, 0)),
            pl.BlockSpec((nb, ATT_HEADS, bq), lambda i: (0, 0, 0))]


def _fox_fwd(pmm, cum, cum_t, ride=None):
    t = pmm.shape[0]
    bq = FOX_Q_BLOCK

    def body(q_ref, k_ref, v_ref, cum_ref, cum_t_ref, o_ref, lse_ref):
        i = pl.program_id(0)
        lse_ref[...] = jnp.zeros_like(lse_ref)
        for h in range(ATT_HEADS):
            hs = slice(h * HEAD_DIM, (h + 1) * HEAD_DIM)
            s = _fox_scores(q_ref, k_ref, cum_ref, cum_t_ref, h, i)
            m = jnp.max(s, axis=-1, keepdims=True)
            p = jnp.exp(s - m)
            l = jnp.sum(p, axis=-1, keepdims=True)
            o_ref[:, hs] = (_dot(p, v_ref[:, hs]) / l).astype(o_ref.dtype)
            lse_ref[:, h:h + 1] = m + jnp.log(l)

    return _pcall(
        body, ride, grid=(t // bq,), in_specs=_fox_specs(t),
        out_specs=[pl.BlockSpec((bq, GROUP_WIDTH), lambda i: (i, 0)), pl.BlockSpec((bq, LANES), lambda i: (i, 0))],
        out_shape=[jax.ShapeDtypeStruct((t, GROUP_WIDTH), ACT_DTYPE), jax.ShapeDtypeStruct((t, LANES), F32)],
        semantics=("parallel",), name="fox_fwd")(pmm, pmm, pmm, cum, cum_t)


def _fox_bwd(pmm, cum, cum_t, lse, dmix, ride=None):
    t = pmm.shape[0]
    bq, nb = FOX_Q_BLOCK, t // FOX_Q_BLOCK
    scale = HEAD_DIM ** -0.5

    def body(q_ref, k_ref, v_ref, cum_ref, cum_t_ref, lse_ref, do_ref, dq_ref, dk_ref, dv_ref, dct_ref, dcq_ref):
        i = pl.program_id(0)

        @pl.when(i == 0)
        def _():
            dk_ref[...] = jnp.zeros_like(dk_ref)
            dv_ref[...] = jnp.zeros_like(dv_ref)
            dct_ref[...] = jnp.zeros_like(dct_ref)

        dcq_ref[...] = jnp.zeros_like(dcq_ref)
        for h in range(ATT_HEADS):
            hs = slice(h * HEAD_DIM, (h + 1) * HEAD_DIM)
            s = _fox_scores(q_ref, k_ref, cum_ref, cum_t_ref, h, i)
            p = jnp.exp(s - lse_ref[:, h:h + 1])
            do = do_ref[:, hs]
            dp = _dot(do, v_ref[:, hs], 1, 1)
            ds = p * (dp - jnp.sum(p * dp, axis=-1, keepdims=True))
            dq_ref[:, hs] = (_dot(ds, k_ref[:, hs]) * scale).astype(dq_ref.dtype)
            dk_ref[:, hs] += _dot(ds, q_ref[:, hs], 0, 0) * scale
            dv_ref[:, hs] += _dot(p, do, 0, 0)
            key_side = -jnp.sum(ds, axis=0, keepdims=True)
            for kb in range(nb):
                dct_ref[kb, h:h + 1, :] += key_side[:, kb * bq:(kb + 1) * bq]
            dcq_ref[:, h:h + 1] = jnp.sum(ds, axis=1, keepdims=True)

    whole = pl.BlockSpec((t, GROUP_WIDTH), lambda i: (0, 0))
    return _pcall(
        body, ride, grid=(t // bq,),
        in_specs=_fox_specs(t) + [pl.BlockSpec((bq, LANES), lambda i: (i, 0)),
                                  pl.BlockSpec((bq, GROUP_WIDTH), lambda i: (i, 1))],
        out_specs=[pl.BlockSpec((bq, GROUP_WIDTH), lambda i: (i, 0)), whole, whole,
                   pl.BlockSpec((nb, LANES, bq), lambda i: (0, 0, 0)), pl.BlockSpec((bq, LANES), lambda i: (i, 0))],
        out_shape=[jax.ShapeDtypeStruct((t, GROUP_WIDTH), ACT_DTYPE), jax.ShapeDtypeStruct((t, GROUP_WIDTH), F32),
                   jax.ShapeDtypeStruct((t, GROUP_WIDTH), F32), jax.ShapeDtypeStruct((nb, LANES, bq), F32),
                   jax.ShapeDtypeStruct((t, LANES), F32)],
        semantics=("arbitrary",), name="fox_bwd")(pmm, pmm, pmm, cum, cum_t, lse, dmix)


CA_Q_BLOCK = 4 * CHUNK
CA_WINDOW = CA_Q_BLOCK + CA_LEFT
CA_BASE = 1024


def _ca_bias_base(rel_bias):
    n = rel_bias.shape[0]
    flat = CA_Q_BLOCK + CA_LEFT - REL_CLIP
    tail = CA_BASE - flat - (2 * REL_CLIP + 1)
    return jnp.concatenate([jnp.broadcast_to(rel_bias[:, 2 * REL_CLIP:], (n, flat)), rel_bias[:, ::-1],
                            jnp.broadcast_to(rel_bias[:, :1], (n, tail))], axis=1)


def _ca_bias_base_grad(dbase):
    flat = CA_Q_BLOCK + CA_LEFT - REL_CLIP
    mid = dbase[:, flat:flat + 2 * REL_CLIP + 1][:, ::-1]
    lo = jnp.sum(dbase[:, flat + 2 * REL_CLIP + 1:], axis=1, keepdims=True)
    hi = jnp.sum(dbase[:, :flat], axis=1, keepdims=True)
    pad = jnp.zeros((dbase.shape[0], 2 * REL_CLIP - 1), F32)
    return mid + jnp.concatenate([lo, pad, hi], axis=1)


def _ca_mask(i):
    r, j = _iota((CA_Q_BLOCK, CA_WINDOW), 0), _iota((CA_Q_BLOCK, CA_WINDOW), 1)
    rc, jc = r // CHUNK, j // CHUNK
    return (jc >= rc) & (jc <= rc + CA_LEFT // CHUNK) & (i * CA_Q_BLOCK + j >= CA_LEFT)


def _ca_fill_bias(i, base_ref, bias_scr):
    @pl.when(i == 0)
    def _():
        for h in range(ATT_HEADS):
            rows = jnp.broadcast_to(base_ref[h:h + 1, :], (CA_Q_BLOCK, CA_BASE))
            bias_scr[h] = pltpu.roll(rows, CA_BASE - CA_Q_BLOCK, 1, stride=1, stride_axis=0)[:, :CA_WINDOW]


def _ca_scores(q_ref, kp_ref, bias_scr, win, h, mask):
    hs = slice(h * HEAD_DIM, (h + 1) * HEAD_DIM)
    s = _dot(q_ref[:, hs], kp_ref[win, hs], 1, 1) * (HEAD_DIM ** -0.5)
    return jnp.where(mask, s + bias_scr[h], NEG)


CA_BIAS_SCRATCH = pltpu.VMEM((ATT_HEADS, CA_Q_BLOCK, CA_WINDOW), F32)


def _ca_fwd(pmm, kp, vp, base, ride=None):
    t = pmm.shape[0]

    def body(q_ref, kp_ref, vp_ref, base_ref, o_ref, lse_ref, bias_scr):
        i = pl.program_id(0)
        _ca_fill_bias(i, base_ref, bias_scr)
        win = pl.ds(pl.multiple_of(i * CA_Q_BLOCK, CA_Q_BLOCK), CA_WINDOW)
        mask = _ca_mask(i)
        lse_ref[...] = jnp.zeros_like(lse_ref)
        for h in range(ATT_HEADS):
            hs = slice(h * HEAD_DIM, (h + 1) * HEAD_DIM)
            s = _ca_scores(q_ref, kp_ref, bias_scr, win, h, mask)
            m = jnp.max(s, axis=-1, keepdims=True)
            p = jnp.exp(s - m)
            l = jnp.sum(p, axis=-1, keepdims=True)
            o_ref[:, hs] = (_dot(p, vp_ref[win, hs]) / l).astype(o_ref.dtype)
            lse_ref[:, h:h + 1] = m + jnp.log(l)

    padded = pl.BlockSpec((t + CA_LEFT, GROUP_WIDTH), lambda i: (0, 0))
    return _pcall(
        body, ride, grid=(t // CA_Q_BLOCK,),
        in_specs=[pl.BlockSpec((CA_Q_BLOCK, GROUP_WIDTH), lambda i: (i, 0)), padded, padded,
                  pl.BlockSpec((ATT_HEADS, CA_BASE), lambda i: (0, 0))],
        out_specs=[pl.BlockSpec((CA_Q_BLOCK, GROUP_WIDTH), lambda i: (i, 0)),
                   pl.BlockSpec((CA_Q_BLOCK, LANES), lambda i: (i, 0))],
        out_shape=[jax.ShapeDtypeStruct((t, GROUP_WIDTH), ACT_DTYPE), jax.ShapeDtypeStruct((t, LANES), F32)],
        scratch_shapes=[CA_BIAS_SCRATCH], semantics=("arbitrary",), name="ca_fwd")(pmm, kp, vp, base)


def _ca_bwd(pmm, kp, vp, base, lse, dmix, ride=None):
    t = pmm.shape[0]
    scale = HEAD_DIM ** -0.5

    def body(q_ref, kp_ref, vp_ref, base_ref, lse_ref, do_ref, dq_ref, dkp_ref, dvp_ref, dbase_ref, bias_scr):
        i = pl.program_id(0)
        _ca_fill_bias(i, base_ref, bias_scr)

        @pl.when(i == 0)
        def _():
            dkp_ref[...] = jnp.zeros_like(dkp_ref)
            dvp_ref[...] = jnp.zeros_like(dvp_ref)
            dbase_ref[...] = jnp.zeros_like(dbase_ref)

        win = pl.ds(pl.multiple_of(i * CA_Q_BLOCK, CA_Q_BLOCK), CA_WINDOW)
        mask = _ca_mask(i)
        flip = (_iota((CA_Q_BLOCK, CA_Q_BLOCK), 0) + _iota((CA_Q_BLOCK, CA_Q_BLOCK), 1) == CA_Q_BLOCK - 1).astype(F32)
        for h in range(ATT_HEADS):
            hs = slice(h * HEAD_DIM, (h + 1) * HEAD_DIM)
            s = _ca_scores(q_ref, kp_ref, bias_scr, win, h, mask)
            p = jnp.exp(s - lse_ref[:, h:h + 1])
            do = do_ref[:, hs]
            dp = _dot(do, vp_ref[win, hs], 1, 1)
            ds = p * (dp - jnp.sum(p * dp, axis=-1, keepdims=True))
            dq_ref[:, hs] = (_dot(ds, kp_ref[win, hs]) * scale).astype(dq_ref.dtype)
            dkp_ref[win, hs] += _dot(ds, q_ref[:, hs], 0, 0) * scale
            dvp_ref[win, hs] += _dot(p, do, 0, 0)
            rev = jnp.concatenate([_dot(flip, ds), jnp.zeros((CA_Q_BLOCK, CA_BASE - CA_WINDOW), F32)], axis=1)
            lined = pltpu.roll(rev, 1, 1, stride=1, stride_axis=0)
            dbase_ref[h:h + 1, :] += jnp.sum(lined, axis=0, keepdims=True)

    padded = pl.BlockSpec((t + CA_LEFT, GROUP_WIDTH), lambda i: (0, 0))
    return _pcall(
        body, ride, grid=(t // CA_Q_BLOCK,),
        in_specs=[pl.BlockSpec((CA_Q_BLOCK, GROUP_WIDTH), lambda i: (i, 0)), padded, padded,
                  pl.BlockSpec((ATT_HEADS, CA_BASE), lambda i: (0, 0)),
                  pl.BlockSpec((CA_Q_BLOCK, LANES), lambda i: (i, 0)),
                  pl.BlockSpec((CA_Q_BLOCK, GROUP_WIDTH), lambda i: (i, 0))],
        out_specs=[pl.BlockSpec((CA_Q_BLOCK, GROUP_WIDTH), lambda i: (i, 0)), padded, padded,
                   pl.BlockSpec((ATT_HEADS, CA_BASE), lambda i: (0, 0))],
        out_shape=[jax.ShapeDtypeStruct((t, GROUP_WIDTH), ACT_DTYPE),
                   jax.ShapeDtypeStruct((t + CA_LEFT, GROUP_WIDTH), F32),
                   jax.ShapeDtypeStruct((t + CA_LEFT, GROUP_WIDTH), F32),
                   jax.ShapeDtypeStruct((ATT_HEADS, CA_BASE), F32)],
        scratch_shapes=[CA_BIAS_SCRATCH], semantics=("arbitrary",), name="ca_bwd")(pmm, kp, vp, base, lse, dmix)


GELU_C = 0.7978845608028654
GELU_A = 0.044715


def _shift_down(v, k, fill):
    return jnp.where(_iota(v.shape, 0) >= k, pltpu.roll(v, k, 0), fill)


def _shift_up(v, k, fill):
    t = v.shape[0]
    return jnp.where(_iota(v.shape, 0) < t - k, pltpu.roll(v, t - k, 0), fill)


def _linear_scan(a, b, shift):
    k = 1
    while k < a.shape[0]:
        b = a * shift(b, k, 0.0) + b
        a = a * shift(a, k, 1.0)
        k *= 2
    return b


def _neg_expm1(y):
    series = -y * (1.0 + y * (0.5 + y * (1.0 / 6.0 + y * (1.0 / 24.0 + y * (1.0 / 120.0)))))
    return jnp.where(y > -0.1, series, 1.0 - jnp.exp(y))


def _lru_forward(x, g_in, cw, cb, wa, ba, wx, bx, lam):
    xs = [_shift_down(x, CONV_WIDTH - 1 - j, 0.0) for j in range(CONV_WIDTH - 1)] + [x]
    xc = cb + sum(cw[j:j + 1, :] * xs[j] for j in range(CONV_WIDTH))
    r = jax.nn.sigmoid(_dot(xc, wa) + ba)
    i = jax.nn.sigmoid(_dot(xc, wx) + bx)
    lsl = _log_sigmoid(lam)
    la = LRU_C * r * lsl
    a = jnp.exp(la)
    s = jnp.sqrt(_neg_expm1(2.0 * la))
    h = _linear_scan(a, s * (i * xc), _shift_down)
    u = GELU_C * (g_in + GELU_A * g_in * g_in * g_in)
    th = jnp.tanh(u)
    gelu = 0.5 * g_in * (1.0 + th)
    return xs, xc, r, i, lsl, a, s, h, th, gelu


def _lru_specs(t):
    col = lambda off: pl.BlockSpec((t, LANES), lambda j: (0, j + off))
    vec = pl.BlockSpec((1, LANES), lambda j: (0, j))
    mat = pl.BlockSpec((None, LANES, LANES), lambda j: (j, 0, 0))
    return [col(0), col(GROUP_WIDTH // LANES), pl.BlockSpec((CONV_WIDTH, LANES), lambda j: (0, j)),
            vec, mat, vec, mat, vec, vec]


def _lru_fwd(pel, conv_w, conv_b, wa, ba, wx, bx, lam, ride=None):
    t = pel.shape[0]

    def body(g_ref, x_ref, cw_ref, cb_ref, wa_ref, ba_ref, wx_ref, bx_ref, lam_ref, o_ref):
        res = _lru_forward(x_ref[...], g_ref[...], cw_ref[...], cb_ref[...], wa_ref[...], ba_ref[...],
                           wx_ref[...], bx_ref[...], lam_ref[...])
        o_ref[...] = (res[7] * res[9]).astype(o_ref.dtype)

    return _pcall(
        body, ride, grid=(GROUP_WIDTH // LANES,), in_specs=_lru_specs(t),
        out_specs=pl.BlockSpec((t, LANES), lambda j: (0, j)),
        out_shape=jax.ShapeDtypeStruct((t, GROUP_WIDTH), ACT_DTYPE),
        semantics=("parallel",), name="lru_fwd")(pel, pel, conv_w, conv_b, wa, ba, wx, bx, lam)


def _lru_bwd(pel, conv_w, conv_b, wa, ba, wx, bx, lam, dmix, ride=None):
    t = pel.shape[0]

    def body(g_ref, x_ref, cw_ref, cb_ref, wa_ref, ba_ref, wx_ref, bx_ref, lam_ref, do_ref,
             dg_ref, dx_ref, dcw_ref, dcb_ref, dwa_ref, dba_ref, dwx_ref, dbx_ref, dlam_ref):
        g_in, cw, lam = g_ref[...], cw_ref[...], lam_ref[...]
        xs, xc, r, i, lsl, a, s, h, th, gelu = _lru_forward(
            x_ref[...], g_in, cw, cb_ref[...], wa_ref[...], ba_ref[...], wx_ref[...], bx_ref[...], lam)
        dout = do_ref[...]
        dgelu = 0.5 * (1.0 + th) + 0.5 * g_in * (1.0 - th * th) * GELU_C * (1.0 + 3.0 * GELU_A * g_in * g_in)
        dg_ref[...] = (dout * h * dgelu).astype(dg_ref.dtype)
        gsum = _linear_scan(_shift_up(a, 1, 0.0), dout * gelu, _shift_up)
        da = gsum * _shift_down(h, 1, 0.0)
        di = gsum * s * xc
        dla = da * a - gsum * (i * xc) * (a * a / s)
        dlam_ref[...] = jnp.sum(dla * (LRU_C * r), axis=0, keepdims=True) * jax.nn.sigmoid(-lam)
        dpr = dla * (LRU_C * lsl) * r * (1.0 - r)
        dpi = di * i * (1.0 - i)
        dxc = gsum * s * i + _dot(dpr, wa_ref[...], 1, 1) + _dot(dpi, wx_ref[...], 1, 1)
        xct = xc.T
        dwa_ref[...] = _dot(xct, dpr)
        dwx_ref[...] = _dot(xct, dpi)
        dba_ref[...] = jnp.sum(dpr, axis=0, keepdims=True)
        dbx_ref[...] = jnp.sum(dpi, axis=0, keepdims=True)
        dcb_ref[...] = jnp.sum(dxc, axis=0, keepdims=True)
        for j in range(CONV_WIDTH):
            dcw_ref[j:j + 1, :] = jnp.sum(dxc * xs[j], axis=0, keepdims=True)
        dx = cw[CONV_WIDTH - 1:CONV_WIDTH, :] * dxc
        for j in range(CONV_WIDTH - 1):
            dx = dx + cw[j:j + 1, :] * _shift_up(dxc, CONV_WIDTH - 1 - j, 0.0)
        dx_ref[...] = dx.astype(dx_ref.dtype)

    col = pl.BlockSpec((t, LANES), lambda j: (0, j))
    vec = pl.BlockSpec((1, LANES), lambda j: (0, j))
    mat = pl.BlockSpec((None, LANES, LANES), lambda j: (j, 0, 0))
    nb = GROUP_WIDTH // LANES
    vshape = jax.ShapeDtypeStruct((1, GROUP_WIDTH), F32)
    mshape = jax.ShapeDtypeStruct((nb, LANES, LANES), F32)
    return _pcall(
        body, ride, grid=(nb,),
        in_specs=_lru_specs(t) + [pl.BlockSpec((t, LANES), lambda j: (0, j + nb))],
        out_specs=[col, col, pl.BlockSpec((CONV_WIDTH, LANES), lambda j: (0, j)), vec, mat, vec, mat, vec, vec],
        out_shape=[jax.ShapeDtypeStruct((t, GROUP_WIDTH), ACT_DTYPE), jax.ShapeDtypeStruct((t, GROUP_WIDTH), ACT_DTYPE),
                   jax.ShapeDtypeStruct((CONV_WIDTH, GROUP_WIDTH), F32), vshape, mshape, vshape, mshape, vshape, vshape],
        semantics=("parallel",), name="lru_bwd")(
            pel, pel, conv_w, conv_b, wa, ba, wx, bx, lam, dmix)


def _block_diag_pairs(w):
    z = jnp.zeros((LRU_BLOCK_DIM, LRU_BLOCK_DIM), w.dtype)
    return jnp.stack([jnp.block([[w[2 * j], z], [z, w[2 * j + 1]]]) for j in range(w.shape[0] // 2)])


def _block_diag_pairs_grad(dw):
    b = LRU_BLOCK_DIM
    return jnp.stack([dw[n // 2, (n % 2) * b:(n % 2 + 1) * b, (n % 2) * b:(n % 2 + 1) * b] for n in range(2 * dw.shape[0])])


def _row_tile(r):
    return ROW_TILE if r % ROW_TILE == 0 else r


def _pair_sum(g, got, place, name):
    _, r, c = g.shape
    tile = r

    def body(place_ref, a_ref, b_ref, o_ref):
        o_ref[...] = (a_ref[...].astype(F32) + b_ref[...].astype(F32)).astype(o_ref.dtype)

    blk = pl.BlockSpec((1, tile, c), lambda k, i, place_ref: (k, i, 0))
    return pl.pallas_call(
        body,
        grid_spec=pltpu.PrefetchScalarGridSpec(
            num_scalar_prefetch=1, grid=(N_CHIPS, r // tile),
            in_specs=[pl.BlockSpec((1, tile, c), lambda k, i, place_ref: (2 * k + place_ref[0], i, 0)), blk],
            out_specs=blk),
        out_shape=jax.ShapeDtypeStruct(got.shape, got.dtype),
        compiler_params=_params("parallel", "parallel"), name=name)(place, g, got)


def _adamw_update(g, w_ref, m_ref, v_ref, g_ref, d_ref, nm_ref, nv_ref):
    nm = ADAM_B1 * m_ref[...] + (1.0 - ADAM_B1) * g
    nv = ADAM_B2 * v_ref[...] + (1.0 - ADAM_B2) * jnp.square(g)
    m_hat = nm / (1.0 - ADAM_B1 ** ADAM_STEP)
    v_hat = nv / (1.0 - ADAM_B2 ** ADAM_STEP)
    g_ref[...] = g
    d_ref[...] = -ADAM_LR * (m_hat / (jnp.sqrt(v_hat) + ADAM_EPS) + ADAM_WD * w_ref[...])
    nm_ref[...] = nm
    nv_ref[...] = nv


def _adamw_sharded(parts, w, m, v, place, name):
    n_layers, r, c = w.shape
    tile = _row_tile(r)
    nb = r // tile

    def body(place_ref, *refs):
        layer = pl.program_id(0)
        g = None
        for l in range(n_layers):
            s_ref, r_ref = refs[2 * l], refs[2 * l + 1]
            g_l = s_ref[0].astype(F32) + r_ref[0].astype(F32) + r_ref[1].astype(F32) + r_ref[2].astype(F32)
            g = g_l if g is None else jnp.where(layer == l, g_l, g)
        _adamw_update(g, *refs[2 * n_layers:])

    def part_specs(l):
        rows = lambda q, i: jnp.where(q < l, 0, jnp.where(q > l, nb - 1, i))
        return [pl.BlockSpec((1, tile, c), lambda q, i, place_ref: (place_ref[1], rows(q, i), 0)),
                pl.BlockSpec((3, tile, c), lambda q, i, place_ref: (0, rows(q, i), 0))]

    in_specs, args = [], []
    for l, (s, recv) in enumerate(parts):
        in_specs += part_specs(l)
        args += [s, recv]
    blk = pl.BlockSpec((None, tile, c), lambda q, i, place_ref: (q, i, 0))
    out = jax.ShapeDtypeStruct((n_layers, r, c), F32)
    return pl.pallas_call(
        body,
        grid_spec=pltpu.PrefetchScalarGridSpec(
            num_scalar_prefetch=1, grid=(n_layers, nb), in_specs=in_specs + [blk, blk, blk],
            out_specs=[blk, blk, blk, blk]),
        out_shape=[out, out, out, out], compiler_params=_params("arbitrary", "arbitrary"), name=name)(
            place, *args, w, m, v)


def _adamw_replicated(parts, w, m, v, name):
    p, r, c = parts.shape
    tile = _row_tile(r)

    def body(p_ref, w_ref, m_ref, v_ref, *outs):
        g = p_ref[0].astype(F32)
        for k in range(1, p):
            g = g + p_ref[k].astype(F32)
        _adamw_update(g, w_ref, m_ref, v_ref, *outs)

    blk = pl.BlockSpec((tile, c), lambda i: (i, 0))
    out = jax.ShapeDtypeStruct((r, c), F32)
    return pl.pallas_call(body, grid=(r // tile,),
                          in_specs=[pl.BlockSpec((p, tile, c), lambda i: (0, i, 0)), blk, blk, blk],
                          out_specs=[blk, blk, blk, blk], out_shape=[out, out, out, out],
                          compiler_params=_params("parallel"), name=name)(parts, w, m, v)


SLAB_COLS = 1024
SHARDED = {"norm_w": 2, "w_in_even": 2, "gla_w_a_up": 2, "w_out_even": 1, "w_in_odd": 2, "conv_w": 2, "conv_b": 1,
           "lru_b_a": 1, "lru_b_x": 1, "lru_lambda": 1, "w_out_odd": 1, "w_mlp_up": 2, "w_mlp_down": 1}
REPLICATED = ["gla_b_a", "gla_norm_w", "fox_b_f", "rel_bias", "lru_w_a", "lru_w_x"]
WEIGHTS = ["norm_w", "w_in_even", "gla_w_a_up", "gla_b_a", "gla_norm_w", "fox_b_f", "w_out_even", "w_in_odd",
           "rel_bias", "conv_w", "conv_b", "lru_w_a", "lru_b_a", "lru_w_x", "lru_b_x", "lru_lambda", "w_out_odd",
           "w_mlp_up", "w_mlp_down"]
MATRICES = ("w_in_even", "w_out_even", "w_in_odd", "w_out_odd", "w_mlp_up", "w_mlp_down")
VECTORS = tuple(n for n in SHARDED if n not in MATRICES)
VEC_SLAB_ROWS = 16
REPL_SLAB_ROWS = 72
MATRIX_BLOCKS = (("w_in_even", 0), ("w_out_even", 0), ("w_in_odd", 0), ("w_out_odd", 0),
                 ("w_mlp_up", 0), ("w_mlp_up", 1), ("w_mlp_down", 0), ("w_mlp_down", 1))


def _rows_of(shape):
    n = 1
    for s in shape:
        n *= s
    return -(-n // SLAB_COLS), n


def _pack(arrays, total_rows, lead=()):
    parts, used = [], 0
    for a in arrays:
        rows, n = _rows_of(a.shape[len(lead):])
        flat = a.reshape(lead + (n,))
        flat = jnp.pad(flat, [(0, 0)] * len(lead) + [(0, rows * SLAB_COLS - n)])
        parts.append(flat.reshape(lead + (rows, SLAB_COLS)))
        used += rows
    parts.append(jnp.zeros(lead + (total_rows - used, SLAB_COLS), parts[0].dtype))
    return jnp.concatenate(parts, axis=len(lead))


def _unpack(slab, shapes, lead=()):
    out, row = [], 0
    for shape in shapes:
        rows, n = _rows_of(shape)
        seg = lax.slice_in_dim(slab, row, row + rows, axis=len(lead))
        out.append(seg.reshape(lead + (rows * SLAB_COLS,))[..., :n].reshape(lead + tuple(shape)))
        row += rows
    return out


def _join_shards(blocks, axis):
    moved = jnp.moveaxis(blocks, 0, axis)
    shape = moved.shape
    return moved.reshape(shape[:axis] + (shape[axis] * shape[axis + 1],) + shape[axis + 2:])


def _split_shards(full, axis):
    shape = full.shape
    cut = full.reshape(shape[:axis] + (N_DEV, shape[axis] // N_DEV) + shape[axis + 1:])
    return jnp.moveaxis(cut, axis, 0)


EVEN_SPLITS = (0, 256, 512, 1024, 1536, 1552, 2064, 2576, 3088, 3096)


def _even_in_split(w):
    c = [w[:, EVEN_SPLITS[k]:EVEN_SPLITS[k + 1]] for k in range(9)]
    gq, gk, gv, gr, ga, fq, fk, fv, ff = c
    padcols = lambda a: jnp.pad(a, ((0, 0), (0, LANES - a.shape[1])))
    return jnp.concatenate([gq, gk, gv, fq, fk, fv], axis=1), jnp.concatenate([gr, padcols(ga), padcols(ff)], axis=1)


def _even_in_merge(dmm, dele):
    return jnp.concatenate([dmm[:, :1024], dele[:, :512], dele[:, 512:512 + GLA_RANK], dmm[:, 1024:2560],
                            dele[:, 640:640 + ATT_HEADS]], axis=1)


def _column_shards(full):
    r, c = full.shape
    return jnp.moveaxis(full.reshape(r, N_DEV, c // N_DEV), 1, 0)


def _forward_backward(x, target, shard, vec_slab, vec_shapes, w, place):
    w = dict(w)
    g, dnorm, sums, recv = {}, {}, {}, {}
    nrm = lambda l, k: w["norm_w"][l, k][None, :]
    gather = lambda *keys: _gather_plan([shard[k] for k in keys])
    blocks = lambda r, c: (N_DEV, r // N_DEV, c)

    def pair_sum(key):
        sums[key] = _pair_sum(g[key], got[key], place, f"rs_pair_sum_{key[0]}_{key[1]}")

    got = {}

    def mlp_fwd(xin, layer, ride_up, ride_down):
        h = _norm_fwd(xin, nrm(layer, 2), out_dtype=ACT_DTYPE, name=f"norm_mlp_{layer}")
        u = _mm(h, w["w_mlp_up"][layer], out_dtype=ACT_DTYPE, tm=TM_FWD, tn=D_FF // N_DEV, b_blocked=True,
                name=f"mlp_up_{layer}", ride=ride_up)
        u, rode_up = u if ride_up is not None else (u, None)
        yv = _mm(u, w["w_mlp_down"][layer], out_dtype=F32, tm=TM_DX, tn=TN, a_sqrelu=True,
                 name=f"mlp_down_{layer}", ride=ride_down)
        yv, rode_down = yv if ride_down is not None else (yv, None)
        xout = _norm_fwd(yv, nrm(layer, 3), out_dtype=F32, res=xin, name=f"norm_mlp_out_{layer}")
        return xout, (xin, h, u, yv), rode_up, rode_down

    def mlp_bwd(dxout, saved, layer, ride):
        xin, h, u, yv = saved
        k_up, k_down = ("w_mlp_up", layer), ("w_mlp_down", layer)
        dy, dnorm[(layer, 3)] = _norm_bwd(dxout, yv, nrm(layer, 3), out_dtype=ACT_DTYPE, name=f"norm_mlp_out_bwd_{layer}")
        du = _mm(dy, w["w_mlp_down"][layer], nt=True, out_dtype=ACT_DTYPE, tm=TM_DX, tn=TN, drelu_of=u,
                 name=f"mlp_down_dx_{layer}", ride=ride)
        rode = None
        if ride is not None:
            du, rode = du
        g[k_down] = _mm(u, dy, ta=True, out_dtype=WIRE_DTYPE, tm=TM_DW, tn=TN, a_sqrelu=True,
                        name=f"mlp_down_dw_{layer}").reshape(blocks(D_FF, D_MODEL))
        g[k_up] = _mm(h, du, ta=True, out_dtype=WIRE_DTYPE, tm=TM_DW, tn=D_FF // N_DEV, out_blocked=True,
                      name=f"mlp_up_dw_{layer}")
        w_up = jnp.moveaxis(w["w_mlp_up"][layer], 0, 1).reshape(D_MODEL, D_FF)
        dh, (got[k_down], got[k_up]) = _mm(du, w_up, nt=True, out_dtype=F32, tm=TM_DX, tn=TN, name=f"mlp_up_dx_{layer}",
                                           ride=_sibling_plan([g[k_down], g[k_up]]))
        pair_sum(k_down)
        pair_sum(k_up)
        dxin, dnorm[(layer, 2)] = _norm_bwd(dh, xin, nrm(layer, 2), out_dtype=F32, add=dxout, name=f"norm_mlp_bwd_{layer}")
        return dxin, rode

    first = _run_plan(_gather_plan([shard[("w_in_even", 0)], vec_slab]), "weights_all_gather_first")
    w["w_in_even"] = _join_shards(first[0], 1)
    for n, b in zip(VECTORS, _unpack(first[1], vec_shapes, lead=(N_DEV,))):
        w[n] = _join_shards(b, SHARDED[n])
    w["w_mlp_up"], w["w_mlp_down"] = [None] * DEPTH, [None] * DEPTH

    wmm_e, wel_e = _even_in_split(w["w_in_even"])
    w_up_pad = jnp.pad(w["gla_w_a_up"][0], ((0, LANES - GLA_RANK), (0, 0)))
    b_f_pad = jnp.pad(w["fox_b_f"], ((0, 0), (0, LANES - ATT_HEADS)))
    h0 = _norm_fwd(x, nrm(0, 0), out_dtype=ACT_DTYPE, name="norm_in_0")
    pmm0, (w_out_even,) = _mm(h0, wmm_e, out_dtype=ACT_DTYPE, tm=TM_FWD, tn=TN, name="in_even_mm",
                              ride=gather(("w_out_even", 0)))
    pel0 = _mm(h0, wel_e, out_dtype=F32, tm=TM_FWD, tn=768, name="in_even_el")
    (out_a, states), (w["w_mlp_up"][0],) = _gla_fwd(pmm0, pel0, w_up_pad, w["gla_b_a"], w["gla_norm_w"],
                                                    ride=gather(("w_mlp_up", 0)))
    cum, cum_t = _fox_gate_fwd(pel0, b_f_pad)
    (out_b, lse_b), (w_mlp_down0, w_in_odd) = _fox_fwd(pmm0, cum, cum_t, ride=gather(("w_mlp_down", 0), ("w_in_odd", 0)))
    w["w_out_even"] = w_out_even.reshape(D_MODEL, D_MODEL)
    w["w_mlp_down"][0] = w_mlp_down0.reshape(D_FF, D_MODEL)
    w["w_in_odd"] = _join_shards(w_in_odd, 1)
    mix_in0 = jnp.concatenate([out_a, out_b], axis=1)
    mix0 = _mm(mix_in0, w["w_out_even"], out_dtype=F32, tm=TM_FWD, tn=TN, name="out_even")
    x1 = _norm_fwd(mix0, nrm(0, 1), out_dtype=F32, res=x, name="norm_mix_0")
    x2, mlp0, (w_out_odd,), (w["w_mlp_up"][1],) = mlp_fwd(x1, 0, gather(("w_out_odd", 0)), gather(("w_mlp_up", 1)))
    w["w_out_odd"] = w_out_odd.reshape(D_MODEL, D_MODEL)

    w_in_o = w["w_in_odd"]
    n_mm_o = 3 * GROUP_WIDTH
    wa_bd, wx_bd = _block_diag_pairs(w["lru_w_a"][0]), _block_diag_pairs(w["lru_w_x"][0])
    base = _ca_bias_base(w["rel_bias"][0])
    h1 = _norm_fwd(x2, nrm(1, 0), out_dtype=ACT_DTYPE, name="norm_in_1")
    pmm1 = _mm(h1, w_in_o[:, :n_mm_o], out_dtype=ACT_DTYPE, tm=TM_FWD, tn=TN, name="in_odd_mm")
    pel1 = _mm(h1, w_in_o[:, n_mm_o:], out_dtype=F32, tm=TM_FWD, tn=TN, name="in_odd_el")
    kp = jnp.pad(pmm1[:, GROUP_WIDTH:2 * GROUP_WIDTH], ((CA_LEFT, 0), (0, 0)))
    vp = jnp.pad(pmm1[:, 2 * GROUP_WIDTH:], ((CA_LEFT, 0), (0, 0)))
    (out_c, lse_c), (w_mlp_down1,) = _ca_fwd(pmm1, kp, vp, base, ride=gather(("w_mlp_down", 1)))
    w["w_mlp_down"][1] = w_mlp_down1.reshape(D_FF, D_MODEL)
    lru_args = (pel1, w["conv_w"][0], w["conv_b"], wa_bd, w["lru_b_a"], wx_bd, w["lru_b_x"], w["lru_lambda"])
    out_d = _lru_fwd(*lru_args)
    mix_in1 = jnp.concatenate([out_c, out_d], axis=1)
    mix1 = _mm(mix_in1, w["w_out_odd"], out_dtype=F32, tm=TM_FWD, tn=TN, name="out_odd")
    x3 = _norm_fwd(mix1, nrm(1, 1), out_dtype=F32, res=x2, name="norm_mix_1")
    x4, mlp1, _, _ = mlp_fwd(x3, 1, None, None)

    loss, dx4 = _loss_fwd_bwd(x4, target)

    k_oo, k_io, k_oe, k_ie = ("w_out_odd", 0), ("w_in_odd", 0), ("w_out_even", 0), ("w_in_even", 0)
    mlp_keys = lambda l: [("w_mlp_down", l), ("w_mlp_up", l)]
    dx3, _ = mlp_bwd(dx4, mlp1, 1, None)
    dmix1, dnorm[(1, 1)] = _norm_bwd(dx3, mix1, nrm(1, 1), out_dtype=ACT_DTYPE, name="norm_mix_bwd_1")
    g[k_oo] = _mm(mix_in1, dmix1, ta=True, out_dtype=WIRE_DTYPE, tm=TM_DW, tn=TN, name="out_odd_dw").reshape(
        blocks(D_MODEL, D_MODEL))
    dmix_in1, (got[k_oo],) = _mm(dmix1, w["w_out_odd"], nt=True, out_dtype=F32, tm=TM_DX, tn=TN, name="out_odd_dx",
                                 ride=_sibling_plan([g[k_oo]]))
    (dq_c, dkp, dvp, dbase), rode = _ca_bwd(pmm1, kp, vp, base, lse_c, dmix_in1,
                                            ride=_chip_plan([sums[k] for k in mlp_keys(1)]))
    recv.update(zip(mlp_keys(1), rode))
    pair_sum(k_oo)
    (dgate, dxin, g_conv_w, g_conv_b, dwa_bd, g_lru_b_a, dwx_bd, g_lru_b_x, g_lru_lambda), (recv[k_oo],) = _lru_bwd(
        *lru_args, dmix_in1, ride=_chip_plan([sums[k_oo]]))
    dp1 = jnp.concatenate([dq_c, dkp[CA_LEFT:].astype(ACT_DTYPE), dvp[CA_LEFT:].astype(ACT_DTYPE), dgate, dxin], axis=1)
    g[k_io] = _column_shards(_mm(h1, dp1, ta=True, out_dtype=WIRE_DTYPE, tm=TM_DW, tn=TN, name="in_odd_dw"))
    dh1, (got[k_io],) = _mm(dp1, w_in_o, nt=True, out_dtype=F32, tm=TM_DX, tn=TN, name="in_odd_dx",
                            ride=_sibling_plan([g[k_io]]))
    pair_sum(k_io)
    dx2, dnorm[(1, 0)] = _norm_bwd(dh1, x2, nrm(1, 0), out_dtype=F32, add=dx3, name="norm_in_bwd_1")
    g["rel_bias"] = _ca_bias_base_grad(dbase)[None]
    g["conv_w"], g["conv_b"] = g_conv_w[None], g_conv_b
    g["lru_w_a"], g["lru_w_x"] = _block_diag_pairs_grad(dwa_bd)[None], _block_diag_pairs_grad(dwx_bd)[None]
    g["lru_b_a"], g["lru_b_x"], g["lru_lambda"] = g_lru_b_a, g_lru_b_x, g_lru_lambda

    dx1, (recv[k_io],) = mlp_bwd(dx2, mlp0, 0, _chip_plan([sums[k_io]]))
    dmix0, dnorm[(0, 1)] = _norm_bwd(dx1, mix0, nrm(0, 1), out_dtype=ACT_DTYPE, name="norm_mix_bwd_0")
    g[k_oe] = _mm(mix_in0, dmix0, ta=True, out_dtype=WIRE_DTYPE, tm=TM_DW, tn=TN, name="out_even_dw").reshape(
        blocks(D_MODEL, D_MODEL))
    dmix_in0, (got[k_oe],) = _mm(dmix0, w["w_out_even"], nt=True, out_dtype=F32, tm=TM_DX, tn=TN, name="out_even_dx",
                                 ride=_sibling_plan([g[k_oe]]))
    (dq_a, dk_a, dv_a, dr_a, da_a, dw_up_pad, g_gla_b_a, g_gla_norm_w), rode = _gla_bwd(
        pmm0, pel0, w_up_pad, w["gla_b_a"], w["gla_norm_w"], states, dmix_in0,
        ride=_chip_plan([sums[k] for k in mlp_keys(0)]))
    recv.update(zip(mlp_keys(0), rode))
    pair_sum(k_oe)
    (dq_b, dk_b, dv_b, dcum_t, dcum_q), (recv[k_oe],) = _fox_bwd(pmm0, cum, cum_t, lse_b, dmix_in0,
                                                                 ride=_chip_plan([sums[k_oe]]))
    df_b, db_f = _fox_gate_bwd(pel0, b_f_pad, dcum_t, dcum_q)
    g["gla_w_a_up"] = dw_up_pad[:GLA_RANK][None]
    g["gla_b_a"], g["gla_norm_w"], g["fox_b_f"] = g_gla_b_a, g_gla_norm_w, db_f[:, :ATT_HEADS]
    dp0 = jnp.concatenate([dq_a, dk_a, dv_a, dq_b, dk_b.astype(ACT_DTYPE), dv_b.astype(ACT_DTYPE), dr_a, da_a, df_b],
                          axis=1)
    w_perm = jnp.concatenate([wmm_e, wel_e], axis=1)
    n_mm_e = wmm_e.shape[1]
    dw_perm, (repl_parts,) = _mm(h0, dp0, ta=True, out_dtype=WIRE_DTYPE, tm=TM_DW, tn=dp0.shape[1] // 2, name="in_even_dw",
                                 ride=_gather_plan([_pack([g[n] for n in REPLICATED], REPL_SLAB_ROWS)]))
    g[k_ie] = _column_shards(_even_in_merge(dw_perm[:, :n_mm_e], dw_perm[:, n_mm_e:]))
    dh0, (got[k_ie],) = _mm(dp0, w_perm, nt=True, out_dtype=F32, tm=TM_DX, tn=TN, name="in_even_dx",
                            ride=_sibling_plan([g[k_ie]]))
    pair_sum(k_ie)
    dx0, dnorm[(0, 0)] = _norm_bwd(dh0, x, nrm(0, 0), out_dtype=F32, add=dx1, name="norm_in_bwd_0")

    g["norm_w"] = jnp.stack([jnp.concatenate([dnorm[(l, k)] for k in range(4)], axis=0) for l in range(DEPTH)])
    k_vec = ("vectors", 0)
    vec_grads = _pack([_split_shards(g[n], SHARDED[n]) for n in VECTORS], VEC_SLAB_ROWS, lead=(N_DEV,))
    g[k_vec] = vec_grads.astype(WIRE_DTYPE)
    (got[k_vec],) = _run_plan(_sibling_plan([g[k_vec]]), "rs_sibling_exchange_last")
    pair_sum(k_vec)
    recv[k_ie], recv[k_vec] = _run_plan(_chip_plan([sums[k_ie], sums[k_vec]]), "rs_chip_exchange_last")
    return loss, dx0, sums, recv, repl_parts


def kernel(x, norm_w, w_in_even, gla_w_a_up, gla_b_a, gla_norm_w, fox_b_f, w_out_even, w_in_odd, rel_bias, conv_w, conv_b, lru_w_a, lru_b_a, lru_w_x, lru_b_x, lru_lambda, w_out_odd, w_mlp_up, w_mlp_down, loss_target, m_norm_w, m_w_in_even, m_gla_w_a_up, m_gla_b_a, m_gla_norm_w, m_fox_b_f, m_w_out_even, m_w_in_odd, m_rel_bias, m_conv_w, m_conv_b, m_lru_w_a, m_lru_b_a, m_lru_w_x, m_lru_b_x, m_lru_lambda, m_w_out_odd, m_w_mlp_up, m_w_mlp_down, v_norm_w, v_w_in_even, v_gla_w_a_up, v_gla_b_a, v_gla_norm_w, v_fox_b_f, v_w_out_even, v_w_in_odd, v_rel_bias, v_conv_w, v_conv_b, v_lru_w_a, v_lru_b_a, v_lru_w_x, v_lru_b_x, v_lru_lambda, v_w_out_odd, v_w_mlp_up, v_w_mlp_down):
    wts = dict(zip(WEIGHTS, (norm_w, w_in_even, gla_w_a_up, gla_b_a, gla_norm_w, fox_b_f, w_out_even, w_in_odd, rel_bias,
                             conv_w, conv_b, lru_w_a, lru_b_a, lru_w_x, lru_b_x, lru_lambda, w_out_odd, w_mlp_up,
                             w_mlp_down)))
    mom = dict(zip(WEIGHTS, (m_norm_w, m_w_in_even, m_gla_w_a_up, m_gla_b_a, m_gla_norm_w, m_fox_b_f, m_w_out_even,
                             m_w_in_odd, m_rel_bias, m_conv_w, m_conv_b, m_lru_w_a, m_lru_b_a, m_lru_w_x, m_lru_b_x,
                             m_lru_lambda, m_w_out_odd, m_w_mlp_up, m_w_mlp_down)))
    var = dict(zip(WEIGHTS, (v_norm_w, v_w_in_even, v_gla_w_a_up, v_gla_b_a, v_gla_norm_w, v_fox_b_f, v_w_out_even,
                             v_w_in_odd, v_rel_bias, v_conv_w, v_conv_b, v_lru_w_a, v_lru_b_a, v_lru_w_x, v_lru_b_x,
                             v_lru_lambda, v_w_out_odd, v_w_mlp_up, v_w_mlp_down)))
    vec_shapes = [wts[n].shape for n in VECTORS]
    repl_shapes = [wts[n].shape for n in REPLICATED]
    place = jnp.stack([lax.axis_index("c"), 2 * lax.axis_index("x") + lax.axis_index("y")]).astype(jnp.int32)

    shard = {(n, l): wts[n][l].astype(WIRE_DTYPE) for n, l in MATRIX_BLOCKS}
    vec_slab = _pack([wts[n] for n in VECTORS], VEC_SLAB_ROWS)
    loss_blk, dx, sums, recv, repl_parts = _forward_backward(
        x[0], loss_target[0], shard, vec_slab, vec_shapes, {n: wts[n] for n in REPLICATED}, place)
    loss = lax.psum(loss_blk[0, 0], ("x", "y", "c"))

    k_vec = ("vectors", 0)

    vec_of = lambda d: _pack([d[n] for n in VECTORS], VEC_SLAB_ROWS)[None]
    upd = {n: _adamw_sharded([(sums[(n, l)], recv[(n, l)]) for l in range(wts[n].shape[0])], wts[n], mom[n], var[n],
                             place, f"adamw_{n}") for n in MATRICES}
    vec_upd = _adamw_sharded([(sums[k_vec], recv[k_vec])], vec_of(wts), vec_of(mom), vec_of(var), place, "adamw_vectors")
    rp = _adamw_replicated(repl_parts, _pack([wts[n] for n in REPLICATED], REPL_SLAB_ROWS),
                           _pack([mom[n] for n in REPLICATED], REPL_SLAB_ROWS),
                           _pack([var[n] for n in REPLICATED], REPL_SLAB_ROWS), "adamw_replicated")
    outs = []
    for kind in range(4):
        vals = dict(zip(VECTORS, _unpack(vec_upd[kind][0], vec_shapes)))
        vals.update(zip(REPLICATED, _unpack(rp[kind], repl_shapes)))
        vals.update((n, upd[n][kind]) for n in MATRICES)
        outs += [vals[n] for n in WEIGHTS]
    return (loss, dx[None], *outs)
```

```python
import functools
from typing import Callable, NamedTuple

import jax
import jax.numpy as jnp
from jax import lax
from jax.experimental import pallas as pl
from jax.experimental.pallas import tpu as pltpu

F32 = jnp.float32
MXU_DTYPE = jnp.bfloat16
ACT_DTYPE = jnp.bfloat16
WIRE_DTYPE = jnp.bfloat16

V7X_VMEM_BYTES = 64 * 1024 * 1024
VMEM_LIMIT = (V7X_VMEM_BYTES * 7) // 8
LANES = 128

D_MODEL = 1024
SEQ = 2048
DEPTH = 2
CHUNK = 64
GROUP_WIDTH = D_MODEL // 2
D_FF = 4 * D_MODEL
NORM_EPS = 1e-6
GLA_HEADS = 4
GLA_DV = GROUP_WIDTH // GLA_HEADS
GLA_DK = GLA_DV // 2
GLA_KW = GLA_HEADS * GLA_DK
GLA_RANK = 16
GLA_GATE_TAU = 16.0
HEAD_DIM = 64
ATT_HEADS = GROUP_WIDTH // HEAD_DIM
CA_LEFT = 8 * CHUNK
REL_CLIP = 128
LRU_BLOCK_DIM = 64
CONV_WIDTH = 4
LRU_C = 8.0
N_DEV = 8

ADAM_LR = 0.001
ADAM_B1 = 0.9
ADAM_B2 = 0.999
ADAM_EPS = 1e-08
ADAM_WD = 0.01
ADAM_STEP = 10

NEG = float(jnp.finfo(jnp.float32).min)
MESH = pl.DeviceIdType.MESH


def _params(*sem):
    return pltpu.CompilerParams(dimension_semantics=sem, vmem_limit_bytes=VMEM_LIMIT)


def _dot(a, b, ca=1, cb=0):
    return lax.dot_general(a.astype(MXU_DTYPE), b.astype(MXU_DTYPE), (((ca,), (cb,)), ((), ())),
                           preferred_element_type=F32)


def _dot_exact(a, b):
    return lax.dot_general(a, b, (((1,), (0,)), ((), ())), precision=lax.Precision.HIGHEST,
                           preferred_element_type=F32)


def _log_sigmoid(x):
    return jnp.minimum(x, 0.0) - jnp.log1p(jnp.exp(-jnp.abs(x)))


def _iota(shape, axis):
    return lax.broadcasted_iota(jnp.int32, shape, axis)


ANY = pl.BlockSpec(memory_space=pl.ANY)
N_CHIPS = 4


class _Plan(NamedTuple):
    ins: list
    outs: list
    sems: list
    start: Callable
    finish: Callable


def _place():
    x, y, c = lax.axis_index("x"), lax.axis_index("y"), lax.axis_index("c")
    return x, y, c, [(1 - x, y), (x, 1 - y), (1 - x, 1 - y)]


def _gather_plan(xs):
    n = len(xs)

    def parts(x_refs, out_refs, sems):
        send_sems, recv_sems, local_sems = sems
        x, y, c, chips = _place()
        me, sibling = (x, y, c), (x, y, 1 - c)

        def rows(a, px, py, pc):
            return out_refs[a].at[4 * px + 2 * py + pc]

        def copy(a, k, block, to, src=None):
            return pltpu.make_async_remote_copy(
                src_ref=rows(a, *block) if src is None else src, dst_ref=rows(a, *block),
                send_sem=send_sems.at[7 * a + k], recv_sem=recv_sems.at[7 * a + k], device_id=to, device_id_type=MESH)

        mine = [pltpu.make_async_copy(x_refs[a], rows(a, *me), local_sems.at[a]) for a in range(n)]
        first = []
        for a in range(n):
            first.append(copy(a, 0, me, sibling, src=x_refs[a]))
            first += [copy(a, 1 + j, me, (*chip, c), src=x_refs[a]) for j, chip in enumerate(chips)]
        return c, me, sibling, chips, copy, mine, first

    def start(x_refs, out_refs, sems):
        *_, mine, first = parts(x_refs, out_refs, sems)
        for cp in first + mine:
            cp.start()

    def finish(x_refs, out_refs, sems):
        c, me, sibling, chips, copy, mine, first = parts(x_refs, out_refs, sems)
        passed = []
        for j, chip in enumerate(chips):
            for a in range(n):
                copy(a, 1 + j, (*chip, c), me).wait_recv()
                passed.append(copy(a, 4 + j, (*chip, c), sibling))
                passed[-1].start()
        for a in range(n):
            copy(a, 0, sibling, me).wait_recv()
            for j, chip in enumerate(chips):
                copy(a, 4 + j, (*chip, 1 - c), me).wait_recv()
        for cp in first + passed:
            cp.wait_send()
        for cp in mine:
            cp.wait()

    return _Plan(list(xs), [jax.ShapeDtypeStruct((N_DEV,) + x.shape, x.dtype) for x in xs],
                 [pltpu.SemaphoreType.DMA((7 * n,)), pltpu.SemaphoreType.DMA((7 * n,)), pltpu.SemaphoreType.DMA((n,))],
                 start, finish)


def _exchange_plan(copies_of, ins, outs, per_array):
    n = len(ins)

    def start(in_refs, out_refs, sems):
        for cp in copies_of(in_refs, out_refs, sems):
            cp.start()

    def finish(in_refs, out_refs, sems):
        copies = copies_of(in_refs, out_refs, sems)
        for cp in copies:
            cp.wait_recv()
        for cp in copies:
            cp.wait_send()

    return _Plan(list(ins), outs, [pltpu.SemaphoreType.DMA((per_array * n,)), pltpu.SemaphoreType.DMA((per_array * n,))],
                 start, finish)


def _sibling_plan(gs):
    def copies_of(g_refs, got_refs, sems):
        x, y, c, _ = _place()
        return [pltpu.make_async_remote_copy(
            src_ref=g_refs[a].at[2 * k + (1 - c)], dst_ref=got_refs[a].at[k], send_sem=sems[0].at[N_CHIPS * a + k],
            recv_sem=sems[1].at[N_CHIPS * a + k], device_id=(x, y, 1 - c), device_id_type=MESH)
            for a in range(len(gs)) for k in range(N_CHIPS)]

    return _exchange_plan(copies_of, gs, [jax.ShapeDtypeStruct((N_CHIPS,) + g.shape[1:], g.dtype) for g in gs], N_CHIPS)


def _chip_plan(ss):
    def copies_of(s_refs, out_refs, sems):
        x, y, c, chips = _place()
        return [pltpu.make_async_remote_copy(
            src_ref=s_refs[a].at[2 * px + py], dst_ref=out_refs[a].at[j], send_sem=sems[0].at[3 * a + j],
            recv_sem=sems[1].at[3 * a + j], device_id=(px, py, c), device_id_type=MESH)
            for a in range(len(ss)) for j, (px, py) in enumerate(chips)]

    return _exchange_plan(copies_of, ss, [jax.ShapeDtypeStruct((3,) + s.shape[1:], s.dtype) for s in ss], 3)


def _run_plan(plan, name):
    n_in, n_out = len(plan.ins), len(plan.outs)

    def body(*refs):
        args = refs[:n_in], refs[n_in:n_in + n_out], refs[n_in + n_out:]
        plan.start(*args)
        plan.finish(*args)

    return pl.pallas_call(body, out_shape=plan.outs, in_specs=[ANY] * n_in, out_specs=[ANY] * n_out,
                          scratch_shapes=plan.sems, name=name)(*plan.ins)


def _pcall(body, ride, *, grid, in_specs, out_specs, out_shape, scratch_shapes=(), semantics, name):
    if ride is None:
        return pl.pallas_call(body, grid=grid, in_specs=in_specs, out_specs=out_specs, out_shape=out_shape,
                              scratch_shapes=list(scratch_shapes), compiler_params=_params(*semantics), name=name)
    single = not isinstance(out_shape, (list, tuple))
    out_specs_l, out_shape_l = ([out_specs], [out_shape]) if single else (list(out_specs), list(out_shape))
    n_in, n_out, n_scr = len(in_specs), len(out_shape_l), len(scratch_shapes)
    r_in, r_out = len(ride.ins), len(ride.outs)

    def riding(*refs):
        cuts = [n_in, r_in, n_out, r_out, n_scr]
        groups, at = [], 0
        for width in cuts:
            groups.append(refs[at:at + width])
            at += width
        ins, r_ins, outs, r_outs, scr = groups
        sems = refs[at:]
        first = functools.reduce(jnp.logical_and, [pl.program_id(d) == 0 for d in range(len(grid))])
        last = functools.reduce(jnp.logical_and, [pl.program_id(d) == grid[d] - 1 for d in range(len(grid))])

        @pl.when(first)
        def _():
            ride.start(r_ins, r_outs, sems)

        body(*ins, *outs, *scr)

        @pl.when(last)
        def _():
            ride.finish(r_ins, r_outs, sems)

    call = pl.pallas_call(
        riding, grid=grid, in_specs=list(in_specs) + [ANY] * r_in, out_specs=out_specs_l + [ANY] * r_out,
        out_shape=out_shape_l + list(ride.outs), scratch_shapes=list(scratch_shapes) + list(ride.sems),
        compiler_params=_params(*(["arbitrary"] * len(grid))), name=name)

    def run(*args):
        res = call(*args, *ride.ins)
        return (res[0] if single else list(res[:n_out])), list(res[n_out:])

    return run


def _mm(a, b, *, nt=False, ta=False, out_dtype, tm, tn, a_sqrelu=False, drelu_of=None, b_blocked=False,
        out_blocked=False, name, ride=None):
    k, m = a.shape if ta else a.shape[::-1]
    if b_blocked:
        assert not nt and b.shape[1] == k and b.shape[2] == tn
        n = b.shape[0] * tn
    else:
        n = b.shape[0] if nt else b.shape[1]
        assert (b.shape[1] if nt else b.shape[0]) == k
    tm, tn = min(tm, m), min(tn, n)
    assert m % tm == 0 and n % tn == 0

    def body(*refs):
        a_ref, b_ref = refs[0], refs[1]
        o_ref = refs[-1]
        av = a_ref[...]
        if a_sqrelu:
            av = jnp.square(jnp.maximum(av.astype(F32), 0.0))
        acc = _dot(av, b_ref[...], 0 if ta else 1, 1 if nt else 0)
        if drelu_of is not None:
            acc = acc * (2.0 * jnp.maximum(refs[2][...].astype(F32), 0.0))
        o_ref[...] = acc.astype(out_dtype)

    if b_blocked:
        b_spec = pl.BlockSpec((None, k, tn), lambda i, j: (j, 0, 0))
    elif nt:
        b_spec = pl.BlockSpec((tn, k), lambda i, j: (j, 0))
    else:
        b_spec = pl.BlockSpec((k, tn), lambda i, j: (0, j))
    a_spec = pl.BlockSpec((k, tm), lambda i, j: (0, i)) if ta else pl.BlockSpec((tm, k), lambda i, j: (i, 0))
    in_specs = [a_spec, b_spec]
    args = [a, b]
    if drelu_of is not None:
        in_specs.append(pl.BlockSpec((tm, tn), lambda i, j: (i, j)))
        args.append(drelu_of)
    if out_blocked:
        out_spec = pl.BlockSpec((None, tm, tn), lambda i, j: (j, i, 0))
        out_shape = jax.ShapeDtypeStruct((n // tn, m, tn), out_dtype)
    else:
        out_spec = pl.BlockSpec((tm, tn), lambda i, j: (i, j))
        out_shape = jax.ShapeDtypeStruct((m, n), out_dtype)
    return _pcall(body, ride, grid=(m // tm, n // tn), in_specs=in_specs, out_specs=out_spec, out_shape=out_shape,
                  semantics=("parallel", "parallel"), name=name)(*args)


def _mm_nt_blocked(a, b, *, out_dtype, tm, tn, name):
    m = a.shape[0]
    p, n, kp = b.shape
    assert a.shape[1] == p * kp and m % tm == 0 and n % tn == 0

    def body(a_ref, b_ref, o_ref, acc_ref):
        @pl.when(pl.program_id(2) == 0)
        def _():
            acc_ref[...] = jnp.zeros_like(acc_ref)

        acc_ref[...] += _dot(a_ref[...], b_ref[...], 1, 1)

        @pl.when(pl.program_id(2) == p - 1)
        def _():
            o_ref[...] = acc_ref[...].astype(out_dtype)

    return pl.pallas_call(
        body, grid=(m // tm, n // tn, p),
        in_specs=[pl.BlockSpec((tm, kp), lambda i, j, q: (i, q)), pl.BlockSpec((None, tn, kp), lambda i, j, q: (q, j, 0))],
        out_specs=pl.BlockSpec((tm, tn), lambda i, j, q: (i, j)),
        out_shape=jax.ShapeDtypeStruct((m, n), out_dtype),
        scratch_shapes=[pltpu.VMEM((tm, tn), F32)],
        compiler_params=_params("parallel", "parallel", "arbitrary"), name=name)(a, b)


ROW_TILE = 512
TM_FWD, TM_DX, TM_DW, TN = 2048, 1024, 1024, 512


def _norm_fwd(x, w, *, out_dtype, res=None, name):
    t, d = x.shape

    def body(*refs):
        x_ref, w_ref, o_ref = refs[0], refs[1], refs[-1]
        xv = x_ref[...]
        y = xv * lax.rsqrt(jnp.mean(xv * xv, axis=-1, keepdims=True) + NORM_EPS) * w_ref[...]
        if res is not None:
            y = refs[2][...] + y
        o_ref[...] = y.astype(out_dtype)

    row = pl.BlockSpec((ROW_TILE, d), lambda i: (i, 0))
    in_specs = [row, pl.BlockSpec((1, d), lambda i: (0, 0))] + ([row] if res is not None else [])
    args = [x, w] + ([res] if res is not None else [])
    return pl.pallas_call(body, grid=(t // ROW_TILE,), in_specs=in_specs, out_specs=row,
                          out_shape=jax.ShapeDtypeStruct((t, d), out_dtype),
                          compiler_params=_params("parallel"), name=name)(*args)


def _norm_bwd(dy, x, w, *, out_dtype, add=None, name, ride=None):
    t, d = x.shape

    def body(*refs):
        dy_ref, x_ref, w_ref = refs[0], refs[1], refs[2]
        dx_ref, dw_ref = refs[-2], refs[-1]
        xv = x_ref[...]
        rstd = lax.rsqrt(jnp.mean(xv * xv, axis=-1, keepdims=True) + NORM_EPS)
        xhat = xv * rstd
        dyv = dy_ref[...].astype(F32)
        g = dyv * w_ref[...]
        dx = rstd * (g - xhat * jnp.mean(g * xhat, axis=-1, keepdims=True))
        if add is not None:
            dx = dx + refs[3][...]
        dx_ref[...] = dx.astype(out_dtype)

        @pl.when(pl.program_id(0) == 0)
        def _():
            dw_ref[...] = jnp.zeros_like(dw_ref)

        dw_ref[...] += jnp.sum(dyv * xhat, axis=0, keepdims=True)

    row = pl.BlockSpec((ROW_TILE, d), lambda i: (i, 0))
    vec = pl.BlockSpec((1, d), lambda i: (0, 0))
    in_specs = [row, row, vec] + ([row] if add is not None else [])
    args = [dy, x, w] + ([add] if add is not None else [])
    return _pcall(body, ride, grid=(t // ROW_TILE,), in_specs=in_specs, out_specs=[row, vec],
                  out_shape=[jax.ShapeDtypeStruct((t, d), out_dtype), jax.ShapeDtypeStruct((1, d), F32)],
                  semantics=("arbitrary",), name=name)(*args)


def _loss_fwd_bwd(y, target):
    t, d = y.shape

    def body(y_ref, t_ref, l_ref, dy_ref):
        diff = y_ref[...] - t_ref[...]
        dy_ref[...] = diff * (1.0 / d)

        @pl.when(pl.program_id(0) == 0)
        def _():
            l_ref[...] = jnp.zeros_like(l_ref)

        l_ref[...] += 0.5 * jnp.sum(jnp.mean(diff * diff, axis=-1, keepdims=True), axis=0, keepdims=True)

    row = pl.BlockSpec((ROW_TILE, d), lambda i: (i, 0))
    return pl.pallas_call(body, grid=(t // ROW_TILE,), in_specs=[row, row],
                          out_specs=[pl.BlockSpec((8, LANES), lambda i: (0, 0)), row],
                          out_shape=[jax.ShapeDtypeStruct((8, LANES), F32), jax.ShapeDtypeStruct((t, d), F32)],
                          compiler_params=_params("arbitrary"), name="loss")(y, target)


GLA_UNROLL = (8, 4)


def _gla_specs(t):
    return [pl.BlockSpec((t, GLA_KW), lambda i: (0, 0)),
            pl.BlockSpec((t, GLA_KW), lambda i: (0, 1)),
            pl.BlockSpec((t, GROUP_WIDTH), lambda i: (0, 1)),
            pl.BlockSpec((t, GROUP_WIDTH), lambda i: (0, 0)),
            pl.BlockSpec((t, LANES), lambda i: (0, 4)),
            pl.BlockSpec((LANES, GLA_KW), lambda i: (0, 0)),
            pl.BlockSpec((1, GLA_KW), lambda i: (0, 0)),
            pl.BlockSpec((1, GROUP_WIDTH), lambda i: (0, 0))]


def _gla_fwd(pmm, pel, w_up, b_a, gnorm_w, ride=None):
    t = pmm.shape[0]
    nc = t // CHUNK
    scale = GLA_DK ** -0.5

    def body(q_ref, k_ref, v_ref, r_ref, a_ref, wup_ref, ba_ref, gw_ref, o_ref, st_ref, la_scr, s_scr):
        z = _dot(a_ref[...], wup_ref[...]) + ba_ref[...]
        la_scr[...] = _log_sigmoid(z) * (1.0 / GLA_GATE_TAU)
        s_scr[...] = jnp.zeros_like(s_scr)
        tri = (_iota((CHUNK, CHUNK), 1) <= _iota((CHUNK, CHUNK), 0)).astype(F32)

        def chunk(c, carry):
            rows = pl.ds(pl.multiple_of(c * CHUNK, CHUNK), CHUNK)
            cum = _dot_exact(tri, la_scr[rows, :])
            tot = cum[CHUNK - 1:CHUNK, :]
            kd = k_ref[rows, :].astype(F32) * jnp.exp(tot - cum)
            decay = jnp.exp(tot)
            qs = q_ref[rows, :].astype(F32) * scale
            vv = v_ref[rows, :].astype(F32)
            rr = r_ref[rows, :]
            gate = rr * jax.nn.sigmoid(rr) * gw_ref[...]
            for h in range(GLA_HEADS):
                ks = slice(h * GLA_DK, (h + 1) * GLA_DK)
                vs = slice(h * GLA_DV, (h + 1) * GLA_DV)
                inc_t = _dot(vv[:, vs].T, kd[:, ks])
                s_t = s_scr[vs, :] * decay[:, ks] + inc_t
                s_scr[vs, :] = s_t
                st_ref[c, vs, :] = s_t
                o = _dot(qs[:, ks], s_t, 1, 1)
                y = o * lax.rsqrt(jnp.mean(o * o, axis=-1, keepdims=True) + NORM_EPS)
                o_ref[rows, vs] = (y * gate[:, vs]).astype(o_ref.dtype)
            return carry

        lax.fori_loop(0, nc, chunk, 0, unroll=GLA_UNROLL[0])

    return _pcall(
        body, ride, grid=(1,), in_specs=_gla_specs(t),
        out_specs=[pl.BlockSpec((t, GROUP_WIDTH), lambda i: (0, 0)),
                   pl.BlockSpec((nc, GLA_HEADS * GLA_DV, GLA_DK), lambda i: (0, 0, 0))],
        out_shape=[jax.ShapeDtypeStruct((t, GROUP_WIDTH), ACT_DTYPE),
                   jax.ShapeDtypeStruct((nc, GLA_HEADS * GLA_DV, GLA_DK), F32)],
        scratch_shapes=[pltpu.VMEM((t, GLA_KW), F32), pltpu.VMEM((GLA_HEADS * GLA_DV, GLA_DK), F32)],
        semantics=("arbitrary",), name="gla_fwd")(pmm, pmm, pmm, pel, pel, w_up, b_a, gnorm_w)


def _gla_bwd(pmm, pel, w_up, b_a, gnorm_w, states, dmix, ride=None):
    t = pmm.shape[0]
    nc = t // CHUNK
    scale = GLA_DK ** -0.5

    def body(q_ref, k_ref, v_ref, r_ref, a_ref, wup_ref, ba_ref, gw_ref, st_ref, do_ref,
             dq_ref, dk_ref, dv_ref, dr_ref, da_ref, dwup_ref, dba_ref, dgw_ref, la_scr, dz_scr, ds_scr):
        z = _dot(a_ref[...], wup_ref[...]) + ba_ref[...]
        la_scr[...] = _log_sigmoid(z) * (1.0 / GLA_GATE_TAU)
        ds_scr[...] = jnp.zeros_like(ds_scr)
        dgw_ref[...] = jnp.zeros_like(dgw_ref)
        row_i, col_i = _iota((CHUNK, CHUNK), 0), _iota((CHUNK, CHUNK), 1)
        tri = (col_i <= row_i).astype(F32)
        tri_strict = (col_i < row_i).astype(F32)

        def chunk(n, carry):
            c = nc - 1 - n
            rows = pl.ds(pl.multiple_of(c * CHUNK, CHUNK), CHUNK)
            cum = _dot_exact(tri, la_scr[rows, :])
            tot = cum[CHUNK - 1:CHUNK, :]
            e = jnp.exp(tot - cum)
            kd = k_ref[rows, :].astype(F32) * e
            decay = jnp.exp(tot)
            qs = q_ref[rows, :].astype(F32) * scale
            vv = v_ref[rows, :].astype(F32)
            rr = r_ref[rows, :]
            sig = jax.nn.sigmoid(rr)
            silu = rr * sig
            dsilu = sig * (1.0 + rr * (1.0 - sig))
            dout = do_ref[rows, :]
            gw = gw_ref[...]
            c_prev = jnp.maximum(c - 1, 0)
            has_prev = (c > 0).astype(F32)
            zc = _dot(a_ref[rows, :], wup_ref[...]) + ba_ref[...]
            dz_scale = jax.nn.sigmoid(-zc) * (1.0 / GLA_GATE_TAU)
            for h in range(GLA_HEADS):
                ks = slice(h * GLA_DK, (h + 1) * GLA_DK)
                vs = slice(h * GLA_DV, (h + 1) * GLA_DV)
                s_t = st_ref[c, vs, :]
                s_prev = st_ref[c_prev, vs, :] * has_prev
                o = _dot(qs[:, ks], s_t, 1, 1)
                rstd = lax.rsqrt(jnp.mean(o * o, axis=-1, keepdims=True) + NORM_EPS)
                y = o * rstd
                dg = dout[:, vs]
                dgw_ref[:, vs] += jnp.sum(dg * y * silu[:, vs], axis=0, keepdims=True)
                dr_ref[rows, vs] = (dg * y * gw[:, vs] * dsilu[:, vs]).astype(dr_ref.dtype)
                dy = dg * gw[:, vs] * silu[:, vs]
                d_o = rstd * (dy - y * jnp.mean(dy * y, axis=-1, keepdims=True))
                dq_ref[rows, ks] = (_dot(d_o, s_t) * scale).astype(dq_ref.dtype)
                ds_t = ds_scr[vs, :] + _dot(d_o.T, qs[:, ks])
                dv_ref[rows, vs] = _dot(kd[:, ks], ds_t, 1, 1).astype(dv_ref.dtype)
                dkd = _dot(vv[:, vs], ds_t)
                ddecay = jnp.sum(ds_t * s_prev, axis=0, keepdims=True)
                ds_scr[vs, :] = ds_t * decay[:, ks]
                dla = ddecay * decay[:, ks] + _dot_exact(tri_strict, dkd * kd[:, ks])
                dz_scr[rows, ks] = dla * dz_scale[:, ks]
                dk_ref[rows, ks] = (dkd * e[:, ks]).astype(dk_ref.dtype)
            return carry

        lax.fori_loop(0, nc, chunk, 0, unroll=GLA_UNROLL[1])
        dz = dz_scr[...]
        da_ref[...] = _dot(dz, wup_ref[...], 1, 1).astype(da_ref.dtype)
        dwup_ref[...] = _dot(a_ref[...].T, dz)
        dba_ref[...] = jnp.sum(dz, axis=0, keepdims=True)

    in_specs = _gla_specs(t) + [
        pl.BlockSpec((nc, GLA_HEADS * GLA_DV, GLA_DK), lambda i: (0, 0, 0)),
        pl.BlockSpec((t, GROUP_WIDTH), lambda i: (0, 0))]
    full = lambda r, c: pl.BlockSpec((r, c), lambda i: (0, 0))
    return _pcall(
        body, ride, grid=(1,), in_specs=in_specs,
        out_specs=[full(t, GLA_KW), full(t, GLA_KW), full(t, GROUP_WIDTH), full(t, GROUP_WIDTH), full(t, LANES),
                   full(LANES, GLA_KW), full(1, GLA_KW), full(1, GROUP_WIDTH)],
        out_shape=[jax.ShapeDtypeStruct((t, GLA_KW), ACT_DTYPE), jax.ShapeDtypeStruct((t, GLA_KW), ACT_DTYPE),
                   jax.ShapeDtypeStruct((t, GROUP_WIDTH), ACT_DTYPE), jax.ShapeDtypeStruct((t, GROUP_WIDTH), ACT_DTYPE),
                   jax.ShapeDtypeStruct((t, LANES), ACT_DTYPE), jax.ShapeDtypeStruct((LANES, GLA_KW), F32),
                   jax.ShapeDtypeStruct((1, GLA_KW), F32), jax.ShapeDtypeStruct((1, GROUP_WIDTH), F32)],
        scratch_shapes=[pltpu.VMEM((t, GLA_KW), F32), pltpu.VMEM((t, GLA_KW), F32),
                        pltpu.VMEM((GLA_HEADS * GLA_DV, GLA_DK), F32)],
        semantics=("arbitrary",), name="gla_bwd")(
            pmm, pmm, pmm, pel, pel, w_up, b_a, gnorm_w, states, dmix)


CUM_BLOCK = 256


def _fox_gate_fwd(pel, b_f):
    t = pel.shape[0]
    nb = t // CUM_BLOCK

    def body(f_ref, b_ref, cum_ref, cum_t_ref):
        tri = (_iota((CUM_BLOCK, CUM_BLOCK), 1) <= _iota((CUM_BLOCK, CUM_BLOCK), 0)).astype(F32)
        carry = jnp.zeros((1, LANES), F32)
        for blk in range(nb):
            rows = slice(blk * CUM_BLOCK, (blk + 1) * CUM_BLOCK)
            cum = _dot_exact(tri, _log_sigmoid(f_ref[rows, :] + b_ref[...])) + carry
            cum_ref[rows, :] = cum
            cum_t_ref[blk] = cum.T[:ATT_HEADS, :]
            carry = cum[CUM_BLOCK - 1:CUM_BLOCK, :]

    return pl.pallas_call(
        body, grid=(1,),
        in_specs=[pl.BlockSpec((t, LANES), lambda i: (0, 5)), pl.BlockSpec((1, LANES), lambda i: (0, 0))],
        out_specs=[pl.BlockSpec((t, LANES), lambda i: (0, 0)),
                   pl.BlockSpec((nb, ATT_HEADS, CUM_BLOCK), lambda i: (0, 0, 0))],
        out_shape=[jax.ShapeDtypeStruct((t, LANES), F32), jax.ShapeDtypeStruct((nb, ATT_HEADS, CUM_BLOCK), F32)],
        compiler_params=_params("arbitrary"), name="fox_gate_fwd")(pel, b_f)


def _fox_gate_bwd(pel, b_f, dcum_t, dcum_q):
    t = pel.shape[0]
    nb = t // CUM_BLOCK

    def body(f_ref, b_ref, dct_ref, dcq_ref, df_ref, db_ref):
        tri_up = (_iota((CUM_BLOCK, CUM_BLOCK), 1) >= _iota((CUM_BLOCK, CUM_BLOCK), 0)).astype(F32)
        carry = jnp.zeros((1, LANES), F32)
        db = jnp.zeros((1, LANES), F32)
        for blk in reversed(range(nb)):
            rows = slice(blk * CUM_BLOCK, (blk + 1) * CUM_BLOCK)
            dls = _dot_exact(tri_up, dct_ref[blk].T + dcq_ref[rows, :]) + carry
            carry = dls[0:1, :]
            df = dls * jax.nn.sigmoid(-(f_ref[rows, :] + b_ref[...]))
            df_ref[rows, :] = df.astype(df_ref.dtype)
            db = db + jnp.sum(df, axis=0, keepdims=True)
        db_ref[...] = db

    return pl.pallas_call(
        body, grid=(1,),
        in_specs=[pl.BlockSpec((t, LANES), lambda i: (0, 5)), pl.BlockSpec((1, LANES), lambda i: (0, 0)),
                  pl.BlockSpec((nb, LANES, CUM_BLOCK), lambda i: (0, 0, 0)), pl.BlockSpec((t, LANES), lambda i: (0, 0))],
        out_specs=[pl.BlockSpec((t, LANES), lambda i: (0, 0)), pl.BlockSpec((1, LANES), lambda i: (0, 0))],
        out_shape=[jax.ShapeDtypeStruct((t, LANES), ACT_DTYPE), jax.ShapeDtypeStruct((1, LANES), F32)],
        compiler_params=_params("arbitrary"), name="fox_gate_bwd")(pel, b_f, dcum_t, dcum_q)


FOX_Q_BLOCK = 256


assert FOX_Q_BLOCK == CUM_BLOCK


def _fox_scores(q_ref, k_ref, cum_ref, cum_t_ref, h, i):
    hs = slice(h * HEAD_DIM, (h + 1) * HEAD_DIM)
    nb = cum_t_ref.shape[0]
    key_gate = jnp.concatenate([cum_t_ref[kb, h:h + 1, :] for kb in range(nb)], axis=1)
    s = _dot(q_ref[:, hs], k_ref[:, hs], 1, 1) * (HEAD_DIM ** -0.5) + (cum_ref[:, h:h + 1] - key_gate)
    shape = (FOX_Q_BLOCK, nb * FOX_Q_BLOCK)
    return jnp.where(_iota(shape, 1) <= i * FOX_Q_BLOCK + _iota(shape, 0), s, NEG)


def _fox_specs(t):
    bq, nb = FOX_Q_BLOCK, t // FOX_Q_BLOCK
    return [pl.BlockSpec((bq, GROUP_WIDTH), lambda i: (i, 2)), pl.BlockSpec((t, GROUP_WIDTH), lambda i: (0, 3)),
            pl.BlockSpec((t, GROUP_WIDTH), lambda i: (0, 4)), pl.BlockSpec((bq, LANES), lambda i: (i, 0)),
            pl.BlockSpec((nb, ATT_HEADS, bq), lambda i: (0, 0, 0))]


def _fox_fwd(pmm, cum, cum_t, ride=None):
    t = pmm.shape[0]
    bq = FOX_Q_BLOCK

    def body(q_ref, k_ref, v_ref, cum_ref, cum_t_ref, o_ref, lse_ref):
        i = pl.program_id(0)
        lse_ref[...] = jnp.zeros_like(lse_ref)
        for h in range(ATT_HEADS):
            hs = slice(h * HEAD_DIM, (h + 1) * HEAD_DIM)
            s = _fox_scores(q_ref, k_ref, cum_ref, cum_t_ref, h, i)
            m = jnp.max(s, axis=-1, keepdims=True)
            p = jnp.exp(s - m)
            l = jnp.sum(p, axis=-1, keepdims=True)
            o_ref[:, hs] = (_dot(p, v_ref[:, hs]) / l).astype(o_ref.dtype)
            lse_ref[:, h:h + 1] = m + jnp.log(l)

    return _pcall(
        body, ride, grid=(t // bq,), in_specs=_fox_specs(t),
        out_specs=[pl.BlockSpec((bq, GROUP_WIDTH), lambda i: (i, 0)), pl.BlockSpec((bq, LANES), lambda i: (i, 0))],
        out_shape=[jax.ShapeDtypeStruct((t, GROUP_WIDTH), ACT_DTYPE), jax.ShapeDtypeStruct((t, LANES), F32)],
        semantics=("parallel",), name="fox_fwd")(pmm, pmm, pmm, cum, cum_t)


def _fox_bwd(pmm, cum, cum_t, lse, dmix, ride=None):
    t = pmm.shape[0]
    bq, nb = FOX_Q_BLOCK, t // FOX_Q_BLOCK
    scale = HEAD_DIM ** -0.5

    def body(q_ref, k_ref, v_ref, cum_ref, cum_t_ref, lse_ref, do_ref, dq_ref, dk_ref, dv_ref, dct_ref, dcq_ref):
        i = pl.program_id(0)

        @pl.when(i == 0)
        def _():
            dk_ref[...] = jnp.zeros_like(dk_ref)
            dv_ref[...] = jnp.zeros_like(dv_ref)
            dct_ref[...] = jnp.zeros_like(dct_ref)

        dcq_ref[...] = jnp.zeros_like(dcq_ref)
        for h in range(ATT_HEADS):
            hs = slice(h * HEAD_DIM, (h + 1) * HEAD_DIM)
            s = _fox_scores(q_ref, k_ref, cum_ref, cum_t_ref, h, i)
            p = jnp.exp(s - lse_ref[:, h:h + 1])
            do = do_ref[:, hs]
            dp = _dot(do, v_ref[:, hs], 1, 1)
            ds = p * (dp - jnp.sum(p * dp, axis=-1, keepdims=True))
            dq_ref[:, hs] = (_dot(ds, k_ref[:, hs]) * scale).astype(dq_ref.dtype)
            dk_ref[:, hs] += _dot(ds, q_ref[:, hs], 0, 0) * scale
            dv_ref[:, hs] += _dot(p, do, 0, 0)
            key_side = -jnp.sum(ds, axis=0, keepdims=True)
            for kb in range(nb):
                dct_ref[kb, h:h + 1, :] += key_side[:, kb * bq:(kb + 1) * bq]
            dcq_ref[:, h:h + 1] = jnp.sum(ds, axis=1, keepdims=True)

    whole = pl.BlockSpec((t, GROUP_WIDTH), lambda i: (0, 0))
    return _pcall(
        body, ride, grid=(t // bq,),
        in_specs=_fox_specs(t) + [pl.BlockSpec((bq, LANES), lambda i: (i, 0)),
                                  pl.BlockSpec((bq, GROUP_WIDTH), lambda i: (i, 1))],
        out_specs=[pl.BlockSpec((bq, GROUP_WIDTH), lambda i: (i, 0)), whole, whole,
                   pl.BlockSpec((nb, LANES, bq), lambda i: (0, 0, 0)), pl.BlockSpec((bq, LANES), lambda i: (i, 0))],
        out_shape=[jax.ShapeDtypeStruct((t, GROUP_WIDTH), ACT_DTYPE), jax.ShapeDtypeStruct((t, GROUP_WIDTH), F32),
                   jax.ShapeDtypeStruct((t, GROUP_WIDTH), F32), jax.ShapeDtypeStruct((nb, LANES, bq), F32),
                   jax.ShapeDtypeStruct((t, LANES), F32)],
        semantics=("arbitrary",), name="fox_bwd")(pmm, pmm, pmm, cum, cum_t, lse, dmix)


CA_Q_BLOCK = 4 * CHUNK
CA_WINDOW = CA_Q_BLOCK + CA_LEFT
CA_BASE = 1024


def _ca_bias_base(rel_bias):
    n = rel_bias.shape[0]
    flat = CA_Q_BLOCK + CA_LEFT - REL_CLIP
    tail = CA_BASE - flat - (2 * REL_CLIP + 1)
    return jnp.concatenate([jnp.broadcast_to(rel_bias[:, 2 * REL_CLIP:], (n, flat)), rel_bias[:, ::-1],
                            jnp.broadcast_to(rel_bias[:, :1], (n, tail))], axis=1)


def _ca_bias_base_grad(dbase):
    flat = CA_Q_BLOCK + CA_LEFT - REL_CLIP
    mid = dbase[:, flat:flat + 2 * REL_CLIP + 1][:, ::-1]
    lo = jnp.sum(dbase[:, flat + 2 * REL_CLIP + 1:], axis=1, keepdims=True)
    hi = jnp.sum(dbase[:, :flat], axis=1, keepdims=True)
    pad = jnp.zeros((dbase.shape[0], 2 * REL_CLIP - 1), F32)
    return mid + jnp.concatenate([lo, pad, hi], axis=1)


def _ca_mask(i):
    r, j = _iota((CA_Q_BLOCK, CA_WINDOW), 0), _iota((CA_Q_BLOCK, CA_WINDOW), 1)
    rc, jc = r // CHUNK, j // CHUNK
    return (jc >= rc) & (jc <= rc + CA_LEFT // CHUNK) & (i * CA_Q_BLOCK + j >= CA_LEFT)


def _ca_fill_bias(i, base_ref, bias_scr):
    @pl.when(i == 0)
    def _():
        for h in range(ATT_HEADS):
            rows = jnp.broadcast_to(base_ref[h:h + 1, :], (CA_Q_BLOCK, CA_BASE))
            bias_scr[h] = pltpu.roll(rows, CA_BASE - CA_Q_BLOCK, 1, stride=1, stride_axis=0)[:, :CA_WINDOW]


def _ca_scores(q_ref, kp_ref, bias_scr, win, h, mask):
    hs = slice(h * HEAD_DIM, (h + 1) * HEAD_DIM)
    s = _dot(q_ref[:, hs], kp_ref[win, hs], 1, 1) * (HEAD_DIM ** -0.5)
    return jnp.where(mask, s + bias_scr[h], NEG)


CA_BIAS_SCRATCH = pltpu.VMEM((ATT_HEADS, CA_Q_BLOCK, CA_WINDOW), F32)


def _ca_fwd(pmm, kp, vp, base, ride=None):
    t = pmm.shape[0]

    def body(q_ref, kp_ref, vp_ref, base_ref, o_ref, lse_ref, bias_scr):
        i = pl.program_id(0)
        _ca_fill_bias(i, base_ref, bias_scr)
        win = pl.ds(pl.multiple_of(i * CA_Q_BLOCK, CA_Q_BLOCK), CA_WINDOW)
        mask = _ca_mask(i)
        lse_ref[...] = jnp.zeros_like(lse_ref)
        for h in range(ATT_HEADS):
            hs = slice(h * HEAD_DIM, (h + 1) * HEAD_DIM)
            s = _ca_scores(q_ref, kp_ref, bias_scr, win, h, mask)
            m = jnp.max(s, axis=-1, keepdims=True)
            p = jnp.exp(s - m)
            l = jnp.sum(p, axis=-1, keepdims=True)
            o_ref[:, hs] = (_dot(p, vp_ref[win, hs]) / l).astype(o_ref.dtype)
            lse_ref[:, h:h + 1] = m + jnp.log(l)

    padded = pl.BlockSpec((t + CA_LEFT, GROUP_WIDTH), lambda i: (0, 0))
    return _pcall(
        body, ride, grid=(t // CA_Q_BLOCK,),
        in_specs=[pl.BlockSpec((CA_Q_BLOCK, GROUP_WIDTH), lambda i: (i, 0)), padded, padded,
                  pl.BlockSpec((ATT_HEADS, CA_BASE), lambda i: (0, 0))],
        out_specs=[pl.BlockSpec((CA_Q_BLOCK, GROUP_WIDTH), lambda i: (i, 0)),
                   pl.BlockSpec((CA_Q_BLOCK, LANES), lambda i: (i, 0))],
        out_shape=[jax.ShapeDtypeStruct((t, GROUP_WIDTH), ACT_DTYPE), jax.ShapeDtypeStruct((t, LANES), F32)],
        scratch_shapes=[CA_BIAS_SCRATCH], semantics=("arbitrary",), name="ca_fwd")(pmm, kp, vp, base)


def _ca_bwd(pmm, kp, vp, base, lse, dmix, ride=None):
    t = pmm.shape[0]
    scale = HEAD_DIM ** -0.5

    def body(q_ref, kp_ref, vp_ref, base_ref, lse_ref, do_ref, dq_ref, dkp_ref, dvp_ref, dbase_ref, bias_scr):
        i = pl.program_id(0)
        _ca_fill_bias(i, base_ref, bias_scr)

        @pl.when(i == 0)
        def _():
            dkp_ref[...] = jnp.zeros_like(dkp_ref)
            dvp_ref[...] = jnp.zeros_like(dvp_ref)
            dbase_ref[...] = jnp.zeros_like(dbase_ref)

        win = pl.ds(pl.multiple_of(i * CA_Q_BLOCK, CA_Q_BLOCK), CA_WINDOW)
        mask = _ca_mask(i)
        flip = (_iota((CA_Q_BLOCK, CA_Q_BLOCK), 0) + _iota((CA_Q_BLOCK, CA_Q_BLOCK), 1) == CA_Q_BLOCK - 1).astype(F32)
        for h in range(ATT_HEADS):
            hs = slice(h * HEAD_DIM, (h + 1) * HEAD_DIM)
            s = _ca_scores(q_ref, kp_ref, bias_scr, win, h, mask)
            p = jnp.exp(s - lse_ref[:, h:h + 1])
            do = do_ref[:, hs]
            dp = _dot(do, vp_ref[win, hs], 1, 1)
            ds = p * (dp - jnp.sum(p * dp, axis=-1, keepdims=True))
            dq_ref[:, hs] = (_dot(ds, kp_ref[win, hs]) * scale).astype(dq_ref.dtype)
            dkp_ref[win, hs] += _dot(ds, q_ref[:, hs], 0, 0) * scale
            dvp_ref[win, hs] += _dot(p, do, 0, 0)
            rev = jnp.concatenate([_dot(flip, ds), jnp.zeros((CA_Q_BLOCK, CA_BASE - CA_WINDOW), F32)], axis=1)
            lined = pltpu.roll(rev, 1, 1, stride=1, stride_axis=0)
            dbase_ref[h:h + 1, :] += jnp.sum(lined, axis=0, keepdims=True)

    padded = pl.BlockSpec((t + CA_LEFT, GROUP_WIDTH), lambda i: (0, 0))
    return _pcall(
        body, ride, grid=(t // CA_Q_BLOCK,),
        in_specs=[pl.BlockSpec((CA_Q_BLOCK, GROUP_WIDTH), lambda i: (i, 0)), padded, padded,
                  pl.BlockSpec((ATT_HEADS, CA_BASE), lambda i: (0, 0)),
                  pl.BlockSpec((CA_Q_BLOCK, LANES), lambda i: (i, 0)),
                  pl.BlockSpec((CA_Q_BLOCK, GROUP_WIDTH), lambda i: (i, 0))],
        out_specs=[pl.BlockSpec((CA_Q_BLOCK, GROUP_WIDTH), lambda i: (i, 0)), padded, padded,
                   pl.BlockSpec((ATT_HEADS, CA_BASE), lambda i: (0, 0))],
        out_shape=[jax.ShapeDtypeStruct((t, GROUP_WIDTH), ACT_DTYPE),
                   jax.ShapeDtypeStruct((t + CA_LEFT, GROUP_WIDTH), F32),
                   jax.ShapeDtypeStruct((t + CA_LEFT, GROUP_WIDTH), F32),
                   jax.ShapeDtypeStruct((ATT_HEADS, CA_BASE), F32)],
        scratch_shapes=[CA_BIAS_SCRATCH], semantics=("arbitrary",), name="ca_bwd")(pmm, kp, vp, base, lse, dmix)


GELU_C = 0.7978845608028654
GELU_A = 0.044715


def _shift_down(v, k, fill):
    return jnp.where(_iota(v.shape, 0) >= k, pltpu.roll(v, k, 0), fill)


def _shift_up(v, k, fill):
    t = v.shape[0]
    return jnp.where(_iota(v.shape, 0) < t - k, pltpu.roll(v, t - k, 0), fill)


def _linear_scan(a, b, shift):
    k = 1
    while k < a.shape[0]:
        b = a * shift(b, k, 0.0) + b
        a = a * shift(a, k, 1.0)
        k *= 2
    return b


def _neg_expm1(y):
    series = -y * (1.0 + y * (0.5 + y * (1.0 / 6.0 + y * (1.0 / 24.0 + y * (1.0 / 120.0)))))
    return jnp.where(y > -0.1, series, 1.0 - jnp.exp(y))


def _lru_forward(x, g_in, cw, cb, wa, ba, wx, bx, lam):
    xs = [_shift_down(x, CONV_WIDTH - 1 - j, 0.0) for j in range(CONV_WIDTH - 1)] + [x]
    xc = cb + sum(cw[j:j + 1, :] * xs[j] for j in range(CONV_WIDTH))
    r = jax.nn.sigmoid(_dot(xc, wa) + ba)
    i = jax.nn.sigmoid(_dot(xc, wx) + bx)
    lsl = _log_sigmoid(lam)
    la = LRU_C * r * lsl
    a = jnp.exp(la)
    s = jnp.sqrt(_neg_expm1(2.0 * la))
    h = _linear_scan(a, s * (i * xc), _shift_down)
    u = GELU_C * (g_in + GELU_A * g_in * g_in * g_in)
    th = jnp.tanh(u)
    gelu = 0.5 * g_in * (1.0 + th)
    return xs, xc, r, i, lsl, a, s, h, th, gelu


def _lru_specs(t):
    col = lambda off: pl.BlockSpec((t, LANES), lambda j: (0, j + off))
    vec = pl.BlockSpec((1, LANES), lambda j: (0, j))
    mat = pl.BlockSpec((None, LANES, LANES), lambda j: (j, 0, 0))
    return [col(0), col(GROUP_WIDTH // LANES), pl.BlockSpec((CONV_WIDTH, LANES), lambda j: (0, j)),
            vec, mat, vec, mat, vec, vec]


def _lru_fwd(pel, conv_w, conv_b, wa, ba, wx, bx, lam, ride=None):
    t = pel.shape[0]

    def body(g_ref, x_ref, cw_ref, cb_ref, wa_ref, ba_ref, wx_ref, bx_ref, lam_ref, o_ref):
        res = _lru_forward(x_ref[...], g_ref[...], cw_ref[...], cb_ref[...], wa_ref[...], ba_ref[...],
                           wx_ref[...], bx_ref[...], lam_ref[...])
        o_ref[...] = (res[7] * res[9]).astype(o_ref.dtype)

    return _pcall(
        body, ride, grid=(GROUP_WIDTH // LANES,), in_specs=_lru_specs(t),
        out_specs=pl.BlockSpec((t, LANES), lambda j: (0, j)),
        out_shape=jax.ShapeDtypeStruct((t, GROUP_WIDTH), ACT_DTYPE),
        semantics=("parallel",), name="lru_fwd")(pel, pel, conv_w, conv_b, wa, ba, wx, bx, lam)


def _lru_bwd(pel, conv_w, conv_b, wa, ba, wx, bx, lam, dmix, ride=None):
    t = pel.shape[0]

    def body(g_ref, x_ref, cw_ref, cb_ref, wa_ref, ba_ref, wx_ref, bx_ref, lam_ref, do_ref,
             dg_ref, dx_ref, dcw_ref, dcb_ref, dwa_ref, dba_ref, dwx_ref, dbx_ref, dlam_ref):
        g_in, cw, lam = g_ref[...], cw_ref[...], lam_ref[...]
        xs, xc, r, i, lsl, a, s, h, th, gelu = _lru_forward(
            x_ref[...], g_in, cw, cb_ref[...], wa_ref[...], ba_ref[...], wx_ref[...], bx_ref[...], lam)
        dout = do_ref[...]
        dgelu = 0.5 * (1.0 + th) + 0.5 * g_in * (1.0 - th * th) * GELU_C * (1.0 + 3.0 * GELU_A * g_in * g_in)
        dg_ref[...] = (dout * h * dgelu).astype(dg_ref.dtype)
        gsum = _linear_scan(_shift_up(a, 1, 0.0), dout * gelu, _shift_up)
        da = gsum * _shift_down(h, 1, 0.0)
        di = gsum * s * xc
        dla = da * a - gsum * (i * xc) * (a * a / s)
        dlam_ref[...] = jnp.sum(dla * (LRU_C * r), axis=0, keepdims=True) * jax.nn.sigmoid(-lam)
        dpr = dla * (LRU_C * lsl) * r * (1.0 - r)
        dpi = di * i * (1.0 - i)
        dxc = gsum * s * i + _dot(dpr, wa_ref[...], 1, 1) + _dot(dpi, wx_ref[...], 1, 1)
        xct = xc.T
        dwa_ref[...] = _dot(xct, dpr)
        dwx_ref[...] = _dot(xct, dpi)
        dba_ref[...] = jnp.sum(dpr, axis=0, keepdims=True)
        dbx_ref[...] = jnp.sum(dpi, axis=0, keepdims=True)
        dcb_ref[...] = jnp.sum(dxc, axis=0, keepdims=True)
        for j in range(CONV_WIDTH):
            dcw_ref[j:j + 1, :] = jnp.sum(dxc * xs[j], axis=0, keepdims=True)
        dx = cw[CONV_WIDTH - 1:CONV_WIDTH, :] * dxc
        for j in range(CONV_WIDTH - 1):
            dx = dx + cw[j:j + 1, :] * _shift_up(dxc, CONV_WIDTH - 1 - j, 0.0)
        dx_ref[...] = dx.astype(dx_ref.dtype)

    col = pl.BlockSpec((t, LANES), lambda j: (0, j))
    vec = pl.BlockSpec((1, LANES), lambda j: (0, j))
    mat = pl.BlockSpec((None, LANES, LANES), lambda j: (j, 0, 0))
    nb = GROUP_WIDTH // LANES
    vshape = jax.ShapeDtypeStruct((1, GROUP_WIDTH), F32)
    mshape = jax.ShapeDtypeStruct((nb, LANES, LANES), F32)
    return _pcall(
        body, ride, grid=(nb,),
        in_specs=_lru_specs(t) + [pl.BlockSpec((t, LANES), lambda j: (0, j + nb))],
        out_specs=[col, col, pl.BlockSpec((CONV_WIDTH, LANES), lambda j: (0, j)), vec, mat, vec, mat, vec, vec],
        out_shape=[jax.ShapeDtypeStruct((t, GROUP_WIDTH), ACT_DTYPE), jax.ShapeDtypeStruct((t, GROUP_WIDTH), ACT_DTYPE),
                   jax.ShapeDtypeStruct((CONV_WIDTH, GROUP_WIDTH), F32), vshape, mshape, vshape, mshape, vshape, vshape],
        semantics=("parallel",), name="lru_bwd")(
            pel, pel, conv_w, conv_b, wa, ba, wx, bx, lam, dmix)


def _block_diag_pairs(w):
    z = jnp.zeros((LRU_BLOCK_DIM, LRU_BLOCK_DIM), w.dtype)
    return jnp.stack([jnp.block([[w[2 * j], z], [z, w[2 * j + 1]]]) for j in range(w.shape[0] // 2)])


def _block_diag_pairs_grad(dw):
    b = LRU_BLOCK_DIM
    return jnp.stack([dw[n // 2, (n % 2) * b:(n % 2 + 1) * b, (n % 2) * b:(n % 2 + 1) * b] for n in range(2 * dw.shape[0])])


def _row_tile(r):
    return ROW_TILE if r % ROW_TILE == 0 else r


def _pair_sum(g, got, place, name):
    _, r, c = g.shape
    tile = r

    def body(place_ref, a_ref, b_ref, o_ref):
        o_ref[...] = (a_ref[...].astype(F32) + b_ref[...].astype(F32)).astype(o_ref.dtype)

    blk = pl.BlockSpec((1, tile, c), lambda k, i, place_ref: (k, i, 0))
    return pl.pallas_call(
        body,
        grid_spec=pltpu.PrefetchScalarGridSpec(
            num_scalar_prefetch=1, grid=(N_CHIPS, r // tile),
            in_specs=[pl.BlockSpec((1, tile, c), lambda k, i, place_ref: (2 * k + place_ref[0], i, 0)), blk],
            out_specs=blk),
        out_shape=jax.ShapeDtypeStruct(got.shape, got.dtype),
        compiler_params=_params("parallel", "parallel"), name=name)(place, g, got)


def _adamw_update(g, w_ref, m_ref, v_ref, g_ref, d_ref, nm_ref, nv_ref):
    nm = ADAM_B1 * m_ref[...] + (1.0 - ADAM_B1) * g
    nv = ADAM_B2 * v_ref[...] + (1.0 - ADAM_B2) * jnp.square(g)
    m_hat = nm / (1.0 - ADAM_B1 ** ADAM_STEP)
    v_hat = nv / (1.0 - ADAM_B2 ** ADAM_STEP)
    g_ref[...] = g
    d_ref[...] = -ADAM_LR * (m_hat / (jnp.sqrt(v_hat) + ADAM_EPS) + ADAM_WD * w_ref[...])
    nm_ref[...] = nm
    nv_ref[...] = nv


def _adamw_sharded(parts, w, m, v, place, name):
    n_layers, r, c = w.shape
    tile = _row_tile(r)
    nb = r // tile

    def body(place_ref, *refs):
        layer = pl.program_id(0)
        g = None
        for l in range(n_layers):
            s_ref, r_ref = refs[2 * l], refs[2 * l + 1]
            g_l = s_ref[0].astype(F32) + r_ref[0].astype(F32) + r_ref[1].astype(F32) + r_ref[2].astype(F32)
            g = g_l if g is None else jnp.where(layer == l, g_l, g)
        _adamw_update(g, *refs[2 * n_layers:])

    def part_specs(l):
        rows = lambda q, i: jnp.where(q < l, 0, jnp.where(q > l, nb - 1, i))
        return [pl.BlockSpec((1, tile, c), lambda q, i, place_ref: (place_ref[1], rows(q, i), 0)),
                pl.BlockSpec((3, tile, c), lambda q, i, place_ref: (0, rows(q, i), 0))]

    in_specs, args = [], []
    for l, (s, recv) in enumerate(parts):
        in_specs += part_specs(l)
        args += [s, recv]
    blk = pl.BlockSpec((None, tile, c), lambda q, i, place_ref: (q, i, 0))
    out = jax.ShapeDtypeStruct((n_layers, r, c), F32)
    return pl.pallas_call(
        body,
        grid_spec=pltpu.PrefetchScalarGridSpec(
            num_scalar_prefetch=1, grid=(n_layers, nb), in_specs=in_specs + [blk, blk, blk],
            out_specs=[blk, blk, blk, blk]),
        out_shape=[out, out, out, out], compiler_params=_params("arbitrary", "arbitrary"), name=name)(
            place, *args, w, m, v)


def _adamw_replicated(parts, w, m, v, name):
    p, r, c = parts.shape
    tile = _row_tile(r)

    def body(p_ref, w_ref, m_ref, v_ref, *outs):
        g = p_ref[0].astype(F32)
        for k in range(1, p):
            g = g + p_ref[k].astype(F32)
        _adamw_update(g, w_ref, m_ref, v_ref, *outs)

    blk = pl.BlockSpec((tile, c), lambda i: (i, 0))
    out = jax.ShapeDtypeStruct((r, c), F32)
    return pl.pallas_call(body, grid=(r // tile,),
                          in_specs=[pl.BlockSpec((p, tile, c), lambda i: (0, i, 0)), blk, blk, blk],
                          out_specs=[blk, blk, blk, blk], out_shape=[out, out, out, out],
                          compiler_params=_params("parallel"), name=name)(parts, w, m, v)


SLAB_COLS = 1024
SHARDED = {"norm_w": 2, "w_in_even": 2, "gla_w_a_up": 2, "w_out_even": 1, "w_in_odd": 2, "conv_w": 2, "conv_b": 1,
           "lru_b_a": 1, "lru_b_x": 1, "lru_lambda": 1, "w_out_odd": 1, "w_mlp_up": 2, "w_mlp_down": 1}
REPLICATED = ["gla_b_a", "gla_norm_w", "fox_b_f", "rel_bias", "lru_w_a", "lru_w_x"]
WEIGHTS = ["norm_w", "w_in_even", "gla_w_a_up", "gla_b_a", "gla_norm_w", "fox_b_f", "w_out_even", "w_in_odd",
           "rel_bias", "conv_w", "conv_b", "lru_w_a", "lru_b_a", "lru_w_x", "lru_b_x", "lru_lambda", "w_out_odd",
           "w_mlp_up", "w_mlp_down"]
MATRICES = ("w_in_even", "w_out_even", "w_in_odd", "w_out_odd", "w_mlp_up", "w_mlp_down")
VECTORS = tuple(n for n in SHARDED if n not in MATRICES)
VEC_SLAB_ROWS = 16
REPL_SLAB_ROWS = 72
MATRIX_BLOCKS = (("w_in_even", 0), ("w_out_even", 0), ("w_in_odd", 0), ("w_out_odd", 0),
                 ("w_mlp_up", 0), ("w_mlp_up", 1), ("w_mlp_down", 0), ("w_mlp_down", 1))


def _rows_of(shape):
    n = 1
    for s in shape:
        n *= s
    return -(-n // SLAB_COLS), n


def _pack(arrays, total_rows, lead=()):
    parts, used = [], 0
    for a in arrays:
        rows, n = _rows_of(a.shape[len(lead):])
        flat = a.reshape(lead + (n,))
        flat = jnp.pad(flat, [(0, 0)] * len(lead) + [(0, rows * SLAB_COLS - n)])
        parts.append(flat.reshape(lead + (rows, SLAB_COLS)))
        used += rows
    parts.append(jnp.zeros(lead + (total_rows - used, SLAB_COLS), parts[0].dtype))
    return jnp.concatenate(parts, axis=len(lead))


def _unpack(slab, shapes, lead=()):
    out, row = [], 0
    for shape in shapes:
        rows, n = _rows_of(shape)
        seg = lax.slice_in_dim(slab, row, row + rows, axis=len(lead))
        out.append(seg.reshape(lead + (rows * SLAB_COLS,))[..., :n].reshape(lead + tuple(shape)))
        row += rows
    return out


def _join_shards(blocks, axis):
    moved = jnp.moveaxis(blocks, 0, axis)
    shape = moved.shape
    return moved.reshape(shape[:axis] + (shape[axis] * shape[axis + 1],) + shape[axis + 2:])


def _split_shards(full, axis):
    shape = full.shape
    cut = full.reshape(shape[:axis] + (N_DEV, shape[axis] // N_DEV) + shape[axis + 1:])
    return jnp.moveaxis(cut, axis, 0)


EVEN_SPLITS = (0, 256, 512, 1024, 1536, 1552, 2064, 2576, 3088, 3096)


def _even_in_split(w):
    c = [w[:, EVEN_SPLITS[k]:EVEN_SPLITS[k + 1]] for k in range(9)]
    gq, gk, gv, gr, ga, fq, fk, fv, ff = c
    padcols = lambda a: jnp.pad(a, ((0, 0), (0, LANES - a.shape[1])))
    return jnp.concatenate([gq, gk, gv, fq, fk, fv], axis=1), jnp.concatenate([gr, padcols(ga), padcols(ff)], axis=1)


def _even_in_merge(dmm, dele):
    return jnp.concatenate([dmm[:, :1024], dele[:, :512], dele[:, 512:512 + GLA_RANK], dmm[:, 1024:2560],
                            dele[:, 640:640 + ATT_HEADS]], axis=1)


def _column_shards(full):
    r, c = full.shape
    return jnp.moveaxis(full.reshape(r, N_DEV, c // N_DEV), 1, 0)


def _forward_backward(x, target, shard, vec_slab, vec_shapes, w, place):
    w = dict(w)
    g, dnorm, sums, recv = {}, {}, {}, {}
    nrm = lambda l, k: w["norm_w"][l, k][None, :]
    gather = lambda *keys: _gather_plan([shard[k] for k in keys])
    blocks = lambda r, c: (N_DEV, r // N_DEV, c)

    def pair_sum(key):
        sums[key] = _pair_sum(g[key], got[key], place, f"rs_pair_sum_{key[0]}_{key[1]}")

    got = {}

    def mlp_fwd(xin, layer, ride_up, ride_down):
        h = _norm_fwd(xin, nrm(layer, 2), out_dtype=ACT_DTYPE, name=f"norm_mlp_{layer}")
        u = _mm(h, w["w_mlp_up"][layer], out_dtype=ACT_DTYPE, tm=TM_FWD, tn=D_FF // N_DEV, b_blocked=True,
                name=f"mlp_up_{layer}", ride=ride_up)
        u, rode_up = u if ride_up is not None else (u, None)
        if w["w_mlp_down"][layer] is None:
            w["w_mlp_down"][layer] = rode_up[0].reshape(D_FF, D_MODEL)
        yv = _mm(u, w["w_mlp_down"][layer], out_dtype=F32, tm=TM_DX, tn=TN, a_sqrelu=True,
                 name=f"mlp_down_{layer}", ride=ride_down)
        yv, rode_down = yv if ride_down is not None else (yv, None)
        xout = _norm_fwd(yv, nrm(layer, 3), out_dtype=F32, res=xin, name=f"norm_mlp_out_{layer}")
        return xout, (xin, h, u, yv), rode_up, rode_down

    def mlp_bwd(dxout, saved, layer, ride):
        xin, h, u, yv = saved
        k_up, k_down = ("w_mlp_up", layer), ("w_mlp_down", layer)
        dy, dnorm[(layer, 3)] = _norm_bwd(dxout, yv, nrm(layer, 3), out_dtype=ACT_DTYPE, name=f"norm_mlp_out_bwd_{layer}")
        du = _mm(dy, w["w_mlp_down"][layer], nt=True, out_dtype=ACT_DTYPE, tm=TM_DX, tn=TN, drelu_of=u,
                 name=f"mlp_down_dx_{layer}", ride=ride)
        rode = None
        if ride is not None:
            du, rode = du
        g[k_down] = _mm(u, dy, ta=True, out_dtype=WIRE_DTYPE, tm=TM_DW, tn=TN, a_sqrelu=True,
                        name=f"mlp_down_dw_{layer}").reshape(blocks(D_FF, D_MODEL))
        g[k_up] = _mm(h, du, ta=True, out_dtype=WIRE_DTYPE, tm=TM_DW, tn=D_FF // N_DEV, out_blocked=True,
                      name=f"mlp_up_dw_{layer}")
        w_up = jnp.moveaxis(w["w_mlp_up"][layer], 0, 1).reshape(D_MODEL, D_FF)
        dh, (got[k_down], got[k_up]) = _mm(du, w_up, nt=True, out_dtype=F32, tm=TM_DX, tn=TN, name=f"mlp_up_dx_{layer}",
                                           ride=_sibling_plan([g[k_down], g[k_up]]))
        pair_sum(k_down)
        pair_sum(k_up)
        dxin, dnorm[(layer, 2)] = _norm_bwd(dh, xin, nrm(layer, 2), out_dtype=F32, add=dxout, name=f"norm_mlp_bwd_{layer}")
        return dxin, rode

    first = _run_plan(_gather_plan([shard[("w_in_even", 0)], vec_slab]), "weights_all_gather_first")
    w["w_in_even"] = _join_shards(first[0], 1)
    for n, b in zip(VECTORS, _unpack(first[1], vec_shapes, lead=(N_DEV,))):
        w[n] = _join_shards(b, SHARDED[n])
    w["w_mlp_up"], w["w_mlp_down"] = [None] * DEPTH, [None] * DEPTH

    wmm_e, wel_e = _even_in_split(w["w_in_even"])
    w_up_pad = jnp.pad(w["gla_w_a_up"][0], ((0, LANES - GLA_RANK), (0, 0)))
    b_f_pad = jnp.pad(w["fox_b_f"], ((0, 0), (0, LANES - ATT_HEADS)))
    h0 = _norm_fwd(x, nrm(0, 0), out_dtype=ACT_DTYPE, name="norm_in_0")
    pmm0, (w_out_even,) = _mm(h0, wmm_e, out_dtype=ACT_DTYPE, tm=TM_FWD, tn=TN, name="in_even_mm",
                              ride=gather(("w_out_even", 0)))
    pel0 = _mm(h0, wel_e, out_dtype=F32, tm=TM_FWD, tn=768, name="in_even_el")
    out_a, states = _gla_fwd(pmm0, pel0, w_up_pad, w["gla_b_a"], w["gla_norm_w"])
    cum, cum_t = _fox_gate_fwd(pel0, b_f_pad)
    (out_b, lse_b), (w["w_mlp_up"][0], w_mlp_down0) = _fox_fwd(pmm0, cum, cum_t,
                                                               ride=gather(("w_mlp_up", 0), ("w_mlp_down", 0)))
    w["w_out_even"] = w_out_even.reshape(D_MODEL, D_MODEL)
    w["w_mlp_down"][0] = w_mlp_down0.reshape(D_FF, D_MODEL)
    mix_in0 = jnp.concatenate([out_a, out_b], axis=1)
    mix0 = _mm(mix_in0, w["w_out_even"], out_dtype=F32, tm=TM_FWD, tn=TN, name="out_even")
    x1 = _norm_fwd(mix0, nrm(0, 1), out_dtype=F32, res=x, name="norm_mix_0")
    x2, mlp0, _, (w_in_odd,) = mlp_fwd(x1, 0, None, gather(("w_in_odd", 0)))
    w["w_in_odd"] = _join_shards(w_in_odd, 1)

    w_in_o = w["w_in_odd"]
    n_mm_o = 3 * GROUP_WIDTH
    wa_bd, wx_bd = _block_diag_pairs(w["lru_w_a"][0]), _block_diag_pairs(w["lru_w_x"][0])
    base = _ca_bias_base(w["rel_bias"][0])
    h1 = _norm_fwd(x2, nrm(1, 0), out_dtype=ACT_DTYPE, name="norm_in_1")
    pmm1 = _mm(h1, w_in_o[:, :n_mm_o], out_dtype=ACT_DTYPE, tm=TM_FWD, tn=TN, name="in_odd_mm")
    pel1 = _mm(h1, w_in_o[:, n_mm_o:], out_dtype=F32, tm=TM_FWD, tn=TN, name="in_odd_el")
    kp = jnp.pad(pmm1[:, GROUP_WIDTH:2 * GROUP_WIDTH], ((CA_LEFT, 0), (0, 0)))
    vp = jnp.pad(pmm1[:, 2 * GROUP_WIDTH:], ((CA_LEFT, 0), (0, 0)))
    (out_c, lse_c), (w["w_mlp_up"][1],) = _ca_fwd(pmm1, kp, vp, base, ride=gather(("w_mlp_up", 1)))
    lru_args = (pel1, w["conv_w"][0], w["conv_b"], wa_bd, w["lru_b_a"], wx_bd, w["lru_b_x"], w["lru_lambda"])
    out_d, (w_out_odd,) = _lru_fwd(*lru_args, ride=gather(("w_out_odd", 0)))
    w["w_out_odd"] = w_out_odd.reshape(D_MODEL, D_MODEL)
    mix_in1 = jnp.concatenate([out_c, out_d], axis=1)
    mix1 = _mm(mix_in1, w["w_out_odd"], out_dtype=F32, tm=TM_FWD, tn=TN, name="out_odd")
    x3 = _norm_fwd(mix1, nrm(1, 1), out_dtype=F32, res=x2, name="norm_mix_1")
    x4, mlp1, _, _ = mlp_fwd(x3, 1, gather(("w_mlp_down", 1)), None)

    loss, dx4 = _loss_fwd_bwd(x4, target)

    k_oo, k_io, k_oe, k_ie = ("w_out_odd", 0), ("w_in_odd", 0), ("w_out_even", 0), ("w_in_even", 0)
    mlp_keys = lambda l: [("w_mlp_down", l), ("w_mlp_up", l)]
    dx3, _ = mlp_bwd(dx4, mlp1, 1, None)
    dmix1, dnorm[(1, 1)] = _norm_bwd(dx3, mix1, nrm(1, 1), out_dtype=ACT_DTYPE, name="norm_mix_bwd_1")
    g[k_oo] = _mm(mix_in1, dmix1, ta=True, out_dtype=WIRE_DTYPE, tm=TM_DW, tn=TN, name="out_odd_dw").reshape(
        blocks(D_MODEL, D_MODEL))
    dmix_in1, (got[k_oo],) = _mm(dmix1, w["w_out_odd"], nt=True, out_dtype=F32, tm=TM_DX, tn=TN, name="out_odd_dx",
                                 ride=_sibling_plan([g[k_oo]]))
    (dq_c, dkp, dvp, dbase), rode = _ca_bwd(pmm1, kp, vp, base, lse_c, dmix_in1,
                                            ride=_chip_plan([sums[k] for k in mlp_keys(1)]))
    recv.update(zip(mlp_keys(1), rode))
    pair_sum(k_oo)
    (dgate, dxin, g_conv_w, g_conv_b, dwa_bd, g_lru_b_a, dwx_bd, g_lru_b_x, g_lru_lambda), (recv[k_oo],) = _lru_bwd(
        *lru_args, dmix_in1, ride=_chip_plan([sums[k_oo]]))
    dp1 = jnp.concatenate([dq_c, dkp[CA_LEFT:].astype(ACT_DTYPE), dvp[CA_LEFT:].astype(ACT_DTYPE), dgate, dxin], axis=1)
    g[k_io] = _column_shards(_mm(h1, dp1, ta=True, out_dtype=WIRE_DTYPE, tm=TM_DW, tn=TN, name="in_odd_dw"))
    dh1, (got[k_io],) = _mm(dp1, w_in_o, nt=True, out_dtype=F32, tm=TM_DX, tn=TN, name="in_odd_dx",
                            ride=_sibling_plan([g[k_io]]))
    pair_sum(k_io)
    dx2, dnorm[(1, 0)] = _norm_bwd(dh1, x2, nrm(1, 0), out_dtype=F32, add=dx3, name="norm_in_bwd_1")
    g["rel_bias"] = _ca_bias_base_grad(dbase)[None]
    g["conv_w"], g["conv_b"] = g_conv_w[None], g_conv_b
    g["lru_w_a"], g["lru_w_x"] = _block_diag_pairs_grad(dwa_bd)[None], _block_diag_pairs_grad(dwx_bd)[None]
    g["lru_b_a"], g["lru_b_x"], g["lru_lambda"] = g_lru_b_a, g_lru_b_x, g_lru_lambda

    dx1, (recv[k_io],) = mlp_bwd(dx2, mlp0, 0, _chip_plan([sums[k_io]]))
    dmix0, dnorm[(0, 1)] = _norm_bwd(dx1, mix0, nrm(0, 1), out_dtype=ACT_DTYPE, name="norm_mix_bwd_0")
    g[k_oe] = _mm(mix_in0, dmix0, ta=True, out_dtype=WIRE_DTYPE, tm=TM_DW, tn=TN, name="out_even_dw").reshape(
        blocks(D_MODEL, D_MODEL))
    dmix_in0, (got[k_oe],) = _mm(dmix0, w["w_out_even"], nt=True, out_dtype=F32, tm=TM_DX, tn=TN, name="out_even_dx",
                                 ride=_sibling_plan([g[k_oe]]))
    k_md0, k_mu0 = mlp_keys(0)
    pair_sum(k_oe)
    (dq_a, dk_a, dv_a, dr_a, da_a, dw_up_pad, g_gla_b_a, g_gla_norm_w), (recv[k_md0],) = _gla_bwd(
        pmm0, pel0, w_up_pad, w["gla_b_a"], w["gla_norm_w"], states, dmix_in0, ride=_chip_plan([sums[k_md0]]))
    (dq_b, dk_b, dv_b, dcum_t, dcum_q), (recv[k_mu0], recv[k_oe]) = _fox_bwd(
        pmm0, cum, cum_t, lse_b, dmix_in0, ride=_chip_plan([sums[k_mu0], sums[k_oe]]))
    df_b, db_f = _fox_gate_bwd(pel0, b_f_pad, dcum_t, dcum_q)
    g["gla_w_a_up"] = dw_up_pad[:GLA_RANK][None]
    g["gla_b_a"], g["gla_norm_w"], g["fox_b_f"] = g_gla_b_a, g_gla_norm_w, db_f[:, :ATT_HEADS]
    dp0 = jnp.concatenate([dq_a, dk_a, dv_a, dq_b, dk_b.astype(ACT_DTYPE), dv_b.astype(ACT_DTYPE), dr_a, da_a, df_b],
                          axis=1)
    w_perm = jnp.concatenate([wmm_e, wel_e], axis=1)
    n_mm_e = wmm_e.shape[1]
    dw_perm, (repl_parts,) = _mm(h0, dp0, ta=True, out_dtype=WIRE_DTYPE, tm=TM_DW, tn=dp0.shape[1] // 2, name="in_even_dw",
                                 ride=_gather_plan([_pack([g[n] for n in REPLICATED], REPL_SLAB_ROWS)]))
    g[k_ie] = _column_shards(_even_in_merge(dw_perm[:, :n_mm_e], dw_perm[:, n_mm_e:]))
    dh0, (got[k_ie],) = _mm(dp0, w_perm, nt=True, out_dtype=F32, tm=TM_DX, tn=TN, name="in_even_dx",
                            ride=_sibling_plan([g[k_ie]]))
    pair_sum(k_ie)
    (dx0, dnorm[(0, 0)]), (recv[k_ie],) = _norm_bwd(dh0, x, nrm(0, 0), out_dtype=F32, add=dx1, name="norm_in_bwd_0",
                                                     ride=_chip_plan([sums[k_ie]]))

    g["norm_w"] = jnp.stack([jnp.concatenate([dnorm[(l, k)] for k in range(4)], axis=0) for l in range(DEPTH)])
    k_vec = ("vectors", 0)
    vec_grads = _pack([_split_shards(g[n], SHARDED[n]) for n in VECTORS], VEC_SLAB_ROWS, lead=(N_DEV,))
    g[k_vec] = vec_grads.astype(WIRE_DTYPE)
    (got[k_vec],) = _run_plan(_sibling_plan([g[k_vec]]), "rs_sibling_exchange_last")
    pair_sum(k_vec)
    (recv[k_vec],) = _run_plan(_chip_plan([sums[k_vec]]), "rs_chip_exchange_last")
    return loss, dx0, sums, recv, repl_parts


def kernel(x, norm_w, w_in_even, gla_w_a_up, gla_b_a, gla_norm_w, fox_b_f, w_out_even, w_in_odd, rel_bias, conv_w, conv_b, lru_w_a, lru_b_a, lru_w_x, lru_b_x, lru_lambda, w_out_odd, w_mlp_up, w_mlp_down, loss_target, m_norm_w, m_w_in_even, m_gla_w_a_up, m_gla_b_a, m_gla_norm_w, m_fox_b_f, m_w_out_even, m_w_in_odd, m_rel_bias, m_conv_w, m_conv_b, m_lru_w_a, m_lru_b_a, m_lru_w_x, m_lru_b_x, m_lru_lambda, m_w_out_odd, m_w_mlp_up, m_w_mlp_down, v_norm_w, v_w_in_even, v_gla_w_a_up, v_gla_b_a, v_gla_norm_w, v_fox_b_f, v_w_out_even, v_w_in_odd, v_rel_bias, v_conv_w, v_conv_b, v_lru_w_a, v_lru_b_a, v_lru_w_x, v_lru_b_x, v_lru_lambda, v_w_out_odd, v_w_mlp_up, v_w_mlp_down):
    wts = dict(zip(WEIGHTS, (norm_w, w_in_even, gla_w_a_up, gla_b_a, gla_norm_w, fox_b_f, w_out_even, w_in_odd, rel_bias,
                             conv_w, conv_b, lru_w_a, lru_b_a, lru_w_x, lru_b_x, lru_lambda, w_out_odd, w_mlp_up,
                             w_mlp_down)))
    mom = dict(zip(WEIGHTS, (m_norm_w, m_w_in_even, m_gla_w_a_up, m_gla_b_a, m_gla_norm_w, m_fox_b_f, m_w_out_even,
                             m_w_in_odd, m_rel_bias, m_conv_w, m_conv_b, m_lru_w_a, m_lru_b_a, m_lru_w_x, m_lru_b_x,
                             m_lru_lambda, m_w_out_odd, m_w_mlp_up, m_w_mlp_down)))
    var = dict(zip(WEIGHTS, (v_norm_w, v_w_in_even, v_gla_w_a_up, v_gla_b_a, v_gla_norm_w, v_fox_b_f, v_w_out_even,
                             v_w_in_odd, v_rel_bias, v_conv_w, v_conv_b, v_lru_w_a, v_lru_b_a, v_lru_w_x, v_lru_b_x,
                             v_lru_lambda, v_w_out_odd, v_w_mlp_up, v_w_mlp_down)))
    vec_shapes = [wts[n].shape for n in VECTORS]
    repl_shapes = [wts[n].shape for n in REPLICATED]
    place = jnp.stack([lax.axis_index("c"), 2 * lax.axis_index("x") + lax.axis_index("y")]).astype(jnp.int32)

    shard = {(n, l): wts[n][l].astype(WIRE_DTYPE) for n, l in MATRIX_BLOCKS}
    vec_slab = _pack([wts[n] for n in VECTORS], VEC_SLAB_ROWS)
    loss_blk, dx, sums, recv, repl_parts = _forward_backward(
        x[0], loss_target[0], shard, vec_slab, vec_shapes, {n: wts[n] for n in REPLICATED}, place)
    loss = lax.psum(loss_blk[0, 0], ("x", "y", "c"))

    k_vec = ("vectors", 0)

    vec_of = lambda d: _pack([d[n] for n in VECTORS], VEC_SLAB_ROWS)[None]
    upd = {n: _adamw_sharded([(sums[(n, l)], recv[(n, l)]) for l in range(wts[n].shape[0])], wts[n], mom[n], var[n],
                             place, f"adamw_{n}") for n in MATRICES}
    vec_upd = _adamw_sharded([(sums[k_vec], recv[k_vec])], vec_of(wts), vec_of(mom), vec_of(var), place, "adamw_vectors")
    rp = _adamw_replicated(repl_parts, _pack([wts[n] for n in REPLICATED], REPL_SLAB_ROWS),
                           _pack([mom[n] for n in REPLICATED], REPL_SLAB_ROWS),
                           _pack([var[n] for n in REPLICATED], REPL_SLAB_ROWS), "adamw_replicated")
    outs = []
    for kind in range(4):
        vals = dict(zip(VECTORS, _unpack(vec_upd[kind][0], vec_shapes)))
        vals.update(zip(REPLICATED, _unpack(rp[kind], repl_shapes)))
        vals.update((n, upd[n][kind]) for n in MATRICES)
        outs += [vals[n] for n in WEIGHTS]
    return (loss, dx[None], *outs)
```

```python
import functools
from typing import Callable, NamedTuple

import jax
import jax.numpy as jnp
from jax import lax
from jax.experimental import pallas as pl
from jax.experimental.pallas import tpu as pltpu

F32 = jnp.float32
MXU_DTYPE = jnp.bfloat16
ACT_DTYPE = jnp.bfloat16
WIRE_DTYPE = jnp.bfloat16

V7X_VMEM_BYTES = 64 * 1024 * 1024
VMEM_LIMIT = (V7X_VMEM_BYTES * 7) // 8
LANES = 128

D_MODEL = 1024
SEQ = 2048
DEPTH = 2
CHUNK = 64
GROUP_WIDTH = D_MODEL // 2
D_FF = 4 * D_MODEL
NORM_EPS = 1e-6
GLA_HEADS = 4
GLA_DV = GROUP_WIDTH // GLA_HEADS
GLA_DK = GLA_DV // 2
GLA_KW = GLA_HEADS * GLA_DK
GLA_RANK = 16
GLA_GATE_TAU = 16.0
HEAD_DIM = 64
ATT_HEADS = GROUP_WIDTH // HEAD_DIM
CA_LEFT = 8 * CHUNK
REL_CLIP = 128
LRU_BLOCK_DIM = 64
CONV_WIDTH = 4
LRU_C = 8.0
N_DEV = 8

ADAM_LR = 0.001
ADAM_B1 = 0.9
ADAM_B2 = 0.999
ADAM_EPS = 1e-08
ADAM_WD = 0.01
ADAM_STEP = 10

NEG = float(jnp.finfo(jnp.float32).min)
MESH = pl.DeviceIdType.MESH


def _params(*sem):
    return pltpu.CompilerParams(dimension_semantics=sem, vmem_limit_bytes=VMEM_LIMIT)


def _dot(a, b, ca=1, cb=0):
    return lax.dot_general(a.astype(MXU_DTYPE), b.astype(MXU_DTYPE), (((ca,), (cb,)), ((), ())),
                           preferred_element_type=F32)


def _dot_exact(a, b):
    return lax.dot_general(a, b, (((1,), (0,)), ((), ())), precision=lax.Precision.HIGHEST,
                           preferred_element_type=F32)


def _log_sigmoid(x):
    return jnp.minimum(x, 0.0) - jnp.log1p(jnp.exp(-jnp.abs(x)))


def _iota(shape, axis):
    return lax.broadcasted_iota(jnp.int32, shape, axis)


ANY = pl.BlockSpec(memory_space=pl.ANY)
N_CHIPS = 4


class _Plan(NamedTuple):
    ins: list
    outs: list
    sems: list
    start: Callable
    finish: Callable


def _place():
    x, y, c = lax.axis_index("x"), lax.axis_index("y"), lax.axis_index("c")
    return x, y, c, [(1 - x, y), (x, 1 - y), (1 - x, 1 - y)]


def _gather_plan(xs):
    n = len(xs)

    def parts(x_refs, out_refs, sems):
        send_sems, recv_sems, local_sems = sems
        x, y, c, chips = _place()
        me, sibling = (x, y, c), (x, y, 1 - c)

        def rows(a, px, py, pc):
            return out_refs[a].at[4 * px + 2 * py + pc]

        def copy(a, k, block, to, src=None):
            return pltpu.make_async_remote_copy(
                src_ref=rows(a, *block) if src is None else src, dst_ref=rows(a, *block),
                send_sem=send_sems.at[7 * a + k], recv_sem=recv_sems.at[7 * a + k], device_id=to, device_id_type=MESH)

        mine = [pltpu.make_async_copy(x_refs[a], rows(a, *me), local_sems.at[a]) for a in range(n)]
        first = []
        for a in range(n):
            first.append(copy(a, 0, me, sibling, src=x_refs[a]))
            first += [copy(a, 1 + j, me, (*chip, c), src=x_refs[a]) for j, chip in enumerate(chips)]
        return c, me, sibling, chips, copy, mine, first

    def start(x_refs, out_refs, sems):
        *_, mine, first = parts(x_refs, out_refs, sems)
        for cp in first + mine:
            cp.start()

    def finish(x_refs, out_refs, sems):
        c, me, sibling, chips, copy, mine, first = parts(x_refs, out_refs, sems)
        passed = []
        for j, chip in enumerate(chips):
            for a in range(n):
                copy(a, 1 + j, (*chip, c), me).wait_recv()
                passed.append(copy(a, 4 + j, (*chip, c), sibling))
                passed[-1].start()
        for a in range(n):
            copy(a, 0, sibling, me).wait_recv()
            for j, chip in enumerate(chips):
                copy(a, 4 + j, (*chip, 1 - c), me).wait_recv()
        for cp in first + passed:
            cp.wait_send()
        for cp in mine:
            cp.wait()

    return _Plan(list(xs), [jax.ShapeDtypeStruct((N_DEV,) + x.shape, x.dtype) for x in xs],
                 [pltpu.SemaphoreType.DMA((7 * n,)), pltpu.SemaphoreType.DMA((7 * n,)), pltpu.SemaphoreType.DMA((n,))],
                 start, finish)


def _exchange_plan(copies_of, ins, outs, per_array):
    n = len(ins)

    def start(in_refs, out_refs, sems):
        for cp in copies_of(in_refs, out_refs, sems):
            cp.start()

    def finish(in_refs, out_refs, sems):
        copies = copies_of(in_refs, out_refs, sems)
        for cp in copies:
            cp.wait_recv()
        for cp in copies:
            cp.wait_send()

    return _Plan(list(ins), outs, [pltpu.SemaphoreType.DMA((per_array * n,)), pltpu.SemaphoreType.DMA((per_array * n,))],
                 start, finish)


def _sibling_plan(gs):
    def copies_of(g_refs, got_refs, sems):
        x, y, c, _ = _place()
        return [pltpu.make_async_remote_copy(
            src_ref=g_refs[a].at[2 * k + (1 - c)], dst_ref=got_refs[a].at[k], send_sem=sems[0].at[N_CHIPS * a + k],
            recv_sem=sems[1].at[N_CHIPS * a + k], device_id=(x, y, 1 - c), device_id_type=MESH)
            for a in range(len(gs)) for k in range(N_CHIPS)]

    return _exchange_plan(copies_of, gs, [jax.ShapeDtypeStruct((N_CHIPS,) + g.shape[1:], g.dtype) for g in gs], N_CHIPS)


def _chip_plan(ss):
    def copies_of(s_refs, out_refs, sems):
        x, y, c, chips = _place()
        return [pltpu.make_async_remote_copy(
            src_ref=s_refs[a].at[2 * px + py], dst_ref=out_refs[a].at[j], send_sem=sems[0].at[3 * a + j],
            recv_sem=sems[1].at[3 * a + j], device_id=(px, py, c), device_id_type=MESH)
            for a in range(len(ss)) for j, (px, py) in enumerate(chips)]

    return _exchange_plan(copies_of, ss, [jax.ShapeDtypeStruct((3,) + s.shape[1:], s.dtype) for s in ss], 3)


def _run_plan(plan, name):
    n_in, n_out = len(plan.ins), len(plan.outs)

    def body(*refs):
        args = refs[:n_in], refs[n_in:n_in + n_out], refs[n_in + n_out:]
        plan.start(*args)
        plan.finish(*args)

    return pl.pallas_call(body, out_shape=plan.outs, in_specs=[ANY] * n_in, out_specs=[ANY] * n_out,
                          scratch_shapes=plan.sems, name=name)(*plan.ins)


def _pcall(body, ride, *, grid, in_specs, out_specs, out_shape, scratch_shapes=(), semantics, name):
    if ride is None:
        return pl.pallas_call(body, grid=grid, in_specs=in_specs, out_specs=out_specs, out_shape=out_shape,
                              scratch_shapes=list(scratch_shapes), compiler_params=_params(*semantics), name=name)
    single = not isinstance(out_shape, (list, tuple))
    out_specs_l, out_shape_l = ([out_specs], [out_shape]) if single else (list(out_specs), list(out_shape))
    n_in, n_out, n_scr = len(in_specs), len(out_shape_l), len(scratch_shapes)
    r_in, r_out = len(ride.ins), len(ride.outs)

    def riding(*refs):
        cuts = [n_in, r_in, n_out, r_out, n_scr]
        groups, at = [], 0
        for width in cuts:
            groups.append(refs[at:at + width])
            at += width
        ins, r_ins, outs, r_outs, scr = groups
        sems = refs[at:]
        first = functools.reduce(jnp.logical_and, [pl.program_id(d) == 0 for d in range(len(grid))])
        last = functools.reduce(jnp.logical_and, [pl.program_id(d) == grid[d] - 1 for d in range(len(grid))])

        @pl.when(first)
        def _():
            ride.start(r_ins, r_outs, sems)

        body(*ins, *outs, *scr)

        @pl.when(last)
        def _():
            ride.finish(r_ins, r_outs, sems)

    call = pl.pallas_call(
        riding, grid=grid, in_specs=list(in_specs) + [ANY] * r_in, out_specs=out_specs_l + [ANY] * r_out,
        out_shape=out_shape_l + list(ride.outs), scratch_shapes=list(scratch_shapes) + list(ride.sems),
        compiler_params=_params(*(["arbitrary"] * len(grid))), name=name)

    def run(*args):
        res = call(*args, *ride.ins)
        return (res[0] if single else list(res[:n_out])), list(res[n_out:])

    return run


def _mm(a, b, *, nt=False, ta=False, out_dtype, tm, tn, a_sqrelu=False, drelu_of=None, b_blocked=False,
        out_blocked=False, name, ride=None):
    k, m = a.shape if ta else a.shape[::-1]
    if b_blocked:
        assert not nt and b.shape[1] == k and b.shape[2] == tn
        n = b.shape[0] * tn
    else:
        n = b.shape[0] if nt else b.shape[1]
        assert (b.shape[1] if nt else b.shape[0]) == k
    tm, tn = min(tm, m), min(tn, n)
    assert m % tm == 0 and n % tn == 0

    def body(*refs):
        a_ref, b_ref = refs[0], refs[1]
        o_ref = refs[-1]
        av = a_ref[...]
        if a_sqrelu:
            av = jnp.square(jnp.maximum(av.astype(F32), 0.0))
        acc = _dot(av, b_ref[...], 0 if ta else 1, 1 if nt else 0)
        if drelu_of is not None:
            acc = acc * (2.0 * jnp.maximum(refs[2][...].astype(F32), 0.0))
        o_ref[...] = acc.astype(out_dtype)

    if b_blocked:
        b_spec = pl.BlockSpec((None, k, tn), lambda i, j: (j, 0, 0))
    elif nt:
        b_spec = pl.BlockSpec((tn, k), lambda i, j: (j, 0))
    else:
        b_spec = pl.BlockSpec((k, tn), lambda i, j: (0, j))
    a_spec = pl.BlockSpec((k, tm), lambda i, j: (0, i)) if ta else pl.BlockSpec((tm, k), lambda i, j: (i, 0))
    in_specs = [a_spec, b_spec]
    args = [a, b]
    if drelu_of is not None:
        in_specs.append(pl.BlockSpec((tm, tn), lambda i, j: (i, j)))
        args.append(drelu_of)
    if out_blocked:
        out_spec = pl.BlockSpec((None, tm, tn), lambda i, j: (j, i, 0))
        out_shape = jax.ShapeDtypeStruct((n // tn, m, tn), out_dtype)
    else:
        out_spec = pl.BlockSpec((tm, tn), lambda i, j: (i, j))
        out_shape = jax.ShapeDtypeStruct((m, n), out_dtype)
    return _pcall(body, ride, grid=(m // tm, n // tn), in_specs=in_specs, out_specs=out_spec, out_shape=out_shape,
                  semantics=("parallel", "parallel"), name=name)(*args)


def _mm_nt_blocked(a, b, *, out_dtype, tm, tn, name):
    m = a.shape[0]
    p, n, kp = b.shape
    assert a.shape[1] == p * kp and m % tm == 0 and n % tn == 0

    def body(a_ref, b_ref, o_ref, acc_ref):
        @pl.when(pl.program_id(2) == 0)
        def _():
            acc_ref[...] = jnp.zeros_like(acc_ref)

        acc_ref[...] += _dot(a_ref[...], b_ref[...], 1, 1)

        @pl.when(pl.program_id(2) == p - 1)
        def _():
            o_ref[...] = acc_ref[...].astype(out_dtype)

    return pl.pallas_call(
        body, grid=(m // tm, n // tn, p),
        in_specs=[pl.BlockSpec((tm, kp), lambda i, j, q: (i, q)), pl.BlockSpec((None, tn, kp), lambda i, j, q: (q, j, 0))],
        out_specs=pl.BlockSpec((tm, tn), lambda i, j, q: (i, j)),
        out_shape=jax.ShapeDtypeStruct((m, n), out_dtype),
        scratch_shapes=[pltpu.VMEM((tm, tn), F32)],
        compiler_params=_params("parallel", "parallel", "arbitrary"), name=name)(a, b)


ROW_TILE = 512
TM_FWD, TM_DX, TM_DW, TN = 2048, 1024, 1024, 512


def _norm_fwd(x, w, *, out_dtype, res=None, name):
    t, d = x.shape

    def body(*refs):
        x_ref, w_ref, o_ref = refs[0], refs[1], refs[-1]
        xv = x_ref[...]
        y = xv * lax.rsqrt(jnp.mean(xv * xv, axis=-1, keepdims=True) + NORM_EPS) * w_ref[...]
        if res is not None:
            y = refs[2][...] + y
        o_ref[...] = y.astype(out_dtype)

    row = pl.BlockSpec((ROW_TILE, d), lambda i: (i, 0))
    in_specs = [row, pl.BlockSpec((1, d), lambda i: (0, 0))] + ([row] if res is not None else [])
    args = [x, w] + ([res] if res is not None else [])
    return pl.pallas_call(body, grid=(t // ROW_TILE,), in_specs=in_specs, out_specs=row,
                          out_shape=jax.ShapeDtypeStruct((t, d), out_dtype),
                          compiler_params=_params("parallel"), name=name)(*args)


def _norm_bwd(dy, x, w, *, out_dtype, add=None, name, ride=None):
    t, d = x.shape

    def body(*refs):
        dy_ref, x_ref, w_ref = refs[0], refs[1], refs[2]
        dx_ref, dw_ref = refs[-2], refs[-1]
        xv = x_ref[...]
        rstd = lax.rsqrt(jnp.mean(xv * xv, axis=-1, keepdims=True) + NORM_EPS)
        xhat = xv * rstd
        dyv = dy_ref[...].astype(F32)
        g = dyv * w_ref[...]
        dx = rstd * (g - xhat * jnp.mean(g * xhat, axis=-1, keepdims=True))
        if add is not None:
            dx = dx + refs[3][...]
        dx_ref[...] = dx.astype(out_dtype)

        @pl.when(pl.program_id(0) == 0)
        def _():
            dw_ref[...] = jnp.zeros_like(dw_ref)

        dw_ref[...] += jnp.sum(dyv * xhat, axis=0, keepdims=True)

    row = pl.BlockSpec((ROW_TILE, d), lambda i: (i, 0))
    vec = pl.BlockSpec((1, d), lambda i: (0, 0))
    in_specs = [row, row, vec] + ([row] if add is not None else [])
    args = [dy, x, w] + ([add] if add is not None else [])
    return _pcall(body, ride, grid=(t // ROW_TILE,), in_specs=in_specs, out_specs=[row, vec],
                  out_shape=[jax.ShapeDtypeStruct((t, d), out_dtype), jax.ShapeDtypeStruct((1, d), F32)],
                  semantics=("arbitrary",), name=name)(*args)


def _loss_fwd_bwd(y, target):
    t, d = y.shape

    def body(y_ref, t_ref, l_ref, dy_ref):
        diff = y_ref[...] - t_ref[...]
        dy_ref[...] = diff * (1.0 / d)

        @pl.when(pl.program_id(0) == 0)
        def _():
            l_ref[...] = jnp.zeros_like(l_ref)

        l_ref[...] += 0.5 * jnp.sum(jnp.mean(diff * diff, axis=-1, keepdims=True), axis=0, keepdims=True)

    row = pl.BlockSpec((ROW_TILE, d), lambda i: (i, 0))
    return pl.pallas_call(body, grid=(t // ROW_TILE,), in_specs=[row, row],
                          out_specs=[pl.BlockSpec((8, LANES), lambda i: (0, 0)), row],
                          out_shape=[jax.ShapeDtypeStruct((8, LANES), F32), jax.ShapeDtypeStruct((t, d), F32)],
                          compiler_params=_params("arbitrary"), name="loss")(y, target)


GLA_UNROLL = (8, 4)


def _gla_specs(t):
    return [pl.BlockSpec((t, GLA_KW), lambda i: (0, 0)),
            pl.BlockSpec((t, GLA_KW), lambda i: (0, 1)),
            pl.BlockSpec((t, GROUP_WIDTH), lambda i: (0, 1)),
            pl.BlockSpec((t, GROUP_WIDTH), lambda i: (0, 0)),
            pl.BlockSpec((t, LANES), lambda i: (0, 4)),
            pl.BlockSpec((LANES, GLA_KW), lambda i: (0, 0)),
            pl.BlockSpec((1, GLA_KW), lambda i: (0, 0)),
            pl.BlockSpec((1, GROUP_WIDTH), lambda i: (0, 0))]


def _gla_fwd(pmm, pel, w_up, b_a, gnorm_w, ride=None):
    t = pmm.shape[0]
    nc = t // CHUNK
    scale = GLA_DK ** -0.5

    def body(q_ref, k_ref, v_ref, r_ref, a_ref, wup_ref, ba_ref, gw_ref, o_ref, st_ref, la_scr, s_scr):
        z = _dot(a_ref[...], wup_ref[...]) + ba_ref[...]
        la_scr[...] = _log_sigmoid(z) * (1.0 / GLA_GATE_TAU)
        s_scr[...] = jnp.zeros_like(s_scr)
        tri = (_iota((CHUNK, CHUNK), 1) <= _iota((CHUNK, CHUNK), 0)).astype(F32)

        def chunk(c, carry):
            rows = pl.ds(pl.multiple_of(c * CHUNK, CHUNK), CHUNK)
            cum = _dot_exact(tri, la_scr[rows, :])
            tot = cum[CHUNK - 1:CHUNK, :]
            kd = k_ref[rows, :].astype(F32) * jnp.exp(tot - cum)
            decay = jnp.exp(tot)
            qs = q_ref[rows, :].astype(F32) * scale
            vv = v_ref[rows, :].astype(F32)
            rr = r_ref[rows, :]
            gate = rr * jax.nn.sigmoid(rr) * gw_ref[...]
            for h in range(GLA_HEADS):
                ks = slice(h * GLA_DK, (h + 1) * GLA_DK)
                vs = slice(h * GLA_DV, (h + 1) * GLA_DV)
                inc_t = _dot(vv[:, vs].T, kd[:, ks])
                s_t = s_scr[vs, :] * decay[:, ks] + inc_t
                s_scr[vs, :] = s_t
                st_ref[c, vs, :] = s_t
                o = _dot(qs[:, ks], s_t, 1, 1)
                y = o * lax.rsqrt(jnp.mean(o * o, axis=-1, keepdims=True) + NORM_EPS)
                o_ref[rows, vs] = (y * gate[:, vs]).astype(o_ref.dtype)
            return carry

        lax.fori_loop(0, nc, chunk, 0, unroll=GLA_UNROLL[0])

    return _pcall(
        body, ride, grid=(1,), in_specs=_gla_specs(t),
        out_specs=[pl.BlockSpec((t, GROUP_WIDTH), lambda i: (0, 0)),
                   pl.BlockSpec((nc, GLA_HEADS * GLA_DV, GLA_DK), lambda i: (0, 0, 0))],
        out_shape=[jax.ShapeDtypeStruct((t, GROUP_WIDTH), ACT_DTYPE),
                   jax.ShapeDtypeStruct((nc, GLA_HEADS * GLA_DV, GLA_DK), F32)],
        scratch_shapes=[pltpu.VMEM((t, GLA_KW), F32), pltpu.VMEM((GLA_HEADS * GLA_DV, GLA_DK), F32)],
        semantics=("arbitrary",), name="gla_fwd")(pmm, pmm, pmm, pel, pel, w_up, b_a, gnorm_w)


def _gla_bwd(pmm, pel, w_up, b_a, gnorm_w, states, dmix, ride=None):
    t = pmm.shape[0]
    nc = t // CHUNK
    scale = GLA_DK ** -0.5

    def body(q_ref, k_ref, v_ref, r_ref, a_ref, wup_ref, ba_ref, gw_ref, st_ref, do_ref,
             dq_ref, dk_ref, dv_ref, dr_ref, da_ref, dwup_ref, dba_ref, dgw_ref, la_scr, dz_scr, ds_scr):
        z = _dot(a_ref[...], wup_ref[...]) + ba_ref[...]
        la_scr[...] = _log_sigmoid(z) * (1.0 / GLA_GATE_TAU)
        ds_scr[...] = jnp.zeros_like(ds_scr)
        dgw_ref[...] = jnp.zeros_like(dgw_ref)
        row_i, col_i = _iota((CHUNK, CHUNK), 0), _iota((CHUNK, CHUNK), 1)
        tri = (col_i <= row_i).astype(F32)
        tri_strict = (col_i < row_i).astype(F32)

        def chunk(n, carry):
            c = nc - 1 - n
            rows = pl.ds(pl.multiple_of(c * CHUNK, CHUNK), CHUNK)
            cum = _dot_exact(tri, la_scr[rows, :])
            tot = cum[CHUNK - 1:CHUNK, :]
            e = jnp.exp(tot - cum)
            kd = k_ref[rows, :].astype(F32) * e
            decay = jnp.exp(tot)
            qs = q_ref[rows, :].astype(F32) * scale
            vv = v_ref[rows, :].astype(F32)
            rr = r_ref[rows, :]
            sig = jax.nn.sigmoid(rr)
            silu = rr * sig
            dsilu = sig * (1.0 + rr * (1.0 - sig))
            dout = do_ref[rows, :]
            gw = gw_ref[...]
            c_prev = jnp.maximum(c - 1, 0)
            has_prev = (c > 0).astype(F32)
            zc = _dot(a_ref[rows, :], wup_ref[...]) + ba_ref[...]
            dz_scale = jax.nn.sigmoid(-zc) * (1.0 / GLA_GATE_TAU)
            for h in range(GLA_HEADS):
                ks = slice(h * GLA_DK, (h + 1) * GLA_DK)
                vs = slice(h * GLA_DV, (h + 1) * GLA_DV)
                s_t = st_ref[c, vs, :]
                s_prev = st_ref[c_prev, vs, :] * has_prev
                o = _dot(qs[:, ks], s_t, 1, 1)
                rstd = lax.rsqrt(jnp.mean(o * o, axis=-1, keepdims=True) + NORM_EPS)
                y = o * rstd
                dg = dout[:, vs]
                dgw_ref[:, vs] += jnp.sum(dg * y * silu[:, vs], axis=0, keepdims=True)
                dr_ref[rows, vs] = (dg * y * gw[:, vs] * dsilu[:, vs]).astype(dr_ref.dtype)
                dy = dg * gw[:, vs] * silu[:, vs]
                d_o = rstd * (dy - y * jnp.mean(dy * y, axis=-1, keepdims=True))
                dq_ref[rows, ks] = (_dot(d_o, s_t) * scale).astype(dq_ref.dtype)
                ds_t = ds_scr[vs, :] + _dot(d_o.T, qs[:, ks])
                dv_ref[rows, vs] = _dot(kd[:, ks], ds_t, 1, 1).astype(dv_ref.dtype)
                dkd = _dot(vv[:, vs], ds_t)
                ddecay = jnp.sum(ds_t * s_prev, axis=0, keepdims=True)
                ds_scr[vs, :] = ds_t * decay[:, ks]
                dla = ddecay * decay[:, ks] + _dot_exact(tri_strict, dkd * kd[:, ks])
                dz_scr[rows, ks] = dla * dz_scale[:, ks]
                dk_ref[rows, ks] = (dkd * e[:, ks]).astype(dk_ref.dtype)
            return carry

        lax.fori_loop(0, nc, chunk, 0, unroll=GLA_UNROLL[1])
        dz = dz_scr[...]
        da_ref[...] = _dot(dz, wup_ref[...], 1, 1).astype(da_ref.dtype)
        dwup_ref[...] = _dot(a_ref[...].T, dz)
        dba_ref[...] = jnp.sum(dz, axis=0, keepdims=True)

    in_specs = _gla_specs(t) + [
        pl.BlockSpec((nc, GLA_HEADS * GLA_DV, GLA_DK), lambda i: (0, 0, 0)),
        pl.BlockSpec((t, GROUP_WIDTH), lambda i: (0, 0))]
    full = lambda r, c: pl.BlockSpec((r, c), lambda i: (0, 0))
    return _pcall(
        body, ride, grid=(1,), in_specs=in_specs,
        out_specs=[full(t, GLA_KW), full(t, GLA_KW), full(t, GROUP_WIDTH), full(t, GROUP_WIDTH), full(t, LANES),
                   full(LANES, GLA_KW), full(1, GLA_KW), full(1, GROUP_WIDTH)],
        out_shape=[jax.ShapeDtypeStruct((t, GLA_KW), ACT_DTYPE), jax.ShapeDtypeStruct((t, GLA_KW), ACT_DTYPE),
                   jax.ShapeDtypeStruct((t, GROUP_WIDTH), ACT_DTYPE), jax.ShapeDtypeStruct((t, GROUP_WIDTH), ACT_DTYPE),
                   jax.ShapeDtypeStruct((t, LANES), ACT_DTYPE), jax.ShapeDtypeStruct((LANES, GLA_KW), F32),
                   jax.ShapeDtypeStruct((1, GLA_KW), F32), jax.ShapeDtypeStruct((1, GROUP_WIDTH), F32)],
        scratch_shapes=[pltpu.VMEM((t, GLA_KW), F32), pltpu.VMEM((t, GLA_KW), F32),
                        pltpu.VMEM((GLA_HEADS * GLA_DV, GLA_DK), F32)],
        semantics=("arbitrary",), name="gla_bwd")(
            pmm, pmm, pmm, pel, pel, w_up, b_a, gnorm_w, states, dmix)


CUM_BLOCK = 256


def _fox_gate_fwd(pel, b_f):
    t = pel.shape[0]
    nb = t // CUM_BLOCK

    def body(f_ref, b_ref, cum_ref, cum_t_ref):
        tri = (_iota((CUM_BLOCK, CUM_BLOCK), 1) <= _iota((CUM_BLOCK, CUM_BLOCK), 0)).astype(F32)
        carry = jnp.zeros((1, LANES), F32)
        for blk in range(nb):
            rows = slice(blk * CUM_BLOCK, (blk + 1) * CUM_BLOCK)
            cum = _dot_exact(tri, _log_sigmoid(f_ref[rows, :] + b_ref[...])) + carry
            cum_ref[rows, :] = cum
            cum_t_ref[blk] = cum.T[:ATT_HEADS, :]
            carry = cum[CUM_BLOCK - 1:CUM_BLOCK, :]

    return pl.pallas_call(
        body, grid=(1,),
        in_specs=[pl.BlockSpec((t, LANES), lambda i: (0, 5)), pl.BlockSpec((1, LANES), lambda i: (0, 0))],
        out_specs=[pl.BlockSpec((t, LANES), lambda i: (0, 0)),
                   pl.BlockSpec((nb, ATT_HEADS, CUM_BLOCK), lambda i: (0, 0, 0))],
        out_shape=[jax.ShapeDtypeStruct((t, LANES), F32), jax.ShapeDtypeStruct((nb, ATT_HEADS, CUM_BLOCK), F32)],
        compiler_params=_params("arbitrary"), name="fox_gate_fwd")(pel, b_f)


def _fox_gate_bwd(pel, b_f, dcum_t, dcum_q):
    t = pel.shape[0]
    nb = t // CUM_BLOCK

    def body(f_ref, b_ref, dct_ref, dcq_ref, df_ref, db_ref):
        tri_up = (_iota((CUM_BLOCK, CUM_BLOCK), 1) >= _iota((CUM_BLOCK, CUM_BLOCK), 0)).astype(F32)
        carry = jnp.zeros((1, LANES), F32)
        db = jnp.zeros((1, LANES), F32)
        for blk in reversed(range(nb)):
            rows = slice(blk * CUM_BLOCK, (blk + 1) * CUM_BLOCK)
            dls = _dot_exact(tri_up, dct_ref[blk].T + dcq_ref[rows, :]) + carry
            carry = dls[0:1, :]
            df = dls * jax.nn.sigmoid(-(f_ref[rows, :] + b_ref[...]))
            df_ref[rows, :] = df.astype(df_ref.dtype)
            db = db + jnp.sum(df, axis=0, keepdims=True)
        db_ref[...] = db

    return pl.pallas_call(
        body, grid=(1,),
        in_specs=[pl.BlockSpec((t, LANES), lambda i: (0, 5)), pl.BlockSpec((1, LANES), lambda i: (0, 0)),
                  pl.BlockSpec((nb, LANES, CUM_BLOCK), lambda i: (0, 0, 0)), pl.BlockSpec((t, LANES), lambda i: (0, 0))],
        out_specs=[pl.BlockSpec((t, LANES), lambda i: (0, 0)), pl.BlockSpec((1, LANES), lambda i: (0, 0))],
        out_shape=[jax.ShapeDtypeStruct((t, LANES), ACT_DTYPE), jax.ShapeDtypeStruct((1, LANES), F32)],
        compiler_params=_params("arbitrary"), name="fox_gate_bwd")(pel, b_f, dcum_t, dcum_q)


FOX_Q_BLOCK = 256


assert FOX_Q_BLOCK == CUM_BLOCK


def _fox_scores(q_ref, k_ref, cum_ref, cum_t_ref, h, i):
    hs = slice(h * HEAD_DIM, (h + 1) * HEAD_DIM)
    nb = cum_t_ref.shape[0]
    key_gate = jnp.concatenate([cum_t_ref[kb, h:h + 1, :] for kb in range(nb)], axis=1)
    s = _dot(q_ref[:, hs], k_ref[:, hs], 1, 1) * (HEAD_DIM ** -0.5) + (cum_ref[:, h:h + 1] - key_gate)
    shape = (FOX_Q_BLOCK, nb * FOX_Q_BLOCK)
    return jnp.where(_iota(shape, 1) <= i * FOX_Q_BLOCK + _iota(shape, 0), s, NEG)


def _fox_specs(t):
    bq, nb = FOX_Q_BLOCK, t // FOX_Q_BLOCK
    return [pl.BlockSpec((bq, GROUP_WIDTH), lambda i: (i, 2)), pl.BlockSpec((t, GROUP_WIDTH), lambda i: (0, 3)),
            pl.BlockSpec((t, GROUP_WIDTH), lambda i: (0, 4)), pl.BlockSpec((bq, LANES), lambda i: (i, 0)),
            pl.BlockSpec((nb, ATT_HEADS, bq), lambda i: (0, 0, 0))]


def _fox_fwd(pmm, cum, cum_t, ride=None):
    t = pmm.shape[0]
    bq = FOX_Q_BLOCK

    def body(q_ref, k_ref, v_ref, cum_ref, cum_t_ref, o_ref, lse_ref):
        i = pl.program_id(0)
        lse_ref[...] = jnp.zeros_like(lse_ref)
        for h in range(ATT_HEADS):
            hs = slice(h * HEAD_DIM, (h + 1) * HEAD_DIM)
            s = _fox_scores(q_ref, k_ref, cum_ref, cum_t_ref, h, i)
            m = jnp.max(s, axis=-1, keepdims=True)
            p = jnp.exp(s - m)
            l = jnp.sum(p, axis=-1, keepdims=True)
            o_ref[:, hs] = (_dot(p, v_ref[:, hs]) / l).astype(o_ref.dtype)
            lse_ref[:, h:h + 1] = m + jnp.log(l)

    return _pcall(
        body, ride, grid=(t // bq,), in_specs=_fox_specs(t),
        out_specs=[pl.BlockSpec((bq, GROUP_WIDTH), lambda i: (i, 0)), pl.BlockSpec((bq, LANES), lambda i: (i, 0))],
        out_shape=[jax.ShapeDtypeStruct((t, GROUP_WIDTH), ACT_DTYPE), jax.ShapeDtypeStruct((t, LANES), F32)],
        semantics=("parallel",), name="fox_fwd")(pmm, pmm, pmm, cum, cum_t)


def _fox_bwd(pmm, cum, cum_t, lse, dmix, ride=None):
    t = pmm.shape[0]
    bq, nb = FOX_Q_BLOCK, t // FOX_Q_BLOCK
    scale = HEAD_DIM ** -0.5

    def body(q_ref, k_ref, v_ref, cum_ref, cum_t_ref, lse_ref, do_ref, dq_ref, dk_ref, dv_ref, dct_ref, dcq_ref):
        i = pl.program_id(0)

        @pl.when(i == 0)
        def _():
            dk_ref[...] = jnp.zeros_like(dk_ref)
            dv_ref[...] = jnp.zeros_like(dv_ref)
            dct_ref[...] = jnp.zeros_like(dct_ref)

        dcq_ref[...] = jnp.zeros_like(dcq_ref)
        for h in range(ATT_HEADS):
            hs = slice(h * HEAD_DIM, (h + 1) * HEAD_DIM)
            s = _fox_scores(q_ref, k_ref, cum_ref, cum_t_ref, h, i)
            p = jnp.exp(s - lse_ref[:, h:h + 1])
            do = do_ref[:, hs]
            dp = _dot(do, v_ref[:, hs], 1, 1)
            ds = p * (dp - jnp.sum(p * dp, axis=-1, keepdims=True))
            dq_ref[:, hs] = (_dot(ds, k_ref[:, hs]) * scale).astype(dq_ref.dtype)
            dk_ref[:, hs] += _dot(ds, q_ref[:, hs], 0, 0) * scale
            dv_ref[:, hs] += _dot(p, do, 0, 0)
            key_side = -jnp.sum(ds, axis=0, keepdims=True)
            for kb in range(nb):
                dct_ref[kb, h:h + 1, :] += key_side[:, kb * bq:(kb + 1) * bq]
            dcq_ref[:, h:h + 1] = jnp.sum(ds, axis=1, keepdims=True)

    whole = pl.BlockSpec((t, GROUP_WIDTH), lambda i: (0, 0))
    return _pcall(
        body, ride, grid=(t // bq,),
        in_specs=_fox_specs(t) + [pl.BlockSpec((bq, LANES), lambda i: (i, 0)),
                                  pl.BlockSpec((bq, GROUP_WIDTH), lambda i: (i, 1))],
        out_specs=[pl.BlockSpec((bq, GROUP_WIDTH), lambda i: (i, 0)), whole, whole,
                   pl.BlockSpec((nb, LANES, bq), lambda i: (0, 0, 0)), pl.BlockSpec((bq, LANES), lambda i: (i, 0))],
        out_shape=[jax.ShapeDtypeStruct((t, GROUP_WIDTH), ACT_DTYPE), jax.ShapeDtypeStruct((t, GROUP_WIDTH), F32),
                   jax.ShapeDtypeStruct((t, GROUP_WIDTH), F32), jax.ShapeDtypeStruct((nb, LANES, bq), F32),
                   jax.ShapeDtypeStruct((t, LANES), F32)],
        semantics=("arbitrary",), name="fox_bwd")(pmm, pmm, pmm, cum, cum_t, lse, dmix)


CA_Q_BLOCK = 4 * CHUNK
CA_WINDOW = CA_Q_BLOCK + CA_LEFT
CA_BASE = 1024


def _ca_bias_base(rel_bias):
    n = rel_bias.shape[0]
    flat = CA_Q_BLOCK + CA_LEFT - REL_CLIP
    tail = CA_BASE - flat - (2 * REL_CLIP + 1)
    return jnp.concatenate([jnp.broadcast_to(rel_bias[:, 2 * REL_CLIP:], (n, flat)), rel_bias[:, ::-1],
                            jnp.broadcast_to(rel_bias[:, :1], (n, tail))], axis=1)


def _ca_bias_base_grad(dbase):
    flat = CA_Q_BLOCK + CA_LEFT - REL_CLIP
    mid = dbase[:, flat:flat + 2 * REL_CLIP + 1][:, ::-1]
    lo = jnp.sum(dbase[:, flat + 2 * REL_CLIP + 1:], axis=1, keepdims=True)
    hi = jnp.sum(dbase[:, :flat], axis=1, keepdims=True)
    pad = jnp.zeros((dbase.shape[0], 2 * REL_CLIP - 1), F32)
    return mid + jnp.concatenate([lo, pad, hi], axis=1)


def _ca_mask(i):
    r, j = _iota((CA_Q_BLOCK, CA_WINDOW), 0), _iota((CA_Q_BLOCK, CA_WINDOW), 1)
    rc, jc = r // CHUNK, j // CHUNK
    return (jc >= rc) & (jc <= rc + CA_LEFT // CHUNK) & (i * CA_Q_BLOCK + j >= CA_LEFT)


def _ca_fill_bias(i, base_ref, bias_scr):
    @pl.when(i == 0)
    def _():
        for h in range(ATT_HEADS):
            rows = jnp.broadcast_to(base_ref[h:h + 1, :], (CA_Q_BLOCK, CA_BASE))
            bias_scr[h] = pltpu.roll(rows, CA_BASE - CA_Q_BLOCK, 1, stride=1, stride_axis=0)[:, :CA_WINDOW]


def _ca_scores(q_ref, kp_ref, bias_scr, win, h, mask):
    hs = slice(h * HEAD_DIM, (h + 1) * HEAD_DIM)
    s = _dot(q_ref[:, hs], kp_ref[win, hs], 1, 1) * (HEAD_DIM ** -0.5)
    return jnp.where(mask, s + bias_scr[h], NEG)


CA_BIAS_SCRATCH = pltpu.VMEM((ATT_HEADS, CA_Q_BLOCK, CA_WINDOW), F32)


def _ca_fwd(pmm, kp, vp, base, ride=None):
    t = pmm.shape[0]

    def body(q_ref, kp_ref, vp_ref, base_ref, o_ref, lse_ref, bias_scr):
        i = pl.program_id(0)
        _ca_fill_bias(i, base_ref, bias_scr)
        win = pl.ds(pl.multiple_of(i * CA_Q_BLOCK, CA_Q_BLOCK), CA_WINDOW)
        mask = _ca_mask(i)
        lse_ref[...] = jnp.zeros_like(lse_ref)
        for h in range(ATT_HEADS):
            hs = slice(h * HEAD_DIM, (h + 1) * HEAD_DIM)
            s = _ca_scores(q_ref, kp_ref, bias_scr, win, h, mask)
            m = jnp.max(s, axis=-1, keepdims=True)
            p = jnp.exp(s - m)
            l = jnp.sum(p, axis=-1, keepdims=True)
            o_ref[:, hs] = (_dot(p, vp_ref[win, hs]) / l).astype(o_ref.dtype)
            lse_ref[:, h:h + 1] = m + jnp.log(l)

    padded = pl.BlockSpec((t + CA_LEFT, GROUP_WIDTH), lambda i: (0, 0))
    return _pcall(
        body, ride, grid=(t // CA_Q_BLOCK,),
        in_specs=[pl.BlockSpec((CA_Q_BLOCK, GROUP_WIDTH), lambda i: (i, 0)), padded, padded,
                  pl.BlockSpec((ATT_HEADS, CA_BASE), lambda i: (0, 0))],
        out_specs=[pl.BlockSpec((CA_Q_BLOCK, GROUP_WIDTH), lambda i: (i, 0)),
                   pl.BlockSpec((CA_Q_BLOCK, LANES), lambda i: (i, 0))],
        out_shape=[jax.ShapeDtypeStruct((t, GROUP_WIDTH), ACT_DTYPE), jax.ShapeDtypeStruct((t, LANES), F32)],
        scratch_shapes=[CA_BIAS_SCRATCH], semantics=("arbitrary",), name="ca_fwd")(pmm, kp, vp, base)


def _ca_bwd(pmm, kp, vp, base, lse, dmix, ride=None):
    t = pmm.shape[0]
    scale = HEAD_DIM ** -0.5

    def body(q_ref, kp_ref, vp_ref, base_ref, lse_ref, do_ref, dq_ref, dkp_ref, dvp_ref, dbase_ref, bias_scr):
        i = pl.program_id(0)
        _ca_fill_bias(i, base_ref, bias_scr)

        @pl.when(i == 0)
        def _():
            dkp_ref[...] = jnp.zeros_like(dkp_ref)
            dvp_ref[...] = jnp.zeros_like(dvp_ref)
            dbase_ref[...] = jnp.zeros_like(dbase_ref)

        win = pl.ds(pl.multiple_of(i * CA_Q_BLOCK, CA_Q_BLOCK), CA_WINDOW)
        mask = _ca_mask(i)
        flip = (_iota((CA_Q_BLOCK, CA_Q_BLOCK), 0) + _iota((CA_Q_BLOCK, CA_Q_BLOCK), 1) == CA_Q_BLOCK - 1).astype(F32)
        for h in range(ATT_HEADS):
            hs = slice(h * HEAD_DIM, (h + 1) * HEAD_DIM)
            s = _ca_scores(q_ref, kp_ref, bias_scr, win, h, mask)
            p = jnp.exp(s - lse_ref[:, h:h + 1])
            do = do_ref[:, hs]
            dp = _dot(do, vp_ref[win, hs], 1, 1)
            ds = p * (dp - jnp.sum(p * dp, axis=-1, keepdims=True))
            dq_ref[:, hs] = (_dot(ds, kp_ref[win, hs]) * scale).astype(dq_ref.dtype)
            dkp_ref[win, hs] += _dot(ds, q_ref[:, hs], 0, 0) * scale
            dvp_ref[win, hs] += _dot(p, do, 0, 0)
            rev = jnp.concatenate([_dot(flip, ds), jnp.zeros((CA_Q_BLOCK, CA_BASE - CA_WINDOW), F32)], axis=1)
            lined = pltpu.roll(rev, 1, 1, stride=1, stride_axis=0)
            dbase_ref[h:h + 1, :] += jnp.sum(lined, axis=0, keepdims=True)

    padded = pl.BlockSpec((t + CA_LEFT, GROUP_WIDTH), lambda i: (0, 0))
    return _pcall(
        body, ride, grid=(t // CA_Q_BLOCK,),
        in_specs=[pl.BlockSpec((CA_Q_BLOCK, GROUP_WIDTH), lambda i: (i, 0)), padded, padded,
                  pl.BlockSpec((ATT_HEADS, CA_BASE), lambda i: (0, 0)),
                  pl.BlockSpec((CA_Q_BLOCK, LANES), lambda i: (i, 0)),
                  pl.BlockSpec((CA_Q_BLOCK, GROUP_WIDTH), lambda i: (i, 0))],
        out_specs=[pl.BlockSpec((CA_Q_BLOCK, GROUP_WIDTH), lambda i: (i, 0)), padded, padded,
                   pl.BlockSpec((ATT_HEADS, CA_BASE), lambda i: (0, 0))],
        out_shape=[jax.ShapeDtypeStruct((t, GROUP_WIDTH), ACT_DTYPE),
                   jax.ShapeDtypeStruct((t + CA_LEFT, GROUP_WIDTH), F32),
                   jax.ShapeDtypeStruct((t + CA_LEFT, GROUP_WIDTH), F32),
                   jax.ShapeDtypeStruct((ATT_HEADS, CA_BASE), F32)],
        scratch_shapes=[CA_BIAS_SCRATCH], semantics=("arbitrary",), name="ca_bwd")(pmm, kp, vp, base, lse, dmix)


GELU_C = 0.7978845608028654
GELU_A = 0.044715


def _shift_down(v, k, fill):
    return jnp.where(_iota(v.shape, 0) >= k, pltpu.roll(v, k, 0), fill)


def _shift_up(v, k, fill):
    t = v.shape[0]
    return jnp.where(_iota(v.shape, 0) < t - k, pltpu.roll(v, t - k, 0), fill)


def _linear_scan(a, b, shift):
    k = 1
    while k < a.shape[0]:
        b = a * shift(b, k, 0.0) + b
        a = a * shift(a, k, 1.0)
        k *= 2
    return b


def _neg_expm1(y):
    series = -y * (1.0 + y * (0.5 + y * (1.0 / 6.0 + y * (1.0 / 24.0 + y * (1.0 / 120.0)))))
    return jnp.where(y > -0.1, series, 1.0 - jnp.exp(y))


def _lru_forward(x, g_in, cw, cb, wa, ba, wx, bx, lam):
    xs = [_shift_down(x, CONV_WIDTH - 1 - j, 0.0) for j in range(CONV_WIDTH - 1)] + [x]
    xc = cb + sum(cw[j:j + 1, :] * xs[j] for j in range(CONV_WIDTH))
    r = jax.nn.sigmoid(_dot(xc, wa) + ba)
    i = jax.nn.sigmoid(_dot(xc, wx) + bx)
    lsl = _log_sigmoid(lam)
    la = LRU_C * r * lsl
    a = jnp.exp(la)
    s = jnp.sqrt(_neg_expm1(2.0 * la))
    h = _linear_scan(a, s * (i * xc), _shift_down)
    u = GELU_C * (g_in + GELU_A * g_in * g_in * g_in)
    th = jnp.tanh(u)
    gelu = 0.5 * g_in * (1.0 + th)
    return xs, xc, r, i, lsl, a, s, h, th, gelu


def _lru_specs(t):
    col = lambda off: pl.BlockSpec((t, LANES), lambda j: (0, j + off))
    vec = pl.BlockSpec((1, LANES), lambda j: (0, j))
    mat = pl.BlockSpec((None, LANES, LANES), lambda j: (j, 0, 0))
    return [col(0), col(GROUP_WIDTH // LANES), pl.BlockSpec((CONV_WIDTH, LANES), lambda j: (0, j)),
            vec, mat, vec, mat, vec, vec]


def _lru_fwd(pel, conv_w, conv_b, wa, ba, wx, bx, lam, ride=None):
    t = pel.shape[0]

    def body(g_ref, x_ref, cw_ref, cb_ref, wa_ref, ba_ref, wx_ref, bx_ref, lam_ref, o_ref):
        res = _lru_forward(x_ref[...], g_ref[...], cw_ref[...], cb_ref[...], wa_ref[...], ba_ref[...],
                           wx_ref[...], bx_ref[...], lam_ref[...])
        o_ref[...] = (res[7] * res[9]).astype(o_ref.dtype)

    return _pcall(
        body, ride, grid=(GROUP_WIDTH // LANES,), in_specs=_lru_specs(t),
        out_specs=pl.BlockSpec((t, LANES), lambda j: (0, j)),
        out_shape=jax.ShapeDtypeStruct((t, GROUP_WIDTH), ACT_DTYPE),
        semantics=("parallel",), name="lru_fwd")(pel, pel, conv_w, conv_b, wa, ba, wx, bx, lam)


def _lru_bwd(pel, conv_w, conv_b, wa, ba, wx, bx, lam, dmix, ride=None):
    t = pel.shape[0]

    def body(g_ref, x_ref, cw_ref, cb_ref, wa_ref, ba_ref, wx_ref, bx_ref, lam_ref, do_ref,
             dg_ref, dx_ref, dcw_ref, dcb_ref, dwa_ref, dba_ref, dwx_ref, dbx_ref, dlam_ref):
        g_in, cw, lam = g_ref[...], cw_ref[...], lam_ref[...]
        xs, xc, r, i, lsl, a, s, h, th, gelu = _lru_forward(
            x_ref[...], g_in, cw, cb_ref[...], wa_ref[...], ba_ref[...], wx_ref[...], bx_ref[...], lam)
        dout = do_ref[...]
        dgelu = 0.5 * (1.0 + th) + 0.5 * g_in * (1.0 - th * th) * GELU_C * (1.0 + 3.0 * GELU_A * g_in * g_in)
        dg_ref[...] = (dout * h * dgelu).astype(dg_ref.dtype)
        gsum = _linear_scan(_shift_up(a, 1, 0.0), dout * gelu, _shift_up)
        da = gsum * _shift_down(h, 1, 0.0)
        di = gsum * s * xc
        dla = da * a - gsum * (i * xc) * (a * a / s)
        dlam_ref[...] = jnp.sum(dla * (LRU_C * r), axis=0, keepdims=True) * jax.nn.sigmoid(-lam)
        dpr = dla * (LRU_C * lsl) * r * (1.0 - r)
        dpi = di * i * (1.0 - i)
        dxc = gsum * s * i + _dot(dpr, wa_ref[...], 1, 1) + _dot(dpi, wx_ref[...], 1, 1)
        xct = xc.T
        dwa_ref[...] = _dot(xct, dpr)
        dwx_ref[...] = _dot(xct, dpi)
        dba_ref[...] = jnp.sum(dpr, axis=0, keepdims=True)
        dbx_ref[...] = jnp.sum(dpi, axis=0, keepdims=True)
        dcb_ref[...] = jnp.sum(dxc, axis=0, keepdims=True)
        for j in range(CONV_WIDTH):
            dcw_ref[j:j + 1, :] = jnp.sum(dxc * xs[j], axis=0, keepdims=True)
        dx = cw[CONV_WIDTH - 1:CONV_WIDTH, :] * dxc
        for j in range(CONV_WIDTH - 1):
            dx = dx + cw[j:j + 1, :] * _shift_up(dxc, CONV_WIDTH - 1 - j, 0.0)
        dx_ref[...] = dx.astype(dx_ref.dtype)

    col = pl.BlockSpec((t, LANES), lambda j: (0, j))
    vec = pl.BlockSpec((1, LANES), lambda j: (0, j))
    mat = pl.BlockSpec((None, LANES, LANES), lambda j: (j, 0, 0))
    nb = GROUP_WIDTH // LANES
    vshape = jax.ShapeDtypeStruct((1, GROUP_WIDTH), F32)
    mshape = jax.ShapeDtypeStruct((nb, LANES, LANES), F32)
    return _pcall(
        body, ride, grid=(nb,),
        in_specs=_lru_specs(t) + [pl.BlockSpec((t, LANES), lambda j: (0, j + nb))],
        out_specs=[col, col, pl.BlockSpec((CONV_WIDTH, LANES), lambda j: (0, j)), vec, mat, vec, mat, vec, vec],
        out_shape=[jax.ShapeDtypeStruct((t, GROUP_WIDTH), ACT_DTYPE), jax.ShapeDtypeStruct((t, GROUP_WIDTH), ACT_DTYPE),
                   jax.ShapeDtypeStruct((CONV_WIDTH, GROUP_WIDTH), F32), vshape, mshape, vshape, mshape, vshape, vshape],
        semantics=("parallel",), name="lru_bwd")(
            pel, pel, conv_w, conv_b, wa, ba, wx, bx, lam, dmix)


def _block_diag_pairs(w):
    z = jnp.zeros((LRU_BLOCK_DIM, LRU_BLOCK_DIM), w.dtype)
    return jnp.stack([jnp.block([[w[2 * j], z], [z, w[2 * j + 1]]]) for j in range(w.shape[0] // 2)])


def _block_diag_pairs_grad(dw):
    b = LRU_BLOCK_DIM
    return jnp.stack([dw[n // 2, (n % 2) * b:(n % 2 + 1) * b, (n % 2) * b:(n % 2 + 1) * b] for n in range(2 * dw.shape[0])])


def _row_tile(r):
    return ROW_TILE if r % ROW_TILE == 0 else r


def _pair_sum(g, got, place, name):
    _, r, c = g.shape
    tile = r

    def body(place_ref, a_ref, b_ref, o_ref):
        o_ref[...] = (a_ref[...].astype(F32) + b_ref[...].astype(F32)).astype(o_ref.dtype)

    blk = pl.BlockSpec((1, tile, c), lambda k, i, place_ref: (k, i, 0))
    return pl.pallas_call(
        body,
        grid_spec=pltpu.PrefetchScalarGridSpec(
            num_scalar_prefetch=1, grid=(N_CHIPS, r // tile),
            in_specs=[pl.BlockSpec((1, tile, c), lambda k, i, place_ref: (2 * k + place_ref[0], i, 0)), blk],
            out_specs=blk),
        out_shape=jax.ShapeDtypeStruct(got.shape, got.dtype),
        compiler_params=_params("parallel", "parallel"), name=name)(place, g, got)


def _adamw_update(g, w_ref, m_ref, v_ref, g_ref, d_ref, nm_ref, nv_ref):
    nm = ADAM_B1 * m_ref[...] + (1.0 - ADAM_B1) * g
    nv = ADAM_B2 * v_ref[...] + (1.0 - ADAM_B2) * jnp.square(g)
    m_hat = nm / (1.0 - ADAM_B1 ** ADAM_STEP)
    v_hat = nv / (1.0 - ADAM_B2 ** ADAM_STEP)
    g_ref[...] = g
    d_ref[...] = -ADAM_LR * (m_hat / (jnp.sqrt(v_hat) + ADAM_EPS) + ADAM_WD * w_ref[...])
    nm_ref[...] = nm
    nv_ref[...] = nv


def _adamw_sharded(parts, w, m, v, place, name):
    n_layers, r, c = w.shape
    tile = _row_tile(r)
    nb = r // tile

    def body(place_ref, *refs):
        layer = pl.program_id(0)
        g = None
        for l in range(n_layers):
            s_ref, r_ref = refs[2 * l], refs[2 * l + 1]
            g_l = s_ref[0].astype(F32) + r_ref[0].astype(F32) + r_ref[1].astype(F32) + r_ref[2].astype(F32)
            g = g_l if g is None else jnp.where(layer == l, g_l, g)
        _adamw_update(g, *refs[2 * n_layers:])

    def part_specs(l):
        rows = lambda q, i: jnp.where(q < l, 0, jnp.where(q > l, nb - 1, i))
        return [pl.BlockSpec((1, tile, c), lambda q, i, place_ref: (place_ref[1], rows(q, i), 0)),
                pl.BlockSpec((3, tile, c), lambda q, i, place_ref: (0, rows(q, i), 0))]

    in_specs, args = [], []
    for l, (s, recv) in enumerate(parts):
        in_specs += part_specs(l)
        args += [s, recv]
    blk = pl.BlockSpec((None, tile, c), lambda q, i, place_ref: (q, i, 0))
    out = jax.ShapeDtypeStruct((n_layers, r, c), F32)
    return pl.pallas_call(
        body,
        grid_spec=pltpu.PrefetchScalarGridSpec(
            num_scalar_prefetch=1, grid=(n_layers, nb), in_specs=in_specs + [blk, blk, blk],
            out_specs=[blk, blk, blk, blk]),
        out_shape=[out, out, out, out], compiler_params=_params("arbitrary", "arbitrary"), name=name)(
            place, *args, w, m, v)


def _adamw_replicated(parts, w, m, v, name):
    p, r, c = parts.shape
    tile = _row_tile(r)

    def body(p_ref, w_ref, m_ref, v_ref, *outs):
        g = p_ref[0].astype(F32)
        for k in range(1, p):
            g = g + p_ref[k].astype(F32)
        _adamw_update(g, w_ref, m_ref, v_ref, *outs)

    blk = pl.BlockSpec((tile, c), lambda i: (i, 0))
    out = jax.ShapeDtypeStruct((r, c), F32)
    return pl.pallas_call(body, grid=(r // tile,),
                          in_specs=[pl.BlockSpec((p, tile, c), lambda i: (0, i, 0)), blk, blk, blk],
                          out_specs=[blk, blk, blk, blk], out_shape=[out, out, out, out],
                          compiler_params=_params("parallel"), name=name)(parts, w, m, v)


SLAB_COLS = 1024
SHARDED = {"norm_w": 2, "w_in_even": 2, "gla_w_a_up": 2, "w_out_even": 1, "w_in_odd": 2, "conv_w": 2, "conv_b": 1,
           "lru_b_a": 1, "lru_b_x": 1, "lru_lambda": 1, "w_out_odd": 1, "w_mlp_up": 2, "w_mlp_down": 1}
REPLICATED = ["gla_b_a", "gla_norm_w", "fox_b_f", "rel_bias", "lru_w_a", "lru_w_x"]
WEIGHTS = ["norm_w", "w_in_even", "gla_w_a_up", "gla_b_a", "gla_norm_w", "fox_b_f", "w_out_even", "w_in_odd",
           "rel_bias", "conv_w", "conv_b", "lru_w_a", "lru_b_a", "lru_w_x", "lru_b_x", "lru_lambda", "w_out_odd",
           "w_mlp_up", "w_mlp_down"]
MATRICES = ("w_in_even", "w_out_even", "w_in_odd", "w_out_odd", "w_mlp_up", "w_mlp_down")
TRANSPOSED = ("w_in_even", "w_in_odd")
VECTORS = tuple(n for n in SHARDED if n not in MATRICES)
VEC_SLAB_ROWS = 16
REPL_SLAB_ROWS = 72
MATRIX_BLOCKS = (("w_in_even", 0), ("w_out_even", 0), ("w_in_odd", 0), ("w_out_odd", 0),
                 ("w_mlp_up", 0), ("w_mlp_up", 1), ("w_mlp_down", 0), ("w_mlp_down", 1))


def _rows_of(shape):
    n = 1
    for s in shape:
        n *= s
    return -(-n // SLAB_COLS), n


def _pack(arrays, total_rows, lead=()):
    parts, used = [], 0
    for a in arrays:
        rows, n = _rows_of(a.shape[len(lead):])
        flat = a.reshape(lead + (n,))
        flat = jnp.pad(flat, [(0, 0)] * len(lead) + [(0, rows * SLAB_COLS - n)])
        parts.append(flat.reshape(lead + (rows, SLAB_COLS)))
        used += rows
    parts.append(jnp.zeros(lead + (total_rows - used, SLAB_COLS), parts[0].dtype))
    return jnp.concatenate(parts, axis=len(lead))


def _unpack(slab, shapes, lead=()):
    out, row = [], 0
    for shape in shapes:
        rows, n = _rows_of(shape)
        seg = lax.slice_in_dim(slab, row, row + rows, axis=len(lead))
        out.append(seg.reshape(lead + (rows * SLAB_COLS,))[..., :n].reshape(lead + tuple(shape)))
        row += rows
    return out


def _join_shards(blocks, axis):
    moved = jnp.moveaxis(blocks, 0, axis)
    shape = moved.shape
    return moved.reshape(shape[:axis] + (shape[axis] * shape[axis + 1],) + shape[axis + 2:])


def _split_shards(full, axis):
    shape = full.shape
    cut = full.reshape(shape[:axis] + (N_DEV, shape[axis] // N_DEV) + shape[axis + 1:])
    return jnp.moveaxis(cut, axis, 0)


EVEN_SPLITS = (0, 256, 512, 1024, 1536, 1552, 2064, 2576, 3088, 3096)


def _even_in_split(wt):
    c = [wt[EVEN_SPLITS[k]:EVEN_SPLITS[k + 1]] for k in range(9)]
    gq, gk, gv, gr, ga, fq, fk, fv, ff = c
    padrows = lambda a: jnp.pad(a, ((0, LANES - a.shape[0]), (0, 0)))
    return jnp.concatenate([gq, gk, gv, fq, fk, fv], axis=0), jnp.concatenate([gr, padrows(ga), padrows(ff)], axis=0)


def _even_in_merge(dmm, dele):
    return jnp.concatenate([dmm[:1024], dele[:512], dele[512:512 + GLA_RANK], dmm[1024:2560],
                            dele[640:640 + ATT_HEADS]], axis=0)


def _forward_backward(x, target, shard, vec_slab, vec_shapes, w, place):
    w = dict(w)
    g, dnorm, sums, recv = {}, {}, {}, {}
    nrm = lambda l, k: w["norm_w"][l, k][None, :]
    gather = lambda *keys: _gather_plan([shard[k] for k in keys])
    blocks = lambda r, c: (N_DEV, r // N_DEV, c)

    def pair_sum(key):
        sums[key] = _pair_sum(g[key], got[key], place, f"rs_pair_sum_{key[0]}_{key[1]}")

    got = {}

    def mlp_fwd(xin, layer, ride_up, ride_down):
        h = _norm_fwd(xin, nrm(layer, 2), out_dtype=ACT_DTYPE, name=f"norm_mlp_{layer}")
        u = _mm(h, w["w_mlp_up"][layer], out_dtype=ACT_DTYPE, tm=TM_FWD, tn=D_FF // N_DEV, b_blocked=True,
                name=f"mlp_up_{layer}", ride=ride_up)
        u, rode_up = u if ride_up is not None else (u, None)
        if w["w_mlp_down"][layer] is None:
            w["w_mlp_down"][layer] = rode_up[0].reshape(D_FF, D_MODEL)
        yv = _mm(u, w["w_mlp_down"][layer], out_dtype=F32, tm=TM_DX, tn=TN, a_sqrelu=True,
                 name=f"mlp_down_{layer}", ride=ride_down)
        yv, rode_down = yv if ride_down is not None else (yv, None)
        xout = _norm_fwd(yv, nrm(layer, 3), out_dtype=F32, res=xin, name=f"norm_mlp_out_{layer}")
        return xout, (xin, h, u, yv), rode_up, rode_down

    def mlp_bwd(dxout, saved, layer, ride):
        xin, h, u, yv = saved
        k_up, k_down = ("w_mlp_up", layer), ("w_mlp_down", layer)
        dy, dnorm[(layer, 3)] = _norm_bwd(dxout, yv, nrm(layer, 3), out_dtype=ACT_DTYPE, name=f"norm_mlp_out_bwd_{layer}")
        du = _mm(dy, w["w_mlp_down"][layer], nt=True, out_dtype=ACT_DTYPE, tm=TM_DX, tn=TN, drelu_of=u,
                 name=f"mlp_down_dx_{layer}", ride=ride)
        rode = None
        if ride is not None:
            du, rode = du
        g[k_down] = _mm(u, dy, ta=True, out_dtype=WIRE_DTYPE, tm=TM_DW, tn=TN, a_sqrelu=True,
                        name=f"mlp_down_dw_{layer}").reshape(blocks(D_FF, D_MODEL))
        g[k_up] = _mm(h, du, ta=True, out_dtype=WIRE_DTYPE, tm=TM_DW, tn=D_FF // N_DEV, out_blocked=True,
                      name=f"mlp_up_dw_{layer}")
        w_up = jnp.moveaxis(w["w_mlp_up"][layer], 0, 1).reshape(D_MODEL, D_FF)
        dh, (got[k_down], got[k_up]) = _mm(du, w_up, nt=True, out_dtype=F32, tm=TM_DX, tn=TN, name=f"mlp_up_dx_{layer}",
                                           ride=_sibling_plan([g[k_down], g[k_up]]))
        pair_sum(k_down)
        pair_sum(k_up)
        dxin, dnorm[(layer, 2)] = _norm_bwd(dh, xin, nrm(layer, 2), out_dtype=F32, add=dxout, name=f"norm_mlp_bwd_{layer}")
        return dxin, rode

    first = _run_plan(_gather_plan([shard[("w_in_even", 0)], vec_slab]), "weights_all_gather_first")
    w["w_in_even"] = first[0].reshape(-1, D_MODEL)
    for n, b in zip(VECTORS, _unpack(first[1], vec_shapes, lead=(N_DEV,))):
        w[n] = _join_shards(b, SHARDED[n])
    w["w_mlp_up"], w["w_mlp_down"] = [None] * DEPTH, [None] * DEPTH

    wmm_e, wel_e = _even_in_split(w["w_in_even"])
    w_up_pad = jnp.pad(w["gla_w_a_up"][0], ((0, LANES - GLA_RANK), (0, 0)))
    b_f_pad = jnp.pad(w["fox_b_f"], ((0, 0), (0, LANES - ATT_HEADS)))
    h0 = _norm_fwd(x, nrm(0, 0), out_dtype=ACT_DTYPE, name="norm_in_0")
    pmm0, (w_out_even,) = _mm(h0, wmm_e, nt=True, out_dtype=ACT_DTYPE, tm=TM_FWD, tn=TN, name="in_even_mm",
                              ride=gather(("w_out_even", 0)))
    pel0 = _mm(h0, wel_e, nt=True, out_dtype=F32, tm=TM_FWD, tn=768, name="in_even_el")
    out_a, states = _gla_fwd(pmm0, pel0, w_up_pad, w["gla_b_a"], w["gla_norm_w"])
    cum, cum_t = _fox_gate_fwd(pel0, b_f_pad)
    (out_b, lse_b), (w["w_mlp_up"][0], w_mlp_down0) = _fox_fwd(pmm0, cum, cum_t,
                                                               ride=gather(("w_mlp_up", 0), ("w_mlp_down", 0)))
    w["w_out_even"] = w_out_even.reshape(D_MODEL, D_MODEL)
    w["w_mlp_down"][0] = w_mlp_down0.reshape(D_FF, D_MODEL)
    mix_in0 = jnp.concatenate([out_a, out_b], axis=1)
    mix0 = _mm(mix_in0, w["w_out_even"], out_dtype=F32, tm=TM_FWD, tn=TN, name="out_even")
    x1 = _norm_fwd(mix0, nrm(0, 1), out_dtype=F32, res=x, name="norm_mix_0")
    x2, mlp0, _, (w_in_odd,) = mlp_fwd(x1, 0, None, gather(("w_in_odd", 0)))
    w["w_in_odd"] = w_in_odd.reshape(-1, D_MODEL)

    w_in_o = w["w_in_odd"]
    n_mm_o = 3 * GROUP_WIDTH
    wa_bd, wx_bd = _block_diag_pairs(w["lru_w_a"][0]), _block_diag_pairs(w["lru_w_x"][0])
    base = _ca_bias_base(w["rel_bias"][0])
    h1 = _norm_fwd(x2, nrm(1, 0), out_dtype=ACT_DTYPE, name="norm_in_1")
    pmm1 = _mm(h1, w_in_o[:n_mm_o], nt=True, out_dtype=ACT_DTYPE, tm=TM_FWD, tn=TN, name="in_odd_mm")
    pel1 = _mm(h1, w_in_o[n_mm_o:], nt=True, out_dtype=F32, tm=TM_FWD, tn=TN, name="in_odd_el")
    kp = jnp.pad(pmm1[:, GROUP_WIDTH:2 * GROUP_WIDTH], ((CA_LEFT, 0), (0, 0)))
    vp = jnp.pad(pmm1[:, 2 * GROUP_WIDTH:], ((CA_LEFT, 0), (0, 0)))
    (out_c, lse_c), (w["w_mlp_up"][1],) = _ca_fwd(pmm1, kp, vp, base, ride=gather(("w_mlp_up", 1)))
    lru_args = (pel1, w["conv_w"][0], w["conv_b"], wa_bd, w["lru_b_a"], wx_bd, w["lru_b_x"], w["lru_lambda"])
    out_d, (w_out_odd,) = _lru_fwd(*lru_args, ride=gather(("w_out_odd", 0)))
    w["w_out_odd"] = w_out_odd.reshape(D_MODEL, D_MODEL)
    mix_in1 = jnp.concatenate([out_c, out_d], axis=1)
    mix1 = _mm(mix_in1, w["w_out_odd"], out_dtype=F32, tm=TM_FWD, tn=TN, name="out_odd")
    x3 = _norm_fwd(mix1, nrm(1, 1), out_dtype=F32, res=x2, name="norm_mix_1")
    x4, mlp1, _, _ = mlp_fwd(x3, 1, gather(("w_mlp_down", 1)), None)

    loss, dx4 = _loss_fwd_bwd(x4, target)

    k_oo, k_io, k_oe, k_ie = ("w_out_odd", 0), ("w_in_odd", 0), ("w_out_even", 0), ("w_in_even", 0)
    mlp_keys = lambda l: [("w_mlp_down", l), ("w_mlp_up", l)]
    dx3, _ = mlp_bwd(dx4, mlp1, 1, None)
    dmix1, dnorm[(1, 1)] = _norm_bwd(dx3, mix1, nrm(1, 1), out_dtype=ACT_DTYPE, name="norm_mix_bwd_1")
    g[k_oo] = _mm(mix_in1, dmix1, ta=True, out_dtype=WIRE_DTYPE, tm=TM_DW, tn=TN, name="out_odd_dw").reshape(
        blocks(D_MODEL, D_MODEL))
    dmix_in1, (got[k_oo],) = _mm(dmix1, w["w_out_odd"], nt=True, out_dtype=F32, tm=TM_DX, tn=TN, name="out_odd_dx",
                                 ride=_sibling_plan([g[k_oo]]))
    (dq_c, dkp, dvp, dbase), rode = _ca_bwd(pmm1, kp, vp, base, lse_c, dmix_in1,
                                            ride=_chip_plan([sums[k] for k in mlp_keys(1)]))
    recv.update(zip(mlp_keys(1), rode))
    pair_sum(k_oo)
    (dgate, dxin, g_conv_w, g_conv_b, dwa_bd, g_lru_b_a, dwx_bd, g_lru_b_x, g_lru_lambda), (recv[k_oo],) = _lru_bwd(
        *lru_args, dmix_in1, ride=_chip_plan([sums[k_oo]]))
    dp1 = jnp.concatenate([dq_c, dkp[CA_LEFT:].astype(ACT_DTYPE), dvp[CA_LEFT:].astype(ACT_DTYPE), dgate, dxin], axis=1)
    g[k_io] = _mm(dp1, h1, ta=True, out_dtype=WIRE_DTYPE, tm=dp1.shape[1] // 2, tn=TN, name="in_odd_dw").reshape(
        blocks(dp1.shape[1], D_MODEL))
    dh1, (got[k_io],) = _mm(dp1, w_in_o, out_dtype=F32, tm=TM_DX, tn=TN, name="in_odd_dx",
                            ride=_sibling_plan([g[k_io]]))
    pair_sum(k_io)
    dx2, dnorm[(1, 0)] = _norm_bwd(dh1, x2, nrm(1, 0), out_dtype=F32, add=dx3, name="norm_in_bwd_1")
    g["rel_bias"] = _ca_bias_base_grad(dbase)[None]
    g["conv_w"], g["conv_b"] = g_conv_w[None], g_conv_b
    g["lru_w_a"], g["lru_w_x"] = _block_diag_pairs_grad(dwa_bd)[None], _block_diag_pairs_grad(dwx_bd)[None]
    g["lru_b_a"], g["lru_b_x"], g["lru_lambda"] = g_lru_b_a, g_lru_b_x, g_lru_lambda

    dx1, (recv[k_io],) = mlp_bwd(dx2, mlp0, 0, _chip_plan([sums[k_io]]))
    dmix0, dnorm[(0, 1)] = _norm_bwd(dx1, mix0, nrm(0, 1), out_dtype=ACT_DTYPE, name="norm_mix_bwd_0")
    g[k_oe] = _mm(mix_in0, dmix0, ta=True, out_dtype=WIRE_DTYPE, tm=TM_DW, tn=TN, name="out_even_dw").reshape(
        blocks(D_MODEL, D_MODEL))
    dmix_in0, (got[k_oe],) = _mm(dmix0, w["w_out_even"], nt=True, out_dtype=F32, tm=TM_DX, tn=TN, name="out_even_dx",
                                 ride=_sibling_plan([g[k_oe]]))
    k_md0, k_mu0 = mlp_keys(0)
    pair_sum(k_oe)
    (dq_a, dk_a, dv_a, dr_a, da_a, dw_up_pad, g_gla_b_a, g_gla_norm_w), (recv[k_md0],) = _gla_bwd(
        pmm0, pel0, w_up_pad, w["gla_b_a"], w["gla_norm_w"], states, dmix_in0, ride=_chip_plan([sums[k_md0]]))
    (dq_b, dk_b, dv_b, dcum_t, dcum_q), (recv[k_mu0], recv[k_oe]) = _fox_bwd(
        pmm0, cum, cum_t, lse_b, dmix_in0, ride=_chip_plan([sums[k_mu0], sums[k_oe]]))
    df_b, db_f = _fox_gate_bwd(pel0, b_f_pad, dcum_t, dcum_q)
    g["gla_w_a_up"] = dw_up_pad[:GLA_RANK][None]
    g["gla_b_a"], g["gla_norm_w"], g["fox_b_f"] = g_gla_b_a, g_gla_norm_w, db_f[:, :ATT_HEADS]
    dp0 = jnp.concatenate([dq_a, dk_a, dv_a, dq_b, dk_b.astype(ACT_DTYPE), dv_b.astype(ACT_DTYPE), dr_a, da_a, df_b],
                          axis=1)
    w_perm = jnp.concatenate([wmm_e, wel_e], axis=0)
    n_mm_e = wmm_e.shape[0]
    dw_perm, (repl_parts,) = _mm(dp0, h0, ta=True, out_dtype=WIRE_DTYPE, tm=dp0.shape[1] // 2, tn=TN, name="in_even_dw",
                                 ride=_gather_plan([_pack([g[n] for n in REPLICATED], REPL_SLAB_ROWS)]))
    dw_even = _even_in_merge(dw_perm[:n_mm_e], dw_perm[n_mm_e:])
    g[k_ie] = dw_even.reshape(blocks(dw_even.shape[0], D_MODEL))
    dh0, (got[k_ie],) = _mm(dp0, w_perm, out_dtype=F32, tm=TM_DX, tn=TN, name="in_even_dx",
                            ride=_sibling_plan([g[k_ie]]))
    pair_sum(k_ie)
    (dx0, dnorm[(0, 0)]), (recv[k_ie],) = _norm_bwd(dh0, x, nrm(0, 0), out_dtype=F32, add=dx1, name="norm_in_bwd_0",
                                                     ride=_chip_plan([sums[k_ie]]))

    g["norm_w"] = jnp.stack([jnp.concatenate([dnorm[(l, k)] for k in range(4)], axis=0) for l in range(DEPTH)])
    k_vec = ("vectors", 0)
    vec_grads = _pack([_split_shards(g[n], SHARDED[n]) for n in VECTORS], VEC_SLAB_ROWS, lead=(N_DEV,))
    g[k_vec] = vec_grads.astype(WIRE_DTYPE)
    (got[k_vec],) = _run_plan(_sibling_plan([g[k_vec]]), "rs_sibling_exchange_last")
    pair_sum(k_vec)
    (recv[k_vec],) = _run_plan(_chip_plan([sums[k_vec]]), "rs_chip_exchange_last")
    return loss, dx0, sums, recv, repl_parts


def kernel(x, norm_w, w_in_even, gla_w_a_up, gla_b_a, gla_norm_w, fox_b_f, w_out_even, w_in_odd, rel_bias, conv_w, conv_b, lru_w_a, lru_b_a, lru_w_x, lru_b_x, lru_lambda, w_out_odd, w_mlp_up, w_mlp_down, loss_target, m_norm_w, m_w_in_even, m_gla_w_a_up, m_gla_b_a, m_gla_norm_w, m_fox_b_f, m_w_out_even, m_w_in_odd, m_rel_bias, m_conv_w, m_conv_b, m_lru_w_a, m_lru_b_a, m_lru_w_x, m_lru_b_x, m_lru_lambda, m_w_out_odd, m_w_mlp_up, m_w_mlp_down, v_norm_w, v_w_in_even, v_gla_w_a_up, v_gla_b_a, v_gla_norm_w, v_fox_b_f, v_w_out_even, v_w_in_odd, v_rel_bias, v_conv_w, v_conv_b, v_lru_w_a, v_lru_b_a, v_lru_w_x, v_lru_b_x, v_lru_lambda, v_w_out_odd, v_w_mlp_up, v_w_mlp_down):
    wts = dict(zip(WEIGHTS, (norm_w, w_in_even, gla_w_a_up, gla_b_a, gla_norm_w, fox_b_f, w_out_even, w_in_odd, rel_bias,
                             conv_w, conv_b, lru_w_a, lru_b_a, lru_w_x, lru_b_x, lru_lambda, w_out_odd, w_mlp_up,
                             w_mlp_down)))
    mom = dict(zip(WEIGHTS, (m_norm_w, m_w_in_even, m_gla_w_a_up, m_gla_b_a, m_gla_norm_w, m_fox_b_f, m_w_out_even,
                             m_w_in_odd, m_rel_bias, m_conv_w, m_conv_b, m_lru_w_a, m_lru_b_a, m_lru_w_x, m_lru_b_x,
                             m_lru_lambda, m_w_out_odd, m_w_mlp_up, m_w_mlp_down)))
    var = dict(zip(WEIGHTS, (v_norm_w, v_w_in_even, v_gla_w_a_up, v_gla_b_a, v_gla_norm_w, v_fox_b_f, v_w_out_even,
                             v_w_in_odd, v_rel_bias, v_conv_w, v_conv_b, v_lru_w_a, v_lru_b_a, v_lru_w_x, v_lru_b_x,
                             v_lru_lambda, v_w_out_odd, v_w_mlp_up, v_w_mlp_down)))
    vec_shapes = [wts[n].shape for n in VECTORS]
    repl_shapes = [wts[n].shape for n in REPLICATED]
    place = jnp.stack([lax.axis_index("c"), 2 * lax.axis_index("x") + lax.axis_index("y")]).astype(jnp.int32)

    shard = {(n, l): (wts[n][l].T if n in TRANSPOSED else wts[n][l]).astype(WIRE_DTYPE) for n, l in MATRIX_BLOCKS}
    vec_slab = _pack([wts[n] for n in VECTORS], VEC_SLAB_ROWS)
    loss_blk, dx, sums, recv, repl_parts = _forward_backward(
        x[0], loss_target[0], shard, vec_slab, vec_shapes, {n: wts[n] for n in REPLICATED}, place)
    loss = lax.psum(loss_blk[0, 0], ("x", "y", "c"))

    k_vec = ("vectors", 0)

    vec_of = lambda d: _pack([d[n] for n in VECTORS], VEC_SLAB_ROWS)[None]
    view = lambda n, a: jnp.swapaxes(a, 1, 2) if n in TRANSPOSED else a
    upd = {n: [view(n, o) for o in _adamw_sharded(
        [(sums[(n, l)], recv[(n, l)]) for l in range(wts[n].shape[0])], view(n, wts[n]), view(n, mom[n]), view(n, var[n]),
        place, f"adamw_{n}")] for n in MATRICES}
    vec_upd = _adamw_sharded([(sums[k_vec], recv[k_vec])], vec_of(wts), vec_of(mom), vec_of(var), place, "adamw_vectors")
    rp = _adamw_replicated(repl_parts, _pack([wts[n] for n in REPLICATED], REPL_SLAB_ROWS),
                           _pack([mom[n] for n in REPLICATED], REPL_SLAB_ROWS),
                           _pack([var[n] for n in REPLICATED], REPL_SLAB_ROWS), "adamw_replicated")
    outs = []
    for kind in range(4):
        vals = dict(zip(VECTORS, _unpack(vec_upd[kind][0], vec_shapes)))
        vals.update(zip(REPLICATED, _unpack(rp[kind], repl_shapes)))
        vals.update((n, upd[n][kind]) for n in MATRICES)
        outs += [vals[n] for n in WEIGHTS]
    return (loss, dx[None], *outs)
```

```python
import functools
from typing import Callable, NamedTuple

import jax
import jax.numpy as jnp
from jax import lax
from jax.experimental import pallas as pl
from jax.experimental.pallas import tpu as pltpu

F32 = jnp.float32
MXU_DTYPE = jnp.bfloat16
ACT_DTYPE = jnp.bfloat16
WIRE_DTYPE = jnp.bfloat16

V7X_VMEM_BYTES = 64 * 1024 * 1024
VMEM_LIMIT = (V7X_VMEM_BYTES * 7) // 8
LANES = 128

D_MODEL = 1024
SEQ = 2048
DEPTH = 2
CHUNK = 64
GROUP_WIDTH = D_MODEL // 2
D_FF = 4 * D_MODEL
NORM_EPS = 1e-6
GLA_HEADS = 4
GLA_DV = GROUP_WIDTH // GLA_HEADS
GLA_DK = GLA_DV // 2
GLA_KW = GLA_HEADS * GLA_DK
GLA_RANK = 16
GLA_GATE_TAU = 16.0
HEAD_DIM = 64
ATT_HEADS = GROUP_WIDTH // HEAD_DIM
CA_LEFT = 8 * CHUNK
REL_CLIP = 128
LRU_BLOCK_DIM = 64
CONV_WIDTH = 4
LRU_C = 8.0
N_DEV = 8

ADAM_LR = 0.001
ADAM_B1 = 0.9
ADAM_B2 = 0.999
ADAM_EPS = 1e-08
ADAM_WD = 0.01
ADAM_STEP = 10

NEG = float(jnp.finfo(jnp.float32).min)
MESH = pl.DeviceIdType.MESH


def _params(*sem):
    return pltpu.CompilerParams(dimension_semantics=sem, vmem_limit_bytes=VMEM_LIMIT)


def _dot(a, b, ca=1, cb=0):
    return lax.dot_general(a.astype(MXU_DTYPE), b.astype(MXU_DTYPE), (((ca,), (cb,)), ((), ())),
                           preferred_element_type=F32)


def _dot_exact(a, b):
    return lax.dot_general(a, b, (((1,), (0,)), ((), ())), precision=lax.Precision.HIGHEST,
                           preferred_element_type=F32)


def _log_sigmoid(x):
    return jnp.minimum(x, 0.0) - jnp.log1p(jnp.exp(-jnp.abs(x)))


def _iota(shape, axis):
    return lax.broadcasted_iota(jnp.int32, shape, axis)


ANY = pl.BlockSpec(memory_space=pl.ANY)
N_CHIPS = 4


class _Plan(NamedTuple):
    ins: list
    outs: list
    sems: list
    start: Callable
    finish: Callable


def _place():
    x, y, c = lax.axis_index("x"), lax.axis_index("y"), lax.axis_index("c")
    return x, y, c, [(1 - x, y), (x, 1 - y), (1 - x, 1 - y)]


def _gather_plan(xs):
    n = len(xs)

    def parts(x_refs, out_refs, sems):
        send_sems, recv_sems, local_sems = sems
        x, y, c, chips = _place()
        me, sibling = (x, y, c), (x, y, 1 - c)

        def rows(a, px, py, pc):
            return out_refs[a].at[4 * px + 2 * py + pc]

        def copy(a, k, block, to, src=None):
            return pltpu.make_async_remote_copy(
                src_ref=rows(a, *block) if src is None else src, dst_ref=rows(a, *block),
                send_sem=send_sems.at[7 * a + k], recv_sem=recv_sems.at[7 * a + k], device_id=to, device_id_type=MESH)

        mine = [pltpu.make_async_copy(x_refs[a], rows(a, *me), local_sems.at[a]) for a in range(n)]
        first = []
        for a in range(n):
            first.append(copy(a, 0, me, sibling, src=x_refs[a]))
            first += [copy(a, 1 + j, me, (*chip, c), src=x_refs[a]) for j, chip in enumerate(chips)]
        return c, me, sibling, chips, copy, mine, first

    def start(x_refs, out_refs, sems):
        *_, mine, first = parts(x_refs, out_refs, sems)
        for cp in first + mine:
            cp.start()

    def finish(x_refs, out_refs, sems):
        c, me, sibling, chips, copy, mine, first = parts(x_refs, out_refs, sems)
        passed = []
        for j, chip in enumerate(chips):
            for a in range(n):
                copy(a, 1 + j, (*chip, c), me).wait_recv()
                passed.append(copy(a, 4 + j, (*chip, c), sibling))
                passed[-1].start()
        for a in range(n):
            copy(a, 0, sibling, me).wait_recv()
            for j, chip in enumerate(chips):
                copy(a, 4 + j, (*chip, 1 - c), me).wait_recv()
        for cp in first + passed:
            cp.wait_send()
        for cp in mine:
            cp.wait()

    return _Plan(list(xs), [jax.ShapeDtypeStruct((N_DEV,) + x.shape, x.dtype) for x in xs],
                 [pltpu.SemaphoreType.DMA((7 * n,)), pltpu.SemaphoreType.DMA((7 * n,)), pltpu.SemaphoreType.DMA((n,))],
                 start, finish)


def _exchange_plan(copies_of, ins, outs, per_array):
    n = len(ins)

    def start(in_refs, out_refs, sems):
        for cp in copies_of(in_refs, out_refs, sems):
            cp.start()

    def finish(in_refs, out_refs, sems):
        copies = copies_of(in_refs, out_refs, sems)
        for cp in copies:
            cp.wait_recv()
        for cp in copies:
            cp.wait_send()

    return _Plan(list(ins), outs, [pltpu.SemaphoreType.DMA((per_array * n,)), pltpu.SemaphoreType.DMA((per_array * n,))],
                 start, finish)


def _sibling_plan(gs):
    def copies_of(g_refs, got_refs, sems):
        x, y, c, _ = _place()
        return [pltpu.make_async_remote_copy(
            src_ref=g_refs[a].at[2 * k + (1 - c)], dst_ref=got_refs[a].at[k], send_sem=sems[0].at[N_CHIPS * a + k],
            recv_sem=sems[1].at[N_CHIPS * a + k], device_id=(x, y, 1 - c), device_id_type=MESH)
            for a in range(len(gs)) for k in range(N_CHIPS)]

    return _exchange_plan(copies_of, gs, [jax.ShapeDtypeStruct((N_CHIPS,) + g.shape[1:], g.dtype) for g in gs], N_CHIPS)


def _chip_plan(ss):
    def copies_of(s_refs, out_refs, sems):
        x, y, c, chips = _place()
        return [pltpu.make_async_remote_copy(
            src_ref=s_refs[a].at[2 * px + py], dst_ref=out_refs[a].at[j], send_sem=sems[0].at[3 * a + j],
            recv_sem=sems[1].at[3 * a + j], device_id=(px, py, c), device_id_type=MESH)
            for a in range(len(ss)) for j, (px, py) in enumerate(chips)]

    return _exchange_plan(copies_of, ss, [jax.ShapeDtypeStruct((3,) + s.shape[1:], s.dtype) for s in ss], 3)


def _run_plan(plan, name):
    n_in, n_out = len(plan.ins), len(plan.outs)

    def body(*refs):
        args = refs[:n_in], refs[n_in:n_in + n_out], refs[n_in + n_out:]
        plan.start(*args)
        plan.finish(*args)

    return pl.pallas_call(body, out_shape=plan.outs, in_specs=[ANY] * n_in, out_specs=[ANY] * n_out,
                          scratch_shapes=plan.sems, name=name)(*plan.ins)


def _pcall(body, ride, *, grid, in_specs, out_specs, out_shape, scratch_shapes=(), semantics, name):
    if ride is None:
        return pl.pallas_call(body, grid=grid, in_specs=in_specs, out_specs=out_specs, out_shape=out_shape,
                              scratch_shapes=list(scratch_shapes), compiler_params=_params(*semantics), name=name)
    single = not isinstance(out_shape, (list, tuple))
    out_specs_l, out_shape_l = ([out_specs], [out_shape]) if single else (list(out_specs), list(out_shape))
    n_in, n_out, n_scr = len(in_specs), len(out_shape_l), len(scratch_shapes)
    r_in, r_out = len(ride.ins), len(ride.outs)

    def riding(*refs):
        cuts = [n_in, r_in, n_out, r_out, n_scr]
        groups, at = [], 0
        for width in cuts:
            groups.append(refs[at:at + width])
            at += width
        ins, r_ins, outs, r_outs, scr = groups
        sems = refs[at:]
        first = functools.reduce(jnp.logical_and, [pl.program_id(d) == 0 for d in range(len(grid))])
        last = functools.reduce(jnp.logical_and, [pl.program_id(d) == grid[d] - 1 for d in range(len(grid))])

        @pl.when(first)
        def _():
            ride.start(r_ins, r_outs, sems)

        body(*ins, *outs, *scr)

        @pl.when(last)
        def _():
            ride.finish(r_ins, r_outs, sems)

    call = pl.pallas_call(
        riding, grid=grid, in_specs=list(in_specs) + [ANY] * r_in, out_specs=out_specs_l + [ANY] * r_out,
        out_shape=out_shape_l + list(ride.outs), scratch_shapes=list(scratch_shapes) + list(ride.sems),
        compiler_params=_params(*(["arbitrary"] * len(grid))), name=name)

    def run(*args):
        res = call(*args, *ride.ins)
        return (res[0] if single else list(res[:n_out])), list(res[n_out:])

    return run


def _mm(a, b, *, nt=False, ta=False, out_dtype, tm, tn, a_sqrelu=False, drelu_of=None, b_blocked=False,
        out_blocked=False, name, ride=None):
    k, m = a.shape if ta else a.shape[::-1]
    if b_blocked:
        assert not nt and b.shape[1] == k and b.shape[2] == tn
        n = b.shape[0] * tn
    else:
        n = b.shape[0] if nt else b.shape[1]
        assert (b.shape[1] if nt else b.shape[0]) == k
    tm, tn = min(tm, m), min(tn, n)
    assert m % tm == 0 and n % tn == 0

    def body(*refs):
        a_ref, b_ref = refs[0], refs[1]
        o_ref = refs[-1]
        av = a_ref[...]
        if a_sqrelu:
            av = jnp.square(jnp.maximum(av.astype(F32), 0.0))
        acc = _dot(av, b_ref[...], 0 if ta else 1, 1 if nt else 0)
        if drelu_of is not None:
            acc = acc * (2.0 * jnp.maximum(refs[2][...].astype(F32), 0.0))
        o_ref[...] = acc.astype(out_dtype)

    if b_blocked:
        b_spec = pl.BlockSpec((None, k, tn), lambda i, j: (j, 0, 0))
    elif nt:
        b_spec = pl.BlockSpec((tn, k), lambda i, j: (j, 0))
    else:
        b_spec = pl.BlockSpec((k, tn), lambda i, j: (0, j))
    a_spec = pl.BlockSpec((k, tm), lambda i, j: (0, i)) if ta else pl.BlockSpec((tm, k), lambda i, j: (i, 0))
    in_specs = [a_spec, b_spec]
    args = [a, b]
    if drelu_of is not None:
        in_specs.append(pl.BlockSpec((tm, tn), lambda i, j: (i, j)))
        args.append(drelu_of)
    if out_blocked:
        out_spec = pl.BlockSpec((None, tm, tn), lambda i, j: (j, i, 0))
        out_shape = jax.ShapeDtypeStruct((n // tn, m, tn), out_dtype)
    else:
        out_spec = pl.BlockSpec((tm, tn), lambda i, j: (i, j))
        out_shape = jax.ShapeDtypeStruct((m, n), out_dtype)
    return _pcall(body, ride, grid=(m // tm, n // tn), in_specs=in_specs, out_specs=out_spec, out_shape=out_shape,
                  semantics=("parallel", "parallel"), name=name)(*args)


def _mm_nt_blocked(a, b, *, out_dtype, tm, tn, name):
    m = a.shape[0]
    p, n, kp = b.shape
    assert a.shape[1] == p * kp and m % tm == 0 and n % tn == 0

    def body(a_ref, b_ref, o_ref, acc_ref):
        @pl.when(pl.program_id(2) == 0)
        def _():
            acc_ref[...] = jnp.zeros_like(acc_ref)

        acc_ref[...] += _dot(a_ref[...], b_ref[...], 1, 1)

        @pl.when(pl.program_id(2) == p - 1)
        def _():
            o_ref[...] = acc_ref[...].astype(out_dtype)

    return pl.pallas_call(
        body, grid=(m // tm, n // tn, p),
        in_specs=[pl.BlockSpec((tm, kp), lambda i, j, q: (i, q)), pl.BlockSpec((None, tn, kp), lambda i, j, q: (q, j, 0))],
        out_specs=pl.BlockSpec((tm, tn), lambda i, j, q: (i, j)),
        out_shape=jax.ShapeDtypeStruct((m, n), out_dtype),
        scratch_shapes=[pltpu.VMEM((tm, tn), F32)],
        compiler_params=_params("parallel", "parallel", "arbitrary"), name=name)(a, b)


ROW_TILE = 512
TM_FWD, TM_DX, TM_DW, TN = 2048, 1024, 1024, 512


def _norm_fwd(x, w, *, out_dtype, res=None, name):
    t, d = x.shape

    def body(*refs):
        x_ref, w_ref, o_ref = refs[0], refs[1], refs[-1]
        xv = x_ref[...]
        y = xv * lax.rsqrt(jnp.mean(xv * xv, axis=-1, keepdims=True) + NORM_EPS) * w_ref[...]
        if res is not None:
            y = refs[2][...] + y
        o_ref[...] = y.astype(out_dtype)

    row = pl.BlockSpec((ROW_TILE, d), lambda i: (i, 0))
    in_specs = [row, pl.BlockSpec((1, d), lambda i: (0, 0))] + ([row] if res is not None else [])
    args = [x, w] + ([res] if res is not None else [])
    return pl.pallas_call(body, grid=(t // ROW_TILE,), in_specs=in_specs, out_specs=row,
                          out_shape=jax.ShapeDtypeStruct((t, d), out_dtype),
                          compiler_params=_params("parallel"), name=name)(*args)


def _norm_bwd(dy, x, w, *, out_dtype, add=None, name, ride=None):
    t, d = x.shape

    def body(*refs):
        dy_ref, x_ref, w_ref = refs[0], refs[1], refs[2]
        dx_ref, dw_ref = refs[-2], refs[-1]
        xv = x_ref[...]
        rstd = lax.rsqrt(jnp.mean(xv * xv, axis=-1, keepdims=True) + NORM_EPS)
        xhat = xv * rstd
        dyv = dy_ref[...].astype(F32)
        g = dyv * w_ref[...]
        dx = rstd * (g - xhat * jnp.mean(g * xhat, axis=-1, keepdims=True))
        if add is not None:
            dx = dx + refs[3][...]
        dx_ref[...] = dx.astype(out_dtype)

        @pl.when(pl.program_id(0) == 0)
        def _():
            dw_ref[...] = jnp.zeros_like(dw_ref)

        dw_ref[...] += jnp.sum(dyv * xhat, axis=0, keepdims=True)

    row = pl.BlockSpec((ROW_TILE, d), lambda i: (i, 0))
    vec = pl.BlockSpec((1, d), lambda i: (0, 0))
    in_specs = [row, row, vec] + ([row] if add is not None else [])
    args = [dy, x, w] + ([add] if add is not None else [])
    return _pcall(body, ride, grid=(t // ROW_TILE,), in_specs=in_specs, out_specs=[row, vec],
                  out_shape=[jax.ShapeDtypeStruct((t, d), out_dtype), jax.ShapeDtypeStruct((1, d), F32)],
                  semantics=("arbitrary",), name=name)(*args)


def _loss_fwd_bwd(y, target):
    t, d = y.shape

    def body(y_ref, t_ref, l_ref, dy_ref):
        diff = y_ref[...] - t_ref[...]
        dy_ref[...] = diff * (1.0 / d)

        @pl.when(pl.program_id(0) == 0)
        def _():
            l_ref[...] = jnp.zeros_like(l_ref)

        l_ref[...] += 0.5 * jnp.sum(jnp.mean(diff * diff, axis=-1, keepdims=True), axis=0, keepdims=True)

    row = pl.BlockSpec((ROW_TILE, d), lambda i: (i, 0))
    return pl.pallas_call(body, grid=(t // ROW_TILE,), in_specs=[row, row],
                          out_specs=[pl.BlockSpec((8, LANES), lambda i: (0, 0)), row],
                          out_shape=[jax.ShapeDtypeStruct((8, LANES), F32), jax.ShapeDtypeStruct((t, d), F32)],
                          compiler_params=_params("arbitrary"), name="loss")(y, target)


GLA_UNROLL = (8, 4)


def _gla_specs(t):
    return [pl.BlockSpec((t, GLA_KW), lambda i: (0, 0)),
            pl.BlockSpec((t, GLA_KW), lambda i: (0, 1)),
            pl.BlockSpec((t, GROUP_WIDTH), lambda i: (0, 1)),
            pl.BlockSpec((t, GROUP_WIDTH), lambda i: (0, 0)),
            pl.BlockSpec((t, LANES), lambda i: (0, 4)),
            pl.BlockSpec((LANES, GLA_KW), lambda i: (0, 0)),
            pl.BlockSpec((1, GLA_KW), lambda i: (0, 0)),
            pl.BlockSpec((1, GROUP_WIDTH), lambda i: (0, 0))]


def _gla_fwd(pmm, pel, w_up, b_a, gnorm_w, ride=None):
    t = pmm.shape[0]
    nc = t // CHUNK
    scale = GLA_DK ** -0.5

    def body(q_ref, k_ref, v_ref, r_ref, a_ref, wup_ref, ba_ref, gw_ref, o_ref, st_ref, la_scr, s_scr):
        z = _dot(a_ref[...], wup_ref[...]) + ba_ref[...]
        la_scr[...] = _log_sigmoid(z) * (1.0 / GLA_GATE_TAU)
        s_scr[...] = jnp.zeros_like(s_scr)
        tri = (_iota((CHUNK, CHUNK), 1) <= _iota((CHUNK, CHUNK), 0)).astype(F32)

        def chunk(c, carry):
            rows = pl.ds(pl.multiple_of(c * CHUNK, CHUNK), CHUNK)
            cum = _dot_exact(tri, la_scr[rows, :])
            tot = cum[CHUNK - 1:CHUNK, :]
            kd = k_ref[rows, :].astype(F32) * jnp.exp(tot - cum)
            decay = jnp.exp(tot)
            qs = q_ref[rows, :].astype(F32) * scale
            vv = v_ref[rows, :].astype(F32)
            rr = r_ref[rows, :]
            gate = rr * jax.nn.sigmoid(rr) * gw_ref[...]
            for h in range(GLA_HEADS):
                ks = slice(h * GLA_DK, (h + 1) * GLA_DK)
                vs = slice(h * GLA_DV, (h + 1) * GLA_DV)
                inc_t = _dot(vv[:, vs].T, kd[:, ks])
                s_t = s_scr[vs, :] * decay[:, ks] + inc_t
                s_scr[vs, :] = s_t
                st_ref[c, vs, :] = s_t
                o = _dot(qs[:, ks], s_t, 1, 1)
                y = o * lax.rsqrt(jnp.mean(o * o, axis=-1, keepdims=True) + NORM_EPS)
                o_ref[rows, vs] = (y * gate[:, vs]).astype(o_ref.dtype)
            return carry

        lax.fori_loop(0, nc, chunk, 0, unroll=GLA_UNROLL[0])

    return _pcall(
        body, ride, grid=(1,), in_specs=_gla_specs(t),
        out_specs=[pl.BlockSpec((t, GROUP_WIDTH), lambda i: (0, 0)),
                   pl.BlockSpec((nc, GLA_HEADS * GLA_DV, GLA_DK), lambda i: (0, 0, 0))],
        out_shape=[jax.ShapeDtypeStruct((t, GROUP_WIDTH), ACT_DTYPE),
                   jax.ShapeDtypeStruct((nc, GLA_HEADS * GLA_DV, GLA_DK), F32)],
        scratch_shapes=[pltpu.VMEM((t, GLA_KW), F32), pltpu.VMEM((GLA_HEADS * GLA_DV, GLA_DK), F32)],
        semantics=("arbitrary",), name="gla_fwd")(pmm, pmm, pmm, pel, pel, w_up, b_a, gnorm_w)


def _gla_bwd(pmm, pel, w_up, b_a, gnorm_w, states, dmix, ride=None):
    t = pmm.shape[0]
    nc = t // CHUNK
    scale = GLA_DK ** -0.5

    def body(q_ref, k_ref, v_ref, r_ref, a_ref, wup_ref, ba_ref, gw_ref, st_ref, do_ref,
             dq_ref, dk_ref, dv_ref, dr_ref, da_ref, dwup_ref, dba_ref, dgw_ref, la_scr, dz_scr, ds_scr):
        z = _dot(a_ref[...], wup_ref[...]) + ba_ref[...]
        la_scr[...] = _log_sigmoid(z) * (1.0 / GLA_GATE_TAU)
        ds_scr[...] = jnp.zeros_like(ds_scr)
        dgw_ref[...] = jnp.zeros_like(dgw_ref)
        row_i, col_i = _iota((CHUNK, CHUNK), 0), _iota((CHUNK, CHUNK), 1)
        tri = (col_i <= row_i).astype(F32)
        tri_strict = (col_i < row_i).astype(F32)

        def chunk(n, carry):
            c = nc - 1 - n
            rows = pl.ds(pl.multiple_of(c * CHUNK, CHUNK), CHUNK)
            cum = _dot_exact(tri, la_scr[rows, :])
            tot = cum[CHUNK - 1:CHUNK, :]
            e = jnp.exp(tot - cum)
            kd = k_ref[rows, :].astype(F32) * e
            decay = jnp.exp(tot)
            qs = q_ref[rows, :].astype(F32) * scale
            vv = v_ref[rows, :].astype(F32)
            rr = r_ref[rows, :]
            sig = jax.nn.sigmoid(rr)
            silu = rr * sig
            dsilu = sig * (1.0 + rr * (1.0 - sig))
            dout = do_ref[rows, :]
            gw = gw_ref[...]
            c_prev = jnp.maximum(c - 1, 0)
            has_prev = (c > 0).astype(F32)
            zc = _dot(a_ref[rows, :], wup_ref[...]) + ba_ref[...]
            dz_scale = jax.nn.sigmoid(-zc) * (1.0 / GLA_GATE_TAU)
            for h in range(GLA_HEADS):
                ks = slice(h * GLA_DK, (h + 1) * GLA_DK)
                vs = slice(h * GLA_DV, (h + 1) * GLA_DV)
                s_t = st_ref[c, vs, :]
                s_prev = st_ref[c_prev, vs, :] * has_prev
                o = _dot(qs[:, ks], s_t, 1, 1)
                rstd = lax.rsqrt(jnp.mean(o * o, axis=-1, keepdims=True) + NORM_EPS)
                y = o * rstd
                dg = dout[:, vs]
                dgw_ref[:, vs] += jnp.sum(dg * y * silu[:, vs], axis=0, keepdims=True)
                dr_ref[rows, vs] = (dg * y * gw[:, vs] * dsilu[:, vs]).astype(dr_ref.dtype)
                dy = dg * gw[:, vs] * silu[:, vs]
                d_o = rstd * (dy - y * jnp.mean(dy * y, axis=-1, keepdims=True))
                dq_ref[rows, ks] = (_dot(d_o, s_t) * scale).astype(dq_ref.dtype)
                ds_t = ds_scr[vs, :] + _dot(d_o.T, qs[:, ks])
                dv_ref[rows, vs] = _dot(kd[:, ks], ds_t, 1, 1).astype(dv_ref.dtype)
                dkd = _dot(vv[:, vs], ds_t)
                ddecay = jnp.sum(ds_t * s_prev, axis=0, keepdims=True)
                ds_scr[vs, :] = ds_t * decay[:, ks]
                dla = ddecay * decay[:, ks] + _dot_exact(tri_strict, dkd * kd[:, ks])
                dz_scr[rows, ks] = dla * dz_scale[:, ks]
                dk_ref[rows, ks] = (dkd * e[:, ks]).astype(dk_ref.dtype)
            return carry

        lax.fori_loop(0, nc, chunk, 0, unroll=GLA_UNROLL[1])
        dz = dz_scr[...]
        da_ref[...] = _dot(dz, wup_ref[...], 1, 1).astype(da_ref.dtype)
        dwup_ref[...] = _dot(a_ref[...].T, dz)
        dba_ref[...] = jnp.sum(dz, axis=0, keepdims=True)

    in_specs = _gla_specs(t) + [
        pl.BlockSpec((nc, GLA_HEADS * GLA_DV, GLA_DK), lambda i: (0, 0, 0)),
        pl.BlockSpec((t, GROUP_WIDTH), lambda i: (0, 0))]
    full = lambda r, c: pl.BlockSpec((r, c), lambda i: (0, 0))
    return _pcall(
        body, ride, grid=(1,), in_specs=in_specs,
        out_specs=[full(t, GLA_KW), full(t, GLA_KW), full(t, GROUP_WIDTH), full(t, GROUP_WIDTH), full(t, LANES),
                   full(LANES, GLA_KW), full(1, GLA_KW), full(1, GROUP_WIDTH)],
        out_shape=[jax.ShapeDtypeStruct((t, GLA_KW), ACT_DTYPE), jax.ShapeDtypeStruct((t, GLA_KW), ACT_DTYPE),
                   jax.ShapeDtypeStruct((t, GROUP_WIDTH), ACT_DTYPE), jax.ShapeDtypeStruct((t, GROUP_WIDTH), ACT_DTYPE),
                   jax.ShapeDtypeStruct((t, LANES), ACT_DTYPE), jax.ShapeDtypeStruct((LANES, GLA_KW), F32),
                   jax.ShapeDtypeStruct((1, GLA_KW), F32), jax.ShapeDtypeStruct((1, GROUP_WIDTH), F32)],
        scratch_shapes=[pltpu.VMEM((t, GLA_KW), F32), pltpu.VMEM((t, GLA_KW), F32),
                        pltpu.VMEM((GLA_HEADS * GLA_DV, GLA_DK), F32)],
        semantics=("arbitrary",), name="gla_bwd")(
            pmm, pmm, pmm, pel, pel, w_up, b_a, gnorm_w, states, dmix)


CUM_BLOCK = 256


def _fox_gate_fwd(pel, b_f):
    t = pel.shape[0]
    nb = t // CUM_BLOCK

    def body(f_ref, b_ref, cum_ref, cum_t_ref):
        tri = (_iota((CUM_BLOCK, CUM_BLOCK), 1) <= _iota((CUM_BLOCK, CUM_BLOCK), 0)).astype(F32)
        carry = jnp.zeros((1, LANES), F32)
        for blk in range(nb):
            rows = slice(blk * CUM_BLOCK, (blk + 1) * CUM_BLOCK)
            cum = _dot_exact(tri, _log_sigmoid(f_ref[rows, :] + b_ref[...])) + carry
            cum_ref[rows, :] = cum
            cum_t_ref[blk] = cum.T[:ATT_HEADS, :]
            carry = cum[CUM_BLOCK - 1:CUM_BLOCK, :]

    return pl.pallas_call(
        body, grid=(1,),
        in_specs=[pl.BlockSpec((t, LANES), lambda i: (0, 5)), pl.BlockSpec((1, LANES), lambda i: (0, 0))],
        out_specs=[pl.BlockSpec((t, LANES), lambda i: (0, 0)),
                   pl.BlockSpec((nb, ATT_HEADS, CUM_BLOCK), lambda i: (0, 0, 0))],
        out_shape=[jax.ShapeDtypeStruct((t, LANES), F32), jax.ShapeDtypeStruct((nb, ATT_HEADS, CUM_BLOCK), F32)],
        compiler_params=_params("arbitrary"), name="fox_gate_fwd")(pel, b_f)


def _fox_gate_bwd(pel, b_f, dcum_t, dcum_q):
    t = pel.shape[0]
    nb = t // CUM_BLOCK

    def body(f_ref, b_ref, dct_ref, dcq_ref, df_ref, db_ref):
        tri_up = (_iota((CUM_BLOCK, CUM_BLOCK), 1) >= _iota((CUM_BLOCK, CUM_BLOCK), 0)).astype(F32)
        carry = jnp.zeros((1, LANES), F32)
        db = jnp.zeros((1, LANES), F32)
        for blk in reversed(range(nb)):
            rows = slice(blk * CUM_BLOCK, (blk + 1) * CUM_BLOCK)
            dls = _dot_exact(tri_up, dct_ref[blk].T + dcq_ref[rows, :]) + carry
            carry = dls[0:1, :]
            df = dls * jax.nn.sigmoid(-(f_ref[rows, :] + b_ref[...]))
            df_ref[rows, :] = df.astype(df_ref.dtype)
            db = db + jnp.sum(df, axis=0, keepdims=True)
        db_ref[...] = db

    return pl.pallas_call(
        body, grid=(1,),
        in_specs=[pl.BlockSpec((t, LANES), lambda i: (0, 5)), pl.BlockSpec((1, LANES), lambda i: (0, 0)),
                  pl.BlockSpec((nb, LANES, CUM_BLOCK), lambda i: (0, 0, 0)), pl.BlockSpec((t, LANES), lambda i: (0, 0))],
        out_specs=[pl.BlockSpec((t, LANES), lambda i: (0, 0)), pl.BlockSpec((1, LANES), lambda i: (0, 0))],
        out_shape=[jax.ShapeDtypeStruct((t, LANES), ACT_DTYPE), jax.ShapeDtypeStruct((1, LANES), F32)],
        compiler_params=_params("arbitrary"), name="fox_gate_bwd")(pel, b_f, dcum_t, dcum_q)


FOX_Q_BLOCK = 256


assert FOX_Q_BLOCK == CUM_BLOCK


def _fox_scores(q_ref, k_ref, cum_ref, cum_t_ref, h, i):
    hs = slice(h * HEAD_DIM, (h + 1) * HEAD_DIM)
    nb = cum_t_ref.shape[0]
    key_gate = jnp.concatenate([cum_t_ref[kb, h:h + 1, :] for kb in range(nb)], axis=1)
    s = _dot(q_ref[:, hs], k_ref[:, hs], 1, 1) * (HEAD_DIM ** -0.5) + (cum_ref[:, h:h + 1] - key_gate)
    shape = (FOX_Q_BLOCK, nb * FOX_Q_BLOCK)
    return jnp.where(_iota(shape, 1) <= i * FOX_Q_BLOCK + _iota(shape, 0), s, NEG)


def _fox_specs(t):
    bq, nb = FOX_Q_BLOCK, t // FOX_Q_BLOCK
    return [pl.BlockSpec((bq, GROUP_WIDTH), lambda i: (i, 2)), pl.BlockSpec((t, GROUP_WIDTH), lambda i: (0, 3)),
            pl.BlockSpec((t, GROUP_WIDTH), lambda i: (0, 4)), pl.BlockSpec((bq, LANES), lambda i: (i, 0)),
            pl.BlockSpec((nb, ATT_HEADS, bq), lambda i: (0, 0, 0))]


def _fox_fwd(pmm, cum, cum_t, ride=None):
    t = pmm.shape[0]
    bq = FOX_Q_BLOCK

    def body(q_ref, k_ref, v_ref, cum_ref, cum_t_ref, o_ref, lse_ref):
        i = pl.program_id(0)
        lse_ref[...] = jnp.zeros_like(lse_ref)
        for h in range(ATT_HEADS):
            hs = slice(h * HEAD_DIM, (h + 1) * HEAD_DIM)
            s = _fox_scores(q_ref, k_ref, cum_ref, cum_t_ref, h, i)
            m = jnp.max(s, axis=-1, keepdims=True)
            p = jnp.exp(s - m)
            l = jnp.sum(p, axis=-1, keepdims=True)
            o_ref[:, hs] = (_dot(p, v_ref[:, hs]) / l).astype(o_ref.dtype)
            lse_ref[:, h:h + 1] = m + jnp.log(l)

    return _pcall(
        body, ride, grid=(t // bq,), in_specs=_fox_specs(t),
        out_specs=[pl.BlockSpec((bq, GROUP_WIDTH), lambda i: (i, 0)), pl.BlockSpec((bq, LANES), lambda i: (i, 0))],
        out_shape=[jax.ShapeDtypeStruct((t, GROUP_WIDTH), ACT_DTYPE), jax.ShapeDtypeStruct((t, LANES), F32)],
        semantics=("parallel",), name="fox_fwd")(pmm, pmm, pmm, cum, cum_t)


def _fox_bwd(pmm, cum, cum_t, lse, dmix, ride=None):
    t = pmm.shape[0]
    bq, nb = FOX_Q_BLOCK, t // FOX_Q_BLOCK
    scale = HEAD_DIM ** -0.5

    def body(q_ref, k_ref, v_ref, cum_ref, cum_t_ref, lse_ref, do_ref, dq_ref, dk_ref, dv_ref, dct_ref, dcq_ref):
        i = pl.program_id(0)

        @pl.when(i == 0)
        def _():
            dk_ref[...] = jnp.zeros_like(dk_ref)
            dv_ref[...] = jnp.zeros_like(dv_ref)
            dct_ref[...] = jnp.zeros_like(dct_ref)

        dcq_ref[...] = jnp.zeros_like(dcq_ref)
        for h in range(ATT_HEADS):
            hs = slice(h * HEAD_DIM, (h + 1) * HEAD_DIM)
            s = _fox_scores(q_ref, k_ref, cum_ref, cum_t_ref, h, i)
            p = jnp.exp(s - lse_ref[:, h:h + 1])
            do = do_ref[:, hs]
            dp = _dot(do, v_ref[:, hs], 1, 1)
            ds = p * (dp - jnp.sum(p * dp, axis=-1, keepdims=True))
            dq_ref[:, hs] = (_dot(ds, k_ref[:, hs]) * scale).astype(dq_ref.dtype)
            dk_ref[:, hs] += _dot(ds, q_ref[:, hs], 0, 0) * scale
            dv_ref[:, hs] += _dot(p, do, 0, 0)
            key_side = -jnp.sum(ds, axis=0, keepdims=True)
            for kb in range(nb):
                dct_ref[kb, h:h + 1, :] += key_side[:, kb * bq:(kb + 1) * bq]
            dcq_ref[:, h:h + 1] = jnp.sum(ds, axis=1, keepdims=True)

    whole = pl.BlockSpec((t, GROUP_WIDTH), lambda i: (0, 0))
    return _pcall(
        body, ride, grid=(t // bq,),
        in_specs=_fox_specs(t) + [pl.BlockSpec((bq, LANES), lambda i: (i, 0)),
                                  pl.BlockSpec((bq, GROUP_WIDTH), lambda i: (i, 1))],
        out_specs=[pl.BlockSpec((bq, GROUP_WIDTH), lambda i: (i, 0)), whole, whole,
                   pl.BlockSpec((nb, LANES, bq), lambda i: (0, 0, 0)), pl.BlockSpec((bq, LANES), lambda i: (i, 0))],
        out_shape=[jax.ShapeDtypeStruct((t, GROUP_WIDTH), ACT_DTYPE), jax.ShapeDtypeStruct((t, GROUP_WIDTH), F32),
                   jax.ShapeDtypeStruct((t, GROUP_WIDTH), F32), jax.ShapeDtypeStruct((nb, LANES, bq), F32),
                   jax.ShapeDtypeStruct((t, LANES), F32)],
        semantics=("arbitrary",), name="fox_bwd")(pmm, pmm, pmm, cum, cum_t, lse, dmix)


CA_Q_BLOCK = 4 * CHUNK
CA_WINDOW = CA_Q_BLOCK + CA_LEFT
CA_BASE = 1024


def _ca_bias_base(rel_bias):
    n = rel_bias.shape[0]
    flat = CA_Q_BLOCK + CA_LEFT - REL_CLIP
    tail = CA_BASE - flat - (2 * REL_CLIP + 1)
    return jnp.concatenate([jnp.broadcast_to(rel_bias[:, 2 * REL_CLIP:], (n, flat)), rel_bias[:, ::-1],
                            jnp.broadcast_to(rel_bias[:, :1], (n, tail))], axis=1)


def _ca_bias_base_grad(dbase):
    flat = CA_Q_BLOCK + CA_LEFT - REL_CLIP
    mid = dbase[:, flat:flat + 2 * REL_CLIP + 1][:, ::-1]
    lo = jnp.sum(dbase[:, flat + 2 * REL_CLIP + 1:], axis=1, keepdims=True)
    hi = jnp.sum(dbase[:, :flat], axis=1, keepdims=True)
    pad = jnp.zeros((dbase.shape[0], 2 * REL_CLIP - 1), F32)
    return mid + jnp.concatenate([lo, pad, hi], axis=1)


def _ca_mask(i):
    r, j = _iota((CA_Q_BLOCK, CA_WINDOW), 0), _iota((CA_Q_BLOCK, CA_WINDOW), 1)
    rc, jc = r // CHUNK, j // CHUNK
    return (jc >= rc) & (jc <= rc + CA_LEFT // CHUNK) & (i * CA_Q_BLOCK + j >= CA_LEFT)


def _ca_fill_bias(i, base_ref, bias_scr):
    @pl.when(i == 0)
    def _():
        for h in range(ATT_HEADS):
            rows = jnp.broadcast_to(base_ref[h:h + 1, :], (CA_Q_BLOCK, CA_BASE))
            bias_scr[h] = pltpu.roll(rows, CA_BASE - CA_Q_BLOCK, 1, stride=1, stride_axis=0)[:, :CA_WINDOW]


def _ca_scores(q_ref, kp_ref, bias_scr, win, h, mask):
    hs = slice(h * HEAD_DIM, (h + 1) * HEAD_DIM)
    s = _dot(q_ref[:, hs], kp_ref[win, hs], 1, 1) * (HEAD_DIM ** -0.5)
    return jnp.where(mask, s + bias_scr[h], NEG)


CA_BIAS_SCRATCH = pltpu.VMEM((ATT_HEADS, CA_Q_BLOCK, CA_WINDOW), F32)


def _ca_fwd(pmm, kp, vp, base, ride=None):
    t = pmm.shape[0]

    def body(q_ref, kp_ref, vp_ref, base_ref, o_ref, lse_ref, bias_scr):
        i = pl.program_id(0)
        _ca_fill_bias(i, base_ref, bias_scr)
        win = pl.ds(pl.multiple_of(i * CA_Q_BLOCK, CA_Q_BLOCK), CA_WINDOW)
        mask = _ca_mask(i)
        lse_ref[...] = jnp.zeros_like(lse_ref)
        for h in range(ATT_HEADS):
            hs = slice(h * HEAD_DIM, (h + 1) * HEAD_DIM)
            s = _ca_scores(q_ref, kp_ref, bias_scr, win, h, mask)
            m = jnp.max(s, axis=-1, keepdims=True)
            p = jnp.exp(s - m)
            l = jnp.sum(p, axis=-1, keepdims=True)
            o_ref[:, hs] = (_dot(p, vp_ref[win, hs]) / l).astype(o_ref.dtype)
            lse_ref[:, h:h + 1] = m + jnp.log(l)

    padded = pl.BlockSpec((t + CA_LEFT, GROUP_WIDTH), lambda i: (0, 0))
    return _pcall(
        body, ride, grid=(t // CA_Q_BLOCK,),
        in_specs=[pl.BlockSpec((CA_Q_BLOCK, GROUP_WIDTH), lambda i: (i, 0)), padded, padded,
                  pl.BlockSpec((ATT_HEADS, CA_BASE), lambda i: (0, 0))],
        out_specs=[pl.BlockSpec((CA_Q_BLOCK, GROUP_WIDTH), lambda i: (i, 0)),
                   pl.BlockSpec((CA_Q_BLOCK, LANES), lambda i: (i, 0))],
        out_shape=[jax.ShapeDtypeStruct((t, GROUP_WIDTH), ACT_DTYPE), jax.ShapeDtypeStruct((t, LANES), F32)],
        scratch_shapes=[CA_BIAS_SCRATCH], semantics=("arbitrary",), name="ca_fwd")(pmm, kp, vp, base)


def _ca_bwd(pmm, kp, vp, base, lse, dmix, ride=None):
    t = pmm.shape[0]
    scale = HEAD_DIM ** -0.5

    def body(q_ref, kp_ref, vp_ref, base_ref, lse_ref, do_ref, dq_ref, dkp_ref, dvp_ref, dbase_ref, bias_scr):
        i = pl.program_id(0)
        _ca_fill_bias(i, base_ref, bias_scr)

        @pl.when(i == 0)
        def _():
            dkp_ref[...] = jnp.zeros_like(dkp_ref)
            dvp_ref[...] = jnp.zeros_like(dvp_ref)
            dbase_ref[...] = jnp.zeros_like(dbase_ref)

        win = pl.ds(pl.multiple_of(i * CA_Q_BLOCK, CA_Q_BLOCK), CA_WINDOW)
        mask = _ca_mask(i)
        flip = (_iota((CA_Q_BLOCK, CA_Q_BLOCK), 0) + _iota((CA_Q_BLOCK, CA_Q_BLOCK), 1) == CA_Q_BLOCK - 1).astype(F32)
        for h in range(ATT_HEADS):
            hs = slice(h * HEAD_DIM, (h + 1) * HEAD_DIM)
            s = _ca_scores(q_ref, kp_ref, bias_scr, win, h, mask)
            p = jnp.exp(s - lse_ref[:, h:h + 1])
            do = do_ref[:, hs]
            dp = _dot(do, vp_ref[win, hs], 1, 1)
            ds = p * (dp - jnp.sum(p * dp, axis=-1, keepdims=True))
            dq_ref[:, hs] = (_dot(ds, kp_ref[win, hs]) * scale).astype(dq_ref.dtype)
            dkp_ref[win, hs] += _dot(ds, q_ref[:, hs], 0, 0) * scale
            dvp_ref[win, hs] += _dot(p, do, 0, 0)
            rev = jnp.concatenate([_dot(flip, ds), jnp.zeros((CA_Q_BLOCK, CA_BASE - CA_WINDOW), F32)], axis=1)
            lined = pltpu.roll(rev, 1, 1, stride=1, stride_axis=0)
            dbase_ref[h:h + 1, :] += jnp.sum(lined, axis=0, keepdims=True)

    padded = pl.BlockSpec((t + CA_LEFT, GROUP_WIDTH), lambda i: (0, 0))
    return _pcall(
        body, ride, grid=(t // CA_Q_BLOCK,),
        in_specs=[pl.BlockSpec((CA_Q_BLOCK, GROUP_WIDTH), lambda i: (i, 0)), padded, padded,
                  pl.BlockSpec((ATT_HEADS, CA_BASE), lambda i: (0, 0)),
                  pl.BlockSpec((CA_Q_BLOCK, LANES), lambda i: (i, 0)),
                  pl.BlockSpec((CA_Q_BLOCK, GROUP_WIDTH), lambda i: (i, 0))],
        out_specs=[pl.BlockSpec((CA_Q_BLOCK, GROUP_WIDTH), lambda i: (i, 0)), padded, padded,
                   pl.BlockSpec((ATT_HEADS, CA_BASE), lambda i: (0, 0))],
        out_shape=[jax.ShapeDtypeStruct((t, GROUP_WIDTH), ACT_DTYPE),
                   jax.ShapeDtypeStruct((t + CA_LEFT, GROUP_WIDTH), F32),
                   jax.ShapeDtypeStruct((t + CA_LEFT, GROUP_WIDTH), F32),
                   jax.ShapeDtypeStruct((ATT_HEADS, CA_BASE), F32)],
        scratch_shapes=[CA_BIAS_SCRATCH], semantics=("arbitrary",), name="ca_bwd")(pmm, kp, vp, base, lse, dmix)


GELU_C = 0.7978845608028654
GELU_A = 0.044715


def _shift_down(v, k, fill):
    return jnp.where(_iota(v.shape, 0) >= k, pltpu.roll(v, k, 0), fill)


def _shift_up(v, k, fill):
    t = v.shape[0]
    return jnp.where(_iota(v.shape, 0) < t - k, pltpu.roll(v, t - k, 0), fill)


def _linear_scan(a, b, shift):
    k = 1
    while k < a.shape[0]:
        b = a * shift(b, k, 0.0) + b
        a = a * shift(a, k, 1.0)
        k *= 2
    return b


def _neg_expm1(y):
    series = -y * (1.0 + y * (0.5 + y * (1.0 / 6.0 + y * (1.0 / 24.0 + y * (1.0 / 120.0)))))
    return jnp.where(y > -0.1, series, 1.0 - jnp.exp(y))


def _lru_forward(x, g_in, cw, cb, wa, ba, wx, bx, lam):
    xs = [_shift_down(x, CONV_WIDTH - 1 - j, 0.0) for j in range(CONV_WIDTH - 1)] + [x]
    xc = cb + sum(cw[j:j + 1, :] * xs[j] for j in range(CONV_WIDTH))
    r = jax.nn.sigmoid(_dot(xc, wa) + ba)
    i = jax.nn.sigmoid(_dot(xc, wx) + bx)
    lsl = _log_sigmoid(lam)
    la = LRU_C * r * lsl
    a = jnp.exp(la)
    s = jnp.sqrt(_neg_expm1(2.0 * la))
    h = _linear_scan(a, s * (i * xc), _shift_down)
    u = GELU_C * (g_in + GELU_A * g_in * g_in * g_in)
    th = jnp.tanh(u)
    gelu = 0.5 * g_in * (1.0 + th)
    return xs, xc, r, i, lsl, a, s, h, th, gelu


def _lru_specs(t):
    col = lambda off: pl.BlockSpec((t, LANES), lambda j: (0, j + off))
    vec = pl.BlockSpec((1, LANES), lambda j: (0, j))
    mat = pl.BlockSpec((None, LANES, LANES), lambda j: (j, 0, 0))
    return [col(0), col(GROUP_WIDTH // LANES), pl.BlockSpec((CONV_WIDTH, LANES), lambda j: (0, j)),
            vec, mat, vec, mat, vec, vec]


def _lru_fwd(pel, conv_w, conv_b, wa, ba, wx, bx, lam, ride=None):
    t = pel.shape[0]

    def body(g_ref, x_ref, cw_ref, cb_ref, wa_ref, ba_ref, wx_ref, bx_ref, lam_ref, o_ref):
        res = _lru_forward(x_ref[...], g_ref[...], cw_ref[...], cb_ref[...], wa_ref[...], ba_ref[...],
                           wx_ref[...], bx_ref[...], lam_ref[...])
        o_ref[...] = (res[7] * res[9]).astype(o_ref.dtype)

    return _pcall(
        body, ride, grid=(GROUP_WIDTH // LANES,), in_specs=_lru_specs(t),
        out_specs=pl.BlockSpec((t, LANES), lambda j: (0, j)),
        out_shape=jax.ShapeDtypeStruct((t, GROUP_WIDTH), ACT_DTYPE),
        semantics=("parallel",), name="lru_fwd")(pel, pel, conv_w, conv_b, wa, ba, wx, bx, lam)


def _lru_bwd(pel, conv_w, conv_b, wa, ba, wx, bx, lam, dmix, ride=None):
    t = pel.shape[0]

    def body(g_ref, x_ref, cw_ref, cb_ref, wa_ref, ba_ref, wx_ref, bx_ref, lam_ref, do_ref,
             dg_ref, dx_ref, dcw_ref, dcb_ref, dwa_ref, dba_ref, dwx_ref, dbx_ref, dlam_ref):
        g_in, cw, lam = g_ref[...], cw_ref[...], lam_ref[...]
        xs, xc, r, i, lsl, a, s, h, th, gelu = _lru_forward(
            x_ref[...], g_in, cw, cb_ref[...], wa_ref[...], ba_ref[...], wx_ref[...], bx_ref[...], lam)
        dout = do_ref[...]
        dgelu = 0.5 * (1.0 + th) + 0.5 * g_in * (1.0 - th * th) * GELU_C * (1.0 + 3.0 * GELU_A * g_in * g_in)
        dg_ref[...] = (dout * h * dgelu).astype(dg_ref.dtype)
        gsum = _linear_scan(_shift_up(a, 1, 0.0), dout * gelu, _shift_up)
        da = gsum * _shift_down(h, 1, 0.0)
        di = gsum * s * xc
        dla = da * a - gsum * (i * xc) * (a * a / s)
        dlam_ref[...] = jnp.sum(dla * (LRU_C * r), axis=0, keepdims=True) * jax.nn.sigmoid(-lam)
        dpr = dla * (LRU_C * lsl) * r * (1.0 - r)
        dpi = di * i * (1.0 - i)
        dxc = gsum * s * i + _dot(dpr, wa_ref[...], 1, 1) + _dot(dpi, wx_ref[...], 1, 1)
        xct = xc.T
        dwa_ref[...] = _dot(xct, dpr)
        dwx_ref[...] = _dot(xct, dpi)
        dba_ref[...] = jnp.sum(dpr, axis=0, keepdims=True)
        dbx_ref[...] = jnp.sum(dpi, axis=0, keepdims=True)
        dcb_ref[...] = jnp.sum(dxc, axis=0, keepdims=True)
        for j in range(CONV_WIDTH):
            dcw_ref[j:j + 1, :] = jnp.sum(dxc * xs[j], axis=0, keepdims=True)
        dx = cw[CONV_WIDTH - 1:CONV_WIDTH, :] * dxc
        for j in range(CONV_WIDTH - 1):
            dx = dx + cw[j:j + 1, :] * _shift_up(dxc, CONV_WIDTH - 1 - j, 0.0)
        dx_ref[...] = dx.astype(dx_ref.dtype)

    col = pl.BlockSpec((t, LANES), lambda j: (0, j))
    vec = pl.BlockSpec((1, LANES), lambda j: (0, j))
    mat = pl.BlockSpec((None, LANES, LANES), lambda j: (j, 0, 0))
    nb = GROUP_WIDTH // LANES
    vshape = jax.ShapeDtypeStruct((1, GROUP_WIDTH), F32)
    mshape = jax.ShapeDtypeStruct((nb, LANES, LANES), F32)
    return _pcall(
        body, ride, grid=(nb,),
        in_specs=_lru_specs(t) + [pl.BlockSpec((t, LANES), lambda j: (0, j + nb))],
        out_specs=[col, col, pl.BlockSpec((CONV_WIDTH, LANES), lambda j: (0, j)), vec, mat, vec, mat, vec, vec],
        out_shape=[jax.ShapeDtypeStruct((t, GROUP_WIDTH), ACT_DTYPE), jax.ShapeDtypeStruct((t, GROUP_WIDTH), ACT_DTYPE),
                   jax.ShapeDtypeStruct((CONV_WIDTH, GROUP_WIDTH), F32), vshape, mshape, vshape, mshape, vshape, vshape],
        semantics=("parallel",), name="lru_bwd")(
            pel, pel, conv_w, conv_b, wa, ba, wx, bx, lam, dmix)


def _block_diag_pairs(w):
    z = jnp.zeros((LRU_BLOCK_DIM, LRU_BLOCK_DIM), w.dtype)
    return jnp.stack([jnp.block([[w[2 * j], z], [z, w[2 * j + 1]]]) for j in range(w.shape[0] // 2)])


def _block_diag_pairs_grad(dw):
    b = LRU_BLOCK_DIM
    return jnp.stack([dw[n // 2, (n % 2) * b:(n % 2 + 1) * b, (n % 2) * b:(n % 2 + 1) * b] for n in range(2 * dw.shape[0])])


def _row_tile(r):
    return ROW_TILE if r % ROW_TILE == 0 else r


def _pair_sum(g, got, place, name):
    _, r, c = g.shape
    tile = r

    def body(place_ref, a_ref, b_ref, o_ref):
        o_ref[...] = (a_ref[...].astype(F32) + b_ref[...].astype(F32)).astype(o_ref.dtype)

    blk = pl.BlockSpec((1, tile, c), lambda k, i, place_ref: (k, i, 0))
    return pl.pallas_call(
        body,
        grid_spec=pltpu.PrefetchScalarGridSpec(
            num_scalar_prefetch=1, grid=(N_CHIPS, r // tile),
            in_specs=[pl.BlockSpec((1, tile, c), lambda k, i, place_ref: (2 * k + place_ref[0], i, 0)), blk],
            out_specs=blk),
        out_shape=jax.ShapeDtypeStruct(got.shape, got.dtype),
        compiler_params=_params("parallel", "parallel"), name=name)(place, g, got)


def _adamw_update(g, w_ref, m_ref, v_ref, g_ref, d_ref, nm_ref, nv_ref):
    nm = ADAM_B1 * m_ref[...] + (1.0 - ADAM_B1) * g
    nv = ADAM_B2 * v_ref[...] + (1.0 - ADAM_B2) * jnp.square(g)
    m_hat = nm / (1.0 - ADAM_B1 ** ADAM_STEP)
    v_hat = nv / (1.0 - ADAM_B2 ** ADAM_STEP)
    g_ref[...] = g
    d_ref[...] = -ADAM_LR * (m_hat / (jnp.sqrt(v_hat) + ADAM_EPS) + ADAM_WD * w_ref[...])
    nm_ref[...] = nm
    nv_ref[...] = nv


def _adamw_sharded(parts, w, m, v, place, name):
    n_layers, r, c = w.shape
    tile = _row_tile(r)
    nb = r // tile

    def body(place_ref, *refs):
        layer = pl.program_id(0)
        g = None
        for l in range(n_layers):
            s_ref, r_ref = refs[2 * l], refs[2 * l + 1]
            g_l = s_ref[0].astype(F32) + r_ref[0].astype(F32) + r_ref[1].astype(F32) + r_ref[2].astype(F32)
            g = g_l if g is None else jnp.where(layer == l, g_l, g)
        _adamw_update(g, *refs[2 * n_layers:])

    def part_specs(l):
        rows = lambda q, i: jnp.where(q < l, 0, jnp.where(q > l, nb - 1, i))
        return [pl.BlockSpec((1, tile, c), lambda q, i, place_ref: (place_ref[1], rows(q, i), 0)),
                pl.BlockSpec((3, tile, c), lambda q, i, place_ref: (0, rows(q, i), 0))]

    in_specs, args = [], []
    for l, (s, recv) in enumerate(parts):
        in_specs += part_specs(l)
        args += [s, recv]
    blk = pl.BlockSpec((None, tile, c), lambda q, i, place_ref: (q, i, 0))
    out = jax.ShapeDtypeStruct((n_layers, r, c), F32)
    return pl.pallas_call(
        body,
        grid_spec=pltpu.PrefetchScalarGridSpec(
            num_scalar_prefetch=1, grid=(n_layers, nb), in_specs=in_specs + [blk, blk, blk],
            out_specs=[blk, blk, blk, blk]),
        out_shape=[out, out, out, out], compiler_params=_params("arbitrary", "arbitrary"), name=name)(
            place, *args, w, m, v)


def _adamw_small(repl_parts, vec_parts, w, m, v, place):
    n_r, n = len(repl_parts), len(w)
    shapes = [a.shape for a in w]

    def body(place_ref, *refs):
        parts, rest = refs[:n], refs[n:]
        for k in range(n):
            take = (lambda p: parts[k][p]) if k < n_r else (lambda p: parts[k][p, 0])
            g = take(0)
            for p in range(1, N_DEV):
                g = g + take(p)
            _adamw_update(g, rest[k], rest[n + k], rest[2 * n + k], *rest[3 * n + 4 * k:3 * n + 4 * k + 4])

    def whole(shape):
        return pl.BlockSpec(shape, lambda i, place_ref: (0,) * len(shape))

    def mine(shard):
        return pl.BlockSpec((N_DEV, 1) + shard, lambda i, place_ref: (0, place_ref[2]) + (0,) * len(shard))

    in_specs = [whole(a.shape) for a in repl_parts] + [mine(s) for s in shapes[n_r:]] + [whole(s) for s in shapes] * 3
    outs = pl.pallas_call(
        body,
        grid_spec=pltpu.PrefetchScalarGridSpec(
            num_scalar_prefetch=1, grid=(1,), in_specs=in_specs,
            out_specs=[whole(s) for s in shapes for _ in range(4)]),
        out_shape=[jax.ShapeDtypeStruct(s, F32) for s in shapes for _ in range(4)],
        compiler_params=_params("arbitrary"), name="adamw_small")(place, *repl_parts, *vec_parts, *w, *m, *v)
    return [outs[4 * k:4 * k + 4] for k in range(n)]


SHARDED = {"norm_w": 2, "w_in_even": 2, "gla_w_a_up": 2, "w_out_even": 1, "w_in_odd": 2, "conv_w": 2, "conv_b": 1,
           "lru_b_a": 1, "lru_b_x": 1, "lru_lambda": 1, "w_out_odd": 1, "w_mlp_up": 2, "w_mlp_down": 1}
REPLICATED = ["gla_b_a", "gla_norm_w", "fox_b_f", "rel_bias", "lru_w_a", "lru_w_x"]
WEIGHTS = ["norm_w", "w_in_even", "gla_w_a_up", "gla_b_a", "gla_norm_w", "fox_b_f", "w_out_even", "w_in_odd",
           "rel_bias", "conv_w", "conv_b", "lru_w_a", "lru_b_a", "lru_w_x", "lru_b_x", "lru_lambda", "w_out_odd",
           "w_mlp_up", "w_mlp_down"]
MATRICES = ("w_in_even", "w_out_even", "w_in_odd", "w_out_odd", "w_mlp_up", "w_mlp_down")
TRANSPOSED = ("w_in_even", "w_in_odd")
VECTORS = tuple(n for n in SHARDED if n not in MATRICES)
MATRIX_BLOCKS = (("w_in_even", 0), ("w_out_even", 0), ("w_in_odd", 0), ("w_out_odd", 0),
                 ("w_mlp_up", 0), ("w_mlp_up", 1), ("w_mlp_down", 0), ("w_mlp_down", 1))


def _join_shards(blocks, axis):
    moved = jnp.moveaxis(blocks, 0, axis)
    shape = moved.shape
    return moved.reshape(shape[:axis] + (shape[axis] * shape[axis + 1],) + shape[axis + 2:])


def _split_shards(full, axis):
    shape = full.shape
    cut = full.reshape(shape[:axis] + (N_DEV, shape[axis] // N_DEV) + shape[axis + 1:])
    return jnp.moveaxis(cut, axis, 0)


EVEN_SPLITS = (0, 256, 512, 1024, 1536, 1552, 2064, 2576, 3088, 3096)


def _even_in_split(wt):
    c = [wt[EVEN_SPLITS[k]:EVEN_SPLITS[k + 1]] for k in range(9)]
    gq, gk, gv, gr, ga, fq, fk, fv, ff = c
    padrows = lambda a: jnp.pad(a, ((0, LANES - a.shape[0]), (0, 0)))
    return jnp.concatenate([gq, gk, gv, fq, fk, fv], axis=0), jnp.concatenate([gr, padrows(ga), padrows(ff)], axis=0)


def _even_in_merge(dmm, dele):
    return jnp.concatenate([dmm[:1024], dele[:512], dele[512:512 + GLA_RANK], dmm[1024:2560],
                            dele[640:640 + ATT_HEADS]], axis=0)


def _forward_backward(x, target, shard, vec_shard, w, place):
    w = dict(w)
    g, dnorm, sums, recv = {}, {}, {}, {}
    nrm = lambda l, k: w["norm_w"][l, k][None, :]
    gather = lambda *keys: _gather_plan([shard[k] for k in keys])
    blocks = lambda r, c: (N_DEV, r // N_DEV, c)

    def pair_sum(key):
        sums[key] = _pair_sum(g[key], got[key], place, f"rs_pair_sum_{key[0]}_{key[1]}")

    got = {}

    def mlp_fwd(xin, layer, ride_up, ride_down):
        h = _norm_fwd(xin, nrm(layer, 2), out_dtype=ACT_DTYPE, name=f"norm_mlp_{layer}")
        u = _mm(h, w["w_mlp_up"][layer], out_dtype=ACT_DTYPE, tm=TM_FWD, tn=D_FF // N_DEV, b_blocked=True,
                name=f"mlp_up_{layer}", ride=ride_up)
        u, rode_up = u if ride_up is not None else (u, None)
        if w["w_mlp_down"][layer] is None:
            w["w_mlp_down"][layer] = rode_up[0].reshape(D_FF, D_MODEL)
        yv = _mm(u, w["w_mlp_down"][layer], out_dtype=F32, tm=TM_DX, tn=TN, a_sqrelu=True,
                 name=f"mlp_down_{layer}", ride=ride_down)
        yv, rode_down = yv if ride_down is not None else (yv, None)
        xout = _norm_fwd(yv, nrm(layer, 3), out_dtype=F32, res=xin, name=f"norm_mlp_out_{layer}")
        return xout, (xin, h, u, yv), rode_up, rode_down

    def mlp_bwd(dxout, saved, layer, ride):
        xin, h, u, yv = saved
        k_up, k_down = ("w_mlp_up", layer), ("w_mlp_down", layer)
        dy, dnorm[(layer, 3)] = _norm_bwd(dxout, yv, nrm(layer, 3), out_dtype=ACT_DTYPE, name=f"norm_mlp_out_bwd_{layer}")
        du = _mm(dy, w["w_mlp_down"][layer], nt=True, out_dtype=ACT_DTYPE, tm=TM_DX, tn=TN, drelu_of=u,
                 name=f"mlp_down_dx_{layer}", ride=ride)
        rode = None
        if ride is not None:
            du, rode = du
        g[k_down] = _mm(u, dy, ta=True, out_dtype=WIRE_DTYPE, tm=TM_DW, tn=TN, a_sqrelu=True,
                        name=f"mlp_down_dw_{layer}").reshape(blocks(D_FF, D_MODEL))
        g[k_up] = _mm(h, du, ta=True, out_dtype=WIRE_DTYPE, tm=TM_DW, tn=D_FF // N_DEV, out_blocked=True,
                      name=f"mlp_up_dw_{layer}")
        w_up = jnp.moveaxis(w["w_mlp_up"][layer], 0, 1).reshape(D_MODEL, D_FF)
        dh, (got[k_down], got[k_up]) = _mm(du, w_up, nt=True, out_dtype=F32, tm=TM_DX, tn=TN, name=f"mlp_up_dx_{layer}",
                                           ride=_sibling_plan([g[k_down], g[k_up]]))
        pair_sum(k_down)
        pair_sum(k_up)
        dxin, dnorm[(layer, 2)] = _norm_bwd(dh, xin, nrm(layer, 2), out_dtype=F32, add=dxout, name=f"norm_mlp_bwd_{layer}")
        return dxin, rode

    first = _run_plan(_gather_plan([shard[("w_in_even", 0)]] + [vec_shard[n] for n in VECTORS]),
                      "weights_all_gather_first")
    w["w_in_even"] = first[0].reshape(-1, D_MODEL)
    for n, b in zip(VECTORS, first[1:]):
        w[n] = _join_shards(b, SHARDED[n])
    w["w_mlp_up"], w["w_mlp_down"] = [None] * DEPTH, [None] * DEPTH

    wmm_e, wel_e = _even_in_split(w["w_in_even"])
    w_up_pad = jnp.pad(w["gla_w_a_up"][0], ((0, LANES - GLA_RANK), (0, 0)))
    b_f_pad = jnp.pad(w["fox_b_f"], ((0, 0), (0, LANES - ATT_HEADS)))
    h0 = _norm_fwd(x, nrm(0, 0), out_dtype=ACT_DTYPE, name="norm_in_0")
    pmm0, (w_out_even,) = _mm(h0, wmm_e, nt=True, out_dtype=ACT_DTYPE, tm=TM_FWD, tn=TN, name="in_even_mm",
                              ride=gather(("w_out_even", 0)))
    pel0 = _mm(h0, wel_e, nt=True, out_dtype=F32, tm=TM_FWD, tn=768, name="in_even_el")
    out_a, states = _gla_fwd(pmm0, pel0, w_up_pad, w["gla_b_a"], w["gla_norm_w"])
    cum, cum_t = _fox_gate_fwd(pel0, b_f_pad)
    (out_b, lse_b), (w["w_mlp_up"][0], w_mlp_down0) = _fox_fwd(pmm0, cum, cum_t,
                                                               ride=gather(("w_mlp_up", 0), ("w_mlp_down", 0)))
    w["w_out_even"] = w_out_even.reshape(D_MODEL, D_MODEL)
    w["w_mlp_down"][0] = w_mlp_down0.reshape(D_FF, D_MODEL)
    mix_in0 = jnp.concatenate([out_a, out_b], axis=1)
    mix0 = _mm(mix_in0, w["w_out_even"], out_dtype=F32, tm=TM_FWD, tn=TN, name="out_even")
    x1 = _norm_fwd(mix0, nrm(0, 1), out_dtype=F32, res=x, name="norm_mix_0")
    x2, mlp0, _, (w_in_odd,) = mlp_fwd(x1, 0, None, gather(("w_in_odd", 0)))
    w["w_in_odd"] = w_in_odd.reshape(-1, D_MODEL)

    w_in_o = w["w_in_odd"]
    n_mm_o = 3 * GROUP_WIDTH
    wa_bd, wx_bd = _block_diag_pairs(w["lru_w_a"][0]), _block_diag_pairs(w["lru_w_x"][0])
    base = _ca_bias_base(w["rel_bias"][0])
    h1 = _norm_fwd(x2, nrm(1, 0), out_dtype=ACT_DTYPE, name="norm_in_1")
    pmm1 = _mm(h1, w_in_o[:n_mm_o], nt=True, out_dtype=ACT_DTYPE, tm=TM_FWD, tn=TN, name="in_odd_mm")
    pel1 = _mm(h1, w_in_o[n_mm_o:], nt=True, out_dtype=F32, tm=TM_FWD, tn=TN, name="in_odd_el")
    kp = jnp.pad(pmm1[:, GROUP_WIDTH:2 * GROUP_WIDTH], ((CA_LEFT, 0), (0, 0)))
    vp = jnp.pad(pmm1[:, 2 * GROUP_WIDTH:], ((CA_LEFT, 0), (0, 0)))
    (out_c, lse_c), (w["w_mlp_up"][1],) = _ca_fwd(pmm1, kp, vp, base, ride=gather(("w_mlp_up", 1)))
    lru_args = (pel1, w["conv_w"][0], w["conv_b"], wa_bd, w["lru_b_a"], wx_bd, w["lru_b_x"], w["lru_lambda"])
    out_d, (w_out_odd,) = _lru_fwd(*lru_args, ride=gather(("w_out_odd", 0)))
    w["w_out_odd"] = w_out_odd.reshape(D_MODEL, D_MODEL)
    mix_in1 = jnp.concatenate([out_c, out_d], axis=1)
    mix1 = _mm(mix_in1, w["w_out_odd"], out_dtype=F32, tm=TM_FWD, tn=TN, name="out_odd")
    x3 = _norm_fwd(mix1, nrm(1, 1), out_dtype=F32, res=x2, name="norm_mix_1")
    x4, mlp1, _, _ = mlp_fwd(x3, 1, gather(("w_mlp_down", 1)), None)

    loss, dx4 = _loss_fwd_bwd(x4, target)

    k_oo, k_io, k_oe, k_ie = ("w_out_odd", 0), ("w_in_odd", 0), ("w_out_even", 0), ("w_in_even", 0)
    mlp_keys = lambda l: [("w_mlp_down", l), ("w_mlp_up", l)]
    dx3, _ = mlp_bwd(dx4, mlp1, 1, None)
    dmix1, dnorm[(1, 1)] = _norm_bwd(dx3, mix1, nrm(1, 1), out_dtype=ACT_DTYPE, name="norm_mix_bwd_1")
    g[k_oo] = _mm(mix_in1, dmix1, ta=True, out_dtype=WIRE_DTYPE, tm=TM_DW, tn=TN, name="out_odd_dw").reshape(
        blocks(D_MODEL, D_MODEL))
    dmix_in1, (got[k_oo],) = _mm(dmix1, w["w_out_odd"], nt=True, out_dtype=F32, tm=TM_DX, tn=TN, name="out_odd_dx",
                                 ride=_sibling_plan([g[k_oo]]))
    (dq_c, dkp, dvp, dbase), rode = _ca_bwd(pmm1, kp, vp, base, lse_c, dmix_in1,
                                            ride=_chip_plan([sums[k] for k in mlp_keys(1)]))
    recv.update(zip(mlp_keys(1), rode))
    pair_sum(k_oo)
    (dgate, dxin, g_conv_w, g_conv_b, dwa_bd, g_lru_b_a, dwx_bd, g_lru_b_x, g_lru_lambda), (recv[k_oo],) = _lru_bwd(
        *lru_args, dmix_in1, ride=_chip_plan([sums[k_oo]]))
    dp1 = jnp.concatenate([dq_c, dkp[CA_LEFT:].astype(ACT_DTYPE), dvp[CA_LEFT:].astype(ACT_DTYPE), dgate, dxin], axis=1)
    g[k_io] = _mm(dp1, h1, ta=True, out_dtype=WIRE_DTYPE, tm=dp1.shape[1] // 2, tn=TN, name="in_odd_dw").reshape(
        blocks(dp1.shape[1], D_MODEL))
    dh1, (got[k_io],) = _mm(dp1, w_in_o, out_dtype=F32, tm=TM_DX, tn=TN, name="in_odd_dx",
                            ride=_sibling_plan([g[k_io]]))
    pair_sum(k_io)
    dx2, dnorm[(1, 0)] = _norm_bwd(dh1, x2, nrm(1, 0), out_dtype=F32, add=dx3, name="norm_in_bwd_1")
    g["rel_bias"] = _ca_bias_base_grad(dbase)[None]
    g["conv_w"], g["conv_b"] = g_conv_w[None], g_conv_b
    g["lru_w_a"], g["lru_w_x"] = _block_diag_pairs_grad(dwa_bd)[None], _block_diag_pairs_grad(dwx_bd)[None]
    g["lru_b_a"], g["lru_b_x"], g["lru_lambda"] = g_lru_b_a, g_lru_b_x, g_lru_lambda

    dx1, (recv[k_io],) = mlp_bwd(dx2, mlp0, 0, _chip_plan([sums[k_io]]))
    dmix0, dnorm[(0, 1)] = _norm_bwd(dx1, mix0, nrm(0, 1), out_dtype=ACT_DTYPE, name="norm_mix_bwd_0")
    g[k_oe] = _mm(mix_in0, dmix0, ta=True, out_dtype=WIRE_DTYPE, tm=TM_DW, tn=TN, name="out_even_dw").reshape(
        blocks(D_MODEL, D_MODEL))
    dmix_in0, (got[k_oe],) = _mm(dmix0, w["w_out_even"], nt=True, out_dtype=F32, tm=TM_DX, tn=TN, name="out_even_dx",
                                 ride=_sibling_plan([g[k_oe]]))
    k_md0, k_mu0 = mlp_keys(0)
    pair_sum(k_oe)
    (dq_a, dk_a, dv_a, dr_a, da_a, dw_up_pad, g_gla_b_a, g_gla_norm_w), (recv[k_md0],) = _gla_bwd(
        pmm0, pel0, w_up_pad, w["gla_b_a"], w["gla_norm_w"], states, dmix_in0, ride=_chip_plan([sums[k_md0]]))
    (dq_b, dk_b, dv_b, dcum_t, dcum_q), (recv[k_mu0], recv[k_oe]) = _fox_bwd(
        pmm0, cum, cum_t, lse_b, dmix_in0, ride=_chip_plan([sums[k_mu0], sums[k_oe]]))
    df_b, db_f = _fox_gate_bwd(pel0, b_f_pad, dcum_t, dcum_q)
    g["gla_w_a_up"] = dw_up_pad[:GLA_RANK][None]
    g["gla_b_a"], g["gla_norm_w"], g["fox_b_f"] = g_gla_b_a, g_gla_norm_w, db_f[:, :ATT_HEADS]
    dp0 = jnp.concatenate([dq_a, dk_a, dv_a, dq_b, dk_b.astype(ACT_DTYPE), dv_b.astype(ACT_DTYPE), dr_a, da_a, df_b],
                          axis=1)
    w_perm = jnp.concatenate([wmm_e, wel_e], axis=0)
    n_mm_e = wmm_e.shape[0]
    dw_perm, repl_parts = _mm(dp0, h0, ta=True, out_dtype=WIRE_DTYPE, tm=dp0.shape[1] // 2, tn=TN, name="in_even_dw",
                              ride=_gather_plan([g[n] for n in REPLICATED]))
    dw_even = _even_in_merge(dw_perm[:n_mm_e], dw_perm[n_mm_e:])
    g[k_ie] = dw_even.reshape(blocks(dw_even.shape[0], D_MODEL))
    dh0, (got[k_ie],) = _mm(dp0, w_perm, out_dtype=F32, tm=TM_DX, tn=TN, name="in_even_dx",
                            ride=_sibling_plan([g[k_ie]]))
    pair_sum(k_ie)
    (dx0, dnorm[(0, 0)]), (recv[k_ie],) = _norm_bwd(dh0, x, nrm(0, 0), out_dtype=F32, add=dx1, name="norm_in_bwd_0",
                                                     ride=_chip_plan([sums[k_ie]]))

    g["norm_w"] = jnp.stack([jnp.concatenate([dnorm[(l, k)] for k in range(4)], axis=0) for l in range(DEPTH)])
    vec_parts = _run_plan(_gather_plan([_split_shards(g[n], SHARDED[n]) for n in VECTORS]), "vector_grads_all_gather")
    return loss, dx0, sums, recv, repl_parts, vec_parts


def kernel(x, norm_w, w_in_even, gla_w_a_up, gla_b_a, gla_norm_w, fox_b_f, w_out_even, w_in_odd, rel_bias, conv_w, conv_b, lru_w_a, lru_b_a, lru_w_x, lru_b_x, lru_lambda, w_out_odd, w_mlp_up, w_mlp_down, loss_target, m_norm_w, m_w_in_even, m_gla_w_a_up, m_gla_b_a, m_gla_norm_w, m_fox_b_f, m_w_out_even, m_w_in_odd, m_rel_bias, m_conv_w, m_conv_b, m_lru_w_a, m_lru_b_a, m_lru_w_x, m_lru_b_x, m_lru_lambda, m_w_out_odd, m_w_mlp_up, m_w_mlp_down, v_norm_w, v_w_in_even, v_gla_w_a_up, v_gla_b_a, v_gla_norm_w, v_fox_b_f, v_w_out_even, v_w_in_odd, v_rel_bias, v_conv_w, v_conv_b, v_lru_w_a, v_lru_b_a, v_lru_w_x, v_lru_b_x, v_lru_lambda, v_w_out_odd, v_w_mlp_up, v_w_mlp_down):
    wts = dict(zip(WEIGHTS, (norm_w, w_in_even, gla_w_a_up, gla_b_a, gla_norm_w, fox_b_f, w_out_even, w_in_odd, rel_bias,
                             conv_w, conv_b, lru_w_a, lru_b_a, lru_w_x, lru_b_x, lru_lambda, w_out_odd, w_mlp_up,
                             w_mlp_down)))
    mom = dict(zip(WEIGHTS, (m_norm_w, m_w_in_even, m_gla_w_a_up, m_gla_b_a, m_gla_norm_w, m_fox_b_f, m_w_out_even,
                             m_w_in_odd, m_rel_bias, m_conv_w, m_conv_b, m_lru_w_a, m_lru_b_a, m_lru_w_x, m_lru_b_x,
                             m_lru_lambda, m_w_out_odd, m_w_mlp_up, m_w_mlp_down)))
    var = dict(zip(WEIGHTS, (v_norm_w, v_w_in_even, v_gla_w_a_up, v_gla_b_a, v_gla_norm_w, v_fox_b_f, v_w_out_even,
                             v_w_in_odd, v_rel_bias, v_conv_w, v_conv_b, v_lru_w_a, v_lru_b_a, v_lru_w_x, v_lru_b_x,
                             v_lru_lambda, v_w_out_odd, v_w_mlp_up, v_w_mlp_down)))
    ax, ay, ac = lax.axis_index("x"), lax.axis_index("y"), lax.axis_index("c")
    place = jnp.stack([ac, 2 * ax + ay, 4 * ax + 2 * ay + ac]).astype(jnp.int32)

    shard = {(n, l): (wts[n][l].T if n in TRANSPOSED else wts[n][l]).astype(WIRE_DTYPE) for n, l in MATRIX_BLOCKS}
    loss_blk, dx, sums, recv, repl_parts, vec_parts = _forward_backward(
        x[0], loss_target[0], shard, {n: wts[n] for n in VECTORS}, {n: wts[n] for n in REPLICATED}, place)
    loss = lax.psum(loss_blk[0, 0], ("x", "y", "c"))

    view = lambda n, a: jnp.swapaxes(a, 1, 2) if n in TRANSPOSED else a
    upd = {n: [view(n, o) for o in _adamw_sharded(
        [(sums[(n, l)], recv[(n, l)]) for l in range(wts[n].shape[0])], view(n, wts[n]), view(n, mom[n]), view(n, var[n]),
        place, f"adamw_{n}")] for n in MATRICES}
    small = REPLICATED + list(VECTORS)
    upd.update(zip(small, _adamw_small(repl_parts, vec_parts, [wts[n] for n in small], [mom[n] for n in small],
                                       [var[n] for n in small], place)))
    return (loss, dx[None], *[upd[n][kind] for kind in range(4) for n in WEIGHTS])
```

```python
import functools
from typing import Callable, NamedTuple

import jax
import jax.numpy as jnp
from jax import lax
from jax.experimental import pallas as pl
from jax.experimental.pallas import tpu as pltpu

F32 = jnp.float32
MXU_DTYPE = jnp.bfloat16
ACT_DTYPE = jnp.bfloat16
WIRE_DTYPE = jnp.bfloat16

V7X_VMEM_BYTES = 64 * 1024 * 1024
VMEM_LIMIT = (V7X_VMEM_BYTES * 7) // 8
LANES = 128

D_MODEL = 1024
SEQ = 2048
DEPTH = 2
CHUNK = 64
GROUP_WIDTH = D_MODEL // 2
D_FF = 4 * D_MODEL
NORM_EPS = 1e-6
GLA_HEADS = 4
GLA_DV = GROUP_WIDTH // GLA_HEADS
GLA_DK = GLA_DV // 2
GLA_KW = GLA_HEADS * GLA_DK
GLA_RANK = 16
GLA_GATE_TAU = 16.0
HEAD_DIM = 64
ATT_HEADS = GROUP_WIDTH // HEAD_DIM
CA_LEFT = 8 * CHUNK
REL_CLIP = 128
LRU_BLOCK_DIM = 64
CONV_WIDTH = 4
LRU_C = 8.0
N_DEV = 8

ADAM_LR = 0.001
ADAM_B1 = 0.9
ADAM_B2 = 0.999
ADAM_EPS = 1e-08
ADAM_WD = 0.01
ADAM_STEP = 10

NEG = float(jnp.finfo(jnp.float32).min)
MESH = pl.DeviceIdType.MESH


def _params(*sem):
    return pltpu.CompilerParams(dimension_semantics=sem, vmem_limit_bytes=VMEM_LIMIT)


def _dot(a, b, ca=1, cb=0):
    return lax.dot_general(a.astype(MXU_DTYPE), b.astype(MXU_DTYPE), (((ca,), (cb,)), ((), ())),
                           preferred_element_type=F32)


def _dot_exact(a, b):
    return lax.dot_general(a, b, (((1,), (0,)), ((), ())), precision=lax.Precision.HIGHEST,
                           preferred_element_type=F32)


def _log_sigmoid(x):
    return jnp.minimum(x, 0.0) - jnp.log1p(jnp.exp(-jnp.abs(x)))


def _iota(shape, axis):
    return lax.broadcasted_iota(jnp.int32, shape, axis)


ANY = pl.BlockSpec(memory_space=pl.ANY)
N_CHIPS = 4


class _Plan(NamedTuple):
    ins: list
    outs: list
    sems: list
    start: Callable
    finish: Callable


def _place():
    x, y, c = lax.axis_index("x"), lax.axis_index("y"), lax.axis_index("c")
    return x, y, c, [(1 - x, y), (x, 1 - y), (1 - x, 1 - y)]


def _gather_plan(xs):
    n = len(xs)

    def parts(x_refs, out_refs, sems):
        send_sems, recv_sems, local_sems = sems
        x, y, c, chips = _place()
        me, sibling = (x, y, c), (x, y, 1 - c)

        def rows(a, px, py, pc):
            return out_refs[a].at[4 * px + 2 * py + pc]

        def copy(a, k, block, to, src=None):
            return pltpu.make_async_remote_copy(
                src_ref=rows(a, *block) if src is None else src, dst_ref=rows(a, *block),
                send_sem=send_sems.at[7 * a + k], recv_sem=recv_sems.at[7 * a + k], device_id=to, device_id_type=MESH)

        mine = [pltpu.make_async_copy(x_refs[a], rows(a, *me), local_sems.at[a]) for a in range(n)]
        first = []
        for a in range(n):
            first.append(copy(a, 0, me, sibling, src=x_refs[a]))
            first += [copy(a, 1 + j, me, (*chip, c), src=x_refs[a]) for j, chip in enumerate(chips)]
        return c, me, sibling, chips, copy, mine, first

    def start(x_refs, out_refs, sems):
        *_, mine, first = parts(x_refs, out_refs, sems)
        for cp in first + mine:
            cp.start()

    def finish(x_refs, out_refs, sems):
        c, me, sibling, chips, copy, mine, first = parts(x_refs, out_refs, sems)
        passed = []
        for j, chip in enumerate(chips):
            for a in range(n):
                copy(a, 1 + j, (*chip, c), me).wait_recv()
                passed.append(copy(a, 4 + j, (*chip, c), sibling))
                passed[-1].start()
        for a in range(n):
            copy(a, 0, sibling, me).wait_recv()
            for j, chip in enumerate(chips):
                copy(a, 4 + j, (*chip, 1 - c), me).wait_recv()
        for cp in first + passed:
            cp.wait_send()
        for cp in mine:
            cp.wait()

    return _Plan(list(xs), [jax.ShapeDtypeStruct((N_DEV,) + x.shape, x.dtype) for x in xs],
                 [pltpu.SemaphoreType.DMA((7 * n,)), pltpu.SemaphoreType.DMA((7 * n,)), pltpu.SemaphoreType.DMA((n,))],
                 start, finish)


def _exchange_plan(copies_of, ins, outs, per_array):
    n = len(ins)

    def start(in_refs, out_refs, sems):
        for cp in copies_of(in_refs, out_refs, sems):
            cp.start()

    def finish(in_refs, out_refs, sems):
        copies = copies_of(in_refs, out_refs, sems)
        for cp in copies:
            cp.wait_recv()
        for cp in copies:
            cp.wait_send()

    return _Plan(list(ins), outs, [pltpu.SemaphoreType.DMA((per_array * n,)), pltpu.SemaphoreType.DMA((per_array * n,))],
                 start, finish)


def _sibling_plan(gs):
    def copies_of(g_refs, got_refs, sems):
        x, y, c, _ = _place()
        return [pltpu.make_async_remote_copy(
            src_ref=g_refs[a].at[2 * k + (1 - c)], dst_ref=got_refs[a].at[k], send_sem=sems[0].at[N_CHIPS * a + k],
            recv_sem=sems[1].at[N_CHIPS * a + k], device_id=(x, y, 1 - c), device_id_type=MESH)
            for a in range(len(gs)) for k in range(N_CHIPS)]

    return _exchange_plan(copies_of, gs, [jax.ShapeDtypeStruct((N_CHIPS,) + g.shape[1:], g.dtype) for g in gs], N_CHIPS)


def _chip_plan(ss):
    def copies_of(s_refs, out_refs, sems):
        x, y, c, chips = _place()
        return [pltpu.make_async_remote_copy(
            src_ref=s_refs[a].at[2 * px + py], dst_ref=out_refs[a].at[j], send_sem=sems[0].at[3 * a + j],
            recv_sem=sems[1].at[3 * a + j], device_id=(px, py, c), device_id_type=MESH)
            for a in range(len(ss)) for j, (px, py) in enumerate(chips)]

    return _exchange_plan(copies_of, ss, [jax.ShapeDtypeStruct((3,) + s.shape[1:], s.dtype) for s in ss], 3)


def _run_plan(plan, name):
    n_in, n_out = len(plan.ins), len(plan.outs)

    def body(*refs):
        args = refs[:n_in], refs[n_in:n_in + n_out], refs[n_in + n_out:]
        plan.start(*args)
        plan.finish(*args)

    return pl.pallas_call(body, out_shape=plan.outs, in_specs=[ANY] * n_in, out_specs=[ANY] * n_out,
                          scratch_shapes=plan.sems, name=name)(*plan.ins)


def _pcall(body, ride, *, grid, in_specs, out_specs, out_shape, scratch_shapes=(), semantics, name):
    if ride is None:
        return pl.pallas_call(body, grid=grid, in_specs=in_specs, out_specs=out_specs, out_shape=out_shape,
                              scratch_shapes=list(scratch_shapes), compiler_params=_params(*semantics), name=name)
    single = not isinstance(out_shape, (list, tuple))
    out_specs_l, out_shape_l = ([out_specs], [out_shape]) if single else (list(out_specs), list(out_shape))
    n_in, n_out, n_scr = len(in_specs), len(out_shape_l), len(scratch_shapes)
    r_in, r_out = len(ride.ins), len(ride.outs)

    def riding(*refs):
        cuts = [n_in, r_in, n_out, r_out, n_scr]
        groups, at = [], 0
        for width in cuts:
            groups.append(refs[at:at + width])
            at += width
        ins, r_ins, outs, r_outs, scr = groups
        sems = refs[at:]
        first = functools.reduce(jnp.logical_and, [pl.program_id(d) == 0 for d in range(len(grid))])
        last = functools.reduce(jnp.logical_and, [pl.program_id(d) == grid[d] - 1 for d in range(len(grid))])

        @pl.when(first)
        def _():
            ride.start(r_ins, r_outs, sems)

        body(*ins, *outs, *scr)

        @pl.when(last)
        def _():
            ride.finish(r_ins, r_outs, sems)

    call = pl.pallas_call(
        riding, grid=grid, in_specs=list(in_specs) + [ANY] * r_in, out_specs=out_specs_l + [ANY] * r_out,
        out_shape=out_shape_l + list(ride.outs), scratch_shapes=list(scratch_shapes) + list(ride.sems),
        compiler_params=_params(*(["arbitrary"] * len(grid))), name=name)

    def run(*args):
        res = call(*args, *ride.ins)
        return (res[0] if single else list(res[:n_out])), list(res[n_out:])

    return run


def _mm(a, b, *, nt=False, ta=False, out_dtype, tm, tn, a_sqrelu=False, drelu_of=None, b_blocked=False,
        out_blocked=False, name, ride=None):
    k, m = a.shape if ta else a.shape[::-1]
    if b_blocked:
        assert not nt and b.shape[1] == k and b.shape[2] == tn
        n = b.shape[0] * tn
    else:
        n = b.shape[0] if nt else b.shape[1]
        assert (b.shape[1] if nt else b.shape[0]) == k
    tm, tn = min(tm, m), min(tn, n)
    assert m % tm == 0 and n % tn == 0

    def body(*refs):
        a_ref, b_ref = refs[0], refs[1]
        o_ref = refs[-1]
        av = a_ref[...]
        if a_sqrelu:
            av = jnp.square(jnp.maximum(av.astype(F32), 0.0))
        acc = _dot(av, b_ref[...], 0 if ta else 1, 1 if nt else 0)
        if drelu_of is not None:
            acc = acc * (2.0 * jnp.maximum(refs[2][...].astype(F32), 0.0))
        o_ref[...] = acc.astype(out_dtype)

    if b_blocked:
        b_spec = pl.BlockSpec((None, k, tn), lambda i, j: (j, 0, 0))
    elif nt:
        b_spec = pl.BlockSpec((tn, k), lambda i, j: (j, 0))
    else:
        b_spec = pl.BlockSpec((k, tn), lambda i, j: (0, j))
    a_spec = pl.BlockSpec((k, tm), lambda i, j: (0, i)) if ta else pl.BlockSpec((tm, k), lambda i, j: (i, 0))
    in_specs = [a_spec, b_spec]
    args = [a, b]
    if drelu_of is not None:
        in_specs.append(pl.BlockSpec((tm, tn), lambda i, j: (i, j)))
        args.append(drelu_of)
    if out_blocked:
        out_spec = pl.BlockSpec((None, tm, tn), lambda i, j: (j, i, 0))
        out_shape = jax.ShapeDtypeStruct((n // tn, m, tn), out_dtype)
    else:
        out_spec = pl.BlockSpec((tm, tn), lambda i, j: (i, j))
        out_shape = jax.ShapeDtypeStruct((m, n), out_dtype)
    return _pcall(body, ride, grid=(m // tm, n // tn), in_specs=in_specs, out_specs=out_spec, out_shape=out_shape,
                  semantics=("parallel", "parallel"), name=name)(*args)


def _mm_nt_blocked(a, b, *, out_dtype, tm, tn, name):
    m = a.shape[0]
    p, n, kp = b.shape
    assert a.shape[1] == p * kp and m % tm == 0 and n % tn == 0

    def body(a_ref, b_ref, o_ref, acc_ref):
        @pl.when(pl.program_id(2) == 0)
        def _():
            acc_ref[...] = jnp.zeros_like(acc_ref)

        acc_ref[...] += _dot(a_ref[...], b_ref[...], 1, 1)

        @pl.when(pl.program_id(2) == p - 1)
        def _():
            o_ref[...] = acc_ref[...].astype(out_dtype)

    return pl.pallas_call(
        body, grid=(m // tm, n // tn, p),
        in_specs=[pl.BlockSpec((tm, kp), lambda i, j, q: (i, q)), pl.BlockSpec((None, tn, kp), lambda i, j, q: (q, j, 0))],
        out_specs=pl.BlockSpec((tm, tn), lambda i, j, q: (i, j)),
        out_shape=jax.ShapeDtypeStruct((m, n), out_dtype),
        scratch_shapes=[pltpu.VMEM((tm, tn), F32)],
        compiler_params=_params("parallel", "parallel", "arbitrary"), name=name)(a, b)


ROW_TILE = 512
TM_FWD, TM_DX, TM_DW, TN = 2048, 1024, 1024, 512


def _norm_fwd(x, w, *, out_dtype, res=None, name):
    t, d = x.shape

    def body(*refs):
        x_ref, w_ref, o_ref = refs[0], refs[1], refs[-1]
        xv = x_ref[...]
        y = xv * lax.rsqrt(jnp.mean(xv * xv, axis=-1, keepdims=True) + NORM_EPS) * w_ref[...]
        if res is not None:
            y = refs[2][...] + y
        o_ref[...] = y.astype(out_dtype)

    row = pl.BlockSpec((ROW_TILE, d), lambda i: (i, 0))
    in_specs = [row, pl.BlockSpec((1, d), lambda i: (0, 0))] + ([row] if res is not None else [])
    args = [x, w] + ([res] if res is not None else [])
    return pl.pallas_call(body, grid=(t // ROW_TILE,), in_specs=in_specs, out_specs=row,
                          out_shape=jax.ShapeDtypeStruct((t, d), out_dtype),
                          compiler_params=_params("parallel"), name=name)(*args)


def _norm_bwd(dy, x, w, *, out_dtype, add=None, name, ride=None):
    t, d = x.shape

    def body(*refs):
        dy_ref, x_ref, w_ref = refs[0], refs[1], refs[2]
        dx_ref, dw_ref = refs[-2], refs[-1]
        xv = x_ref[...]
        rstd = lax.rsqrt(jnp.mean(xv * xv, axis=-1, keepdims=True) + NORM_EPS)
        xhat = xv * rstd
        dyv = dy_ref[...].astype(F32)
        g = dyv * w_ref[...]
        dx = rstd * (g - xhat * jnp.mean(g * xhat, axis=-1, keepdims=True))
        if add is not None:
            dx = dx + refs[3][...]
        dx_ref[...] = dx.astype(out_dtype)

        @pl.when(pl.program_id(0) == 0)
        def _():
            dw_ref[...] = jnp.zeros_like(dw_ref)

        dw_ref[...] += jnp.sum(dyv * xhat, axis=0, keepdims=True)

    row = pl.BlockSpec((ROW_TILE, d), lambda i: (i, 0))
    vec = pl.BlockSpec((1, d), lambda i: (0, 0))
    in_specs = [row, row, vec] + ([row] if add is not None else [])
    args = [dy, x, w] + ([add] if add is not None else [])
    return _pcall(body, ride, grid=(t // ROW_TILE,), in_specs=in_specs, out_specs=[row, vec],
                  out_shape=[jax.ShapeDtypeStruct((t, d), out_dtype), jax.ShapeDtypeStruct((1, d), F32)],
                  semantics=("arbitrary",), name=name)(*args)


def _loss_fwd_bwd(y, target):
    t, d = y.shape

    def body(y_ref, t_ref, l_ref, dy_ref):
        diff = y_ref[...] - t_ref[...]
        dy_ref[...] = diff * (1.0 / d)

        @pl.when(pl.program_id(0) == 0)
        def _():
            l_ref[...] = jnp.zeros_like(l_ref)

        l_ref[...] += 0.5 * jnp.sum(jnp.mean(diff * diff, axis=-1, keepdims=True), axis=0, keepdims=True)

    row = pl.BlockSpec((ROW_TILE, d), lambda i: (i, 0))
    return pl.pallas_call(body, grid=(t // ROW_TILE,), in_specs=[row, row],
                          out_specs=[pl.BlockSpec((8, LANES), lambda i: (0, 0)), row],
                          out_shape=[jax.ShapeDtypeStruct((8, LANES), F32), jax.ShapeDtypeStruct((t, d), F32)],
                          compiler_params=_params("arbitrary"), name="loss")(y, target)


GLA_UNROLL = (8, 4)


def _gla_specs(t):
    return [pl.BlockSpec((t, GLA_KW), lambda i: (0, 0)),
            pl.BlockSpec((t, GLA_KW), lambda i: (0, 1)),
            pl.BlockSpec((t, GROUP_WIDTH), lambda i: (0, 1)),
            pl.BlockSpec((t, GROUP_WIDTH), lambda i: (0, 0)),
            pl.BlockSpec((t, LANES), lambda i: (0, 4)),
            pl.BlockSpec((LANES, GLA_KW), lambda i: (0, 0)),
            pl.BlockSpec((1, GLA_KW), lambda i: (0, 0)),
            pl.BlockSpec((1, GROUP_WIDTH), lambda i: (0, 0))]


def _gla_decays(e_scr, dec_scr, nc):
    tri = (_iota((CHUNK, CHUNK), 1) <= _iota((CHUNK, CHUNK), 0)).astype(F32)

    def one(c, carry):
        rows = pl.ds(pl.multiple_of(c * CHUNK, CHUNK), CHUNK)
        cum = _dot_exact(tri, e_scr[rows, :])
        tot = cum[CHUNK - 1:CHUNK, :]
        e_scr[rows, :] = jnp.exp(tot - cum)
        dec_scr[c] = jnp.broadcast_to(jnp.exp(tot), (8, GLA_KW))
        return carry

    lax.fori_loop(0, nc, one, 0, unroll=GLA_UNROLL[0])


def _gla_fwd(pmm, pel, w_up, b_a, gnorm_w, ride=None):
    t = pmm.shape[0]
    nc = t // CHUNK
    scale = GLA_DK ** -0.5

    def body(q_ref, k_ref, v_ref, r_ref, a_ref, wup_ref, ba_ref, gw_ref, o_ref, st_ref, e_scr, dec_scr, s_scr):
        z = _dot(a_ref[...], wup_ref[...]) + ba_ref[...]
        e_scr[...] = _log_sigmoid(z) * (1.0 / GLA_GATE_TAU)
        _gla_decays(e_scr, dec_scr, nc)
        s_scr[...] = jnp.zeros_like(s_scr)

        def chunk(c, carry):
            rows = pl.ds(pl.multiple_of(c * CHUNK, CHUNK), CHUNK)
            kd = k_ref[rows, :].astype(F32) * e_scr[rows, :]
            decay = dec_scr[c, 0:1, :]
            qs = q_ref[rows, :].astype(F32) * scale
            vv = v_ref[rows, :].astype(F32)
            rr = r_ref[rows, :]
            gate = rr * jax.nn.sigmoid(rr) * gw_ref[...]
            for h in range(GLA_HEADS):
                ks = slice(h * GLA_DK, (h + 1) * GLA_DK)
                vs = slice(h * GLA_DV, (h + 1) * GLA_DV)
                inc_t = _dot(vv[:, vs].T, kd[:, ks])
                s_t = s_scr[vs, :] * decay[:, ks] + inc_t
                s_scr[vs, :] = s_t
                st_ref[c, vs, :] = s_t
                o = _dot(qs[:, ks], s_t, 1, 1)
                y = o * lax.rsqrt(jnp.mean(o * o, axis=-1, keepdims=True) + NORM_EPS)
                o_ref[rows, vs] = (y * gate[:, vs]).astype(o_ref.dtype)
            return carry

        lax.fori_loop(0, nc, chunk, 0, unroll=GLA_UNROLL[0])

    return _pcall(
        body, ride, grid=(1,), in_specs=_gla_specs(t),
        out_specs=[pl.BlockSpec((t, GROUP_WIDTH), lambda i: (0, 0)),
                   pl.BlockSpec((nc, GLA_HEADS * GLA_DV, GLA_DK), lambda i: (0, 0, 0))],
        out_shape=[jax.ShapeDtypeStruct((t, GROUP_WIDTH), ACT_DTYPE),
                   jax.ShapeDtypeStruct((nc, GLA_HEADS * GLA_DV, GLA_DK), F32)],
        scratch_shapes=[pltpu.VMEM((t, GLA_KW), F32), pltpu.VMEM((nc, 8, GLA_KW), F32),
                        pltpu.VMEM((GLA_HEADS * GLA_DV, GLA_DK), F32)],
        semantics=("arbitrary",), name="gla_fwd")(pmm, pmm, pmm, pel, pel, w_up, b_a, gnorm_w)


def _gla_bwd(pmm, pel, w_up, b_a, gnorm_w, states, dmix, ride=None):
    t = pmm.shape[0]
    nc = t // CHUNK
    scale = GLA_DK ** -0.5

    def body(q_ref, k_ref, v_ref, r_ref, a_ref, wup_ref, ba_ref, gw_ref, st_ref, do_ref,
             dq_ref, dk_ref, dv_ref, dr_ref, da_ref, dwup_ref, dba_ref, dgw_ref,
             e_scr, dec_scr, dz_scr, g_scr, dd_scr, ds_scr):
        z = _dot(a_ref[...], wup_ref[...]) + ba_ref[...]
        e_scr[...] = _log_sigmoid(z) * (1.0 / GLA_GATE_TAU)
        dz_scr[...] = jax.nn.sigmoid(-z) * (1.0 / GLA_GATE_TAU)
        _gla_decays(e_scr, dec_scr, nc)
        ds_scr[...] = jnp.zeros_like(ds_scr)
        dgw_ref[...] = jnp.zeros_like(dgw_ref)

        def chunk(n, carry):
            c = nc - 1 - n
            rows = pl.ds(pl.multiple_of(c * CHUNK, CHUNK), CHUNK)
            e = e_scr[rows, :]
            kd = k_ref[rows, :].astype(F32) * e
            decay = dec_scr[c, 0:1, :]
            qs = q_ref[rows, :].astype(F32) * scale
            vv = v_ref[rows, :].astype(F32)
            rr = r_ref[rows, :]
            sig = jax.nn.sigmoid(rr)
            silu = rr * sig
            dsilu = sig * (1.0 + rr * (1.0 - sig))
            dout = do_ref[rows, :]
            gw = gw_ref[...]
            c_prev = jnp.maximum(c - 1, 0)
            has_prev = (c > 0).astype(F32)
            for h in range(GLA_HEADS):
                ks = slice(h * GLA_DK, (h + 1) * GLA_DK)
                vs = slice(h * GLA_DV, (h + 1) * GLA_DV)
                s_t = st_ref[c, vs, :]
                s_prev = st_ref[c_prev, vs, :] * has_prev
                o = _dot(qs[:, ks], s_t, 1, 1)
                rstd = lax.rsqrt(jnp.mean(o * o, axis=-1, keepdims=True) + NORM_EPS)
                y = o * rstd
                dg = dout[:, vs]
                dgw_ref[:, vs] += jnp.sum(dg * y * silu[:, vs], axis=0, keepdims=True)
                dr_ref[rows, vs] = (dg * y * gw[:, vs] * dsilu[:, vs]).astype(dr_ref.dtype)
                dy = dg * gw[:, vs] * silu[:, vs]
                d_o = rstd * (dy - y * jnp.mean(dy * y, axis=-1, keepdims=True))
                dq_ref[rows, ks] = (_dot(d_o, s_t) * scale).astype(dq_ref.dtype)
                ds_t = ds_scr[vs, :] + _dot(d_o.T, qs[:, ks])
                dv_ref[rows, vs] = _dot(kd[:, ks], ds_t, 1, 1).astype(dv_ref.dtype)
                dkd = _dot(vv[:, vs], ds_t)
                dd_scr[c, 0:1, ks] = jnp.sum(ds_t * s_prev, axis=0, keepdims=True) * decay[:, ks]
                ds_scr[vs, :] = ds_t * decay[:, ks]
                g_scr[rows, ks] = dkd * kd[:, ks]
                dk_ref[rows, ks] = (dkd * e[:, ks]).astype(dk_ref.dtype)
            return carry

        lax.fori_loop(0, nc, chunk, 0, unroll=GLA_UNROLL[1])

        tri_strict = (_iota((CHUNK, CHUNK), 1) < _iota((CHUNK, CHUNK), 0)).astype(F32)

        def gate_grad(c, carry):
            rows = pl.ds(pl.multiple_of(c * CHUNK, CHUNK), CHUNK)
            dla = dd_scr[c, 0:1, :] + _dot_exact(tri_strict, g_scr[rows, :])
            dz_scr[rows, :] = dla * dz_scr[rows, :]
            return carry

        lax.fori_loop(0, nc, gate_grad, 0, unroll=GLA_UNROLL[0])
        dz = dz_scr[...]
        da_ref[...] = _dot(dz, wup_ref[...], 1, 1).astype(da_ref.dtype)
        dwup_ref[...] = _dot(a_ref[...].T, dz)
        dba_ref[...] = jnp.sum(dz, axis=0, keepdims=True)

    in_specs = _gla_specs(t) + [
        pl.BlockSpec((nc, GLA_HEADS * GLA_DV, GLA_DK), lambda i: (0, 0, 0)),
        pl.BlockSpec((t, GROUP_WIDTH), lambda i: (0, 0))]
    full = lambda r, c: pl.BlockSpec((r, c), lambda i: (0, 0))
    return _pcall(
        body, ride, grid=(1,), in_specs=in_specs,
        out_specs=[full(t, GLA_KW), full(t, GLA_KW), full(t, GROUP_WIDTH), full(t, GROUP_WIDTH), full(t, LANES),
                   full(LANES, GLA_KW), full(1, GLA_KW), full(1, GROUP_WIDTH)],
        out_shape=[jax.ShapeDtypeStruct((t, GLA_KW), ACT_DTYPE), jax.ShapeDtypeStruct((t, GLA_KW), ACT_DTYPE),
                   jax.ShapeDtypeStruct((t, GROUP_WIDTH), ACT_DTYPE), jax.ShapeDtypeStruct((t, GROUP_WIDTH), ACT_DTYPE),
                   jax.ShapeDtypeStruct((t, LANES), ACT_DTYPE), jax.ShapeDtypeStruct((LANES, GLA_KW), F32),
                   jax.ShapeDtypeStruct((1, GLA_KW), F32), jax.ShapeDtypeStruct((1, GROUP_WIDTH), F32)],
        scratch_shapes=[pltpu.VMEM((t, GLA_KW), F32), pltpu.VMEM((nc, 8, GLA_KW), F32), pltpu.VMEM((t, GLA_KW), F32),
                        pltpu.VMEM((t, GLA_KW), F32), pltpu.VMEM((nc, 8, GLA_KW), F32),
                        pltpu.VMEM((GLA_HEADS * GLA_DV, GLA_DK), F32)],
        semantics=("arbitrary",), name="gla_bwd")(
            pmm, pmm, pmm, pel, pel, w_up, b_a, gnorm_w, states, dmix)


CUM_BLOCK = 256


def _fox_gate_fwd(pel, b_f):
    t = pel.shape[0]
    nb = t // CUM_BLOCK

    def body(f_ref, b_ref, cum_ref, cum_t_ref):
        tri = (_iota((CUM_BLOCK, CUM_BLOCK), 1) <= _iota((CUM_BLOCK, CUM_BLOCK), 0)).astype(F32)
        carry = jnp.zeros((1, LANES), F32)
        for blk in range(nb):
            rows = slice(blk * CUM_BLOCK, (blk + 1) * CUM_BLOCK)
            cum = _dot_exact(tri, _log_sigmoid(f_ref[rows, :] + b_ref[...])) + carry
            cum_ref[rows, :] = cum
            cum_t_ref[blk] = cum.T[:ATT_HEADS, :]
            carry = cum[CUM_BLOCK - 1:CUM_BLOCK, :]

    return pl.pallas_call(
        body, grid=(1,),
        in_specs=[pl.BlockSpec((t, LANES), lambda i: (0, 5)), pl.BlockSpec((1, LANES), lambda i: (0, 0))],
        out_specs=[pl.BlockSpec((t, LANES), lambda i: (0, 0)),
                   pl.BlockSpec((nb, ATT_HEADS, CUM_BLOCK), lambda i: (0, 0, 0))],
        out_shape=[jax.ShapeDtypeStruct((t, LANES), F32), jax.ShapeDtypeStruct((nb, ATT_HEADS, CUM_BLOCK), F32)],
        compiler_params=_params("arbitrary"), name="fox_gate_fwd")(pel, b_f)


def _fox_gate_bwd(pel, b_f, dcum_t, dcum_q):
    t = pel.shape[0]
    nb = t // CUM_BLOCK

    def body(f_ref, b_ref, dct_ref, dcq_ref, df_ref, db_ref):
        tri_up = (_iota((CUM_BLOCK, CUM_BLOCK), 1) >= _iota((CUM_BLOCK, CUM_BLOCK), 0)).astype(F32)
        carry = jnp.zeros((1, LANES), F32)
        db = jnp.zeros((1, LANES), F32)
        for blk in reversed(range(nb)):
            rows = slice(blk * CUM_BLOCK, (blk + 1) * CUM_BLOCK)
            dls = _dot_exact(tri_up, dct_ref[blk].T + dcq_ref[rows, :]) + carry
            carry = dls[0:1, :]
            df = dls * jax.nn.sigmoid(-(f_ref[rows, :] + b_ref[...]))
            df_ref[rows, :] = df.astype(df_ref.dtype)
            db = db + jnp.sum(df, axis=0, keepdims=True)
        db_ref[...] = db

    return pl.pallas_call(
        body, grid=(1,),
        in_specs=[pl.BlockSpec((t, LANES), lambda i: (0, 5)), pl.BlockSpec((1, LANES), lambda i: (0, 0)),
                  pl.BlockSpec((nb, LANES, CUM_BLOCK), lambda i: (0, 0, 0)), pl.BlockSpec((t, LANES), lambda i: (0, 0))],
        out_specs=[pl.BlockSpec((t, LANES), lambda i: (0, 0)), pl.BlockSpec((1, LANES), lambda i: (0, 0))],
        out_shape=[jax.ShapeDtypeStruct((t, LANES), ACT_DTYPE), jax.ShapeDtypeStruct((1, LANES), F32)],
        compiler_params=_params("arbitrary"), name="fox_gate_bwd")(pel, b_f, dcum_t, dcum_q)


FOX_Q_BLOCK = 256


assert FOX_Q_BLOCK == CUM_BLOCK


def _fox_scores(q_ref, k_ref, cum_ref, cum_t_ref, h, i):
    hs = slice(h * HEAD_DIM, (h + 1) * HEAD_DIM)
    nb = cum_t_ref.shape[0]
    key_gate = jnp.concatenate([cum_t_ref[kb, h:h + 1, :] for kb in range(nb)], axis=1)
    s = _dot(q_ref[:, hs], k_ref[:, hs], 1, 1) * (HEAD_DIM ** -0.5) + (cum_ref[:, h:h + 1] - key_gate)
    shape = (FOX_Q_BLOCK, nb * FOX_Q_BLOCK)
    return jnp.where(_iota(shape, 1) <= i * FOX_Q_BLOCK + _iota(shape, 0), s, NEG)


def _fox_specs(t):
    bq, nb = FOX_Q_BLOCK, t // FOX_Q_BLOCK
    return [pl.BlockSpec((bq, GROUP_WIDTH), lambda i: (i, 2)), pl.BlockSpec((t, GROUP_WIDTH), lambda i: (0, 3)),
            pl.BlockSpec((t, GROUP_WIDTH), lambda i: (0, 4)), pl.BlockSpec((bq, LANES), lambda i: (i, 0)),
            pl.BlockSpec((nb, ATT_HEADS, bq), lambda i: (0, 0, 0))]


def _fox_fwd(pmm, cum, cum_t, ride=None):
    t = pmm.shape[0]
    bq = FOX_Q_BLOCK

    def body(q_ref, k_ref, v_ref, cum_ref, cum_t_ref, o_ref, lse_ref):
        i = pl.program_id(0)
        lse_ref[...] = jnp.zeros_like(lse_ref)
        for h in range(ATT_HEADS):
            hs = slice(h * HEAD_DIM, (h + 1) * HEAD_DIM)
            s = _fox_scores(q_ref, k_ref, cum_ref, cum_t_ref, h, i)
            m = jnp.max(s, axis=-1, keepdims=True)
            p = jnp.exp(s - m)
            l = jnp.sum(p, axis=-1, keepdims=True)
            o_ref[:, hs] = (_dot(p, v_ref[:, hs]) / l).astype(o_ref.dtype)
            lse_ref[:, h:h + 1] = m + jnp.log(l)

    return _pcall(
        body, ride, grid=(t // bq,), in_specs=_fox_specs(t),
        out_specs=[pl.BlockSpec((bq, GROUP_WIDTH), lambda i: (i, 0)), pl.BlockSpec((bq, LANES), lambda i: (i, 0))],
        out_shape=[jax.ShapeDtypeStruct((t, GROUP_WIDTH), ACT_DTYPE), jax.ShapeDtypeStruct((t, LANES), F32)],
        semantics=("parallel",), name="fox_fwd")(pmm, pmm, pmm, cum, cum_t)


def _fox_bwd(pmm, cum, cum_t, lse, dmix, ride=None):
    t = pmm.shape[0]
    bq, nb = FOX_Q_BLOCK, t // FOX_Q_BLOCK
    scale = HEAD_DIM ** -0.5

    def body(q_ref, k_ref, v_ref, cum_ref, cum_t_ref, lse_ref, do_ref, dq_ref, dk_ref, dv_ref, dct_ref, dcq_ref):
        i = pl.program_id(0)

        @pl.when(i == 0)
        def _():
            dk_ref[...] = jnp.zeros_like(dk_ref)
            dv_ref[...] = jnp.zeros_like(dv_ref)
            dct_ref[...] = jnp.zeros_like(dct_ref)

        dcq_ref[...] = jnp.zeros_like(dcq_ref)
        for h in range(ATT_HEADS):
            hs = slice(h * HEAD_DIM, (h + 1) * HEAD_DIM)
            s = _fox_scores(q_ref, k_ref, cum_ref, cum_t_ref, h, i)
            p = jnp.exp(s - lse_ref[:, h:h + 1])
            do = do_ref[:, hs]
            dp = _dot(do, v_ref[:, hs], 1, 1)
            ds = p * (dp - jnp.sum(p * dp, axis=-1, keepdims=True))
            dq_ref[:, hs] = (_dot(ds, k_ref[:, hs]) * scale).astype(dq_ref.dtype)
            dk_ref[:, hs] += _dot(ds, q_ref[:, hs], 0, 0) * scale
            dv_ref[:, hs] += _dot(p, do, 0, 0)
            key_side = -jnp.sum(ds, axis=0, keepdims=True)
            for kb in range(nb):
                dct_ref[kb, h:h + 1, :] += key_side[:, kb * bq:(kb + 1) * bq]
            dcq_ref[:, h:h + 1] = jnp.sum(ds, axis=1, keepdims=True)

    whole = pl.BlockSpec((t, GROUP_WIDTH), lambda i: (0, 0))
    return _pcall(
        body, ride, grid=(t // bq,),
        in_specs=_fox_specs(t) + [pl.BlockSpec((bq, LANES), lambda i: (i, 0)),
                                  pl.BlockSpec((bq, GROUP_WIDTH), lambda i: (i, 1))],
        out_specs=[pl.BlockSpec((bq, GROUP_WIDTH), lambda i: (i, 0)), whole, whole,
                   pl.BlockSpec((nb, LANES, bq), lambda i: (0, 0, 0)), pl.BlockSpec((bq, LANES), lambda i: (i, 0))],
        out_shape=[jax.ShapeDtypeStruct((t, GROUP_WIDTH), ACT_DTYPE), jax.ShapeDtypeStruct((t, GROUP_WIDTH), F32),
                   jax.ShapeDtypeStruct((t, GROUP_WIDTH), F32), jax.ShapeDtypeStruct((nb, LANES, bq), F32),
                   jax.ShapeDtypeStruct((t, LANES), F32)],
        semantics=("arbitrary",), name="fox_bwd")(pmm, pmm, pmm, cum, cum_t, lse, dmix)


CA_Q_BLOCK = 4 * CHUNK
CA_WINDOW = CA_Q_BLOCK + CA_LEFT
CA_BASE = 1024


def _ca_bias_base(rel_bias):
    n = rel_bias.shape[0]
    flat = CA_Q_BLOCK + CA_LEFT - REL_CLIP
    tail = CA_BASE - flat - (2 * REL_CLIP + 1)
    return jnp.concatenate([jnp.broadcast_to(rel_bias[:, 2 * REL_CLIP:], (n, flat)), rel_bias[:, ::-1],
                            jnp.broadcast_to(rel_bias[:, :1], (n, tail))], axis=1)


def _ca_bias_base_grad(dbase):
    flat = CA_Q_BLOCK + CA_LEFT - REL_CLIP
    mid = dbase[:, flat:flat + 2 * REL_CLIP + 1][:, ::-1]
    lo = jnp.sum(dbase[:, flat + 2 * REL_CLIP + 1:], axis=1, keepdims=True)
    hi = jnp.sum(dbase[:, :flat], axis=1, keepdims=True)
    pad = jnp.zeros((dbase.shape[0], 2 * REL_CLIP - 1), F32)
    return mid + jnp.concatenate([lo, pad, hi], axis=1)


def _ca_mask(i):
    r, j = _iota((CA_Q_BLOCK, CA_WINDOW), 0), _iota((CA_Q_BLOCK, CA_WINDOW), 1)
    rc, jc = r // CHUNK, j // CHUNK
    return (jc >= rc) & (jc <= rc + CA_LEFT // CHUNK) & (i * CA_Q_BLOCK + j >= CA_LEFT)


def _ca_fill_bias(i, base_ref, bias_scr):
    @pl.when(i == 0)
    def _():
        for h in range(ATT_HEADS):
            rows = jnp.broadcast_to(base_ref[h:h + 1, :], (CA_Q_BLOCK, CA_BASE))
            bias_scr[h] = pltpu.roll(rows, CA_BASE - CA_Q_BLOCK, 1, stride=1, stride_axis=0)[:, :CA_WINDOW]


def _ca_scores(q_ref, kp_ref, bias_scr, win, h, mask):
    hs = slice(h * HEAD_DIM, (h + 1) * HEAD_DIM)
    s = _dot(q_ref[:, hs], kp_ref[win, hs], 1, 1) * (HEAD_DIM ** -0.5)
    return jnp.where(mask, s + bias_scr[h], NEG)


CA_BIAS_SCRATCH = pltpu.VMEM((ATT_HEADS, CA_Q_BLOCK, CA_WINDOW), F32)


def _ca_fwd(pmm, kp, vp, base, ride=None):
    t = pmm.shape[0]

    def body(q_ref, kp_ref, vp_ref, base_ref, o_ref, lse_ref, bias_scr):
        i = pl.program_id(0)
        _ca_fill_bias(i, base_ref, bias_scr)
        win = pl.ds(pl.multiple_of(i * CA_Q_BLOCK, CA_Q_BLOCK), CA_WINDOW)
        mask = _ca_mask(i)
        lse_ref[...] = jnp.zeros_like(lse_ref)
        for h in range(ATT_HEADS):
            hs = slice(h * HEAD_DIM, (h + 1) * HEAD_DIM)
            s = _ca_scores(q_ref, kp_ref, bias_scr, win, h, mask)
            m = jnp.max(s, axis=-1, keepdims=True)
            p = jnp.exp(s - m)
            l = jnp.sum(p, axis=-1, keepdims=True)
            o_ref[:, hs] = (_dot(p, vp_ref[win, hs]) / l).astype(o_ref.dtype)
            lse_ref[:, h:h + 1] = m + jnp.log(l)

    padded = pl.BlockSpec((t + CA_LEFT, GROUP_WIDTH), lambda i: (0, 0))
    return _pcall(
        body, ride, grid=(t // CA_Q_BLOCK,),
        in_specs=[pl.BlockSpec((CA_Q_BLOCK, GROUP_WIDTH), lambda i: (i, 0)), padded, padded,
                  pl.BlockSpec((ATT_HEADS, CA_BASE), lambda i: (0, 0))],
        out_specs=[pl.BlockSpec((CA_Q_BLOCK, GROUP_WIDTH), lambda i: (i, 0)),
                   pl.BlockSpec((CA_Q_BLOCK, LANES), lambda i: (i, 0))],
        out_shape=[jax.ShapeDtypeStruct((t, GROUP_WIDTH), ACT_DTYPE), jax.ShapeDtypeStruct((t, LANES), F32)],
        scratch_shapes=[CA_BIAS_SCRATCH], semantics=("arbitrary",), name="ca_fwd")(pmm, kp, vp, base)


def _ca_bwd(pmm, kp, vp, base, lse, dmix, ride=None):
    t = pmm.shape[0]
    scale = HEAD_DIM ** -0.5

    def body(q_ref, kp_ref, vp_ref, base_ref, lse_ref, do_ref, dq_ref, dkp_ref, dvp_ref, dbase_ref, bias_scr):
        i = pl.program_id(0)
        _ca_fill_bias(i, base_ref, bias_scr)

        @pl.when(i == 0)
        def _():
            dkp_ref[...] = jnp.zeros_like(dkp_ref)
            dvp_ref[...] = jnp.zeros_like(dvp_ref)
            dbase_ref[...] = jnp.zeros_like(dbase_ref)

        win = pl.ds(pl.multiple_of(i * CA_Q_BLOCK, CA_Q_BLOCK), CA_WINDOW)
        mask = _ca_mask(i)
        flip = (_iota((CA_Q_BLOCK, CA_Q_BLOCK), 0) + _iota((CA_Q_BLOCK, CA_Q_BLOCK), 1) == CA_Q_BLOCK - 1).astype(F32)
        for h in range(ATT_HEADS):
            hs = slice(h * HEAD_DIM, (h + 1) * HEAD_DIM)
            s = _ca_scores(q_ref, kp_ref, bias_scr, win, h, mask)
            p = jnp.exp(s - lse_ref[:, h:h + 1])
            do = do_ref[:, hs]
            dp = _dot(do, vp_ref[win, hs], 1, 1)
            ds = p * (dp - jnp.sum(p * dp, axis=-1, keepdims=True))
            dq_ref[:, hs] = (_dot(ds, kp_ref[win, hs]) * scale).astype(dq_ref.dtype)
            dkp_ref[win, hs] += _dot(ds, q_ref[:, hs], 0, 0) * scale
            dvp_ref[win, hs] += _dot(p, do, 0, 0)
            rev = jnp.concatenate([_dot(flip, ds), jnp.zeros((CA_Q_BLOCK, CA_BASE - CA_WINDOW), F32)], axis=1)
            lined = pltpu.roll(rev, 1, 1, stride=1, stride_axis=0)
            dbase_ref[h:h + 1, :] += jnp.sum(lined, axis=0, keepdims=True)

    padded = pl.BlockSpec((t + CA_LEFT, GROUP_WIDTH), lambda i: (0, 0))
    return _pcall(
        body, ride, grid=(t // CA_Q_BLOCK,),
        in_specs=[pl.BlockSpec((CA_Q_BLOCK, GROUP_WIDTH), lambda i: (i, 0)), padded, padded,
                  pl.BlockSpec((ATT_HEADS, CA_BASE), lambda i: (0, 0)),
                  pl.BlockSpec((CA_Q_BLOCK, LANES), lambda i: (i, 0)),
                  pl.BlockSpec((CA_Q_BLOCK, GROUP_WIDTH), lambda i: (i, 0))],
        out_specs=[pl.BlockSpec((CA_Q_BLOCK, GROUP_WIDTH), lambda i: (i, 0)), padded, padded,
                   pl.BlockSpec((ATT_HEADS, CA_BASE), lambda i: (0, 0))],
        out_shape=[jax.ShapeDtypeStruct((t, GROUP_WIDTH), ACT_DTYPE),
                   jax.ShapeDtypeStruct((t + CA_LEFT, GROUP_WIDTH), F32),
                   jax.ShapeDtypeStruct((t + CA_LEFT, GROUP_WIDTH), F32),
                   jax.ShapeDtypeStruct((ATT_HEADS, CA_BASE), F32)],
        scratch_shapes=[CA_BIAS_SCRATCH], semantics=("arbitrary",), name="ca_bwd")(pmm, kp, vp, base, lse, dmix)


GELU_C = 0.7978845608028654
GELU_A = 0.044715


def _shift_down(v, k, fill):
    return jnp.where(_iota(v.shape, 0) >= k, pltpu.roll(v, k, 0), fill)


def _shift_up(v, k, fill):
    t = v.shape[0]
    return jnp.where(_iota(v.shape, 0) < t - k, pltpu.roll(v, t - k, 0), fill)


def _linear_scan(a, b, shift):
    k = 1
    while k < a.shape[0]:
        b = a * shift(b, k, 0.0) + b
        a = a * shift(a, k, 1.0)
        k *= 2
    return b


def _neg_expm1(y):
    series = -y * (1.0 + y * (0.5 + y * (1.0 / 6.0 + y * (1.0 / 24.0 + y * (1.0 / 120.0)))))
    return jnp.where(y > -0.1, series, 1.0 - jnp.exp(y))


def _lru_forward(x, g_in, cw, cb, wa, ba, wx, bx, lam):
    xs = [_shift_down(x, CONV_WIDTH - 1 - j, 0.0) for j in range(CONV_WIDTH - 1)] + [x]
    xc = cb + sum(cw[j:j + 1, :] * xs[j] for j in range(CONV_WIDTH))
    r = jax.nn.sigmoid(_dot(xc, wa) + ba)
    i = jax.nn.sigmoid(_dot(xc, wx) + bx)
    lsl = _log_sigmoid(lam)
    la = LRU_C * r * lsl
    a = jnp.exp(la)
    s = jnp.sqrt(_neg_expm1(2.0 * la))
    h = _linear_scan(a, s * (i * xc), _shift_down)
    u = GELU_C * (g_in + GELU_A * g_in * g_in * g_in)
    th = jnp.tanh(u)
    gelu = 0.5 * g_in * (1.0 + th)
    return xs, xc, r, i, lsl, a, s, h, th, gelu


def _lru_specs(t):
    col = lambda off: pl.BlockSpec((t, LANES), lambda j: (0, j + off))
    vec = pl.BlockSpec((1, LANES), lambda j: (0, j))
    mat = pl.BlockSpec((None, LANES, LANES), lambda j: (j, 0, 0))
    return [col(0), col(GROUP_WIDTH // LANES), pl.BlockSpec((CONV_WIDTH, LANES), lambda j: (0, j)),
            vec, mat, vec, mat, vec, vec]


def _lru_fwd(pel, conv_w, conv_b, wa, ba, wx, bx, lam, ride=None):
    t = pel.shape[0]

    def body(g_ref, x_ref, cw_ref, cb_ref, wa_ref, ba_ref, wx_ref, bx_ref, lam_ref, o_ref):
        res = _lru_forward(x_ref[...], g_ref[...], cw_ref[...], cb_ref[...], wa_ref[...], ba_ref[...],
                           wx_ref[...], bx_ref[...], lam_ref[...])
        o_ref[...] = (res[7] * res[9]).astype(o_ref.dtype)

    return _pcall(
        body, ride, grid=(GROUP_WIDTH // LANES,), in_specs=_lru_specs(t),
        out_specs=pl.BlockSpec((t, LANES), lambda j: (0, j)),
        out_shape=jax.ShapeDtypeStruct((t, GROUP_WIDTH), ACT_DTYPE),
        semantics=("parallel",), name="lru_fwd")(pel, pel, conv_w, conv_b, wa, ba, wx, bx, lam)


def _lru_bwd(pel, conv_w, conv_b, wa, ba, wx, bx, lam, dmix, ride=None):
    t = pel.shape[0]

    def body(g_ref, x_ref, cw_ref, cb_ref, wa_ref, ba_ref, wx_ref, bx_ref, lam_ref, do_ref,
             dg_ref, dx_ref, dcw_ref, dcb_ref, dwa_ref, dba_ref, dwx_ref, dbx_ref, dlam_ref):
        g_in, cw, lam = g_ref[...], cw_ref[...], lam_ref[...]
        xs, xc, r, i, lsl, a, s, h, th, gelu = _lru_forward(
            x_ref[...], g_in, cw, cb_ref[...], wa_ref[...], ba_ref[...], wx_ref[...], bx_ref[...], lam)
        dout = do_ref[...]
        dgelu = 0.5 * (1.0 + th) + 0.5 * g_in * (1.0 - th * th) * GELU_C * (1.0 + 3.0 * GELU_A * g_in * g_in)
        dg_ref[...] = (dout * h * dgelu).astype(dg_ref.dtype)
        gsum = _linear_scan(_shift_up(a, 1, 0.0), dout * gelu, _shift_up)
        da = gsum * _shift_down(h, 1, 0.0)
        di = gsum * s * xc
        dla = da * a - gsum * (i * xc) * (a * a / s)
        dlam_ref[...] = jnp.sum(dla * (LRU_C * r), axis=0, keepdims=True) * jax.nn.sigmoid(-lam)
        dpr = dla * (LRU_C * lsl) * r * (1.0 - r)
        dpi = di * i * (1.0 - i)
        dxc = gsum * s * i + _dot(dpr, wa_ref[...], 1, 1) + _dot(dpi, wx_ref[...], 1, 1)
        xct = xc.T
        dwa_ref[...] = _dot(xct, dpr)
        dwx_ref[...] = _dot(xct, dpi)
        dba_ref[...] = jnp.sum(dpr, axis=0, keepdims=True)
        dbx_ref[...] = jnp.sum(dpi, axis=0, keepdims=True)
        dcb_ref[...] = jnp.sum(dxc, axis=0, keepdims=True)
        for j in range(CONV_WIDTH):
            dcw_ref[j:j + 1, :] = jnp.sum(dxc * xs[j], axis=0, keepdims=True)
        dx = cw[CONV_WIDTH - 1:CONV_WIDTH, :] * dxc
        for j in range(CONV_WIDTH - 1):
            dx = dx + cw[j:j + 1, :] * _shift_up(dxc, CONV_WIDTH - 1 - j, 0.0)
        dx_ref[...] = dx.astype(dx_ref.dtype)

    col = pl.BlockSpec((t, LANES), lambda j: (0, j))
    vec = pl.BlockSpec((1, LANES), lambda j: (0, j))
    mat = pl.BlockSpec((None, LANES, LANES), lambda j: (j, 0, 0))
    nb = GROUP_WIDTH // LANES
    vshape = jax.ShapeDtypeStruct((1, GROUP_WIDTH), F32)
    mshape = jax.ShapeDtypeStruct((nb, LANES, LANES), F32)
    return _pcall(
        body, ride, grid=(nb,),
        in_specs=_lru_specs(t) + [pl.BlockSpec((t, LANES), lambda j: (0, j + nb))],
        out_specs=[col, col, pl.BlockSpec((CONV_WIDTH, LANES), lambda j: (0, j)), vec, mat, vec, mat, vec, vec],
        out_shape=[jax.ShapeDtypeStruct((t, GROUP_WIDTH), ACT_DTYPE), jax.ShapeDtypeStruct((t, GROUP_WIDTH), ACT_DTYPE),
                   jax.ShapeDtypeStruct((CONV_WIDTH, GROUP_WIDTH), F32), vshape, mshape, vshape, mshape, vshape, vshape],
        semantics=("parallel",), name="lru_bwd")(
            pel, pel, conv_w, conv_b, wa, ba, wx, bx, lam, dmix)


def _block_diag_pairs(w):
    z = jnp.zeros((LRU_BLOCK_DIM, LRU_BLOCK_DIM), w.dtype)
    return jnp.stack([jnp.block([[w[2 * j], z], [z, w[2 * j + 1]]]) for j in range(w.shape[0] // 2)])


def _block_diag_pairs_grad(dw):
    b = LRU_BLOCK_DIM
    return jnp.stack([dw[n // 2, (n % 2) * b:(n % 2 + 1) * b, (n % 2) * b:(n % 2 + 1) * b] for n in range(2 * dw.shape[0])])


def _row_tile(r):
    return ROW_TILE if r % ROW_TILE == 0 else r


def _pair_sum(g, got, place, name):
    _, r, c = g.shape
    tile = r

    def body(place_ref, a_ref, b_ref, o_ref):
        o_ref[...] = (a_ref[...].astype(F32) + b_ref[...].astype(F32)).astype(o_ref.dtype)

    blk = pl.BlockSpec((1, tile, c), lambda k, i, place_ref: (k, i, 0))
    return pl.pallas_call(
        body,
        grid_spec=pltpu.PrefetchScalarGridSpec(
            num_scalar_prefetch=1, grid=(N_CHIPS, r // tile),
            in_specs=[pl.BlockSpec((1, tile, c), lambda k, i, place_ref: (2 * k + place_ref[0], i, 0)), blk],
            out_specs=blk),
        out_shape=jax.ShapeDtypeStruct(got.shape, got.dtype),
        compiler_params=_params("parallel", "parallel"), name=name)(place, g, got)


def _adamw_update(g, w_ref, m_ref, v_ref, g_ref, d_ref, nm_ref, nv_ref):
    nm = ADAM_B1 * m_ref[...] + (1.0 - ADAM_B1) * g
    nv = ADAM_B2 * v_ref[...] + (1.0 - ADAM_B2) * jnp.square(g)
    m_hat = nm / (1.0 - ADAM_B1 ** ADAM_STEP)
    v_hat = nv / (1.0 - ADAM_B2 ** ADAM_STEP)
    g_ref[...] = g
    d_ref[...] = -ADAM_LR * (m_hat / (jnp.sqrt(v_hat) + ADAM_EPS) + ADAM_WD * w_ref[...])
    nm_ref[...] = nm
    nv_ref[...] = nv


def _adamw_sharded(parts, w, m, v, place, name):
    n_layers, r, c = w.shape
    tile = _row_tile(r)
    nb = r // tile

    def body(place_ref, *refs):
        layer = pl.program_id(0)
        g = None
        for l in range(n_layers):
            s_ref, r_ref = refs[2 * l], refs[2 * l + 1]
            g_l = s_ref[0].astype(F32) + r_ref[0].astype(F32) + r_ref[1].astype(F32) + r_ref[2].astype(F32)
            g = g_l if g is None else jnp.where(layer == l, g_l, g)
        _adamw_update(g, *refs[2 * n_layers:])

    def part_specs(l):
        rows = lambda q, i: jnp.where(q < l, 0, jnp.where(q > l, nb - 1, i))
        return [pl.BlockSpec((1, tile, c), lambda q, i, place_ref: (place_ref[1], rows(q, i), 0)),
                pl.BlockSpec((3, tile, c), lambda q, i, place_ref: (0, rows(q, i), 0))]

    in_specs, args = [], []
    for l, (s, recv) in enumerate(parts):
        in_specs += part_specs(l)
        args += [s, recv]
    blk = pl.BlockSpec((None, tile, c), lambda q, i, place_ref: (q, i, 0))
    out = jax.ShapeDtypeStruct((n_layers, r, c), F32)
    return pl.pallas_call(
        body,
        grid_spec=pltpu.PrefetchScalarGridSpec(
            num_scalar_prefetch=1, grid=(n_layers, nb), in_specs=in_specs + [blk, blk, blk],
            out_specs=[blk, blk, blk, blk]),
        out_shape=[out, out, out, out], compiler_params=_params("arbitrary", "arbitrary"), name=name)(
            place, *args, w, m, v)


def _adamw_small(repl_parts, vec_parts, w, m, v, place):
    n_r, n = len(repl_parts), len(w)
    shapes = [a.shape for a in w]

    def body(place_ref, *refs):
        parts, rest = refs[:n], refs[n:]
        for k in range(n):
            take = (lambda p: parts[k][p]) if k < n_r else (lambda p: parts[k][p, 0])
            g = take(0)
            for p in range(1, N_DEV):
                g = g + take(p)
            _adamw_update(g, rest[k], rest[n + k], rest[2 * n + k], *rest[3 * n + 4 * k:3 * n + 4 * k + 4])

    def whole(shape):
        return pl.BlockSpec(shape, lambda i, place_ref: (0,) * len(shape))

    def mine(shard):
        return pl.BlockSpec((N_DEV, 1) + shard, lambda i, place_ref: (0, place_ref[2]) + (0,) * len(shard))

    in_specs = [whole(a.shape) for a in repl_parts] + [mine(s) for s in shapes[n_r:]] + [whole(s) for s in shapes] * 3
    outs = pl.pallas_call(
        body,
        grid_spec=pltpu.PrefetchScalarGridSpec(
            num_scalar_prefetch=1, grid=(1,), in_specs=in_specs,
            out_specs=[whole(s) for s in shapes for _ in range(4)]),
        out_shape=[jax.ShapeDtypeStruct(s, F32) for s in shapes for _ in range(4)],
        compiler_params=_params("arbitrary"), name="adamw_small")(place, *repl_parts, *vec_parts, *w, *m, *v)
    return [outs[4 * k:4 * k + 4] for k in range(n)]


SHARDED = {"norm_w": 2, "w_in_even": 2, "gla_w_a_up": 2, "w_out_even": 1, "w_in_odd": 2, "conv_w": 2, "conv_b": 1,
           "lru_b_a": 1, "lru_b_x": 1, "lru_lambda": 1, "w_out_odd": 1, "w_mlp_up": 2, "w_mlp_down": 1}
REPLICATED = ["gla_b_a", "gla_norm_w", "fox_b_f", "rel_bias", "lru_w_a", "lru_w_x"]
WEIGHTS = ["norm_w", "w_in_even", "gla_w_a_up", "gla_b_a", "gla_norm_w", "fox_b_f", "w_out_even", "w_in_odd",
           "rel_bias", "conv_w", "conv_b", "lru_w_a", "lru_b_a", "lru_w_x", "lru_b_x", "lru_lambda", "w_out_odd",
           "w_mlp_up", "w_mlp_down"]
MATRICES = ("w_in_even", "w_out_even", "w_in_odd", "w_out_odd", "w_mlp_up", "w_mlp_down")
TRANSPOSED = ("w_in_even", "w_in_odd")
VECTORS = tuple(n for n in SHARDED if n not in MATRICES)
MATRIX_BLOCKS = (("w_in_even", 0), ("w_out_even", 0), ("w_in_odd", 0), ("w_out_odd", 0),
                 ("w_mlp_up", 0), ("w_mlp_up", 1), ("w_mlp_down", 0), ("w_mlp_down", 1))


def _join_shards(blocks, axis):
    moved = jnp.moveaxis(blocks, 0, axis)
    shape = moved.shape
    return moved.reshape(shape[:axis] + (shape[axis] * shape[axis + 1],) + shape[axis + 2:])


def _split_shards(full, axis):
    shape = full.shape
    cut = full.reshape(shape[:axis] + (N_DEV, shape[axis] // N_DEV) + shape[axis + 1:])
    return jnp.moveaxis(cut, axis, 0)


EVEN_SPLITS = (0, 256, 512, 1024, 1536, 1552, 2064, 2576, 3088, 3096)


def _even_in_split(wt):
    c = [wt[EVEN_SPLITS[k]:EVEN_SPLITS[k + 1]] for k in range(9)]
    gq, gk, gv, gr, ga, fq, fk, fv, ff = c
    padrows = lambda a: jnp.pad(a, ((0, LANES - a.shape[0]), (0, 0)))
    return jnp.concatenate([gq, gk, gv, fq, fk, fv], axis=0), jnp.concatenate([gr, padrows(ga), padrows(ff)], axis=0)


def _even_in_merge(dmm, dele):
    return jnp.concatenate([dmm[:1024], dele[:512], dele[512:512 + GLA_RANK], dmm[1024:2560],
                            dele[640:640 + ATT_HEADS]], axis=0)


def _forward_backward(x, target, shard, vec_shard, w, place):
    w = dict(w)
    g, dnorm, sums, recv = {}, {}, {}, {}
    nrm = lambda l, k: w["norm_w"][l, k][None, :]
    gather = lambda *keys: _gather_plan([shard[k] for k in keys])
    blocks = lambda r, c: (N_DEV, r // N_DEV, c)

    def pair_sum(key):
        sums[key] = _pair_sum(g[key], got[key], place, f"rs_pair_sum_{key[0]}_{key[1]}")

    got = {}

    def mlp_fwd(xin, layer, ride_up, ride_down):
        h = _norm_fwd(xin, nrm(layer, 2), out_dtype=ACT_DTYPE, name=f"norm_mlp_{layer}")
        u = _mm(h, w["w_mlp_up"][layer], out_dtype=ACT_DTYPE, tm=TM_FWD, tn=D_FF // N_DEV, b_blocked=True,
                name=f"mlp_up_{layer}", ride=ride_up)
        u, rode_up = u if ride_up is not None else (u, None)
        if w["w_mlp_down"][layer] is None:
            w["w_mlp_down"][layer] = rode_up[0].reshape(D_FF, D_MODEL)
        yv = _mm(u, w["w_mlp_down"][layer], out_dtype=F32, tm=TM_DX, tn=TN, a_sqrelu=True,
                 name=f"mlp_down_{layer}", ride=ride_down)
        yv, rode_down = yv if ride_down is not None else (yv, None)
        xout = _norm_fwd(yv, nrm(layer, 3), out_dtype=F32, res=xin, name=f"norm_mlp_out_{layer}")
        return xout, (xin, h, u, yv), rode_up, rode_down

    def mlp_bwd(dxout, saved, layer, ride):
        xin, h, u, yv = saved
        k_up, k_down = ("w_mlp_up", layer), ("w_mlp_down", layer)
        dy, dnorm[(layer, 3)] = _norm_bwd(dxout, yv, nrm(layer, 3), out_dtype=ACT_DTYPE, name=f"norm_mlp_out_bwd_{layer}")
        du = _mm(dy, w["w_mlp_down"][layer], nt=True, out_dtype=ACT_DTYPE, tm=TM_DX, tn=TN, drelu_of=u,
                 name=f"mlp_down_dx_{layer}", ride=ride)
        rode = None
        if ride is not None:
            du, rode = du
        g[k_down] = _mm(u, dy, ta=True, out_dtype=WIRE_DTYPE, tm=TM_DW, tn=TN, a_sqrelu=True,
                        name=f"mlp_down_dw_{layer}").reshape(blocks(D_FF, D_MODEL))
        g[k_up] = _mm(h, du, ta=True, out_dtype=WIRE_DTYPE, tm=TM_DW, tn=D_FF // N_DEV, out_blocked=True,
                      name=f"mlp_up_dw_{layer}")
        w_up = jnp.moveaxis(w["w_mlp_up"][layer], 0, 1).reshape(D_MODEL, D_FF)
        dh, (got[k_down], got[k_up]) = _mm(du, w_up, nt=True, out_dtype=F32, tm=TM_DX, tn=TN, name=f"mlp_up_dx_{layer}",
                                           ride=_sibling_plan([g[k_down], g[k_up]]))
        pair_sum(k_down)
        pair_sum(k_up)
        dxin, dnorm[(layer, 2)] = _norm_bwd(dh, xin, nrm(layer, 2), out_dtype=F32, add=dxout, name=f"norm_mlp_bwd_{layer}")
        return dxin, rode

    first = _run_plan(_gather_plan([shard[("w_in_even", 0)]] + [vec_shard[n] for n in VECTORS]),
                      "weights_all_gather_first")
    w["w_in_even"] = first[0].reshape(-1, D_MODEL)
    for n, b in zip(VECTORS, first[1:]):
        w[n] = _join_shards(b, SHARDED[n])
    w["w_mlp_up"], w["w_mlp_down"] = [None] * DEPTH, [None] * DEPTH

    wmm_e, wel_e = _even_in_split(w["w_in_even"])
    w_up_pad = jnp.pad(w["gla_w_a_up"][0], ((0, LANES - GLA_RANK), (0, 0)))
    b_f_pad = jnp.pad(w["fox_b_f"], ((0, 0), (0, LANES - ATT_HEADS)))
    h0 = _norm_fwd(x, nrm(0, 0), out_dtype=ACT_DTYPE, name="norm_in_0")
    pmm0, (w_out_even,) = _mm(h0, wmm_e, nt=True, out_dtype=ACT_DTYPE, tm=TM_FWD, tn=TN, name="in_even_mm",
                              ride=gather(("w_out_even", 0)))
    pel0 = _mm(h0, wel_e, nt=True, out_dtype=F32, tm=TM_FWD, tn=768, name="in_even_el")
    out_a, states = _gla_fwd(pmm0, pel0, w_up_pad, w["gla_b_a"], w["gla_norm_w"])
    cum, cum_t = _fox_gate_fwd(pel0, b_f_pad)
    (out_b, lse_b), (w["w_mlp_up"][0], w_mlp_down0) = _fox_fwd(pmm0, cum, cum_t,
                                                               ride=gather(("w_mlp_up", 0), ("w_mlp_down", 0)))
    w["w_out_even"] = w_out_even.reshape(D_MODEL, D_MODEL)
    w["w_mlp_down"][0] = w_mlp_down0.reshape(D_FF, D_MODEL)
    mix_in0 = jnp.concatenate([out_a, out_b], axis=1)
    mix0 = _mm(mix_in0, w["w_out_even"], out_dtype=F32, tm=TM_FWD, tn=TN, name="out_even")
    x1 = _norm_fwd(mix0, nrm(0, 1), out_dtype=F32, res=x, name="norm_mix_0")
    x2, mlp0, _, (w_in_odd,) = mlp_fwd(x1, 0, None, gather(("w_in_odd", 0)))
    w["w_in_odd"] = w_in_odd.reshape(-1, D_MODEL)

    w_in_o = w["w_in_odd"]
    n_mm_o = 3 * GROUP_WIDTH
    wa_bd, wx_bd = _block_diag_pairs(w["lru_w_a"][0]), _block_diag_pairs(w["lru_w_x"][0])
    base = _ca_bias_base(w["rel_bias"][0])
    h1 = _norm_fwd(x2, nrm(1, 0), out_dtype=ACT_DTYPE, name="norm_in_1")
    pmm1 = _mm(h1, w_in_o[:n_mm_o], nt=True, out_dtype=ACT_DTYPE, tm=TM_FWD, tn=TN, name="in_odd_mm")
    pel1 = _mm(h1, w_in_o[n_mm_o:], nt=True, out_dtype=F32, tm=TM_FWD, tn=TN, name="in_odd_el")
    kp = jnp.pad(pmm1[:, GROUP_WIDTH:2 * GROUP_WIDTH], ((CA_LEFT, 0), (0, 0)))
    vp = jnp.pad(pmm1[:, 2 * GROUP_WIDTH:], ((CA_LEFT, 0), (0, 0)))
    (out_c, lse_c), (w["w_mlp_up"][1],) = _ca_fwd(pmm1, kp, vp, base, ride=gather(("w_mlp_up", 1)))
    lru_args = (pel1, w["conv_w"][0], w["conv_b"], wa_bd, w["lru_b_a"], wx_bd, w["lru_b_x"], w["lru_lambda"])
    out_d, (w_out_odd,) = _lru_fwd(*lru_args, ride=gather(("w_out_odd", 0)))
    w["w_out_odd"] = w_out_odd.reshape(D_MODEL, D_MODEL)
    mix_in1 = jnp.concatenate([out_c, out_d], axis=1)
    mix1 = _mm(mix_in1, w["w_out_odd"], out_dtype=F32, tm=TM_FWD, tn=TN, name="out_odd")
    x3 = _norm_fwd(mix1, nrm(1, 1), out_dtype=F32, res=x2, name="norm_mix_1")
    x4, mlp1, _, _ = mlp_fwd(x3, 1, gather(("w_mlp_down", 1)), None)

    loss, dx4 = _loss_fwd_bwd(x4, target)

    k_oo, k_io, k_oe, k_ie = ("w_out_odd", 0), ("w_in_odd", 0), ("w_out_even", 0), ("w_in_even", 0)
    mlp_keys = lambda l: [("w_mlp_down", l), ("w_mlp_up", l)]
    dx3, _ = mlp_bwd(dx4, mlp1, 1, None)
    dmix1, dnorm[(1, 1)] = _norm_bwd(dx3, mix1, nrm(1, 1), out_dtype=ACT_DTYPE, name="norm_mix_bwd_1")
    g[k_oo] = _mm(mix_in1, dmix1, ta=True, out_dtype=WIRE_DTYPE, tm=TM_DW, tn=TN, name="out_odd_dw").reshape(
        blocks(D_MODEL, D_MODEL))
    dmix_in1, (got[k_oo],) = _mm(dmix1, w["w_out_odd"], nt=True, out_dtype=F32, tm=TM_DX, tn=TN, name="out_odd_dx",
                                 ride=_sibling_plan([g[k_oo]]))
    (dq_c, dkp, dvp, dbase), rode = _ca_bwd(pmm1, kp, vp, base, lse_c, dmix_in1,
                                            ride=_chip_plan([sums[k] for k in mlp_keys(1)]))
    recv.update(zip(mlp_keys(1), rode))
    pair_sum(k_oo)
    (dgate, dxin, g_conv_w, g_conv_b, dwa_bd, g_lru_b_a, dwx_bd, g_lru_b_x, g_lru_lambda), (recv[k_oo],) = _lru_bwd(
        *lru_args, dmix_in1, ride=_chip_plan([sums[k_oo]]))
    dp1 = jnp.concatenate([dq_c, dkp[CA_LEFT:].astype(ACT_DTYPE), dvp[CA_LEFT:].astype(ACT_DTYPE), dgate, dxin], axis=1)
    g[k_io] = _mm(dp1, h1, ta=True, out_dtype=WIRE_DTYPE, tm=dp1.shape[1] // 2, tn=TN, name="in_odd_dw").reshape(
        blocks(dp1.shape[1], D_MODEL))
    dh1, (got[k_io],) = _mm(dp1, w_in_o, out_dtype=F32, tm=TM_DX, tn=TN, name="in_odd_dx",
                            ride=_sibling_plan([g[k_io]]))
    pair_sum(k_io)
    dx2, dnorm[(1, 0)] = _norm_bwd(dh1, x2, nrm(1, 0), out_dtype=F32, add=dx3, name="norm_in_bwd_1")
    g["rel_bias"] = _ca_bias_base_grad(dbase)[None]
    g["conv_w"], g["conv_b"] = g_conv_w[None], g_conv_b
    g["lru_w_a"], g["lru_w_x"] = _block_diag_pairs_grad(dwa_bd)[None], _block_diag_pairs_grad(dwx_bd)[None]
    g["lru_b_a"], g["lru_b_x"], g["lru_lambda"] = g_lru_b_a, g_lru_b_x, g_lru_lambda

    dx1, (recv[k_io],) = mlp_bwd(dx2, mlp0, 0, _chip_plan([sums[k_io]]))
    dmix0, dnorm[(0, 1)] = _norm_bwd(dx1, mix0, nrm(0, 1), out_dtype=ACT_DTYPE, name="norm_mix_bwd_0")
    g[k_oe] = _mm(mix_in0, dmix0, ta=True, out_dtype=WIRE_DTYPE, tm=TM_DW, tn=TN, name="out_even_dw").reshape(
        blocks(D_MODEL, D_MODEL))
    dmix_in0, (got[k_oe],) = _mm(dmix0, w["w_out_even"], nt=True, out_dtype=F32, tm=TM_DX, tn=TN, name="out_even_dx",
                                 ride=_sibling_plan([g[k_oe]]))
    k_md0, k_mu0 = mlp_keys(0)
    pair_sum(k_oe)
    dq_a, dk_a, dv_a, dr_a, da_a, dw_up_pad, g_gla_b_a, g_gla_norm_w = _gla_bwd(
        pmm0, pel0, w_up_pad, w["gla_b_a"], w["gla_norm_w"], states, dmix_in0)
    (dq_b, dk_b, dv_b, dcum_t, dcum_q), (recv[k_md0], recv[k_mu0], recv[k_oe]) = _fox_bwd(
        pmm0, cum, cum_t, lse_b, dmix_in0, ride=_chip_plan([sums[k_md0], sums[k_mu0], sums[k_oe]]))
    df_b, db_f = _fox_gate_bwd(pel0, b_f_pad, dcum_t, dcum_q)
    g["gla_w_a_up"] = dw_up_pad[:GLA_RANK][None]
    g["gla_b_a"], g["gla_norm_w"], g["fox_b_f"] = g_gla_b_a, g_gla_norm_w, db_f[:, :ATT_HEADS]
    dp0 = jnp.concatenate([dq_a, dk_a, dv_a, dq_b, dk_b.astype(ACT_DTYPE), dv_b.astype(ACT_DTYPE), dr_a, da_a, df_b],
                          axis=1)
    w_perm = jnp.concatenate([wmm_e, wel_e], axis=0)
    n_mm_e = wmm_e.shape[0]
    dw_perm, repl_parts = _mm(dp0, h0, ta=True, out_dtype=WIRE_DTYPE, tm=dp0.shape[1] // 2, tn=TN, name="in_even_dw",
                              ride=_gather_plan([g[n] for n in REPLICATED]))
    dw_even = _even_in_merge(dw_perm[:n_mm_e], dw_perm[n_mm_e:])
    g[k_ie] = dw_even.reshape(blocks(dw_even.shape[0], D_MODEL))
    dh0, (got[k_ie],) = _mm(dp0, w_perm, out_dtype=F32, tm=TM_DX, tn=TN, name="in_even_dx",
                            ride=_sibling_plan([g[k_ie]]))
    pair_sum(k_ie)
    (dx0, dnorm[(0, 0)]), (recv[k_ie],) = _norm_bwd(dh0, x, nrm(0, 0), out_dtype=F32, add=dx1, name="norm_in_bwd_0",
                                                     ride=_chip_plan([sums[k_ie]]))

    g["norm_w"] = jnp.stack([jnp.concatenate([dnorm[(l, k)] for k in range(4)], axis=0) for l in range(DEPTH)])
    vec_parts = _run_plan(_gather_plan([_split_shards(g[n], SHARDED[n]) for n in VECTORS]), "vector_grads_all_gather")
    return loss, dx0, sums, recv, repl_parts, vec_parts


def kernel(x, norm_w, w_in_even, gla_w_a_up, gla_b_a, gla_norm_w, fox_b_f, w_out_even, w_in_odd, rel_bias, conv_w, conv_b, lru_w_a, lru_b_a, lru_w_x, lru_b_x, lru_lambda, w_out_odd, w_mlp_up, w_mlp_down, loss_target, m_norm_w, m_w_in_even, m_gla_w_a_up, m_gla_b_a, m_gla_norm_w, m_fox_b_f, m_w_out_even, m_w_in_odd, m_rel_bias, m_conv_w, m_conv_b, m_lru_w_a, m_lru_b_a, m_lru_w_x, m_lru_b_x, m_lru_lambda, m_w_out_odd, m_w_mlp_up, m_w_mlp_down, v_norm_w, v_w_in_even, v_gla_w_a_up, v_gla_b_a, v_gla_norm_w, v_fox_b_f, v_w_out_even, v_w_in_odd, v_rel_bias, v_conv_w, v_conv_b, v_lru_w_a, v_lru_b_a, v_lru_w_x, v_lru_b_x, v_lru_lambda, v_w_out_odd, v_w_mlp_up, v_w_mlp_down):
    wts = dict(zip(WEIGHTS, (norm_w, w_in_even, gla_w_a_up, gla_b_a, gla_norm_w, fox_b_f, w_out_even, w_in_odd, rel_bias,
                             conv_w, conv_b, lru_w_a, lru_b_a, lru_w_x, lru_b_x, lru_lambda, w_out_odd, w_mlp_up,
                             w_mlp_down)))
    mom = dict(zip(WEIGHTS, (m_norm_w, m_w_in_even, m_gla_w_a_up, m_gla_b_a, m_gla_norm_w, m_fox_b_f, m_w_out_even,
                             m_w_in_odd, m_rel_bias, m_conv_w, m_conv_b, m_lru_w_a, m_lru_b_a, m_lru_w_x, m_lru_b_x,
                             m_lru_lambda, m_w_out_odd, m_w_mlp_up, m_w_mlp_down)))
    var = dict(zip(WEIGHTS, (v_norm_w, v_w_in_even, v_gla_w_a_up, v_gla_b_a, v_gla_norm_w, v_fox_b_f, v_w_out_even,
                             v_w_in_odd, v_rel_bias, v_conv_w, v_conv_b, v_lru_w_a, v_lru_b_a, v_lru_w_x, v_lru_b_x,
                             v_lru_lambda, v_w_out_odd, v_w_mlp_up, v_w_mlp_down)))
    ax, ay, ac = lax.axis_index("x"), lax.axis_index("y"), lax.axis_index("c")
    place = jnp.stack([ac, 2 * ax + ay, 4 * ax + 2 * ay + ac]).astype(jnp.int32)

    shard = {(n, l): (wts[n][l].T if n in TRANSPOSED else wts[n][l]).astype(WIRE_DTYPE) for n, l in MATRIX_BLOCKS}
    loss_blk, dx, sums, recv, repl_parts, vec_parts = _forward_backward(
        x[0], loss_target[0], shard, {n: wts[n] for n in VECTORS}, {n: wts[n] for n in REPLICATED}, place)
    loss = lax.psum(loss_blk[0, 0], ("x", "y", "c"))

    view = lambda n, a: jnp.swapaxes(a, 1, 2) if n in TRANSPOSED else a
    upd = {n: [view(n, o) for o in _adamw_sharded(
        [(sums[(n, l)], recv[(n, l)]) for l in range(wts[n].shape[0])], view(n, wts[n]), view(n, mom[n]), view(n, var[n]),
        place, f"adamw_{n}")] for n in MATRICES}
    small = REPLICATED + list(VECTORS)
    upd.update(zip(small, _adamw_small(repl_parts, vec_parts, [wts[n] for n in small], [mom[n] for n in small],
                                       [var[n] for n in small], place)))
    return (loss, dx[None], *[upd[n][kind] for kind in range(4) for n in WEIGHTS])
```

```python
import functools
from typing import Callable, NamedTuple

import jax
import jax.numpy as jnp
from jax import lax
from jax.experimental import pallas as pl
from jax.experimental.pallas import tpu as pltpu

F32 = jnp.float32
MXU_DTYPE = jnp.bfloat16
ACT_DTYPE = jnp.bfloat16
WIRE_DTYPE = jnp.bfloat16

V7X_VMEM_BYTES = 64 * 1024 * 1024
VMEM_LIMIT = (V7X_VMEM_BYTES * 7) // 8
LANES = 128

D_MODEL = 1024
SEQ = 2048
DEPTH = 2
CHUNK = 64
GROUP_WIDTH = D_MODEL // 2
D_FF = 4 * D_MODEL
NORM_EPS = 1e-6
GLA_HEADS = 4
GLA_DV = GROUP_WIDTH // GLA_HEADS
GLA_DK = GLA_DV // 2
GLA_KW = GLA_HEADS * GLA_DK
GLA_RANK = 16
GLA_GATE_TAU = 16.0
HEAD_DIM = 64
ATT_HEADS = GROUP_WIDTH // HEAD_DIM
CA_LEFT = 8 * CHUNK
REL_CLIP = 128
LRU_BLOCK_DIM = 64
CONV_WIDTH = 4
LRU_C = 8.0
N_DEV = 8

ADAM_LR = 0.001
ADAM_B1 = 0.9
ADAM_B2 = 0.999
ADAM_EPS = 1e-08
ADAM_WD = 0.01
ADAM_STEP = 10

NEG = float(jnp.finfo(jnp.float32).min)
MESH = pl.DeviceIdType.MESH


def _params(*sem):
    return pltpu.CompilerParams(dimension_semantics=sem, vmem_limit_bytes=VMEM_LIMIT)


def _dot(a, b, ca=1, cb=0):
    return lax.dot_general(a.astype(MXU_DTYPE), b.astype(MXU_DTYPE), (((ca,), (cb,)), ((), ())),
                           preferred_element_type=F32)


def _dot_exact(a, b):
    return lax.dot_general(a, b, (((1,), (0,)), ((), ())), precision=lax.Precision.HIGHEST,
                           preferred_element_type=F32)


def _log_sigmoid(x):
    return jnp.minimum(x, 0.0) - jnp.log1p(jnp.exp(-jnp.abs(x)))


def _iota(shape, axis):
    return lax.broadcasted_iota(jnp.int32, shape, axis)


ANY = pl.BlockSpec(memory_space=pl.ANY)
N_CHIPS = 4


class _Plan(NamedTuple):
    ins: list
    outs: list
    sems: list
    start: Callable
    finish: Callable


def _place():
    x, y, c = lax.axis_index("x"), lax.axis_index("y"), lax.axis_index("c")
    return x, y, c, [(1 - x, y), (x, 1 - y), (1 - x, 1 - y)]


def _gather_plan(xs):
    n = len(xs)

    def parts(x_refs, out_refs, sems):
        send_sems, recv_sems, local_sems = sems
        x, y, c, chips = _place()
        me, sibling = (x, y, c), (x, y, 1 - c)

        def rows(a, px, py, pc):
            return out_refs[a].at[4 * px + 2 * py + pc]

        def copy(a, k, block, to, src=None):
            return pltpu.make_async_remote_copy(
                src_ref=rows(a, *block) if src is None else src, dst_ref=rows(a, *block),
                send_sem=send_sems.at[7 * a + k], recv_sem=recv_sems.at[7 * a + k], device_id=to, device_id_type=MESH)

        mine = [pltpu.make_async_copy(x_refs[a], rows(a, *me), local_sems.at[a]) for a in range(n)]
        first = []
        for a in range(n):
            first.append(copy(a, 0, me, sibling, src=x_refs[a]))
            first += [copy(a, 1 + j, me, (*chip, c), src=x_refs[a]) for j, chip in enumerate(chips)]
        return c, me, sibling, chips, copy, mine, first

    def start(x_refs, out_refs, sems):
        *_, mine, first = parts(x_refs, out_refs, sems)
        for cp in first + mine:
            cp.start()

    def finish(x_refs, out_refs, sems):
        c, me, sibling, chips, copy, mine, first = parts(x_refs, out_refs, sems)
        passed = []
        for j, chip in enumerate(chips):
            for a in range(n):
                copy(a, 1 + j, (*chip, c), me).wait_recv()
                passed.append(copy(a, 4 + j, (*chip, c), sibling))
                passed[-1].start()
        for a in range(n):
            copy(a, 0, sibling, me).wait_recv()
            for j, chip in enumerate(chips):
                copy(a, 4 + j, (*chip, 1 - c), me).wait_recv()
        for cp in first + passed:
            cp.wait_send()
        for cp in mine:
            cp.wait()

    return _Plan(list(xs), [jax.ShapeDtypeStruct((N_DEV,) + x.shape, x.dtype) for x in xs],
                 [pltpu.SemaphoreType.DMA((7 * n,)), pltpu.SemaphoreType.DMA((7 * n,)), pltpu.SemaphoreType.DMA((n,))],
                 start, finish)


def _exchange_plan(copies_of, ins, outs, per_array):
    n = len(ins)

    def start(in_refs, out_refs, sems):
        for cp in copies_of(in_refs, out_refs, sems):
            cp.start()

    def finish(in_refs, out_refs, sems):
        copies = copies_of(in_refs, out_refs, sems)
        for cp in copies:
            cp.wait_recv()
        for cp in copies:
            cp.wait_send()

    return _Plan(list(ins), outs, [pltpu.SemaphoreType.DMA((per_array * n,)), pltpu.SemaphoreType.DMA((per_array * n,))],
                 start, finish)


def _sibling_plan(gs):
    def copies_of(g_refs, got_refs, sems):
        x, y, c, _ = _place()
        return [pltpu.make_async_remote_copy(
            src_ref=g_refs[a].at[2 * k + (1 - c)], dst_ref=got_refs[a].at[k], send_sem=sems[0].at[N_CHIPS * a + k],
            recv_sem=sems[1].at[N_CHIPS * a + k], device_id=(x, y, 1 - c), device_id_type=MESH)
            for a in range(len(gs)) for k in range(N_CHIPS)]

    return _exchange_plan(copies_of, gs, [jax.ShapeDtypeStruct((N_CHIPS,) + g.shape[1:], g.dtype) for g in gs], N_CHIPS)


def _chip_plan(ss):
    def copies_of(s_refs, out_refs, sems):
        x, y, c, chips = _place()
        return [pltpu.make_async_remote_copy(
            src_ref=s_refs[a].at[2 * px + py], dst_ref=out_refs[a].at[j], send_sem=sems[0].at[3 * a + j],
            recv_sem=sems[1].at[3 * a + j], device_id=(px, py, c), device_id_type=MESH)
            for a in range(len(ss)) for j, (px, py) in enumerate(chips)]

    return _exchange_plan(copies_of, ss, [jax.ShapeDtypeStruct((3,) + s.shape[1:], s.dtype) for s in ss], 3)


def _run_plan(plan, name):
    n_in, n_out = len(plan.ins), len(plan.outs)

    def body(*refs):
        args = refs[:n_in], refs[n_in:n_in + n_out], refs[n_in + n_out:]
        plan.start(*args)
        plan.finish(*args)

    return pl.pallas_call(body, out_shape=plan.outs, in_specs=[ANY] * n_in, out_specs=[ANY] * n_out,
                          scratch_shapes=plan.sems, name=name)(*plan.ins)


def _pcall(body, ride, *, grid, in_specs, out_specs, out_shape, scratch_shapes=(), semantics, name):
    if ride is None:
        return pl.pallas_call(body, grid=grid, in_specs=in_specs, out_specs=out_specs, out_shape=out_shape,
                              scratch_shapes=list(scratch_shapes), compiler_params=_params(*semantics), name=name)
    single = not isinstance(out_shape, (list, tuple))
    out_specs_l, out_shape_l = ([out_specs], [out_shape]) if single else (list(out_specs), list(out_shape))
    n_in, n_out, n_scr = len(in_specs), len(out_shape_l), len(scratch_shapes)
    r_in, r_out = len(ride.ins), len(ride.outs)

    def riding(*refs):
        cuts = [n_in, r_in, n_out, r_out, n_scr]
        groups, at = [], 0
        for width in cuts:
            groups.append(refs[at:at + width])
            at += width
        ins, r_ins, outs, r_outs, scr = groups
        sems = refs[at:]
        first = functools.reduce(jnp.logical_and, [pl.program_id(d) == 0 for d in range(len(grid))])
        last = functools.reduce(jnp.logical_and, [pl.program_id(d) == grid[d] - 1 for d in range(len(grid))])

        @pl.when(first)
        def _():
            ride.start(r_ins, r_outs, sems)

        body(*ins, *outs, *scr)

        @pl.when(last)
        def _():
            ride.finish(r_ins, r_outs, sems)

    call = pl.pallas_call(
        riding, grid=grid, in_specs=list(in_specs) + [ANY] * r_in, out_specs=out_specs_l + [ANY] * r_out,
        out_shape=out_shape_l + list(ride.outs), scratch_shapes=list(scratch_shapes) + list(ride.sems),
        compiler_params=_params(*(["arbitrary"] * len(grid))), name=name)

    def run(*args):
        res = call(*args, *ride.ins)
        return (res[0] if single else list(res[:n_out])), list(res[n_out:])

    return run


def _mm(a, b, *, nt=False, ta=False, out_dtype, tm, tn, a_sqrelu=False, drelu_of=None, b_blocked=False,
        out_blocked=False, name, ride=None):
    k, m = a.shape if ta else a.shape[::-1]
    if b_blocked:
        assert not nt and b.shape[1] == k and b.shape[2] == tn
        n = b.shape[0] * tn
    else:
        n = b.shape[0] if nt else b.shape[1]
        assert (b.shape[1] if nt else b.shape[0]) == k
    tm, tn = min(tm, m), min(tn, n)
    assert m % tm == 0 and n % tn == 0

    def body(*refs):
        a_ref, b_ref = refs[0], refs[1]
        o_ref = refs[-1]
        av = a_ref[...]
        if a_sqrelu:
            av = jnp.square(jnp.maximum(av.astype(F32), 0.0))
        acc = _dot(av, b_ref[...], 0 if ta else 1, 1 if nt else 0)
        if drelu_of is not None:
            acc = acc * (2.0 * jnp.maximum(refs[2][...].astype(F32), 0.0))
        o_ref[...] = acc.astype(out_dtype)

    if b_blocked:
        b_spec = pl.BlockSpec((None, k, tn), lambda i, j: (j, 0, 0))
    elif nt:
        b_spec = pl.BlockSpec((tn, k), lambda i, j: (j, 0))
    else:
        b_spec = pl.BlockSpec((k, tn), lambda i, j: (0, j))
    a_spec = pl.BlockSpec((k, tm), lambda i, j: (0, i)) if ta else pl.BlockSpec((tm, k), lambda i, j: (i, 0))
    in_specs = [a_spec, b_spec]
    args = [a, b]
    if drelu_of is not None:
        in_specs.append(pl.BlockSpec((tm, tn), lambda i, j: (i, j)))
        args.append(drelu_of)
    if out_blocked:
        out_spec = pl.BlockSpec((None, tm, tn), lambda i, j: (j, i, 0))
        out_shape = jax.ShapeDtypeStruct((n // tn, m, tn), out_dtype)
    else:
        out_spec = pl.BlockSpec((tm, tn), lambda i, j: (i, j))
        out_shape = jax.ShapeDtypeStruct((m, n), out_dtype)
    return _pcall(body, ride, grid=(m // tm, n // tn), in_specs=in_specs, out_specs=out_spec, out_shape=out_shape,
                  semantics=("parallel", "parallel"), name=name)(*args)


def _mm_nt_blocked(a, b, *, out_dtype, tm, tn, name):
    m = a.shape[0]
    p, n, kp = b.shape
    assert a.shape[1] == p * kp and m % tm == 0 and n % tn == 0

    def body(a_ref, b_ref, o_ref, acc_ref):
        @pl.when(pl.program_id(2) == 0)
        def _():
            acc_ref[...] = jnp.zeros_like(acc_ref)

        acc_ref[...] += _dot(a_ref[...], b_ref[...], 1, 1)

        @pl.when(pl.program_id(2) == p - 1)
        def _():
            o_ref[...] = acc_ref[...].astype(out_dtype)

    return pl.pallas_call(
        body, grid=(m // tm, n // tn, p),
        in_specs=[pl.BlockSpec((tm, kp), lambda i, j, q: (i, q)), pl.BlockSpec((None, tn, kp), lambda i, j, q: (q, j, 0))],
        out_specs=pl.BlockSpec((tm, tn), lambda i, j, q: (i, j)),
        out_shape=jax.ShapeDtypeStruct((m, n), out_dtype),
        scratch_shapes=[pltpu.VMEM((tm, tn), F32)],
        compiler_params=_params("parallel", "parallel", "arbitrary"), name=name)(a, b)


ROW_TILE = 512
TM_FWD, TM_DX, TM_DW, TN = 2048, 1024, 1024, 512


def _norm_fwd(x, w, *, out_dtype, res=None, name):
    t, d = x.shape

    def body(*refs):
        x_ref, w_ref, o_ref = refs[0], refs[1], refs[-1]
        xv = x_ref[...]
        y = xv * lax.rsqrt(jnp.mean(xv * xv, axis=-1, keepdims=True) + NORM_EPS) * w_ref[...]
        if res is not None:
            y = refs[2][...] + y
        o_ref[...] = y.astype(out_dtype)

    row = pl.BlockSpec((ROW_TILE, d), lambda i: (i, 0))
    in_specs = [row, pl.BlockSpec((1, d), lambda i: (0, 0))] + ([row] if res is not None else [])
    args = [x, w] + ([res] if res is not None else [])
    return pl.pallas_call(body, grid=(t // ROW_TILE,), in_specs=in_specs, out_specs=row,
                          out_shape=jax.ShapeDtypeStruct((t, d), out_dtype),
                          compiler_params=_params("parallel"), name=name)(*args)


def _norm_bwd(dy, x, w, *, out_dtype, add=None, name, ride=None):
    t, d = x.shape

    def body(*refs):
        dy_ref, x_ref, w_ref = refs[0], refs[1], refs[2]
        dx_ref, dw_ref = refs[-2], refs[-1]
        xv = x_ref[...]
        rstd = lax.rsqrt(jnp.mean(xv * xv, axis=-1, keepdims=True) + NORM_EPS)
        xhat = xv * rstd
        dyv = dy_ref[...].astype(F32)
        g = dyv * w_ref[...]
        dx = rstd * (g - xhat * jnp.mean(g * xhat, axis=-1, keepdims=True))
        if add is not None:
            dx = dx + refs[3][...]
        dx_ref[...] = dx.astype(out_dtype)

        @pl.when(pl.program_id(0) == 0)
        def _():
            dw_ref[...] = jnp.zeros_like(dw_ref)

        dw_ref[...] += jnp.sum(dyv * xhat, axis=0, keepdims=True)

    row = pl.BlockSpec((ROW_TILE, d), lambda i: (i, 0))
    vec = pl.BlockSpec((1, d), lambda i: (0, 0))
    in_specs = [row, row, vec] + ([row] if add is not None else [])
    args = [dy, x, w] + ([add] if add is not None else [])
    return _pcall(body, ride, grid=(t // ROW_TILE,), in_specs=in_specs, out_specs=[row, vec],
                  out_shape=[jax.ShapeDtypeStruct((t, d), out_dtype), jax.ShapeDtypeStruct((1, d), F32)],
                  semantics=("arbitrary",), name=name)(*args)


def _loss_fwd_bwd(y, target):
    t, d = y.shape

    def body(y_ref, t_ref, l_ref, dy_ref):
        diff = y_ref[...] - t_ref[...]
        dy_ref[...] = diff * (1.0 / d)

        @pl.when(pl.program_id(0) == 0)
        def _():
            l_ref[...] = jnp.zeros_like(l_ref)

        l_ref[...] += 0.5 * jnp.sum(jnp.mean(diff * diff, axis=-1, keepdims=True), axis=0, keepdims=True)

    row = pl.BlockSpec((ROW_TILE, d), lambda i: (i, 0))
    return pl.pallas_call(body, grid=(t // ROW_TILE,), in_specs=[row, row],
                          out_specs=[pl.BlockSpec((8, LANES), lambda i: (0, 0)), row],
                          out_shape=[jax.ShapeDtypeStruct((8, LANES), F32), jax.ShapeDtypeStruct((t, d), F32)],
                          compiler_params=_params("arbitrary"), name="loss")(y, target)


GLA_UNROLL = (8, 4)


def _gla_specs(t):
    return [pl.BlockSpec((t, GLA_KW), lambda i: (0, 0)),
            pl.BlockSpec((t, GLA_KW), lambda i: (0, 1)),
            pl.BlockSpec((t, GROUP_WIDTH), lambda i: (0, 1)),
            pl.BlockSpec((t, GROUP_WIDTH), lambda i: (0, 0)),
            pl.BlockSpec((t, LANES), lambda i: (0, 4)),
            pl.BlockSpec((LANES, GLA_KW), lambda i: (0, 0)),
            pl.BlockSpec((1, GLA_KW), lambda i: (0, 0)),
            pl.BlockSpec((1, GROUP_WIDTH), lambda i: (0, 0))]


def _gla_decays(e_scr, dec_scr, nc):
    tri = (_iota((CHUNK, CHUNK), 1) <= _iota((CHUNK, CHUNK), 0)).astype(F32)

    def one(c, carry):
        rows = pl.ds(pl.multiple_of(c * CHUNK, CHUNK), CHUNK)
        cum = _dot_exact(tri, e_scr[rows, :])
        tot = cum[CHUNK - 1:CHUNK, :]
        e_scr[rows, :] = jnp.exp(tot - cum)
        dec_scr[c] = jnp.broadcast_to(jnp.exp(tot), (8, GLA_KW))
        return carry

    lax.fori_loop(0, nc, one, 0, unroll=GLA_UNROLL[0])


def _gla_fwd(pmm, pel, w_up, b_a, gnorm_w, ride=None):
    t = pmm.shape[0]
    nc = t // CHUNK
    scale = GLA_DK ** -0.5

    def body(q_ref, k_ref, v_ref, r_ref, a_ref, wup_ref, ba_ref, gw_ref, o_ref, st_ref, e_scr, dec_scr, s_scr):
        z = _dot(a_ref[...], wup_ref[...]) + ba_ref[...]
        e_scr[...] = _log_sigmoid(z) * (1.0 / GLA_GATE_TAU)
        _gla_decays(e_scr, dec_scr, nc)
        s_scr[...] = jnp.zeros_like(s_scr)

        def chunk(c, carry):
            rows = pl.ds(pl.multiple_of(c * CHUNK, CHUNK), CHUNK)
            kd = k_ref[rows, :].astype(F32) * e_scr[rows, :]
            decay = dec_scr[c, 0:1, :]
            qs = q_ref[rows, :].astype(F32) * scale
            vv = v_ref[rows, :].astype(F32)
            rr = r_ref[rows, :]
            gate = rr * jax.nn.sigmoid(rr) * gw_ref[...]
            for h in range(GLA_HEADS):
                ks = slice(h * GLA_DK, (h + 1) * GLA_DK)
                vs = slice(h * GLA_DV, (h + 1) * GLA_DV)
                inc_t = _dot(vv[:, vs].T, kd[:, ks])
                s_t = s_scr[vs, :] * decay[:, ks] + inc_t
                s_scr[vs, :] = s_t
                st_ref[c, vs, :] = s_t
                o = _dot(qs[:, ks], s_t, 1, 1)
                y = o * lax.rsqrt(jnp.mean(o * o, axis=-1, keepdims=True) + NORM_EPS)
                o_ref[rows, vs] = (y * gate[:, vs]).astype(o_ref.dtype)
            return carry

        lax.fori_loop(0, nc, chunk, 0, unroll=GLA_UNROLL[0])

    return _pcall(
        body, ride, grid=(1,), in_specs=_gla_specs(t),
        out_specs=[pl.BlockSpec((t, GROUP_WIDTH), lambda i: (0, 0)),
                   pl.BlockSpec((nc, GLA_HEADS * GLA_DV, GLA_DK), lambda i: (0, 0, 0))],
        out_shape=[jax.ShapeDtypeStruct((t, GROUP_WIDTH), ACT_DTYPE),
                   jax.ShapeDtypeStruct((nc, GLA_HEADS * GLA_DV, GLA_DK), F32)],
        scratch_shapes=[pltpu.VMEM((t, GLA_KW), F32), pltpu.VMEM((nc, 8, GLA_KW), F32),
                        pltpu.VMEM((GLA_HEADS * GLA_DV, GLA_DK), F32)],
        semantics=("arbitrary",), name="gla_fwd")(pmm, pmm, pmm, pel, pel, w_up, b_a, gnorm_w)


def _gla_bwd(pmm, pel, w_up, b_a, gnorm_w, states, dmix, ride=None):
    t = pmm.shape[0]
    nc = t // CHUNK
    scale = GLA_DK ** -0.5

    def body(q_ref, k_ref, v_ref, r_ref, a_ref, wup_ref, ba_ref, gw_ref, st_ref, do_ref,
             dq_ref, dk_ref, dv_ref, dr_ref, da_ref, dwup_ref, dba_ref, dgw_ref,
             e_scr, dec_scr, dz_scr, g_scr, dd_scr, ds_scr):
        z = _dot(a_ref[...], wup_ref[...]) + ba_ref[...]
        e_scr[...] = _log_sigmoid(z) * (1.0 / GLA_GATE_TAU)
        dz_scr[...] = jax.nn.sigmoid(-z) * (1.0 / GLA_GATE_TAU)
        _gla_decays(e_scr, dec_scr, nc)
        ds_scr[...] = jnp.zeros_like(ds_scr)
        dgw_ref[...] = jnp.zeros_like(dgw_ref)

        def chunk(n, carry):
            c = nc - 1 - n
            rows = pl.ds(pl.multiple_of(c * CHUNK, CHUNK), CHUNK)
            e = e_scr[rows, :]
            kd = k_ref[rows, :].astype(F32) * e
            decay = dec_scr[c, 0:1, :]
            qs = q_ref[rows, :].astype(F32) * scale
            vv = v_ref[rows, :].astype(F32)
            rr = r_ref[rows, :]
            sig = jax.nn.sigmoid(rr)
            silu = rr * sig
            dsilu = sig * (1.0 + rr * (1.0 - sig))
            dout = do_ref[rows, :]
            gw = gw_ref[...]
            c_prev = jnp.maximum(c - 1, 0)
            has_prev = (c > 0).astype(F32)
            for h in range(GLA_HEADS):
                ks = slice(h * GLA_DK, (h + 1) * GLA_DK)
                vs = slice(h * GLA_DV, (h + 1) * GLA_DV)
                s_t = st_ref[c, vs, :]
                s_prev = st_ref[c_prev, vs, :] * has_prev
                o = _dot(qs[:, ks], s_t, 1, 1)
                rstd = lax.rsqrt(jnp.mean(o * o, axis=-1, keepdims=True) + NORM_EPS)
                y = o * rstd
                dg = dout[:, vs]
                dgw_ref[:, vs] += jnp.sum(dg * y * silu[:, vs], axis=0, keepdims=True)
                dr_ref[rows, vs] = (dg * y * gw[:, vs] * dsilu[:, vs]).astype(dr_ref.dtype)
                dy = dg * gw[:, vs] * silu[:, vs]
                d_o = rstd * (dy - y * jnp.mean(dy * y, axis=-1, keepdims=True))
                dq_ref[rows, ks] = (_dot(d_o, s_t) * scale).astype(dq_ref.dtype)
                ds_t = ds_scr[vs, :] + _dot(d_o.T, qs[:, ks])
                dv_ref[rows, vs] = _dot(kd[:, ks], ds_t, 1, 1).astype(dv_ref.dtype)
                dkd = _dot(vv[:, vs], ds_t)
                dd_scr[c, 0:1, ks] = jnp.sum(ds_t * s_prev, axis=0, keepdims=True) * decay[:, ks]
                ds_scr[vs, :] = ds_t * decay[:, ks]
                g_scr[rows, ks] = dkd * kd[:, ks]
                dk_ref[rows, ks] = (dkd * e[:, ks]).astype(dk_ref.dtype)
            return carry

        lax.fori_loop(0, nc, chunk, 0, unroll=GLA_UNROLL[1])

        tri_strict = (_iota((CHUNK, CHUNK), 1) < _iota((CHUNK, CHUNK), 0)).astype(F32)

        def gate_grad(c, carry):
            rows = pl.ds(pl.multiple_of(c * CHUNK, CHUNK), CHUNK)
            dla = dd_scr[c, 0:1, :] + _dot_exact(tri_strict, g_scr[rows, :])
            dz_scr[rows, :] = dla * dz_scr[rows, :]
            return carry

        lax.fori_loop(0, nc, gate_grad, 0, unroll=GLA_UNROLL[0])
        dz = dz_scr[...]
        da_ref[...] = _dot(dz, wup_ref[...], 1, 1).astype(da_ref.dtype)
        dwup_ref[...] = _dot(a_ref[...].T, dz)
        dba_ref[...] = jnp.sum(dz, axis=0, keepdims=True)

    in_specs = _gla_specs(t) + [
        pl.BlockSpec((nc, GLA_HEADS * GLA_DV, GLA_DK), lambda i: (0, 0, 0)),
        pl.BlockSpec((t, GROUP_WIDTH), lambda i: (0, 0))]
    full = lambda r, c: pl.BlockSpec((r, c), lambda i: (0, 0))
    return _pcall(
        body, ride, grid=(1,), in_specs=in_specs,
        out_specs=[full(t, GLA_KW), full(t, GLA_KW), full(t, GROUP_WIDTH), full(t, GROUP_WIDTH), full(t, LANES),
                   full(LANES, GLA_KW), full(1, GLA_KW), full(1, GROUP_WIDTH)],
        out_shape=[jax.ShapeDtypeStruct((t, GLA_KW), ACT_DTYPE), jax.ShapeDtypeStruct((t, GLA_KW), ACT_DTYPE),
                   jax.ShapeDtypeStruct((t, GROUP_WIDTH), ACT_DTYPE), jax.ShapeDtypeStruct((t, GROUP_WIDTH), ACT_DTYPE),
                   jax.ShapeDtypeStruct((t, LANES), ACT_DTYPE), jax.ShapeDtypeStruct((LANES, GLA_KW), F32),
                   jax.ShapeDtypeStruct((1, GLA_KW), F32), jax.ShapeDtypeStruct((1, GROUP_WIDTH), F32)],
        scratch_shapes=[pltpu.VMEM((t, GLA_KW), F32), pltpu.VMEM((nc, 8, GLA_KW), F32), pltpu.VMEM((t, GLA_KW), F32),
                        pltpu.VMEM((t, GLA_KW), F32), pltpu.VMEM((nc, 8, GLA_KW), F32),
                        pltpu.VMEM((GLA_HEADS * GLA_DV, GLA_DK), F32)],
        semantics=("arbitrary",), name="gla_bwd")(
            pmm, pmm, pmm, pel, pel, w_up, b_a, gnorm_w, states, dmix)


CUM_BLOCK = 256


def _fox_gate_fwd(pel, b_f):
    t = pel.shape[0]
    nb = t // CUM_BLOCK

    def body(f_ref, b_ref, cum_ref, cum_t_ref):
        tri = (_iota((CUM_BLOCK, CUM_BLOCK), 1) <= _iota((CUM_BLOCK, CUM_BLOCK), 0)).astype(F32)
        carry = jnp.zeros((1, LANES), F32)
        for blk in range(nb):
            rows = slice(blk * CUM_BLOCK, (blk + 1) * CUM_BLOCK)
            cum = _dot_exact(tri, _log_sigmoid(f_ref[rows, :] + b_ref[...])) + carry
            cum_ref[rows, :] = cum
            cum_t_ref[blk] = cum.T[:ATT_HEADS, :]
            carry = cum[CUM_BLOCK - 1:CUM_BLOCK, :]

    return pl.pallas_call(
        body, grid=(1,),
        in_specs=[pl.BlockSpec((t, LANES), lambda i: (0, 5)), pl.BlockSpec((1, LANES), lambda i: (0, 0))],
        out_specs=[pl.BlockSpec((t, LANES), lambda i: (0, 0)),
                   pl.BlockSpec((nb, ATT_HEADS, CUM_BLOCK), lambda i: (0, 0, 0))],
        out_shape=[jax.ShapeDtypeStruct((t, LANES), F32), jax.ShapeDtypeStruct((nb, ATT_HEADS, CUM_BLOCK), F32)],
        compiler_params=_params("arbitrary"), name="fox_gate_fwd")(pel, b_f)


def _fox_gate_bwd(pel, b_f, dcum_t, dcum_q):
    t = pel.shape[0]
    nb = t // CUM_BLOCK

    def body(f_ref, b_ref, dct_ref, dcq_ref, df_ref, db_ref):
        tri_up = (_iota((CUM_BLOCK, CUM_BLOCK), 1) >= _iota((CUM_BLOCK, CUM_BLOCK), 0)).astype(F32)
        carry = jnp.zeros((1, LANES), F32)
        db = jnp.zeros((1, LANES), F32)
        for blk in reversed(range(nb)):
            rows = slice(blk * CUM_BLOCK, (blk + 1) * CUM_BLOCK)
            dls = _dot_exact(tri_up, dct_ref[blk].T + dcq_ref[rows, :]) + carry
            carry = dls[0:1, :]
            df = dls * jax.nn.sigmoid(-(f_ref[rows, :] + b_ref[...]))
            df_ref[rows, :] = df.astype(df_ref.dtype)
            db = db + jnp.sum(df, axis=0, keepdims=True)
        db_ref[...] = db

    return pl.pallas_call(
        body, grid=(1,),
        in_specs=[pl.BlockSpec((t, LANES), lambda i: (0, 5)), pl.BlockSpec((1, LANES), lambda i: (0, 0)),
                  pl.BlockSpec((nb, LANES, CUM_BLOCK), lambda i: (0, 0, 0)), pl.BlockSpec((t, LANES), lambda i: (0, 0))],
        out_specs=[pl.BlockSpec((t, LANES), lambda i: (0, 0)), pl.BlockSpec((1, LANES), lambda i: (0, 0))],
        out_shape=[jax.ShapeDtypeStruct((t, LANES), ACT_DTYPE), jax.ShapeDtypeStruct((1, LANES), F32)],
        compiler_params=_params("arbitrary"), name="fox_gate_bwd")(pel, b_f, dcum_t, dcum_q)


FOX_Q_BLOCK = 256


assert FOX_Q_BLOCK == CUM_BLOCK


def _fox_scores(q_ref, k_ref, cum_ref, cum_t_ref, h, i):
    hs = slice(h * HEAD_DIM, (h + 1) * HEAD_DIM)
    nb = cum_t_ref.shape[0]
    key_gate = jnp.concatenate([cum_t_ref[kb, h:h + 1, :] for kb in range(nb)], axis=1)
    s = _dot(q_ref[:, hs], k_ref[:, hs], 1, 1) * (HEAD_DIM ** -0.5) + (cum_ref[:, h:h + 1] - key_gate)
    shape = (FOX_Q_BLOCK, nb * FOX_Q_BLOCK)
    return jnp.where(_iota(shape, 1) <= i * FOX_Q_BLOCK + _iota(shape, 0), s, NEG)


def _fox_specs(t):
    bq, nb = FOX_Q_BLOCK, t // FOX_Q_BLOCK
    return [pl.BlockSpec((bq, GROUP_WIDTH), lambda i: (i, 2)), pl.BlockSpec((t, GROUP_WIDTH), lambda i: (0, 3)),
            pl.BlockSpec((t, GROUP_WIDTH), lambda i: (0, 4)), pl.BlockSpec((bq, LANES), lambda i: (i, 0)),
            pl.BlockSpec((nb, ATT_HEADS, bq), lambda i: (0, 0, 0))]


def _fox_fwd(pmm, cum, cum_t, ride=None):
    t = pmm.shape[0]
    bq = FOX_Q_BLOCK

    def body(q_ref, k_ref, v_ref, cum_ref, cum_t_ref, o_ref, lse_ref):
        i = pl.program_id(0)
        lse_ref[...] = jnp.zeros_like(lse_ref)
        for h in range(ATT_HEADS):
            hs = slice(h * HEAD_DIM, (h + 1) * HEAD_DIM)
            s = _fox_scores(q_ref, k_ref, cum_ref, cum_t_ref, h, i)
            m = jnp.max(s, axis=-1, keepdims=True)
            p = jnp.exp(s - m)
            l = jnp.sum(p, axis=-1, keepdims=True)
            o_ref[:, hs] = (_dot(p, v_ref[:, hs]) / l).astype(o_ref.dtype)
            lse_ref[:, h:h + 1] = m + jnp.log(l)

    return _pcall(
        body, ride, grid=(t // bq,), in_specs=_fox_specs(t),
        out_specs=[pl.BlockSpec((bq, GROUP_WIDTH), lambda i: (i, 0)), pl.BlockSpec((bq, LANES), lambda i: (i, 0))],
        out_shape=[jax.ShapeDtypeStruct((t, GROUP_WIDTH), ACT_DTYPE), jax.ShapeDtypeStruct((t, LANES), F32)],
        semantics=("parallel",), name="fox_fwd")(pmm, pmm, pmm, cum, cum_t)


def _fox_bwd(pmm, cum, cum_t, lse, dmix, ride=None):
    t = pmm.shape[0]
    bq, nb = FOX_Q_BLOCK, t // FOX_Q_BLOCK
    scale = HEAD_DIM ** -0.5

    def body(q_ref, k_ref, v_ref, cum_ref, cum_t_ref, lse_ref, do_ref, dq_ref, dk_ref, dv_ref, dct_ref, dcq_ref):
        i = pl.program_id(0)

        @pl.when(i == 0)
        def _():
            dk_ref[...] = jnp.zeros_like(dk_ref)
            dv_ref[...] = jnp.zeros_like(dv_ref)
            dct_ref[...] = jnp.zeros_like(dct_ref)

        dcq_ref[...] = jnp.zeros_like(dcq_ref)
        for h in range(ATT_HEADS):
            hs = slice(h * HEAD_DIM, (h + 1) * HEAD_DIM)
            s = _fox_scores(q_ref, k_ref, cum_ref, cum_t_ref, h, i)
            p = jnp.exp(s - lse_ref[:, h:h + 1])
            do = do_ref[:, hs]
            dp = _dot(do, v_ref[:, hs], 1, 1)
            ds = p * (dp - jnp.sum(p * dp, axis=-1, keepdims=True))
            dq_ref[:, hs] = (_dot(ds, k_ref[:, hs]) * scale).astype(dq_ref.dtype)
            dk_ref[:, hs] += _dot(ds, q_ref[:, hs], 0, 0) * scale
            dv_ref[:, hs] += _dot(p, do, 0, 0)
            key_side = -jnp.sum(ds, axis=0, keepdims=True)
            for kb in range(nb):
                dct_ref[kb, h:h + 1, :] += key_side[:, kb * bq:(kb + 1) * bq]
            dcq_ref[:, h:h + 1] = jnp.sum(ds, axis=1, keepdims=True)

    whole = pl.BlockSpec((t, GROUP_WIDTH), lambda i: (0, 0))
    return _pcall(
        body, ride, grid=(t // bq,),
        in_specs=_fox_specs(t) + [pl.BlockSpec((bq, LANES), lambda i: (i, 0)),
                                  pl.BlockSpec((bq, GROUP_WIDTH), lambda i: (i, 1))],
        out_specs=[pl.BlockSpec((bq, GROUP_WIDTH), lambda i: (i, 0)), whole, whole,
                   pl.BlockSpec((nb, LANES, bq), lambda i: (0, 0, 0)), pl.BlockSpec((bq, LANES), lambda i: (i, 0))],
        out_shape=[jax.ShapeDtypeStruct((t, GROUP_WIDTH), ACT_DTYPE), jax.ShapeDtypeStruct((t, GROUP_WIDTH), F32),
                   jax.ShapeDtypeStruct((t, GROUP_WIDTH), F32), jax.ShapeDtypeStruct((nb, LANES, bq), F32),
                   jax.ShapeDtypeStruct((t, LANES), F32)],
        semantics=("arbitrary",), name="fox_bwd")(pmm, pmm, pmm, cum, cum_t, lse, dmix)


CA_Q_BLOCK = 4 * CHUNK
CA_WINDOW = CA_Q_BLOCK + CA_LEFT
CA_BASE = 1024


def _ca_bias_base(rel_bias):
    n = rel_bias.shape[0]
    flat = CA_Q_BLOCK + CA_LEFT - REL_CLIP
    tail = CA_BASE - flat - (2 * REL_CLIP + 1)
    return jnp.concatenate([jnp.broadcast_to(rel_bias[:, 2 * REL_CLIP:], (n, flat)), rel_bias[:, ::-1],
                            jnp.broadcast_to(rel_bias[:, :1], (n, tail))], axis=1)


def _ca_bias_base_grad(dbase):
    flat = CA_Q_BLOCK + CA_LEFT - REL_CLIP
    mid = dbase[:, flat:flat + 2 * REL_CLIP + 1][:, ::-1]
    lo = jnp.sum(dbase[:, flat + 2 * REL_CLIP + 1:], axis=1, keepdims=True)
    hi = jnp.sum(dbase[:, :flat], axis=1, keepdims=True)
    pad = jnp.zeros((dbase.shape[0], 2 * REL_CLIP - 1), F32)
    return mid + jnp.concatenate([lo, pad, hi], axis=1)


def _ca_mask(i):
    r, j = _iota((CA_Q_BLOCK, CA_WINDOW), 0), _iota((CA_Q_BLOCK, CA_WINDOW), 1)
    rc, jc = r // CHUNK, j // CHUNK
    return (jc >= rc) & (jc <= rc + CA_LEFT // CHUNK) & (i * CA_Q_BLOCK + j >= CA_LEFT)


def _ca_fill_bias(i, base_ref, bias_scr):
    @pl.when(i == 0)
    def _():
        for h in range(ATT_HEADS):
            rows = jnp.broadcast_to(base_ref[h:h + 1, :], (CA_Q_BLOCK, CA_BASE))
            bias_scr[h] = pltpu.roll(rows, CA_BASE - CA_Q_BLOCK, 1, stride=1, stride_axis=0)[:, :CA_WINDOW]


def _ca_scores(q_ref, kp_ref, bias_scr, win, h, mask):
    hs = slice(h * HEAD_DIM, (h + 1) * HEAD_DIM)
    s = _dot(q_ref[:, hs], kp_ref[win, hs], 1, 1) * (HEAD_DIM ** -0.5)
    return jnp.where(mask, s + bias_scr[h], NEG)


CA_BIAS_SCRATCH = pltpu.VMEM((ATT_HEADS, CA_Q_BLOCK, CA_WINDOW), F32)


def _ca_fwd(pmm, kp, vp, base, ride=None):
    t = pmm.shape[0]

    def body(q_ref, kp_ref, vp_ref, base_ref, o_ref, lse_ref, bias_scr):
        i = pl.program_id(0)
        _ca_fill_bias(i, base_ref, bias_scr)
        win = pl.ds(pl.multiple_of(i * CA_Q_BLOCK, CA_Q_BLOCK), CA_WINDOW)
        mask = _ca_mask(i)
        lse_ref[...] = jnp.zeros_like(lse_ref)
        for h in range(ATT_HEADS):
            hs = slice(h * HEAD_DIM, (h + 1) * HEAD_DIM)
            s = _ca_scores(q_ref, kp_ref, bias_scr, win, h, mask)
            m = jnp.max(s, axis=-1, keepdims=True)
            p = jnp.exp(s - m)
            l = jnp.sum(p, axis=-1, keepdims=True)
            o_ref[:, hs] = (_dot(p, vp_ref[win, hs]) / l).astype(o_ref.dtype)
            lse_ref[:, h:h + 1] = m + jnp.log(l)

    padded = pl.BlockSpec((t + CA_LEFT, GROUP_WIDTH), lambda i: (0, 0))
    return _pcall(
        body, ride, grid=(t // CA_Q_BLOCK,),
        in_specs=[pl.BlockSpec((CA_Q_BLOCK, GROUP_WIDTH), lambda i: (i, 0)), padded, padded,
                  pl.BlockSpec((ATT_HEADS, CA_BASE), lambda i: (0, 0))],
        out_specs=[pl.BlockSpec((CA_Q_BLOCK, GROUP_WIDTH), lambda i: (i, 0)),
                   pl.BlockSpec((CA_Q_BLOCK, LANES), lambda i: (i, 0))],
        out_shape=[jax.ShapeDtypeStruct((t, GROUP_WIDTH), ACT_DTYPE), jax.ShapeDtypeStruct((t, LANES), F32)],
        scratch_shapes=[CA_BIAS_SCRATCH], semantics=("arbitrary",), name="ca_fwd")(pmm, kp, vp, base)


def _ca_bwd(pmm, kp, vp, base, lse, dmix, ride=None):
    t = pmm.shape[0]
    scale = HEAD_DIM ** -0.5

    def body(q_ref, kp_ref, vp_ref, base_ref, lse_ref, do_ref, dq_ref, dkp_ref, dvp_ref, dbase_ref, bias_scr):
        i = pl.program_id(0)
        _ca_fill_bias(i, base_ref, bias_scr)

        @pl.when(i == 0)
        def _():
            dkp_ref[...] = jnp.zeros_like(dkp_ref)
            dvp_ref[...] = jnp.zeros_like(dvp_ref)
            dbase_ref[...] = jnp.zeros_like(dbase_ref)

        win = pl.ds(pl.multiple_of(i * CA_Q_BLOCK, CA_Q_BLOCK), CA_WINDOW)
        mask = _ca_mask(i)
        flip = (_iota((CA_Q_BLOCK, CA_Q_BLOCK), 0) + _iota((CA_Q_BLOCK, CA_Q_BLOCK), 1) == CA_Q_BLOCK - 1).astype(F32)
        for h in range(ATT_HEADS):
            hs = slice(h * HEAD_DIM, (h + 1) * HEAD_DIM)
            s = _ca_scores(q_ref, kp_ref, bias_scr, win, h, mask)
            p = jnp.exp(s - lse_ref[:, h:h + 1])
            do = do_ref[:, hs]
            dp = _dot(do, vp_ref[win, hs], 1, 1)
            ds = p * (dp - jnp.sum(p * dp, axis=-1, keepdims=True))
            dq_ref[:, hs] = (_dot(ds, kp_ref[win, hs]) * scale).astype(dq_ref.dtype)
            dkp_ref[win, hs] += _dot(ds, q_ref[:, hs], 0, 0) * scale
            dvp_ref[win, hs] += _dot(p, do, 0, 0)
            rev = jnp.concatenate([_dot(flip, ds), jnp.zeros((CA_Q_BLOCK, CA_BASE - CA_WINDOW), F32)], axis=1)
            lined = pltpu.roll(rev, 1, 1, stride=1, stride_axis=0)
            dbase_ref[h:h + 1, :] += jnp.sum(lined, axis=0, keepdims=True)

    padded = pl.BlockSpec((t + CA_LEFT, GROUP_WIDTH), lambda i: (0, 0))
    return _pcall(
        body, ride, grid=(t // CA_Q_BLOCK,),
        in_specs=[pl.BlockSpec((CA_Q_BLOCK, GROUP_WIDTH), lambda i: (i, 0)), padded, padded,
                  pl.BlockSpec((ATT_HEADS, CA_BASE), lambda i: (0, 0)),
                  pl.BlockSpec((CA_Q_BLOCK, LANES), lambda i: (i, 0)),
                  pl.BlockSpec((CA_Q_BLOCK, GROUP_WIDTH), lambda i: (i, 0))],
        out_specs=[pl.BlockSpec((CA_Q_BLOCK, GROUP_WIDTH), lambda i: (i, 0)), padded, padded,
                   pl.BlockSpec((ATT_HEADS, CA_BASE), lambda i: (0, 0))],
        out_shape=[jax.ShapeDtypeStruct((t, GROUP_WIDTH), ACT_DTYPE),
                   jax.ShapeDtypeStruct((t + CA_LEFT, GROUP_WIDTH), F32),
                   jax.ShapeDtypeStruct((t + CA_LEFT, GROUP_WIDTH), F32),
                   jax.ShapeDtypeStruct((ATT_HEADS, CA_BASE), F32)],
        scratch_shapes=[CA_BIAS_SCRATCH], semantics=("arbitrary",), name="ca_bwd")(pmm, kp, vp, base, lse, dmix)


GELU_C = 0.7978845608028654
GELU_A = 0.044715


def _shift_down(v, k, fill):
    return jnp.where(_iota(v.shape, 0) >= k, pltpu.roll(v, k, 0), fill)


def _shift_up(v, k, fill):
    t = v.shape[0]
    return jnp.where(_iota(v.shape, 0) < t - k, pltpu.roll(v, t - k, 0), fill)


def _linear_scan(a, b, shift):
    k = 1
    while k < a.shape[0]:
        b = a * shift(b, k, 0.0) + b
        a = a * shift(a, k, 1.0)
        k *= 2
    return b


def _neg_expm1(y):
    series = -y * (1.0 + y * (0.5 + y * (1.0 / 6.0 + y * (1.0 / 24.0 + y * (1.0 / 120.0)))))
    return jnp.where(y > -0.1, series, 1.0 - jnp.exp(y))


def _lru_forward(x, g_in, cw, cb, wa, ba, wx, bx, lam):
    xs = [_shift_down(x, CONV_WIDTH - 1 - j, 0.0) for j in range(CONV_WIDTH - 1)] + [x]
    xc = cb + sum(cw[j:j + 1, :] * xs[j] for j in range(CONV_WIDTH))
    r = jax.nn.sigmoid(_dot(xc, wa) + ba)
    i = jax.nn.sigmoid(_dot(xc, wx) + bx)
    lsl = _log_sigmoid(lam)
    la = LRU_C * r * lsl
    a = jnp.exp(la)
    s = jnp.sqrt(_neg_expm1(2.0 * la))
    h = _linear_scan(a, s * (i * xc), _shift_down)
    u = GELU_C * (g_in + GELU_A * g_in * g_in * g_in)
    th = jnp.tanh(u)
    gelu = 0.5 * g_in * (1.0 + th)
    return xs, xc, r, i, lsl, a, s, h, th, gelu


def _lru_specs(t):
    col = lambda off: pl.BlockSpec((t, LANES), lambda j: (0, j + off))
    vec = pl.BlockSpec((1, LANES), lambda j: (0, j))
    mat = pl.BlockSpec((None, LANES, LANES), lambda j: (j, 0, 0))
    return [col(0), col(GROUP_WIDTH // LANES), pl.BlockSpec((CONV_WIDTH, LANES), lambda j: (0, j)),
            vec, mat, vec, mat, vec, vec]


def _lru_fwd(pel, conv_w, conv_b, wa, ba, wx, bx, lam, ride=None):
    t = pel.shape[0]

    def body(g_ref, x_ref, cw_ref, cb_ref, wa_ref, ba_ref, wx_ref, bx_ref, lam_ref, o_ref):
        res = _lru_forward(x_ref[...], g_ref[...], cw_ref[...], cb_ref[...], wa_ref[...], ba_ref[...],
                           wx_ref[...], bx_ref[...], lam_ref[...])
        o_ref[...] = (res[7] * res[9]).astype(o_ref.dtype)

    return _pcall(
        body, ride, grid=(GROUP_WIDTH // LANES,), in_specs=_lru_specs(t),
        out_specs=pl.BlockSpec((t, LANES), lambda j: (0, j)),
        out_shape=jax.ShapeDtypeStruct((t, GROUP_WIDTH), ACT_DTYPE),
        semantics=("parallel",), name="lru_fwd")(pel, pel, conv_w, conv_b, wa, ba, wx, bx, lam)


def _lru_bwd(pel, conv_w, conv_b, wa, ba, wx, bx, lam, dmix, ride=None):
    t = pel.shape[0]

    def body(g_ref, x_ref, cw_ref, cb_ref, wa_ref, ba_ref, wx_ref, bx_ref, lam_ref, do_ref,
             dg_ref, dx_ref, dcw_ref, dcb_ref, dwa_ref, dba_ref, dwx_ref, dbx_ref, dlam_ref):
        g_in, cw, lam = g_ref[...], cw_ref[...], lam_ref[...]
        xs, xc, r, i, lsl, a, s, h, th, gelu = _lru_forward(
            x_ref[...], g_in, cw, cb_ref[...], wa_ref[...], ba_ref[...], wx_ref[...], bx_ref[...], lam)
        dout = do_ref[...]
        dgelu = 0.5 * (1.0 + th) + 0.5 * g_in * (1.0 - th * th) * GELU_C * (1.0 + 3.0 * GELU_A * g_in * g_in)
        dg_ref[...] = (dout * h * dgelu).astype(dg_ref.dtype)
        gsum = _linear_scan(_shift_up(a, 1, 0.0), dout * gelu, _shift_up)
        da = gsum * _shift_down(h, 1, 0.0)
        di = gsum * s * xc
        dla = da * a - gsum * (i * xc) * (a * a / s)
        dlam_ref[...] = jnp.sum(dla * (LRU_C * r), axis=0, keepdims=True) * jax.nn.sigmoid(-lam)
        dpr = dla * (LRU_C * lsl) * r * (1.0 - r)
        dpi = di * i * (1.0 - i)
        dxc = gsum * s * i + _dot(dpr, wa_ref[...], 1, 1) + _dot(dpi, wx_ref[...], 1, 1)
        xct = xc.T
        dwa_ref[...] = _dot(xct, dpr)
        dwx_ref[...] = _dot(xct, dpi)
        dba_ref[...] = jnp.sum(dpr, axis=0, keepdims=True)
        dbx_ref[...] = jnp.sum(dpi, axis=0, keepdims=True)
        dcb_ref[...] = jnp.sum(dxc, axis=0, keepdims=True)
        for j in range(CONV_WIDTH):
            dcw_ref[j:j + 1, :] = jnp.sum(dxc * xs[j], axis=0, keepdims=True)
        dx = cw[CONV_WIDTH - 1:CONV_WIDTH, :] * dxc
        for j in range(CONV_WIDTH - 1):
            dx = dx + cw[j:j + 1, :] * _shift_up(dxc, CONV_WIDTH - 1 - j, 0.0)
        dx_ref[...] = dx.astype(dx_ref.dtype)

    col = pl.BlockSpec((t, LANES), lambda j: (0, j))
    vec = pl.BlockSpec((1, LANES), lambda j: (0, j))
    mat = pl.BlockSpec((None, LANES, LANES), lambda j: (j, 0, 0))
    nb = GROUP_WIDTH // LANES
    vshape = jax.ShapeDtypeStruct((1, GROUP_WIDTH), F32)
    mshape = jax.ShapeDtypeStruct((nb, LANES, LANES), F32)
    return _pcall(
        body, ride, grid=(nb,),
        in_specs=_lru_specs(t) + [pl.BlockSpec((t, LANES), lambda j: (0, j + nb))],
        out_specs=[col, col, pl.BlockSpec((CONV_WIDTH, LANES), lambda j: (0, j)), vec, mat, vec, mat, vec, vec],
        out_shape=[jax.ShapeDtypeStruct((t, GROUP_WIDTH), ACT_DTYPE), jax.ShapeDtypeStruct((t, GROUP_WIDTH), ACT_DTYPE),
                   jax.ShapeDtypeStruct((CONV_WIDTH, GROUP_WIDTH), F32), vshape, mshape, vshape, mshape, vshape, vshape],
        semantics=("parallel",), name="lru_bwd")(
            pel, pel, conv_w, conv_b, wa, ba, wx, bx, lam, dmix)


def _block_diag_pairs(w):
    z = jnp.zeros((LRU_BLOCK_DIM, LRU_BLOCK_DIM), w.dtype)
    return jnp.stack([jnp.block([[w[2 * j], z], [z, w[2 * j + 1]]]) for j in range(w.shape[0] // 2)])


def _block_diag_pairs_grad(dw):
    b = LRU_BLOCK_DIM
    return jnp.stack([dw[n // 2, (n % 2) * b:(n % 2 + 1) * b, (n % 2) * b:(n % 2 + 1) * b] for n in range(2 * dw.shape[0])])


def _row_tile(r):
    return ROW_TILE if r % ROW_TILE == 0 else r


def _pair_sum(g, got, place, name):
    _, r, c = g.shape
    tile = r

    def body(place_ref, a_ref, b_ref, o_ref):
        o_ref[...] = (a_ref[...].astype(F32) + b_ref[...].astype(F32)).astype(o_ref.dtype)

    blk = pl.BlockSpec((1, tile, c), lambda k, i, place_ref: (k, i, 0))
    return pl.pallas_call(
        body,
        grid_spec=pltpu.PrefetchScalarGridSpec(
            num_scalar_prefetch=1, grid=(N_CHIPS, r // tile),
            in_specs=[pl.BlockSpec((1, tile, c), lambda k, i, place_ref: (2 * k + place_ref[0], i, 0)), blk],
            out_specs=blk),
        out_shape=jax.ShapeDtypeStruct(got.shape, got.dtype),
        compiler_params=_params("parallel", "parallel"), name=name)(place, g, got)


def _adamw_update(g, w_ref, m_ref, v_ref, g_ref, d_ref, nm_ref, nv_ref):
    nm = ADAM_B1 * m_ref[...] + (1.0 - ADAM_B1) * g
    nv = ADAM_B2 * v_ref[...] + (1.0 - ADAM_B2) * jnp.square(g)
    m_hat = nm / (1.0 - ADAM_B1 ** ADAM_STEP)
    v_hat = nv / (1.0 - ADAM_B2 ** ADAM_STEP)
    g_ref[...] = g
    d_ref[...] = -ADAM_LR * (m_hat / (jnp.sqrt(v_hat) + ADAM_EPS) + ADAM_WD * w_ref[...])
    nm_ref[...] = nm
    nv_ref[...] = nv


def _adamw_sharded(parts, w, m, v, place, name):
    n_layers, r, c = w.shape
    tile = _row_tile(r)
    nb = r // tile

    def body(place_ref, *refs):
        layer = pl.program_id(0)
        g = None
        for l in range(n_layers):
            s_ref, r_ref = refs[2 * l], refs[2 * l + 1]
            g_l = s_ref[0].astype(F32) + r_ref[0].astype(F32) + r_ref[1].astype(F32) + r_ref[2].astype(F32)
            g = g_l if g is None else jnp.where(layer == l, g_l, g)
        _adamw_update(g, *refs[2 * n_layers:])

    def part_specs(l):
        rows = lambda q, i: jnp.where(q < l, 0, jnp.where(q > l, nb - 1, i))
        return [pl.BlockSpec((1, tile, c), lambda q, i, place_ref: (place_ref[1], rows(q, i), 0)),
                pl.BlockSpec((3, tile, c), lambda q, i, place_ref: (0, rows(q, i), 0))]

    in_specs, args = [], []
    for l, (s, recv) in enumerate(parts):
        in_specs += part_specs(l)
        args += [s, recv]
    blk = pl.BlockSpec((None, tile, c), lambda q, i, place_ref: (q, i, 0))
    out = jax.ShapeDtypeStruct((n_layers, r, c), F32)
    return pl.pallas_call(
        body,
        grid_spec=pltpu.PrefetchScalarGridSpec(
            num_scalar_prefetch=1, grid=(n_layers, nb), in_specs=in_specs + [blk, blk, blk],
            out_specs=[blk, blk, blk, blk]),
        out_shape=[out, out, out, out], compiler_params=_params("arbitrary", "arbitrary"), name=name)(
            place, *args, w, m, v)


def _adamw_small(repl_parts, vec_parts, w, m, v, place):
    n_r, n = len(repl_parts), len(w)
    shapes = [a.shape for a in w]

    def body(place_ref, *refs):
        parts, rest = refs[:n], refs[n:]
        for k in range(n):
            take = (lambda p: parts[k][p]) if k < n_r else (lambda p: parts[k][p, 0])
            g = take(0)
            for p in range(1, N_DEV):
                g = g + take(p)
            _adamw_update(g, rest[k], rest[n + k], rest[2 * n + k], *rest[3 * n + 4 * k:3 * n + 4 * k + 4])

    def whole(shape):
        return pl.BlockSpec(shape, lambda i, place_ref: (0,) * len(shape))

    def mine(shard):
        return pl.BlockSpec((N_DEV, 1) + shard, lambda i, place_ref: (0, place_ref[2]) + (0,) * len(shard))

    in_specs = [whole(a.shape) for a in repl_parts] + [mine(s) for s in shapes[n_r:]] + [whole(s) for s in shapes] * 3
    outs = pl.pallas_call(
        body,
        grid_spec=pltpu.PrefetchScalarGridSpec(
            num_scalar_prefetch=1, grid=(1,), in_specs=in_specs,
            out_specs=[whole(s) for s in shapes for _ in range(4)]),
        out_shape=[jax.ShapeDtypeStruct(s, F32) for s in shapes for _ in range(4)],
        compiler_params=_params("arbitrary"), name="adamw_small")(place, *repl_parts, *vec_parts, *w, *m, *v)
    return [outs[4 * k:4 * k + 4] for k in range(n)]


SHARDED = {"norm_w": 2, "w_in_even": 2, "gla_w_a_up": 2, "w_out_even": 1, "w_in_odd": 2, "conv_w": 2, "conv_b": 1,
           "lru_b_a": 1, "lru_b_x": 1, "lru_lambda": 1, "w_out_odd": 1, "w_mlp_up": 2, "w_mlp_down": 1}
REPLICATED = ["gla_b_a", "gla_norm_w", "fox_b_f", "rel_bias", "lru_w_a", "lru_w_x"]
WEIGHTS = ["norm_w", "w_in_even", "gla_w_a_up", "gla_b_a", "gla_norm_w", "fox_b_f", "w_out_even", "w_in_odd",
           "rel_bias", "conv_w", "conv_b", "lru_w_a", "lru_b_a", "lru_w_x", "lru_b_x", "lru_lambda", "w_out_odd",
           "w_mlp_up", "w_mlp_down"]
MATRICES = ("w_in_even", "w_out_even", "w_in_odd", "w_out_odd", "w_mlp_up", "w_mlp_down")
TRANSPOSED = ("w_in_even", "w_in_odd")
VECTORS = tuple(n for n in SHARDED if n not in MATRICES)
MATRIX_BLOCKS = (("w_in_even", 0), ("w_out_even", 0), ("w_in_odd", 0), ("w_out_odd", 0),
                 ("w_mlp_up", 0), ("w_mlp_up", 1), ("w_mlp_down", 0), ("w_mlp_down", 1))


def _join_shards(blocks, axis):
    moved = jnp.moveaxis(blocks, 0, axis)
    shape = moved.shape
    return moved.reshape(shape[:axis] + (shape[axis] * shape[axis + 1],) + shape[axis + 2:])


def _split_shards(full, axis):
    shape = full.shape
    cut = full.reshape(shape[:axis] + (N_DEV, shape[axis] // N_DEV) + shape[axis + 1:])
    return jnp.moveaxis(cut, axis, 0)


EVEN_SPLITS = (0, 256, 512, 1024, 1536, 1552, 2064, 2576, 3088, 3096)


def _even_in_split(wt):
    c = [wt[EVEN_SPLITS[k]:EVEN_SPLITS[k + 1]] for k in range(9)]
    gq, gk, gv, gr, ga, fq, fk, fv, ff = c
    padrows = lambda a: jnp.pad(a, ((0, LANES - a.shape[0]), (0, 0)))
    return jnp.concatenate([gq, gk, gv, fq, fk, fv], axis=0), jnp.concatenate([gr, padrows(ga), padrows(ff)], axis=0)


def _even_in_merge(dmm, dele):
    return jnp.concatenate([dmm[:1024], dele[:512], dele[512:512 + GLA_RANK], dmm[1024:2560],
                            dele[640:640 + ATT_HEADS]], axis=0)


def _forward_backward(x, target, shard, vec_shard, w, place):
    w = dict(w)
    g, dnorm, sums, recv = {}, {}, {}, {}
    nrm = lambda l, k: w["norm_w"][l, k][None, :]
    gather = lambda *keys: _gather_plan([shard[k] for k in keys])
    blocks = lambda r, c: (N_DEV, r // N_DEV, c)

    def pair_sum(key):
        sums[key] = _pair_sum(g[key], got[key], place, f"rs_pair_sum_{key[0]}_{key[1]}")

    got = {}

    def mlp_fwd(xin, layer, ride_up, ride_down):
        h = _norm_fwd(xin, nrm(layer, 2), out_dtype=ACT_DTYPE, name=f"norm_mlp_{layer}")
        u = _mm(h, w["w_mlp_up"][layer], out_dtype=ACT_DTYPE, tm=TM_FWD, tn=D_FF // N_DEV, b_blocked=True,
                name=f"mlp_up_{layer}", ride=ride_up)
        u, rode_up = u if ride_up is not None else (u, None)
        if w["w_mlp_down"][layer] is None:
            w["w_mlp_down"][layer] = rode_up[0].reshape(D_FF, D_MODEL)
        yv = _mm(u, w["w_mlp_down"][layer], out_dtype=F32, tm=TM_DX, tn=TN, a_sqrelu=True,
                 name=f"mlp_down_{layer}", ride=ride_down)
        yv, rode_down = yv if ride_down is not None else (yv, None)
        xout = _norm_fwd(yv, nrm(layer, 3), out_dtype=F32, res=xin, name=f"norm_mlp_out_{layer}")
        return xout, (xin, h, u, yv), rode_up, rode_down

    def mlp_bwd(dxout, saved, layer, ride):
        xin, h, u, yv = saved
        k_up, k_down = ("w_mlp_up", layer), ("w_mlp_down", layer)
        dy, dnorm[(layer, 3)] = _norm_bwd(dxout, yv, nrm(layer, 3), out_dtype=ACT_DTYPE, name=f"norm_mlp_out_bwd_{layer}")
        du = _mm(dy, w["w_mlp_down"][layer], nt=True, out_dtype=ACT_DTYPE, tm=TM_DX, tn=TN, drelu_of=u,
                 name=f"mlp_down_dx_{layer}", ride=ride)
        rode = None
        if ride is not None:
            du, rode = du
        g[k_down] = _mm(u, dy, ta=True, out_dtype=WIRE_DTYPE, tm=TM_DW, tn=TN, a_sqrelu=True,
                        name=f"mlp_down_dw_{layer}").reshape(blocks(D_FF, D_MODEL))
        g[k_up] = _mm(h, du, ta=True, out_dtype=WIRE_DTYPE, tm=TM_DW, tn=D_FF // N_DEV, out_blocked=True,
                      name=f"mlp_up_dw_{layer}")
        w_up = jnp.moveaxis(w["w_mlp_up"][layer], 0, 1).reshape(D_MODEL, D_FF)
        dh, (got[k_down], got[k_up]) = _mm(du, w_up, nt=True, out_dtype=F32, tm=TM_DX, tn=TN, name=f"mlp_up_dx_{layer}",
                                           ride=_sibling_plan([g[k_down], g[k_up]]))
        pair_sum(k_down)
        pair_sum(k_up)
        dxin, dnorm[(layer, 2)] = _norm_bwd(dh, xin, nrm(layer, 2), out_dtype=F32, add=dxout, name=f"norm_mlp_bwd_{layer}")
        return dxin, rode

    first = _run_plan(_gather_plan([shard[("w_in_even", 0)]] + [vec_shard[n] for n in VECTORS]),
                      "weights_all_gather_first")
    w["w_in_even"] = first[0].reshape(-1, D_MODEL)
    for n, b in zip(VECTORS, first[1:]):
        w[n] = _join_shards(b, SHARDED[n])
    w["w_mlp_up"], w["w_mlp_down"] = [None] * DEPTH, [None] * DEPTH

    wmm_e, wel_e = _even_in_split(w["w_in_even"])
    w_up_pad = jnp.pad(w["gla_w_a_up"][0], ((0, LANES - GLA_RANK), (0, 0)))
    b_f_pad = jnp.pad(w["fox_b_f"], ((0, 0), (0, LANES - ATT_HEADS)))
    h0 = _norm_fwd(x, nrm(0, 0), out_dtype=ACT_DTYPE, name="norm_in_0")
    pmm0, (w_out_even,) = _mm(h0, wmm_e, nt=True, out_dtype=ACT_DTYPE, tm=TM_FWD, tn=TN, name="in_even_mm",
                              ride=gather(("w_out_even", 0)))
    pel0 = _mm(h0, wel_e, nt=True, out_dtype=F32, tm=TM_FWD, tn=768, name="in_even_el")
    (out_a, states), (w_in_odd,) = _gla_fwd(pmm0, pel0, w_up_pad, w["gla_b_a"], w["gla_norm_w"],
                                            ride=gather(("w_in_odd", 0)))
    cum, cum_t = _fox_gate_fwd(pel0, b_f_pad)
    (out_b, lse_b), (w["w_mlp_up"][0], w_mlp_down0) = _fox_fwd(pmm0, cum, cum_t,
                                                               ride=gather(("w_mlp_up", 0), ("w_mlp_down", 0)))
    w["w_out_even"] = w_out_even.reshape(D_MODEL, D_MODEL)
    w["w_mlp_down"][0] = w_mlp_down0.reshape(D_FF, D_MODEL)
    mix_in0 = jnp.concatenate([out_a, out_b], axis=1)
    mix0 = _mm(mix_in0, w["w_out_even"], out_dtype=F32, tm=TM_FWD, tn=TN, name="out_even")
    x1 = _norm_fwd(mix0, nrm(0, 1), out_dtype=F32, res=x, name="norm_mix_0")
    x2, mlp0, _, (w["w_mlp_up"][1],) = mlp_fwd(x1, 0, None, gather(("w_mlp_up", 1)))
    w["w_in_odd"] = w_in_odd.reshape(-1, D_MODEL)

    w_in_o = w["w_in_odd"]
    n_mm_o = 3 * GROUP_WIDTH
    wa_bd, wx_bd = _block_diag_pairs(w["lru_w_a"][0]), _block_diag_pairs(w["lru_w_x"][0])
    base = _ca_bias_base(w["rel_bias"][0])
    h1 = _norm_fwd(x2, nrm(1, 0), out_dtype=ACT_DTYPE, name="norm_in_1")
    pmm1 = _mm(h1, w_in_o[:n_mm_o], nt=True, out_dtype=ACT_DTYPE, tm=TM_FWD, tn=TN, name="in_odd_mm")
    pel1 = _mm(h1, w_in_o[n_mm_o:], nt=True, out_dtype=F32, tm=TM_FWD, tn=TN, name="in_odd_el")
    kp = jnp.pad(pmm1[:, GROUP_WIDTH:2 * GROUP_WIDTH], ((CA_LEFT, 0), (0, 0)))
    vp = jnp.pad(pmm1[:, 2 * GROUP_WIDTH:], ((CA_LEFT, 0), (0, 0)))
    (out_c, lse_c), (w_mlp_down1,) = _ca_fwd(pmm1, kp, vp, base, ride=gather(("w_mlp_down", 1)))
    w["w_mlp_down"][1] = w_mlp_down1.reshape(D_FF, D_MODEL)
    lru_args = (pel1, w["conv_w"][0], w["conv_b"], wa_bd, w["lru_b_a"], wx_bd, w["lru_b_x"], w["lru_lambda"])
    out_d, (w_out_odd,) = _lru_fwd(*lru_args, ride=gather(("w_out_odd", 0)))
    w["w_out_odd"] = w_out_odd.reshape(D_MODEL, D_MODEL)
    mix_in1 = jnp.concatenate([out_c, out_d], axis=1)
    mix1 = _mm(mix_in1, w["w_out_odd"], out_dtype=F32, tm=TM_FWD, tn=TN, name="out_odd")
    x3 = _norm_fwd(mix1, nrm(1, 1), out_dtype=F32, res=x2, name="norm_mix_1")
    x4, mlp1, _, _ = mlp_fwd(x3, 1, None, None)

    loss, dx4 = _loss_fwd_bwd(x4, target)

    k_oo, k_io, k_oe, k_ie = ("w_out_odd", 0), ("w_in_odd", 0), ("w_out_even", 0), ("w_in_even", 0)
    mlp_keys = lambda l: [("w_mlp_down", l), ("w_mlp_up", l)]
    dx3, _ = mlp_bwd(dx4, mlp1, 1, None)
    dmix1, dnorm[(1, 1)] = _norm_bwd(dx3, mix1, nrm(1, 1), out_dtype=ACT_DTYPE, name="norm_mix_bwd_1")
    g[k_oo] = _mm(mix_in1, dmix1, ta=True, out_dtype=WIRE_DTYPE, tm=TM_DW, tn=TN, name="out_odd_dw").reshape(
        blocks(D_MODEL, D_MODEL))
    dmix_in1, (got[k_oo],) = _mm(dmix1, w["w_out_odd"], nt=True, out_dtype=F32, tm=TM_DX, tn=TN, name="out_odd_dx",
                                 ride=_sibling_plan([g[k_oo]]))
    (dq_c, dkp, dvp, dbase), rode = _ca_bwd(pmm1, kp, vp, base, lse_c, dmix_in1,
                                            ride=_chip_plan([sums[k] for k in mlp_keys(1)]))
    recv.update(zip(mlp_keys(1), rode))
    pair_sum(k_oo)
    (dgate, dxin, g_conv_w, g_conv_b, dwa_bd, g_lru_b_a, dwx_bd, g_lru_b_x, g_lru_lambda), (recv[k_oo],) = _lru_bwd(
        *lru_args, dmix_in1, ride=_chip_plan([sums[k_oo]]))
    dp1 = jnp.concatenate([dq_c, dkp[CA_LEFT:].astype(ACT_DTYPE), dvp[CA_LEFT:].astype(ACT_DTYPE), dgate, dxin], axis=1)
    g[k_io] = _mm(dp1, h1, ta=True, out_dtype=WIRE_DTYPE, tm=dp1.shape[1] // 2, tn=TN, name="in_odd_dw").reshape(
        blocks(dp1.shape[1], D_MODEL))
    dh1, (got[k_io],) = _mm(dp1, w_in_o, out_dtype=F32, tm=TM_DX, tn=TN, name="in_odd_dx",
                            ride=_sibling_plan([g[k_io]]))
    pair_sum(k_io)
    dx2, dnorm[(1, 0)] = _norm_bwd(dh1, x2, nrm(1, 0), out_dtype=F32, add=dx3, name="norm_in_bwd_1")
    g["rel_bias"] = _ca_bias_base_grad(dbase)[None]
    g["conv_w"], g["conv_b"] = g_conv_w[None], g_conv_b
    g["lru_w_a"], g["lru_w_x"] = _block_diag_pairs_grad(dwa_bd)[None], _block_diag_pairs_grad(dwx_bd)[None]
    g["lru_b_a"], g["lru_b_x"], g["lru_lambda"] = g_lru_b_a, g_lru_b_x, g_lru_lambda

    dx1, (recv[k_io],) = mlp_bwd(dx2, mlp0, 0, _chip_plan([sums[k_io]]))
    dmix0, dnorm[(0, 1)] = _norm_bwd(dx1, mix0, nrm(0, 1), out_dtype=ACT_DTYPE, name="norm_mix_bwd_0")
    g[k_oe] = _mm(mix_in0, dmix0, ta=True, out_dtype=WIRE_DTYPE, tm=TM_DW, tn=TN, name="out_even_dw").reshape(
        blocks(D_MODEL, D_MODEL))
    dmix_in0, (got[k_oe],) = _mm(dmix0, w["w_out_even"], nt=True, out_dtype=F32, tm=TM_DX, tn=TN, name="out_even_dx",
                                 ride=_sibling_plan([g[k_oe]]))
    k_md0, k_mu0 = mlp_keys(0)
    pair_sum(k_oe)
    dq_a, dk_a, dv_a, dr_a, da_a, dw_up_pad, g_gla_b_a, g_gla_norm_w = _gla_bwd(
        pmm0, pel0, w_up_pad, w["gla_b_a"], w["gla_norm_w"], states, dmix_in0)
    (dq_b, dk_b, dv_b, dcum_t, dcum_q), (recv[k_md0], recv[k_mu0], recv[k_oe]) = _fox_bwd(
        pmm0, cum, cum_t, lse_b, dmix_in0, ride=_chip_plan([sums[k_md0], sums[k_mu0], sums[k_oe]]))
    df_b, db_f = _fox_gate_bwd(pel0, b_f_pad, dcum_t, dcum_q)
    g["gla_w_a_up"] = dw_up_pad[:GLA_RANK][None]
    g["gla_b_a"], g["gla_norm_w"], g["fox_b_f"] = g_gla_b_a, g_gla_norm_w, db_f[:, :ATT_HEADS]
    dp0 = jnp.concatenate([dq_a, dk_a, dv_a, dq_b, dk_b.astype(ACT_DTYPE), dv_b.astype(ACT_DTYPE), dr_a, da_a, df_b],
                          axis=1)
    w_perm = jnp.concatenate([wmm_e, wel_e], axis=0)
    n_mm_e = wmm_e.shape[0]
    dw_perm, repl_parts = _mm(dp0, h0, ta=True, out_dtype=WIRE_DTYPE, tm=dp0.shape[1] // 2, tn=TN, name="in_even_dw",
                              ride=_gather_plan([g[n] for n in REPLICATED]))
    dw_even = _even_in_merge(dw_perm[:n_mm_e], dw_perm[n_mm_e:])
    g[k_ie] = dw_even.reshape(blocks(dw_even.shape[0], D_MODEL))
    dh0, (got[k_ie],) = _mm(dp0, w_perm, out_dtype=F32, tm=TM_DX, tn=TN, name="in_even_dx",
                            ride=_sibling_plan([g[k_ie]]))
    pair_sum(k_ie)
    (dx0, dnorm[(0, 0)]), (recv[k_ie],) = _norm_bwd(dh0, x, nrm(0, 0), out_dtype=F32, add=dx1, name="norm_in_bwd_0",
                                                     ride=_chip_plan([sums[k_ie]]))

    g["norm_w"] = jnp.stack([jnp.concatenate([dnorm[(l, k)] for k in range(4)], axis=0) for l in range(DEPTH)])
    vec_parts = _run_plan(_gather_plan([_split_shards(g[n], SHARDED[n]) for n in VECTORS]), "vector_grads_all_gather")
    return loss, dx0, sums, recv, repl_parts, vec_parts


def kernel(x, norm_w, w_in_even, gla_w_a_up, gla_b_a, gla_norm_w, fox_b_f, w_out_even, w_in_odd, rel_bias, conv_w, conv_b, lru_w_a, lru_b_a, lru_w_x, lru_b_x, lru_lambda, w_out_odd, w_mlp_up, w_mlp_down, loss_target, m_norm_w, m_w_in_even, m_gla_w_a_up, m_gla_b_a, m_gla_norm_w, m_fox_b_f, m_w_out_even, m_w_in_odd, m_rel_bias, m_conv_w, m_conv_b, m_lru_w_a, m_lru_b_a, m_lru_w_x, m_lru_b_x, m_lru_lambda, m_w_out_odd, m_w_mlp_up, m_w_mlp_down, v_norm_w, v_w_in_even, v_gla_w_a_up, v_gla_b_a, v_gla_norm_w, v_fox_b_f, v_w_out_even, v_w_in_odd, v_rel_bias, v_conv_w, v_conv_b, v_lru_w_a, v_lru_b_a, v_lru_w_x, v_lru_b_x, v_lru_lambda, v_w_out_odd, v_w_mlp_up, v_w_mlp_down):
    wts = dict(zip(WEIGHTS, (norm_w, w_in_even, gla_w_a_up, gla_b_a, gla_norm_w, fox_b_f, w_out_even, w_in_odd, rel_bias,
                             conv_w, conv_b, lru_w_a, lru_b_a, lru_w_x, lru_b_x, lru_lambda, w_out_odd, w_mlp_up,
                             w_mlp_down)))
    mom = dict(zip(WEIGHTS, (m_norm_w, m_w_in_even, m_gla_w_a_up, m_gla_b_a, m_gla_norm_w, m_fox_b_f, m_w_out_even,
                             m_w_in_odd, m_rel_bias, m_conv_w, m_conv_b, m_lru_w_a, m_lru_b_a, m_lru_w_x, m_lru_b_x,
                             m_lru_lambda, m_w_out_odd, m_w_mlp_up, m_w_mlp_down)))
    var = dict(zip(WEIGHTS, (v_norm_w, v_w_in_even, v_gla_w_a_up, v_gla_b_a, v_gla_norm_w, v_fox_b_f, v_w_out_even,
                             v_w_in_odd, v_rel_bias, v_conv_w, v_conv_b, v_lru_w_a, v_lru_b_a, v_lru_w_x, v_lru_b_x,
                             v_lru_lambda, v_w_out_odd, v_w_mlp_up, v_w_mlp_down)))
    ax, ay, ac = lax.axis_index("x"), lax.axis_index("y"), lax.axis_index("c")
    place = jnp.stack([ac, 2 * ax + ay, 4 * ax + 2 * ay + ac]).astype(jnp.int32)

    shard = {(n, l): (wts[n][l].T if n in TRANSPOSED else wts[n][l]).astype(WIRE_DTYPE) for n, l in MATRIX_BLOCKS}
    loss_blk, dx, sums, recv, repl_parts, vec_parts = _forward_backward(
        x[0], loss_target[0], shard, {n: wts[n] for n in VECTORS}, {n: wts[n] for n in REPLICATED}, place)
    loss = lax.psum(loss_blk[0, 0], ("x", "y", "c"))

    view = lambda n, a: jnp.swapaxes(a, 1, 2) if n in TRANSPOSED else a
    upd = {n: [view(n, o) for o in _adamw_sharded(
        [(sums[(n, l)], recv[(n, l)]) for l in range(wts[n].shape[0])], view(n, wts[n]), view(n, mom[n]), view(n, var[n]),
        place, f"adamw_{n}")] for n in MATRICES}
    small = REPLICATED + list(VECTORS)
    upd.update(zip(small, _adamw_small(repl_parts, vec_parts, [wts[n] for n in small], [mom[n] for n in small],
                                       [var[n] for n in small], place)))
    return (loss, dx[None], *[upd[n][kind] for kind in range(4) for n in WEIGHTS])
```

```python
import functools
from typing import Callable, NamedTuple

import jax
import jax.numpy as jnp
from jax import lax
from jax.experimental import pallas as pl
from jax.experimental.pallas import tpu as pltpu

F32 = jnp.float32
MXU_DTYPE = jnp.bfloat16
ACT_DTYPE = jnp.bfloat16
WIRE_DTYPE = jnp.bfloat16

V7X_VMEM_BYTES = 64 * 1024 * 1024
VMEM_LIMIT = (V7X_VMEM_BYTES * 7) // 8
LANES = 128

D_MODEL = 1024
SEQ = 2048
DEPTH = 2
CHUNK = 64
GROUP_WIDTH = D_MODEL // 2
D_FF = 4 * D_MODEL
NORM_EPS = 1e-6
GLA_HEADS = 4
GLA_DV = GROUP_WIDTH // GLA_HEADS
GLA_DK = GLA_DV // 2
GLA_KW = GLA_HEADS * GLA_DK
GLA_RANK = 16
GLA_GATE_TAU = 16.0
HEAD_DIM = 64
ATT_HEADS = GROUP_WIDTH // HEAD_DIM
CA_LEFT = 8 * CHUNK
REL_CLIP = 128
LRU_BLOCK_DIM = 64
CONV_WIDTH = 4
LRU_C = 8.0
N_DEV = 8

ADAM_LR = 0.001
ADAM_B1 = 0.9
ADAM_B2 = 0.999
ADAM_EPS = 1e-08
ADAM_WD = 0.01
ADAM_STEP = 10

NEG = float(jnp.finfo(jnp.float32).min)
MESH = pl.DeviceIdType.MESH


def _params(*sem):
    return pltpu.CompilerParams(dimension_semantics=sem, vmem_limit_bytes=VMEM_LIMIT)


def _dot(a, b, ca=1, cb=0):
    return lax.dot_general(a.astype(MXU_DTYPE), b.astype(MXU_DTYPE), (((ca,), (cb,)), ((), ())),
                           preferred_element_type=F32)


def _dot_exact(a, b):
    return lax.dot_general(a, b, (((1,), (0,)), ((), ())), precision=lax.Precision.HIGHEST,
                           preferred_element_type=F32)


def _log_sigmoid(x):
    return jnp.minimum(x, 0.0) - jnp.log1p(jnp.exp(-jnp.abs(x)))


def _iota(shape, axis):
    return lax.broadcasted_iota(jnp.int32, shape, axis)


ANY = pl.BlockSpec(memory_space=pl.ANY)
N_CHIPS = 4


class _Plan(NamedTuple):
    ins: list
    outs: list
    sems: list
    start: Callable
    finish: Callable


def _place():
    x, y, c = lax.axis_index("x"), lax.axis_index("y"), lax.axis_index("c")
    return x, y, c, [(1 - x, y), (x, 1 - y), (1 - x, 1 - y)]


def _gather_plan(xs):
    n = len(xs)

    def parts(x_refs, out_refs, sems):
        send_sems, recv_sems, local_sems = sems
        x, y, c, chips = _place()
        me, sibling = (x, y, c), (x, y, 1 - c)

        def rows(a, px, py, pc):
            return out_refs[a].at[4 * px + 2 * py + pc]

        def copy(a, k, block, to, src=None):
            return pltpu.make_async_remote_copy(
                src_ref=rows(a, *block) if src is None else src, dst_ref=rows(a, *block),
                send_sem=send_sems.at[7 * a + k], recv_sem=recv_sems.at[7 * a + k], device_id=to, device_id_type=MESH)

        mine = [pltpu.make_async_copy(x_refs[a], rows(a, *me), local_sems.at[a]) for a in range(n)]
        first = []
        for a in range(n):
            first.append(copy(a, 0, me, sibling, src=x_refs[a]))
            first += [copy(a, 1 + j, me, (*chip, c), src=x_refs[a]) for j, chip in enumerate(chips)]
        return c, me, sibling, chips, copy, mine, first

    def start(x_refs, out_refs, sems):
        *_, mine, first = parts(x_refs, out_refs, sems)
        for cp in first + mine:
            cp.start()

    def finish(x_refs, out_refs, sems):
        c, me, sibling, chips, copy, mine, first = parts(x_refs, out_refs, sems)
        passed = []
        for j, chip in enumerate(chips):
            for a in range(n):
                copy(a, 1 + j, (*chip, c), me).wait_recv()
                passed.append(copy(a, 4 + j, (*chip, c), sibling))
                passed[-1].start()
        for a in range(n):
            copy(a, 0, sibling, me).wait_recv()
            for j, chip in enumerate(chips):
                copy(a, 4 + j, (*chip, 1 - c), me).wait_recv()
        for cp in first + passed:
            cp.wait_send()
        for cp in mine:
            cp.wait()

    return _Plan(list(xs), [jax.ShapeDtypeStruct((N_DEV,) + x.shape, x.dtype) for x in xs],
                 [pltpu.SemaphoreType.DMA((7 * n,)), pltpu.SemaphoreType.DMA((7 * n,)), pltpu.SemaphoreType.DMA((n,))],
                 start, finish)


def _exchange_plan(copies_of, ins, outs, per_array):
    n = len(ins)

    def start(in_refs, out_refs, sems):
        for cp in copies_of(in_refs, out_refs, sems):
            cp.start()

    def finish(in_refs, out_refs, sems):
        copies = copies_of(in_refs, out_refs, sems)
        for cp in copies:
            cp.wait_recv()
        for cp in copies:
            cp.wait_send()

    return _Plan(list(ins), outs, [pltpu.SemaphoreType.DMA((per_array * n,)), pltpu.SemaphoreType.DMA((per_array * n,))],
                 start, finish)


def _sibling_plan(gs):
    def copies_of(g_refs, got_refs, sems):
        x, y, c, _ = _place()
        return [pltpu.make_async_remote_copy(
            src_ref=g_refs[a].at[2 * k + (1 - c)], dst_ref=got_refs[a].at[k], send_sem=sems[0].at[N_CHIPS * a + k],
            recv_sem=sems[1].at[N_CHIPS * a + k], device_id=(x, y, 1 - c), device_id_type=MESH)
            for a in range(len(gs)) for k in range(N_CHIPS)]

    return _exchange_plan(copies_of, gs, [jax.ShapeDtypeStruct((N_CHIPS,) + g.shape[1:], g.dtype) for g in gs], N_CHIPS)


def _chip_plan(ss):
    def copies_of(s_refs, out_refs, sems):
        x, y, c, chips = _place()
        return [pltpu.make_async_remote_copy(
            src_ref=s_refs[a].at[2 * px + py], dst_ref=out_refs[a].at[j], send_sem=sems[0].at[3 * a + j],
            recv_sem=sems[1].at[3 * a + j], device_id=(px, py, c), device_id_type=MESH)
            for a in range(len(ss)) for j, (px, py) in enumerate(chips)]

    return _exchange_plan(copies_of, ss, [jax.ShapeDtypeStruct((3,) + s.shape[1:], s.dtype) for s in ss], 3)


def _run_plan(plan, name):
    n_in, n_out = len(plan.ins), len(plan.outs)

    def body(*refs):
        args = refs[:n_in], refs[n_in:n_in + n_out], refs[n_in + n_out:]
        plan.start(*args)
        plan.finish(*args)

    return pl.pallas_call(body, out_shape=plan.outs, in_specs=[ANY] * n_in, out_specs=[ANY] * n_out,
                          scratch_shapes=plan.sems, name=name)(*plan.ins)


def _pcall(body, ride, *, grid, in_specs, out_specs, out_shape, scratch_shapes=(), semantics, name):
    if ride is None:
        return pl.pallas_call(body, grid=grid, in_specs=in_specs, out_specs=out_specs, out_shape=out_shape,
                              scratch_shapes=list(scratch_shapes), compiler_params=_params(*semantics), name=name)
    single = not isinstance(out_shape, (list, tuple))
    out_specs_l, out_shape_l = ([out_specs], [out_shape]) if single else (list(out_specs), list(out_shape))
    n_in, n_out, n_scr = len(in_specs), len(out_shape_l), len(scratch_shapes)
    r_in, r_out = len(ride.ins), len(ride.outs)

    def riding(*refs):
        cuts = [n_in, r_in, n_out, r_out, n_scr]
        groups, at = [], 0
        for width in cuts:
            groups.append(refs[at:at + width])
            at += width
        ins, r_ins, outs, r_outs, scr = groups
        sems = refs[at:]
        first = functools.reduce(jnp.logical_and, [pl.program_id(d) == 0 for d in range(len(grid))])
        last = functools.reduce(jnp.logical_and, [pl.program_id(d) == grid[d] - 1 for d in range(len(grid))])

        @pl.when(first)
        def _():
            ride.start(r_ins, r_outs, sems)

        body(*ins, *outs, *scr)

        @pl.when(last)
        def _():
            ride.finish(r_ins, r_outs, sems)

    call = pl.pallas_call(
        riding, grid=grid, in_specs=list(in_specs) + [ANY] * r_in, out_specs=out_specs_l + [ANY] * r_out,
        out_shape=out_shape_l + list(ride.outs), scratch_shapes=list(scratch_shapes) + list(ride.sems),
        compiler_params=_params(*(["arbitrary"] * len(grid))), name=name)

    def run(*args):
        res = call(*args, *ride.ins)
        return (res[0] if single else list(res[:n_out])), list(res[n_out:])

    return run


def _mm(a, b, *, nt=False, ta=False, out_dtype, tm, tn, a_sqrelu=False, drelu_of=None, b_blocked=False,
        out_blocked=False, name, ride=None):
    k, m = a.shape if ta else a.shape[::-1]
    if b_blocked:
        assert not nt and b.shape[1] == k and b.shape[2] == tn
        n = b.shape[0] * tn
    else:
        n = b.shape[0] if nt else b.shape[1]
        assert (b.shape[1] if nt else b.shape[0]) == k
    tm, tn = min(tm, m), min(tn, n)
    assert m % tm == 0 and n % tn == 0

    def body(*refs):
        a_ref, b_ref = refs[0], refs[1]
        o_ref = refs[-1]
        av = a_ref[...]
        if a_sqrelu:
            av = jnp.square(jnp.maximum(av.astype(F32), 0.0))
        acc = _dot(av, b_ref[...], 0 if ta else 1, 1 if nt else 0)
        if drelu_of is not None:
            acc = acc * (2.0 * jnp.maximum(refs[2][...].astype(F32), 0.0))
        o_ref[...] = acc.astype(out_dtype)

    if b_blocked:
        b_spec = pl.BlockSpec((None, k, tn), lambda i, j: (j, 0, 0))
    elif nt:
        b_spec = pl.BlockSpec((tn, k), lambda i, j: (j, 0))
    else:
        b_spec = pl.BlockSpec((k, tn), lambda i, j: (0, j))
    a_spec = pl.BlockSpec((k, tm), lambda i, j: (0, i)) if ta else pl.BlockSpec((tm, k), lambda i, j: (i, 0))
    in_specs = [a_spec, b_spec]
    args = [a, b]
    if drelu_of is not None:
        in_specs.append(pl.BlockSpec((tm, tn), lambda i, j: (i, j)))
        args.append(drelu_of)
    if out_blocked:
        out_spec = pl.BlockSpec((None, tm, tn), lambda i, j: (j, i, 0))
        out_shape = jax.ShapeDtypeStruct((n // tn, m, tn), out_dtype)
    else:
        out_spec = pl.BlockSpec((tm, tn), lambda i, j: (i, j))
        out_shape = jax.ShapeDtypeStruct((m, n), out_dtype)
    return _pcall(body, ride, grid=(m // tm, n // tn), in_specs=in_specs, out_specs=out_spec, out_shape=out_shape,
                  semantics=("parallel", "parallel"), name=name)(*args)


def _mm_nt_blocked(a, b, *, out_dtype, tm, tn, name):
    m = a.shape[0]
    p, n, kp = b.shape
    assert a.shape[1] == p * kp and m % tm == 0 and n % tn == 0

    def body(a_ref, b_ref, o_ref, acc_ref):
        @pl.when(pl.program_id(2) == 0)
        def _():
            acc_ref[...] = jnp.zeros_like(acc_ref)

        acc_ref[...] += _dot(a_ref[...], b_ref[...], 1, 1)

        @pl.when(pl.program_id(2) == p - 1)
        def _():
            o_ref[...] = acc_ref[...].astype(out_dtype)

    return pl.pallas_call(
        body, grid=(m // tm, n // tn, p),
        in_specs=[pl.BlockSpec((tm, kp), lambda i, j, q: (i, q)), pl.BlockSpec((None, tn, kp), lambda i, j, q: (q, j, 0))],
        out_specs=pl.BlockSpec((tm, tn), lambda i, j, q: (i, j)),
        out_shape=jax.ShapeDtypeStruct((m, n), out_dtype),
        scratch_shapes=[pltpu.VMEM((tm, tn), F32)],
        compiler_params=_params("parallel", "parallel", "arbitrary"), name=name)(a, b)


ROW_TILE = 512
TM_FWD, TM_DX, TM_DW, TN = 2048, 1024, 1024, 512


def _norm_fwd(x, w, *, out_dtype, res=None, name):
    t, d = x.shape

    def body(*refs):
        x_ref, w_ref, o_ref = refs[0], refs[1], refs[-1]
        xv = x_ref[...]
        y = xv * lax.rsqrt(jnp.mean(xv * xv, axis=-1, keepdims=True) + NORM_EPS) * w_ref[...]
        if res is not None:
            y = refs[2][...] + y
        o_ref[...] = y.astype(out_dtype)

    row = pl.BlockSpec((ROW_TILE, d), lambda i: (i, 0))
    in_specs = [row, pl.BlockSpec((1, d), lambda i: (0, 0))] + ([row] if res is not None else [])
    args = [x, w] + ([res] if res is not None else [])
    return pl.pallas_call(body, grid=(t // ROW_TILE,), in_specs=in_specs, out_specs=row,
                          out_shape=jax.ShapeDtypeStruct((t, d), out_dtype),
                          compiler_params=_params("parallel"), name=name)(*args)


def _norm_bwd(dy, x, w, *, out_dtype, add=None, name, ride=None):
    t, d = x.shape

    def body(*refs):
        dy_ref, x_ref, w_ref = refs[0], refs[1], refs[2]
        dx_ref, dw_ref = refs[-2], refs[-1]
        xv = x_ref[...]
        rstd = lax.rsqrt(jnp.mean(xv * xv, axis=-1, keepdims=True) + NORM_EPS)
        xhat = xv * rstd
        dyv = dy_ref[...].astype(F32)
        g = dyv * w_ref[...]
        dx = rstd * (g - xhat * jnp.mean(g * xhat, axis=-1, keepdims=True))
        if add is not None:
            dx = dx + refs[3][...]
        dx_ref[...] = dx.astype(out_dtype)

        @pl.when(pl.program_id(0) == 0)
        def _():
            dw_ref[...] = jnp.zeros_like(dw_ref)

        dw_ref[...] += jnp.sum(dyv * xhat, axis=0, keepdims=True)

    row = pl.BlockSpec((ROW_TILE, d), lambda i: (i, 0))
    vec = pl.BlockSpec((1, d), lambda i: (0, 0))
    in_specs = [row, row, vec] + ([row] if add is not None else [])
    args = [dy, x, w] + ([add] if add is not None else [])
    return _pcall(body, ride, grid=(t // ROW_TILE,), in_specs=in_specs, out_specs=[row, vec],
                  out_shape=[jax.ShapeDtypeStruct((t, d), out_dtype), jax.ShapeDtypeStruct((1, d), F32)],
                  semantics=("arbitrary",), name=name)(*args)


def _loss_fwd_bwd(y, target):
    t, d = y.shape

    def body(y_ref, t_ref, l_ref, dy_ref):
        diff = y_ref[...] - t_ref[...]
        dy_ref[...] = diff * (1.0 / d)

        @pl.when(pl.program_id(0) == 0)
        def _():
            l_ref[...] = jnp.zeros_like(l_ref)

        l_ref[...] += 0.5 * jnp.sum(jnp.mean(diff * diff, axis=-1, keepdims=True), axis=0, keepdims=True)

    row = pl.BlockSpec((ROW_TILE, d), lambda i: (i, 0))
    return pl.pallas_call(body, grid=(t // ROW_TILE,), in_specs=[row, row],
                          out_specs=[pl.BlockSpec((8, LANES), lambda i: (0, 0)), row],
                          out_shape=[jax.ShapeDtypeStruct((8, LANES), F32), jax.ShapeDtypeStruct((t, d), F32)],
                          compiler_params=_params("arbitrary"), name="loss")(y, target)


GLA_STATE = (GLA_HEADS * GLA_DV, GLA_KW)


def _gla_specs(chunk_of):
    rows = lambda width, col: pl.BlockSpec((CHUNK, width), lambda i: (chunk_of(i), col))
    const = lambda r, c: pl.BlockSpec((r, c), lambda i: (0, 0))
    return [rows(GLA_KW, 0),
            rows(GLA_KW, 1),
            rows(GROUP_WIDTH, 1),
            rows(GROUP_WIDTH, 0),
            rows(LANES, 4),
            const(LANES, GLA_KW),
            const(1, GLA_KW),
            const(1, GROUP_WIDTH)]


def _gla_chunk(q_ref, k_ref, v_ref, a_ref, wup_ref, ba_ref):
    z = _dot(a_ref[...], wup_ref[...]) + ba_ref[...]
    tri = (_iota((CHUNK, CHUNK), 1) <= _iota((CHUNK, CHUNK), 0)).astype(F32)
    cum = _dot_exact(tri, _log_sigmoid(z) * (1.0 / GLA_GATE_TAU))
    tot = cum[CHUNK - 1:CHUNK, :]
    e = jnp.exp(tot - cum)
    return (z, e, jnp.exp(tot), k_ref[...].astype(F32) * e, q_ref[...].astype(F32) * (GLA_DK ** -0.5),
            v_ref[...].astype(F32))


def _gla_head_mask():
    return _iota(GLA_STATE, 0) // GLA_DV == _iota(GLA_STATE, 1) // GLA_DK


def _gla_fwd(pmm, pel, w_up, b_a, gnorm_w, ride=None):
    t = pmm.shape[0]
    nc = t // CHUNK

    def body(q_ref, k_ref, v_ref, r_ref, a_ref, wup_ref, ba_ref, gw_ref, o_ref, st_ref, m_scr):
        @pl.when(pl.program_id(0) == 0)
        def _():
            m_scr[...] = jnp.zeros_like(m_scr)

        _, _, decay, kd, qs, vv = _gla_chunk(q_ref, k_ref, v_ref, a_ref, wup_ref, ba_ref)
        m = m_scr[...] * decay + jnp.where(_gla_head_mask(), _dot(vv, kd, 0, 0), 0.0)
        m_scr[...] = m
        st_ref[...] = m
        o = _dot(qs, m, 1, 1)
        rr = r_ref[...]
        gate = rr * jax.nn.sigmoid(rr) * gw_ref[...]
        for h in range(GLA_HEADS):
            vs = slice(h * GLA_DV, (h + 1) * GLA_DV)
            oh = o[:, vs]
            y = oh * lax.rsqrt(jnp.mean(oh * oh, axis=-1, keepdims=True) + NORM_EPS)
            o_ref[:, vs] = (y * gate[:, vs]).astype(o_ref.dtype)

    return _pcall(
        body, ride, grid=(nc,), in_specs=_gla_specs(lambda i: i),
        out_specs=[pl.BlockSpec((CHUNK, GROUP_WIDTH), lambda i: (i, 0)),
                   pl.BlockSpec((None,) + GLA_STATE, lambda i: (i, 0, 0))],
        out_shape=[jax.ShapeDtypeStruct((t, GROUP_WIDTH), ACT_DTYPE), jax.ShapeDtypeStruct((nc,) + GLA_STATE, F32)],
        scratch_shapes=[pltpu.VMEM(GLA_STATE, F32)],
        semantics=("arbitrary",), name="gla_fwd")(pmm, pmm, pmm, pel, pel, w_up, b_a, gnorm_w)


def _gla_bwd(pmm, pel, w_up, b_a, gnorm_w, states, dmix, ride=None):
    t = pmm.shape[0]
    nc = t // CHUNK
    scale = GLA_DK ** -0.5

    def body(q_ref, k_ref, v_ref, r_ref, a_ref, wup_ref, ba_ref, gw_ref, st_ref, prev_ref, do_ref,
             dq_ref, dk_ref, dv_ref, dr_ref, da_ref, dwup_ref, dba_ref, dgw_ref, dm_scr):
        step = pl.program_id(0)

        @pl.when(step == 0)
        def _():
            dm_scr[...] = jnp.zeros_like(dm_scr)
            dwup_ref[...] = jnp.zeros_like(dwup_ref)
            dba_ref[...] = jnp.zeros_like(dba_ref)
            dgw_ref[...] = jnp.zeros_like(dgw_ref)

        z, e, decay, kd, qs, vv = _gla_chunk(q_ref, k_ref, v_ref, a_ref, wup_ref, ba_ref)
        m = st_ref[...]
        m_prev = prev_ref[...] * (step < nc - 1).astype(F32)
        rr, dout, gw = r_ref[...], do_ref[...], gw_ref[...]
        sig = jax.nn.sigmoid(rr)
        silu = rr * sig
        dsilu = sig * (1.0 + rr * (1.0 - sig))
        o = _dot(qs, m, 1, 1)
        d_o, dgw = [], []
        for h in range(GLA_HEADS):
            vs = slice(h * GLA_DV, (h + 1) * GLA_DV)
            oh, dg = o[:, vs], dout[:, vs]
            rstd = lax.rsqrt(jnp.mean(oh * oh, axis=-1, keepdims=True) + NORM_EPS)
            y = oh * rstd
            dgw.append(jnp.sum(dg * y * silu[:, vs], axis=0, keepdims=True))
            dr_ref[:, vs] = (dg * y * gw[:, vs] * dsilu[:, vs]).astype(dr_ref.dtype)
            dy = dg * gw[:, vs] * silu[:, vs]
            d_o.append(rstd * (dy - y * jnp.mean(dy * y, axis=-1, keepdims=True)))
        d_o = jnp.concatenate(d_o, axis=1)
        dgw_ref[...] += jnp.concatenate(dgw, axis=1)
        dq_ref[...] = (_dot(d_o, m) * scale).astype(dq_ref.dtype)
        dm = dm_scr[...] + jnp.where(_gla_head_mask(), _dot(d_o, qs, 0, 0), 0.0)
        dv_ref[...] = _dot(kd, dm, 1, 1).astype(dv_ref.dtype)
        dkd = _dot(vv, dm)
        dk_ref[...] = (dkd * e).astype(dk_ref.dtype)
        dm_scr[...] = dm * decay
        tri_strict = (_iota((CHUNK, CHUNK), 1) < _iota((CHUNK, CHUNK), 0)).astype(F32)
        dla = jnp.sum(dm * m_prev, axis=0, keepdims=True) * decay + _dot_exact(tri_strict, dkd * kd)
        dz = dla * jax.nn.sigmoid(-z) * (1.0 / GLA_GATE_TAU)
        da_ref[...] = _dot(dz, wup_ref[...], 1, 1).astype(da_ref.dtype)
        dwup_ref[...] += _dot(a_ref[...], dz, 0, 0)
        dba_ref[...] += jnp.sum(dz, axis=0, keepdims=True)

    chunk_of = lambda i: nc - 1 - i
    in_specs = _gla_specs(chunk_of) + [
        pl.BlockSpec((None,) + GLA_STATE, lambda i: (chunk_of(i), 0, 0)),
        pl.BlockSpec((None,) + GLA_STATE, lambda i: (jnp.maximum(chunk_of(i) - 1, 0), 0, 0)),
        pl.BlockSpec((CHUNK, GROUP_WIDTH), lambda i: (chunk_of(i), 0))]
    rows = lambda width: pl.BlockSpec((CHUNK, width), lambda i: (chunk_of(i), 0))
    const = lambda r, c: pl.BlockSpec((r, c), lambda i: (0, 0))
    return _pcall(
        body, ride, grid=(nc,), in_specs=in_specs,
        out_specs=[rows(GLA_KW), rows(GLA_KW), rows(GROUP_WIDTH), rows(GROUP_WIDTH), rows(LANES),
                   const(LANES, GLA_KW), const(1, GLA_KW), const(1, GROUP_WIDTH)],
        out_shape=[jax.ShapeDtypeStruct((t, GLA_KW), ACT_DTYPE), jax.ShapeDtypeStruct((t, GLA_KW), ACT_DTYPE),
                   jax.ShapeDtypeStruct((t, GROUP_WIDTH), ACT_DTYPE), jax.ShapeDtypeStruct((t, GROUP_WIDTH), ACT_DTYPE),
                   jax.ShapeDtypeStruct((t, LANES), ACT_DTYPE), jax.ShapeDtypeStruct((LANES, GLA_KW), F32),
                   jax.ShapeDtypeStruct((1, GLA_KW), F32), jax.ShapeDtypeStruct((1, GROUP_WIDTH), F32)],
        scratch_shapes=[pltpu.VMEM(GLA_STATE, F32)],
        semantics=("arbitrary",), name="gla_bwd")(
            pmm, pmm, pmm, pel, pel, w_up, b_a, gnorm_w, states, states, dmix)


CUM_BLOCK = 256


def _fox_gate_fwd(pel, b_f):
    t = pel.shape[0]
    nb = t // CUM_BLOCK

    def body(f_ref, b_ref, cum_ref, cum_t_ref):
        tri = (_iota((CUM_BLOCK, CUM_BLOCK), 1) <= _iota((CUM_BLOCK, CUM_BLOCK), 0)).astype(F32)
        carry = jnp.zeros((1, LANES), F32)
        for blk in range(nb):
            rows = slice(blk * CUM_BLOCK, (blk + 1) * CUM_BLOCK)
            cum = _dot_exact(tri, _log_sigmoid(f_ref[rows, :] + b_ref[...])) + carry
            cum_ref[rows, :] = cum
            cum_t_ref[blk] = cum.T[:ATT_HEADS, :]
            carry = cum[CUM_BLOCK - 1:CUM_BLOCK, :]

    return pl.pallas_call(
        body, grid=(1,),
        in_specs=[pl.BlockSpec((t, LANES), lambda i: (0, 5)), pl.BlockSpec((1, LANES), lambda i: (0, 0))],
        out_specs=[pl.BlockSpec((t, LANES), lambda i: (0, 0)),
                   pl.BlockSpec((nb, ATT_HEADS, CUM_BLOCK), lambda i: (0, 0, 0))],
        out_shape=[jax.ShapeDtypeStruct((t, LANES), F32), jax.ShapeDtypeStruct((nb, ATT_HEADS, CUM_BLOCK), F32)],
        compiler_params=_params("arbitrary"), name="fox_gate_fwd")(pel, b_f)


def _fox_gate_bwd(pel, b_f, dcum_t, dcum_q):
    t = pel.shape[0]
    nb = t // CUM_BLOCK

    def body(f_ref, b_ref, dct_ref, dcq_ref, df_ref, db_ref):
        tri_up = (_iota((CUM_BLOCK, CUM_BLOCK), 1) >= _iota((CUM_BLOCK, CUM_BLOCK), 0)).astype(F32)
        carry = jnp.zeros((1, LANES), F32)
        db = jnp.zeros((1, LANES), F32)
        for blk in reversed(range(nb)):
            rows = slice(blk * CUM_BLOCK, (blk + 1) * CUM_BLOCK)
            dls = _dot_exact(tri_up, dct_ref[blk].T + dcq_ref[rows, :]) + carry
            carry = dls[0:1, :]
            df = dls * jax.nn.sigmoid(-(f_ref[rows, :] + b_ref[...]))
            df_ref[rows, :] = df.astype(df_ref.dtype)
            db = db + jnp.sum(df, axis=0, keepdims=True)
        db_ref[...] = db

    return pl.pallas_call(
        body, grid=(1,),
        in_specs=[pl.BlockSpec((t, LANES), lambda i: (0, 5)), pl.BlockSpec((1, LANES), lambda i: (0, 0)),
                  pl.BlockSpec((nb, LANES, CUM_BLOCK), lambda i: (0, 0, 0)), pl.BlockSpec((t, LANES), lambda i: (0, 0))],
        out_specs=[pl.BlockSpec((t, LANES), lambda i: (0, 0)), pl.BlockSpec((1, LANES), lambda i: (0, 0))],
        out_shape=[jax.ShapeDtypeStruct((t, LANES), ACT_DTYPE), jax.ShapeDtypeStruct((1, LANES), F32)],
        compiler_params=_params("arbitrary"), name="fox_gate_bwd")(pel, b_f, dcum_t, dcum_q)


FOX_Q_BLOCK = 256


assert FOX_Q_BLOCK == CUM_BLOCK


def _fox_scores(q_ref, k_ref, cum_ref, cum_t_ref, h, i):
    hs = slice(h * HEAD_DIM, (h + 1) * HEAD_DIM)
    nb = cum_t_ref.shape[0]
    key_gate = jnp.concatenate([cum_t_ref[kb, h:h + 1, :] for kb in range(nb)], axis=1)
    s = _dot(q_ref[:, hs], k_ref[:, hs], 1, 1) * (HEAD_DIM ** -0.5) + (cum_ref[:, h:h + 1] - key_gate)
    shape = (FOX_Q_BLOCK, nb * FOX_Q_BLOCK)
    return jnp.where(_iota(shape, 1) <= i * FOX_Q_BLOCK + _iota(shape, 0), s, NEG)


def _fox_specs(t):
    bq, nb = FOX_Q_BLOCK, t // FOX_Q_BLOCK
    return [pl.BlockSpec((bq, GROUP_WIDTH), lambda i: (i, 2)), pl.BlockSpec((t, GROUP_WIDTH), lambda i: (0, 3)),
            pl.BlockSpec((t, GROUP_WIDTH), lambda i: (0, 4)), pl.BlockSpec((bq, LANES), lambda i: (i, 0)),
            pl.BlockSpec((nb, ATT_HEADS, bq), lambda i: (0, 0, 0))]


def _fox_fwd(pmm, cum, cum_t, ride=None):
    t = pmm.shape[0]
    bq = FOX_Q_BLOCK

    def body(q_ref, k_ref, v_ref, cum_ref, cum_t_ref, o_ref, lse_ref):
        i = pl.program_id(0)
        lse_ref[...] = jnp.zeros_like(lse_ref)
        for h in range(ATT_HEADS):
            hs = slice(h * HEAD_DIM, (h + 1) * HEAD_DIM)
            s = _fox_scores(q_ref, k_ref, cum_ref, cum_t_ref, h, i)
            m = jnp.max(s, axis=-1, keepdims=True)
            p = jnp.exp(s - m)
            l = jnp.sum(p, axis=-1, keepdims=True)
            o_ref[:, hs] = (_dot(p, v_ref[:, hs]) / l).astype(o_ref.dtype)
            lse_ref[:, h:h + 1] = m + jnp.log(l)

    return _pcall(
        body, ride, grid=(t // bq,), in_specs=_fox_specs(t),
        out_specs=[pl.BlockSpec((bq, GROUP_WIDTH), lambda i: (i, 0)), pl.BlockSpec((bq, LANES), lambda i: (i, 0))],
        out_shape=[jax.ShapeDtypeStruct((t, GROUP_WIDTH), ACT_DTYPE), jax.ShapeDtypeStruct((t, LANES), F32)],
        semantics=("parallel",), name="fox_fwd")(pmm, pmm, pmm, cum, cum_t)


def _fox_bwd(pmm, cum, cum_t, lse, dmix, ride=None):
    t = pmm.shape[0]
    bq, nb = FOX_Q_BLOCK, t // FOX_Q_BLOCK
    scale = HEAD_DIM ** -0.5

    def body(q_ref, k_ref, v_ref, cum_ref, cum_t_ref, lse_ref, do_ref, dq_ref, dk_ref, dv_ref, dct_ref, dcq_ref):
        i = pl.program_id(0)

        @pl.when(i == 0)
        def _():
            dk_ref[...] = jnp.zeros_like(dk_ref)
            dv_ref[...] = jnp.zeros_like(dv_ref)
            dct_ref[...] = jnp.zeros_like(dct_ref)

        dcq_ref[...] = jnp.zeros_like(dcq_ref)
        for h in range(ATT_HEADS):
            hs = slice(h * HEAD_DIM, (h + 1) * HEAD_DIM)
            s = _fox_scores(q_ref, k_ref, cum_ref, cum_t_ref, h, i)
            p = jnp.exp(s - lse_ref[:, h:h + 1])
            do = do_ref[:, hs]
            dp = _dot(do, v_ref[:, hs], 1, 1)
            ds = p * (dp - jnp.sum(p * dp, axis=-1, keepdims=True))
            dq_ref[:, hs] = (_dot(ds, k_ref[:, hs]) * scale).astype(dq_ref.dtype)
            dk_ref[:, hs] += _dot(ds, q_ref[:, hs], 0, 0) * scale
            dv_ref[:, hs] += _dot(p, do, 0, 0)
            key_side = -jnp.sum(ds, axis=0, keepdims=True)
            for kb in range(nb):
                dct_ref[kb, h:h + 1, :] += key_side[:, kb * bq:(kb + 1) * bq]
            dcq_ref[:, h:h + 1] = jnp.sum(ds, axis=1, keepdims=True)

    whole = pl.BlockSpec((t, GROUP_WIDTH), lambda i: (0, 0))
    return _pcall(
        body, ride, grid=(t // bq,),
        in_specs=_fox_specs(t) + [pl.BlockSpec((bq, LANES), lambda i: (i, 0)),
                                  pl.BlockSpec((bq, GROUP_WIDTH), lambda i: (i, 1))],
        out_specs=[pl.BlockSpec((bq, GROUP_WIDTH), lambda i: (i, 0)), whole, whole,
                   pl.BlockSpec((nb, LANES, bq), lambda i: (0, 0, 0)), pl.BlockSpec((bq, LANES), lambda i: (i, 0))],
        out_shape=[jax.ShapeDtypeStruct((t, GROUP_WIDTH), ACT_DTYPE), jax.ShapeDtypeStruct((t, GROUP_WIDTH), F32),
                   jax.ShapeDtypeStruct((t, GROUP_WIDTH), F32), jax.ShapeDtypeStruct((nb, LANES, bq), F32),
                   jax.ShapeDtypeStruct((t, LANES), F32)],
        semantics=("arbitrary",), name="fox_bwd")(pmm, pmm, pmm, cum, cum_t, lse, dmix)


CA_Q_BLOCK = 4 * CHUNK
CA_WINDOW = CA_Q_BLOCK + CA_LEFT
CA_BASE = 1024


def _ca_bias_base(rel_bias):
    n = rel_bias.shape[0]
    flat = CA_Q_BLOCK + CA_LEFT - REL_CLIP
    tail = CA_BASE - flat - (2 * REL_CLIP + 1)
    return jnp.concatenate([jnp.broadcast_to(rel_bias[:, 2 * REL_CLIP:], (n, flat)), rel_bias[:, ::-1],
                            jnp.broadcast_to(rel_bias[:, :1], (n, tail))], axis=1)


def _ca_bias_base_grad(dbase):
    flat = CA_Q_BLOCK + CA_LEFT - REL_CLIP
    mid = dbase[:, flat:flat + 2 * REL_CLIP + 1][:, ::-1]
    lo = jnp.sum(dbase[:, flat + 2 * REL_CLIP + 1:], axis=1, keepdims=True)
    hi = jnp.sum(dbase[:, :flat], axis=1, keepdims=True)
    pad = jnp.zeros((dbase.shape[0], 2 * REL_CLIP - 1), F32)
    return mid + jnp.concatenate([lo, pad, hi], axis=1)


def _ca_mask(i):
    r, j = _iota((CA_Q_BLOCK, CA_WINDOW), 0), _iota((CA_Q_BLOCK, CA_WINDOW), 1)
    rc, jc = r // CHUNK, j // CHUNK
    return (jc >= rc) & (jc <= rc + CA_LEFT // CHUNK) & (i * CA_Q_BLOCK + j >= CA_LEFT)


def _ca_fill_bias(i, base_ref, bias_scr):
    @pl.when(i == 0)
    def _():
        for h in range(ATT_HEADS):
            rows = jnp.broadcast_to(base_ref[h:h + 1, :], (CA_Q_BLOCK, CA_BASE))
            bias_scr[h] = pltpu.roll(rows, CA_BASE - CA_Q_BLOCK, 1, stride=1, stride_axis=0)[:, :CA_WINDOW]


def _ca_scores(q_ref, kp_ref, bias_scr, win, h, mask):
    hs = slice(h * HEAD_DIM, (h + 1) * HEAD_DIM)
    s = _dot(q_ref[:, hs], kp_ref[win, hs], 1, 1) * (HEAD_DIM ** -0.5)
    return jnp.where(mask, s + bias_scr[h], NEG)


CA_BIAS_SCRATCH = pltpu.VMEM((ATT_HEADS, CA_Q_BLOCK, CA_WINDOW), F32)


def _ca_fwd(pmm, kp, vp, base, ride=None):
    t = pmm.shape[0]

    def body(q_ref, kp_ref, vp_ref, base_ref, o_ref, lse_ref, bias_scr):
        i = pl.program_id(0)
        _ca_fill_bias(i, base_ref, bias_scr)
        win = pl.ds(pl.multiple_of(i * CA_Q_BLOCK, CA_Q_BLOCK), CA_WINDOW)
        mask = _ca_mask(i)
        lse_ref[...] = jnp.zeros_like(lse_ref)
        for h in range(ATT_HEADS):
            hs = slice(h * HEAD_DIM, (h + 1) * HEAD_DIM)
            s = _ca_scores(q_ref, kp_ref, bias_scr, win, h, mask)
            m = jnp.max(s, axis=-1, keepdims=True)
            p = jnp.exp(s - m)
            l = jnp.sum(p, axis=-1, keepdims=True)
            o_ref[:, hs] = (_dot(p, vp_ref[win, hs]) / l).astype(o_ref.dtype)
            lse_ref[:, h:h + 1] = m + jnp.log(l)

    padded = pl.BlockSpec((t + CA_LEFT, GROUP_WIDTH), lambda i: (0, 0))
    return _pcall(
        body, ride, grid=(t // CA_Q_BLOCK,),
        in_specs=[pl.BlockSpec((CA_Q_BLOCK, GROUP_WIDTH), lambda i: (i, 0)), padded, padded,
                  pl.BlockSpec((ATT_HEADS, CA_BASE), lambda i: (0, 0))],
        out_specs=[pl.BlockSpec((CA_Q_BLOCK, GROUP_WIDTH), lambda i: (i, 0)),
                   pl.BlockSpec((CA_Q_BLOCK, LANES), lambda i: (i, 0))],
        out_shape=[jax.ShapeDtypeStruct((t, GROUP_WIDTH), ACT_DTYPE), jax.ShapeDtypeStruct((t, LANES), F32)],
        scratch_shapes=[CA_BIAS_SCRATCH], semantics=("arbitrary",), name="ca_fwd")(pmm, kp, vp, base)


def _ca_bwd(pmm, kp, vp, base, lse, dmix, ride=None):
    t = pmm.shape[0]
    scale = HEAD_DIM ** -0.5

    def body(q_ref, kp_ref, vp_ref, base_ref, lse_ref, do_ref, dq_ref, dkp_ref, dvp_ref, dbase_ref, bias_scr):
        i = pl.program_id(0)
        _ca_fill_bias(i, base_ref, bias_scr)

        @pl.when(i == 0)
        def _():
            dkp_ref[...] = jnp.zeros_like(dkp_ref)
            dvp_ref[...] = jnp.zeros_like(dvp_ref)
            dbase_ref[...] = jnp.zeros_like(dbase_ref)

        win = pl.ds(pl.multiple_of(i * CA_Q_BLOCK, CA_Q_BLOCK), CA_WINDOW)
        mask = _ca_mask(i)
        flip = (_iota((CA_Q_BLOCK, CA_Q_BLOCK), 0) + _iota((CA_Q_BLOCK, CA_Q_BLOCK), 1) == CA_Q_BLOCK - 1).astype(F32)
        for h in range(ATT_HEADS):
            hs = slice(h * HEAD_DIM, (h + 1) * HEAD_DIM)
            s = _ca_scores(q_ref, kp_ref, bias_scr, win, h, mask)
            p = jnp.exp(s - lse_ref[:, h:h + 1])
            do = do_ref[:, hs]
            dp = _dot(do, vp_ref[win, hs], 1, 1)
            ds = p * (dp - jnp.sum(p * dp, axis=-1, keepdims=True))
            dq_ref[:, hs] = (_dot(ds, kp_ref[win, hs]) * scale).astype(dq_ref.dtype)
            dkp_ref[win, hs] += _dot(ds, q_ref[:, hs], 0, 0) * scale
            dvp_ref[win, hs] += _dot(p, do, 0, 0)
            rev = jnp.concatenate([_dot(flip, ds), jnp.zeros((CA_Q_BLOCK, CA_BASE - CA_WINDOW), F32)], axis=1)
            lined = pltpu.roll(rev, 1, 1, stride=1, stride_axis=0)
            dbase_ref[h:h + 1, :] += jnp.sum(lined, axis=0, keepdims=True)

    padded = pl.BlockSpec((t + CA_LEFT, GROUP_WIDTH), lambda i: (0, 0))
    return _pcall(
        body, ride, grid=(t // CA_Q_BLOCK,),
        in_specs=[pl.BlockSpec((CA_Q_BLOCK, GROUP_WIDTH), lambda i: (i, 0)), padded, padded,
                  pl.BlockSpec((ATT_HEADS, CA_BASE), lambda i: (0, 0)),
                  pl.BlockSpec((CA_Q_BLOCK, LANES), lambda i: (i, 0)),
                  pl.BlockSpec((CA_Q_BLOCK, GROUP_WIDTH), lambda i: (i, 0))],
        out_specs=[pl.BlockSpec((CA_Q_BLOCK, GROUP_WIDTH), lambda i: (i, 0)), padded, padded,
                   pl.BlockSpec((ATT_HEADS, CA_BASE), lambda i: (0, 0))],
        out_shape=[jax.ShapeDtypeStruct((t, GROUP_WIDTH), ACT_DTYPE),
                   jax.ShapeDtypeStruct((t + CA_LEFT, GROUP_WIDTH), F32),
                   jax.ShapeDtypeStruct((t + CA_LEFT, GROUP_WIDTH), F32),
                   jax.ShapeDtypeStruct((ATT_HEADS, CA_BASE), F32)],
        scratch_shapes=[CA_BIAS_SCRATCH], semantics=("arbitrary",), name="ca_bwd")(pmm, kp, vp, base, lse, dmix)


GELU_C = 0.7978845608028654
GELU_A = 0.044715


def _shift_down(v, k, fill):
    return jnp.where(_iota(v.shape, 0) >= k, pltpu.roll(v, k, 0), fill)


def _shift_up(v, k, fill):
    t = v.shape[0]
    return jnp.where(_iota(v.shape, 0) < t - k, pltpu.roll(v, t - k, 0), fill)


def _linear_scan(a, b, shift):
    k = 1
    while k < a.shape[0]:
        b = a * shift(b, k, 0.0) + b
        a = a * shift(a, k, 1.0)
        k *= 2
    return b


def _neg_expm1(y):
    series = -y * (1.0 + y * (0.5 + y * (1.0 / 6.0 + y * (1.0 / 24.0 + y * (1.0 / 120.0)))))
    return jnp.where(y > -0.1, series, 1.0 - jnp.exp(y))


def _lru_forward(x, g_in, cw, cb, wa, ba, wx, bx, lam):
    xs = [_shift_down(x, CONV_WIDTH - 1 - j, 0.0) for j in range(CONV_WIDTH - 1)] + [x]
    xc = cb + sum(cw[j:j + 1, :] * xs[j] for j in range(CONV_WIDTH))
    r = jax.nn.sigmoid(_dot(xc, wa) + ba)
    i = jax.nn.sigmoid(_dot(xc, wx) + bx)
    lsl = _log_sigmoid(lam)
    la = LRU_C * r * lsl
    a = jnp.exp(la)
    s = jnp.sqrt(_neg_expm1(2.0 * la))
    h = _linear_scan(a, s * (i * xc), _shift_down)
    u = GELU_C * (g_in + GELU_A * g_in * g_in * g_in)
    th = jnp.tanh(u)
    gelu = 0.5 * g_in * (1.0 + th)
    return xs, xc, r, i, lsl, a, s, h, th, gelu


def _lru_specs(t):
    col = lambda off: pl.BlockSpec((t, LANES), lambda j: (0, j + off))
    vec = pl.BlockSpec((1, LANES), lambda j: (0, j))
    mat = pl.BlockSpec((None, LANES, LANES), lambda j: (j, 0, 0))
    return [col(0), col(GROUP_WIDTH // LANES), pl.BlockSpec((CONV_WIDTH, LANES), lambda j: (0, j)),
            vec, mat, vec, mat, vec, vec]


def _lru_fwd(pel, conv_w, conv_b, wa, ba, wx, bx, lam, ride=None):
    t = pel.shape[0]

    def body(g_ref, x_ref, cw_ref, cb_ref, wa_ref, ba_ref, wx_ref, bx_ref, lam_ref, o_ref):
        res = _lru_forward(x_ref[...], g_ref[...], cw_ref[...], cb_ref[...], wa_ref[...], ba_ref[...],
                           wx_ref[...], bx_ref[...], lam_ref[...])
        o_ref[...] = (res[7] * res[9]).astype(o_ref.dtype)

    return _pcall(
        body, ride, grid=(GROUP_WIDTH // LANES,), in_specs=_lru_specs(t),
        out_specs=pl.BlockSpec((t, LANES), lambda j: (0, j)),
        out_shape=jax.ShapeDtypeStruct((t, GROUP_WIDTH), ACT_DTYPE),
        semantics=("parallel",), name="lru_fwd")(pel, pel, conv_w, conv_b, wa, ba, wx, bx, lam)


def _lru_bwd(pel, conv_w, conv_b, wa, ba, wx, bx, lam, dmix, ride=None):
    t = pel.shape[0]

    def body(g_ref, x_ref, cw_ref, cb_ref, wa_ref, ba_ref, wx_ref, bx_ref, lam_ref, do_ref,
             dg_ref, dx_ref, dcw_ref, dcb_ref, dwa_ref, dba_ref, dwx_ref, dbx_ref, dlam_ref):
        g_in, cw, lam = g_ref[...], cw_ref[...], lam_ref[...]
        xs, xc, r, i, lsl, a, s, h, th, gelu = _lru_forward(
            x_ref[...], g_in, cw, cb_ref[...], wa_ref[...], ba_ref[...], wx_ref[...], bx_ref[...], lam)
        dout = do_ref[...]
        dgelu = 0.5 * (1.0 + th) + 0.5 * g_in * (1.0 - th * th) * GELU_C * (1.0 + 3.0 * GELU_A * g_in * g_in)
        dg_ref[...] = (dout * h * dgelu).astype(dg_ref.dtype)
        gsum = _linear_scan(_shift_up(a, 1, 0.0), dout * gelu, _shift_up)
        da = gsum * _shift_down(h, 1, 0.0)
        di = gsum * s * xc
        dla = da * a - gsum * (i * xc) * (a * a / s)
        dlam_ref[...] = jnp.sum(dla * (LRU_C * r), axis=0, keepdims=True) * jax.nn.sigmoid(-lam)
        dpr = dla * (LRU_C * lsl) * r * (1.0 - r)
        dpi = di * i * (1.0 - i)
        dxc = gsum * s * i + _dot(dpr, wa_ref[...], 1, 1) + _dot(dpi, wx_ref[...], 1, 1)
        xct = xc.T
        dwa_ref[...] = _dot(xct, dpr)
        dwx_ref[...] = _dot(xct, dpi)
        dba_ref[...] = jnp.sum(dpr, axis=0, keepdims=True)
        dbx_ref[...] = jnp.sum(dpi, axis=0, keepdims=True)
        dcb_ref[...] = jnp.sum(dxc, axis=0, keepdims=True)
        for j in range(CONV_WIDTH):
            dcw_ref[j:j + 1, :] = jnp.sum(dxc * xs[j], axis=0, keepdims=True)
        dx = cw[CONV_WIDTH - 1:CONV_WIDTH, :] * dxc
        for j in range(CONV_WIDTH - 1):
            dx = dx + cw[j:j + 1, :] * _shift_up(dxc, CONV_WIDTH - 1 - j, 0.0)
        dx_ref[...] = dx.astype(dx_ref.dtype)

    col = pl.BlockSpec((t, LANES), lambda j: (0, j))
    vec = pl.BlockSpec((1, LANES), lambda j: (0, j))
    mat = pl.BlockSpec((None, LANES, LANES), lambda j: (j, 0, 0))
    nb = GROUP_WIDTH // LANES
    vshape = jax.ShapeDtypeStruct((1, GROUP_WIDTH), F32)
    mshape = jax.ShapeDtypeStruct((nb, LANES, LANES), F32)
    return _pcall(
        body, ride, grid=(nb,),
        in_specs=_lru_specs(t) + [pl.BlockSpec((t, LANES), lambda j: (0, j + nb))],
        out_specs=[col, col, pl.BlockSpec((CONV_WIDTH, LANES), lambda j: (0, j)), vec, mat, vec, mat, vec, vec],
        out_shape=[jax.ShapeDtypeStruct((t, GROUP_WIDTH), ACT_DTYPE), jax.ShapeDtypeStruct((t, GROUP_WIDTH), ACT_DTYPE),
                   jax.ShapeDtypeStruct((CONV_WIDTH, GROUP_WIDTH), F32), vshape, mshape, vshape, mshape, vshape, vshape],
        semantics=("parallel",), name="lru_bwd")(
            pel, pel, conv_w, conv_b, wa, ba, wx, bx, lam, dmix)


def _block_diag_pairs(w):
    z = jnp.zeros((LRU_BLOCK_DIM, LRU_BLOCK_DIM), w.dtype)
    return jnp.stack([jnp.block([[w[2 * j], z], [z, w[2 * j + 1]]]) for j in range(w.shape[0] // 2)])


def _block_diag_pairs_grad(dw):
    b = LRU_BLOCK_DIM
    return jnp.stack([dw[n // 2, (n % 2) * b:(n % 2 + 1) * b, (n % 2) * b:(n % 2 + 1) * b] for n in range(2 * dw.shape[0])])


def _row_tile(r):
    return ROW_TILE if r % ROW_TILE == 0 else r


def _pair_sum(g, got, place, name):
    _, r, c = g.shape
    tile = r

    def body(place_ref, a_ref, b_ref, o_ref):
        o_ref[...] = (a_ref[...].astype(F32) + b_ref[...].astype(F32)).astype(o_ref.dtype)

    blk = pl.BlockSpec((1, tile, c), lambda k, i, place_ref: (k, i, 0))
    return pl.pallas_call(
        body,
        grid_spec=pltpu.PrefetchScalarGridSpec(
            num_scalar_prefetch=1, grid=(N_CHIPS, r // tile),
            in_specs=[pl.BlockSpec((1, tile, c), lambda k, i, place_ref: (2 * k + place_ref[0], i, 0)), blk],
            out_specs=blk),
        out_shape=jax.ShapeDtypeStruct(got.shape, got.dtype),
        compiler_params=_params("parallel", "parallel"), name=name)(place, g, got)


def _adamw_update(g, w_ref, m_ref, v_ref, g_ref, d_ref, nm_ref, nv_ref):
    nm = ADAM_B1 * m_ref[...] + (1.0 - ADAM_B1) * g
    nv = ADAM_B2 * v_ref[...] + (1.0 - ADAM_B2) * jnp.square(g)
    m_hat = nm / (1.0 - ADAM_B1 ** ADAM_STEP)
    v_hat = nv / (1.0 - ADAM_B2 ** ADAM_STEP)
    g_ref[...] = g
    d_ref[...] = -ADAM_LR * (m_hat / (jnp.sqrt(v_hat) + ADAM_EPS) + ADAM_WD * w_ref[...])
    nm_ref[...] = nm
    nv_ref[...] = nv


def _adamw_sharded(parts, w, m, v, place, name):
    n_layers, r, c = w.shape
    tile = _row_tile(r)
    nb = r // tile

    def body(place_ref, *refs):
        layer = pl.program_id(0)
        g = None
        for l in range(n_layers):
            s_ref, r_ref = refs[2 * l], refs[2 * l + 1]
            g_l = s_ref[0].astype(F32) + r_ref[0].astype(F32) + r_ref[1].astype(F32) + r_ref[2].astype(F32)
            g = g_l if g is None else jnp.where(layer == l, g_l, g)
        _adamw_update(g, *refs[2 * n_layers:])

    def part_specs(l):
        rows = lambda q, i: jnp.where(q < l, 0, jnp.where(q > l, nb - 1, i))
        return [pl.BlockSpec((1, tile, c), lambda q, i, place_ref: (place_ref[1], rows(q, i), 0)),
                pl.BlockSpec((3, tile, c), lambda q, i, place_ref: (0, rows(q, i), 0))]

    in_specs, args = [], []
    for l, (s, recv) in enumerate(parts):
        in_specs += part_specs(l)
        args += [s, recv]
    blk = pl.BlockSpec((None, tile, c), lambda q, i, place_ref: (q, i, 0))
    out = jax.ShapeDtypeStruct((n_layers, r, c), F32)
    return pl.pallas_call(
        body,
        grid_spec=pltpu.PrefetchScalarGridSpec(
            num_scalar_prefetch=1, grid=(n_layers, nb), in_specs=in_specs + [blk, blk, blk],
            out_specs=[blk, blk, blk, blk]),
        out_shape=[out, out, out, out], compiler_params=_params("arbitrary", "arbitrary"), name=name)(
            place, *args, w, m, v)


def _adamw_small(repl_parts, vec_parts, w, m, v, place):
    n_r, n = len(repl_parts), len(w)
    shapes = [a.shape for a in w]

    def body(place_ref, *refs):
        parts, rest = refs[:n], refs[n:]
        for k in range(n):
            take = (lambda p: parts[k][p]) if k < n_r else (lambda p: parts[k][p, 0])
            g = take(0)
            for p in range(1, N_DEV):
                g = g + take(p)
            _adamw_update(g, rest[k], rest[n + k], rest[2 * n + k], *rest[3 * n + 4 * k:3 * n + 4 * k + 4])

    def whole(shape):
        return pl.BlockSpec(shape, lambda i, place_ref: (0,) * len(shape))

    def mine(shard):
        return pl.BlockSpec((N_DEV, 1) + shard, lambda i, place_ref: (0, place_ref[2]) + (0,) * len(shard))

    in_specs = [whole(a.shape) for a in repl_parts] + [mine(s) for s in shapes[n_r:]] + [whole(s) for s in shapes] * 3
    outs = pl.pallas_call(
        body,
        grid_spec=pltpu.PrefetchScalarGridSpec(
            num_scalar_prefetch=1, grid=(1,), in_specs=in_specs,
            out_specs=[whole(s) for s in shapes for _ in range(4)]),
        out_shape=[jax.ShapeDtypeStruct(s, F32) for s in shapes for _ in range(4)],
        compiler_params=_params("arbitrary"), name="adamw_small")(place, *repl_parts, *vec_parts, *w, *m, *v)
    return [outs[4 * k:4 * k + 4] for k in range(n)]


SHARDED = {"norm_w": 2, "w_in_even": 2, "gla_w_a_up": 2, "w_out_even": 1, "w_in_odd": 2, "conv_w": 2, "conv_b": 1,
           "lru_b_a": 1, "lru_b_x": 1, "lru_lambda": 1, "w_out_odd": 1, "w_mlp_up": 2, "w_mlp_down": 1}
REPLICATED = ["gla_b_a", "gla_norm_w", "fox_b_f", "rel_bias", "lru_w_a", "lru_w_x"]
WEIGHTS = ["norm_w", "w_in_even", "gla_w_a_up", "gla_b_a", "gla_norm_w", "fox_b_f", "w_out_even", "w_in_odd",
           "rel_bias", "conv_w", "conv_b", "lru_w_a", "lru_b_a", "lru_w_x", "lru_b_x", "lru_lambda", "w_out_odd",
           "w_mlp_up", "w_mlp_down"]
MATRICES = ("w_in_even", "w_out_even", "w_in_odd", "w_out_odd", "w_mlp_up", "w_mlp_down")
TRANSPOSED = ("w_in_even", "w_in_odd")
VECTORS = tuple(n for n in SHARDED if n not in MATRICES)
MATRIX_BLOCKS = (("w_in_even", 0), ("w_out_even", 0), ("w_in_odd", 0), ("w_out_odd", 0),
                 ("w_mlp_up", 0), ("w_mlp_up", 1), ("w_mlp_down", 0), ("w_mlp_down", 1))


def _join_shards(blocks, axis):
    moved = jnp.moveaxis(blocks, 0, axis)
    shape = moved.shape
    return moved.reshape(shape[:axis] + (shape[axis] * shape[axis + 1],) + shape[axis + 2:])


def _split_shards(full, axis):
    shape = full.shape
    cut = full.reshape(shape[:axis] + (N_DEV, shape[axis] // N_DEV) + shape[axis + 1:])
    return jnp.moveaxis(cut, axis, 0)


EVEN_SPLITS = (0, 256, 512, 1024, 1536, 1552, 2064, 2576, 3088, 3096)


def _even_in_split(wt):
    c = [wt[EVEN_SPLITS[k]:EVEN_SPLITS[k + 1]] for k in range(9)]
    gq, gk, gv, gr, ga, fq, fk, fv, ff = c
    padrows = lambda a: jnp.pad(a, ((0, LANES - a.shape[0]), (0, 0)))
    return jnp.concatenate([gq, gk, gv, fq, fk, fv], axis=0), jnp.concatenate([gr, padrows(ga), padrows(ff)], axis=0)


def _even_in_merge(dmm, dele):
    return jnp.concatenate([dmm[:1024], dele[:512], dele[512:512 + GLA_RANK], dmm[1024:2560],
                            dele[640:640 + ATT_HEADS]], axis=0)


def _forward_backward(x, target, shard, vec_shard, w, place):
    w = dict(w)
    g, dnorm, sums, recv = {}, {}, {}, {}
    nrm = lambda l, k: w["norm_w"][l, k][None, :]
    gather = lambda *keys: _gather_plan([shard[k] for k in keys])
    blocks = lambda r, c: (N_DEV, r // N_DEV, c)

    def pair_sum(key):
        sums[key] = _pair_sum(g[key], got[key], place, f"rs_pair_sum_{key[0]}_{key[1]}")

    got = {}

    def mlp_fwd(xin, layer, ride_up, ride_down):
        h = _norm_fwd(xin, nrm(layer, 2), out_dtype=ACT_DTYPE, name=f"norm_mlp_{layer}")
        u = _mm(h, w["w_mlp_up"][layer], out_dtype=ACT_DTYPE, tm=TM_FWD, tn=D_FF // N_DEV, b_blocked=True,
                name=f"mlp_up_{layer}", ride=ride_up)
        u, rode_up = u if ride_up is not None else (u, None)
        if w["w_mlp_down"][layer] is None:
            w["w_mlp_down"][layer] = rode_up[0].reshape(D_FF, D_MODEL)
        yv = _mm(u, w["w_mlp_down"][layer], out_dtype=F32, tm=TM_DX, tn=TN, a_sqrelu=True,
                 name=f"mlp_down_{layer}", ride=ride_down)
        yv, rode_down = yv if ride_down is not None else (yv, None)
        xout = _norm_fwd(yv, nrm(layer, 3), out_dtype=F32, res=xin, name=f"norm_mlp_out_{layer}")
        return xout, (xin, h, u, yv), rode_up, rode_down

    def mlp_bwd(dxout, saved, layer, ride):
        xin, h, u, yv = saved
        k_up, k_down = ("w_mlp_up", layer), ("w_mlp_down", layer)
        dy, dnorm[(layer, 3)] = _norm_bwd(dxout, yv, nrm(layer, 3), out_dtype=ACT_DTYPE, name=f"norm_mlp_out_bwd_{layer}")
        du = _mm(dy, w["w_mlp_down"][layer], nt=True, out_dtype=ACT_DTYPE, tm=TM_DX, tn=TN, drelu_of=u,
                 name=f"mlp_down_dx_{layer}", ride=ride)
        rode = None
        if ride is not None:
            du, rode = du
        g[k_down] = _mm(u, dy, ta=True, out_dtype=WIRE_DTYPE, tm=TM_DW, tn=TN, a_sqrelu=True,
                        name=f"mlp_down_dw_{layer}").reshape(blocks(D_FF, D_MODEL))
        g[k_up] = _mm(h, du, ta=True, out_dtype=WIRE_DTYPE, tm=TM_DW, tn=D_FF // N_DEV, out_blocked=True,
                      name=f"mlp_up_dw_{layer}")
        w_up = jnp.moveaxis(w["w_mlp_up"][layer], 0, 1).reshape(D_MODEL, D_FF)
        dh, (got[k_down], got[k_up]) = _mm(du, w_up, nt=True, out_dtype=F32, tm=TM_DX, tn=TN, name=f"mlp_up_dx_{layer}",
                                           ride=_sibling_plan([g[k_down], g[k_up]]))
        pair_sum(k_down)
        pair_sum(k_up)
        dxin, dnorm[(layer, 2)] = _norm_bwd(dh, xin, nrm(layer, 2), out_dtype=F32, add=dxout, name=f"norm_mlp_bwd_{layer}")
        return dxin, rode

    first = _run_plan(_gather_plan([shard[("w_in_even", 0)]] + [vec_shard[n] for n in VECTORS]),
                      "weights_all_gather_first")
    w["w_in_even"] = first[0].reshape(-1, D_MODEL)
    for n, b in zip(VECTORS, first[1:]):
        w[n] = _join_shards(b, SHARDED[n])
    w["w_mlp_up"], w["w_mlp_down"] = [None] * DEPTH, [None] * DEPTH

    wmm_e, wel_e = _even_in_split(w["w_in_even"])
    w_up_pad = jnp.pad(w["gla_w_a_up"][0], ((0, LANES - GLA_RANK), (0, 0)))
    b_f_pad = jnp.pad(w["fox_b_f"], ((0, 0), (0, LANES - ATT_HEADS)))
    h0 = _norm_fwd(x, nrm(0, 0), out_dtype=ACT_DTYPE, name="norm_in_0")
    pmm0, (w_out_even,) = _mm(h0, wmm_e, nt=True, out_dtype=ACT_DTYPE, tm=TM_FWD, tn=TN, name="in_even_mm",
                              ride=gather(("w_out_even", 0)))
    pel0 = _mm(h0, wel_e, nt=True, out_dtype=F32, tm=TM_FWD, tn=768, name="in_even_el")
    (out_a, states), (w_in_odd,) = _gla_fwd(pmm0, pel0, w_up_pad, w["gla_b_a"], w["gla_norm_w"],
                                            ride=gather(("w_in_odd", 0)))
    cum, cum_t = _fox_gate_fwd(pel0, b_f_pad)
    (out_b, lse_b), (w["w_mlp_up"][0], w_mlp_down0) = _fox_fwd(pmm0, cum, cum_t,
                                                               ride=gather(("w_mlp_up", 0), ("w_mlp_down", 0)))
    w["w_out_even"] = w_out_even.reshape(D_MODEL, D_MODEL)
    w["w_mlp_down"][0] = w_mlp_down0.reshape(D_FF, D_MODEL)
    mix_in0 = jnp.concatenate([out_a, out_b], axis=1)
    mix0 = _mm(mix_in0, w["w_out_even"], out_dtype=F32, tm=TM_FWD, tn=TN, name="out_even")
    x1 = _norm_fwd(mix0, nrm(0, 1), out_dtype=F32, res=x, name="norm_mix_0")
    x2, mlp0, _, (w["w_mlp_up"][1],) = mlp_fwd(x1, 0, None, gather(("w_mlp_up", 1)))
    w["w_in_odd"] = w_in_odd.reshape(-1, D_MODEL)

    w_in_o = w["w_in_odd"]
    n_mm_o = 3 * GROUP_WIDTH
    wa_bd, wx_bd = _block_diag_pairs(w["lru_w_a"][0]), _block_diag_pairs(w["lru_w_x"][0])
    base = _ca_bias_base(w["rel_bias"][0])
    h1 = _norm_fwd(x2, nrm(1, 0), out_dtype=ACT_DTYPE, name="norm_in_1")
    pmm1 = _mm(h1, w_in_o[:n_mm_o], nt=True, out_dtype=ACT_DTYPE, tm=TM_FWD, tn=TN, name="in_odd_mm")
    pel1 = _mm(h1, w_in_o[n_mm_o:], nt=True, out_dtype=F32, tm=TM_FWD, tn=TN, name="in_odd_el")
    kp = jnp.pad(pmm1[:, GROUP_WIDTH:2 * GROUP_WIDTH], ((CA_LEFT, 0), (0, 0)))
    vp = jnp.pad(pmm1[:, 2 * GROUP_WIDTH:], ((CA_LEFT, 0), (0, 0)))
    (out_c, lse_c), (w_mlp_down1,) = _ca_fwd(pmm1, kp, vp, base, ride=gather(("w_mlp_down", 1)))
    w["w_mlp_down"][1] = w_mlp_down1.reshape(D_FF, D_MODEL)
    lru_args = (pel1, w["conv_w"][0], w["conv_b"], wa_bd, w["lru_b_a"], wx_bd, w["lru_b_x"], w["lru_lambda"])
    out_d, (w_out_odd,) = _lru_fwd(*lru_args, ride=gather(("w_out_odd", 0)))
    w["w_out_odd"] = w_out_odd.reshape(D_MODEL, D_MODEL)
    mix_in1 = jnp.concatenate([out_c, out_d], axis=1)
    mix1 = _mm(mix_in1, w["w_out_odd"], out_dtype=F32, tm=TM_FWD, tn=TN, name="out_odd")
    x3 = _norm_fwd(mix1, nrm(1, 1), out_dtype=F32, res=x2, name="norm_mix_1")
    x4, mlp1, _, _ = mlp_fwd(x3, 1, None, None)

    loss, dx4 = _loss_fwd_bwd(x4, target)

    k_oo, k_io, k_oe, k_ie = ("w_out_odd", 0), ("w_in_odd", 0), ("w_out_even", 0), ("w_in_even", 0)
    mlp_keys = lambda l: [("w_mlp_down", l), ("w_mlp_up", l)]
    dx3, _ = mlp_bwd(dx4, mlp1, 1, None)
    dmix1, dnorm[(1, 1)] = _norm_bwd(dx3, mix1, nrm(1, 1), out_dtype=ACT_DTYPE, name="norm_mix_bwd_1")
    g[k_oo] = _mm(mix_in1, dmix1, ta=True, out_dtype=WIRE_DTYPE, tm=TM_DW, tn=TN, name="out_odd_dw").reshape(
        blocks(D_MODEL, D_MODEL))
    dmix_in1, (got[k_oo],) = _mm(dmix1, w["w_out_odd"], nt=True, out_dtype=F32, tm=TM_DX, tn=TN, name="out_odd_dx",
                                 ride=_sibling_plan([g[k_oo]]))
    (dq_c, dkp, dvp, dbase), rode = _ca_bwd(pmm1, kp, vp, base, lse_c, dmix_in1,
                                            ride=_chip_plan([sums[k] for k in mlp_keys(1)]))
    recv.update(zip(mlp_keys(1), rode))
    pair_sum(k_oo)
    (dgate, dxin, g_conv_w, g_conv_b, dwa_bd, g_lru_b_a, dwx_bd, g_lru_b_x, g_lru_lambda), (recv[k_oo],) = _lru_bwd(
        *lru_args, dmix_in1, ride=_chip_plan([sums[k_oo]]))
    dp1 = jnp.concatenate([dq_c, dkp[CA_LEFT:].astype(ACT_DTYPE), dvp[CA_LEFT:].astype(ACT_DTYPE), dgate, dxin], axis=1)
    g[k_io] = _mm(dp1, h1, ta=True, out_dtype=WIRE_DTYPE, tm=dp1.shape[1] // 2, tn=TN, name="in_odd_dw").reshape(
        blocks(dp1.shape[1], D_MODEL))
    dh1, (got[k_io],) = _mm(dp1, w_in_o, out_dtype=F32, tm=TM_DX, tn=TN, name="in_odd_dx",
                            ride=_sibling_plan([g[k_io]]))
    pair_sum(k_io)
    dx2, dnorm[(1, 0)] = _norm_bwd(dh1, x2, nrm(1, 0), out_dtype=F32, add=dx3, name="norm_in_bwd_1")
    g["rel_bias"] = _ca_bias_base_grad(dbase)[None]
    g["conv_w"], g["conv_b"] = g_conv_w[None], g_conv_b
    g["lru_w_a"], g["lru_w_x"] = _block_diag_pairs_grad(dwa_bd)[None], _block_diag_pairs_grad(dwx_bd)[None]
    g["lru_b_a"], g["lru_b_x"], g["lru_lambda"] = g_lru_b_a, g_lru_b_x, g_lru_lambda

    dx1, (recv[k_io],) = mlp_bwd(dx2, mlp0, 0, _chip_plan([sums[k_io]]))
    dmix0, dnorm[(0, 1)] = _norm_bwd(dx1, mix0, nrm(0, 1), out_dtype=ACT_DTYPE, name="norm_mix_bwd_0")
    g[k_oe] = _mm(mix_in0, dmix0, ta=True, out_dtype=WIRE_DTYPE, tm=TM_DW, tn=TN, name="out_even_dw").reshape(
        blocks(D_MODEL, D_MODEL))
    dmix_in0, (got[k_oe],) = _mm(dmix0, w["w_out_even"], nt=True, out_dtype=F32, tm=TM_DX, tn=TN, name="out_even_dx",
                                 ride=_sibling_plan([g[k_oe]]))
    k_md0, k_mu0 = mlp_keys(0)
    pair_sum(k_oe)
    dq_a, dk_a, dv_a, dr_a, da_a, dw_up_pad, g_gla_b_a, g_gla_norm_w = _gla_bwd(
        pmm0, pel0, w_up_pad, w["gla_b_a"], w["gla_norm_w"], states, dmix_in0)
    (dq_b, dk_b, dv_b, dcum_t, dcum_q), (recv[k_md0], recv[k_mu0], recv[k_oe]) = _fox_bwd(
        pmm0, cum, cum_t, lse_b, dmix_in0, ride=_chip_plan([sums[k_md0], sums[k_mu0], sums[k_oe]]))
    df_b, db_f = _fox_gate_bwd(pel0, b_f_pad, dcum_t, dcum_q)
    g["gla_w_a_up"] = dw_up_pad[:GLA_RANK][None]
    g["gla_b_a"], g["gla_norm_w"], g["fox_b_f"] = g_gla_b_a, g_gla_norm_w, db_f[:, :ATT_HEADS]
    dp0 = jnp.concatenate([dq_a, dk_a, dv_a, dq_b, dk_b.astype(ACT_DTYPE), dv_b.astype(ACT_DTYPE), dr_a, da_a, df_b],
                          axis=1)
    w_perm = jnp.concatenate([wmm_e, wel_e], axis=0)
    n_mm_e = wmm_e.shape[0]
    dw_perm, repl_parts = _mm(dp0, h0, ta=True, out_dtype=WIRE_DTYPE, tm=dp0.shape[1] // 2, tn=TN, name="in_even_dw",
                              ride=_gather_plan([g[n] for n in REPLICATED]))
    dw_even = _even_in_merge(dw_perm[:n_mm_e], dw_perm[n_mm_e:])
    g[k_ie] = dw_even.reshape(blocks(dw_even.shape[0], D_MODEL))
    dh0, (got[k_ie],) = _mm(dp0, w_perm, out_dtype=F32, tm=TM_DX, tn=TN, name="in_even_dx",
                            ride=_sibling_plan([g[k_ie]]))
    pair_sum(k_ie)
    (dx0, dnorm[(0, 0)]), (recv[k_ie],) = _norm_bwd(dh0, x, nrm(0, 0), out_dtype=F32, add=dx1, name="norm_in_bwd_0",
                                                     ride=_chip_plan([sums[k_ie]]))

    g["norm_w"] = jnp.stack([jnp.concatenate([dnorm[(l, k)] for k in range(4)], axis=0) for l in range(DEPTH)])
    vec_parts = _run_plan(_gather_plan([_split_shards(g[n], SHARDED[n]) for n in VECTORS]), "vector_grads_all_gather")
    return loss, dx0, sums, recv, repl_parts, vec_parts


def kernel(x, norm_w, w_in_even, gla_w_a_up, gla_b_a, gla_norm_w, fox_b_f, w_out_even, w_in_odd, rel_bias, conv_w, conv_b, lru_w_a, lru_b_a, lru_w_x, lru_b_x, lru_lambda, w_out_odd, w_mlp_up, w_mlp_down, loss_target, m_norm_w, m_w_in_even, m_gla_w_a_up, m_gla_b_a, m_gla_norm_w, m_fox_b_f, m_w_out_even, m_w_in_odd, m_rel_bias, m_conv_w, m_conv_b, m_lru_w_a, m_lru_b_a, m_lru_w_x, m_lru_b_x, m_lru_lambda, m_w_out_odd, m_w_mlp_up, m_w_mlp_down, v_norm_w, v_w_in_even, v_gla_w_a_up, v_gla_b_a, v_gla_norm_w, v_fox_b_f, v_w_out_even, v_w_in_odd, v_rel_bias, v_conv_w, v_conv_b, v_lru_w_a, v_lru_b_a, v_lru_w_x, v_lru_b_x, v_lru_lambda, v_w_out_odd, v_w_mlp_up, v_w_mlp_down):
    wts = dict(zip(WEIGHTS, (norm_w, w_in_even, gla_w_a_up, gla_b_a, gla_norm_w, fox_b_f, w_out_even, w_in_odd, rel_bias,
                             conv_w, conv_b, lru_w_a, lru_b_a, lru_w_x, lru_b_x, lru_lambda, w_out_odd, w_mlp_up,
                             w_mlp_down)))
    mom = dict(zip(WEIGHTS, (m_norm_w, m_w_in_even, m_gla_w_a_up, m_gla_b_a, m_gla_norm_w, m_fox_b_f, m_w_out_even,
                             m_w_in_odd, m_rel_bias, m_conv_w, m_conv_b, m_lru_w_a, m_lru_b_a, m_lru_w_x, m_lru_b_x,
                             m_lru_lambda, m_w_out_odd, m_w_mlp_up, m_w_mlp_down)))
    var = dict(zip(WEIGHTS, (v_norm_w, v_w_in_even, v_gla_w_a_up, v_gla_b_a, v_gla_norm_w, v_fox_b_f, v_w_out_even,
                             v_w_in_odd, v_rel_bias, v_conv_w, v_conv_b, v_lru_w_a, v_lru_b_a, v_lru_w_x, v_lru_b_x,
                             v_lru_lambda, v_w_out_odd, v_w_mlp_up, v_w_mlp_down)))
    ax, ay, ac = lax.axis_index("x"), lax.axis_index("y"), lax.axis_index("c")
    place = jnp.stack([ac, 2 * ax + ay, 4 * ax + 2 * ay + ac]).astype(jnp.int32)

    shard = {(n, l): (wts[n][l].T if n in TRANSPOSED else wts[n][l]).astype(WIRE_DTYPE) for n, l in MATRIX_BLOCKS}
    loss_blk, dx, sums, recv, repl_parts, vec_parts = _forward_backward(
        x[0], loss_target[0], shard, {n: wts[n] for n in VECTORS}, {n: wts[n] for n in REPLICATED}, place)
    loss = lax.psum(loss_blk[0, 0], ("x", "y", "c"))

    view = lambda n, a: jnp.swapaxes(a, 1, 2) if n in TRANSPOSED else a
    upd = {n: [view(n, o) for o in _adamw_sharded(
        [(sums[(n, l)], recv[(n, l)]) for l in range(wts[n].shape[0])], view(n, wts[n]), view(n, mom[n]), view(n, var[n]),
        place, f"adamw_{n}")] for n in MATRICES}
    small = REPLICATED + list(VECTORS)
    upd.update(zip(small, _adamw_small(repl_parts, vec_parts, [wts[n] for n in small], [mom[n] for n in small],
                                       [var[n] for n in small], place)))
    return (loss, dx[None], *[upd[n][kind] for kind in range(4) for n in WEIGHTS])
```

```python
import functools
from typing import Callable, NamedTuple

import jax
import jax.numpy as jnp
from jax import lax
from jax.experimental import pallas as pl
from jax.experimental.pallas import tpu as pltpu

F32 = jnp.float32
MXU_DTYPE = jnp.bfloat16
ACT_DTYPE = jnp.bfloat16
WIRE_DTYPE = jnp.bfloat16

V7X_VMEM_BYTES = 64 * 1024 * 1024
VMEM_LIMIT = (V7X_VMEM_BYTES * 7) // 8
LANES = 128

D_MODEL = 1024
SEQ = 2048
DEPTH = 2
CHUNK = 64
GROUP_WIDTH = D_MODEL // 2
D_FF = 4 * D_MODEL
NORM_EPS = 1e-6
GLA_HEADS = 4
GLA_DV = GROUP_WIDTH // GLA_HEADS
GLA_DK = GLA_DV // 2
GLA_KW = GLA_HEADS * GLA_DK
GLA_RANK = 16
GLA_GATE_TAU = 16.0
HEAD_DIM = 64
ATT_HEADS = GROUP_WIDTH // HEAD_DIM
CA_LEFT = 8 * CHUNK
REL_CLIP = 128
LRU_BLOCK_DIM = 64
CONV_WIDTH = 4
LRU_C = 8.0
N_DEV = 8

ADAM_LR = 0.001
ADAM_B1 = 0.9
ADAM_B2 = 0.999
ADAM_EPS = 1e-08
ADAM_WD = 0.01
ADAM_STEP = 10

NEG = float(jnp.finfo(jnp.float32).min)
MESH = pl.DeviceIdType.MESH


def _params(*sem):
    return pltpu.CompilerParams(dimension_semantics=sem, vmem_limit_bytes=VMEM_LIMIT)


def _dot(a, b, ca=1, cb=0):
    return lax.dot_general(a.astype(MXU_DTYPE), b.astype(MXU_DTYPE), (((ca,), (cb,)), ((), ())),
                           preferred_element_type=F32)


def _dot_exact(a, b):
    return lax.dot_general(a, b, (((1,), (0,)), ((), ())), precision=lax.Precision.HIGHEST,
                           preferred_element_type=F32)


def _log_sigmoid(x):
    return jnp.minimum(x, 0.0) - jnp.log1p(jnp.exp(-jnp.abs(x)))


def _iota(shape, axis):
    return lax.broadcasted_iota(jnp.int32, shape, axis)


ANY = pl.BlockSpec(memory_space=pl.ANY)
N_CHIPS = 4


class _Plan(NamedTuple):
    ins: list
    outs: list
    sems: list
    start: Callable
    finish: Callable


def _place():
    x, y, c = lax.axis_index("x"), lax.axis_index("y"), lax.axis_index("c")
    return x, y, c, [(1 - x, y), (x, 1 - y), (1 - x, 1 - y)]


def _gather_plan(xs):
    n = len(xs)

    def parts(x_refs, out_refs, sems):
        send_sems, recv_sems, local_sems = sems
        x, y, c, chips = _place()
        me, sibling = (x, y, c), (x, y, 1 - c)

        def rows(a, px, py, pc):
            return out_refs[a].at[4 * px + 2 * py + pc]

        def copy(a, k, block, to, src=None):
            return pltpu.make_async_remote_copy(
                src_ref=rows(a, *block) if src is None else src, dst_ref=rows(a, *block),
                send_sem=send_sems.at[7 * a + k], recv_sem=recv_sems.at[7 * a + k], device_id=to, device_id_type=MESH)

        mine = [pltpu.make_async_copy(x_refs[a], rows(a, *me), local_sems.at[a]) for a in range(n)]
        first = []
        for a in range(n):
            first.append(copy(a, 0, me, sibling, src=x_refs[a]))
            first += [copy(a, 1 + j, me, (*chip, c), src=x_refs[a]) for j, chip in enumerate(chips)]
        return c, me, sibling, chips, copy, mine, first

    def start(x_refs, out_refs, sems):
        *_, mine, first = parts(x_refs, out_refs, sems)
        for cp in first + mine:
            cp.start()

    def finish(x_refs, out_refs, sems):
        c, me, sibling, chips, copy, mine, first = parts(x_refs, out_refs, sems)
        passed = []
        for j, chip in enumerate(chips):
            for a in range(n):
                copy(a, 1 + j, (*chip, c), me).wait_recv()
                passed.append(copy(a, 4 + j, (*chip, c), sibling))
                passed[-1].start()
        for a in range(n):
            copy(a, 0, sibling, me).wait_recv()
            for j, chip in enumerate(chips):
                copy(a, 4 + j, (*chip, 1 - c), me).wait_recv()
        for cp in first + passed:
            cp.wait_send()
        for cp in mine:
            cp.wait()

    return _Plan(list(xs), [jax.ShapeDtypeStruct((N_DEV,) + x.shape, x.dtype) for x in xs],
                 [pltpu.SemaphoreType.DMA((7 * n,)), pltpu.SemaphoreType.DMA((7 * n,)), pltpu.SemaphoreType.DMA((n,))],
                 start, finish)


def _exchange_plan(copies_of, ins, outs, per_array):
    n = len(ins)

    def start(in_refs, out_refs, sems):
        for cp in copies_of(in_refs, out_refs, sems):
            cp.start()

    def finish(in_refs, out_refs, sems):
        copies = copies_of(in_refs, out_refs, sems)
        for cp in copies:
            cp.wait_recv()
        for cp in copies:
            cp.wait_send()

    return _Plan(list(ins), outs, [pltpu.SemaphoreType.DMA((per_array * n,)), pltpu.SemaphoreType.DMA((per_array * n,))],
                 start, finish)


def _sibling_plan(gs):
    def copies_of(g_refs, got_refs, sems):
        x, y, c, _ = _place()
        return [pltpu.make_async_remote_copy(
            src_ref=g_refs[a].at[2 * k + (1 - c)], dst_ref=got_refs[a].at[k], send_sem=sems[0].at[N_CHIPS * a + k],
            recv_sem=sems[1].at[N_CHIPS * a + k], device_id=(x, y, 1 - c), device_id_type=MESH)
            for a in range(len(gs)) for k in range(N_CHIPS)]

    return _exchange_plan(copies_of, gs, [jax.ShapeDtypeStruct((N_CHIPS,) + g.shape[1:], g.dtype) for g in gs], N_CHIPS)


def _chip_plan(ss):
    def copies_of(s_refs, out_refs, sems):
        x, y, c, chips = _place()
        return [pltpu.make_async_remote_copy(
            src_ref=s_refs[a].at[2 * px + py], dst_ref=out_refs[a].at[j], send_sem=sems[0].at[3 * a + j],
            recv_sem=sems[1].at[3 * a + j], device_id=(px, py, c), device_id_type=MESH)
            for a in range(len(ss)) for j, (px, py) in enumerate(chips)]

    return _exchange_plan(copies_of, ss, [jax.ShapeDtypeStruct((3,) + s.shape[1:], s.dtype) for s in ss], 3)


def _run_plan(plan, name):
    n_in, n_out = len(plan.ins), len(plan.outs)

    def body(*refs):
        args = refs[:n_in], refs[n_in:n_in + n_out], refs[n_in + n_out:]
        plan.start(*args)
        plan.finish(*args)

    return pl.pallas_call(body, out_shape=plan.outs, in_specs=[ANY] * n_in, out_specs=[ANY] * n_out,
                          scratch_shapes=plan.sems, name=name)(*plan.ins)


def _pcall(body, ride, *, grid, in_specs, out_specs, out_shape, scratch_shapes=(), semantics, name):
    if ride is None:
        return pl.pallas_call(body, grid=grid, in_specs=in_specs, out_specs=out_specs, out_shape=out_shape,
                              scratch_shapes=list(scratch_shapes), compiler_params=_params(*semantics), name=name)
    single = not isinstance(out_shape, (list, tuple))
    out_specs_l, out_shape_l = ([out_specs], [out_shape]) if single else (list(out_specs), list(out_shape))
    n_in, n_out, n_scr = len(in_specs), len(out_shape_l), len(scratch_shapes)
    r_in, r_out = len(ride.ins), len(ride.outs)

    def riding(*refs):
        cuts = [n_in, r_in, n_out, r_out, n_scr]
        groups, at = [], 0
        for width in cuts:
            groups.append(refs[at:at + width])
            at += width
        ins, r_ins, outs, r_outs, scr = groups
        sems = refs[at:]
        first = functools.reduce(jnp.logical_and, [pl.program_id(d) == 0 for d in range(len(grid))])
        last = functools.reduce(jnp.logical_and, [pl.program_id(d) == grid[d] - 1 for d in range(len(grid))])

        @pl.when(first)
        def _():
            ride.start(r_ins, r_outs, sems)

        body(*ins, *outs, *scr)

        @pl.when(last)
        def _():
            ride.finish(r_ins, r_outs, sems)

    call = pl.pallas_call(
        riding, grid=grid, in_specs=list(in_specs) + [ANY] * r_in, out_specs=out_specs_l + [ANY] * r_out,
        out_shape=out_shape_l + list(ride.outs), scratch_shapes=list(scratch_shapes) + list(ride.sems),
        compiler_params=_params(*(["arbitrary"] * len(grid))), name=name)

    def run(*args):
        res = call(*args, *ride.ins)
        return (res[0] if single else list(res[:n_out])), list(res[n_out:])

    return run


def _rms(x):
    return x * lax.rsqrt(jnp.mean(x * x, axis=-1, keepdims=True) + NORM_EPS)


def _mm(a, b, *, nt=False, ta=False, out_dtype, tm, tn, a_sqrelu=False, drelu_of=None, b_blocked=False,
        out_blocked=False, a_norm=None, res_norm=None, name, ride=None):
    k, m = a.shape if ta else a.shape[::-1]
    if b_blocked:
        assert not nt and b.shape[1] == k and b.shape[2] == tn
        n = b.shape[0] * tn
    else:
        n = b.shape[0] if nt else b.shape[1]
        assert (b.shape[1] if nt else b.shape[0]) == k
    tm, tn = min(tm, m), min(tn, n)
    assert m % tm == 0 and n % tn == 0
    assert res_norm is None or tn == n
    n_in = 2 + (drelu_of is not None) + (a_norm is not None) + 2 * (res_norm is not None)

    def body(*refs):
        a_ref, b_ref = refs[0], refs[1]
        extra = list(refs[2:n_in])
        outs = list(refs[n_in:])
        o_ref = outs.pop(0)
        u_ref = extra.pop(0) if drelu_of is not None else None
        if a_norm is not None:
            wn_ref, h_ref, h_scr = extra.pop(0), outs.pop(0), outs.pop()

            @pl.when(pl.program_id(1) == 0)
            def _():
                h = (_rms(a_ref[...]) * wn_ref[...]).astype(ACT_DTYPE)
                h_scr[...] = h
                h_ref[...] = h

            av = h_scr[...]
        else:
            av = a_ref[...]
        if a_sqrelu:
            av = jnp.square(jnp.maximum(av.astype(F32), 0.0))
        acc = _dot(av, b_ref[...], 0 if ta else 1, 1 if nt else 0)
        if u_ref is not None:
            acc = acc * (2.0 * jnp.maximum(u_ref[...].astype(F32), 0.0))
        o_ref[...] = acc.astype(out_dtype)
        if res_norm is not None:
            res_ref, wr_ref = extra
            outs[0][...] = res_ref[...] + _rms(acc) * wr_ref[...]

    if b_blocked:
        b_spec = pl.BlockSpec((None, k, tn), lambda i, j: (j, 0, 0))
    elif nt:
        b_spec = pl.BlockSpec((tn, k), lambda i, j: (j, 0))
    else:
        b_spec = pl.BlockSpec((k, tn), lambda i, j: (0, j))
    a_spec = pl.BlockSpec((k, tm), lambda i, j: (0, i)) if ta else pl.BlockSpec((tm, k), lambda i, j: (i, 0))
    in_specs = [a_spec, b_spec]
    args = [a, b]
    if drelu_of is not None:
        in_specs.append(pl.BlockSpec((tm, tn), lambda i, j: (i, j)))
        args.append(drelu_of)
    if out_blocked:
        out_specs = [pl.BlockSpec((None, tm, tn), lambda i, j: (j, i, 0))]
        out_shape = [jax.ShapeDtypeStruct((n // tn, m, tn), out_dtype)]
    else:
        out_specs = [pl.BlockSpec((tm, tn), lambda i, j: (i, j))]
        out_shape = [jax.ShapeDtypeStruct((m, n), out_dtype)]
    scratch = []
    if a_norm is not None:
        assert not ta
        in_specs.append(pl.BlockSpec((1, k), lambda i, j: (0, 0)))
        args.append(a_norm)
        out_specs.append(pl.BlockSpec((tm, k), lambda i, j: (i, 0)))
        out_shape.append(jax.ShapeDtypeStruct((m, k), ACT_DTYPE))
        scratch.append(pltpu.VMEM((tm, k), ACT_DTYPE))
    if res_norm is not None:
        in_specs += [pl.BlockSpec((tm, n), lambda i, j: (i, 0)), pl.BlockSpec((1, n), lambda i, j: (0, 0))]
        args += list(res_norm)
        out_specs.append(pl.BlockSpec((tm, n), lambda i, j: (i, 0)))
        out_shape.append(jax.ShapeDtypeStruct((m, n), F32))
    single = len(out_shape) == 1
    return _pcall(body, ride, grid=(m // tm, n // tn), in_specs=in_specs,
                  out_specs=out_specs[0] if single else out_specs, out_shape=out_shape[0] if single else out_shape,
                  scratch_shapes=scratch, semantics=("parallel", "arbitrary"), name=name)(*args)


ROW_TILE = 512
TM_FWD, TM_DX, TM_DW, TN = 2048, 1024, 1024, 512


def _norm_bwd(dy, x, w, *, out_dtype, add=None, name, ride=None):
    t, d = x.shape

    def body(*refs):
        dy_ref, x_ref, w_ref = refs[0], refs[1], refs[2]
        dx_ref, dw_ref = refs[-2], refs[-1]
        xv = x_ref[...]
        rstd = lax.rsqrt(jnp.mean(xv * xv, axis=-1, keepdims=True) + NORM_EPS)
        xhat = xv * rstd
        dyv = dy_ref[...].astype(F32)
        g = dyv * w_ref[...]
        dx = rstd * (g - xhat * jnp.mean(g * xhat, axis=-1, keepdims=True))
        if add is not None:
            dx = dx + refs[3][...]
        dx_ref[...] = dx.astype(out_dtype)

        @pl.when(pl.program_id(0) == 0)
        def _():
            dw_ref[...] = jnp.zeros_like(dw_ref)

        dw_ref[...] += jnp.sum(dyv * xhat, axis=0, keepdims=True)

    row = pl.BlockSpec((ROW_TILE, d), lambda i: (i, 0))
    vec = pl.BlockSpec((1, d), lambda i: (0, 0))
    in_specs = [row, row, vec] + ([row] if add is not None else [])
    args = [dy, x, w] + ([add] if add is not None else [])
    return _pcall(body, ride, grid=(t // ROW_TILE,), in_specs=in_specs, out_specs=[row, vec],
                  out_shape=[jax.ShapeDtypeStruct((t, d), out_dtype), jax.ShapeDtypeStruct((1, d), F32)],
                  semantics=("arbitrary",), name=name)(*args)


def _loss_fwd_bwd(y, target):
    t, d = y.shape

    def body(y_ref, t_ref, l_ref, dy_ref):
        diff = y_ref[...] - t_ref[...]
        dy_ref[...] = diff * (1.0 / d)

        @pl.when(pl.program_id(0) == 0)
        def _():
            l_ref[...] = jnp.zeros_like(l_ref)

        l_ref[...] += 0.5 * jnp.sum(jnp.mean(diff * diff, axis=-1, keepdims=True), axis=0, keepdims=True)

    row = pl.BlockSpec((ROW_TILE, d), lambda i: (i, 0))
    return pl.pallas_call(body, grid=(t // ROW_TILE,), in_specs=[row, row],
                          out_specs=[pl.BlockSpec((8, LANES), lambda i: (0, 0)), row],
                          out_shape=[jax.ShapeDtypeStruct((8, LANES), F32), jax.ShapeDtypeStruct((t, d), F32)],
                          compiler_params=_params("arbitrary"), name="loss")(y, target)


GLA_STATE = (GLA_HEADS * GLA_DV, GLA_KW)


def _gla_specs(chunk_of):
    rows = lambda width, col: pl.BlockSpec((CHUNK, width), lambda i: (chunk_of(i), col))
    const = lambda r, c: pl.BlockSpec((r, c), lambda i: (0, 0))
    return [rows(GLA_KW, 0),
            rows(GLA_KW, 1),
            rows(GROUP_WIDTH, 1),
            rows(GROUP_WIDTH, 0),
            rows(LANES, 4),
            const(LANES, GLA_KW),
            const(1, GLA_KW),
            const(1, GROUP_WIDTH)]


def _gla_chunk(q_ref, k_ref, v_ref, a_ref, wup_ref, ba_ref):
    z = _dot(a_ref[...], wup_ref[...]) + ba_ref[...]
    tri = (_iota((CHUNK, CHUNK), 1) <= _iota((CHUNK, CHUNK), 0)).astype(F32)
    cum = _dot_exact(tri, _log_sigmoid(z) * (1.0 / GLA_GATE_TAU))
    tot = cum[CHUNK - 1:CHUNK, :]
    e = jnp.exp(tot - cum)
    return (z, e, jnp.exp(tot), k_ref[...].astype(F32) * e, q_ref[...].astype(F32) * (GLA_DK ** -0.5),
            v_ref[...].astype(F32))


def _gla_head_mask():
    return _iota(GLA_STATE, 0) // GLA_DV == _iota(GLA_STATE, 1) // GLA_DK


def _gla_fwd(pmm, pel, w_up, b_a, gnorm_w, ride=None):
    t = pmm.shape[0]
    nc = t // CHUNK

    def body(q_ref, k_ref, v_ref, r_ref, a_ref, wup_ref, ba_ref, gw_ref, o_ref, st_ref, m_scr):
        @pl.when(pl.program_id(0) == 0)
        def _():
            m_scr[...] = jnp.zeros_like(m_scr)

        _, _, decay, kd, qs, vv = _gla_chunk(q_ref, k_ref, v_ref, a_ref, wup_ref, ba_ref)
        m = m_scr[...] * decay + jnp.where(_gla_head_mask(), _dot(vv, kd, 0, 0), 0.0)
        m_scr[...] = m
        st_ref[...] = m
        o = _dot(qs, m, 1, 1)
        rr = r_ref[...]
        gate = rr * jax.nn.sigmoid(rr) * gw_ref[...]
        for h in range(GLA_HEADS):
            vs = slice(h * GLA_DV, (h + 1) * GLA_DV)
            oh = o[:, vs]
            y = oh * lax.rsqrt(jnp.mean(oh * oh, axis=-1, keepdims=True) + NORM_EPS)
            o_ref[:, vs] = (y * gate[:, vs]).astype(o_ref.dtype)

    return _pcall(
        body, ride, grid=(nc,), in_specs=_gla_specs(lambda i: i),
        out_specs=[pl.BlockSpec((CHUNK, GROUP_WIDTH), lambda i: (i, 0)),
                   pl.BlockSpec((None,) + GLA_STATE, lambda i: (i, 0, 0))],
        out_shape=[jax.ShapeDtypeStruct((t, GROUP_WIDTH), ACT_DTYPE), jax.ShapeDtypeStruct((nc,) + GLA_STATE, F32)],
        scratch_shapes=[pltpu.VMEM(GLA_STATE, F32)],
        semantics=("arbitrary",), name="gla_fwd")(pmm, pmm, pmm, pel, pel, w_up, b_a, gnorm_w)


def _gla_bwd(pmm, pel, w_up, b_a, gnorm_w, states, dmix, ride=None):
    t = pmm.shape[0]
    nc = t // CHUNK
    scale = GLA_DK ** -0.5

    def body(q_ref, k_ref, v_ref, r_ref, a_ref, wup_ref, ba_ref, gw_ref, st_ref, prev_ref, do_ref,
             dq_ref, dk_ref, dv_ref, dr_ref, da_ref, dwup_ref, dba_ref, dgw_ref, dm_scr):
        step = pl.program_id(0)

        @pl.when(step == 0)
        def _():
            dm_scr[...] = jnp.zeros_like(dm_scr)
            dwup_ref[...] = jnp.zeros_like(dwup_ref)
            dba_ref[...] = jnp.zeros_like(dba_ref)
            dgw_ref[...] = jnp.zeros_like(dgw_ref)

        z, e, decay, kd, qs, vv = _gla_chunk(q_ref, k_ref, v_ref, a_ref, wup_ref, ba_ref)
        m = st_ref[...]
        m_prev = prev_ref[...] * (step < nc - 1).astype(F32)
        rr, dout, gw = r_ref[...], do_ref[...], gw_ref[...]
        sig = jax.nn.sigmoid(rr)
        silu = rr * sig
        dsilu = sig * (1.0 + rr * (1.0 - sig))
        o = _dot(qs, m, 1, 1)
        d_o, dgw = [], []
        for h in range(GLA_HEADS):
            vs = slice(h * GLA_DV, (h + 1) * GLA_DV)
            oh, dg = o[:, vs], dout[:, vs]
            rstd = lax.rsqrt(jnp.mean(oh * oh, axis=-1, keepdims=True) + NORM_EPS)
            y = oh * rstd
            dgw.append(jnp.sum(dg * y * silu[:, vs], axis=0, keepdims=True))
            dr_ref[:, vs] = (dg * y * gw[:, vs] * dsilu[:, vs]).astype(dr_ref.dtype)
            dy = dg * gw[:, vs] * silu[:, vs]
            d_o.append(rstd * (dy - y * jnp.mean(dy * y, axis=-1, keepdims=True)))
        d_o = jnp.concatenate(d_o, axis=1)
        dgw_ref[...] += jnp.concatenate(dgw, axis=1)
        dq_ref[...] = (_dot(d_o, m) * scale).astype(dq_ref.dtype)
        dm = dm_scr[...] + jnp.where(_gla_head_mask(), _dot(d_o, qs, 0, 0), 0.0)
        dv_ref[...] = _dot(kd, dm, 1, 1).astype(dv_ref.dtype)
        dkd = _dot(vv, dm)
        dk_ref[...] = (dkd * e).astype(dk_ref.dtype)
        dm_scr[...] = dm * decay
        tri_strict = (_iota((CHUNK, CHUNK), 1) < _iota((CHUNK, CHUNK), 0)).astype(F32)
        dla = jnp.sum(dm * m_prev, axis=0, keepdims=True) * decay + _dot_exact(tri_strict, dkd * kd)
        dz = dla * jax.nn.sigmoid(-z) * (1.0 / GLA_GATE_TAU)
        da_ref[...] = _dot(dz, wup_ref[...], 1, 1).astype(da_ref.dtype)
        dwup_ref[...] += _dot(a_ref[...], dz, 0, 0)
        dba_ref[...] += jnp.sum(dz, axis=0, keepdims=True)

    chunk_of = lambda i: nc - 1 - i
    in_specs = _gla_specs(chunk_of) + [
        pl.BlockSpec((None,) + GLA_STATE, lambda i: (chunk_of(i), 0, 0)),
        pl.BlockSpec((None,) + GLA_STATE, lambda i: (jnp.maximum(chunk_of(i) - 1, 0), 0, 0)),
        pl.BlockSpec((CHUNK, GROUP_WIDTH), lambda i: (chunk_of(i), 0))]
    rows = lambda width: pl.BlockSpec((CHUNK, width), lambda i: (chunk_of(i), 0))
    const = lambda r, c: pl.BlockSpec((r, c), lambda i: (0, 0))
    return _pcall(
        body, ride, grid=(nc,), in_specs=in_specs,
        out_specs=[rows(GLA_KW), rows(GLA_KW), rows(GROUP_WIDTH), rows(GROUP_WIDTH), rows(LANES),
                   const(LANES, GLA_KW), const(1, GLA_KW), const(1, GROUP_WIDTH)],
        out_shape=[jax.ShapeDtypeStruct((t, GLA_KW), ACT_DTYPE), jax.ShapeDtypeStruct((t, GLA_KW), ACT_DTYPE),
                   jax.ShapeDtypeStruct((t, GROUP_WIDTH), ACT_DTYPE), jax.ShapeDtypeStruct((t, GROUP_WIDTH), ACT_DTYPE),
                   jax.ShapeDtypeStruct((t, LANES), ACT_DTYPE), jax.ShapeDtypeStruct((LANES, GLA_KW), F32),
                   jax.ShapeDtypeStruct((1, GLA_KW), F32), jax.ShapeDtypeStruct((1, GROUP_WIDTH), F32)],
        scratch_shapes=[pltpu.VMEM(GLA_STATE, F32)],
        semantics=("arbitrary",), name="gla_bwd")(
            pmm, pmm, pmm, pel, pel, w_up, b_a, gnorm_w, states, states, dmix)


CUM_BLOCK = 256


def _fox_gate_fwd(pel, b_f):
    t = pel.shape[0]
    nb = t // CUM_BLOCK

    def body(f_ref, b_ref, cum_ref, cum_t_ref):
        tri = (_iota((CUM_BLOCK, CUM_BLOCK), 1) <= _iota((CUM_BLOCK, CUM_BLOCK), 0)).astype(F32)
        carry = jnp.zeros((1, LANES), F32)
        for blk in range(nb):
            rows = slice(blk * CUM_BLOCK, (blk + 1) * CUM_BLOCK)
            cum = _dot_exact(tri, _log_sigmoid(f_ref[rows, :] + b_ref[...])) + carry
            cum_ref[rows, :] = cum
            cum_t_ref[blk] = cum.T[:ATT_HEADS, :]
            carry = cum[CUM_BLOCK - 1:CUM_BLOCK, :]

    return pl.pallas_call(
        body, grid=(1,),
        in_specs=[pl.BlockSpec((t, LANES), lambda i: (0, 5)), pl.BlockSpec((1, LANES), lambda i: (0, 0))],
        out_specs=[pl.BlockSpec((t, LANES), lambda i: (0, 0)),
                   pl.BlockSpec((nb, ATT_HEADS, CUM_BLOCK), lambda i: (0, 0, 0))],
        out_shape=[jax.ShapeDtypeStruct((t, LANES), F32), jax.ShapeDtypeStruct((nb, ATT_HEADS, CUM_BLOCK), F32)],
        compiler_params=_params("arbitrary"), name="fox_gate_fwd")(pel, b_f)


def _fox_gate_bwd(pel, b_f, dcum_t, dcum_q):
    t = pel.shape[0]
    nb = t // CUM_BLOCK

    def body(f_ref, b_ref, dct_ref, dcq_ref, df_ref, db_ref):
        tri_up = (_iota((CUM_BLOCK, CUM_BLOCK), 1) >= _iota((CUM_BLOCK, CUM_BLOCK), 0)).astype(F32)
        carry = jnp.zeros((1, LANES), F32)
        db = jnp.zeros((1, LANES), F32)
        for blk in reversed(range(nb)):
            rows = slice(blk * CUM_BLOCK, (blk + 1) * CUM_BLOCK)
            dls = _dot_exact(tri_up, dct_ref[blk].T + dcq_ref[rows, :]) + carry
            carry = dls[0:1, :]
            df = dls * jax.nn.sigmoid(-(f_ref[rows, :] + b_ref[...]))
            df_ref[rows, :] = df.astype(df_ref.dtype)
            db = db + jnp.sum(df, axis=0, keepdims=True)
        db_ref[...] = db

    return pl.pallas_call(
        body, grid=(1,),
        in_specs=[pl.BlockSpec((t, LANES), lambda i: (0, 5)), pl.BlockSpec((1, LANES), lambda i: (0, 0)),
                  pl.BlockSpec((nb, LANES, CUM_BLOCK), lambda i: (0, 0, 0)), pl.BlockSpec((t, LANES), lambda i: (0, 0))],
        out_specs=[pl.BlockSpec((t, LANES), lambda i: (0, 0)), pl.BlockSpec((1, LANES), lambda i: (0, 0))],
        out_shape=[jax.ShapeDtypeStruct((t, LANES), ACT_DTYPE), jax.ShapeDtypeStruct((1, LANES), F32)],
        compiler_params=_params("arbitrary"), name="fox_gate_bwd")(pel, b_f, dcum_t, dcum_q)


FOX_Q_BLOCK = 256


assert FOX_Q_BLOCK == CUM_BLOCK


def _fox_scores(q_ref, k_ref, cum_ref, cum_t_ref, h, i):
    hs = slice(h * HEAD_DIM, (h + 1) * HEAD_DIM)
    nb = cum_t_ref.shape[0]
    key_gate = jnp.concatenate([cum_t_ref[kb, h:h + 1, :] for kb in range(nb)], axis=1)
    s = _dot(q_ref[:, hs], k_ref[:, hs], 1, 1) * (HEAD_DIM ** -0.5) + (cum_ref[:, h:h + 1] - key_gate)
    shape = (FOX_Q_BLOCK, nb * FOX_Q_BLOCK)
    return jnp.where(_iota(shape, 1) <= i * FOX_Q_BLOCK + _iota(shape, 0), s, NEG)


def _fox_specs(t):
    bq, nb = FOX_Q_BLOCK, t // FOX_Q_BLOCK
    return [pl.BlockSpec((bq, GROUP_WIDTH), lambda i: (i, 2)), pl.BlockSpec((t, GROUP_WIDTH), lambda i: (0, 3)),
            pl.BlockSpec((t, GROUP_WIDTH), lambda i: (0, 4)), pl.BlockSpec((bq, LANES), lambda i: (i, 0)),
            pl.BlockSpec((nb, ATT_HEADS, bq), lambda i: (0, 0, 0))]


def _fox_fwd(pmm, cum, cum_t, ride=None):
    t = pmm.shape[0]
    bq = FOX_Q_BLOCK

    def body(q_ref, k_ref, v_ref, cum_ref, cum_t_ref, o_ref, lse_ref):
        i = pl.program_id(0)
        lse_ref[...] = jnp.zeros_like(lse_ref)
        for h in range(ATT_HEADS):
            hs = slice(h * HEAD_DIM, (h + 1) * HEAD_DIM)
            s = _fox_scores(q_ref, k_ref, cum_ref, cum_t_ref, h, i)
            m = jnp.max(s, axis=-1, keepdims=True)
            p = jnp.exp(s - m)
            l = jnp.sum(p, axis=-1, keepdims=True)
            o_ref[:, hs] = (_dot(p, v_ref[:, hs]) / l).astype(o_ref.dtype)
            lse_ref[:, h:h + 1] = m + jnp.log(l)

    return _pcall(
        body, ride, grid=(t // bq,), in_specs=_fox_specs(t),
        out_specs=[pl.BlockSpec((bq, GROUP_WIDTH), lambda i: (i, 0)), pl.BlockSpec((bq, LANES), lambda i: (i, 0))],
        out_shape=[jax.ShapeDtypeStruct((t, GROUP_WIDTH), ACT_DTYPE), jax.ShapeDtypeStruct((t, LANES), F32)],
        semantics=("parallel",), name="fox_fwd")(pmm, pmm, pmm, cum, cum_t)


def _fox_bwd(pmm, cum, cum_t, lse, dmix, ride=None):
    t = pmm.shape[0]
    bq, nb = FOX_Q_BLOCK, t // FOX_Q_BLOCK
    scale = HEAD_DIM ** -0.5

    def body(q_ref, k_ref, v_ref, cum_ref, cum_t_ref, lse_ref, do_ref, dq_ref, dk_ref, dv_ref, dct_ref, dcq_ref):
        i = pl.program_id(0)

        @pl.when(i == 0)
        def _():
            dk_ref[...] = jnp.zeros_like(dk_ref)
            dv_ref[...] = jnp.zeros_like(dv_ref)
            dct_ref[...] = jnp.zeros_like(dct_ref)

        dcq_ref[...] = jnp.zeros_like(dcq_ref)
        for h in range(ATT_HEADS):
            hs = slice(h * HEAD_DIM, (h + 1) * HEAD_DIM)
            s = _fox_scores(q_ref, k_ref, cum_ref, cum_t_ref, h, i)
            p = jnp.exp(s - lse_ref[:, h:h + 1])
            do = do_ref[:, hs]
            dp = _dot(do, v_ref[:, hs], 1, 1)
            ds = p * (dp - jnp.sum(p * dp, axis=-1, keepdims=True))
            dq_ref[:, hs] = (_dot(ds, k_ref[:, hs]) * scale).astype(dq_ref.dtype)
            dk_ref[:, hs] += _dot(ds, q_ref[:, hs], 0, 0) * scale
            dv_ref[:, hs] += _dot(p, do, 0, 0)
            key_side = -jnp.sum(ds, axis=0, keepdims=True)
            for kb in range(nb):
                dct_ref[kb, h:h + 1, :] += key_side[:, kb * bq:(kb + 1) * bq]
            dcq_ref[:, h:h + 1] = jnp.sum(ds, axis=1, keepdims=True)

    whole = pl.BlockSpec((t, GROUP_WIDTH), lambda i: (0, 0))
    return _pcall(
        body, ride, grid=(t // bq,),
        in_specs=_fox_specs(t) + [pl.BlockSpec((bq, LANES), lambda i: (i, 0)),
                                  pl.BlockSpec((bq, GROUP_WIDTH), lambda i: (i, 1))],
        out_specs=[pl.BlockSpec((bq, GROUP_WIDTH), lambda i: (i, 0)), whole, whole,
                   pl.BlockSpec((nb, LANES, bq), lambda i: (0, 0, 0)), pl.BlockSpec((bq, LANES), lambda i: (i, 0))],
        out_shape=[jax.ShapeDtypeStruct((t, GROUP_WIDTH), ACT_DTYPE), jax.ShapeDtypeStruct((t, GROUP_WIDTH), F32),
                   jax.ShapeDtypeStruct((t, GROUP_WIDTH), F32), jax.ShapeDtypeStruct((nb, LANES, bq), F32),
                   jax.ShapeDtypeStruct((t, LANES), F32)],
        semantics=("arbitrary",), name="fox_bwd")(pmm, pmm, pmm, cum, cum_t, lse, dmix)


CA_Q_BLOCK = 4 * CHUNK
CA_WINDOW = CA_Q_BLOCK + CA_LEFT
CA_BASE = 1024


def _ca_bias_base(rel_bias):
    n = rel_bias.shape[0]
    flat = CA_Q_BLOCK + CA_LEFT - REL_CLIP
    tail = CA_BASE - flat - (2 * REL_CLIP + 1)
    return jnp.concatenate([jnp.broadcast_to(rel_bias[:, 2 * REL_CLIP:], (n, flat)), rel_bias[:, ::-1],
                            jnp.broadcast_to(rel_bias[:, :1], (n, tail))], axis=1)


def _ca_bias_base_grad(dbase):
    flat = CA_Q_BLOCK + CA_LEFT - REL_CLIP
    mid = dbase[:, flat:flat + 2 * REL_CLIP + 1][:, ::-1]
    lo = jnp.sum(dbase[:, flat + 2 * REL_CLIP + 1:], axis=1, keepdims=True)
    hi = jnp.sum(dbase[:, :flat], axis=1, keepdims=True)
    pad = jnp.zeros((dbase.shape[0], 2 * REL_CLIP - 1), F32)
    return mid + jnp.concatenate([lo, pad, hi], axis=1)


def _ca_mask(i):
    r, j = _iota((CA_Q_BLOCK, CA_WINDOW), 0), _iota((CA_Q_BLOCK, CA_WINDOW), 1)
    rc, jc = r // CHUNK, j // CHUNK
    return (jc >= rc) & (jc <= rc + CA_LEFT // CHUNK) & (i * CA_Q_BLOCK + j >= CA_LEFT)


def _ca_fill_bias(i, base_ref, bias_scr):
    @pl.when(i == 0)
    def _():
        for h in range(ATT_HEADS):
            rows = jnp.broadcast_to(base_ref[h:h + 1, :], (CA_Q_BLOCK, CA_BASE))
            bias_scr[h] = pltpu.roll(rows, CA_BASE - CA_Q_BLOCK, 1, stride=1, stride_axis=0)[:, :CA_WINDOW]


def _ca_scores(q_ref, kp_ref, bias_scr, win, h, mask):
    hs = slice(h * HEAD_DIM, (h + 1) * HEAD_DIM)
    s = _dot(q_ref[:, hs], kp_ref[win, hs], 1, 1) * (HEAD_DIM ** -0.5)
    return jnp.where(mask, s + bias_scr[h], NEG)


CA_BIAS_SCRATCH = pltpu.VMEM((ATT_HEADS, CA_Q_BLOCK, CA_WINDOW), F32)


def _ca_fwd(pmm, kp, vp, base, ride=None):
    t = pmm.shape[0]

    def body(q_ref, kp_ref, vp_ref, base_ref, o_ref, lse_ref, bias_scr):
        i = pl.program_id(0)
        _ca_fill_bias(i, base_ref, bias_scr)
        win = pl.ds(pl.multiple_of(i * CA_Q_BLOCK, CA_Q_BLOCK), CA_WINDOW)
        mask = _ca_mask(i)
        lse_ref[...] = jnp.zeros_like(lse_ref)
        for h in range(ATT_HEADS):
            hs = slice(h * HEAD_DIM, (h + 1) * HEAD_DIM)
            s = _ca_scores(q_ref, kp_ref, bias_scr, win, h, mask)
            m = jnp.max(s, axis=-1, keepdims=True)
            p = jnp.exp(s - m)
            l = jnp.sum(p, axis=-1, keepdims=True)
            o_ref[:, hs] = (_dot(p, vp_ref[win, hs]) / l).astype(o_ref.dtype)
            lse_ref[:, h:h + 1] = m + jnp.log(l)

    padded = pl.BlockSpec((t + CA_LEFT, GROUP_WIDTH), lambda i: (0, 0))
    return _pcall(
        body, ride, grid=(t // CA_Q_BLOCK,),
        in_specs=[pl.BlockSpec((CA_Q_BLOCK, GROUP_WIDTH), lambda i: (i, 0)), padded, padded,
                  pl.BlockSpec((ATT_HEADS, CA_BASE), lambda i: (0, 0))],
        out_specs=[pl.BlockSpec((CA_Q_BLOCK, GROUP_WIDTH), lambda i: (i, 0)),
                   pl.BlockSpec((CA_Q_BLOCK, LANES), lambda i: (i, 0))],
        out_shape=[jax.ShapeDtypeStruct((t, GROUP_WIDTH), ACT_DTYPE), jax.ShapeDtypeStruct((t, LANES), F32)],
        scratch_shapes=[CA_BIAS_SCRATCH], semantics=("arbitrary",), name="ca_fwd")(pmm, kp, vp, base)


def _ca_bwd(pmm, kp, vp, base, lse, dmix, ride=None):
    t = pmm.shape[0]
    scale = HEAD_DIM ** -0.5

    def body(q_ref, kp_ref, vp_ref, base_ref, lse_ref, do_ref, dq_ref, dkp_ref, dvp_ref, dbase_ref, bias_scr):
        i = pl.program_id(0)
        _ca_fill_bias(i, base_ref, bias_scr)

        @pl.when(i == 0)
        def _():
            dkp_ref[...] = jnp.zeros_like(dkp_ref)
            dvp_ref[...] = jnp.zeros_like(dvp_ref)
            dbase_ref[...] = jnp.zeros_like(dbase_ref)

        win = pl.ds(pl.multiple_of(i * CA_Q_BLOCK, CA_Q_BLOCK), CA_WINDOW)
        mask = _ca_mask(i)
        flip = (_iota((CA_Q_BLOCK, CA_Q_BLOCK), 0) + _iota((CA_Q_BLOCK, CA_Q_BLOCK), 1) == CA_Q_BLOCK - 1).astype(F32)
        for h in range(ATT_HEADS):
            hs = slice(h * HEAD_DIM, (h + 1) * HEAD_DIM)
            s = _ca_scores(q_ref, kp_ref, bias_scr, win, h, mask)
            p = jnp.exp(s - lse_ref[:, h:h + 1])
            do = do_ref[:, hs]
            dp = _dot(do, vp_ref[win, hs], 1, 1)
            ds = p * (dp - jnp.sum(p * dp, axis=-1, keepdims=True))
            dq_ref[:, hs] = (_dot(ds, kp_ref[win, hs]) * scale).astype(dq_ref.dtype)
            dkp_ref[win, hs] += _dot(ds, q_ref[:, hs], 0, 0) * scale
            dvp_ref[win, hs] += _dot(p, do, 0, 0)
            rev = jnp.concatenate([_dot(flip, ds), jnp.zeros((CA_Q_BLOCK, CA_BASE - CA_WINDOW), F32)], axis=1)
            lined = pltpu.roll(rev, 1, 1, stride=1, stride_axis=0)
            dbase_ref[h:h + 1, :] += jnp.sum(lined, axis=0, keepdims=True)

    padded = pl.BlockSpec((t + CA_LEFT, GROUP_WIDTH), lambda i: (0, 0))
    return _pcall(
        body, ride, grid=(t // CA_Q_BLOCK,),
        in_specs=[pl.BlockSpec((CA_Q_BLOCK, GROUP_WIDTH), lambda i: (i, 0)), padded, padded,
                  pl.BlockSpec((ATT_HEADS, CA_BASE), lambda i: (0, 0)),
                  pl.BlockSpec((CA_Q_BLOCK, LANES), lambda i: (i, 0)),
                  pl.BlockSpec((CA_Q_BLOCK, GROUP_WIDTH), lambda i: (i, 0))],
        out_specs=[pl.BlockSpec((CA_Q_BLOCK, GROUP_WIDTH), lambda i: (i, 0)), padded, padded,
                   pl.BlockSpec((ATT_HEADS, CA_BASE), lambda i: (0, 0))],
        out_shape=[jax.ShapeDtypeStruct((t, GROUP_WIDTH), ACT_DTYPE),
                   jax.ShapeDtypeStruct((t + CA_LEFT, GROUP_WIDTH), F32),
                   jax.ShapeDtypeStruct((t + CA_LEFT, GROUP_WIDTH), F32),
                   jax.ShapeDtypeStruct((ATT_HEADS, CA_BASE), F32)],
        scratch_shapes=[CA_BIAS_SCRATCH], semantics=("arbitrary",), name="ca_bwd")(pmm, kp, vp, base, lse, dmix)


GELU_C = 0.7978845608028654
GELU_A = 0.044715


def _shift_down(v, k, fill):
    return jnp.where(_iota(v.shape, 0) >= k, pltpu.roll(v, k, 0), fill)


def _shift_up(v, k, fill):
    t = v.shape[0]
    return jnp.where(_iota(v.shape, 0) < t - k, pltpu.roll(v, t - k, 0), fill)


def _linear_scan(a, b, shift):
    k = 1
    while k < a.shape[0]:
        b = a * shift(b, k, 0.0) + b
        a = a * shift(a, k, 1.0)
        k *= 2
    return b


def _neg_expm1(y):
    series = -y * (1.0 + y * (0.5 + y * (1.0 / 6.0 + y * (1.0 / 24.0 + y * (1.0 / 120.0)))))
    return jnp.where(y > -0.1, series, 1.0 - jnp.exp(y))


def _lru_forward(x, g_in, cw, cb, wa, ba, wx, bx, lam):
    xs = [_shift_down(x, CONV_WIDTH - 1 - j, 0.0) for j in range(CONV_WIDTH - 1)] + [x]
    xc = cb + sum(cw[j:j + 1, :] * xs[j] for j in range(CONV_WIDTH))
    r = jax.nn.sigmoid(_dot(xc, wa) + ba)
    i = jax.nn.sigmoid(_dot(xc, wx) + bx)
    lsl = _log_sigmoid(lam)
    la = LRU_C * r * lsl
    a = jnp.exp(la)
    s = jnp.sqrt(_neg_expm1(2.0 * la))
    h = _linear_scan(a, s * (i * xc), _shift_down)
    u = GELU_C * (g_in + GELU_A * g_in * g_in * g_in)
    th = jnp.tanh(u)
    gelu = 0.5 * g_in * (1.0 + th)
    return xs, xc, r, i, lsl, a, s, h, th, gelu


def _lru_specs(t):
    col = lambda off: pl.BlockSpec((t, LANES), lambda j: (0, j + off))
    vec = pl.BlockSpec((1, LANES), lambda j: (0, j))
    mat = pl.BlockSpec((None, LANES, LANES), lambda j: (j, 0, 0))
    return [col(0), col(GROUP_WIDTH // LANES), pl.BlockSpec((CONV_WIDTH, LANES), lambda j: (0, j)),
            vec, mat, vec, mat, vec, vec]


def _lru_fwd(pel, conv_w, conv_b, wa, ba, wx, bx, lam, ride=None):
    t = pel.shape[0]

    def body(g_ref, x_ref, cw_ref, cb_ref, wa_ref, ba_ref, wx_ref, bx_ref, lam_ref, o_ref):
        res = _lru_forward(x_ref[...], g_ref[...], cw_ref[...], cb_ref[...], wa_ref[...], ba_ref[...],
                           wx_ref[...], bx_ref[...], lam_ref[...])
        o_ref[...] = (res[7] * res[9]).astype(o_ref.dtype)

    return _pcall(
        body, ride, grid=(GROUP_WIDTH // LANES,), in_specs=_lru_specs(t),
        out_specs=pl.BlockSpec((t, LANES), lambda j: (0, j)),
        out_shape=jax.ShapeDtypeStruct((t, GROUP_WIDTH), ACT_DTYPE),
        semantics=("parallel",), name="lru_fwd")(pel, pel, conv_w, conv_b, wa, ba, wx, bx, lam)


def _lru_bwd(pel, conv_w, conv_b, wa, ba, wx, bx, lam, dmix, ride=None):
    t = pel.shape[0]

    def body(g_ref, x_ref, cw_ref, cb_ref, wa_ref, ba_ref, wx_ref, bx_ref, lam_ref, do_ref,
             dg_ref, dx_ref, dcw_ref, dcb_ref, dwa_ref, dba_ref, dwx_ref, dbx_ref, dlam_ref):
        g_in, cw, lam = g_ref[...], cw_ref[...], lam_ref[...]
        xs, xc, r, i, lsl, a, s, h, th, gelu = _lru_forward(
            x_ref[...], g_in, cw, cb_ref[...], wa_ref[...], ba_ref[...], wx_ref[...], bx_ref[...], lam)
        dout = do_ref[...]
        dgelu = 0.5 * (1.0 + th) + 0.5 * g_in * (1.0 - th * th) * GELU_C * (1.0 + 3.0 * GELU_A * g_in * g_in)
        dg_ref[...] = (dout * h * dgelu).astype(dg_ref.dtype)
        gsum = _linear_scan(_shift_up(a, 1, 0.0), dout * gelu, _shift_up)
        da = gsum * _shift_down(h, 1, 0.0)
        di = gsum * s * xc
        dla = da * a - gsum * (i * xc) * (a * a / s)
        dlam_ref[...] = jnp.sum(dla * (LRU_C * r), axis=0, keepdims=True) * jax.nn.sigmoid(-lam)
        dpr = dla * (LRU_C * lsl) * r * (1.0 - r)
        dpi = di * i * (1.0 - i)
        dxc = gsum * s * i + _dot(dpr, wa_ref[...], 1, 1) + _dot(dpi, wx_ref[...], 1, 1)
        xct = xc.T
        dwa_ref[...] = _dot(xct, dpr)
        dwx_ref[...] = _dot(xct, dpi)
        dba_ref[...] = jnp.sum(dpr, axis=0, keepdims=True)
        dbx_ref[...] = jnp.sum(dpi, axis=0, keepdims=True)
        dcb_ref[...] = jnp.sum(dxc, axis=0, keepdims=True)
        for j in range(CONV_WIDTH):
            dcw_ref[j:j + 1, :] = jnp.sum(dxc * xs[j], axis=0, keepdims=True)
        dx = cw[CONV_WIDTH - 1:CONV_WIDTH, :] * dxc
        for j in range(CONV_WIDTH - 1):
            dx = dx + cw[j:j + 1, :] * _shift_up(dxc, CONV_WIDTH - 1 - j, 0.0)
        dx_ref[...] = dx.astype(dx_ref.dtype)

    col = pl.BlockSpec((t, LANES), lambda j: (0, j))
    vec = pl.BlockSpec((1, LANES), lambda j: (0, j))
    mat = pl.BlockSpec((None, LANES, LANES), lambda j: (j, 0, 0))
    nb = GROUP_WIDTH // LANES
    vshape = jax.ShapeDtypeStruct((1, GROUP_WIDTH), F32)
    mshape = jax.ShapeDtypeStruct((nb, LANES, LANES), F32)
    return _pcall(
        body, ride, grid=(nb,),
        in_specs=_lru_specs(t) + [pl.BlockSpec((t, LANES), lambda j: (0, j + nb))],
        out_specs=[col, col, pl.BlockSpec((CONV_WIDTH, LANES), lambda j: (0, j)), vec, mat, vec, mat, vec, vec],
        out_shape=[jax.ShapeDtypeStruct((t, GROUP_WIDTH), ACT_DTYPE), jax.ShapeDtypeStruct((t, GROUP_WIDTH), ACT_DTYPE),
                   jax.ShapeDtypeStruct((CONV_WIDTH, GROUP_WIDTH), F32), vshape, mshape, vshape, mshape, vshape, vshape],
        semantics=("parallel",), name="lru_bwd")(
            pel, pel, conv_w, conv_b, wa, ba, wx, bx, lam, dmix)


def _block_diag_pairs(w):
    z = jnp.zeros((LRU_BLOCK_DIM, LRU_BLOCK_DIM), w.dtype)
    return jnp.stack([jnp.block([[w[2 * j], z], [z, w[2 * j + 1]]]) for j in range(w.shape[0] // 2)])


def _block_diag_pairs_grad(dw):
    b = LRU_BLOCK_DIM
    return jnp.stack([dw[n // 2, (n % 2) * b:(n % 2 + 1) * b, (n % 2) * b:(n % 2 + 1) * b] for n in range(2 * dw.shape[0])])


def _row_tile(r):
    return ROW_TILE if r % ROW_TILE == 0 else r


def _pair_sum(g, got, place, name):
    _, r, c = g.shape
    tile = r

    def body(place_ref, a_ref, b_ref, o_ref):
        o_ref[...] = (a_ref[...].astype(F32) + b_ref[...].astype(F32)).astype(o_ref.dtype)

    blk = pl.BlockSpec((1, tile, c), lambda k, i, place_ref: (k, i, 0))
    return pl.pallas_call(
        body,
        grid_spec=pltpu.PrefetchScalarGridSpec(
            num_scalar_prefetch=1, grid=(N_CHIPS, r // tile),
            in_specs=[pl.BlockSpec((1, tile, c), lambda k, i, place_ref: (2 * k + place_ref[0], i, 0)), blk],
            out_specs=blk),
        out_shape=jax.ShapeDtypeStruct(got.shape, got.dtype),
        compiler_params=_params("parallel", "parallel"), name=name)(place, g, got)


def _adamw_update(g, w_ref, m_ref, v_ref, g_ref, d_ref, nm_ref, nv_ref):
    nm = ADAM_B1 * m_ref[...] + (1.0 - ADAM_B1) * g
    nv = ADAM_B2 * v_ref[...] + (1.0 - ADAM_B2) * jnp.square(g)
    m_hat = nm / (1.0 - ADAM_B1 ** ADAM_STEP)
    v_hat = nv / (1.0 - ADAM_B2 ** ADAM_STEP)
    g_ref[...] = g
    d_ref[...] = -ADAM_LR * (m_hat / (jnp.sqrt(v_hat) + ADAM_EPS) + ADAM_WD * w_ref[...])
    nm_ref[...] = nm
    nv_ref[...] = nv


def _adamw_sharded(parts, w, m, v, place, name):
    n_layers, r, c = w.shape
    tile = _row_tile(r)
    nb = r // tile

    def body(place_ref, *refs):
        layer = pl.program_id(0)
        g = None
        for l in range(n_layers):
            s_ref, r_ref = refs[2 * l], refs[2 * l + 1]
            g_l = s_ref[0].astype(F32) + r_ref[0].astype(F32) + r_ref[1].astype(F32) + r_ref[2].astype(F32)
            g = g_l if g is None else jnp.where(layer == l, g_l, g)
        _adamw_update(g, *refs[2 * n_layers:])

    def part_specs(l):
        rows = lambda q, i: jnp.where(q < l, 0, jnp.where(q > l, nb - 1, i))
        return [pl.BlockSpec((1, tile, c), lambda q, i, place_ref: (place_ref[1], rows(q, i), 0)),
                pl.BlockSpec((3, tile, c), lambda q, i, place_ref: (0, rows(q, i), 0))]

    in_specs, args = [], []
    for l, (s, recv) in enumerate(parts):
        in_specs += part_specs(l)
        args += [s, recv]
    blk = pl.BlockSpec((None, tile, c), lambda q, i, place_ref: (q, i, 0))
    out = jax.ShapeDtypeStruct((n_layers, r, c), F32)
    return pl.pallas_call(
        body,
        grid_spec=pltpu.PrefetchScalarGridSpec(
            num_scalar_prefetch=1, grid=(n_layers, nb), in_specs=in_specs + [blk, blk, blk],
            out_specs=[blk, blk, blk, blk]),
        out_shape=[out, out, out, out], compiler_params=_params("arbitrary", "arbitrary"), name=name)(
            place, *args, w, m, v)


def _adamw_small(repl_parts, vec_parts, w, m, v, place):
    n_r, n = len(repl_parts), len(w)
    shapes = [a.shape for a in w]

    def body(place_ref, *refs):
        parts, rest = refs[:n], refs[n:]
        for k in range(n):
            take = (lambda p: parts[k][p]) if k < n_r else (lambda p: parts[k][p, 0])
            g = take(0)
            for p in range(1, N_DEV):
                g = g + take(p)
            _adamw_update(g, rest[k], rest[n + k], rest[2 * n + k], *rest[3 * n + 4 * k:3 * n + 4 * k + 4])

    def whole(shape):
        return pl.BlockSpec(shape, lambda i, place_ref: (0,) * len(shape))

    def mine(shard):
        return pl.BlockSpec((N_DEV, 1) + shard, lambda i, place_ref: (0, place_ref[2]) + (0,) * len(shard))

    in_specs = [whole(a.shape) for a in repl_parts] + [mine(s) for s in shapes[n_r:]] + [whole(s) for s in shapes] * 3
    outs = pl.pallas_call(
        body,
        grid_spec=pltpu.PrefetchScalarGridSpec(
            num_scalar_prefetch=1, grid=(1,), in_specs=in_specs,
            out_specs=[whole(s) for s in shapes for _ in range(4)]),
        out_shape=[jax.ShapeDtypeStruct(s, F32) for s in shapes for _ in range(4)],
        compiler_params=_params("arbitrary"), name="adamw_small")(place, *repl_parts, *vec_parts, *w, *m, *v)
    return [outs[4 * k:4 * k + 4] for k in range(n)]


SHARDED = {"norm_w": 2, "w_in_even": 2, "gla_w_a_up": 2, "w_out_even": 1, "w_in_odd": 2, "conv_w": 2, "conv_b": 1,
           "lru_b_a": 1, "lru_b_x": 1, "lru_lambda": 1, "w_out_odd": 1, "w_mlp_up": 2, "w_mlp_down": 1}
REPLICATED = ["gla_b_a", "gla_norm_w", "fox_b_f", "rel_bias", "lru_w_a", "lru_w_x"]
WEIGHTS = ["norm_w", "w_in_even", "gla_w_a_up", "gla_b_a", "gla_norm_w", "fox_b_f", "w_out_even", "w_in_odd",
           "rel_bias", "conv_w", "conv_b", "lru_w_a", "lru_b_a", "lru_w_x", "lru_b_x", "lru_lambda", "w_out_odd",
           "w_mlp_up", "w_mlp_down"]
MATRICES = ("w_in_even", "w_out_even", "w_in_odd", "w_out_odd", "w_mlp_up", "w_mlp_down")
TRANSPOSED = ("w_in_even", "w_in_odd")
VECTORS = tuple(n for n in SHARDED if n not in MATRICES)
MATRIX_BLOCKS = (("w_in_even", 0), ("w_out_even", 0), ("w_in_odd", 0), ("w_out_odd", 0),
                 ("w_mlp_up", 0), ("w_mlp_up", 1), ("w_mlp_down", 0), ("w_mlp_down", 1))


def _join_shards(blocks, axis):
    moved = jnp.moveaxis(blocks, 0, axis)
    shape = moved.shape
    return moved.reshape(shape[:axis] + (shape[axis] * shape[axis + 1],) + shape[axis + 2:])


def _split_shards(full, axis):
    shape = full.shape
    cut = full.reshape(shape[:axis] + (N_DEV, shape[axis] // N_DEV) + shape[axis + 1:])
    return jnp.moveaxis(cut, axis, 0)


EVEN_SPLITS = (0, 256, 512, 1024, 1536, 1552, 2064, 2576, 3088, 3096)


def _even_in_split(wt):
    c = [wt[EVEN_SPLITS[k]:EVEN_SPLITS[k + 1]] for k in range(9)]
    gq, gk, gv, gr, ga, fq, fk, fv, ff = c
    padrows = lambda a: jnp.pad(a, ((0, LANES - a.shape[0]), (0, 0)))
    return jnp.concatenate([gq, gk, gv, fq, fk, fv], axis=0), jnp.concatenate([gr, padrows(ga), padrows(ff)], axis=0)


def _even_in_merge(dmm, dele):
    return jnp.concatenate([dmm[:1024], dele[:512], dele[512:512 + GLA_RANK], dmm[1024:2560],
                            dele[640:640 + ATT_HEADS]], axis=0)


def _forward_backward(x, target, shard, vec_shard, w, place):
    w = dict(w)
    g, dnorm, sums, recv = {}, {}, {}, {}
    nrm = lambda l, k: w["norm_w"][l, k][None, :]
    gather = lambda *keys: _gather_plan([shard[k] for k in keys])
    blocks = lambda r, c: (N_DEV, r // N_DEV, c)

    def pair_sum(key):
        sums[key] = _pair_sum(g[key], got[key], place, f"rs_pair_sum_{key[0]}_{key[1]}")

    got = {}

    def mlp_fwd(xin, layer, ride_up, ride_down):
        up = _mm(xin, w["w_mlp_up"][layer], out_dtype=ACT_DTYPE, tm=TM_FWD, tn=D_FF // N_DEV, b_blocked=True,
                 a_norm=nrm(layer, 2), name=f"mlp_up_{layer}", ride=ride_up)
        (u, h), rode_up = up if ride_up is not None else (up, None)
        down = _mm(u, w["w_mlp_down"][layer], out_dtype=F32, tm=TM_DX // 2, tn=D_MODEL, a_sqrelu=True,
                   res_norm=(xin, nrm(layer, 3)), name=f"mlp_down_{layer}", ride=ride_down)
        (yv, xout), rode_down = down if ride_down is not None else (down, None)
        return xout, (xin, h, u, yv), rode_up, rode_down

    def mlp_bwd(dxout, saved, layer, ride):
        xin, h, u, yv = saved
        k_up, k_down = ("w_mlp_up", layer), ("w_mlp_down", layer)
        dy, dnorm[(layer, 3)] = _norm_bwd(dxout, yv, nrm(layer, 3), out_dtype=ACT_DTYPE, name=f"norm_mlp_out_bwd_{layer}")
        du = _mm(dy, w["w_mlp_down"][layer], nt=True, out_dtype=ACT_DTYPE, tm=TM_DX, tn=TN, drelu_of=u,
                 name=f"mlp_down_dx_{layer}", ride=ride)
        rode = None
        if ride is not None:
            du, rode = du
        g[k_down] = _mm(u, dy, ta=True, out_dtype=WIRE_DTYPE, tm=TM_DW, tn=TN, a_sqrelu=True,
                        name=f"mlp_down_dw_{layer}").reshape(blocks(D_FF, D_MODEL))
        g[k_up] = _mm(h, du, ta=True, out_dtype=WIRE_DTYPE, tm=TM_DW, tn=D_FF // N_DEV, out_blocked=True,
                      name=f"mlp_up_dw_{layer}")
        w_up = jnp.moveaxis(w["w_mlp_up"][layer], 0, 1).reshape(D_MODEL, D_FF)
        dh, (got[k_down], got[k_up]) = _mm(du, w_up, nt=True, out_dtype=F32, tm=TM_DX, tn=TN, name=f"mlp_up_dx_{layer}",
                                           ride=_sibling_plan([g[k_down], g[k_up]]))
        pair_sum(k_down)
        pair_sum(k_up)
        dxin, dnorm[(layer, 2)] = _norm_bwd(dh, xin, nrm(layer, 2), out_dtype=F32, add=dxout, name=f"norm_mlp_bwd_{layer}")
        return dxin, rode

    first = _run_plan(_gather_plan([shard[("w_in_even", 0)]] + [vec_shard[n] for n in VECTORS]),
                      "weights_all_gather_first")
    w["w_in_even"] = first[0].reshape(-1, D_MODEL)
    for n, b in zip(VECTORS, first[1:]):
        w[n] = _join_shards(b, SHARDED[n])
    w["w_mlp_up"], w["w_mlp_down"] = [None] * DEPTH, [None] * DEPTH

    wmm_e, wel_e = _even_in_split(w["w_in_even"])
    w_up_pad = jnp.pad(w["gla_w_a_up"][0], ((0, LANES - GLA_RANK), (0, 0)))
    b_f_pad = jnp.pad(w["fox_b_f"], ((0, 0), (0, LANES - ATT_HEADS)))
    (pmm0, h0), (w_out_even,) = _mm(x, wmm_e, nt=True, out_dtype=ACT_DTYPE, tm=TM_FWD, tn=TN, a_norm=nrm(0, 0),
                                    name="in_even_mm", ride=gather(("w_out_even", 0)))
    pel0 = _mm(h0, wel_e, nt=True, out_dtype=F32, tm=TM_FWD, tn=768, name="in_even_el")
    (out_a, states), (w_in_odd,) = _gla_fwd(pmm0, pel0, w_up_pad, w["gla_b_a"], w["gla_norm_w"],
                                            ride=gather(("w_in_odd", 0)))
    cum, cum_t = _fox_gate_fwd(pel0, b_f_pad)
    (out_b, lse_b), (w["w_mlp_up"][0], w_mlp_down0) = _fox_fwd(pmm0, cum, cum_t,
                                                               ride=gather(("w_mlp_up", 0), ("w_mlp_down", 0)))
    w["w_out_even"] = w_out_even.reshape(D_MODEL, D_MODEL)
    w["w_mlp_down"][0] = w_mlp_down0.reshape(D_FF, D_MODEL)
    mix_in0 = jnp.concatenate([out_a, out_b], axis=1)
    mix0, x1 = _mm(mix_in0, w["w_out_even"], out_dtype=F32, tm=TM_DX, tn=D_MODEL, res_norm=(x, nrm(0, 1)),
                   name="out_even")
    x2, mlp0, _, (w["w_mlp_up"][1],) = mlp_fwd(x1, 0, None, gather(("w_mlp_up", 1)))
    w["w_in_odd"] = w_in_odd.reshape(-1, D_MODEL)

    w_in_o = w["w_in_odd"]
    n_mm_o = 3 * GROUP_WIDTH
    wa_bd, wx_bd = _block_diag_pairs(w["lru_w_a"][0]), _block_diag_pairs(w["lru_w_x"][0])
    base = _ca_bias_base(w["rel_bias"][0])
    pmm1, h1 = _mm(x2, w_in_o[:n_mm_o], nt=True, out_dtype=ACT_DTYPE, tm=TM_FWD, tn=TN, a_norm=nrm(1, 0),
                   name="in_odd_mm")
    pel1 = _mm(h1, w_in_o[n_mm_o:], nt=True, out_dtype=F32, tm=TM_FWD, tn=TN, name="in_odd_el")
    kp = jnp.pad(pmm1[:, GROUP_WIDTH:2 * GROUP_WIDTH], ((CA_LEFT, 0), (0, 0)))
    vp = jnp.pad(pmm1[:, 2 * GROUP_WIDTH:], ((CA_LEFT, 0), (0, 0)))
    (out_c, lse_c), (w_mlp_down1,) = _ca_fwd(pmm1, kp, vp, base, ride=gather(("w_mlp_down", 1)))
    w["w_mlp_down"][1] = w_mlp_down1.reshape(D_FF, D_MODEL)
    lru_args = (pel1, w["conv_w"][0], w["conv_b"], wa_bd, w["lru_b_a"], wx_bd, w["lru_b_x"], w["lru_lambda"])
    out_d, (w_out_odd,) = _lru_fwd(*lru_args, ride=gather(("w_out_odd", 0)))
    w["w_out_odd"] = w_out_odd.reshape(D_MODEL, D_MODEL)
    mix_in1 = jnp.concatenate([out_c, out_d], axis=1)
    mix1, x3 = _mm(mix_in1, w["w_out_odd"], out_dtype=F32, tm=TM_DX, tn=D_MODEL, res_norm=(x2, nrm(1, 1)),
                   name="out_odd")
    x4, mlp1, _, _ = mlp_fwd(x3, 1, None, None)

    loss, dx4 = _loss_fwd_bwd(x4, target)

    k_oo, k_io, k_oe, k_ie = ("w_out_odd", 0), ("w_in_odd", 0), ("w_out_even", 0), ("w_in_even", 0)
    mlp_keys = lambda l: [("w_mlp_down", l), ("w_mlp_up", l)]
    dx3, _ = mlp_bwd(dx4, mlp1, 1, None)
    dmix1, dnorm[(1, 1)] = _norm_bwd(dx3, mix1, nrm(1, 1), out_dtype=ACT_DTYPE, name="norm_mix_bwd_1")
    g[k_oo] = _mm(mix_in1, dmix1, ta=True, out_dtype=WIRE_DTYPE, tm=TM_DW, tn=TN, name="out_odd_dw").reshape(
        blocks(D_MODEL, D_MODEL))
    dmix_in1, (got[k_oo],) = _mm(dmix1, w["w_out_odd"], nt=True, out_dtype=F32, tm=TM_DX, tn=TN, name="out_odd_dx",
                                 ride=_sibling_plan([g[k_oo]]))
    (dq_c, dkp, dvp, dbase), rode = _ca_bwd(pmm1, kp, vp, base, lse_c, dmix_in1,
                                            ride=_chip_plan([sums[k] for k in mlp_keys(1)]))
    recv.update(zip(mlp_keys(1), rode))
    pair_sum(k_oo)
    (dgate, dxin, g_conv_w, g_conv_b, dwa_bd, g_lru_b_a, dwx_bd, g_lru_b_x, g_lru_lambda), (recv[k_oo],) = _lru_bwd(
        *lru_args, dmix_in1, ride=_chip_plan([sums[k_oo]]))
    dp1 = jnp.concatenate([dq_c, dkp[CA_LEFT:].astype(ACT_DTYPE), dvp[CA_LEFT:].astype(ACT_DTYPE), dgate, dxin], axis=1)
    g[k_io] = _mm(dp1, h1, ta=True, out_dtype=WIRE_DTYPE, tm=dp1.shape[1] // 2, tn=TN, name="in_odd_dw").reshape(
        blocks(dp1.shape[1], D_MODEL))
    dh1, (got[k_io],) = _mm(dp1, w_in_o, out_dtype=F32, tm=TM_DX, tn=TN, name="in_odd_dx",
                            ride=_sibling_plan([g[k_io]]))
    pair_sum(k_io)
    dx2, dnorm[(1, 0)] = _norm_bwd(dh1, x2, nrm(1, 0), out_dtype=F32, add=dx3, name="norm_in_bwd_1")
    g["rel_bias"] = _ca_bias_base_grad(dbase)[None]
    g["conv_w"], g["conv_b"] = g_conv_w[None], g_conv_b
    g["lru_w_a"], g["lru_w_x"] = _block_diag_pairs_grad(dwa_bd)[None], _block_diag_pairs_grad(dwx_bd)[None]
    g["lru_b_a"], g["lru_b_x"], g["lru_lambda"] = g_lru_b_a, g_lru_b_x, g_lru_lambda

    dx1, (recv[k_io],) = mlp_bwd(dx2, mlp0, 0, _chip_plan([sums[k_io]]))
    dmix0, dnorm[(0, 1)] = _norm_bwd(dx1, mix0, nrm(0, 1), out_dtype=ACT_DTYPE, name="norm_mix_bwd_0")
    g[k_oe] = _mm(mix_in0, dmix0, ta=True, out_dtype=WIRE_DTYPE, tm=TM_DW, tn=TN, name="out_even_dw").reshape(
        blocks(D_MODEL, D_MODEL))
    dmix_in0, (got[k_oe],) = _mm(dmix0, w["w_out_even"], nt=True, out_dtype=F32, tm=TM_DX, tn=TN, name="out_even_dx",
                                 ride=_sibling_plan([g[k_oe]]))
    k_md0, k_mu0 = mlp_keys(0)
    pair_sum(k_oe)
    dq_a, dk_a, dv_a, dr_a, da_a, dw_up_pad, g_gla_b_a, g_gla_norm_w = _gla_bwd(
        pmm0, pel0, w_up_pad, w["gla_b_a"], w["gla_norm_w"], states, dmix_in0)
    (dq_b, dk_b, dv_b, dcum_t, dcum_q), (recv[k_md0], recv[k_mu0], recv[k_oe]) = _fox_bwd(
        pmm0, cum, cum_t, lse_b, dmix_in0, ride=_chip_plan([sums[k_md0], sums[k_mu0], sums[k_oe]]))
    df_b, db_f = _fox_gate_bwd(pel0, b_f_pad, dcum_t, dcum_q)
    g["gla_w_a_up"] = dw_up_pad[:GLA_RANK][None]
    g["gla_b_a"], g["gla_norm_w"], g["fox_b_f"] = g_gla_b_a, g_gla_norm_w, db_f[:, :ATT_HEADS]
    dp0 = jnp.concatenate([dq_a, dk_a, dv_a, dq_b, dk_b.astype(ACT_DTYPE), dv_b.astype(ACT_DTYPE), dr_a, da_a, df_b],
                          axis=1)
    w_perm = jnp.concatenate([wmm_e, wel_e], axis=0)
    n_mm_e = wmm_e.shape[0]
    dw_perm, repl_parts = _mm(dp0, h0, ta=True, out_dtype=WIRE_DTYPE, tm=dp0.shape[1] // 2, tn=TN, name="in_even_dw",
                              ride=_gather_plan([g[n] for n in REPLICATED]))
    dw_even = _even_in_merge(dw_perm[:n_mm_e], dw_perm[n_mm_e:])
    g[k_ie] = dw_even.reshape(blocks(dw_even.shape[0], D_MODEL))
    dh0, (got[k_ie],) = _mm(dp0, w_perm, out_dtype=F32, tm=TM_DX, tn=TN, name="in_even_dx",
                            ride=_sibling_plan([g[k_ie]]))
    pair_sum(k_ie)
    (dx0, dnorm[(0, 0)]), (recv[k_ie],) = _norm_bwd(dh0, x, nrm(0, 0), out_dtype=F32, add=dx1, name="norm_in_bwd_0",
                                                     ride=_chip_plan([sums[k_ie]]))

    g["norm_w"] = jnp.stack([jnp.concatenate([dnorm[(l, k)] for k in range(4)], axis=0) for l in range(DEPTH)])
    vec_parts = _run_plan(_gather_plan([_split_shards(g[n], SHARDED[n]) for n in VECTORS]), "vector_grads_all_gather")
    return loss, dx0, sums, recv, repl_parts, vec_parts


def kernel(x, norm_w, w_in_even, gla_w_a_up, gla_b_a, gla_norm_w, fox_b_f, w_out_even, w_in_odd, rel_bias, conv_w, conv_b, lru_w_a, lru_b_a, lru_w_x, lru_b_x, lru_lambda, w_out_odd, w_mlp_up, w_mlp_down, loss_target, m_norm_w, m_w_in_even, m_gla_w_a_up, m_gla_b_a, m_gla_norm_w, m_fox_b_f, m_w_out_even, m_w_in_odd, m_rel_bias, m_conv_w, m_conv_b, m_lru_w_a, m_lru_b_a, m_lru_w_x, m_lru_b_x, m_lru_lambda, m_w_out_odd, m_w_mlp_up, m_w_mlp_down, v_norm_w, v_w_in_even, v_gla_w_a_up, v_gla_b_a, v_gla_norm_w, v_fox_b_f, v_w_out_even, v_w_in_odd, v_rel_bias, v_conv_w, v_conv_b, v_lru_w_a, v_lru_b_a, v_lru_w_x, v_lru_b_x, v_lru_lambda, v_w_out_odd, v_w_mlp_up, v_w_mlp_down):
    wts = dict(zip(WEIGHTS, (norm_w, w_in_even, gla_w_a_up, gla_b_a, gla_norm_w, fox_b_f, w_out_even, w_in_odd, rel_bias,
                             conv_w, conv_b, lru_w_a, lru_b_a, lru_w_x, lru_b_x, lru_lambda, w_out_odd, w_mlp_up,
                             w_mlp_down)))
    mom = dict(zip(WEIGHTS, (m_norm_w, m_w_in_even, m_gla_w_a_up, m_gla_b_a, m_gla_norm_w, m_fox_b_f, m_w_out_even,
                             m_w_in_odd, m_rel_bias, m_conv_w, m_conv_b, m_lru_w_a, m_lru_b_a, m_lru_w_x, m_lru_b_x,
                             m_lru_lambda, m_w_out_odd, m_w_mlp_up, m_w_mlp_down)))
    var = dict(zip(WEIGHTS, (v_norm_w, v_w_in_even, v_gla_w_a_up, v_gla_b_a, v_gla_norm_w, v_fox_b_f, v_w_out_even,
                             v_w_in_odd, v_rel_bias, v_conv_w, v_conv_b, v_lru_w_a, v_lru_b_a, v_lru_w_x, v_lru_b_x,
                             v_lru_lambda, v_w_out_odd, v_w_mlp_up, v_w_mlp_down)))
    ax, ay, ac = lax.axis_index("x"), lax.axis_index("y"), lax.axis_index("c")
    place = jnp.stack([ac, 2 * ax + ay, 4 * ax + 2 * ay + ac]).astype(jnp.int32)

    shard = {(n, l): (wts[n][l].T if n in TRANSPOSED else wts[n][l]).astype(WIRE_DTYPE) for n, l in MATRIX_BLOCKS}
    loss_blk, dx, sums, recv, repl_parts, vec_parts = _forward_backward(
        x[0], loss_target[0], shard, {n: wts[n] for n in VECTORS}, {n: wts[n] for n in REPLICATED}, place)
    loss = lax.psum(loss_blk[0, 0], ("x", "y", "c"))

    view = lambda n, a: jnp.swapaxes(a, 1, 2) if n in TRANSPOSED else a
    upd = {n: [view(n, o) for o in _adamw_sharded(
        [(sums[(n, l)], recv[(n, l)]) for l in range(wts[n].shape[0])], view(n, wts[n]), view(n, mom[n]), view(n, var[n]),
        place, f"adamw_{n}")] for n in MATRICES}
    small = REPLICATED + list(VECTORS)
    upd.update(zip(small, _adamw_small(repl_parts, vec_parts, [wts[n] for n in small], [mom[n] for n in small],
                                       [var[n] for n in small], place)))
    return (loss, dx[None], *[upd[n][kind] for kind in range(4) for n in WEIGHTS])
```

```python
import functools
from typing import Callable, NamedTuple

import jax
import jax.numpy as jnp
from jax import lax
from jax.experimental import pallas as pl
from jax.experimental.pallas import tpu as pltpu

F32 = jnp.float32
MXU_DTYPE = jnp.bfloat16
ACT_DTYPE = jnp.bfloat16
WIRE_DTYPE = jnp.bfloat16

V7X_VMEM_BYTES = 64 * 1024 * 1024
VMEM_LIMIT = (V7X_VMEM_BYTES * 7) // 8
LANES = 128

D_MODEL = 1024
SEQ = 2048
DEPTH = 2
CHUNK = 64
GROUP_WIDTH = D_MODEL // 2
D_FF = 4 * D_MODEL
NORM_EPS = 1e-6
GLA_HEADS = 4
GLA_DV = GROUP_WIDTH // GLA_HEADS
GLA_DK = GLA_DV // 2
GLA_KW = GLA_HEADS * GLA_DK
GLA_RANK = 16
GLA_GATE_TAU = 16.0
HEAD_DIM = 64
ATT_HEADS = GROUP_WIDTH // HEAD_DIM
CA_LEFT = 8 * CHUNK
REL_CLIP = 128
LRU_BLOCK_DIM = 64
CONV_WIDTH = 4
LRU_C = 8.0
N_DEV = 8

ADAM_LR = 0.001
ADAM_B1 = 0.9
ADAM_B2 = 0.999
ADAM_EPS = 1e-08
ADAM_WD = 0.01
ADAM_STEP = 10

NEG = float(jnp.finfo(jnp.float32).min)
MESH = pl.DeviceIdType.MESH


def _params(*sem):
    return pltpu.CompilerParams(dimension_semantics=sem, vmem_limit_bytes=VMEM_LIMIT)


def _dot(a, b, ca=1, cb=0):
    return lax.dot_general(a.astype(MXU_DTYPE), b.astype(MXU_DTYPE), (((ca,), (cb,)), ((), ())),
                           preferred_element_type=F32)


def _dot_exact(a, b):
    return lax.dot_general(a, b, (((1,), (0,)), ((), ())), precision=lax.Precision.HIGHEST,
                           preferred_element_type=F32)


def _log_sigmoid(x):
    return jnp.minimum(x, 0.0) - jnp.log1p(jnp.exp(-jnp.abs(x)))


def _iota(shape, axis):
    return lax.broadcasted_iota(jnp.int32, shape, axis)


ANY = pl.BlockSpec(memory_space=pl.ANY)
N_CHIPS = 4


class _Plan(NamedTuple):
    ins: list
    outs: list
    sems: list
    start: Callable
    finish: Callable


def _place():
    x, y, c = lax.axis_index("x"), lax.axis_index("y"), lax.axis_index("c")
    return x, y, c, [(1 - x, y), (x, 1 - y), (1 - x, 1 - y)]


def _gather_plan(xs):
    n = len(xs)

    def parts(x_refs, out_refs, sems):
        send_sems, recv_sems, local_sems = sems
        x, y, c, chips = _place()
        me, sibling = (x, y, c), (x, y, 1 - c)

        def rows(a, px, py, pc):
            return out_refs[a].at[4 * px + 2 * py + pc]

        def copy(a, k, block, to, src=None):
            return pltpu.make_async_remote_copy(
                src_ref=rows(a, *block) if src is None else src, dst_ref=rows(a, *block),
                send_sem=send_sems.at[7 * a + k], recv_sem=recv_sems.at[7 * a + k], device_id=to, device_id_type=MESH)

        mine = [pltpu.make_async_copy(x_refs[a], rows(a, *me), local_sems.at[a]) for a in range(n)]
        first = []
        for a in range(n):
            first.append(copy(a, 0, me, sibling, src=x_refs[a]))
            first += [copy(a, 1 + j, me, (*chip, c), src=x_refs[a]) for j, chip in enumerate(chips)]
        return c, me, sibling, chips, copy, mine, first

    def start(x_refs, out_refs, sems):
        *_, mine, first = parts(x_refs, out_refs, sems)
        for cp in first + mine:
            cp.start()

    def finish(x_refs, out_refs, sems):
        c, me, sibling, chips, copy, mine, first = parts(x_refs, out_refs, sems)
        passed = []
        for j, chip in enumerate(chips):
            for a in range(n):
                copy(a, 1 + j, (*chip, c), me).wait_recv()
                passed.append(copy(a, 4 + j, (*chip, c), sibling))
                passed[-1].start()
        for a in range(n):
            copy(a, 0, sibling, me).wait_recv()
            for j, chip in enumerate(chips):
                copy(a, 4 + j, (*chip, 1 - c), me).wait_recv()
        for cp in first + passed:
            cp.wait_send()
        for cp in mine:
            cp.wait()

    return _Plan(list(xs), [jax.ShapeDtypeStruct((N_DEV,) + x.shape, x.dtype) for x in xs],
                 [pltpu.SemaphoreType.DMA((7 * n,)), pltpu.SemaphoreType.DMA((7 * n,)), pltpu.SemaphoreType.DMA((n,))],
                 start, finish)


def _exchange_plan(copies_of, ins, outs, per_array):
    n = len(ins)

    def start(in_refs, out_refs, sems):
        for cp in copies_of(in_refs, out_refs, sems):
            cp.start()

    def finish(in_refs, out_refs, sems):
        copies = copies_of(in_refs, out_refs, sems)
        for cp in copies:
            cp.wait_recv()
        for cp in copies:
            cp.wait_send()

    return _Plan(list(ins), outs, [pltpu.SemaphoreType.DMA((per_array * n,)), pltpu.SemaphoreType.DMA((per_array * n,))],
                 start, finish)


def _sibling_plan(gs):
    def copies_of(g_refs, got_refs, sems):
        x, y, c, _ = _place()
        return [pltpu.make_async_remote_copy(
            src_ref=g_refs[a].at[2 * k + (1 - c)], dst_ref=got_refs[a].at[k], send_sem=sems[0].at[N_CHIPS * a + k],
            recv_sem=sems[1].at[N_CHIPS * a + k], device_id=(x, y, 1 - c), device_id_type=MESH)
            for a in range(len(gs)) for k in range(N_CHIPS)]

    return _exchange_plan(copies_of, gs, [jax.ShapeDtypeStruct((N_CHIPS,) + g.shape[1:], g.dtype) for g in gs], N_CHIPS)


def _chip_plan(ss):
    def copies_of(s_refs, out_refs, sems):
        x, y, c, chips = _place()
        return [pltpu.make_async_remote_copy(
            src_ref=s_refs[a].at[2 * px + py], dst_ref=out_refs[a].at[j], send_sem=sems[0].at[3 * a + j],
            recv_sem=sems[1].at[3 * a + j], device_id=(px, py, c), device_id_type=MESH)
            for a in range(len(ss)) for j, (px, py) in enumerate(chips)]

    return _exchange_plan(copies_of, ss, [jax.ShapeDtypeStruct((3,) + s.shape[1:], s.dtype) for s in ss], 3)


def _run_plan(plan, name):
    n_in, n_out = len(plan.ins), len(plan.outs)

    def body(*refs):
        args = refs[:n_in], refs[n_in:n_in + n_out], refs[n_in + n_out:]
        plan.start(*args)
        plan.finish(*args)

    return pl.pallas_call(body, out_shape=plan.outs, in_specs=[ANY] * n_in, out_specs=[ANY] * n_out,
                          scratch_shapes=plan.sems, name=name)(*plan.ins)


def _pcall(body, ride, *, grid, in_specs, out_specs, out_shape, scratch_shapes=(), semantics, name):
    if ride is None:
        return pl.pallas_call(body, grid=grid, in_specs=in_specs, out_specs=out_specs, out_shape=out_shape,
                              scratch_shapes=list(scratch_shapes), compiler_params=_params(*semantics), name=name)
    single = not isinstance(out_shape, (list, tuple))
    out_specs_l, out_shape_l = ([out_specs], [out_shape]) if single else (list(out_specs), list(out_shape))
    n_in, n_out, n_scr = len(in_specs), len(out_shape_l), len(scratch_shapes)
    r_in, r_out = len(ride.ins), len(ride.outs)

    def riding(*refs):
        cuts = [n_in, r_in, n_out, r_out, n_scr]
        groups, at = [], 0
        for width in cuts:
            groups.append(refs[at:at + width])
            at += width
        ins, r_ins, outs, r_outs, scr = groups
        sems = refs[at:]
        first = functools.reduce(jnp.logical_and, [pl.program_id(d) == 0 for d in range(len(grid))])
        last = functools.reduce(jnp.logical_and, [pl.program_id(d) == grid[d] - 1 for d in range(len(grid))])

        @pl.when(first)
        def _():
            ride.start(r_ins, r_outs, sems)

        body(*ins, *outs, *scr)

        @pl.when(last)
        def _():
            ride.finish(r_ins, r_outs, sems)

    call = pl.pallas_call(
        riding, grid=grid, in_specs=list(in_specs) + [ANY] * r_in, out_specs=out_specs_l + [ANY] * r_out,
        out_shape=out_shape_l + list(ride.outs), scratch_shapes=list(scratch_shapes) + list(ride.sems),
        compiler_params=_params(*(["arbitrary"] * len(grid))), name=name)

    def run(*args):
        res = call(*args, *ride.ins)
        return (res[0] if single else list(res[:n_out])), list(res[n_out:])

    return run


def _rms(x):
    return x * lax.rsqrt(jnp.mean(x * x, axis=-1, keepdims=True) + NORM_EPS)


def _mm(a, b, *, nt=False, ta=False, out_dtype, tm, tn, a_sqrelu=False, drelu_of=None, b_blocked=False,
        out_blocked=False, a_norm=None, res_norm=None, norm_bwd=None, name, ride=None):
    k, m = a.shape if ta else a.shape[::-1]
    if b_blocked:
        assert not nt and b.shape[1] == k and b.shape[2] == tn
        n = b.shape[0] * tn
    else:
        n = b.shape[0] if nt else b.shape[1]
        assert (b.shape[1] if nt else b.shape[0]) == k
    tm, tn = min(tm, m), min(tn, n)
    assert m % tm == 0 and n % tn == 0
    assert (res_norm is None and norm_bwd is None) or tn == n
    n_in = 2 + (drelu_of is not None) + (a_norm is not None) + 2 * (res_norm is not None) + 3 * (norm_bwd is not None)

    def body(*refs):
        a_ref, b_ref = refs[0], refs[1]
        extra = list(refs[2:n_in])
        outs = list(refs[n_in:])
        o_ref = outs.pop(0)
        u_ref = extra.pop(0) if drelu_of is not None else None
        if a_norm is not None:
            wn_ref, h_ref, h_scr = extra.pop(0), outs.pop(0), outs.pop()

            @pl.when(pl.program_id(1) == 0)
            def _():
                h = (_rms(a_ref[...]) * wn_ref[...]).astype(ACT_DTYPE)
                h_scr[...] = h
                h_ref[...] = h

            av = h_scr[...]
        else:
            av = a_ref[...]
        if a_sqrelu:
            av = jnp.square(jnp.maximum(av.astype(F32), 0.0))
        acc = _dot(av, b_ref[...], 0 if ta else 1, 1 if nt else 0)
        if u_ref is not None:
            acc = acc * (2.0 * jnp.maximum(u_ref[...].astype(F32), 0.0))
        if norm_bwd is not None:
            x_ref, wb_ref, add_ref = extra
            dw_ref = outs[0]
            xv = x_ref[...]
            rstd = lax.rsqrt(jnp.mean(xv * xv, axis=-1, keepdims=True) + NORM_EPS)
            xhat = xv * rstd
            g = acc * wb_ref[...]
            o_ref[...] = rstd * (g - xhat * jnp.mean(g * xhat, axis=-1, keepdims=True)) + add_ref[...]

            @pl.when(pl.program_id(0) == 0)
            def _():
                dw_ref[...] = jnp.zeros_like(dw_ref)

            dw_ref[...] += jnp.sum(acc * xhat, axis=0, keepdims=True)
            return
        o_ref[...] = acc.astype(out_dtype)
        if res_norm is not None:
            res_ref, wr_ref = extra
            outs[0][...] = res_ref[...] + _rms(acc) * wr_ref[...]

    if b_blocked:
        b_spec = pl.BlockSpec((None, k, tn), lambda i, j: (j, 0, 0))
    elif nt:
        b_spec = pl.BlockSpec((tn, k), lambda i, j: (j, 0))
    else:
        b_spec = pl.BlockSpec((k, tn), lambda i, j: (0, j))
    a_spec = pl.BlockSpec((k, tm), lambda i, j: (0, i)) if ta else pl.BlockSpec((tm, k), lambda i, j: (i, 0))
    in_specs = [a_spec, b_spec]
    args = [a, b]
    if drelu_of is not None:
        in_specs.append(pl.BlockSpec((tm, tn), lambda i, j: (i, j)))
        args.append(drelu_of)
    if out_blocked:
        out_specs = [pl.BlockSpec((None, tm, tn), lambda i, j: (j, i, 0))]
        out_shape = [jax.ShapeDtypeStruct((n // tn, m, tn), out_dtype)]
    else:
        out_specs = [pl.BlockSpec((tm, tn), lambda i, j: (i, j))]
        out_shape = [jax.ShapeDtypeStruct((m, n), out_dtype)]
    scratch = []
    if a_norm is not None:
        assert not ta
        in_specs.append(pl.BlockSpec((1, k), lambda i, j: (0, 0)))
        args.append(a_norm)
        out_specs.append(pl.BlockSpec((tm, k), lambda i, j: (i, 0)))
        out_shape.append(jax.ShapeDtypeStruct((m, k), ACT_DTYPE))
        scratch.append(pltpu.VMEM((tm, k), ACT_DTYPE))
    if res_norm is not None:
        in_specs += [pl.BlockSpec((tm, n), lambda i, j: (i, 0)), pl.BlockSpec((1, n), lambda i, j: (0, 0))]
        args += list(res_norm)
        out_specs.append(pl.BlockSpec((tm, n), lambda i, j: (i, 0)))
        out_shape.append(jax.ShapeDtypeStruct((m, n), F32))
    if norm_bwd is not None:
        rows = pl.BlockSpec((tm, n), lambda i, j: (i, 0))
        in_specs += [rows, pl.BlockSpec((1, n), lambda i, j: (0, 0)), rows]
        args += list(norm_bwd)
        out_specs.append(pl.BlockSpec((1, n), lambda i, j: (0, 0)))
        out_shape.append(jax.ShapeDtypeStruct((1, n), F32))
    single = len(out_shape) == 1
    return _pcall(body, ride, grid=(m // tm, n // tn), in_specs=in_specs,
                  out_specs=out_specs[0] if single else out_specs, out_shape=out_shape[0] if single else out_shape,
                  scratch_shapes=scratch, semantics=("arbitrary", "arbitrary"), name=name)(*args)


ROW_TILE = 512
TM_FWD, TM_DX, TM_DW, TN = 2048, 1024, 1024, 512


def _norm_bwd(dy, x, w, *, out_dtype, add=None, name, ride=None):
    t, d = x.shape

    def body(*refs):
        dy_ref, x_ref, w_ref = refs[0], refs[1], refs[2]
        dx_ref, dw_ref = refs[-2], refs[-1]
        xv = x_ref[...]
        rstd = lax.rsqrt(jnp.mean(xv * xv, axis=-1, keepdims=True) + NORM_EPS)
        xhat = xv * rstd
        dyv = dy_ref[...].astype(F32)
        g = dyv * w_ref[...]
        dx = rstd * (g - xhat * jnp.mean(g * xhat, axis=-1, keepdims=True))
        if add is not None:
            dx = dx + refs[3][...]
        dx_ref[...] = dx.astype(out_dtype)

        @pl.when(pl.program_id(0) == 0)
        def _():
            dw_ref[...] = jnp.zeros_like(dw_ref)

        dw_ref[...] += jnp.sum(dyv * xhat, axis=0, keepdims=True)

    row = pl.BlockSpec((ROW_TILE, d), lambda i: (i, 0))
    vec = pl.BlockSpec((1, d), lambda i: (0, 0))
    in_specs = [row, row, vec] + ([row] if add is not None else [])
    args = [dy, x, w] + ([add] if add is not None else [])
    return _pcall(body, ride, grid=(t // ROW_TILE,), in_specs=in_specs, out_specs=[row, vec],
                  out_shape=[jax.ShapeDtypeStruct((t, d), out_dtype), jax.ShapeDtypeStruct((1, d), F32)],
                  semantics=("arbitrary",), name=name)(*args)


def _loss_fwd_bwd(y, target):
    t, d = y.shape

    def body(y_ref, t_ref, l_ref, dy_ref):
        diff = y_ref[...] - t_ref[...]
        dy_ref[...] = diff * (1.0 / d)

        @pl.when(pl.program_id(0) == 0)
        def _():
            l_ref[...] = jnp.zeros_like(l_ref)

        l_ref[...] += 0.5 * jnp.sum(jnp.mean(diff * diff, axis=-1, keepdims=True), axis=0, keepdims=True)

    row = pl.BlockSpec((ROW_TILE, d), lambda i: (i, 0))
    return pl.pallas_call(body, grid=(t // ROW_TILE,), in_specs=[row, row],
                          out_specs=[pl.BlockSpec((8, LANES), lambda i: (0, 0)), row],
                          out_shape=[jax.ShapeDtypeStruct((8, LANES), F32), jax.ShapeDtypeStruct((t, d), F32)],
                          compiler_params=_params("arbitrary"), name="loss")(y, target)


GLA_STATE = (GLA_HEADS * GLA_DV, GLA_KW)


def _gla_specs(chunk_of):
    rows = lambda width, col: pl.BlockSpec((CHUNK, width), lambda i: (chunk_of(i), col))
    const = lambda r, c: pl.BlockSpec((r, c), lambda i: (0, 0))
    return [rows(GLA_KW, 0),
            rows(GLA_KW, 1),
            rows(GROUP_WIDTH, 1),
            rows(GROUP_WIDTH, 0),
            rows(LANES, 4),
            const(LANES, GLA_KW),
            const(1, GLA_KW),
            const(1, GROUP_WIDTH)]


def _gla_chunk(q_ref, k_ref, v_ref, a_ref, wup_ref, ba_ref):
    z = _dot(a_ref[...], wup_ref[...]) + ba_ref[...]
    tri = (_iota((CHUNK, CHUNK), 1) <= _iota((CHUNK, CHUNK), 0)).astype(F32)
    cum = _dot_exact(tri, _log_sigmoid(z) * (1.0 / GLA_GATE_TAU))
    tot = cum[CHUNK - 1:CHUNK, :]
    e = jnp.exp(tot - cum)
    return (z, e, jnp.exp(tot), k_ref[...].astype(F32) * e, q_ref[...].astype(F32) * (GLA_DK ** -0.5),
            v_ref[...].astype(F32))


def _gla_head_mask():
    return _iota(GLA_STATE, 0) // GLA_DV == _iota(GLA_STATE, 1) // GLA_DK


def _gla_fwd(pmm, pel, w_up, b_a, gnorm_w, ride=None):
    t = pmm.shape[0]
    nc = t // CHUNK

    def body(q_ref, k_ref, v_ref, r_ref, a_ref, wup_ref, ba_ref, gw_ref, o_ref, st_ref, m_scr):
        @pl.when(pl.program_id(0) == 0)
        def _():
            m_scr[...] = jnp.zeros_like(m_scr)

        _, _, decay, kd, qs, vv = _gla_chunk(q_ref, k_ref, v_ref, a_ref, wup_ref, ba_ref)
        m = m_scr[...] * decay + jnp.where(_gla_head_mask(), _dot(vv, kd, 0, 0), 0.0)
        m_scr[...] = m
        st_ref[...] = m
        o = _dot(qs, m, 1, 1)
        rr = r_ref[...]
        gate = rr * jax.nn.sigmoid(rr) * gw_ref[...]
        for h in range(GLA_HEADS):
            vs = slice(h * GLA_DV, (h + 1) * GLA_DV)
            oh = o[:, vs]
            y = oh * lax.rsqrt(jnp.mean(oh * oh, axis=-1, keepdims=True) + NORM_EPS)
            o_ref[:, vs] = (y * gate[:, vs]).astype(o_ref.dtype)

    return _pcall(
        body, ride, grid=(nc,), in_specs=_gla_specs(lambda i: i),
        out_specs=[pl.BlockSpec((CHUNK, GROUP_WIDTH), lambda i: (i, 0)),
                   pl.BlockSpec((None,) + GLA_STATE, lambda i: (i, 0, 0))],
        out_shape=[jax.ShapeDtypeStruct((t, GROUP_WIDTH), ACT_DTYPE), jax.ShapeDtypeStruct((nc,) + GLA_STATE, F32)],
        scratch_shapes=[pltpu.VMEM(GLA_STATE, F32)],
        semantics=("arbitrary",), name="gla_fwd")(pmm, pmm, pmm, pel, pel, w_up, b_a, gnorm_w)


def _gla_bwd(pmm, pel, w_up, b_a, gnorm_w, states, dmix, ride=None):
    t = pmm.shape[0]
    nc = t // CHUNK
    scale = GLA_DK ** -0.5

    def body(q_ref, k_ref, v_ref, r_ref, a_ref, wup_ref, ba_ref, gw_ref, st_ref, prev_ref, do_ref,
             dq_ref, dk_ref, dv_ref, dr_ref, da_ref, dwup_ref, dba_ref, dgw_ref, dm_scr):
        step = pl.program_id(0)

        @pl.when(step == 0)
        def _():
            dm_scr[...] = jnp.zeros_like(dm_scr)
            dwup_ref[...] = jnp.zeros_like(dwup_ref)
            dba_ref[...] = jnp.zeros_like(dba_ref)
            dgw_ref[...] = jnp.zeros_like(dgw_ref)

        z, e, decay, kd, qs, vv = _gla_chunk(q_ref, k_ref, v_ref, a_ref, wup_ref, ba_ref)
        m = st_ref[...]
        m_prev = prev_ref[...] * (step < nc - 1).astype(F32)
        rr, dout, gw = r_ref[...], do_ref[...], gw_ref[...]
        sig = jax.nn.sigmoid(rr)
        silu = rr * sig
        dsilu = sig * (1.0 + rr * (1.0 - sig))
        o = _dot(qs, m, 1, 1)
        d_o, dgw = [], []
        for h in range(GLA_HEADS):
            vs = slice(h * GLA_DV, (h + 1) * GLA_DV)
            oh, dg = o[:, vs], dout[:, vs]
            rstd = lax.rsqrt(jnp.mean(oh * oh, axis=-1, keepdims=True) + NORM_EPS)
            y = oh * rstd
            dgw.append(jnp.sum(dg * y * silu[:, vs], axis=0, keepdims=True))
            dr_ref[:, vs] = (dg * y * gw[:, vs] * dsilu[:, vs]).astype(dr_ref.dtype)
            dy = dg * gw[:, vs] * silu[:, vs]
            d_o.append(rstd * (dy - y * jnp.mean(dy * y, axis=-1, keepdims=True)))
        d_o = jnp.concatenate(d_o, axis=1)
        dgw_ref[...] += jnp.concatenate(dgw, axis=1)
        dq_ref[...] = (_dot(d_o, m) * scale).astype(dq_ref.dtype)
        dm = dm_scr[...] + jnp.where(_gla_head_mask(), _dot(d_o, qs, 0, 0), 0.0)
        dv_ref[...] = _dot(kd, dm, 1, 1).astype(dv_ref.dtype)
        dkd = _dot(vv, dm)
        dk_ref[...] = (dkd * e).astype(dk_ref.dtype)
        dm_scr[...] = dm * decay
        tri_strict = (_iota((CHUNK, CHUNK), 1) < _iota((CHUNK, CHUNK), 0)).astype(F32)
        dla = jnp.sum(dm * m_prev, axis=0, keepdims=True) * decay + _dot_exact(tri_strict, dkd * kd)
        dz = dla * jax.nn.sigmoid(-z) * (1.0 / GLA_GATE_TAU)
        da_ref[...] = _dot(dz, wup_ref[...], 1, 1).astype(da_ref.dtype)
        dwup_ref[...] += _dot(a_ref[...], dz, 0, 0)
        dba_ref[...] += jnp.sum(dz, axis=0, keepdims=True)

    chunk_of = lambda i: nc - 1 - i
    in_specs = _gla_specs(chunk_of) + [
        pl.BlockSpec((None,) + GLA_STATE, lambda i: (chunk_of(i), 0, 0)),
        pl.BlockSpec((None,) + GLA_STATE, lambda i: (jnp.maximum(chunk_of(i) - 1, 0), 0, 0)),
        pl.BlockSpec((CHUNK, GROUP_WIDTH), lambda i: (chunk_of(i), 0))]
    rows = lambda width: pl.BlockSpec((CHUNK, width), lambda i: (chunk_of(i), 0))
    const = lambda r, c: pl.BlockSpec((r, c), lambda i: (0, 0))
    return _pcall(
        body, ride, grid=(nc,), in_specs=in_specs,
        out_specs=[rows(GLA_KW), rows(GLA_KW), rows(GROUP_WIDTH), rows(GROUP_WIDTH), rows(LANES),
                   const(LANES, GLA_KW), const(1, GLA_KW), const(1, GROUP_WIDTH)],
        out_shape=[jax.ShapeDtypeStruct((t, GLA_KW), ACT_DTYPE), jax.ShapeDtypeStruct((t, GLA_KW), ACT_DTYPE),
                   jax.ShapeDtypeStruct((t, GROUP_WIDTH), ACT_DTYPE), jax.ShapeDtypeStruct((t, GROUP_WIDTH), ACT_DTYPE),
                   jax.ShapeDtypeStruct((t, LANES), ACT_DTYPE), jax.ShapeDtypeStruct((LANES, GLA_KW), F32),
                   jax.ShapeDtypeStruct((1, GLA_KW), F32), jax.ShapeDtypeStruct((1, GROUP_WIDTH), F32)],
        scratch_shapes=[pltpu.VMEM(GLA_STATE, F32)],
        semantics=("arbitrary",), name="gla_bwd")(
            pmm, pmm, pmm, pel, pel, w_up, b_a, gnorm_w, states, states, dmix)


CUM_BLOCK = 256


def _fox_gate_fwd(pel, b_f):
    t = pel.shape[0]
    nb = t // CUM_BLOCK

    def body(f_ref, b_ref, cum_ref, cum_t_ref):
        tri = (_iota((CUM_BLOCK, CUM_BLOCK), 1) <= _iota((CUM_BLOCK, CUM_BLOCK), 0)).astype(F32)
        carry = jnp.zeros((1, LANES), F32)
        for blk in range(nb):
            rows = slice(blk * CUM_BLOCK, (blk + 1) * CUM_BLOCK)
            cum = _dot_exact(tri, _log_sigmoid(f_ref[rows, :] + b_ref[...])) + carry
            cum_ref[rows, :] = cum
            cum_t_ref[blk] = cum.T[:ATT_HEADS, :]
            carry = cum[CUM_BLOCK - 1:CUM_BLOCK, :]

    return pl.pallas_call(
        body, grid=(1,),
        in_specs=[pl.BlockSpec((t, LANES), lambda i: (0, 5)), pl.BlockSpec((1, LANES), lambda i: (0, 0))],
        out_specs=[pl.BlockSpec((t, LANES), lambda i: (0, 0)),
                   pl.BlockSpec((nb, ATT_HEADS, CUM_BLOCK), lambda i: (0, 0, 0))],
        out_shape=[jax.ShapeDtypeStruct((t, LANES), F32), jax.ShapeDtypeStruct((nb, ATT_HEADS, CUM_BLOCK), F32)],
        compiler_params=_params("arbitrary"), name="fox_gate_fwd")(pel, b_f)


def _fox_gate_bwd(pel, b_f, dcum_t, dcum_q):
    t = pel.shape[0]
    nb = t // CUM_BLOCK

    def body(f_ref, b_ref, dct_ref, dcq_ref, df_ref, db_ref):
        tri_up = (_iota((CUM_BLOCK, CUM_BLOCK), 1) >= _iota((CUM_BLOCK, CUM_BLOCK), 0)).astype(F32)
        carry = jnp.zeros((1, LANES), F32)
        db = jnp.zeros((1, LANES), F32)
        for blk in reversed(range(nb)):
            rows = slice(blk * CUM_BLOCK, (blk + 1) * CUM_BLOCK)
            dls = _dot_exact(tri_up, dct_ref[blk].T + dcq_ref[rows, :]) + carry
            carry = dls[0:1, :]
            df = dls * jax.nn.sigmoid(-(f_ref[rows, :] + b_ref[...]))
            df_ref[rows, :] = df.astype(df_ref.dtype)
            db = db + jnp.sum(df, axis=0, keepdims=True)
        db_ref[...] = db

    return pl.pallas_call(
        body, grid=(1,),
        in_specs=[pl.BlockSpec((t, LANES), lambda i: (0, 5)), pl.BlockSpec((1, LANES), lambda i: (0, 0)),
                  pl.BlockSpec((nb, LANES, CUM_BLOCK), lambda i: (0, 0, 0)), pl.BlockSpec((t, LANES), lambda i: (0, 0))],
        out_specs=[pl.BlockSpec((t, LANES), lambda i: (0, 0)), pl.BlockSpec((1, LANES), lambda i: (0, 0))],
        out_shape=[jax.ShapeDtypeStruct((t, LANES), ACT_DTYPE), jax.ShapeDtypeStruct((1, LANES), F32)],
        compiler_params=_params("arbitrary"), name="fox_gate_bwd")(pel, b_f, dcum_t, dcum_q)


FOX_Q_BLOCK = 256


assert FOX_Q_BLOCK == CUM_BLOCK


def _fox_scores(q_ref, k_ref, cum_ref, cum_t_ref, h, i):
    hs = slice(h * HEAD_DIM, (h + 1) * HEAD_DIM)
    nb = cum_t_ref.shape[0]
    key_gate = jnp.concatenate([cum_t_ref[kb, h:h + 1, :] for kb in range(nb)], axis=1)
    s = _dot(q_ref[:, hs], k_ref[:, hs], 1, 1) * (HEAD_DIM ** -0.5) + (cum_ref[:, h:h + 1] - key_gate)
    shape = (FOX_Q_BLOCK, nb * FOX_Q_BLOCK)
    return jnp.where(_iota(shape, 1) <= i * FOX_Q_BLOCK + _iota(shape, 0), s, NEG)


def _fox_specs(t):
    bq, nb = FOX_Q_BLOCK, t // FOX_Q_BLOCK
    return [pl.BlockSpec((bq, GROUP_WIDTH), lambda i: (i, 2)), pl.BlockSpec((t, GROUP_WIDTH), lambda i: (0, 3)),
            pl.BlockSpec((t, GROUP_WIDTH), lambda i: (0, 4)), pl.BlockSpec((bq, LANES), lambda i: (i, 0)),
            pl.BlockSpec((nb, ATT_HEADS, bq), lambda i: (0, 0, 0))]


def _fox_fwd(pmm, cum, cum_t, ride=None):
    t = pmm.shape[0]
    bq = FOX_Q_BLOCK

    def body(q_ref, k_ref, v_ref, cum_ref, cum_t_ref, o_ref, lse_ref):
        i = pl.program_id(0)
        lse_ref[...] = jnp.zeros_like(lse_ref)
        for h in range(ATT_HEADS):
            hs = slice(h * HEAD_DIM, (h + 1) * HEAD_DIM)
            s = _fox_scores(q_ref, k_ref, cum_ref, cum_t_ref, h, i)
            m = jnp.max(s, axis=-1, keepdims=True)
            p = jnp.exp(s - m)
            l = jnp.sum(p, axis=-1, keepdims=True)
            o_ref[:, hs] = (_dot(p, v_ref[:, hs]) / l).astype(o_ref.dtype)
            lse_ref[:, h:h + 1] = m + jnp.log(l)

    return _pcall(
        body, ride, grid=(t // bq,), in_specs=_fox_specs(t),
        out_specs=[pl.BlockSpec((bq, GROUP_WIDTH), lambda i: (i, 0)), pl.BlockSpec((bq, LANES), lambda i: (i, 0))],
        out_shape=[jax.ShapeDtypeStruct((t, GROUP_WIDTH), ACT_DTYPE), jax.ShapeDtypeStruct((t, LANES), F32)],
        semantics=("parallel",), name="fox_fwd")(pmm, pmm, pmm, cum, cum_t)


def _fox_bwd(pmm, cum, cum_t, lse, dmix, ride=None):
    t = pmm.shape[0]
    bq, nb = FOX_Q_BLOCK, t // FOX_Q_BLOCK
    scale = HEAD_DIM ** -0.5

    def body(q_ref, k_ref, v_ref, cum_ref, cum_t_ref, lse_ref, do_ref, dq_ref, dk_ref, dv_ref, dct_ref, dcq_ref):
        i = pl.program_id(0)

        @pl.when(i == 0)
        def _():
            dk_ref[...] = jnp.zeros_like(dk_ref)
            dv_ref[...] = jnp.zeros_like(dv_ref)
            dct_ref[...] = jnp.zeros_like(dct_ref)

        dcq_ref[...] = jnp.zeros_like(dcq_ref)
        for h in range(ATT_HEADS):
            hs = slice(h * HEAD_DIM, (h + 1) * HEAD_DIM)
            s = _fox_scores(q_ref, k_ref, cum_ref, cum_t_ref, h, i)
            p = jnp.exp(s - lse_ref[:, h:h + 1])
            do = do_ref[:, hs]
            dp = _dot(do, v_ref[:, hs], 1, 1)
            ds = p * (dp - jnp.sum(p * dp, axis=-1, keepdims=True))
            dq_ref[:, hs] = (_dot(ds, k_ref[:, hs]) * scale).astype(dq_ref.dtype)
            dk_ref[:, hs] += _dot(ds, q_ref[:, hs], 0, 0) * scale
            dv_ref[:, hs] += _dot(p, do, 0, 0)
            key_side = -jnp.sum(ds, axis=0, keepdims=True)
            for kb in range(nb):
                dct_ref[kb, h:h + 1, :] += key_side[:, kb * bq:(kb + 1) * bq]
            dcq_ref[:, h:h + 1] = jnp.sum(ds, axis=1, keepdims=True)

    whole = pl.BlockSpec((t, GROUP_WIDTH), lambda i: (0, 0))
    return _pcall(
        body, ride, grid=(t // bq,),
        in_specs=_fox_specs(t) + [pl.BlockSpec((bq, LANES), lambda i: (i, 0)),
                                  pl.BlockSpec((bq, GROUP_WIDTH), lambda i: (i, 1))],
        out_specs=[pl.BlockSpec((bq, GROUP_WIDTH), lambda i: (i, 0)), whole, whole,
                   pl.BlockSpec((nb, LANES, bq), lambda i: (0, 0, 0)), pl.BlockSpec((bq, LANES), lambda i: (i, 0))],
        out_shape=[jax.ShapeDtypeStruct((t, GROUP_WIDTH), ACT_DTYPE), jax.ShapeDtypeStruct((t, GROUP_WIDTH), F32),
                   jax.ShapeDtypeStruct((t, GROUP_WIDTH), F32), jax.ShapeDtypeStruct((nb, LANES, bq), F32),
                   jax.ShapeDtypeStruct((t, LANES), F32)],
        semantics=("arbitrary",), name="fox_bwd")(pmm, pmm, pmm, cum, cum_t, lse, dmix)


CA_Q_BLOCK = 4 * CHUNK
CA_WINDOW = CA_Q_BLOCK + CA_LEFT
CA_BASE = 1024


def _ca_bias_base(rel_bias):
    n = rel_bias.shape[0]
    flat = CA_Q_BLOCK + CA_LEFT - REL_CLIP
    tail = CA_BASE - flat - (2 * REL_CLIP + 1)
    return jnp.concatenate([jnp.broadcast_to(rel_bias[:, 2 * REL_CLIP:], (n, flat)), rel_bias[:, ::-1],
                            jnp.broadcast_to(rel_bias[:, :1], (n, tail))], axis=1)


def _ca_bias_base_grad(dbase):
    flat = CA_Q_BLOCK + CA_LEFT - REL_CLIP
    mid = dbase[:, flat:flat + 2 * REL_CLIP + 1][:, ::-1]
    lo = jnp.sum(dbase[:, flat + 2 * REL_CLIP + 1:], axis=1, keepdims=True)
    hi = jnp.sum(dbase[:, :flat], axis=1, keepdims=True)
    pad = jnp.zeros((dbase.shape[0], 2 * REL_CLIP - 1), F32)
    return mid + jnp.concatenate([lo, pad, hi], axis=1)


def _ca_mask(i):
    r, j = _iota((CA_Q_BLOCK, CA_WINDOW), 0), _iota((CA_Q_BLOCK, CA_WINDOW), 1)
    rc, jc = r // CHUNK, j // CHUNK
    return (jc >= rc) & (jc <= rc + CA_LEFT // CHUNK) & (i * CA_Q_BLOCK + j >= CA_LEFT)


def _ca_fill_bias(i, base_ref, bias_scr):
    @pl.when(i == 0)
    def _():
        for h in range(ATT_HEADS):
            rows = jnp.broadcast_to(base_ref[h:h + 1, :], (CA_Q_BLOCK, CA_BASE))
            bias_scr[h] = pltpu.roll(rows, CA_BASE - CA_Q_BLOCK, 1, stride=1, stride_axis=0)[:, :CA_WINDOW]


def _ca_scores(q_ref, kp_ref, bias_scr, win, h, mask):
    hs = slice(h * HEAD_DIM, (h + 1) * HEAD_DIM)
    s = _dot(q_ref[:, hs], kp_ref[win, hs], 1, 1) * (HEAD_DIM ** -0.5)
    return jnp.where(mask, s + bias_scr[h], NEG)


CA_BIAS_SCRATCH = pltpu.VMEM((ATT_HEADS, CA_Q_BLOCK, CA_WINDOW), F32)


def _ca_fwd(pmm, kp, vp, base, ride=None):
    t = pmm.shape[0]

    def body(q_ref, kp_ref, vp_ref, base_ref, o_ref, lse_ref, bias_scr):
        i = pl.program_id(0)
        _ca_fill_bias(i, base_ref, bias_scr)
        win = pl.ds(pl.multiple_of(i * CA_Q_BLOCK, CA_Q_BLOCK), CA_WINDOW)
        mask = _ca_mask(i)
        lse_ref[...] = jnp.zeros_like(lse_ref)
        for h in range(ATT_HEADS):
            hs = slice(h * HEAD_DIM, (h + 1) * HEAD_DIM)
            s = _ca_scores(q_ref, kp_ref, bias_scr, win, h, mask)
            m = jnp.max(s, axis=-1, keepdims=True)
            p = jnp.exp(s - m)
            l = jnp.sum(p, axis=-1, keepdims=True)
            o_ref[:, hs] = (_dot(p, vp_ref[win, hs]) / l).astype(o_ref.dtype)
            lse_ref[:, h:h + 1] = m + jnp.log(l)

    padded = pl.BlockSpec((t + CA_LEFT, GROUP_WIDTH), lambda i: (0, 0))
    return _pcall(
        body, ride, grid=(t // CA_Q_BLOCK,),
        in_specs=[pl.BlockSpec((CA_Q_BLOCK, GROUP_WIDTH), lambda i: (i, 0)), padded, padded,
                  pl.BlockSpec((ATT_HEADS, CA_BASE), lambda i: (0, 0))],
        out_specs=[pl.BlockSpec((CA_Q_BLOCK, GROUP_WIDTH), lambda i: (i, 0)),
                   pl.BlockSpec((CA_Q_BLOCK, LANES), lambda i: (i, 0))],
        out_shape=[jax.ShapeDtypeStruct((t, GROUP_WIDTH), ACT_DTYPE), jax.ShapeDtypeStruct((t, LANES), F32)],
        scratch_shapes=[CA_BIAS_SCRATCH], semantics=("arbitrary",), name="ca_fwd")(pmm, kp, vp, base)


def _ca_bwd(pmm, kp, vp, base, lse, dmix, ride=None):
    t = pmm.shape[0]
    scale = HEAD_DIM ** -0.5

    def body(q_ref, kp_ref, vp_ref, base_ref, lse_ref, do_ref, dq_ref, dkp_ref, dvp_ref, dbase_ref, bias_scr):
        i = pl.program_id(0)
        _ca_fill_bias(i, base_ref, bias_scr)

        @pl.when(i == 0)
        def _():
            dkp_ref[...] = jnp.zeros_like(dkp_ref)
            dvp_ref[...] = jnp.zeros_like(dvp_ref)
            dbase_ref[...] = jnp.zeros_like(dbase_ref)

        win = pl.ds(pl.multiple_of(i * CA_Q_BLOCK, CA_Q_BLOCK), CA_WINDOW)
        mask = _ca_mask(i)
        flip = (_iota((CA_Q_BLOCK, CA_Q_BLOCK), 0) + _iota((CA_Q_BLOCK, CA_Q_BLOCK), 1) == CA_Q_BLOCK - 1).astype(F32)
        for h in range(ATT_HEADS):
            hs = slice(h * HEAD_DIM, (h + 1) * HEAD_DIM)
            s = _ca_scores(q_ref, kp_ref, bias_scr, win, h, mask)
            p = jnp.exp(s - lse_ref[:, h:h + 1])
            do = do_ref[:, hs]
            dp = _dot(do, vp_ref[win, hs], 1, 1)
            ds = p * (dp - jnp.sum(p * dp, axis=-1, keepdims=True))
            dq_ref[:, hs] = (_dot(ds, kp_ref[win, hs]) * scale).astype(dq_ref.dtype)
            dkp_ref[win, hs] += _dot(ds, q_ref[:, hs], 0, 0) * scale
            dvp_ref[win, hs] += _dot(p, do, 0, 0)
            rev = jnp.concatenate([_dot(flip, ds), jnp.zeros((CA_Q_BLOCK, CA_BASE - CA_WINDOW), F32)], axis=1)
            lined = pltpu.roll(rev, 1, 1, stride=1, stride_axis=0)
            dbase_ref[h:h + 1, :] += jnp.sum(lined, axis=0, keepdims=True)

    padded = pl.BlockSpec((t + CA_LEFT, GROUP_WIDTH), lambda i: (0, 0))
    return _pcall(
        body, ride, grid=(t // CA_Q_BLOCK,),
        in_specs=[pl.BlockSpec((CA_Q_BLOCK, GROUP_WIDTH), lambda i: (i, 0)), padded, padded,
                  pl.BlockSpec((ATT_HEADS, CA_BASE), lambda i: (0, 0)),
                  pl.BlockSpec((CA_Q_BLOCK, LANES), lambda i: (i, 0)),
                  pl.BlockSpec((CA_Q_BLOCK, GROUP_WIDTH), lambda i: (i, 0))],
        out_specs=[pl.BlockSpec((CA_Q_BLOCK, GROUP_WIDTH), lambda i: (i, 0)), padded, padded,
                   pl.BlockSpec((ATT_HEADS, CA_BASE), lambda i: (0, 0))],
        out_shape=[jax.ShapeDtypeStruct((t, GROUP_WIDTH), ACT_DTYPE),
                   jax.ShapeDtypeStruct((t + CA_LEFT, GROUP_WIDTH), F32),
                   jax.ShapeDtypeStruct((t + CA_LEFT, GROUP_WIDTH), F32),
                   jax.ShapeDtypeStruct((ATT_HEADS, CA_BASE), F32)],
        scratch_shapes=[CA_BIAS_SCRATCH], semantics=("arbitrary",), name="ca_bwd")(pmm, kp, vp, base, lse, dmix)


GELU_C = 0.7978845608028654
GELU_A = 0.044715


def _shift_down(v, k, fill):
    return jnp.where(_iota(v.shape, 0) >= k, pltpu.roll(v, k, 0), fill)


def _shift_up(v, k, fill):
    t = v.shape[0]
    return jnp.where(_iota(v.shape, 0) < t - k, pltpu.roll(v, t - k, 0), fill)


def _linear_scan(a, b, shift):
    k = 1
    while k < a.shape[0]:
        b = a * shift(b, k, 0.0) + b
        a = a * shift(a, k, 1.0)
        k *= 2
    return b


def _neg_expm1(y):
    series = -y * (1.0 + y * (0.5 + y * (1.0 / 6.0 + y * (1.0 / 24.0 + y * (1.0 / 120.0)))))
    return jnp.where(y > -0.1, series, 1.0 - jnp.exp(y))


def _lru_forward(x, g_in, cw, cb, wa, ba, wx, bx, lam):
    xs = [_shift_down(x, CONV_WIDTH - 1 - j, 0.0) for j in range(CONV_WIDTH - 1)] + [x]
    xc = cb + sum(cw[j:j + 1, :] * xs[j] for j in range(CONV_WIDTH))
    r = jax.nn.sigmoid(_dot(xc, wa) + ba)
    i = jax.nn.sigmoid(_dot(xc, wx) + bx)
    lsl = _log_sigmoid(lam)
    la = LRU_C * r * lsl
    a = jnp.exp(la)
    s = jnp.sqrt(_neg_expm1(2.0 * la))
    h = _linear_scan(a, s * (i * xc), _shift_down)
    u = GELU_C * (g_in + GELU_A * g_in * g_in * g_in)
    th = jnp.tanh(u)
    gelu = 0.5 * g_in * (1.0 + th)
    return xs, xc, r, i, lsl, a, s, h, th, gelu


def _lru_specs(t):
    col = lambda off: pl.BlockSpec((t, LANES), lambda j: (0, j + off))
    vec = pl.BlockSpec((1, LANES), lambda j: (0, j))
    mat = pl.BlockSpec((None, LANES, LANES), lambda j: (j, 0, 0))
    return [col(0), col(GROUP_WIDTH // LANES), pl.BlockSpec((CONV_WIDTH, LANES), lambda j: (0, j)),
            vec, mat, vec, mat, vec, vec]


def _lru_fwd(pel, conv_w, conv_b, wa, ba, wx, bx, lam, ride=None):
    t = pel.shape[0]

    def body(g_ref, x_ref, cw_ref, cb_ref, wa_ref, ba_ref, wx_ref, bx_ref, lam_ref, o_ref):
        res = _lru_forward(x_ref[...], g_ref[...], cw_ref[...], cb_ref[...], wa_ref[...], ba_ref[...],
                           wx_ref[...], bx_ref[...], lam_ref[...])
        o_ref[...] = (res[7] * res[9]).astype(o_ref.dtype)

    return _pcall(
        body, ride, grid=(GROUP_WIDTH // LANES,), in_specs=_lru_specs(t),
        out_specs=pl.BlockSpec((t, LANES), lambda j: (0, j)),
        out_shape=jax.ShapeDtypeStruct((t, GROUP_WIDTH), ACT_DTYPE),
        semantics=("parallel",), name="lru_fwd")(pel, pel, conv_w, conv_b, wa, ba, wx, bx, lam)


def _lru_bwd(pel, conv_w, conv_b, wa, ba, wx, bx, lam, dmix, ride=None):
    t = pel.shape[0]

    def body(g_ref, x_ref, cw_ref, cb_ref, wa_ref, ba_ref, wx_ref, bx_ref, lam_ref, do_ref,
             dg_ref, dx_ref, dcw_ref, dcb_ref, dwa_ref, dba_ref, dwx_ref, dbx_ref, dlam_ref):
        g_in, cw, lam = g_ref[...], cw_ref[...], lam_ref[...]
        xs, xc, r, i, lsl, a, s, h, th, gelu = _lru_forward(
            x_ref[...], g_in, cw, cb_ref[...], wa_ref[...], ba_ref[...], wx_ref[...], bx_ref[...], lam)
        dout = do_ref[...]
        dgelu = 0.5 * (1.0 + th) + 0.5 * g_in * (1.0 - th * th) * GELU_C * (1.0 + 3.0 * GELU_A * g_in * g_in)
        dg_ref[...] = (dout * h * dgelu).astype(dg_ref.dtype)
        gsum = _linear_scan(_shift_up(a, 1, 0.0), dout * gelu, _shift_up)
        da = gsum * _shift_down(h, 1, 0.0)
        di = gsum * s * xc
        dla = da * a - gsum * (i * xc) * (a * a / s)
        dlam_ref[...] = jnp.sum(dla * (LRU_C * r), axis=0, keepdims=True) * jax.nn.sigmoid(-lam)
        dpr = dla * (LRU_C * lsl) * r * (1.0 - r)
        dpi = di * i * (1.0 - i)
        dxc = gsum * s * i + _dot(dpr, wa_ref[...], 1, 1) + _dot(dpi, wx_ref[...], 1, 1)
        xct = xc.T
        dwa_ref[...] = _dot(xct, dpr)
        dwx_ref[...] = _dot(xct, dpi)
        dba_ref[...] = jnp.sum(dpr, axis=0, keepdims=True)
        dbx_ref[...] = jnp.sum(dpi, axis=0, keepdims=True)
        dcb_ref[...] = jnp.sum(dxc, axis=0, keepdims=True)
        for j in range(CONV_WIDTH):
            dcw_ref[j:j + 1, :] = jnp.sum(dxc * xs[j], axis=0, keepdims=True)
        dx = cw[CONV_WIDTH - 1:CONV_WIDTH, :] * dxc
        for j in range(CONV_WIDTH - 1):
            dx = dx + cw[j:j + 1, :] * _shift_up(dxc, CONV_WIDTH - 1 - j, 0.0)
        dx_ref[...] = dx.astype(dx_ref.dtype)

    col = pl.BlockSpec((t, LANES), lambda j: (0, j))
    vec = pl.BlockSpec((1, LANES), lambda j: (0, j))
    mat = pl.BlockSpec((None, LANES, LANES), lambda j: (j, 0, 0))
    nb = GROUP_WIDTH // LANES
    vshape = jax.ShapeDtypeStruct((1, GROUP_WIDTH), F32)
    mshape = jax.ShapeDtypeStruct((nb, LANES, LANES), F32)
    return _pcall(
        body, ride, grid=(nb,),
        in_specs=_lru_specs(t) + [pl.BlockSpec((t, LANES), lambda j: (0, j + nb))],
        out_specs=[col, col, pl.BlockSpec((CONV_WIDTH, LANES), lambda j: (0, j)), vec, mat, vec, mat, vec, vec],
        out_shape=[jax.ShapeDtypeStruct((t, GROUP_WIDTH), ACT_DTYPE), jax.ShapeDtypeStruct((t, GROUP_WIDTH), ACT_DTYPE),
                   jax.ShapeDtypeStruct((CONV_WIDTH, GROUP_WIDTH), F32), vshape, mshape, vshape, mshape, vshape, vshape],
        semantics=("parallel",), name="lru_bwd")(
            pel, pel, conv_w, conv_b, wa, ba, wx, bx, lam, dmix)


def _block_diag_pairs(w):
    z = jnp.zeros((LRU_BLOCK_DIM, LRU_BLOCK_DIM), w.dtype)
    return jnp.stack([jnp.block([[w[2 * j], z], [z, w[2 * j + 1]]]) for j in range(w.shape[0] // 2)])


def _block_diag_pairs_grad(dw):
    b = LRU_BLOCK_DIM
    return jnp.stack([dw[n // 2, (n % 2) * b:(n % 2 + 1) * b, (n % 2) * b:(n % 2 + 1) * b] for n in range(2 * dw.shape[0])])


def _row_tile(r):
    return ROW_TILE if r % ROW_TILE == 0 else r


def _pair_sum(g, got, place, name):
    _, r, c = g.shape
    tile = r

    def body(place_ref, a_ref, b_ref, o_ref):
        o_ref[...] = (a_ref[...].astype(F32) + b_ref[...].astype(F32)).astype(o_ref.dtype)

    blk = pl.BlockSpec((1, tile, c), lambda k, i, place_ref: (k, i, 0))
    return pl.pallas_call(
        body,
        grid_spec=pltpu.PrefetchScalarGridSpec(
            num_scalar_prefetch=1, grid=(N_CHIPS, r // tile),
            in_specs=[pl.BlockSpec((1, tile, c), lambda k, i, place_ref: (2 * k + place_ref[0], i, 0)), blk],
            out_specs=blk),
        out_shape=jax.ShapeDtypeStruct(got.shape, got.dtype),
        compiler_params=_params("parallel", "parallel"), name=name)(place, g, got)


def _adamw_update(g, w_ref, m_ref, v_ref, g_ref, d_ref, nm_ref, nv_ref):
    nm = ADAM_B1 * m_ref[...] + (1.0 - ADAM_B1) * g
    nv = ADAM_B2 * v_ref[...] + (1.0 - ADAM_B2) * jnp.square(g)
    m_hat = nm / (1.0 - ADAM_B1 ** ADAM_STEP)
    v_hat = nv / (1.0 - ADAM_B2 ** ADAM_STEP)
    g_ref[...] = g
    d_ref[...] = -ADAM_LR * (m_hat / (jnp.sqrt(v_hat) + ADAM_EPS) + ADAM_WD * w_ref[...])
    nm_ref[...] = nm
    nv_ref[...] = nv


def _adamw_sharded(parts, w, m, v, place, name):
    n_layers, r, c = w.shape
    tile = _row_tile(r)
    nb = r // tile

    def body(place_ref, *refs):
        layer = pl.program_id(0)
        g = None
        for l in range(n_layers):
            s_ref, r_ref = refs[2 * l], refs[2 * l + 1]
            g_l = s_ref[0].astype(F32) + r_ref[0].astype(F32) + r_ref[1].astype(F32) + r_ref[2].astype(F32)
            g = g_l if g is None else jnp.where(layer == l, g_l, g)
        _adamw_update(g, *refs[2 * n_layers:])

    def part_specs(l):
        rows = lambda q, i: jnp.where(q < l, 0, jnp.where(q > l, nb - 1, i))
        return [pl.BlockSpec((1, tile, c), lambda q, i, place_ref: (place_ref[1], rows(q, i), 0)),
                pl.BlockSpec((3, tile, c), lambda q, i, place_ref: (0, rows(q, i), 0))]

    in_specs, args = [], []
    for l, (s, recv) in enumerate(parts):
        in_specs += part_specs(l)
        args += [s, recv]
    blk = pl.BlockSpec((None, tile, c), lambda q, i, place_ref: (q, i, 0))
    out = jax.ShapeDtypeStruct((n_layers, r, c), F32)
    return pl.pallas_call(
        body,
        grid_spec=pltpu.PrefetchScalarGridSpec(
            num_scalar_prefetch=1, grid=(n_layers, nb), in_specs=in_specs + [blk, blk, blk],
            out_specs=[blk, blk, blk, blk]),
        out_shape=[out, out, out, out], compiler_params=_params("arbitrary", "arbitrary"), name=name)(
            place, *args, w, m, v)


def _adamw_small(repl_parts, vec_parts, w, m, v, place):
    n_r, n = len(repl_parts), len(w)
    shapes = [a.shape for a in w]

    def body(place_ref, *refs):
        parts, rest = refs[:n], refs[n:]
        for k in range(n):
            take = (lambda p: parts[k][p]) if k < n_r else (lambda p: parts[k][p, 0])
            g = take(0)
            for p in range(1, N_DEV):
                g = g + take(p)
            _adamw_update(g, rest[k], rest[n + k], rest[2 * n + k], *rest[3 * n + 4 * k:3 * n + 4 * k + 4])

    def whole(shape):
        return pl.BlockSpec(shape, lambda i, place_ref: (0,) * len(shape))

    def mine(shard):
        return pl.BlockSpec((N_DEV, 1) + shard, lambda i, place_ref: (0, place_ref[2]) + (0,) * len(shard))

    in_specs = [whole(a.shape) for a in repl_parts] + [mine(s) for s in shapes[n_r:]] + [whole(s) for s in shapes] * 3
    outs = pl.pallas_call(
        body,
        grid_spec=pltpu.PrefetchScalarGridSpec(
            num_scalar_prefetch=1, grid=(1,), in_specs=in_specs,
            out_specs=[whole(s) for s in shapes for _ in range(4)]),
        out_shape=[jax.ShapeDtypeStruct(s, F32) for s in shapes for _ in range(4)],
        compiler_params=_params("arbitrary"), name="adamw_small")(place, *repl_parts, *vec_parts, *w, *m, *v)
    return [outs[4 * k:4 * k + 4] for k in range(n)]


SHARDED = {"norm_w": 2, "w_in_even": 2, "gla_w_a_up": 2, "w_out_even": 1, "w_in_odd": 2, "conv_w": 2, "conv_b": 1,
           "lru_b_a": 1, "lru_b_x": 1, "lru_lambda": 1, "w_out_odd": 1, "w_mlp_up": 2, "w_mlp_down": 1}
REPLICATED = ["gla_b_a", "gla_norm_w", "fox_b_f", "rel_bias", "lru_w_a", "lru_w_x"]
WEIGHTS = ["norm_w", "w_in_even", "gla_w_a_up", "gla_b_a", "gla_norm_w", "fox_b_f", "w_out_even", "w_in_odd",
           "rel_bias", "conv_w", "conv_b", "lru_w_a", "lru_b_a", "lru_w_x", "lru_b_x", "lru_lambda", "w_out_odd",
           "w_mlp_up", "w_mlp_down"]
MATRICES = ("w_in_even", "w_out_even", "w_in_odd", "w_out_odd", "w_mlp_up", "w_mlp_down")
TRANSPOSED = ("w_in_even", "w_in_odd")
VECTORS = tuple(n for n in SHARDED if n not in MATRICES)
MATRIX_BLOCKS = (("w_in_even", 0), ("w_out_even", 0), ("w_in_odd", 0), ("w_out_odd", 0),
                 ("w_mlp_up", 0), ("w_mlp_up", 1), ("w_mlp_down", 0), ("w_mlp_down", 1))


def _join_shards(blocks, axis):
    moved = jnp.moveaxis(blocks, 0, axis)
    shape = moved.shape
    return moved.reshape(shape[:axis] + (shape[axis] * shape[axis + 1],) + shape[axis + 2:])


def _split_shards(full, axis):
    shape = full.shape
    cut = full.reshape(shape[:axis] + (N_DEV, shape[axis] // N_DEV) + shape[axis + 1:])
    return jnp.moveaxis(cut, axis, 0)


EVEN_SPLITS = (0, 256, 512, 1024, 1536, 1552, 2064, 2576, 3088, 3096)


def _even_in_split(wt):
    c = [wt[EVEN_SPLITS[k]:EVEN_SPLITS[k + 1]] for k in range(9)]
    gq, gk, gv, gr, ga, fq, fk, fv, ff = c
    padrows = lambda a: jnp.pad(a, ((0, LANES - a.shape[0]), (0, 0)))
    return jnp.concatenate([gq, gk, gv, fq, fk, fv], axis=0), jnp.concatenate([gr, padrows(ga), padrows(ff)], axis=0)


def _even_in_merge(dmm, dele):
    return jnp.concatenate([dmm[:1024], dele[:512], dele[512:512 + GLA_RANK], dmm[1024:2560],
                            dele[640:640 + ATT_HEADS]], axis=0)


def _forward_backward(x, target, shard, vec_shard, w, place):
    w = dict(w)
    g, dnorm, sums, recv = {}, {}, {}, {}
    nrm = lambda l, k: w["norm_w"][l, k][None, :]
    gather = lambda *keys: _gather_plan([shard[k] for k in keys])
    blocks = lambda r, c: (N_DEV, r // N_DEV, c)

    def pair_sum(key):
        sums[key] = _pair_sum(g[key], got[key], place, f"rs_pair_sum_{key[0]}_{key[1]}")

    got = {}

    def mlp_fwd(xin, layer, ride_up, ride_down):
        up = _mm(xin, w["w_mlp_up"][layer], out_dtype=ACT_DTYPE, tm=TM_FWD, tn=D_FF // N_DEV, b_blocked=True,
                 a_norm=nrm(layer, 2), name=f"mlp_up_{layer}", ride=ride_up)
        (u, h), rode_up = up if ride_up is not None else (up, None)
        down = _mm(u, w["w_mlp_down"][layer], out_dtype=F32, tm=TM_DX // 2, tn=D_MODEL, a_sqrelu=True,
                   res_norm=(xin, nrm(layer, 3)), name=f"mlp_down_{layer}", ride=ride_down)
        (yv, xout), rode_down = down if ride_down is not None else (down, None)
        return xout, (xin, h, u, yv), rode_up, rode_down

    def mlp_bwd(dxout, saved, layer, ride):
        xin, h, u, yv = saved
        k_up, k_down = ("w_mlp_up", layer), ("w_mlp_down", layer)
        dy, dnorm[(layer, 3)] = _norm_bwd(dxout, yv, nrm(layer, 3), out_dtype=ACT_DTYPE, name=f"norm_mlp_out_bwd_{layer}")
        du = _mm(dy, w["w_mlp_down"][layer], nt=True, out_dtype=ACT_DTYPE, tm=TM_DX, tn=TN, drelu_of=u,
                 name=f"mlp_down_dx_{layer}", ride=ride)
        rode = None
        if ride is not None:
            du, rode = du
        g[k_down] = _mm(u, dy, ta=True, out_dtype=WIRE_DTYPE, tm=TM_DW, tn=TN, a_sqrelu=True,
                        name=f"mlp_down_dw_{layer}").reshape(blocks(D_FF, D_MODEL))
        g[k_up] = _mm(h, du, ta=True, out_dtype=WIRE_DTYPE, tm=TM_DW, tn=D_FF // N_DEV, out_blocked=True,
                      name=f"mlp_up_dw_{layer}")
        w_up = jnp.moveaxis(w["w_mlp_up"][layer], 0, 1).reshape(D_MODEL, D_FF)
        (dxin, dnorm[(layer, 2)]), (got[k_down], got[k_up]) = _mm(
            du, w_up, nt=True, out_dtype=F32, tm=TM_DX // 2, tn=D_MODEL, norm_bwd=(xin, nrm(layer, 2), dxout),
            name=f"mlp_up_dx_{layer}", ride=_sibling_plan([g[k_down], g[k_up]]))
        pair_sum(k_down)
        pair_sum(k_up)
        return dxin, rode

    first = _run_plan(_gather_plan([shard[("w_in_even", 0)]] + [vec_shard[n] for n in VECTORS]),
                      "weights_all_gather_first")
    w["w_in_even"] = first[0].reshape(-1, D_MODEL)
    for n, b in zip(VECTORS, first[1:]):
        w[n] = _join_shards(b, SHARDED[n])
    w["w_mlp_up"], w["w_mlp_down"] = [None] * DEPTH, [None] * DEPTH

    wmm_e, wel_e = _even_in_split(w["w_in_even"])
    w_up_pad = jnp.pad(w["gla_w_a_up"][0], ((0, LANES - GLA_RANK), (0, 0)))
    b_f_pad = jnp.pad(w["fox_b_f"], ((0, 0), (0, LANES - ATT_HEADS)))
    (pmm0, h0), (w_out_even,) = _mm(x, wmm_e, nt=True, out_dtype=ACT_DTYPE, tm=TM_FWD, tn=TN, a_norm=nrm(0, 0),
                                    name="in_even_mm", ride=gather(("w_out_even", 0)))
    pel0 = _mm(h0, wel_e, nt=True, out_dtype=F32, tm=TM_FWD, tn=768, name="in_even_el")
    (out_a, states), (w_in_odd,) = _gla_fwd(pmm0, pel0, w_up_pad, w["gla_b_a"], w["gla_norm_w"],
                                            ride=gather(("w_in_odd", 0)))
    cum, cum_t = _fox_gate_fwd(pel0, b_f_pad)
    (out_b, lse_b), (w["w_mlp_up"][0], w_mlp_down0) = _fox_fwd(pmm0, cum, cum_t,
                                                               ride=gather(("w_mlp_up", 0), ("w_mlp_down", 0)))
    w["w_out_even"] = w_out_even.reshape(D_MODEL, D_MODEL)
    w["w_mlp_down"][0] = w_mlp_down0.reshape(D_FF, D_MODEL)
    mix_in0 = jnp.concatenate([out_a, out_b], axis=1)
    mix0, x1 = _mm(mix_in0, w["w_out_even"], out_dtype=F32, tm=TM_DX, tn=D_MODEL, res_norm=(x, nrm(0, 1)),
                   name="out_even")
    x2, mlp0, _, (w["w_mlp_up"][1],) = mlp_fwd(x1, 0, None, gather(("w_mlp_up", 1)))
    w["w_in_odd"] = w_in_odd.reshape(-1, D_MODEL)

    w_in_o = w["w_in_odd"]
    n_mm_o = 3 * GROUP_WIDTH
    wa_bd, wx_bd = _block_diag_pairs(w["lru_w_a"][0]), _block_diag_pairs(w["lru_w_x"][0])
    base = _ca_bias_base(w["rel_bias"][0])
    pmm1, h1 = _mm(x2, w_in_o[:n_mm_o], nt=True, out_dtype=ACT_DTYPE, tm=TM_FWD, tn=TN, a_norm=nrm(1, 0),
                   name="in_odd_mm")
    pel1 = _mm(h1, w_in_o[n_mm_o:], nt=True, out_dtype=F32, tm=TM_FWD, tn=TN, name="in_odd_el")
    kp = jnp.pad(pmm1[:, GROUP_WIDTH:2 * GROUP_WIDTH], ((CA_LEFT, 0), (0, 0)))
    vp = jnp.pad(pmm1[:, 2 * GROUP_WIDTH:], ((CA_LEFT, 0), (0, 0)))
    (out_c, lse_c), (w_mlp_down1,) = _ca_fwd(pmm1, kp, vp, base, ride=gather(("w_mlp_down", 1)))
    w["w_mlp_down"][1] = w_mlp_down1.reshape(D_FF, D_MODEL)
    lru_args = (pel1, w["conv_w"][0], w["conv_b"], wa_bd, w["lru_b_a"], wx_bd, w["lru_b_x"], w["lru_lambda"])
    out_d, (w_out_odd,) = _lru_fwd(*lru_args, ride=gather(("w_out_odd", 0)))
    w["w_out_odd"] = w_out_odd.reshape(D_MODEL, D_MODEL)
    mix_in1 = jnp.concatenate([out_c, out_d], axis=1)
    mix1, x3 = _mm(mix_in1, w["w_out_odd"], out_dtype=F32, tm=TM_DX, tn=D_MODEL, res_norm=(x2, nrm(1, 1)),
                   name="out_odd")
    x4, mlp1, _, _ = mlp_fwd(x3, 1, None, None)

    loss, dx4 = _loss_fwd_bwd(x4, target)

    k_oo, k_io, k_oe, k_ie = ("w_out_odd", 0), ("w_in_odd", 0), ("w_out_even", 0), ("w_in_even", 0)
    mlp_keys = lambda l: [("w_mlp_down", l), ("w_mlp_up", l)]
    dx3, _ = mlp_bwd(dx4, mlp1, 1, None)
    dmix1, dnorm[(1, 1)] = _norm_bwd(dx3, mix1, nrm(1, 1), out_dtype=ACT_DTYPE, name="norm_mix_bwd_1")
    g[k_oo] = _mm(mix_in1, dmix1, ta=True, out_dtype=WIRE_DTYPE, tm=TM_DW, tn=TN, name="out_odd_dw").reshape(
        blocks(D_MODEL, D_MODEL))
    dmix_in1, (got[k_oo],) = _mm(dmix1, w["w_out_odd"], nt=True, out_dtype=F32, tm=TM_DX, tn=TN, name="out_odd_dx",
                                 ride=_sibling_plan([g[k_oo]]))
    (dq_c, dkp, dvp, dbase), rode = _ca_bwd(pmm1, kp, vp, base, lse_c, dmix_in1,
                                            ride=_chip_plan([sums[k] for k in mlp_keys(1)]))
    recv.update(zip(mlp_keys(1), rode))
    pair_sum(k_oo)
    (dgate, dxin, g_conv_w, g_conv_b, dwa_bd, g_lru_b_a, dwx_bd, g_lru_b_x, g_lru_lambda), (recv[k_oo],) = _lru_bwd(
        *lru_args, dmix_in1, ride=_chip_plan([sums[k_oo]]))
    dp1 = jnp.concatenate([dq_c, dkp[CA_LEFT:].astype(ACT_DTYPE), dvp[CA_LEFT:].astype(ACT_DTYPE), dgate, dxin], axis=1)
    g[k_io] = _mm(dp1, h1, ta=True, out_dtype=WIRE_DTYPE, tm=dp1.shape[1] // 2, tn=TN, name="in_odd_dw").reshape(
        blocks(dp1.shape[1], D_MODEL))
    (dx2, dnorm[(1, 0)]), (got[k_io],) = _mm(dp1, w_in_o, out_dtype=F32, tm=TM_DX // 2, tn=D_MODEL,
                                             norm_bwd=(x2, nrm(1, 0), dx3), name="in_odd_dx",
                                             ride=_sibling_plan([g[k_io]]))
    pair_sum(k_io)
    g["rel_bias"] = _ca_bias_base_grad(dbase)[None]
    g["conv_w"], g["conv_b"] = g_conv_w[None], g_conv_b
    g["lru_w_a"], g["lru_w_x"] = _block_diag_pairs_grad(dwa_bd)[None], _block_diag_pairs_grad(dwx_bd)[None]
    g["lru_b_a"], g["lru_b_x"], g["lru_lambda"] = g_lru_b_a, g_lru_b_x, g_lru_lambda

    dx1, (recv[k_io],) = mlp_bwd(dx2, mlp0, 0, _chip_plan([sums[k_io]]))
    dmix0, dnorm[(0, 1)] = _norm_bwd(dx1, mix0, nrm(0, 1), out_dtype=ACT_DTYPE, name="norm_mix_bwd_0")
    g[k_oe] = _mm(mix_in0, dmix0, ta=True, out_dtype=WIRE_DTYPE, tm=TM_DW, tn=TN, name="out_even_dw").reshape(
        blocks(D_MODEL, D_MODEL))
    dmix_in0, (got[k_oe],) = _mm(dmix0, w["w_out_even"], nt=True, out_dtype=F32, tm=TM_DX, tn=TN, name="out_even_dx",
                                 ride=_sibling_plan([g[k_oe]]))
    k_md0, k_mu0 = mlp_keys(0)
    pair_sum(k_oe)
    dq_a, dk_a, dv_a, dr_a, da_a, dw_up_pad, g_gla_b_a, g_gla_norm_w = _gla_bwd(
        pmm0, pel0, w_up_pad, w["gla_b_a"], w["gla_norm_w"], states, dmix_in0)
    (dq_b, dk_b, dv_b, dcum_t, dcum_q), (recv[k_md0], recv[k_mu0], recv[k_oe]) = _fox_bwd(
        pmm0, cum, cum_t, lse_b, dmix_in0, ride=_chip_plan([sums[k_md0], sums[k_mu0], sums[k_oe]]))
    df_b, db_f = _fox_gate_bwd(pel0, b_f_pad, dcum_t, dcum_q)
    g["gla_w_a_up"] = dw_up_pad[:GLA_RANK][None]
    g["gla_b_a"], g["gla_norm_w"], g["fox_b_f"] = g_gla_b_a, g_gla_norm_w, db_f[:, :ATT_HEADS]
    dp0 = jnp.concatenate([dq_a, dk_a, dv_a, dq_b, dk_b.astype(ACT_DTYPE), dv_b.astype(ACT_DTYPE), dr_a, da_a, df_b],
                          axis=1)
    w_perm = jnp.concatenate([wmm_e, wel_e], axis=0)
    n_mm_e = wmm_e.shape[0]
    dw_perm, repl_parts = _mm(dp0, h0, ta=True, out_dtype=WIRE_DTYPE, tm=dp0.shape[1] // 2, tn=TN, name="in_even_dw",
                              ride=_gather_plan([g[n] for n in REPLICATED]))
    dw_even = _even_in_merge(dw_perm[:n_mm_e], dw_perm[n_mm_e:])
    g[k_ie] = dw_even.reshape(blocks(dw_even.shape[0], D_MODEL))
    dh0, (got[k_ie],) = _mm(dp0, w_perm, out_dtype=F32, tm=TM_DX, tn=TN, name="in_even_dx",
                            ride=_sibling_plan([g[k_ie]]))
    pair_sum(k_ie)
    (dx0, dnorm[(0, 0)]), (recv[k_ie],) = _norm_bwd(dh0, x, nrm(0, 0), out_dtype=F32, add=dx1, name="norm_in_bwd_0",
                                                     ride=_chip_plan([sums[k_ie]]))

    g["norm_w"] = jnp.stack([jnp.concatenate([dnorm[(l, k)] for k in range(4)], axis=0) for l in range(DEPTH)])
    vec_parts = _run_plan(_gather_plan([_split_shards(g[n], SHARDED[n]) for n in VECTORS]), "vector_grads_all_gather")
    return loss, dx0, sums, recv, repl_parts, vec_parts


def kernel(x, norm_w, w_in_even, gla_w_a_up, gla_b_a, gla_norm_w, fox_b_f, w_out_even, w_in_odd, rel_bias, conv_w, conv_b, lru_w_a, lru_b_a, lru_w_x, lru_b_x, lru_lambda, w_out_odd, w_mlp_up, w_mlp_down, loss_target, m_norm_w, m_w_in_even, m_gla_w_a_up, m_gla_b_a, m_gla_norm_w, m_fox_b_f, m_w_out_even, m_w_in_odd, m_rel_bias, m_conv_w, m_conv_b, m_lru_w_a, m_lru_b_a, m_lru_w_x, m_lru_b_x, m_lru_lambda, m_w_out_odd, m_w_mlp_up, m_w_mlp_down, v_norm_w, v_w_in_even, v_gla_w_a_up, v_gla_b_a, v_gla_norm_w, v_fox_b_f, v_w_out_even, v_w_in_odd, v_rel_bias, v_conv_w, v_conv_b, v_lru_w_a, v_lru_b_a, v_lru_w_x, v_lru_b_x, v_lru_lambda, v_w_out_odd, v_w_mlp_up, v_w_mlp_down):
    wts = dict(zip(WEIGHTS, (norm_w, w_in_even, gla_w_a_up, gla_b_a, gla_norm_w, fox_b_f, w_out_even, w_in_odd, rel_bias,
                             conv_w, conv_b, lru_w_a, lru_b_a, lru_w_x, lru_b_x, lru_lambda, w_out_odd, w_mlp_up,
                             w_mlp_down)))
    mom = dict(zip(WEIGHTS, (m_norm_w, m_w_in_even, m_gla_w_a_up, m_gla_b_a, m_gla_norm_w, m_fox_b_f, m_w_out_even,
                             m_w_in_odd, m_rel_bias, m_conv_w, m_conv_b, m_lru_w_a, m_lru_b_a, m_lru_w_x, m_lru_b_x,
                             m_lru_lambda, m_w_out_odd, m_w_mlp_up, m_w_mlp_down)))
    var = dict(zip(WEIGHTS, (v_norm_w, v_w_in_even, v_gla_w_a_up, v_gla_b_a, v_gla_norm_w, v_fox_b_f, v_w_out_even,
                             v_w_in_odd, v_rel_bias, v_conv_w, v_conv_b, v_lru_w_a, v_lru_b_a, v_lru_w_x, v_lru_b_x,
                             v_lru_lambda, v_w_out_odd, v_w_mlp_up, v_w_mlp_down)))
    ax, ay, ac = lax.axis_index("x"), lax.axis_index("y"), lax.axis_index("c")
    place = jnp.stack([ac, 2 * ax + ay, 4 * ax + 2 * ay + ac]).astype(jnp.int32)

    shard = {(n, l): (wts[n][l].T if n in TRANSPOSED else wts[n][l]).astype(WIRE_DTYPE) for n, l in MATRIX_BLOCKS}
    loss_blk, dx, sums, recv, repl_parts, vec_parts = _forward_backward(
        x[0], loss_target[0], shard, {n: wts[n] for n in VECTORS}, {n: wts[n] for n in REPLICATED}, place)
    loss = lax.psum(loss_blk[0, 0], ("x", "y", "c"))

    view = lambda n, a: jnp.swapaxes(a, 1, 2) if n in TRANSPOSED else a
    upd = {n: [view(n, o) for o in _adamw_sharded(
        [(sums[(n, l)], recv[(n, l)]) for l in range(wts[n].shape[0])], view(n, wts[n]), view(n, mom[n]), view(n, var[n]),
        place, f"adamw_{n}")] for n in MATRICES}
    small = REPLICATED + list(VECTORS)
    upd.update(zip(small, _adamw_small(repl_parts, vec_parts, [wts[n] for n in small], [mom[n] for n in small],
                                       [var[n] for n in small], place)))
    return (loss, dx[None], *[upd[n][kind] for kind in range(4) for n in WEIGHTS])
```

```python
import functools
from typing import Callable, NamedTuple

import jax
import jax.numpy as jnp
from jax import lax
from jax.experimental import pallas as pl
from jax.experimental.pallas import tpu as pltpu

F32 = jnp.float32
MXU_DTYPE = jnp.bfloat16
ACT_DTYPE = jnp.bfloat16
WIRE_DTYPE = jnp.bfloat16

V7X_VMEM_BYTES = 64 * 1024 * 1024
VMEM_LIMIT = (V7X_VMEM_BYTES * 7) // 8
LANES = 128

D_MODEL = 1024
SEQ = 2048
DEPTH = 2
CHUNK = 64
GROUP_WIDTH = D_MODEL // 2
D_FF = 4 * D_MODEL
NORM_EPS = 1e-6
GLA_HEADS = 4
GLA_DV = GROUP_WIDTH // GLA_HEADS
GLA_DK = GLA_DV // 2
GLA_KW = GLA_HEADS * GLA_DK
GLA_RANK = 16
GLA_GATE_TAU = 16.0
HEAD_DIM = 64
ATT_HEADS = GROUP_WIDTH // HEAD_DIM
CA_LEFT = 8 * CHUNK
REL_CLIP = 128
LRU_BLOCK_DIM = 64
CONV_WIDTH = 4
LRU_C = 8.0
N_DEV = 8

ADAM_LR = 0.001
ADAM_B1 = 0.9
ADAM_B2 = 0.999
ADAM_EPS = 1e-08
ADAM_WD = 0.01
ADAM_STEP = 10

NEG = float(jnp.finfo(jnp.float32).min)
MESH = pl.DeviceIdType.MESH


def _params(*sem):
    return pltpu.CompilerParams(dimension_semantics=sem, vmem_limit_bytes=VMEM_LIMIT)


def _dot(a, b, ca=1, cb=0):
    return lax.dot_general(a.astype(MXU_DTYPE), b.astype(MXU_DTYPE), (((ca,), (cb,)), ((), ())),
                           preferred_element_type=F32)


def _dot_exact(a, b):
    return lax.dot_general(a, b, (((1,), (0,)), ((), ())), precision=lax.Precision.HIGHEST,
                           preferred_element_type=F32)


def _log_sigmoid(x):
    return jnp.minimum(x, 0.0) - jnp.log1p(jnp.exp(-jnp.abs(x)))


def _iota(shape, axis):
    return lax.broadcasted_iota(jnp.int32, shape, axis)


ANY = pl.BlockSpec(memory_space=pl.ANY)
N_CHIPS = 4


class _Plan(NamedTuple):
    ins: list
    outs: list
    sems: list
    start: Callable
    finish: Callable


def _place():
    x, y, c = lax.axis_index("x"), lax.axis_index("y"), lax.axis_index("c")
    return x, y, c, [(1 - x, y), (x, 1 - y), (1 - x, 1 - y)]


def _gather_plan(xs):
    n = len(xs)

    def parts(x_refs, out_refs, sems):
        send_sems, recv_sems, local_sems = sems
        x, y, c, chips = _place()
        me, sibling = (x, y, c), (x, y, 1 - c)

        def rows(a, px, py, pc):
            return out_refs[a].at[4 * px + 2 * py + pc]

        def copy(a, k, block, to, src=None):
            return pltpu.make_async_remote_copy(
                src_ref=rows(a, *block) if src is None else src, dst_ref=rows(a, *block),
                send_sem=send_sems.at[7 * a + k], recv_sem=recv_sems.at[7 * a + k], device_id=to, device_id_type=MESH)

        mine = [pltpu.make_async_copy(x_refs[a], rows(a, *me), local_sems.at[a]) for a in range(n)]
        first = []
        for a in range(n):
            first.append(copy(a, 0, me, sibling, src=x_refs[a]))
            first += [copy(a, 1 + j, me, (*chip, c), src=x_refs[a]) for j, chip in enumerate(chips)]
        return c, me, sibling, chips, copy, mine, first

    def start(x_refs, out_refs, sems):
        *_, mine, first = parts(x_refs, out_refs, sems)
        for cp in first + mine:
            cp.start()

    def finish(x_refs, out_refs, sems):
        c, me, sibling, chips, copy, mine, first = parts(x_refs, out_refs, sems)
        passed = []
        for j, chip in enumerate(chips):
            for a in range(n):
                copy(a, 1 + j, (*chip, c), me).wait_recv()
                passed.append(copy(a, 4 + j, (*chip, c), sibling))
                passed[-1].start()
        for a in range(n):
            copy(a, 0, sibling, me).wait_recv()
            for j, chip in enumerate(chips):
                copy(a, 4 + j, (*chip, 1 - c), me).wait_recv()
        for cp in first + passed:
            cp.wait_send()
        for cp in mine:
            cp.wait()

    return _Plan(list(xs), [jax.ShapeDtypeStruct((N_DEV,) + x.shape, x.dtype) for x in xs],
                 [pltpu.SemaphoreType.DMA((7 * n,)), pltpu.SemaphoreType.DMA((7 * n,)), pltpu.SemaphoreType.DMA((n,))],
                 start, finish)


def _exchange_plan(copies_of, ins, outs, per_array):
    n = len(ins)

    def start(in_refs, out_refs, sems):
        for cp in copies_of(in_refs, out_refs, sems):
            cp.start()

    def finish(in_refs, out_refs, sems):
        copies = copies_of(in_refs, out_refs, sems)
        for cp in copies:
            cp.wait_recv()
        for cp in copies:
            cp.wait_send()

    return _Plan(list(ins), outs, [pltpu.SemaphoreType.DMA((per_array * n,)), pltpu.SemaphoreType.DMA((per_array * n,))],
                 start, finish)


def _sibling_plan(gs):
    def copies_of(g_refs, got_refs, sems):
        x, y, c, _ = _place()
        return [pltpu.make_async_remote_copy(
            src_ref=g_refs[a].at[2 * k + (1 - c)], dst_ref=got_refs[a].at[k], send_sem=sems[0].at[N_CHIPS * a + k],
            recv_sem=sems[1].at[N_CHIPS * a + k], device_id=(x, y, 1 - c), device_id_type=MESH)
            for a in range(len(gs)) for k in range(N_CHIPS)]

    return _exchange_plan(copies_of, gs, [jax.ShapeDtypeStruct((N_CHIPS,) + g.shape[1:], g.dtype) for g in gs], N_CHIPS)


def _chip_plan(ss):
    def copies_of(s_refs, out_refs, sems):
        x, y, c, chips = _place()
        return [pltpu.make_async_remote_copy(
            src_ref=s_refs[a].at[2 * px + py], dst_ref=out_refs[a].at[j], send_sem=sems[0].at[3 * a + j],
            recv_sem=sems[1].at[3 * a + j], device_id=(px, py, c), device_id_type=MESH)
            for a in range(len(ss)) for j, (px, py) in enumerate(chips)]

    return _exchange_plan(copies_of, ss, [jax.ShapeDtypeStruct((3,) + s.shape[1:], s.dtype) for s in ss], 3)


def _join_plans(*plans):
    def cut(refs, counts):
        at = 0
        for n in counts:
            yield refs[at:at + n]
            at += n

    def each(in_refs, out_refs, sems):
        return zip(plans, cut(in_refs, [len(p.ins) for p in plans]), cut(out_refs, [len(p.outs) for p in plans]),
                   cut(sems, [len(p.sems) for p in plans]))

    def start(*refs):
        for p, i, o, s in each(*refs):
            p.start(i, o, s)

    def finish(*refs):
        for p, i, o, s in each(*refs):
            p.finish(i, o, s)

    return _Plan([a for p in plans for a in p.ins], [a for p in plans for a in p.outs],
                 [a for p in plans for a in p.sems], start, finish)


def _run_plan(plan, name):
    n_in, n_out = len(plan.ins), len(plan.outs)

    def body(*refs):
        args = refs[:n_in], refs[n_in:n_in + n_out], refs[n_in + n_out:]
        plan.start(*args)
        plan.finish(*args)

    return pl.pallas_call(body, out_shape=plan.outs, in_specs=[ANY] * n_in, out_specs=[ANY] * n_out,
                          scratch_shapes=plan.sems, name=name)(*plan.ins)


def _pcall(body, ride, *, grid, in_specs, out_specs, out_shape, scratch_shapes=(), semantics, name):
    if ride is None:
        return pl.pallas_call(body, grid=grid, in_specs=in_specs, out_specs=out_specs, out_shape=out_shape,
                              scratch_shapes=list(scratch_shapes), compiler_params=_params(*semantics), name=name)
    single = not isinstance(out_shape, (list, tuple))
    out_specs_l, out_shape_l = ([out_specs], [out_shape]) if single else (list(out_specs), list(out_shape))
    n_in, n_out, n_scr = len(in_specs), len(out_shape_l), len(scratch_shapes)
    r_in, r_out = len(ride.ins), len(ride.outs)

    def riding(*refs):
        cuts = [n_in, r_in, n_out, r_out, n_scr]
        groups, at = [], 0
        for width in cuts:
            groups.append(refs[at:at + width])
            at += width
        ins, r_ins, outs, r_outs, scr = groups
        sems = refs[at:]
        first = functools.reduce(jnp.logical_and, [pl.program_id(d) == 0 for d in range(len(grid))])
        last = functools.reduce(jnp.logical_and, [pl.program_id(d) == grid[d] - 1 for d in range(len(grid))])

        @pl.when(first)
        def _():
            ride.start(r_ins, r_outs, sems)

        body(*ins, *outs, *scr)

        @pl.when(last)
        def _():
            ride.finish(r_ins, r_outs, sems)

    call = pl.pallas_call(
        riding, grid=grid, in_specs=list(in_specs) + [ANY] * r_in, out_specs=out_specs_l + [ANY] * r_out,
        out_shape=out_shape_l + list(ride.outs), scratch_shapes=list(scratch_shapes) + list(ride.sems),
        compiler_params=_params(*(["arbitrary"] * len(grid))), name=name)

    def run(*args):
        res = call(*args, *ride.ins)
        return (res[0] if single else list(res[:n_out])), list(res[n_out:])

    return run


def _rms(x):
    return x * lax.rsqrt(jnp.mean(x * x, axis=-1, keepdims=True) + NORM_EPS)


def _mm(a, b, *, nt=False, ta=False, out_dtype, tm, tn, a_sqrelu=False, drelu_of=None, b_blocked=False,
        out_blocked=False, a_norm=None, a_norm_bwd=None, res_norm=None, norm_bwd=None, name, ride=None):
    k, m = a.shape if ta else a.shape[::-1]
    if b_blocked:
        assert not nt and b.shape[1] == k and b.shape[2] == tn
        n = b.shape[0] * tn
    else:
        n = b.shape[0] if nt else b.shape[1]
        assert (b.shape[1] if nt else b.shape[0]) == k
    tm, tn = min(tm, m), min(tn, n)
    assert m % tm == 0 and n % tn == 0
    assert (res_norm is None and norm_bwd is None) or tn == n
    assert a_norm is None or a_norm_bwd is None
    n_in = (2 + (drelu_of is not None) + (a_norm is not None) + 2 * (a_norm_bwd is not None)
            + 2 * (res_norm is not None) + 3 * (norm_bwd is not None))

    def body(*refs):
        a_ref, b_ref = refs[0], refs[1]
        extra = list(refs[2:n_in])
        outs = list(refs[n_in:])
        o_ref = outs.pop(0)
        u_ref = extra.pop(0) if drelu_of is not None else None
        if a_norm is not None:
            wn_ref, h_ref, h_scr = extra.pop(0), outs.pop(0), outs.pop()

            @pl.when(pl.program_id(1) == 0)
            def _():
                h = (_rms(a_ref[...]) * wn_ref[...]).astype(ACT_DTYPE)
                h_scr[...] = h
                h_ref[...] = h

            av = h_scr[...]
        elif a_norm_bwd is not None:
            y_ref, wy_ref = extra.pop(0), extra.pop(0)
            dy_ref, dwy_ref, dy_scr = outs.pop(0), outs.pop(0), outs.pop()
            first_rows = pl.program_id(0) == 0

            @pl.when(pl.program_id(1) == 0)
            def _():
                yv, up = y_ref[...], a_ref[...]
                rstd = lax.rsqrt(jnp.mean(yv * yv, axis=-1, keepdims=True) + NORM_EPS)
                yhat = yv * rstd
                g = up * wy_ref[...]
                dy = (rstd * (g - yhat * jnp.mean(g * yhat, axis=-1, keepdims=True))).astype(ACT_DTYPE)
                dy_scr[...] = dy
                dy_ref[...] = dy

                @pl.when(first_rows)
                def _():
                    dwy_ref[...] = jnp.zeros_like(dwy_ref)

                dwy_ref[...] += jnp.sum(up * yhat, axis=0, keepdims=True)

            av = dy_scr[...]
        else:
            av = a_ref[...]
        if a_sqrelu:
            av = jnp.square(jnp.maximum(av.astype(F32), 0.0))
        acc = _dot(av, b_ref[...], 0 if ta else 1, 1 if nt else 0)
        if u_ref is not None:
            acc = acc * (2.0 * jnp.maximum(u_ref[...].astype(F32), 0.0))
        if norm_bwd is not None:
            x_ref, wb_ref, add_ref = extra
            dw_ref = outs[0]
            xv = x_ref[...]
            rstd = lax.rsqrt(jnp.mean(xv * xv, axis=-1, keepdims=True) + NORM_EPS)
            xhat = xv * rstd
            g = acc * wb_ref[...]
            o_ref[...] = rstd * (g - xhat * jnp.mean(g * xhat, axis=-1, keepdims=True)) + add_ref[...]

            @pl.when(pl.program_id(0) == 0)
            def _():
                dw_ref[...] = jnp.zeros_like(dw_ref)

            dw_ref[...] += jnp.sum(acc * xhat, axis=0, keepdims=True)
            return
        o_ref[...] = acc.astype(out_dtype)
        if res_norm is not None:
            res_ref, wr_ref = extra
            outs[0][...] = res_ref[...] + _rms(acc) * wr_ref[...]

    if b_blocked:
        b_spec = pl.BlockSpec((None, k, tn), lambda i, j: (j, 0, 0))
    elif nt:
        b_spec = pl.BlockSpec((tn, k), lambda i, j: (j, 0))
    else:
        b_spec = pl.BlockSpec((k, tn), lambda i, j: (0, j))
    a_spec = pl.BlockSpec((k, tm), lambda i, j: (0, i)) if ta else pl.BlockSpec((tm, k), lambda i, j: (i, 0))
    in_specs = [a_spec, b_spec]
    args = [a, b]
    if drelu_of is not None:
        in_specs.append(pl.BlockSpec((tm, tn), lambda i, j: (i, j)))
        args.append(drelu_of)
    if out_blocked:
        out_specs = [pl.BlockSpec((None, tm, tn), lambda i, j: (j, i, 0))]
        out_shape = [jax.ShapeDtypeStruct((n // tn, m, tn), out_dtype)]
    else:
        out_specs = [pl.BlockSpec((tm, tn), lambda i, j: (i, j))]
        out_shape = [jax.ShapeDtypeStruct((m, n), out_dtype)]
    scratch = []
    if a_norm is not None:
        assert not ta
        in_specs.append(pl.BlockSpec((1, k), lambda i, j: (0, 0)))
        args.append(a_norm)
        out_specs.append(pl.BlockSpec((tm, k), lambda i, j: (i, 0)))
        out_shape.append(jax.ShapeDtypeStruct((m, k), ACT_DTYPE))
        scratch.append(pltpu.VMEM((tm, k), ACT_DTYPE))
    if a_norm_bwd is not None:
        assert not ta
        in_specs += [pl.BlockSpec((tm, k), lambda i, j: (i, 0)), pl.BlockSpec((1, k), lambda i, j: (0, 0))]
        args += list(a_norm_bwd)
        out_specs += [pl.BlockSpec((tm, k), lambda i, j: (i, 0)), pl.BlockSpec((1, k), lambda i, j: (0, 0))]
        out_shape += [jax.ShapeDtypeStruct((m, k), ACT_DTYPE), jax.ShapeDtypeStruct((1, k), F32)]
        scratch.append(pltpu.VMEM((tm, k), ACT_DTYPE))
    if res_norm is not None:
        in_specs += [pl.BlockSpec((tm, n), lambda i, j: (i, 0)), pl.BlockSpec((1, n), lambda i, j: (0, 0))]
        args += list(res_norm)
        out_specs.append(pl.BlockSpec((tm, n), lambda i, j: (i, 0)))
        out_shape.append(jax.ShapeDtypeStruct((m, n), F32))
    if norm_bwd is not None:
        rows = pl.BlockSpec((tm, n), lambda i, j: (i, 0))
        in_specs += [rows, pl.BlockSpec((1, n), lambda i, j: (0, 0)), rows]
        args += list(norm_bwd)
        out_specs.append(pl.BlockSpec((1, n), lambda i, j: (0, 0)))
        out_shape.append(jax.ShapeDtypeStruct((1, n), F32))
    single = len(out_shape) == 1
    return _pcall(body, ride, grid=(m // tm, n // tn), in_specs=in_specs,
                  out_specs=out_specs[0] if single else out_specs, out_shape=out_shape[0] if single else out_shape,
                  scratch_shapes=scratch, semantics=("arbitrary", "arbitrary"), name=name)(*args)


ROW_TILE = 512
TM_FWD, TM_DX, TM_DW, TN = 2048, 1024, 1024, 512


def _norm_bwd(dy, x, w, *, out_dtype, add=None, name, ride=None):
    t, d = x.shape

    def body(*refs):
        dy_ref, x_ref, w_ref = refs[0], refs[1], refs[2]
        dx_ref, dw_ref = refs[-2], refs[-1]
        xv = x_ref[...]
        rstd = lax.rsqrt(jnp.mean(xv * xv, axis=-1, keepdims=True) + NORM_EPS)
        xhat = xv * rstd
        dyv = dy_ref[...].astype(F32)
        g = dyv * w_ref[...]
        dx = rstd * (g - xhat * jnp.mean(g * xhat, axis=-1, keepdims=True))
        if add is not None:
            dx = dx + refs[3][...]
        dx_ref[...] = dx.astype(out_dtype)

        @pl.when(pl.program_id(0) == 0)
        def _():
            dw_ref[...] = jnp.zeros_like(dw_ref)

        dw_ref[...] += jnp.sum(dyv * xhat, axis=0, keepdims=True)

    row = pl.BlockSpec((ROW_TILE, d), lambda i: (i, 0))
    vec = pl.BlockSpec((1, d), lambda i: (0, 0))
    in_specs = [row, row, vec] + ([row] if add is not None else [])
    args = [dy, x, w] + ([add] if add is not None else [])
    return _pcall(body, ride, grid=(t // ROW_TILE,), in_specs=in_specs, out_specs=[row, vec],
                  out_shape=[jax.ShapeDtypeStruct((t, d), out_dtype), jax.ShapeDtypeStruct((1, d), F32)],
                  semantics=("arbitrary",), name=name)(*args)


def _loss_fwd_bwd(y, target):
    t, d = y.shape

    def body(y_ref, t_ref, l_ref, dy_ref):
        diff = y_ref[...] - t_ref[...]
        dy_ref[...] = diff * (1.0 / d)

        @pl.when(pl.program_id(0) == 0)
        def _():
            l_ref[...] = jnp.zeros_like(l_ref)

        l_ref[...] += 0.5 * jnp.sum(jnp.mean(diff * diff, axis=-1, keepdims=True), axis=0, keepdims=True)

    row = pl.BlockSpec((ROW_TILE, d), lambda i: (i, 0))
    return pl.pallas_call(body, grid=(t // ROW_TILE,), in_specs=[row, row],
                          out_specs=[pl.BlockSpec((8, LANES), lambda i: (0, 0)), row],
                          out_shape=[jax.ShapeDtypeStruct((8, LANES), F32), jax.ShapeDtypeStruct((t, d), F32)],
                          compiler_params=_params("arbitrary"), name="loss")(y, target)


GLA_STATE = (GLA_HEADS * GLA_DV, GLA_KW)


def _gla_specs(chunk_of):
    rows = lambda width, col: pl.BlockSpec((CHUNK, width), lambda i: (chunk_of(i), col))
    const = lambda r, c: pl.BlockSpec((r, c), lambda i: (0, 0))
    return [rows(GLA_KW, 0),
            rows(GLA_KW, 1),
            rows(GROUP_WIDTH, 1),
            rows(GROUP_WIDTH, 0),
            rows(LANES, 4),
            const(LANES, GLA_KW),
            const(1, GLA_KW),
            const(1, GROUP_WIDTH)]


def _gla_chunk(q_ref, k_ref, v_ref, a_ref, wup_ref, ba_ref):
    z = _dot(a_ref[...], wup_ref[...]) + ba_ref[...]
    tri = (_iota((CHUNK, CHUNK), 1) <= _iota((CHUNK, CHUNK), 0)).astype(F32)
    cum = _dot_exact(tri, _log_sigmoid(z) * (1.0 / GLA_GATE_TAU))
    tot = cum[CHUNK - 1:CHUNK, :]
    e = jnp.exp(tot - cum)
    return (z, e, jnp.exp(tot), k_ref[...].astype(F32) * e, q_ref[...].astype(F32) * (GLA_DK ** -0.5),
            v_ref[...].astype(F32))


def _gla_head_mask():
    return _iota(GLA_STATE, 0) // GLA_DV == _iota(GLA_STATE, 1) // GLA_DK


def _gla_fwd(pmm, pel, w_up, b_a, gnorm_w, ride=None):
    t = pmm.shape[0]
    nc = t // CHUNK

    def body(q_ref, k_ref, v_ref, r_ref, a_ref, wup_ref, ba_ref, gw_ref, o_ref, st_ref, m_scr):
        @pl.when(pl.program_id(0) == 0)
        def _():
            m_scr[...] = jnp.zeros_like(m_scr)

        _, _, decay, kd, qs, vv = _gla_chunk(q_ref, k_ref, v_ref, a_ref, wup_ref, ba_ref)
        m = m_scr[...] * decay + jnp.where(_gla_head_mask(), _dot(vv, kd, 0, 0), 0.0)
        m_scr[...] = m
        st_ref[...] = m
        o = _dot(qs, m, 1, 1)
        rr = r_ref[...]
        gate = rr * jax.nn.sigmoid(rr) * gw_ref[...]
        for h in range(GLA_HEADS):
            vs = slice(h * GLA_DV, (h + 1) * GLA_DV)
            oh = o[:, vs]
            y = oh * lax.rsqrt(jnp.mean(oh * oh, axis=-1, keepdims=True) + NORM_EPS)
            o_ref[:, vs] = (y * gate[:, vs]).astype(o_ref.dtype)

    return _pcall(
        body, ride, grid=(nc,), in_specs=_gla_specs(lambda i: i),
        out_specs=[pl.BlockSpec((CHUNK, GROUP_WIDTH), lambda i: (i, 0)),
                   pl.BlockSpec((None,) + GLA_STATE, lambda i: (i, 0, 0))],
        out_shape=[jax.ShapeDtypeStruct((t, GROUP_WIDTH), ACT_DTYPE), jax.ShapeDtypeStruct((nc,) + GLA_STATE, F32)],
        scratch_shapes=[pltpu.VMEM(GLA_STATE, F32)],
        semantics=("arbitrary",), name="gla_fwd")(pmm, pmm, pmm, pel, pel, w_up, b_a, gnorm_w)


def _gla_bwd(pmm, pel, w_up, b_a, gnorm_w, states, dmix, ride=None):
    t = pmm.shape[0]
    nc = t // CHUNK
    scale = GLA_DK ** -0.5

    def body(q_ref, k_ref, v_ref, r_ref, a_ref, wup_ref, ba_ref, gw_ref, st_ref, prev_ref, do_ref,
             dq_ref, dk_ref, dv_ref, dr_ref, da_ref, dwup_ref, dba_ref, dgw_ref, dm_scr):
        step = pl.program_id(0)

        @pl.when(step == 0)
        def _():
            dm_scr[...] = jnp.zeros_like(dm_scr)
            dwup_ref[...] = jnp.zeros_like(dwup_ref)
            dba_ref[...] = jnp.zeros_like(dba_ref)
            dgw_ref[...] = jnp.zeros_like(dgw_ref)

        z, e, decay, kd, qs, vv = _gla_chunk(q_ref, k_ref, v_ref, a_ref, wup_ref, ba_ref)
        m = st_ref[...]
        m_prev = prev_ref[...] * (step < nc - 1).astype(F32)
        rr, dout, gw = r_ref[...], do_ref[...], gw_ref[...]
        sig = jax.nn.sigmoid(rr)
        silu = rr * sig
        dsilu = sig * (1.0 + rr * (1.0 - sig))
        o = _dot(qs, m, 1, 1)
        d_o, dgw = [], []
        for h in range(GLA_HEADS):
            vs = slice(h * GLA_DV, (h + 1) * GLA_DV)
            oh, dg = o[:, vs], dout[:, vs]
            rstd = lax.rsqrt(jnp.mean(oh * oh, axis=-1, keepdims=True) + NORM_EPS)
            y = oh * rstd
            dgw.append(jnp.sum(dg * y * silu[:, vs], axis=0, keepdims=True))
            dr_ref[:, vs] = (dg * y * gw[:, vs] * dsilu[:, vs]).astype(dr_ref.dtype)
            dy = dg * gw[:, vs] * silu[:, vs]
            d_o.append(rstd * (dy - y * jnp.mean(dy * y, axis=-1, keepdims=True)))
        d_o = jnp.concatenate(d_o, axis=1)
        dgw_ref[...] += jnp.concatenate(dgw, axis=1)
        dq_ref[...] = (_dot(d_o, m) * scale).astype(dq_ref.dtype)
        dm = dm_scr[...] + jnp.where(_gla_head_mask(), _dot(d_o, qs, 0, 0), 0.0)
        dv_ref[...] = _dot(kd, dm, 1, 1).astype(dv_ref.dtype)
        dkd = _dot(vv, dm)
        dk_ref[...] = (dkd * e).astype(dk_ref.dtype)
        dm_scr[...] = dm * decay
        tri_strict = (_iota((CHUNK, CHUNK), 1) < _iota((CHUNK, CHUNK), 0)).astype(F32)
        dla = jnp.sum(dm * m_prev, axis=0, keepdims=True) * decay + _dot_exact(tri_strict, dkd * kd)
        dz = dla * jax.nn.sigmoid(-z) * (1.0 / GLA_GATE_TAU)
        da_ref[...] = _dot(dz, wup_ref[...], 1, 1).astype(da_ref.dtype)
        dwup_ref[...] += _dot(a_ref[...], dz, 0, 0)
        dba_ref[...] += jnp.sum(dz, axis=0, keepdims=True)

    chunk_of = lambda i: nc - 1 - i
    in_specs = _gla_specs(chunk_of) + [
        pl.BlockSpec((None,) + GLA_STATE, lambda i: (chunk_of(i), 0, 0)),
        pl.BlockSpec((None,) + GLA_STATE, lambda i: (jnp.maximum(chunk_of(i) - 1, 0), 0, 0)),
        pl.BlockSpec((CHUNK, GROUP_WIDTH), lambda i: (chunk_of(i), 0))]
    rows = lambda width: pl.BlockSpec((CHUNK, width), lambda i: (chunk_of(i), 0))
    const = lambda r, c: pl.BlockSpec((r, c), lambda i: (0, 0))
    return _pcall(
        body, ride, grid=(nc,), in_specs=in_specs,
        out_specs=[rows(GLA_KW), rows(GLA_KW), rows(GROUP_WIDTH), rows(GROUP_WIDTH), rows(LANES),
                   const(LANES, GLA_KW), const(1, GLA_KW), const(1, GROUP_WIDTH)],
        out_shape=[jax.ShapeDtypeStruct((t, GLA_KW), ACT_DTYPE), jax.ShapeDtypeStruct((t, GLA_KW), ACT_DTYPE),
                   jax.ShapeDtypeStruct((t, GROUP_WIDTH), ACT_DTYPE), jax.ShapeDtypeStruct((t, GROUP_WIDTH), ACT_DTYPE),
                   jax.ShapeDtypeStruct((t, LANES), ACT_DTYPE), jax.ShapeDtypeStruct((LANES, GLA_KW), F32),
                   jax.ShapeDtypeStruct((1, GLA_KW), F32), jax.ShapeDtypeStruct((1, GROUP_WIDTH), F32)],
        scratch_shapes=[pltpu.VMEM(GLA_STATE, F32)],
        semantics=("arbitrary",), name="gla_bwd")(
            pmm, pmm, pmm, pel, pel, w_up, b_a, gnorm_w, states, states, dmix)


CUM_BLOCK = 256


def _fox_gate_fwd(pel, b_f):
    t = pel.shape[0]
    nb = t // CUM_BLOCK

    def body(f_ref, b_ref, cum_ref, cum_t_ref):
        tri = (_iota((CUM_BLOCK, CUM_BLOCK), 1) <= _iota((CUM_BLOCK, CUM_BLOCK), 0)).astype(F32)
        carry = jnp.zeros((1, LANES), F32)
        for blk in range(nb):
            rows = slice(blk * CUM_BLOCK, (blk + 1) * CUM_BLOCK)
            cum = _dot_exact(tri, _log_sigmoid(f_ref[rows, :] + b_ref[...])) + carry
            cum_ref[rows, :] = cum
            cum_t_ref[blk] = cum.T[:ATT_HEADS, :]
            carry = cum[CUM_BLOCK - 1:CUM_BLOCK, :]

    return pl.pallas_call(
        body, grid=(1,),
        in_specs=[pl.BlockSpec((t, LANES), lambda i: (0, 5)), pl.BlockSpec((1, LANES), lambda i: (0, 0))],
        out_specs=[pl.BlockSpec((t, LANES), lambda i: (0, 0)),
                   pl.BlockSpec((nb, ATT_HEADS, CUM_BLOCK), lambda i: (0, 0, 0))],
        out_shape=[jax.ShapeDtypeStruct((t, LANES), F32), jax.ShapeDtypeStruct((nb, ATT_HEADS, CUM_BLOCK), F32)],
        compiler_params=_params("arbitrary"), name="fox_gate_fwd")(pel, b_f)


def _fox_gate_bwd(pel, b_f, dcum_t, dcum_q):
    t = pel.shape[0]
    nb = t // CUM_BLOCK

    def body(f_ref, b_ref, dct_ref, dcq_ref, df_ref, db_ref):
        tri_up = (_iota((CUM_BLOCK, CUM_BLOCK), 1) >= _iota((CUM_BLOCK, CUM_BLOCK), 0)).astype(F32)
        carry = jnp.zeros((1, LANES), F32)
        db = jnp.zeros((1, LANES), F32)
        for blk in reversed(range(nb)):
            rows = slice(blk * CUM_BLOCK, (blk + 1) * CUM_BLOCK)
            dls = _dot_exact(tri_up, dct_ref[blk].T + dcq_ref[rows, :]) + carry
            carry = dls[0:1, :]
            df = dls * jax.nn.sigmoid(-(f_ref[rows, :] + b_ref[...]))
            df_ref[rows, :] = df.astype(df_ref.dtype)
            db = db + jnp.sum(df, axis=0, keepdims=True)
        db_ref[...] = db

    return pl.pallas_call(
        body, grid=(1,),
        in_specs=[pl.BlockSpec((t, LANES), lambda i: (0, 5)), pl.BlockSpec((1, LANES), lambda i: (0, 0)),
                  pl.BlockSpec((nb, LANES, CUM_BLOCK), lambda i: (0, 0, 0)), pl.BlockSpec((t, LANES), lambda i: (0, 0))],
        out_specs=[pl.BlockSpec((t, LANES), lambda i: (0, 0)), pl.BlockSpec((1, LANES), lambda i: (0, 0))],
        out_shape=[jax.ShapeDtypeStruct((t, LANES), ACT_DTYPE), jax.ShapeDtypeStruct((1, LANES), F32)],
        compiler_params=_params("arbitrary"), name="fox_gate_bwd")(pel, b_f, dcum_t, dcum_q)


FOX_Q_BLOCK = 256


assert FOX_Q_BLOCK == CUM_BLOCK


def _fox_scores(q_ref, k_ref, cum_ref, cum_t_ref, h, i):
    hs = slice(h * HEAD_DIM, (h + 1) * HEAD_DIM)
    nb = cum_t_ref.shape[0]
    key_gate = jnp.concatenate([cum_t_ref[kb, h:h + 1, :] for kb in range(nb)], axis=1)
    s = _dot(q_ref[:, hs], k_ref[:, hs], 1, 1) * (HEAD_DIM ** -0.5) + (cum_ref[:, h:h + 1] - key_gate)
    shape = (FOX_Q_BLOCK, nb * FOX_Q_BLOCK)
    return jnp.where(_iota(shape, 1) <= i * FOX_Q_BLOCK + _iota(shape, 0), s, NEG)


def _fox_specs(t):
    bq, nb = FOX_Q_BLOCK, t // FOX_Q_BLOCK
    return [pl.BlockSpec((bq, GROUP_WIDTH), lambda i: (i, 2)), pl.BlockSpec((t, GROUP_WIDTH), lambda i: (0, 3)),
            pl.BlockSpec((t, GROUP_WIDTH), lambda i: (0, 4)), pl.BlockSpec((bq, LANES), lambda i: (i, 0)),
            pl.BlockSpec((nb, ATT_HEADS, bq), lambda i: (0, 0, 0))]


def _fox_fwd(pmm, cum, cum_t, ride=None):
    t = pmm.shape[0]
    bq = FOX_Q_BLOCK

    def body(q_ref, k_ref, v_ref, cum_ref, cum_t_ref, o_ref, lse_ref):
        i = pl.program_id(0)
        lse_ref[...] = jnp.zeros_like(lse_ref)
        for h in range(ATT_HEADS):
            hs = slice(h * HEAD_DIM, (h + 1) * HEAD_DIM)
            s = _fox_scores(q_ref, k_ref, cum_ref, cum_t_ref, h, i)
            m = jnp.max(s, axis=-1, keepdims=True)
            p = jnp.exp(s - m)
            l = jnp.sum(p, axis=-1, keepdims=True)
            o_ref[:, hs] = (_dot(p, v_ref[:, hs]) / l).astype(o_ref.dtype)
            lse_ref[:, h:h + 1] = m + jnp.log(l)

    return _pcall(
        body, ride, grid=(t // bq,), in_specs=_fox_specs(t),
        out_specs=[pl.BlockSpec((bq, GROUP_WIDTH), lambda i: (i, 0)), pl.BlockSpec((bq, LANES), lambda i: (i, 0))],
        out_shape=[jax.ShapeDtypeStruct((t, GROUP_WIDTH), ACT_DTYPE), jax.ShapeDtypeStruct((t, LANES), F32)],
        semantics=("parallel",), name="fox_fwd")(pmm, pmm, pmm, cum, cum_t)


def _fox_bwd(pmm, cum, cum_t, lse, dmix, ride=None):
    t = pmm.shape[0]
    bq, nb = FOX_Q_BLOCK, t // FOX_Q_BLOCK
    scale = HEAD_DIM ** -0.5

    def body(q_ref, k_ref, v_ref, cum_ref, cum_t_ref, lse_ref, do_ref, dq_ref, dk_ref, dv_ref, dct_ref, dcq_ref):
        i = pl.program_id(0)

        @pl.when(i == 0)
        def _():
            dk_ref[...] = jnp.zeros_like(dk_ref)
            dv_ref[...] = jnp.zeros_like(dv_ref)
            dct_ref[...] = jnp.zeros_like(dct_ref)

        dcq_ref[...] = jnp.zeros_like(dcq_ref)
        for h in range(ATT_HEADS):
            hs = slice(h * HEAD_DIM, (h + 1) * HEAD_DIM)
            s = _fox_scores(q_ref, k_ref, cum_ref, cum_t_ref, h, i)
            p = jnp.exp(s - lse_ref[:, h:h + 1])
            do = do_ref[:, hs]
            dp = _dot(do, v_ref[:, hs], 1, 1)
            ds = p * (dp - jnp.sum(p * dp, axis=-1, keepdims=True))
            dq_ref[:, hs] = (_dot(ds, k_ref[:, hs]) * scale).astype(dq_ref.dtype)
            dk_ref[:, hs] += _dot(ds, q_ref[:, hs], 0, 0) * scale
            dv_ref[:, hs] += _dot(p, do, 0, 0)
            key_side = -jnp.sum(ds, axis=0, keepdims=True)
            for kb in range(nb):
                dct_ref[kb, h:h + 1, :] += key_side[:, kb * bq:(kb + 1) * bq]
            dcq_ref[:, h:h + 1] = jnp.sum(ds, axis=1, keepdims=True)

    whole = pl.BlockSpec((t, GROUP_WIDTH), lambda i: (0, 0))
    return _pcall(
        body, ride, grid=(t // bq,),
        in_specs=_fox_specs(t) + [pl.BlockSpec((bq, LANES), lambda i: (i, 0)),
                                  pl.BlockSpec((bq, GROUP_WIDTH), lambda i: (i, 1))],
        out_specs=[pl.BlockSpec((bq, GROUP_WIDTH), lambda i: (i, 0)), whole, whole,
                   pl.BlockSpec((nb, LANES, bq), lambda i: (0, 0, 0)), pl.BlockSpec((bq, LANES), lambda i: (i, 0))],
        out_shape=[jax.ShapeDtypeStruct((t, GROUP_WIDTH), ACT_DTYPE), jax.ShapeDtypeStruct((t, GROUP_WIDTH), F32),
                   jax.ShapeDtypeStruct((t, GROUP_WIDTH), F32), jax.ShapeDtypeStruct((nb, LANES, bq), F32),
                   jax.ShapeDtypeStruct((t, LANES), F32)],
        semantics=("arbitrary",), name="fox_bwd")(pmm, pmm, pmm, cum, cum_t, lse, dmix)


CA_Q_BLOCK = 4 * CHUNK
CA_WINDOW = CA_Q_BLOCK + CA_LEFT
CA_BASE = 1024


def _ca_bias_base(rel_bias):
    n = rel_bias.shape[0]
    flat = CA_Q_BLOCK + CA_LEFT - REL_CLIP
    tail = CA_BASE - flat - (2 * REL_CLIP + 1)
    return jnp.concatenate([jnp.broadcast_to(rel_bias[:, 2 * REL_CLIP:], (n, flat)), rel_bias[:, ::-1],
                            jnp.broadcast_to(rel_bias[:, :1], (n, tail))], axis=1)


def _ca_bias_base_grad(dbase):
    flat = CA_Q_BLOCK + CA_LEFT - REL_CLIP
    mid = dbase[:, flat:flat + 2 * REL_CLIP + 1][:, ::-1]
    lo = jnp.sum(dbase[:, flat + 2 * REL_CLIP + 1:], axis=1, keepdims=True)
    hi = jnp.sum(dbase[:, :flat], axis=1, keepdims=True)
    pad = jnp.zeros((dbase.shape[0], 2 * REL_CLIP - 1), F32)
    return mid + jnp.concatenate([lo, pad, hi], axis=1)


def _ca_mask(i):
    r, j = _iota((CA_Q_BLOCK, CA_WINDOW), 0), _iota((CA_Q_BLOCK, CA_WINDOW), 1)
    rc, jc = r // CHUNK, j // CHUNK
    return (jc >= rc) & (jc <= rc + CA_LEFT // CHUNK) & (i * CA_Q_BLOCK + j >= CA_LEFT)


def _ca_fill_bias(i, base_ref, bias_scr):
    @pl.when(i == 0)
    def _():
        for h in range(ATT_HEADS):
            rows = jnp.broadcast_to(base_ref[h:h + 1, :], (CA_Q_BLOCK, CA_BASE))
            bias_scr[h] = pltpu.roll(rows, CA_BASE - CA_Q_BLOCK, 1, stride=1, stride_axis=0)[:, :CA_WINDOW]


def _ca_scores(q_ref, kp_ref, bias_scr, win, h, mask):
    hs = slice(h * HEAD_DIM, (h + 1) * HEAD_DIM)
    s = _dot(q_ref[:, hs], kp_ref[win, hs], 1, 1) * (HEAD_DIM ** -0.5)
    return jnp.where(mask, s + bias_scr[h], NEG)


CA_BIAS_SCRATCH = pltpu.VMEM((ATT_HEADS, CA_Q_BLOCK, CA_WINDOW), F32)


def _ca_fwd(pmm, kp, vp, base, ride=None):
    t = pmm.shape[0]

    def body(q_ref, kp_ref, vp_ref, base_ref, o_ref, lse_ref, bias_scr):
        i = pl.program_id(0)
        _ca_fill_bias(i, base_ref, bias_scr)
        win = pl.ds(pl.multiple_of(i * CA_Q_BLOCK, CA_Q_BLOCK), CA_WINDOW)
        mask = _ca_mask(i)
        lse_ref[...] = jnp.zeros_like(lse_ref)
        for h in range(ATT_HEADS):
            hs = slice(h * HEAD_DIM, (h + 1) * HEAD_DIM)
            s = _ca_scores(q_ref, kp_ref, bias_scr, win, h, mask)
            m = jnp.max(s, axis=-1, keepdims=True)
            p = jnp.exp(s - m)
            l = jnp.sum(p, axis=-1, keepdims=True)
            o_ref[:, hs] = (_dot(p, vp_ref[win, hs]) / l).astype(o_ref.dtype)
            lse_ref[:, h:h + 1] = m + jnp.log(l)

    padded = pl.BlockSpec((t + CA_LEFT, GROUP_WIDTH), lambda i: (0, 0))
    return _pcall(
        body, ride, grid=(t // CA_Q_BLOCK,),
        in_specs=[pl.BlockSpec((CA_Q_BLOCK, GROUP_WIDTH), lambda i: (i, 0)), padded, padded,
                  pl.BlockSpec((ATT_HEADS, CA_BASE), lambda i: (0, 0))],
        out_specs=[pl.BlockSpec((CA_Q_BLOCK, GROUP_WIDTH), lambda i: (i, 0)),
                   pl.BlockSpec((CA_Q_BLOCK, LANES), lambda i: (i, 0))],
        out_shape=[jax.ShapeDtypeStruct((t, GROUP_WIDTH), ACT_DTYPE), jax.ShapeDtypeStruct((t, LANES), F32)],
        scratch_shapes=[CA_BIAS_SCRATCH], semantics=("arbitrary",), name="ca_fwd")(pmm, kp, vp, base)


def _ca_bwd(pmm, kp, vp, base, lse, dmix, ride=None):
    t = pmm.shape[0]
    scale = HEAD_DIM ** -0.5

    def body(q_ref, kp_ref, vp_ref, base_ref, lse_ref, do_ref, dq_ref, dkp_ref, dvp_ref, dbase_ref, bias_scr):
        i = pl.program_id(0)
        _ca_fill_bias(i, base_ref, bias_scr)

        @pl.when(i == 0)
        def _():
            dkp_ref[...] = jnp.zeros_like(dkp_ref)
            dvp_ref[...] = jnp.zeros_like(dvp_ref)
            dbase_ref[...] = jnp.zeros_like(dbase_ref)

        win = pl.ds(pl.multiple_of(i * CA_Q_BLOCK, CA_Q_BLOCK), CA_WINDOW)
        mask = _ca_mask(i)
        flip = (_iota((CA_Q_BLOCK, CA_Q_BLOCK), 0) + _iota((CA_Q_BLOCK, CA_Q_BLOCK), 1) == CA_Q_BLOCK - 1).astype(F32)
        for h in range(ATT_HEADS):
            hs = slice(h * HEAD_DIM, (h + 1) * HEAD_DIM)
            s = _ca_scores(q_ref, kp_ref, bias_scr, win, h, mask)
            p = jnp.exp(s - lse_ref[:, h:h + 1])
            do = do_ref[:, hs]
            dp = _dot(do, vp_ref[win, hs], 1, 1)
            ds = p * (dp - jnp.sum(p * dp, axis=-1, keepdims=True))
            dq_ref[:, hs] = (_dot(ds, kp_ref[win, hs]) * scale).astype(dq_ref.dtype)
            dkp_ref[win, hs] += _dot(ds, q_ref[:, hs], 0, 0) * scale
            dvp_ref[win, hs] += _dot(p, do, 0, 0)
            rev = jnp.concatenate([_dot(flip, ds), jnp.zeros((CA_Q_BLOCK, CA_BASE - CA_WINDOW), F32)], axis=1)
            lined = pltpu.roll(rev, 1, 1, stride=1, stride_axis=0)
            dbase_ref[h:h + 1, :] += jnp.sum(lined, axis=0, keepdims=True)

    padded = pl.BlockSpec((t + CA_LEFT, GROUP_WIDTH), lambda i: (0, 0))
    return _pcall(
        body, ride, grid=(t // CA_Q_BLOCK,),
        in_specs=[pl.BlockSpec((CA_Q_BLOCK, GROUP_WIDTH), lambda i: (i, 0)), padded, padded,
                  pl.BlockSpec((ATT_HEADS, CA_BASE), lambda i: (0, 0)),
                  pl.BlockSpec((CA_Q_BLOCK, LANES), lambda i: (i, 0)),
                  pl.BlockSpec((CA_Q_BLOCK, GROUP_WIDTH), lambda i: (i, 0))],
        out_specs=[pl.BlockSpec((CA_Q_BLOCK, GROUP_WIDTH), lambda i: (i, 0)), padded, padded,
                   pl.BlockSpec((ATT_HEADS, CA_BASE), lambda i: (0, 0))],
        out_shape=[jax.ShapeDtypeStruct((t, GROUP_WIDTH), ACT_DTYPE),
                   jax.ShapeDtypeStruct((t + CA_LEFT, GROUP_WIDTH), F32),
                   jax.ShapeDtypeStruct((t + CA_LEFT, GROUP_WIDTH), F32),
                   jax.ShapeDtypeStruct((ATT_HEADS, CA_BASE), F32)],
        scratch_shapes=[CA_BIAS_SCRATCH], semantics=("arbitrary",), name="ca_bwd")(pmm, kp, vp, base, lse, dmix)


GELU_C = 0.7978845608028654
GELU_A = 0.044715


def _shift_down(v, k, fill):
    return jnp.where(_iota(v.shape, 0) >= k, pltpu.roll(v, k, 0), fill)


def _shift_up(v, k, fill):
    t = v.shape[0]
    return jnp.where(_iota(v.shape, 0) < t - k, pltpu.roll(v, t - k, 0), fill)


def _linear_scan(a, b, shift):
    k = 1
    while k < a.shape[0]:
        b = a * shift(b, k, 0.0) + b
        a = a * shift(a, k, 1.0)
        k *= 2
    return b


def _neg_expm1(y):
    series = -y * (1.0 + y * (0.5 + y * (1.0 / 6.0 + y * (1.0 / 24.0 + y * (1.0 / 120.0)))))
    return jnp.where(y > -0.1, series, 1.0 - jnp.exp(y))


def _lru_forward(x, g_in, cw, cb, wa, ba, wx, bx, lam):
    xs = [_shift_down(x, CONV_WIDTH - 1 - j, 0.0) for j in range(CONV_WIDTH - 1)] + [x]
    xc = cb + sum(cw[j:j + 1, :] * xs[j] for j in range(CONV_WIDTH))
    r = jax.nn.sigmoid(_dot(xc, wa) + ba)
    i = jax.nn.sigmoid(_dot(xc, wx) + bx)
    lsl = _log_sigmoid(lam)
    la = LRU_C * r * lsl
    a = jnp.exp(la)
    s = jnp.sqrt(_neg_expm1(2.0 * la))
    h = _linear_scan(a, s * (i * xc), _shift_down)
    u = GELU_C * (g_in + GELU_A * g_in * g_in * g_in)
    th = jnp.tanh(u)
    gelu = 0.5 * g_in * (1.0 + th)
    return xs, xc, r, i, lsl, a, s, h, th, gelu


def _lru_specs(t):
    col = lambda off: pl.BlockSpec((t, LANES), lambda j: (0, j + off))
    vec = pl.BlockSpec((1, LANES), lambda j: (0, j))
    mat = pl.BlockSpec((None, LANES, LANES), lambda j: (j, 0, 0))
    return [col(0), col(GROUP_WIDTH // LANES), pl.BlockSpec((CONV_WIDTH, LANES), lambda j: (0, j)),
            vec, mat, vec, mat, vec, vec]


def _lru_fwd(pel, conv_w, conv_b, wa, ba, wx, bx, lam, ride=None):
    t = pel.shape[0]

    def body(g_ref, x_ref, cw_ref, cb_ref, wa_ref, ba_ref, wx_ref, bx_ref, lam_ref, o_ref):
        res = _lru_forward(x_ref[...], g_ref[...], cw_ref[...], cb_ref[...], wa_ref[...], ba_ref[...],
                           wx_ref[...], bx_ref[...], lam_ref[...])
        o_ref[...] = (res[7] * res[9]).astype(o_ref.dtype)

    return _pcall(
        body, ride, grid=(GROUP_WIDTH // LANES,), in_specs=_lru_specs(t),
        out_specs=pl.BlockSpec((t, LANES), lambda j: (0, j)),
        out_shape=jax.ShapeDtypeStruct((t, GROUP_WIDTH), ACT_DTYPE),
        semantics=("parallel",), name="lru_fwd")(pel, pel, conv_w, conv_b, wa, ba, wx, bx, lam)


def _lru_bwd(pel, conv_w, conv_b, wa, ba, wx, bx, lam, dmix, ride=None):
    t = pel.shape[0]

    def body(g_ref, x_ref, cw_ref, cb_ref, wa_ref, ba_ref, wx_ref, bx_ref, lam_ref, do_ref,
             dg_ref, dx_ref, dcw_ref, dcb_ref, dwa_ref, dba_ref, dwx_ref, dbx_ref, dlam_ref):
        g_in, cw, lam = g_ref[...], cw_ref[...], lam_ref[...]
        xs, xc, r, i, lsl, a, s, h, th, gelu = _lru_forward(
            x_ref[...], g_in, cw, cb_ref[...], wa_ref[...], ba_ref[...], wx_ref[...], bx_ref[...], lam)
        dout = do_ref[...]
        dgelu = 0.5 * (1.0 + th) + 0.5 * g_in * (1.0 - th * th) * GELU_C * (1.0 + 3.0 * GELU_A * g_in * g_in)
        dg_ref[...] = (dout * h * dgelu).astype(dg_ref.dtype)
        gsum = _linear_scan(_shift_up(a, 1, 0.0), dout * gelu, _shift_up)
        da = gsum * _shift_down(h, 1, 0.0)
        di = gsum * s * xc
        dla = da * a - gsum * (i * xc) * (a * a / s)
        dlam_ref[...] = jnp.sum(dla * (LRU_C * r), axis=0, keepdims=True) * jax.nn.sigmoid(-lam)
        dpr = dla * (LRU_C * lsl) * r * (1.0 - r)
        dpi = di * i * (1.0 - i)
        dxc = gsum * s * i + _dot(dpr, wa_ref[...], 1, 1) + _dot(dpi, wx_ref[...], 1, 1)
        xct = xc.T
        dwa_ref[...] = _dot(xct, dpr)
        dwx_ref[...] = _dot(xct, dpi)
        dba_ref[...] = jnp.sum(dpr, axis=0, keepdims=True)
        dbx_ref[...] = jnp.sum(dpi, axis=0, keepdims=True)
        dcb_ref[...] = jnp.sum(dxc, axis=0, keepdims=True)
        for j in range(CONV_WIDTH):
            dcw_ref[j:j + 1, :] = jnp.sum(dxc * xs[j], axis=0, keepdims=True)
        dx = cw[CONV_WIDTH - 1:CONV_WIDTH, :] * dxc
        for j in range(CONV_WIDTH - 1):
            dx = dx + cw[j:j + 1, :] * _shift_up(dxc, CONV_WIDTH - 1 - j, 0.0)
        dx_ref[...] = dx.astype(dx_ref.dtype)

    col = pl.BlockSpec((t, LANES), lambda j: (0, j))
    vec = pl.BlockSpec((1, LANES), lambda j: (0, j))
    mat = pl.BlockSpec((None, LANES, LANES), lambda j: (j, 0, 0))
    nb = GROUP_WIDTH // LANES
    vshape = jax.ShapeDtypeStruct((1, GROUP_WIDTH), F32)
    mshape = jax.ShapeDtypeStruct((nb, LANES, LANES), F32)
    return _pcall(
        body, ride, grid=(nb,),
        in_specs=_lru_specs(t) + [pl.BlockSpec((t, LANES), lambda j: (0, j + nb))],
        out_specs=[col, col, pl.BlockSpec((CONV_WIDTH, LANES), lambda j: (0, j)), vec, mat, vec, mat, vec, vec],
        out_shape=[jax.ShapeDtypeStruct((t, GROUP_WIDTH), ACT_DTYPE), jax.ShapeDtypeStruct((t, GROUP_WIDTH), ACT_DTYPE),
                   jax.ShapeDtypeStruct((CONV_WIDTH, GROUP_WIDTH), F32), vshape, mshape, vshape, mshape, vshape, vshape],
        semantics=("parallel",), name="lru_bwd")(
            pel, pel, conv_w, conv_b, wa, ba, wx, bx, lam, dmix)


def _block_diag_pairs(w):
    z = jnp.zeros((LRU_BLOCK_DIM, LRU_BLOCK_DIM), w.dtype)
    return jnp.stack([jnp.block([[w[2 * j], z], [z, w[2 * j + 1]]]) for j in range(w.shape[0] // 2)])


def _block_diag_pairs_grad(dw):
    b = LRU_BLOCK_DIM
    return jnp.stack([dw[n // 2, (n % 2) * b:(n % 2 + 1) * b, (n % 2) * b:(n % 2 + 1) * b] for n in range(2 * dw.shape[0])])


def _row_tile(r):
    return ROW_TILE if r % ROW_TILE == 0 else r


def _pair_sum(g, got, place, name):
    _, r, c = g.shape
    tile = r

    def body(place_ref, a_ref, b_ref, o_ref):
        o_ref[...] = (a_ref[...].astype(F32) + b_ref[...].astype(F32)).astype(o_ref.dtype)

    blk = pl.BlockSpec((1, tile, c), lambda k, i, place_ref: (k, i, 0))
    return pl.pallas_call(
        body,
        grid_spec=pltpu.PrefetchScalarGridSpec(
            num_scalar_prefetch=1, grid=(N_CHIPS, r // tile),
            in_specs=[pl.BlockSpec((1, tile, c), lambda k, i, place_ref: (2 * k + place_ref[0], i, 0)), blk],
            out_specs=blk),
        out_shape=jax.ShapeDtypeStruct(got.shape, got.dtype),
        compiler_params=_params("parallel", "parallel"), name=name)(place, g, got)


def _adamw_update(g, w_ref, m_ref, v_ref, g_ref, d_ref, nm_ref, nv_ref):
    nm = ADAM_B1 * m_ref[...] + (1.0 - ADAM_B1) * g
    nv = ADAM_B2 * v_ref[...] + (1.0 - ADAM_B2) * jnp.square(g)
    m_hat = nm / (1.0 - ADAM_B1 ** ADAM_STEP)
    v_hat = nv / (1.0 - ADAM_B2 ** ADAM_STEP)
    g_ref[...] = g
    d_ref[...] = -ADAM_LR * (m_hat / (jnp.sqrt(v_hat) + ADAM_EPS) + ADAM_WD * w_ref[...])
    nm_ref[...] = nm
    nv_ref[...] = nv


def _adamw_sharded(parts, w, m, v, place, name):
    n_layers, r, c = w.shape
    tile = _row_tile(r)
    nb = r // tile

    def body(place_ref, *refs):
        layer = pl.program_id(0)
        g = None
        for l in range(n_layers):
            s_ref, r_ref = refs[2 * l], refs[2 * l + 1]
            g_l = s_ref[0].astype(F32) + r_ref[0].astype(F32) + r_ref[1].astype(F32) + r_ref[2].astype(F32)
            g = g_l if g is None else jnp.where(layer == l, g_l, g)
        _adamw_update(g, *refs[2 * n_layers:])

    def part_specs(l):
        rows = lambda q, i: jnp.where(q < l, 0, jnp.where(q > l, nb - 1, i))
        return [pl.BlockSpec((1, tile, c), lambda q, i, place_ref: (place_ref[1], rows(q, i), 0)),
                pl.BlockSpec((3, tile, c), lambda q, i, place_ref: (0, rows(q, i), 0))]

    in_specs, args = [], []
    for l, (s, recv) in enumerate(parts):
        in_specs += part_specs(l)
        args += [s, recv]
    blk = pl.BlockSpec((None, tile, c), lambda q, i, place_ref: (q, i, 0))
    out = jax.ShapeDtypeStruct((n_layers, r, c), F32)
    return pl.pallas_call(
        body,
        grid_spec=pltpu.PrefetchScalarGridSpec(
            num_scalar_prefetch=1, grid=(n_layers, nb), in_specs=in_specs + [blk, blk, blk],
            out_specs=[blk, blk, blk, blk]),
        out_shape=[out, out, out, out], compiler_params=_params("arbitrary", "arbitrary"), name=name)(
            place, *args, w, m, v)


def _adamw_small(repl_parts, vec_parts, w, m, v, place):
    n_r, n = len(repl_parts), len(w)
    shapes = [a.shape for a in w]

    def body(place_ref, *refs):
        parts, rest = refs[:n], refs[n:]
        for k in range(n):
            take = (lambda p: parts[k][p]) if k < n_r else (lambda p: parts[k][p, 0])
            g = take(0)
            for p in range(1, N_DEV):
                g = g + take(p)
            _adamw_update(g, rest[k], rest[n + k], rest[2 * n + k], *rest[3 * n + 4 * k:3 * n + 4 * k + 4])

    def whole(shape):
        return pl.BlockSpec(shape, lambda i, place_ref: (0,) * len(shape))

    def mine(shard):
        return pl.BlockSpec((N_DEV, 1) + shard, lambda i, place_ref: (0, place_ref[2]) + (0,) * len(shard))

    in_specs = [whole(a.shape) for a in repl_parts] + [mine(s) for s in shapes[n_r:]] + [whole(s) for s in shapes] * 3
    outs = pl.pallas_call(
        body,
        grid_spec=pltpu.PrefetchScalarGridSpec(
            num_scalar_prefetch=1, grid=(1,), in_specs=in_specs,
            out_specs=[whole(s) for s in shapes for _ in range(4)]),
        out_shape=[jax.ShapeDtypeStruct(s, F32) for s in shapes for _ in range(4)],
        compiler_params=_params("arbitrary"), name="adamw_small")(place, *repl_parts, *vec_parts, *w, *m, *v)
    return [outs[4 * k:4 * k + 4] for k in range(n)]


SHARDED = {"norm_w": 2, "w_in_even": 2, "gla_w_a_up": 2, "w_out_even": 1, "w_in_odd": 2, "conv_w": 2, "conv_b": 1,
           "lru_b_a": 1, "lru_b_x": 1, "lru_lambda": 1, "w_out_odd": 1, "w_mlp_up": 2, "w_mlp_down": 1}
REPLICATED = ["gla_b_a", "gla_norm_w", "fox_b_f", "rel_bias", "lru_w_a", "lru_w_x"]
WEIGHTS = ["norm_w", "w_in_even", "gla_w_a_up", "gla_b_a", "gla_norm_w", "fox_b_f", "w_out_even", "w_in_odd",
           "rel_bias", "conv_w", "conv_b", "lru_w_a", "lru_b_a", "lru_w_x", "lru_b_x", "lru_lambda", "w_out_odd",
           "w_mlp_up", "w_mlp_down"]
MATRICES = ("w_in_even", "w_out_even", "w_in_odd", "w_out_odd", "w_mlp_up", "w_mlp_down")
TRANSPOSED = ("w_in_even", "w_in_odd")
VECTORS = tuple(n for n in SHARDED if n not in MATRICES)
MATRIX_BLOCKS = (("w_in_even", 0), ("w_out_even", 0), ("w_in_odd", 0), ("w_out_odd", 0),
                 ("w_mlp_up", 0), ("w_mlp_up", 1), ("w_mlp_down", 0), ("w_mlp_down", 1))


def _join_shards(blocks, axis):
    moved = jnp.moveaxis(blocks, 0, axis)
    shape = moved.shape
    return moved.reshape(shape[:axis] + (shape[axis] * shape[axis + 1],) + shape[axis + 2:])


def _split_shards(full, axis):
    shape = full.shape
    cut = full.reshape(shape[:axis] + (N_DEV, shape[axis] // N_DEV) + shape[axis + 1:])
    return jnp.moveaxis(cut, axis, 0)


EVEN_SPLITS = (0, 256, 512, 1024, 1536, 1552, 2064, 2576, 3088, 3096)


def _even_in_split(wt):
    c = [wt[EVEN_SPLITS[k]:EVEN_SPLITS[k + 1]] for k in range(9)]
    gq, gk, gv, gr, ga, fq, fk, fv, ff = c
    padrows = lambda a: jnp.pad(a, ((0, LANES - a.shape[0]), (0, 0)))
    return jnp.concatenate([gq, gk, gv, fq, fk, fv], axis=0), jnp.concatenate([gr, padrows(ga), padrows(ff)], axis=0)


def _even_in_merge(dmm, dele):
    return jnp.concatenate([dmm[:1024], dele[:512], dele[512:512 + GLA_RANK], dmm[1024:2560],
                            dele[640:640 + ATT_HEADS]], axis=0)


def _forward_backward(x, target, shard, vec_shard, w, place):
    w = dict(w)
    g, dnorm, sums, recv = {}, {}, {}, {}
    nrm = lambda l, k: w["norm_w"][l, k][None, :]
    gather = lambda *keys: _gather_plan([shard[k] for k in keys])
    blocks = lambda r, c: (N_DEV, r // N_DEV, c)

    def pair_sum(key):
        sums[key] = _pair_sum(g[key], got[key], place, f"rs_pair_sum_{key[0]}_{key[1]}")

    got = {}

    def mlp_fwd(xin, layer, ride_up, ride_down):
        up = _mm(xin, w["w_mlp_up"][layer], out_dtype=ACT_DTYPE, tm=TM_FWD, tn=D_FF // N_DEV, b_blocked=True,
                 a_norm=nrm(layer, 2), name=f"mlp_up_{layer}", ride=ride_up)
        (u, h), rode_up = up if ride_up is not None else (up, None)
        down = _mm(u, w["w_mlp_down"][layer], out_dtype=F32, tm=TM_DX // 2, tn=D_MODEL, a_sqrelu=True,
                   res_norm=(xin, nrm(layer, 3)), name=f"mlp_down_{layer}", ride=ride_down)
        (yv, xout), rode_down = down if ride_down is not None else (down, None)
        return xout, (xin, h, u, yv), rode_up, rode_down

    def mlp_bwd(dxout, saved, layer, ride):
        xin, h, u, yv = saved
        k_up, k_down = ("w_mlp_up", layer), ("w_mlp_down", layer)
        res = _mm(dxout, w["w_mlp_down"][layer], nt=True, out_dtype=ACT_DTYPE, tm=TM_DX, tn=TN, drelu_of=u,
                  a_norm_bwd=(yv, nrm(layer, 3)), name=f"mlp_down_dx_{layer}", ride=ride)
        (du, dy, dnorm[(layer, 3)]), rode = res if ride is not None else (res, None)
        g[k_down] = _mm(u, dy, ta=True, out_dtype=WIRE_DTYPE, tm=TM_DW, tn=TN, a_sqrelu=True,
                        name=f"mlp_down_dw_{layer}").reshape(blocks(D_FF, D_MODEL))
        g[k_up] = _mm(h, du, ta=True, out_dtype=WIRE_DTYPE, tm=TM_DW, tn=D_FF // N_DEV, out_blocked=True,
                      name=f"mlp_up_dw_{layer}")
        w_up = jnp.moveaxis(w["w_mlp_up"][layer], 0, 1).reshape(D_MODEL, D_FF)
        (dxin, dnorm[(layer, 2)]), (got[k_down], got[k_up]) = _mm(
            du, w_up, nt=True, out_dtype=F32, tm=TM_DX // 2, tn=D_MODEL, norm_bwd=(xin, nrm(layer, 2), dxout),
            name=f"mlp_up_dx_{layer}", ride=_sibling_plan([g[k_down], g[k_up]]))
        pair_sum(k_down)
        pair_sum(k_up)
        return dxin, rode

    first = _run_plan(_gather_plan([shard[("w_in_even", 0)]] + [vec_shard[n] for n in VECTORS]),
                      "weights_all_gather_first")
    w["w_in_even"] = first[0].reshape(-1, D_MODEL)
    for n, b in zip(VECTORS, first[1:]):
        w[n] = _join_shards(b, SHARDED[n])
    w["w_mlp_up"], w["w_mlp_down"] = [None] * DEPTH, [None] * DEPTH

    wmm_e, wel_e = _even_in_split(w["w_in_even"])
    w_up_pad = jnp.pad(w["gla_w_a_up"][0], ((0, LANES - GLA_RANK), (0, 0)))
    b_f_pad = jnp.pad(w["fox_b_f"], ((0, 0), (0, LANES - ATT_HEADS)))
    (pmm0, h0), (w_out_even,) = _mm(x, wmm_e, nt=True, out_dtype=ACT_DTYPE, tm=TM_FWD, tn=TN, a_norm=nrm(0, 0),
                                    name="in_even_mm", ride=gather(("w_out_even", 0)))
    pel0 = _mm(h0, wel_e, nt=True, out_dtype=F32, tm=TM_FWD, tn=768, name="in_even_el")
    (out_a, states), (w_in_odd,) = _gla_fwd(pmm0, pel0, w_up_pad, w["gla_b_a"], w["gla_norm_w"],
                                            ride=gather(("w_in_odd", 0)))
    cum, cum_t = _fox_gate_fwd(pel0, b_f_pad)
    (out_b, lse_b), (w["w_mlp_up"][0], w_mlp_down0) = _fox_fwd(pmm0, cum, cum_t,
                                                               ride=gather(("w_mlp_up", 0), ("w_mlp_down", 0)))
    w["w_out_even"] = w_out_even.reshape(D_MODEL, D_MODEL)
    w["w_mlp_down"][0] = w_mlp_down0.reshape(D_FF, D_MODEL)
    mix_in0 = jnp.concatenate([out_a, out_b], axis=1)
    mix0, x1 = _mm(mix_in0, w["w_out_even"], out_dtype=F32, tm=TM_DX, tn=D_MODEL, res_norm=(x, nrm(0, 1)),
                   name="out_even")
    x2, mlp0, _, (w["w_mlp_up"][1],) = mlp_fwd(x1, 0, None, gather(("w_mlp_up", 1)))
    w["w_in_odd"] = w_in_odd.reshape(-1, D_MODEL)

    w_in_o = w["w_in_odd"]
    n_mm_o = 3 * GROUP_WIDTH
    wa_bd, wx_bd = _block_diag_pairs(w["lru_w_a"][0]), _block_diag_pairs(w["lru_w_x"][0])
    base = _ca_bias_base(w["rel_bias"][0])
    pmm1, h1 = _mm(x2, w_in_o[:n_mm_o], nt=True, out_dtype=ACT_DTYPE, tm=TM_FWD, tn=TN, a_norm=nrm(1, 0),
                   name="in_odd_mm")
    pel1 = _mm(h1, w_in_o[n_mm_o:], nt=True, out_dtype=F32, tm=TM_FWD, tn=TN, name="in_odd_el")
    kp = jnp.pad(pmm1[:, GROUP_WIDTH:2 * GROUP_WIDTH], ((CA_LEFT, 0), (0, 0)))
    vp = jnp.pad(pmm1[:, 2 * GROUP_WIDTH:], ((CA_LEFT, 0), (0, 0)))
    (out_c, lse_c), (w_mlp_down1,) = _ca_fwd(pmm1, kp, vp, base, ride=gather(("w_mlp_down", 1)))
    w["w_mlp_down"][1] = w_mlp_down1.reshape(D_FF, D_MODEL)
    lru_args = (pel1, w["conv_w"][0], w["conv_b"], wa_bd, w["lru_b_a"], wx_bd, w["lru_b_x"], w["lru_lambda"])
    out_d, (w_out_odd,) = _lru_fwd(*lru_args, ride=gather(("w_out_odd", 0)))
    w["w_out_odd"] = w_out_odd.reshape(D_MODEL, D_MODEL)
    mix_in1 = jnp.concatenate([out_c, out_d], axis=1)
    mix1, x3 = _mm(mix_in1, w["w_out_odd"], out_dtype=F32, tm=TM_DX, tn=D_MODEL, res_norm=(x2, nrm(1, 1)),
                   name="out_odd")
    x4, mlp1, _, _ = mlp_fwd(x3, 1, None, None)

    loss, dx4 = _loss_fwd_bwd(x4, target)

    k_oo, k_io, k_oe, k_ie = ("w_out_odd", 0), ("w_in_odd", 0), ("w_out_even", 0), ("w_in_even", 0)
    mlp_keys = lambda l: [("w_mlp_down", l), ("w_mlp_up", l)]
    dx3, _ = mlp_bwd(dx4, mlp1, 1, None)
    dmix_in1, dmix1, dnorm[(1, 1)] = _mm(dx3, w["w_out_odd"], nt=True, out_dtype=F32, tm=TM_DX, tn=TN,
                                         a_norm_bwd=(mix1, nrm(1, 1)), name="out_odd_dx")
    g[k_oo] = _mm(mix_in1, dmix1, ta=True, out_dtype=WIRE_DTYPE, tm=TM_DW, tn=TN, name="out_odd_dw").reshape(
        blocks(D_MODEL, D_MODEL))
    (dq_c, dkp, dvp, dbase), rode = _ca_bwd(
        pmm1, kp, vp, base, lse_c, dmix_in1,
        ride=_join_plans(_chip_plan([sums[k] for k in mlp_keys(1)]), _sibling_plan([g[k_oo]])))
    recv.update(zip(mlp_keys(1), rode[:2]))
    got[k_oo] = rode[2]
    pair_sum(k_oo)
    (dgate, dxin, g_conv_w, g_conv_b, dwa_bd, g_lru_b_a, dwx_bd, g_lru_b_x, g_lru_lambda), (recv[k_oo],) = _lru_bwd(
        *lru_args, dmix_in1, ride=_chip_plan([sums[k_oo]]))
    dp1 = jnp.concatenate([dq_c, dkp[CA_LEFT:].astype(ACT_DTYPE), dvp[CA_LEFT:].astype(ACT_DTYPE), dgate, dxin], axis=1)
    g[k_io] = _mm(dp1, h1, ta=True, out_dtype=WIRE_DTYPE, tm=dp1.shape[1] // 2, tn=TN, name="in_odd_dw").reshape(
        blocks(dp1.shape[1], D_MODEL))
    (dx2, dnorm[(1, 0)]), (got[k_io],) = _mm(dp1, w_in_o, out_dtype=F32, tm=TM_DX // 2, tn=D_MODEL,
                                             norm_bwd=(x2, nrm(1, 0), dx3), name="in_odd_dx",
                                             ride=_sibling_plan([g[k_io]]))
    pair_sum(k_io)
    g["rel_bias"] = _ca_bias_base_grad(dbase)[None]
    g["conv_w"], g["conv_b"] = g_conv_w[None], g_conv_b
    g["lru_w_a"], g["lru_w_x"] = _block_diag_pairs_grad(dwa_bd)[None], _block_diag_pairs_grad(dwx_bd)[None]
    g["lru_b_a"], g["lru_b_x"], g["lru_lambda"] = g_lru_b_a, g_lru_b_x, g_lru_lambda

    dx1, (recv[k_io],) = mlp_bwd(dx2, mlp0, 0, _chip_plan([sums[k_io]]))
    dmix_in0, dmix0, dnorm[(0, 1)] = _mm(dx1, w["w_out_even"], nt=True, out_dtype=F32, tm=TM_DX, tn=TN,
                                         a_norm_bwd=(mix0, nrm(0, 1)), name="out_even_dx")
    g[k_oe] = _mm(mix_in0, dmix0, ta=True, out_dtype=WIRE_DTYPE, tm=TM_DW, tn=TN, name="out_even_dw").reshape(
        blocks(D_MODEL, D_MODEL))
    k_md0, k_mu0 = mlp_keys(0)
    (dq_a, dk_a, dv_a, dr_a, da_a, dw_up_pad, g_gla_b_a, g_gla_norm_w), (got[k_oe],) = _gla_bwd(
        pmm0, pel0, w_up_pad, w["gla_b_a"], w["gla_norm_w"], states, dmix_in0, ride=_sibling_plan([g[k_oe]]))
    pair_sum(k_oe)
    (dq_b, dk_b, dv_b, dcum_t, dcum_q), (recv[k_md0], recv[k_mu0], recv[k_oe]) = _fox_bwd(
        pmm0, cum, cum_t, lse_b, dmix_in0, ride=_chip_plan([sums[k_md0], sums[k_mu0], sums[k_oe]]))
    df_b, db_f = _fox_gate_bwd(pel0, b_f_pad, dcum_t, dcum_q)
    g["gla_w_a_up"] = dw_up_pad[:GLA_RANK][None]
    g["gla_b_a"], g["gla_norm_w"], g["fox_b_f"] = g_gla_b_a, g_gla_norm_w, db_f[:, :ATT_HEADS]
    dp0 = jnp.concatenate([dq_a, dk_a, dv_a, dq_b, dk_b.astype(ACT_DTYPE), dv_b.astype(ACT_DTYPE), dr_a, da_a, df_b],
                          axis=1)
    w_perm = jnp.concatenate([wmm_e, wel_e], axis=0)
    n_mm_e = wmm_e.shape[0]
    dw_perm, repl_parts = _mm(dp0, h0, ta=True, out_dtype=WIRE_DTYPE, tm=dp0.shape[1] // 2, tn=TN, name="in_even_dw",
                              ride=_gather_plan([g[n] for n in REPLICATED]))
    dw_even = _even_in_merge(dw_perm[:n_mm_e], dw_perm[n_mm_e:])
    g[k_ie] = dw_even.reshape(blocks(dw_even.shape[0], D_MODEL))
    dh0, (got[k_ie],) = _mm(dp0, w_perm, out_dtype=F32, tm=TM_DX, tn=TN, name="in_even_dx",
                            ride=_sibling_plan([g[k_ie]]))
    pair_sum(k_ie)
    (dx0, dnorm[(0, 0)]), (recv[k_ie],) = _norm_bwd(dh0, x, nrm(0, 0), out_dtype=F32, add=dx1, name="norm_in_bwd_0",
                                                     ride=_chip_plan([sums[k_ie]]))

    g["norm_w"] = jnp.stack([jnp.concatenate([dnorm[(l, k)] for k in range(4)], axis=0) for l in range(DEPTH)])
    vec_parts = _run_plan(_gather_plan([_split_shards(g[n], SHARDED[n]) for n in VECTORS]), "vector_grads_all_gather")
    return loss, dx0, sums, recv, repl_parts, vec_parts


def kernel(x, norm_w, w_in_even, gla_w_a_up, gla_b_a, gla_norm_w, fox_b_f, w_out_even, w_in_odd, rel_bias, conv_w, conv_b, lru_w_a, lru_b_a, lru_w_x, lru_b_x, lru_lambda, w_out_odd, w_mlp_up, w_mlp_down, loss_target, m_norm_w, m_w_in_even, m_gla_w_a_up, m_gla_b_a, m_gla_norm_w, m_fox_b_f, m_w_out_even, m_w_in_odd, m_rel_bias, m_conv_w, m_conv_b, m_lru_w_a, m_lru_b_a, m_lru_w_x, m_lru_b_x, m_lru_lambda, m_w_out_odd, m_w_mlp_up, m_w_mlp_down, v_norm_w, v_w_in_even, v_gla_w_a_up, v_gla_b_a, v_gla_norm_w, v_fox_b_f, v_w_out_even, v_w_in_odd, v_rel_bias, v_conv_w, v_conv_b, v_lru_w_a, v_lru_b_a, v_lru_w_x, v_lru_b_x, v_lru_lambda, v_w_out_odd, v_w_mlp_up, v_w_mlp_down):
    wts = dict(zip(WEIGHTS, (norm_w, w_in_even, gla_w_a_up, gla_b_a, gla_norm_w, fox_b_f, w_out_even, w_in_odd, rel_bias,
                             conv_w, conv_b, lru_w_a, lru_b_a, lru_w_x, lru_b_x, lru_lambda, w_out_odd, w_mlp_up,
                             w_mlp_down)))
    mom = dict(zip(WEIGHTS, (m_norm_w, m_w_in_even, m_gla_w_a_up, m_gla_b_a, m_gla_norm_w, m_fox_b_f, m_w_out_even,
                             m_w_in_odd, m_rel_bias, m_conv_w, m_conv_b, m_lru_w_a, m_lru_b_a, m_lru_w_x, m_lru_b_x,
                             m_lru_lambda, m_w_out_odd, m_w_mlp_up, m_w_mlp_down)))
    var = dict(zip(WEIGHTS, (v_norm_w, v_w_in_even, v_gla_w_a_up, v_gla_b_a, v_gla_norm_w, v_fox_b_f, v_w_out_even,
                             v_w_in_odd, v_rel_bias, v_conv_w, v_conv_b, v_lru_w_a, v_lru_b_a, v_lru_w_x, v_lru_b_x,
                             v_lru_lambda, v_w_out_odd, v_w_mlp_up, v_w_mlp_down)))
    ax, ay, ac = lax.axis_index("x"), lax.axis_index("y"), lax.axis_index("c")
    place = jnp.stack([ac, 2 * ax + ay, 4 * ax + 2 * ay + ac]).astype(jnp.int32)

    shard = {(n, l): (wts[n][l].T if n in TRANSPOSED else wts[n][l]).astype(WIRE_DTYPE) for n, l in MATRIX_BLOCKS}
    loss_blk, dx, sums, recv, repl_parts, vec_parts = _forward_backward(
        x[0], loss_target[0], shard, {n: wts[n] for n in VECTORS}, {n: wts[n] for n in REPLICATED}, place)
    loss = lax.psum(loss_blk[0, 0], ("x", "y", "c"))

    view = lambda n, a: jnp.swapaxes(a, 1, 2) if n in TRANSPOSED else a
    upd = {n: [view(n, o) for o in _adamw_sharded(
        [(sums[(n, l)], recv[(n, l)]) for l in range(wts[n].shape[0])], view(n, wts[n]), view(n, mom[n]), view(n, var[n]),
        place, f"adamw_{n}")] for n in MATRICES}
    small = REPLICATED + list(VECTORS)
    upd.update(zip(small, _adamw_small(repl_parts, vec_parts, [wts[n] for n in small], [mom[n] for n in small],
                                       [var[n] for n in small], place)))
    return (loss, dx[None], *[upd[n][kind] for kind in range(4) for n in WEIGHTS])
```

```python
import functools
from typing import Callable, NamedTuple

import jax
import jax.numpy as jnp
from jax import lax
from jax.experimental import pallas as pl
from jax.experimental.pallas import tpu as pltpu

F32 = jnp.float32
MXU_DTYPE = jnp.bfloat16
ACT_DTYPE = jnp.bfloat16
WIRE_DTYPE = jnp.bfloat16

V7X_VMEM_BYTES = 64 * 1024 * 1024
VMEM_LIMIT = (V7X_VMEM_BYTES * 7) // 8
LANES = 128

D_MODEL = 1024
SEQ = 2048
DEPTH = 2
CHUNK = 64
GROUP_WIDTH = D_MODEL // 2
D_FF = 4 * D_MODEL
NORM_EPS = 1e-6
GLA_HEADS = 4
GLA_DV = GROUP_WIDTH // GLA_HEADS
GLA_DK = GLA_DV // 2
GLA_KW = GLA_HEADS * GLA_DK
GLA_RANK = 16
GLA_GATE_TAU = 16.0
HEAD_DIM = 64
ATT_HEADS = GROUP_WIDTH // HEAD_DIM
CA_LEFT = 8 * CHUNK
REL_CLIP = 128
LRU_BLOCK_DIM = 64
CONV_WIDTH = 4
LRU_C = 8.0
N_DEV = 8

ADAM_LR = 0.001
ADAM_B1 = 0.9
ADAM_B2 = 0.999
ADAM_EPS = 1e-08
ADAM_WD = 0.01
ADAM_STEP = 10

NEG = float(jnp.finfo(jnp.float32).min)
MESH = pl.DeviceIdType.MESH


def _params(*sem):
    return pltpu.CompilerParams(dimension_semantics=sem, vmem_limit_bytes=VMEM_LIMIT)


def _dot(a, b, ca=1, cb=0):
    return lax.dot_general(a.astype(MXU_DTYPE), b.astype(MXU_DTYPE), (((ca,), (cb,)), ((), ())),
                           preferred_element_type=F32)


def _dot_exact(a, b):
    return lax.dot_general(a, b, (((1,), (0,)), ((), ())), precision=lax.Precision.HIGHEST,
                           preferred_element_type=F32)


def _log_sigmoid(x):
    return jnp.minimum(x, 0.0) - jnp.log1p(jnp.exp(-jnp.abs(x)))


def _iota(shape, axis):
    return lax.broadcasted_iota(jnp.int32, shape, axis)


ANY = pl.BlockSpec(memory_space=pl.ANY)
N_CHIPS = 4


class _Plan(NamedTuple):
    ins: list
    outs: list
    sems: list
    start: Callable
    finish: Callable


def _place():
    x, y, c = lax.axis_index("x"), lax.axis_index("y"), lax.axis_index("c")
    return x, y, c, [(1 - x, y), (x, 1 - y), (1 - x, 1 - y)]


def _gather_plan(xs):
    n = len(xs)

    def parts(x_refs, out_refs, sems):
        send_sems, recv_sems, local_sems = sems
        x, y, c, chips = _place()
        me, sibling = (x, y, c), (x, y, 1 - c)

        def rows(a, px, py, pc):
            return out_refs[a].at[4 * px + 2 * py + pc]

        def copy(a, k, block, to, src=None):
            return pltpu.make_async_remote_copy(
                src_ref=rows(a, *block) if src is None else src, dst_ref=rows(a, *block),
                send_sem=send_sems.at[7 * a + k], recv_sem=recv_sems.at[7 * a + k], device_id=to, device_id_type=MESH)

        mine = [pltpu.make_async_copy(x_refs[a], rows(a, *me), local_sems.at[a]) for a in range(n)]
        first = []
        for a in range(n):
            first.append(copy(a, 0, me, sibling, src=x_refs[a]))
            first += [copy(a, 1 + j, me, (*chip, c), src=x_refs[a]) for j, chip in enumerate(chips)]
        return c, me, sibling, chips, copy, mine, first

    def start(x_refs, out_refs, sems):
        *_, mine, first = parts(x_refs, out_refs, sems)
        for cp in first + mine:
            cp.start()

    def finish(x_refs, out_refs, sems):
        c, me, sibling, chips, copy, mine, first = parts(x_refs, out_refs, sems)
        passed = []
        for j, chip in enumerate(chips):
            for a in range(n):
                copy(a, 1 + j, (*chip, c), me).wait_recv()
                passed.append(copy(a, 4 + j, (*chip, c), sibling))
                passed[-1].start()
        for a in range(n):
            copy(a, 0, sibling, me).wait_recv()
            for j, chip in enumerate(chips):
                copy(a, 4 + j, (*chip, 1 - c), me).wait_recv()
        for cp in first + passed:
            cp.wait_send()
        for cp in mine:
            cp.wait()

    return _Plan(list(xs), [jax.ShapeDtypeStruct((N_DEV,) + x.shape, x.dtype) for x in xs],
                 [pltpu.SemaphoreType.DMA((7 * n,)), pltpu.SemaphoreType.DMA((7 * n,)), pltpu.SemaphoreType.DMA((n,))],
                 start, finish)


def _exchange_plan(copies_of, ins, outs, per_array):
    n = len(ins)

    def start(in_refs, out_refs, sems):
        for cp in copies_of(in_refs, out_refs, sems):
            cp.start()

    def finish(in_refs, out_refs, sems):
        copies = copies_of(in_refs, out_refs, sems)
        for cp in copies:
            cp.wait_recv()
        for cp in copies:
            cp.wait_send()

    return _Plan(list(ins), outs, [pltpu.SemaphoreType.DMA((per_array * n,)), pltpu.SemaphoreType.DMA((per_array * n,))],
                 start, finish)


def _sibling_plan(gs):
    def copies_of(g_refs, got_refs, sems):
        x, y, c, _ = _place()
        return [pltpu.make_async_remote_copy(
            src_ref=g_refs[a].at[2 * k + (1 - c)], dst_ref=got_refs[a].at[k], send_sem=sems[0].at[N_CHIPS * a + k],
            recv_sem=sems[1].at[N_CHIPS * a + k], device_id=(x, y, 1 - c), device_id_type=MESH)
            for a in range(len(gs)) for k in range(N_CHIPS)]

    return _exchange_plan(copies_of, gs, [jax.ShapeDtypeStruct((N_CHIPS,) + g.shape[1:], g.dtype) for g in gs], N_CHIPS)


def _chip_plan(ss):
    def copies_of(s_refs, out_refs, sems):
        x, y, c, chips = _place()
        return [pltpu.make_async_remote_copy(
            src_ref=s_refs[a].at[2 * px + py], dst_ref=out_refs[a].at[j], send_sem=sems[0].at[3 * a + j],
            recv_sem=sems[1].at[3 * a + j], device_id=(px, py, c), device_id_type=MESH)
            for a in range(len(ss)) for j, (px, py) in enumerate(chips)]

    return _exchange_plan(copies_of, ss, [jax.ShapeDtypeStruct((3,) + s.shape[1:], s.dtype) for s in ss], 3)


def _join_plans(*plans):
    def cut(refs, counts):
        at = 0
        for n in counts:
            yield refs[at:at + n]
            at += n

    def each(in_refs, out_refs, sems):
        return zip(plans, cut(in_refs, [len(p.ins) for p in plans]), cut(out_refs, [len(p.outs) for p in plans]),
                   cut(sems, [len(p.sems) for p in plans]))

    def start(*refs):
        for p, i, o, s in each(*refs):
            p.start(i, o, s)

    def finish(*refs):
        for p, i, o, s in each(*refs):
            p.finish(i, o, s)

    return _Plan([a for p in plans for a in p.ins], [a for p in plans for a in p.outs],
                 [a for p in plans for a in p.sems], start, finish)


def _run_plan(plan, name):
    n_in, n_out = len(plan.ins), len(plan.outs)

    def body(*refs):
        args = refs[:n_in], refs[n_in:n_in + n_out], refs[n_in + n_out:]
        plan.start(*args)
        plan.finish(*args)

    return pl.pallas_call(body, out_shape=plan.outs, in_specs=[ANY] * n_in, out_specs=[ANY] * n_out,
                          scratch_shapes=plan.sems, name=name)(*plan.ins)


def _pcall(body, ride, *, grid, in_specs, out_specs, out_shape, scratch_shapes=(), semantics, name):
    if ride is None:
        return pl.pallas_call(body, grid=grid, in_specs=in_specs, out_specs=out_specs, out_shape=out_shape,
                              scratch_shapes=list(scratch_shapes), compiler_params=_params(*semantics), name=name)
    single = not isinstance(out_shape, (list, tuple))
    out_specs_l, out_shape_l = ([out_specs], [out_shape]) if single else (list(out_specs), list(out_shape))
    n_in, n_out, n_scr = len(in_specs), len(out_shape_l), len(scratch_shapes)
    r_in, r_out = len(ride.ins), len(ride.outs)

    def riding(*refs):
        cuts = [n_in, r_in, n_out, r_out, n_scr]
        groups, at = [], 0
        for width in cuts:
            groups.append(refs[at:at + width])
            at += width
        ins, r_ins, outs, r_outs, scr = groups
        sems = refs[at:]
        first = functools.reduce(jnp.logical_and, [pl.program_id(d) == 0 for d in range(len(grid))])
        last = functools.reduce(jnp.logical_and, [pl.program_id(d) == grid[d] - 1 for d in range(len(grid))])

        @pl.when(first)
        def _():
            ride.start(r_ins, r_outs, sems)

        body(*ins, *outs, *scr)

        @pl.when(last)
        def _():
            ride.finish(r_ins, r_outs, sems)

    call = pl.pallas_call(
        riding, grid=grid, in_specs=list(in_specs) + [ANY] * r_in, out_specs=out_specs_l + [ANY] * r_out,
        out_shape=out_shape_l + list(ride.outs), scratch_shapes=list(scratch_shapes) + list(ride.sems),
        compiler_params=_params(*(["arbitrary"] * len(grid))), name=name)

    def run(*args):
        res = call(*args, *ride.ins)
        return (res[0] if single else list(res[:n_out])), list(res[n_out:])

    return run


def _rms(x):
    return x * lax.rsqrt(jnp.mean(x * x, axis=-1, keepdims=True) + NORM_EPS)


def _mm(a, b, *, nt=False, ta=False, out_dtype, tm, tn, a_sqrelu=False, drelu_of=None, b_blocked=False,
        out_blocked=False, a_norm=None, a_norm_bwd=None, res_norm=None, norm_bwd=None, name, ride=None):
    k, m = a.shape if ta else a.shape[::-1]
    if b_blocked:
        assert not nt and b.shape[1] == k and b.shape[2] == tn
        n = b.shape[0] * tn
    else:
        n = b.shape[0] if nt else b.shape[1]
        assert (b.shape[1] if nt else b.shape[0]) == k
    tm, tn = min(tm, m), min(tn, n)
    assert m % tm == 0 and n % tn == 0
    assert (res_norm is None and norm_bwd is None) or tn == n
    assert a_norm is None or a_norm_bwd is None
    n_in = (2 + (drelu_of is not None) + (a_norm is not None) + 2 * (a_norm_bwd is not None)
            + 2 * (res_norm is not None) + 3 * (norm_bwd is not None))

    def body(*refs):
        a_ref, b_ref = refs[0], refs[1]
        extra = list(refs[2:n_in])
        outs = list(refs[n_in:])
        o_ref = outs.pop(0)
        u_ref = extra.pop(0) if drelu_of is not None else None
        if a_norm is not None:
            wn_ref, h_ref, h_scr = extra.pop(0), outs.pop(0), outs.pop()

            @pl.when(pl.program_id(1) == 0)
            def _():
                h = (_rms(a_ref[...]) * wn_ref[...]).astype(ACT_DTYPE)
                h_scr[...] = h
                h_ref[...] = h

            av = h_scr[...]
        elif a_norm_bwd is not None:
            y_ref, wy_ref = extra.pop(0), extra.pop(0)
            dy_ref, dwy_ref, dy_scr = outs.pop(0), outs.pop(0), outs.pop()
            first_rows = pl.program_id(0) == 0

            @pl.when(pl.program_id(1) == 0)
            def _():
                yv, up = y_ref[...], a_ref[...]
                rstd = lax.rsqrt(jnp.mean(yv * yv, axis=-1, keepdims=True) + NORM_EPS)
                yhat = yv * rstd
                g = up * wy_ref[...]
                dy = (rstd * (g - yhat * jnp.mean(g * yhat, axis=-1, keepdims=True))).astype(ACT_DTYPE)
                dy_scr[...] = dy
                dy_ref[...] = dy

                @pl.when(first_rows)
                def _():
                    dwy_ref[...] = jnp.zeros_like(dwy_ref)

                dwy_ref[...] += jnp.sum(up * yhat, axis=0, keepdims=True)

            av = dy_scr[...]
        else:
            av = a_ref[...]
        if a_sqrelu:
            av = jnp.square(jnp.maximum(av.astype(F32), 0.0))
        acc = _dot(av, b_ref[...], 0 if ta else 1, 1 if nt else 0)
        if u_ref is not None:
            acc = acc * (2.0 * jnp.maximum(u_ref[...].astype(F32), 0.0))
        if norm_bwd is not None:
            x_ref, wb_ref, add_ref = extra
            dw_ref = outs[0]
            xv = x_ref[...]
            rstd = lax.rsqrt(jnp.mean(xv * xv, axis=-1, keepdims=True) + NORM_EPS)
            xhat = xv * rstd
            g = acc * wb_ref[...]
            o_ref[...] = rstd * (g - xhat * jnp.mean(g * xhat, axis=-1, keepdims=True)) + add_ref[...]

            @pl.when(pl.program_id(0) == 0)
            def _():
                dw_ref[...] = jnp.zeros_like(dw_ref)

            dw_ref[...] += jnp.sum(acc * xhat, axis=0, keepdims=True)
            return
        o_ref[...] = acc.astype(out_dtype)
        if res_norm is not None:
            res_ref, wr_ref = extra
            outs[0][...] = res_ref[...] + _rms(acc) * wr_ref[...]

    if b_blocked:
        b_spec = pl.BlockSpec((None, k, tn), lambda i, j: (j, 0, 0))
    elif nt:
        b_spec = pl.BlockSpec((tn, k), lambda i, j: (j, 0))
    else:
        b_spec = pl.BlockSpec((k, tn), lambda i, j: (0, j))
    a_spec = pl.BlockSpec((k, tm), lambda i, j: (0, i)) if ta else pl.BlockSpec((tm, k), lambda i, j: (i, 0))
    in_specs = [a_spec, b_spec]
    args = [a, b]
    if drelu_of is not None:
        in_specs.append(pl.BlockSpec((tm, tn), lambda i, j: (i, j)))
        args.append(drelu_of)
    if out_blocked:
        out_specs = [pl.BlockSpec((None, tm, tn), lambda i, j: (j, i, 0))]
        out_shape = [jax.ShapeDtypeStruct((n // tn, m, tn), out_dtype)]
    else:
        out_specs = [pl.BlockSpec((tm, tn), lambda i, j: (i, j))]
        out_shape = [jax.ShapeDtypeStruct((m, n), out_dtype)]
    scratch = []
    if a_norm is not None:
        assert not ta
        in_specs.append(pl.BlockSpec((1, k), lambda i, j: (0, 0)))
        args.append(a_norm)
        out_specs.append(pl.BlockSpec((tm, k), lambda i, j: (i, 0)))
        out_shape.append(jax.ShapeDtypeStruct((m, k), ACT_DTYPE))
        scratch.append(pltpu.VMEM((tm, k), ACT_DTYPE))
    if a_norm_bwd is not None:
        assert not ta
        in_specs += [pl.BlockSpec((tm, k), lambda i, j: (i, 0)), pl.BlockSpec((1, k), lambda i, j: (0, 0))]
        args += list(a_norm_bwd)
        out_specs += [pl.BlockSpec((tm, k), lambda i, j: (i, 0)), pl.BlockSpec((1, k), lambda i, j: (0, 0))]
        out_shape += [jax.ShapeDtypeStruct((m, k), ACT_DTYPE), jax.ShapeDtypeStruct((1, k), F32)]
        scratch.append(pltpu.VMEM((tm, k), ACT_DTYPE))
    if res_norm is not None:
        in_specs += [pl.BlockSpec((tm, n), lambda i, j: (i, 0)), pl.BlockSpec((1, n), lambda i, j: (0, 0))]
        args += list(res_norm)
        out_specs.append(pl.BlockSpec((tm, n), lambda i, j: (i, 0)))
        out_shape.append(jax.ShapeDtypeStruct((m, n), F32))
    if norm_bwd is not None:
        rows = pl.BlockSpec((tm, n), lambda i, j: (i, 0))
        in_specs += [rows, pl.BlockSpec((1, n), lambda i, j: (0, 0)), rows]
        args += list(norm_bwd)
        out_specs.append(pl.BlockSpec((1, n), lambda i, j: (0, 0)))
        out_shape.append(jax.ShapeDtypeStruct((1, n), F32))
    single = len(out_shape) == 1
    return _pcall(body, ride, grid=(m // tm, n // tn), in_specs=in_specs,
                  out_specs=out_specs[0] if single else out_specs, out_shape=out_shape[0] if single else out_shape,
                  scratch_shapes=scratch, semantics=("arbitrary", "arbitrary"), name=name)(*args)


ROW_TILE = 512
TM_FWD, TM_DX, TM_DW, TN = 2048, 1024, 1024, 512


def _norm_bwd(dy, x, w, *, out_dtype, add=None, name, ride=None):
    t, d = x.shape

    def body(*refs):
        dy_ref, x_ref, w_ref = refs[0], refs[1], refs[2]
        dx_ref, dw_ref = refs[-2], refs[-1]
        xv = x_ref[...]
        rstd = lax.rsqrt(jnp.mean(xv * xv, axis=-1, keepdims=True) + NORM_EPS)
        xhat = xv * rstd
        dyv = dy_ref[...].astype(F32)
        g = dyv * w_ref[...]
        dx = rstd * (g - xhat * jnp.mean(g * xhat, axis=-1, keepdims=True))
        if add is not None:
            dx = dx + refs[3][...]
        dx_ref[...] = dx.astype(out_dtype)

        @pl.when(pl.program_id(0) == 0)
        def _():
            dw_ref[...] = jnp.zeros_like(dw_ref)

        dw_ref[...] += jnp.sum(dyv * xhat, axis=0, keepdims=True)

    row = pl.BlockSpec((ROW_TILE, d), lambda i: (i, 0))
    vec = pl.BlockSpec((1, d), lambda i: (0, 0))
    in_specs = [row, row, vec] + ([row] if add is not None else [])
    args = [dy, x, w] + ([add] if add is not None else [])
    return _pcall(body, ride, grid=(t // ROW_TILE,), in_specs=in_specs, out_specs=[row, vec],
                  out_shape=[jax.ShapeDtypeStruct((t, d), out_dtype), jax.ShapeDtypeStruct((1, d), F32)],
                  semantics=("arbitrary",), name=name)(*args)


def _loss_fwd_bwd(y, target):
    t, d = y.shape

    def body(y_ref, t_ref, l_ref, dy_ref):
        diff = y_ref[...] - t_ref[...]
        dy_ref[...] = diff * (1.0 / d)

        @pl.when(pl.program_id(0) == 0)
        def _():
            l_ref[...] = jnp.zeros_like(l_ref)

        l_ref[...] += 0.5 * jnp.sum(jnp.mean(diff * diff, axis=-1, keepdims=True), axis=0, keepdims=True)

    row = pl.BlockSpec((ROW_TILE, d), lambda i: (i, 0))
    return pl.pallas_call(body, grid=(t // ROW_TILE,), in_specs=[row, row],
                          out_specs=[pl.BlockSpec((8, LANES), lambda i: (0, 0)), row],
                          out_shape=[jax.ShapeDtypeStruct((8, LANES), F32), jax.ShapeDtypeStruct((t, d), F32)],
                          compiler_params=_params("arbitrary"), name="loss")(y, target)


GLA_STATE = (GLA_HEADS * GLA_DV, GLA_KW)


def _gla_specs(chunk_of):
    rows = lambda width, col: pl.BlockSpec((CHUNK, width), lambda i: (chunk_of(i), col))
    const = lambda r, c: pl.BlockSpec((r, c), lambda i: (0, 0))
    return [rows(GLA_KW, 0),
            rows(GLA_KW, 1),
            rows(GROUP_WIDTH, 1),
            rows(GROUP_WIDTH, 0),
            rows(LANES, 4),
            const(LANES, GLA_KW),
            const(1, GLA_KW),
            const(1, GROUP_WIDTH)]


def _gla_chunk(q_ref, k_ref, v_ref, a_ref, wup_ref, ba_ref):
    z = _dot(a_ref[...], wup_ref[...]) + ba_ref[...]
    tri = (_iota((CHUNK, CHUNK), 1) <= _iota((CHUNK, CHUNK), 0)).astype(F32)
    cum = _dot_exact(tri, _log_sigmoid(z) * (1.0 / GLA_GATE_TAU))
    tot = cum[CHUNK - 1:CHUNK, :]
    e = jnp.exp(tot - cum)
    return (z, e, jnp.exp(tot), k_ref[...].astype(F32) * e, q_ref[...].astype(F32) * (GLA_DK ** -0.5),
            v_ref[...].astype(F32))


def _gla_head_mask():
    return _iota(GLA_STATE, 0) // GLA_DV == _iota(GLA_STATE, 1) // GLA_DK


def _gla_fwd(pmm, pel, w_up, b_a, gnorm_w, ride=None):
    t = pmm.shape[0]
    nc = t // CHUNK

    def body(q_ref, k_ref, v_ref, r_ref, a_ref, wup_ref, ba_ref, gw_ref, o_ref, st_ref, m_scr):
        @pl.when(pl.program_id(0) == 0)
        def _():
            m_scr[...] = jnp.zeros_like(m_scr)

        _, _, decay, kd, qs, vv = _gla_chunk(q_ref, k_ref, v_ref, a_ref, wup_ref, ba_ref)
        m = m_scr[...] * decay + jnp.where(_gla_head_mask(), _dot(vv, kd, 0, 0), 0.0)
        m_scr[...] = m
        st_ref[...] = m
        o = _dot(qs, m, 1, 1)
        rr = r_ref[...]
        gate = rr * jax.nn.sigmoid(rr) * gw_ref[...]
        for h in range(GLA_HEADS):
            vs = slice(h * GLA_DV, (h + 1) * GLA_DV)
            oh = o[:, vs]
            y = oh * lax.rsqrt(jnp.mean(oh * oh, axis=-1, keepdims=True) + NORM_EPS)
            o_ref[:, vs] = (y * gate[:, vs]).astype(o_ref.dtype)

    return _pcall(
        body, ride, grid=(nc,), in_specs=_gla_specs(lambda i: i),
        out_specs=[pl.BlockSpec((CHUNK, GROUP_WIDTH), lambda i: (i, 0)),
                   pl.BlockSpec((None,) + GLA_STATE, lambda i: (i, 0, 0))],
        out_shape=[jax.ShapeDtypeStruct((t, GROUP_WIDTH), ACT_DTYPE), jax.ShapeDtypeStruct((nc,) + GLA_STATE, F32)],
        scratch_shapes=[pltpu.VMEM(GLA_STATE, F32)],
        semantics=("arbitrary",), name="gla_fwd")(pmm, pmm, pmm, pel, pel, w_up, b_a, gnorm_w)


def _gla_bwd(pmm, pel, w_up, b_a, gnorm_w, states, dmix, ride=None):
    t = pmm.shape[0]
    nc = t // CHUNK
    scale = GLA_DK ** -0.5

    def body(q_ref, k_ref, v_ref, r_ref, a_ref, wup_ref, ba_ref, gw_ref, st_ref, prev_ref, do_ref,
             dq_ref, dk_ref, dv_ref, dr_ref, da_ref, dwup_ref, dba_ref, dgw_ref, dm_scr):
        step = pl.program_id(0)

        @pl.when(step == 0)
        def _():
            dm_scr[...] = jnp.zeros_like(dm_scr)
            dwup_ref[...] = jnp.zeros_like(dwup_ref)
            dba_ref[...] = jnp.zeros_like(dba_ref)
            dgw_ref[...] = jnp.zeros_like(dgw_ref)

        z, e, decay, kd, qs, vv = _gla_chunk(q_ref, k_ref, v_ref, a_ref, wup_ref, ba_ref)
        m = st_ref[...]
        m_prev = prev_ref[...] * (step < nc - 1).astype(F32)
        rr, dout, gw = r_ref[...], do_ref[...], gw_ref[...]
        sig = jax.nn.sigmoid(rr)
        silu = rr * sig
        dsilu = sig * (1.0 + rr * (1.0 - sig))
        o = _dot(qs, m, 1, 1)
        d_o, dgw = [], []
        for h in range(GLA_HEADS):
            vs = slice(h * GLA_DV, (h + 1) * GLA_DV)
            oh, dg = o[:, vs], dout[:, vs]
            rstd = lax.rsqrt(jnp.mean(oh * oh, axis=-1, keepdims=True) + NORM_EPS)
            y = oh * rstd
            dgw.append(jnp.sum(dg * y * silu[:, vs], axis=0, keepdims=True))
            dr_ref[:, vs] = (dg * y * gw[:, vs] * dsilu[:, vs]).astype(dr_ref.dtype)
            dy = dg * gw[:, vs] * silu[:, vs]
            d_o.append(rstd * (dy - y * jnp.mean(dy * y, axis=-1, keepdims=True)))
        d_o = jnp.concatenate(d_o, axis=1)
        dgw_ref[...] += jnp.concatenate(dgw, axis=1)
        dq_ref[...] = (_dot(d_o, m) * scale).astype(dq_ref.dtype)
        dm = dm_scr[...] + jnp.where(_gla_head_mask(), _dot(d_o, qs, 0, 0), 0.0)
        dv_ref[...] = _dot(kd, dm, 1, 1).astype(dv_ref.dtype)
        dkd = _dot(vv, dm)
        dk_ref[...] = (dkd * e).astype(dk_ref.dtype)
        dm_scr[...] = dm * decay
        tri_strict = (_iota((CHUNK, CHUNK), 1) < _iota((CHUNK, CHUNK), 0)).astype(F32)
        dla = jnp.sum(dm * m_prev, axis=0, keepdims=True) * decay + _dot_exact(tri_strict, dkd * kd)
        dz = dla * jax.nn.sigmoid(-z) * (1.0 / GLA_GATE_TAU)
        da_ref[...] = _dot(dz, wup_ref[...], 1, 1).astype(da_ref.dtype)
        dwup_ref[...] += _dot(a_ref[...], dz, 0, 0)
        dba_ref[...] += jnp.sum(dz, axis=0, keepdims=True)

    chunk_of = lambda i: nc - 1 - i
    in_specs = _gla_specs(chunk_of) + [
        pl.BlockSpec((None,) + GLA_STATE, lambda i: (chunk_of(i), 0, 0)),
        pl.BlockSpec((None,) + GLA_STATE, lambda i: (jnp.maximum(chunk_of(i) - 1, 0), 0, 0)),
        pl.BlockSpec((CHUNK, GROUP_WIDTH), lambda i: (chunk_of(i), 0))]
    rows = lambda width: pl.BlockSpec((CHUNK, width), lambda i: (chunk_of(i), 0))
    const = lambda r, c: pl.BlockSpec((r, c), lambda i: (0, 0))
    return _pcall(
        body, ride, grid=(nc,), in_specs=in_specs,
        out_specs=[rows(GLA_KW), rows(GLA_KW), rows(GROUP_WIDTH), rows(GROUP_WIDTH), rows(LANES),
                   const(LANES, GLA_KW), const(1, GLA_KW), const(1, GROUP_WIDTH)],
        out_shape=[jax.ShapeDtypeStruct((t, GLA_KW), ACT_DTYPE), jax.ShapeDtypeStruct((t, GLA_KW), ACT_DTYPE),
                   jax.ShapeDtypeStruct((t, GROUP_WIDTH), ACT_DTYPE), jax.ShapeDtypeStruct((t, GROUP_WIDTH), ACT_DTYPE),
                   jax.ShapeDtypeStruct((t, LANES), ACT_DTYPE), jax.ShapeDtypeStruct((LANES, GLA_KW), F32),
                   jax.ShapeDtypeStruct((1, GLA_KW), F32), jax.ShapeDtypeStruct((1, GROUP_WIDTH), F32)],
        scratch_shapes=[pltpu.VMEM(GLA_STATE, F32)],
        semantics=("arbitrary",), name="gla_bwd")(
            pmm, pmm, pmm, pel, pel, w_up, b_a, gnorm_w, states, states, dmix)


CUM_BLOCK = 256


def _fox_gate_fwd(pel, b_f):
    t = pel.shape[0]
    nb = t // CUM_BLOCK

    def body(f_ref, b_ref, cum_ref, cum_t_ref):
        tri = (_iota((CUM_BLOCK, CUM_BLOCK), 1) <= _iota((CUM_BLOCK, CUM_BLOCK), 0)).astype(F32)
        carry = jnp.zeros((1, LANES), F32)
        for blk in range(nb):
            rows = slice(blk * CUM_BLOCK, (blk + 1) * CUM_BLOCK)
            cum = _dot_exact(tri, _log_sigmoid(f_ref[rows, :] + b_ref[...])) + carry
            cum_ref[rows, :] = cum
            cum_t_ref[blk] = cum.T[:ATT_HEADS, :]
            carry = cum[CUM_BLOCK - 1:CUM_BLOCK, :]

    return pl.pallas_call(
        body, grid=(1,),
        in_specs=[pl.BlockSpec((t, LANES), lambda i: (0, 5)), pl.BlockSpec((1, LANES), lambda i: (0, 0))],
        out_specs=[pl.BlockSpec((t, LANES), lambda i: (0, 0)),
                   pl.BlockSpec((nb, ATT_HEADS, CUM_BLOCK), lambda i: (0, 0, 0))],
        out_shape=[jax.ShapeDtypeStruct((t, LANES), F32), jax.ShapeDtypeStruct((nb, ATT_HEADS, CUM_BLOCK), F32)],
        compiler_params=_params("arbitrary"), name="fox_gate_fwd")(pel, b_f)


def _fox_gate_bwd(pel, b_f, dcum_t, dcum_q):
    t = pel.shape[0]
    nb = t // CUM_BLOCK

    def body(f_ref, b_ref, dct_ref, dcq_ref, df_ref, db_ref):
        tri_up = (_iota((CUM_BLOCK, CUM_BLOCK), 1) >= _iota((CUM_BLOCK, CUM_BLOCK), 0)).astype(F32)
        carry = jnp.zeros((1, LANES), F32)
        db = jnp.zeros((1, LANES), F32)
        for blk in reversed(range(nb)):
            rows = slice(blk * CUM_BLOCK, (blk + 1) * CUM_BLOCK)
            query_side = sum(dcq_ref[pair, rows, :] for pair in range(dcq_ref.shape[0]))
            dls = _dot_exact(tri_up, dct_ref[blk].T + query_side) + carry
            carry = dls[0:1, :]
            df = dls * jax.nn.sigmoid(-(f_ref[rows, :] + b_ref[...]))
            df_ref[rows, :] = df.astype(df_ref.dtype)
            db = db + jnp.sum(df, axis=0, keepdims=True)
        db_ref[...] = db

    return pl.pallas_call(
        body, grid=(1,),
        in_specs=[pl.BlockSpec((t, LANES), lambda i: (0, 5)), pl.BlockSpec((1, LANES), lambda i: (0, 0)),
                  pl.BlockSpec((nb, LANES, CUM_BLOCK), lambda i: (0, 0, 0)),
                  pl.BlockSpec((dcum_q.shape[0], t, LANES), lambda i: (0, 0, 0))],
        out_specs=[pl.BlockSpec((t, LANES), lambda i: (0, 0)), pl.BlockSpec((1, LANES), lambda i: (0, 0))],
        out_shape=[jax.ShapeDtypeStruct((t, LANES), ACT_DTYPE), jax.ShapeDtypeStruct((1, LANES), F32)],
        compiler_params=_params("arbitrary"), name="fox_gate_bwd")(pel, b_f, dcum_t, dcum_q)


FOX_Q_BLOCK = 256


assert FOX_Q_BLOCK == CUM_BLOCK
FOX_KEY_STEP = 512


def _fox_scores(q_ref, k_ref, cum_ref, cum_t_ref, h, i):
    hs = slice(h * HEAD_DIM, (h + 1) * HEAD_DIM)
    nb = cum_t_ref.shape[0]
    key_gate = jnp.concatenate([cum_t_ref[kb, h:h + 1, :] for kb in range(nb)], axis=1)
    s = _dot(q_ref[:, hs], k_ref[:, hs], 1, 1) * (HEAD_DIM ** -0.5) + (cum_ref[:, h:h + 1] - key_gate)
    shape = (FOX_Q_BLOCK, nb * FOX_Q_BLOCK)
    return jnp.where(_iota(shape, 1) <= i * FOX_Q_BLOCK + _iota(shape, 0), s, NEG)


def _fox_specs(t):
    bq, nb = FOX_Q_BLOCK, t // FOX_Q_BLOCK
    return [pl.BlockSpec((bq, GROUP_WIDTH), lambda i: (i, 2)), pl.BlockSpec((t, GROUP_WIDTH), lambda i: (0, 3)),
            pl.BlockSpec((t, GROUP_WIDTH), lambda i: (0, 4)), pl.BlockSpec((bq, LANES), lambda i: (i, 0)),
            pl.BlockSpec((nb, ATT_HEADS, bq), lambda i: (0, 0, 0))]


def _fox_fwd(pmm, cum, cum_t, ride=None):
    t = pmm.shape[0]
    bq = FOX_Q_BLOCK

    def body(q_ref, k_ref, v_ref, cum_ref, cum_t_ref, o_ref, lse_ref):
        i = pl.program_id(0)
        lse_ref[...] = jnp.zeros_like(lse_ref)
        for h in range(ATT_HEADS):
            hs = slice(h * HEAD_DIM, (h + 1) * HEAD_DIM)
            s = _fox_scores(q_ref, k_ref, cum_ref, cum_t_ref, h, i)
            m = jnp.max(s, axis=-1, keepdims=True)
            p = jnp.exp(s - m)
            l = jnp.sum(p, axis=-1, keepdims=True)
            o_ref[:, hs] = (_dot(p, v_ref[:, hs]) / l).astype(o_ref.dtype)
            lse_ref[:, h:h + 1] = m + jnp.log(l)

    return _pcall(
        body, ride, grid=(t // bq,), in_specs=_fox_specs(t),
        out_specs=[pl.BlockSpec((bq, GROUP_WIDTH), lambda i: (i, 0)), pl.BlockSpec((bq, LANES), lambda i: (i, 0))],
        out_shape=[jax.ShapeDtypeStruct((t, GROUP_WIDTH), ACT_DTYPE), jax.ShapeDtypeStruct((t, LANES), F32)],
        semantics=("parallel",), name="fox_fwd")(pmm, pmm, pmm, cum, cum_t)


def _fox_bwd(pmm, cum, cum_t, lse, dmix, ride=None):
    t = pmm.shape[0]
    bq, nb = FOX_Q_BLOCK, t // FOX_Q_BLOCK
    pairs, per_pair = ATT_HEADS // 2, LANES // HEAD_DIM
    scale = HEAD_DIM ** -0.5

    def body(q_ref, k_ref, v_ref, cum_ref, cum_t_ref, lse_ref, do_ref, dq_ref, dk_ref, dv_ref, dct_ref, dcq_ref):
        g, i = pl.program_id(0), pl.program_id(1)

        @pl.when(i == 0)
        def _():
            dk_ref[...] = jnp.zeros_like(dk_ref)
            dv_ref[...] = jnp.zeros_like(dv_ref)

        @pl.when((i == 0) & (g == 0))
        def _():
            dct_ref[...] = jnp.zeros_like(dct_ref)

        lane = _iota((1, LANES), 1)

        def run(n):
            causal = _iota((bq, n), 1) <= i * bq + _iota((bq, n), 0)
            dcq = jnp.zeros((bq, LANES), F32)
            for hh in range(per_pair):
                h = per_pair * g + hh
                hs = slice(hh * HEAD_DIM, (hh + 1) * HEAD_DIM)
                pick = (lane == h).astype(F32)
                cq = jnp.sum(cum_ref[...] * pick, axis=1, keepdims=True)
                lse_h = jnp.sum(lse_ref[...] * pick, axis=1, keepdims=True)
                key_gate = jnp.concatenate([cum_t_ref[kb, pl.ds(h, 1), :] for kb in range(n // bq)], axis=1)
                s = _dot(q_ref[:, hs], k_ref[:n, hs], 1, 1) * scale + (cq - key_gate)
                p = jnp.exp(jnp.where(causal, s, NEG) - lse_h)
                do = do_ref[:, hs]
                dp = _dot(do, v_ref[:n, hs], 1, 1)
                ds = p * (dp - jnp.sum(p * dp, axis=-1, keepdims=True))
                dq_ref[:, hs] = (_dot(ds, k_ref[:n, hs]) * scale).astype(dq_ref.dtype)
                dk_ref[:n, hs] += _dot(ds, q_ref[:, hs], 0, 0) * scale
                dv_ref[:n, hs] += _dot(p, do, 0, 0)
                key_side = -jnp.sum(ds, axis=0, keepdims=True)
                for kb in range(n // bq):
                    dct_ref[kb, pl.ds(h, 1), :] += key_side[:, kb * bq:(kb + 1) * bq]
                dcq = dcq + jnp.sum(ds, axis=1, keepdims=True) * pick
            dcq_ref[...] = dcq

        for kx in range(t // FOX_KEY_STEP):
            pl.when(i // (FOX_KEY_STEP // bq) == kx)(functools.partial(run, (kx + 1) * FOX_KEY_STEP))

    cols = lambda first: pl.BlockSpec((bq, LANES), lambda g, i: (i, first + g))
    keys = lambda first: pl.BlockSpec((t, LANES), lambda g, i: (0, first + g))
    per_head = pl.BlockSpec((bq, LANES), lambda g, i: (i, 0))
    fox_q, fox_k, fox_v = (GROUP_WIDTH * n // LANES for n in (2, 3, 4))
    return _pcall(
        body, ride, grid=(pairs, t // bq),
        in_specs=[cols(fox_q), keys(fox_k), keys(fox_v), per_head,
                  pl.BlockSpec((nb, ATT_HEADS, bq), lambda g, i: (0, 0, 0)), per_head, cols(GROUP_WIDTH // LANES)],
        out_specs=[cols(0), keys(0), keys(0), pl.BlockSpec((nb, LANES, bq), lambda g, i: (0, 0, 0)),
                   pl.BlockSpec((None, bq, LANES), lambda g, i: (g, i, 0))],
        out_shape=[jax.ShapeDtypeStruct((t, GROUP_WIDTH), ACT_DTYPE), jax.ShapeDtypeStruct((t, GROUP_WIDTH), F32),
                   jax.ShapeDtypeStruct((t, GROUP_WIDTH), F32), jax.ShapeDtypeStruct((nb, LANES, bq), F32),
                   jax.ShapeDtypeStruct((pairs, t, LANES), F32)],
        semantics=("arbitrary", "arbitrary"), name="fox_bwd")(pmm, pmm, pmm, cum, cum_t, lse, dmix)


CA_Q_BLOCK = 4 * CHUNK
CA_WINDOW = CA_Q_BLOCK + CA_LEFT
CA_BASE = 1024


def _ca_bias_base(rel_bias):
    n = rel_bias.shape[0]
    flat = CA_Q_BLOCK + CA_LEFT - REL_CLIP
    tail = CA_BASE - flat - (2 * REL_CLIP + 1)
    return jnp.concatenate([jnp.broadcast_to(rel_bias[:, 2 * REL_CLIP:], (n, flat)), rel_bias[:, ::-1],
                            jnp.broadcast_to(rel_bias[:, :1], (n, tail))], axis=1)


def _ca_bias_base_grad(dbase):
    flat = CA_Q_BLOCK + CA_LEFT - REL_CLIP
    mid = dbase[:, flat:flat + 2 * REL_CLIP + 1][:, ::-1]
    lo = jnp.sum(dbase[:, flat + 2 * REL_CLIP + 1:], axis=1, keepdims=True)
    hi = jnp.sum(dbase[:, :flat], axis=1, keepdims=True)
    pad = jnp.zeros((dbase.shape[0], 2 * REL_CLIP - 1), F32)
    return mid + jnp.concatenate([lo, pad, hi], axis=1)


def _ca_mask(i):
    r, j = _iota((CA_Q_BLOCK, CA_WINDOW), 0), _iota((CA_Q_BLOCK, CA_WINDOW), 1)
    rc, jc = r // CHUNK, j // CHUNK
    return (jc >= rc) & (jc <= rc + CA_LEFT // CHUNK) & (i * CA_Q_BLOCK + j >= CA_LEFT)


def _ca_fill_bias(i, base_ref, bias_scr):
    @pl.when(i == 0)
    def _():
        for h in range(ATT_HEADS):
            rows = jnp.broadcast_to(base_ref[h:h + 1, :], (CA_Q_BLOCK, CA_BASE))
            bias_scr[h] = pltpu.roll(rows, CA_BASE - CA_Q_BLOCK, 1, stride=1, stride_axis=0)[:, :CA_WINDOW]


def _ca_scores(q_ref, kp_ref, bias_scr, win, h, mask):
    hs = slice(h * HEAD_DIM, (h + 1) * HEAD_DIM)
    s = _dot(q_ref[:, hs], kp_ref[win, hs], 1, 1) * (HEAD_DIM ** -0.5)
    return jnp.where(mask, s + bias_scr[h], NEG)


CA_BIAS_SCRATCH = pltpu.VMEM((ATT_HEADS, CA_Q_BLOCK, CA_WINDOW), F32)


def _ca_fwd(pmm, kp, vp, base, ride=None):
    t = pmm.shape[0]

    def body(q_ref, kp_ref, vp_ref, base_ref, o_ref, lse_ref, bias_scr):
        i = pl.program_id(0)
        _ca_fill_bias(i, base_ref, bias_scr)
        win = pl.ds(pl.multiple_of(i * CA_Q_BLOCK, CA_Q_BLOCK), CA_WINDOW)
        mask = _ca_mask(i)
        lse_ref[...] = jnp.zeros_like(lse_ref)
        for h in range(ATT_HEADS):
            hs = slice(h * HEAD_DIM, (h + 1) * HEAD_DIM)
            s = _ca_scores(q_ref, kp_ref, bias_scr, win, h, mask)
            m = jnp.max(s, axis=-1, keepdims=True)
            p = jnp.exp(s - m)
            l = jnp.sum(p, axis=-1, keepdims=True)
            o_ref[:, hs] = (_dot(p, vp_ref[win, hs]) / l).astype(o_ref.dtype)
            lse_ref[:, h:h + 1] = m + jnp.log(l)

    padded = pl.BlockSpec((t + CA_LEFT, GROUP_WIDTH), lambda i: (0, 0))
    return _pcall(
        body, ride, grid=(t // CA_Q_BLOCK,),
        in_specs=[pl.BlockSpec((CA_Q_BLOCK, GROUP_WIDTH), lambda i: (i, 0)), padded, padded,
                  pl.BlockSpec((ATT_HEADS, CA_BASE), lambda i: (0, 0))],
        out_specs=[pl.BlockSpec((CA_Q_BLOCK, GROUP_WIDTH), lambda i: (i, 0)),
                   pl.BlockSpec((CA_Q_BLOCK, LANES), lambda i: (i, 0))],
        out_shape=[jax.ShapeDtypeStruct((t, GROUP_WIDTH), ACT_DTYPE), jax.ShapeDtypeStruct((t, LANES), F32)],
        scratch_shapes=[CA_BIAS_SCRATCH], semantics=("arbitrary",), name="ca_fwd")(pmm, kp, vp, base)


def _ca_bwd(pmm, kp, vp, base, lse, dmix, ride=None):
    t = pmm.shape[0]
    scale = HEAD_DIM ** -0.5

    def body(q_ref, kp_ref, vp_ref, base_ref, lse_ref, do_ref, dq_ref, dkp_ref, dvp_ref, dbase_ref, bias_scr):
        i = pl.program_id(0)
        _ca_fill_bias(i, base_ref, bias_scr)

        @pl.when(i == 0)
        def _():
            dkp_ref[...] = jnp.zeros_like(dkp_ref)
            dvp_ref[...] = jnp.zeros_like(dvp_ref)
            dbase_ref[...] = jnp.zeros_like(dbase_ref)

        win = pl.ds(pl.multiple_of(i * CA_Q_BLOCK, CA_Q_BLOCK), CA_WINDOW)
        mask = _ca_mask(i)
        flip = (_iota((CA_Q_BLOCK, CA_Q_BLOCK), 0) + _iota((CA_Q_BLOCK, CA_Q_BLOCK), 1) == CA_Q_BLOCK - 1).astype(F32)
        for h in range(ATT_HEADS):
            hs = slice(h * HEAD_DIM, (h + 1) * HEAD_DIM)
            s = _ca_scores(q_ref, kp_ref, bias_scr, win, h, mask)
            p = jnp.exp(s - lse_ref[:, h:h + 1])
            do = do_ref[:, hs]
            dp = _dot(do, vp_ref[win, hs], 1, 1)
            ds = p * (dp - jnp.sum(p * dp, axis=-1, keepdims=True))
            dq_ref[:, hs] = (_dot(ds, kp_ref[win, hs]) * scale).astype(dq_ref.dtype)
            dkp_ref[win, hs] += _dot(ds, q_ref[:, hs], 0, 0) * scale
            dvp_ref[win, hs] += _dot(p, do, 0, 0)
            rev = jnp.concatenate([_dot(flip, ds), jnp.zeros((CA_Q_BLOCK, CA_BASE - CA_WINDOW), F32)], axis=1)
            lined = pltpu.roll(rev, 1, 1, stride=1, stride_axis=0)
            dbase_ref[h:h + 1, :] += jnp.sum(lined, axis=0, keepdims=True)

    padded = pl.BlockSpec((t + CA_LEFT, GROUP_WIDTH), lambda i: (0, 0))
    return _pcall(
        body, ride, grid=(t // CA_Q_BLOCK,),
        in_specs=[pl.BlockSpec((CA_Q_BLOCK, GROUP_WIDTH), lambda i: (i, 0)), padded, padded,
                  pl.BlockSpec((ATT_HEADS, CA_BASE), lambda i: (0, 0)),
                  pl.BlockSpec((CA_Q_BLOCK, LANES), lambda i: (i, 0)),
                  pl.BlockSpec((CA_Q_BLOCK, GROUP_WIDTH), lambda i: (i, 0))],
        out_specs=[pl.BlockSpec((CA_Q_BLOCK, GROUP_WIDTH), lambda i: (i, 0)), padded, padded,
                   pl.BlockSpec((ATT_HEADS, CA_BASE), lambda i: (0, 0))],
        out_shape=[jax.ShapeDtypeStruct((t, GROUP_WIDTH), ACT_DTYPE),
                   jax.ShapeDtypeStruct((t + CA_LEFT, GROUP_WIDTH), F32),
                   jax.ShapeDtypeStruct((t + CA_LEFT, GROUP_WIDTH), F32),
                   jax.ShapeDtypeStruct((ATT_HEADS, CA_BASE), F32)],
        scratch_shapes=[CA_BIAS_SCRATCH], semantics=("arbitrary",), name="ca_bwd")(pmm, kp, vp, base, lse, dmix)


GELU_C = 0.7978845608028654
GELU_A = 0.044715


def _shift_down(v, k, fill):
    return jnp.where(_iota(v.shape, 0) >= k, pltpu.roll(v, k, 0), fill)


def _shift_up(v, k, fill):
    t = v.shape[0]
    return jnp.where(_iota(v.shape, 0) < t - k, pltpu.roll(v, t - k, 0), fill)


def _linear_scan(a, b, shift):
    k = 1
    while k < a.shape[0]:
        b = a * shift(b, k, 0.0) + b
        a = a * shift(a, k, 1.0)
        k *= 2
    return b


def _neg_expm1(y):
    series = -y * (1.0 + y * (0.5 + y * (1.0 / 6.0 + y * (1.0 / 24.0 + y * (1.0 / 120.0)))))
    return jnp.where(y > -0.1, series, 1.0 - jnp.exp(y))


def _lru_forward(x, g_in, cw, cb, wa, ba, wx, bx, lam):
    xs = [_shift_down(x, CONV_WIDTH - 1 - j, 0.0) for j in range(CONV_WIDTH - 1)] + [x]
    xc = cb + sum(cw[j:j + 1, :] * xs[j] for j in range(CONV_WIDTH))
    r = jax.nn.sigmoid(_dot(xc, wa) + ba)
    i = jax.nn.sigmoid(_dot(xc, wx) + bx)
    lsl = _log_sigmoid(lam)
    la = LRU_C * r * lsl
    a = jnp.exp(la)
    s = jnp.sqrt(_neg_expm1(2.0 * la))
    h = _linear_scan(a, s * (i * xc), _shift_down)
    u = GELU_C * (g_in + GELU_A * g_in * g_in * g_in)
    th = jnp.tanh(u)
    gelu = 0.5 * g_in * (1.0 + th)
    return xs, xc, r, i, lsl, a, s, h, th, gelu


def _lru_specs(t):
    col = lambda off: pl.BlockSpec((t, LANES), lambda j: (0, j + off))
    vec = pl.BlockSpec((1, LANES), lambda j: (0, j))
    mat = pl.BlockSpec((None, LANES, LANES), lambda j: (j, 0, 0))
    return [col(0), col(GROUP_WIDTH // LANES), pl.BlockSpec((CONV_WIDTH, LANES), lambda j: (0, j)),
            vec, mat, vec, mat, vec, vec]


def _lru_fwd(pel, conv_w, conv_b, wa, ba, wx, bx, lam, ride=None):
    t = pel.shape[0]

    def body(g_ref, x_ref, cw_ref, cb_ref, wa_ref, ba_ref, wx_ref, bx_ref, lam_ref, o_ref):
        res = _lru_forward(x_ref[...], g_ref[...], cw_ref[...], cb_ref[...], wa_ref[...], ba_ref[...],
                           wx_ref[...], bx_ref[...], lam_ref[...])
        o_ref[...] = (res[7] * res[9]).astype(o_ref.dtype)

    return _pcall(
        body, ride, grid=(GROUP_WIDTH // LANES,), in_specs=_lru_specs(t),
        out_specs=pl.BlockSpec((t, LANES), lambda j: (0, j)),
        out_shape=jax.ShapeDtypeStruct((t, GROUP_WIDTH), ACT_DTYPE),
        semantics=("parallel",), name="lru_fwd")(pel, pel, conv_w, conv_b, wa, ba, wx, bx, lam)


def _lru_bwd(pel, conv_w, conv_b, wa, ba, wx, bx, lam, dmix, ride=None):
    t = pel.shape[0]

    def body(g_ref, x_ref, cw_ref, cb_ref, wa_ref, ba_ref, wx_ref, bx_ref, lam_ref, do_ref,
             dg_ref, dx_ref, dcw_ref, dcb_ref, dwa_ref, dba_ref, dwx_ref, dbx_ref, dlam_ref):
        g_in, cw, lam = g_ref[...], cw_ref[...], lam_ref[...]
        xs, xc, r, i, lsl, a, s, h, th, gelu = _lru_forward(
            x_ref[...], g_in, cw, cb_ref[...], wa_ref[...], ba_ref[...], wx_ref[...], bx_ref[...], lam)
        dout = do_ref[...]
        dgelu = 0.5 * (1.0 + th) + 0.5 * g_in * (1.0 - th * th) * GELU_C * (1.0 + 3.0 * GELU_A * g_in * g_in)
        dg_ref[...] = (dout * h * dgelu).astype(dg_ref.dtype)
        gsum = _linear_scan(_shift_up(a, 1, 0.0), dout * gelu, _shift_up)
        da = gsum * _shift_down(h, 1, 0.0)
        di = gsum * s * xc
        dla = da * a - gsum * (i * xc) * (a * a / s)
        dlam_ref[...] = jnp.sum(dla * (LRU_C * r), axis=0, keepdims=True) * jax.nn.sigmoid(-lam)
        dpr = dla * (LRU_C * lsl) * r * (1.0 - r)
        dpi = di * i * (1.0 - i)
        dxc = gsum * s * i + _dot(dpr, wa_ref[...], 1, 1) + _dot(dpi, wx_ref[...], 1, 1)
        xct = xc.T
        dwa_ref[...] = _dot(xct, dpr)
        dwx_ref[...] = _dot(xct, dpi)
        dba_ref[...] = jnp.sum(dpr, axis=0, keepdims=True)
        dbx_ref[...] = jnp.sum(dpi, axis=0, keepdims=True)
        dcb_ref[...] = jnp.sum(dxc, axis=0, keepdims=True)
        for j in range(CONV_WIDTH):
            dcw_ref[j:j + 1, :] = jnp.sum(dxc * xs[j], axis=0, keepdims=True)
        dx = cw[CONV_WIDTH - 1:CONV_WIDTH, :] * dxc
        for j in range(CONV_WIDTH - 1):
            dx = dx + cw[j:j + 1, :] * _shift_up(dxc, CONV_WIDTH - 1 - j, 0.0)
        dx_ref[...] = dx.astype(dx_ref.dtype)

    col = pl.BlockSpec((t, LANES), lambda j: (0, j))
    vec = pl.BlockSpec((1, LANES), lambda j: (0, j))
    mat = pl.BlockSpec((None, LANES, LANES), lambda j: (j, 0, 0))
    nb = GROUP_WIDTH // LANES
    vshape = jax.ShapeDtypeStruct((1, GROUP_WIDTH), F32)
    mshape = jax.ShapeDtypeStruct((nb, LANES, LANES), F32)
    return _pcall(
        body, ride, grid=(nb,),
        in_specs=_lru_specs(t) + [pl.BlockSpec((t, LANES), lambda j: (0, j + nb))],
        out_specs=[col, col, pl.BlockSpec((CONV_WIDTH, LANES), lambda j: (0, j)), vec, mat, vec, mat, vec, vec],
        out_shape=[jax.ShapeDtypeStruct((t, GROUP_WIDTH), ACT_DTYPE), jax.ShapeDtypeStruct((t, GROUP_WIDTH), ACT_DTYPE),
                   jax.ShapeDtypeStruct((CONV_WIDTH, GROUP_WIDTH), F32), vshape, mshape, vshape, mshape, vshape, vshape],
        semantics=("parallel",), name="lru_bwd")(
            pel, pel, conv_w, conv_b, wa, ba, wx, bx, lam, dmix)


def _block_diag_pairs(w):
    z = jnp.zeros((LRU_BLOCK_DIM, LRU_BLOCK_DIM), w.dtype)
    return jnp.stack([jnp.block([[w[2 * j], z], [z, w[2 * j + 1]]]) for j in range(w.shape[0] // 2)])


def _block_diag_pairs_grad(dw):
    b = LRU_BLOCK_DIM
    return jnp.stack([dw[n // 2, (n % 2) * b:(n % 2 + 1) * b, (n % 2) * b:(n % 2 + 1) * b] for n in range(2 * dw.shape[0])])


def _row_tile(r):
    return ROW_TILE if r % ROW_TILE == 0 else r


def _pair_sum(g, got, place, name):
    _, r, c = g.shape
    tile = r

    def body(place_ref, a_ref, b_ref, o_ref):
        o_ref[...] = (a_ref[...].astype(F32) + b_ref[...].astype(F32)).astype(o_ref.dtype)

    blk = pl.BlockSpec((1, tile, c), lambda k, i, place_ref: (k, i, 0))
    return pl.pallas_call(
        body,
        grid_spec=pltpu.PrefetchScalarGridSpec(
            num_scalar_prefetch=1, grid=(N_CHIPS, r // tile),
            in_specs=[pl.BlockSpec((1, tile, c), lambda k, i, place_ref: (2 * k + place_ref[0], i, 0)), blk],
            out_specs=blk),
        out_shape=jax.ShapeDtypeStruct(got.shape, got.dtype),
        compiler_params=_params("parallel", "parallel"), name=name)(place, g, got)


def _adamw_update(g, w_ref, m_ref, v_ref, g_ref, d_ref, nm_ref, nv_ref):
    nm = ADAM_B1 * m_ref[...] + (1.0 - ADAM_B1) * g
    nv = ADAM_B2 * v_ref[...] + (1.0 - ADAM_B2) * jnp.square(g)
    m_hat = nm / (1.0 - ADAM_B1 ** ADAM_STEP)
    v_hat = nv / (1.0 - ADAM_B2 ** ADAM_STEP)
    g_ref[...] = g
    d_ref[...] = -ADAM_LR * (m_hat / (jnp.sqrt(v_hat) + ADAM_EPS) + ADAM_WD * w_ref[...])
    nm_ref[...] = nm
    nv_ref[...] = nv


def _adamw_sharded(parts, w, m, v, place, name):
    n_layers, r, c = w.shape
    tile = _row_tile(r)
    nb = r // tile

    def body(place_ref, *refs):
        layer = pl.program_id(0)
        g = None
        for l in range(n_layers):
            s_ref, r_ref = refs[2 * l], refs[2 * l + 1]
            g_l = s_ref[0].astype(F32) + r_ref[0].astype(F32) + r_ref[1].astype(F32) + r_ref[2].astype(F32)
            g = g_l if g is None else jnp.where(layer == l, g_l, g)
        _adamw_update(g, *refs[2 * n_layers:])

    def part_specs(l):
        rows = lambda q, i: jnp.where(q < l, 0, jnp.where(q > l, nb - 1, i))
        return [pl.BlockSpec((1, tile, c), lambda q, i, place_ref: (place_ref[1], rows(q, i), 0)),
                pl.BlockSpec((3, tile, c), lambda q, i, place_ref: (0, rows(q, i), 0))]

    in_specs, args = [], []
    for l, (s, recv) in enumerate(parts):
        in_specs += part_specs(l)
        args += [s, recv]
    blk = pl.BlockSpec((None, tile, c), lambda q, i, place_ref: (q, i, 0))
    out = jax.ShapeDtypeStruct((n_layers, r, c), F32)
    return pl.pallas_call(
        body,
        grid_spec=pltpu.PrefetchScalarGridSpec(
            num_scalar_prefetch=1, grid=(n_layers, nb), in_specs=in_specs + [blk, blk, blk],
            out_specs=[blk, blk, blk, blk]),
        out_shape=[out, out, out, out], compiler_params=_params("arbitrary", "arbitrary"), name=name)(
            place, *args, w, m, v)


def _adamw_small(repl_parts, vec_parts, w, m, v, place):
    n_r, n = len(repl_parts), len(w)
    shapes = [a.shape for a in w]

    def body(place_ref, *refs):
        parts, rest = refs[:n], refs[n:]
        for k in range(n):
            take = (lambda p: parts[k][p]) if k < n_r else (lambda p: parts[k][p, 0])
            g = take(0)
            for p in range(1, N_DEV):
                g = g + take(p)
            _adamw_update(g, rest[k], rest[n + k], rest[2 * n + k], *rest[3 * n + 4 * k:3 * n + 4 * k + 4])

    def whole(shape):
        return pl.BlockSpec(shape, lambda i, place_ref: (0,) * len(shape))

    def mine(shard):
        return pl.BlockSpec((N_DEV, 1) + shard, lambda i, place_ref: (0, place_ref[2]) + (0,) * len(shard))

    in_specs = [whole(a.shape) for a in repl_parts] + [mine(s) for s in shapes[n_r:]] + [whole(s) for s in shapes] * 3
    outs = pl.pallas_call(
        body,
        grid_spec=pltpu.PrefetchScalarGridSpec(
            num_scalar_prefetch=1, grid=(1,), in_specs=in_specs,
            out_specs=[whole(s) for s in shapes for _ in range(4)]),
        out_shape=[jax.ShapeDtypeStruct(s, F32) for s in shapes for _ in range(4)],
        compiler_params=_params("arbitrary"), name="adamw_small")(place, *repl_parts, *vec_parts, *w, *m, *v)
    return [outs[4 * k:4 * k + 4] for k in range(n)]


SHARDED = {"norm_w": 2, "w_in_even": 2, "gla_w_a_up": 2, "w_out_even": 1, "w_in_odd": 2, "conv_w": 2, "conv_b": 1,
           "lru_b_a": 1, "lru_b_x": 1, "lru_lambda": 1, "w_out_odd": 1, "w_mlp_up": 2, "w_mlp_down": 1}
REPLICATED = ["gla_b_a", "gla_norm_w", "fox_b_f", "rel_bias", "lru_w_a", "lru_w_x"]
WEIGHTS = ["norm_w", "w_in_even", "gla_w_a_up", "gla_b_a", "gla_norm_w", "fox_b_f", "w_out_even", "w_in_odd",
           "rel_bias", "conv_w", "conv_b", "lru_w_a", "lru_b_a", "lru_w_x", "lru_b_x", "lru_lambda", "w_out_odd",
           "w_mlp_up", "w_mlp_down"]
MATRICES = ("w_in_even", "w_out_even", "w_in_odd", "w_out_odd", "w_mlp_up", "w_mlp_down")
TRANSPOSED = ("w_in_even", "w_in_odd")
VECTORS = tuple(n for n in SHARDED if n not in MATRICES)
MATRIX_BLOCKS = (("w_in_even", 0), ("w_out_even", 0), ("w_in_odd", 0), ("w_out_odd", 0),
                 ("w_mlp_up", 0), ("w_mlp_up", 1), ("w_mlp_down", 0), ("w_mlp_down", 1))


def _join_shards(blocks, axis):
    moved = jnp.moveaxis(blocks, 0, axis)
    shape = moved.shape
    return moved.reshape(shape[:axis] + (shape[axis] * shape[axis + 1],) + shape[axis + 2:])


def _split_shards(full, axis):
    shape = full.shape
    cut = full.reshape(shape[:axis] + (N_DEV, shape[axis] // N_DEV) + shape[axis + 1:])
    return jnp.moveaxis(cut, axis, 0)


EVEN_SPLITS = (0, 256, 512, 1024, 1536, 1552, 2064, 2576, 3088, 3096)


def _even_in_split(wt):
    c = [wt[EVEN_SPLITS[k]:EVEN_SPLITS[k + 1]] for k in range(9)]
    gq, gk, gv, gr, ga, fq, fk, fv, ff = c
    padrows = lambda a: jnp.pad(a, ((0, LANES - a.shape[0]), (0, 0)))
    return jnp.concatenate([gq, gk, gv, fq, fk, fv], axis=0), jnp.concatenate([gr, padrows(ga), padrows(ff)], axis=0)


def _even_in_merge(dmm, dele):
    return jnp.concatenate([dmm[:1024], dele[:512], dele[512:512 + GLA_RANK], dmm[1024:2560],
                            dele[640:640 + ATT_HEADS]], axis=0)


def _forward_backward(x, target, shard, vec_shard, w, place):
    w = dict(w)
    g, dnorm, sums, recv = {}, {}, {}, {}
    nrm = lambda l, k: w["norm_w"][l, k][None, :]
    gather = lambda *keys: _gather_plan([shard[k] for k in keys])
    blocks = lambda r, c: (N_DEV, r // N_DEV, c)

    def pair_sum(key):
        sums[key] = _pair_sum(g[key], got[key], place, f"rs_pair_sum_{key[0]}_{key[1]}")

    got = {}

    def mlp_fwd(xin, layer, ride_up, ride_down):
        up = _mm(xin, w["w_mlp_up"][layer], out_dtype=ACT_DTYPE, tm=TM_FWD, tn=D_FF // N_DEV, b_blocked=True,
                 a_norm=nrm(layer, 2), name=f"mlp_up_{layer}", ride=ride_up)
        (u, h), rode_up = up if ride_up is not None else (up, None)
        down = _mm(u, w["w_mlp_down"][layer], out_dtype=F32, tm=TM_DX // 2, tn=D_MODEL, a_sqrelu=True,
                   res_norm=(xin, nrm(layer, 3)), name=f"mlp_down_{layer}", ride=ride_down)
        (yv, xout), rode_down = down if ride_down is not None else (down, None)
        return xout, (xin, h, u, yv), rode_up, rode_down

    def mlp_bwd(dxout, saved, layer, ride):
        xin, h, u, yv = saved
        k_up, k_down = ("w_mlp_up", layer), ("w_mlp_down", layer)
        res = _mm(dxout, w["w_mlp_down"][layer], nt=True, out_dtype=ACT_DTYPE, tm=TM_DX, tn=TN, drelu_of=u,
                  a_norm_bwd=(yv, nrm(layer, 3)), name=f"mlp_down_dx_{layer}", ride=ride)
        (du, dy, dnorm[(layer, 3)]), rode = res if ride is not None else (res, None)
        g[k_down] = _mm(u, dy, ta=True, out_dtype=WIRE_DTYPE, tm=TM_DW, tn=TN, a_sqrelu=True,
                        name=f"mlp_down_dw_{layer}").reshape(blocks(D_FF, D_MODEL))
        g[k_up] = _mm(h, du, ta=True, out_dtype=WIRE_DTYPE, tm=TM_DW, tn=D_FF // N_DEV, out_blocked=True,
                      name=f"mlp_up_dw_{layer}")
        w_up = jnp.moveaxis(w["w_mlp_up"][layer], 0, 1).reshape(D_MODEL, D_FF)
        (dxin, dnorm[(layer, 2)]), (got[k_down], got[k_up]) = _mm(
            du, w_up, nt=True, out_dtype=F32, tm=TM_DX // 2, tn=D_MODEL, norm_bwd=(xin, nrm(layer, 2), dxout),
            name=f"mlp_up_dx_{layer}", ride=_sibling_plan([g[k_down], g[k_up]]))
        pair_sum(k_down)
        pair_sum(k_up)
        return dxin, rode

    first = _run_plan(_gather_plan([shard[("w_in_even", 0)]] + [vec_shard[n] for n in VECTORS]),
                      "weights_all_gather_first")
    w["w_in_even"] = first[0].reshape(-1, D_MODEL)
    for n, b in zip(VECTORS, first[1:]):
        w[n] = _join_shards(b, SHARDED[n])
    w["w_mlp_up"], w["w_mlp_down"] = [None] * DEPTH, [None] * DEPTH

    wmm_e, wel_e = _even_in_split(w["w_in_even"])
    w_up_pad = jnp.pad(w["gla_w_a_up"][0], ((0, LANES - GLA_RANK), (0, 0)))
    b_f_pad = jnp.pad(w["fox_b_f"], ((0, 0), (0, LANES - ATT_HEADS)))
    (pmm0, h0), (w_out_even,) = _mm(x, wmm_e, nt=True, out_dtype=ACT_DTYPE, tm=TM_FWD, tn=TN, a_norm=nrm(0, 0),
                                    name="in_even_mm", ride=gather(("w_out_even", 0)))
    pel0 = _mm(h0, wel_e, nt=True, out_dtype=F32, tm=TM_FWD, tn=768, name="in_even_el")
    (out_a, states), (w_in_odd,) = _gla_fwd(pmm0, pel0, w_up_pad, w["gla_b_a"], w["gla_norm_w"],
                                            ride=gather(("w_in_odd", 0)))
    cum, cum_t = _fox_gate_fwd(pel0, b_f_pad)
    (out_b, lse_b), (w["w_mlp_up"][0], w_mlp_down0) = _fox_fwd(pmm0, cum, cum_t,
                                                               ride=gather(("w_mlp_up", 0), ("w_mlp_down", 0)))
    w["w_out_even"] = w_out_even.reshape(D_MODEL, D_MODEL)
    w["w_mlp_down"][0] = w_mlp_down0.reshape(D_FF, D_MODEL)
    mix_in0 = jnp.concatenate([out_a, out_b], axis=1)
    mix0, x1 = _mm(mix_in0, w["w_out_even"], out_dtype=F32, tm=TM_DX, tn=D_MODEL, res_norm=(x, nrm(0, 1)),
                   name="out_even")
    x2, mlp0, _, (w["w_mlp_up"][1],) = mlp_fwd(x1, 0, None, gather(("w_mlp_up", 1)))
    w["w_in_odd"] = w_in_odd.reshape(-1, D_MODEL)

    w_in_o = w["w_in_odd"]
    n_mm_o = 3 * GROUP_WIDTH
    wa_bd, wx_bd = _block_diag_pairs(w["lru_w_a"][0]), _block_diag_pairs(w["lru_w_x"][0])
    base = _ca_bias_base(w["rel_bias"][0])
    pmm1, h1 = _mm(x2, w_in_o[:n_mm_o], nt=True, out_dtype=ACT_DTYPE, tm=TM_FWD, tn=TN, a_norm=nrm(1, 0),
                   name="in_odd_mm")
    pel1 = _mm(h1, w_in_o[n_mm_o:], nt=True, out_dtype=F32, tm=TM_FWD, tn=TN, name="in_odd_el")
    kp = jnp.pad(pmm1[:, GROUP_WIDTH:2 * GROUP_WIDTH], ((CA_LEFT, 0), (0, 0)))
    vp = jnp.pad(pmm1[:, 2 * GROUP_WIDTH:], ((CA_LEFT, 0), (0, 0)))
    (out_c, lse_c), (w_mlp_down1,) = _ca_fwd(pmm1, kp, vp, base, ride=gather(("w_mlp_down", 1)))
    w["w_mlp_down"][1] = w_mlp_down1.reshape(D_FF, D_MODEL)
    lru_args = (pel1, w["conv_w"][0], w["conv_b"], wa_bd, w["lru_b_a"], wx_bd, w["lru_b_x"], w["lru_lambda"])
    out_d, (w_out_odd,) = _lru_fwd(*lru_args, ride=gather(("w_out_odd", 0)))
    w["w_out_odd"] = w_out_odd.reshape(D_MODEL, D_MODEL)
    mix_in1 = jnp.concatenate([out_c, out_d], axis=1)
    mix1, x3 = _mm(mix_in1, w["w_out_odd"], out_dtype=F32, tm=TM_DX, tn=D_MODEL, res_norm=(x2, nrm(1, 1)),
                   name="out_odd")
    x4, mlp1, _, _ = mlp_fwd(x3, 1, None, None)

    loss, dx4 = _loss_fwd_bwd(x4, target)

    k_oo, k_io, k_oe, k_ie = ("w_out_odd", 0), ("w_in_odd", 0), ("w_out_even", 0), ("w_in_even", 0)
    mlp_keys = lambda l: [("w_mlp_down", l), ("w_mlp_up", l)]
    dx3, _ = mlp_bwd(dx4, mlp1, 1, None)
    dmix_in1, dmix1, dnorm[(1, 1)] = _mm(dx3, w["w_out_odd"], nt=True, out_dtype=F32, tm=TM_DX, tn=TN,
                                         a_norm_bwd=(mix1, nrm(1, 1)), name="out_odd_dx")
    g[k_oo] = _mm(mix_in1, dmix1, ta=True, out_dtype=WIRE_DTYPE, tm=TM_DW, tn=TN, name="out_odd_dw").reshape(
        blocks(D_MODEL, D_MODEL))
    (dq_c, dkp, dvp, dbase), rode = _ca_bwd(
        pmm1, kp, vp, base, lse_c, dmix_in1,
        ride=_join_plans(_chip_plan([sums[k] for k in mlp_keys(1)]), _sibling_plan([g[k_oo]])))
    recv.update(zip(mlp_keys(1), rode[:2]))
    got[k_oo] = rode[2]
    pair_sum(k_oo)
    (dgate, dxin, g_conv_w, g_conv_b, dwa_bd, g_lru_b_a, dwx_bd, g_lru_b_x, g_lru_lambda), (recv[k_oo],) = _lru_bwd(
        *lru_args, dmix_in1, ride=_chip_plan([sums[k_oo]]))
    dp1 = jnp.concatenate([dq_c, dkp[CA_LEFT:].astype(ACT_DTYPE), dvp[CA_LEFT:].astype(ACT_DTYPE), dgate, dxin], axis=1)
    g[k_io] = _mm(dp1, h1, ta=True, out_dtype=WIRE_DTYPE, tm=dp1.shape[1] // 2, tn=TN, name="in_odd_dw").reshape(
        blocks(dp1.shape[1], D_MODEL))
    (dx2, dnorm[(1, 0)]), (got[k_io],) = _mm(dp1, w_in_o, out_dtype=F32, tm=TM_DX // 2, tn=D_MODEL,
                                             norm_bwd=(x2, nrm(1, 0), dx3), name="in_odd_dx",
                                             ride=_sibling_plan([g[k_io]]))
    pair_sum(k_io)
    g["rel_bias"] = _ca_bias_base_grad(dbase)[None]
    g["conv_w"], g["conv_b"] = g_conv_w[None], g_conv_b
    g["lru_w_a"], g["lru_w_x"] = _block_diag_pairs_grad(dwa_bd)[None], _block_diag_pairs_grad(dwx_bd)[None]
    g["lru_b_a"], g["lru_b_x"], g["lru_lambda"] = g_lru_b_a, g_lru_b_x, g_lru_lambda

    dx1, (recv[k_io],) = mlp_bwd(dx2, mlp0, 0, _chip_plan([sums[k_io]]))
    dmix_in0, dmix0, dnorm[(0, 1)] = _mm(dx1, w["w_out_even"], nt=True, out_dtype=F32, tm=TM_DX, tn=TN,
                                         a_norm_bwd=(mix0, nrm(0, 1)), name="out_even_dx")
    g[k_oe] = _mm(mix_in0, dmix0, ta=True, out_dtype=WIRE_DTYPE, tm=TM_DW, tn=TN, name="out_even_dw").reshape(
        blocks(D_MODEL, D_MODEL))
    k_md0, k_mu0 = mlp_keys(0)
    (dq_a, dk_a, dv_a, dr_a, da_a, dw_up_pad, g_gla_b_a, g_gla_norm_w), (got[k_oe], recv[k_md0]) = _gla_bwd(
        pmm0, pel0, w_up_pad, w["gla_b_a"], w["gla_norm_w"], states, dmix_in0,
        ride=_join_plans(_sibling_plan([g[k_oe]]), _chip_plan([sums[k_md0]])))
    pair_sum(k_oe)
    (dq_b, dk_b, dv_b, dcum_t, dcum_q), (recv[k_mu0], recv[k_oe]) = _fox_bwd(
        pmm0, cum, cum_t, lse_b, dmix_in0, ride=_chip_plan([sums[k_mu0], sums[k_oe]]))
    df_b, db_f = _fox_gate_bwd(pel0, b_f_pad, dcum_t, dcum_q)
    g["gla_w_a_up"] = dw_up_pad[:GLA_RANK][None]
    g["gla_b_a"], g["gla_norm_w"], g["fox_b_f"] = g_gla_b_a, g_gla_norm_w, db_f[:, :ATT_HEADS]
    dp0 = jnp.concatenate([dq_a, dk_a, dv_a, dq_b, dk_b.astype(ACT_DTYPE), dv_b.astype(ACT_DTYPE), dr_a, da_a, df_b],
                          axis=1)
    w_perm = jnp.concatenate([wmm_e, wel_e], axis=0)
    n_mm_e = wmm_e.shape[0]
    dw_perm, repl_parts = _mm(dp0, h0, ta=True, out_dtype=WIRE_DTYPE, tm=dp0.shape[1] // 2, tn=TN, name="in_even_dw",
                              ride=_gather_plan([g[n] for n in REPLICATED]))
    dw_even = _even_in_merge(dw_perm[:n_mm_e], dw_perm[n_mm_e:])
    g[k_ie] = dw_even.reshape(blocks(dw_even.shape[0], D_MODEL))
    dh0, (got[k_ie],) = _mm(dp0, w_perm, out_dtype=F32, tm=TM_DX, tn=TN, name="in_even_dx",
                            ride=_sibling_plan([g[k_ie]]))
    pair_sum(k_ie)
    (dx0, dnorm[(0, 0)]), (recv[k_ie],) = _norm_bwd(dh0, x, nrm(0, 0), out_dtype=F32, add=dx1, name="norm_in_bwd_0",
                                                     ride=_chip_plan([sums[k_ie]]))

    g["norm_w"] = jnp.stack([jnp.concatenate([dnorm[(l, k)] for k in range(4)], axis=0) for l in range(DEPTH)])
    vec_parts = _run_plan(_gather_plan([_split_shards(g[n], SHARDED[n]) for n in VECTORS]), "vector_grads_all_gather")
    return loss, dx0, sums, recv, repl_parts, vec_parts


def kernel(x, norm_w, w_in_even, gla_w_a_up, gla_b_a, gla_norm_w, fox_b_f, w_out_even, w_in_odd, rel_bias, conv_w, conv_b, lru_w_a, lru_b_a, lru_w_x, lru_b_x, lru_lambda, w_out_odd, w_mlp_up, w_mlp_down, loss_target, m_norm_w, m_w_in_even, m_gla_w_a_up, m_gla_b_a, m_gla_norm_w, m_fox_b_f, m_w_out_even, m_w_in_odd, m_rel_bias, m_conv_w, m_conv_b, m_lru_w_a, m_lru_b_a, m_lru_w_x, m_lru_b_x, m_lru_lambda, m_w_out_odd, m_w_mlp_up, m_w_mlp_down, v_norm_w, v_w_in_even, v_gla_w_a_up, v_gla_b_a, v_gla_norm_w, v_fox_b_f, v_w_out_even, v_w_in_odd, v_rel_bias, v_conv_w, v_conv_b, v_lru_w_a, v_lru_b_a, v_lru_w_x, v_lru_b_x, v_lru_lambda, v_w_out_odd, v_w_mlp_up, v_w_mlp_down):
    wts = dict(zip(WEIGHTS, (norm_w, w_in_even, gla_w_a_up, gla_b_a, gla_norm_w, fox_b_f, w_out_even, w_in_odd, rel_bias,
                             conv_w, conv_b, lru_w_a, lru_b_a, lru_w_x, lru_b_x, lru_lambda, w_out_odd, w_mlp_up,
                             w_mlp_down)))
    mom = dict(zip(WEIGHTS, (m_norm_w, m_w_in_even, m_gla_w_a_up, m_gla_b_a, m_gla_norm_w, m_fox_b_f, m_w_out_even,
                             m_w_in_odd, m_rel_bias, m_conv_w, m_conv_b, m_lru_w_a, m_lru_b_a, m_lru_w_x, m_lru_b_x,
                             m_lru_lambda, m_w_out_odd, m_w_mlp_up, m_w_mlp_down)))
    var = dict(zip(WEIGHTS, (v_norm_w, v_w_in_even, v_gla_w_a_up, v_gla_b_a, v_gla_norm_w, v_fox_b_f, v_w_out_even,
                             v_w_in_odd, v_rel_bias, v_conv_w, v_conv_b, v_lru_w_a, v_lru_b_a, v_lru_w_x, v_lru_b_x,
                             v_lru_lambda, v_w_out_odd, v_w_mlp_up, v_w_mlp_down)))
    ax, ay, ac = lax.axis_index("x"), lax.axis_index("y"), lax.axis_index("c")
    place = jnp.stack([ac, 2 * ax + ay, 4 * ax + 2 * ay + ac]).astype(jnp.int32)

    shard = {(n, l): (wts[n][l].T if n in TRANSPOSED else wts[n][l]).astype(WIRE_DTYPE) for n, l in MATRIX_BLOCKS}
    loss_blk, dx, sums, recv, repl_parts, vec_parts = _forward_backward(
        x[0], loss_target[0], shard, {n: wts[n] for n in VECTORS}, {n: wts[n] for n in REPLICATED}, place)
    loss = lax.psum(loss_blk[0, 0], ("x", "y", "c"))

    view = lambda n, a: jnp.swapaxes(a, 1, 2) if n in TRANSPOSED else a
    upd = {n: [view(n, o) for o in _adamw_sharded(
        [(sums[(n, l)], recv[(n, l)]) for l in range(wts[n].shape[0])], view(n, wts[n]), view(n, mom[n]), view(n, var[n]),
        place, f"adamw_{n}")] for n in MATRICES}
    small = REPLICATED + list(VECTORS)
    upd.update(zip(small, _adamw_small(repl_parts, vec_parts, [wts[n] for n in small], [mom[n] for n in small],
                                       [var[n] for n in small], place)))
    return (loss, dx[None], *[upd[n][kind] for kind in range(4) for n in WEIGHTS])
```

```python
import functools
from typing import Callable, NamedTuple, Optional

import jax
import jax.numpy as jnp
from jax import lax
from jax.experimental import pallas as pl
from jax.experimental.pallas import tpu as pltpu

F32 = jnp.float32
MXU_DTYPE = jnp.bfloat16
ACT_DTYPE = jnp.bfloat16
WIRE_DTYPE = jnp.bfloat16

V7X_VMEM_BYTES = 64 * 1024 * 1024
VMEM_LIMIT = (V7X_VMEM_BYTES * 7) // 8
LANES = 128

D_MODEL = 1024
SEQ = 2048
DEPTH = 2
CHUNK = 64
GROUP_WIDTH = D_MODEL // 2
D_FF = 4 * D_MODEL
NORM_EPS = 1e-6
GLA_HEADS = 4
GLA_DV = GROUP_WIDTH // GLA_HEADS
GLA_DK = GLA_DV // 2
GLA_KW = GLA_HEADS * GLA_DK
GLA_RANK = 16
GLA_GATE_TAU = 16.0
HEAD_DIM = 64
ATT_HEADS = GROUP_WIDTH // HEAD_DIM
CA_LEFT = 8 * CHUNK
REL_CLIP = 128
LRU_BLOCK_DIM = 64
CONV_WIDTH = 4
LRU_C = 8.0
N_DEV = 8

ADAM_LR = 0.001
ADAM_B1 = 0.9
ADAM_B2 = 0.999
ADAM_EPS = 1e-08
ADAM_WD = 0.01
ADAM_STEP = 10

NEG = float(jnp.finfo(jnp.float32).min)
MESH = pl.DeviceIdType.MESH


def _params(*sem):
    return pltpu.CompilerParams(dimension_semantics=sem, vmem_limit_bytes=VMEM_LIMIT)


def _dot(a, b, ca=1, cb=0):
    return lax.dot_general(a.astype(MXU_DTYPE), b.astype(MXU_DTYPE), (((ca,), (cb,)), ((), ())),
                           preferred_element_type=F32)


def _dot_exact(a, b):
    return lax.dot_general(a, b, (((1,), (0,)), ((), ())), precision=lax.Precision.HIGHEST,
                           preferred_element_type=F32)


def _log_sigmoid(x):
    return jnp.minimum(x, 0.0) - jnp.log1p(jnp.exp(-jnp.abs(x)))


def _iota(shape, axis):
    return lax.broadcasted_iota(jnp.int32, shape, axis)


ANY = pl.BlockSpec(memory_space=pl.ANY)
N_CHIPS = 4


class _Plan(NamedTuple):
    ins: list
    outs: list
    sems: list
    start: Callable
    finish: Callable
    relay: Optional[Callable] = None


def _place():
    x, y, c = lax.axis_index("x"), lax.axis_index("y"), lax.axis_index("c")
    return x, y, c, [(1 - x, y), (x, 1 - y), (1 - x, 1 - y)]


def _gather_plan(xs):
    n = len(xs)

    def parts(x_refs, out_refs, sems):
        send_sems, recv_sems, local_sems = sems
        x, y, c, chips = _place()
        me, sibling = (x, y, c), (x, y, 1 - c)

        def rows(a, px, py, pc):
            return out_refs[a].at[4 * px + 2 * py + pc]

        def copy(a, k, block, to, src=None):
            return pltpu.make_async_remote_copy(
                src_ref=rows(a, *block) if src is None else src, dst_ref=rows(a, *block),
                send_sem=send_sems.at[7 * a + k], recv_sem=recv_sems.at[7 * a + k], device_id=to, device_id_type=MESH)

        mine = [pltpu.make_async_copy(x_refs[a], rows(a, *me), local_sems.at[a]) for a in range(n)]
        first = []
        for a in range(n):
            first.append(copy(a, 0, me, sibling, src=x_refs[a]))
            first += [copy(a, 1 + j, me, (*chip, c), src=x_refs[a]) for j, chip in enumerate(chips)]
        return c, me, sibling, chips, copy, mine, first

    def start(x_refs, out_refs, sems):
        *_, mine, first = parts(x_refs, out_refs, sems)
        for cp in first + mine:
            cp.start()

    def relay(x_refs, out_refs, sems):
        c, me, sibling, chips, copy, _, _ = parts(x_refs, out_refs, sems)
        for j, chip in enumerate(chips):
            for a in range(n):
                copy(a, 1 + j, (*chip, c), me).wait_recv()
                copy(a, 4 + j, (*chip, c), sibling).start()

    def finish(x_refs, out_refs, sems):
        c, me, sibling, chips, copy, mine, first = parts(x_refs, out_refs, sems)
        for a in range(n):
            copy(a, 0, sibling, me).wait_recv()
            for j, chip in enumerate(chips):
                copy(a, 4 + j, (*chip, 1 - c), me).wait_recv()
        for cp in first + [copy(a, 4 + j, (*chip, c), sibling) for j, chip in enumerate(chips) for a in range(n)]:
            cp.wait_send()
        for cp in mine:
            cp.wait()

    return _Plan(list(xs), [jax.ShapeDtypeStruct((N_DEV,) + x.shape, x.dtype) for x in xs],
                 [pltpu.SemaphoreType.DMA((7 * n,)), pltpu.SemaphoreType.DMA((7 * n,)), pltpu.SemaphoreType.DMA((n,))],
                 start, finish, relay)


def _exchange_plan(copies_of, ins, outs, per_array):
    n = len(ins)

    def start(in_refs, out_refs, sems):
        for cp in copies_of(in_refs, out_refs, sems):
            cp.start()

    def finish(in_refs, out_refs, sems):
        copies = copies_of(in_refs, out_refs, sems)
        for cp in copies:
            cp.wait_recv()
        for cp in copies:
            cp.wait_send()

    return _Plan(list(ins), outs, [pltpu.SemaphoreType.DMA((per_array * n,)), pltpu.SemaphoreType.DMA((per_array * n,))],
                 start, finish)


def _sibling_plan(gs):
    def copies_of(g_refs, got_refs, sems):
        x, y, c, _ = _place()
        return [pltpu.make_async_remote_copy(
            src_ref=g_refs[a].at[2 * k + (1 - c)], dst_ref=got_refs[a].at[k], send_sem=sems[0].at[N_CHIPS * a + k],
            recv_sem=sems[1].at[N_CHIPS * a + k], device_id=(x, y, 1 - c), device_id_type=MESH)
            for a in range(len(gs)) for k in range(N_CHIPS)]

    return _exchange_plan(copies_of, gs, [jax.ShapeDtypeStruct((N_CHIPS,) + g.shape[1:], g.dtype) for g in gs], N_CHIPS)


def _chip_plan(ss):
    def copies_of(s_refs, out_refs, sems):
        x, y, c, chips = _place()
        return [pltpu.make_async_remote_copy(
            src_ref=s_refs[a].at[2 * px + py], dst_ref=out_refs[a].at[j], send_sem=sems[0].at[3 * a + j],
            recv_sem=sems[1].at[3 * a + j], device_id=(px, py, c), device_id_type=MESH)
            for a in range(len(ss)) for j, (px, py) in enumerate(chips)]

    return _exchange_plan(copies_of, ss, [jax.ShapeDtypeStruct((3,) + s.shape[1:], s.dtype) for s in ss], 3)


def _join_plans(*plans):
    def cut(refs, counts):
        at = 0
        for n in counts:
            yield refs[at:at + n]
            at += n

    def each(in_refs, out_refs, sems):
        return zip(plans, cut(in_refs, [len(p.ins) for p in plans]), cut(out_refs, [len(p.outs) for p in plans]),
                   cut(sems, [len(p.sems) for p in plans]))

    def start(*refs):
        for p, i, o, s in each(*refs):
            p.start(i, o, s)

    def relay(*refs):
        for p, i, o, s in each(*refs):
            if p.relay is not None:
                p.relay(i, o, s)

    def finish(*refs):
        for p, i, o, s in each(*refs):
            p.finish(i, o, s)

    return _Plan([a for p in plans for a in p.ins], [a for p in plans for a in p.outs],
                 [a for p in plans for a in p.sems], start, finish, relay)


def _run_plan(plan, name):
    n_in, n_out = len(plan.ins), len(plan.outs)

    def body(*refs):
        args = refs[:n_in], refs[n_in:n_in + n_out], refs[n_in + n_out:]
        plan.start(*args)
        if plan.relay is not None:
            plan.relay(*args)
        plan.finish(*args)

    return pl.pallas_call(body, out_shape=plan.outs, in_specs=[ANY] * n_in, out_specs=[ANY] * n_out,
                          scratch_shapes=plan.sems, name=name)(*plan.ins)


def _pcall(body, ride, *, grid, in_specs, out_specs, out_shape, scratch_shapes=(), semantics, name):
    if ride is None:
        return pl.pallas_call(body, grid=grid, in_specs=in_specs, out_specs=out_specs, out_shape=out_shape,
                              scratch_shapes=list(scratch_shapes), compiler_params=_params(*semantics), name=name)
    single = not isinstance(out_shape, (list, tuple))
    out_specs_l, out_shape_l = ([out_specs], [out_shape]) if single else (list(out_specs), list(out_shape))
    n_in, n_out, n_scr = len(in_specs), len(out_shape_l), len(scratch_shapes)
    r_in, r_out = len(ride.ins), len(ride.outs)

    def riding(*refs):
        cuts = [n_in, r_in, n_out, r_out, n_scr]
        groups, at = [], 0
        for width in cuts:
            groups.append(refs[at:at + width])
            at += width
        ins, r_ins, outs, r_outs, scr = groups
        sems = refs[at:]
        first = functools.reduce(jnp.logical_and, [pl.program_id(d) == 0 for d in range(len(grid))])
        last = functools.reduce(jnp.logical_and, [pl.program_id(d) == grid[d] - 1 for d in range(len(grid))])

        @pl.when(first)
        def _():
            ride.start(r_ins, r_outs, sems)

        several_steps = any(n > 1 for n in grid)
        if ride.relay is not None and several_steps:
            @pl.when(last)
            def _():
                ride.relay(r_ins, r_outs, sems)

        body(*ins, *outs, *scr)

        @pl.when(last)
        def _():
            if ride.relay is not None and not several_steps:
                ride.relay(r_ins, r_outs, sems)
            ride.finish(r_ins, r_outs, sems)

    call = pl.pallas_call(
        riding, grid=grid, in_specs=list(in_specs) + [ANY] * r_in, out_specs=out_specs_l + [ANY] * r_out,
        out_shape=out_shape_l + list(ride.outs), scratch_shapes=list(scratch_shapes) + list(ride.sems),
        compiler_params=_params(*(["arbitrary"] * len(grid))), name=name)

    def run(*args):
        res = call(*args, *ride.ins)
        return (res[0] if single else list(res[:n_out])), list(res[n_out:])

    return run


def _rms(x):
    return x * lax.rsqrt(jnp.mean(x * x, axis=-1, keepdims=True) + NORM_EPS)


def _mm(a, b, *, nt=False, ta=False, out_dtype, tm, tn, a_sqrelu=False, drelu_of=None, b_blocked=False,
        out_blocked=False, a_norm=None, a_norm_bwd=None, res_norm=None, norm_bwd=None, name, ride=None):
    k, m = a.shape if ta else a.shape[::-1]
    if b_blocked:
        assert not nt and b.shape[1] == k and b.shape[2] == tn
        n = b.shape[0] * tn
    else:
        n = b.shape[0] if nt else b.shape[1]
        assert (b.shape[1] if nt else b.shape[0]) == k
    tm, tn = min(tm, m), min(tn, n)
    assert m % tm == 0 and n % tn == 0
    assert (res_norm is None and norm_bwd is None) or tn == n
    assert a_norm is None or a_norm_bwd is None
    n_in = (2 + (drelu_of is not None) + (a_norm is not None) + 2 * (a_norm_bwd is not None)
            + 2 * (res_norm is not None) + 3 * (norm_bwd is not None))

    def body(*refs):
        a_ref, b_ref = refs[0], refs[1]
        extra = list(refs[2:n_in])
        outs = list(refs[n_in:])
        o_ref = outs.pop(0)
        u_ref = extra.pop(0) if drelu_of is not None else None
        if a_norm is not None:
            wn_ref, h_ref, h_scr = extra.pop(0), outs.pop(0), outs.pop()

            @pl.when(pl.program_id(1) == 0)
            def _():
                h = (_rms(a_ref[...]) * wn_ref[...]).astype(ACT_DTYPE)
                h_scr[...] = h
                h_ref[...] = h

            av = h_scr[...]
        elif a_norm_bwd is not None:
            y_ref, wy_ref = extra.pop(0), extra.pop(0)
            dy_ref, dwy_ref, dy_scr = outs.pop(0), outs.pop(0), outs.pop()
            first_rows = pl.program_id(0) == 0

            @pl.when(pl.program_id(1) == 0)
            def _():
                yv, up = y_ref[...], a_ref[...]
                rstd = lax.rsqrt(jnp.mean(yv * yv, axis=-1, keepdims=True) + NORM_EPS)
                yhat = yv * rstd
                g = up * wy_ref[...]
                dy = (rstd * (g - yhat * jnp.mean(g * yhat, axis=-1, keepdims=True))).astype(ACT_DTYPE)
                dy_scr[...] = dy
                dy_ref[...] = dy

                @pl.when(first_rows)
                def _():
                    dwy_ref[...] = jnp.zeros_like(dwy_ref)

                dwy_ref[...] += jnp.sum(up * yhat, axis=0, keepdims=True)

            av = dy_scr[...]
        else:
            av = a_ref[...]
        if a_sqrelu:
            av = jnp.square(jnp.maximum(av.astype(F32), 0.0))
        acc = _dot(av, b_ref[...], 0 if ta else 1, 1 if nt else 0)
        if u_ref is not None:
            acc = acc * (2.0 * jnp.maximum(u_ref[...].astype(F32), 0.0))
        if norm_bwd is not None:
            x_ref, wb_ref, add_ref = extra
            dw_ref = outs[0]
            xv = x_ref[...]
            rstd = lax.rsqrt(jnp.mean(xv * xv, axis=-1, keepdims=True) + NORM_EPS)
            xhat = xv * rstd
            g = acc * wb_ref[...]
            o_ref[...] = rstd * (g - xhat * jnp.mean(g * xhat, axis=-1, keepdims=True)) + add_ref[...]

            @pl.when(pl.program_id(0) == 0)
            def _():
                dw_ref[...] = jnp.zeros_like(dw_ref)

            dw_ref[...] += jnp.sum(acc * xhat, axis=0, keepdims=True)
            return
        o_ref[...] = acc.astype(out_dtype)
        if res_norm is not None:
            res_ref, wr_ref = extra
            outs[0][...] = res_ref[...] + _rms(acc) * wr_ref[...]

    if b_blocked:
        b_spec = pl.BlockSpec((None, k, tn), lambda i, j: (j, 0, 0))
    elif nt:
        b_spec = pl.BlockSpec((tn, k), lambda i, j: (j, 0))
    else:
        b_spec = pl.BlockSpec((k, tn), lambda i, j: (0, j))
    a_spec = pl.BlockSpec((k, tm), lambda i, j: (0, i)) if ta else pl.BlockSpec((tm, k), lambda i, j: (i, 0))
    in_specs = [a_spec, b_spec]
    args = [a, b]
    if drelu_of is not None:
        in_specs.append(pl.BlockSpec((tm, tn), lambda i, j: (i, j)))
        args.append(drelu_of)
    if out_blocked:
        out_specs = [pl.BlockSpec((None, tm, tn), lambda i, j: (j, i, 0))]
        out_shape = [jax.ShapeDtypeStruct((n // tn, m, tn), out_dtype)]
    else:
        out_specs = [pl.BlockSpec((tm, tn), lambda i, j: (i, j))]
        out_shape = [jax.ShapeDtypeStruct((m, n), out_dtype)]
    scratch = []
    if a_norm is not None:
        assert not ta
        in_specs.append(pl.BlockSpec((1, k), lambda i, j: (0, 0)))
        args.append(a_norm)
        out_specs.append(pl.BlockSpec((tm, k), lambda i, j: (i, 0)))
        out_shape.append(jax.ShapeDtypeStruct((m, k), ACT_DTYPE))
        scratch.append(pltpu.VMEM((tm, k), ACT_DTYPE))
    if a_norm_bwd is not None:
        assert not ta
        in_specs += [pl.BlockSpec((tm, k), lambda i, j: (i, 0)), pl.BlockSpec((1, k), lambda i, j: (0, 0))]
        args += list(a_norm_bwd)
        out_specs += [pl.BlockSpec((tm, k), lambda i, j: (i, 0)), pl.BlockSpec((1, k), lambda i, j: (0, 0))]
        out_shape += [jax.ShapeDtypeStruct((m, k), ACT_DTYPE), jax.ShapeDtypeStruct((1, k), F32)]
        scratch.append(pltpu.VMEM((tm, k), ACT_DTYPE))
    if res_norm is not None:
        in_specs += [pl.BlockSpec((tm, n), lambda i, j: (i, 0)), pl.BlockSpec((1, n), lambda i, j: (0, 0))]
        args += list(res_norm)
        out_specs.append(pl.BlockSpec((tm, n), lambda i, j: (i, 0)))
        out_shape.append(jax.ShapeDtypeStruct((m, n), F32))
    if norm_bwd is not None:
        rows = pl.BlockSpec((tm, n), lambda i, j: (i, 0))
        in_specs += [rows, pl.BlockSpec((1, n), lambda i, j: (0, 0)), rows]
        args += list(norm_bwd)
        out_specs.append(pl.BlockSpec((1, n), lambda i, j: (0, 0)))
        out_shape.append(jax.ShapeDtypeStruct((1, n), F32))
    single = len(out_shape) == 1
    return _pcall(body, ride, grid=(m // tm, n // tn), in_specs=in_specs,
                  out_specs=out_specs[0] if single else out_specs, out_shape=out_shape[0] if single else out_shape,
                  scratch_shapes=scratch, semantics=("arbitrary", "arbitrary"), name=name)(*args)


ROW_TILE = 512
TM_FWD, TM_DX, TM_DW, TN = 2048, 1024, 1024, 512


def _norm_bwd(dy, x, w, *, out_dtype, add=None, name, ride=None):
    t, d = x.shape

    def body(*refs):
        dy_ref, x_ref, w_ref = refs[0], refs[1], refs[2]
        dx_ref, dw_ref = refs[-2], refs[-1]
        xv = x_ref[...]
        rstd = lax.rsqrt(jnp.mean(xv * xv, axis=-1, keepdims=True) + NORM_EPS)
        xhat = xv * rstd
        dyv = dy_ref[...].astype(F32)
        g = dyv * w_ref[...]
        dx = rstd * (g - xhat * jnp.mean(g * xhat, axis=-1, keepdims=True))
        if add is not None:
            dx = dx + refs[3][...]
        dx_ref[...] = dx.astype(out_dtype)

        @pl.when(pl.program_id(0) == 0)
        def _():
            dw_ref[...] = jnp.zeros_like(dw_ref)

        dw_ref[...] += jnp.sum(dyv * xhat, axis=0, keepdims=True)

    row = pl.BlockSpec((ROW_TILE, d), lambda i: (i, 0))
    vec = pl.BlockSpec((1, d), lambda i: (0, 0))
    in_specs = [row, row, vec] + ([row] if add is not None else [])
    args = [dy, x, w] + ([add] if add is not None else [])
    return _pcall(body, ride, grid=(t // ROW_TILE,), in_specs=in_specs, out_specs=[row, vec],
                  out_shape=[jax.ShapeDtypeStruct((t, d), out_dtype), jax.ShapeDtypeStruct((1, d), F32)],
                  semantics=("arbitrary",), name=name)(*args)


def _loss_fwd_bwd(y, target):
    t, d = y.shape

    def body(y_ref, t_ref, l_ref, dy_ref):
        diff = y_ref[...] - t_ref[...]
        dy_ref[...] = diff * (1.0 / d)

        @pl.when(pl.program_id(0) == 0)
        def _():
            l_ref[...] = jnp.zeros_like(l_ref)

        l_ref[...] += 0.5 * jnp.sum(jnp.mean(diff * diff, axis=-1, keepdims=True), axis=0, keepdims=True)

    row = pl.BlockSpec((ROW_TILE, d), lambda i: (i, 0))
    return pl.pallas_call(body, grid=(t // ROW_TILE,), in_specs=[row, row],
                          out_specs=[pl.BlockSpec((8, LANES), lambda i: (0, 0)), row],
                          out_shape=[jax.ShapeDtypeStruct((8, LANES), F32), jax.ShapeDtypeStruct((t, d), F32)],
                          compiler_params=_params("arbitrary"), name="loss")(y, target)


GLA_STATE = (GLA_HEADS * GLA_DV, GLA_KW)


def _gla_specs(chunk_of):
    rows = lambda width, col: pl.BlockSpec((CHUNK, width), lambda i: (chunk_of(i), col))
    const = lambda r, c: pl.BlockSpec((r, c), lambda i: (0, 0))
    return [rows(GLA_KW, 0),
            rows(GLA_KW, 1),
            rows(GROUP_WIDTH, 1),
            rows(GROUP_WIDTH, 0),
            rows(LANES, 4),
            const(LANES, GLA_KW),
            const(1, GLA_KW),
            const(1, GROUP_WIDTH)]


def _gla_chunk(q_ref, k_ref, v_ref, a_ref, wup_ref, ba_ref):
    z = _dot(a_ref[...], wup_ref[...]) + ba_ref[...]
    tri = (_iota((CHUNK, CHUNK), 1) <= _iota((CHUNK, CHUNK), 0)).astype(F32)
    cum = _dot_exact(tri, _log_sigmoid(z) * (1.0 / GLA_GATE_TAU))
    tot = cum[CHUNK - 1:CHUNK, :]
    e = jnp.exp(tot - cum)
    return (z, e, jnp.exp(tot), k_ref[...].astype(F32) * e, q_ref[...].astype(F32) * (GLA_DK ** -0.5),
            v_ref[...].astype(F32))


def _gla_head_mask():
    return _iota(GLA_STATE, 0) // GLA_DV == _iota(GLA_STATE, 1) // GLA_DK


def _gla_fwd(pmm, pel, w_up, b_a, gnorm_w, ride=None):
    t = pmm.shape[0]
    nc = t // CHUNK

    def body(q_ref, k_ref, v_ref, r_ref, a_ref, wup_ref, ba_ref, gw_ref, o_ref, st_ref, m_scr):
        @pl.when(pl.program_id(0) == 0)
        def _():
            m_scr[...] = jnp.zeros_like(m_scr)

        _, _, decay, kd, qs, vv = _gla_chunk(q_ref, k_ref, v_ref, a_ref, wup_ref, ba_ref)
        m = m_scr[...] * decay + jnp.where(_gla_head_mask(), _dot(vv, kd, 0, 0), 0.0)
        m_scr[...] = m
        st_ref[...] = m
        o = _dot(qs, m, 1, 1)
        rr = r_ref[...]
        gate = rr * jax.nn.sigmoid(rr) * gw_ref[...]
        for h in range(GLA_HEADS):
            vs = slice(h * GLA_DV, (h + 1) * GLA_DV)
            oh = o[:, vs]
            y = oh * lax.rsqrt(jnp.mean(oh * oh, axis=-1, keepdims=True) + NORM_EPS)
            o_ref[:, vs] = (y * gate[:, vs]).astype(o_ref.dtype)

    return _pcall(
        body, ride, grid=(nc,), in_specs=_gla_specs(lambda i: i),
        out_specs=[pl.BlockSpec((CHUNK, GROUP_WIDTH), lambda i: (i, 0)),
                   pl.BlockSpec((None,) + GLA_STATE, lambda i: (i, 0, 0))],
        out_shape=[jax.ShapeDtypeStruct((t, GROUP_WIDTH), ACT_DTYPE), jax.ShapeDtypeStruct((nc,) + GLA_STATE, F32)],
        scratch_shapes=[pltpu.VMEM(GLA_STATE, F32)],
        semantics=("arbitrary",), name="gla_fwd")(pmm, pmm, pmm, pel, pel, w_up, b_a, gnorm_w)


def _gla_bwd(pmm, pel, w_up, b_a, gnorm_w, states, dmix, ride=None):
    t = pmm.shape[0]
    nc = t // CHUNK
    scale = GLA_DK ** -0.5

    def body(q_ref, k_ref, v_ref, r_ref, a_ref, wup_ref, ba_ref, gw_ref, st_ref, prev_ref, do_ref,
             dq_ref, dk_ref, dv_ref, dr_ref, da_ref, dwup_ref, dba_ref, dgw_ref, dm_scr):
        step = pl.program_id(0)

        @pl.when(step == 0)
        def _():
            dm_scr[...] = jnp.zeros_like(dm_scr)
            dwup_ref[...] = jnp.zeros_like(dwup_ref)
            dba_ref[...] = jnp.zeros_like(dba_ref)
            dgw_ref[...] = jnp.zeros_like(dgw_ref)

        z, e, decay, kd, qs, vv = _gla_chunk(q_ref, k_ref, v_ref, a_ref, wup_ref, ba_ref)
        m = st_ref[...]
        m_prev = prev_ref[...] * (step < nc - 1).astype(F32)
        rr, dout, gw = r_ref[...], do_ref[...], gw_ref[...]
        sig = jax.nn.sigmoid(rr)
        silu = rr * sig
        dsilu = sig * (1.0 + rr * (1.0 - sig))
        o = _dot(qs, m, 1, 1)
        d_o, dgw = [], []
        for h in range(GLA_HEADS):
            vs = slice(h * GLA_DV, (h + 1) * GLA_DV)
            oh, dg = o[:, vs], dout[:, vs]
            rstd = lax.rsqrt(jnp.mean(oh * oh, axis=-1, keepdims=True) + NORM_EPS)
            y = oh * rstd
            dgw.append(jnp.sum(dg * y * silu[:, vs], axis=0, keepdims=True))
            dr_ref[:, vs] = (dg * y * gw[:, vs] * dsilu[:, vs]).astype(dr_ref.dtype)
            dy = dg * gw[:, vs] * silu[:, vs]
            d_o.append(rstd * (dy - y * jnp.mean(dy * y, axis=-1, keepdims=True)))
        d_o = jnp.concatenate(d_o, axis=1)
        dgw_ref[...] += jnp.concatenate(dgw, axis=1)
        dq_ref[...] = (_dot(d_o, m) * scale).astype(dq_ref.dtype)
        dm = dm_scr[...] + jnp.where(_gla_head_mask(), _dot(d_o, qs, 0, 0), 0.0)
        dv_ref[...] = _dot(kd, dm, 1, 1).astype(dv_ref.dtype)
        dkd = _dot(vv, dm)
        dk_ref[...] = (dkd * e).astype(dk_ref.dtype)
        dm_scr[...] = dm * decay
        tri_strict = (_iota((CHUNK, CHUNK), 1) < _iota((CHUNK, CHUNK), 0)).astype(F32)
        dla = jnp.sum(dm * m_prev, axis=0, keepdims=True) * decay + _dot_exact(tri_strict, dkd * kd)
        dz = dla * jax.nn.sigmoid(-z) * (1.0 / GLA_GATE_TAU)
        da_ref[...] = _dot(dz, wup_ref[...], 1, 1).astype(da_ref.dtype)
        dwup_ref[...] += _dot(a_ref[...], dz, 0, 0)
        dba_ref[...] += jnp.sum(dz, axis=0, keepdims=True)

    chunk_of = lambda i: nc - 1 - i
    in_specs = _gla_specs(chunk_of) + [
        pl.BlockSpec((None,) + GLA_STATE, lambda i: (chunk_of(i), 0, 0)),
        pl.BlockSpec((None,) + GLA_STATE, lambda i: (jnp.maximum(chunk_of(i) - 1, 0), 0, 0)),
        pl.BlockSpec((CHUNK, GROUP_WIDTH), lambda i: (chunk_of(i), 0))]
    rows = lambda width: pl.BlockSpec((CHUNK, width), lambda i: (chunk_of(i), 0))
    const = lambda r, c: pl.BlockSpec((r, c), lambda i: (0, 0))
    return _pcall(
        body, ride, grid=(nc,), in_specs=in_specs,
        out_specs=[rows(GLA_KW), rows(GLA_KW), rows(GROUP_WIDTH), rows(GROUP_WIDTH), rows(LANES),
                   const(LANES, GLA_KW), const(1, GLA_KW), const(1, GROUP_WIDTH)],
        out_shape=[jax.ShapeDtypeStruct((t, GLA_KW), ACT_DTYPE), jax.ShapeDtypeStruct((t, GLA_KW), ACT_DTYPE),
                   jax.ShapeDtypeStruct((t, GROUP_WIDTH), ACT_DTYPE), jax.ShapeDtypeStruct((t, GROUP_WIDTH), ACT_DTYPE),
                   jax.ShapeDtypeStruct((t, LANES), ACT_DTYPE), jax.ShapeDtypeStruct((LANES, GLA_KW), F32),
                   jax.ShapeDtypeStruct((1, GLA_KW), F32), jax.ShapeDtypeStruct((1, GROUP_WIDTH), F32)],
        scratch_shapes=[pltpu.VMEM(GLA_STATE, F32)],
        semantics=("arbitrary",), name="gla_bwd")(
            pmm, pmm, pmm, pel, pel, w_up, b_a, gnorm_w, states, states, dmix)


CUM_BLOCK = 256


def _fox_gate_fwd(pel, b_f):
    t = pel.shape[0]
    nb = t // CUM_BLOCK

    def body(f_ref, b_ref, cum_ref, cum_t_ref):
        tri = (_iota((CUM_BLOCK, CUM_BLOCK), 1) <= _iota((CUM_BLOCK, CUM_BLOCK), 0)).astype(F32)
        carry = jnp.zeros((1, LANES), F32)
        for blk in range(nb):
            rows = slice(blk * CUM_BLOCK, (blk + 1) * CUM_BLOCK)
            cum = _dot_exact(tri, _log_sigmoid(f_ref[rows, :] + b_ref[...])) + carry
            cum_ref[rows, :] = cum
            cum_t_ref[blk] = cum.T[:ATT_HEADS, :]
            carry = cum[CUM_BLOCK - 1:CUM_BLOCK, :]

    return pl.pallas_call(
        body, grid=(1,),
        in_specs=[pl.BlockSpec((t, LANES), lambda i: (0, 5)), pl.BlockSpec((1, LANES), lambda i: (0, 0))],
        out_specs=[pl.BlockSpec((t, LANES), lambda i: (0, 0)),
                   pl.BlockSpec((nb, ATT_HEADS, CUM_BLOCK), lambda i: (0, 0, 0))],
        out_shape=[jax.ShapeDtypeStruct((t, LANES), F32), jax.ShapeDtypeStruct((nb, ATT_HEADS, CUM_BLOCK), F32)],
        compiler_params=_params("arbitrary"), name="fox_gate_fwd")(pel, b_f)


def _fox_gate_bwd(pel, b_f, dcum_t, dcum_q):
    t = pel.shape[0]
    nb = t // CUM_BLOCK

    def body(f_ref, b_ref, dct_ref, dcq_ref, df_ref, db_ref):
        tri_up = (_iota((CUM_BLOCK, CUM_BLOCK), 1) >= _iota((CUM_BLOCK, CUM_BLOCK), 0)).astype(F32)
        carry = jnp.zeros((1, LANES), F32)
        db = jnp.zeros((1, LANES), F32)
        for blk in reversed(range(nb)):
            rows = slice(blk * CUM_BLOCK, (blk + 1) * CUM_BLOCK)
            query_side = sum(dcq_ref[pair, rows, :] for pair in range(dcq_ref.shape[0]))
            dls = _dot_exact(tri_up, dct_ref[blk].T + query_side) + carry
            carry = dls[0:1, :]
            df = dls * jax.nn.sigmoid(-(f_ref[rows, :] + b_ref[...]))
            df_ref[rows, :] = df.astype(df_ref.dtype)
            db = db + jnp.sum(df, axis=0, keepdims=True)
        db_ref[...] = db

    return pl.pallas_call(
        body, grid=(1,),
        in_specs=[pl.BlockSpec((t, LANES), lambda i: (0, 5)), pl.BlockSpec((1, LANES), lambda i: (0, 0)),
                  pl.BlockSpec((nb, LANES, CUM_BLOCK), lambda i: (0, 0, 0)),
                  pl.BlockSpec((dcum_q.shape[0], t, LANES), lambda i: (0, 0, 0))],
        out_specs=[pl.BlockSpec((t, LANES), lambda i: (0, 0)), pl.BlockSpec((1, LANES), lambda i: (0, 0))],
        out_shape=[jax.ShapeDtypeStruct((t, LANES), ACT_DTYPE), jax.ShapeDtypeStruct((1, LANES), F32)],
        compiler_params=_params("arbitrary"), name="fox_gate_bwd")(pel, b_f, dcum_t, dcum_q)


FOX_Q_BLOCK = 256


assert FOX_Q_BLOCK == CUM_BLOCK
FOX_KEY_STEP = 512


def _fox_scores(q_ref, k_ref, cum_ref, cum_t_ref, h, i):
    hs = slice(h * HEAD_DIM, (h + 1) * HEAD_DIM)
    nb = cum_t_ref.shape[0]
    key_gate = jnp.concatenate([cum_t_ref[kb, h:h + 1, :] for kb in range(nb)], axis=1)
    s = _dot(q_ref[:, hs], k_ref[:, hs], 1, 1) * (HEAD_DIM ** -0.5) + (cum_ref[:, h:h + 1] - key_gate)
    shape = (FOX_Q_BLOCK, nb * FOX_Q_BLOCK)
    return jnp.where(_iota(shape, 1) <= i * FOX_Q_BLOCK + _iota(shape, 0), s, NEG)


def _fox_specs(t):
    bq, nb = FOX_Q_BLOCK, t // FOX_Q_BLOCK
    return [pl.BlockSpec((bq, GROUP_WIDTH), lambda i: (i, 2)), pl.BlockSpec((t, GROUP_WIDTH), lambda i: (0, 3)),
            pl.BlockSpec((t, GROUP_WIDTH), lambda i: (0, 4)), pl.BlockSpec((bq, LANES), lambda i: (i, 0)),
            pl.BlockSpec((nb, ATT_HEADS, bq), lambda i: (0, 0, 0))]


def _fox_fwd(pmm, cum, cum_t, ride=None):
    t = pmm.shape[0]
    bq = FOX_Q_BLOCK

    def body(q_ref, k_ref, v_ref, cum_ref, cum_t_ref, o_ref, lse_ref):
        i = pl.program_id(0)
        lse_ref[...] = jnp.zeros_like(lse_ref)
        for h in range(ATT_HEADS):
            hs = slice(h * HEAD_DIM, (h + 1) * HEAD_DIM)
            s = _fox_scores(q_ref, k_ref, cum_ref, cum_t_ref, h, i)
            m = jnp.max(s, axis=-1, keepdims=True)
            p = jnp.exp(s - m)
            l = jnp.sum(p, axis=-1, keepdims=True)
            o_ref[:, hs] = (_dot(p, v_ref[:, hs]) / l).astype(o_ref.dtype)
            lse_ref[:, h:h + 1] = m + jnp.log(l)

    return _pcall(
        body, ride, grid=(t // bq,), in_specs=_fox_specs(t),
        out_specs=[pl.BlockSpec((bq, GROUP_WIDTH), lambda i: (i, 0)), pl.BlockSpec((bq, LANES), lambda i: (i, 0))],
        out_shape=[jax.ShapeDtypeStruct((t, GROUP_WIDTH), ACT_DTYPE), jax.ShapeDtypeStruct((t, LANES), F32)],
        semantics=("parallel",), name="fox_fwd")(pmm, pmm, pmm, cum, cum_t)


def _fox_bwd(pmm, cum, cum_t, lse, dmix, ride=None):
    t = pmm.shape[0]
    bq, nb = FOX_Q_BLOCK, t // FOX_Q_BLOCK
    pairs, per_pair = ATT_HEADS // 2, LANES // HEAD_DIM
    scale = HEAD_DIM ** -0.5

    def body(q_ref, k_ref, v_ref, cum_ref, cum_t_ref, lse_ref, do_ref, dq_ref, dk_ref, dv_ref, dct_ref, dcq_ref):
        g, i = pl.program_id(0), pl.program_id(1)

        @pl.when(i == 0)
        def _():
            dk_ref[...] = jnp.zeros_like(dk_ref)
            dv_ref[...] = jnp.zeros_like(dv_ref)

        @pl.when((i == 0) & (g == 0))
        def _():
            dct_ref[...] = jnp.zeros_like(dct_ref)

        lane = _iota((1, LANES), 1)

        def run(n):
            causal = _iota((bq, n), 1) <= i * bq + _iota((bq, n), 0)
            dcq = jnp.zeros((bq, LANES), F32)
            for hh in range(per_pair):
                h = per_pair * g + hh
                hs = slice(hh * HEAD_DIM, (hh + 1) * HEAD_DIM)
                pick = (lane == h).astype(F32)
                cq = jnp.sum(cum_ref[...] * pick, axis=1, keepdims=True)
                lse_h = jnp.sum(lse_ref[...] * pick, axis=1, keepdims=True)
                key_gate = jnp.concatenate([cum_t_ref[kb, pl.ds(h, 1), :] for kb in range(n // bq)], axis=1)
                s = _dot(q_ref[:, hs], k_ref[:n, hs], 1, 1) * scale + (cq - key_gate)
                p = jnp.exp(jnp.where(causal, s, NEG) - lse_h)
                do = do_ref[:, hs]
                dp = _dot(do, v_ref[:n, hs], 1, 1)
                ds = p * (dp - jnp.sum(p * dp, axis=-1, keepdims=True))
                dq_ref[:, hs] = (_dot(ds, k_ref[:n, hs]) * scale).astype(dq_ref.dtype)
                dk_ref[:n, hs] += _dot(ds, q_ref[:, hs], 0, 0) * scale
                dv_ref[:n, hs] += _dot(p, do, 0, 0)
                key_side = -jnp.sum(ds, axis=0, keepdims=True)
                for kb in range(n // bq):
                    dct_ref[kb, pl.ds(h, 1), :] += key_side[:, kb * bq:(kb + 1) * bq]
                dcq = dcq + jnp.sum(ds, axis=1, keepdims=True) * pick
            dcq_ref[...] = dcq

        for kx in range(t // FOX_KEY_STEP):
            pl.when(i // (FOX_KEY_STEP // bq) == kx)(functools.partial(run, (kx + 1) * FOX_KEY_STEP))

    cols = lambda first: pl.BlockSpec((bq, LANES), lambda g, i: (i, first + g))
    keys = lambda first: pl.BlockSpec((t, LANES), lambda g, i: (0, first + g))
    per_head = pl.BlockSpec((bq, LANES), lambda g, i: (i, 0))
    fox_q, fox_k, fox_v = (GROUP_WIDTH * n // LANES for n in (2, 3, 4))
    return _pcall(
        body, ride, grid=(pairs, t // bq),
        in_specs=[cols(fox_q), keys(fox_k), keys(fox_v), per_head,
                  pl.BlockSpec((nb, ATT_HEADS, bq), lambda g, i: (0, 0, 0)), per_head, cols(GROUP_WIDTH // LANES)],
        out_specs=[cols(0), keys(0), keys(0), pl.BlockSpec((nb, LANES, bq), lambda g, i: (0, 0, 0)),
                   pl.BlockSpec((None, bq, LANES), lambda g, i: (g, i, 0))],
        out_shape=[jax.ShapeDtypeStruct((t, GROUP_WIDTH), ACT_DTYPE), jax.ShapeDtypeStruct((t, GROUP_WIDTH), F32),
                   jax.ShapeDtypeStruct((t, GROUP_WIDTH), F32), jax.ShapeDtypeStruct((nb, LANES, bq), F32),
                   jax.ShapeDtypeStruct((pairs, t, LANES), F32)],
        semantics=("arbitrary", "arbitrary"), name="fox_bwd")(pmm, pmm, pmm, cum, cum_t, lse, dmix)


CA_Q_BLOCK = 4 * CHUNK
CA_WINDOW = CA_Q_BLOCK + CA_LEFT
CA_BASE = 1024


def _ca_bias_base(rel_bias):
    n = rel_bias.shape[0]
    flat = CA_Q_BLOCK + CA_LEFT - REL_CLIP
    tail = CA_BASE - flat - (2 * REL_CLIP + 1)
    return jnp.concatenate([jnp.broadcast_to(rel_bias[:, 2 * REL_CLIP:], (n, flat)), rel_bias[:, ::-1],
                            jnp.broadcast_to(rel_bias[:, :1], (n, tail))], axis=1)


def _ca_bias_base_grad(dbase):
    flat = CA_Q_BLOCK + CA_LEFT - REL_CLIP
    mid = dbase[:, flat:flat + 2 * REL_CLIP + 1][:, ::-1]
    lo = jnp.sum(dbase[:, flat + 2 * REL_CLIP + 1:], axis=1, keepdims=True)
    hi = jnp.sum(dbase[:, :flat], axis=1, keepdims=True)
    pad = jnp.zeros((dbase.shape[0], 2 * REL_CLIP - 1), F32)
    return mid + jnp.concatenate([lo, pad, hi], axis=1)


def _ca_mask(i):
    r, j = _iota((CA_Q_BLOCK, CA_WINDOW), 0), _iota((CA_Q_BLOCK, CA_WINDOW), 1)
    rc, jc = r // CHUNK, j // CHUNK
    return (jc >= rc) & (jc <= rc + CA_LEFT // CHUNK) & (i * CA_Q_BLOCK + j >= CA_LEFT)


def _ca_fill_bias(i, base_ref, bias_scr):
    @pl.when(i == 0)
    def _():
        for h in range(ATT_HEADS):
            rows = jnp.broadcast_to(base_ref[h:h + 1, :], (CA_Q_BLOCK, CA_BASE))
            bias_scr[h] = pltpu.roll(rows, CA_BASE - CA_Q_BLOCK, 1, stride=1, stride_axis=0)[:, :CA_WINDOW]


def _ca_scores(q_ref, kp_ref, bias_scr, win, h, mask):
    hs = slice(h * HEAD_DIM, (h + 1) * HEAD_DIM)
    s = _dot(q_ref[:, hs], kp_ref[win, hs], 1, 1) * (HEAD_DIM ** -0.5)
    return jnp.where(mask, s + bias_scr[h], NEG)


CA_BIAS_SCRATCH = pltpu.VMEM((ATT_HEADS, CA_Q_BLOCK, CA_WINDOW), F32)


def _ca_fwd(pmm, kp, vp, base, ride=None):
    t = pmm.shape[0]

    def body(q_ref, kp_ref, vp_ref, base_ref, o_ref, lse_ref, bias_scr):
        i = pl.program_id(0)
        _ca_fill_bias(i, base_ref, bias_scr)
        win = pl.ds(pl.multiple_of(i * CA_Q_BLOCK, CA_Q_BLOCK), CA_WINDOW)
        mask = _ca_mask(i)
        lse_ref[...] = jnp.zeros_like(lse_ref)
        for h in range(ATT_HEADS):
            hs = slice(h * HEAD_DIM, (h + 1) * HEAD_DIM)
            s = _ca_scores(q_ref, kp_ref, bias_scr, win, h, mask)
            m = jnp.max(s, axis=-1, keepdims=True)
            p = jnp.exp(s - m)
            l = jnp.sum(p, axis=-1, keepdims=True)
            o_ref[:, hs] = (_dot(p, vp_ref[win, hs]) / l).astype(o_ref.dtype)
            lse_ref[:, h:h + 1] = m + jnp.log(l)

    padded = pl.BlockSpec((t + CA_LEFT, GROUP_WIDTH), lambda i: (0, 0))
    return _pcall(
        body, ride, grid=(t // CA_Q_BLOCK,),
        in_specs=[pl.BlockSpec((CA_Q_BLOCK, GROUP_WIDTH), lambda i: (i, 0)), padded, padded,
                  pl.BlockSpec((ATT_HEADS, CA_BASE), lambda i: (0, 0))],
        out_specs=[pl.BlockSpec((CA_Q_BLOCK, GROUP_WIDTH), lambda i: (i, 0)),
                   pl.BlockSpec((CA_Q_BLOCK, LANES), lambda i: (i, 0))],
        out_shape=[jax.ShapeDtypeStruct((t, GROUP_WIDTH), ACT_DTYPE), jax.ShapeDtypeStruct((t, LANES), F32)],
        scratch_shapes=[CA_BIAS_SCRATCH], semantics=("arbitrary",), name="ca_fwd")(pmm, kp, vp, base)


def _ca_bwd(pmm, kp, vp, base, lse, dmix, ride=None):
    t = pmm.shape[0]
    scale = HEAD_DIM ** -0.5

    def body(q_ref, kp_ref, vp_ref, base_ref, lse_ref, do_ref, dq_ref, dkp_ref, dvp_ref, dbase_ref, bias_scr):
        i = pl.program_id(0)
        _ca_fill_bias(i, base_ref, bias_scr)

        @pl.when(i == 0)
        def _():
            dkp_ref[...] = jnp.zeros_like(dkp_ref)
            dvp_ref[...] = jnp.zeros_like(dvp_ref)
            dbase_ref[...] = jnp.zeros_like(dbase_ref)

        win = pl.ds(pl.multiple_of(i * CA_Q_BLOCK, CA_Q_BLOCK), CA_WINDOW)
        mask = _ca_mask(i)
        flip = (_iota((CA_Q_BLOCK, CA_Q_BLOCK), 0) + _iota((CA_Q_BLOCK, CA_Q_BLOCK), 1) == CA_Q_BLOCK - 1).astype(F32)
        for h in range(ATT_HEADS):
            hs = slice(h * HEAD_DIM, (h + 1) * HEAD_DIM)
            s = _ca_scores(q_ref, kp_ref, bias_scr, win, h, mask)
            p = jnp.exp(s - lse_ref[:, h:h + 1])
            do = do_ref[:, hs]
            dp = _dot(do, vp_ref[win, hs], 1, 1)
            ds = p * (dp - jnp.sum(p * dp, axis=-1, keepdims=True))
            dq_ref[:, hs] = (_dot(ds, kp_ref[win, hs]) * scale).astype(dq_ref.dtype)
            dkp_ref[win, hs] += _dot(ds, q_ref[:, hs], 0, 0) * scale
            dvp_ref[win, hs] += _dot(p, do, 0, 0)
            rev = jnp.concatenate([_dot(flip, ds), jnp.zeros((CA_Q_BLOCK, CA_BASE - CA_WINDOW), F32)], axis=1)
            lined = pltpu.roll(rev, 1, 1, stride=1, stride_axis=0)
            dbase_ref[h:h + 1, :] += jnp.sum(lined, axis=0, keepdims=True)

    padded = pl.BlockSpec((t + CA_LEFT, GROUP_WIDTH), lambda i: (0, 0))
    return _pcall(
        body, ride, grid=(t // CA_Q_BLOCK,),
        in_specs=[pl.BlockSpec((CA_Q_BLOCK, GROUP_WIDTH), lambda i: (i, 0)), padded, padded,
                  pl.BlockSpec((ATT_HEADS, CA_BASE), lambda i: (0, 0)),
                  pl.BlockSpec((CA_Q_BLOCK, LANES), lambda i: (i, 0)),
                  pl.BlockSpec((CA_Q_BLOCK, GROUP_WIDTH), lambda i: (i, 0))],
        out_specs=[pl.BlockSpec((CA_Q_BLOCK, GROUP_WIDTH), lambda i: (i, 0)), padded, padded,
                   pl.BlockSpec((ATT_HEADS, CA_BASE), lambda i: (0, 0))],
        out_shape=[jax.ShapeDtypeStruct((t, GROUP_WIDTH), ACT_DTYPE),
                   jax.ShapeDtypeStruct((t + CA_LEFT, GROUP_WIDTH), F32),
                   jax.ShapeDtypeStruct((t + CA_LEFT, GROUP_WIDTH), F32),
                   jax.ShapeDtypeStruct((ATT_HEADS, CA_BASE), F32)],
        scratch_shapes=[CA_BIAS_SCRATCH], semantics=("arbitrary",), name="ca_bwd")(pmm, kp, vp, base, lse, dmix)


GELU_C = 0.7978845608028654
GELU_A = 0.044715


def _shift_down(v, k, fill):
    return jnp.where(_iota(v.shape, 0) >= k, pltpu.roll(v, k, 0), fill)


def _shift_up(v, k, fill):
    t = v.shape[0]
    return jnp.where(_iota(v.shape, 0) < t - k, pltpu.roll(v, t - k, 0), fill)


def _linear_scan(a, b, shift):
    k = 1
    while k < a.shape[0]:
        b = a * shift(b, k, 0.0) + b
        a = a * shift(a, k, 1.0)
        k *= 2
    return b


def _neg_expm1(y):
    series = -y * (1.0 + y * (0.5 + y * (1.0 / 6.0 + y * (1.0 / 24.0 + y * (1.0 / 120.0)))))
    return jnp.where(y > -0.1, series, 1.0 - jnp.exp(y))


def _lru_forward(x, g_in, cw, cb, wa, ba, wx, bx, lam):
    xs = [_shift_down(x, CONV_WIDTH - 1 - j, 0.0) for j in range(CONV_WIDTH - 1)] + [x]
    xc = cb + sum(cw[j:j + 1, :] * xs[j] for j in range(CONV_WIDTH))
    r = jax.nn.sigmoid(_dot(xc, wa) + ba)
    i = jax.nn.sigmoid(_dot(xc, wx) + bx)
    lsl = _log_sigmoid(lam)
    la = LRU_C * r * lsl
    a = jnp.exp(la)
    s = jnp.sqrt(_neg_expm1(2.0 * la))
    h = _linear_scan(a, s * (i * xc), _shift_down)
    u = GELU_C * (g_in + GELU_A * g_in * g_in * g_in)
    th = jnp.tanh(u)
    gelu = 0.5 * g_in * (1.0 + th)
    return xs, xc, r, i, lsl, a, s, h, th, gelu


def _lru_specs(t):
    col = lambda off: pl.BlockSpec((t, LANES), lambda j: (0, j + off))
    vec = pl.BlockSpec((1, LANES), lambda j: (0, j))
    mat = pl.BlockSpec((None, LANES, LANES), lambda j: (j, 0, 0))
    return [col(0), col(GROUP_WIDTH // LANES), pl.BlockSpec((CONV_WIDTH, LANES), lambda j: (0, j)),
            vec, mat, vec, mat, vec, vec]


def _lru_fwd(pel, conv_w, conv_b, wa, ba, wx, bx, lam, ride=None):
    t = pel.shape[0]

    def body(g_ref, x_ref, cw_ref, cb_ref, wa_ref, ba_ref, wx_ref, bx_ref, lam_ref, o_ref):
        res = _lru_forward(x_ref[...], g_ref[...], cw_ref[...], cb_ref[...], wa_ref[...], ba_ref[...],
                           wx_ref[...], bx_ref[...], lam_ref[...])
        o_ref[...] = (res[7] * res[9]).astype(o_ref.dtype)

    return _pcall(
        body, ride, grid=(GROUP_WIDTH // LANES,), in_specs=_lru_specs(t),
        out_specs=pl.BlockSpec((t, LANES), lambda j: (0, j)),
        out_shape=jax.ShapeDtypeStruct((t, GROUP_WIDTH), ACT_DTYPE),
        semantics=("parallel",), name="lru_fwd")(pel, pel, conv_w, conv_b, wa, ba, wx, bx, lam)


def _lru_bwd(pel, conv_w, conv_b, wa, ba, wx, bx, lam, dmix, ride=None):
    t = pel.shape[0]

    def body(g_ref, x_ref, cw_ref, cb_ref, wa_ref, ba_ref, wx_ref, bx_ref, lam_ref, do_ref,
             dg_ref, dx_ref, dcw_ref, dcb_ref, dwa_ref, dba_ref, dwx_ref, dbx_ref, dlam_ref):
        g_in, cw, lam = g_ref[...], cw_ref[...], lam_ref[...]
        xs, xc, r, i, lsl, a, s, h, th, gelu = _lru_forward(
            x_ref[...], g_in, cw, cb_ref[...], wa_ref[...], ba_ref[...], wx_ref[...], bx_ref[...], lam)
        dout = do_ref[...]
        dgelu = 0.5 * (1.0 + th) + 0.5 * g_in * (1.0 - th * th) * GELU_C * (1.0 + 3.0 * GELU_A * g_in * g_in)
        dg_ref[...] = (dout * h * dgelu).astype(dg_ref.dtype)
        gsum = _linear_scan(_shift_up(a, 1, 0.0), dout * gelu, _shift_up)
        da = gsum * _shift_down(h, 1, 0.0)
        di = gsum * s * xc
        dla = da * a - gsum * (i * xc) * (a * a / s)
        dlam_ref[...] = jnp.sum(dla * (LRU_C * r), axis=0, keepdims=True) * jax.nn.sigmoid(-lam)
        dpr = dla * (LRU_C * lsl) * r * (1.0 - r)
        dpi = di * i * (1.0 - i)
        dxc = gsum * s * i + _dot(dpr, wa_ref[...], 1, 1) + _dot(dpi, wx_ref[...], 1, 1)
        xct = xc.T
        dwa_ref[...] = _dot(xct, dpr)
        dwx_ref[...] = _dot(xct, dpi)
        dba_ref[...] = jnp.sum(dpr, axis=0, keepdims=True)
        dbx_ref[...] = jnp.sum(dpi, axis=0, keepdims=True)
        dcb_ref[...] = jnp.sum(dxc, axis=0, keepdims=True)
        for j in range(CONV_WIDTH):
            dcw_ref[j:j + 1, :] = jnp.sum(dxc * xs[j], axis=0, keepdims=True)
        dx = cw[CONV_WIDTH - 1:CONV_WIDTH, :] * dxc
        for j in range(CONV_WIDTH - 1):
            dx = dx + cw[j:j + 1, :] * _shift_up(dxc, CONV_WIDTH - 1 - j, 0.0)
        dx_ref[...] = dx.astype(dx_ref.dtype)

    col = pl.BlockSpec((t, LANES), lambda j: (0, j))
    vec = pl.BlockSpec((1, LANES), lambda j: (0, j))
    mat = pl.BlockSpec((None, LANES, LANES), lambda j: (j, 0, 0))
    nb = GROUP_WIDTH // LANES
    vshape = jax.ShapeDtypeStruct((1, GROUP_WIDTH), F32)
    mshape = jax.ShapeDtypeStruct((nb, LANES, LANES), F32)
    return _pcall(
        body, ride, grid=(nb,),
        in_specs=_lru_specs(t) + [pl.BlockSpec((t, LANES), lambda j: (0, j + nb))],
        out_specs=[col, col, pl.BlockSpec((CONV_WIDTH, LANES), lambda j: (0, j)), vec, mat, vec, mat, vec, vec],
        out_shape=[jax.ShapeDtypeStruct((t, GROUP_WIDTH), ACT_DTYPE), jax.ShapeDtypeStruct((t, GROUP_WIDTH), ACT_DTYPE),
                   jax.ShapeDtypeStruct((CONV_WIDTH, GROUP_WIDTH), F32), vshape, mshape, vshape, mshape, vshape, vshape],
        semantics=("parallel",), name="lru_bwd")(
            pel, pel, conv_w, conv_b, wa, ba, wx, bx, lam, dmix)


def _block_diag_pairs(w):
    z = jnp.zeros((LRU_BLOCK_DIM, LRU_BLOCK_DIM), w.dtype)
    return jnp.stack([jnp.block([[w[2 * j], z], [z, w[2 * j + 1]]]) for j in range(w.shape[0] // 2)])


def _block_diag_pairs_grad(dw):
    b = LRU_BLOCK_DIM
    return jnp.stack([dw[n // 2, (n % 2) * b:(n % 2 + 1) * b, (n % 2) * b:(n % 2 + 1) * b] for n in range(2 * dw.shape[0])])


def _row_tile(r):
    return ROW_TILE if r % ROW_TILE == 0 else r


def _pair_sum(g, got, place, name):
    _, r, c = g.shape
    tile = r

    def body(place_ref, a_ref, b_ref, o_ref):
        o_ref[...] = (a_ref[...].astype(F32) + b_ref[...].astype(F32)).astype(o_ref.dtype)

    blk = pl.BlockSpec((1, tile, c), lambda k, i, place_ref: (k, i, 0))
    return pl.pallas_call(
        body,
        grid_spec=pltpu.PrefetchScalarGridSpec(
            num_scalar_prefetch=1, grid=(N_CHIPS, r // tile),
            in_specs=[pl.BlockSpec((1, tile, c), lambda k, i, place_ref: (2 * k + place_ref[0], i, 0)), blk],
            out_specs=blk),
        out_shape=jax.ShapeDtypeStruct(got.shape, got.dtype),
        compiler_params=_params("parallel", "parallel"), name=name)(place, g, got)


def _adamw_update(g, w_ref, m_ref, v_ref, g_ref, d_ref, nm_ref, nv_ref):
    nm = ADAM_B1 * m_ref[...] + (1.0 - ADAM_B1) * g
    nv = ADAM_B2 * v_ref[...] + (1.0 - ADAM_B2) * jnp.square(g)
    m_hat = nm / (1.0 - ADAM_B1 ** ADAM_STEP)
    v_hat = nv / (1.0 - ADAM_B2 ** ADAM_STEP)
    g_ref[...] = g
    d_ref[...] = -ADAM_LR * (m_hat / (jnp.sqrt(v_hat) + ADAM_EPS) + ADAM_WD * w_ref[...])
    nm_ref[...] = nm
    nv_ref[...] = nv


def _adamw_sharded(parts, w, m, v, place, name):
    n_layers, r, c = w.shape
    tile = _row_tile(r)
    nb = r // tile

    def body(place_ref, *refs):
        layer = pl.program_id(0)
        g = None
        for l in range(n_layers):
            s_ref, r_ref = refs[2 * l], refs[2 * l + 1]
            g_l = s_ref[0].astype(F32) + r_ref[0].astype(F32) + r_ref[1].astype(F32) + r_ref[2].astype(F32)
            g = g_l if g is None else jnp.where(layer == l, g_l, g)
        _adamw_update(g, *refs[2 * n_layers:])

    def part_specs(l):
        rows = lambda q, i: jnp.where(q < l, 0, jnp.where(q > l, nb - 1, i))
        return [pl.BlockSpec((1, tile, c), lambda q, i, place_ref: (place_ref[1], rows(q, i), 0)),
                pl.BlockSpec((3, tile, c), lambda q, i, place_ref: (0, rows(q, i), 0))]

    in_specs, args = [], []
    for l, (s, recv) in enumerate(parts):
        in_specs += part_specs(l)
        args += [s, recv]
    blk = pl.BlockSpec((None, tile, c), lambda q, i, place_ref: (q, i, 0))
    out = jax.ShapeDtypeStruct((n_layers, r, c), F32)
    return pl.pallas_call(
        body,
        grid_spec=pltpu.PrefetchScalarGridSpec(
            num_scalar_prefetch=1, grid=(n_layers, nb), in_specs=in_specs + [blk, blk, blk],
            out_specs=[blk, blk, blk, blk]),
        out_shape=[out, out, out, out], compiler_params=_params("arbitrary", "arbitrary"), name=name)(
            place, *args, w, m, v)


def _adamw_small(repl_parts, vec_parts, w, m, v, place):
    n_r, n = len(repl_parts), len(w)
    shapes = [a.shape for a in w]

    def body(place_ref, *refs):
        parts, rest = refs[:n], refs[n:]
        for k in range(n):
            take = (lambda p: parts[k][p]) if k < n_r else (lambda p: parts[k][p, 0])
            g = take(0)
            for p in range(1, N_DEV):
                g = g + take(p)
            _adamw_update(g, rest[k], rest[n + k], rest[2 * n + k], *rest[3 * n + 4 * k:3 * n + 4 * k + 4])

    def whole(shape):
        return pl.BlockSpec(shape, lambda i, place_ref: (0,) * len(shape))

    def mine(shard):
        return pl.BlockSpec((N_DEV, 1) + shard, lambda i, place_ref: (0, place_ref[2]) + (0,) * len(shard))

    in_specs = [whole(a.shape) for a in repl_parts] + [mine(s) for s in shapes[n_r:]] + [whole(s) for s in shapes] * 3
    outs = pl.pallas_call(
        body,
        grid_spec=pltpu.PrefetchScalarGridSpec(
            num_scalar_prefetch=1, grid=(1,), in_specs=in_specs,
            out_specs=[whole(s) for s in shapes for _ in range(4)]),
        out_shape=[jax.ShapeDtypeStruct(s, F32) for s in shapes for _ in range(4)],
        compiler_params=_params("arbitrary"), name="adamw_small")(place, *repl_parts, *vec_parts, *w, *m, *v)
    return [outs[4 * k:4 * k + 4] for k in range(n)]


SHARDED = {"norm_w": 2, "w_in_even": 2, "gla_w_a_up": 2, "w_out_even": 1, "w_in_odd": 2, "conv_w": 2, "conv_b": 1,
           "lru_b_a": 1, "lru_b_x": 1, "lru_lambda": 1, "w_out_odd": 1, "w_mlp_up": 2, "w_mlp_down": 1}
REPLICATED = ["gla_b_a", "gla_norm_w", "fox_b_f", "rel_bias", "lru_w_a", "lru_w_x"]
WEIGHTS = ["norm_w", "w_in_even", "gla_w_a_up", "gla_b_a", "gla_norm_w", "fox_b_f", "w_out_even", "w_in_odd",
           "rel_bias", "conv_w", "conv_b", "lru_w_a", "lru_b_a", "lru_w_x", "lru_b_x", "lru_lambda", "w_out_odd",
           "w_mlp_up", "w_mlp_down"]
MATRICES = ("w_in_even", "w_out_even", "w_in_odd", "w_out_odd", "w_mlp_up", "w_mlp_down")
TRANSPOSED = ("w_in_even", "w_in_odd")
VECTORS = tuple(n for n in SHARDED if n not in MATRICES)
MATRIX_BLOCKS = (("w_in_even", 0), ("w_out_even", 0), ("w_in_odd", 0), ("w_out_odd", 0),
                 ("w_mlp_up", 0), ("w_mlp_up", 1), ("w_mlp_down", 0), ("w_mlp_down", 1))


def _join_shards(blocks, axis):
    moved = jnp.moveaxis(blocks, 0, axis)
    shape = moved.shape
    return moved.reshape(shape[:axis] + (shape[axis] * shape[axis + 1],) + shape[axis + 2:])


def _split_shards(full, axis):
    shape = full.shape
    cut = full.reshape(shape[:axis] + (N_DEV, shape[axis] // N_DEV) + shape[axis + 1:])
    return jnp.moveaxis(cut, axis, 0)


EVEN_SPLITS = (0, 256, 512, 1024, 1536, 1552, 2064, 2576, 3088, 3096)


def _even_in_split(wt):
    c = [wt[EVEN_SPLITS[k]:EVEN_SPLITS[k + 1]] for k in range(9)]
    gq, gk, gv, gr, ga, fq, fk, fv, ff = c
    padrows = lambda a: jnp.pad(a, ((0, LANES - a.shape[0]), (0, 0)))
    return jnp.concatenate([gq, gk, gv, fq, fk, fv], axis=0), jnp.concatenate([gr, padrows(ga), padrows(ff)], axis=0)


def _even_in_merge(dmm, dele):
    return jnp.concatenate([dmm[:1024], dele[:512], dele[512:512 + GLA_RANK], dmm[1024:2560],
                            dele[640:640 + ATT_HEADS]], axis=0)


def _forward_backward(x, target, shard, vec_shard, w, place):
    w = dict(w)
    g, dnorm, sums, recv = {}, {}, {}, {}
    nrm = lambda l, k: w["norm_w"][l, k][None, :]
    gather = lambda *keys: _gather_plan([shard[k] for k in keys])
    blocks = lambda r, c: (N_DEV, r // N_DEV, c)

    def pair_sum(key):
        sums[key] = _pair_sum(g[key], got[key], place, f"rs_pair_sum_{key[0]}_{key[1]}")

    got = {}

    def mlp_fwd(xin, layer, ride_up, ride_down):
        up = _mm(xin, w["w_mlp_up"][layer], out_dtype=ACT_DTYPE, tm=TM_FWD, tn=D_FF // N_DEV, b_blocked=True,
                 a_norm=nrm(layer, 2), name=f"mlp_up_{layer}", ride=ride_up)
        (u, h), rode_up = up if ride_up is not None else (up, None)
        down = _mm(u, w["w_mlp_down"][layer], out_dtype=F32, tm=TM_DX // 2, tn=D_MODEL, a_sqrelu=True,
                   res_norm=(xin, nrm(layer, 3)), name=f"mlp_down_{layer}", ride=ride_down)
        (yv, xout), rode_down = down if ride_down is not None else (down, None)
        return xout, (xin, h, u, yv), rode_up, rode_down

    def mlp_bwd(dxout, saved, layer, ride):
        xin, h, u, yv = saved
        k_up, k_down = ("w_mlp_up", layer), ("w_mlp_down", layer)
        res = _mm(dxout, w["w_mlp_down"][layer], nt=True, out_dtype=ACT_DTYPE, tm=TM_DX, tn=TN, drelu_of=u,
                  a_norm_bwd=(yv, nrm(layer, 3)), name=f"mlp_down_dx_{layer}", ride=ride)
        (du, dy, dnorm[(layer, 3)]), rode = res if ride is not None else (res, None)
        g[k_down] = _mm(u, dy, ta=True, out_dtype=WIRE_DTYPE, tm=TM_DW, tn=TN, a_sqrelu=True,
                        name=f"mlp_down_dw_{layer}").reshape(blocks(D_FF, D_MODEL))
        g[k_up] = _mm(h, du, ta=True, out_dtype=WIRE_DTYPE, tm=TM_DW, tn=D_FF // N_DEV, out_blocked=True,
                      name=f"mlp_up_dw_{layer}")
        w_up = jnp.moveaxis(w["w_mlp_up"][layer], 0, 1).reshape(D_MODEL, D_FF)
        (dxin, dnorm[(layer, 2)]), (got[k_down], got[k_up]) = _mm(
            du, w_up, nt=True, out_dtype=F32, tm=TM_DX // 2, tn=D_MODEL, norm_bwd=(xin, nrm(layer, 2), dxout),
            name=f"mlp_up_dx_{layer}", ride=_sibling_plan([g[k_down], g[k_up]]))
        pair_sum(k_down)
        pair_sum(k_up)
        return dxin, rode

    first = _run_plan(_gather_plan([shard[("w_in_even", 0)]] + [vec_shard[n] for n in VECTORS]),
                      "weights_all_gather_first")
    w["w_in_even"] = first[0].reshape(-1, D_MODEL)
    for n, b in zip(VECTORS, first[1:]):
        w[n] = _join_shards(b, SHARDED[n])
    w["w_mlp_up"], w["w_mlp_down"] = [None] * DEPTH, [None] * DEPTH

    wmm_e, wel_e = _even_in_split(w["w_in_even"])
    w_up_pad = jnp.pad(w["gla_w_a_up"][0], ((0, LANES - GLA_RANK), (0, 0)))
    b_f_pad = jnp.pad(w["fox_b_f"], ((0, 0), (0, LANES - ATT_HEADS)))
    (pmm0, h0), (w_out_even,) = _mm(x, wmm_e, nt=True, out_dtype=ACT_DTYPE, tm=TM_FWD, tn=TN, a_norm=nrm(0, 0),
                                    name="in_even_mm", ride=gather(("w_out_even", 0)))
    pel0 = _mm(h0, wel_e, nt=True, out_dtype=F32, tm=TM_FWD, tn=768, name="in_even_el")
    (out_a, states), (w_in_odd,) = _gla_fwd(pmm0, pel0, w_up_pad, w["gla_b_a"], w["gla_norm_w"],
                                            ride=gather(("w_in_odd", 0)))
    cum, cum_t = _fox_gate_fwd(pel0, b_f_pad)
    (out_b, lse_b), (w["w_mlp_up"][0], w_mlp_down0) = _fox_fwd(pmm0, cum, cum_t,
                                                               ride=gather(("w_mlp_up", 0), ("w_mlp_down", 0)))
    w["w_out_even"] = w_out_even.reshape(D_MODEL, D_MODEL)
    w["w_mlp_down"][0] = w_mlp_down0.reshape(D_FF, D_MODEL)
    mix_in0 = jnp.concatenate([out_a, out_b], axis=1)
    mix0, x1 = _mm(mix_in0, w["w_out_even"], out_dtype=F32, tm=TM_DX, tn=D_MODEL, res_norm=(x, nrm(0, 1)),
                   name="out_even")
    x2, mlp0, _, (w["w_mlp_up"][1],) = mlp_fwd(x1, 0, None, gather(("w_mlp_up", 1)))
    w["w_in_odd"] = w_in_odd.reshape(-1, D_MODEL)

    w_in_o = w["w_in_odd"]
    n_mm_o = 3 * GROUP_WIDTH
    wa_bd, wx_bd = _block_diag_pairs(w["lru_w_a"][0]), _block_diag_pairs(w["lru_w_x"][0])
    base = _ca_bias_base(w["rel_bias"][0])
    pmm1, h1 = _mm(x2, w_in_o[:n_mm_o], nt=True, out_dtype=ACT_DTYPE, tm=TM_FWD, tn=TN, a_norm=nrm(1, 0),
                   name="in_odd_mm")
    pel1 = _mm(h1, w_in_o[n_mm_o:], nt=True, out_dtype=F32, tm=TM_FWD, tn=TN, name="in_odd_el")
    kp = jnp.pad(pmm1[:, GROUP_WIDTH:2 * GROUP_WIDTH], ((CA_LEFT, 0), (0, 0)))
    vp = jnp.pad(pmm1[:, 2 * GROUP_WIDTH:], ((CA_LEFT, 0), (0, 0)))
    (out_c, lse_c), (w_mlp_down1,) = _ca_fwd(pmm1, kp, vp, base, ride=gather(("w_mlp_down", 1)))
    w["w_mlp_down"][1] = w_mlp_down1.reshape(D_FF, D_MODEL)
    lru_args = (pel1, w["conv_w"][0], w["conv_b"], wa_bd, w["lru_b_a"], wx_bd, w["lru_b_x"], w["lru_lambda"])
    out_d, (w_out_odd,) = _lru_fwd(*lru_args, ride=gather(("w_out_odd", 0)))
    w["w_out_odd"] = w_out_odd.reshape(D_MODEL, D_MODEL)
    mix_in1 = jnp.concatenate([out_c, out_d], axis=1)
    mix1, x3 = _mm(mix_in1, w["w_out_odd"], out_dtype=F32, tm=TM_DX, tn=D_MODEL, res_norm=(x2, nrm(1, 1)),
                   name="out_odd")
    x4, mlp1, _, _ = mlp_fwd(x3, 1, None, None)

    loss, dx4 = _loss_fwd_bwd(x4, target)

    k_oo, k_io, k_oe, k_ie = ("w_out_odd", 0), ("w_in_odd", 0), ("w_out_even", 0), ("w_in_even", 0)
    mlp_keys = lambda l: [("w_mlp_down", l), ("w_mlp_up", l)]
    dx3, _ = mlp_bwd(dx4, mlp1, 1, None)
    dmix_in1, dmix1, dnorm[(1, 1)] = _mm(dx3, w["w_out_odd"], nt=True, out_dtype=F32, tm=TM_DX, tn=TN,
                                         a_norm_bwd=(mix1, nrm(1, 1)), name="out_odd_dx")
    g[k_oo] = _mm(mix_in1, dmix1, ta=True, out_dtype=WIRE_DTYPE, tm=TM_DW, tn=TN, name="out_odd_dw").reshape(
        blocks(D_MODEL, D_MODEL))
    (dq_c, dkp, dvp, dbase), rode = _ca_bwd(
        pmm1, kp, vp, base, lse_c, dmix_in1,
        ride=_join_plans(_chip_plan([sums[k] for k in mlp_keys(1)]), _sibling_plan([g[k_oo]])))
    recv.update(zip(mlp_keys(1), rode[:2]))
    got[k_oo] = rode[2]
    pair_sum(k_oo)
    (dgate, dxin, g_conv_w, g_conv_b, dwa_bd, g_lru_b_a, dwx_bd, g_lru_b_x, g_lru_lambda), (recv[k_oo],) = _lru_bwd(
        *lru_args, dmix_in1, ride=_chip_plan([sums[k_oo]]))
    dp1 = jnp.concatenate([dq_c, dkp[CA_LEFT:].astype(ACT_DTYPE), dvp[CA_LEFT:].astype(ACT_DTYPE), dgate, dxin], axis=1)
    g[k_io] = _mm(dp1, h1, ta=True, out_dtype=WIRE_DTYPE, tm=dp1.shape[1] // 2, tn=TN, name="in_odd_dw").reshape(
        blocks(dp1.shape[1], D_MODEL))
    (dx2, dnorm[(1, 0)]), (got[k_io],) = _mm(dp1, w_in_o, out_dtype=F32, tm=TM_DX // 2, tn=D_MODEL,
                                             norm_bwd=(x2, nrm(1, 0), dx3), name="in_odd_dx",
                                             ride=_sibling_plan([g[k_io]]))
    pair_sum(k_io)
    g["rel_bias"] = _ca_bias_base_grad(dbase)[None]
    g["conv_w"], g["conv_b"] = g_conv_w[None], g_conv_b
    g["lru_w_a"], g["lru_w_x"] = _block_diag_pairs_grad(dwa_bd)[None], _block_diag_pairs_grad(dwx_bd)[None]
    g["lru_b_a"], g["lru_b_x"], g["lru_lambda"] = g_lru_b_a, g_lru_b_x, g_lru_lambda

    dx1, (recv[k_io],) = mlp_bwd(dx2, mlp0, 0, _chip_plan([sums[k_io]]))
    dmix_in0, dmix0, dnorm[(0, 1)] = _mm(dx1, w["w_out_even"], nt=True, out_dtype=F32, tm=TM_DX, tn=TN,
                                         a_norm_bwd=(mix0, nrm(0, 1)), name="out_even_dx")
    g[k_oe] = _mm(mix_in0, dmix0, ta=True, out_dtype=WIRE_DTYPE, tm=TM_DW, tn=TN, name="out_even_dw").reshape(
        blocks(D_MODEL, D_MODEL))
    k_md0, k_mu0 = mlp_keys(0)
    (dq_a, dk_a, dv_a, dr_a, da_a, dw_up_pad, g_gla_b_a, g_gla_norm_w), (got[k_oe], recv[k_md0]) = _gla_bwd(
        pmm0, pel0, w_up_pad, w["gla_b_a"], w["gla_norm_w"], states, dmix_in0,
        ride=_join_plans(_sibling_plan([g[k_oe]]), _chip_plan([sums[k_md0]])))
    pair_sum(k_oe)
    (dq_b, dk_b, dv_b, dcum_t, dcum_q), (recv[k_mu0], recv[k_oe]) = _fox_bwd(
        pmm0, cum, cum_t, lse_b, dmix_in0, ride=_chip_plan([sums[k_mu0], sums[k_oe]]))
    df_b, db_f = _fox_gate_bwd(pel0, b_f_pad, dcum_t, dcum_q)
    g["gla_w_a_up"] = dw_up_pad[:GLA_RANK][None]
    g["gla_b_a"], g["gla_norm_w"], g["fox_b_f"] = g_gla_b_a, g_gla_norm_w, db_f[:, :ATT_HEADS]
    dp0 = jnp.concatenate([dq_a, dk_a, dv_a, dq_b, dk_b.astype(ACT_DTYPE), dv_b.astype(ACT_DTYPE), dr_a, da_a, df_b],
                          axis=1)
    w_perm = jnp.concatenate([wmm_e, wel_e], axis=0)
    n_mm_e = wmm_e.shape[0]
    dw_perm, repl_parts = _mm(dp0, h0, ta=True, out_dtype=WIRE_DTYPE, tm=dp0.shape[1] // 2, tn=TN, name="in_even_dw",
                              ride=_gather_plan([g[n] for n in REPLICATED]))
    dw_even = _even_in_merge(dw_perm[:n_mm_e], dw_perm[n_mm_e:])
    g[k_ie] = dw_even.reshape(blocks(dw_even.shape[0], D_MODEL))
    dh0, (got[k_ie],) = _mm(dp0, w_perm, out_dtype=F32, tm=TM_DX, tn=TN, name="in_even_dx",
                            ride=_sibling_plan([g[k_ie]]))
    pair_sum(k_ie)
    (dx0, dnorm[(0, 0)]), (recv[k_ie],) = _norm_bwd(dh0, x, nrm(0, 0), out_dtype=F32, add=dx1, name="norm_in_bwd_0",
                                                     ride=_chip_plan([sums[k_ie]]))

    g["norm_w"] = jnp.stack([jnp.concatenate([dnorm[(l, k)] for k in range(4)], axis=0) for l in range(DEPTH)])
    vec_parts = _run_plan(_gather_plan([_split_shards(g[n], SHARDED[n]) for n in VECTORS]), "vector_grads_all_gather")
    return loss, dx0, sums, recv, repl_parts, vec_parts


def kernel(x, norm_w, w_in_even, gla_w_a_up, gla_b_a, gla_norm_w, fox_b_f, w_out_even, w_in_odd, rel_bias, conv_w, conv_b, lru_w_a, lru_b_a, lru_w_x, lru_b_x, lru_lambda, w_out_odd, w_mlp_up, w_mlp_down, loss_target, m_norm_w, m_w_in_even, m_gla_w_a_up, m_gla_b_a, m_gla_norm_w, m_fox_b_f, m_w_out_even, m_w_in_odd, m_rel_bias, m_conv_w, m_conv_b, m_lru_w_a, m_lru_b_a, m_lru_w_x, m_lru_b_x, m_lru_lambda, m_w_out_odd, m_w_mlp_up, m_w_mlp_down, v_norm_w, v_w_in_even, v_gla_w_a_up, v_gla_b_a, v_gla_norm_w, v_fox_b_f, v_w_out_even, v_w_in_odd, v_rel_bias, v_conv_w, v_conv_b, v_lru_w_a, v_lru_b_a, v_lru_w_x, v_lru_b_x, v_lru_lambda, v_w_out_odd, v_w_mlp_up, v_w_mlp_down):
    wts = dict(zip(WEIGHTS, (norm_w, w_in_even, gla_w_a_up, gla_b_a, gla_norm_w, fox_b_f, w_out_even, w_in_odd, rel_bias,
                             conv_w, conv_b, lru_w_a, lru_b_a, lru_w_x, lru_b_x, lru_lambda, w_out_odd, w_mlp_up,
                             w_mlp_down)))
    mom = dict(zip(WEIGHTS, (m_norm_w, m_w_in_even, m_gla_w_a_up, m_gla_b_a, m_gla_norm_w, m_fox_b_f, m_w_out_even,
                             m_w_in_odd, m_rel_bias, m_conv_w, m_conv_b, m_lru_w_a, m_lru_b_a, m_lru_w_x, m_lru_b_x,
                             m_lru_lambda, m_w_out_odd, m_w_mlp_up, m_w_mlp_down)))
    var = dict(zip(WEIGHTS, (v_norm_w, v_w_in_even, v_gla_w_a_up, v_gla_b_a, v_gla_norm_w, v_fox_b_f, v_w_out_even,
                             v_w_in_odd, v_rel_bias, v_conv_w, v_conv_b, v_lru_w_a, v_lru_b_a, v_lru_w_x, v_lru_b_x,
                             v_lru_lambda, v_w_out_odd, v_w_mlp_up, v_w_mlp_down)))
    ax, ay, ac = lax.axis_index("x"), lax.axis_index("y"), lax.axis_index("c")
    place = jnp.stack([ac, 2 * ax + ay, 4 * ax + 2 * ay + ac]).astype(jnp.int32)

    shard = {(n, l): (wts[n][l].T if n in TRANSPOSED else wts[n][l]).astype(WIRE_DTYPE) for n, l in MATRIX_BLOCKS}
    loss_blk, dx, sums, recv, repl_parts, vec_parts = _forward_backward(
        x[0], loss_target[0], shard, {n: wts[n] for n in VECTORS}, {n: wts[n] for n in REPLICATED}, place)
    loss = lax.psum(loss_blk[0, 0], ("x", "y", "c"))

    view = lambda n, a: jnp.swapaxes(a, 1, 2) if n in TRANSPOSED else a
    upd = {n: [view(n, o) for o in _adamw_sharded(
        [(sums[(n, l)], recv[(n, l)]) for l in range(wts[n].shape[0])], view(n, wts[n]), view(n, mom[n]), view(n, var[n]),
        place, f"adamw_{n}")] for n in MATRICES}
    small = REPLICATED + list(VECTORS)
    upd.update(zip(small, _adamw_small(repl_parts, vec_parts, [wts[n] for n in small], [mom[n] for n in small],
                                       [var[n] for n in small], place)))
    return (loss, dx[None], *[upd[n][kind] for kind in range(4) for n in WEIGHTS])
```

```python
import functools
from typing import Callable, NamedTuple, Optional

import jax
import jax.numpy as jnp
from jax import lax
from jax.experimental import pallas as pl
from jax.experimental.pallas import tpu as pltpu

F32 = jnp.float32
MXU_DTYPE = jnp.bfloat16
ACT_DTYPE = jnp.bfloat16
WIRE_DTYPE = jnp.bfloat16

V7X_VMEM_BYTES = 64 * 1024 * 1024
VMEM_LIMIT = (V7X_VMEM_BYTES * 7) // 8
LANES = 128

D_MODEL = 1024
SEQ = 2048
DEPTH = 2
CHUNK = 64
GROUP_WIDTH = D_MODEL // 2
D_FF = 4 * D_MODEL
NORM_EPS = 1e-6
GLA_HEADS = 4
GLA_DV = GROUP_WIDTH // GLA_HEADS
GLA_DK = GLA_DV // 2
GLA_KW = GLA_HEADS * GLA_DK
GLA_RANK = 16
GLA_GATE_TAU = 16.0
HEAD_DIM = 64
ATT_HEADS = GROUP_WIDTH // HEAD_DIM
CA_LEFT = 8 * CHUNK
REL_CLIP = 128
LRU_BLOCK_DIM = 64
CONV_WIDTH = 4
LRU_C = 8.0
N_DEV = 8

ADAM_LR = 0.001
ADAM_B1 = 0.9
ADAM_B2 = 0.999
ADAM_EPS = 1e-08
ADAM_WD = 0.01
ADAM_STEP = 10

NEG = float(jnp.finfo(jnp.float32).min)
MESH = pl.DeviceIdType.MESH


def _params(*sem):
    return pltpu.CompilerParams(dimension_semantics=sem, vmem_limit_bytes=VMEM_LIMIT)


def _dot(a, b, ca=1, cb=0):
    return lax.dot_general(a.astype(MXU_DTYPE), b.astype(MXU_DTYPE), (((ca,), (cb,)), ((), ())),
                           preferred_element_type=F32)


def _dot_exact(a, b):
    return lax.dot_general(a, b, (((1,), (0,)), ((), ())), precision=lax.Precision.HIGHEST,
                           preferred_element_type=F32)


def _log_sigmoid(x):
    return jnp.minimum(x, 0.0) - jnp.log1p(jnp.exp(-jnp.abs(x)))


def _iota(shape, axis):
    return lax.broadcasted_iota(jnp.int32, shape, axis)


ANY = pl.BlockSpec(memory_space=pl.ANY)
N_CHIPS = 4


class _Plan(NamedTuple):
    ins: list
    outs: list
    sems: list
    start: Callable
    finish: Callable
    relay: Optional[Callable] = None


def _place():
    x, y, c = lax.axis_index("x"), lax.axis_index("y"), lax.axis_index("c")
    return x, y, c, [(1 - x, y), (x, 1 - y), (1 - x, 1 - y)]


def _gather_plan(xs):
    n = len(xs)

    def parts(x_refs, out_refs, sems):
        send_sems, recv_sems, local_sems = sems
        x, y, c, chips = _place()
        me, sibling = (x, y, c), (x, y, 1 - c)

        def rows(a, px, py, pc):
            return out_refs[a].at[4 * px + 2 * py + pc]

        def copy(a, k, block, to, src=None):
            return pltpu.make_async_remote_copy(
                src_ref=rows(a, *block) if src is None else src, dst_ref=rows(a, *block),
                send_sem=send_sems.at[7 * a + k], recv_sem=recv_sems.at[7 * a + k], device_id=to, device_id_type=MESH)

        def own():
            mine = [pltpu.make_async_copy(x_refs[a], rows(a, *me), local_sems.at[a]) for a in range(n)]
            first = []
            for a in range(n):
                first.append(copy(a, 0, me, sibling, src=x_refs[a]))
                first += [copy(a, 1 + j, me, (*chip, c), src=x_refs[a]) for j, chip in enumerate(chips)]
            return mine, first

        return c, me, sibling, chips, copy, own

    def start(x_refs, out_refs, sems):
        mine, first = parts(x_refs, out_refs, sems)[-1]()
        for cp in first + mine:
            cp.start()

    def relay(x_refs, out_refs, sems):
        c, me, sibling, chips, copy, _ = parts(x_refs, out_refs, sems)
        for j, chip in enumerate(chips):
            for a in range(n):
                copy(a, 1 + j, (*chip, c), me).wait_recv()
                copy(a, 4 + j, (*chip, c), sibling).start()

    def finish(x_refs, out_refs, sems):
        c, me, sibling, chips, copy, own = parts(x_refs, out_refs, sems)
        mine, first = own()
        for a in range(n):
            copy(a, 0, sibling, me).wait_recv()
            for j, chip in enumerate(chips):
                copy(a, 4 + j, (*chip, 1 - c), me).wait_recv()
        for cp in first + [copy(a, 4 + j, (*chip, c), sibling) for j, chip in enumerate(chips) for a in range(n)]:
            cp.wait_send()
        for cp in mine:
            cp.wait()

    return _Plan(list(xs), [jax.ShapeDtypeStruct((N_DEV,) + x.shape, x.dtype) for x in xs],
                 [pltpu.SemaphoreType.DMA((7 * n,)), pltpu.SemaphoreType.DMA((7 * n,)), pltpu.SemaphoreType.DMA((n,))],
                 start, finish, relay)


def _exchange_plan(copies_of, ins, outs, per_array):
    n = len(ins)

    def start(in_refs, out_refs, sems):
        for cp in copies_of(in_refs, out_refs, sems):
            cp.start()

    def finish(in_refs, out_refs, sems):
        copies = copies_of(in_refs, out_refs, sems)
        for cp in copies:
            cp.wait_recv()
        for cp in copies:
            cp.wait_send()

    return _Plan(list(ins), outs, [pltpu.SemaphoreType.DMA((per_array * n,)), pltpu.SemaphoreType.DMA((per_array * n,))],
                 start, finish)


def _sibling_plan(gs):
    def copies_of(g_refs, got_refs, sems):
        x, y, c, _ = _place()
        return [pltpu.make_async_remote_copy(
            src_ref=g_refs[a].at[2 * k + (1 - c)], dst_ref=got_refs[a].at[k], send_sem=sems[0].at[N_CHIPS * a + k],
            recv_sem=sems[1].at[N_CHIPS * a + k], device_id=(x, y, 1 - c), device_id_type=MESH)
            for a in range(len(gs)) for k in range(N_CHIPS)]

    return _exchange_plan(copies_of, gs, [jax.ShapeDtypeStruct((N_CHIPS,) + g.shape[1:], g.dtype) for g in gs], N_CHIPS)


def _chip_plan(ss):
    def copies_of(s_refs, out_refs, sems):
        x, y, c, chips = _place()
        return [pltpu.make_async_remote_copy(
            src_ref=s_refs[a].at[2 * px + py], dst_ref=out_refs[a].at[j], send_sem=sems[0].at[3 * a + j],
            recv_sem=sems[1].at[3 * a + j], device_id=(px, py, c), device_id_type=MESH)
            for a in range(len(ss)) for j, (px, py) in enumerate(chips)]

    return _exchange_plan(copies_of, ss, [jax.ShapeDtypeStruct((3,) + s.shape[1:], s.dtype) for s in ss], 3)


def _join_plans(*plans):
    def cut(refs, counts):
        at = 0
        for n in counts:
            yield refs[at:at + n]
            at += n

    def each(in_refs, out_refs, sems):
        return zip(plans, cut(in_refs, [len(p.ins) for p in plans]), cut(out_refs, [len(p.outs) for p in plans]),
                   cut(sems, [len(p.sems) for p in plans]))

    def start(*refs):
        for p, i, o, s in each(*refs):
            p.start(i, o, s)

    def relay(*refs):
        for p, i, o, s in each(*refs):
            if p.relay is not None:
                p.relay(i, o, s)

    def finish(*refs):
        for p, i, o, s in each(*refs):
            p.finish(i, o, s)

    return _Plan([a for p in plans for a in p.ins], [a for p in plans for a in p.outs],
                 [a for p in plans for a in p.sems], start, finish, relay)


def _run_plan(plan, name):
    n_in, n_out = len(plan.ins), len(plan.outs)

    def body(*refs):
        args = refs[:n_in], refs[n_in:n_in + n_out], refs[n_in + n_out:]
        plan.start(*args)
        if plan.relay is not None:
            plan.relay(*args)
        plan.finish(*args)

    return pl.pallas_call(body, out_shape=plan.outs, in_specs=[ANY] * n_in, out_specs=[ANY] * n_out,
                          scratch_shapes=plan.sems, name=name)(*plan.ins)


def _pcall(body, ride, *, grid, in_specs, out_specs, out_shape, scratch_shapes=(), semantics, name):
    if ride is None:
        return pl.pallas_call(body, grid=grid, in_specs=in_specs, out_specs=out_specs, out_shape=out_shape,
                              scratch_shapes=list(scratch_shapes), compiler_params=_params(*semantics), name=name)
    single = not isinstance(out_shape, (list, tuple))
    out_specs_l, out_shape_l = ([out_specs], [out_shape]) if single else (list(out_specs), list(out_shape))
    n_in, n_out, n_scr = len(in_specs), len(out_shape_l), len(scratch_shapes)
    r_in, r_out = len(ride.ins), len(ride.outs)

    def riding(*refs):
        cuts = [n_in, r_in, n_out, r_out, n_scr]
        groups, at = [], 0
        for width in cuts:
            groups.append(refs[at:at + width])
            at += width
        ins, r_ins, outs, r_outs, scr = groups
        sems = refs[at:]
        first = functools.reduce(jnp.logical_and, [pl.program_id(d) == 0 for d in range(len(grid))])
        last = functools.reduce(jnp.logical_and, [pl.program_id(d) == grid[d] - 1 for d in range(len(grid))])

        @pl.when(first)
        def _():
            ride.start(r_ins, r_outs, sems)

        several_steps = any(n > 1 for n in grid)
        if ride.relay is not None and several_steps:
            @pl.when(last)
            def _():
                ride.relay(r_ins, r_outs, sems)

        body(*ins, *outs, *scr)

        @pl.when(last)
        def _():
            if ride.relay is not None and not several_steps:
                ride.relay(r_ins, r_outs, sems)
            ride.finish(r_ins, r_outs, sems)

    call = pl.pallas_call(
        riding, grid=grid, in_specs=list(in_specs) + [ANY] * r_in, out_specs=out_specs_l + [ANY] * r_out,
        out_shape=out_shape_l + list(ride.outs), scratch_shapes=list(scratch_shapes) + list(ride.sems),
        compiler_params=_params(*(["arbitrary"] * len(grid))), name=name)

    def run(*args):
        res = call(*args, *ride.ins)
        return (res[0] if single else list(res[:n_out])), list(res[n_out:])

    return run


def _rms(x):
    return x * lax.rsqrt(jnp.mean(x * x, axis=-1, keepdims=True) + NORM_EPS)


def _mm(a, b, *, nt=False, ta=False, out_dtype, tm, tn, a_sqrelu=False, drelu_of=None, b_blocked=False,
        out_blocked=False, a_norm=None, a_norm_bwd=None, res_norm=None, norm_bwd=None, name, ride=None):
    k, m = a.shape if ta else a.shape[::-1]
    if b_blocked:
        assert not nt and b.shape[1] == k and b.shape[2] == tn
        n = b.shape[0] * tn
    else:
        n = b.shape[0] if nt else b.shape[1]
        assert (b.shape[1] if nt else b.shape[0]) == k
    tm, tn = min(tm, m), min(tn, n)
    assert m % tm == 0 and n % tn == 0
    assert (res_norm is None and norm_bwd is None) or tn == n
    assert a_norm is None or a_norm_bwd is None
    n_in = (2 + (drelu_of is not None) + (a_norm is not None) + 2 * (a_norm_bwd is not None)
            + 2 * (res_norm is not None) + 3 * (norm_bwd is not None))

    def body(*refs):
        a_ref, b_ref = refs[0], refs[1]
        extra = list(refs[2:n_in])
        outs = list(refs[n_in:])
        o_ref = outs.pop(0)
        u_ref = extra.pop(0) if drelu_of is not None else None
        if a_norm is not None:
            wn_ref, h_ref, h_scr = extra.pop(0), outs.pop(0), outs.pop()

            @pl.when(pl.program_id(1) == 0)
            def _():
                h = (_rms(a_ref[...]) * wn_ref[...]).astype(ACT_DTYPE)
                h_scr[...] = h
                h_ref[...] = h

            av = h_scr[...]
        elif a_norm_bwd is not None:
            y_ref, wy_ref = extra.pop(0), extra.pop(0)
            dy_ref, dwy_ref, dy_scr = outs.pop(0), outs.pop(0), outs.pop()
            first_rows = pl.program_id(0) == 0

            @pl.when(pl.program_id(1) == 0)
            def _():
                yv, up = y_ref[...], a_ref[...]
                rstd = lax.rsqrt(jnp.mean(yv * yv, axis=-1, keepdims=True) + NORM_EPS)
                yhat = yv * rstd
                g = up * wy_ref[...]
                dy = (rstd * (g - yhat * jnp.mean(g * yhat, axis=-1, keepdims=True))).astype(ACT_DTYPE)
                dy_scr[...] = dy
                dy_ref[...] = dy

                @pl.when(first_rows)
                def _():
                    dwy_ref[...] = jnp.zeros_like(dwy_ref)

                dwy_ref[...] += jnp.sum(up * yhat, axis=0, keepdims=True)

            av = dy_scr[...]
        else:
            av = a_ref[...]
        if a_sqrelu:
            av = jnp.square(jnp.maximum(av.astype(F32), 0.0))
        acc = _dot(av, b_ref[...], 0 if ta else 1, 1 if nt else 0)
        if u_ref is not None:
            acc = acc * (2.0 * jnp.maximum(u_ref[...].astype(F32), 0.0))
        if norm_bwd is not None:
            x_ref, wb_ref, add_ref = extra
            dw_ref = outs[0]
            xv = x_ref[...]
            rstd = lax.rsqrt(jnp.mean(xv * xv, axis=-1, keepdims=True) + NORM_EPS)
            xhat = xv * rstd
            g = acc * wb_ref[...]
            o_ref[...] = rstd * (g - xhat * jnp.mean(g * xhat, axis=-1, keepdims=True)) + add_ref[...]

            @pl.when(pl.program_id(0) == 0)
            def _():
                dw_ref[...] = jnp.zeros_like(dw_ref)

            dw_ref[...] += jnp.sum(acc * xhat, axis=0, keepdims=True)
            return
        o_ref[...] = acc.astype(out_dtype)
        if res_norm is not None:
            res_ref, wr_ref = extra
            outs[0][...] = res_ref[...] + _rms(acc) * wr_ref[...]

    if b_blocked:
        b_spec = pl.BlockSpec((None, k, tn), lambda i, j: (j, 0, 0))
    elif nt:
        b_spec = pl.BlockSpec((tn, k), lambda i, j: (j, 0))
    else:
        b_spec = pl.BlockSpec((k, tn), lambda i, j: (0, j))
    a_spec = pl.BlockSpec((k, tm), lambda i, j: (0, i)) if ta else pl.BlockSpec((tm, k), lambda i, j: (i, 0))
    in_specs = [a_spec, b_spec]
    args = [a, b]
    if drelu_of is not None:
        in_specs.append(pl.BlockSpec((tm, tn), lambda i, j: (i, j)))
        args.append(drelu_of)
    if out_blocked:
        out_specs = [pl.BlockSpec((None, tm, tn), lambda i, j: (j, i, 0))]
        out_shape = [jax.ShapeDtypeStruct((n // tn, m, tn), out_dtype)]
    else:
        out_specs = [pl.BlockSpec((tm, tn), lambda i, j: (i, j))]
        out_shape = [jax.ShapeDtypeStruct((m, n), out_dtype)]
    scratch = []
    if a_norm is not None:
        assert not ta
        in_specs.append(pl.BlockSpec((1, k), lambda i, j: (0, 0)))
        args.append(a_norm)
        out_specs.append(pl.BlockSpec((tm, k), lambda i, j: (i, 0)))
        out_shape.append(jax.ShapeDtypeStruct((m, k), ACT_DTYPE))
        scratch.append(pltpu.VMEM((tm, k), ACT_DTYPE))
    if a_norm_bwd is not None:
        assert not ta
        in_specs += [pl.BlockSpec((tm, k), lambda i, j: (i, 0)), pl.BlockSpec((1, k), lambda i, j: (0, 0))]
        args += list(a_norm_bwd)
        out_specs += [pl.BlockSpec((tm, k), lambda i, j: (i, 0)), pl.BlockSpec((1, k), lambda i, j: (0, 0))]
        out_shape += [jax.ShapeDtypeStruct((m, k), ACT_DTYPE), jax.ShapeDtypeStruct((1, k), F32)]
        scratch.append(pltpu.VMEM((tm, k), ACT_DTYPE))
    if res_norm is not None:
        in_specs += [pl.BlockSpec((tm, n), lambda i, j: (i, 0)), pl.BlockSpec((1, n), lambda i, j: (0, 0))]
        args += list(res_norm)
        out_specs.append(pl.BlockSpec((tm, n), lambda i, j: (i, 0)))
        out_shape.append(jax.ShapeDtypeStruct((m, n), F32))
    if norm_bwd is not None:
        rows = pl.BlockSpec((tm, n), lambda i, j: (i, 0))
        in_specs += [rows, pl.BlockSpec((1, n), lambda i, j: (0, 0)), rows]
        args += list(norm_bwd)
        out_specs.append(pl.BlockSpec((1, n), lambda i, j: (0, 0)))
        out_shape.append(jax.ShapeDtypeStruct((1, n), F32))
    single = len(out_shape) == 1
    return _pcall(body, ride, grid=(m // tm, n // tn), in_specs=in_specs,
                  out_specs=out_specs[0] if single else out_specs, out_shape=out_shape[0] if single else out_shape,
                  scratch_shapes=scratch, semantics=("arbitrary", "arbitrary"), name=name)(*args)


ROW_TILE = 512
TM_FWD, TM_DX, TM_DW, TN = 2048, 1024, 1024, 512


def _norm_bwd(dy, x, w, *, out_dtype, add=None, name, ride=None):
    t, d = x.shape

    def body(*refs):
        dy_ref, x_ref, w_ref = refs[0], refs[1], refs[2]
        dx_ref, dw_ref = refs[-2], refs[-1]
        xv = x_ref[...]
        rstd = lax.rsqrt(jnp.mean(xv * xv, axis=-1, keepdims=True) + NORM_EPS)
        xhat = xv * rstd
        dyv = dy_ref[...].astype(F32)
        g = dyv * w_ref[...]
        dx = rstd * (g - xhat * jnp.mean(g * xhat, axis=-1, keepdims=True))
        if add is not None:
            dx = dx + refs[3][...]
        dx_ref[...] = dx.astype(out_dtype)

        @pl.when(pl.program_id(0) == 0)
        def _():
            dw_ref[...] = jnp.zeros_like(dw_ref)

        dw_ref[...] += jnp.sum(dyv * xhat, axis=0, keepdims=True)

    row = pl.BlockSpec((ROW_TILE, d), lambda i: (i, 0))
    vec = pl.BlockSpec((1, d), lambda i: (0, 0))
    in_specs = [row, row, vec] + ([row] if add is not None else [])
    args = [dy, x, w] + ([add] if add is not None else [])
    return _pcall(body, ride, grid=(t // ROW_TILE,), in_specs=in_specs, out_specs=[row, vec],
                  out_shape=[jax.ShapeDtypeStruct((t, d), out_dtype), jax.ShapeDtypeStruct((1, d), F32)],
                  semantics=("arbitrary",), name=name)(*args)


def _loss_fwd_bwd(y, target):
    t, d = y.shape

    def body(y_ref, t_ref, l_ref, dy_ref):
        diff = y_ref[...] - t_ref[...]
        dy_ref[...] = diff * (1.0 / d)

        @pl.when(pl.program_id(0) == 0)
        def _():
            l_ref[...] = jnp.zeros_like(l_ref)

        l_ref[...] += 0.5 * jnp.sum(jnp.mean(diff * diff, axis=-1, keepdims=True), axis=0, keepdims=True)

    row = pl.BlockSpec((ROW_TILE, d), lambda i: (i, 0))
    return pl.pallas_call(body, grid=(t // ROW_TILE,), in_specs=[row, row],
                          out_specs=[pl.BlockSpec((8, LANES), lambda i: (0, 0)), row],
                          out_shape=[jax.ShapeDtypeStruct((8, LANES), F32), jax.ShapeDtypeStruct((t, d), F32)],
                          compiler_params=_params("arbitrary"), name="loss")(y, target)


GLA_STATE = (GLA_HEADS * GLA_DV, GLA_KW)


def _gla_specs(chunk_of):
    rows = lambda width, col: pl.BlockSpec((CHUNK, width), lambda i: (chunk_of(i), col))
    const = lambda r, c: pl.BlockSpec((r, c), lambda i: (0, 0))
    return [rows(GLA_KW, 0),
            rows(GLA_KW, 1),
            rows(GROUP_WIDTH, 1),
            rows(GROUP_WIDTH, 0),
            rows(LANES, 4),
            const(LANES, GLA_KW),
            const(1, GLA_KW),
            const(1, GROUP_WIDTH)]


def _gla_chunk(q_ref, k_ref, v_ref, a_ref, wup_ref, ba_ref):
    z = _dot(a_ref[...], wup_ref[...]) + ba_ref[...]
    tri = (_iota((CHUNK, CHUNK), 1) <= _iota((CHUNK, CHUNK), 0)).astype(F32)
    cum = _dot_exact(tri, _log_sigmoid(z) * (1.0 / GLA_GATE_TAU))
    tot = cum[CHUNK - 1:CHUNK, :]
    e = jnp.exp(tot - cum)
    return (z, e, jnp.exp(tot), k_ref[...].astype(F32) * e, q_ref[...].astype(F32) * (GLA_DK ** -0.5),
            v_ref[...].astype(F32))


def _gla_head_mask():
    return _iota(GLA_STATE, 0) // GLA_DV == _iota(GLA_STATE, 1) // GLA_DK


def _gla_fwd(pmm, pel, w_up, b_a, gnorm_w, ride=None):
    t = pmm.shape[0]
    nc = t // CHUNK

    def body(q_ref, k_ref, v_ref, r_ref, a_ref, wup_ref, ba_ref, gw_ref, o_ref, st_ref, m_scr):
        @pl.when(pl.program_id(0) == 0)
        def _():
            m_scr[...] = jnp.zeros_like(m_scr)

        _, _, decay, kd, qs, vv = _gla_chunk(q_ref, k_ref, v_ref, a_ref, wup_ref, ba_ref)
        m = m_scr[...] * decay + jnp.where(_gla_head_mask(), _dot(vv, kd, 0, 0), 0.0)
        m_scr[...] = m
        st_ref[...] = m
        o = _dot(qs, m, 1, 1)
        rr = r_ref[...]
        gate = rr * jax.nn.sigmoid(rr) * gw_ref[...]
        for h in range(GLA_HEADS):
            vs = slice(h * GLA_DV, (h + 1) * GLA_DV)
            oh = o[:, vs]
            y = oh * lax.rsqrt(jnp.mean(oh * oh, axis=-1, keepdims=True) + NORM_EPS)
            o_ref[:, vs] = (y * gate[:, vs]).astype(o_ref.dtype)

    return _pcall(
        body, ride, grid=(nc,), in_specs=_gla_specs(lambda i: i),
        out_specs=[pl.BlockSpec((CHUNK, GROUP_WIDTH), lambda i: (i, 0)),
                   pl.BlockSpec((None,) + GLA_STATE, lambda i: (i, 0, 0))],
        out_shape=[jax.ShapeDtypeStruct((t, GROUP_WIDTH), ACT_DTYPE), jax.ShapeDtypeStruct((nc,) + GLA_STATE, F32)],
        scratch_shapes=[pltpu.VMEM(GLA_STATE, F32)],
        semantics=("arbitrary",), name="gla_fwd")(pmm, pmm, pmm, pel, pel, w_up, b_a, gnorm_w)


def _gla_bwd(pmm, pel, w_up, b_a, gnorm_w, states, dmix, ride=None):
    t = pmm.shape[0]
    nc = t // CHUNK
    scale = GLA_DK ** -0.5

    def body(q_ref, k_ref, v_ref, r_ref, a_ref, wup_ref, ba_ref, gw_ref, st_ref, prev_ref, do_ref,
             dq_ref, dk_ref, dv_ref, dr_ref, da_ref, dwup_ref, dba_ref, dgw_ref, dm_scr):
        step = pl.program_id(0)

        @pl.when(step == 0)
        def _():
            dm_scr[...] = jnp.zeros_like(dm_scr)
            dwup_ref[...] = jnp.zeros_like(dwup_ref)
            dba_ref[...] = jnp.zeros_like(dba_ref)
            dgw_ref[...] = jnp.zeros_like(dgw_ref)

        z, e, decay, kd, qs, vv = _gla_chunk(q_ref, k_ref, v_ref, a_ref, wup_ref, ba_ref)
        m = st_ref[...]
        m_prev = prev_ref[...] * (step < nc - 1).astype(F32)
        rr, dout, gw = r_ref[...], do_ref[...], gw_ref[...]
        sig = jax.nn.sigmoid(rr)
        silu = rr * sig
        dsilu = sig * (1.0 + rr * (1.0 - sig))
        o = _dot(qs, m, 1, 1)
        d_o, dgw = [], []
        for h in range(GLA_HEADS):
            vs = slice(h * GLA_DV, (h + 1) * GLA_DV)
            oh, dg = o[:, vs], dout[:, vs]
            rstd = lax.rsqrt(jnp.mean(oh * oh, axis=-1, keepdims=True) + NORM_EPS)
            y = oh * rstd
            dgw.append(jnp.sum(dg * y * silu[:, vs], axis=0, keepdims=True))
            dr_ref[:, vs] = (dg * y * gw[:, vs] * dsilu[:, vs]).astype(dr_ref.dtype)
            dy = dg * gw[:, vs] * silu[:, vs]
            d_o.append(rstd * (dy - y * jnp.mean(dy * y, axis=-1, keepdims=True)))
        d_o = jnp.concatenate(d_o, axis=1)
        dgw_ref[...] += jnp.concatenate(dgw, axis=1)
        dq_ref[...] = (_dot(d_o, m) * scale).astype(dq_ref.dtype)
        dm = dm_scr[...] + jnp.where(_gla_head_mask(), _dot(d_o, qs, 0, 0), 0.0)
        dv_ref[...] = _dot(kd, dm, 1, 1).astype(dv_ref.dtype)
        dkd = _dot(vv, dm)
        dk_ref[...] = (dkd * e).astype(dk_ref.dtype)
        dm_scr[...] = dm * decay
        tri_strict = (_iota((CHUNK, CHUNK), 1) < _iota((CHUNK, CHUNK), 0)).astype(F32)
        dla = jnp.sum(dm * m_prev, axis=0, keepdims=True) * decay + _dot_exact(tri_strict, dkd * kd)
        dz = dla * jax.nn.sigmoid(-z) * (1.0 / GLA_GATE_TAU)
        da_ref[...] = _dot(dz, wup_ref[...], 1, 1).astype(da_ref.dtype)
        dwup_ref[...] += _dot(a_ref[...], dz, 0, 0)
        dba_ref[...] += jnp.sum(dz, axis=0, keepdims=True)

    chunk_of = lambda i: nc - 1 - i
    in_specs = _gla_specs(chunk_of) + [
        pl.BlockSpec((None,) + GLA_STATE, lambda i: (chunk_of(i), 0, 0)),
        pl.BlockSpec((None,) + GLA_STATE, lambda i: (jnp.maximum(chunk_of(i) - 1, 0), 0, 0)),
        pl.BlockSpec((CHUNK, GROUP_WIDTH), lambda i: (chunk_of(i), 0))]
    rows = lambda width: pl.BlockSpec((CHUNK, width), lambda i: (chunk_of(i), 0))
    const = lambda r, c: pl.BlockSpec((r, c), lambda i: (0, 0))
    return _pcall(
        body, ride, grid=(nc,), in_specs=in_specs,
        out_specs=[rows(GLA_KW), rows(GLA_KW), rows(GROUP_WIDTH), rows(GROUP_WIDTH), rows(LANES),
                   const(LANES, GLA_KW), const(1, GLA_KW), const(1, GROUP_WIDTH)],
        out_shape=[jax.ShapeDtypeStruct((t, GLA_KW), ACT_DTYPE), jax.ShapeDtypeStruct((t, GLA_KW), ACT_DTYPE),
                   jax.ShapeDtypeStruct((t, GROUP_WIDTH), ACT_DTYPE), jax.ShapeDtypeStruct((t, GROUP_WIDTH), ACT_DTYPE),
                   jax.ShapeDtypeStruct((t, LANES), ACT_DTYPE), jax.ShapeDtypeStruct((LANES, GLA_KW), F32),
                   jax.ShapeDtypeStruct((1, GLA_KW), F32), jax.ShapeDtypeStruct((1, GROUP_WIDTH), F32)],
        scratch_shapes=[pltpu.VMEM(GLA_STATE, F32)],
        semantics=("arbitrary",), name="gla_bwd")(
            pmm, pmm, pmm, pel, pel, w_up, b_a, gnorm_w, states, states, dmix)


CUM_BLOCK = 256


def _fox_gate_fwd(pel, b_f):
    t = pel.shape[0]
    nb = t // CUM_BLOCK

    def body(f_ref, b_ref, cum_ref, cum_t_ref):
        tri = (_iota((CUM_BLOCK, CUM_BLOCK), 1) <= _iota((CUM_BLOCK, CUM_BLOCK), 0)).astype(F32)
        carry = jnp.zeros((1, LANES), F32)
        for blk in range(nb):
            rows = slice(blk * CUM_BLOCK, (blk + 1) * CUM_BLOCK)
            cum = _dot_exact(tri, _log_sigmoid(f_ref[rows, :] + b_ref[...])) + carry
            cum_ref[rows, :] = cum
            cum_t_ref[blk] = cum.T[:ATT_HEADS, :]
            carry = cum[CUM_BLOCK - 1:CUM_BLOCK, :]

    return pl.pallas_call(
        body, grid=(1,),
        in_specs=[pl.BlockSpec((t, LANES), lambda i: (0, 5)), pl.BlockSpec((1, LANES), lambda i: (0, 0))],
        out_specs=[pl.BlockSpec((t, LANES), lambda i: (0, 0)),
                   pl.BlockSpec((nb, ATT_HEADS, CUM_BLOCK), lambda i: (0, 0, 0))],
        out_shape=[jax.ShapeDtypeStruct((t, LANES), F32), jax.ShapeDtypeStruct((nb, ATT_HEADS, CUM_BLOCK), F32)],
        compiler_params=_params("arbitrary"), name="fox_gate_fwd")(pel, b_f)


def _fox_gate_bwd(pel, b_f, dcum_t, dcum_q):
    t = pel.shape[0]
    nb = t // CUM_BLOCK

    def body(f_ref, b_ref, dct_ref, dcq_ref, df_ref, db_ref):
        tri_up = (_iota((CUM_BLOCK, CUM_BLOCK), 1) >= _iota((CUM_BLOCK, CUM_BLOCK), 0)).astype(F32)
        carry = jnp.zeros((1, LANES), F32)
        db = jnp.zeros((1, LANES), F32)
        for blk in reversed(range(nb)):
            rows = slice(blk * CUM_BLOCK, (blk + 1) * CUM_BLOCK)
            query_side = sum(dcq_ref[pair, rows, :] for pair in range(dcq_ref.shape[0]))
            dls = _dot_exact(tri_up, dct_ref[blk].T + query_side) + carry
            carry = dls[0:1, :]
            df = dls * jax.nn.sigmoid(-(f_ref[rows, :] + b_ref[...]))
            df_ref[rows, :] = df.astype(df_ref.dtype)
            db = db + jnp.sum(df, axis=0, keepdims=True)
        db_ref[...] = db

    return pl.pallas_call(
        body, grid=(1,),
        in_specs=[pl.BlockSpec((t, LANES), lambda i: (0, 5)), pl.BlockSpec((1, LANES), lambda i: (0, 0)),
                  pl.BlockSpec((nb, LANES, CUM_BLOCK), lambda i: (0, 0, 0)),
                  pl.BlockSpec((dcum_q.shape[0], t, LANES), lambda i: (0, 0, 0))],
        out_specs=[pl.BlockSpec((t, LANES), lambda i: (0, 0)), pl.BlockSpec((1, LANES), lambda i: (0, 0))],
        out_shape=[jax.ShapeDtypeStruct((t, LANES), ACT_DTYPE), jax.ShapeDtypeStruct((1, LANES), F32)],
        compiler_params=_params("arbitrary"), name="fox_gate_bwd")(pel, b_f, dcum_t, dcum_q)


FOX_Q_BLOCK = 256


assert FOX_Q_BLOCK == CUM_BLOCK
FOX_KEY_STEP = 512


def _fox_scores(q_ref, k_ref, cum_ref, cum_t_ref, h, i):
    hs = slice(h * HEAD_DIM, (h + 1) * HEAD_DIM)
    nb = cum_t_ref.shape[0]
    key_gate = jnp.concatenate([cum_t_ref[kb, h:h + 1, :] for kb in range(nb)], axis=1)
    s = _dot(q_ref[:, hs], k_ref[:, hs], 1, 1) * (HEAD_DIM ** -0.5) + (cum_ref[:, h:h + 1] - key_gate)
    shape = (FOX_Q_BLOCK, nb * FOX_Q_BLOCK)
    return jnp.where(_iota(shape, 1) <= i * FOX_Q_BLOCK + _iota(shape, 0), s, NEG)


def _fox_specs(t):
    bq, nb = FOX_Q_BLOCK, t // FOX_Q_BLOCK
    return [pl.BlockSpec((bq, GROUP_WIDTH), lambda i: (i, 2)), pl.BlockSpec((t, GROUP_WIDTH), lambda i: (0, 3)),
            pl.BlockSpec((t, GROUP_WIDTH), lambda i: (0, 4)), pl.BlockSpec((bq, LANES), lambda i: (i, 0)),
            pl.BlockSpec((nb, ATT_HEADS, bq), lambda i: (0, 0, 0))]


def _fox_fwd(pmm, cum, cum_t, ride=None):
    t = pmm.shape[0]
    bq = FOX_Q_BLOCK

    def body(q_ref, k_ref, v_ref, cum_ref, cum_t_ref, o_ref, lse_ref):
        i = pl.program_id(0)
        lse_ref[...] = jnp.zeros_like(lse_ref)
        for h in range(ATT_HEADS):
            hs = slice(h * HEAD_DIM, (h + 1) * HEAD_DIM)
            s = _fox_scores(q_ref, k_ref, cum_ref, cum_t_ref, h, i)
            m = jnp.max(s, axis=-1, keepdims=True)
            p = jnp.exp(s - m)
            l = jnp.sum(p, axis=-1, keepdims=True)
            o_ref[:, hs] = (_dot(p, v_ref[:, hs]) / l).astype(o_ref.dtype)
            lse_ref[:, h:h + 1] = m + jnp.log(l)

    return _pcall(
        body, ride, grid=(t // bq,), in_specs=_fox_specs(t),
        out_specs=[pl.BlockSpec((bq, GROUP_WIDTH), lambda i: (i, 0)), pl.BlockSpec((bq, LANES), lambda i: (i, 0))],
        out_shape=[jax.ShapeDtypeStruct((t, GROUP_WIDTH), ACT_DTYPE), jax.ShapeDtypeStruct((t, LANES), F32)],
        semantics=("parallel",), name="fox_fwd")(pmm, pmm, pmm, cum, cum_t)


def _fox_bwd(pmm, cum, cum_t, lse, dmix, ride=None):
    t = pmm.shape[0]
    bq, nb = FOX_Q_BLOCK, t // FOX_Q_BLOCK
    pairs, per_pair = ATT_HEADS // 2, LANES // HEAD_DIM
    scale = HEAD_DIM ** -0.5

    def body(q_ref, k_ref, v_ref, cum_ref, cum_t_ref, lse_ref, do_ref, dq_ref, dk_ref, dv_ref, dct_ref, dcq_ref):
        g, i = pl.program_id(0), pl.program_id(1)

        @pl.when(i == 0)
        def _():
            dk_ref[...] = jnp.zeros_like(dk_ref)
            dv_ref[...] = jnp.zeros_like(dv_ref)

        @pl.when((i == 0) & (g == 0))
        def _():
            dct_ref[...] = jnp.zeros_like(dct_ref)

        lane = _iota((1, LANES), 1)

        def run(n):
            causal = _iota((bq, n), 1) <= i * bq + _iota((bq, n), 0)
            dcq = jnp.zeros((bq, LANES), F32)
            for hh in range(per_pair):
                h = per_pair * g + hh
                hs = slice(hh * HEAD_DIM, (hh + 1) * HEAD_DIM)
                pick = (lane == h).astype(F32)
                cq = jnp.sum(cum_ref[...] * pick, axis=1, keepdims=True)
                lse_h = jnp.sum(lse_ref[...] * pick, axis=1, keepdims=True)
                key_gate = jnp.concatenate([cum_t_ref[kb, pl.ds(h, 1), :] for kb in range(n // bq)], axis=1)
                s = _dot(q_ref[:, hs], k_ref[:n, hs], 1, 1) * scale + (cq - key_gate)
                p = jnp.exp(jnp.where(causal, s, NEG) - lse_h)
                do = do_ref[:, hs]
                dp = _dot(do, v_ref[:n, hs], 1, 1)
                ds = p * (dp - jnp.sum(p * dp, axis=-1, keepdims=True))
                dq_ref[:, hs] = (_dot(ds, k_ref[:n, hs]) * scale).astype(dq_ref.dtype)
                dk_ref[:n, hs] += _dot(ds, q_ref[:, hs], 0, 0) * scale
                dv_ref[:n, hs] += _dot(p, do, 0, 0)
                key_side = -jnp.sum(ds, axis=0, keepdims=True)
                for kb in range(n // bq):
                    dct_ref[kb, pl.ds(h, 1), :] += key_side[:, kb * bq:(kb + 1) * bq]
                dcq = dcq + jnp.sum(ds, axis=1, keepdims=True) * pick
            dcq_ref[...] = dcq

        for kx in range(t // FOX_KEY_STEP):
            pl.when(i // (FOX_KEY_STEP // bq) == kx)(functools.partial(run, (kx + 1) * FOX_KEY_STEP))

    cols = lambda first: pl.BlockSpec((bq, LANES), lambda g, i: (i, first + g))
    keys = lambda first: pl.BlockSpec((t, LANES), lambda g, i: (0, first + g))
    per_head = pl.BlockSpec((bq, LANES), lambda g, i: (i, 0))
    fox_q, fox_k, fox_v = (GROUP_WIDTH * n // LANES for n in (2, 3, 4))
    return _pcall(
        body, ride, grid=(pairs, t // bq),
        in_specs=[cols(fox_q), keys(fox_k), keys(fox_v), per_head,
                  pl.BlockSpec((nb, ATT_HEADS, bq), lambda g, i: (0, 0, 0)), per_head, cols(GROUP_WIDTH // LANES)],
        out_specs=[cols(0), keys(0), keys(0), pl.BlockSpec((nb, LANES, bq), lambda g, i: (0, 0, 0)),
                   pl.BlockSpec((None, bq, LANES), lambda g, i: (g, i, 0))],
        out_shape=[jax.ShapeDtypeStruct((t, GROUP_WIDTH), ACT_DTYPE), jax.ShapeDtypeStruct((t, GROUP_WIDTH), F32),
                   jax.ShapeDtypeStruct((t, GROUP_WIDTH), F32), jax.ShapeDtypeStruct((nb, LANES, bq), F32),
                   jax.ShapeDtypeStruct((pairs, t, LANES), F32)],
        semantics=("arbitrary", "arbitrary"), name="fox_bwd")(pmm, pmm, pmm, cum, cum_t, lse, dmix)


CA_Q_BLOCK = 4 * CHUNK
CA_WINDOW = CA_Q_BLOCK + CA_LEFT
CA_BASE = 1024


def _ca_bias_base(rel_bias):
    n = rel_bias.shape[0]
    flat = CA_Q_BLOCK + CA_LEFT - REL_CLIP
    tail = CA_BASE - flat - (2 * REL_CLIP + 1)
    return jnp.concatenate([jnp.broadcast_to(rel_bias[:, 2 * REL_CLIP:], (n, flat)), rel_bias[:, ::-1],
                            jnp.broadcast_to(rel_bias[:, :1], (n, tail))], axis=1)


def _ca_bias_base_grad(dbase):
    flat = CA_Q_BLOCK + CA_LEFT - REL_CLIP
    mid = dbase[:, flat:flat + 2 * REL_CLIP + 1][:, ::-1]
    lo = jnp.sum(dbase[:, flat + 2 * REL_CLIP + 1:], axis=1, keepdims=True)
    hi = jnp.sum(dbase[:, :flat], axis=1, keepdims=True)
    pad = jnp.zeros((dbase.shape[0], 2 * REL_CLIP - 1), F32)
    return mid + jnp.concatenate([lo, pad, hi], axis=1)


def _ca_mask(i):
    r, j = _iota((CA_Q_BLOCK, CA_WINDOW), 0), _iota((CA_Q_BLOCK, CA_WINDOW), 1)
    rc, jc = r // CHUNK, j // CHUNK
    return (jc >= rc) & (jc <= rc + CA_LEFT // CHUNK) & (i * CA_Q_BLOCK + j >= CA_LEFT)


def _ca_fill_bias(i, base_ref, bias_scr):
    @pl.when(i == 0)
    def _():
        for h in range(ATT_HEADS):
            rows = jnp.broadcast_to(base_ref[h:h + 1, :], (CA_Q_BLOCK, CA_BASE))
            bias_scr[h] = pltpu.roll(rows, CA_BASE - CA_Q_BLOCK, 1, stride=1, stride_axis=0)[:, :CA_WINDOW]


def _ca_scores(q_ref, kp_ref, bias_scr, win, h, mask):
    hs = slice(h * HEAD_DIM, (h + 1) * HEAD_DIM)
    s = _dot(q_ref[:, hs], kp_ref[win, hs], 1, 1) * (HEAD_DIM ** -0.5)
    return jnp.where(mask, s + bias_scr[h], NEG)


CA_BIAS_SCRATCH = pltpu.VMEM((ATT_HEADS, CA_Q_BLOCK, CA_WINDOW), F32)


def _ca_fwd(pmm, kp, vp, base, ride=None):
    t = pmm.shape[0]

    def body(q_ref, kp_ref, vp_ref, base_ref, o_ref, lse_ref, bias_scr):
        i = pl.program_id(0)
        _ca_fill_bias(i, base_ref, bias_scr)
        win = pl.ds(pl.multiple_of(i * CA_Q_BLOCK, CA_Q_BLOCK), CA_WINDOW)
        mask = _ca_mask(i)
        lse_ref[...] = jnp.zeros_like(lse_ref)
        for h in range(ATT_HEADS):
            hs = slice(h * HEAD_DIM, (h + 1) * HEAD_DIM)
            s = _ca_scores(q_ref, kp_ref, bias_scr, win, h, mask)
            m = jnp.max(s, axis=-1, keepdims=True)
            p = jnp.exp(s - m)
            l = jnp.sum(p, axis=-1, keepdims=True)
            o_ref[:, hs] = (_dot(p, vp_ref[win, hs]) / l).astype(o_ref.dtype)
            lse_ref[:, h:h + 1] = m + jnp.log(l)

    padded = pl.BlockSpec((t + CA_LEFT, GROUP_WIDTH), lambda i: (0, 0))
    return _pcall(
        body, ride, grid=(t // CA_Q_BLOCK,),
        in_specs=[pl.BlockSpec((CA_Q_BLOCK, GROUP_WIDTH), lambda i: (i, 0)), padded, padded,
                  pl.BlockSpec((ATT_HEADS, CA_BASE), lambda i: (0, 0))],
        out_specs=[pl.BlockSpec((CA_Q_BLOCK, GROUP_WIDTH), lambda i: (i, 0)),
                   pl.BlockSpec((CA_Q_BLOCK, LANES), lambda i: (i, 0))],
        out_shape=[jax.ShapeDtypeStruct((t, GROUP_WIDTH), ACT_DTYPE), jax.ShapeDtypeStruct((t, LANES), F32)],
        scratch_shapes=[CA_BIAS_SCRATCH], semantics=("arbitrary",), name="ca_fwd")(pmm, kp, vp, base)


def _ca_bwd(pmm, kp, vp, base, lse, dmix, ride=None):
    t = pmm.shape[0]
    scale = HEAD_DIM ** -0.5

    def body(q_ref, kp_ref, vp_ref, base_ref, lse_ref, do_ref, dq_ref, dkp_ref, dvp_ref, dbase_ref, bias_scr):
        i = pl.program_id(0)
        _ca_fill_bias(i, base_ref, bias_scr)

        @pl.when(i == 0)
        def _():
            dkp_ref[...] = jnp.zeros_like(dkp_ref)
            dvp_ref[...] = jnp.zeros_like(dvp_ref)
            dbase_ref[...] = jnp.zeros_like(dbase_ref)

        win = pl.ds(pl.multiple_of(i * CA_Q_BLOCK, CA_Q_BLOCK), CA_WINDOW)
        mask = _ca_mask(i)
        flip = (_iota((CA_Q_BLOCK, CA_Q_BLOCK), 0) + _iota((CA_Q_BLOCK, CA_Q_BLOCK), 1) == CA_Q_BLOCK - 1).astype(F32)
        for h in range(ATT_HEADS):
            hs = slice(h * HEAD_DIM, (h + 1) * HEAD_DIM)
            s = _ca_scores(q_ref, kp_ref, bias_scr, win, h, mask)
            p = jnp.exp(s - lse_ref[:, h:h + 1])
            do = do_ref[:, hs]
            dp = _dot(do, vp_ref[win, hs], 1, 1)
            ds = p * (dp - jnp.sum(p * dp, axis=-1, keepdims=True))
            dq_ref[:, hs] = (_dot(ds, kp_ref[win, hs]) * scale).astype(dq_ref.dtype)
            dkp_ref[win, hs] += _dot(ds, q_ref[:, hs], 0, 0) * scale
            dvp_ref[win, hs] += _dot(p, do, 0, 0)
            rev = jnp.concatenate([_dot(flip, ds), jnp.zeros((CA_Q_BLOCK, CA_BASE - CA_WINDOW), F32)], axis=1)
            lined = pltpu.roll(rev, 1, 1, stride=1, stride_axis=0)
            dbase_ref[h:h + 1, :] += jnp.sum(lined, axis=0, keepdims=True)

    padded = pl.BlockSpec((t + CA_LEFT, GROUP_WIDTH), lambda i: (0, 0))
    return _pcall(
        body, ride, grid=(t // CA_Q_BLOCK,),
        in_specs=[pl.BlockSpec((CA_Q_BLOCK, GROUP_WIDTH), lambda i: (i, 0)), padded, padded,
                  pl.BlockSpec((ATT_HEADS, CA_BASE), lambda i: (0, 0)),
                  pl.BlockSpec((CA_Q_BLOCK, LANES), lambda i: (i, 0)),
                  pl.BlockSpec((CA_Q_BLOCK, GROUP_WIDTH), lambda i: (i, 0))],
        out_specs=[pl.BlockSpec((CA_Q_BLOCK, GROUP_WIDTH), lambda i: (i, 0)), padded, padded,
                   pl.BlockSpec((ATT_HEADS, CA_BASE), lambda i: (0, 0))],
        out_shape=[jax.ShapeDtypeStruct((t, GROUP_WIDTH), ACT_DTYPE),
                   jax.ShapeDtypeStruct((t + CA_LEFT, GROUP_WIDTH), F32),
                   jax.ShapeDtypeStruct((t + CA_LEFT, GROUP_WIDTH), F32),
                   jax.ShapeDtypeStruct((ATT_HEADS, CA_BASE), F32)],
        scratch_shapes=[CA_BIAS_SCRATCH], semantics=("arbitrary",), name="ca_bwd")(pmm, kp, vp, base, lse, dmix)


GELU_C = 0.7978845608028654
GELU_A = 0.044715


def _shift_down(v, k, fill, period=None):
    rows = _iota(v.shape, 0)
    rows = rows if period is None else rows & (period - 1)
    return jnp.where(rows >= k, pltpu.roll(v, k, 0), fill)


def _shift_up(v, k, fill, period=None):
    t = v.shape[0]
    rows = _iota(v.shape, 0)
    rows, length = (rows, t) if period is None else (rows & (period - 1), period)
    return jnp.where(rows < length - k, pltpu.roll(v, t - k, 0), fill)


LRU_SCAN_BLOCK = 256


def _linear_scan(a, b, reverse=False):
    shift = _shift_up if reverse else _shift_down
    k = 1
    while k < LRU_SCAN_BLOCK:
        b = a * shift(b, k, 0.0, LRU_SCAN_BLOCK) + b
        a = a * shift(a, k, 1.0, LRU_SCAN_BLOCK)
        k *= 2
    nb = a.shape[0] // LRU_SCAN_BLOCK
    carry = jnp.zeros((1, a.shape[1]), F32)
    out = [None] * nb
    for blk in (reversed(range(nb)) if reverse else range(nb)):
        rows = slice(blk * LRU_SCAN_BLOCK, (blk + 1) * LRU_SCAN_BLOCK)
        h = b[rows] + a[rows] * carry
        out[blk] = h
        carry = h[0:1] if reverse else h[LRU_SCAN_BLOCK - 1:LRU_SCAN_BLOCK]
    return jnp.concatenate(out, axis=0)


def _neg_expm1(y):
    series = -y * (1.0 + y * (0.5 + y * (1.0 / 6.0 + y * (1.0 / 24.0 + y * (1.0 / 120.0)))))
    return jnp.where(y > -0.1, series, 1.0 - jnp.exp(y))


def _lru_forward(x, g_in, cw, cb, wa, ba, wx, bx, lam):
    xs = [_shift_down(x, CONV_WIDTH - 1 - j, 0.0) for j in range(CONV_WIDTH - 1)] + [x]
    xc = cb + sum(cw[j:j + 1, :] * xs[j] for j in range(CONV_WIDTH))
    r = jax.nn.sigmoid(_dot(xc, wa) + ba)
    i = jax.nn.sigmoid(_dot(xc, wx) + bx)
    lsl = _log_sigmoid(lam)
    la = LRU_C * r * lsl
    a = jnp.exp(la)
    s = jnp.sqrt(_neg_expm1(2.0 * la))
    h = _linear_scan(a, s * (i * xc))
    u = GELU_C * (g_in + GELU_A * g_in * g_in * g_in)
    th = jnp.tanh(u)
    gelu = 0.5 * g_in * (1.0 + th)
    return xs, xc, r, i, lsl, a, s, h, th, gelu


def _lru_specs(t):
    col = lambda off: pl.BlockSpec((t, LANES), lambda j: (0, j + off))
    vec = pl.BlockSpec((1, LANES), lambda j: (0, j))
    mat = pl.BlockSpec((None, LANES, LANES), lambda j: (j, 0, 0))
    return [col(0), col(GROUP_WIDTH // LANES), pl.BlockSpec((CONV_WIDTH, LANES), lambda j: (0, j)),
            vec, mat, vec, mat, vec, vec]


def _lru_fwd(pel, conv_w, conv_b, wa, ba, wx, bx, lam, ride=None):
    t = pel.shape[0]

    def body(g_ref, x_ref, cw_ref, cb_ref, wa_ref, ba_ref, wx_ref, bx_ref, lam_ref, o_ref):
        res = _lru_forward(x_ref[...], g_ref[...], cw_ref[...], cb_ref[...], wa_ref[...], ba_ref[...],
                           wx_ref[...], bx_ref[...], lam_ref[...])
        o_ref[...] = (res[7] * res[9]).astype(o_ref.dtype)

    return _pcall(
        body, ride, grid=(GROUP_WIDTH // LANES,), in_specs=_lru_specs(t),
        out_specs=pl.BlockSpec((t, LANES), lambda j: (0, j)),
        out_shape=jax.ShapeDtypeStruct((t, GROUP_WIDTH), ACT_DTYPE),
        semantics=("parallel",), name="lru_fwd")(pel, pel, conv_w, conv_b, wa, ba, wx, bx, lam)


def _lru_bwd(pel, conv_w, conv_b, wa, ba, wx, bx, lam, dmix, ride=None):
    t = pel.shape[0]

    def body(g_ref, x_ref, cw_ref, cb_ref, wa_ref, ba_ref, wx_ref, bx_ref, lam_ref, do_ref,
             dg_ref, dx_ref, dcw_ref, dcb_ref, dwa_ref, dba_ref, dwx_ref, dbx_ref, dlam_ref):
        g_in, cw, lam = g_ref[...], cw_ref[...], lam_ref[...]
        xs, xc, r, i, lsl, a, s, h, th, gelu = _lru_forward(
            x_ref[...], g_in, cw, cb_ref[...], wa_ref[...], ba_ref[...], wx_ref[...], bx_ref[...], lam)
        dout = do_ref[...]
        dgelu = 0.5 * (1.0 + th) + 0.5 * g_in * (1.0 - th * th) * GELU_C * (1.0 + 3.0 * GELU_A * g_in * g_in)
        dg_ref[...] = (dout * h * dgelu).astype(dg_ref.dtype)
        gsum = _linear_scan(_shift_up(a, 1, 0.0), dout * gelu, reverse=True)
        da = gsum * _shift_down(h, 1, 0.0)
        di = gsum * s * xc
        dla = da * a - gsum * (i * xc) * (a * a / s)
        dlam_ref[...] = jnp.sum(dla * (LRU_C * r), axis=0, keepdims=True) * jax.nn.sigmoid(-lam)
        dpr = dla * (LRU_C * lsl) * r * (1.0 - r)
        dpi = di * i * (1.0 - i)
        dxc = gsum * s * i + _dot(dpr, wa_ref[...], 1, 1) + _dot(dpi, wx_ref[...], 1, 1)
        xct = xc.T
        dwa_ref[...] = _dot(xct, dpr)
        dwx_ref[...] = _dot(xct, dpi)
        dba_ref[...] = jnp.sum(dpr, axis=0, keepdims=True)
        dbx_ref[...] = jnp.sum(dpi, axis=0, keepdims=True)
        dcb_ref[...] = jnp.sum(dxc, axis=0, keepdims=True)
        for j in range(CONV_WIDTH):
            dcw_ref[j:j + 1, :] = jnp.sum(dxc * xs[j], axis=0, keepdims=True)
        dx = cw[CONV_WIDTH - 1:CONV_WIDTH, :] * dxc
        for j in range(CONV_WIDTH - 1):
            dx = dx + cw[j:j + 1, :] * _shift_up(dxc, CONV_WIDTH - 1 - j, 0.0)
        dx_ref[...] = dx.astype(dx_ref.dtype)

    col = pl.BlockSpec((t, LANES), lambda j: (0, j))
    vec = pl.BlockSpec((1, LANES), lambda j: (0, j))
    mat = pl.BlockSpec((None, LANES, LANES), lambda j: (j, 0, 0))
    nb = GROUP_WIDTH // LANES
    vshape = jax.ShapeDtypeStruct((1, GROUP_WIDTH), F32)
    mshape = jax.ShapeDtypeStruct((nb, LANES, LANES), F32)
    return _pcall(
        body, ride, grid=(nb,),
        in_specs=_lru_specs(t) + [pl.BlockSpec((t, LANES), lambda j: (0, j + nb))],
        out_specs=[col, col, pl.BlockSpec((CONV_WIDTH, LANES), lambda j: (0, j)), vec, mat, vec, mat, vec, vec],
        out_shape=[jax.ShapeDtypeStruct((t, GROUP_WIDTH), ACT_DTYPE), jax.ShapeDtypeStruct((t, GROUP_WIDTH), ACT_DTYPE),
                   jax.ShapeDtypeStruct((CONV_WIDTH, GROUP_WIDTH), F32), vshape, mshape, vshape, mshape, vshape, vshape],
        semantics=("parallel",), name="lru_bwd")(
            pel, pel, conv_w, conv_b, wa, ba, wx, bx, lam, dmix)


def _block_diag_pairs(w):
    z = jnp.zeros((LRU_BLOCK_DIM, LRU_BLOCK_DIM), w.dtype)
    return jnp.stack([jnp.block([[w[2 * j], z], [z, w[2 * j + 1]]]) for j in range(w.shape[0] // 2)])


def _block_diag_pairs_grad(dw):
    b = LRU_BLOCK_DIM
    return jnp.stack([dw[n // 2, (n % 2) * b:(n % 2 + 1) * b, (n % 2) * b:(n % 2 + 1) * b] for n in range(2 * dw.shape[0])])


def _row_tile(r):
    return ROW_TILE if r % ROW_TILE == 0 else r


def _pair_sum(g, got, place, name):
    _, r, c = g.shape
    tile = r

    def body(place_ref, a_ref, b_ref, o_ref):
        o_ref[...] = (a_ref[...].astype(F32) + b_ref[...].astype(F32)).astype(o_ref.dtype)

    blk = pl.BlockSpec((1, tile, c), lambda k, i, place_ref: (k, i, 0))
    return pl.pallas_call(
        body,
        grid_spec=pltpu.PrefetchScalarGridSpec(
            num_scalar_prefetch=1, grid=(N_CHIPS, r // tile),
            in_specs=[pl.BlockSpec((1, tile, c), lambda k, i, place_ref: (2 * k + place_ref[0], i, 0)), blk],
            out_specs=blk),
        out_shape=jax.ShapeDtypeStruct(got.shape, got.dtype),
        compiler_params=_params("parallel", "parallel"), name=name)(place, g, got)


def _adamw_update(g, w_ref, m_ref, v_ref, g_ref, d_ref, nm_ref, nv_ref):
    nm = ADAM_B1 * m_ref[...] + (1.0 - ADAM_B1) * g
    nv = ADAM_B2 * v_ref[...] + (1.0 - ADAM_B2) * jnp.square(g)
    m_hat = nm / (1.0 - ADAM_B1 ** ADAM_STEP)
    v_hat = nv / (1.0 - ADAM_B2 ** ADAM_STEP)
    g_ref[...] = g
    d_ref[...] = -ADAM_LR * (m_hat / (jnp.sqrt(v_hat) + ADAM_EPS) + ADAM_WD * w_ref[...])
    nm_ref[...] = nm
    nv_ref[...] = nv


def _adamw_sharded(parts, w, m, v, place, name):
    n_layers, r, c = w.shape
    tile = _row_tile(r)
    nb = r // tile

    def body(place_ref, *refs):
        layer = pl.program_id(0)
        g = None
        for l in range(n_layers):
            s_ref, r_ref = refs[2 * l], refs[2 * l + 1]
            g_l = s_ref[0].astype(F32) + r_ref[0].astype(F32) + r_ref[1].astype(F32) + r_ref[2].astype(F32)
            g = g_l if g is None else jnp.where(layer == l, g_l, g)
        _adamw_update(g, *refs[2 * n_layers:])

    def part_specs(l):
        rows = lambda q, i: jnp.where(q < l, 0, jnp.where(q > l, nb - 1, i))
        return [pl.BlockSpec((1, tile, c), lambda q, i, place_ref: (place_ref[1], rows(q, i), 0)),
                pl.BlockSpec((3, tile, c), lambda q, i, place_ref: (0, rows(q, i), 0))]

    in_specs, args = [], []
    for l, (s, recv) in enumerate(parts):
        in_specs += part_specs(l)
        args += [s, recv]
    blk = pl.BlockSpec((None, tile, c), lambda q, i, place_ref: (q, i, 0))
    out = jax.ShapeDtypeStruct((n_layers, r, c), F32)
    return pl.pallas_call(
        body,
        grid_spec=pltpu.PrefetchScalarGridSpec(
            num_scalar_prefetch=1, grid=(n_layers, nb), in_specs=in_specs + [blk, blk, blk],
            out_specs=[blk, blk, blk, blk]),
        out_shape=[out, out, out, out], compiler_params=_params("arbitrary", "arbitrary"), name=name)(
            place, *args, w, m, v)


def _adamw_small(repl_parts, vec_parts, w, m, v, place):
    n_r, n = len(repl_parts), len(w)
    shapes = [a.shape for a in w]

    def body(place_ref, *refs):
        parts, rest = refs[:n], refs[n:]
        for k in range(n):
            take = (lambda p: parts[k][p]) if k < n_r else (lambda p: parts[k][p, 0])
            g = take(0)
            for p in range(1, N_DEV):
                g = g + take(p)
            _adamw_update(g, rest[k], rest[n + k], rest[2 * n + k], *rest[3 * n + 4 * k:3 * n + 4 * k + 4])

    def whole(shape):
        return pl.BlockSpec(shape, lambda i, place_ref: (0,) * len(shape))

    def mine(shard):
        return pl.BlockSpec((N_DEV, 1) + shard, lambda i, place_ref: (0, place_ref[2]) + (0,) * len(shard))

    in_specs = [whole(a.shape) for a in repl_parts] + [mine(s) for s in shapes[n_r:]] + [whole(s) for s in shapes] * 3
    outs = pl.pallas_call(
        body,
        grid_spec=pltpu.PrefetchScalarGridSpec(
            num_scalar_prefetch=1, grid=(1,), in_specs=in_specs,
            out_specs=[whole(s) for s in shapes for _ in range(4)]),
        out_shape=[jax.ShapeDtypeStruct(s, F32) for s in shapes for _ in range(4)],
        compiler_params=_params("arbitrary"), name="adamw_small")(place, *repl_parts, *vec_parts, *w, *m, *v)
    return [outs[4 * k:4 * k + 4] for k in range(n)]


SHARDED = {"norm_w": 2, "w_in_even": 2, "gla_w_a_up": 2, "w_out_even": 1, "w_in_odd": 2, "conv_w": 2, "conv_b": 1,
           "lru_b_a": 1, "lru_b_x": 1, "lru_lambda": 1, "w_out_odd": 1, "w_mlp_up": 2, "w_mlp_down": 1}
REPLICATED = ["gla_b_a", "gla_norm_w", "fox_b_f", "rel_bias", "lru_w_a", "lru_w_x"]
WEIGHTS = ["norm_w", "w_in_even", "gla_w_a_up", "gla_b_a", "gla_norm_w", "fox_b_f", "w_out_even", "w_in_odd",
           "rel_bias", "conv_w", "conv_b", "lru_w_a", "lru_b_a", "lru_w_x", "lru_b_x", "lru_lambda", "w_out_odd",
           "w_mlp_up", "w_mlp_down"]
MATRICES = ("w_in_even", "w_out_even", "w_in_odd", "w_out_odd", "w_mlp_up", "w_mlp_down")
TRANSPOSED = ("w_in_even", "w_in_odd")
VECTORS = tuple(n for n in SHARDED if n not in MATRICES)
MATRIX_BLOCKS = (("w_in_even", 0), ("w_out_even", 0), ("w_in_odd", 0), ("w_out_odd", 0),
                 ("w_mlp_up", 0), ("w_mlp_up", 1), ("w_mlp_down", 0), ("w_mlp_down", 1))


def _join_shards(blocks, axis):
    moved = jnp.moveaxis(blocks, 0, axis)
    shape = moved.shape
    return moved.reshape(shape[:axis] + (shape[axis] * shape[axis + 1],) + shape[axis + 2:])


def _split_shards(full, axis):
    shape = full.shape
    cut = full.reshape(shape[:axis] + (N_DEV, shape[axis] // N_DEV) + shape[axis + 1:])
    return jnp.moveaxis(cut, axis, 0)


EVEN_SPLITS = (0, 256, 512, 1024, 1536, 1552, 2064, 2576, 3088, 3096)


def _even_in_split(wt):
    c = [wt[EVEN_SPLITS[k]:EVEN_SPLITS[k + 1]] for k in range(9)]
    gq, gk, gv, gr, ga, fq, fk, fv, ff = c
    padrows = lambda a: jnp.pad(a, ((0, LANES - a.shape[0]), (0, 0)))
    return jnp.concatenate([gq, gk, gv, fq, fk, fv], axis=0), jnp.concatenate([gr, padrows(ga), padrows(ff)], axis=0)


def _even_in_merge(dmm, dele):
    return jnp.concatenate([dmm[:1024], dele[:512], dele[512:512 + GLA_RANK], dmm[1024:2560],
                            dele[640:640 + ATT_HEADS]], axis=0)


def _forward_backward(x, target, shard, vec_shard, w, place):
    w = dict(w)
    g, dnorm, sums, recv = {}, {}, {}, {}
    nrm = lambda l, k: w["norm_w"][l, k][None, :]
    gather = lambda *keys: _gather_plan([shard[k] for k in keys])
    blocks = lambda r, c: (N_DEV, r // N_DEV, c)

    def pair_sum(key):
        sums[key] = _pair_sum(g[key], got[key], place, f"rs_pair_sum_{key[0]}_{key[1]}")

    got = {}

    def mlp_fwd(xin, layer, ride_up, ride_down):
        up = _mm(xin, w["w_mlp_up"][layer], out_dtype=ACT_DTYPE, tm=TM_FWD, tn=D_FF // N_DEV, b_blocked=True,
                 a_norm=nrm(layer, 2), name=f"mlp_up_{layer}", ride=ride_up)
        (u, h), rode_up = up if ride_up is not None else (up, None)
        down = _mm(u, w["w_mlp_down"][layer], out_dtype=F32, tm=TM_DX // 2, tn=D_MODEL, a_sqrelu=True,
                   res_norm=(xin, nrm(layer, 3)), name=f"mlp_down_{layer}", ride=ride_down)
        (yv, xout), rode_down = down if ride_down is not None else (down, None)
        return xout, (xin, h, u, yv), rode_up, rode_down

    def mlp_bwd(dxout, saved, layer, ride):
        xin, h, u, yv = saved
        k_up, k_down = ("w_mlp_up", layer), ("w_mlp_down", layer)
        res = _mm(dxout, w["w_mlp_down"][layer], nt=True, out_dtype=ACT_DTYPE, tm=TM_DX, tn=TN, drelu_of=u,
                  a_norm_bwd=(yv, nrm(layer, 3)), name=f"mlp_down_dx_{layer}", ride=ride)
        (du, dy, dnorm[(layer, 3)]), rode = res if ride is not None else (res, None)
        g[k_down] = _mm(u, dy, ta=True, out_dtype=WIRE_DTYPE, tm=TM_DW, tn=TN, a_sqrelu=True,
                        name=f"mlp_down_dw_{layer}").reshape(blocks(D_FF, D_MODEL))
        g[k_up] = _mm(h, du, ta=True, out_dtype=WIRE_DTYPE, tm=TM_DW, tn=D_FF // N_DEV, out_blocked=True,
                      name=f"mlp_up_dw_{layer}")
        w_up = jnp.moveaxis(w["w_mlp_up"][layer], 0, 1).reshape(D_MODEL, D_FF)
        (dxin, dnorm[(layer, 2)]), (got[k_down], got[k_up]) = _mm(
            du, w_up, nt=True, out_dtype=F32, tm=TM_DX // 2, tn=D_MODEL, norm_bwd=(xin, nrm(layer, 2), dxout),
            name=f"mlp_up_dx_{layer}", ride=_sibling_plan([g[k_down], g[k_up]]))
        pair_sum(k_down)
        pair_sum(k_up)
        return dxin, rode

    first = _run_plan(_gather_plan([shard[("w_in_even", 0)]] + [vec_shard[n] for n in VECTORS]),
                      "weights_all_gather_first")
    w["w_in_even"] = first[0].reshape(-1, D_MODEL)
    for n, b in zip(VECTORS, first[1:]):
        w[n] = _join_shards(b, SHARDED[n])
    w["w_mlp_up"], w["w_mlp_down"] = [None] * DEPTH, [None] * DEPTH

    wmm_e, wel_e = _even_in_split(w["w_in_even"])
    w_up_pad = jnp.pad(w["gla_w_a_up"][0], ((0, LANES - GLA_RANK), (0, 0)))
    b_f_pad = jnp.pad(w["fox_b_f"], ((0, 0), (0, LANES - ATT_HEADS)))
    (pmm0, h0), (w_out_even,) = _mm(x, wmm_e, nt=True, out_dtype=ACT_DTYPE, tm=TM_FWD, tn=TN, a_norm=nrm(0, 0),
                                    name="in_even_mm", ride=gather(("w_out_even", 0)))
    pel0 = _mm(h0, wel_e, nt=True, out_dtype=F32, tm=TM_FWD, tn=768, name="in_even_el")
    (out_a, states), (w_in_odd,) = _gla_fwd(pmm0, pel0, w_up_pad, w["gla_b_a"], w["gla_norm_w"],
                                            ride=gather(("w_in_odd", 0)))
    cum, cum_t = _fox_gate_fwd(pel0, b_f_pad)
    (out_b, lse_b), (w["w_mlp_up"][0], w_mlp_down0) = _fox_fwd(pmm0, cum, cum_t,
                                                               ride=gather(("w_mlp_up", 0), ("w_mlp_down", 0)))
    w["w_out_even"] = w_out_even.reshape(D_MODEL, D_MODEL)
    w["w_mlp_down"][0] = w_mlp_down0.reshape(D_FF, D_MODEL)
    mix_in0 = jnp.concatenate([out_a, out_b], axis=1)
    mix0, x1 = _mm(mix_in0, w["w_out_even"], out_dtype=F32, tm=TM_DX, tn=D_MODEL, res_norm=(x, nrm(0, 1)),
                   name="out_even")
    x2, mlp0, _, (w["w_mlp_up"][1],) = mlp_fwd(x1, 0, None, gather(("w_mlp_up", 1)))
    w["w_in_odd"] = w_in_odd.reshape(-1, D_MODEL)

    w_in_o = w["w_in_odd"]
    n_mm_o = 3 * GROUP_WIDTH
    wa_bd, wx_bd = _block_diag_pairs(w["lru_w_a"][0]), _block_diag_pairs(w["lru_w_x"][0])
    base = _ca_bias_base(w["rel_bias"][0])
    pmm1, h1 = _mm(x2, w_in_o[:n_mm_o], nt=True, out_dtype=ACT_DTYPE, tm=TM_FWD, tn=TN, a_norm=nrm(1, 0),
                   name="in_odd_mm")
    pel1 = _mm(h1, w_in_o[n_mm_o:], nt=True, out_dtype=F32, tm=TM_FWD, tn=TN, name="in_odd_el")
    kp = jnp.pad(pmm1[:, GROUP_WIDTH:2 * GROUP_WIDTH], ((CA_LEFT, 0), (0, 0)))
    vp = jnp.pad(pmm1[:, 2 * GROUP_WIDTH:], ((CA_LEFT, 0), (0, 0)))
    (out_c, lse_c), (w_mlp_down1,) = _ca_fwd(pmm1, kp, vp, base, ride=gather(("w_mlp_down", 1)))
    w["w_mlp_down"][1] = w_mlp_down1.reshape(D_FF, D_MODEL)
    lru_args = (pel1, w["conv_w"][0], w["conv_b"], wa_bd, w["lru_b_a"], wx_bd, w["lru_b_x"], w["lru_lambda"])
    out_d, (w_out_odd,) = _lru_fwd(*lru_args, ride=gather(("w_out_odd", 0)))
    w["w_out_odd"] = w_out_odd.reshape(D_MODEL, D_MODEL)
    mix_in1 = jnp.concatenate([out_c, out_d], axis=1)
    mix1, x3 = _mm(mix_in1, w["w_out_odd"], out_dtype=F32, tm=TM_DX, tn=D_MODEL, res_norm=(x2, nrm(1, 1)),
                   name="out_odd")
    x4, mlp1, _, _ = mlp_fwd(x3, 1, None, None)

    loss, dx4 = _loss_fwd_bwd(x4, target)

    k_oo, k_io, k_oe, k_ie = ("w_out_odd", 0), ("w_in_odd", 0), ("w_out_even", 0), ("w_in_even", 0)
    mlp_keys = lambda l: [("w_mlp_down", l), ("w_mlp_up", l)]
    dx3, _ = mlp_bwd(dx4, mlp1, 1, None)
    dmix_in1, dmix1, dnorm[(1, 1)] = _mm(dx3, w["w_out_odd"], nt=True, out_dtype=F32, tm=TM_DX, tn=TN,
                                         a_norm_bwd=(mix1, nrm(1, 1)), name="out_odd_dx")
    g[k_oo] = _mm(mix_in1, dmix1, ta=True, out_dtype=WIRE_DTYPE, tm=TM_DW, tn=TN, name="out_odd_dw").reshape(
        blocks(D_MODEL, D_MODEL))
    (dq_c, dkp, dvp, dbase), rode = _ca_bwd(
        pmm1, kp, vp, base, lse_c, dmix_in1,
        ride=_join_plans(_chip_plan([sums[k] for k in mlp_keys(1)]), _sibling_plan([g[k_oo]])))
    recv.update(zip(mlp_keys(1), rode[:2]))
    got[k_oo] = rode[2]
    pair_sum(k_oo)
    (dgate, dxin, g_conv_w, g_conv_b, dwa_bd, g_lru_b_a, dwx_bd, g_lru_b_x, g_lru_lambda), (recv[k_oo],) = _lru_bwd(
        *lru_args, dmix_in1, ride=_chip_plan([sums[k_oo]]))
    dp1 = jnp.concatenate([dq_c, dkp[CA_LEFT:].astype(ACT_DTYPE), dvp[CA_LEFT:].astype(ACT_DTYPE), dgate, dxin], axis=1)
    g[k_io] = _mm(dp1, h1, ta=True, out_dtype=WIRE_DTYPE, tm=dp1.shape[1] // 2, tn=TN, name="in_odd_dw").reshape(
        blocks(dp1.shape[1], D_MODEL))
    (dx2, dnorm[(1, 0)]), (got[k_io],) = _mm(dp1, w_in_o, out_dtype=F32, tm=TM_DX // 2, tn=D_MODEL,
                                             norm_bwd=(x2, nrm(1, 0), dx3), name="in_odd_dx",
                                             ride=_sibling_plan([g[k_io]]))
    pair_sum(k_io)
    g["rel_bias"] = _ca_bias_base_grad(dbase)[None]
    g["conv_w"], g["conv_b"] = g_conv_w[None], g_conv_b
    g["lru_w_a"], g["lru_w_x"] = _block_diag_pairs_grad(dwa_bd)[None], _block_diag_pairs_grad(dwx_bd)[None]
    g["lru_b_a"], g["lru_b_x"], g["lru_lambda"] = g_lru_b_a, g_lru_b_x, g_lru_lambda

    dx1, (recv[k_io],) = mlp_bwd(dx2, mlp0, 0, _chip_plan([sums[k_io]]))
    dmix_in0, dmix0, dnorm[(0, 1)] = _mm(dx1, w["w_out_even"], nt=True, out_dtype=F32, tm=TM_DX, tn=TN,
                                         a_norm_bwd=(mix0, nrm(0, 1)), name="out_even_dx")
    g[k_oe] = _mm(mix_in0, dmix0, ta=True, out_dtype=WIRE_DTYPE, tm=TM_DW, tn=TN, name="out_even_dw").reshape(
        blocks(D_MODEL, D_MODEL))
    k_md0, k_mu0 = mlp_keys(0)
    (dq_a, dk_a, dv_a, dr_a, da_a, dw_up_pad, g_gla_b_a, g_gla_norm_w), (got[k_oe], recv[k_md0]) = _gla_bwd(
        pmm0, pel0, w_up_pad, w["gla_b_a"], w["gla_norm_w"], states, dmix_in0,
        ride=_join_plans(_sibling_plan([g[k_oe]]), _chip_plan([sums[k_md0]])))
    pair_sum(k_oe)
    (dq_b, dk_b, dv_b, dcum_t, dcum_q), (recv[k_mu0], recv[k_oe]) = _fox_bwd(
        pmm0, cum, cum_t, lse_b, dmix_in0, ride=_chip_plan([sums[k_mu0], sums[k_oe]]))
    df_b, db_f = _fox_gate_bwd(pel0, b_f_pad, dcum_t, dcum_q)
    g["gla_w_a_up"] = dw_up_pad[:GLA_RANK][None]
    g["gla_b_a"], g["gla_norm_w"], g["fox_b_f"] = g_gla_b_a, g_gla_norm_w, db_f[:, :ATT_HEADS]
    dp0 = jnp.concatenate([dq_a, dk_a, dv_a, dq_b, dk_b.astype(ACT_DTYPE), dv_b.astype(ACT_DTYPE), dr_a, da_a, df_b],
                          axis=1)
    w_perm = jnp.concatenate([wmm_e, wel_e], axis=0)
    n_mm_e = wmm_e.shape[0]
    dw_perm, repl_parts = _mm(dp0, h0, ta=True, out_dtype=WIRE_DTYPE, tm=dp0.shape[1] // 2, tn=TN, name="in_even_dw",
                              ride=_gather_plan([g[n] for n in REPLICATED]))
    dw_even = _even_in_merge(dw_perm[:n_mm_e], dw_perm[n_mm_e:])
    g[k_ie] = dw_even.reshape(blocks(dw_even.shape[0], D_MODEL))
    dh0, (got[k_ie],) = _mm(dp0, w_perm, out_dtype=F32, tm=TM_DX, tn=TN, name="in_even_dx",
                            ride=_sibling_plan([g[k_ie]]))
    pair_sum(k_ie)
    (dx0, dnorm[(0, 0)]), (recv[k_ie],) = _norm_bwd(dh0, x, nrm(0, 0), out_dtype=F32, add=dx1, name="norm_in_bwd_0",
                                                     ride=_chip_plan([sums[k_ie]]))

    g["norm_w"] = jnp.stack([jnp.concatenate([dnorm[(l, k)] for k in range(4)], axis=0) for l in range(DEPTH)])
    vec_parts = _run_plan(_gather_plan([_split_shards(g[n], SHARDED[n]) for n in VECTORS]), "vector_grads_all_gather")
    return loss, dx0, sums, recv, repl_parts, vec_parts


def kernel(x, norm_w, w_in_even, gla_w_a_up, gla_b_a, gla_norm_w, fox_b_f, w_out_even, w_in_odd, rel_bias, conv_w, conv_b, lru_w_a, lru_b_a, lru_w_x, lru_b_x, lru_lambda, w_out_odd, w_mlp_up, w_mlp_down, loss_target, m_norm_w, m_w_in_even, m_gla_w_a_up, m_gla_b_a, m_gla_norm_w, m_fox_b_f, m_w_out_even, m_w_in_odd, m_rel_bias, m_conv_w, m_conv_b, m_lru_w_a, m_lru_b_a, m_lru_w_x, m_lru_b_x, m_lru_lambda, m_w_out_odd, m_w_mlp_up, m_w_mlp_down, v_norm_w, v_w_in_even, v_gla_w_a_up, v_gla_b_a, v_gla_norm_w, v_fox_b_f, v_w_out_even, v_w_in_odd, v_rel_bias, v_conv_w, v_conv_b, v_lru_w_a, v_lru_b_a, v_lru_w_x, v_lru_b_x, v_lru_lambda, v_w_out_odd, v_w_mlp_up, v_w_mlp_down):
    wts = dict(zip(WEIGHTS, (norm_w, w_in_even, gla_w_a_up, gla_b_a, gla_norm_w, fox_b_f, w_out_even, w_in_odd, rel_bias,
                             conv_w, conv_b, lru_w_a, lru_b_a, lru_w_x, lru_b_x, lru_lambda, w_out_odd, w_mlp_up,
                             w_mlp_down)))
    mom = dict(zip(WEIGHTS, (m_norm_w, m_w_in_even, m_gla_w_a_up, m_gla_b_a, m_gla_norm_w, m_fox_b_f, m_w_out_even,
                             m_w_in_odd, m_rel_bias, m_conv_w, m_conv_b, m_lru_w_a, m_lru_b_a, m_lru_w_x, m_lru_b_x,
                             m_lru_lambda, m_w_out_odd, m_w_mlp_up, m_w_mlp_down)))
    var = dict(zip(WEIGHTS, (v_norm_w, v_w_in_even, v_gla_w_a_up, v_gla_b_a, v_gla_norm_w, v_fox_b_f, v_w_out_even,
                             v_w_in_odd, v_rel_bias, v_conv_w, v_conv_b, v_lru_w_a, v_lru_b_a, v_lru_w_x, v_lru_b_x,
                             v_lru_lambda, v_w_out_odd, v_w_mlp_up, v_w_mlp_down)))
    ax, ay, ac = lax.axis_index("x"), lax.axis_index("y"), lax.axis_index("c")
    place = jnp.stack([ac, 2 * ax + ay, 4 * ax + 2 * ay + ac]).astype(jnp.int32)

    shard = {(n, l): (wts[n][l].T if n in TRANSPOSED else wts[n][l]).astype(WIRE_DTYPE) for n, l in MATRIX_BLOCKS}
    loss_blk, dx, sums, recv, repl_parts, vec_parts = _forward_backward(
        x[0], loss_target[0], shard, {n: wts[n] for n in VECTORS}, {n: wts[n] for n in REPLICATED}, place)
    loss = lax.psum(loss_blk[0, 0], ("x", "y", "c"))

    view = lambda n, a: jnp.swapaxes(a, 1, 2) if n in TRANSPOSED else a
    upd = {n: [view(n, o) for o in _adamw_sharded(
        [(sums[(n, l)], recv[(n, l)]) for l in range(wts[n].shape[0])], view(n, wts[n]), view(n, mom[n]), view(n, var[n]),
        place, f"adamw_{n}")] for n in MATRICES}
    small = REPLICATED + list(VECTORS)
    upd.update(zip(small, _adamw_small(repl_parts, vec_parts, [wts[n] for n in small], [mom[n] for n in small],
                                       [var[n] for n in small], place)))
    return (loss, dx[None], *[upd[n][kind] for kind in range(4) for n in WEIGHTS])
```

```python
import functools
from typing import Callable, NamedTuple, Optional

import jax
import jax.numpy as jnp
from jax import lax
from jax.experimental import pallas as pl
from jax.experimental.pallas import tpu as pltpu

F32 = jnp.float32
MXU_DTYPE = jnp.bfloat16
ACT_DTYPE = jnp.bfloat16
WIRE_DTYPE = jnp.bfloat16

V7X_VMEM_BYTES = 64 * 1024 * 1024
VMEM_LIMIT = (V7X_VMEM_BYTES * 7) // 8
LANES = 128

D_MODEL = 1024
SEQ = 2048
DEPTH = 2
CHUNK = 64
GROUP_WIDTH = D_MODEL // 2
D_FF = 4 * D_MODEL
NORM_EPS = 1e-6
GLA_HEADS = 4
GLA_DV = GROUP_WIDTH // GLA_HEADS
GLA_DK = GLA_DV // 2
GLA_KW = GLA_HEADS * GLA_DK
GLA_RANK = 16
GLA_GATE_TAU = 16.0
HEAD_DIM = 64
ATT_HEADS = GROUP_WIDTH // HEAD_DIM
CA_LEFT = 8 * CHUNK
REL_CLIP = 128
LRU_BLOCK_DIM = 64
CONV_WIDTH = 4
LRU_C = 8.0
N_DEV = 8

ADAM_LR = 0.001
ADAM_B1 = 0.9
ADAM_B2 = 0.999
ADAM_EPS = 1e-08
ADAM_WD = 0.01
ADAM_STEP = 10

NEG = float(jnp.finfo(jnp.float32).min)
MESH = pl.DeviceIdType.MESH


def _params(*sem):
    return pltpu.CompilerParams(dimension_semantics=sem, vmem_limit_bytes=VMEM_LIMIT)


def _dot(a, b, ca=1, cb=0):
    return lax.dot_general(a.astype(MXU_DTYPE), b.astype(MXU_DTYPE), (((ca,), (cb,)), ((), ())),
                           preferred_element_type=F32)


def _dot_exact(a, b):
    return lax.dot_general(a, b, (((1,), (0,)), ((), ())), precision=lax.Precision.HIGHEST,
                           preferred_element_type=F32)


def _log_sigmoid(x):
    return jnp.minimum(x, 0.0) - jnp.log1p(jnp.exp(-jnp.abs(x)))


def _iota(shape, axis):
    return lax.broadcasted_iota(jnp.int32, shape, axis)


ANY = pl.BlockSpec(memory_space=pl.ANY)
N_CHIPS = 4


class _Plan(NamedTuple):
    ins: list
    outs: list
    sems: list
    start: Callable
    finish: Callable
    relay: Optional[Callable] = None


def _place():
    x, y, c = lax.axis_index("x"), lax.axis_index("y"), lax.axis_index("c")
    return x, y, c, [(1 - x, y), (x, 1 - y), (1 - x, 1 - y)]


def _gather_plan(xs):
    n = len(xs)

    def parts(x_refs, out_refs, sems):
        send_sems, recv_sems, local_sems = sems
        x, y, c, chips = _place()
        me, sibling = (x, y, c), (x, y, 1 - c)

        def rows(a, px, py, pc):
            return out_refs[a].at[4 * px + 2 * py + pc]

        def copy(a, k, block, to, src=None):
            return pltpu.make_async_remote_copy(
                src_ref=rows(a, *block) if src is None else src, dst_ref=rows(a, *block),
                send_sem=send_sems.at[7 * a + k], recv_sem=recv_sems.at[7 * a + k], device_id=to, device_id_type=MESH)

        def own():
            mine = [pltpu.make_async_copy(x_refs[a], rows(a, *me), local_sems.at[a]) for a in range(n)]
            first = []
            for a in range(n):
                first.append(copy(a, 0, me, sibling, src=x_refs[a]))
                first += [copy(a, 1 + j, me, (*chip, c), src=x_refs[a]) for j, chip in enumerate(chips)]
            return mine, first

        return c, me, sibling, chips, copy, own

    def start(x_refs, out_refs, sems):
        mine, first = parts(x_refs, out_refs, sems)[-1]()
        for cp in first + mine:
            cp.start()

    def relay(x_refs, out_refs, sems):
        c, me, sibling, chips, copy, _ = parts(x_refs, out_refs, sems)
        for j, chip in enumerate(chips):
            for a in range(n):
                copy(a, 1 + j, (*chip, c), me).wait_recv()
                copy(a, 4 + j, (*chip, c), sibling).start()

    def finish(x_refs, out_refs, sems):
        c, me, sibling, chips, copy, own = parts(x_refs, out_refs, sems)
        mine, first = own()
        for a in range(n):
            copy(a, 0, sibling, me).wait_recv()
            for j, chip in enumerate(chips):
                copy(a, 4 + j, (*chip, 1 - c), me).wait_recv()
        for cp in first + [copy(a, 4 + j, (*chip, c), sibling) for j, chip in enumerate(chips) for a in range(n)]:
            cp.wait_send()
        for cp in mine:
            cp.wait()

    return _Plan(list(xs), [jax.ShapeDtypeStruct((N_DEV,) + x.shape, x.dtype) for x in xs],
                 [pltpu.SemaphoreType.DMA((7 * n,)), pltpu.SemaphoreType.DMA((7 * n,)), pltpu.SemaphoreType.DMA((n,))],
                 start, finish, relay)


def _exchange_plan(copies_of, ins, outs, per_array):
    n = len(ins)

    def start(in_refs, out_refs, sems):
        for cp in copies_of(in_refs, out_refs, sems):
            cp.start()

    def finish(in_refs, out_refs, sems):
        copies = copies_of(in_refs, out_refs, sems)
        for cp in copies:
            cp.wait_recv()
        for cp in copies:
            cp.wait_send()

    return _Plan(list(ins), outs, [pltpu.SemaphoreType.DMA((per_array * n,)), pltpu.SemaphoreType.DMA((per_array * n,))],
                 start, finish)


def _sibling_plan(gs):
    def copies_of(g_refs, got_refs, sems):
        x, y, c, _ = _place()
        return [pltpu.make_async_remote_copy(
            src_ref=g_refs[a].at[2 * k + (1 - c)], dst_ref=got_refs[a].at[k], send_sem=sems[0].at[N_CHIPS * a + k],
            recv_sem=sems[1].at[N_CHIPS * a + k], device_id=(x, y, 1 - c), device_id_type=MESH)
            for a in range(len(gs)) for k in range(N_CHIPS)]

    return _exchange_plan(copies_of, gs, [jax.ShapeDtypeStruct((N_CHIPS,) + g.shape[1:], g.dtype) for g in gs], N_CHIPS)


def _chip_plan(ss, relations=(0, 1, 2)):
    n_rel = len(relations)

    def copies_of(s_refs, out_refs, sems):
        x, y, c, chips = _place()
        return [pltpu.make_async_remote_copy(
            src_ref=s_refs[a].at[2 * chips[j][0] + chips[j][1]], dst_ref=out_refs[a].at[slot],
            send_sem=sems[0].at[n_rel * a + slot], recv_sem=sems[1].at[n_rel * a + slot],
            device_id=(*chips[j], c), device_id_type=MESH)
            for a in range(len(ss)) for slot, j in enumerate(relations)]

    return _exchange_plan(copies_of, ss, [jax.ShapeDtypeStruct((n_rel,) + s.shape[1:], s.dtype) for s in ss], n_rel)


def _join_plans(*plans):
    def cut(refs, counts):
        at = 0
        for n in counts:
            yield refs[at:at + n]
            at += n

    def each(in_refs, out_refs, sems):
        return zip(plans, cut(in_refs, [len(p.ins) for p in plans]), cut(out_refs, [len(p.outs) for p in plans]),
                   cut(sems, [len(p.sems) for p in plans]))

    def start(*refs):
        for p, i, o, s in each(*refs):
            p.start(i, o, s)

    def relay(*refs):
        for p, i, o, s in each(*refs):
            if p.relay is not None:
                p.relay(i, o, s)

    def finish(*refs):
        for p, i, o, s in each(*refs):
            p.finish(i, o, s)

    return _Plan([a for p in plans for a in p.ins], [a for p in plans for a in p.outs],
                 [a for p in plans for a in p.sems], start, finish, relay)


def _run_plan(plan, name):
    n_in, n_out = len(plan.ins), len(plan.outs)

    def body(*refs):
        args = refs[:n_in], refs[n_in:n_in + n_out], refs[n_in + n_out:]
        plan.start(*args)
        if plan.relay is not None:
            plan.relay(*args)
        plan.finish(*args)

    return pl.pallas_call(body, out_shape=plan.outs, in_specs=[ANY] * n_in, out_specs=[ANY] * n_out,
                          scratch_shapes=plan.sems, name=name)(*plan.ins)


def _pcall(body, ride, *, grid, in_specs, out_specs, out_shape, scratch_shapes=(), semantics, name, prefetch=False):
    n_pre = int(prefetch)

    def build(kernel, ins, outs, shapes, scratch, sem):
        if prefetch:
            return pl.pallas_call(
                kernel, grid_spec=pltpu.PrefetchScalarGridSpec(num_scalar_prefetch=1, grid=grid, in_specs=ins,
                                                               out_specs=outs, scratch_shapes=scratch),
                out_shape=shapes, compiler_params=_params(*sem), name=name)
        return pl.pallas_call(kernel, grid=grid, in_specs=ins, out_specs=outs, out_shape=shapes,
                              scratch_shapes=scratch, compiler_params=_params(*sem), name=name)

    if ride is None:
        return build(body, in_specs, out_specs, out_shape, list(scratch_shapes), semantics)
    single = not isinstance(out_shape, (list, tuple))
    out_specs_l, out_shape_l = ([out_specs], [out_shape]) if single else (list(out_specs), list(out_shape))
    n_in, n_out, n_scr = len(in_specs), len(out_shape_l), len(scratch_shapes)
    r_in, r_out = len(ride.ins), len(ride.outs)

    def riding(*refs):
        pre, refs = refs[:n_pre], refs[n_pre:]
        cuts = [n_in, r_in, n_out, r_out, n_scr]
        groups, at = [], 0
        for width in cuts:
            groups.append(refs[at:at + width])
            at += width
        ins, r_ins, outs, r_outs, scr = groups
        sems = refs[at:]
        first = functools.reduce(jnp.logical_and, [pl.program_id(d) == 0 for d in range(len(grid))])
        last = functools.reduce(jnp.logical_and, [pl.program_id(d) == grid[d] - 1 for d in range(len(grid))])

        @pl.when(first)
        def _():
            ride.start(r_ins, r_outs, sems)

        several_steps = any(n > 1 for n in grid)
        if ride.relay is not None and several_steps:
            @pl.when(last)
            def _():
                ride.relay(r_ins, r_outs, sems)

        body(*pre, *ins, *outs, *scr)

        @pl.when(last)
        def _():
            if ride.relay is not None and not several_steps:
                ride.relay(r_ins, r_outs, sems)
            ride.finish(r_ins, r_outs, sems)

    call = build(riding, list(in_specs) + [ANY] * r_in, out_specs_l + [ANY] * r_out, out_shape_l + list(ride.outs),
                 list(scratch_shapes) + list(ride.sems), ["arbitrary"] * len(grid))

    def run(*args):
        res = call(*args, *ride.ins)
        return (res[0] if single else list(res[:n_out])), list(res[n_out:])

    return run


def _rms(x):
    return x * lax.rsqrt(jnp.mean(x * x, axis=-1, keepdims=True) + NORM_EPS)


def _mm(a, b, *, nt=False, ta=False, out_dtype, tm, tn, a_sqrelu=False, drelu_of=None, b_blocked=False,
        out_blocked=False, a_norm=None, a_norm_bwd=None, res_norm=None, norm_bwd=None, name, ride=None):
    k, m = a.shape if ta else a.shape[::-1]
    if b_blocked:
        assert not nt and b.shape[1] == k and b.shape[2] == tn
        n = b.shape[0] * tn
    else:
        n = b.shape[0] if nt else b.shape[1]
        assert (b.shape[1] if nt else b.shape[0]) == k
    tm, tn = min(tm, m), min(tn, n)
    assert m % tm == 0 and n % tn == 0
    assert (res_norm is None and norm_bwd is None) or tn == n
    assert a_norm is None or a_norm_bwd is None
    n_in = (2 + (drelu_of is not None) + (a_norm is not None) + 2 * (a_norm_bwd is not None)
            + 2 * (res_norm is not None) + 3 * (norm_bwd is not None))

    def body(*refs):
        a_ref, b_ref = refs[0], refs[1]
        extra = list(refs[2:n_in])
        outs = list(refs[n_in:])
        o_ref = outs.pop(0)
        u_ref = extra.pop(0) if drelu_of is not None else None
        if a_norm is not None:
            wn_ref, h_ref, h_scr = extra.pop(0), outs.pop(0), outs.pop()

            @pl.when(pl.program_id(1) == 0)
            def _():
                h = (_rms(a_ref[...]) * wn_ref[...]).astype(ACT_DTYPE)
                h_scr[...] = h
                h_ref[...] = h

            av = h_scr[...]
        elif a_norm_bwd is not None:
            y_ref, wy_ref = extra.pop(0), extra.pop(0)
            dy_ref, dwy_ref, dy_scr = outs.pop(0), outs.pop(0), outs.pop()
            first_rows = pl.program_id(0) == 0

            @pl.when(pl.program_id(1) == 0)
            def _():
                yv, up = y_ref[...], a_ref[...]
                rstd = lax.rsqrt(jnp.mean(yv * yv, axis=-1, keepdims=True) + NORM_EPS)
                yhat = yv * rstd
                g = up * wy_ref[...]
                dy = (rstd * (g - yhat * jnp.mean(g * yhat, axis=-1, keepdims=True))).astype(ACT_DTYPE)
                dy_scr[...] = dy
                dy_ref[...] = dy

                @pl.when(first_rows)
                def _():
                    dwy_ref[...] = jnp.zeros_like(dwy_ref)

                dwy_ref[...] += jnp.sum(up * yhat, axis=0, keepdims=True)

            av = dy_scr[...]
        else:
            av = a_ref[...]
        if a_sqrelu:
            av = jnp.square(jnp.maximum(av.astype(F32), 0.0))
        acc = _dot(av, b_ref[...], 0 if ta else 1, 1 if nt else 0)
        if u_ref is not None:
            acc = acc * (2.0 * jnp.maximum(u_ref[...].astype(F32), 0.0))
        if norm_bwd is not None:
            x_ref, wb_ref, add_ref = extra
            dw_ref = outs[0]
            xv = x_ref[...]
            rstd = lax.rsqrt(jnp.mean(xv * xv, axis=-1, keepdims=True) + NORM_EPS)
            xhat = xv * rstd
            g = acc * wb_ref[...]
            o_ref[...] = rstd * (g - xhat * jnp.mean(g * xhat, axis=-1, keepdims=True)) + add_ref[...]

            @pl.when(pl.program_id(0) == 0)
            def _():
                dw_ref[...] = jnp.zeros_like(dw_ref)

            dw_ref[...] += jnp.sum(acc * xhat, axis=0, keepdims=True)
            return
        o_ref[...] = acc.astype(out_dtype)
        if res_norm is not None:
            res_ref, wr_ref = extra
            outs[0][...] = res_ref[...] + _rms(acc) * wr_ref[...]

    if b_blocked:
        b_spec = pl.BlockSpec((None, k, tn), lambda i, j: (j, 0, 0))
    elif nt:
        b_spec = pl.BlockSpec((tn, k), lambda i, j: (j, 0))
    else:
        b_spec = pl.BlockSpec((k, tn), lambda i, j: (0, j))
    a_spec = pl.BlockSpec((k, tm), lambda i, j: (0, i)) if ta else pl.BlockSpec((tm, k), lambda i, j: (i, 0))
    in_specs = [a_spec, b_spec]
    args = [a, b]
    if drelu_of is not None:
        in_specs.append(pl.BlockSpec((tm, tn), lambda i, j: (i, j)))
        args.append(drelu_of)
    if out_blocked:
        out_specs = [pl.BlockSpec((None, tm, tn), lambda i, j: (j, i, 0))]
        out_shape = [jax.ShapeDtypeStruct((n // tn, m, tn), out_dtype)]
    else:
        out_specs = [pl.BlockSpec((tm, tn), lambda i, j: (i, j))]
        out_shape = [jax.ShapeDtypeStruct((m, n), out_dtype)]
    scratch = []
    if a_norm is not None:
        assert not ta
        in_specs.append(pl.BlockSpec((1, k), lambda i, j: (0, 0)))
        args.append(a_norm)
        out_specs.append(pl.BlockSpec((tm, k), lambda i, j: (i, 0)))
        out_shape.append(jax.ShapeDtypeStruct((m, k), ACT_DTYPE))
        scratch.append(pltpu.VMEM((tm, k), ACT_DTYPE))
    if a_norm_bwd is not None:
        assert not ta
        in_specs += [pl.BlockSpec((tm, k), lambda i, j: (i, 0)), pl.BlockSpec((1, k), lambda i, j: (0, 0))]
        args += list(a_norm_bwd)
        out_specs += [pl.BlockSpec((tm, k), lambda i, j: (i, 0)), pl.BlockSpec((1, k), lambda i, j: (0, 0))]
        out_shape += [jax.ShapeDtypeStruct((m, k), ACT_DTYPE), jax.ShapeDtypeStruct((1, k), F32)]
        scratch.append(pltpu.VMEM((tm, k), ACT_DTYPE))
    if res_norm is not None:
        in_specs += [pl.BlockSpec((tm, n), lambda i, j: (i, 0)), pl.BlockSpec((1, n), lambda i, j: (0, 0))]
        args += list(res_norm)
        out_specs.append(pl.BlockSpec((tm, n), lambda i, j: (i, 0)))
        out_shape.append(jax.ShapeDtypeStruct((m, n), F32))
    if norm_bwd is not None:
        rows = pl.BlockSpec((tm, n), lambda i, j: (i, 0))
        in_specs += [rows, pl.BlockSpec((1, n), lambda i, j: (0, 0)), rows]
        args += list(norm_bwd)
        out_specs.append(pl.BlockSpec((1, n), lambda i, j: (0, 0)))
        out_shape.append(jax.ShapeDtypeStruct((1, n), F32))
    single = len(out_shape) == 1
    return _pcall(body, ride, grid=(m // tm, n // tn), in_specs=in_specs,
                  out_specs=out_specs[0] if single else out_specs, out_shape=out_shape[0] if single else out_shape,
                  scratch_shapes=scratch, semantics=("arbitrary", "arbitrary"), name=name)(*args)


ROW_TILE = 512
TM_FWD, TM_DX, TM_DW, TN = 2048, 1024, 1024, 512


def _norm_bwd(dy, x, w, *, out_dtype, add=None, name, ride=None):
    t, d = x.shape

    def body(*refs):
        dy_ref, x_ref, w_ref = refs[0], refs[1], refs[2]
        dx_ref, dw_ref = refs[-2], refs[-1]
        xv = x_ref[...]
        rstd = lax.rsqrt(jnp.mean(xv * xv, axis=-1, keepdims=True) + NORM_EPS)
        xhat = xv * rstd
        dyv = dy_ref[...].astype(F32)
        g = dyv * w_ref[...]
        dx = rstd * (g - xhat * jnp.mean(g * xhat, axis=-1, keepdims=True))
        if add is not None:
            dx = dx + refs[3][...]
        dx_ref[...] = dx.astype(out_dtype)

        @pl.when(pl.program_id(0) == 0)
        def _():
            dw_ref[...] = jnp.zeros_like(dw_ref)

        dw_ref[...] += jnp.sum(dyv * xhat, axis=0, keepdims=True)

    row = pl.BlockSpec((ROW_TILE, d), lambda i: (i, 0))
    vec = pl.BlockSpec((1, d), lambda i: (0, 0))
    in_specs = [row, row, vec] + ([row] if add is not None else [])
    args = [dy, x, w] + ([add] if add is not None else [])
    return _pcall(body, ride, grid=(t // ROW_TILE,), in_specs=in_specs, out_specs=[row, vec],
                  out_shape=[jax.ShapeDtypeStruct((t, d), out_dtype), jax.ShapeDtypeStruct((1, d), F32)],
                  semantics=("arbitrary",), name=name)(*args)


def _loss_fwd_bwd(y, target):
    t, d = y.shape

    def body(y_ref, t_ref, l_ref, dy_ref):
        diff = y_ref[...] - t_ref[...]
        dy_ref[...] = diff * (1.0 / d)

        @pl.when(pl.program_id(0) == 0)
        def _():
            l_ref[...] = jnp.zeros_like(l_ref)

        l_ref[...] += 0.5 * jnp.sum(jnp.mean(diff * diff, axis=-1, keepdims=True), axis=0, keepdims=True)

    row = pl.BlockSpec((ROW_TILE, d), lambda i: (i, 0))
    return pl.pallas_call(body, grid=(t // ROW_TILE,), in_specs=[row, row],
                          out_specs=[pl.BlockSpec((8, LANES), lambda i: (0, 0)), row],
                          out_shape=[jax.ShapeDtypeStruct((8, LANES), F32), jax.ShapeDtypeStruct((t, d), F32)],
                          compiler_params=_params("arbitrary"), name="loss")(y, target)


GLA_STATE = (GLA_HEADS * GLA_DV, GLA_KW)


def _gla_specs(chunk_of):
    rows = lambda width, col: pl.BlockSpec((CHUNK, width), lambda i: (chunk_of(i), col))
    const = lambda r, c: pl.BlockSpec((r, c), lambda i: (0, 0))
    return [rows(GLA_KW, 0),
            rows(GLA_KW, 1),
            rows(GROUP_WIDTH, 1),
            rows(GROUP_WIDTH, 0),
            rows(LANES, 4),
            const(LANES, GLA_KW),
            const(1, GLA_KW),
            const(1, GROUP_WIDTH)]


def _gla_chunk(q_ref, k_ref, v_ref, a_ref, wup_ref, ba_ref):
    z = _dot(a_ref[...], wup_ref[...]) + ba_ref[...]
    tri = (_iota((CHUNK, CHUNK), 1) <= _iota((CHUNK, CHUNK), 0)).astype(F32)
    cum = _dot_exact(tri, _log_sigmoid(z) * (1.0 / GLA_GATE_TAU))
    tot = cum[CHUNK - 1:CHUNK, :]
    e = jnp.exp(tot - cum)
    return (z, e, jnp.exp(tot), k_ref[...].astype(F32) * e, q_ref[...].astype(F32) * (GLA_DK ** -0.5),
            v_ref[...].astype(F32))


def _gla_head_mask():
    return _iota(GLA_STATE, 0) // GLA_DV == _iota(GLA_STATE, 1) // GLA_DK


def _gla_fwd(pmm, pel, w_up, b_a, gnorm_w, ride=None):
    t = pmm.shape[0]
    nc = t // CHUNK

    def body(q_ref, k_ref, v_ref, r_ref, a_ref, wup_ref, ba_ref, gw_ref, o_ref, st_ref, m_scr):
        @pl.when(pl.program_id(0) == 0)
        def _():
            m_scr[...] = jnp.zeros_like(m_scr)

        _, _, decay, kd, qs, vv = _gla_chunk(q_ref, k_ref, v_ref, a_ref, wup_ref, ba_ref)
        m = m_scr[...] * decay + jnp.where(_gla_head_mask(), _dot(vv, kd, 0, 0), 0.0)
        m_scr[...] = m
        st_ref[...] = m
        o = _dot(qs, m, 1, 1)
        rr = r_ref[...]
        gate = rr * jax.nn.sigmoid(rr) * gw_ref[...]
        for h in range(GLA_HEADS):
            vs = slice(h * GLA_DV, (h + 1) * GLA_DV)
            oh = o[:, vs]
            y = oh * lax.rsqrt(jnp.mean(oh * oh, axis=-1, keepdims=True) + NORM_EPS)
            o_ref[:, vs] = (y * gate[:, vs]).astype(o_ref.dtype)

    return _pcall(
        body, ride, grid=(nc,), in_specs=_gla_specs(lambda i: i),
        out_specs=[pl.BlockSpec((CHUNK, GROUP_WIDTH), lambda i: (i, 0)),
                   pl.BlockSpec((None,) + GLA_STATE, lambda i: (i, 0, 0))],
        out_shape=[jax.ShapeDtypeStruct((t, GROUP_WIDTH), ACT_DTYPE), jax.ShapeDtypeStruct((nc,) + GLA_STATE, F32)],
        scratch_shapes=[pltpu.VMEM(GLA_STATE, F32)],
        semantics=("arbitrary",), name="gla_fwd")(pmm, pmm, pmm, pel, pel, w_up, b_a, gnorm_w)


def _gla_bwd(pmm, pel, w_up, b_a, gnorm_w, states, dmix, ride=None):
    t = pmm.shape[0]
    nc = t // CHUNK
    scale = GLA_DK ** -0.5

    def body(q_ref, k_ref, v_ref, r_ref, a_ref, wup_ref, ba_ref, gw_ref, st_ref, prev_ref, do_ref,
             dq_ref, dk_ref, dv_ref, dr_ref, da_ref, dwup_ref, dba_ref, dgw_ref, dm_scr):
        step = pl.program_id(0)

        @pl.when(step == 0)
        def _():
            dm_scr[...] = jnp.zeros_like(dm_scr)
            dwup_ref[...] = jnp.zeros_like(dwup_ref)
            dba_ref[...] = jnp.zeros_like(dba_ref)
            dgw_ref[...] = jnp.zeros_like(dgw_ref)

        z, e, decay, kd, qs, vv = _gla_chunk(q_ref, k_ref, v_ref, a_ref, wup_ref, ba_ref)
        m = st_ref[...]
        m_prev = prev_ref[...] * (step < nc - 1).astype(F32)
        rr, dout, gw = r_ref[...], do_ref[...], gw_ref[...]
        sig = jax.nn.sigmoid(rr)
        silu = rr * sig
        dsilu = sig * (1.0 + rr * (1.0 - sig))
        o = _dot(qs, m, 1, 1)
        d_o, dgw = [], []
        for h in range(GLA_HEADS):
            vs = slice(h * GLA_DV, (h + 1) * GLA_DV)
            oh, dg = o[:, vs], dout[:, vs]
            rstd = lax.rsqrt(jnp.mean(oh * oh, axis=-1, keepdims=True) + NORM_EPS)
            y = oh * rstd
            dgw.append(jnp.sum(dg * y * silu[:, vs], axis=0, keepdims=True))
            dr_ref[:, vs] = (dg * y * gw[:, vs] * dsilu[:, vs]).astype(dr_ref.dtype)
            dy = dg * gw[:, vs] * silu[:, vs]
            d_o.append(rstd * (dy - y * jnp.mean(dy * y, axis=-1, keepdims=True)))
        d_o = jnp.concatenate(d_o, axis=1)
        dgw_ref[...] += jnp.concatenate(dgw, axis=1)
        dq_ref[...] = (_dot(d_o, m) * scale).astype(dq_ref.dtype)
        dm = dm_scr[...] + jnp.where(_gla_head_mask(), _dot(d_o, qs, 0, 0), 0.0)
        dv_ref[...] = _dot(kd, dm, 1, 1).astype(dv_ref.dtype)
        dkd = _dot(vv, dm)
        dk_ref[...] = (dkd * e).astype(dk_ref.dtype)
        dm_scr[...] = dm * decay
        tri_strict = (_iota((CHUNK, CHUNK), 1) < _iota((CHUNK, CHUNK), 0)).astype(F32)
        dla = jnp.sum(dm * m_prev, axis=0, keepdims=True) * decay + _dot_exact(tri_strict, dkd * kd)
        dz = dla * jax.nn.sigmoid(-z) * (1.0 / GLA_GATE_TAU)
        da_ref[...] = _dot(dz, wup_ref[...], 1, 1).astype(da_ref.dtype)
        dwup_ref[...] += _dot(a_ref[...], dz, 0, 0)
        dba_ref[...] += jnp.sum(dz, axis=0, keepdims=True)

    chunk_of = lambda i: nc - 1 - i
    in_specs = _gla_specs(chunk_of) + [
        pl.BlockSpec((None,) + GLA_STATE, lambda i: (chunk_of(i), 0, 0)),
        pl.BlockSpec((None,) + GLA_STATE, lambda i: (jnp.maximum(chunk_of(i) - 1, 0), 0, 0)),
        pl.BlockSpec((CHUNK, GROUP_WIDTH), lambda i: (chunk_of(i), 0))]
    rows = lambda width: pl.BlockSpec((CHUNK, width), lambda i: (chunk_of(i), 0))
    const = lambda r, c: pl.BlockSpec((r, c), lambda i: (0, 0))
    return _pcall(
        body, ride, grid=(nc,), in_specs=in_specs,
        out_specs=[rows(GLA_KW), rows(GLA_KW), rows(GROUP_WIDTH), rows(GROUP_WIDTH), rows(LANES),
                   const(LANES, GLA_KW), const(1, GLA_KW), const(1, GROUP_WIDTH)],
        out_shape=[jax.ShapeDtypeStruct((t, GLA_KW), ACT_DTYPE), jax.ShapeDtypeStruct((t, GLA_KW), ACT_DTYPE),
                   jax.ShapeDtypeStruct((t, GROUP_WIDTH), ACT_DTYPE), jax.ShapeDtypeStruct((t, GROUP_WIDTH), ACT_DTYPE),
                   jax.ShapeDtypeStruct((t, LANES), ACT_DTYPE), jax.ShapeDtypeStruct((LANES, GLA_KW), F32),
                   jax.ShapeDtypeStruct((1, GLA_KW), F32), jax.ShapeDtypeStruct((1, GROUP_WIDTH), F32)],
        scratch_shapes=[pltpu.VMEM(GLA_STATE, F32)],
        semantics=("arbitrary",), name="gla_bwd")(
            pmm, pmm, pmm, pel, pel, w_up, b_a, gnorm_w, states, states, dmix)


CUM_BLOCK = 256


def _fox_gate_fwd(pel, b_f):
    t = pel.shape[0]
    nb = t // CUM_BLOCK

    def body(f_ref, b_ref, cum_ref, cum_t_ref):
        tri = (_iota((CUM_BLOCK, CUM_BLOCK), 1) <= _iota((CUM_BLOCK, CUM_BLOCK), 0)).astype(F32)
        carry = jnp.zeros((1, LANES), F32)
        for blk in range(nb):
            rows = slice(blk * CUM_BLOCK, (blk + 1) * CUM_BLOCK)
            cum = _dot_exact(tri, _log_sigmoid(f_ref[rows, :] + b_ref[...])) + carry
            cum_ref[rows, :] = cum
            cum_t_ref[blk] = cum.T[:ATT_HEADS, :]
            carry = cum[CUM_BLOCK - 1:CUM_BLOCK, :]

    return pl.pallas_call(
        body, grid=(1,),
        in_specs=[pl.BlockSpec((t, LANES), lambda i: (0, 5)), pl.BlockSpec((1, LANES), lambda i: (0, 0))],
        out_specs=[pl.BlockSpec((t, LANES), lambda i: (0, 0)),
                   pl.BlockSpec((nb, ATT_HEADS, CUM_BLOCK), lambda i: (0, 0, 0))],
        out_shape=[jax.ShapeDtypeStruct((t, LANES), F32), jax.ShapeDtypeStruct((nb, ATT_HEADS, CUM_BLOCK), F32)],
        compiler_params=_params("arbitrary"), name="fox_gate_fwd")(pel, b_f)


def _fox_gate_bwd(pel, b_f, dcum_t, dcum_q):
    t = pel.shape[0]
    nb = t // CUM_BLOCK

    def body(f_ref, b_ref, dct_ref, dcq_ref, df_ref, db_ref):
        tri_up = (_iota((CUM_BLOCK, CUM_BLOCK), 1) >= _iota((CUM_BLOCK, CUM_BLOCK), 0)).astype(F32)
        carry = jnp.zeros((1, LANES), F32)
        db = jnp.zeros((1, LANES), F32)
        for blk in reversed(range(nb)):
            rows = slice(blk * CUM_BLOCK, (blk + 1) * CUM_BLOCK)
            query_side = sum(dcq_ref[pair, rows, :] for pair in range(dcq_ref.shape[0]))
            dls = _dot_exact(tri_up, dct_ref[blk].T + query_side) + carry
            carry = dls[0:1, :]
            df = dls * jax.nn.sigmoid(-(f_ref[rows, :] + b_ref[...]))
            df_ref[rows, :] = df.astype(df_ref.dtype)
            db = db + jnp.sum(df, axis=0, keepdims=True)
        db_ref[...] = db

    return pl.pallas_call(
        body, grid=(1,),
        in_specs=[pl.BlockSpec((t, LANES), lambda i: (0, 5)), pl.BlockSpec((1, LANES), lambda i: (0, 0)),
                  pl.BlockSpec((nb, LANES, CUM_BLOCK), lambda i: (0, 0, 0)),
                  pl.BlockSpec((dcum_q.shape[0], t, LANES), lambda i: (0, 0, 0))],
        out_specs=[pl.BlockSpec((t, LANES), lambda i: (0, 0)), pl.BlockSpec((1, LANES), lambda i: (0, 0))],
        out_shape=[jax.ShapeDtypeStruct((t, LANES), ACT_DTYPE), jax.ShapeDtypeStruct((1, LANES), F32)],
        compiler_params=_params("arbitrary"), name="fox_gate_bwd")(pel, b_f, dcum_t, dcum_q)


FOX_Q_BLOCK = 256


assert FOX_Q_BLOCK == CUM_BLOCK
FOX_KEY_STEP = 512


def _fox_scores(q_ref, k_ref, cum_ref, cum_t_ref, h, i):
    hs = slice(h * HEAD_DIM, (h + 1) * HEAD_DIM)
    nb = cum_t_ref.shape[0]
    key_gate = jnp.concatenate([cum_t_ref[kb, h:h + 1, :] for kb in range(nb)], axis=1)
    s = _dot(q_ref[:, hs], k_ref[:, hs], 1, 1) * (HEAD_DIM ** -0.5) + (cum_ref[:, h:h + 1] - key_gate)
    shape = (FOX_Q_BLOCK, nb * FOX_Q_BLOCK)
    return jnp.where(_iota(shape, 1) <= i * FOX_Q_BLOCK + _iota(shape, 0), s, NEG)


def _fox_specs(t):
    bq, nb = FOX_Q_BLOCK, t // FOX_Q_BLOCK
    return [pl.BlockSpec((bq, GROUP_WIDTH), lambda i: (i, 2)), pl.BlockSpec((t, GROUP_WIDTH), lambda i: (0, 3)),
            pl.BlockSpec((t, GROUP_WIDTH), lambda i: (0, 4)), pl.BlockSpec((bq, LANES), lambda i: (i, 0)),
            pl.BlockSpec((nb, ATT_HEADS, bq), lambda i: (0, 0, 0))]


def _fox_fwd(pmm, cum, cum_t, ride=None):
    t = pmm.shape[0]
    bq = FOX_Q_BLOCK

    def body(q_ref, k_ref, v_ref, cum_ref, cum_t_ref, o_ref, lse_ref):
        i = pl.program_id(0)
        lse_ref[...] = jnp.zeros_like(lse_ref)
        for h in range(ATT_HEADS):
            hs = slice(h * HEAD_DIM, (h + 1) * HEAD_DIM)
            s = _fox_scores(q_ref, k_ref, cum_ref, cum_t_ref, h, i)
            m = jnp.max(s, axis=-1, keepdims=True)
            p = jnp.exp(s - m)
            l = jnp.sum(p, axis=-1, keepdims=True)
            o_ref[:, hs] = (_dot(p, v_ref[:, hs]) / l).astype(o_ref.dtype)
            lse_ref[:, h:h + 1] = m + jnp.log(l)

    return _pcall(
        body, ride, grid=(t // bq,), in_specs=_fox_specs(t),
        out_specs=[pl.BlockSpec((bq, GROUP_WIDTH), lambda i: (i, 0)), pl.BlockSpec((bq, LANES), lambda i: (i, 0))],
        out_shape=[jax.ShapeDtypeStruct((t, GROUP_WIDTH), ACT_DTYPE), jax.ShapeDtypeStruct((t, LANES), F32)],
        semantics=("parallel",), name="fox_fwd")(pmm, pmm, pmm, cum, cum_t)


def _fox_bwd(pmm, cum, cum_t, lse, dmix, ride=None):
    t = pmm.shape[0]
    bq, nb = FOX_Q_BLOCK, t // FOX_Q_BLOCK
    pairs, per_pair = ATT_HEADS // 2, LANES // HEAD_DIM
    scale = HEAD_DIM ** -0.5

    def body(q_ref, k_ref, v_ref, cum_ref, cum_t_ref, lse_ref, do_ref, dq_ref, dk_ref, dv_ref, dct_ref, dcq_ref):
        g, i = pl.program_id(0), pl.program_id(1)

        @pl.when(i == 0)
        def _():
            dk_ref[...] = jnp.zeros_like(dk_ref)
            dv_ref[...] = jnp.zeros_like(dv_ref)

        @pl.when((i == 0) & (g == 0))
        def _():
            dct_ref[...] = jnp.zeros_like(dct_ref)

        lane = _iota((1, LANES), 1)

        def run(n):
            causal = _iota((bq, n), 1) <= i * bq + _iota((bq, n), 0)
            dcq = jnp.zeros((bq, LANES), F32)
            for hh in range(per_pair):
                h = per_pair * g + hh
                hs = slice(hh * HEAD_DIM, (hh + 1) * HEAD_DIM)
                pick = (lane == h).astype(F32)
                cq = jnp.sum(cum_ref[...] * pick, axis=1, keepdims=True)
                lse_h = jnp.sum(lse_ref[...] * pick, axis=1, keepdims=True)
                key_gate = jnp.concatenate([cum_t_ref[kb, pl.ds(h, 1), :] for kb in range(n // bq)], axis=1)
                s = _dot(q_ref[:, hs], k_ref[:n, hs], 1, 1) * scale + (cq - key_gate)
                p = jnp.exp(jnp.where(causal, s, NEG) - lse_h)
                do = do_ref[:, hs]
                dp = _dot(do, v_ref[:n, hs], 1, 1)
                ds = p * (dp - jnp.sum(p * dp, axis=-1, keepdims=True))
                dq_ref[:, hs] = (_dot(ds, k_ref[:n, hs]) * scale).astype(dq_ref.dtype)
                dk_ref[:n, hs] += _dot(ds, q_ref[:, hs], 0, 0) * scale
                dv_ref[:n, hs] += _dot(p, do, 0, 0)
                key_side = -jnp.sum(ds, axis=0, keepdims=True)
                for kb in range(n // bq):
                    dct_ref[kb, pl.ds(h, 1), :] += key_side[:, kb * bq:(kb + 1) * bq]
                dcq = dcq + jnp.sum(ds, axis=1, keepdims=True) * pick
            dcq_ref[...] = dcq

        for kx in range(t // FOX_KEY_STEP):
            pl.when(i // (FOX_KEY_STEP // bq) == kx)(functools.partial(run, (kx + 1) * FOX_KEY_STEP))

    cols = lambda first: pl.BlockSpec((bq, LANES), lambda g, i: (i, first + g))
    keys = lambda first: pl.BlockSpec((t, LANES), lambda g, i: (0, first + g))
    per_head = pl.BlockSpec((bq, LANES), lambda g, i: (i, 0))
    fox_q, fox_k, fox_v = (GROUP_WIDTH * n // LANES for n in (2, 3, 4))
    return _pcall(
        body, ride, grid=(pairs, t // bq),
        in_specs=[cols(fox_q), keys(fox_k), keys(fox_v), per_head,
                  pl.BlockSpec((nb, ATT_HEADS, bq), lambda g, i: (0, 0, 0)), per_head, cols(GROUP_WIDTH // LANES)],
        out_specs=[cols(0), keys(0), keys(0), pl.BlockSpec((nb, LANES, bq), lambda g, i: (0, 0, 0)),
                   pl.BlockSpec((None, bq, LANES), lambda g, i: (g, i, 0))],
        out_shape=[jax.ShapeDtypeStruct((t, GROUP_WIDTH), ACT_DTYPE), jax.ShapeDtypeStruct((t, GROUP_WIDTH), F32),
                   jax.ShapeDtypeStruct((t, GROUP_WIDTH), F32), jax.ShapeDtypeStruct((nb, LANES, bq), F32),
                   jax.ShapeDtypeStruct((pairs, t, LANES), F32)],
        semantics=("arbitrary", "arbitrary"), name="fox_bwd")(pmm, pmm, pmm, cum, cum_t, lse, dmix)


CA_Q_BLOCK = 4 * CHUNK
CA_WINDOW = CA_Q_BLOCK + CA_LEFT
CA_BASE = 1024


def _ca_bias_base(rel_bias):
    n = rel_bias.shape[0]
    flat = CA_Q_BLOCK + CA_LEFT - REL_CLIP
    tail = CA_BASE - flat - (2 * REL_CLIP + 1)
    return jnp.concatenate([jnp.broadcast_to(rel_bias[:, 2 * REL_CLIP:], (n, flat)), rel_bias[:, ::-1],
                            jnp.broadcast_to(rel_bias[:, :1], (n, tail))], axis=1)


def _ca_bias_base_grad(dbase):
    flat = CA_Q_BLOCK + CA_LEFT - REL_CLIP
    mid = dbase[:, flat:flat + 2 * REL_CLIP + 1][:, ::-1]
    lo = jnp.sum(dbase[:, flat + 2 * REL_CLIP + 1:], axis=1, keepdims=True)
    hi = jnp.sum(dbase[:, :flat], axis=1, keepdims=True)
    pad = jnp.zeros((dbase.shape[0], 2 * REL_CLIP - 1), F32)
    return mid + jnp.concatenate([lo, pad, hi], axis=1)


def _ca_mask(i):
    r, j = _iota((CA_Q_BLOCK, CA_WINDOW), 0), _iota((CA_Q_BLOCK, CA_WINDOW), 1)
    rc, jc = r // CHUNK, j // CHUNK
    return (jc >= rc) & (jc <= rc + CA_LEFT // CHUNK) & (i * CA_Q_BLOCK + j >= CA_LEFT)


def _ca_fill_bias(i, base_ref, bias_scr):
    @pl.when(i == 0)
    def _():
        for h in range(ATT_HEADS):
            rows = jnp.broadcast_to(base_ref[h:h + 1, :], (CA_Q_BLOCK, CA_BASE))
            bias_scr[h] = pltpu.roll(rows, CA_BASE - CA_Q_BLOCK, 1, stride=1, stride_axis=0)[:, :CA_WINDOW]


def _ca_scores(q_ref, kp_ref, bias_scr, win, h, mask):
    hs = slice(h * HEAD_DIM, (h + 1) * HEAD_DIM)
    s = _dot(q_ref[:, hs], kp_ref[win, hs], 1, 1) * (HEAD_DIM ** -0.5)
    return jnp.where(mask, s + bias_scr[h], NEG)


CA_BIAS_SCRATCH = pltpu.VMEM((ATT_HEADS, CA_Q_BLOCK, CA_WINDOW), F32)


def _ca_fwd(pmm, kp, vp, base, ride=None):
    t = pmm.shape[0]

    def body(q_ref, kp_ref, vp_ref, base_ref, o_ref, lse_ref, bias_scr):
        i = pl.program_id(0)
        _ca_fill_bias(i, base_ref, bias_scr)
        win = pl.ds(pl.multiple_of(i * CA_Q_BLOCK, CA_Q_BLOCK), CA_WINDOW)
        mask = _ca_mask(i)
        lse_ref[...] = jnp.zeros_like(lse_ref)
        for h in range(ATT_HEADS):
            hs = slice(h * HEAD_DIM, (h + 1) * HEAD_DIM)
            s = _ca_scores(q_ref, kp_ref, bias_scr, win, h, mask)
            m = jnp.max(s, axis=-1, keepdims=True)
            p = jnp.exp(s - m)
            l = jnp.sum(p, axis=-1, keepdims=True)
            o_ref[:, hs] = (_dot(p, vp_ref[win, hs]) / l).astype(o_ref.dtype)
            lse_ref[:, h:h + 1] = m + jnp.log(l)

    padded = pl.BlockSpec((t + CA_LEFT, GROUP_WIDTH), lambda i: (0, 0))
    return _pcall(
        body, ride, grid=(t // CA_Q_BLOCK,),
        in_specs=[pl.BlockSpec((CA_Q_BLOCK, GROUP_WIDTH), lambda i: (i, 0)), padded, padded,
                  pl.BlockSpec((ATT_HEADS, CA_BASE), lambda i: (0, 0))],
        out_specs=[pl.BlockSpec((CA_Q_BLOCK, GROUP_WIDTH), lambda i: (i, 0)),
                   pl.BlockSpec((CA_Q_BLOCK, LANES), lambda i: (i, 0))],
        out_shape=[jax.ShapeDtypeStruct((t, GROUP_WIDTH), ACT_DTYPE), jax.ShapeDtypeStruct((t, LANES), F32)],
        scratch_shapes=[CA_BIAS_SCRATCH], semantics=("arbitrary",), name="ca_fwd")(pmm, kp, vp, base)


def _ca_bwd(pmm, kp, vp, base, lse, dmix, ride=None):
    t = pmm.shape[0]
    scale = HEAD_DIM ** -0.5

    def body(q_ref, kp_ref, vp_ref, base_ref, lse_ref, do_ref, dq_ref, dkp_ref, dvp_ref, dbase_ref, bias_scr):
        i = pl.program_id(0)
        _ca_fill_bias(i, base_ref, bias_scr)

        @pl.when(i == 0)
        def _():
            dkp_ref[...] = jnp.zeros_like(dkp_ref)
            dvp_ref[...] = jnp.zeros_like(dvp_ref)
            dbase_ref[...] = jnp.zeros_like(dbase_ref)

        win = pl.ds(pl.multiple_of(i * CA_Q_BLOCK, CA_Q_BLOCK), CA_WINDOW)
        mask = _ca_mask(i)
        flip = (_iota((CA_Q_BLOCK, CA_Q_BLOCK), 0) + _iota((CA_Q_BLOCK, CA_Q_BLOCK), 1) == CA_Q_BLOCK - 1).astype(F32)
        for h in range(ATT_HEADS):
            hs = slice(h * HEAD_DIM, (h + 1) * HEAD_DIM)
            s = _ca_scores(q_ref, kp_ref, bias_scr, win, h, mask)
            p = jnp.exp(s - lse_ref[:, h:h + 1])
            do = do_ref[:, hs]
            dp = _dot(do, vp_ref[win, hs], 1, 1)
            ds = p * (dp - jnp.sum(p * dp, axis=-1, keepdims=True))
            dq_ref[:, hs] = (_dot(ds, kp_ref[win, hs]) * scale).astype(dq_ref.dtype)
            dkp_ref[win, hs] += _dot(ds, q_ref[:, hs], 0, 0) * scale
            dvp_ref[win, hs] += _dot(p, do, 0, 0)
            rev = jnp.concatenate([_dot(flip, ds), jnp.zeros((CA_Q_BLOCK, CA_BASE - CA_WINDOW), F32)], axis=1)
            lined = pltpu.roll(rev, 1, 1, stride=1, stride_axis=0)
            dbase_ref[h:h + 1, :] += jnp.sum(lined, axis=0, keepdims=True)

    padded = pl.BlockSpec((t + CA_LEFT, GROUP_WIDTH), lambda i: (0, 0))
    return _pcall(
        body, ride, grid=(t // CA_Q_BLOCK,),
        in_specs=[pl.BlockSpec((CA_Q_BLOCK, GROUP_WIDTH), lambda i: (i, 0)), padded, padded,
                  pl.BlockSpec((ATT_HEADS, CA_BASE), lambda i: (0, 0)),
                  pl.BlockSpec((CA_Q_BLOCK, LANES), lambda i: (i, 0)),
                  pl.BlockSpec((CA_Q_BLOCK, GROUP_WIDTH), lambda i: (i, 0))],
        out_specs=[pl.BlockSpec((CA_Q_BLOCK, GROUP_WIDTH), lambda i: (i, 0)), padded, padded,
                   pl.BlockSpec((ATT_HEADS, CA_BASE), lambda i: (0, 0))],
        out_shape=[jax.ShapeDtypeStruct((t, GROUP_WIDTH), ACT_DTYPE),
                   jax.ShapeDtypeStruct((t + CA_LEFT, GROUP_WIDTH), F32),
                   jax.ShapeDtypeStruct((t + CA_LEFT, GROUP_WIDTH), F32),
                   jax.ShapeDtypeStruct((ATT_HEADS, CA_BASE), F32)],
        scratch_shapes=[CA_BIAS_SCRATCH], semantics=("arbitrary",), name="ca_bwd")(pmm, kp, vp, base, lse, dmix)


GELU_C = 0.7978845608028654
GELU_A = 0.044715


def _shift_down(v, k, fill, period=None):
    rows = _iota(v.shape, 0)
    rows = rows if period is None else rows & (period - 1)
    return jnp.where(rows >= k, pltpu.roll(v, k, 0), fill)


def _shift_up(v, k, fill, period=None):
    t = v.shape[0]
    rows = _iota(v.shape, 0)
    rows, length = (rows, t) if period is None else (rows & (period - 1), period)
    return jnp.where(rows < length - k, pltpu.roll(v, t - k, 0), fill)


LRU_SCAN_BLOCK = 256


def _linear_scan(a, b, reverse=False):
    shift = _shift_up if reverse else _shift_down
    k = 1
    while k < LRU_SCAN_BLOCK:
        b = a * shift(b, k, 0.0, LRU_SCAN_BLOCK) + b
        a = a * shift(a, k, 1.0, LRU_SCAN_BLOCK)
        k *= 2
    nb = a.shape[0] // LRU_SCAN_BLOCK
    carry = jnp.zeros((1, a.shape[1]), F32)
    out = [None] * nb
    for blk in (reversed(range(nb)) if reverse else range(nb)):
        rows = slice(blk * LRU_SCAN_BLOCK, (blk + 1) * LRU_SCAN_BLOCK)
        h = b[rows] + a[rows] * carry
        out[blk] = h
        carry = h[0:1] if reverse else h[LRU_SCAN_BLOCK - 1:LRU_SCAN_BLOCK]
    return jnp.concatenate(out, axis=0)


def _neg_expm1(y):
    series = -y * (1.0 + y * (0.5 + y * (1.0 / 6.0 + y * (1.0 / 24.0 + y * (1.0 / 120.0)))))
    return jnp.where(y > -0.1, series, 1.0 - jnp.exp(y))


def _lru_forward(x, g_in, cw, cb, wa, ba, wx, bx, lam):
    xs = [_shift_down(x, CONV_WIDTH - 1 - j, 0.0) for j in range(CONV_WIDTH - 1)] + [x]
    xc = cb + sum(cw[j:j + 1, :] * xs[j] for j in range(CONV_WIDTH))
    r = jax.nn.sigmoid(_dot(xc, wa) + ba)
    i = jax.nn.sigmoid(_dot(xc, wx) + bx)
    lsl = _log_sigmoid(lam)
    la = LRU_C * r * lsl
    a = jnp.exp(la)
    s = jnp.sqrt(_neg_expm1(2.0 * la))
    h = _linear_scan(a, s * (i * xc))
    u = GELU_C * (g_in + GELU_A * g_in * g_in * g_in)
    th = jnp.tanh(u)
    gelu = 0.5 * g_in * (1.0 + th)
    return xs, xc, r, i, lsl, a, s, h, th, gelu


def _lru_specs(t):
    col = lambda off: pl.BlockSpec((t, LANES), lambda j: (0, j + off))
    vec = pl.BlockSpec((1, LANES), lambda j: (0, j))
    mat = pl.BlockSpec((None, LANES, LANES), lambda j: (j, 0, 0))
    return [col(0), col(GROUP_WIDTH // LANES), pl.BlockSpec((CONV_WIDTH, LANES), lambda j: (0, j)),
            vec, mat, vec, mat, vec, vec]


def _lru_fwd(pel, conv_w, conv_b, wa, ba, wx, bx, lam, ride=None):
    t = pel.shape[0]

    def body(g_ref, x_ref, cw_ref, cb_ref, wa_ref, ba_ref, wx_ref, bx_ref, lam_ref, o_ref):
        res = _lru_forward(x_ref[...], g_ref[...], cw_ref[...], cb_ref[...], wa_ref[...], ba_ref[...],
                           wx_ref[...], bx_ref[...], lam_ref[...])
        o_ref[...] = (res[7] * res[9]).astype(o_ref.dtype)

    return _pcall(
        body, ride, grid=(GROUP_WIDTH // LANES,), in_specs=_lru_specs(t),
        out_specs=pl.BlockSpec((t, LANES), lambda j: (0, j)),
        out_shape=jax.ShapeDtypeStruct((t, GROUP_WIDTH), ACT_DTYPE),
        semantics=("parallel",), name="lru_fwd")(pel, pel, conv_w, conv_b, wa, ba, wx, bx, lam)


def _lru_bwd(pel, conv_w, conv_b, wa, ba, wx, bx, lam, dmix, ride=None):
    t = pel.shape[0]

    def body(g_ref, x_ref, cw_ref, cb_ref, wa_ref, ba_ref, wx_ref, bx_ref, lam_ref, do_ref,
             dg_ref, dx_ref, dcw_ref, dcb_ref, dwa_ref, dba_ref, dwx_ref, dbx_ref, dlam_ref):
        g_in, cw, lam = g_ref[...], cw_ref[...], lam_ref[...]
        xs, xc, r, i, lsl, a, s, h, th, gelu = _lru_forward(
            x_ref[...], g_in, cw, cb_ref[...], wa_ref[...], ba_ref[...], wx_ref[...], bx_ref[...], lam)
        dout = do_ref[...]
        dgelu = 0.5 * (1.0 + th) + 0.5 * g_in * (1.0 - th * th) * GELU_C * (1.0 + 3.0 * GELU_A * g_in * g_in)
        dg_ref[...] = (dout * h * dgelu).astype(dg_ref.dtype)
        gsum = _linear_scan(_shift_up(a, 1, 0.0), dout * gelu, reverse=True)
        da = gsum * _shift_down(h, 1, 0.0)
        di = gsum * s * xc
        dla = da * a - gsum * (i * xc) * (a * a / s)
        dlam_ref[...] = jnp.sum(dla * (LRU_C * r), axis=0, keepdims=True) * jax.nn.sigmoid(-lam)
        dpr = dla * (LRU_C * lsl) * r * (1.0 - r)
        dpi = di * i * (1.0 - i)
        dxc = gsum * s * i + _dot(dpr, wa_ref[...], 1, 1) + _dot(dpi, wx_ref[...], 1, 1)
        xct = xc.T
        dwa_ref[...] = _dot(xct, dpr)
        dwx_ref[...] = _dot(xct, dpi)
        dba_ref[...] = jnp.sum(dpr, axis=0, keepdims=True)
        dbx_ref[...] = jnp.sum(dpi, axis=0, keepdims=True)
        dcb_ref[...] = jnp.sum(dxc, axis=0, keepdims=True)
        for j in range(CONV_WIDTH):
            dcw_ref[j:j + 1, :] = jnp.sum(dxc * xs[j], axis=0, keepdims=True)
        dx = cw[CONV_WIDTH - 1:CONV_WIDTH, :] * dxc
        for j in range(CONV_WIDTH - 1):
            dx = dx + cw[j:j + 1, :] * _shift_up(dxc, CONV_WIDTH - 1 - j, 0.0)
        dx_ref[...] = dx.astype(dx_ref.dtype)

    col = pl.BlockSpec((t, LANES), lambda j: (0, j))
    vec = pl.BlockSpec((1, LANES), lambda j: (0, j))
    mat = pl.BlockSpec((None, LANES, LANES), lambda j: (j, 0, 0))
    nb = GROUP_WIDTH // LANES
    vshape = jax.ShapeDtypeStruct((1, GROUP_WIDTH), F32)
    mshape = jax.ShapeDtypeStruct((nb, LANES, LANES), F32)
    return _pcall(
        body, ride, grid=(nb,),
        in_specs=_lru_specs(t) + [pl.BlockSpec((t, LANES), lambda j: (0, j + nb))],
        out_specs=[col, col, pl.BlockSpec((CONV_WIDTH, LANES), lambda j: (0, j)), vec, mat, vec, mat, vec, vec],
        out_shape=[jax.ShapeDtypeStruct((t, GROUP_WIDTH), ACT_DTYPE), jax.ShapeDtypeStruct((t, GROUP_WIDTH), ACT_DTYPE),
                   jax.ShapeDtypeStruct((CONV_WIDTH, GROUP_WIDTH), F32), vshape, mshape, vshape, mshape, vshape, vshape],
        semantics=("parallel",), name="lru_bwd")(
            pel, pel, conv_w, conv_b, wa, ba, wx, bx, lam, dmix)


def _block_diag_pairs(w):
    z = jnp.zeros((LRU_BLOCK_DIM, LRU_BLOCK_DIM), w.dtype)
    return jnp.stack([jnp.block([[w[2 * j], z], [z, w[2 * j + 1]]]) for j in range(w.shape[0] // 2)])


def _block_diag_pairs_grad(dw):
    b = LRU_BLOCK_DIM
    return jnp.stack([dw[n // 2, (n % 2) * b:(n % 2 + 1) * b, (n % 2) * b:(n % 2 + 1) * b] for n in range(2 * dw.shape[0])])


def _row_tile(r):
    return ROW_TILE if r % ROW_TILE == 0 else r


def _pair_sum(g, got, place, name):
    _, r, c = g.shape
    tile = r

    def body(place_ref, a_ref, b_ref, o_ref):
        o_ref[...] = (a_ref[...].astype(F32) + b_ref[...].astype(F32)).astype(o_ref.dtype)

    blk = pl.BlockSpec((1, tile, c), lambda k, i, place_ref: (k, i, 0))
    return pl.pallas_call(
        body,
        grid_spec=pltpu.PrefetchScalarGridSpec(
            num_scalar_prefetch=1, grid=(N_CHIPS, r // tile),
            in_specs=[pl.BlockSpec((1, tile, c), lambda k, i, place_ref: (2 * k + place_ref[0], i, 0)), blk],
            out_specs=blk),
        out_shape=jax.ShapeDtypeStruct(got.shape, got.dtype),
        compiler_params=_params("parallel", "parallel"), name=name)(place, g, got)


def _adamw_update(g, w_ref, m_ref, v_ref, g_ref, d_ref, nm_ref, nv_ref):
    nm = ADAM_B1 * m_ref[...] + (1.0 - ADAM_B1) * g
    nv = ADAM_B2 * v_ref[...] + (1.0 - ADAM_B2) * jnp.square(g)
    m_hat = nm / (1.0 - ADAM_B1 ** ADAM_STEP)
    v_hat = nv / (1.0 - ADAM_B2 ** ADAM_STEP)
    g_ref[...] = g
    d_ref[...] = -ADAM_LR * (m_hat / (jnp.sqrt(v_hat) + ADAM_EPS) + ADAM_WD * w_ref[...])
    nm_ref[...] = nm
    nv_ref[...] = nv


def _adamw_sharded(parts, w, m, v, place, name, ride=None):
    n_layers, r, c = w.shape
    tile = _row_tile(r)
    nb = r // tile
    counts = [1 + len(recvs) for _, recvs in parts]

    def body(place_ref, *refs):
        layer = pl.program_id(0)
        g, at = None, 0
        for l in range(n_layers):
            g_l = refs[at][0].astype(F32)
            for r_ref in refs[at + 1:at + counts[l]]:
                for k in range(r_ref.shape[0]):
                    g_l = g_l + r_ref[k].astype(F32)
            g = g_l if g is None else jnp.where(layer == l, g_l, g)
            at += counts[l]
        _adamw_update(g, *refs[at:])

    def part_specs(l, recvs):
        rows = lambda q, i: jnp.where(q < l, 0, jnp.where(q > l, nb - 1, i))
        return ([pl.BlockSpec((1, tile, c), lambda q, i, place_ref: (place_ref[1], rows(q, i), 0))] +
                [pl.BlockSpec((a.shape[0], tile, c), lambda q, i, place_ref: (0, rows(q, i), 0)) for a in recvs])

    in_specs, args = [], []
    for l, (s, recvs) in enumerate(parts):
        in_specs += part_specs(l, recvs)
        args += [s, *recvs]
    blk = pl.BlockSpec((None, tile, c), lambda q, i, place_ref: (q, i, 0))
    out = jax.ShapeDtypeStruct((n_layers, r, c), F32)
    return _pcall(body, ride, grid=(n_layers, nb), in_specs=in_specs + [blk, blk, blk], out_specs=[blk, blk, blk, blk],
                  out_shape=[out, out, out, out], semantics=("arbitrary", "arbitrary"), name=name, prefetch=True)(
                      place, *args, w, m, v)


def _adamw_small(repl_parts, vec_parts, w, m, v, place):
    n_r, n = len(repl_parts), len(w)
    shapes = [a.shape for a in w]

    def body(place_ref, *refs):
        parts, rest = refs[:n], refs[n:]
        for k in range(n):
            take = (lambda p: parts[k][p]) if k < n_r else (lambda p: parts[k][p, 0])
            g = take(0)
            for p in range(1, N_DEV):
                g = g + take(p)
            _adamw_update(g, rest[k], rest[n + k], rest[2 * n + k], *rest[3 * n + 4 * k:3 * n + 4 * k + 4])

    def whole(shape):
        return pl.BlockSpec(shape, lambda i, place_ref: (0,) * len(shape))

    def mine(shard):
        return pl.BlockSpec((N_DEV, 1) + shard, lambda i, place_ref: (0, place_ref[2]) + (0,) * len(shard))

    in_specs = [whole(a.shape) for a in repl_parts] + [mine(s) for s in shapes[n_r:]] + [whole(s) for s in shapes] * 3
    outs = pl.pallas_call(
        body,
        grid_spec=pltpu.PrefetchScalarGridSpec(
            num_scalar_prefetch=1, grid=(1,), in_specs=in_specs,
            out_specs=[whole(s) for s in shapes for _ in range(4)]),
        out_shape=[jax.ShapeDtypeStruct(s, F32) for s in shapes for _ in range(4)],
        compiler_params=_params("arbitrary"), name="adamw_small")(place, *repl_parts, *vec_parts, *w, *m, *v)
    return [outs[4 * k:4 * k + 4] for k in range(n)]


SHARDED = {"norm_w": 2, "w_in_even": 2, "gla_w_a_up": 2, "w_out_even": 1, "w_in_odd": 2, "conv_w": 2, "conv_b": 1,
           "lru_b_a": 1, "lru_b_x": 1, "lru_lambda": 1, "w_out_odd": 1, "w_mlp_up": 2, "w_mlp_down": 1}
REPLICATED = ["gla_b_a", "gla_norm_w", "fox_b_f", "rel_bias", "lru_w_a", "lru_w_x"]
WEIGHTS = ["norm_w", "w_in_even", "gla_w_a_up", "gla_b_a", "gla_norm_w", "fox_b_f", "w_out_even", "w_in_odd",
           "rel_bias", "conv_w", "conv_b", "lru_w_a", "lru_b_a", "lru_w_x", "lru_b_x", "lru_lambda", "w_out_odd",
           "w_mlp_up", "w_mlp_down"]
MATRICES = ("w_in_even", "w_out_even", "w_in_odd", "w_out_odd", "w_mlp_up", "w_mlp_down")
TRANSPOSED = ("w_in_even", "w_in_odd")
VECTORS = tuple(n for n in SHARDED if n not in MATRICES)
MATRIX_BLOCKS = (("w_in_even", 0), ("w_out_even", 0), ("w_in_odd", 0), ("w_out_odd", 0),
                 ("w_mlp_up", 0), ("w_mlp_up", 1), ("w_mlp_down", 0), ("w_mlp_down", 1))


def _join_shards(blocks, axis):
    moved = jnp.moveaxis(blocks, 0, axis)
    shape = moved.shape
    return moved.reshape(shape[:axis] + (shape[axis] * shape[axis + 1],) + shape[axis + 2:])


def _split_shards(full, axis):
    shape = full.shape
    cut = full.reshape(shape[:axis] + (N_DEV, shape[axis] // N_DEV) + shape[axis + 1:])
    return jnp.moveaxis(cut, axis, 0)


EVEN_SPLITS = (0, 256, 512, 1024, 1536, 1552, 2064, 2576, 3088, 3096)


def _even_in_split(wt):
    c = [wt[EVEN_SPLITS[k]:EVEN_SPLITS[k + 1]] for k in range(9)]
    gq, gk, gv, gr, ga, fq, fk, fv, ff = c
    padrows = lambda a: jnp.pad(a, ((0, LANES - a.shape[0]), (0, 0)))
    return jnp.concatenate([gq, gk, gv, fq, fk, fv], axis=0), jnp.concatenate([gr, padrows(ga), padrows(ff)], axis=0)


def _even_in_merge(dmm, dele):
    return jnp.concatenate([dmm[:1024], dele[:512], dele[512:512 + GLA_RANK], dmm[1024:2560],
                            dele[640:640 + ATT_HEADS]], axis=0)


def _forward_backward(x, target, shard, vec_shard, w, place):
    w = dict(w)
    g, dnorm, sums, recv = {}, {}, {}, {}
    nrm = lambda l, k: w["norm_w"][l, k][None, :]
    gather = lambda *keys: _gather_plan([shard[k] for k in keys])
    blocks = lambda r, c: (N_DEV, r // N_DEV, c)

    def pair_sum(key):
        sums[key] = _pair_sum(g[key], got[key], place, f"rs_pair_sum_{key[0]}_{key[1]}")

    got = {}

    def mlp_fwd(xin, layer, ride_up, ride_down):
        up = _mm(xin, w["w_mlp_up"][layer], out_dtype=ACT_DTYPE, tm=TM_FWD, tn=D_FF // N_DEV, b_blocked=True,
                 a_norm=nrm(layer, 2), name=f"mlp_up_{layer}", ride=ride_up)
        (u, h), rode_up = up if ride_up is not None else (up, None)
        down = _mm(u, w["w_mlp_down"][layer], out_dtype=F32, tm=TM_DX // 2, tn=D_MODEL, a_sqrelu=True,
                   res_norm=(xin, nrm(layer, 3)), name=f"mlp_down_{layer}", ride=ride_down)
        (yv, xout), rode_down = down if ride_down is not None else (down, None)
        return xout, (xin, h, u, yv), rode_up, rode_down

    def mlp_bwd(dxout, saved, layer, ride):
        xin, h, u, yv = saved
        k_up, k_down = ("w_mlp_up", layer), ("w_mlp_down", layer)
        res = _mm(dxout, w["w_mlp_down"][layer], nt=True, out_dtype=ACT_DTYPE, tm=TM_DX, tn=TN, drelu_of=u,
                  a_norm_bwd=(yv, nrm(layer, 3)), name=f"mlp_down_dx_{layer}", ride=ride)
        (du, dy, dnorm[(layer, 3)]), rode = res if ride is not None else (res, None)
        g[k_down] = _mm(u, dy, ta=True, out_dtype=WIRE_DTYPE, tm=TM_DW, tn=TN, a_sqrelu=True,
                        name=f"mlp_down_dw_{layer}").reshape(blocks(D_FF, D_MODEL))
        g[k_up] = _mm(h, du, ta=True, out_dtype=WIRE_DTYPE, tm=TM_DW, tn=D_FF // N_DEV, out_blocked=True,
                      name=f"mlp_up_dw_{layer}")
        w_up = jnp.moveaxis(w["w_mlp_up"][layer], 0, 1).reshape(D_MODEL, D_FF)
        (dxin, dnorm[(layer, 2)]), (got[k_down], got[k_up]) = _mm(
            du, w_up, nt=True, out_dtype=F32, tm=TM_DX // 2, tn=D_MODEL, norm_bwd=(xin, nrm(layer, 2), dxout),
            name=f"mlp_up_dx_{layer}", ride=_sibling_plan([g[k_down], g[k_up]]))
        pair_sum(k_down)
        pair_sum(k_up)
        return dxin, rode

    first = _run_plan(_gather_plan([shard[("w_in_even", 0)]] + [vec_shard[n] for n in VECTORS]),
                      "weights_all_gather_first")
    w["w_in_even"] = first[0].reshape(-1, D_MODEL)
    for n, b in zip(VECTORS, first[1:]):
        w[n] = _join_shards(b, SHARDED[n])
    w["w_mlp_up"], w["w_mlp_down"] = [None] * DEPTH, [None] * DEPTH

    wmm_e, wel_e = _even_in_split(w["w_in_even"])
    w_up_pad = jnp.pad(w["gla_w_a_up"][0], ((0, LANES - GLA_RANK), (0, 0)))
    b_f_pad = jnp.pad(w["fox_b_f"], ((0, 0), (0, LANES - ATT_HEADS)))
    (pmm0, h0), (w_out_even,) = _mm(x, wmm_e, nt=True, out_dtype=ACT_DTYPE, tm=TM_FWD, tn=TN, a_norm=nrm(0, 0),
                                    name="in_even_mm", ride=gather(("w_out_even", 0)))
    pel0 = _mm(h0, wel_e, nt=True, out_dtype=F32, tm=TM_FWD, tn=768, name="in_even_el")
    (out_a, states), (w_in_odd,) = _gla_fwd(pmm0, pel0, w_up_pad, w["gla_b_a"], w["gla_norm_w"],
                                            ride=gather(("w_in_odd", 0)))
    cum, cum_t = _fox_gate_fwd(pel0, b_f_pad)
    (out_b, lse_b), (w["w_mlp_up"][0], w_mlp_down0) = _fox_fwd(pmm0, cum, cum_t,
                                                               ride=gather(("w_mlp_up", 0), ("w_mlp_down", 0)))
    w["w_out_even"] = w_out_even.reshape(D_MODEL, D_MODEL)
    w["w_mlp_down"][0] = w_mlp_down0.reshape(D_FF, D_MODEL)
    mix_in0 = jnp.concatenate([out_a, out_b], axis=1)
    mix0, x1 = _mm(mix_in0, w["w_out_even"], out_dtype=F32, tm=TM_DX, tn=D_MODEL, res_norm=(x, nrm(0, 1)),
                   name="out_even")
    x2, mlp0, _, (w["w_mlp_up"][1],) = mlp_fwd(x1, 0, None, gather(("w_mlp_up", 1)))
    w["w_in_odd"] = w_in_odd.reshape(-1, D_MODEL)

    w_in_o = w["w_in_odd"]
    n_mm_o = 3 * GROUP_WIDTH
    wa_bd, wx_bd = _block_diag_pairs(w["lru_w_a"][0]), _block_diag_pairs(w["lru_w_x"][0])
    base = _ca_bias_base(w["rel_bias"][0])
    pmm1, h1 = _mm(x2, w_in_o[:n_mm_o], nt=True, out_dtype=ACT_DTYPE, tm=TM_FWD, tn=TN, a_norm=nrm(1, 0),
                   name="in_odd_mm")
    pel1 = _mm(h1, w_in_o[n_mm_o:], nt=True, out_dtype=F32, tm=TM_FWD, tn=TN, name="in_odd_el")
    kp = jnp.pad(pmm1[:, GROUP_WIDTH:2 * GROUP_WIDTH], ((CA_LEFT, 0), (0, 0)))
    vp = jnp.pad(pmm1[:, 2 * GROUP_WIDTH:], ((CA_LEFT, 0), (0, 0)))
    (out_c, lse_c), (w_mlp_down1,) = _ca_fwd(pmm1, kp, vp, base, ride=gather(("w_mlp_down", 1)))
    w["w_mlp_down"][1] = w_mlp_down1.reshape(D_FF, D_MODEL)
    lru_args = (pel1, w["conv_w"][0], w["conv_b"], wa_bd, w["lru_b_a"], wx_bd, w["lru_b_x"], w["lru_lambda"])
    out_d, (w_out_odd,) = _lru_fwd(*lru_args, ride=gather(("w_out_odd", 0)))
    w["w_out_odd"] = w_out_odd.reshape(D_MODEL, D_MODEL)
    mix_in1 = jnp.concatenate([out_c, out_d], axis=1)
    mix1, x3 = _mm(mix_in1, w["w_out_odd"], out_dtype=F32, tm=TM_DX, tn=D_MODEL, res_norm=(x2, nrm(1, 1)),
                   name="out_odd")
    x4, mlp1, _, _ = mlp_fwd(x3, 1, None, None)

    loss, dx4 = _loss_fwd_bwd(x4, target)

    k_oo, k_io, k_oe, k_ie = ("w_out_odd", 0), ("w_in_odd", 0), ("w_out_even", 0), ("w_in_even", 0)
    mlp_keys = lambda l: [("w_mlp_down", l), ("w_mlp_up", l)]
    dx3, _ = mlp_bwd(dx4, mlp1, 1, None)
    dmix_in1, dmix1, dnorm[(1, 1)] = _mm(dx3, w["w_out_odd"], nt=True, out_dtype=F32, tm=TM_DX, tn=TN,
                                         a_norm_bwd=(mix1, nrm(1, 1)), name="out_odd_dx")
    g[k_oo] = _mm(mix_in1, dmix1, ta=True, out_dtype=WIRE_DTYPE, tm=TM_DW, tn=TN, name="out_odd_dw").reshape(
        blocks(D_MODEL, D_MODEL))
    (dq_c, dkp, dvp, dbase), rode = _ca_bwd(
        pmm1, kp, vp, base, lse_c, dmix_in1,
        ride=_join_plans(_chip_plan([sums[k] for k in mlp_keys(1)]), _sibling_plan([g[k_oo]])))
    recv.update(zip(mlp_keys(1), rode[:2]))
    got[k_oo] = rode[2]
    pair_sum(k_oo)
    (dgate, dxin, g_conv_w, g_conv_b, dwa_bd, g_lru_b_a, dwx_bd, g_lru_b_x, g_lru_lambda), (recv[k_oo],) = _lru_bwd(
        *lru_args, dmix_in1, ride=_chip_plan([sums[k_oo]]))
    dp1 = jnp.concatenate([dq_c, dkp[CA_LEFT:].astype(ACT_DTYPE), dvp[CA_LEFT:].astype(ACT_DTYPE), dgate, dxin], axis=1)
    g[k_io] = _mm(dp1, h1, ta=True, out_dtype=WIRE_DTYPE, tm=dp1.shape[1] // 2, tn=TN, name="in_odd_dw").reshape(
        blocks(dp1.shape[1], D_MODEL))
    (dx2, dnorm[(1, 0)]), (got[k_io],) = _mm(dp1, w_in_o, out_dtype=F32, tm=TM_DX // 2, tn=D_MODEL,
                                             norm_bwd=(x2, nrm(1, 0), dx3), name="in_odd_dx",
                                             ride=_sibling_plan([g[k_io]]))
    pair_sum(k_io)
    g["rel_bias"] = _ca_bias_base_grad(dbase)[None]
    g["conv_w"], g["conv_b"] = g_conv_w[None], g_conv_b
    g["lru_w_a"], g["lru_w_x"] = _block_diag_pairs_grad(dwa_bd)[None], _block_diag_pairs_grad(dwx_bd)[None]
    g["lru_b_a"], g["lru_b_x"], g["lru_lambda"] = g_lru_b_a, g_lru_b_x, g_lru_lambda

    dx1, (recv[k_io],) = mlp_bwd(dx2, mlp0, 0, _chip_plan([sums[k_io]]))
    dmix_in0, dmix0, dnorm[(0, 1)] = _mm(dx1, w["w_out_even"], nt=True, out_dtype=F32, tm=TM_DX, tn=TN,
                                         a_norm_bwd=(mix0, nrm(0, 1)), name="out_even_dx")
    g[k_oe] = _mm(mix_in0, dmix0, ta=True, out_dtype=WIRE_DTYPE, tm=TM_DW, tn=TN, name="out_even_dw").reshape(
        blocks(D_MODEL, D_MODEL))
    k_md0, k_mu0 = mlp_keys(0)
    (dq_a, dk_a, dv_a, dr_a, da_a, dw_up_pad, g_gla_b_a, g_gla_norm_w), (got[k_oe], recv[k_md0]) = _gla_bwd(
        pmm0, pel0, w_up_pad, w["gla_b_a"], w["gla_norm_w"], states, dmix_in0,
        ride=_join_plans(_sibling_plan([g[k_oe]]), _chip_plan([sums[k_md0]])))
    pair_sum(k_oe)
    (dq_b, dk_b, dv_b, dcum_t, dcum_q), (recv[k_mu0], recv[k_oe]) = _fox_bwd(
        pmm0, cum, cum_t, lse_b, dmix_in0, ride=_chip_plan([sums[k_mu0], sums[k_oe]]))
    df_b, db_f = _fox_gate_bwd(pel0, b_f_pad, dcum_t, dcum_q)
    g["gla_w_a_up"] = dw_up_pad[:GLA_RANK][None]
    g["gla_b_a"], g["gla_norm_w"], g["fox_b_f"] = g_gla_b_a, g_gla_norm_w, db_f[:, :ATT_HEADS]
    dp0 = jnp.concatenate([dq_a, dk_a, dv_a, dq_b, dk_b.astype(ACT_DTYPE), dv_b.astype(ACT_DTYPE), dr_a, da_a, df_b],
                          axis=1)
    w_perm = jnp.concatenate([wmm_e, wel_e], axis=0)
    n_mm_e = wmm_e.shape[0]
    dw_perm, repl_parts = _mm(dp0, h0, ta=True, out_dtype=WIRE_DTYPE, tm=dp0.shape[1] // 2, tn=TN, name="in_even_dw",
                              ride=_gather_plan([g[n] for n in REPLICATED]))
    dw_even = _even_in_merge(dw_perm[:n_mm_e], dw_perm[n_mm_e:])
    g[k_ie] = dw_even.reshape(blocks(dw_even.shape[0], D_MODEL))
    dh0, (got[k_ie],) = _mm(dp0, w_perm, out_dtype=F32, tm=TM_DX, tn=TN, name="in_even_dx",
                            ride=_sibling_plan([g[k_ie]]))
    pair_sum(k_ie)
    dx0, dnorm[(0, 0)] = _norm_bwd(dh0, x, nrm(0, 0), out_dtype=F32, add=dx1, name="norm_in_bwd_0")

    g["norm_w"] = jnp.stack([jnp.concatenate([dnorm[(l, k)] for k in range(4)], axis=0) for l in range(DEPTH)])
    vec_parts = _run_plan(_gather_plan([_split_shards(g[n], SHARDED[n]) for n in VECTORS]), "vector_grads_all_gather")
    return loss, dx0, sums, recv, repl_parts, vec_parts


def kernel(x, norm_w, w_in_even, gla_w_a_up, gla_b_a, gla_norm_w, fox_b_f, w_out_even, w_in_odd, rel_bias, conv_w, conv_b, lru_w_a, lru_b_a, lru_w_x, lru_b_x, lru_lambda, w_out_odd, w_mlp_up, w_mlp_down, loss_target, m_norm_w, m_w_in_even, m_gla_w_a_up, m_gla_b_a, m_gla_norm_w, m_fox_b_f, m_w_out_even, m_w_in_odd, m_rel_bias, m_conv_w, m_conv_b, m_lru_w_a, m_lru_b_a, m_lru_w_x, m_lru_b_x, m_lru_lambda, m_w_out_odd, m_w_mlp_up, m_w_mlp_down, v_norm_w, v_w_in_even, v_gla_w_a_up, v_gla_b_a, v_gla_norm_w, v_fox_b_f, v_w_out_even, v_w_in_odd, v_rel_bias, v_conv_w, v_conv_b, v_lru_w_a, v_lru_b_a, v_lru_w_x, v_lru_b_x, v_lru_lambda, v_w_out_odd, v_w_mlp_up, v_w_mlp_down):
    wts = dict(zip(WEIGHTS, (norm_w, w_in_even, gla_w_a_up, gla_b_a, gla_norm_w, fox_b_f, w_out_even, w_in_odd, rel_bias,
                             conv_w, conv_b, lru_w_a, lru_b_a, lru_w_x, lru_b_x, lru_lambda, w_out_odd, w_mlp_up,
                             w_mlp_down)))
    mom = dict(zip(WEIGHTS, (m_norm_w, m_w_in_even, m_gla_w_a_up, m_gla_b_a, m_gla_norm_w, m_fox_b_f, m_w_out_even,
                             m_w_in_odd, m_rel_bias, m_conv_w, m_conv_b, m_lru_w_a, m_lru_b_a, m_lru_w_x, m_lru_b_x,
                             m_lru_lambda, m_w_out_odd, m_w_mlp_up, m_w_mlp_down)))
    var = dict(zip(WEIGHTS, (v_norm_w, v_w_in_even, v_gla_w_a_up, v_gla_b_a, v_gla_norm_w, v_fox_b_f, v_w_out_even,
                             v_w_in_odd, v_rel_bias, v_conv_w, v_conv_b, v_lru_w_a, v_lru_b_a, v_lru_w_x, v_lru_b_x,
                             v_lru_lambda, v_w_out_odd, v_w_mlp_up, v_w_mlp_down)))
    ax, ay, ac = lax.axis_index("x"), lax.axis_index("y"), lax.axis_index("c")
    place = jnp.stack([ac, 2 * ax + ay, 4 * ax + 2 * ay + ac]).astype(jnp.int32)

    shard = {(n, l): (wts[n][l].T if n in TRANSPOSED else wts[n][l]).astype(WIRE_DTYPE) for n, l in MATRIX_BLOCKS}
    loss_blk, dx, sums, recv, repl_parts, vec_parts = _forward_backward(
        x[0], loss_target[0], shard, {n: wts[n] for n in VECTORS}, {n: wts[n] for n in REPLICATED}, place)
    loss = lax.psum(loss_blk[0, 0], ("x", "y", "c"))

    view = lambda n, a: jnp.swapaxes(a, 1, 2) if n in TRANSPOSED else a
    upd = {}

    def adamw(n, ride=None):
        received = lambda k: recv[k] if isinstance(recv[k], list) else [recv[k]]
        res = _adamw_sharded([(sums[(n, l)], received((n, l))) for l in range(wts[n].shape[0])], view(n, wts[n]),
                             view(n, mom[n]), view(n, var[n]), place, f"adamw_{n}", ride=ride)
        res, rode = res if ride is not None else (res, None)
        upd[n] = [view(n, o) for o in res]
        return rode

    k_ie = ("w_in_even", 0)
    (diagonal,) = adamw("w_mlp_up", _chip_plan([sums[k_ie]], relations=(2,)))
    (neighbours,) = adamw("w_mlp_down", _chip_plan([sums[k_ie]], relations=(0, 1)))
    recv[k_ie] = [neighbours, diagonal]
    for n in MATRICES:
        if n not in upd:
            adamw(n)
    small = REPLICATED + list(VECTORS)
    upd.update(zip(small, _adamw_small(repl_parts, vec_parts, [wts[n] for n in small], [mom[n] for n in small],
                                       [var[n] for n in small], place)))
    return (loss, dx[None], *[upd[n][kind] for kind in range(4) for n in WEIGHTS])
```

```python
import functools
from typing import Callable, NamedTuple, Optional

import jax
import jax.numpy as jnp
from jax import lax
from jax.experimental import pallas as pl
from jax.experimental.pallas import tpu as pltpu

F32 = jnp.float32
MXU_DTYPE = jnp.bfloat16
ACT_DTYPE = jnp.bfloat16
WIRE_DTYPE = jnp.bfloat16

V7X_VMEM_BYTES = 64 * 1024 * 1024
VMEM_LIMIT = (V7X_VMEM_BYTES * 7) // 8
LANES = 128

D_MODEL = 1024
SEQ = 2048
DEPTH = 2
CHUNK = 64
GROUP_WIDTH = D_MODEL // 2
D_FF = 4 * D_MODEL
NORM_EPS = 1e-6
GLA_HEADS = 4
GLA_DV = GROUP_WIDTH // GLA_HEADS
GLA_DK = GLA_DV // 2
GLA_KW = GLA_HEADS * GLA_DK
GLA_RANK = 16
GLA_GATE_TAU = 16.0
HEAD_DIM = 64
ATT_HEADS = GROUP_WIDTH // HEAD_DIM
CA_LEFT = 8 * CHUNK
REL_CLIP = 128
LRU_BLOCK_DIM = 64
CONV_WIDTH = 4
LRU_C = 8.0
N_DEV = 8

ADAM_LR = 0.001
ADAM_B1 = 0.9
ADAM_B2 = 0.999
ADAM_EPS = 1e-08
ADAM_WD = 0.01
ADAM_STEP = 10

NEG = float(jnp.finfo(jnp.float32).min)
MESH = pl.DeviceIdType.MESH


def _params(*sem):
    return pltpu.CompilerParams(dimension_semantics=sem, vmem_limit_bytes=VMEM_LIMIT)


def _dot(a, b, ca=1, cb=0):
    return lax.dot_general(a.astype(MXU_DTYPE), b.astype(MXU_DTYPE), (((ca,), (cb,)), ((), ())),
                           preferred_element_type=F32)


def _dot_exact(a, b):
    return lax.dot_general(a, b, (((1,), (0,)), ((), ())), precision=lax.Precision.HIGHEST,
                           preferred_element_type=F32)


def _log_sigmoid(x):
    return jnp.minimum(x, 0.0) - jnp.log1p(jnp.exp(-jnp.abs(x)))


def _iota(shape, axis):
    return lax.broadcasted_iota(jnp.int32, shape, axis)


ANY = pl.BlockSpec(memory_space=pl.ANY)
N_CHIPS = 4


class _Plan(NamedTuple):
    ins: list
    outs: list
    sems: list
    start: Callable
    finish: Callable
    relay: Optional[Callable] = None


def _place():
    x, y, c = lax.axis_index("x"), lax.axis_index("y"), lax.axis_index("c")
    return x, y, c, [(1 - x, y), (x, 1 - y), (1 - x, 1 - y)]


def _gather_plan(xs):
    n = len(xs)

    def parts(x_refs, out_refs, sems):
        send_sems, recv_sems, local_sems = sems
        x, y, c, chips = _place()
        me, sibling = (x, y, c), (x, y, 1 - c)

        def rows(a, px, py, pc):
            return out_refs[a].at[4 * px + 2 * py + pc]

        def copy(a, k, block, to, src=None):
            return pltpu.make_async_remote_copy(
                src_ref=rows(a, *block) if src is None else src, dst_ref=rows(a, *block),
                send_sem=send_sems.at[7 * a + k], recv_sem=recv_sems.at[7 * a + k], device_id=to, device_id_type=MESH)

        def own():
            mine = [pltpu.make_async_copy(x_refs[a], rows(a, *me), local_sems.at[a]) for a in range(n)]
            first = []
            for a in range(n):
                first.append(copy(a, 0, me, sibling, src=x_refs[a]))
                first += [copy(a, 1 + j, me, (*chip, c), src=x_refs[a]) for j, chip in enumerate(chips)]
            return mine, first

        return c, me, sibling, chips, copy, own

    def start(x_refs, out_refs, sems):
        mine, first = parts(x_refs, out_refs, sems)[-1]()
        for cp in first + mine:
            cp.start()

    def relay(x_refs, out_refs, sems):
        c, me, sibling, chips, copy, _ = parts(x_refs, out_refs, sems)
        for j, chip in enumerate(chips):
            for a in range(n):
                copy(a, 1 + j, (*chip, c), me).wait_recv()
                copy(a, 4 + j, (*chip, c), sibling).start()

    def finish(x_refs, out_refs, sems):
        c, me, sibling, chips, copy, own = parts(x_refs, out_refs, sems)
        mine, first = own()
        for a in range(n):
            copy(a, 0, sibling, me).wait_recv()
            for j, chip in enumerate(chips):
                copy(a, 4 + j, (*chip, 1 - c), me).wait_recv()
        for cp in first + [copy(a, 4 + j, (*chip, c), sibling) for j, chip in enumerate(chips) for a in range(n)]:
            cp.wait_send()
        for cp in mine:
            cp.wait()

    return _Plan(list(xs), [jax.ShapeDtypeStruct((N_DEV,) + x.shape, x.dtype) for x in xs],
                 [pltpu.SemaphoreType.DMA((7 * n,)), pltpu.SemaphoreType.DMA((7 * n,)), pltpu.SemaphoreType.DMA((n,))],
                 start, finish, relay)


def _exchange_plan(copies_of, ins, outs, per_array):
    n = len(ins)

    def start(in_refs, out_refs, sems):
        for cp in copies_of(in_refs, out_refs, sems):
            cp.start()

    def finish(in_refs, out_refs, sems):
        copies = copies_of(in_refs, out_refs, sems)
        for cp in copies:
            cp.wait_recv()
        for cp in copies:
            cp.wait_send()

    return _Plan(list(ins), outs, [pltpu.SemaphoreType.DMA((per_array * n,)), pltpu.SemaphoreType.DMA((per_array * n,))],
                 start, finish)


def _sibling_plan(gs):
    def copies_of(g_refs, got_refs, sems):
        x, y, c, _ = _place()
        return [pltpu.make_async_remote_copy(
            src_ref=g_refs[a].at[2 * k + (1 - c)], dst_ref=got_refs[a].at[k], send_sem=sems[0].at[N_CHIPS * a + k],
            recv_sem=sems[1].at[N_CHIPS * a + k], device_id=(x, y, 1 - c), device_id_type=MESH)
            for a in range(len(gs)) for k in range(N_CHIPS)]

    return _exchange_plan(copies_of, gs, [jax.ShapeDtypeStruct((N_CHIPS,) + g.shape[1:], g.dtype) for g in gs], N_CHIPS)


def _chip_plan(ss, relations=(0, 1, 2)):
    n_rel = len(relations)

    def copies_of(s_refs, out_refs, sems):
        x, y, c, chips = _place()
        return [pltpu.make_async_remote_copy(
            src_ref=s_refs[a].at[2 * chips[j][0] + chips[j][1]], dst_ref=out_refs[a].at[slot],
            send_sem=sems[0].at[n_rel * a + slot], recv_sem=sems[1].at[n_rel * a + slot],
            device_id=(*chips[j], c), device_id_type=MESH)
            for a in range(len(ss)) for slot, j in enumerate(relations)]

    return _exchange_plan(copies_of, ss, [jax.ShapeDtypeStruct((n_rel,) + s.shape[1:], s.dtype) for s in ss], n_rel)


def _join_plans(*plans):
    def cut(refs, counts):
        at = 0
        for n in counts:
            yield refs[at:at + n]
            at += n

    def each(in_refs, out_refs, sems):
        return zip(plans, cut(in_refs, [len(p.ins) for p in plans]), cut(out_refs, [len(p.outs) for p in plans]),
                   cut(sems, [len(p.sems) for p in plans]))

    def start(*refs):
        for p, i, o, s in each(*refs):
            p.start(i, o, s)

    def relay(*refs):
        for p, i, o, s in each(*refs):
            if p.relay is not None:
                p.relay(i, o, s)

    def finish(*refs):
        for p, i, o, s in each(*refs):
            p.finish(i, o, s)

    return _Plan([a for p in plans for a in p.ins], [a for p in plans for a in p.outs],
                 [a for p in plans for a in p.sems], start, finish, relay)


def _run_plan(plan, name):
    n_in, n_out = len(plan.ins), len(plan.outs)

    def body(*refs):
        args = refs[:n_in], refs[n_in:n_in + n_out], refs[n_in + n_out:]
        plan.start(*args)
        if plan.relay is not None:
            plan.relay(*args)
        plan.finish(*args)

    return pl.pallas_call(body, out_shape=plan.outs, in_specs=[ANY] * n_in, out_specs=[ANY] * n_out,
                          scratch_shapes=plan.sems, name=name)(*plan.ins)


def _pcall(body, ride, *, grid, in_specs, out_specs, out_shape, scratch_shapes=(), semantics, name, prefetch=False):
    n_pre = int(prefetch)

    def build(kernel, ins, outs, shapes, scratch, sem):
        if prefetch:
            return pl.pallas_call(
                kernel, grid_spec=pltpu.PrefetchScalarGridSpec(num_scalar_prefetch=1, grid=grid, in_specs=ins,
                                                               out_specs=outs, scratch_shapes=scratch),
                out_shape=shapes, compiler_params=_params(*sem), name=name)
        return pl.pallas_call(kernel, grid=grid, in_specs=ins, out_specs=outs, out_shape=shapes,
                              scratch_shapes=scratch, compiler_params=_params(*sem), name=name)

    if ride is None:
        return build(body, in_specs, out_specs, out_shape, list(scratch_shapes), semantics)
    single = not isinstance(out_shape, (list, tuple))
    out_specs_l, out_shape_l = ([out_specs], [out_shape]) if single else (list(out_specs), list(out_shape))
    n_in, n_out, n_scr = len(in_specs), len(out_shape_l), len(scratch_shapes)
    r_in, r_out = len(ride.ins), len(ride.outs)

    def riding(*refs):
        pre, refs = refs[:n_pre], refs[n_pre:]
        cuts = [n_in, r_in, n_out, r_out, n_scr]
        groups, at = [], 0
        for width in cuts:
            groups.append(refs[at:at + width])
            at += width
        ins, r_ins, outs, r_outs, scr = groups
        sems = refs[at:]
        first = functools.reduce(jnp.logical_and, [pl.program_id(d) == 0 for d in range(len(grid))])
        last = functools.reduce(jnp.logical_and, [pl.program_id(d) == grid[d] - 1 for d in range(len(grid))])

        @pl.when(first)
        def _():
            ride.start(r_ins, r_outs, sems)

        several_steps = any(n > 1 for n in grid)
        if ride.relay is not None and several_steps:
            @pl.when(last)
            def _():
                ride.relay(r_ins, r_outs, sems)

        body(*pre, *ins, *outs, *scr)

        @pl.when(last)
        def _():
            if ride.relay is not None and not several_steps:
                ride.relay(r_ins, r_outs, sems)
            ride.finish(r_ins, r_outs, sems)

    call = build(riding, list(in_specs) + [ANY] * r_in, out_specs_l + [ANY] * r_out, out_shape_l + list(ride.outs),
                 list(scratch_shapes) + list(ride.sems), ["arbitrary"] * len(grid))

    def run(*args):
        res = call(*args, *ride.ins)
        return (res[0] if single else list(res[:n_out])), list(res[n_out:])

    return run


def _rms(x):
    return x * lax.rsqrt(jnp.mean(x * x, axis=-1, keepdims=True) + NORM_EPS)


def _mm(a, b, *, nt=False, ta=False, out_dtype, tm, tn, a_sqrelu=False, drelu_of=None, b_blocked=False,
        out_blocked=False, a_norm=None, a_norm_bwd=None, res_norm=None, norm_bwd=None, name, ride=None):
    k, m = a.shape if ta else a.shape[::-1]
    if b_blocked:
        assert not nt and b.shape[1] == k and b.shape[2] == tn
        n = b.shape[0] * tn
    else:
        n = b.shape[0] if nt else b.shape[1]
        assert (b.shape[1] if nt else b.shape[0]) == k
    tm, tn = min(tm, m), min(tn, n)
    assert m % tm == 0 and n % tn == 0
    assert (res_norm is None and norm_bwd is None) or tn == n
    assert a_norm is None or a_norm_bwd is None
    n_in = (2 + (drelu_of is not None) + (a_norm is not None) + 2 * (a_norm_bwd is not None)
            + 2 * (res_norm is not None) + 3 * (norm_bwd is not None))

    def body(*refs):
        a_ref, b_ref = refs[0], refs[1]
        extra = list(refs[2:n_in])
        outs = list(refs[n_in:])
        o_ref = outs.pop(0)
        u_ref = extra.pop(0) if drelu_of is not None else None
        if a_norm is not None:
            wn_ref, h_ref, h_scr = extra.pop(0), outs.pop(0), outs.pop()

            @pl.when(pl.program_id(1) == 0)
            def _():
                h = (_rms(a_ref[...]) * wn_ref[...]).astype(ACT_DTYPE)
                h_scr[...] = h
                h_ref[...] = h

            av = h_scr[...]
        elif a_norm_bwd is not None:
            y_ref, wy_ref = extra.pop(0), extra.pop(0)
            dy_ref, dwy_ref, dy_scr = outs.pop(0), outs.pop(0), outs.pop()
            first_rows = pl.program_id(0) == 0

            @pl.when(pl.program_id(1) == 0)
            def _():
                yv, up = y_ref[...], a_ref[...]
                rstd = lax.rsqrt(jnp.mean(yv * yv, axis=-1, keepdims=True) + NORM_EPS)
                yhat = yv * rstd
                g = up * wy_ref[...]
                dy = (rstd * (g - yhat * jnp.mean(g * yhat, axis=-1, keepdims=True))).astype(ACT_DTYPE)
                dy_scr[...] = dy
                dy_ref[...] = dy

                @pl.when(first_rows)
                def _():
                    dwy_ref[...] = jnp.zeros_like(dwy_ref)

                dwy_ref[...] += jnp.sum(up * yhat, axis=0, keepdims=True)

            av = dy_scr[...]
        else:
            av = a_ref[...]
        if a_sqrelu:
            av = jnp.square(jnp.maximum(av.astype(F32), 0.0))
        acc = _dot(av, b_ref[...], 0 if ta else 1, 1 if nt else 0)
        if u_ref is not None:
            acc = acc * (2.0 * jnp.maximum(u_ref[...].astype(F32), 0.0))
        if norm_bwd is not None:
            x_ref, wb_ref, add_ref = extra
            dw_ref = outs[0]
            xv = x_ref[...]
            rstd = lax.rsqrt(jnp.mean(xv * xv, axis=-1, keepdims=True) + NORM_EPS)
            xhat = xv * rstd
            g = acc * wb_ref[...]
            o_ref[...] = rstd * (g - xhat * jnp.mean(g * xhat, axis=-1, keepdims=True)) + add_ref[...]

            @pl.when(pl.program_id(0) == 0)
            def _():
                dw_ref[...] = jnp.zeros_like(dw_ref)

            dw_ref[...] += jnp.sum(acc * xhat, axis=0, keepdims=True)
            return
        o_ref[...] = acc.astype(out_dtype)
        if res_norm is not None:
            res_ref, wr_ref = extra
            outs[0][...] = res_ref[...] + _rms(acc) * wr_ref[...]

    if b_blocked:
        b_spec = pl.BlockSpec((None, k, tn), lambda i, j: (j, 0, 0))
    elif nt:
        b_spec = pl.BlockSpec((tn, k), lambda i, j: (j, 0))
    else:
        b_spec = pl.BlockSpec((k, tn), lambda i, j: (0, j))
    a_spec = pl.BlockSpec((k, tm), lambda i, j: (0, i)) if ta else pl.BlockSpec((tm, k), lambda i, j: (i, 0))
    in_specs = [a_spec, b_spec]
    args = [a, b]
    if drelu_of is not None:
        in_specs.append(pl.BlockSpec((tm, tn), lambda i, j: (i, j)))
        args.append(drelu_of)
    if out_blocked:
        out_specs = [pl.BlockSpec((None, tm, tn), lambda i, j: (j, i, 0))]
        out_shape = [jax.ShapeDtypeStruct((n // tn, m, tn), out_dtype)]
    else:
        out_specs = [pl.BlockSpec((tm, tn), lambda i, j: (i, j))]
        out_shape = [jax.ShapeDtypeStruct((m, n), out_dtype)]
    scratch = []
    if a_norm is not None:
        assert not ta
        in_specs.append(pl.BlockSpec((1, k), lambda i, j: (0, 0)))
        args.append(a_norm)
        out_specs.append(pl.BlockSpec((tm, k), lambda i, j: (i, 0)))
        out_shape.append(jax.ShapeDtypeStruct((m, k), ACT_DTYPE))
        scratch.append(pltpu.VMEM((tm, k), ACT_DTYPE))
    if a_norm_bwd is not None:
        assert not ta
        in_specs += [pl.BlockSpec((tm, k), lambda i, j: (i, 0)), pl.BlockSpec((1, k), lambda i, j: (0, 0))]
        args += list(a_norm_bwd)
        out_specs += [pl.BlockSpec((tm, k), lambda i, j: (i, 0)), pl.BlockSpec((1, k), lambda i, j: (0, 0))]
        out_shape += [jax.ShapeDtypeStruct((m, k), ACT_DTYPE), jax.ShapeDtypeStruct((1, k), F32)]
        scratch.append(pltpu.VMEM((tm, k), ACT_DTYPE))
    if res_norm is not None:
        in_specs += [pl.BlockSpec((tm, n), lambda i, j: (i, 0)), pl.BlockSpec((1, n), lambda i, j: (0, 0))]
        args += list(res_norm)
        out_specs.append(pl.BlockSpec((tm, n), lambda i, j: (i, 0)))
        out_shape.append(jax.ShapeDtypeStruct((m, n), F32))
    if norm_bwd is not None:
        rows = pl.BlockSpec((tm, n), lambda i, j: (i, 0))
        in_specs += [rows, pl.BlockSpec((1, n), lambda i, j: (0, 0)), rows]
        args += list(norm_bwd)
        out_specs.append(pl.BlockSpec((1, n), lambda i, j: (0, 0)))
        out_shape.append(jax.ShapeDtypeStruct((1, n), F32))
    single = len(out_shape) == 1
    return _pcall(body, ride, grid=(m // tm, n // tn), in_specs=in_specs,
                  out_specs=out_specs[0] if single else out_specs, out_shape=out_shape[0] if single else out_shape,
                  scratch_shapes=scratch, semantics=("arbitrary", "arbitrary"), name=name)(*args)


ROW_TILE = 512
TM_FWD, TM_DX, TM_DW, TN = 2048, 1024, 1024, 512


def _norm_bwd(dy, x, w, *, out_dtype, add=None, name, ride=None):
    t, d = x.shape

    def body(*refs):
        dy_ref, x_ref, w_ref = refs[0], refs[1], refs[2]
        dx_ref, dw_ref = refs[-2], refs[-1]
        xv = x_ref[...]
        rstd = lax.rsqrt(jnp.mean(xv * xv, axis=-1, keepdims=True) + NORM_EPS)
        xhat = xv * rstd
        dyv = dy_ref[...].astype(F32)
        g = dyv * w_ref[...]
        dx = rstd * (g - xhat * jnp.mean(g * xhat, axis=-1, keepdims=True))
        if add is not None:
            dx = dx + refs[3][...]
        dx_ref[...] = dx.astype(out_dtype)

        @pl.when(pl.program_id(0) == 0)
        def _():
            dw_ref[...] = jnp.zeros_like(dw_ref)

        dw_ref[...] += jnp.sum(dyv * xhat, axis=0, keepdims=True)

    row = pl.BlockSpec((ROW_TILE, d), lambda i: (i, 0))
    vec = pl.BlockSpec((1, d), lambda i: (0, 0))
    in_specs = [row, row, vec] + ([row] if add is not None else [])
    args = [dy, x, w] + ([add] if add is not None else [])
    return _pcall(body, ride, grid=(t // ROW_TILE,), in_specs=in_specs, out_specs=[row, vec],
                  out_shape=[jax.ShapeDtypeStruct((t, d), out_dtype), jax.ShapeDtypeStruct((1, d), F32)],
                  semantics=("arbitrary",), name=name)(*args)


def _loss_fwd_bwd(y, target):
    t, d = y.shape

    def body(y_ref, t_ref, l_ref, dy_ref):
        diff = y_ref[...] - t_ref[...]
        dy_ref[...] = diff * (1.0 / d)

        @pl.when(pl.program_id(0) == 0)
        def _():
            l_ref[...] = jnp.zeros_like(l_ref)

        l_ref[...] += 0.5 * jnp.sum(jnp.mean(diff * diff, axis=-1, keepdims=True), axis=0, keepdims=True)

    row = pl.BlockSpec((ROW_TILE, d), lambda i: (i, 0))
    return pl.pallas_call(body, grid=(t // ROW_TILE,), in_specs=[row, row],
                          out_specs=[pl.BlockSpec((8, LANES), lambda i: (0, 0)), row],
                          out_shape=[jax.ShapeDtypeStruct((8, LANES), F32), jax.ShapeDtypeStruct((t, d), F32)],
                          compiler_params=_params("arbitrary"), name="loss")(y, target)


GLA_STATE = (GLA_HEADS * GLA_DV, GLA_KW)


def _gla_specs(chunk_of):
    rows = lambda width, col: pl.BlockSpec((CHUNK, width), lambda i: (chunk_of(i), col))
    const = lambda r, c: pl.BlockSpec((r, c), lambda i: (0, 0))
    return [rows(GLA_KW, 0),
            rows(GLA_KW, 1),
            rows(GROUP_WIDTH, 1),
            rows(GROUP_WIDTH, 0),
            rows(LANES, 4),
            const(LANES, GLA_KW),
            const(1, GLA_KW),
            const(1, GROUP_WIDTH)]


def _gla_chunk(q_ref, k_ref, v_ref, a_ref, wup_ref, ba_ref):
    z = _dot(a_ref[...], wup_ref[...]) + ba_ref[...]
    tri = (_iota((CHUNK, CHUNK), 1) <= _iota((CHUNK, CHUNK), 0)).astype(F32)
    cum = _dot_exact(tri, _log_sigmoid(z) * (1.0 / GLA_GATE_TAU))
    tot = cum[CHUNK - 1:CHUNK, :]
    e = jnp.exp(tot - cum)
    return (z, e, jnp.exp(tot), k_ref[...].astype(F32) * e, q_ref[...].astype(F32) * (GLA_DK ** -0.5),
            v_ref[...].astype(F32))


def _gla_head_mask():
    return _iota(GLA_STATE, 0) // GLA_DV == _iota(GLA_STATE, 1) // GLA_DK


def _gla_fwd(pmm, pel, w_up, b_a, gnorm_w, ride=None):
    t = pmm.shape[0]
    nc = t // CHUNK

    def body(q_ref, k_ref, v_ref, r_ref, a_ref, wup_ref, ba_ref, gw_ref, o_ref, st_ref, m_scr):
        @pl.when(pl.program_id(0) == 0)
        def _():
            m_scr[...] = jnp.zeros_like(m_scr)

        _, _, decay, kd, qs, vv = _gla_chunk(q_ref, k_ref, v_ref, a_ref, wup_ref, ba_ref)
        m = m_scr[...] * decay + jnp.where(_gla_head_mask(), _dot(vv, kd, 0, 0), 0.0)
        m_scr[...] = m
        st_ref[...] = m
        o = _dot(qs, m, 1, 1)
        rr = r_ref[...]
        gate = rr * jax.nn.sigmoid(rr) * gw_ref[...]
        for h in range(GLA_HEADS):
            vs = slice(h * GLA_DV, (h + 1) * GLA_DV)
            oh = o[:, vs]
            y = oh * lax.rsqrt(jnp.mean(oh * oh, axis=-1, keepdims=True) + NORM_EPS)
            o_ref[:, vs] = (y * gate[:, vs]).astype(o_ref.dtype)

    return _pcall(
        body, ride, grid=(nc,), in_specs=_gla_specs(lambda i: i),
        out_specs=[pl.BlockSpec((CHUNK, GROUP_WIDTH), lambda i: (i, 0)),
                   pl.BlockSpec((None,) + GLA_STATE, lambda i: (i, 0, 0))],
        out_shape=[jax.ShapeDtypeStruct((t, GROUP_WIDTH), ACT_DTYPE), jax.ShapeDtypeStruct((nc,) + GLA_STATE, F32)],
        scratch_shapes=[pltpu.VMEM(GLA_STATE, F32)],
        semantics=("arbitrary",), name="gla_fwd")(pmm, pmm, pmm, pel, pel, w_up, b_a, gnorm_w)


def _gla_bwd(pmm, pel, w_up, b_a, gnorm_w, states, dmix, ride=None):
    t = pmm.shape[0]
    nc = t // CHUNK
    scale = GLA_DK ** -0.5

    def body(q_ref, k_ref, v_ref, r_ref, a_ref, wup_ref, ba_ref, gw_ref, st_ref, prev_ref, do_ref,
             dq_ref, dk_ref, dv_ref, dr_ref, da_ref, dwup_ref, dba_ref, dgw_ref, dm_scr):
        step = pl.program_id(0)

        @pl.when(step == 0)
        def _():
            dm_scr[...] = jnp.zeros_like(dm_scr)
            dwup_ref[...] = jnp.zeros_like(dwup_ref)
            dba_ref[...] = jnp.zeros_like(dba_ref)
            dgw_ref[...] = jnp.zeros_like(dgw_ref)

        z, e, decay, kd, qs, vv = _gla_chunk(q_ref, k_ref, v_ref, a_ref, wup_ref, ba_ref)
        m = st_ref[...]
        m_prev = prev_ref[...] * (step < nc - 1).astype(F32)
        rr, dout, gw = r_ref[...], do_ref[...], gw_ref[...]
        sig = jax.nn.sigmoid(rr)
        silu = rr * sig
        dsilu = sig * (1.0 + rr * (1.0 - sig))
        o = _dot(qs, m, 1, 1)
        d_o, dgw = [], []
        for h in range(GLA_HEADS):
            vs = slice(h * GLA_DV, (h + 1) * GLA_DV)
            oh, dg = o[:, vs], dout[:, vs]
            rstd = lax.rsqrt(jnp.mean(oh * oh, axis=-1, keepdims=True) + NORM_EPS)
            y = oh * rstd
            dgw.append(jnp.sum(dg * y * silu[:, vs], axis=0, keepdims=True))
            dr_ref[:, vs] = (dg * y * gw[:, vs] * dsilu[:, vs]).astype(dr_ref.dtype)
            dy = dg * gw[:, vs] * silu[:, vs]
            d_o.append(rstd * (dy - y * jnp.mean(dy * y, axis=-1, keepdims=True)))
        d_o = jnp.concatenate(d_o, axis=1)
        dgw_ref[...] += jnp.concatenate(dgw, axis=1)
        dq_ref[...] = (_dot(d_o, m) * scale).astype(dq_ref.dtype)
        dm = dm_scr[...] + jnp.where(_gla_head_mask(), _dot(d_o, qs, 0, 0), 0.0)
        dv_ref[...] = _dot(kd, dm, 1, 1).astype(dv_ref.dtype)
        dkd = _dot(vv, dm)
        dk_ref[...] = (dkd * e).astype(dk_ref.dtype)
        dm_scr[...] = dm * decay
        tri_strict = (_iota((CHUNK, CHUNK), 1) < _iota((CHUNK, CHUNK), 0)).astype(F32)
        dla = jnp.sum(dm * m_prev, axis=0, keepdims=True) * decay + _dot_exact(tri_strict, dkd * kd)
        dz = dla * jax.nn.sigmoid(-z) * (1.0 / GLA_GATE_TAU)
        da_ref[...] = _dot(dz, wup_ref[...], 1, 1).astype(da_ref.dtype)
        dwup_ref[...] += _dot(a_ref[...], dz, 0, 0)
        dba_ref[...] += jnp.sum(dz, axis=0, keepdims=True)

    chunk_of = lambda i: nc - 1 - i
    in_specs = _gla_specs(chunk_of) + [
        pl.BlockSpec((None,) + GLA_STATE, lambda i: (chunk_of(i), 0, 0)),
        pl.BlockSpec((None,) + GLA_STATE, lambda i: (jnp.maximum(chunk_of(i) - 1, 0), 0, 0)),
        pl.BlockSpec((CHUNK, GROUP_WIDTH), lambda i: (chunk_of(i), 0))]
    rows = lambda width: pl.BlockSpec((CHUNK, width), lambda i: (chunk_of(i), 0))
    const = lambda r, c: pl.BlockSpec((r, c), lambda i: (0, 0))
    return _pcall(
        body, ride, grid=(nc,), in_specs=in_specs,
        out_specs=[rows(GLA_KW), rows(GLA_KW), rows(GROUP_WIDTH), rows(GROUP_WIDTH), rows(LANES),
                   const(LANES, GLA_KW), const(1, GLA_KW), const(1, GROUP_WIDTH)],
        out_shape=[jax.ShapeDtypeStruct((t, GLA_KW), ACT_DTYPE), jax.ShapeDtypeStruct((t, GLA_KW), ACT_DTYPE),
                   jax.ShapeDtypeStruct((t, GROUP_WIDTH), ACT_DTYPE), jax.ShapeDtypeStruct((t, GROUP_WIDTH), ACT_DTYPE),
                   jax.ShapeDtypeStruct((t, LANES), ACT_DTYPE), jax.ShapeDtypeStruct((LANES, GLA_KW), F32),
                   jax.ShapeDtypeStruct((1, GLA_KW), F32), jax.ShapeDtypeStruct((1, GROUP_WIDTH), F32)],
        scratch_shapes=[pltpu.VMEM(GLA_STATE, F32)],
        semantics=("arbitrary",), name="gla_bwd")(
            pmm, pmm, pmm, pel, pel, w_up, b_a, gnorm_w, states, states, dmix)


CUM_BLOCK = 256


def _fox_gate_fwd(pel, b_f):
    t = pel.shape[0]
    nb = t // CUM_BLOCK

    def body(f_ref, b_ref, cum_ref, cum_t_ref):
        tri = (_iota((CUM_BLOCK, CUM_BLOCK), 1) <= _iota((CUM_BLOCK, CUM_BLOCK), 0)).astype(F32)
        carry = jnp.zeros((1, LANES), F32)
        for blk in range(nb):
            rows = slice(blk * CUM_BLOCK, (blk + 1) * CUM_BLOCK)
            cum = _dot_exact(tri, _log_sigmoid(f_ref[rows, :] + b_ref[...])) + carry
            cum_ref[rows, :] = cum
            cum_t_ref[blk] = cum.T[:ATT_HEADS, :]
            carry = cum[CUM_BLOCK - 1:CUM_BLOCK, :]

    return pl.pallas_call(
        body, grid=(1,),
        in_specs=[pl.BlockSpec((t, LANES), lambda i: (0, 5)), pl.BlockSpec((1, LANES), lambda i: (0, 0))],
        out_specs=[pl.BlockSpec((t, LANES), lambda i: (0, 0)),
                   pl.BlockSpec((nb, ATT_HEADS, CUM_BLOCK), lambda i: (0, 0, 0))],
        out_shape=[jax.ShapeDtypeStruct((t, LANES), F32), jax.ShapeDtypeStruct((nb, ATT_HEADS, CUM_BLOCK), F32)],
        compiler_params=_params("arbitrary"), name="fox_gate_fwd")(pel, b_f)


def _fox_gate_bwd(pel, b_f, dcum_t, dcum_q):
    t = pel.shape[0]
    nb = t // CUM_BLOCK

    def body(f_ref, b_ref, dct_ref, dcq_ref, df_ref, db_ref):
        tri_up = (_iota((CUM_BLOCK, CUM_BLOCK), 1) >= _iota((CUM_BLOCK, CUM_BLOCK), 0)).astype(F32)
        carry = jnp.zeros((1, LANES), F32)
        db = jnp.zeros((1, LANES), F32)
        for blk in reversed(range(nb)):
            rows = slice(blk * CUM_BLOCK, (blk + 1) * CUM_BLOCK)
            query_side = sum(dcq_ref[pair, rows, :] for pair in range(dcq_ref.shape[0]))
            dls = _dot_exact(tri_up, dct_ref[blk].T + query_side) + carry
            carry = dls[0:1, :]
            df = dls * jax.nn.sigmoid(-(f_ref[rows, :] + b_ref[...]))
            df_ref[rows, :] = df.astype(df_ref.dtype)
            db = db + jnp.sum(df, axis=0, keepdims=True)
        db_ref[...] = db

    return pl.pallas_call(
        body, grid=(1,),
        in_specs=[pl.BlockSpec((t, LANES), lambda i: (0, 5)), pl.BlockSpec((1, LANES), lambda i: (0, 0)),
                  pl.BlockSpec((nb, LANES, CUM_BLOCK), lambda i: (0, 0, 0)),
                  pl.BlockSpec((dcum_q.shape[0], t, LANES), lambda i: (0, 0, 0))],
        out_specs=[pl.BlockSpec((t, LANES), lambda i: (0, 0)), pl.BlockSpec((1, LANES), lambda i: (0, 0))],
        out_shape=[jax.ShapeDtypeStruct((t, LANES), ACT_DTYPE), jax.ShapeDtypeStruct((1, LANES), F32)],
        compiler_params=_params("arbitrary"), name="fox_gate_bwd")(pel, b_f, dcum_t, dcum_q)


FOX_Q_BLOCK = 256


assert FOX_Q_BLOCK == CUM_BLOCK
FOX_KEY_STEP = 512


def _fox_scores(q_ref, k_ref, cum_ref, cum_t_ref, h, i):
    hs = slice(h * HEAD_DIM, (h + 1) * HEAD_DIM)
    nb = cum_t_ref.shape[0]
    key_gate = jnp.concatenate([cum_t_ref[kb, h:h + 1, :] for kb in range(nb)], axis=1)
    s = _dot(q_ref[:, hs], k_ref[:, hs], 1, 1) * (HEAD_DIM ** -0.5) + (cum_ref[:, h:h + 1] - key_gate)
    shape = (FOX_Q_BLOCK, nb * FOX_Q_BLOCK)
    return jnp.where(_iota(shape, 1) <= i * FOX_Q_BLOCK + _iota(shape, 0), s, NEG)


def _fox_specs(t):
    bq, nb = FOX_Q_BLOCK, t // FOX_Q_BLOCK
    return [pl.BlockSpec((bq, GROUP_WIDTH), lambda i: (i, 2)), pl.BlockSpec((t, GROUP_WIDTH), lambda i: (0, 3)),
            pl.BlockSpec((t, GROUP_WIDTH), lambda i: (0, 4)), pl.BlockSpec((bq, LANES), lambda i: (i, 0)),
            pl.BlockSpec((nb, ATT_HEADS, bq), lambda i: (0, 0, 0))]


def _fox_fwd(pmm, cum, cum_t, ride=None):
    t = pmm.shape[0]
    bq = FOX_Q_BLOCK

    def body(q_ref, k_ref, v_ref, cum_ref, cum_t_ref, o_ref, lse_ref):
        i = pl.program_id(0)
        lse_ref[...] = jnp.zeros_like(lse_ref)
        for h in range(ATT_HEADS):
            hs = slice(h * HEAD_DIM, (h + 1) * HEAD_DIM)
            s = _fox_scores(q_ref, k_ref, cum_ref, cum_t_ref, h, i)
            m = jnp.max(s, axis=-1, keepdims=True)
            p = jnp.exp(s - m)
            l = jnp.sum(p, axis=-1, keepdims=True)
            o_ref[:, hs] = (_dot(p, v_ref[:, hs]) / l).astype(o_ref.dtype)
            lse_ref[:, h:h + 1] = m + jnp.log(l)

    return _pcall(
        body, ride, grid=(t // bq,), in_specs=_fox_specs(t),
        out_specs=[pl.BlockSpec((bq, GROUP_WIDTH), lambda i: (i, 0)), pl.BlockSpec((bq, LANES), lambda i: (i, 0))],
        out_shape=[jax.ShapeDtypeStruct((t, GROUP_WIDTH), ACT_DTYPE), jax.ShapeDtypeStruct((t, LANES), F32)],
        semantics=("parallel",), name="fox_fwd")(pmm, pmm, pmm, cum, cum_t)


def _fox_bwd(pmm, cum, cum_t, lse, dmix, ride=None):
    t = pmm.shape[0]
    bq, nb = FOX_Q_BLOCK, t // FOX_Q_BLOCK
    pairs, per_pair = ATT_HEADS // 2, LANES // HEAD_DIM
    scale = HEAD_DIM ** -0.5

    def body(q_ref, k_ref, v_ref, cum_ref, cum_t_ref, lse_ref, do_ref, dq_ref, dk_ref, dv_ref, dct_ref, dcq_ref):
        g, i = pl.program_id(0), pl.program_id(1)

        @pl.when(i == 0)
        def _():
            dk_ref[...] = jnp.zeros_like(dk_ref)
            dv_ref[...] = jnp.zeros_like(dv_ref)

        @pl.when((i == 0) & (g == 0))
        def _():
            dct_ref[...] = jnp.zeros_like(dct_ref)

        lane = _iota((1, LANES), 1)

        def run(n):
            causal = _iota((bq, n), 1) <= i * bq + _iota((bq, n), 0)
            dcq = jnp.zeros((bq, LANES), F32)
            for hh in range(per_pair):
                h = per_pair * g + hh
                hs = slice(hh * HEAD_DIM, (hh + 1) * HEAD_DIM)
                pick = (lane == h).astype(F32)
                cq = jnp.sum(cum_ref[...] * pick, axis=1, keepdims=True)
                lse_h = jnp.sum(lse_ref[...] * pick, axis=1, keepdims=True)
                key_gate = jnp.concatenate([cum_t_ref[kb, pl.ds(h, 1), :] for kb in range(n // bq)], axis=1)
                s = _dot(q_ref[:, hs], k_ref[:n, hs], 1, 1) * scale + (cq - key_gate)
                p = jnp.exp(jnp.where(causal, s, NEG) - lse_h)
                do = do_ref[:, hs]
                dp = _dot(do, v_ref[:n, hs], 1, 1)
                ds = p * (dp - jnp.sum(p * dp, axis=-1, keepdims=True))
                dq_ref[:, hs] = (_dot(ds, k_ref[:n, hs]) * scale).astype(dq_ref.dtype)
                dk_ref[:n, hs] += _dot(ds, q_ref[:, hs], 0, 0) * scale
                dv_ref[:n, hs] += _dot(p, do, 0, 0)
                key_side = -jnp.sum(ds, axis=0, keepdims=True)
                for kb in range(n // bq):
                    dct_ref[kb, pl.ds(h, 1), :] += key_side[:, kb * bq:(kb + 1) * bq]
                dcq = dcq + jnp.sum(ds, axis=1, keepdims=True) * pick
            dcq_ref[...] = dcq

        for kx in range(t // FOX_KEY_STEP):
            pl.when(i // (FOX_KEY_STEP // bq) == kx)(functools.partial(run, (kx + 1) * FOX_KEY_STEP))

    cols = lambda first: pl.BlockSpec((bq, LANES), lambda g, i: (i, first + g))
    keys = lambda first: pl.BlockSpec((t, LANES), lambda g, i: (0, first + g))
    per_head = pl.BlockSpec((bq, LANES), lambda g, i: (i, 0))
    fox_q, fox_k, fox_v = (GROUP_WIDTH * n // LANES for n in (2, 3, 4))
    return _pcall(
        body, ride, grid=(pairs, t // bq),
        in_specs=[cols(fox_q), keys(fox_k), keys(fox_v), per_head,
                  pl.BlockSpec((nb, ATT_HEADS, bq), lambda g, i: (0, 0, 0)), per_head, cols(GROUP_WIDTH // LANES)],
        out_specs=[cols(0), keys(0), keys(0), pl.BlockSpec((nb, LANES, bq), lambda g, i: (0, 0, 0)),
                   pl.BlockSpec((None, bq, LANES), lambda g, i: (g, i, 0))],
        out_shape=[jax.ShapeDtypeStruct((t, GROUP_WIDTH), ACT_DTYPE), jax.ShapeDtypeStruct((t, GROUP_WIDTH), F32),
                   jax.ShapeDtypeStruct((t, GROUP_WIDTH), F32), jax.ShapeDtypeStruct((nb, LANES, bq), F32),
                   jax.ShapeDtypeStruct((pairs, t, LANES), F32)],
        semantics=("arbitrary", "arbitrary"), name="fox_bwd")(pmm, pmm, pmm, cum, cum_t, lse, dmix)


CA_Q_BLOCK = 4 * CHUNK
CA_WINDOW = CA_Q_BLOCK + CA_LEFT
CA_BASE = 1024


def _ca_bias_base(rel_bias):
    n = rel_bias.shape[0]
    flat = CA_Q_BLOCK + CA_LEFT - REL_CLIP
    tail = CA_BASE - flat - (2 * REL_CLIP + 1)
    return jnp.concatenate([jnp.broadcast_to(rel_bias[:, 2 * REL_CLIP:], (n, flat)), rel_bias[:, ::-1],
                            jnp.broadcast_to(rel_bias[:, :1], (n, tail))], axis=1)


def _ca_bias_base_grad(dbase):
    flat = CA_Q_BLOCK + CA_LEFT - REL_CLIP
    mid = dbase[:, flat:flat + 2 * REL_CLIP + 1][:, ::-1]
    lo = jnp.sum(dbase[:, flat + 2 * REL_CLIP + 1:], axis=1, keepdims=True)
    hi = jnp.sum(dbase[:, :flat], axis=1, keepdims=True)
    pad = jnp.zeros((dbase.shape[0], 2 * REL_CLIP - 1), F32)
    return mid + jnp.concatenate([lo, pad, hi], axis=1)


def _ca_mask(i):
    r, j = _iota((CA_Q_BLOCK, CA_WINDOW), 0), _iota((CA_Q_BLOCK, CA_WINDOW), 1)
    rc, jc = r // CHUNK, j // CHUNK
    return (jc >= rc) & (jc <= rc + CA_LEFT // CHUNK) & (i * CA_Q_BLOCK + j >= CA_LEFT)


def _ca_fill_bias(i, base_ref, bias_scr):
    @pl.when(i == 0)
    def _():
        for h in range(ATT_HEADS):
            rows = jnp.broadcast_to(base_ref[h:h + 1, :], (CA_Q_BLOCK, CA_BASE))
            bias_scr[h] = pltpu.roll(rows, CA_BASE - CA_Q_BLOCK, 1, stride=1, stride_axis=0)[:, :CA_WINDOW]


def _ca_scores(q_ref, kp_ref, bias_scr, win, h, mask):
    hs = slice(h * HEAD_DIM, (h + 1) * HEAD_DIM)
    s = _dot(q_ref[:, hs], kp_ref[win, hs], 1, 1) * (HEAD_DIM ** -0.5)
    return jnp.where(mask, s + bias_scr[h], NEG)


CA_BIAS_SCRATCH = pltpu.VMEM((ATT_HEADS, CA_Q_BLOCK, CA_WINDOW), F32)


def _ca_fwd(pmm, kp, vp, base, ride=None):
    t = pmm.shape[0]

    def body(q_ref, kp_ref, vp_ref, base_ref, o_ref, lse_ref, bias_scr):
        i = pl.program_id(0)
        _ca_fill_bias(i, base_ref, bias_scr)
        win = pl.ds(pl.multiple_of(i * CA_Q_BLOCK, CA_Q_BLOCK), CA_WINDOW)
        mask = _ca_mask(i)
        lse_ref[...] = jnp.zeros_like(lse_ref)
        for h in range(ATT_HEADS):
            hs = slice(h * HEAD_DIM, (h + 1) * HEAD_DIM)
            s = _ca_scores(q_ref, kp_ref, bias_scr, win, h, mask)
            m = jnp.max(s, axis=-1, keepdims=True)
            p = jnp.exp(s - m)
            l = jnp.sum(p, axis=-1, keepdims=True)
            o_ref[:, hs] = (_dot(p, vp_ref[win, hs]) / l).astype(o_ref.dtype)
            lse_ref[:, h:h + 1] = m + jnp.log(l)

    padded = pl.BlockSpec((t + CA_LEFT, GROUP_WIDTH), lambda i: (0, 0))
    return _pcall(
        body, ride, grid=(t // CA_Q_BLOCK,),
        in_specs=[pl.BlockSpec((CA_Q_BLOCK, GROUP_WIDTH), lambda i: (i, 0)), padded, padded,
                  pl.BlockSpec((ATT_HEADS, CA_BASE), lambda i: (0, 0))],
        out_specs=[pl.BlockSpec((CA_Q_BLOCK, GROUP_WIDTH), lambda i: (i, 0)),
                   pl.BlockSpec((CA_Q_BLOCK, LANES), lambda i: (i, 0))],
        out_shape=[jax.ShapeDtypeStruct((t, GROUP_WIDTH), ACT_DTYPE), jax.ShapeDtypeStruct((t, LANES), F32)],
        scratch_shapes=[CA_BIAS_SCRATCH], semantics=("arbitrary",), name="ca_fwd")(pmm, kp, vp, base)


def _ca_bwd(pmm, kp, vp, base, lse, dmix, ride=None):
    t = pmm.shape[0]
    scale = HEAD_DIM ** -0.5

    def body(q_ref, kp_ref, vp_ref, base_ref, lse_ref, do_ref, dq_ref, dkp_ref, dvp_ref, dbase_ref, bias_scr):
        i = pl.program_id(0)
        _ca_fill_bias(i, base_ref, bias_scr)

        @pl.when(i == 0)
        def _():
            dkp_ref[...] = jnp.zeros_like(dkp_ref)
            dvp_ref[...] = jnp.zeros_like(dvp_ref)
            dbase_ref[...] = jnp.zeros_like(dbase_ref)

        win = pl.ds(pl.multiple_of(i * CA_Q_BLOCK, CA_Q_BLOCK), CA_WINDOW)
        mask = _ca_mask(i)
        flip = (_iota((CA_Q_BLOCK, CA_Q_BLOCK), 0) + _iota((CA_Q_BLOCK, CA_Q_BLOCK), 1) == CA_Q_BLOCK - 1).astype(F32)
        for h in range(ATT_HEADS):
            hs = slice(h * HEAD_DIM, (h + 1) * HEAD_DIM)
            s = _ca_scores(q_ref, kp_ref, bias_scr, win, h, mask)
            p = jnp.exp(s - lse_ref[:, h:h + 1])
            do = do_ref[:, hs]
            dp = _dot(do, vp_ref[win, hs], 1, 1)
            ds = p * (dp - jnp.sum(p * dp, axis=-1, keepdims=True))
            dq_ref[:, hs] = (_dot(ds, kp_ref[win, hs]) * scale).astype(dq_ref.dtype)
            dkp_ref[win, hs] += _dot(ds, q_ref[:, hs], 0, 0) * scale
            dvp_ref[win, hs] += _dot(p, do, 0, 0)
            rev = jnp.concatenate([_dot(flip, ds), jnp.zeros((CA_Q_BLOCK, CA_BASE - CA_WINDOW), F32)], axis=1)
            lined = pltpu.roll(rev, 1, 1, stride=1, stride_axis=0)
            dbase_ref[h:h + 1, :] += jnp.sum(lined, axis=0, keepdims=True)

    padded = pl.BlockSpec((t + CA_LEFT, GROUP_WIDTH), lambda i: (0, 0))
    return _pcall(
        body, ride, grid=(t // CA_Q_BLOCK,),
        in_specs=[pl.BlockSpec((CA_Q_BLOCK, GROUP_WIDTH), lambda i: (i, 0)), padded, padded,
                  pl.BlockSpec((ATT_HEADS, CA_BASE), lambda i: (0, 0)),
                  pl.BlockSpec((CA_Q_BLOCK, LANES), lambda i: (i, 0)),
                  pl.BlockSpec((CA_Q_BLOCK, GROUP_WIDTH), lambda i: (i, 0))],
        out_specs=[pl.BlockSpec((CA_Q_BLOCK, GROUP_WIDTH), lambda i: (i, 0)), padded, padded,
                   pl.BlockSpec((ATT_HEADS, CA_BASE), lambda i: (0, 0))],
        out_shape=[jax.ShapeDtypeStruct((t, GROUP_WIDTH), ACT_DTYPE),
                   jax.ShapeDtypeStruct((t + CA_LEFT, GROUP_WIDTH), F32),
                   jax.ShapeDtypeStruct((t + CA_LEFT, GROUP_WIDTH), F32),
                   jax.ShapeDtypeStruct((ATT_HEADS, CA_BASE), F32)],
        scratch_shapes=[CA_BIAS_SCRATCH], semantics=("arbitrary",), name="ca_bwd")(pmm, kp, vp, base, lse, dmix)


GELU_C = 0.7978845608028654
GELU_A = 0.044715


def _shift_down(v, k, fill, period=None):
    rows = _iota(v.shape, 0)
    rows = rows if period is None else rows & (period - 1)
    return jnp.where(rows >= k, pltpu.roll(v, k, 0), fill)


def _shift_up(v, k, fill, period=None):
    t = v.shape[0]
    rows = _iota(v.shape, 0)
    rows, length = (rows, t) if period is None else (rows & (period - 1), period)
    return jnp.where(rows < length - k, pltpu.roll(v, t - k, 0), fill)


LRU_SCAN_BLOCK = 256


def _linear_scan(a, b, reverse=False):
    shift = _shift_up if reverse else _shift_down
    k = 1
    while k < LRU_SCAN_BLOCK:
        b = a * shift(b, k, 0.0, LRU_SCAN_BLOCK) + b
        a = a * shift(a, k, 1.0, LRU_SCAN_BLOCK)
        k *= 2
    nb = a.shape[0] // LRU_SCAN_BLOCK
    carry = jnp.zeros((1, a.shape[1]), F32)
    out = [None] * nb
    for blk in (reversed(range(nb)) if reverse else range(nb)):
        rows = slice(blk * LRU_SCAN_BLOCK, (blk + 1) * LRU_SCAN_BLOCK)
        h = b[rows] + a[rows] * carry
        out[blk] = h
        carry = h[0:1] if reverse else h[LRU_SCAN_BLOCK - 1:LRU_SCAN_BLOCK]
    return jnp.concatenate(out, axis=0)


def _neg_expm1(y):
    series = -y * (1.0 + y * (0.5 + y * (1.0 / 6.0 + y * (1.0 / 24.0 + y * (1.0 / 120.0)))))
    return jnp.where(y > -0.1, series, 1.0 - jnp.exp(y))


def _lru_forward(x, g_in, cw, cb, wa, ba, wx, bx, lam):
    xs = [_shift_down(x, CONV_WIDTH - 1 - j, 0.0) for j in range(CONV_WIDTH - 1)] + [x]
    xc = cb + sum(cw[j:j + 1, :] * xs[j] for j in range(CONV_WIDTH))
    r = jax.nn.sigmoid(_dot(xc, wa) + ba)
    i = jax.nn.sigmoid(_dot(xc, wx) + bx)
    lsl = _log_sigmoid(lam)
    la = LRU_C * r * lsl
    a = jnp.exp(la)
    s = jnp.sqrt(_neg_expm1(2.0 * la))
    h = _linear_scan(a, s * (i * xc))
    u = GELU_C * (g_in + GELU_A * g_in * g_in * g_in)
    th = jnp.tanh(u)
    gelu = 0.5 * g_in * (1.0 + th)
    return xs, xc, r, i, lsl, a, s, h, th, gelu


def _lru_specs(t):
    col = lambda off: pl.BlockSpec((t, LANES), lambda j: (0, j + off))
    vec = pl.BlockSpec((1, LANES), lambda j: (0, j))
    mat = pl.BlockSpec((None, LANES, LANES), lambda j: (j, 0, 0))
    return [col(0), col(GROUP_WIDTH // LANES), pl.BlockSpec((CONV_WIDTH, LANES), lambda j: (0, j)),
            vec, mat, vec, mat, vec, vec]


def _lru_fwd(pel, conv_w, conv_b, wa, ba, wx, bx, lam, ride=None):
    t = pel.shape[0]

    def body(g_ref, x_ref, cw_ref, cb_ref, wa_ref, ba_ref, wx_ref, bx_ref, lam_ref, o_ref):
        res = _lru_forward(x_ref[...], g_ref[...], cw_ref[...], cb_ref[...], wa_ref[...], ba_ref[...],
                           wx_ref[...], bx_ref[...], lam_ref[...])
        o_ref[...] = (res[7] * res[9]).astype(o_ref.dtype)

    return _pcall(
        body, ride, grid=(GROUP_WIDTH // LANES,), in_specs=_lru_specs(t),
        out_specs=pl.BlockSpec((t, LANES), lambda j: (0, j)),
        out_shape=jax.ShapeDtypeStruct((t, GROUP_WIDTH), ACT_DTYPE),
        semantics=("parallel",), name="lru_fwd")(pel, pel, conv_w, conv_b, wa, ba, wx, bx, lam)


def _lru_bwd(pel, conv_w, conv_b, wa, ba, wx, bx, lam, dmix, ride=None):
    t = pel.shape[0]

    def body(g_ref, x_ref, cw_ref, cb_ref, wa_ref, ba_ref, wx_ref, bx_ref, lam_ref, do_ref,
             dg_ref, dx_ref, dcw_ref, dcb_ref, dwa_ref, dba_ref, dwx_ref, dbx_ref, dlam_ref):
        g_in, cw, lam = g_ref[...], cw_ref[...], lam_ref[...]
        xs, xc, r, i, lsl, a, s, h, th, gelu = _lru_forward(
            x_ref[...], g_in, cw, cb_ref[...], wa_ref[...], ba_ref[...], wx_ref[...], bx_ref[...], lam)
        dout = do_ref[...]
        dgelu = 0.5 * (1.0 + th) + 0.5 * g_in * (1.0 - th * th) * GELU_C * (1.0 + 3.0 * GELU_A * g_in * g_in)
        dg_ref[...] = (dout * h * dgelu).astype(dg_ref.dtype)
        gsum = _linear_scan(_shift_up(a, 1, 0.0), dout * gelu, reverse=True)
        da = gsum * _shift_down(h, 1, 0.0)
        di = gsum * s * xc
        dla = da * a - gsum * (i * xc) * (a * a / s)
        dlam_ref[...] = jnp.sum(dla * (LRU_C * r), axis=0, keepdims=True) * jax.nn.sigmoid(-lam)
        dpr = dla * (LRU_C * lsl) * r * (1.0 - r)
        dpi = di * i * (1.0 - i)
        dxc = gsum * s * i + _dot(dpr, wa_ref[...], 1, 1) + _dot(dpi, wx_ref[...], 1, 1)
        xct = xc.T
        dwa_ref[...] = _dot(xct, dpr)
        dwx_ref[...] = _dot(xct, dpi)
        dba_ref[...] = jnp.sum(dpr, axis=0, keepdims=True)
        dbx_ref[...] = jnp.sum(dpi, axis=0, keepdims=True)
        dcb_ref[...] = jnp.sum(dxc, axis=0, keepdims=True)
        for j in range(CONV_WIDTH):
            dcw_ref[j:j + 1, :] = jnp.sum(dxc * xs[j], axis=0, keepdims=True)
        dx = cw[CONV_WIDTH - 1:CONV_WIDTH, :] * dxc
        for j in range(CONV_WIDTH - 1):
            dx = dx + cw[j:j + 1, :] * _shift_up(dxc, CONV_WIDTH - 1 - j, 0.0)
        dx_ref[...] = dx.astype(dx_ref.dtype)

    col = pl.BlockSpec((t, LANES), lambda j: (0, j))
    vec = pl.BlockSpec((1, LANES), lambda j: (0, j))
    mat = pl.BlockSpec((None, LANES, LANES), lambda j: (j, 0, 0))
    nb = GROUP_WIDTH // LANES
    vshape = jax.ShapeDtypeStruct((1, GROUP_WIDTH), F32)
    mshape = jax.ShapeDtypeStruct((nb, LANES, LANES), F32)
    return _pcall(
        body, ride, grid=(nb,),
        in_specs=_lru_specs(t) + [pl.BlockSpec((t, LANES), lambda j: (0, j + nb))],
        out_specs=[col, col, pl.BlockSpec((CONV_WIDTH, LANES), lambda j: (0, j)), vec, mat, vec, mat, vec, vec],
        out_shape=[jax.ShapeDtypeStruct((t, GROUP_WIDTH), ACT_DTYPE), jax.ShapeDtypeStruct((t, GROUP_WIDTH), ACT_DTYPE),
                   jax.ShapeDtypeStruct((CONV_WIDTH, GROUP_WIDTH), F32), vshape, mshape, vshape, mshape, vshape, vshape],
        semantics=("parallel",), name="lru_bwd")(
            pel, pel, conv_w, conv_b, wa, ba, wx, bx, lam, dmix)


def _block_diag_pairs(w):
    z = jnp.zeros((LRU_BLOCK_DIM, LRU_BLOCK_DIM), w.dtype)
    return jnp.stack([jnp.block([[w[2 * j], z], [z, w[2 * j + 1]]]) for j in range(w.shape[0] // 2)])


def _block_diag_pairs_grad(dw):
    b = LRU_BLOCK_DIM
    return jnp.stack([dw[n // 2, (n % 2) * b:(n % 2 + 1) * b, (n % 2) * b:(n % 2 + 1) * b] for n in range(2 * dw.shape[0])])


def _row_tile(r):
    return ROW_TILE if r % ROW_TILE == 0 else r


def _pair_sum(g, got, place, name):
    _, r, c = g.shape
    tile = r

    def body(place_ref, a_ref, b_ref, o_ref):
        o_ref[...] = (a_ref[...].astype(F32) + b_ref[...].astype(F32)).astype(o_ref.dtype)

    blk = pl.BlockSpec((1, tile, c), lambda k, i, place_ref: (k, i, 0))
    return pl.pallas_call(
        body,
        grid_spec=pltpu.PrefetchScalarGridSpec(
            num_scalar_prefetch=1, grid=(N_CHIPS, r // tile),
            in_specs=[pl.BlockSpec((1, tile, c), lambda k, i, place_ref: (2 * k + place_ref[0], i, 0)), blk],
            out_specs=blk),
        out_shape=jax.ShapeDtypeStruct(got.shape, got.dtype),
        compiler_params=_params("parallel", "parallel"), name=name)(place, g, got)


def _adamw_update(g, w_ref, m_ref, v_ref, g_ref, d_ref, nm_ref, nv_ref):
    nm = ADAM_B1 * m_ref[...] + (1.0 - ADAM_B1) * g
    nv = ADAM_B2 * v_ref[...] + (1.0 - ADAM_B2) * jnp.square(g)
    m_hat = nm / (1.0 - ADAM_B1 ** ADAM_STEP)
    v_hat = nv / (1.0 - ADAM_B2 ** ADAM_STEP)
    g_ref[...] = g
    d_ref[...] = -ADAM_LR * (m_hat / (jnp.sqrt(v_hat) + ADAM_EPS) + ADAM_WD * w_ref[...])
    nm_ref[...] = nm
    nv_ref[...] = nv


def _adamw_sharded(parts, w, m, v, place, name, ride=None):
    n_layers, r, c = w.shape
    tile = _row_tile(r)
    nb = r // tile
    counts = [1 + len(recvs) for _, recvs in parts]

    def body(place_ref, *refs):
        layer = pl.program_id(0)
        g, at = None, 0
        for l in range(n_layers):
            g_l = refs[at][0].astype(F32)
            for r_ref in refs[at + 1:at + counts[l]]:
                for k in range(r_ref.shape[0]):
                    g_l = g_l + r_ref[k].astype(F32)
            g = g_l if g is None else jnp.where(layer == l, g_l, g)
            at += counts[l]
        _adamw_update(g, *refs[at:])

    def part_specs(l, recvs):
        rows = lambda q, i: jnp.where(q < l, 0, jnp.where(q > l, nb - 1, i))
        return ([pl.BlockSpec((1, tile, c), lambda q, i, place_ref: (place_ref[1], rows(q, i), 0))] +
                [pl.BlockSpec((a.shape[0], tile, c), lambda q, i, place_ref: (0, rows(q, i), 0)) for a in recvs])

    in_specs, args = [], []
    for l, (s, recvs) in enumerate(parts):
        in_specs += part_specs(l, recvs)
        args += [s, *recvs]
    blk = pl.BlockSpec((None, tile, c), lambda q, i, place_ref: (q, i, 0))
    out = jax.ShapeDtypeStruct((n_layers, r, c), F32)
    return _pcall(body, ride, grid=(n_layers, nb), in_specs=in_specs + [blk, blk, blk], out_specs=[blk, blk, blk, blk],
                  out_shape=[out, out, out, out], semantics=("arbitrary", "arbitrary"), name=name, prefetch=True)(
                      place, *args, w, m, v)


def _adamw_small(repl_parts, vec_parts, w, m, v, place):
    n_r, n = len(repl_parts), len(w)
    shapes = [a.shape for a in w]

    def body(place_ref, *refs):
        parts, rest = refs[:n], refs[n:]
        for k in range(n):
            take = (lambda p: parts[k][p]) if k < n_r else (lambda p: parts[k][p, 0])
            g = take(0)
            for p in range(1, N_DEV):
                g = g + take(p)
            _adamw_update(g, rest[k], rest[n + k], rest[2 * n + k], *rest[3 * n + 4 * k:3 * n + 4 * k + 4])

    def whole(shape):
        return pl.BlockSpec(shape, lambda i, place_ref: (0,) * len(shape))

    def mine(shard):
        return pl.BlockSpec((N_DEV, 1) + shard, lambda i, place_ref: (0, place_ref[2]) + (0,) * len(shard))

    in_specs = [whole(a.shape) for a in repl_parts] + [mine(s) for s in shapes[n_r:]] + [whole(s) for s in shapes] * 3
    outs = pl.pallas_call(
        body,
        grid_spec=pltpu.PrefetchScalarGridSpec(
            num_scalar_prefetch=1, grid=(1,), in_specs=in_specs,
            out_specs=[whole(s) for s in shapes for _ in range(4)]),
        out_shape=[jax.ShapeDtypeStruct(s, F32) for s in shapes for _ in range(4)],
        compiler_params=_params("arbitrary"), name="adamw_small")(place, *repl_parts, *vec_parts, *w, *m, *v)
    return [outs[4 * k:4 * k + 4] for k in range(n)]


SHARDED = {"norm_w": 2, "w_in_even": 2, "gla_w_a_up": 2, "w_out_even": 1, "w_in_odd": 2, "conv_w": 2, "conv_b": 1,
           "lru_b_a": 1, "lru_b_x": 1, "lru_lambda": 1, "w_out_odd": 1, "w_mlp_up": 2, "w_mlp_down": 1}
REPLICATED = ["gla_b_a", "gla_norm_w", "fox_b_f", "rel_bias", "lru_w_a", "lru_w_x"]
WEIGHTS = ["norm_w", "w_in_even", "gla_w_a_up", "gla_b_a", "gla_norm_w", "fox_b_f", "w_out_even", "w_in_odd",
           "rel_bias", "conv_w", "conv_b", "lru_w_a", "lru_b_a", "lru_w_x", "lru_b_x", "lru_lambda", "w_out_odd",
           "w_mlp_up", "w_mlp_down"]
MATRICES = ("w_in_even", "w_out_even", "w_in_odd", "w_out_odd", "w_mlp_up", "w_mlp_down")
TRANSPOSED = ("w_in_even", "w_in_odd")
VECTORS = tuple(n for n in SHARDED if n not in MATRICES)
MATRIX_BLOCKS = (("w_in_even", 0), ("w_out_even", 0), ("w_in_odd", 0), ("w_out_odd", 0),
                 ("w_mlp_up", 0), ("w_mlp_up", 1), ("w_mlp_down", 0), ("w_mlp_down", 1))


def _join_shards(blocks, axis):
    moved = jnp.moveaxis(blocks, 0, axis)
    shape = moved.shape
    return moved.reshape(shape[:axis] + (shape[axis] * shape[axis + 1],) + shape[axis + 2:])


def _split_shards(full, axis):
    shape = full.shape
    cut = full.reshape(shape[:axis] + (N_DEV, shape[axis] // N_DEV) + shape[axis + 1:])
    return jnp.moveaxis(cut, axis, 0)


EVEN_SPLITS = (0, 256, 512, 1024, 1536, 1552, 2064, 2576, 3088, 3096)


def _even_in_split(wt):
    c = [wt[EVEN_SPLITS[k]:EVEN_SPLITS[k + 1]] for k in range(9)]
    gq, gk, gv, gr, ga, fq, fk, fv, ff = c
    padrows = lambda a: jnp.pad(a, ((0, LANES - a.shape[0]), (0, 0)))
    return jnp.concatenate([gq, gk, gv, fq, fk, fv], axis=0), jnp.concatenate([gr, padrows(ga), padrows(ff)], axis=0)


def _even_in_merge(dmm, dele):
    return jnp.concatenate([dmm[:1024], dele[:512], dele[512:512 + GLA_RANK], dmm[1024:2560],
                            dele[640:640 + ATT_HEADS]], axis=0)


def _forward_backward(x, target, shard, vec_shard, w, place):
    w = dict(w)
    g, dnorm, sums, recv = {}, {}, {}, {}
    nrm = lambda l, k: w["norm_w"][l, k][None, :]
    gather = lambda *keys: _gather_plan([shard[k] for k in keys])
    blocks = lambda r, c: (N_DEV, r // N_DEV, c)

    def pair_sum(key):
        sums[key] = _pair_sum(g[key], got[key], place, f"rs_pair_sum_{key[0]}_{key[1]}")

    got = {}

    def mlp_fwd(xin, layer, ride_up, ride_down):
        up = _mm(xin, w["w_mlp_up"][layer], out_dtype=ACT_DTYPE, tm=TM_FWD, tn=D_FF // N_DEV, b_blocked=True,
                 a_norm=nrm(layer, 2), name=f"mlp_up_{layer}", ride=ride_up)
        (u, h), rode_up = up if ride_up is not None else (up, None)
        down = _mm(u, w["w_mlp_down"][layer], out_dtype=F32, tm=TM_DX // 2, tn=D_MODEL, a_sqrelu=True,
                   res_norm=(xin, nrm(layer, 3)), name=f"mlp_down_{layer}", ride=ride_down)
        (yv, xout), rode_down = down if ride_down is not None else (down, None)
        return xout, (xin, h, u, yv), rode_up, rode_down

    def mlp_bwd(dxout, saved, layer, ride):
        xin, h, u, yv = saved
        k_up, k_down = ("w_mlp_up", layer), ("w_mlp_down", layer)
        res = _mm(dxout, w["w_mlp_down"][layer], nt=True, out_dtype=ACT_DTYPE, tm=TM_DX, tn=TN, drelu_of=u,
                  a_norm_bwd=(yv, nrm(layer, 3)), name=f"mlp_down_dx_{layer}", ride=ride)
        (du, dy, dnorm[(layer, 3)]), rode = res if ride is not None else (res, None)
        g[k_down] = _mm(u, dy, ta=True, out_dtype=WIRE_DTYPE, tm=TM_DW, tn=TN, a_sqrelu=True,
                        name=f"mlp_down_dw_{layer}").reshape(blocks(D_FF, D_MODEL))
        g[k_up] = _mm(h, du, ta=True, out_dtype=WIRE_DTYPE, tm=TM_DW, tn=D_FF // N_DEV, out_blocked=True,
                      name=f"mlp_up_dw_{layer}")
        w_up = jnp.moveaxis(w["w_mlp_up"][layer], 0, 1).reshape(D_MODEL, D_FF)
        (dxin, dnorm[(layer, 2)]), (got[k_down], got[k_up]) = _mm(
            du, w_up, nt=True, out_dtype=F32, tm=TM_DX // 2, tn=D_MODEL, norm_bwd=(xin, nrm(layer, 2), dxout),
            name=f"mlp_up_dx_{layer}", ride=_sibling_plan([g[k_down], g[k_up]]))
        pair_sum(k_down)
        pair_sum(k_up)
        return dxin, rode

    first = _run_plan(_gather_plan([shard[("w_in_even", 0)]] + [vec_shard[n] for n in VECTORS]),
                      "weights_all_gather_first")
    w["w_in_even"] = first[0].reshape(-1, D_MODEL)
    for n, b in zip(VECTORS, first[1:]):
        w[n] = _join_shards(b, SHARDED[n])
    w["w_mlp_up"], w["w_mlp_down"] = [None] * DEPTH, [None] * DEPTH

    wmm_e, wel_e = _even_in_split(w["w_in_even"])
    w_up_pad = jnp.pad(w["gla_w_a_up"][0], ((0, LANES - GLA_RANK), (0, 0)))
    b_f_pad = jnp.pad(w["fox_b_f"], ((0, 0), (0, LANES - ATT_HEADS)))
    (pmm0, h0), (w_out_even,) = _mm(x, wmm_e, nt=True, out_dtype=ACT_DTYPE, tm=TM_FWD, tn=TN, a_norm=nrm(0, 0),
                                    name="in_even_mm", ride=gather(("w_out_even", 0)))
    pel0 = _mm(h0, wel_e, nt=True, out_dtype=F32, tm=TM_FWD, tn=768, name="in_even_el")
    (out_a, states), (w["w_mlp_up"][0],) = _gla_fwd(pmm0, pel0, w_up_pad, w["gla_b_a"], w["gla_norm_w"],
                                                    ride=gather(("w_mlp_up", 0)))
    cum, cum_t = _fox_gate_fwd(pel0, b_f_pad)
    (out_b, lse_b), (w_mlp_down0, w_in_odd) = _fox_fwd(pmm0, cum, cum_t,
                                                       ride=gather(("w_mlp_down", 0), ("w_in_odd", 0)))
    w["w_out_even"] = w_out_even.reshape(D_MODEL, D_MODEL)
    w["w_mlp_down"][0] = w_mlp_down0.reshape(D_FF, D_MODEL)
    mix_in0 = jnp.concatenate([out_a, out_b], axis=1)
    mix0, x1 = _mm(mix_in0, w["w_out_even"], out_dtype=F32, tm=TM_DX, tn=D_MODEL, res_norm=(x, nrm(0, 1)),
                   name="out_even")
    x2, mlp0, _, (w["w_mlp_up"][1],) = mlp_fwd(x1, 0, None, gather(("w_mlp_up", 1)))
    w["w_in_odd"] = w_in_odd.reshape(-1, D_MODEL)

    w_in_o = w["w_in_odd"]
    n_mm_o = 3 * GROUP_WIDTH
    wa_bd, wx_bd = _block_diag_pairs(w["lru_w_a"][0]), _block_diag_pairs(w["lru_w_x"][0])
    base = _ca_bias_base(w["rel_bias"][0])
    pmm1, h1 = _mm(x2, w_in_o[:n_mm_o], nt=True, out_dtype=ACT_DTYPE, tm=TM_FWD, tn=TN, a_norm=nrm(1, 0),
                   name="in_odd_mm")
    pel1 = _mm(h1, w_in_o[n_mm_o:], nt=True, out_dtype=F32, tm=TM_FWD, tn=TN, name="in_odd_el")
    kp = jnp.pad(pmm1[:, GROUP_WIDTH:2 * GROUP_WIDTH], ((CA_LEFT, 0), (0, 0)))
    vp = jnp.pad(pmm1[:, 2 * GROUP_WIDTH:], ((CA_LEFT, 0), (0, 0)))
    (out_c, lse_c), (w_mlp_down1,) = _ca_fwd(pmm1, kp, vp, base, ride=gather(("w_mlp_down", 1)))
    w["w_mlp_down"][1] = w_mlp_down1.reshape(D_FF, D_MODEL)
    lru_args = (pel1, w["conv_w"][0], w["conv_b"], wa_bd, w["lru_b_a"], wx_bd, w["lru_b_x"], w["lru_lambda"])
    out_d, (w_out_odd,) = _lru_fwd(*lru_args, ride=gather(("w_out_odd", 0)))
    w["w_out_odd"] = w_out_odd.reshape(D_MODEL, D_MODEL)
    mix_in1 = jnp.concatenate([out_c, out_d], axis=1)
    mix1, x3 = _mm(mix_in1, w["w_out_odd"], out_dtype=F32, tm=TM_DX, tn=D_MODEL, res_norm=(x2, nrm(1, 1)),
                   name="out_odd")
    x4, mlp1, _, _ = mlp_fwd(x3, 1, None, None)

    loss, dx4 = _loss_fwd_bwd(x4, target)

    k_oo, k_io, k_oe, k_ie = ("w_out_odd", 0), ("w_in_odd", 0), ("w_out_even", 0), ("w_in_even", 0)
    mlp_keys = lambda l: [("w_mlp_down", l), ("w_mlp_up", l)]
    dx3, _ = mlp_bwd(dx4, mlp1, 1, None)
    dmix_in1, dmix1, dnorm[(1, 1)] = _mm(dx3, w["w_out_odd"], nt=True, out_dtype=F32, tm=TM_DX, tn=TN,
                                         a_norm_bwd=(mix1, nrm(1, 1)), name="out_odd_dx")
    g[k_oo] = _mm(mix_in1, dmix1, ta=True, out_dtype=WIRE_DTYPE, tm=TM_DW, tn=TN, name="out_odd_dw").reshape(
        blocks(D_MODEL, D_MODEL))
    (dq_c, dkp, dvp, dbase), rode = _ca_bwd(
        pmm1, kp, vp, base, lse_c, dmix_in1,
        ride=_join_plans(_chip_plan([sums[k] for k in mlp_keys(1)]), _sibling_plan([g[k_oo]])))
    recv.update(zip(mlp_keys(1), rode[:2]))
    got[k_oo] = rode[2]
    pair_sum(k_oo)
    (dgate, dxin, g_conv_w, g_conv_b, dwa_bd, g_lru_b_a, dwx_bd, g_lru_b_x, g_lru_lambda), (recv[k_oo],) = _lru_bwd(
        *lru_args, dmix_in1, ride=_chip_plan([sums[k_oo]]))
    dp1 = jnp.concatenate([dq_c, dkp[CA_LEFT:].astype(ACT_DTYPE), dvp[CA_LEFT:].astype(ACT_DTYPE), dgate, dxin], axis=1)
    g[k_io] = _mm(dp1, h1, ta=True, out_dtype=WIRE_DTYPE, tm=dp1.shape[1] // 2, tn=TN, name="in_odd_dw").reshape(
        blocks(dp1.shape[1], D_MODEL))
    (dx2, dnorm[(1, 0)]), (got[k_io],) = _mm(dp1, w_in_o, out_dtype=F32, tm=TM_DX // 2, tn=D_MODEL,
                                             norm_bwd=(x2, nrm(1, 0), dx3), name="in_odd_dx",
                                             ride=_sibling_plan([g[k_io]]))
    pair_sum(k_io)
    g["rel_bias"] = _ca_bias_base_grad(dbase)[None]
    g["conv_w"], g["conv_b"] = g_conv_w[None], g_conv_b
    g["lru_w_a"], g["lru_w_x"] = _block_diag_pairs_grad(dwa_bd)[None], _block_diag_pairs_grad(dwx_bd)[None]
    g["lru_b_a"], g["lru_b_x"], g["lru_lambda"] = g_lru_b_a, g_lru_b_x, g_lru_lambda

    dx1, (recv[k_io],) = mlp_bwd(dx2, mlp0, 0, _chip_plan([sums[k_io]]))
    dmix_in0, dmix0, dnorm[(0, 1)] = _mm(dx1, w["w_out_even"], nt=True, out_dtype=F32, tm=TM_DX, tn=TN,
                                         a_norm_bwd=(mix0, nrm(0, 1)), name="out_even_dx")
    g[k_oe] = _mm(mix_in0, dmix0, ta=True, out_dtype=WIRE_DTYPE, tm=TM_DW, tn=TN, name="out_even_dw").reshape(
        blocks(D_MODEL, D_MODEL))
    k_md0, k_mu0 = mlp_keys(0)
    (dq_a, dk_a, dv_a, dr_a, da_a, dw_up_pad, g_gla_b_a, g_gla_norm_w), (got[k_oe], recv[k_md0]) = _gla_bwd(
        pmm0, pel0, w_up_pad, w["gla_b_a"], w["gla_norm_w"], states, dmix_in0,
        ride=_join_plans(_sibling_plan([g[k_oe]]), _chip_plan([sums[k_md0]])))
    pair_sum(k_oe)
    (dq_b, dk_b, dv_b, dcum_t, dcum_q), (recv[k_mu0], recv[k_oe]) = _fox_bwd(
        pmm0, cum, cum_t, lse_b, dmix_in0, ride=_chip_plan([sums[k_mu0], sums[k_oe]]))
    df_b, db_f = _fox_gate_bwd(pel0, b_f_pad, dcum_t, dcum_q)
    g["gla_w_a_up"] = dw_up_pad[:GLA_RANK][None]
    g["gla_b_a"], g["gla_norm_w"], g["fox_b_f"] = g_gla_b_a, g_gla_norm_w, db_f[:, :ATT_HEADS]
    dp0 = jnp.concatenate([dq_a, dk_a, dv_a, dq_b, dk_b.astype(ACT_DTYPE), dv_b.astype(ACT_DTYPE), dr_a, da_a, df_b],
                          axis=1)
    w_perm = jnp.concatenate([wmm_e, wel_e], axis=0)
    n_mm_e = wmm_e.shape[0]
    dw_perm, repl_parts = _mm(dp0, h0, ta=True, out_dtype=WIRE_DTYPE, tm=dp0.shape[1] // 2, tn=TN, name="in_even_dw",
                              ride=_gather_plan([g[n] for n in REPLICATED]))
    dw_even = _even_in_merge(dw_perm[:n_mm_e], dw_perm[n_mm_e:])
    g[k_ie] = dw_even.reshape(blocks(dw_even.shape[0], D_MODEL))
    dh0, (got[k_ie],) = _mm(dp0, w_perm, out_dtype=F32, tm=TM_DX, tn=TN, name="in_even_dx",
                            ride=_sibling_plan([g[k_ie]]))
    pair_sum(k_ie)
    (dx0, dnorm[(0, 0)]), (recv[k_ie],) = _norm_bwd(dh0, x, nrm(0, 0), out_dtype=F32, add=dx1, name="norm_in_bwd_0",
                                                     ride=_chip_plan([sums[k_ie]]))

    g["norm_w"] = jnp.stack([jnp.concatenate([dnorm[(l, k)] for k in range(4)], axis=0) for l in range(DEPTH)])
    vec_parts = _run_plan(_gather_plan([_split_shards(g[n], SHARDED[n]) for n in VECTORS]), "vector_grads_all_gather")
    return loss, dx0, sums, recv, repl_parts, vec_parts


def kernel(x, norm_w, w_in_even, gla_w_a_up, gla_b_a, gla_norm_w, fox_b_f, w_out_even, w_in_odd, rel_bias, conv_w, conv_b, lru_w_a, lru_b_a, lru_w_x, lru_b_x, lru_lambda, w_out_odd, w_mlp_up, w_mlp_down, loss_target, m_norm_w, m_w_in_even, m_gla_w_a_up, m_gla_b_a, m_gla_norm_w, m_fox_b_f, m_w_out_even, m_w_in_odd, m_rel_bias, m_conv_w, m_conv_b, m_lru_w_a, m_lru_b_a, m_lru_w_x, m_lru_b_x, m_lru_lambda, m_w_out_odd, m_w_mlp_up, m_w_mlp_down, v_norm_w, v_w_in_even, v_gla_w_a_up, v_gla_b_a, v_gla_norm_w, v_fox_b_f, v_w_out_even, v_w_in_odd, v_rel_bias, v_conv_w, v_conv_b, v_lru_w_a, v_lru_b_a, v_lru_w_x, v_lru_b_x, v_lru_lambda, v_w_out_odd, v_w_mlp_up, v_w_mlp_down):
    wts = dict(zip(WEIGHTS, (norm_w, w_in_even, gla_w_a_up, gla_b_a, gla_norm_w, fox_b_f, w_out_even, w_in_odd, rel_bias,
                             conv_w, conv_b, lru_w_a, lru_b_a, lru_w_x, lru_b_x, lru_lambda, w_out_odd, w_mlp_up,
                             w_mlp_down)))
    mom = dict(zip(WEIGHTS, (m_norm_w, m_w_in_even, m_gla_w_a_up, m_gla_b_a, m_gla_norm_w, m_fox_b_f, m_w_out_even,
                             m_w_in_odd, m_rel_bias, m_conv_w, m_conv_b, m_lru_w_a, m_lru_b_a, m_lru_w_x, m_lru_b_x,
                             m_lru_lambda, m_w_out_odd, m_w_mlp_up, m_w_mlp_down)))
    var = dict(zip(WEIGHTS, (v_norm_w, v_w_in_even, v_gla_w_a_up, v_gla_b_a, v_gla_norm_w, v_fox_b_f, v_w_out_even,
                             v_w_in_odd, v_rel_bias, v_conv_w, v_conv_b, v_lru_w_a, v_lru_b_a, v_lru_w_x, v_lru_b_x,
                             v_lru_lambda, v_w_out_odd, v_w_mlp_up, v_w_mlp_down)))
    ax, ay, ac = lax.axis_index("x"), lax.axis_index("y"), lax.axis_index("c")
    place = jnp.stack([ac, 2 * ax + ay, 4 * ax + 2 * ay + ac]).astype(jnp.int32)

    shard = {(n, l): (wts[n][l].T if n in TRANSPOSED else wts[n][l]).astype(WIRE_DTYPE) for n, l in MATRIX_BLOCKS}
    loss_blk, dx, sums, recv, repl_parts, vec_parts = _forward_backward(
        x[0], loss_target[0], shard, {n: wts[n] for n in VECTORS}, {n: wts[n] for n in REPLICATED}, place)
    loss = lax.psum(loss_blk[0, 0], ("x", "y", "c"))

    view = lambda n, a: jnp.swapaxes(a, 1, 2) if n in TRANSPOSED else a
    upd = {n: [view(n, o) for o in _adamw_sharded(
        [(sums[(n, l)], [recv[(n, l)]]) for l in range(wts[n].shape[0])], view(n, wts[n]), view(n, mom[n]),
        view(n, var[n]), place, f"adamw_{n}")] for n in MATRICES}
    small = REPLICATED + list(VECTORS)
    upd.update(zip(small, _adamw_small(repl_parts, vec_parts, [wts[n] for n in small], [mom[n] for n in small],
                                       [var[n] for n in small], place)))
    return (loss, dx[None], *[upd[n][kind] for kind in range(4) for n in WEIGHTS])
```

```python
import functools
from typing import Callable, NamedTuple, Optional

import jax
import jax.numpy as jnp
from jax import lax
from jax.experimental import pallas as pl
from jax.experimental.pallas import tpu as pltpu

F32 = jnp.float32
MXU_DTYPE = jnp.bfloat16
ACT_DTYPE = jnp.bfloat16
WIRE_DTYPE = jnp.bfloat16

V7X_VMEM_BYTES = 64 * 1024 * 1024
VMEM_LIMIT = (V7X_VMEM_BYTES * 7) // 8
LANES = 128

D_MODEL = 1024
DEPTH = 2
CHUNK = 64
GROUP_WIDTH = D_MODEL // 2
D_FF = 4 * D_MODEL
NORM_EPS = 1e-6
GLA_HEADS = 4
GLA_DV = GROUP_WIDTH // GLA_HEADS
GLA_DK = GLA_DV // 2
GLA_KW = GLA_HEADS * GLA_DK
GLA_RANK = 16
GLA_GATE_TAU = 16.0
HEAD_DIM = 64
ATT_HEADS = GROUP_WIDTH // HEAD_DIM
CA_LEFT = 8 * CHUNK
REL_CLIP = 128
LRU_BLOCK_DIM = 64
CONV_WIDTH = 4
LRU_C = 8.0
N_DEV = 8

ADAM_LR = 0.001
ADAM_B1 = 0.9
ADAM_B2 = 0.999
ADAM_EPS = 1e-08
ADAM_WD = 0.01
ADAM_STEP = 10

NEG = float(jnp.finfo(jnp.float32).min)
MESH = pl.DeviceIdType.MESH


def _params(*sem):
    return pltpu.CompilerParams(dimension_semantics=sem, vmem_limit_bytes=VMEM_LIMIT)


def _dot(a, b, ca=1, cb=0):
    return lax.dot_general(a.astype(MXU_DTYPE), b.astype(MXU_DTYPE), (((ca,), (cb,)), ((), ())),
                           preferred_element_type=F32)


def _dot_exact(a, b):
    return lax.dot_general(a, b, (((1,), (0,)), ((), ())), precision=lax.Precision.HIGHEST,
                           preferred_element_type=F32)


def _log_sigmoid(x):
    return jnp.minimum(x, 0.0) - jnp.log1p(jnp.exp(-jnp.abs(x)))


def _iota(shape, axis):
    return lax.broadcasted_iota(jnp.int32, shape, axis)


ANY = pl.BlockSpec(memory_space=pl.ANY)
N_CHIPS = 4


class _Plan(NamedTuple):
    ins: list
    outs: list
    sems: list
    start: Callable
    finish: Callable
    relay: Optional[Callable] = None


def _place():
    x, y, c = lax.axis_index("x"), lax.axis_index("y"), lax.axis_index("c")
    return x, y, c, [(1 - x, y), (x, 1 - y), (1 - x, 1 - y)]


def _gather_plan(xs):
    n = len(xs)

    def parts(x_refs, out_refs, sems):
        send_sems, recv_sems, local_sems = sems
        x, y, c, chips = _place()
        me, sibling = (x, y, c), (x, y, 1 - c)

        def rows(a, px, py, pc):
            return out_refs[a].at[4 * px + 2 * py + pc]

        def copy(a, k, block, to, src=None):
            return pltpu.make_async_remote_copy(
                src_ref=rows(a, *block) if src is None else src, dst_ref=rows(a, *block),
                send_sem=send_sems.at[7 * a + k], recv_sem=recv_sems.at[7 * a + k], device_id=to, device_id_type=MESH)

        def own():
            mine = [pltpu.make_async_copy(x_refs[a], rows(a, *me), local_sems.at[a]) for a in range(n)]
            first = []
            for a in range(n):
                first.append(copy(a, 0, me, sibling, src=x_refs[a]))
                first += [copy(a, 1 + j, me, (*chip, c), src=x_refs[a]) for j, chip in enumerate(chips)]
            return mine, first

        return c, me, sibling, chips, copy, own

    def start(x_refs, out_refs, sems):
        mine, first = parts(x_refs, out_refs, sems)[-1]()
        for cp in first + mine:
            cp.start()

    def relay(x_refs, out_refs, sems):
        c, me, sibling, chips, copy, _ = parts(x_refs, out_refs, sems)
        for j, chip in enumerate(chips):
            for a in range(n):
                copy(a, 1 + j, (*chip, c), me).wait_recv()
                copy(a, 4 + j, (*chip, c), sibling).start()

    def finish(x_refs, out_refs, sems):
        c, me, sibling, chips, copy, own = parts(x_refs, out_refs, sems)
        mine, first = own()
        for a in range(n):
            copy(a, 0, sibling, me).wait_recv()
            for j, chip in enumerate(chips):
                copy(a, 4 + j, (*chip, 1 - c), me).wait_recv()
        for cp in first + [copy(a, 4 + j, (*chip, c), sibling) for j, chip in enumerate(chips) for a in range(n)]:
            cp.wait_send()
        for cp in mine:
            cp.wait()

    return _Plan(list(xs), [jax.ShapeDtypeStruct((N_DEV,) + x.shape, x.dtype) for x in xs],
                 [pltpu.SemaphoreType.DMA((7 * n,)), pltpu.SemaphoreType.DMA((7 * n,)), pltpu.SemaphoreType.DMA((n,))],
                 start, finish, relay)


def _exchange_plan(copies_of, ins, outs, per_array):
    n = len(ins)

    def start(in_refs, out_refs, sems):
        for cp in copies_of(in_refs, out_refs, sems):
            cp.start()

    def finish(in_refs, out_refs, sems):
        copies = copies_of(in_refs, out_refs, sems)
        for cp in copies:
            cp.wait_recv()
        for cp in copies:
            cp.wait_send()

    return _Plan(list(ins), outs, [pltpu.SemaphoreType.DMA((per_array * n,)), pltpu.SemaphoreType.DMA((per_array * n,))],
                 start, finish)


def _sibling_plan(gs):
    def copies_of(g_refs, got_refs, sems):
        x, y, c, _ = _place()
        return [pltpu.make_async_remote_copy(
            src_ref=g_refs[a].at[2 * k + (1 - c)], dst_ref=got_refs[a].at[k], send_sem=sems[0].at[N_CHIPS * a + k],
            recv_sem=sems[1].at[N_CHIPS * a + k], device_id=(x, y, 1 - c), device_id_type=MESH)
            for a in range(len(gs)) for k in range(N_CHIPS)]

    return _exchange_plan(copies_of, gs, [jax.ShapeDtypeStruct((N_CHIPS,) + g.shape[1:], g.dtype) for g in gs], N_CHIPS)


def _chip_plan(ss, relations=(0, 1, 2)):
    n_rel = len(relations)

    def copies_of(s_refs, out_refs, sems):
        x, y, c, chips = _place()
        return [pltpu.make_async_remote_copy(
            src_ref=s_refs[a].at[2 * chips[j][0] + chips[j][1]], dst_ref=out_refs[a].at[slot],
            send_sem=sems[0].at[n_rel * a + slot], recv_sem=sems[1].at[n_rel * a + slot],
            device_id=(*chips[j], c), device_id_type=MESH)
            for a in range(len(ss)) for slot, j in enumerate(relations)]

    return _exchange_plan(copies_of, ss, [jax.ShapeDtypeStruct((n_rel,) + s.shape[1:], s.dtype) for s in ss], n_rel)


def _join_plans(*plans):
    def cut(refs, counts):
        at = 0
        for n in counts:
            yield refs[at:at + n]
            at += n

    def each(in_refs, out_refs, sems):
        return zip(plans, cut(in_refs, [len(p.ins) for p in plans]), cut(out_refs, [len(p.outs) for p in plans]),
                   cut(sems, [len(p.sems) for p in plans]))

    def start(*refs):
        for p, i, o, s in each(*refs):
            p.start(i, o, s)

    def relay(*refs):
        for p, i, o, s in each(*refs):
            if p.relay is not None:
                p.relay(i, o, s)

    def finish(*refs):
        for p, i, o, s in each(*refs):
            p.finish(i, o, s)

    return _Plan([a for p in plans for a in p.ins], [a for p in plans for a in p.outs],
                 [a for p in plans for a in p.sems], start, finish, relay)


def _run_plan(plan, name):
    n_in, n_out = len(plan.ins), len(plan.outs)

    def body(*refs):
        args = refs[:n_in], refs[n_in:n_in + n_out], refs[n_in + n_out:]
        plan.start(*args)
        if plan.relay is not None:
            plan.relay(*args)
        plan.finish(*args)

    return pl.pallas_call(body, out_shape=plan.outs, in_specs=[ANY] * n_in, out_specs=[ANY] * n_out,
                          scratch_shapes=plan.sems, name=name)(*plan.ins)


def _pcall(body, ride, *, grid, in_specs, out_specs, out_shape, scratch_shapes=(), semantics, name, prefetch=False):
    n_pre = int(prefetch)

    def build(kernel, ins, outs, shapes, scratch, sem):
        if prefetch:
            return pl.pallas_call(
                kernel, grid_spec=pltpu.PrefetchScalarGridSpec(num_scalar_prefetch=1, grid=grid, in_specs=ins,
                                                               out_specs=outs, scratch_shapes=scratch),
                out_shape=shapes, compiler_params=_params(*sem), name=name)
        return pl.pallas_call(kernel, grid=grid, in_specs=ins, out_specs=outs, out_shape=shapes,
                              scratch_shapes=scratch, compiler_params=_params(*sem), name=name)

    if ride is None:
        return build(body, in_specs, out_specs, out_shape, list(scratch_shapes), semantics)
    single = not isinstance(out_shape, (list, tuple))
    out_specs_l, out_shape_l = ([out_specs], [out_shape]) if single else (list(out_specs), list(out_shape))
    n_in, n_out, n_scr = len(in_specs), len(out_shape_l), len(scratch_shapes)
    r_in, r_out = len(ride.ins), len(ride.outs)

    def riding(*refs):
        pre, refs = refs[:n_pre], refs[n_pre:]
        cuts = [n_in, r_in, n_out, r_out, n_scr]
        groups, at = [], 0
        for width in cuts:
            groups.append(refs[at:at + width])
            at += width
        ins, r_ins, outs, r_outs, scr = groups
        sems = refs[at:]
        first = functools.reduce(jnp.logical_and, [pl.program_id(d) == 0 for d in range(len(grid))])
        last = functools.reduce(jnp.logical_and, [pl.program_id(d) == grid[d] - 1 for d in range(len(grid))])

        @pl.when(first)
        def _():
            ride.start(r_ins, r_outs, sems)

        several_steps = any(n > 1 for n in grid)
        if ride.relay is not None and several_steps:
            @pl.when(last)
            def _():
                ride.relay(r_ins, r_outs, sems)

        body(*pre, *ins, *outs, *scr)

        @pl.when(last)
        def _():
            if ride.relay is not None and not several_steps:
                ride.relay(r_ins, r_outs, sems)
            ride.finish(r_ins, r_outs, sems)

    call = build(riding, list(in_specs) + [ANY] * r_in, out_specs_l + [ANY] * r_out, out_shape_l + list(ride.outs),
                 list(scratch_shapes) + list(ride.sems), ["arbitrary"] * len(grid))

    def run(*args):
        res = call(*args, *ride.ins)
        return (res[0] if single else list(res[:n_out])), list(res[n_out:])

    return run


def _rms(x):
    return x * lax.rsqrt(jnp.mean(x * x, axis=-1, keepdims=True) + NORM_EPS)


def _mm(a, b, *, nt=False, ta=False, out_dtype, tm, tn, a_sqrelu=False, drelu_of=None, b_blocked=False,
        out_blocked=False, a_norm=None, a_norm_bwd=None, res_norm=None, norm_bwd=None, name, ride=None):
    k, m = a.shape if ta else a.shape[::-1]
    if b_blocked:
        assert not nt and b.shape[1] == k and b.shape[2] == tn
        n = b.shape[0] * tn
    else:
        n = b.shape[0] if nt else b.shape[1]
        assert (b.shape[1] if nt else b.shape[0]) == k
    tm, tn = min(tm, m), min(tn, n)
    assert m % tm == 0 and n % tn == 0
    assert (res_norm is None and norm_bwd is None) or tn == n
    assert a_norm is None or a_norm_bwd is None
    n_in = (2 + (drelu_of is not None) + (a_norm is not None) + 2 * (a_norm_bwd is not None)
            + 2 * (res_norm is not None) + 3 * (norm_bwd is not None))

    def body(*refs):
        a_ref, b_ref = refs[0], refs[1]
        extra = list(refs[2:n_in])
        outs = list(refs[n_in:])
        o_ref = outs.pop(0)
        u_ref = extra.pop(0) if drelu_of is not None else None
        if a_norm is not None:
            wn_ref, h_ref, h_scr = extra.pop(0), outs.pop(0), outs.pop()

            @pl.when(pl.program_id(1) == 0)
            def _():
                h = (_rms(a_ref[...]) * wn_ref[...]).astype(ACT_DTYPE)
                h_scr[...] = h
                h_ref[...] = h

            av = h_scr[...]
        elif a_norm_bwd is not None:
            y_ref, wy_ref = extra.pop(0), extra.pop(0)
            dy_ref, dwy_ref, dy_scr = outs.pop(0), outs.pop(0), outs.pop()
            first_rows = pl.program_id(0) == 0

            @pl.when(pl.program_id(1) == 0)
            def _():
                yv, up = y_ref[...], a_ref[...]
                rstd = lax.rsqrt(jnp.mean(yv * yv, axis=-1, keepdims=True) + NORM_EPS)
                yhat = yv * rstd
                g = up * wy_ref[...]
                dy = (rstd * (g - yhat * jnp.mean(g * yhat, axis=-1, keepdims=True))).astype(ACT_DTYPE)
                dy_scr[...] = dy
                dy_ref[...] = dy

                @pl.when(first_rows)
                def _():
                    dwy_ref[...] = jnp.zeros_like(dwy_ref)

                dwy_ref[...] += jnp.sum(up * yhat, axis=0, keepdims=True)

            av = dy_scr[...]
        else:
            av = a_ref[...]
        if a_sqrelu:
            av = jnp.square(jnp.maximum(av.astype(F32), 0.0))
        acc = _dot(av, b_ref[...], 0 if ta else 1, 1 if nt else 0)
        if u_ref is not None:
            acc = acc * (2.0 * jnp.maximum(u_ref[...].astype(F32), 0.0))
        if norm_bwd is not None:
            x_ref, wb_ref, add_ref = extra
            dw_ref = outs[0]
            xv = x_ref[...]
            rstd = lax.rsqrt(jnp.mean(xv * xv, axis=-1, keepdims=True) + NORM_EPS)
            xhat = xv * rstd
            g = acc * wb_ref[...]
            o_ref[...] = rstd * (g - xhat * jnp.mean(g * xhat, axis=-1, keepdims=True)) + add_ref[...]

            @pl.when(pl.program_id(0) == 0)
            def _():
                dw_ref[...] = jnp.zeros_like(dw_ref)

            dw_ref[...] += jnp.sum(acc * xhat, axis=0, keepdims=True)
            return
        o_ref[...] = acc.astype(out_dtype)
        if res_norm is not None:
            res_ref, wr_ref = extra
            outs[0][...] = res_ref[...] + _rms(acc) * wr_ref[...]

    if b_blocked:
        b_spec = pl.BlockSpec((None, k, tn), lambda i, j: (j, 0, 0))
    elif nt:
        b_spec = pl.BlockSpec((tn, k), lambda i, j: (j, 0))
    else:
        b_spec = pl.BlockSpec((k, tn), lambda i, j: (0, j))
    a_spec = pl.BlockSpec((k, tm), lambda i, j: (0, i)) if ta else pl.BlockSpec((tm, k), lambda i, j: (i, 0))
    in_specs = [a_spec, b_spec]
    args = [a, b]
    if drelu_of is not None:
        in_specs.append(pl.BlockSpec((tm, tn), lambda i, j: (i, j)))
        args.append(drelu_of)
    if out_blocked:
        out_specs = [pl.BlockSpec((None, tm, tn), lambda i, j: (j, i, 0))]
        out_shape = [jax.ShapeDtypeStruct((n // tn, m, tn), out_dtype)]
    else:
        out_specs = [pl.BlockSpec((tm, tn), lambda i, j: (i, j))]
        out_shape = [jax.ShapeDtypeStruct((m, n), out_dtype)]
    scratch = []
    if a_norm is not None:
        assert not ta
        in_specs.append(pl.BlockSpec((1, k), lambda i, j: (0, 0)))
        args.append(a_norm)
        out_specs.append(pl.BlockSpec((tm, k), lambda i, j: (i, 0)))
        out_shape.append(jax.ShapeDtypeStruct((m, k), ACT_DTYPE))
        scratch.append(pltpu.VMEM((tm, k), ACT_DTYPE))
    if a_norm_bwd is not None:
        assert not ta
        in_specs += [pl.BlockSpec((tm, k), lambda i, j: (i, 0)), pl.BlockSpec((1, k), lambda i, j: (0, 0))]
        args += list(a_norm_bwd)
        out_specs += [pl.BlockSpec((tm, k), lambda i, j: (i, 0)), pl.BlockSpec((1, k), lambda i, j: (0, 0))]
        out_shape += [jax.ShapeDtypeStruct((m, k), ACT_DTYPE), jax.ShapeDtypeStruct((1, k), F32)]
        scratch.append(pltpu.VMEM((tm, k), ACT_DTYPE))
    if res_norm is not None:
        in_specs += [pl.BlockSpec((tm, n), lambda i, j: (i, 0)), pl.BlockSpec((1, n), lambda i, j: (0, 0))]
        args += list(res_norm)
        out_specs.append(pl.BlockSpec((tm, n), lambda i, j: (i, 0)))
        out_shape.append(jax.ShapeDtypeStruct((m, n), F32))
    if norm_bwd is not None:
        rows = pl.BlockSpec((tm, n), lambda i, j: (i, 0))
        in_specs += [rows, pl.BlockSpec((1, n), lambda i, j: (0, 0)), rows]
        args += list(norm_bwd)
        out_specs.append(pl.BlockSpec((1, n), lambda i, j: (0, 0)))
        out_shape.append(jax.ShapeDtypeStruct((1, n), F32))
    single = len(out_shape) == 1
    return _pcall(body, ride, grid=(m // tm, n // tn), in_specs=in_specs,
                  out_specs=out_specs[0] if single else out_specs, out_shape=out_shape[0] if single else out_shape,
                  scratch_shapes=scratch, semantics=("arbitrary", "arbitrary"), name=name)(*args)


ROW_TILE = 512
TM_FWD, TM_DX, TM_DW, TN = 2048, 1024, 1024, 512


def _norm_bwd(dy, x, w, *, out_dtype, add=None, name, ride=None):
    t, d = x.shape

    def body(*refs):
        dy_ref, x_ref, w_ref = refs[0], refs[1], refs[2]
        dx_ref, dw_ref = refs[-2], refs[-1]
        xv = x_ref[...]
        rstd = lax.rsqrt(jnp.mean(xv * xv, axis=-1, keepdims=True) + NORM_EPS)
        xhat = xv * rstd
        dyv = dy_ref[...].astype(F32)
        g = dyv * w_ref[...]
        dx = rstd * (g - xhat * jnp.mean(g * xhat, axis=-1, keepdims=True))
        if add is not None:
            dx = dx + refs[3][...]
        dx_ref[...] = dx.astype(out_dtype)

        @pl.when(pl.program_id(0) == 0)
        def _():
            dw_ref[...] = jnp.zeros_like(dw_ref)

        dw_ref[...] += jnp.sum(dyv * xhat, axis=0, keepdims=True)

    row = pl.BlockSpec((ROW_TILE, d), lambda i: (i, 0))
    vec = pl.BlockSpec((1, d), lambda i: (0, 0))
    in_specs = [row, row, vec] + ([row] if add is not None else [])
    args = [dy, x, w] + ([add] if add is not None else [])
    return _pcall(body, ride, grid=(t // ROW_TILE,), in_specs=in_specs, out_specs=[row, vec],
                  out_shape=[jax.ShapeDtypeStruct((t, d), out_dtype), jax.ShapeDtypeStruct((1, d), F32)],
                  semantics=("arbitrary",), name=name)(*args)


def _loss_fwd_bwd(y, target):
    t, d = y.shape

    def body(y_ref, t_ref, l_ref, dy_ref):
        diff = y_ref[...] - t_ref[...]
        dy_ref[...] = diff * (1.0 / d)

        @pl.when(pl.program_id(0) == 0)
        def _():
            l_ref[...] = jnp.zeros_like(l_ref)

        l_ref[...] += 0.5 * jnp.sum(jnp.mean(diff * diff, axis=-1, keepdims=True), axis=0, keepdims=True)

    row = pl.BlockSpec((ROW_TILE, d), lambda i: (i, 0))
    return pl.pallas_call(body, grid=(t // ROW_TILE,), in_specs=[row, row],
                          out_specs=[pl.BlockSpec((8, LANES), lambda i: (0, 0)), row],
                          out_shape=[jax.ShapeDtypeStruct((8, LANES), F32), jax.ShapeDtypeStruct((t, d), F32)],
                          compiler_params=_params("arbitrary"), name="loss")(y, target)


GLA_STATE = (GLA_HEADS * GLA_DV, GLA_KW)


def _gla_specs(chunk_of):
    rows = lambda width, col: pl.BlockSpec((CHUNK, width), lambda i: (chunk_of(i), col))
    const = lambda r, c: pl.BlockSpec((r, c), lambda i: (0, 0))
    return [rows(GLA_KW, 0),
            rows(GLA_KW, 1),
            rows(GROUP_WIDTH, 1),
            rows(GROUP_WIDTH, 0),
            rows(LANES, 4),
            const(LANES, GLA_KW),
            const(1, GLA_KW),
            const(1, GROUP_WIDTH)]


def _gla_chunk(q_ref, k_ref, v_ref, a_ref, wup_ref, ba_ref):
    z = _dot(a_ref[...], wup_ref[...]) + ba_ref[...]
    tri = (_iota((CHUNK, CHUNK), 1) <= _iota((CHUNK, CHUNK), 0)).astype(F32)
    cum = _dot_exact(tri, _log_sigmoid(z) * (1.0 / GLA_GATE_TAU))
    tot = cum[CHUNK - 1:CHUNK, :]
    e = jnp.exp(tot - cum)
    return (z, e, jnp.exp(tot), k_ref[...].astype(F32) * e, q_ref[...].astype(F32) * (GLA_DK ** -0.5),
            v_ref[...].astype(F32))


def _gla_head_mask():
    return _iota(GLA_STATE, 0) // GLA_DV == _iota(GLA_STATE, 1) // GLA_DK


def _gla_fwd(pmm, pel, w_up, b_a, gnorm_w, ride=None):
    t = pmm.shape[0]
    nc = t // CHUNK

    def body(q_ref, k_ref, v_ref, r_ref, a_ref, wup_ref, ba_ref, gw_ref, o_ref, st_ref, m_scr):
        @pl.when(pl.program_id(0) == 0)
        def _():
            m_scr[...] = jnp.zeros_like(m_scr)

        _, _, decay, kd, qs, vv = _gla_chunk(q_ref, k_ref, v_ref, a_ref, wup_ref, ba_ref)
        m = m_scr[...] * decay + jnp.where(_gla_head_mask(), _dot(vv, kd, 0, 0), 0.0)
        m_scr[...] = m
        st_ref[...] = m
        o = _dot(qs, m, 1, 1)
        rr = r_ref[...]
        gate = rr * jax.nn.sigmoid(rr) * gw_ref[...]
        for h in range(GLA_HEADS):
            vs = slice(h * GLA_DV, (h + 1) * GLA_DV)
            oh = o[:, vs]
            y = oh * lax.rsqrt(jnp.mean(oh * oh, axis=-1, keepdims=True) + NORM_EPS)
            o_ref[:, vs] = (y * gate[:, vs]).astype(o_ref.dtype)

    return _pcall(
        body, ride, grid=(nc,), in_specs=_gla_specs(lambda i: i),
        out_specs=[pl.BlockSpec((CHUNK, GROUP_WIDTH), lambda i: (i, 0)),
                   pl.BlockSpec((None,) + GLA_STATE, lambda i: (i, 0, 0))],
        out_shape=[jax.ShapeDtypeStruct((t, GROUP_WIDTH), ACT_DTYPE), jax.ShapeDtypeStruct((nc,) + GLA_STATE, F32)],
        scratch_shapes=[pltpu.VMEM(GLA_STATE, F32)],
        semantics=("arbitrary",), name="gla_fwd")(pmm, pmm, pmm, pel, pel, w_up, b_a, gnorm_w)


def _gla_bwd(pmm, pel, w_up, b_a, gnorm_w, states, dmix, ride=None):
    t = pmm.shape[0]
    nc = t // CHUNK
    scale = GLA_DK ** -0.5

    def body(q_ref, k_ref, v_ref, r_ref, a_ref, wup_ref, ba_ref, gw_ref, st_ref, prev_ref, do_ref,
             dq_ref, dk_ref, dv_ref, dr_ref, da_ref, dwup_ref, dba_ref, dgw_ref, dm_scr):
        step = pl.program_id(0)

        @pl.when(step == 0)
        def _():
            dm_scr[...] = jnp.zeros_like(dm_scr)
            dwup_ref[...] = jnp.zeros_like(dwup_ref)
            dba_ref[...] = jnp.zeros_like(dba_ref)
            dgw_ref[...] = jnp.zeros_like(dgw_ref)

        z, e, decay, kd, qs, vv = _gla_chunk(q_ref, k_ref, v_ref, a_ref, wup_ref, ba_ref)
        m = st_ref[...]
        m_prev = prev_ref[...] * (step < nc - 1).astype(F32)
        rr, dout, gw = r_ref[...], do_ref[...], gw_ref[...]
        sig = jax.nn.sigmoid(rr)
        silu = rr * sig
        dsilu = sig * (1.0 + rr * (1.0 - sig))
        o = _dot(qs, m, 1, 1)
        d_o, dgw = [], []
        for h in range(GLA_HEADS):
            vs = slice(h * GLA_DV, (h + 1) * GLA_DV)
            oh, dg = o[:, vs], dout[:, vs]
            rstd = lax.rsqrt(jnp.mean(oh * oh, axis=-1, keepdims=True) + NORM_EPS)
            y = oh * rstd
            dgw.append(jnp.sum(dg * y * silu[:, vs], axis=0, keepdims=True))
            dr_ref[:, vs] = (dg * y * gw[:, vs] * dsilu[:, vs]).astype(dr_ref.dtype)
            dy = dg * gw[:, vs] * silu[:, vs]
            d_o.append(rstd * (dy - y * jnp.mean(dy * y, axis=-1, keepdims=True)))
        d_o = jnp.concatenate(d_o, axis=1)
        dgw_ref[...] += jnp.concatenate(dgw, axis=1)
        dq_ref[...] = (_dot(d_o, m) * scale).astype(dq_ref.dtype)
        dm = dm_scr[...] + jnp.where(_gla_head_mask(), _dot(d_o, qs, 0, 0), 0.0)
        dv_ref[...] = _dot(kd, dm, 1, 1).astype(dv_ref.dtype)
        dkd = _dot(vv, dm)
        dk_ref[...] = (dkd * e).astype(dk_ref.dtype)
        dm_scr[...] = dm * decay
        tri_strict = (_iota((CHUNK, CHUNK), 1) < _iota((CHUNK, CHUNK), 0)).astype(F32)
        dla = jnp.sum(dm * m_prev, axis=0, keepdims=True) * decay + _dot_exact(tri_strict, dkd * kd)
        dz = dla * jax.nn.sigmoid(-z) * (1.0 / GLA_GATE_TAU)
        da_ref[...] = _dot(dz, wup_ref[...], 1, 1).astype(da_ref.dtype)
        dwup_ref[...] += _dot(a_ref[...], dz, 0, 0)
        dba_ref[...] += jnp.sum(dz, axis=0, keepdims=True)

    chunk_of = lambda i: nc - 1 - i
    in_specs = _gla_specs(chunk_of) + [
        pl.BlockSpec((None,) + GLA_STATE, lambda i: (chunk_of(i), 0, 0)),
        pl.BlockSpec((None,) + GLA_STATE, lambda i: (jnp.maximum(chunk_of(i) - 1, 0), 0, 0)),
        pl.BlockSpec((CHUNK, GROUP_WIDTH), lambda i: (chunk_of(i), 0))]
    rows = lambda width: pl.BlockSpec((CHUNK, width), lambda i: (chunk_of(i), 0))
    const = lambda r, c: pl.BlockSpec((r, c), lambda i: (0, 0))
    return _pcall(
        body, ride, grid=(nc,), in_specs=in_specs,
        out_specs=[rows(GLA_KW), rows(GLA_KW), rows(GROUP_WIDTH), rows(GROUP_WIDTH), rows(LANES),
                   const(LANES, GLA_KW), const(1, GLA_KW), const(1, GROUP_WIDTH)],
        out_shape=[jax.ShapeDtypeStruct((t, GLA_KW), ACT_DTYPE), jax.ShapeDtypeStruct((t, GLA_KW), ACT_DTYPE),
                   jax.ShapeDtypeStruct((t, GROUP_WIDTH), ACT_DTYPE), jax.ShapeDtypeStruct((t, GROUP_WIDTH), ACT_DTYPE),
                   jax.ShapeDtypeStruct((t, LANES), ACT_DTYPE), jax.ShapeDtypeStruct((LANES, GLA_KW), F32),
                   jax.ShapeDtypeStruct((1, GLA_KW), F32), jax.ShapeDtypeStruct((1, GROUP_WIDTH), F32)],
        scratch_shapes=[pltpu.VMEM(GLA_STATE, F32)],
        semantics=("arbitrary",), name="gla_bwd")(
            pmm, pmm, pmm, pel, pel, w_up, b_a, gnorm_w, states, states, dmix)


CUM_BLOCK = 256


def _fox_gate_fwd(pel, b_f):
    t = pel.shape[0]
    nb = t // CUM_BLOCK

    def body(f_ref, b_ref, cum_ref, cum_t_ref):
        tri = (_iota((CUM_BLOCK, CUM_BLOCK), 1) <= _iota((CUM_BLOCK, CUM_BLOCK), 0)).astype(F32)
        carry = jnp.zeros((1, LANES), F32)
        for blk in range(nb):
            rows = slice(blk * CUM_BLOCK, (blk + 1) * CUM_BLOCK)
            cum = _dot_exact(tri, _log_sigmoid(f_ref[rows, :] + b_ref[...])) + carry
            cum_ref[rows, :] = cum
            cum_t_ref[blk] = cum.T[:ATT_HEADS, :]
            carry = cum[CUM_BLOCK - 1:CUM_BLOCK, :]

    return pl.pallas_call(
        body, grid=(1,),
        in_specs=[pl.BlockSpec((t, LANES), lambda i: (0, 5)), pl.BlockSpec((1, LANES), lambda i: (0, 0))],
        out_specs=[pl.BlockSpec((t, LANES), lambda i: (0, 0)),
                   pl.BlockSpec((nb, ATT_HEADS, CUM_BLOCK), lambda i: (0, 0, 0))],
        out_shape=[jax.ShapeDtypeStruct((t, LANES), F32), jax.ShapeDtypeStruct((nb, ATT_HEADS, CUM_BLOCK), F32)],
        compiler_params=_params("arbitrary"), name="fox_gate_fwd")(pel, b_f)


def _fox_gate_bwd(pel, b_f, dcum_t, dcum_q):
    t = pel.shape[0]
    nb = t // CUM_BLOCK

    def body(f_ref, b_ref, dct_ref, dcq_ref, df_ref, db_ref):
        tri_up = (_iota((CUM_BLOCK, CUM_BLOCK), 1) >= _iota((CUM_BLOCK, CUM_BLOCK), 0)).astype(F32)
        carry = jnp.zeros((1, LANES), F32)
        db = jnp.zeros((1, LANES), F32)
        for blk in reversed(range(nb)):
            rows = slice(blk * CUM_BLOCK, (blk + 1) * CUM_BLOCK)
            query_side = sum(dcq_ref[pair, rows, :] for pair in range(dcq_ref.shape[0]))
            dls = _dot_exact(tri_up, dct_ref[blk].T + query_side) + carry
            carry = dls[0:1, :]
            df = dls * jax.nn.sigmoid(-(f_ref[rows, :] + b_ref[...]))
            df_ref[rows, :] = df.astype(df_ref.dtype)
            db = db + jnp.sum(df, axis=0, keepdims=True)
        db_ref[...] = db

    return pl.pallas_call(
        body, grid=(1,),
        in_specs=[pl.BlockSpec((t, LANES), lambda i: (0, 5)), pl.BlockSpec((1, LANES), lambda i: (0, 0)),
                  pl.BlockSpec((nb, LANES, CUM_BLOCK), lambda i: (0, 0, 0)),
                  pl.BlockSpec((dcum_q.shape[0], t, LANES), lambda i: (0, 0, 0))],
        out_specs=[pl.BlockSpec((t, LANES), lambda i: (0, 0)), pl.BlockSpec((1, LANES), lambda i: (0, 0))],
        out_shape=[jax.ShapeDtypeStruct((t, LANES), ACT_DTYPE), jax.ShapeDtypeStruct((1, LANES), F32)],
        compiler_params=_params("arbitrary"), name="fox_gate_bwd")(pel, b_f, dcum_t, dcum_q)


FOX_Q_BLOCK = 256


assert FOX_Q_BLOCK == CUM_BLOCK
FOX_KEY_STEP = 512


def _fox_scores(q_ref, k_ref, cum_ref, cum_t_ref, h, i):
    hs = slice(h * HEAD_DIM, (h + 1) * HEAD_DIM)
    nb = cum_t_ref.shape[0]
    key_gate = jnp.concatenate([cum_t_ref[kb, h:h + 1, :] for kb in range(nb)], axis=1)
    s = _dot(q_ref[:, hs], k_ref[:, hs], 1, 1) * (HEAD_DIM ** -0.5) + (cum_ref[:, h:h + 1] - key_gate)
    shape = (FOX_Q_BLOCK, nb * FOX_Q_BLOCK)
    return jnp.where(_iota(shape, 1) <= i * FOX_Q_BLOCK + _iota(shape, 0), s, NEG)


def _fox_specs(t):
    bq, nb = FOX_Q_BLOCK, t // FOX_Q_BLOCK
    return [pl.BlockSpec((bq, GROUP_WIDTH), lambda i: (i, 2)), pl.BlockSpec((t, GROUP_WIDTH), lambda i: (0, 3)),
            pl.BlockSpec((t, GROUP_WIDTH), lambda i: (0, 4)), pl.BlockSpec((bq, LANES), lambda i: (i, 0)),
            pl.BlockSpec((nb, ATT_HEADS, bq), lambda i: (0, 0, 0))]


def _fox_fwd(pmm, cum, cum_t, ride=None):
    t = pmm.shape[0]
    bq = FOX_Q_BLOCK

    def body(q_ref, k_ref, v_ref, cum_ref, cum_t_ref, o_ref, lse_ref):
        i = pl.program_id(0)
        lse_ref[...] = jnp.zeros_like(lse_ref)
        for h in range(ATT_HEADS):
            hs = slice(h * HEAD_DIM, (h + 1) * HEAD_DIM)
            s = _fox_scores(q_ref, k_ref, cum_ref, cum_t_ref, h, i)
            m = jnp.max(s, axis=-1, keepdims=True)
            p = jnp.exp(s - m)
            l = jnp.sum(p, axis=-1, keepdims=True)
            o_ref[:, hs] = (_dot(p, v_ref[:, hs]) / l).astype(o_ref.dtype)
            lse_ref[:, h:h + 1] = m + jnp.log(l)

    return _pcall(
        body, ride, grid=(t // bq,), in_specs=_fox_specs(t),
        out_specs=[pl.BlockSpec((bq, GROUP_WIDTH), lambda i: (i, 0)), pl.BlockSpec((bq, LANES), lambda i: (i, 0))],
        out_shape=[jax.ShapeDtypeStruct((t, GROUP_WIDTH), ACT_DTYPE), jax.ShapeDtypeStruct((t, LANES), F32)],
        semantics=("parallel",), name="fox_fwd")(pmm, pmm, pmm, cum, cum_t)


def _fox_bwd(pmm, cum, cum_t, lse, dmix, ride=None):
    t = pmm.shape[0]
    bq, nb = FOX_Q_BLOCK, t // FOX_Q_BLOCK
    pairs, per_pair = ATT_HEADS // 2, LANES // HEAD_DIM
    scale = HEAD_DIM ** -0.5

    def body(q_ref, k_ref, v_ref, cum_ref, cum_t_ref, lse_ref, do_ref, dq_ref, dk_ref, dv_ref, dct_ref, dcq_ref):
        g, i = pl.program_id(0), pl.program_id(1)

        @pl.when(i == 0)
        def _():
            dk_ref[...] = jnp.zeros_like(dk_ref)
            dv_ref[...] = jnp.zeros_like(dv_ref)

        @pl.when((i == 0) & (g == 0))
        def _():
            dct_ref[...] = jnp.zeros_like(dct_ref)

        lane = _iota((1, LANES), 1)

        def run(n):
            causal = _iota((bq, n), 1) <= i * bq + _iota((bq, n), 0)
            dcq = jnp.zeros((bq, LANES), F32)
            for hh in range(per_pair):
                h = per_pair * g + hh
                hs = slice(hh * HEAD_DIM, (hh + 1) * HEAD_DIM)
                pick = (lane == h).astype(F32)
                cq = jnp.sum(cum_ref[...] * pick, axis=1, keepdims=True)
                lse_h = jnp.sum(lse_ref[...] * pick, axis=1, keepdims=True)
                key_gate = jnp.concatenate([cum_t_ref[kb, pl.ds(h, 1), :] for kb in range(n // bq)], axis=1)
                s = _dot(q_ref[:, hs], k_ref[:n, hs], 1, 1) * scale + (cq - key_gate)
                p = jnp.exp(jnp.where(causal, s, NEG) - lse_h)
                do = do_ref[:, hs]
                dp = _dot(do, v_ref[:n, hs], 1, 1)
                ds = p * (dp - jnp.sum(p * dp, axis=-1, keepdims=True))
                dq_ref[:, hs] = (_dot(ds, k_ref[:n, hs]) * scale).astype(dq_ref.dtype)
                dk_ref[:n, hs] += _dot(ds, q_ref[:, hs], 0, 0) * scale
                dv_ref[:n, hs] += _dot(p, do, 0, 0)
                key_side = -jnp.sum(ds, axis=0, keepdims=True)
                for kb in range(n // bq):
                    dct_ref[kb, pl.ds(h, 1), :] += key_side[:, kb * bq:(kb + 1) * bq]
                dcq = dcq + jnp.sum(ds, axis=1, keepdims=True) * pick
            dcq_ref[...] = dcq

        for kx in range(t // FOX_KEY_STEP):
            pl.when(i // (FOX_KEY_STEP // bq) == kx)(functools.partial(run, (kx + 1) * FOX_KEY_STEP))

    cols = lambda first: pl.BlockSpec((bq, LANES), lambda g, i: (i, first + g))
    keys = lambda first: pl.BlockSpec((t, LANES), lambda g, i: (0, first + g))
    per_head = pl.BlockSpec((bq, LANES), lambda g, i: (i, 0))
    fox_q, fox_k, fox_v = (GROUP_WIDTH * n // LANES for n in (2, 3, 4))
    return _pcall(
        body, ride, grid=(pairs, t // bq),
        in_specs=[cols(fox_q), keys(fox_k), keys(fox_v), per_head,
                  pl.BlockSpec((nb, ATT_HEADS, bq), lambda g, i: (0, 0, 0)), per_head, cols(GROUP_WIDTH // LANES)],
        out_specs=[cols(0), keys(0), keys(0), pl.BlockSpec((nb, LANES, bq), lambda g, i: (0, 0, 0)),
                   pl.BlockSpec((None, bq, LANES), lambda g, i: (g, i, 0))],
        out_shape=[jax.ShapeDtypeStruct((t, GROUP_WIDTH), ACT_DTYPE), jax.ShapeDtypeStruct((t, GROUP_WIDTH), F32),
                   jax.ShapeDtypeStruct((t, GROUP_WIDTH), F32), jax.ShapeDtypeStruct((nb, LANES, bq), F32),
                   jax.ShapeDtypeStruct((pairs, t, LANES), F32)],
        semantics=("arbitrary", "arbitrary"), name="fox_bwd")(pmm, pmm, pmm, cum, cum_t, lse, dmix)


CA_Q_BLOCK = 4 * CHUNK
CA_WINDOW = CA_Q_BLOCK + CA_LEFT
CA_BASE = 1024


def _ca_bias_base(rel_bias):
    n = rel_bias.shape[0]
    flat = CA_Q_BLOCK + CA_LEFT - REL_CLIP
    tail = CA_BASE - flat - (2 * REL_CLIP + 1)
    return jnp.concatenate([jnp.broadcast_to(rel_bias[:, 2 * REL_CLIP:], (n, flat)), rel_bias[:, ::-1],
                            jnp.broadcast_to(rel_bias[:, :1], (n, tail))], axis=1)


def _ca_bias_base_grad(dbase):
    flat = CA_Q_BLOCK + CA_LEFT - REL_CLIP
    mid = dbase[:, flat:flat + 2 * REL_CLIP + 1][:, ::-1]
    lo = jnp.sum(dbase[:, flat + 2 * REL_CLIP + 1:], axis=1, keepdims=True)
    hi = jnp.sum(dbase[:, :flat], axis=1, keepdims=True)
    pad = jnp.zeros((dbase.shape[0], 2 * REL_CLIP - 1), F32)
    return mid + jnp.concatenate([lo, pad, hi], axis=1)


def _ca_mask(i):
    r, j = _iota((CA_Q_BLOCK, CA_WINDOW), 0), _iota((CA_Q_BLOCK, CA_WINDOW), 1)
    rc, jc = r // CHUNK, j // CHUNK
    return (jc >= rc) & (jc <= rc + CA_LEFT // CHUNK) & (i * CA_Q_BLOCK + j >= CA_LEFT)


def _ca_fill_bias(i, base_ref, bias_scr):
    @pl.when(i == 0)
    def _():
        for h in range(ATT_HEADS):
            rows = jnp.broadcast_to(base_ref[h:h + 1, :], (CA_Q_BLOCK, CA_BASE))
            bias_scr[h] = pltpu.roll(rows, CA_BASE - CA_Q_BLOCK, 1, stride=1, stride_axis=0)[:, :CA_WINDOW]


def _ca_scores(q_ref, kp_ref, bias_scr, win, h, mask):
    hs = slice(h * HEAD_DIM, (h + 1) * HEAD_DIM)
    s = _dot(q_ref[:, hs], kp_ref[win, hs], 1, 1) * (HEAD_DIM ** -0.5)
    return jnp.where(mask, s + bias_scr[h], NEG)


CA_BIAS_SCRATCH = pltpu.VMEM((ATT_HEADS, CA_Q_BLOCK, CA_WINDOW), F32)


def _ca_fwd(pmm, kp, vp, base, ride=None):
    t = pmm.shape[0]

    def body(q_ref, kp_ref, vp_ref, base_ref, o_ref, lse_ref, bias_scr):
        i = pl.program_id(0)
        _ca_fill_bias(i, base_ref, bias_scr)
        win = pl.ds(pl.multiple_of(i * CA_Q_BLOCK, CA_Q_BLOCK), CA_WINDOW)
        mask = _ca_mask(i)
        lse_ref[...] = jnp.zeros_like(lse_ref)
        for h in range(ATT_HEADS):
            hs = slice(h * HEAD_DIM, (h + 1) * HEAD_DIM)
            s = _ca_scores(q_ref, kp_ref, bias_scr, win, h, mask)
            m = jnp.max(s, axis=-1, keepdims=True)
            p = jnp.exp(s - m)
            l = jnp.sum(p, axis=-1, keepdims=True)
            o_ref[:, hs] = (_dot(p, vp_ref[win, hs]) / l).astype(o_ref.dtype)
            lse_ref[:, h:h + 1] = m + jnp.log(l)

    padded = pl.BlockSpec((t + CA_LEFT, GROUP_WIDTH), lambda i: (0, 0))
    return _pcall(
        body, ride, grid=(t // CA_Q_BLOCK,),
        in_specs=[pl.BlockSpec((CA_Q_BLOCK, GROUP_WIDTH), lambda i: (i, 0)), padded, padded,
                  pl.BlockSpec((ATT_HEADS, CA_BASE), lambda i: (0, 0))],
        out_specs=[pl.BlockSpec((CA_Q_BLOCK, GROUP_WIDTH), lambda i: (i, 0)),
                   pl.BlockSpec((CA_Q_BLOCK, LANES), lambda i: (i, 0))],
        out_shape=[jax.ShapeDtypeStruct((t, GROUP_WIDTH), ACT_DTYPE), jax.ShapeDtypeStruct((t, LANES), F32)],
        scratch_shapes=[CA_BIAS_SCRATCH], semantics=("arbitrary",), name="ca_fwd")(pmm, kp, vp, base)


def _ca_bwd(pmm, kp, vp, base, lse, dmix, ride=None):
    t = pmm.shape[0]
    scale = HEAD_DIM ** -0.5

    def body(q_ref, kp_ref, vp_ref, base_ref, lse_ref, do_ref, dq_ref, dkp_ref, dvp_ref, dbase_ref, bias_scr):
        i = pl.program_id(0)
        _ca_fill_bias(i, base_ref, bias_scr)

        @pl.when(i == 0)
        def _():
            dkp_ref[...] = jnp.zeros_like(dkp_ref)
            dvp_ref[...] = jnp.zeros_like(dvp_ref)
            dbase_ref[...] = jnp.zeros_like(dbase_ref)

        win = pl.ds(pl.multiple_of(i * CA_Q_BLOCK, CA_Q_BLOCK), CA_WINDOW)
        mask = _ca_mask(i)
        flip = (_iota((CA_Q_BLOCK, CA_Q_BLOCK), 0) + _iota((CA_Q_BLOCK, CA_Q_BLOCK), 1) == CA_Q_BLOCK - 1).astype(F32)
        for h in range(ATT_HEADS):
            hs = slice(h * HEAD_DIM, (h + 1) * HEAD_DIM)
            s = _ca_scores(q_ref, kp_ref, bias_scr, win, h, mask)
            p = jnp.exp(s - lse_ref[:, h:h + 1])
            do = do_ref[:, hs]
            dp = _dot(do, vp_ref[win, hs], 1, 1)
            ds = p * (dp - jnp.sum(p * dp, axis=-1, keepdims=True))
            dq_ref[:, hs] = (_dot(ds, kp_ref[win, hs]) * scale).astype(dq_ref.dtype)
            dkp_ref[win, hs] += _dot(ds, q_ref[:, hs], 0, 0) * scale
            dvp_ref[win, hs] += _dot(p, do, 0, 0)
            rev = jnp.concatenate([_dot(flip, ds), jnp.zeros((CA_Q_BLOCK, CA_BASE - CA_WINDOW), F32)], axis=1)
            lined = pltpu.roll(rev, 1, 1, stride=1, stride_axis=0)
            dbase_ref[h:h + 1, :] += jnp.sum(lined, axis=0, keepdims=True)

    padded = pl.BlockSpec((t + CA_LEFT, GROUP_WIDTH), lambda i: (0, 0))
    return _pcall(
        body, ride, grid=(t // CA_Q_BLOCK,),
        in_specs=[pl.BlockSpec((CA_Q_BLOCK, GROUP_WIDTH), lambda i: (i, 0)), padded, padded,
                  pl.BlockSpec((ATT_HEADS, CA_BASE), lambda i: (0, 0)),
                  pl.BlockSpec((CA_Q_BLOCK, LANES), lambda i: (i, 0)),
                  pl.BlockSpec((CA_Q_BLOCK, GROUP_WIDTH), lambda i: (i, 0))],
        out_specs=[pl.BlockSpec((CA_Q_BLOCK, GROUP_WIDTH), lambda i: (i, 0)), padded, padded,
                   pl.BlockSpec((ATT_HEADS, CA_BASE), lambda i: (0, 0))],
        out_shape=[jax.ShapeDtypeStruct((t, GROUP_WIDTH), ACT_DTYPE),
                   jax.ShapeDtypeStruct((t + CA_LEFT, GROUP_WIDTH), F32),
                   jax.ShapeDtypeStruct((t + CA_LEFT, GROUP_WIDTH), F32),
                   jax.ShapeDtypeStruct((ATT_HEADS, CA_BASE), F32)],
        scratch_shapes=[CA_BIAS_SCRATCH], semantics=("arbitrary",), name="ca_bwd")(pmm, kp, vp, base, lse, dmix)


GELU_C = 0.7978845608028654
GELU_A = 0.044715


def _shift_down(v, k, fill, period=None):
    rows = _iota(v.shape, 0)
    rows = rows if period is None else rows & (period - 1)
    return jnp.where(rows >= k, pltpu.roll(v, k, 0), fill)


def _shift_up(v, k, fill, period=None):
    t = v.shape[0]
    rows = _iota(v.shape, 0)
    rows, length = (rows, t) if period is None else (rows & (period - 1), period)
    return jnp.where(rows < length - k, pltpu.roll(v, t - k, 0), fill)


LRU_SCAN_BLOCK = 256


def _linear_scan(a, b, reverse=False):
    shift = _shift_up if reverse else _shift_down
    k = 1
    while k < LRU_SCAN_BLOCK:
        b = a * shift(b, k, 0.0, LRU_SCAN_BLOCK) + b
        a = a * shift(a, k, 1.0, LRU_SCAN_BLOCK)
        k *= 2
    nb = a.shape[0] // LRU_SCAN_BLOCK
    carry = jnp.zeros((1, a.shape[1]), F32)
    out = [None] * nb
    for blk in (reversed(range(nb)) if reverse else range(nb)):
        rows = slice(blk * LRU_SCAN_BLOCK, (blk + 1) * LRU_SCAN_BLOCK)
        h = b[rows] + a[rows] * carry
        out[blk] = h
        carry = h[0:1] if reverse else h[LRU_SCAN_BLOCK - 1:LRU_SCAN_BLOCK]
    return jnp.concatenate(out, axis=0)


def _neg_expm1(y):
    series = -y * (1.0 + y * (0.5 + y * (1.0 / 6.0 + y * (1.0 / 24.0 + y * (1.0 / 120.0)))))
    return jnp.where(y > -0.1, series, 1.0 - jnp.exp(y))


def _lru_forward(x, g_in, cw, cb, wa, ba, wx, bx, lam):
    xs = [_shift_down(x, CONV_WIDTH - 1 - j, 0.0) for j in range(CONV_WIDTH - 1)] + [x]
    xc = cb + sum(cw[j:j + 1, :] * xs[j] for j in range(CONV_WIDTH))
    r = jax.nn.sigmoid(_dot(xc, wa) + ba)
    i = jax.nn.sigmoid(_dot(xc, wx) + bx)
    lsl = _log_sigmoid(lam)
    la = LRU_C * r * lsl
    a = jnp.exp(la)
    s = jnp.sqrt(_neg_expm1(2.0 * la))
    h = _linear_scan(a, s * (i * xc))
    u = GELU_C * (g_in + GELU_A * g_in * g_in * g_in)
    th = jnp.tanh(u)
    gelu = 0.5 * g_in * (1.0 + th)
    return xs, xc, r, i, lsl, a, s, h, th, gelu


def _lru_specs(t):
    col = lambda off: pl.BlockSpec((t, LANES), lambda j: (0, j + off))
    vec = pl.BlockSpec((1, LANES), lambda j: (0, j))
    mat = pl.BlockSpec((None, LANES, LANES), lambda j: (j, 0, 0))
    return [col(0), col(GROUP_WIDTH // LANES), pl.BlockSpec((CONV_WIDTH, LANES), lambda j: (0, j)),
            vec, mat, vec, mat, vec, vec]


def _lru_fwd(pel, conv_w, conv_b, wa, ba, wx, bx, lam, ride=None):
    t = pel.shape[0]

    def body(g_ref, x_ref, cw_ref, cb_ref, wa_ref, ba_ref, wx_ref, bx_ref, lam_ref, o_ref):
        res = _lru_forward(x_ref[...], g_ref[...], cw_ref[...], cb_ref[...], wa_ref[...], ba_ref[...],
                           wx_ref[...], bx_ref[...], lam_ref[...])
        o_ref[...] = (res[7] * res[9]).astype(o_ref.dtype)

    return _pcall(
        body, ride, grid=(GROUP_WIDTH // LANES,), in_specs=_lru_specs(t),
        out_specs=pl.BlockSpec((t, LANES), lambda j: (0, j)),
        out_shape=jax.ShapeDtypeStruct((t, GROUP_WIDTH), ACT_DTYPE),
        semantics=("parallel",), name="lru_fwd")(pel, pel, conv_w, conv_b, wa, ba, wx, bx, lam)


def _lru_bwd(pel, conv_w, conv_b, wa, ba, wx, bx, lam, dmix, ride=None):
    t = pel.shape[0]

    def body(g_ref, x_ref, cw_ref, cb_ref, wa_ref, ba_ref, wx_ref, bx_ref, lam_ref, do_ref,
             dg_ref, dx_ref, dcw_ref, dcb_ref, dwa_ref, dba_ref, dwx_ref, dbx_ref, dlam_ref):
        g_in, cw, lam = g_ref[...], cw_ref[...], lam_ref[...]
        xs, xc, r, i, lsl, a, s, h, th, gelu = _lru_forward(
            x_ref[...], g_in, cw, cb_ref[...], wa_ref[...], ba_ref[...], wx_ref[...], bx_ref[...], lam)
        dout = do_ref[...]
        dgelu = 0.5 * (1.0 + th) + 0.5 * g_in * (1.0 - th * th) * GELU_C * (1.0 + 3.0 * GELU_A * g_in * g_in)
        dg_ref[...] = (dout * h * dgelu).astype(dg_ref.dtype)
        gsum = _linear_scan(_shift_up(a, 1, 0.0), dout * gelu, reverse=True)
        da = gsum * _shift_down(h, 1, 0.0)
        di = gsum * s * xc
        dla = da * a - gsum * (i * xc) * (a * a / s)
        dlam_ref[...] = jnp.sum(dla * (LRU_C * r), axis=0, keepdims=True) * jax.nn.sigmoid(-lam)
        dpr = dla * (LRU_C * lsl) * r * (1.0 - r)
        dpi = di * i * (1.0 - i)
        dxc = gsum * s * i + _dot(dpr, wa_ref[...], 1, 1) + _dot(dpi, wx_ref[...], 1, 1)
        xct = xc.T
        dwa_ref[...] = _dot(xct, dpr)
        dwx_ref[...] = _dot(xct, dpi)
        dba_ref[...] = jnp.sum(dpr, axis=0, keepdims=True)
        dbx_ref[...] = jnp.sum(dpi, axis=0, keepdims=True)
        dcb_ref[...] = jnp.sum(dxc, axis=0, keepdims=True)
        for j in range(CONV_WIDTH):
            dcw_ref[j:j + 1, :] = jnp.sum(dxc * xs[j], axis=0, keepdims=True)
        dx = cw[CONV_WIDTH - 1:CONV_WIDTH, :] * dxc
        for j in range(CONV_WIDTH - 1):
            dx = dx + cw[j:j + 1, :] * _shift_up(dxc, CONV_WIDTH - 1 - j, 0.0)
        dx_ref[...] = dx.astype(dx_ref.dtype)

    col = pl.BlockSpec((t, LANES), lambda j: (0, j))
    vec = pl.BlockSpec((1, LANES), lambda j: (0, j))
    mat = pl.BlockSpec((None, LANES, LANES), lambda j: (j, 0, 0))
    nb = GROUP_WIDTH // LANES
    vshape = jax.ShapeDtypeStruct((1, GROUP_WIDTH), F32)
    mshape = jax.ShapeDtypeStruct((nb, LANES, LANES), F32)
    return _pcall(
        body, ride, grid=(nb,),
        in_specs=_lru_specs(t) + [pl.BlockSpec((t, LANES), lambda j: (0, j + nb))],
        out_specs=[col, col, pl.BlockSpec((CONV_WIDTH, LANES), lambda j: (0, j)), vec, mat, vec, mat, vec, vec],
        out_shape=[jax.ShapeDtypeStruct((t, GROUP_WIDTH), ACT_DTYPE), jax.ShapeDtypeStruct((t, GROUP_WIDTH), ACT_DTYPE),
                   jax.ShapeDtypeStruct((CONV_WIDTH, GROUP_WIDTH), F32), vshape, mshape, vshape, mshape, vshape, vshape],
        semantics=("parallel",), name="lru_bwd")(
            pel, pel, conv_w, conv_b, wa, ba, wx, bx, lam, dmix)


def _block_diag_pairs(w):
    z = jnp.zeros((LRU_BLOCK_DIM, LRU_BLOCK_DIM), w.dtype)
    return jnp.stack([jnp.block([[w[2 * j], z], [z, w[2 * j + 1]]]) for j in range(w.shape[0] // 2)])


def _block_diag_pairs_grad(dw):
    b = LRU_BLOCK_DIM
    return jnp.stack([dw[n // 2, (n % 2) * b:(n % 2 + 1) * b, (n % 2) * b:(n % 2 + 1) * b] for n in range(2 * dw.shape[0])])


def _row_tile(r):
    return ROW_TILE if r % ROW_TILE == 0 else r


def _pair_sum(g, got, place, name):
    _, r, c = g.shape
    tile = r

    def body(place_ref, a_ref, b_ref, o_ref):
        o_ref[...] = (a_ref[...].astype(F32) + b_ref[...].astype(F32)).astype(o_ref.dtype)

    blk = pl.BlockSpec((1, tile, c), lambda k, i, place_ref: (k, i, 0))
    return pl.pallas_call(
        body,
        grid_spec=pltpu.PrefetchScalarGridSpec(
            num_scalar_prefetch=1, grid=(N_CHIPS, r // tile),
            in_specs=[pl.BlockSpec((1, tile, c), lambda k, i, place_ref: (2 * k + place_ref[0], i, 0)), blk],
            out_specs=blk),
        out_shape=jax.ShapeDtypeStruct(got.shape, got.dtype),
        compiler_params=_params("parallel", "parallel"), name=name)(place, g, got)


def _adamw_update(g, w_ref, m_ref, v_ref, g_ref, d_ref, nm_ref, nv_ref):
    nm = ADAM_B1 * m_ref[...] + (1.0 - ADAM_B1) * g
    nv = ADAM_B2 * v_ref[...] + (1.0 - ADAM_B2) * jnp.square(g)
    m_hat = nm / (1.0 - ADAM_B1 ** ADAM_STEP)
    v_hat = nv / (1.0 - ADAM_B2 ** ADAM_STEP)
    g_ref[...] = g
    d_ref[...] = -ADAM_LR * (m_hat / (jnp.sqrt(v_hat) + ADAM_EPS) + ADAM_WD * w_ref[...])
    nm_ref[...] = nm
    nv_ref[...] = nv


def _adamw_sharded(parts, w, m, v, place, name, ride=None):
    n_layers, r, c = w.shape
    tile = _row_tile(r)
    nb = r // tile
    counts = [1 + len(recvs) for _, recvs in parts]

    def body(place_ref, *refs):
        layer = pl.program_id(0)
        g, at = None, 0
        for l in range(n_layers):
            g_l = refs[at][0].astype(F32)
            for r_ref in refs[at + 1:at + counts[l]]:
                for k in range(r_ref.shape[0]):
                    g_l = g_l + r_ref[k].astype(F32)
            g = g_l if g is None else jnp.where(layer == l, g_l, g)
            at += counts[l]
        _adamw_update(g, *refs[at:])

    def part_specs(l, recvs):
        rows = lambda q, i: jnp.where(q < l, 0, jnp.where(q > l, nb - 1, i))
        return ([pl.BlockSpec((1, tile, c), lambda q, i, place_ref: (place_ref[1], rows(q, i), 0))] +
                [pl.BlockSpec((a.shape[0], tile, c), lambda q, i, place_ref: (0, rows(q, i), 0)) for a in recvs])

    in_specs, args = [], []
    for l, (s, recvs) in enumerate(parts):
        in_specs += part_specs(l, recvs)
        args += [s, *recvs]
    blk = pl.BlockSpec((None, tile, c), lambda q, i, place_ref: (q, i, 0))
    out = jax.ShapeDtypeStruct((n_layers, r, c), F32)
    return _pcall(body, ride, grid=(n_layers, nb), in_specs=in_specs + [blk, blk, blk], out_specs=[blk, blk, blk, blk],
                  out_shape=[out, out, out, out], semantics=("arbitrary", "arbitrary"), name=name, prefetch=True)(
                      place, *args, w, m, v)


def _adamw_small(repl_parts, vec_parts, w, m, v, place):
    n_r, n = len(repl_parts), len(w)
    shapes = [a.shape for a in w]

    def body(place_ref, *refs):
        parts, rest = refs[:n], refs[n:]
        for k in range(n):
            take = (lambda p: parts[k][p]) if k < n_r else (lambda p: parts[k][p, 0])
            g = take(0)
            for p in range(1, N_DEV):
                g = g + take(p)
            _adamw_update(g, rest[k], rest[n + k], rest[2 * n + k], *rest[3 * n + 4 * k:3 * n + 4 * k + 4])

    def whole(shape):
        return pl.BlockSpec(shape, lambda i, place_ref: (0,) * len(shape))

    def mine(shard):
        return pl.BlockSpec((N_DEV, 1) + shard, lambda i, place_ref: (0, place_ref[2]) + (0,) * len(shard))

    in_specs = [whole(a.shape) for a in repl_parts] + [mine(s) for s in shapes[n_r:]] + [whole(s) for s in shapes] * 3
    outs = pl.pallas_call(
        body,
        grid_spec=pltpu.PrefetchScalarGridSpec(
            num_scalar_prefetch=1, grid=(1,), in_specs=in_specs,
            out_specs=[whole(s) for s in shapes for _ in range(4)]),
        out_shape=[jax.ShapeDtypeStruct(s, F32) for s in shapes for _ in range(4)],
        compiler_params=_params("arbitrary"), name="adamw_small")(place, *repl_parts, *vec_parts, *w, *m, *v)
    return [outs[4 * k:4 * k + 4] for k in range(n)]


SHARDED = {"norm_w": 2, "w_in_even": 2, "gla_w_a_up": 2, "w_out_even": 1, "w_in_odd": 2, "conv_w": 2, "conv_b": 1,
           "lru_b_a": 1, "lru_b_x": 1, "lru_lambda": 1, "w_out_odd": 1, "w_mlp_up": 2, "w_mlp_down": 1}
REPLICATED = ["gla_b_a", "gla_norm_w", "fox_b_f", "rel_bias", "lru_w_a", "lru_w_x"]
WEIGHTS = ["norm_w", "w_in_even", "gla_w_a_up", "gla_b_a", "gla_norm_w", "fox_b_f", "w_out_even", "w_in_odd",
           "rel_bias", "conv_w", "conv_b", "lru_w_a", "lru_b_a", "lru_w_x", "lru_b_x", "lru_lambda", "w_out_odd",
           "w_mlp_up", "w_mlp_down"]
MATRICES = ("w_in_even", "w_out_even", "w_in_odd", "w_out_odd", "w_mlp_up", "w_mlp_down")
TRANSPOSED = ("w_in_even", "w_in_odd")
VECTORS = tuple(n for n in SHARDED if n not in MATRICES)
MATRIX_BLOCKS = (("w_in_even", 0), ("w_out_even", 0), ("w_in_odd", 0), ("w_out_odd", 0),
                 ("w_mlp_up", 0), ("w_mlp_up", 1), ("w_mlp_down", 0), ("w_mlp_down", 1))


def _join_shards(blocks, axis):
    moved = jnp.moveaxis(blocks, 0, axis)
    shape = moved.shape
    return moved.reshape(shape[:axis] + (shape[axis] * shape[axis + 1],) + shape[axis + 2:])


def _split_shards(full, axis):
    shape = full.shape
    cut = full.reshape(shape[:axis] + (N_DEV, shape[axis] // N_DEV) + shape[axis + 1:])
    return jnp.moveaxis(cut, axis, 0)


EVEN_SPLITS = (0, 256, 512, 1024, 1536, 1552, 2064, 2576, 3088, 3096)


def _even_in_split(wt):
    c = [wt[EVEN_SPLITS[k]:EVEN_SPLITS[k + 1]] for k in range(9)]
    gq, gk, gv, gr, ga, fq, fk, fv, ff = c
    padrows = lambda a: jnp.pad(a, ((0, LANES - a.shape[0]), (0, 0)))
    return jnp.concatenate([gq, gk, gv, fq, fk, fv], axis=0), jnp.concatenate([gr, padrows(ga), padrows(ff)], axis=0)


def _even_in_merge(dmm, dele):
    return jnp.concatenate([dmm[:1024], dele[:512], dele[512:512 + GLA_RANK], dmm[1024:2560],
                            dele[640:640 + ATT_HEADS]], axis=0)


def _forward_backward(x, target, shard, vec_shard, w, place):
    w = dict(w)
    g, dnorm, sums, recv = {}, {}, {}, {}
    nrm = lambda l, k: w["norm_w"][l, k][None, :]
    gather = lambda *keys: _gather_plan([shard[k] for k in keys])
    blocks = lambda r, c: (N_DEV, r // N_DEV, c)

    def pair_sum(key):
        sums[key] = _pair_sum(g[key], got[key], place, f"rs_pair_sum_{key[0]}_{key[1]}")

    got = {}

    def mlp_fwd(xin, layer, ride_up, ride_down):
        up = _mm(xin, w["w_mlp_up"][layer], out_dtype=ACT_DTYPE, tm=TM_FWD, tn=D_FF // N_DEV, b_blocked=True,
                 a_norm=nrm(layer, 2), name=f"mlp_up_{layer}", ride=ride_up)
        (u, h), rode_up = up if ride_up is not None else (up, None)
        down = _mm(u, w["w_mlp_down"][layer], out_dtype=F32, tm=TM_DX // 2, tn=D_MODEL, a_sqrelu=True,
                   res_norm=(xin, nrm(layer, 3)), name=f"mlp_down_{layer}", ride=ride_down)
        (yv, xout), rode_down = down if ride_down is not None else (down, None)
        return xout, (xin, h, u, yv), rode_up, rode_down

    def mlp_bwd(dxout, saved, layer, ride):
        xin, h, u, yv = saved
        k_up, k_down = ("w_mlp_up", layer), ("w_mlp_down", layer)
        res = _mm(dxout, w["w_mlp_down"][layer], nt=True, out_dtype=ACT_DTYPE, tm=TM_DX, tn=TN, drelu_of=u,
                  a_norm_bwd=(yv, nrm(layer, 3)), name=f"mlp_down_dx_{layer}", ride=ride)
        (du, dy, dnorm[(layer, 3)]), rode = res if ride is not None else (res, None)
        g[k_down] = _mm(u, dy, ta=True, out_dtype=WIRE_DTYPE, tm=TM_DW, tn=TN, a_sqrelu=True,
                        name=f"mlp_down_dw_{layer}").reshape(blocks(D_FF, D_MODEL))
        g[k_up] = _mm(h, du, ta=True, out_dtype=WIRE_DTYPE, tm=TM_DW, tn=D_FF // N_DEV, out_blocked=True,
                      name=f"mlp_up_dw_{layer}")
        w_up = jnp.moveaxis(w["w_mlp_up"][layer], 0, 1).reshape(D_MODEL, D_FF)
        (dxin, dnorm[(layer, 2)]), (got[k_down], got[k_up]) = _mm(
            du, w_up, nt=True, out_dtype=F32, tm=TM_DX // 2, tn=D_MODEL, norm_bwd=(xin, nrm(layer, 2), dxout),
            name=f"mlp_up_dx_{layer}", ride=_sibling_plan([g[k_down], g[k_up]]))
        pair_sum(k_down)
        pair_sum(k_up)
        return dxin, rode

    first = _run_plan(_gather_plan([shard[("w_in_even", 0)]] + [vec_shard[n] for n in VECTORS]),
                      "weights_all_gather_first")
    w["w_in_even"] = first[0].reshape(-1, D_MODEL)
    for n, b in zip(VECTORS, first[1:]):
        w[n] = _join_shards(b, SHARDED[n])
    w["w_mlp_up"], w["w_mlp_down"] = [None] * DEPTH, [None] * DEPTH

    wmm_e, wel_e = _even_in_split(w["w_in_even"])
    w_up_pad = jnp.pad(w["gla_w_a_up"][0], ((0, LANES - GLA_RANK), (0, 0)))
    b_f_pad = jnp.pad(w["fox_b_f"], ((0, 0), (0, LANES - ATT_HEADS)))
    (pmm0, h0), (w_out_even,) = _mm(x, wmm_e, nt=True, out_dtype=ACT_DTYPE, tm=TM_FWD, tn=TN, a_norm=nrm(0, 0),
                                    name="in_even_mm", ride=gather(("w_out_even", 0)))
    pel0 = _mm(h0, wel_e, nt=True, out_dtype=F32, tm=TM_FWD, tn=768, name="in_even_el")
    (out_a, states), (w["w_mlp_up"][0],) = _gla_fwd(pmm0, pel0, w_up_pad, w["gla_b_a"], w["gla_norm_w"],
                                                    ride=gather(("w_mlp_up", 0)))
    cum, cum_t = _fox_gate_fwd(pel0, b_f_pad)
    (out_b, lse_b), (w_mlp_down0, w_in_odd) = _fox_fwd(pmm0, cum, cum_t,
                                                       ride=gather(("w_mlp_down", 0), ("w_in_odd", 0)))
    w["w_out_even"] = w_out_even.reshape(D_MODEL, D_MODEL)
    w["w_mlp_down"][0] = w_mlp_down0.reshape(D_FF, D_MODEL)
    mix_in0 = jnp.concatenate([out_a, out_b], axis=1)
    mix0, x1 = _mm(mix_in0, w["w_out_even"], out_dtype=F32, tm=TM_DX, tn=D_MODEL, res_norm=(x, nrm(0, 1)),
                   name="out_even")
    x2, mlp0, _, (w["w_mlp_up"][1],) = mlp_fwd(x1, 0, None, gather(("w_mlp_up", 1)))
    w["w_in_odd"] = w_in_odd.reshape(-1, D_MODEL)

    w_in_o = w["w_in_odd"]
    n_mm_o = 3 * GROUP_WIDTH
    wa_bd, wx_bd = _block_diag_pairs(w["lru_w_a"][0]), _block_diag_pairs(w["lru_w_x"][0])
    base = _ca_bias_base(w["rel_bias"][0])
    pmm1, h1 = _mm(x2, w_in_o[:n_mm_o], nt=True, out_dtype=ACT_DTYPE, tm=TM_FWD, tn=TN, a_norm=nrm(1, 0),
                   name="in_odd_mm")
    pel1 = _mm(h1, w_in_o[n_mm_o:], nt=True, out_dtype=F32, tm=TM_FWD, tn=TN, name="in_odd_el")
    kp = jnp.pad(pmm1[:, GROUP_WIDTH:2 * GROUP_WIDTH], ((CA_LEFT, 0), (0, 0)))
    vp = jnp.pad(pmm1[:, 2 * GROUP_WIDTH:], ((CA_LEFT, 0), (0, 0)))
    (out_c, lse_c), (w_mlp_down1,) = _ca_fwd(pmm1, kp, vp, base, ride=gather(("w_mlp_down", 1)))
    w["w_mlp_down"][1] = w_mlp_down1.reshape(D_FF, D_MODEL)
    lru_args = (pel1, w["conv_w"][0], w["conv_b"], wa_bd, w["lru_b_a"], wx_bd, w["lru_b_x"], w["lru_lambda"])
    out_d, (w_out_odd,) = _lru_fwd(*lru_args, ride=gather(("w_out_odd", 0)))
    w["w_out_odd"] = w_out_odd.reshape(D_MODEL, D_MODEL)
    mix_in1 = jnp.concatenate([out_c, out_d], axis=1)
    mix1, x3 = _mm(mix_in1, w["w_out_odd"], out_dtype=F32, tm=TM_DX, tn=D_MODEL, res_norm=(x2, nrm(1, 1)),
                   name="out_odd")
    x4, mlp1, _, _ = mlp_fwd(x3, 1, None, None)

    loss, dx4 = _loss_fwd_bwd(x4, target)

    k_oo, k_io, k_oe, k_ie = ("w_out_odd", 0), ("w_in_odd", 0), ("w_out_even", 0), ("w_in_even", 0)
    mlp_keys = lambda l: [("w_mlp_down", l), ("w_mlp_up", l)]
    dx3, _ = mlp_bwd(dx4, mlp1, 1, None)
    dmix_in1, dmix1, dnorm[(1, 1)] = _mm(dx3, w["w_out_odd"], nt=True, out_dtype=F32, tm=TM_DX, tn=TN,
                                         a_norm_bwd=(mix1, nrm(1, 1)), name="out_odd_dx")
    g[k_oo] = _mm(mix_in1, dmix1, ta=True, out_dtype=WIRE_DTYPE, tm=TM_DW, tn=TN, name="out_odd_dw").reshape(
        blocks(D_MODEL, D_MODEL))
    (dq_c, dkp, dvp, dbase), rode = _ca_bwd(
        pmm1, kp, vp, base, lse_c, dmix_in1,
        ride=_join_plans(_chip_plan([sums[k] for k in mlp_keys(1)]), _sibling_plan([g[k_oo]])))
    recv.update(zip(mlp_keys(1), rode[:2]))
    got[k_oo] = rode[2]
    pair_sum(k_oo)
    (dgate, dxin, g_conv_w, g_conv_b, dwa_bd, g_lru_b_a, dwx_bd, g_lru_b_x, g_lru_lambda), (recv[k_oo],) = _lru_bwd(
        *lru_args, dmix_in1, ride=_chip_plan([sums[k_oo]]))
    dp1 = jnp.concatenate([dq_c, dkp[CA_LEFT:].astype(ACT_DTYPE), dvp[CA_LEFT:].astype(ACT_DTYPE), dgate, dxin], axis=1)
    g[k_io] = _mm(dp1, h1, ta=True, out_dtype=WIRE_DTYPE, tm=dp1.shape[1] // 2, tn=TN, name="in_odd_dw").reshape(
        blocks(dp1.shape[1], D_MODEL))
    (dx2, dnorm[(1, 0)]), (got[k_io],) = _mm(dp1, w_in_o, out_dtype=F32, tm=TM_DX // 2, tn=D_MODEL,
                                             norm_bwd=(x2, nrm(1, 0), dx3), name="in_odd_dx",
                                             ride=_sibling_plan([g[k_io]]))
    pair_sum(k_io)
    g["rel_bias"] = _ca_bias_base_grad(dbase)[None]
    g["conv_w"], g["conv_b"] = g_conv_w[None], g_conv_b
    g["lru_w_a"], g["lru_w_x"] = _block_diag_pairs_grad(dwa_bd)[None], _block_diag_pairs_grad(dwx_bd)[None]
    g["lru_b_a"], g["lru_b_x"], g["lru_lambda"] = g_lru_b_a, g_lru_b_x, g_lru_lambda

    dx1, (recv[k_io],) = mlp_bwd(dx2, mlp0, 0, _chip_plan([sums[k_io]]))
    dmix_in0, dmix0, dnorm[(0, 1)] = _mm(dx1, w["w_out_even"], nt=True, out_dtype=F32, tm=TM_DX, tn=TN,
                                         a_norm_bwd=(mix0, nrm(0, 1)), name="out_even_dx")
    g[k_oe] = _mm(mix_in0, dmix0, ta=True, out_dtype=WIRE_DTYPE, tm=TM_DW, tn=TN, name="out_even_dw").reshape(
        blocks(D_MODEL, D_MODEL))
    k_md0, k_mu0 = mlp_keys(0)
    (dq_a, dk_a, dv_a, dr_a, da_a, dw_up_pad, g_gla_b_a, g_gla_norm_w), (got[k_oe],) = _gla_bwd(
        pmm0, pel0, w_up_pad, w["gla_b_a"], w["gla_norm_w"], states, dmix_in0, ride=_sibling_plan([g[k_oe]]))
    pair_sum(k_oe)
    (dq_b, dk_b, dv_b, dcum_t, dcum_q), (recv[k_md0], recv[k_mu0], recv[k_oe]) = _fox_bwd(
        pmm0, cum, cum_t, lse_b, dmix_in0, ride=_chip_plan([sums[k_md0], sums[k_mu0], sums[k_oe]]))
    df_b, db_f = _fox_gate_bwd(pel0, b_f_pad, dcum_t, dcum_q)
    g["gla_w_a_up"] = dw_up_pad[:GLA_RANK][None]
    g["gla_b_a"], g["gla_norm_w"], g["fox_b_f"] = g_gla_b_a, g_gla_norm_w, db_f[:, :ATT_HEADS]
    dp0 = jnp.concatenate([dq_a, dk_a, dv_a, dq_b, dk_b.astype(ACT_DTYPE), dv_b.astype(ACT_DTYPE), dr_a, da_a, df_b],
                          axis=1)
    w_perm = jnp.concatenate([wmm_e, wel_e], axis=0)
    n_mm_e = wmm_e.shape[0]
    dw_perm, repl_parts = _mm(dp0, h0, ta=True, out_dtype=WIRE_DTYPE, tm=dp0.shape[1] // 2, tn=TN, name="in_even_dw",
                              ride=_gather_plan([g[n] for n in REPLICATED]))
    dw_even = _even_in_merge(dw_perm[:n_mm_e], dw_perm[n_mm_e:])
    g[k_ie] = dw_even.reshape(blocks(dw_even.shape[0], D_MODEL))
    dh0, (got[k_ie],) = _mm(dp0, w_perm, out_dtype=F32, tm=TM_DX, tn=TN, name="in_even_dx",
                            ride=_sibling_plan([g[k_ie]]))
    pair_sum(k_ie)
    (dx0, dnorm[(0, 0)]), (recv[k_ie],) = _norm_bwd(dh0, x, nrm(0, 0), out_dtype=F32, add=dx1, name="norm_in_bwd_0",
                                                     ride=_chip_plan([sums[k_ie]]))

    g["norm_w"] = jnp.stack([jnp.concatenate([dnorm[(l, k)] for k in range(4)], axis=0) for l in range(DEPTH)])
    vec_parts = _run_plan(_gather_plan([_split_shards(g[n], SHARDED[n]) for n in VECTORS]), "vector_grads_all_gather")
    return loss, dx0, sums, recv, repl_parts, vec_parts


def kernel(x, norm_w, w_in_even, gla_w_a_up, gla_b_a, gla_norm_w, fox_b_f, w_out_even, w_in_odd, rel_bias, conv_w, conv_b, lru_w_a, lru_b_a, lru_w_x, lru_b_x, lru_lambda, w_out_odd, w_mlp_up, w_mlp_down, loss_target, m_norm_w, m_w_in_even, m_gla_w_a_up, m_gla_b_a, m_gla_norm_w, m_fox_b_f, m_w_out_even, m_w_in_odd, m_rel_bias, m_conv_w, m_conv_b, m_lru_w_a, m_lru_b_a, m_lru_w_x, m_lru_b_x, m_lru_lambda, m_w_out_odd, m_w_mlp_up, m_w_mlp_down, v_norm_w, v_w_in_even, v_gla_w_a_up, v_gla_b_a, v_gla_norm_w, v_fox_b_f, v_w_out_even, v_w_in_odd, v_rel_bias, v_conv_w, v_conv_b, v_lru_w_a, v_lru_b_a, v_lru_w_x, v_lru_b_x, v_lru_lambda, v_w_out_odd, v_w_mlp_up, v_w_mlp_down):
    wts = dict(zip(WEIGHTS, (norm_w, w_in_even, gla_w_a_up, gla_b_a, gla_norm_w, fox_b_f, w_out_even, w_in_odd, rel_bias,
                             conv_w, conv_b, lru_w_a, lru_b_a, lru_w_x, lru_b_x, lru_lambda, w_out_odd, w_mlp_up,
                             w_mlp_down)))
    mom = dict(zip(WEIGHTS, (m_norm_w, m_w_in_even, m_gla_w_a_up, m_gla_b_a, m_gla_norm_w, m_fox_b_f, m_w_out_even,
                             m_w_in_odd, m_rel_bias, m_conv_w, m_conv_b, m_lru_w_a, m_lru_b_a, m_lru_w_x, m_lru_b_x,
                             m_lru_lambda, m_w_out_odd, m_w_mlp_up, m_w_mlp_down)))
    var = dict(zip(WEIGHTS, (v_norm_w, v_w_in_even, v_gla_w_a_up, v_gla_b_a, v_gla_norm_w, v_fox_b_f, v_w_out_even,
                             v_w_in_odd, v_rel_bias, v_conv_w, v_conv_b, v_lru_w_a, v_lru_b_a, v_lru_w_x, v_lru_b_x,
                             v_lru_lambda, v_w_out_odd, v_w_mlp_up, v_w_mlp_down)))
    ax, ay, ac = lax.axis_index("x"), lax.axis_index("y"), lax.axis_index("c")
    place = jnp.stack([ac, 2 * ax + ay, 4 * ax + 2 * ay + ac]).astype(jnp.int32)

    shard = {(n, l): (wts[n][l].T if n in TRANSPOSED else wts[n][l]).astype(WIRE_DTYPE) for n, l in MATRIX_BLOCKS}
    loss_blk, dx, sums, recv, repl_parts, vec_parts = _forward_backward(
        x[0], loss_target[0], shard, {n: wts[n] for n in VECTORS}, {n: wts[n] for n in REPLICATED}, place)
    loss = lax.psum(loss_blk[0, 0], ("x", "y", "c"))

    view = lambda n, a: jnp.swapaxes(a, 1, 2) if n in TRANSPOSED else a
    upd = {n: [view(n, o) for o in _adamw_sharded(
        [(sums[(n, l)], [recv[(n, l)]]) for l in range(wts[n].shape[0])], view(n, wts[n]), view(n, mom[n]),
        view(n, var[n]), place, f"adamw_{n}")] for n in MATRICES}
    small = REPLICATED + list(VECTORS)
    upd.update(zip(small, _adamw_small(repl_parts, vec_parts, [wts[n] for n in small], [mom[n] for n in small],
                                       [var[n] for n in small], place)))
    return (loss, dx[None], *[upd[n][kind] for kind in range(4) for n in WEIGHTS])
```

```python
import functools
from typing import Callable, NamedTuple, Optional

import jax
import jax.numpy as jnp
from jax import lax
from jax.experimental import pallas as pl
from jax.experimental.pallas import tpu as pltpu

F32 = jnp.float32
MXU_DTYPE = jnp.bfloat16
ACT_DTYPE = jnp.bfloat16
WIRE_DTYPE = jnp.bfloat16

V7X_VMEM_BYTES = 64 * 1024 * 1024
VMEM_LIMIT = (V7X_VMEM_BYTES * 7) // 8
LANES = 128

D_MODEL = 1024
DEPTH = 2
CHUNK = 64
GROUP_WIDTH = D_MODEL // 2
D_FF = 4 * D_MODEL
NORM_EPS = 1e-6
GLA_HEADS = 4
GLA_DV = GROUP_WIDTH // GLA_HEADS
GLA_DK = GLA_DV // 2
GLA_KW = GLA_HEADS * GLA_DK
GLA_RANK = 16
GLA_GATE_TAU = 16.0
HEAD_DIM = 64
ATT_HEADS = GROUP_WIDTH // HEAD_DIM
CA_LEFT = 8 * CHUNK
REL_CLIP = 128
LRU_BLOCK_DIM = 64
CONV_WIDTH = 4
LRU_C = 8.0
N_DEV = 8

ADAM_LR = 0.001
ADAM_B1 = 0.9
ADAM_B2 = 0.999
ADAM_EPS = 1e-08
ADAM_WD = 0.01
ADAM_STEP = 10

NEG = float(jnp.finfo(jnp.float32).min)
MESH = pl.DeviceIdType.MESH


def _params(*sem):
    return pltpu.CompilerParams(dimension_semantics=sem, vmem_limit_bytes=VMEM_LIMIT)


def _dot(a, b, ca=1, cb=0):
    return lax.dot_general(a.astype(MXU_DTYPE), b.astype(MXU_DTYPE), (((ca,), (cb,)), ((), ())),
                           preferred_element_type=F32)


def _dot_exact(a, b):
    return lax.dot_general(a, b, (((1,), (0,)), ((), ())), precision=lax.Precision.HIGHEST,
                           preferred_element_type=F32)


def _log_sigmoid(x):
    return jnp.minimum(x, 0.0) - jnp.log1p(jnp.exp(-jnp.abs(x)))


def _iota(shape, axis):
    return lax.broadcasted_iota(jnp.int32, shape, axis)


ANY = pl.BlockSpec(memory_space=pl.ANY)
N_CHIPS = 4


class _Plan(NamedTuple):
    ins: list
    outs: list
    sems: list
    start: Callable
    finish: Callable
    relay: Optional[Callable] = None


def _place():
    x, y, c = lax.axis_index("x"), lax.axis_index("y"), lax.axis_index("c")
    return x, y, c, [(1 - x, y), (x, 1 - y), (1 - x, 1 - y)]


def _gather_plan(xs):
    n = len(xs)

    def parts(x_refs, out_refs, sems):
        send_sems, recv_sems, local_sems = sems
        x, y, c, chips = _place()
        me, sibling = (x, y, c), (x, y, 1 - c)

        def rows(a, px, py, pc):
            return out_refs[a].at[4 * px + 2 * py + pc]

        def copy(a, k, block, to, src=None):
            return pltpu.make_async_remote_copy(
                src_ref=rows(a, *block) if src is None else src, dst_ref=rows(a, *block),
                send_sem=send_sems.at[7 * a + k], recv_sem=recv_sems.at[7 * a + k], device_id=to, device_id_type=MESH)

        def own():
            mine = [pltpu.make_async_copy(x_refs[a], rows(a, *me), local_sems.at[a]) for a in range(n)]
            first = []
            for a in range(n):
                first.append(copy(a, 0, me, sibling, src=x_refs[a]))
                first += [copy(a, 1 + j, me, (*chip, c), src=x_refs[a]) for j, chip in enumerate(chips)]
            return mine, first

        return c, me, sibling, chips, copy, own

    def start(x_refs, out_refs, sems):
        mine, first = parts(x_refs, out_refs, sems)[-1]()
        for cp in first + mine:
            cp.start()

    def relay(x_refs, out_refs, sems):
        c, me, sibling, chips, copy, _ = parts(x_refs, out_refs, sems)
        for j, chip in enumerate(chips):
            for a in range(n):
                copy(a, 1 + j, (*chip, c), me).wait_recv()
                copy(a, 4 + j, (*chip, c), sibling).start()

    def finish(x_refs, out_refs, sems):
        c, me, sibling, chips, copy, own = parts(x_refs, out_refs, sems)
        mine, first = own()
        for a in range(n):
            copy(a, 0, sibling, me).wait_recv()
            for j, chip in enumerate(chips):
                copy(a, 4 + j, (*chip, 1 - c), me).wait_recv()
        for cp in first + [copy(a, 4 + j, (*chip, c), sibling) for j, chip in enumerate(chips) for a in range(n)]:
            cp.wait_send()
        for cp in mine:
            cp.wait()

    return _Plan(list(xs), [jax.ShapeDtypeStruct((N_DEV,) + x.shape, x.dtype) for x in xs],
                 [pltpu.SemaphoreType.DMA((7 * n,)), pltpu.SemaphoreType.DMA((7 * n,)), pltpu.SemaphoreType.DMA((n,))],
                 start, finish, relay)


def _exchange_plan(copies_of, ins, outs, per_array):
    n = len(ins)

    def start(in_refs, out_refs, sems):
        for cp in copies_of(in_refs, out_refs, sems):
            cp.start()

    def finish(in_refs, out_refs, sems):
        copies = copies_of(in_refs, out_refs, sems)
        for cp in copies:
            cp.wait_recv()
        for cp in copies:
            cp.wait_send()

    return _Plan(list(ins), outs, [pltpu.SemaphoreType.DMA((per_array * n,)), pltpu.SemaphoreType.DMA((per_array * n,))],
                 start, finish)


def _sibling_plan(gs):
    def copies_of(g_refs, got_refs, sems):
        x, y, c, _ = _place()
        return [pltpu.make_async_remote_copy(
            src_ref=g_refs[a].at[2 * k + (1 - c)], dst_ref=got_refs[a].at[k], send_sem=sems[0].at[N_CHIPS * a + k],
            recv_sem=sems[1].at[N_CHIPS * a + k], device_id=(x, y, 1 - c), device_id_type=MESH)
            for a in range(len(gs)) for k in range(N_CHIPS)]

    return _exchange_plan(copies_of, gs, [jax.ShapeDtypeStruct((N_CHIPS,) + g.shape[1:], g.dtype) for g in gs], N_CHIPS)


def _chip_plan(ss, relations=(0, 1, 2)):
    n_rel = len(relations)

    def copies_of(s_refs, out_refs, sems):
        x, y, c, chips = _place()
        return [pltpu.make_async_remote_copy(
            src_ref=s_refs[a].at[2 * chips[j][0] + chips[j][1]], dst_ref=out_refs[a].at[slot],
            send_sem=sems[0].at[n_rel * a + slot], recv_sem=sems[1].at[n_rel * a + slot],
            device_id=(*chips[j], c), device_id_type=MESH)
            for a in range(len(ss)) for slot, j in enumerate(relations)]

    return _exchange_plan(copies_of, ss, [jax.ShapeDtypeStruct((n_rel,) + s.shape[1:], s.dtype) for s in ss], n_rel)


def _join_plans(*plans):
    def cut(refs, counts):
        at = 0
        for n in counts:
            yield refs[at:at + n]
            at += n

    def each(in_refs, out_refs, sems):
        return zip(plans, cut(in_refs, [len(p.ins) for p in plans]), cut(out_refs, [len(p.outs) for p in plans]),
                   cut(sems, [len(p.sems) for p in plans]))

    def start(*refs):
        for p, i, o, s in each(*refs):
            p.start(i, o, s)

    def relay(*refs):
        for p, i, o, s in each(*refs):
            if p.relay is not None:
                p.relay(i, o, s)

    def finish(*refs):
        for p, i, o, s in each(*refs):
            p.finish(i, o, s)

    return _Plan([a for p in plans for a in p.ins], [a for p in plans for a in p.outs],
                 [a for p in plans for a in p.sems], start, finish, relay)


def _run_plan(plan, name):
    n_in, n_out = len(plan.ins), len(plan.outs)

    def body(*refs):
        args = refs[:n_in], refs[n_in:n_in + n_out], refs[n_in + n_out:]
        plan.start(*args)
        if plan.relay is not None:
            plan.relay(*args)
        plan.finish(*args)

    return pl.pallas_call(body, out_shape=plan.outs, in_specs=[ANY] * n_in, out_specs=[ANY] * n_out,
                          scratch_shapes=plan.sems, name=name)(*plan.ins)


def _pcall(body, ride, *, grid, in_specs, out_specs, out_shape, scratch_shapes=(), semantics, name, prefetch=False):
    n_pre = int(prefetch)

    def build(kernel, ins, outs, shapes, scratch, sem):
        if prefetch:
            return pl.pallas_call(
                kernel, grid_spec=pltpu.PrefetchScalarGridSpec(num_scalar_prefetch=1, grid=grid, in_specs=ins,
                                                               out_specs=outs, scratch_shapes=scratch),
                out_shape=shapes, compiler_params=_params(*sem), name=name)
        return pl.pallas_call(kernel, grid=grid, in_specs=ins, out_specs=outs, out_shape=shapes,
                              scratch_shapes=scratch, compiler_params=_params(*sem), name=name)

    if ride is None:
        return build(body, in_specs, out_specs, out_shape, list(scratch_shapes), semantics)
    single = not isinstance(out_shape, (list, tuple))
    out_specs_l, out_shape_l = ([out_specs], [out_shape]) if single else (list(out_specs), list(out_shape))
    n_in, n_out, n_scr = len(in_specs), len(out_shape_l), len(scratch_shapes)
    r_in, r_out = len(ride.ins), len(ride.outs)

    def riding(*refs):
        pre, refs = refs[:n_pre], refs[n_pre:]
        cuts = [n_in, r_in, n_out, r_out, n_scr]
        groups, at = [], 0
        for width in cuts:
            groups.append(refs[at:at + width])
            at += width
        ins, r_ins, outs, r_outs, scr = groups
        sems = refs[at:]
        first = functools.reduce(jnp.logical_and, [pl.program_id(d) == 0 for d in range(len(grid))])
        last = functools.reduce(jnp.logical_and, [pl.program_id(d) == grid[d] - 1 for d in range(len(grid))])

        @pl.when(first)
        def _():
            ride.start(r_ins, r_outs, sems)

        several_steps = any(n > 1 for n in grid)
        if ride.relay is not None and several_steps:
            @pl.when(last)
            def _():
                ride.relay(r_ins, r_outs, sems)

        body(*pre, *ins, *outs, *scr)

        @pl.when(last)
        def _():
            if ride.relay is not None and not several_steps:
                ride.relay(r_ins, r_outs, sems)
            ride.finish(r_ins, r_outs, sems)

    call = build(riding, list(in_specs) + [ANY] * r_in, out_specs_l + [ANY] * r_out, out_shape_l + list(ride.outs),
                 list(scratch_shapes) + list(ride.sems), ["arbitrary"] * len(grid))

    def run(*args):
        res = call(*args, *ride.ins)
        return (res[0] if single else list(res[:n_out])), list(res[n_out:])

    return run


def _rms(x):
    return x * lax.rsqrt(jnp.mean(x * x, axis=-1, keepdims=True) + NORM_EPS)


def _mm(a, b, *, nt=False, ta=False, out_dtype, tm, tn, a_sqrelu=False, drelu_of=None, b_blocked=False,
        out_blocked=False, a_norm=None, a_norm_bwd=None, res_norm=None, norm_bwd=None, name, ride=None):
    k, m = a.shape if ta else a.shape[::-1]
    if b_blocked:
        assert not nt and b.shape[1] == k and b.shape[2] == tn
        n = b.shape[0] * tn
    else:
        n = b.shape[0] if nt else b.shape[1]
        assert (b.shape[1] if nt else b.shape[0]) == k
    tm, tn = min(tm, m), min(tn, n)
    assert m % tm == 0 and n % tn == 0
    assert (res_norm is None and norm_bwd is None) or tn == n
    assert a_norm is None or a_norm_bwd is None
    n_in = (2 + (drelu_of is not None) + (a_norm is not None) + 2 * (a_norm_bwd is not None)
            + 2 * (res_norm is not None) + 3 * (norm_bwd is not None))

    def body(*refs):
        a_ref, b_ref = refs[0], refs[1]
        extra = list(refs[2:n_in])
        outs = list(refs[n_in:])
        o_ref = outs.pop(0)
        u_ref = extra.pop(0) if drelu_of is not None else None
        if a_norm is not None:
            wn_ref, h_ref, h_scr = extra.pop(0), outs.pop(0), outs.pop()

            @pl.when(pl.program_id(1) == 0)
            def _():
                h = (_rms(a_ref[...]) * wn_ref[...]).astype(ACT_DTYPE)
                h_scr[...] = h
                h_ref[...] = h

            av = h_scr[...]
        elif a_norm_bwd is not None:
            y_ref, wy_ref = extra.pop(0), extra.pop(0)
            dy_ref, dwy_ref, dy_scr = outs.pop(0), outs.pop(0), outs.pop()
            first_rows = pl.program_id(0) == 0

            @pl.when(pl.program_id(1) == 0)
            def _():
                yv, up = y_ref[...], a_ref[...]
                rstd = lax.rsqrt(jnp.mean(yv * yv, axis=-1, keepdims=True) + NORM_EPS)
                yhat = yv * rstd
                g = up * wy_ref[...]
                dy = (rstd * (g - yhat * jnp.mean(g * yhat, axis=-1, keepdims=True))).astype(ACT_DTYPE)
                dy_scr[...] = dy
                dy_ref[...] = dy

                @pl.when(first_rows)
                def _():
                    dwy_ref[...] = jnp.zeros_like(dwy_ref)

                dwy_ref[...] += jnp.sum(up * yhat, axis=0, keepdims=True)

            av = dy_scr[...]
        else:
            av = a_ref[...]
        if a_sqrelu:
            av = jnp.square(jnp.maximum(av.astype(F32), 0.0))
        acc = _dot(av, b_ref[...], 0 if ta else 1, 1 if nt else 0)
        if u_ref is not None:
            acc = acc * (2.0 * jnp.maximum(u_ref[...].astype(F32), 0.0))
        if norm_bwd is not None:
            x_ref, wb_ref, add_ref = extra
            dw_ref = outs[0]
            xv = x_ref[...]
            rstd = lax.rsqrt(jnp.mean(xv * xv, axis=-1, keepdims=True) + NORM_EPS)
            xhat = xv * rstd
            g = acc * wb_ref[...]
            o_ref[...] = rstd * (g - xhat * jnp.mean(g * xhat, axis=-1, keepdims=True)) + add_ref[...]

            @pl.when(pl.program_id(0) == 0)
            def _():
                dw_ref[...] = jnp.zeros_like(dw_ref)

            dw_ref[...] += jnp.sum(acc * xhat, axis=0, keepdims=True)
            return
        o_ref[...] = acc.astype(out_dtype)
        if res_norm is not None:
            res_ref, wr_ref = extra
            outs[0][...] = res_ref[...] + _rms(acc) * wr_ref[...]

    if b_blocked:
        b_spec = pl.BlockSpec((None, k, tn), lambda i, j: (j, 0, 0))
    elif nt:
        b_spec = pl.BlockSpec((tn, k), lambda i, j: (j, 0))
    else:
        b_spec = pl.BlockSpec((k, tn), lambda i, j: (0, j))
    a_spec = pl.BlockSpec((k, tm), lambda i, j: (0, i)) if ta else pl.BlockSpec((tm, k), lambda i, j: (i, 0))
    in_specs = [a_spec, b_spec]
    args = [a, b]
    if drelu_of is not None:
        in_specs.append(pl.BlockSpec((tm, tn), lambda i, j: (i, j)))
        args.append(drelu_of)
    if out_blocked:
        out_specs = [pl.BlockSpec((None, tm, tn), lambda i, j: (j, i, 0))]
        out_shape = [jax.ShapeDtypeStruct((n // tn, m, tn), out_dtype)]
    else:
        out_specs = [pl.BlockSpec((tm, tn), lambda i, j: (i, j))]
        out_shape = [jax.ShapeDtypeStruct((m, n), out_dtype)]
    scratch = []
    if a_norm is not None:
        assert not ta
        in_specs.append(pl.BlockSpec((1, k), lambda i, j: (0, 0)))
        args.append(a_norm)
        out_specs.append(pl.BlockSpec((tm, k), lambda i, j: (i, 0)))
        out_shape.append(jax.ShapeDtypeStruct((m, k), ACT_DTYPE))
        scratch.append(pltpu.VMEM((tm, k), ACT_DTYPE))
    if a_norm_bwd is not None:
        assert not ta
        in_specs += [pl.BlockSpec((tm, k), lambda i, j: (i, 0)), pl.BlockSpec((1, k), lambda i, j: (0, 0))]
        args += list(a_norm_bwd)
        out_specs += [pl.BlockSpec((tm, k), lambda i, j: (i, 0)), pl.BlockSpec((1, k), lambda i, j: (0, 0))]
        out_shape += [jax.ShapeDtypeStruct((m, k), ACT_DTYPE), jax.ShapeDtypeStruct((1, k), F32)]
        scratch.append(pltpu.VMEM((tm, k), ACT_DTYPE))
    if res_norm is not None:
        in_specs += [pl.BlockSpec((tm, n), lambda i, j: (i, 0)), pl.BlockSpec((1, n), lambda i, j: (0, 0))]
        args += list(res_norm)
        out_specs.append(pl.BlockSpec((tm, n), lambda i, j: (i, 0)))
        out_shape.append(jax.ShapeDtypeStruct((m, n), F32))
    if norm_bwd is not None:
        rows = pl.BlockSpec((tm, n), lambda i, j: (i, 0))
        in_specs += [rows, pl.BlockSpec((1, n), lambda i, j: (0, 0)), rows]
        args += list(norm_bwd)
        out_specs.append(pl.BlockSpec((1, n), lambda i, j: (0, 0)))
        out_shape.append(jax.ShapeDtypeStruct((1, n), F32))
    single = len(out_shape) == 1
    return _pcall(body, ride, grid=(m // tm, n // tn), in_specs=in_specs,
                  out_specs=out_specs[0] if single else out_specs, out_shape=out_shape[0] if single else out_shape,
                  scratch_shapes=scratch, semantics=("arbitrary", "arbitrary"), name=name)(*args)


ROW_TILE = 512
TM_FWD, TM_DX, TM_DW, TN = 2048, 1024, 1024, 512


def _norm_bwd(dy, x, w, *, out_dtype, add=None, name, ride=None):
    t, d = x.shape

    def body(*refs):
        dy_ref, x_ref, w_ref = refs[0], refs[1], refs[2]
        dx_ref, dw_ref = refs[-2], refs[-1]
        xv = x_ref[...]
        rstd = lax.rsqrt(jnp.mean(xv * xv, axis=-1, keepdims=True) + NORM_EPS)
        xhat = xv * rstd
        dyv = dy_ref[...].astype(F32)
        g = dyv * w_ref[...]
        dx = rstd * (g - xhat * jnp.mean(g * xhat, axis=-1, keepdims=True))
        if add is not None:
            dx = dx + refs[3][...]
        dx_ref[...] = dx.astype(out_dtype)

        @pl.when(pl.program_id(0) == 0)
        def _():
            dw_ref[...] = jnp.zeros_like(dw_ref)

        dw_ref[...] += jnp.sum(dyv * xhat, axis=0, keepdims=True)

    row = pl.BlockSpec((ROW_TILE, d), lambda i: (i, 0))
    vec = pl.BlockSpec((1, d), lambda i: (0, 0))
    in_specs = [row, row, vec] + ([row] if add is not None else [])
    args = [dy, x, w] + ([add] if add is not None else [])
    return _pcall(body, ride, grid=(t // ROW_TILE,), in_specs=in_specs, out_specs=[row, vec],
                  out_shape=[jax.ShapeDtypeStruct((t, d), out_dtype), jax.ShapeDtypeStruct((1, d), F32)],
                  semantics=("arbitrary",), name=name)(*args)


def _loss_fwd_bwd(y, target):
    t, d = y.shape

    def body(y_ref, t_ref, l_ref, dy_ref):
        diff = y_ref[...] - t_ref[...]
        dy_ref[...] = diff * (1.0 / d)

        @pl.when(pl.program_id(0) == 0)
        def _():
            l_ref[...] = jnp.zeros_like(l_ref)

        l_ref[...] += 0.5 * jnp.sum(jnp.mean(diff * diff, axis=-1, keepdims=True), axis=0, keepdims=True)

    row = pl.BlockSpec((ROW_TILE, d), lambda i: (i, 0))
    return pl.pallas_call(body, grid=(t // ROW_TILE,), in_specs=[row, row],
                          out_specs=[pl.BlockSpec((8, LANES), lambda i: (0, 0)), row],
                          out_shape=[jax.ShapeDtypeStruct((8, LANES), F32), jax.ShapeDtypeStruct((t, d), F32)],
                          compiler_params=_params("arbitrary"), name="loss")(y, target)


GLA_STATE = (GLA_HEADS * GLA_DV, GLA_KW)


def _gla_specs(chunk_of):
    rows = lambda width, col: pl.BlockSpec((CHUNK, width), lambda i: (chunk_of(i), col))
    const = lambda r, c: pl.BlockSpec((r, c), lambda i: (0, 0))
    return [rows(GLA_KW, 0),
            rows(GLA_KW, 1),
            rows(GROUP_WIDTH, 1),
            rows(GROUP_WIDTH, 0),
            rows(LANES, 4),
            const(LANES, GLA_KW),
            const(1, GLA_KW),
            const(1, GROUP_WIDTH)]


def _gla_chunk(q_ref, k_ref, v_ref, a_ref, wup_ref, ba_ref):
    z = _dot(a_ref[...], wup_ref[...]) + ba_ref[...]
    tri = (_iota((CHUNK, CHUNK), 1) <= _iota((CHUNK, CHUNK), 0)).astype(F32)
    cum = _dot_exact(tri, _log_sigmoid(z) * (1.0 / GLA_GATE_TAU))
    tot = cum[CHUNK - 1:CHUNK, :]
    e = jnp.exp(tot - cum)
    return (z, e, jnp.exp(tot), k_ref[...].astype(F32) * e, q_ref[...].astype(F32) * (GLA_DK ** -0.5),
            v_ref[...].astype(F32))


def _gla_head_mask():
    return _iota(GLA_STATE, 0) // GLA_DV == _iota(GLA_STATE, 1) // GLA_DK


def _gla_fwd(pmm, pel, w_up, b_a, gnorm_w, ride=None):
    t = pmm.shape[0]
    nc = t // CHUNK

    def body(q_ref, k_ref, v_ref, r_ref, a_ref, wup_ref, ba_ref, gw_ref, o_ref, st_ref, m_scr):
        @pl.when(pl.program_id(0) == 0)
        def _():
            m_scr[...] = jnp.zeros_like(m_scr)

        _, _, decay, kd, qs, vv = _gla_chunk(q_ref, k_ref, v_ref, a_ref, wup_ref, ba_ref)
        m = m_scr[...] * decay + jnp.where(_gla_head_mask(), _dot(vv, kd, 0, 0), 0.0)
        m_scr[...] = m
        st_ref[...] = m
        o = _dot(qs, m, 1, 1)
        rr = r_ref[...]
        gate = rr * jax.nn.sigmoid(rr) * gw_ref[...]
        for h in range(GLA_HEADS):
            vs = slice(h * GLA_DV, (h + 1) * GLA_DV)
            oh = o[:, vs]
            y = oh * lax.rsqrt(jnp.mean(oh * oh, axis=-1, keepdims=True) + NORM_EPS)
            o_ref[:, vs] = (y * gate[:, vs]).astype(o_ref.dtype)

    return _pcall(
        body, ride, grid=(nc,), in_specs=_gla_specs(lambda i: i),
        out_specs=[pl.BlockSpec((CHUNK, GROUP_WIDTH), lambda i: (i, 0)),
                   pl.BlockSpec((None,) + GLA_STATE, lambda i: (i, 0, 0))],
        out_shape=[jax.ShapeDtypeStruct((t, GROUP_WIDTH), ACT_DTYPE), jax.ShapeDtypeStruct((nc,) + GLA_STATE, F32)],
        scratch_shapes=[pltpu.VMEM(GLA_STATE, F32)],
        semantics=("arbitrary",), name="gla_fwd")(pmm, pmm, pmm, pel, pel, w_up, b_a, gnorm_w)


def _gla_bwd(pmm, pel, w_up, b_a, gnorm_w, states, dmix, ride=None):
    t = pmm.shape[0]
    nc = t // CHUNK
    scale = GLA_DK ** -0.5

    def body(q_ref, k_ref, v_ref, r_ref, a_ref, wup_ref, ba_ref, gw_ref, st_ref, prev_ref, do_ref,
             dq_ref, dk_ref, dv_ref, dr_ref, da_ref, dwup_ref, dba_ref, dgw_ref, dm_scr):
        step = pl.program_id(0)

        @pl.when(step == 0)
        def _():
            dm_scr[...] = jnp.zeros_like(dm_scr)
            dwup_ref[...] = jnp.zeros_like(dwup_ref)
            dba_ref[...] = jnp.zeros_like(dba_ref)
            dgw_ref[...] = jnp.zeros_like(dgw_ref)

        z, e, decay, kd, qs, vv = _gla_chunk(q_ref, k_ref, v_ref, a_ref, wup_ref, ba_ref)
        m = st_ref[...]
        m_prev = prev_ref[...] * (step < nc - 1).astype(F32)
        rr, dout, gw = r_ref[...], do_ref[...], gw_ref[...]
        sig = jax.nn.sigmoid(rr)
        silu = rr * sig
        dsilu = sig * (1.0 + rr * (1.0 - sig))
        o = _dot(qs, m, 1, 1)
        d_o, dgw = [], []
        for h in range(GLA_HEADS):
            vs = slice(h * GLA_DV, (h + 1) * GLA_DV)
            oh, dg = o[:, vs], dout[:, vs]
            rstd = lax.rsqrt(jnp.mean(oh * oh, axis=-1, keepdims=True) + NORM_EPS)
            y = oh * rstd
            dgw.append(jnp.sum(dg * y * silu[:, vs], axis=0, keepdims=True))
            dr_ref[:, vs] = (dg * y * gw[:, vs] * dsilu[:, vs]).astype(dr_ref.dtype)
            dy = dg * gw[:, vs] * silu[:, vs]
            d_o.append(rstd * (dy - y * jnp.mean(dy * y, axis=-1, keepdims=True)))
        d_o = jnp.concatenate(d_o, axis=1)
        dgw_ref[...] += jnp.concatenate(dgw, axis=1)
        dq_ref[...] = (_dot(d_o, m) * scale).astype(dq_ref.dtype)
        dm = dm_scr[...] + jnp.where(_gla_head_mask(), _dot(d_o, qs, 0, 0), 0.0)
        dv_ref[...] = _dot(kd, dm, 1, 1).astype(dv_ref.dtype)
        dkd = _dot(vv, dm)
        dk_ref[...] = (dkd * e).astype(dk_ref.dtype)
        dm_scr[...] = dm * decay
        tri_strict = (_iota((CHUNK, CHUNK), 1) < _iota((CHUNK, CHUNK), 0)).astype(F32)
        dla = jnp.sum(dm * m_prev, axis=0, keepdims=True) * decay + _dot_exact(tri_strict, dkd * kd)
        dz = dla * jax.nn.sigmoid(-z) * (1.0 / GLA_GATE_TAU)
        da_ref[...] = _dot(dz, wup_ref[...], 1, 1).astype(da_ref.dtype)
        dwup_ref[...] += _dot(a_ref[...], dz, 0, 0)
        dba_ref[...] += jnp.sum(dz, axis=0, keepdims=True)

    chunk_of = lambda i: nc - 1 - i
    in_specs = _gla_specs(chunk_of) + [
        pl.BlockSpec((None,) + GLA_STATE, lambda i: (chunk_of(i), 0, 0)),
        pl.BlockSpec((None,) + GLA_STATE, lambda i: (jnp.maximum(chunk_of(i) - 1, 0), 0, 0)),
        pl.BlockSpec((CHUNK, GROUP_WIDTH), lambda i: (chunk_of(i), 0))]
    rows = lambda width: pl.BlockSpec((CHUNK, width), lambda i: (chunk_of(i), 0))
    const = lambda r, c: pl.BlockSpec((r, c), lambda i: (0, 0))
    return _pcall(
        body, ride, grid=(nc,), in_specs=in_specs,
        out_specs=[rows(GLA_KW), rows(GLA_KW), rows(GROUP_WIDTH), rows(GROUP_WIDTH), rows(LANES),
                   const(LANES, GLA_KW), const(1, GLA_KW), const(1, GROUP_WIDTH)],
        out_shape=[jax.ShapeDtypeStruct((t, GLA_KW), ACT_DTYPE), jax.ShapeDtypeStruct((t, GLA_KW), ACT_DTYPE),
                   jax.ShapeDtypeStruct((t, GROUP_WIDTH), ACT_DTYPE), jax.ShapeDtypeStruct((t, GROUP_WIDTH), ACT_DTYPE),
                   jax.ShapeDtypeStruct((t, LANES), ACT_DTYPE), jax.ShapeDtypeStruct((LANES, GLA_KW), F32),
                   jax.ShapeDtypeStruct((1, GLA_KW), F32), jax.ShapeDtypeStruct((1, GROUP_WIDTH), F32)],
        scratch_shapes=[pltpu.VMEM(GLA_STATE, F32)],
        semantics=("arbitrary",), name="gla_bwd")(
            pmm, pmm, pmm, pel, pel, w_up, b_a, gnorm_w, states, states, dmix)


CUM_BLOCK = 256


def _fox_gate_fwd(pel, b_f):
    t = pel.shape[0]
    nb = t // CUM_BLOCK

    def body(f_ref, b_ref, cum_ref, cum_t_ref):
        tri = (_iota((CUM_BLOCK, CUM_BLOCK), 1) <= _iota((CUM_BLOCK, CUM_BLOCK), 0)).astype(F32)
        carry = jnp.zeros((1, LANES), F32)
        for blk in range(nb):
            rows = slice(blk * CUM_BLOCK, (blk + 1) * CUM_BLOCK)
            cum = _dot_exact(tri, _log_sigmoid(f_ref[rows, :] + b_ref[...])) + carry
            cum_ref[rows, :] = cum
            cum_t_ref[blk] = cum.T[:ATT_HEADS, :]
            carry = cum[CUM_BLOCK - 1:CUM_BLOCK, :]

    return pl.pallas_call(
        body, grid=(1,),
        in_specs=[pl.BlockSpec((t, LANES), lambda i: (0, 5)), pl.BlockSpec((1, LANES), lambda i: (0, 0))],
        out_specs=[pl.BlockSpec((t, LANES), lambda i: (0, 0)),
                   pl.BlockSpec((nb, ATT_HEADS, CUM_BLOCK), lambda i: (0, 0, 0))],
        out_shape=[jax.ShapeDtypeStruct((t, LANES), F32), jax.ShapeDtypeStruct((nb, ATT_HEADS, CUM_BLOCK), F32)],
        compiler_params=_params("arbitrary"), name="fox_gate_fwd")(pel, b_f)


def _fox_gate_bwd(pel, b_f, dcum_t, dcum_q):
    t = pel.shape[0]
    nb = t // CUM_BLOCK

    def body(f_ref, b_ref, dct_ref, dcq_ref, df_ref, db_ref):
        tri_up = (_iota((CUM_BLOCK, CUM_BLOCK), 1) >= _iota((CUM_BLOCK, CUM_BLOCK), 0)).astype(F32)
        carry = jnp.zeros((1, LANES), F32)
        db = jnp.zeros((1, LANES), F32)
        for blk in reversed(range(nb)):
            rows = slice(blk * CUM_BLOCK, (blk + 1) * CUM_BLOCK)
            query_side = sum(dcq_ref[pair, rows, :] for pair in range(dcq_ref.shape[0]))
            dls = _dot_exact(tri_up, dct_ref[blk].T + query_side) + carry
            carry = dls[0:1, :]
            df = dls * jax.nn.sigmoid(-(f_ref[rows, :] + b_ref[...]))
            df_ref[rows, :] = df.astype(df_ref.dtype)
            db = db + jnp.sum(df, axis=0, keepdims=True)
        db_ref[...] = db

    return pl.pallas_call(
        body, grid=(1,),
        in_specs=[pl.BlockSpec((t, LANES), lambda i: (0, 5)), pl.BlockSpec((1, LANES), lambda i: (0, 0)),
                  pl.BlockSpec((nb, LANES, CUM_BLOCK), lambda i: (0, 0, 0)),
                  pl.BlockSpec((dcum_q.shape[0], t, LANES), lambda i: (0, 0, 0))],
        out_specs=[pl.BlockSpec((t, LANES), lambda i: (0, 0)), pl.BlockSpec((1, LANES), lambda i: (0, 0))],
        out_shape=[jax.ShapeDtypeStruct((t, LANES), ACT_DTYPE), jax.ShapeDtypeStruct((1, LANES), F32)],
        compiler_params=_params("arbitrary"), name="fox_gate_bwd")(pel, b_f, dcum_t, dcum_q)


FOX_Q_BLOCK = 256


assert FOX_Q_BLOCK == CUM_BLOCK
FOX_KEY_STEP = 512


def _fox_scores(q_ref, k_ref, cum_ref, cum_t_ref, h, i):
    hs = slice(h * HEAD_DIM, (h + 1) * HEAD_DIM)
    nb = cum_t_ref.shape[0]
    key_gate = jnp.concatenate([cum_t_ref[kb, h:h + 1, :] for kb in range(nb)], axis=1)
    s = _dot(q_ref[:, hs], k_ref[:, hs], 1, 1) * (HEAD_DIM ** -0.5) + (cum_ref[:, h:h + 1] - key_gate)
    shape = (FOX_Q_BLOCK, nb * FOX_Q_BLOCK)
    return jnp.where(_iota(shape, 1) <= i * FOX_Q_BLOCK + _iota(shape, 0), s, NEG)


def _fox_specs(t):
    bq, nb = FOX_Q_BLOCK, t // FOX_Q_BLOCK
    return [pl.BlockSpec((bq, GROUP_WIDTH), lambda i: (i, 2)), pl.BlockSpec((t, GROUP_WIDTH), lambda i: (0, 3)),
            pl.BlockSpec((t, GROUP_WIDTH), lambda i: (0, 4)), pl.BlockSpec((bq, LANES), lambda i: (i, 0)),
            pl.BlockSpec((nb, ATT_HEADS, bq), lambda i: (0, 0, 0))]


def _fox_fwd(pmm, cum, cum_t, ride=None):
    t = pmm.shape[0]
    bq = FOX_Q_BLOCK

    def body(q_ref, k_ref, v_ref, cum_ref, cum_t_ref, o_ref, lse_ref):
        i = pl.program_id(0)
        lse_ref[...] = jnp.zeros_like(lse_ref)
        for h in range(ATT_HEADS):
            hs = slice(h * HEAD_DIM, (h + 1) * HEAD_DIM)
            s = _fox_scores(q_ref, k_ref, cum_ref, cum_t_ref, h, i)
            m = jnp.max(s, axis=-1, keepdims=True)
            p = jnp.exp(s - m)
            l = jnp.sum(p, axis=-1, keepdims=True)
            o_ref[:, hs] = (_dot(p, v_ref[:, hs]) / l).astype(o_ref.dtype)
            lse_ref[:, h:h + 1] = m + jnp.log(l)

    return _pcall(
        body, ride, grid=(t // bq,), in_specs=_fox_specs(t),
        out_specs=[pl.BlockSpec((bq, GROUP_WIDTH), lambda i: (i, 0)), pl.BlockSpec((bq, LANES), lambda i: (i, 0))],
        out_shape=[jax.ShapeDtypeStruct((t, GROUP_WIDTH), ACT_DTYPE), jax.ShapeDtypeStruct((t, LANES), F32)],
        semantics=("parallel",), name="fox_fwd")(pmm, pmm, pmm, cum, cum_t)


def _fox_bwd(pmm, cum, cum_t, lse, dmix, ride=None):
    t = pmm.shape[0]
    bq, nb = FOX_Q_BLOCK, t // FOX_Q_BLOCK
    pairs, per_pair = ATT_HEADS // 2, LANES // HEAD_DIM
    scale = HEAD_DIM ** -0.5

    def body(q_ref, k_ref, v_ref, cum_ref, cum_t_ref, lse_ref, do_ref, dq_ref, dk_ref, dv_ref, dct_ref, dcq_ref):
        g, i = pl.program_id(0), pl.program_id(1)

        @pl.when(i == 0)
        def _():
            dk_ref[...] = jnp.zeros_like(dk_ref)
            dv_ref[...] = jnp.zeros_like(dv_ref)

        @pl.when((i == 0) & (g == 0))
        def _():
            dct_ref[...] = jnp.zeros_like(dct_ref)

        lane = _iota((1, LANES), 1)

        def run(n):
            causal = _iota((bq, n), 1) <= i * bq + _iota((bq, n), 0)
            dcq = jnp.zeros((bq, LANES), F32)
            for hh in range(per_pair):
                h = per_pair * g + hh
                hs = slice(hh * HEAD_DIM, (hh + 1) * HEAD_DIM)
                pick = (lane == h).astype(F32)
                cq = jnp.sum(cum_ref[...] * pick, axis=1, keepdims=True)
                lse_h = jnp.sum(lse_ref[...] * pick, axis=1, keepdims=True)
                key_gate = jnp.concatenate([cum_t_ref[kb, pl.ds(h, 1), :] for kb in range(n // bq)], axis=1)
                s = _dot(q_ref[:, hs], k_ref[:n, hs], 1, 1) * scale + (cq - key_gate)
                p = jnp.exp(jnp.where(causal, s, NEG) - lse_h)
                do = do_ref[:, hs]
                dp = _dot(do, v_ref[:n, hs], 1, 1)
                ds = p * (dp - jnp.sum(p * dp, axis=-1, keepdims=True))
                dq_ref[:, hs] = (_dot(ds, k_ref[:n, hs]) * scale).astype(dq_ref.dtype)
                dk_ref[:n, hs] += _dot(ds, q_ref[:, hs], 0, 0) * scale
                dv_ref[:n, hs] += _dot(p, do, 0, 0)
                key_side = -jnp.sum(ds, axis=0, keepdims=True)
                for kb in range(n // bq):
                    dct_ref[kb, pl.ds(h, 1), :] += key_side[:, kb * bq:(kb + 1) * bq]
                dcq = dcq + jnp.sum(ds, axis=1, keepdims=True) * pick
            dcq_ref[...] = dcq

        for kx in range(t // FOX_KEY_STEP):
            pl.when(i // (FOX_KEY_STEP // bq) == kx)(functools.partial(run, (kx + 1) * FOX_KEY_STEP))

    cols = lambda first: pl.BlockSpec((bq, LANES), lambda g, i: (i, first + g))
    keys = lambda first: pl.BlockSpec((t, LANES), lambda g, i: (0, first + g))
    per_head = pl.BlockSpec((bq, LANES), lambda g, i: (i, 0))
    fox_q, fox_k, fox_v = (GROUP_WIDTH * n // LANES for n in (2, 3, 4))
    return _pcall(
        body, ride, grid=(pairs, t // bq),
        in_specs=[cols(fox_q), keys(fox_k), keys(fox_v), per_head,
                  pl.BlockSpec((nb, ATT_HEADS, bq), lambda g, i: (0, 0, 0)), per_head, cols(GROUP_WIDTH // LANES)],
        out_specs=[cols(0), keys(0), keys(0), pl.BlockSpec((nb, LANES, bq), lambda g, i: (0, 0, 0)),
                   pl.BlockSpec((None, bq, LANES), lambda g, i: (g, i, 0))],
        out_shape=[jax.ShapeDtypeStruct((t, GROUP_WIDTH), ACT_DTYPE), jax.ShapeDtypeStruct((t, GROUP_WIDTH), F32),
                   jax.ShapeDtypeStruct((t, GROUP_WIDTH), F32), jax.ShapeDtypeStruct((nb, LANES, bq), F32),
                   jax.ShapeDtypeStruct((pairs, t, LANES), F32)],
        semantics=("arbitrary", "arbitrary"), name="fox_bwd")(pmm, pmm, pmm, cum, cum_t, lse, dmix)


CA_Q_BLOCK = 4 * CHUNK
CA_WINDOW = CA_Q_BLOCK + CA_LEFT
CA_BASE = 1024


def _ca_bias_base(rel_bias):
    n = rel_bias.shape[0]
    flat = CA_Q_BLOCK + CA_LEFT - REL_CLIP
    tail = CA_BASE - flat - (2 * REL_CLIP + 1)
    return jnp.concatenate([jnp.broadcast_to(rel_bias[:, 2 * REL_CLIP:], (n, flat)), rel_bias[:, ::-1],
                            jnp.broadcast_to(rel_bias[:, :1], (n, tail))], axis=1)


def _ca_bias_base_grad(dbase):
    flat = CA_Q_BLOCK + CA_LEFT - REL_CLIP
    mid = dbase[:, flat:flat + 2 * REL_CLIP + 1][:, ::-1]
    lo = jnp.sum(dbase[:, flat + 2 * REL_CLIP + 1:], axis=1, keepdims=True)
    hi = jnp.sum(dbase[:, :flat], axis=1, keepdims=True)
    pad = jnp.zeros((dbase.shape[0], 2 * REL_CLIP - 1), F32)
    return mid + jnp.concatenate([lo, pad, hi], axis=1)


def _ca_mask(i):
    r, j = _iota((CA_Q_BLOCK, CA_WINDOW), 0), _iota((CA_Q_BLOCK, CA_WINDOW), 1)
    rc, jc = r // CHUNK, j // CHUNK
    return (jc >= rc) & (jc <= rc + CA_LEFT // CHUNK) & (i * CA_Q_BLOCK + j >= CA_LEFT)


def _ca_fill_bias(i, base_ref, bias_scr):
    @pl.when(i == 0)
    def _():
        for h in range(ATT_HEADS):
            rows = jnp.broadcast_to(base_ref[h:h + 1, :], (CA_Q_BLOCK, CA_BASE))
            bias_scr[h] = pltpu.roll(rows, CA_BASE - CA_Q_BLOCK, 1, stride=1, stride_axis=0)[:, :CA_WINDOW]


def _ca_scores(q_ref, kp_ref, bias_scr, win, h, mask):
    hs = slice(h * HEAD_DIM, (h + 1) * HEAD_DIM)
    s = _dot(q_ref[:, hs], kp_ref[win, hs], 1, 1) * (HEAD_DIM ** -0.5)
    return jnp.where(mask, s + bias_scr[h], NEG)


CA_BIAS_SCRATCH = pltpu.VMEM((ATT_HEADS, CA_Q_BLOCK, CA_WINDOW), F32)


def _ca_fwd(pmm, kp, vp, base, ride=None):
    t = pmm.shape[0]

    def body(q_ref, kp_ref, vp_ref, base_ref, o_ref, lse_ref, bias_scr):
        i = pl.program_id(0)
        _ca_fill_bias(i, base_ref, bias_scr)
        win = pl.ds(pl.multiple_of(i * CA_Q_BLOCK, CA_Q_BLOCK), CA_WINDOW)
        mask = _ca_mask(i)
        lse_ref[...] = jnp.zeros_like(lse_ref)
        for h in range(ATT_HEADS):
            hs = slice(h * HEAD_DIM, (h + 1) * HEAD_DIM)
            s = _ca_scores(q_ref, kp_ref, bias_scr, win, h, mask)
            m = jnp.max(s, axis=-1, keepdims=True)
            p = jnp.exp(s - m)
            l = jnp.sum(p, axis=-1, keepdims=True)
            o_ref[:, hs] = (_dot(p, vp_ref[win, hs]) / l).astype(o_ref.dtype)
            lse_ref[:, h:h + 1] = m + jnp.log(l)

    padded = pl.BlockSpec((t + CA_LEFT, GROUP_WIDTH), lambda i: (0, 0))
    return _pcall(
        body, ride, grid=(t // CA_Q_BLOCK,),
        in_specs=[pl.BlockSpec((CA_Q_BLOCK, GROUP_WIDTH), lambda i: (i, 0)), padded, padded,
                  pl.BlockSpec((ATT_HEADS, CA_BASE), lambda i: (0, 0))],
        out_specs=[pl.BlockSpec((CA_Q_BLOCK, GROUP_WIDTH), lambda i: (i, 0)),
                   pl.BlockSpec((CA_Q_BLOCK, LANES), lambda i: (i, 0))],
        out_shape=[jax.ShapeDtypeStruct((t, GROUP_WIDTH), ACT_DTYPE), jax.ShapeDtypeStruct((t, LANES), F32)],
        scratch_shapes=[CA_BIAS_SCRATCH], semantics=("arbitrary",), name="ca_fwd")(pmm, kp, vp, base)


def _ca_bwd(pmm, kp, vp, base, lse, dmix, ride=None):
    t = pmm.shape[0]
    scale = HEAD_DIM ** -0.5

    def body(q_ref, kp_ref, vp_ref, base_ref, lse_ref, do_ref, dq_ref, dkp_ref, dvp_ref, dbase_ref, bias_scr):
        i = pl.program_id(0)
        _ca_fill_bias(i, base_ref, bias_scr)

        @pl.when(i == 0)
        def _():
            dkp_ref[...] = jnp.zeros_like(dkp_ref)
            dvp_ref[...] = jnp.zeros_like(dvp_ref)
            dbase_ref[...] = jnp.zeros_like(dbase_ref)

        win = pl.ds(pl.multiple_of(i * CA_Q_BLOCK, CA_Q_BLOCK), CA_WINDOW)
        mask = _ca_mask(i)
        flip = (_iota((CA_Q_BLOCK, CA_Q_BLOCK), 0) + _iota((CA_Q_BLOCK, CA_Q_BLOCK), 1) == CA_Q_BLOCK - 1).astype(F32)
        for h in range(ATT_HEADS):
            hs = slice(h * HEAD_DIM, (h + 1) * HEAD_DIM)
            s = _ca_scores(q_ref, kp_ref, bias_scr, win, h, mask)
            p = jnp.exp(s - lse_ref[:, h:h + 1])
            do = do_ref[:, hs]
            dp = _dot(do, vp_ref[win, hs], 1, 1)
            ds = p * (dp - jnp.sum(p * dp, axis=-1, keepdims=True))
            dq_ref[:, hs] = (_dot(ds, kp_ref[win, hs]) * scale).astype(dq_ref.dtype)
            dkp_ref[win, hs] += _dot(ds, q_ref[:, hs], 0, 0) * scale
            dvp_ref[win, hs] += _dot(p, do, 0, 0)
            rev = jnp.concatenate([_dot(flip, ds), jnp.zeros((CA_Q_BLOCK, CA_BASE - CA_WINDOW), F32)], axis=1)
            lined = pltpu.roll(rev, 1, 1, stride=1, stride_axis=0)
            dbase_ref[h:h + 1, :] += jnp.sum(lined, axis=0, keepdims=True)

    padded = pl.BlockSpec((t + CA_LEFT, GROUP_WIDTH), lambda i: (0, 0))
    return _pcall(
        body, ride, grid=(t // CA_Q_BLOCK,),
        in_specs=[pl.BlockSpec((CA_Q_BLOCK, GROUP_WIDTH), lambda i: (i, 0)), padded, padded,
                  pl.BlockSpec((ATT_HEADS, CA_BASE), lambda i: (0, 0)),
                  pl.BlockSpec((CA_Q_BLOCK, LANES), lambda i: (i, 0)),
                  pl.BlockSpec((CA_Q_BLOCK, GROUP_WIDTH), lambda i: (i, 0))],
        out_specs=[pl.BlockSpec((CA_Q_BLOCK, GROUP_WIDTH), lambda i: (i, 0)), padded, padded,
                   pl.BlockSpec((ATT_HEADS, CA_BASE), lambda i: (0, 0))],
        out_shape=[jax.ShapeDtypeStruct((t, GROUP_WIDTH), ACT_DTYPE),
                   jax.ShapeDtypeStruct((t + CA_LEFT, GROUP_WIDTH), F32),
                   jax.ShapeDtypeStruct((t + CA_LEFT, GROUP_WIDTH), F32),
                   jax.ShapeDtypeStruct((ATT_HEADS, CA_BASE), F32)],
        scratch_shapes=[CA_BIAS_SCRATCH], semantics=("arbitrary",), name="ca_bwd")(pmm, kp, vp, base, lse, dmix)


GELU_C = 0.7978845608028654
GELU_A = 0.044715


def _shift_down(v, k, fill, period=None):
    rows = _iota(v.shape, 0)
    rows = rows if period is None else rows & (period - 1)
    return jnp.where(rows >= k, pltpu.roll(v, k, 0), fill)


def _shift_up(v, k, fill, period=None):
    t = v.shape[0]
    rows = _iota(v.shape, 0)
    rows, length = (rows, t) if period is None else (rows & (period - 1), period)
    return jnp.where(rows < length - k, pltpu.roll(v, t - k, 0), fill)


LRU_SCAN_BLOCK = 256


def _linear_scan(a, b, reverse=False):
    shift = _shift_up if reverse else _shift_down
    k = 1
    while k < LRU_SCAN_BLOCK:
        b = a * shift(b, k, 0.0, LRU_SCAN_BLOCK) + b
        a = a * shift(a, k, 1.0, LRU_SCAN_BLOCK)
        k *= 2
    nb = a.shape[0] // LRU_SCAN_BLOCK
    carry = jnp.zeros((1, a.shape[1]), F32)
    out = [None] * nb
    for blk in (reversed(range(nb)) if reverse else range(nb)):
        rows = slice(blk * LRU_SCAN_BLOCK, (blk + 1) * LRU_SCAN_BLOCK)
        h = b[rows] + a[rows] * carry
        out[blk] = h
        carry = h[0:1] if reverse else h[LRU_SCAN_BLOCK - 1:LRU_SCAN_BLOCK]
    return jnp.concatenate(out, axis=0)


def _neg_expm1(y):
    series = -y * (1.0 + y * (0.5 + y * (1.0 / 6.0 + y * (1.0 / 24.0 + y * (1.0 / 120.0)))))
    return jnp.where(y > -0.1, series, 1.0 - jnp.exp(y))


def _lru_forward(x, g_in, cw, cb, wa, ba, wx, bx, lam):
    xs = [_shift_down(x, CONV_WIDTH - 1 - j, 0.0) for j in range(CONV_WIDTH - 1)] + [x]
    xc = cb + sum(cw[j:j + 1, :] * xs[j] for j in range(CONV_WIDTH))
    r = jax.nn.sigmoid(_dot(xc, wa) + ba)
    i = jax.nn.sigmoid(_dot(xc, wx) + bx)
    lsl = _log_sigmoid(lam)
    la = LRU_C * r * lsl
    a = jnp.exp(la)
    s = jnp.sqrt(_neg_expm1(2.0 * la))
    h = _linear_scan(a, s * (i * xc))
    u = GELU_C * (g_in + GELU_A * g_in * g_in * g_in)
    th = jnp.tanh(u)
    gelu = 0.5 * g_in * (1.0 + th)
    return xs, xc, r, i, lsl, a, s, h, th, gelu


def _lru_specs(t):
    col = lambda off: pl.BlockSpec((t, LANES), lambda j: (0, j + off))
    vec = pl.BlockSpec((1, LANES), lambda j: (0, j))
    mat = pl.BlockSpec((None, LANES, LANES), lambda j: (j, 0, 0))
    return [col(0), col(GROUP_WIDTH // LANES), pl.BlockSpec((CONV_WIDTH, LANES), lambda j: (0, j)),
            vec, mat, vec, mat, vec, vec]


def _lru_fwd(pel, conv_w, conv_b, wa, ba, wx, bx, lam, ride=None):
    t = pel.shape[0]

    def body(g_ref, x_ref, cw_ref, cb_ref, wa_ref, ba_ref, wx_ref, bx_ref, lam_ref, o_ref):
        res = _lru_forward(x_ref[...], g_ref[...], cw_ref[...], cb_ref[...], wa_ref[...], ba_ref[...],
                           wx_ref[...], bx_ref[...], lam_ref[...])
        o_ref[...] = (res[7] * res[9]).astype(o_ref.dtype)

    return _pcall(
        body, ride, grid=(GROUP_WIDTH // LANES,), in_specs=_lru_specs(t),
        out_specs=pl.BlockSpec((t, LANES), lambda j: (0, j)),
        out_shape=jax.ShapeDtypeStruct((t, GROUP_WIDTH), ACT_DTYPE),
        semantics=("parallel",), name="lru_fwd")(pel, pel, conv_w, conv_b, wa, ba, wx, bx, lam)


def _lru_bwd(pel, conv_w, conv_b, wa, ba, wx, bx, lam, dmix, ride=None):
    t = pel.shape[0]

    def body(g_ref, x_ref, cw_ref, cb_ref, wa_ref, ba_ref, wx_ref, bx_ref, lam_ref, do_ref,
             dg_ref, dx_ref, dcw_ref, dcb_ref, dwa_ref, dba_ref, dwx_ref, dbx_ref, dlam_ref):
        g_in, cw, lam = g_ref[...], cw_ref[...], lam_ref[...]
        xs, xc, r, i, lsl, a, s, h, th, gelu = _lru_forward(
            x_ref[...], g_in, cw, cb_ref[...], wa_ref[...], ba_ref[...], wx_ref[...], bx_ref[...], lam)
        dout = do_ref[...]
        dgelu = 0.5 * (1.0 + th) + 0.5 * g_in * (1.0 - th * th) * GELU_C * (1.0 + 3.0 * GELU_A * g_in * g_in)
        dg_ref[...] = (dout * h * dgelu).astype(dg_ref.dtype)
        gsum = _linear_scan(_shift_up(a, 1, 0.0), dout * gelu, reverse=True)
        da = gsum * _shift_down(h, 1, 0.0)
        di = gsum * s * xc
        dla = da * a - gsum * (i * xc) * (a * a / s)
        dlam_ref[...] = jnp.sum(dla * (LRU_C * r), axis=0, keepdims=True) * jax.nn.sigmoid(-lam)
        dpr = dla * (LRU_C * lsl) * r * (1.0 - r)
        dpi = di * i * (1.0 - i)
        dxc = gsum * s * i + _dot(dpr, wa_ref[...], 1, 1) + _dot(dpi, wx_ref[...], 1, 1)
        xct = xc.T
        dwa_ref[...] = _dot(xct, dpr)
        dwx_ref[...] = _dot(xct, dpi)
        dba_ref[...] = jnp.sum(dpr, axis=0, keepdims=True)
        dbx_ref[...] = jnp.sum(dpi, axis=0, keepdims=True)
        dcb_ref[...] = jnp.sum(dxc, axis=0, keepdims=True)
        for j in range(CONV_WIDTH):
            dcw_ref[j:j + 1, :] = jnp.sum(dxc * xs[j], axis=0, keepdims=True)
        dx = cw[CONV_WIDTH - 1:CONV_WIDTH, :] * dxc
        for j in range(CONV_WIDTH - 1):
            dx = dx + cw[j:j + 1, :] * _shift_up(dxc, CONV_WIDTH - 1 - j, 0.0)
        dx_ref[...] = dx.astype(dx_ref.dtype)

    col = pl.BlockSpec((t, LANES), lambda j: (0, j))
    vec = pl.BlockSpec((1, LANES), lambda j: (0, j))
    mat = pl.BlockSpec((None, LANES, LANES), lambda j: (j, 0, 0))
    nb = GROUP_WIDTH // LANES
    vshape = jax.ShapeDtypeStruct((1, GROUP_WIDTH), F32)
    mshape = jax.ShapeDtypeStruct((nb, LANES, LANES), F32)
    return _pcall(
        body, ride, grid=(nb,),
        in_specs=_lru_specs(t) + [pl.BlockSpec((t, LANES), lambda j: (0, j + nb))],
        out_specs=[col, col, pl.BlockSpec((CONV_WIDTH, LANES), lambda j: (0, j)), vec, mat, vec, mat, vec, vec],
        out_shape=[jax.ShapeDtypeStruct((t, GROUP_WIDTH), ACT_DTYPE), jax.ShapeDtypeStruct((t, GROUP_WIDTH), ACT_DTYPE),
                   jax.ShapeDtypeStruct((CONV_WIDTH, GROUP_WIDTH), F32), vshape, mshape, vshape, mshape, vshape, vshape],
        semantics=("parallel",), name="lru_bwd")(
            pel, pel, conv_w, conv_b, wa, ba, wx, bx, lam, dmix)


def _block_diag_pairs(w):
    z = jnp.zeros((LRU_BLOCK_DIM, LRU_BLOCK_DIM), w.dtype)
    return jnp.stack([jnp.block([[w[2 * j], z], [z, w[2 * j + 1]]]) for j in range(w.shape[0] // 2)])


def _block_diag_pairs_grad(dw):
    b = LRU_BLOCK_DIM
    return jnp.stack([dw[n // 2, (n % 2) * b:(n % 2 + 1) * b, (n % 2) * b:(n % 2 + 1) * b] for n in range(2 * dw.shape[0])])


def _row_tile(r):
    return ROW_TILE if r % ROW_TILE == 0 else r


def _pair_sum(g, got, place, name):
    _, r, c = g.shape
    tile = r

    def body(place_ref, a_ref, b_ref, o_ref):
        o_ref[...] = (a_ref[...].astype(F32) + b_ref[...].astype(F32)).astype(o_ref.dtype)

    blk = pl.BlockSpec((1, tile, c), lambda k, i, place_ref: (k, i, 0))
    return pl.pallas_call(
        body,
        grid_spec=pltpu.PrefetchScalarGridSpec(
            num_scalar_prefetch=1, grid=(N_CHIPS, r // tile),
            in_specs=[pl.BlockSpec((1, tile, c), lambda k, i, place_ref: (2 * k + place_ref[0], i, 0)), blk],
            out_specs=blk),
        out_shape=jax.ShapeDtypeStruct(got.shape, got.dtype),
        compiler_params=_params("parallel", "parallel"), name=name)(place, g, got)


def _adamw_update(g, w_ref, m_ref, v_ref, g_ref, d_ref, nm_ref, nv_ref):
    nm = ADAM_B1 * m_ref[...] + (1.0 - ADAM_B1) * g
    nv = ADAM_B2 * v_ref[...] + (1.0 - ADAM_B2) * jnp.square(g)
    m_hat = nm / (1.0 - ADAM_B1 ** ADAM_STEP)
    v_hat = nv / (1.0 - ADAM_B2 ** ADAM_STEP)
    g_ref[...] = g
    d_ref[...] = -ADAM_LR * (m_hat / (jnp.sqrt(v_hat) + ADAM_EPS) + ADAM_WD * w_ref[...])
    nm_ref[...] = nm
    nv_ref[...] = nv


def _adamw_sharded(parts, w, m, v, place, name, ride=None):
    n_layers, r, c = w.shape
    tile = _row_tile(r)
    nb = r // tile
    counts = [1 + len(recvs) for _, recvs in parts]

    def body(place_ref, *refs):
        layer = pl.program_id(0)
        g, at = None, 0
        for l in range(n_layers):
            g_l = refs[at][0].astype(F32)
            for r_ref in refs[at + 1:at + counts[l]]:
                for k in range(r_ref.shape[0]):
                    g_l = g_l + r_ref[k].astype(F32)
            g = g_l if g is None else jnp.where(layer == l, g_l, g)
            at += counts[l]
        _adamw_update(g, *refs[at:])

    def part_specs(l, recvs):
        rows = lambda q, i: jnp.where(q < l, 0, jnp.where(q > l, nb - 1, i))
        return ([pl.BlockSpec((1, tile, c), lambda q, i, place_ref: (place_ref[1], rows(q, i), 0))] +
                [pl.BlockSpec((a.shape[0], tile, c), lambda q, i, place_ref: (0, rows(q, i), 0)) for a in recvs])

    in_specs, args = [], []
    for l, (s, recvs) in enumerate(parts):
        in_specs += part_specs(l, recvs)
        args += [s, *recvs]
    blk = pl.BlockSpec((None, tile, c), lambda q, i, place_ref: (q, i, 0))
    out = jax.ShapeDtypeStruct((n_layers, r, c), F32)
    return _pcall(body, ride, grid=(n_layers, nb), in_specs=in_specs + [blk, blk, blk], out_specs=[blk, blk, blk, blk],
                  out_shape=[out, out, out, out], semantics=("arbitrary", "arbitrary"), name=name, prefetch=True)(
                      place, *args, w, m, v)


def _adamw_small(repl_parts, vec_parts, w, m, v, place):
    n_r, n = len(repl_parts), len(w)
    shapes = [a.shape for a in w]

    def body(place_ref, *refs):
        parts, rest = refs[:n], refs[n:]
        for k in range(n):
            take = (lambda p: parts[k][p]) if k < n_r else (lambda p: parts[k][p, 0])
            g = take(0)
            for p in range(1, N_DEV):
                g = g + take(p)
            _adamw_update(g, rest[k], rest[n + k], rest[2 * n + k], *rest[3 * n + 4 * k:3 * n + 4 * k + 4])

    def whole(shape):
        return pl.BlockSpec(shape, lambda i, place_ref: (0,) * len(shape))

    def mine(shard):
        return pl.BlockSpec((N_DEV, 1) + shard, lambda i, place_ref: (0, place_ref[2]) + (0,) * len(shard))

    in_specs = [whole(a.shape) for a in repl_parts] + [mine(s) for s in shapes[n_r:]] + [whole(s) for s in shapes] * 3
    outs = pl.pallas_call(
        body,
        grid_spec=pltpu.PrefetchScalarGridSpec(
            num_scalar_prefetch=1, grid=(1,), in_specs=in_specs,
            out_specs=[whole(s) for s in shapes for _ in range(4)]),
        out_shape=[jax.ShapeDtypeStruct(s, F32) for s in shapes for _ in range(4)],
        compiler_params=_params("arbitrary"), name="adamw_small")(place, *repl_parts, *vec_parts, *w, *m, *v)
    return [outs[4 * k:4 * k + 4] for k in range(n)]


SHARDED = {"norm_w": 2, "w_in_even": 2, "gla_w_a_up": 2, "w_out_even": 1, "w_in_odd": 2, "conv_w": 2, "conv_b": 1,
           "lru_b_a": 1, "lru_b_x": 1, "lru_lambda": 1, "w_out_odd": 1, "w_mlp_up": 2, "w_mlp_down": 1}
REPLICATED = ["gla_b_a", "gla_norm_w", "fox_b_f", "rel_bias", "lru_w_a", "lru_w_x"]
WEIGHTS = ["norm_w", "w_in_even", "gla_w_a_up", "gla_b_a", "gla_norm_w", "fox_b_f", "w_out_even", "w_in_odd",
           "rel_bias", "conv_w", "conv_b", "lru_w_a", "lru_b_a", "lru_w_x", "lru_b_x", "lru_lambda", "w_out_odd",
           "w_mlp_up", "w_mlp_down"]
MATRICES = ("w_in_even", "w_out_even", "w_in_odd", "w_out_odd", "w_mlp_up", "w_mlp_down")
TRANSPOSED = ("w_in_even", "w_in_odd")
VECTORS = tuple(n for n in SHARDED if n not in MATRICES)
MATRIX_BLOCKS = (("w_in_even", 0), ("w_out_even", 0), ("w_in_odd", 0), ("w_out_odd", 0),
                 ("w_mlp_up", 0), ("w_mlp_up", 1), ("w_mlp_down", 0), ("w_mlp_down", 1))


def _join_shards(blocks, axis):
    moved = jnp.moveaxis(blocks, 0, axis)
    shape = moved.shape
    return moved.reshape(shape[:axis] + (shape[axis] * shape[axis + 1],) + shape[axis + 2:])


def _split_shards(full, axis):
    shape = full.shape
    cut = full.reshape(shape[:axis] + (N_DEV, shape[axis] // N_DEV) + shape[axis + 1:])
    return jnp.moveaxis(cut, axis, 0)


EVEN_SPLITS = (0, 256, 512, 1024, 1536, 1552, 2064, 2576, 3088, 3096)


def _even_in_split(wt):
    c = [wt[EVEN_SPLITS[k]:EVEN_SPLITS[k + 1]] for k in range(9)]
    gq, gk, gv, gr, ga, fq, fk, fv, ff = c
    padrows = lambda a: jnp.pad(a, ((0, LANES - a.shape[0]), (0, 0)))
    return jnp.concatenate([gq, gk, gv, fq, fk, fv], axis=0), jnp.concatenate([gr, padrows(ga), padrows(ff)], axis=0)


def _even_in_merge(dmm, dele):
    return jnp.concatenate([dmm[:1024], dele[:512], dele[512:512 + GLA_RANK], dmm[1024:2560],
                            dele[640:640 + ATT_HEADS]], axis=0)


def _forward_backward(x, target, shard, vec_shard, w, place):
    w = dict(w)
    g, dnorm, sums, recv = {}, {}, {}, {}
    nrm = lambda l, k: w["norm_w"][l, k][None, :]
    gather = lambda *keys: _gather_plan([shard[k] for k in keys])
    blocks = lambda r, c: (N_DEV, r // N_DEV, c)

    def pair_sum(key):
        sums[key] = _pair_sum(g[key], got[key], place, f"rs_pair_sum_{key[0]}_{key[1]}")

    got = {}

    def mlp_fwd(xin, layer, ride_up, ride_down):
        up = _mm(xin, w["w_mlp_up"][layer], out_dtype=ACT_DTYPE, tm=TM_FWD, tn=D_FF // N_DEV, b_blocked=True,
                 a_norm=nrm(layer, 2), name=f"mlp_up_{layer}", ride=ride_up)
        (u, h), rode_up = up if ride_up is not None else (up, None)
        down = _mm(u, w["w_mlp_down"][layer], out_dtype=F32, tm=TM_DX // 2, tn=D_MODEL, a_sqrelu=True,
                   res_norm=(xin, nrm(layer, 3)), name=f"mlp_down_{layer}", ride=ride_down)
        (yv, xout), rode_down = down if ride_down is not None else (down, None)
        return xout, (xin, h, u, yv), rode_up, rode_down

    def mlp_bwd(dxout, saved, layer, ride):
        xin, h, u, yv = saved
        k_up, k_down = ("w_mlp_up", layer), ("w_mlp_down", layer)
        res = _mm(dxout, w["w_mlp_down"][layer], nt=True, out_dtype=ACT_DTYPE, tm=TM_DX, tn=TN, drelu_of=u,
                  a_norm_bwd=(yv, nrm(layer, 3)), name=f"mlp_down_dx_{layer}", ride=ride)
        (du, dy, dnorm[(layer, 3)]), rode = res if ride is not None else (res, None)
        g[k_down] = _mm(u, dy, ta=True, out_dtype=WIRE_DTYPE, tm=TM_DW, tn=TN, a_sqrelu=True,
                        name=f"mlp_down_dw_{layer}").reshape(blocks(D_FF, D_MODEL))
        g[k_up] = _mm(h, du, ta=True, out_dtype=WIRE_DTYPE, tm=TM_DW, tn=D_FF // N_DEV, out_blocked=True,
                      name=f"mlp_up_dw_{layer}")
        w_up = jnp.moveaxis(w["w_mlp_up"][layer], 0, 1).reshape(D_MODEL, D_FF)
        (dxin, dnorm[(layer, 2)]), (got[k_down], got[k_up]) = _mm(
            du, w_up, nt=True, out_dtype=F32, tm=TM_DX // 2, tn=D_MODEL, norm_bwd=(xin, nrm(layer, 2), dxout),
            name=f"mlp_up_dx_{layer}", ride=_sibling_plan([g[k_down], g[k_up]]))
        pair_sum(k_down)
        pair_sum(k_up)
        return dxin, rode

    first = _run_plan(_gather_plan([shard[("w_in_even", 0)]] + [vec_shard[n] for n in VECTORS]),
                      "weights_all_gather_first")
    w["w_in_even"] = first[0].reshape(-1, D_MODEL)
    for n, b in zip(VECTORS, first[1:]):
        w[n] = _join_shards(b, SHARDED[n])
    w["w_mlp_up"], w["w_mlp_down"] = [None] * DEPTH, [None] * DEPTH

    wmm_e, wel_e = _even_in_split(w["w_in_even"])
    w_up_pad = jnp.pad(w["gla_w_a_up"][0], ((0, LANES - GLA_RANK), (0, 0)))
    b_f_pad = jnp.pad(w["fox_b_f"], ((0, 0), (0, LANES - ATT_HEADS)))
    (pmm0, h0), (w_out_even,) = _mm(x, wmm_e, nt=True, out_dtype=ACT_DTYPE, tm=TM_FWD, tn=TN, a_norm=nrm(0, 0),
                                    name="in_even_mm", ride=gather(("w_out_even", 0)))
    pel0 = _mm(h0, wel_e, nt=True, out_dtype=F32, tm=TM_FWD, tn=768, name="in_even_el")
    (out_a, states), (w["w_mlp_up"][0],) = _gla_fwd(pmm0, pel0, w_up_pad, w["gla_b_a"], w["gla_norm_w"],
                                                    ride=gather(("w_mlp_up", 0)))
    cum, cum_t = _fox_gate_fwd(pel0, b_f_pad)
    (out_b, lse_b), (w_mlp_down0, w_in_odd) = _fox_fwd(pmm0, cum, cum_t,
                                                       ride=gather(("w_mlp_down", 0), ("w_in_odd", 0)))
    w["w_out_even"] = w_out_even.reshape(D_MODEL, D_MODEL)
    w["w_mlp_down"][0] = w_mlp_down0.reshape(D_FF, D_MODEL)
    mix_in0 = jnp.concatenate([out_a, out_b], axis=1)
    mix0, x1 = _mm(mix_in0, w["w_out_even"], out_dtype=F32, tm=TM_DX, tn=D_MODEL, res_norm=(x, nrm(0, 1)),
                   name="out_even")
    x2, mlp0, _, (w["w_mlp_up"][1],) = mlp_fwd(x1, 0, None, gather(("w_mlp_up", 1)))
    w["w_in_odd"] = w_in_odd.reshape(-1, D_MODEL)

    w_in_o = w["w_in_odd"]
    n_mm_o = 3 * GROUP_WIDTH
    wa_bd, wx_bd = _block_diag_pairs(w["lru_w_a"][0]), _block_diag_pairs(w["lru_w_x"][0])
    base = _ca_bias_base(w["rel_bias"][0])
    pmm1, h1 = _mm(x2, w_in_o[:n_mm_o], nt=True, out_dtype=ACT_DTYPE, tm=TM_FWD, tn=TN, a_norm=nrm(1, 0),
                   name="in_odd_mm")
    pel1 = _mm(h1, w_in_o[n_mm_o:], nt=True, out_dtype=F32, tm=TM_FWD, tn=TN, name="in_odd_el")
    kp = jnp.pad(pmm1[:, GROUP_WIDTH:2 * GROUP_WIDTH], ((CA_LEFT, 0), (0, 0)))
    vp = jnp.pad(pmm1[:, 2 * GROUP_WIDTH:], ((CA_LEFT, 0), (0, 0)))
    (out_c, lse_c), (w_mlp_down1,) = _ca_fwd(pmm1, kp, vp, base, ride=gather(("w_mlp_down", 1)))
    w["w_mlp_down"][1] = w_mlp_down1.reshape(D_FF, D_MODEL)
    lru_args = (pel1, w["conv_w"][0], w["conv_b"], wa_bd, w["lru_b_a"], wx_bd, w["lru_b_x"], w["lru_lambda"])
    out_d, (w_out_odd,) = _lru_fwd(*lru_args, ride=gather(("w_out_odd", 0)))
    w["w_out_odd"] = w_out_odd.reshape(D_MODEL, D_MODEL)
    mix_in1 = jnp.concatenate([out_c, out_d], axis=1)
    mix1, x3 = _mm(mix_in1, w["w_out_odd"], out_dtype=F32, tm=TM_DX, tn=D_MODEL, res_norm=(x2, nrm(1, 1)),
                   name="out_odd")
    x4, mlp1, _, _ = mlp_fwd(x3, 1, None, None)

    loss, dx4 = _loss_fwd_bwd(x4, target)

    k_oo, k_io, k_oe, k_ie = ("w_out_odd", 0), ("w_in_odd", 0), ("w_out_even", 0), ("w_in_even", 0)
    mlp_keys = lambda l: [("w_mlp_down", l), ("w_mlp_up", l)]
    dx3, _ = mlp_bwd(dx4, mlp1, 1, None)
    dmix_in1, dmix1, dnorm[(1, 1)] = _mm(dx3, w["w_out_odd"], nt=True, out_dtype=F32, tm=TM_DX, tn=TN,
                                         a_norm_bwd=(mix1, nrm(1, 1)), name="out_odd_dx")
    g[k_oo] = _mm(mix_in1, dmix1, ta=True, out_dtype=WIRE_DTYPE, tm=TM_DW, tn=TN, name="out_odd_dw").reshape(
        blocks(D_MODEL, D_MODEL))
    (dq_c, dkp, dvp, dbase), rode = _ca_bwd(
        pmm1, kp, vp, base, lse_c, dmix_in1,
        ride=_join_plans(_chip_plan([sums[k] for k in mlp_keys(1)]), _sibling_plan([g[k_oo]])))
    recv.update(zip(mlp_keys(1), rode[:2]))
    got[k_oo] = rode[2]
    pair_sum(k_oo)
    (dgate, dxin, g_conv_w, g_conv_b, dwa_bd, g_lru_b_a, dwx_bd, g_lru_b_x, g_lru_lambda), (recv[k_oo],) = _lru_bwd(
        *lru_args, dmix_in1, ride=_chip_plan([sums[k_oo]]))
    dp1 = jnp.concatenate([dq_c, dkp[CA_LEFT:].astype(ACT_DTYPE), dvp[CA_LEFT:].astype(ACT_DTYPE), dgate, dxin], axis=1)
    g[k_io] = _mm(dp1, h1, ta=True, out_dtype=WIRE_DTYPE, tm=dp1.shape[1] // 2, tn=TN, name="in_odd_dw").reshape(
        blocks(dp1.shape[1], D_MODEL))
    (dx2, dnorm[(1, 0)]), (got[k_io],) = _mm(dp1, w_in_o, out_dtype=F32, tm=TM_DX // 2, tn=D_MODEL,
                                             norm_bwd=(x2, nrm(1, 0), dx3), name="in_odd_dx",
                                             ride=_sibling_plan([g[k_io]]))
    pair_sum(k_io)
    g["rel_bias"] = _ca_bias_base_grad(dbase)[None]
    g["conv_w"], g["conv_b"] = g_conv_w[None], g_conv_b
    g["lru_w_a"], g["lru_w_x"] = _block_diag_pairs_grad(dwa_bd)[None], _block_diag_pairs_grad(dwx_bd)[None]
    g["lru_b_a"], g["lru_b_x"], g["lru_lambda"] = g_lru_b_a, g_lru_b_x, g_lru_lambda

    dx1, (recv[k_io],) = mlp_bwd(dx2, mlp0, 0, _chip_plan([sums[k_io]]))
    dmix_in0, dmix0, dnorm[(0, 1)] = _mm(dx1, w["w_out_even"], nt=True, out_dtype=F32, tm=TM_DX, tn=TN,
                                         a_norm_bwd=(mix0, nrm(0, 1)), name="out_even_dx")
    g[k_oe] = _mm(mix_in0, dmix0, ta=True, out_dtype=WIRE_DTYPE, tm=TM_DW, tn=TN, name="out_even_dw").reshape(
        blocks(D_MODEL, D_MODEL))
    k_md0, k_mu0 = mlp_keys(0)
    (dq_a, dk_a, dv_a, dr_a, da_a, dw_up_pad, g_gla_b_a, g_gla_norm_w), (got[k_oe],) = _gla_bwd(
        pmm0, pel0, w_up_pad, w["gla_b_a"], w["gla_norm_w"], states, dmix_in0, ride=_sibling_plan([g[k_oe]]))
    pair_sum(k_oe)
    (dq_b, dk_b, dv_b, dcum_t, dcum_q), (recv[k_md0], recv[k_mu0], recv[k_oe]) = _fox_bwd(
        pmm0, cum, cum_t, lse_b, dmix_in0, ride=_chip_plan([sums[k_md0], sums[k_mu0], sums[k_oe]]))
    df_b, db_f = _fox_gate_bwd(pel0, b_f_pad, dcum_t, dcum_q)
    g["gla_w_a_up"] = dw_up_pad[:GLA_RANK][None]
    g["gla_b_a"], g["gla_norm_w"], g["fox_b_f"] = g_gla_b_a, g_gla_norm_w, db_f[:, :ATT_HEADS]
    dp0 = jnp.concatenate([dq_a, dk_a, dv_a, dq_b, dk_b.astype(ACT_DTYPE), dv_b.astype(ACT_DTYPE), dr_a, da_a, df_b],
                          axis=1)
    w_perm = jnp.concatenate([wmm_e, wel_e], axis=0)
    n_mm_e = wmm_e.shape[0]
    dw_perm, repl_parts = _mm(dp0, h0, ta=True, out_dtype=WIRE_DTYPE, tm=dp0.shape[1] // 2, tn=TN, name="in_even_dw",
                              ride=_gather_plan([g[n] for n in REPLICATED]))
    dw_even = _even_in_merge(dw_perm[:n_mm_e], dw_perm[n_mm_e:])
    g[k_ie] = dw_even.reshape(blocks(dw_even.shape[0], D_MODEL))
    dh0, (got[k_ie],) = _mm(dp0, w_perm, out_dtype=F32, tm=TM_DX, tn=TN, name="in_even_dx",
                            ride=_sibling_plan([g[k_ie]]))
    pair_sum(k_ie)
    dx0, dnorm[(0, 0)] = _norm_bwd(dh0, x, nrm(0, 0), out_dtype=F32, add=dx1, name="norm_in_bwd_0")

    g["norm_w"] = jnp.stack([jnp.concatenate([dnorm[(l, k)] for k in range(4)], axis=0) for l in range(DEPTH)])
    recv[k_ie], losses, *vec_parts = _run_plan(
        _join_plans(_chip_plan([sums[k_ie]]),
                    _gather_plan([loss] + [_split_shards(g[n], SHARDED[n]) for n in VECTORS])), "last_exchanges")
    return losses, dx0, sums, recv, repl_parts, vec_parts


def kernel(x, norm_w, w_in_even, gla_w_a_up, gla_b_a, gla_norm_w, fox_b_f, w_out_even, w_in_odd, rel_bias, conv_w, conv_b, lru_w_a, lru_b_a, lru_w_x, lru_b_x, lru_lambda, w_out_odd, w_mlp_up, w_mlp_down, loss_target, m_norm_w, m_w_in_even, m_gla_w_a_up, m_gla_b_a, m_gla_norm_w, m_fox_b_f, m_w_out_even, m_w_in_odd, m_rel_bias, m_conv_w, m_conv_b, m_lru_w_a, m_lru_b_a, m_lru_w_x, m_lru_b_x, m_lru_lambda, m_w_out_odd, m_w_mlp_up, m_w_mlp_down, v_norm_w, v_w_in_even, v_gla_w_a_up, v_gla_b_a, v_gla_norm_w, v_fox_b_f, v_w_out_even, v_w_in_odd, v_rel_bias, v_conv_w, v_conv_b, v_lru_w_a, v_lru_b_a, v_lru_w_x, v_lru_b_x, v_lru_lambda, v_w_out_odd, v_w_mlp_up, v_w_mlp_down):
    wts = dict(zip(WEIGHTS, (norm_w, w_in_even, gla_w_a_up, gla_b_a, gla_norm_w, fox_b_f, w_out_even, w_in_odd, rel_bias,
                             conv_w, conv_b, lru_w_a, lru_b_a, lru_w_x, lru_b_x, lru_lambda, w_out_odd, w_mlp_up,
                             w_mlp_down)))
    mom = dict(zip(WEIGHTS, (m_norm_w, m_w_in_even, m_gla_w_a_up, m_gla_b_a, m_gla_norm_w, m_fox_b_f, m_w_out_even,
                             m_w_in_odd, m_rel_bias, m_conv_w, m_conv_b, m_lru_w_a, m_lru_b_a, m_lru_w_x, m_lru_b_x,
                             m_lru_lambda, m_w_out_odd, m_w_mlp_up, m_w_mlp_down)))
    var = dict(zip(WEIGHTS, (v_norm_w, v_w_in_even, v_gla_w_a_up, v_gla_b_a, v_gla_norm_w, v_fox_b_f, v_w_out_even,
                             v_w_in_odd, v_rel_bias, v_conv_w, v_conv_b, v_lru_w_a, v_lru_b_a, v_lru_w_x, v_lru_b_x,
                             v_lru_lambda, v_w_out_odd, v_w_mlp_up, v_w_mlp_down)))
    ax, ay, ac = lax.axis_index("x"), lax.axis_index("y"), lax.axis_index("c")
    place = jnp.stack([ac, 2 * ax + ay, 4 * ax + 2 * ay + ac]).astype(jnp.int32)

    shard = {(n, l): (wts[n][l].T if n in TRANSPOSED else wts[n][l]).astype(WIRE_DTYPE) for n, l in MATRIX_BLOCKS}
    losses, dx, sums, recv, repl_parts, vec_parts = _forward_backward(
        x[0], loss_target[0], shard, {n: wts[n] for n in VECTORS}, {n: wts[n] for n in REPLICATED}, place)
    loss = jnp.sum(losses[:, 0, 0])

    view = lambda n, a: jnp.swapaxes(a, 1, 2) if n in TRANSPOSED else a
    upd = {n: [view(n, o) for o in _adamw_sharded(
        [(sums[(n, l)], [recv[(n, l)]]) for l in range(wts[n].shape[0])], view(n, wts[n]), view(n, mom[n]),
        view(n, var[n]), place, f"adamw_{n}")] for n in MATRICES}
    small = REPLICATED + list(VECTORS)
    upd.update(zip(small, _adamw_small(repl_parts, vec_parts, [wts[n] for n in small], [mom[n] for n in small],
                                       [var[n] for n in small], place)))
    return (loss, dx[None], *[upd[n][kind] for kind in range(4) for n in WEIGHTS])
```

```python
import functools
from typing import Callable, NamedTuple, Optional

import jax
import jax.numpy as jnp
from jax import lax
from jax.experimental import pallas as pl
from jax.experimental.pallas import tpu as pltpu

F32 = jnp.float32
MXU_DTYPE = jnp.bfloat16
ACT_DTYPE = jnp.bfloat16
WIRE_DTYPE = jnp.bfloat16

V7X_VMEM_BYTES = 64 * 1024 * 1024
VMEM_LIMIT = (V7X_VMEM_BYTES * 7) // 8
LANES = 128

D_MODEL = 1024
DEPTH = 2
CHUNK = 64
GROUP_WIDTH = D_MODEL // 2
D_FF = 4 * D_MODEL
NORM_EPS = 1e-6
GLA_HEADS = 4
GLA_DV = GROUP_WIDTH // GLA_HEADS
GLA_DK = GLA_DV // 2
GLA_KW = GLA_HEADS * GLA_DK
GLA_RANK = 16
GLA_GATE_TAU = 16.0
HEAD_DIM = 64
ATT_HEADS = GROUP_WIDTH // HEAD_DIM
CA_LEFT = 8 * CHUNK
REL_CLIP = 128
LRU_BLOCK_DIM = 64
CONV_WIDTH = 4
LRU_C = 8.0
N_DEV = 8

ADAM_LR = 0.001
ADAM_B1 = 0.9
ADAM_B2 = 0.999
ADAM_EPS = 1e-08
ADAM_WD = 0.01
ADAM_STEP = 10

NEG = float(jnp.finfo(jnp.float32).min)
MESH = pl.DeviceIdType.MESH


def _params(*sem):
    return pltpu.CompilerParams(dimension_semantics=sem, vmem_limit_bytes=VMEM_LIMIT)


def _dot(a, b, ca=1, cb=0):
    return lax.dot_general(a.astype(MXU_DTYPE), b.astype(MXU_DTYPE), (((ca,), (cb,)), ((), ())),
                           preferred_element_type=F32)


def _dot_exact(a, b):
    return lax.dot_general(a, b, (((1,), (0,)), ((), ())), precision=lax.Precision.HIGHEST,
                           preferred_element_type=F32)


def _log_sigmoid(x):
    return jnp.minimum(x, 0.0) - jnp.log1p(jnp.exp(-jnp.abs(x)))


def _iota(shape, axis):
    return lax.broadcasted_iota(jnp.int32, shape, axis)


ANY = pl.BlockSpec(memory_space=pl.ANY)
N_CHIPS = 4


class _Plan(NamedTuple):
    ins: list
    outs: list
    sems: list
    start: Callable
    finish: Callable
    relay: Optional[Callable] = None


def _place():
    x, y, c = lax.axis_index("x"), lax.axis_index("y"), lax.axis_index("c")
    return x, y, c, [(1 - x, y), (x, 1 - y), (1 - x, 1 - y)]


def _gather_plan(xs):
    n = len(xs)

    def parts(x_refs, out_refs, sems):
        send_sems, recv_sems, local_sems = sems
        x, y, c, chips = _place()
        me, sibling = (x, y, c), (x, y, 1 - c)

        def rows(a, px, py, pc):
            return out_refs[a].at[4 * px + 2 * py + pc]

        def copy(a, k, block, to, src=None):
            return pltpu.make_async_remote_copy(
                src_ref=rows(a, *block) if src is None else src, dst_ref=rows(a, *block),
                send_sem=send_sems.at[7 * a + k], recv_sem=recv_sems.at[7 * a + k], device_id=to, device_id_type=MESH)

        def own():
            mine = [pltpu.make_async_copy(x_refs[a], rows(a, *me), local_sems.at[a]) for a in range(n)]
            first = []
            for a in range(n):
                first.append(copy(a, 0, me, sibling, src=x_refs[a]))
                first += [copy(a, 1 + j, me, (*chip, c), src=x_refs[a]) for j, chip in enumerate(chips)]
            return mine, first

        return c, me, sibling, chips, copy, own

    def start(x_refs, out_refs, sems):
        mine, first = parts(x_refs, out_refs, sems)[-1]()
        for cp in first + mine:
            cp.start()

    def relay(x_refs, out_refs, sems):
        c, me, sibling, chips, copy, _ = parts(x_refs, out_refs, sems)
        for j, chip in enumerate(chips):
            for a in range(n):
                copy(a, 1 + j, (*chip, c), me).wait_recv()
                copy(a, 4 + j, (*chip, c), sibling).start()

    def finish(x_refs, out_refs, sems):
        c, me, sibling, chips, copy, own = parts(x_refs, out_refs, sems)
        mine, first = own()
        for a in range(n):
            copy(a, 0, sibling, me).wait_recv()
            for j, chip in enumerate(chips):
                copy(a, 4 + j, (*chip, 1 - c), me).wait_recv()
        for cp in first + [copy(a, 4 + j, (*chip, c), sibling) for j, chip in enumerate(chips) for a in range(n)]:
            cp.wait_send()
        for cp in mine:
            cp.wait()

    return _Plan(list(xs), [jax.ShapeDtypeStruct((N_DEV,) + x.shape, x.dtype) for x in xs],
                 [pltpu.SemaphoreType.DMA((7 * n,)), pltpu.SemaphoreType.DMA((7 * n,)), pltpu.SemaphoreType.DMA((n,))],
                 start, finish, relay)


def _exchange_plan(copies_of, ins, outs, per_array):
    n = len(ins)

    def start(in_refs, out_refs, sems):
        for cp in copies_of(in_refs, out_refs, sems):
            cp.start()

    def finish(in_refs, out_refs, sems):
        copies = copies_of(in_refs, out_refs, sems)
        for cp in copies:
            cp.wait_recv()
        for cp in copies:
            cp.wait_send()

    return _Plan(list(ins), outs, [pltpu.SemaphoreType.DMA((per_array * n,)), pltpu.SemaphoreType.DMA((per_array * n,))],
                 start, finish)


def _sibling_plan(gs):
    def copies_of(g_refs, got_refs, sems):
        x, y, c, _ = _place()
        return [pltpu.make_async_remote_copy(
            src_ref=g_refs[a].at[2 * k + (1 - c)], dst_ref=got_refs[a].at[k], send_sem=sems[0].at[N_CHIPS * a + k],
            recv_sem=sems[1].at[N_CHIPS * a + k], device_id=(x, y, 1 - c), device_id_type=MESH)
            for a in range(len(gs)) for k in range(N_CHIPS)]

    return _exchange_plan(copies_of, gs, [jax.ShapeDtypeStruct((N_CHIPS,) + g.shape[1:], g.dtype) for g in gs], N_CHIPS)


def _chip_plan(ss, relations=(0, 1, 2)):
    n_rel = len(relations)

    def copies_of(s_refs, out_refs, sems):
        x, y, c, chips = _place()
        return [pltpu.make_async_remote_copy(
            src_ref=s_refs[a].at[2 * chips[j][0] + chips[j][1]], dst_ref=out_refs[a].at[slot],
            send_sem=sems[0].at[n_rel * a + slot], recv_sem=sems[1].at[n_rel * a + slot],
            device_id=(*chips[j], c), device_id_type=MESH)
            for a in range(len(ss)) for slot, j in enumerate(relations)]

    return _exchange_plan(copies_of, ss, [jax.ShapeDtypeStruct((n_rel,) + s.shape[1:], s.dtype) for s in ss], n_rel)


def _join_plans(*plans):
    def cut(refs, counts):
        at = 0
        for n in counts:
            yield refs[at:at + n]
            at += n

    def each(in_refs, out_refs, sems):
        return zip(plans, cut(in_refs, [len(p.ins) for p in plans]), cut(out_refs, [len(p.outs) for p in plans]),
                   cut(sems, [len(p.sems) for p in plans]))

    def start(*refs):
        for p, i, o, s in each(*refs):
            p.start(i, o, s)

    def relay(*refs):
        for p, i, o, s in each(*refs):
            if p.relay is not None:
                p.relay(i, o, s)

    def finish(*refs):
        for p, i, o, s in each(*refs):
            p.finish(i, o, s)

    return _Plan([a for p in plans for a in p.ins], [a for p in plans for a in p.outs],
                 [a for p in plans for a in p.sems], start, finish, relay)


def _run_plan(plan, name):
    n_in, n_out = len(plan.ins), len(plan.outs)

    def body(*refs):
        args = refs[:n_in], refs[n_in:n_in + n_out], refs[n_in + n_out:]
        plan.start(*args)
        if plan.relay is not None:
            plan.relay(*args)
        plan.finish(*args)

    return pl.pallas_call(body, out_shape=plan.outs, in_specs=[ANY] * n_in, out_specs=[ANY] * n_out,
                          scratch_shapes=plan.sems, name=name)(*plan.ins)


def _pcall(body, ride, *, grid, in_specs, out_specs, out_shape, scratch_shapes=(), semantics, name, prefetch=False):
    n_pre = int(prefetch)

    def build(kernel, ins, outs, shapes, scratch, sem):
        if prefetch:
            return pl.pallas_call(
                kernel, grid_spec=pltpu.PrefetchScalarGridSpec(num_scalar_prefetch=1, grid=grid, in_specs=ins,
                                                               out_specs=outs, scratch_shapes=scratch),
                out_shape=shapes, compiler_params=_params(*sem), name=name)
        return pl.pallas_call(kernel, grid=grid, in_specs=ins, out_specs=outs, out_shape=shapes,
                              scratch_shapes=scratch, compiler_params=_params(*sem), name=name)

    if ride is None:
        return build(body, in_specs, out_specs, out_shape, list(scratch_shapes), semantics)
    single = not isinstance(out_shape, (list, tuple))
    out_specs_l, out_shape_l = ([out_specs], [out_shape]) if single else (list(out_specs), list(out_shape))
    n_in, n_out, n_scr = len(in_specs), len(out_shape_l), len(scratch_shapes)
    r_in, r_out = len(ride.ins), len(ride.outs)

    def riding(*refs):
        pre, refs = refs[:n_pre], refs[n_pre:]
        cuts = [n_in, r_in, n_out, r_out, n_scr]
        groups, at = [], 0
        for width in cuts:
            groups.append(refs[at:at + width])
            at += width
        ins, r_ins, outs, r_outs, scr = groups
        sems = refs[at:]
        first = functools.reduce(jnp.logical_and, [pl.program_id(d) == 0 for d in range(len(grid))])
        last = functools.reduce(jnp.logical_and, [pl.program_id(d) == grid[d] - 1 for d in range(len(grid))])

        @pl.when(first)
        def _():
            ride.start(r_ins, r_outs, sems)

        several_steps = any(n > 1 for n in grid)
        if ride.relay is not None and several_steps:
            @pl.when(last)
            def _():
                ride.relay(r_ins, r_outs, sems)

        body(*pre, *ins, *outs, *scr)

        @pl.when(last)
        def _():
            if ride.relay is not None and not several_steps:
                ride.relay(r_ins, r_outs, sems)
            ride.finish(r_ins, r_outs, sems)

    call = build(riding, list(in_specs) + [ANY] * r_in, out_specs_l + [ANY] * r_out, out_shape_l + list(ride.outs),
                 list(scratch_shapes) + list(ride.sems), ["arbitrary"] * len(grid))

    def run(*args):
        res = call(*args, *ride.ins)
        return (res[0] if single else list(res[:n_out])), list(res[n_out:])

    return run


def _rms(x):
    return x * lax.rsqrt(jnp.mean(x * x, axis=-1, keepdims=True) + NORM_EPS)


def _mm(a, b, *, nt=False, ta=False, out_dtype, tm, tn, a_sqrelu=False, drelu_of=None, b_blocked=False,
        out_blocked=False, a_norm=None, a_norm_bwd=None, res_norm=None, norm_bwd=None, name, ride=None):
    k, m = a.shape if ta else a.shape[::-1]
    if b_blocked:
        assert not nt and b.shape[1] == k and b.shape[2] == tn
        n = b.shape[0] * tn
    else:
        n = b.shape[0] if nt else b.shape[1]
        assert (b.shape[1] if nt else b.shape[0]) == k
    tm, tn = min(tm, m), min(tn, n)
    assert m % tm == 0 and n % tn == 0
    assert (res_norm is None and norm_bwd is None) or tn == n
    assert a_norm is None or a_norm_bwd is None
    n_in = (2 + (drelu_of is not None) + (a_norm is not None) + 2 * (a_norm_bwd is not None)
            + 2 * (res_norm is not None) + 3 * (norm_bwd is not None))

    def body(*refs):
        a_ref, b_ref = refs[0], refs[1]
        extra = list(refs[2:n_in])
        outs = list(refs[n_in:])
        o_ref = outs.pop(0)
        u_ref = extra.pop(0) if drelu_of is not None else None
        if a_norm is not None:
            wn_ref, h_ref, h_scr = extra.pop(0), outs.pop(0), outs.pop()

            @pl.when(pl.program_id(1) == 0)
            def _():
                h = (_rms(a_ref[...]) * wn_ref[...]).astype(ACT_DTYPE)
                h_scr[...] = h
                h_ref[...] = h

            av = h_scr[...]
        elif a_norm_bwd is not None:
            y_ref, wy_ref = extra.pop(0), extra.pop(0)
            dy_ref, dwy_ref, dy_scr = outs.pop(0), outs.pop(0), outs.pop()
            first_rows = pl.program_id(0) == 0

            @pl.when(pl.program_id(1) == 0)
            def _():
                yv, up = y_ref[...], a_ref[...]
                rstd = lax.rsqrt(jnp.mean(yv * yv, axis=-1, keepdims=True) + NORM_EPS)
                yhat = yv * rstd
                g = up * wy_ref[...]
                dy = (rstd * (g - yhat * jnp.mean(g * yhat, axis=-1, keepdims=True))).astype(ACT_DTYPE)
                dy_scr[...] = dy
                dy_ref[...] = dy

                @pl.when(first_rows)
                def _():
                    dwy_ref[...] = jnp.zeros_like(dwy_ref)

                dwy_ref[...] += jnp.sum(up * yhat, axis=0, keepdims=True)

            av = dy_scr[...]
        else:
            av = a_ref[...]
        if a_sqrelu:
            av = jnp.square(jnp.maximum(av.astype(F32), 0.0))
        acc = _dot(av, b_ref[...], 0 if ta else 1, 1 if nt else 0)
        if u_ref is not None:
            acc = acc * (2.0 * jnp.maximum(u_ref[...].astype(F32), 0.0))
        if norm_bwd is not None:
            x_ref, wb_ref, add_ref = extra
            dw_ref = outs[0]
            xv = x_ref[...]
            rstd = lax.rsqrt(jnp.mean(xv * xv, axis=-1, keepdims=True) + NORM_EPS)
            xhat = xv * rstd
            g = acc * wb_ref[...]
            o_ref[...] = rstd * (g - xhat * jnp.mean(g * xhat, axis=-1, keepdims=True)) + add_ref[...]

            @pl.when(pl.program_id(0) == 0)
            def _():
                dw_ref[...] = jnp.zeros_like(dw_ref)

            dw_ref[...] += jnp.sum(acc * xhat, axis=0, keepdims=True)
            return
        o_ref[...] = acc.astype(out_dtype)
        if res_norm is not None:
            res_ref, wr_ref = extra
            outs[0][...] = res_ref[...] + _rms(acc) * wr_ref[...]

    if b_blocked:
        b_spec = pl.BlockSpec((None, k, tn), lambda i, j: (j, 0, 0))
    elif nt:
        b_spec = pl.BlockSpec((tn, k), lambda i, j: (j, 0))
    else:
        b_spec = pl.BlockSpec((k, tn), lambda i, j: (0, j))
    a_spec = pl.BlockSpec((k, tm), lambda i, j: (0, i)) if ta else pl.BlockSpec((tm, k), lambda i, j: (i, 0))
    in_specs = [a_spec, b_spec]
    args = [a, b]
    if drelu_of is not None:
        in_specs.append(pl.BlockSpec((tm, tn), lambda i, j: (i, j)))
        args.append(drelu_of)
    if out_blocked:
        out_specs = [pl.BlockSpec((None, tm, tn), lambda i, j: (j, i, 0))]
        out_shape = [jax.ShapeDtypeStruct((n // tn, m, tn), out_dtype)]
    else:
        out_specs = [pl.BlockSpec((tm, tn), lambda i, j: (i, j))]
        out_shape = [jax.ShapeDtypeStruct((m, n), out_dtype)]
    scratch = []
    if a_norm is not None:
        assert not ta
        in_specs.append(pl.BlockSpec((1, k), lambda i, j: (0, 0)))
        args.append(a_norm)
        out_specs.append(pl.BlockSpec((tm, k), lambda i, j: (i, 0)))
        out_shape.append(jax.ShapeDtypeStruct((m, k), ACT_DTYPE))
        scratch.append(pltpu.VMEM((tm, k), ACT_DTYPE))
    if a_norm_bwd is not None:
        assert not ta
        in_specs += [pl.BlockSpec((tm, k), lambda i, j: (i, 0)), pl.BlockSpec((1, k), lambda i, j: (0, 0))]
        args += list(a_norm_bwd)
        out_specs += [pl.BlockSpec((tm, k), lambda i, j: (i, 0)), pl.BlockSpec((1, k), lambda i, j: (0, 0))]
        out_shape += [jax.ShapeDtypeStruct((m, k), ACT_DTYPE), jax.ShapeDtypeStruct((1, k), F32)]
        scratch.append(pltpu.VMEM((tm, k), ACT_DTYPE))
    if res_norm is not None:
        in_specs += [pl.BlockSpec((tm, n), lambda i, j: (i, 0)), pl.BlockSpec((1, n), lambda i, j: (0, 0))]
        args += list(res_norm)
        out_specs.append(pl.BlockSpec((tm, n), lambda i, j: (i, 0)))
        out_shape.append(jax.ShapeDtypeStruct((m, n), F32))
    if norm_bwd is not None:
        rows = pl.BlockSpec((tm, n), lambda i, j: (i, 0))
        in_specs += [rows, pl.BlockSpec((1, n), lambda i, j: (0, 0)), rows]
        args += list(norm_bwd)
        out_specs.append(pl.BlockSpec((1, n), lambda i, j: (0, 0)))
        out_shape.append(jax.ShapeDtypeStruct((1, n), F32))
    single = len(out_shape) == 1
    return _pcall(body, ride, grid=(m // tm, n // tn), in_specs=in_specs,
                  out_specs=out_specs[0] if single else out_specs, out_shape=out_shape[0] if single else out_shape,
                  scratch_shapes=scratch, semantics=("arbitrary", "arbitrary"), name=name)(*args)


ROW_TILE = 512
TM_FWD, TM_DX, TM_DW, TN = 2048, 1024, 1024, 512


def _norm_bwd(dy, x, w, *, out_dtype, add=None, name, ride=None):
    t, d = x.shape

    def body(*refs):
        dy_ref, x_ref, w_ref = refs[0], refs[1], refs[2]
        dx_ref, dw_ref = refs[-2], refs[-1]
        xv = x_ref[...]
        rstd = lax.rsqrt(jnp.mean(xv * xv, axis=-1, keepdims=True) + NORM_EPS)
        xhat = xv * rstd
        dyv = dy_ref[...].astype(F32)
        g = dyv * w_ref[...]
        dx = rstd * (g - xhat * jnp.mean(g * xhat, axis=-1, keepdims=True))
        if add is not None:
            dx = dx + refs[3][...]
        dx_ref[...] = dx.astype(out_dtype)

        @pl.when(pl.program_id(0) == 0)
        def _():
            dw_ref[...] = jnp.zeros_like(dw_ref)

        dw_ref[...] += jnp.sum(dyv * xhat, axis=0, keepdims=True)

    row = pl.BlockSpec((ROW_TILE, d), lambda i: (i, 0))
    vec = pl.BlockSpec((1, d), lambda i: (0, 0))
    in_specs = [row, row, vec] + ([row] if add is not None else [])
    args = [dy, x, w] + ([add] if add is not None else [])
    return _pcall(body, ride, grid=(t // ROW_TILE,), in_specs=in_specs, out_specs=[row, vec],
                  out_shape=[jax.ShapeDtypeStruct((t, d), out_dtype), jax.ShapeDtypeStruct((1, d), F32)],
                  semantics=("arbitrary",), name=name)(*args)


def _loss_fwd_bwd(y, target):
    t, d = y.shape

    def body(y_ref, t_ref, l_ref, dy_ref):
        diff = y_ref[...] - t_ref[...]
        dy_ref[...] = diff * (1.0 / d)

        @pl.when(pl.program_id(0) == 0)
        def _():
            l_ref[...] = jnp.zeros_like(l_ref)

        l_ref[...] += 0.5 * jnp.sum(jnp.mean(diff * diff, axis=-1, keepdims=True), axis=0, keepdims=True)

    row = pl.BlockSpec((ROW_TILE, d), lambda i: (i, 0))
    return pl.pallas_call(body, grid=(t // ROW_TILE,), in_specs=[row, row],
                          out_specs=[pl.BlockSpec((8, LANES), lambda i: (0, 0)), row],
                          out_shape=[jax.ShapeDtypeStruct((8, LANES), F32), jax.ShapeDtypeStruct((t, d), F32)],
                          compiler_params=_params("arbitrary"), name="loss")(y, target)


GLA_STATE = (GLA_HEADS * GLA_DV, GLA_KW)


def _gla_specs(chunk_of):
    rows = lambda width, col: pl.BlockSpec((CHUNK, width), lambda i: (chunk_of(i), col))
    const = lambda r, c: pl.BlockSpec((r, c), lambda i: (0, 0))
    return [rows(GLA_KW, 0),
            rows(GLA_KW, 1),
            rows(GROUP_WIDTH, 1),
            rows(GROUP_WIDTH, 0),
            rows(LANES, 4),
            const(LANES, GLA_KW),
            const(1, GLA_KW),
            const(1, GROUP_WIDTH)]


def _gla_chunk(q_ref, k_ref, v_ref, a_ref, wup_ref, ba_ref):
    z = _dot(a_ref[...], wup_ref[...]) + ba_ref[...]
    tri = (_iota((CHUNK, CHUNK), 1) <= _iota((CHUNK, CHUNK), 0)).astype(F32)
    cum = _dot_exact(tri, _log_sigmoid(z) * (1.0 / GLA_GATE_TAU))
    tot = cum[CHUNK - 1:CHUNK, :]
    e = jnp.exp(tot - cum)
    return (z, e, jnp.exp(tot), k_ref[...].astype(F32) * e, q_ref[...].astype(F32) * (GLA_DK ** -0.5),
            v_ref[...].astype(F32))


def _gla_head_mask():
    return _iota(GLA_STATE, 0) // GLA_DV == _iota(GLA_STATE, 1) // GLA_DK


def _gla_fwd(pmm, pel, w_up, b_a, gnorm_w, ride=None):
    t = pmm.shape[0]
    nc = t // CHUNK

    def body(q_ref, k_ref, v_ref, r_ref, a_ref, wup_ref, ba_ref, gw_ref, o_ref, st_ref, m_scr):
        @pl.when(pl.program_id(0) == 0)
        def _():
            m_scr[...] = jnp.zeros_like(m_scr)

        _, _, decay, kd, qs, vv = _gla_chunk(q_ref, k_ref, v_ref, a_ref, wup_ref, ba_ref)
        m = m_scr[...] * decay + jnp.where(_gla_head_mask(), _dot(vv, kd, 0, 0), 0.0)
        m_scr[...] = m
        st_ref[...] = m
        o = _dot(qs, m, 1, 1)
        rr = r_ref[...]
        gate = rr * jax.nn.sigmoid(rr) * gw_ref[...]
        for h in range(GLA_HEADS):
            vs = slice(h * GLA_DV, (h + 1) * GLA_DV)
            oh = o[:, vs]
            y = oh * lax.rsqrt(jnp.mean(oh * oh, axis=-1, keepdims=True) + NORM_EPS)
            o_ref[:, vs] = (y * gate[:, vs]).astype(o_ref.dtype)

    return _pcall(
        body, ride, grid=(nc,), in_specs=_gla_specs(lambda i: i),
        out_specs=[pl.BlockSpec((CHUNK, GROUP_WIDTH), lambda i: (i, 0)),
                   pl.BlockSpec((None,) + GLA_STATE, lambda i: (i, 0, 0))],
        out_shape=[jax.ShapeDtypeStruct((t, GROUP_WIDTH), ACT_DTYPE), jax.ShapeDtypeStruct((nc,) + GLA_STATE, F32)],
        scratch_shapes=[pltpu.VMEM(GLA_STATE, F32)],
        semantics=("arbitrary",), name="gla_fwd")(pmm, pmm, pmm, pel, pel, w_up, b_a, gnorm_w)


def _gla_bwd(pmm, pel, w_up, b_a, gnorm_w, states, dmix, ride=None):
    t = pmm.shape[0]
    nc = t // CHUNK
    scale = GLA_DK ** -0.5

    def body(q_ref, k_ref, v_ref, r_ref, a_ref, wup_ref, ba_ref, gw_ref, st_ref, prev_ref, do_ref,
             dq_ref, dk_ref, dv_ref, dr_ref, da_ref, dwup_ref, dba_ref, dgw_ref, dm_scr):
        step = pl.program_id(0)

        @pl.when(step == 0)
        def _():
            dm_scr[...] = jnp.zeros_like(dm_scr)
            dwup_ref[...] = jnp.zeros_like(dwup_ref)
            dba_ref[...] = jnp.zeros_like(dba_ref)
            dgw_ref[...] = jnp.zeros_like(dgw_ref)

        z, e, decay, kd, qs, vv = _gla_chunk(q_ref, k_ref, v_ref, a_ref, wup_ref, ba_ref)
        m = st_ref[...]
        m_prev = prev_ref[...] * (step < nc - 1).astype(F32)
        rr, dout, gw = r_ref[...], do_ref[...], gw_ref[...]
        sig = jax.nn.sigmoid(rr)
        silu = rr * sig
        dsilu = sig * (1.0 + rr * (1.0 - sig))
        o = _dot(qs, m, 1, 1)
        d_o, dgw = [], []
        for h in range(GLA_HEADS):
            vs = slice(h * GLA_DV, (h + 1) * GLA_DV)
            oh, dg = o[:, vs], dout[:, vs]
            rstd = lax.rsqrt(jnp.mean(oh * oh, axis=-1, keepdims=True) + NORM_EPS)
            y = oh * rstd
            dgw.append(jnp.sum(dg * y * silu[:, vs], axis=0, keepdims=True))
            dr_ref[:, vs] = (dg * y * gw[:, vs] * dsilu[:, vs]).astype(dr_ref.dtype)
            dy = dg * gw[:, vs] * silu[:, vs]
            d_o.append(rstd * (dy - y * jnp.mean(dy * y, axis=-1, keepdims=True)))
        d_o = jnp.concatenate(d_o, axis=1)
        dgw_ref[...] += jnp.concatenate(dgw, axis=1)
        dq_ref[...] = (_dot(d_o, m) * scale).astype(dq_ref.dtype)
        dm = dm_scr[...] + jnp.where(_gla_head_mask(), _dot(d_o, qs, 0, 0), 0.0)
        dv_ref[...] = _dot(kd, dm, 1, 1).astype(dv_ref.dtype)
        dkd = _dot(vv, dm)
        dk_ref[...] = (dkd * e).astype(dk_ref.dtype)
        dm_scr[...] = dm * decay
        tri_strict = (_iota((CHUNK, CHUNK), 1) < _iota((CHUNK, CHUNK), 0)).astype(F32)
        dla = jnp.sum(dm * m_prev, axis=0, keepdims=True) * decay + _dot_exact(tri_strict, dkd * kd)
        dz = dla * jax.nn.sigmoid(-z) * (1.0 / GLA_GATE_TAU)
        da_ref[...] = _dot(dz, wup_ref[...], 1, 1).astype(da_ref.dtype)
        dwup_ref[...] += _dot(a_ref[...], dz, 0, 0)
        dba_ref[...] += jnp.sum(dz, axis=0, keepdims=True)

    chunk_of = lambda i: nc - 1 - i
    in_specs = _gla_specs(chunk_of) + [
        pl.BlockSpec((None,) + GLA_STATE, lambda i: (chunk_of(i), 0, 0)),
        pl.BlockSpec((None,) + GLA_STATE, lambda i: (jnp.maximum(chunk_of(i) - 1, 0), 0, 0)),
        pl.BlockSpec((CHUNK, GROUP_WIDTH), lambda i: (chunk_of(i), 0))]
    rows = lambda width: pl.BlockSpec((CHUNK, width), lambda i: (chunk_of(i), 0))
    const = lambda r, c: pl.BlockSpec((r, c), lambda i: (0, 0))
    return _pcall(
        body, ride, grid=(nc,), in_specs=in_specs,
        out_specs=[rows(GLA_KW), rows(GLA_KW), rows(GROUP_WIDTH), rows(GROUP_WIDTH), rows(LANES),
                   const(LANES, GLA_KW), const(1, GLA_KW), const(1, GROUP_WIDTH)],
        out_shape=[jax.ShapeDtypeStruct((t, GLA_KW), ACT_DTYPE), jax.ShapeDtypeStruct((t, GLA_KW), ACT_DTYPE),
                   jax.ShapeDtypeStruct((t, GROUP_WIDTH), ACT_DTYPE), jax.ShapeDtypeStruct((t, GROUP_WIDTH), ACT_DTYPE),
                   jax.ShapeDtypeStruct((t, LANES), ACT_DTYPE), jax.ShapeDtypeStruct((LANES, GLA_KW), F32),
                   jax.ShapeDtypeStruct((1, GLA_KW), F32), jax.ShapeDtypeStruct((1, GROUP_WIDTH), F32)],
        scratch_shapes=[pltpu.VMEM(GLA_STATE, F32)],
        semantics=("arbitrary",), name="gla_bwd")(
            pmm, pmm, pmm, pel, pel, w_up, b_a, gnorm_w, states, states, dmix)


CUM_BLOCK = 256


def _fox_gate_fwd(pel, b_f):
    t = pel.shape[0]
    nb = t // CUM_BLOCK

    def body(f_ref, b_ref, cum_ref, cum_t_ref):
        tri = (_iota((CUM_BLOCK, CUM_BLOCK), 1) <= _iota((CUM_BLOCK, CUM_BLOCK), 0)).astype(F32)
        carry = jnp.zeros((1, LANES), F32)
        for blk in range(nb):
            rows = slice(blk * CUM_BLOCK, (blk + 1) * CUM_BLOCK)
            cum = _dot_exact(tri, _log_sigmoid(f_ref[rows, :] + b_ref[...])) + carry
            cum_ref[rows, :] = cum
            cum_t_ref[blk] = cum.T[:ATT_HEADS, :]
            carry = cum[CUM_BLOCK - 1:CUM_BLOCK, :]

    return pl.pallas_call(
        body, grid=(1,),
        in_specs=[pl.BlockSpec((t, LANES), lambda i: (0, 5)), pl.BlockSpec((1, LANES), lambda i: (0, 0))],
        out_specs=[pl.BlockSpec((t, LANES), lambda i: (0, 0)),
                   pl.BlockSpec((nb, ATT_HEADS, CUM_BLOCK), lambda i: (0, 0, 0))],
        out_shape=[jax.ShapeDtypeStruct((t, LANES), F32), jax.ShapeDtypeStruct((nb, ATT_HEADS, CUM_BLOCK), F32)],
        compiler_params=_params("arbitrary"), name="fox_gate_fwd")(pel, b_f)


def _fox_gate_bwd(pel, b_f, dcum_t, dcum_q):
    t = pel.shape[0]
    nb = t // CUM_BLOCK

    def body(f_ref, b_ref, dct_ref, dcq_ref, df_ref, db_ref):
        tri_up = (_iota((CUM_BLOCK, CUM_BLOCK), 1) >= _iota((CUM_BLOCK, CUM_BLOCK), 0)).astype(F32)
        carry = jnp.zeros((1, LANES), F32)
        db = jnp.zeros((1, LANES), F32)
        for blk in reversed(range(nb)):
            rows = slice(blk * CUM_BLOCK, (blk + 1) * CUM_BLOCK)
            query_side = sum(dcq_ref[pair, rows, :] for pair in range(dcq_ref.shape[0]))
            dls = _dot_exact(tri_up, dct_ref[blk].T + query_side) + carry
            carry = dls[0:1, :]
            df = dls * jax.nn.sigmoid(-(f_ref[rows, :] + b_ref[...]))
            df_ref[rows, :] = df.astype(df_ref.dtype)
            db = db + jnp.sum(df, axis=0, keepdims=True)
        db_ref[...] = db

    return pl.pallas_call(
        body, grid=(1,),
        in_specs=[pl.BlockSpec((t, LANES), lambda i: (0, 5)), pl.BlockSpec((1, LANES), lambda i: (0, 0)),
                  pl.BlockSpec((nb, LANES, CUM_BLOCK), lambda i: (0, 0, 0)),
                  pl.BlockSpec((dcum_q.shape[0], t, LANES), lambda i: (0, 0, 0))],
        out_specs=[pl.BlockSpec((t, LANES), lambda i: (0, 0)), pl.BlockSpec((1, LANES), lambda i: (0, 0))],
        out_shape=[jax.ShapeDtypeStruct((t, LANES), ACT_DTYPE), jax.ShapeDtypeStruct((1, LANES), F32)],
        compiler_params=_params("arbitrary"), name="fox_gate_bwd")(pel, b_f, dcum_t, dcum_q)


FOX_Q_BLOCK = 256


assert FOX_Q_BLOCK == CUM_BLOCK
FOX_KEY_STEP = 512


def _fox_scores(q_ref, k_ref, cum_ref, cum_t_ref, h, i):
    hs = slice(h * HEAD_DIM, (h + 1) * HEAD_DIM)
    nb = cum_t_ref.shape[0]
    key_gate = jnp.concatenate([cum_t_ref[kb, h:h + 1, :] for kb in range(nb)], axis=1)
    s = _dot(q_ref[:, hs], k_ref[:, hs], 1, 1) * (HEAD_DIM ** -0.5) + (cum_ref[:, h:h + 1] - key_gate)
    shape = (FOX_Q_BLOCK, nb * FOX_Q_BLOCK)
    return jnp.where(_iota(shape, 1) <= i * FOX_Q_BLOCK + _iota(shape, 0), s, NEG)


def _fox_specs(t):
    bq, nb = FOX_Q_BLOCK, t // FOX_Q_BLOCK
    return [pl.BlockSpec((bq, GROUP_WIDTH), lambda i: (i, 2)), pl.BlockSpec((t, GROUP_WIDTH), lambda i: (0, 3)),
            pl.BlockSpec((t, GROUP_WIDTH), lambda i: (0, 4)), pl.BlockSpec((bq, LANES), lambda i: (i, 0)),
            pl.BlockSpec((nb, ATT_HEADS, bq), lambda i: (0, 0, 0))]


def _fox_fwd(pmm, cum, cum_t, ride=None):
    t = pmm.shape[0]
    bq = FOX_Q_BLOCK

    def body(q_ref, k_ref, v_ref, cum_ref, cum_t_ref, o_ref, lse_ref):
        i = pl.program_id(0)
        lse_ref[...] = jnp.zeros_like(lse_ref)
        for h in range(ATT_HEADS):
            hs = slice(h * HEAD_DIM, (h + 1) * HEAD_DIM)
            s = _fox_scores(q_ref, k_ref, cum_ref, cum_t_ref, h, i)
            m = jnp.max(s, axis=-1, keepdims=True)
            p = jnp.exp(s - m)
            l = jnp.sum(p, axis=-1, keepdims=True)
            o_ref[:, hs] = (_dot(p, v_ref[:, hs]) / l).astype(o_ref.dtype)
            lse_ref[:, h:h + 1] = m + jnp.log(l)

    return _pcall(
        body, ride, grid=(t // bq,), in_specs=_fox_specs(t),
        out_specs=[pl.BlockSpec((bq, GROUP_WIDTH), lambda i: (i, 0)), pl.BlockSpec((bq, LANES), lambda i: (i, 0))],
        out_shape=[jax.ShapeDtypeStruct((t, GROUP_WIDTH), ACT_DTYPE), jax.ShapeDtypeStruct((t, LANES), F32)],
        semantics=("parallel",), name="fox_fwd")(pmm, pmm, pmm, cum, cum_t)


def _fox_bwd(pmm, cum, cum_t, lse, dmix, ride=None):
    t = pmm.shape[0]
    bq, nb = FOX_Q_BLOCK, t // FOX_Q_BLOCK
    pairs, per_pair = ATT_HEADS // 2, LANES // HEAD_DIM
    scale = HEAD_DIM ** -0.5

    def body(q_ref, k_ref, v_ref, cum_ref, cum_t_ref, lse_ref, do_ref, dq_ref, dk_ref, dv_ref, dct_ref, dcq_ref):
        g, i = pl.program_id(0), pl.program_id(1)

        @pl.when(i == 0)
        def _():
            dk_ref[...] = jnp.zeros_like(dk_ref)
            dv_ref[...] = jnp.zeros_like(dv_ref)

        @pl.when((i == 0) & (g == 0))
        def _():
            dct_ref[...] = jnp.zeros_like(dct_ref)

        lane = _iota((1, LANES), 1)

        def run(n):
            causal = _iota((bq, n), 1) <= i * bq + _iota((bq, n), 0)
            dcq = jnp.zeros((bq, LANES), F32)
            for hh in range(per_pair):
                h = per_pair * g + hh
                hs = slice(hh * HEAD_DIM, (hh + 1) * HEAD_DIM)
                pick = (lane == h).astype(F32)
                cq = jnp.sum(cum_ref[...] * pick, axis=1, keepdims=True)
                lse_h = jnp.sum(lse_ref[...] * pick, axis=1, keepdims=True)
                key_gate = jnp.concatenate([cum_t_ref[kb, pl.ds(h, 1), :] for kb in range(n // bq)], axis=1)
                s = _dot(q_ref[:, hs], k_ref[:n, hs], 1, 1) * scale + (cq - key_gate)
                p = jnp.exp(jnp.where(causal, s, NEG) - lse_h)
                do = do_ref[:, hs]
                dp = _dot(do, v_ref[:n, hs], 1, 1)
                ds = p * (dp - jnp.sum(p * dp, axis=-1, keepdims=True))
                dq_ref[:, hs] = (_dot(ds, k_ref[:n, hs]) * scale).astype(dq_ref.dtype)
                dk_ref[:n, hs] += _dot(ds, q_ref[:, hs], 0, 0) * scale
                dv_ref[:n, hs] += _dot(p, do, 0, 0)
                key_side = -jnp.sum(ds, axis=0, keepdims=True)
                for kb in range(n // bq):
                    dct_ref[kb, pl.ds(h, 1), :] += key_side[:, kb * bq:(kb + 1) * bq]
                dcq = dcq + jnp.sum(ds, axis=1, keepdims=True) * pick
            dcq_ref[...] = dcq

        for kx in range(t // FOX_KEY_STEP):
            pl.when(i // (FOX_KEY_STEP // bq) == kx)(functools.partial(run, (kx + 1) * FOX_KEY_STEP))

    cols = lambda first: pl.BlockSpec((bq, LANES), lambda g, i: (i, first + g))
    keys = lambda first: pl.BlockSpec((t, LANES), lambda g, i: (0, first + g))
    per_head = pl.BlockSpec((bq, LANES), lambda g, i: (i, 0))
    fox_q, fox_k, fox_v = (GROUP_WIDTH * n // LANES for n in (2, 3, 4))
    return _pcall(
        body, ride, grid=(pairs, t // bq),
        in_specs=[cols(fox_q), keys(fox_k), keys(fox_v), per_head,
                  pl.BlockSpec((nb, ATT_HEADS, bq), lambda g, i: (0, 0, 0)), per_head, cols(GROUP_WIDTH // LANES)],
        out_specs=[cols(0), keys(0), keys(0), pl.BlockSpec((nb, LANES, bq), lambda g, i: (0, 0, 0)),
                   pl.BlockSpec((None, bq, LANES), lambda g, i: (g, i, 0))],
        out_shape=[jax.ShapeDtypeStruct((t, GROUP_WIDTH), ACT_DTYPE), jax.ShapeDtypeStruct((t, GROUP_WIDTH), F32),
                   jax.ShapeDtypeStruct((t, GROUP_WIDTH), F32), jax.ShapeDtypeStruct((nb, LANES, bq), F32),
                   jax.ShapeDtypeStruct((pairs, t, LANES), F32)],
        semantics=("arbitrary", "arbitrary"), name="fox_bwd")(pmm, pmm, pmm, cum, cum_t, lse, dmix)


CA_Q_BLOCK = 4 * CHUNK
CA_WINDOW = CA_Q_BLOCK + CA_LEFT
CA_BASE = 1024


def _ca_bias_base(rel_bias):
    n = rel_bias.shape[0]
    flat = CA_Q_BLOCK + CA_LEFT - REL_CLIP
    tail = CA_BASE - flat - (2 * REL_CLIP + 1)
    return jnp.concatenate([jnp.broadcast_to(rel_bias[:, 2 * REL_CLIP:], (n, flat)), rel_bias[:, ::-1],
                            jnp.broadcast_to(rel_bias[:, :1], (n, tail))], axis=1)


def _ca_bias_base_grad(dbase):
    flat = CA_Q_BLOCK + CA_LEFT - REL_CLIP
    mid = dbase[:, flat:flat + 2 * REL_CLIP + 1][:, ::-1]
    lo = jnp.sum(dbase[:, flat + 2 * REL_CLIP + 1:], axis=1, keepdims=True)
    hi = jnp.sum(dbase[:, :flat], axis=1, keepdims=True)
    pad = jnp.zeros((dbase.shape[0], 2 * REL_CLIP - 1), F32)
    return mid + jnp.concatenate([lo, pad, hi], axis=1)


def _ca_mask(i):
    r, j = _iota((CA_Q_BLOCK, CA_WINDOW), 0), _iota((CA_Q_BLOCK, CA_WINDOW), 1)
    rc, jc = r // CHUNK, j // CHUNK
    return (jc >= rc) & (jc <= rc + CA_LEFT // CHUNK) & (i * CA_Q_BLOCK + j >= CA_LEFT)


def _ca_fill_bias(i, base_ref, bias_scr):
    @pl.when(i == 0)
    def _():
        for h in range(ATT_HEADS):
            rows = jnp.broadcast_to(base_ref[h:h + 1, :], (CA_Q_BLOCK, CA_BASE))
            bias_scr[h] = pltpu.roll(rows, CA_BASE - CA_Q_BLOCK, 1, stride=1, stride_axis=0)[:, :CA_WINDOW]


def _ca_scores(q_ref, kp_ref, bias_scr, win, h, mask):
    hs = slice(h * HEAD_DIM, (h + 1) * HEAD_DIM)
    s = _dot(q_ref[:, hs], kp_ref[win, hs], 1, 1) * (HEAD_DIM ** -0.5)
    return jnp.where(mask, s + bias_scr[h], NEG)


CA_BIAS_SCRATCH = pltpu.VMEM((ATT_HEADS, CA_Q_BLOCK, CA_WINDOW), F32)


def _ca_fwd(pmm, kp, vp, base, ride=None):
    t = pmm.shape[0]

    def body(q_ref, kp_ref, vp_ref, base_ref, o_ref, lse_ref, bias_scr):
        i = pl.program_id(0)
        _ca_fill_bias(i, base_ref, bias_scr)
        win = pl.ds(pl.multiple_of(i * CA_Q_BLOCK, CA_Q_BLOCK), CA_WINDOW)
        mask = _ca_mask(i)
        lse_ref[...] = jnp.zeros_like(lse_ref)
        for h in range(ATT_HEADS):
            hs = slice(h * HEAD_DIM, (h + 1) * HEAD_DIM)
            s = _ca_scores(q_ref, kp_ref, bias_scr, win, h, mask)
            m = jnp.max(s, axis=-1, keepdims=True)
            p = jnp.exp(s - m)
            l = jnp.sum(p, axis=-1, keepdims=True)
            o_ref[:, hs] = (_dot(p, vp_ref[win, hs]) / l).astype(o_ref.dtype)
            lse_ref[:, h:h + 1] = m + jnp.log(l)

    padded = pl.BlockSpec((t + CA_LEFT, GROUP_WIDTH), lambda i: (0, 0))
    return _pcall(
        body, ride, grid=(t // CA_Q_BLOCK,),
        in_specs=[pl.BlockSpec((CA_Q_BLOCK, GROUP_WIDTH), lambda i: (i, 0)), padded, padded,
                  pl.BlockSpec((ATT_HEADS, CA_BASE), lambda i: (0, 0))],
        out_specs=[pl.BlockSpec((CA_Q_BLOCK, GROUP_WIDTH), lambda i: (i, 0)),
                   pl.BlockSpec((CA_Q_BLOCK, LANES), lambda i: (i, 0))],
        out_shape=[jax.ShapeDtypeStruct((t, GROUP_WIDTH), ACT_DTYPE), jax.ShapeDtypeStruct((t, LANES), F32)],
        scratch_shapes=[CA_BIAS_SCRATCH], semantics=("arbitrary",), name="ca_fwd")(pmm, kp, vp, base)


def _ca_bwd(pmm, kp, vp, base, lse, dmix, ride=None):
    t = pmm.shape[0]
    scale = HEAD_DIM ** -0.5

    def body(q_ref, kp_ref, vp_ref, base_ref, lse_ref, do_ref, dq_ref, dkp_ref, dvp_ref, dbase_ref, bias_scr):
        i = pl.program_id(0)
        _ca_fill_bias(i, base_ref, bias_scr)

        @pl.when(i == 0)
        def _():
            dkp_ref[...] = jnp.zeros_like(dkp_ref)
            dvp_ref[...] = jnp.zeros_like(dvp_ref)
            dbase_ref[...] = jnp.zeros_like(dbase_ref)

        win = pl.ds(pl.multiple_of(i * CA_Q_BLOCK, CA_Q_BLOCK), CA_WINDOW)
        mask = _ca_mask(i)
        flip = (_iota((CA_Q_BLOCK, CA_Q_BLOCK), 0) + _iota((CA_Q_BLOCK, CA_Q_BLOCK), 1) == CA_Q_BLOCK - 1).astype(F32)
        for h in range(ATT_HEADS):
            hs = slice(h * HEAD_DIM, (h + 1) * HEAD_DIM)
            s = _ca_scores(q_ref, kp_ref, bias_scr, win, h, mask)
            p = jnp.exp(s - lse_ref[:, h:h + 1])
            do = do_ref[:, hs]
            dp = _dot(do, vp_ref[win, hs], 1, 1)
            ds = p * (dp - jnp.sum(p * dp, axis=-1, keepdims=True))
            dq_ref[:, hs] = (_dot(ds, kp_ref[win, hs]) * scale).astype(dq_ref.dtype)
            dkp_ref[win, hs] += _dot(ds, q_ref[:, hs], 0, 0) * scale
            dvp_ref[win, hs] += _dot(p, do, 0, 0)
            rev = jnp.concatenate([_dot(flip, ds), jnp.zeros((CA_Q_BLOCK, CA_BASE - CA_WINDOW), F32)], axis=1)
            lined = pltpu.roll(rev, 1, 1, stride=1, stride_axis=0)
            dbase_ref[h:h + 1, :] += jnp.sum(lined, axis=0, keepdims=True)

    padded = pl.BlockSpec((t + CA_LEFT, GROUP_WIDTH), lambda i: (0, 0))
    return _pcall(
        body, ride, grid=(t // CA_Q_BLOCK,),
        in_specs=[pl.BlockSpec((CA_Q_BLOCK, GROUP_WIDTH), lambda i: (i, 0)), padded, padded,
                  pl.BlockSpec((ATT_HEADS, CA_BASE), lambda i: (0, 0)),
                  pl.BlockSpec((CA_Q_BLOCK, LANES), lambda i: (i, 0)),
                  pl.BlockSpec((CA_Q_BLOCK, GROUP_WIDTH), lambda i: (i, 0))],
        out_specs=[pl.BlockSpec((CA_Q_BLOCK, GROUP_WIDTH), lambda i: (i, 0)), padded, padded,
                   pl.BlockSpec((ATT_HEADS, CA_BASE), lambda i: (0, 0))],
        out_shape=[jax.ShapeDtypeStruct((t, GROUP_WIDTH), ACT_DTYPE),
                   jax.ShapeDtypeStruct((t + CA_LEFT, GROUP_WIDTH), F32),
                   jax.ShapeDtypeStruct((t + CA_LEFT, GROUP_WIDTH), F32),
                   jax.ShapeDtypeStruct((ATT_HEADS, CA_BASE), F32)],
        scratch_shapes=[CA_BIAS_SCRATCH], semantics=("arbitrary",), name="ca_bwd")(pmm, kp, vp, base, lse, dmix)


GELU_C = 0.7978845608028654
GELU_A = 0.044715


def _shift_down(v, k, fill, period=None):
    rows = _iota(v.shape, 0)
    rows = rows if period is None else rows & (period - 1)
    return jnp.where(rows >= k, pltpu.roll(v, k, 0), fill)


def _shift_up(v, k, fill, period=None):
    t = v.shape[0]
    rows = _iota(v.shape, 0)
    rows, length = (rows, t) if period is None else (rows & (period - 1), period)
    return jnp.where(rows < length - k, pltpu.roll(v, t - k, 0), fill)


LRU_SCAN_BLOCK = 256


def _linear_scan(a, b, reverse=False):
    shift = _shift_up if reverse else _shift_down
    k = 1
    while k < LRU_SCAN_BLOCK:
        b = a * shift(b, k, 0.0, LRU_SCAN_BLOCK) + b
        a = a * shift(a, k, 1.0, LRU_SCAN_BLOCK)
        k *= 2
    nb = a.shape[0] // LRU_SCAN_BLOCK
    carry = jnp.zeros((1, a.shape[1]), F32)
    out = [None] * nb
    for blk in (reversed(range(nb)) if reverse else range(nb)):
        rows = slice(blk * LRU_SCAN_BLOCK, (blk + 1) * LRU_SCAN_BLOCK)
        h = b[rows] + a[rows] * carry
        out[blk] = h
        carry = h[0:1] if reverse else h[LRU_SCAN_BLOCK - 1:LRU_SCAN_BLOCK]
    return jnp.concatenate(out, axis=0)


def _neg_expm1(y):
    series = -y * (1.0 + y * (0.5 + y * (1.0 / 6.0 + y * (1.0 / 24.0 + y * (1.0 / 120.0)))))
    return jnp.where(y > -0.1, series, 1.0 - jnp.exp(y))


def _lru_forward(x, g_in, cw, cb, wa, ba, wx, bx, lam):
    xs = [_shift_down(x, CONV_WIDTH - 1 - j, 0.0) for j in range(CONV_WIDTH - 1)] + [x]
    xc = cb + sum(cw[j:j + 1, :] * xs[j] for j in range(CONV_WIDTH))
    r = jax.nn.sigmoid(_dot(xc, wa) + ba)
    i = jax.nn.sigmoid(_dot(xc, wx) + bx)
    lsl = _log_sigmoid(lam)
    la = LRU_C * r * lsl
    a = jnp.exp(la)
    s = jnp.sqrt(_neg_expm1(2.0 * la))
    h = _linear_scan(a, s * (i * xc))
    u = GELU_C * (g_in + GELU_A * g_in * g_in * g_in)
    th = jnp.tanh(u)
    gelu = 0.5 * g_in * (1.0 + th)
    return xs, xc, r, i, lsl, a, s, h, th, gelu


def _lru_specs(t):
    col = lambda off: pl.BlockSpec((t, LANES), lambda j: (0, j + off))
    vec = pl.BlockSpec((1, LANES), lambda j: (0, j))
    mat = pl.BlockSpec((None, LANES, LANES), lambda j: (j, 0, 0))
    return [col(0), col(GROUP_WIDTH // LANES), pl.BlockSpec((CONV_WIDTH, LANES), lambda j: (0, j)),
            vec, mat, vec, mat, vec, vec]


def _lru_fwd(pel, conv_w, conv_b, wa, ba, wx, bx, lam, ride=None):
    t = pel.shape[0]

    def body(g_ref, x_ref, cw_ref, cb_ref, wa_ref, ba_ref, wx_ref, bx_ref, lam_ref, o_ref):
        res = _lru_forward(x_ref[...], g_ref[...], cw_ref[...], cb_ref[...], wa_ref[...], ba_ref[...],
                           wx_ref[...], bx_ref[...], lam_ref[...])
        o_ref[...] = (res[7] * res[9]).astype(o_ref.dtype)

    return _pcall(
        body, ride, grid=(GROUP_WIDTH // LANES,), in_specs=_lru_specs(t),
        out_specs=pl.BlockSpec((t, LANES), lambda j: (0, j)),
        out_shape=jax.ShapeDtypeStruct((t, GROUP_WIDTH), ACT_DTYPE),
        semantics=("parallel",), name="lru_fwd")(pel, pel, conv_w, conv_b, wa, ba, wx, bx, lam)


def _lru_bwd(pel, conv_w, conv_b, wa, ba, wx, bx, lam, dmix, ride=None):
    t = pel.shape[0]

    def body(g_ref, x_ref, cw_ref, cb_ref, wa_ref, ba_ref, wx_ref, bx_ref, lam_ref, do_ref,
             dg_ref, dx_ref, dcw_ref, dcb_ref, dwa_ref, dba_ref, dwx_ref, dbx_ref, dlam_ref):
        g_in, cw, lam = g_ref[...], cw_ref[...], lam_ref[...]
        xs, xc, r, i, lsl, a, s, h, th, gelu = _lru_forward(
            x_ref[...], g_in, cw, cb_ref[...], wa_ref[...], ba_ref[...], wx_ref[...], bx_ref[...], lam)
        dout = do_ref[...]
        dgelu = 0.5 * (1.0 + th) + 0.5 * g_in * (1.0 - th * th) * GELU_C * (1.0 + 3.0 * GELU_A * g_in * g_in)
        dg_ref[...] = (dout * h * dgelu).astype(dg_ref.dtype)
        gsum = _linear_scan(_shift_up(a, 1, 0.0), dout * gelu, reverse=True)
        da = gsum * _shift_down(h, 1, 0.0)
        di = gsum * s * xc
        dla = da * a - gsum * (i * xc) * (a * a / s)
        dlam_ref[...] = jnp.sum(dla * (LRU_C * r), axis=0, keepdims=True) * jax.nn.sigmoid(-lam)
        dpr = dla * (LRU_C * lsl) * r * (1.0 - r)
        dpi = di * i * (1.0 - i)
        dxc = gsum * s * i + _dot(dpr, wa_ref[...], 1, 1) + _dot(dpi, wx_ref[...], 1, 1)
        xct = xc.T
        dwa_ref[...] = _dot(xct, dpr)
        dwx_ref[...] = _dot(xct, dpi)
        dba_ref[...] = jnp.sum(dpr, axis=0, keepdims=True)
        dbx_ref[...] = jnp.sum(dpi, axis=0, keepdims=True)
        dcb_ref[...] = jnp.sum(dxc, axis=0, keepdims=True)
        for j in range(CONV_WIDTH):
            dcw_ref[j:j + 1, :] = jnp.sum(dxc * xs[j], axis=0, keepdims=True)
        dx = cw[CONV_WIDTH - 1:CONV_WIDTH, :] * dxc
        for j in range(CONV_WIDTH - 1):
            dx = dx + cw[j:j + 1, :] * _shift_up(dxc, CONV_WIDTH - 1 - j, 0.0)
        dx_ref[...] = dx.astype(dx_ref.dtype)

    col = pl.BlockSpec((t, LANES), lambda j: (0, j))
    vec = pl.BlockSpec((1, LANES), lambda j: (0, j))
    mat = pl.BlockSpec((None, LANES, LANES), lambda j: (j, 0, 0))
    nb = GROUP_WIDTH // LANES
    vshape = jax.ShapeDtypeStruct((1, GROUP_WIDTH), F32)
    mshape = jax.ShapeDtypeStruct((nb, LANES, LANES), F32)
    return _pcall(
        body, ride, grid=(nb,),
        in_specs=_lru_specs(t) + [pl.BlockSpec((t, LANES), lambda j: (0, j + nb))],
        out_specs=[col, col, pl.BlockSpec((CONV_WIDTH, LANES), lambda j: (0, j)), vec, mat, vec, mat, vec, vec],
        out_shape=[jax.ShapeDtypeStruct((t, GROUP_WIDTH), ACT_DTYPE), jax.ShapeDtypeStruct((t, GROUP_WIDTH), ACT_DTYPE),
                   jax.ShapeDtypeStruct((CONV_WIDTH, GROUP_WIDTH), F32), vshape, mshape, vshape, mshape, vshape, vshape],
        semantics=("parallel",), name="lru_bwd")(
            pel, pel, conv_w, conv_b, wa, ba, wx, bx, lam, dmix)


def _block_diag_pairs(w):
    z = jnp.zeros((LRU_BLOCK_DIM, LRU_BLOCK_DIM), w.dtype)
    return jnp.stack([jnp.block([[w[2 * j], z], [z, w[2 * j + 1]]]) for j in range(w.shape[0] // 2)])


def _block_diag_pairs_grad(dw):
    b = LRU_BLOCK_DIM
    return jnp.stack([dw[n // 2, (n % 2) * b:(n % 2 + 1) * b, (n % 2) * b:(n % 2 + 1) * b] for n in range(2 * dw.shape[0])])


def _row_tile(r):
    return ROW_TILE if r % ROW_TILE == 0 else r


def _pair_sum(g, got, place, name):
    _, r, c = g.shape
    tile = r

    def body(place_ref, a_ref, b_ref, o_ref):
        o_ref[...] = (a_ref[...].astype(F32) + b_ref[...].astype(F32)).astype(o_ref.dtype)

    blk = pl.BlockSpec((1, tile, c), lambda k, i, place_ref: (k, i, 0))
    return pl.pallas_call(
        body,
        grid_spec=pltpu.PrefetchScalarGridSpec(
            num_scalar_prefetch=1, grid=(N_CHIPS, r // tile),
            in_specs=[pl.BlockSpec((1, tile, c), lambda k, i, place_ref: (2 * k + place_ref[0], i, 0)), blk],
            out_specs=blk),
        out_shape=jax.ShapeDtypeStruct(got.shape, got.dtype),
        compiler_params=_params("parallel", "parallel"), name=name)(place, g, got)


def _adamw_update(g, w_ref, m_ref, v_ref, g_ref, d_ref, nm_ref, nv_ref):
    nm = ADAM_B1 * m_ref[...] + (1.0 - ADAM_B1) * g
    nv = ADAM_B2 * v_ref[...] + (1.0 - ADAM_B2) * jnp.square(g)
    m_hat = nm / (1.0 - ADAM_B1 ** ADAM_STEP)
    v_hat = nv / (1.0 - ADAM_B2 ** ADAM_STEP)
    g_ref[...] = g
    d_ref[...] = -ADAM_LR * (m_hat / (jnp.sqrt(v_hat) + ADAM_EPS) + ADAM_WD * w_ref[...])
    nm_ref[...] = nm
    nv_ref[...] = nv


def _adamw_sharded(parts, w, m, v, place, name, ride=None):
    n_layers, r, c = w.shape
    tile = _row_tile(r)
    nb = r // tile
    counts = [1 + len(recvs) for _, recvs in parts]

    def body(place_ref, *refs):
        layer = pl.program_id(0)
        g, at = None, 0
        for l in range(n_layers):
            g_l = refs[at][0].astype(F32)
            for r_ref in refs[at + 1:at + counts[l]]:
                for k in range(r_ref.shape[0]):
                    g_l = g_l + r_ref[k].astype(F32)
            g = g_l if g is None else jnp.where(layer == l, g_l, g)
            at += counts[l]
        _adamw_update(g, *refs[at:])

    def part_specs(l, recvs):
        rows = lambda q, i: jnp.where(q < l, 0, jnp.where(q > l, nb - 1, i))
        return ([pl.BlockSpec((1, tile, c), lambda q, i, place_ref: (place_ref[1], rows(q, i), 0))] +
                [pl.BlockSpec((a.shape[0], tile, c), lambda q, i, place_ref: (0, rows(q, i), 0)) for a in recvs])

    in_specs, args = [], []
    for l, (s, recvs) in enumerate(parts):
        in_specs += part_specs(l, recvs)
        args += [s, *recvs]
    blk = pl.BlockSpec((None, tile, c), lambda q, i, place_ref: (q, i, 0))
    out = jax.ShapeDtypeStruct((n_layers, r, c), F32)
    return _pcall(body, ride, grid=(n_layers, nb), in_specs=in_specs + [blk, blk, blk], out_specs=[blk, blk, blk, blk],
                  out_shape=[out, out, out, out], semantics=("arbitrary", "arbitrary"), name=name, prefetch=True)(
                      place, *args, w, m, v)


def _adamw_small(repl_parts, vec_parts, w, m, v, place):
    n_r, n = len(repl_parts), len(w)
    shapes = [a.shape for a in w]

    def body(place_ref, *refs):
        parts, rest = refs[:n], refs[n:]
        for k in range(n):
            take = (lambda p: parts[k][p]) if k < n_r else (lambda p: parts[k][p, 0])
            g = take(0)
            for p in range(1, N_DEV):
                g = g + take(p)
            _adamw_update(g, rest[k], rest[n + k], rest[2 * n + k], *rest[3 * n + 4 * k:3 * n + 4 * k + 4])

    def whole(shape):
        return pl.BlockSpec(shape, lambda i, place_ref: (0,) * len(shape))

    def mine(shard):
        return pl.BlockSpec((N_DEV, 1) + shard, lambda i, place_ref: (0, place_ref[2]) + (0,) * len(shard))

    in_specs = [whole(a.shape) for a in repl_parts] + [mine(s) for s in shapes[n_r:]] + [whole(s) for s in shapes] * 3
    outs = pl.pallas_call(
        body,
        grid_spec=pltpu.PrefetchScalarGridSpec(
            num_scalar_prefetch=1, grid=(1,), in_specs=in_specs,
            out_specs=[whole(s) for s in shapes for _ in range(4)]),
        out_shape=[jax.ShapeDtypeStruct(s, F32) for s in shapes for _ in range(4)],
        compiler_params=_params("arbitrary"), name="adamw_small")(place, *repl_parts, *vec_parts, *w, *m, *v)
    return [outs[4 * k:4 * k + 4] for k in range(n)]


SHARDED = {"norm_w": 2, "w_in_even": 2, "gla_w_a_up": 2, "w_out_even": 1, "w_in_odd": 2, "conv_w": 2, "conv_b": 1,
           "lru_b_a": 1, "lru_b_x": 1, "lru_lambda": 1, "w_out_odd": 1, "w_mlp_up": 2, "w_mlp_down": 1}
REPLICATED = ["gla_b_a", "gla_norm_w", "fox_b_f", "rel_bias", "lru_w_a", "lru_w_x"]
WEIGHTS = ["norm_w", "w_in_even", "gla_w_a_up", "gla_b_a", "gla_norm_w", "fox_b_f", "w_out_even", "w_in_odd",
           "rel_bias", "conv_w", "conv_b", "lru_w_a", "lru_b_a", "lru_w_x", "lru_b_x", "lru_lambda", "w_out_odd",
           "w_mlp_up", "w_mlp_down"]
MATRICES = ("w_in_even", "w_out_even", "w_in_odd", "w_out_odd", "w_mlp_up", "w_mlp_down")
TRANSPOSED = ("w_in_even", "w_in_odd")
VECTORS = tuple(n for n in SHARDED if n not in MATRICES)
MATRIX_BLOCKS = (("w_in_even", 0), ("w_out_even", 0), ("w_in_odd", 0), ("w_out_odd", 0),
                 ("w_mlp_up", 0), ("w_mlp_up", 1), ("w_mlp_down", 0), ("w_mlp_down", 1))


def _join_shards(blocks, axis):
    moved = jnp.moveaxis(blocks, 0, axis)
    shape = moved.shape
    return moved.reshape(shape[:axis] + (shape[axis] * shape[axis + 1],) + shape[axis + 2:])


def _split_shards(full, axis):
    shape = full.shape
    cut = full.reshape(shape[:axis] + (N_DEV, shape[axis] // N_DEV) + shape[axis + 1:])
    return jnp.moveaxis(cut, axis, 0)


EVEN_SPLITS = (0, 256, 512, 1024, 1536, 1552, 2064, 2576, 3088, 3096)


def _even_in_split(wt):
    c = [wt[EVEN_SPLITS[k]:EVEN_SPLITS[k + 1]] for k in range(9)]
    gq, gk, gv, gr, ga, fq, fk, fv, ff = c
    padrows = lambda a: jnp.pad(a, ((0, LANES - a.shape[0]), (0, 0)))
    return jnp.concatenate([gq, gk, gv, fq, fk, fv], axis=0), jnp.concatenate([gr, padrows(ga), padrows(ff)], axis=0)


def _even_in_merge(dmm, dele):
    return jnp.concatenate([dmm[:1024], dele[:512], dele[512:512 + GLA_RANK], dmm[1024:2560],
                            dele[640:640 + ATT_HEADS]], axis=0)


def _forward_backward(x, target, shard, vec_shard, w, place):
    w = dict(w)
    g, dnorm, sums, recv = {}, {}, {}, {}
    nrm = lambda l, k: w["norm_w"][l, k][None, :]
    gather = lambda *keys: _gather_plan([shard[k] for k in keys])
    blocks = lambda r, c: (N_DEV, r // N_DEV, c)

    def pair_sum(key):
        sums[key] = _pair_sum(g[key], got[key], place, f"rs_pair_sum_{key[0]}_{key[1]}")

    got = {}

    def mlp_fwd(xin, layer, ride_up, ride_down):
        up = _mm(xin, w["w_mlp_up"][layer], out_dtype=ACT_DTYPE, tm=TM_FWD, tn=D_FF // N_DEV, b_blocked=True,
                 a_norm=nrm(layer, 2), name=f"mlp_up_{layer}", ride=ride_up)
        (u, h), rode_up = up if ride_up is not None else (up, None)
        down = _mm(u, w["w_mlp_down"][layer], out_dtype=F32, tm=TM_DX // 2, tn=D_MODEL, a_sqrelu=True,
                   res_norm=(xin, nrm(layer, 3)), name=f"mlp_down_{layer}", ride=ride_down)
        (yv, xout), rode_down = down if ride_down is not None else (down, None)
        return xout, (xin, h, u, yv), rode_up, rode_down

    def mlp_bwd(dxout, saved, layer, ride):
        xin, h, u, yv = saved
        k_up, k_down = ("w_mlp_up", layer), ("w_mlp_down", layer)
        res = _mm(dxout, w["w_mlp_down"][layer], nt=True, out_dtype=ACT_DTYPE, tm=TM_DX, tn=TN, drelu_of=u,
                  a_norm_bwd=(yv, nrm(layer, 3)), name=f"mlp_down_dx_{layer}", ride=ride)
        (du, dy, dnorm[(layer, 3)]), rode = res if ride is not None else (res, None)
        g[k_down] = _mm(u, dy, ta=True, out_dtype=WIRE_DTYPE, tm=TM_DW, tn=TN, a_sqrelu=True,
                        name=f"mlp_down_dw_{layer}").reshape(blocks(D_FF, D_MODEL))
        g[k_up] = _mm(h, du, ta=True, out_dtype=WIRE_DTYPE, tm=TM_DW, tn=D_FF // N_DEV, out_blocked=True,
                      name=f"mlp_up_dw_{layer}")
        w_up = jnp.moveaxis(w["w_mlp_up"][layer], 0, 1).reshape(D_MODEL, D_FF)
        (dxin, dnorm[(layer, 2)]), (got[k_down], got[k_up]) = _mm(
            du, w_up, nt=True, out_dtype=F32, tm=TM_DX // 2, tn=D_MODEL, norm_bwd=(xin, nrm(layer, 2), dxout),
            name=f"mlp_up_dx_{layer}", ride=_sibling_plan([g[k_down], g[k_up]]))
        pair_sum(k_down)
        pair_sum(k_up)
        return dxin, rode

    first = _run_plan(_gather_plan([shard[("w_in_even", 0)]] + [vec_shard[n] for n in VECTORS]),
                      "weights_all_gather_first")
    w["w_in_even"] = first[0].reshape(-1, D_MODEL)
    for n, b in zip(VECTORS, first[1:]):
        w[n] = _join_shards(b, SHARDED[n])
    w["w_mlp_up"], w["w_mlp_down"] = [None] * DEPTH, [None] * DEPTH

    wmm_e, wel_e = _even_in_split(w["w_in_even"])
    w_up_pad = jnp.pad(w["gla_w_a_up"][0], ((0, LANES - GLA_RANK), (0, 0)))
    b_f_pad = jnp.pad(w["fox_b_f"], ((0, 0), (0, LANES - ATT_HEADS)))
    (pmm0, h0), (w_out_even,) = _mm(x, wmm_e, nt=True, out_dtype=ACT_DTYPE, tm=TM_FWD, tn=TN, a_norm=nrm(0, 0),
                                    name="in_even_mm", ride=gather(("w_out_even", 0)))
    pel0 = _mm(h0, wel_e, nt=True, out_dtype=F32, tm=TM_FWD, tn=768, name="in_even_el")
    (out_a, states), (w["w_mlp_up"][0],) = _gla_fwd(pmm0, pel0, w_up_pad, w["gla_b_a"], w["gla_norm_w"],
                                                    ride=gather(("w_mlp_up", 0)))
    cum, cum_t = _fox_gate_fwd(pel0, b_f_pad)
    (out_b, lse_b), (w_mlp_down0, w_in_odd) = _fox_fwd(pmm0, cum, cum_t,
                                                       ride=gather(("w_mlp_down", 0), ("w_in_odd", 0)))
    w["w_out_even"] = w_out_even.reshape(D_MODEL, D_MODEL)
    w["w_mlp_down"][0] = w_mlp_down0.reshape(D_FF, D_MODEL)
    mix_in0 = jnp.concatenate([out_a, out_b], axis=1)
    mix0, x1 = _mm(mix_in0, w["w_out_even"], out_dtype=F32, tm=TM_DX, tn=D_MODEL, res_norm=(x, nrm(0, 1)),
                   name="out_even")
    x2, mlp0, _, (w["w_mlp_up"][1],) = mlp_fwd(x1, 0, None, gather(("w_mlp_up", 1)))
    w["w_in_odd"] = w_in_odd.reshape(-1, D_MODEL)

    w_in_o = w["w_in_odd"]
    n_mm_o = 3 * GROUP_WIDTH
    wa_bd, wx_bd = _block_diag_pairs(w["lru_w_a"][0]), _block_diag_pairs(w["lru_w_x"][0])
    base = _ca_bias_base(w["rel_bias"][0])
    pmm1, h1 = _mm(x2, w_in_o[:n_mm_o], nt=True, out_dtype=ACT_DTYPE, tm=TM_FWD, tn=TN, a_norm=nrm(1, 0),
                   name="in_odd_mm")
    pel1 = _mm(h1, w_in_o[n_mm_o:], nt=True, out_dtype=F32, tm=TM_FWD, tn=TN, name="in_odd_el")
    kp = jnp.pad(pmm1[:, GROUP_WIDTH:2 * GROUP_WIDTH], ((CA_LEFT, 0), (0, 0)))
    vp = jnp.pad(pmm1[:, 2 * GROUP_WIDTH:], ((CA_LEFT, 0), (0, 0)))
    (out_c, lse_c), (w_mlp_down1,) = _ca_fwd(pmm1, kp, vp, base, ride=gather(("w_mlp_down", 1)))
    w["w_mlp_down"][1] = w_mlp_down1.reshape(D_FF, D_MODEL)
    lru_args = (pel1, w["conv_w"][0], w["conv_b"], wa_bd, w["lru_b_a"], wx_bd, w["lru_b_x"], w["lru_lambda"])
    out_d, (w_out_odd,) = _lru_fwd(*lru_args, ride=gather(("w_out_odd", 0)))
    w["w_out_odd"] = w_out_odd.reshape(D_MODEL, D_MODEL)
    mix_in1 = jnp.concatenate([out_c, out_d], axis=1)
    mix1, x3 = _mm(mix_in1, w["w_out_odd"], out_dtype=F32, tm=TM_DX, tn=D_MODEL, res_norm=(x2, nrm(1, 1)),
                   name="out_odd")
    x4, mlp1, _, _ = mlp_fwd(x3, 1, None, None)

    loss, dx4 = _loss_fwd_bwd(x4, target)

    k_oo, k_io, k_oe, k_ie = ("w_out_odd", 0), ("w_in_odd", 0), ("w_out_even", 0), ("w_in_even", 0)
    mlp_keys = lambda l: [("w_mlp_down", l), ("w_mlp_up", l)]
    dx3, _ = mlp_bwd(dx4, mlp1, 1, None)
    dmix_in1, dmix1, dnorm[(1, 1)] = _mm(dx3, w["w_out_odd"], nt=True, out_dtype=F32, tm=TM_DX, tn=TN,
                                         a_norm_bwd=(mix1, nrm(1, 1)), name="out_odd_dx")
    g[k_oo] = _mm(mix_in1, dmix1, ta=True, out_dtype=WIRE_DTYPE, tm=TM_DW, tn=TN, name="out_odd_dw").reshape(
        blocks(D_MODEL, D_MODEL))
    (dq_c, dkp, dvp, dbase), rode = _ca_bwd(
        pmm1, kp, vp, base, lse_c, dmix_in1,
        ride=_join_plans(_chip_plan([sums[k] for k in mlp_keys(1)]), _sibling_plan([g[k_oo]])))
    recv.update(zip(mlp_keys(1), rode[:2]))
    got[k_oo] = rode[2]
    pair_sum(k_oo)
    (dgate, dxin, g_conv_w, g_conv_b, dwa_bd, g_lru_b_a, dwx_bd, g_lru_b_x, g_lru_lambda), (recv[k_oo],) = _lru_bwd(
        *lru_args, dmix_in1, ride=_chip_plan([sums[k_oo]]))
    dp1 = jnp.concatenate([dq_c, dkp[CA_LEFT:].astype(ACT_DTYPE), dvp[CA_LEFT:].astype(ACT_DTYPE), dgate, dxin], axis=1)
    g[k_io] = _mm(dp1, h1, ta=True, out_dtype=WIRE_DTYPE, tm=dp1.shape[1] // 2, tn=TN, name="in_odd_dw").reshape(
        blocks(dp1.shape[1], D_MODEL))
    (dx2, dnorm[(1, 0)]), (got[k_io],) = _mm(dp1, w_in_o, out_dtype=F32, tm=TM_DX // 2, tn=D_MODEL,
                                             norm_bwd=(x2, nrm(1, 0), dx3), name="in_odd_dx",
                                             ride=_sibling_plan([g[k_io]]))
    pair_sum(k_io)
    g["rel_bias"] = _ca_bias_base_grad(dbase)[None]
    g["conv_w"], g["conv_b"] = g_conv_w[None], g_conv_b
    g["lru_w_a"], g["lru_w_x"] = _block_diag_pairs_grad(dwa_bd)[None], _block_diag_pairs_grad(dwx_bd)[None]
    g["lru_b_a"], g["lru_b_x"], g["lru_lambda"] = g_lru_b_a, g_lru_b_x, g_lru_lambda

    dx1, (recv[k_io],) = mlp_bwd(dx2, mlp0, 0, _chip_plan([sums[k_io]]))
    dmix_in0, dmix0, dnorm[(0, 1)] = _mm(dx1, w["w_out_even"], nt=True, out_dtype=F32, tm=TM_DX, tn=TN,
                                         a_norm_bwd=(mix0, nrm(0, 1)), name="out_even_dx")
    g[k_oe] = _mm(mix_in0, dmix0, ta=True, out_dtype=WIRE_DTYPE, tm=TM_DW, tn=TN, name="out_even_dw").reshape(
        blocks(D_MODEL, D_MODEL))
    k_md0, k_mu0 = mlp_keys(0)
    (dq_a, dk_a, dv_a, dr_a, da_a, dw_up_pad, g_gla_b_a, g_gla_norm_w), (got[k_oe],) = _gla_bwd(
        pmm0, pel0, w_up_pad, w["gla_b_a"], w["gla_norm_w"], states, dmix_in0, ride=_sibling_plan([g[k_oe]]))
    pair_sum(k_oe)
    (dq_b, dk_b, dv_b, dcum_t, dcum_q), (recv[k_md0], recv[k_mu0], recv[k_oe]) = _fox_bwd(
        pmm0, cum, cum_t, lse_b, dmix_in0, ride=_chip_plan([sums[k_md0], sums[k_mu0], sums[k_oe]]))
    df_b, db_f = _fox_gate_bwd(pel0, b_f_pad, dcum_t, dcum_q)
    g["gla_w_a_up"] = dw_up_pad[:GLA_RANK][None]
    g["gla_b_a"], g["gla_norm_w"], g["fox_b_f"] = g_gla_b_a, g_gla_norm_w, db_f[:, :ATT_HEADS]
    dp0 = jnp.concatenate([dq_a, dk_a, dv_a, dq_b, dk_b.astype(ACT_DTYPE), dv_b.astype(ACT_DTYPE), dr_a, da_a, df_b],
                          axis=1)
    w_perm = jnp.concatenate([wmm_e, wel_e], axis=0)
    n_mm_e = wmm_e.shape[0]
    dw_perm = _mm(dp0, h0, ta=True, out_dtype=WIRE_DTYPE, tm=dp0.shape[1] // 2, tn=TN, name="in_even_dw")
    dw_even = _even_in_merge(dw_perm[:n_mm_e], dw_perm[n_mm_e:])
    g[k_ie] = dw_even.reshape(blocks(dw_even.shape[0], D_MODEL))
    dh0, (got[k_ie],) = _mm(dp0, w_perm, out_dtype=F32, tm=TM_DX, tn=TN, name="in_even_dx",
                            ride=_sibling_plan([g[k_ie]]))
    pair_sum(k_ie)
    dx0, dnorm[(0, 0)] = _norm_bwd(dh0, x, nrm(0, 0), out_dtype=F32, add=dx1, name="norm_in_bwd_0")

    g["norm_w"] = jnp.stack([jnp.concatenate([dnorm[(l, k)] for k in range(4)], axis=0) for l in range(DEPTH)])
    recv[k_ie], losses, *small_parts = _run_plan(
        _join_plans(_chip_plan([sums[k_ie]]),
                    _gather_plan([loss] + [g[n] for n in REPLICATED]
                                 + [_split_shards(g[n], SHARDED[n]) for n in VECTORS])), "last_exchanges")
    return losses, dx0, sums, recv, small_parts[:len(REPLICATED)], small_parts[len(REPLICATED):]


def kernel(x, norm_w, w_in_even, gla_w_a_up, gla_b_a, gla_norm_w, fox_b_f, w_out_even, w_in_odd, rel_bias, conv_w, conv_b, lru_w_a, lru_b_a, lru_w_x, lru_b_x, lru_lambda, w_out_odd, w_mlp_up, w_mlp_down, loss_target, m_norm_w, m_w_in_even, m_gla_w_a_up, m_gla_b_a, m_gla_norm_w, m_fox_b_f, m_w_out_even, m_w_in_odd, m_rel_bias, m_conv_w, m_conv_b, m_lru_w_a, m_lru_b_a, m_lru_w_x, m_lru_b_x, m_lru_lambda, m_w_out_odd, m_w_mlp_up, m_w_mlp_down, v_norm_w, v_w_in_even, v_gla_w_a_up, v_gla_b_a, v_gla_norm_w, v_fox_b_f, v_w_out_even, v_w_in_odd, v_rel_bias, v_conv_w, v_conv_b, v_lru_w_a, v_lru_b_a, v_lru_w_x, v_lru_b_x, v_lru_lambda, v_w_out_odd, v_w_mlp_up, v_w_mlp_down):
    wts = dict(zip(WEIGHTS, (norm_w, w_in_even, gla_w_a_up, gla_b_a, gla_norm_w, fox_b_f, w_out_even, w_in_odd, rel_bias,
                             conv_w, conv_b, lru_w_a, lru_b_a, lru_w_x, lru_b_x, lru_lambda, w_out_odd, w_mlp_up,
                             w_mlp_down)))
    mom = dict(zip(WEIGHTS, (m_norm_w, m_w_in_even, m_gla_w_a_up, m_gla_b_a, m_gla_norm_w, m_fox_b_f, m_w_out_even,
                             m_w_in_odd, m_rel_bias, m_conv_w, m_conv_b, m_lru_w_a, m_lru_b_a, m_lru_w_x, m_lru_b_x,
                             m_lru_lambda, m_w_out_odd, m_w_mlp_up, m_w_mlp_down)))
    var = dict(zip(WEIGHTS, (v_norm_w, v_w_in_even, v_gla_w_a_up, v_gla_b_a, v_gla_norm_w, v_fox_b_f, v_w_out_even,
                             v_w_in_odd, v_rel_bias, v_conv_w, v_conv_b, v_lru_w_a, v_lru_b_a, v_lru_w_x, v_lru_b_x,
                             v_lru_lambda, v_w_out_odd, v_w_mlp_up, v_w_mlp_down)))
    ax, ay, ac = lax.axis_index("x"), lax.axis_index("y"), lax.axis_index("c")
    place = jnp.stack([ac, 2 * ax + ay, 4 * ax + 2 * ay + ac]).astype(jnp.int32)

    shard = {(n, l): (wts[n][l].T if n in TRANSPOSED else wts[n][l]).astype(WIRE_DTYPE) for n, l in MATRIX_BLOCKS}
    losses, dx, sums, recv, repl_parts, vec_parts = _forward_backward(
        x[0], loss_target[0], shard, {n: wts[n] for n in VECTORS}, {n: wts[n] for n in REPLICATED}, place)
    loss = jnp.sum(losses[:, 0, 0])

    view = lambda n, a: jnp.swapaxes(a, 1, 2) if n in TRANSPOSED else a
    upd = {n: [view(n, o) for o in _adamw_sharded(
        [(sums[(n, l)], [recv[(n, l)]]) for l in range(wts[n].shape[0])], view(n, wts[n]), view(n, mom[n]),
        view(n, var[n]), place, f"adamw_{n}")] for n in MATRICES}
    small = REPLICATED + list(VECTORS)
    upd.update(zip(small, _adamw_small(repl_parts, vec_parts, [wts[n] for n in small], [mom[n] for n in small],
                                       [var[n] for n in small], place)))
    return (loss, dx[None], *[upd[n][kind] for kind in range(4) for n in WEIGHTS])
```

```python
import functools
from typing import Callable, NamedTuple, Optional

import jax
import jax.numpy as jnp
from jax import lax
from jax.experimental import pallas as pl
from jax.experimental.pallas import tpu as pltpu

F32 = jnp.float32
MXU_DTYPE = jnp.bfloat16
ACT_DTYPE = jnp.bfloat16
WIRE_DTYPE = jnp.bfloat16

V7X_VMEM_BYTES = 64 * 1024 * 1024
VMEM_LIMIT = (V7X_VMEM_BYTES * 7) // 8
LANES = 128

D_MODEL = 1024
DEPTH = 2
CHUNK = 64
GROUP_WIDTH = D_MODEL // 2
D_FF = 4 * D_MODEL
NORM_EPS = 1e-6
GLA_HEADS = 4
GLA_DV = GROUP_WIDTH // GLA_HEADS
GLA_DK = GLA_DV // 2
GLA_KW = GLA_HEADS * GLA_DK
GLA_RANK = 16
GLA_GATE_TAU = 16.0
HEAD_DIM = 64
ATT_HEADS = GROUP_WIDTH // HEAD_DIM
CA_LEFT = 8 * CHUNK
REL_CLIP = 128
LRU_BLOCK_DIM = 64
CONV_WIDTH = 4
LRU_C = 8.0
N_DEV = 8

ADAM_LR = 0.001
ADAM_B1 = 0.9
ADAM_B2 = 0.999
ADAM_EPS = 1e-08
ADAM_WD = 0.01
ADAM_STEP = 10

NEG = float(jnp.finfo(jnp.float32).min)
MESH = pl.DeviceIdType.MESH


def _params(*sem):
    return pltpu.CompilerParams(dimension_semantics=sem, vmem_limit_bytes=VMEM_LIMIT)


def _dot(a, b, ca=1, cb=0):
    return lax.dot_general(a.astype(MXU_DTYPE), b.astype(MXU_DTYPE), (((ca,), (cb,)), ((), ())),
                           preferred_element_type=F32)


def _dot_exact(a, b):
    return lax.dot_general(a, b, (((1,), (0,)), ((), ())), precision=lax.Precision.HIGHEST,
                           preferred_element_type=F32)


def _log_sigmoid(x):
    return jnp.minimum(x, 0.0) - jnp.log1p(jnp.exp(-jnp.abs(x)))


def _iota(shape, axis):
    return lax.broadcasted_iota(jnp.int32, shape, axis)


ANY = pl.BlockSpec(memory_space=pl.ANY)
N_CHIPS = 4


class _Plan(NamedTuple):
    ins: list
    outs: list
    sems: list
    start: Callable
    finish: Callable
    relay: Optional[Callable] = None


def _place():
    x, y, c = lax.axis_index("x"), lax.axis_index("y"), lax.axis_index("c")
    return x, y, c, [(1 - x, y), (x, 1 - y), (1 - x, 1 - y)]


def _gather_plan(xs):
    n = len(xs)

    def parts(x_refs, out_refs, sems):
        send_sems, recv_sems, local_sems = sems
        x, y, c, chips = _place()
        me, sibling = (x, y, c), (x, y, 1 - c)

        def rows(a, px, py, pc):
            return out_refs[a].at[4 * px + 2 * py + pc]

        def copy(a, k, block, to, src=None):
            return pltpu.make_async_remote_copy(
                src_ref=rows(a, *block) if src is None else src, dst_ref=rows(a, *block),
                send_sem=send_sems.at[7 * a + k], recv_sem=recv_sems.at[7 * a + k], device_id=to, device_id_type=MESH)

        def own():
            mine = [pltpu.make_async_copy(x_refs[a], rows(a, *me), local_sems.at[a]) for a in range(n)]
            first = []
            for a in range(n):
                first.append(copy(a, 0, me, sibling, src=x_refs[a]))
                first += [copy(a, 1 + j, me, (*chip, c), src=x_refs[a]) for j, chip in enumerate(chips)]
            return mine, first

        return c, me, sibling, chips, copy, own

    def start(x_refs, out_refs, sems):
        mine, first = parts(x_refs, out_refs, sems)[-1]()
        for cp in first + mine:
            cp.start()

    def relay(x_refs, out_refs, sems):
        c, me, sibling, chips, copy, _ = parts(x_refs, out_refs, sems)
        for j, chip in enumerate(chips):
            for a in range(n):
                copy(a, 1 + j, (*chip, c), me).wait_recv()
                copy(a, 4 + j, (*chip, c), sibling).start()

    def finish(x_refs, out_refs, sems):
        c, me, sibling, chips, copy, own = parts(x_refs, out_refs, sems)
        mine, first = own()
        for a in range(n):
            copy(a, 0, sibling, me).wait_recv()
            for j, chip in enumerate(chips):
                copy(a, 4 + j, (*chip, 1 - c), me).wait_recv()
        for cp in first + [copy(a, 4 + j, (*chip, c), sibling) for j, chip in enumerate(chips) for a in range(n)]:
            cp.wait_send()
        for cp in mine:
            cp.wait()

    return _Plan(list(xs), [jax.ShapeDtypeStruct((N_DEV,) + x.shape, x.dtype) for x in xs],
                 [pltpu.SemaphoreType.DMA((7 * n,)), pltpu.SemaphoreType.DMA((7 * n,)), pltpu.SemaphoreType.DMA((n,))],
                 start, finish, relay)


def _exchange_plan(copies_of, ins, outs, per_array):
    n = len(ins)

    def start(in_refs, out_refs, sems):
        for cp in copies_of(in_refs, out_refs, sems):
            cp.start()

    def finish(in_refs, out_refs, sems):
        copies = copies_of(in_refs, out_refs, sems)
        for cp in copies:
            cp.wait_recv()
        for cp in copies:
            cp.wait_send()

    return _Plan(list(ins), outs, [pltpu.SemaphoreType.DMA((per_array * n,)), pltpu.SemaphoreType.DMA((per_array * n,))],
                 start, finish)


def _sibling_plan(gs):
    def copies_of(g_refs, got_refs, sems):
        x, y, c, _ = _place()
        return [pltpu.make_async_remote_copy(
            src_ref=g_refs[a].at[2 * k + (1 - c)], dst_ref=got_refs[a].at[k], send_sem=sems[0].at[N_CHIPS * a + k],
            recv_sem=sems[1].at[N_CHIPS * a + k], device_id=(x, y, 1 - c), device_id_type=MESH)
            for a in range(len(gs)) for k in range(N_CHIPS)]

    return _exchange_plan(copies_of, gs, [jax.ShapeDtypeStruct((N_CHIPS,) + g.shape[1:], g.dtype) for g in gs], N_CHIPS)


def _chip_plan(ss, relations=(0, 1, 2)):
    n_rel = len(relations)

    def copies_of(s_refs, out_refs, sems):
        x, y, c, chips = _place()
        return [pltpu.make_async_remote_copy(
            src_ref=s_refs[a].at[2 * chips[j][0] + chips[j][1]], dst_ref=out_refs[a].at[slot],
            send_sem=sems[0].at[n_rel * a + slot], recv_sem=sems[1].at[n_rel * a + slot],
            device_id=(*chips[j], c), device_id_type=MESH)
            for a in range(len(ss)) for slot, j in enumerate(relations)]

    return _exchange_plan(copies_of, ss, [jax.ShapeDtypeStruct((n_rel,) + s.shape[1:], s.dtype) for s in ss], n_rel)


def _join_plans(*plans):
    def cut(refs, counts):
        at = 0
        for n in counts:
            yield refs[at:at + n]
            at += n

    def each(in_refs, out_refs, sems):
        return zip(plans, cut(in_refs, [len(p.ins) for p in plans]), cut(out_refs, [len(p.outs) for p in plans]),
                   cut(sems, [len(p.sems) for p in plans]))

    def start(*refs):
        for p, i, o, s in each(*refs):
            p.start(i, o, s)

    def relay(*refs):
        for p, i, o, s in each(*refs):
            if p.relay is not None:
                p.relay(i, o, s)

    def finish(*refs):
        for p, i, o, s in each(*refs):
            p.finish(i, o, s)

    return _Plan([a for p in plans for a in p.ins], [a for p in plans for a in p.outs],
                 [a for p in plans for a in p.sems], start, finish, relay)


def _run_plan(plan, name):
    n_in, n_out = len(plan.ins), len(plan.outs)

    def body(*refs):
        args = refs[:n_in], refs[n_in:n_in + n_out], refs[n_in + n_out:]
        plan.start(*args)
        if plan.relay is not None:
            plan.relay(*args)
        plan.finish(*args)

    return pl.pallas_call(body, out_shape=plan.outs, in_specs=[ANY] * n_in, out_specs=[ANY] * n_out,
                          scratch_shapes=plan.sems, name=name)(*plan.ins)


def _pcall(body, ride, *, grid, in_specs, out_specs, out_shape, scratch_shapes=(), semantics, name, prefetch=False):
    n_pre = int(prefetch)

    def build(kernel, ins, outs, shapes, scratch, sem):
        if prefetch:
            return pl.pallas_call(
                kernel, grid_spec=pltpu.PrefetchScalarGridSpec(num_scalar_prefetch=1, grid=grid, in_specs=ins,
                                                               out_specs=outs, scratch_shapes=scratch),
                out_shape=shapes, compiler_params=_params(*sem), name=name)
        return pl.pallas_call(kernel, grid=grid, in_specs=ins, out_specs=outs, out_shape=shapes,
                              scratch_shapes=scratch, compiler_params=_params(*sem), name=name)

    if ride is None:
        return build(body, in_specs, out_specs, out_shape, list(scratch_shapes), semantics)
    single = not isinstance(out_shape, (list, tuple))
    out_specs_l, out_shape_l = ([out_specs], [out_shape]) if single else (list(out_specs), list(out_shape))
    n_in, n_out, n_scr = len(in_specs), len(out_shape_l), len(scratch_shapes)
    r_in, r_out = len(ride.ins), len(ride.outs)

    def riding(*refs):
        pre, refs = refs[:n_pre], refs[n_pre:]
        cuts = [n_in, r_in, n_out, r_out, n_scr]
        groups, at = [], 0
        for width in cuts:
            groups.append(refs[at:at + width])
            at += width
        ins, r_ins, outs, r_outs, scr = groups
        sems = refs[at:]
        first = functools.reduce(jnp.logical_and, [pl.program_id(d) == 0 for d in range(len(grid))])
        last = functools.reduce(jnp.logical_and, [pl.program_id(d) == grid[d] - 1 for d in range(len(grid))])

        @pl.when(first)
        def _():
            ride.start(r_ins, r_outs, sems)

        several_steps = any(n > 1 for n in grid)
        if ride.relay is not None and several_steps:
            @pl.when(last)
            def _():
                ride.relay(r_ins, r_outs, sems)

        body(*pre, *ins, *outs, *scr)

        @pl.when(last)
        def _():
            if ride.relay is not None and not several_steps:
                ride.relay(r_ins, r_outs, sems)
            ride.finish(r_ins, r_outs, sems)

    call = build(riding, list(in_specs) + [ANY] * r_in, out_specs_l + [ANY] * r_out, out_shape_l + list(ride.outs),
                 list(scratch_shapes) + list(ride.sems), ["arbitrary"] * len(grid))

    def run(*args):
        res = call(*args, *ride.ins)
        return (res[0] if single else list(res[:n_out])), list(res[n_out:])

    return run


def _rms(x):
    return x * lax.rsqrt(jnp.mean(x * x, axis=-1, keepdims=True) + NORM_EPS)


def _mm(a, b, *, nt=False, ta=False, out_dtype, tm, tn, a_sqrelu=False, drelu_of=None, b_blocked=False,
        out_blocked=False, a_norm=None, a_norm_bwd=None, res_norm=None, norm_bwd=None, name, ride=None):
    k, m = a.shape if ta else a.shape[::-1]
    if b_blocked:
        assert not nt and b.shape[1] == k and b.shape[2] == tn
        n = b.shape[0] * tn
    else:
        n = b.shape[0] if nt else b.shape[1]
        assert (b.shape[1] if nt else b.shape[0]) == k
    tm, tn = min(tm, m), min(tn, n)
    assert m % tm == 0 and n % tn == 0
    assert (res_norm is None and norm_bwd is None) or tn == n
    assert a_norm is None or a_norm_bwd is None
    n_in = (2 + (drelu_of is not None) + (a_norm is not None) + 2 * (a_norm_bwd is not None)
            + 2 * (res_norm is not None) + 3 * (norm_bwd is not None))

    def body(*refs):
        a_ref, b_ref = refs[0], refs[1]
        extra = list(refs[2:n_in])
        outs = list(refs[n_in:])
        o_ref = outs.pop(0)
        u_ref = extra.pop(0) if drelu_of is not None else None
        if a_norm is not None:
            wn_ref, h_ref, h_scr = extra.pop(0), outs.pop(0), outs.pop()

            @pl.when(pl.program_id(1) == 0)
            def _():
                h = (_rms(a_ref[...]) * wn_ref[...]).astype(ACT_DTYPE)
                h_scr[...] = h
                h_ref[...] = h

            av = h_scr[...]
        elif a_norm_bwd is not None:
            y_ref, wy_ref = extra.pop(0), extra.pop(0)
            dy_ref, dwy_ref, dy_scr = outs.pop(0), outs.pop(0), outs.pop()
            first_rows = pl.program_id(0) == 0

            @pl.when(pl.program_id(1) == 0)
            def _():
                yv, up = y_ref[...], a_ref[...]
                rstd = lax.rsqrt(jnp.mean(yv * yv, axis=-1, keepdims=True) + NORM_EPS)
                yhat = yv * rstd
                g = up * wy_ref[...]
                dy = (rstd * (g - yhat * jnp.mean(g * yhat, axis=-1, keepdims=True))).astype(ACT_DTYPE)
                dy_scr[...] = dy
                dy_ref[...] = dy

                @pl.when(first_rows)
                def _():
                    dwy_ref[...] = jnp.zeros_like(dwy_ref)

                dwy_ref[...] += jnp.sum(up * yhat, axis=0, keepdims=True)

            av = dy_scr[...]
        else:
            av = a_ref[...]
        if a_sqrelu:
            av = jnp.square(jnp.maximum(av.astype(F32), 0.0))
        acc = _dot(av, b_ref[...], 0 if ta else 1, 1 if nt else 0)
        if u_ref is not None:
            acc = acc * (2.0 * jnp.maximum(u_ref[...].astype(F32), 0.0))
        if norm_bwd is not None:
            x_ref, wb_ref, add_ref = extra
            dw_ref = outs[0]
            xv = x_ref[...]
            rstd = lax.rsqrt(jnp.mean(xv * xv, axis=-1, keepdims=True) + NORM_EPS)
            xhat = xv * rstd
            g = acc * wb_ref[...]
            o_ref[...] = rstd * (g - xhat * jnp.mean(g * xhat, axis=-1, keepdims=True)) + add_ref[...]

            @pl.when(pl.program_id(0) == 0)
            def _():
                dw_ref[...] = jnp.zeros_like(dw_ref)

            dw_ref[...] += jnp.sum(acc * xhat, axis=0, keepdims=True)
            return
        o_ref[...] = acc.astype(out_dtype)
        if res_norm is not None:
            res_ref, wr_ref = extra
            outs[0][...] = res_ref[...] + _rms(acc) * wr_ref[...]

    if b_blocked:
        b_spec = pl.BlockSpec((None, k, tn), lambda i, j: (j, 0, 0))
    elif nt:
        b_spec = pl.BlockSpec((tn, k), lambda i, j: (j, 0))
    else:
        b_spec = pl.BlockSpec((k, tn), lambda i, j: (0, j))
    a_spec = pl.BlockSpec((k, tm), lambda i, j: (0, i)) if ta else pl.BlockSpec((tm, k), lambda i, j: (i, 0))
    in_specs = [a_spec, b_spec]
    args = [a, b]
    if drelu_of is not None:
        in_specs.append(pl.BlockSpec((tm, tn), lambda i, j: (i, j)))
        args.append(drelu_of)
    if out_blocked:
        out_specs = [pl.BlockSpec((None, tm, tn), lambda i, j: (j, i, 0))]
        out_shape = [jax.ShapeDtypeStruct((n // tn, m, tn), out_dtype)]
    else:
        out_specs = [pl.BlockSpec((tm, tn), lambda i, j: (i, j))]
        out_shape = [jax.ShapeDtypeStruct((m, n), out_dtype)]
    scratch = []
    if a_norm is not None:
        assert not ta
        in_specs.append(pl.BlockSpec((1, k), lambda i, j: (0, 0)))
        args.append(a_norm)
        out_specs.append(pl.BlockSpec((tm, k), lambda i, j: (i, 0)))
        out_shape.append(jax.ShapeDtypeStruct((m, k), ACT_DTYPE))
        scratch.append(pltpu.VMEM((tm, k), ACT_DTYPE))
    if a_norm_bwd is not None:
        assert not ta
        in_specs += [pl.BlockSpec((tm, k), lambda i, j: (i, 0)), pl.BlockSpec((1, k), lambda i, j: (0, 0))]
        args += list(a_norm_bwd)
        out_specs += [pl.BlockSpec((tm, k), lambda i, j: (i, 0)), pl.BlockSpec((1, k), lambda i, j: (0, 0))]
        out_shape += [jax.ShapeDtypeStruct((m, k), ACT_DTYPE), jax.ShapeDtypeStruct((1, k), F32)]
        scratch.append(pltpu.VMEM((tm, k), ACT_DTYPE))
    if res_norm is not None:
        in_specs += [pl.BlockSpec((tm, n), lambda i, j: (i, 0)), pl.BlockSpec((1, n), lambda i, j: (0, 0))]
        args += list(res_norm)
        out_specs.append(pl.BlockSpec((tm, n), lambda i, j: (i, 0)))
        out_shape.append(jax.ShapeDtypeStruct((m, n), F32))
    if norm_bwd is not None:
        rows = pl.BlockSpec((tm, n), lambda i, j: (i, 0))
        in_specs += [rows, pl.BlockSpec((1, n), lambda i, j: (0, 0)), rows]
        args += list(norm_bwd)
        out_specs.append(pl.BlockSpec((1, n), lambda i, j: (0, 0)))
        out_shape.append(jax.ShapeDtypeStruct((1, n), F32))
    single = len(out_shape) == 1
    return _pcall(body, ride, grid=(m // tm, n // tn), in_specs=in_specs,
                  out_specs=out_specs[0] if single else out_specs, out_shape=out_shape[0] if single else out_shape,
                  scratch_shapes=scratch, semantics=("arbitrary", "arbitrary"), name=name)(*args)


ROW_TILE = 512
TM_FWD, TM_DX, TM_DW, TN = 2048, 1024, 1024, 512


def _norm_bwd(dy, x, w, *, out_dtype, add=None, name, ride=None):
    t, d = x.shape

    def body(*refs):
        dy_ref, x_ref, w_ref = refs[0], refs[1], refs[2]
        dx_ref, dw_ref = refs[-2], refs[-1]
        xv = x_ref[...]
        rstd = lax.rsqrt(jnp.mean(xv * xv, axis=-1, keepdims=True) + NORM_EPS)
        xhat = xv * rstd
        dyv = dy_ref[...].astype(F32)
        g = dyv * w_ref[...]
        dx = rstd * (g - xhat * jnp.mean(g * xhat, axis=-1, keepdims=True))
        if add is not None:
            dx = dx + refs[3][...]
        dx_ref[...] = dx.astype(out_dtype)

        @pl.when(pl.program_id(0) == 0)
        def _():
            dw_ref[...] = jnp.zeros_like(dw_ref)

        dw_ref[...] += jnp.sum(dyv * xhat, axis=0, keepdims=True)

    row = pl.BlockSpec((ROW_TILE, d), lambda i: (i, 0))
    vec = pl.BlockSpec((1, d), lambda i: (0, 0))
    in_specs = [row, row, vec] + ([row] if add is not None else [])
    args = [dy, x, w] + ([add] if add is not None else [])
    return _pcall(body, ride, grid=(t // ROW_TILE,), in_specs=in_specs, out_specs=[row, vec],
                  out_shape=[jax.ShapeDtypeStruct((t, d), out_dtype), jax.ShapeDtypeStruct((1, d), F32)],
                  semantics=("arbitrary",), name=name)(*args)


def _loss_fwd_bwd(y, target):
    t, d = y.shape

    def body(y_ref, t_ref, l_ref, dy_ref):
        diff = y_ref[...] - t_ref[...]
        dy_ref[...] = diff * (1.0 / d)

        @pl.when(pl.program_id(0) == 0)
        def _():
            l_ref[...] = jnp.zeros_like(l_ref)

        l_ref[...] += 0.5 * jnp.sum(jnp.mean(diff * diff, axis=-1, keepdims=True), axis=0, keepdims=True)

    row = pl.BlockSpec((ROW_TILE, d), lambda i: (i, 0))
    return pl.pallas_call(body, grid=(t // ROW_TILE,), in_specs=[row, row],
                          out_specs=[pl.BlockSpec((8, LANES), lambda i: (0, 0)), row],
                          out_shape=[jax.ShapeDtypeStruct((8, LANES), F32), jax.ShapeDtypeStruct((t, d), F32)],
                          compiler_params=_params("arbitrary"), name="loss")(y, target)


GLA_STATE = (GLA_HEADS * GLA_DV, GLA_KW)


def _gla_specs(chunk_of):
    rows = lambda width, col: pl.BlockSpec((CHUNK, width), lambda i: (chunk_of(i), col))
    const = lambda r, c: pl.BlockSpec((r, c), lambda i: (0, 0))
    return [rows(GLA_KW, 0),
            rows(GLA_KW, 1),
            rows(GROUP_WIDTH, 1),
            rows(GROUP_WIDTH, 0),
            rows(LANES, 4),
            const(LANES, GLA_KW),
            const(1, GLA_KW),
            const(1, GROUP_WIDTH)]


def _gla_chunk(q_ref, k_ref, v_ref, a_ref, wup_ref, ba_ref):
    z = _dot(a_ref[...], wup_ref[...]) + ba_ref[...]
    tri = (_iota((CHUNK, CHUNK), 1) <= _iota((CHUNK, CHUNK), 0)).astype(F32)
    cum = _dot_exact(tri, _log_sigmoid(z) * (1.0 / GLA_GATE_TAU))
    tot = cum[CHUNK - 1:CHUNK, :]
    e = jnp.exp(tot - cum)
    return (z, e, jnp.exp(tot), k_ref[...].astype(F32) * e, q_ref[...].astype(F32) * (GLA_DK ** -0.5),
            v_ref[...].astype(F32))


def _gla_head_mask():
    return _iota(GLA_STATE, 0) // GLA_DV == _iota(GLA_STATE, 1) // GLA_DK


def _gla_fwd(pmm, pel, w_up, b_a, gnorm_w, ride=None):
    t = pmm.shape[0]
    nc = t // CHUNK

    def body(q_ref, k_ref, v_ref, r_ref, a_ref, wup_ref, ba_ref, gw_ref, o_ref, st_ref, m_scr):
        @pl.when(pl.program_id(0) == 0)
        def _():
            m_scr[...] = jnp.zeros_like(m_scr)

        _, _, decay, kd, qs, vv = _gla_chunk(q_ref, k_ref, v_ref, a_ref, wup_ref, ba_ref)
        m = m_scr[...] * decay + jnp.where(_gla_head_mask(), _dot(vv, kd, 0, 0), 0.0)
        m_scr[...] = m
        st_ref[...] = m
        o = _dot(qs, m, 1, 1)
        rr = r_ref[...]
        gate = rr * jax.nn.sigmoid(rr) * gw_ref[...]
        for h in range(GLA_HEADS):
            vs = slice(h * GLA_DV, (h + 1) * GLA_DV)
            oh = o[:, vs]
            y = oh * lax.rsqrt(jnp.mean(oh * oh, axis=-1, keepdims=True) + NORM_EPS)
            o_ref[:, vs] = (y * gate[:, vs]).astype(o_ref.dtype)

    return _pcall(
        body, ride, grid=(nc,), in_specs=_gla_specs(lambda i: i),
        out_specs=[pl.BlockSpec((CHUNK, GROUP_WIDTH), lambda i: (i, 0)),
                   pl.BlockSpec((None,) + GLA_STATE, lambda i: (i, 0, 0))],
        out_shape=[jax.ShapeDtypeStruct((t, GROUP_WIDTH), ACT_DTYPE), jax.ShapeDtypeStruct((nc,) + GLA_STATE, F32)],
        scratch_shapes=[pltpu.VMEM(GLA_STATE, F32)],
        semantics=("arbitrary",), name="gla_fwd")(pmm, pmm, pmm, pel, pel, w_up, b_a, gnorm_w)


def _gla_bwd(pmm, pel, w_up, b_a, gnorm_w, states, dmix, ride=None):
    t = pmm.shape[0]
    nc = t // CHUNK
    scale = GLA_DK ** -0.5

    def body(q_ref, k_ref, v_ref, r_ref, a_ref, wup_ref, ba_ref, gw_ref, st_ref, prev_ref, do_ref,
             dq_ref, dk_ref, dv_ref, dr_ref, da_ref, dwup_ref, dba_ref, dgw_ref, dm_scr):
        step = pl.program_id(0)

        @pl.when(step == 0)
        def _():
            dm_scr[...] = jnp.zeros_like(dm_scr)
            dwup_ref[...] = jnp.zeros_like(dwup_ref)
            dba_ref[...] = jnp.zeros_like(dba_ref)
            dgw_ref[...] = jnp.zeros_like(dgw_ref)

        z, e, decay, kd, qs, vv = _gla_chunk(q_ref, k_ref, v_ref, a_ref, wup_ref, ba_ref)
        m = st_ref[...]
        m_prev = prev_ref[...] * (step < nc - 1).astype(F32)
        rr, dout, gw = r_ref[...], do_ref[...], gw_ref[...]
        sig = jax.nn.sigmoid(rr)
        silu = rr * sig
        dsilu = sig * (1.0 + rr * (1.0 - sig))
        o = _dot(qs, m, 1, 1)
        d_o, dgw = [], []
        for h in range(GLA_HEADS):
            vs = slice(h * GLA_DV, (h + 1) * GLA_DV)
            oh, dg = o[:, vs], dout[:, vs]
            rstd = lax.rsqrt(jnp.mean(oh * oh, axis=-1, keepdims=True) + NORM_EPS)
            y = oh * rstd
            dgw.append(jnp.sum(dg * y * silu[:, vs], axis=0, keepdims=True))
            dr_ref[:, vs] = (dg * y * gw[:, vs] * dsilu[:, vs]).astype(dr_ref.dtype)
            dy = dg * gw[:, vs] * silu[:, vs]
            d_o.append(rstd * (dy - y * jnp.mean(dy * y, axis=-1, keepdims=True)))
        d_o = jnp.concatenate(d_o, axis=1)
        dgw_ref[...] += jnp.concatenate(dgw, axis=1)
        dq_ref[...] = (_dot(d_o, m) * scale).astype(dq_ref.dtype)
        dm = dm_scr[...] + jnp.where(_gla_head_mask(), _dot(d_o, qs, 0, 0), 0.0)
        dv_ref[...] = _dot(kd, dm, 1, 1).astype(dv_ref.dtype)
        dkd = _dot(vv, dm)
        dk_ref[...] = (dkd * e).astype(dk_ref.dtype)
        dm_scr[...] = dm * decay
        tri_strict = (_iota((CHUNK, CHUNK), 1) < _iota((CHUNK, CHUNK), 0)).astype(F32)
        dla = jnp.sum(dm * m_prev, axis=0, keepdims=True) * decay + _dot_exact(tri_strict, dkd * kd)
        dz = dla * jax.nn.sigmoid(-z) * (1.0 / GLA_GATE_TAU)
        da_ref[...] = _dot(dz, wup_ref[...], 1, 1).astype(da_ref.dtype)
        dwup_ref[...] += _dot(a_ref[...], dz, 0, 0)
        dba_ref[...] += jnp.sum(dz, axis=0, keepdims=True)

    chunk_of = lambda i: nc - 1 - i
    in_specs = _gla_specs(chunk_of) + [
        pl.BlockSpec((None,) + GLA_STATE, lambda i: (chunk_of(i), 0, 0)),
        pl.BlockSpec((None,) + GLA_STATE, lambda i: (jnp.maximum(chunk_of(i) - 1, 0), 0, 0)),
        pl.BlockSpec((CHUNK, GROUP_WIDTH), lambda i: (chunk_of(i), 0))]
    rows = lambda width: pl.BlockSpec((CHUNK, width), lambda i: (chunk_of(i), 0))
    const = lambda r, c: pl.BlockSpec((r, c), lambda i: (0, 0))
    return _pcall(
        body, ride, grid=(nc,), in_specs=in_specs,
        out_specs=[rows(GLA_KW), rows(GLA_KW), rows(GROUP_WIDTH), rows(GROUP_WIDTH), rows(LANES),
                   const(LANES, GLA_KW), const(1, GLA_KW), const(1, GROUP_WIDTH)],
        out_shape=[jax.ShapeDtypeStruct((t, GLA_KW), ACT_DTYPE), jax.ShapeDtypeStruct((t, GLA_KW), ACT_DTYPE),
                   jax.ShapeDtypeStruct((t, GROUP_WIDTH), ACT_DTYPE), jax.ShapeDtypeStruct((t, GROUP_WIDTH), ACT_DTYPE),
                   jax.ShapeDtypeStruct((t, LANES), ACT_DTYPE), jax.ShapeDtypeStruct((LANES, GLA_KW), F32),
                   jax.ShapeDtypeStruct((1, GLA_KW), F32), jax.ShapeDtypeStruct((1, GROUP_WIDTH), F32)],
        scratch_shapes=[pltpu.VMEM(GLA_STATE, F32)],
        semantics=("arbitrary",), name="gla_bwd")(
            pmm, pmm, pmm, pel, pel, w_up, b_a, gnorm_w, states, states, dmix)


CUM_BLOCK = 256


def _fox_gate_fwd(pel, b_f):
    t = pel.shape[0]
    nb = t // CUM_BLOCK

    def body(f_ref, b_ref, cum_ref, cum_t_ref):
        tri = (_iota((CUM_BLOCK, CUM_BLOCK), 1) <= _iota((CUM_BLOCK, CUM_BLOCK), 0)).astype(F32)
        carry = jnp.zeros((1, LANES), F32)
        for blk in range(nb):
            rows = slice(blk * CUM_BLOCK, (blk + 1) * CUM_BLOCK)
            cum = _dot_exact(tri, _log_sigmoid(f_ref[rows, :] + b_ref[...])) + carry
            cum_ref[rows, :] = cum
            cum_t_ref[blk] = cum.T[:ATT_HEADS, :]
            carry = cum[CUM_BLOCK - 1:CUM_BLOCK, :]

    return pl.pallas_call(
        body, grid=(1,),
        in_specs=[pl.BlockSpec((t, LANES), lambda i: (0, 5)), pl.BlockSpec((1, LANES), lambda i: (0, 0))],
        out_specs=[pl.BlockSpec((t, LANES), lambda i: (0, 0)),
                   pl.BlockSpec((nb, ATT_HEADS, CUM_BLOCK), lambda i: (0, 0, 0))],
        out_shape=[jax.ShapeDtypeStruct((t, LANES), F32), jax.ShapeDtypeStruct((nb, ATT_HEADS, CUM_BLOCK), F32)],
        compiler_params=_params("arbitrary"), name="fox_gate_fwd")(pel, b_f)


def _fox_gate_bwd(pel, b_f, dcum_t, dcum_q):
    t = pel.shape[0]
    nb = t // CUM_BLOCK

    def body(f_ref, b_ref, dct_ref, dcq_ref, df_ref, db_ref):
        tri_up = (_iota((CUM_BLOCK, CUM_BLOCK), 1) >= _iota((CUM_BLOCK, CUM_BLOCK), 0)).astype(F32)
        carry = jnp.zeros((1, LANES), F32)
        db = jnp.zeros((1, LANES), F32)
        for blk in reversed(range(nb)):
            rows = slice(blk * CUM_BLOCK, (blk + 1) * CUM_BLOCK)
            query_side = sum(dcq_ref[pair, rows, :] for pair in range(dcq_ref.shape[0]))
            dls = _dot_exact(tri_up, dct_ref[blk].T + query_side) + carry
            carry = dls[0:1, :]
            df = dls * jax.nn.sigmoid(-(f_ref[rows, :] + b_ref[...]))
            df_ref[rows, :] = df.astype(df_ref.dtype)
            db = db + jnp.sum(df, axis=0, keepdims=True)
        db_ref[...] = db

    return pl.pallas_call(
        body, grid=(1,),
        in_specs=[pl.BlockSpec((t, LANES), lambda i: (0, 5)), pl.BlockSpec((1, LANES), lambda i: (0, 0)),
                  pl.BlockSpec((nb, LANES, CUM_BLOCK), lambda i: (0, 0, 0)),
                  pl.BlockSpec((dcum_q.shape[0], t, LANES), lambda i: (0, 0, 0))],
        out_specs=[pl.BlockSpec((t, LANES), lambda i: (0, 0)), pl.BlockSpec((1, LANES), lambda i: (0, 0))],
        out_shape=[jax.ShapeDtypeStruct((t, LANES), ACT_DTYPE), jax.ShapeDtypeStruct((1, LANES), F32)],
        compiler_params=_params("arbitrary"), name="fox_gate_bwd")(pel, b_f, dcum_t, dcum_q)


FOX_Q_BLOCK = 256


assert FOX_Q_BLOCK == CUM_BLOCK
FOX_KEY_STEP = 512


def _fox_scores(q_ref, k_ref, cum_ref, cum_t_ref, h, i):
    hs = slice(h * HEAD_DIM, (h + 1) * HEAD_DIM)
    nb = cum_t_ref.shape[0]
    key_gate = jnp.concatenate([cum_t_ref[kb, h:h + 1, :] for kb in range(nb)], axis=1)
    s = _dot(q_ref[:, hs], k_ref[:, hs], 1, 1) * (HEAD_DIM ** -0.5) + (cum_ref[:, h:h + 1] - key_gate)
    shape = (FOX_Q_BLOCK, nb * FOX_Q_BLOCK)
    return jnp.where(_iota(shape, 1) <= i * FOX_Q_BLOCK + _iota(shape, 0), s, NEG)


def _fox_specs(t):
    bq, nb = FOX_Q_BLOCK, t // FOX_Q_BLOCK
    return [pl.BlockSpec((bq, GROUP_WIDTH), lambda i: (i, 2)), pl.BlockSpec((t, GROUP_WIDTH), lambda i: (0, 3)),
            pl.BlockSpec((t, GROUP_WIDTH), lambda i: (0, 4)), pl.BlockSpec((bq, LANES), lambda i: (i, 0)),
            pl.BlockSpec((nb, ATT_HEADS, bq), lambda i: (0, 0, 0))]


def _fox_fwd(pmm, cum, cum_t, ride=None):
    t = pmm.shape[0]
    bq = FOX_Q_BLOCK

    def body(q_ref, k_ref, v_ref, cum_ref, cum_t_ref, o_ref, lse_ref):
        i = pl.program_id(0)
        lse_ref[...] = jnp.zeros_like(lse_ref)
        for h in range(ATT_HEADS):
            hs = slice(h * HEAD_DIM, (h + 1) * HEAD_DIM)
            s = _fox_scores(q_ref, k_ref, cum_ref, cum_t_ref, h, i)
            m = jnp.max(s, axis=-1, keepdims=True)
            p = jnp.exp(s - m)
            l = jnp.sum(p, axis=-1, keepdims=True)
            o_ref[:, hs] = (_dot(p, v_ref[:, hs]) / l).astype(o_ref.dtype)
            lse_ref[:, h:h + 1] = m + jnp.log(l)

    return _pcall(
        body, ride, grid=(t // bq,), in_specs=_fox_specs(t),
        out_specs=[pl.BlockSpec((bq, GROUP_WIDTH), lambda i: (i, 0)), pl.BlockSpec((bq, LANES), lambda i: (i, 0))],
        out_shape=[jax.ShapeDtypeStruct((t, GROUP_WIDTH), ACT_DTYPE), jax.ShapeDtypeStruct((t, LANES), F32)],
        semantics=("parallel",), name="fox_fwd")(pmm, pmm, pmm, cum, cum_t)


def _fox_bwd(pmm, cum, cum_t, lse, dmix, ride=None):
    t = pmm.shape[0]
    bq, nb = FOX_Q_BLOCK, t // FOX_Q_BLOCK
    pairs, per_pair = ATT_HEADS // 2, LANES // HEAD_DIM
    scale = HEAD_DIM ** -0.5

    def body(q_ref, k_ref, v_ref, cum_ref, cum_t_ref, lse_ref, do_ref, dq_ref, dk_ref, dv_ref, dct_ref, dcq_ref):
        g, i = pl.program_id(0), pl.program_id(1)

        @pl.when(i == 0)
        def _():
            dk_ref[...] = jnp.zeros_like(dk_ref)
            dv_ref[...] = jnp.zeros_like(dv_ref)

        @pl.when((i == 0) & (g == 0))
        def _():
            dct_ref[...] = jnp.zeros_like(dct_ref)

        lane = _iota((1, LANES), 1)

        def run(n):
            causal = _iota((bq, n), 1) <= i * bq + _iota((bq, n), 0)
            dcq = jnp.zeros((bq, LANES), F32)
            for hh in range(per_pair):
                h = per_pair * g + hh
                hs = slice(hh * HEAD_DIM, (hh + 1) * HEAD_DIM)
                pick = (lane == h).astype(F32)
                cq = jnp.sum(cum_ref[...] * pick, axis=1, keepdims=True)
                lse_h = jnp.sum(lse_ref[...] * pick, axis=1, keepdims=True)
                key_gate = jnp.concatenate([cum_t_ref[kb, pl.ds(h, 1), :] for kb in range(n // bq)], axis=1)
                s = _dot(q_ref[:, hs], k_ref[:n, hs], 1, 1) * scale + (cq - key_gate)
                p = jnp.exp(jnp.where(causal, s, NEG) - lse_h)
                do = do_ref[:, hs]
                dp = _dot(do, v_ref[:n, hs], 1, 1)
                ds = p * (dp - jnp.sum(p * dp, axis=-1, keepdims=True))
                dq_ref[:, hs] = (_dot(ds, k_ref[:n, hs]) * scale).astype(dq_ref.dtype)
                dk_ref[:n, hs] += _dot(ds, q_ref[:, hs], 0, 0) * scale
                dv_ref[:n, hs] += _dot(p, do, 0, 0)
                key_side = -jnp.sum(ds, axis=0, keepdims=True)
                for kb in range(n // bq):
                    dct_ref[kb, pl.ds(h, 1), :] += key_side[:, kb * bq:(kb + 1) * bq]
                dcq = dcq + jnp.sum(ds, axis=1, keepdims=True) * pick
            dcq_ref[...] = dcq

        for kx in range(t // FOX_KEY_STEP):
            pl.when(i // (FOX_KEY_STEP // bq) == kx)(functools.partial(run, (kx + 1) * FOX_KEY_STEP))

    cols = lambda first: pl.BlockSpec((bq, LANES), lambda g, i: (i, first + g))
    keys = lambda first: pl.BlockSpec((t, LANES), lambda g, i: (0, first + g))
    per_head = pl.BlockSpec((bq, LANES), lambda g, i: (i, 0))
    fox_q, fox_k, fox_v = (GROUP_WIDTH * n // LANES for n in (2, 3, 4))
    return _pcall(
        body, ride, grid=(pairs, t // bq),
        in_specs=[cols(fox_q), keys(fox_k), keys(fox_v), per_head,
                  pl.BlockSpec((nb, ATT_HEADS, bq), lambda g, i: (0, 0, 0)), per_head, cols(GROUP_WIDTH // LANES)],
        out_specs=[cols(0), keys(0), keys(0), pl.BlockSpec((nb, LANES, bq), lambda g, i: (0, 0, 0)),
                   pl.BlockSpec((None, bq, LANES), lambda g, i: (g, i, 0))],
        out_shape=[jax.ShapeDtypeStruct((t, GROUP_WIDTH), ACT_DTYPE), jax.ShapeDtypeStruct((t, GROUP_WIDTH), F32),
                   jax.ShapeDtypeStruct((t, GROUP_WIDTH), F32), jax.ShapeDtypeStruct((nb, LANES, bq), F32),
                   jax.ShapeDtypeStruct((pairs, t, LANES), F32)],
        semantics=("arbitrary", "arbitrary"), name="fox_bwd")(pmm, pmm, pmm, cum, cum_t, lse, dmix)


CA_Q_BLOCK = 4 * CHUNK
CA_WINDOW = CA_Q_BLOCK + CA_LEFT
CA_BASE = 1024


def _ca_bias_base(rel_bias):
    n = rel_bias.shape[0]
    flat = CA_Q_BLOCK + CA_LEFT - REL_CLIP
    tail = CA_BASE - flat - (2 * REL_CLIP + 1)
    return jnp.concatenate([jnp.broadcast_to(rel_bias[:, 2 * REL_CLIP:], (n, flat)), rel_bias[:, ::-1],
                            jnp.broadcast_to(rel_bias[:, :1], (n, tail))], axis=1)


def _ca_bias_base_grad(dbase):
    flat = CA_Q_BLOCK + CA_LEFT - REL_CLIP
    mid = dbase[:, flat:flat + 2 * REL_CLIP + 1][:, ::-1]
    lo = jnp.sum(dbase[:, flat + 2 * REL_CLIP + 1:], axis=1, keepdims=True)
    hi = jnp.sum(dbase[:, :flat], axis=1, keepdims=True)
    pad = jnp.zeros((dbase.shape[0], 2 * REL_CLIP - 1), F32)
    return mid + jnp.concatenate([lo, pad, hi], axis=1)


def _ca_mask(i):
    r, j = _iota((CA_Q_BLOCK, CA_WINDOW), 0), _iota((CA_Q_BLOCK, CA_WINDOW), 1)
    rc, jc = r // CHUNK, j // CHUNK
    return (jc >= rc) & (jc <= rc + CA_LEFT // CHUNK) & (i * CA_Q_BLOCK + j >= CA_LEFT)


def _ca_fill_bias(i, base_ref, bias_scr):
    @pl.when(i == 0)
    def _():
        for h in range(ATT_HEADS):
            rows = jnp.broadcast_to(base_ref[h:h + 1, :], (CA_Q_BLOCK, CA_BASE))
            bias_scr[h] = pltpu.roll(rows, CA_BASE - CA_Q_BLOCK, 1, stride=1, stride_axis=0)[:, :CA_WINDOW]


def _ca_scores(q_ref, kp_ref, bias_scr, win, h, mask):
    hs = slice(h * HEAD_DIM, (h + 1) * HEAD_DIM)
    s = _dot(q_ref[:, hs], kp_ref[win, hs], 1, 1) * (HEAD_DIM ** -0.5)
    return jnp.where(mask, s + bias_scr[h], NEG)


CA_BIAS_SCRATCH = pltpu.VMEM((ATT_HEADS, CA_Q_BLOCK, CA_WINDOW), F32)


def _ca_fwd(pmm, kp, vp, base, ride=None):
    t = pmm.shape[0]

    def body(q_ref, kp_ref, vp_ref, base_ref, o_ref, lse_ref, bias_scr):
        i = pl.program_id(0)
        _ca_fill_bias(i, base_ref, bias_scr)
        win = pl.ds(pl.multiple_of(i * CA_Q_BLOCK, CA_Q_BLOCK), CA_WINDOW)
        mask = _ca_mask(i)
        lse_ref[...] = jnp.zeros_like(lse_ref)
        for h in range(ATT_HEADS):
            hs = slice(h * HEAD_DIM, (h + 1) * HEAD_DIM)
            s = _ca_scores(q_ref, kp_ref, bias_scr, win, h, mask)
            m = jnp.max(s, axis=-1, keepdims=True)
            p = jnp.exp(s - m)
            l = jnp.sum(p, axis=-1, keepdims=True)
            o_ref[:, hs] = (_dot(p, vp_ref[win, hs]) / l).astype(o_ref.dtype)
            lse_ref[:, h:h + 1] = m + jnp.log(l)

    padded = pl.BlockSpec((t + CA_LEFT, GROUP_WIDTH), lambda i: (0, 0))
    return _pcall(
        body, ride, grid=(t // CA_Q_BLOCK,),
        in_specs=[pl.BlockSpec((CA_Q_BLOCK, GROUP_WIDTH), lambda i: (i, 0)), padded, padded,
                  pl.BlockSpec((ATT_HEADS, CA_BASE), lambda i: (0, 0))],
        out_specs=[pl.BlockSpec((CA_Q_BLOCK, GROUP_WIDTH), lambda i: (i, 0)),
                   pl.BlockSpec((CA_Q_BLOCK, LANES), lambda i: (i, 0))],
        out_shape=[jax.ShapeDtypeStruct((t, GROUP_WIDTH), ACT_DTYPE), jax.ShapeDtypeStruct((t, LANES), F32)],
        scratch_shapes=[CA_BIAS_SCRATCH], semantics=("arbitrary",), name="ca_fwd")(pmm, kp, vp, base)


def _ca_bwd(pmm, kp, vp, base, lse, dmix, ride=None):
    t = pmm.shape[0]
    scale = HEAD_DIM ** -0.5

    def body(q_ref, kp_ref, vp_ref, base_ref, lse_ref, do_ref, dq_ref, dkp_ref, dvp_ref, dbase_ref, bias_scr):
        i = pl.program_id(0)
        _ca_fill_bias(i, base_ref, bias_scr)

        @pl.when(i == 0)
        def _():
            dkp_ref[...] = jnp.zeros_like(dkp_ref)
            dvp_ref[...] = jnp.zeros_like(dvp_ref)
            dbase_ref[...] = jnp.zeros_like(dbase_ref)

        win = pl.ds(pl.multiple_of(i * CA_Q_BLOCK, CA_Q_BLOCK), CA_WINDOW)
        mask = _ca_mask(i)
        flip = (_iota((CA_Q_BLOCK, CA_Q_BLOCK), 0) + _iota((CA_Q_BLOCK, CA_Q_BLOCK), 1) == CA_Q_BLOCK - 1).astype(F32)
        for h in range(ATT_HEADS):
            hs = slice(h * HEAD_DIM, (h + 1) * HEAD_DIM)
            s = _ca_scores(q_ref, kp_ref, bias_scr, win, h, mask)
            p = jnp.exp(s - lse_ref[:, h:h + 1])
            do = do_ref[:, hs]
            dp = _dot(do, vp_ref[win, hs], 1, 1)
            ds = p * (dp - jnp.sum(p * dp, axis=-1, keepdims=True))
            dq_ref[:, hs] = (_dot(ds, kp_ref[win, hs]) * scale).astype(dq_ref.dtype)
            dkp_ref[win, hs] += _dot(ds, q_ref[:, hs], 0, 0) * scale
            dvp_ref[win, hs] += _dot(p, do, 0, 0)
            rev = jnp.concatenate([_dot(flip, ds), jnp.zeros((CA_Q_BLOCK, CA_BASE - CA_WINDOW), F32)], axis=1)
            lined = pltpu.roll(rev, 1, 1, stride=1, stride_axis=0)
            dbase_ref[h:h + 1, :] += jnp.sum(lined, axis=0, keepdims=True)

    padded = pl.BlockSpec((t + CA_LEFT, GROUP_WIDTH), lambda i: (0, 0))
    return _pcall(
        body, ride, grid=(t // CA_Q_BLOCK,),
        in_specs=[pl.BlockSpec((CA_Q_BLOCK, GROUP_WIDTH), lambda i: (i, 0)), padded, padded,
                  pl.BlockSpec((ATT_HEADS, CA_BASE), lambda i: (0, 0)),
                  pl.BlockSpec((CA_Q_BLOCK, LANES), lambda i: (i, 0)),
                  pl.BlockSpec((CA_Q_BLOCK, GROUP_WIDTH), lambda i: (i, 0))],
        out_specs=[pl.BlockSpec((CA_Q_BLOCK, GROUP_WIDTH), lambda i: (i, 0)), padded, padded,
                   pl.BlockSpec((ATT_HEADS, CA_BASE), lambda i: (0, 0))],
        out_shape=[jax.ShapeDtypeStruct((t, GROUP_WIDTH), ACT_DTYPE),
                   jax.ShapeDtypeStruct((t + CA_LEFT, GROUP_WIDTH), F32),
                   jax.ShapeDtypeStruct((t + CA_LEFT, GROUP_WIDTH), F32),
                   jax.ShapeDtypeStruct((ATT_HEADS, CA_BASE), F32)],
        scratch_shapes=[CA_BIAS_SCRATCH], semantics=("arbitrary",), name="ca_bwd")(pmm, kp, vp, base, lse, dmix)


GELU_C = 0.7978845608028654
GELU_A = 0.044715


def _shift_down(v, k, fill, period=None):
    rows = _iota(v.shape, 0)
    rows = rows if period is None else rows & (period - 1)
    return jnp.where(rows >= k, pltpu.roll(v, k, 0), fill)


def _shift_up(v, k, fill, period=None):
    t = v.shape[0]
    rows = _iota(v.shape, 0)
    rows, length = (rows, t) if period is None else (rows & (period - 1), period)
    return jnp.where(rows < length - k, pltpu.roll(v, t - k, 0), fill)


LRU_SCAN_BLOCK = 256


def _linear_scan(a, b, reverse=False):
    shift = _shift_up if reverse else _shift_down
    k = 1
    while k < LRU_SCAN_BLOCK:
        b = a * shift(b, k, 0.0, LRU_SCAN_BLOCK) + b
        a = a * shift(a, k, 1.0, LRU_SCAN_BLOCK)
        k *= 2
    nb = a.shape[0] // LRU_SCAN_BLOCK
    carry = jnp.zeros((1, a.shape[1]), F32)
    out = [None] * nb
    for blk in (reversed(range(nb)) if reverse else range(nb)):
        rows = slice(blk * LRU_SCAN_BLOCK, (blk + 1) * LRU_SCAN_BLOCK)
        h = b[rows] + a[rows] * carry
        out[blk] = h
        carry = h[0:1] if reverse else h[LRU_SCAN_BLOCK - 1:LRU_SCAN_BLOCK]
    return jnp.concatenate(out, axis=0)


def _neg_expm1(y):
    series = -y * (1.0 + y * (0.5 + y * (1.0 / 6.0 + y * (1.0 / 24.0 + y * (1.0 / 120.0)))))
    return jnp.where(y > -0.1, series, 1.0 - jnp.exp(y))


def _lru_forward(x, g_in, cw, cb, wa, ba, wx, bx, lam):
    xs = [_shift_down(x, CONV_WIDTH - 1 - j, 0.0) for j in range(CONV_WIDTH - 1)] + [x]
    xc = cb + sum(cw[j:j + 1, :] * xs[j] for j in range(CONV_WIDTH))
    r = jax.nn.sigmoid(_dot(xc, wa) + ba)
    i = jax.nn.sigmoid(_dot(xc, wx) + bx)
    lsl = _log_sigmoid(lam)
    la = LRU_C * r * lsl
    a = jnp.exp(la)
    s = jnp.sqrt(_neg_expm1(2.0 * la))
    h = _linear_scan(a, s * (i * xc))
    u = GELU_C * (g_in + GELU_A * g_in * g_in * g_in)
    th = jnp.tanh(u)
    gelu = 0.5 * g_in * (1.0 + th)
    return xs, xc, r, i, lsl, a, s, h, th, gelu


def _lru_specs(t):
    col = lambda off: pl.BlockSpec((t, LANES), lambda j: (0, j + off))
    vec = pl.BlockSpec((1, LANES), lambda j: (0, j))
    mat = pl.BlockSpec((None, LANES, LANES), lambda j: (j, 0, 0))
    return [col(0), col(GROUP_WIDTH // LANES), pl.BlockSpec((CONV_WIDTH, LANES), lambda j: (0, j)),
            vec, mat, vec, mat, vec, vec]


def _lru_fwd(pel, conv_w, conv_b, wa, ba, wx, bx, lam, ride=None):
    t = pel.shape[0]

    def body(g_ref, x_ref, cw_ref, cb_ref, wa_ref, ba_ref, wx_ref, bx_ref, lam_ref, o_ref):
        res = _lru_forward(x_ref[...], g_ref[...], cw_ref[...], cb_ref[...], wa_ref[...], ba_ref[...],
                           wx_ref[...], bx_ref[...], lam_ref[...])
        o_ref[...] = (res[7] * res[9]).astype(o_ref.dtype)

    return _pcall(
        body, ride, grid=(GROUP_WIDTH // LANES,), in_specs=_lru_specs(t),
        out_specs=pl.BlockSpec((t, LANES), lambda j: (0, j)),
        out_shape=jax.ShapeDtypeStruct((t, GROUP_WIDTH), ACT_DTYPE),
        semantics=("parallel",), name="lru_fwd")(pel, pel, conv_w, conv_b, wa, ba, wx, bx, lam)


def _lru_bwd(pel, conv_w, conv_b, wa, ba, wx, bx, lam, dmix, ride=None):
    t = pel.shape[0]

    def body(g_ref, x_ref, cw_ref, cb_ref, wa_ref, ba_ref, wx_ref, bx_ref, lam_ref, do_ref,
             dg_ref, dx_ref, dcw_ref, dcb_ref, dwa_ref, dba_ref, dwx_ref, dbx_ref, dlam_ref):
        g_in, cw, lam = g_ref[...], cw_ref[...], lam_ref[...]
        xs, xc, r, i, lsl, a, s, h, th, gelu = _lru_forward(
            x_ref[...], g_in, cw, cb_ref[...], wa_ref[...], ba_ref[...], wx_ref[...], bx_ref[...], lam)
        dout = do_ref[...]
        dgelu = 0.5 * (1.0 + th) + 0.5 * g_in * (1.0 - th * th) * GELU_C * (1.0 + 3.0 * GELU_A * g_in * g_in)
        dg_ref[...] = (dout * h * dgelu).astype(dg_ref.dtype)
        gsum = _linear_scan(_shift_up(a, 1, 0.0), dout * gelu, reverse=True)
        da = gsum * _shift_down(h, 1, 0.0)
        di = gsum * s * xc
        dla = da * a - gsum * (i * xc) * (a * a / s)
        dlam_ref[...] = jnp.sum(dla * (LRU_C * r), axis=0, keepdims=True) * jax.nn.sigmoid(-lam)
        dpr = dla * (LRU_C * lsl) * r * (1.0 - r)
        dpi = di * i * (1.0 - i)
        dxc = gsum * s * i + _dot(dpr, wa_ref[...], 1, 1) + _dot(dpi, wx_ref[...], 1, 1)
        xct = xc.T
        dwa_ref[...] = _dot(xct, dpr)
        dwx_ref[...] = _dot(xct, dpi)
        dba_ref[...] = jnp.sum(dpr, axis=0, keepdims=True)
        dbx_ref[...] = jnp.sum(dpi, axis=0, keepdims=True)
        dcb_ref[...] = jnp.sum(dxc, axis=0, keepdims=True)
        for j in range(CONV_WIDTH):
            dcw_ref[j:j + 1, :] = jnp.sum(dxc * xs[j], axis=0, keepdims=True)
        dx = cw[CONV_WIDTH - 1:CONV_WIDTH, :] * dxc
        for j in range(CONV_WIDTH - 1):
            dx = dx + cw[j:j + 1, :] * _shift_up(dxc, CONV_WIDTH - 1 - j, 0.0)
        dx_ref[...] = dx.astype(dx_ref.dtype)

    col = pl.BlockSpec((t, LANES), lambda j: (0, j))
    vec = pl.BlockSpec((1, LANES), lambda j: (0, j))
    mat = pl.BlockSpec((None, LANES, LANES), lambda j: (j, 0, 0))
    nb = GROUP_WIDTH // LANES
    vshape = jax.ShapeDtypeStruct((1, GROUP_WIDTH), F32)
    mshape = jax.ShapeDtypeStruct((nb, LANES, LANES), F32)
    return _pcall(
        body, ride, grid=(nb,),
        in_specs=_lru_specs(t) + [pl.BlockSpec((t, LANES), lambda j: (0, j + nb))],
        out_specs=[col, col, pl.BlockSpec((CONV_WIDTH, LANES), lambda j: (0, j)), vec, mat, vec, mat, vec, vec],
        out_shape=[jax.ShapeDtypeStruct((t, GROUP_WIDTH), ACT_DTYPE), jax.ShapeDtypeStruct((t, GROUP_WIDTH), ACT_DTYPE),
                   jax.ShapeDtypeStruct((CONV_WIDTH, GROUP_WIDTH), F32), vshape, mshape, vshape, mshape, vshape, vshape],
        semantics=("parallel",), name="lru_bwd")(
            pel, pel, conv_w, conv_b, wa, ba, wx, bx, lam, dmix)


def _block_diag_pairs(w):
    z = jnp.zeros((LRU_BLOCK_DIM, LRU_BLOCK_DIM), w.dtype)
    return jnp.stack([jnp.block([[w[2 * j], z], [z, w[2 * j + 1]]]) for j in range(w.shape[0] // 2)])


def _block_diag_pairs_grad(dw):
    b = LRU_BLOCK_DIM
    return jnp.stack([dw[n // 2, (n % 2) * b:(n % 2 + 1) * b, (n % 2) * b:(n % 2 + 1) * b] for n in range(2 * dw.shape[0])])


def _row_tile(r):
    return ROW_TILE if r % ROW_TILE == 0 else r


def _pair_sum(g, got, place, name):
    _, r, c = g.shape
    tile = r

    def body(place_ref, a_ref, b_ref, o_ref):
        o_ref[...] = (a_ref[...].astype(F32) + b_ref[...].astype(F32)).astype(o_ref.dtype)

    blk = pl.BlockSpec((1, tile, c), lambda k, i, place_ref: (k, i, 0))
    return pl.pallas_call(
        body,
        grid_spec=pltpu.PrefetchScalarGridSpec(
            num_scalar_prefetch=1, grid=(N_CHIPS, r // tile),
            in_specs=[pl.BlockSpec((1, tile, c), lambda k, i, place_ref: (2 * k + place_ref[0], i, 0)), blk],
            out_specs=blk),
        out_shape=jax.ShapeDtypeStruct(got.shape, got.dtype),
        compiler_params=_params("parallel", "parallel"), name=name)(place, g, got)


def _adamw_update(g, w_ref, m_ref, v_ref, g_ref, d_ref, nm_ref, nv_ref):
    nm = ADAM_B1 * m_ref[...] + (1.0 - ADAM_B1) * g
    nv = ADAM_B2 * v_ref[...] + (1.0 - ADAM_B2) * jnp.square(g)
    m_hat = nm / (1.0 - ADAM_B1 ** ADAM_STEP)
    v_hat = nv / (1.0 - ADAM_B2 ** ADAM_STEP)
    g_ref[...] = g
    d_ref[...] = -ADAM_LR * (m_hat / (jnp.sqrt(v_hat) + ADAM_EPS) + ADAM_WD * w_ref[...])
    nm_ref[...] = nm
    nv_ref[...] = nv


def _adamw_sharded(parts, w, m, v, place, name, ride=None):
    n_layers, r, c = w.shape
    tile = _row_tile(r)
    nb = r // tile
    counts = [1 + len(recvs) for _, recvs in parts]

    def body(place_ref, *refs):
        layer = pl.program_id(0)
        g, at = None, 0
        for l in range(n_layers):
            g_l = refs[at][0].astype(F32)
            for r_ref in refs[at + 1:at + counts[l]]:
                for k in range(r_ref.shape[0]):
                    g_l = g_l + r_ref[k].astype(F32)
            g = g_l if g is None else jnp.where(layer == l, g_l, g)
            at += counts[l]
        _adamw_update(g, *refs[at:])

    def part_specs(l, recvs):
        rows = lambda q, i: jnp.where(q < l, 0, jnp.where(q > l, nb - 1, i))
        return ([pl.BlockSpec((1, tile, c), lambda q, i, place_ref: (place_ref[1], rows(q, i), 0))] +
                [pl.BlockSpec((a.shape[0], tile, c), lambda q, i, place_ref: (0, rows(q, i), 0)) for a in recvs])

    in_specs, args = [], []
    for l, (s, recvs) in enumerate(parts):
        in_specs += part_specs(l, recvs)
        args += [s, *recvs]
    blk = pl.BlockSpec((None, tile, c), lambda q, i, place_ref: (q, i, 0))
    out = jax.ShapeDtypeStruct((n_layers, r, c), F32)
    return _pcall(body, ride, grid=(n_layers, nb), in_specs=in_specs + [blk, blk, blk], out_specs=[blk, blk, blk, blk],
                  out_shape=[out, out, out, out], semantics=("arbitrary", "arbitrary"), name=name, prefetch=True)(
                      place, *args, w, m, v)


def _adamw_small(repl_parts, vec_parts, w, m, v, place):
    n_r, n = len(repl_parts), len(w)
    shapes = [a.shape for a in w]

    def body(place_ref, *refs):
        parts, rest = refs[:n], refs[n:]
        for k in range(n):
            take = (lambda p: parts[k][p]) if k < n_r else (lambda p: parts[k][p, 0])
            g = take(0)
            for p in range(1, N_DEV):
                g = g + take(p)
            _adamw_update(g, rest[k], rest[n + k], rest[2 * n + k], *rest[3 * n + 4 * k:3 * n + 4 * k + 4])

    def whole(shape):
        return pl.BlockSpec(shape, lambda i, place_ref: (0,) * len(shape))

    def mine(shard):
        return pl.BlockSpec((N_DEV, 1) + shard, lambda i, place_ref: (0, place_ref[2]) + (0,) * len(shard))

    in_specs = [whole(a.shape) for a in repl_parts] + [mine(s) for s in shapes[n_r:]] + [whole(s) for s in shapes] * 3
    outs = pl.pallas_call(
        body,
        grid_spec=pltpu.PrefetchScalarGridSpec(
            num_scalar_prefetch=1, grid=(1,), in_specs=in_specs,
            out_specs=[whole(s) for s in shapes for _ in range(4)]),
        out_shape=[jax.ShapeDtypeStruct(s, F32) for s in shapes for _ in range(4)],
        compiler_params=_params("arbitrary"), name="adamw_small")(place, *repl_parts, *vec_parts, *w, *m, *v)
    return [outs[4 * k:4 * k + 4] for k in range(n)]


SHARDED = {"norm_w": 2, "w_in_even": 2, "gla_w_a_up": 2, "w_out_even": 1, "w_in_odd": 2, "conv_w": 2, "conv_b": 1,
           "lru_b_a": 1, "lru_b_x": 1, "lru_lambda": 1, "w_out_odd": 1, "w_mlp_up": 2, "w_mlp_down": 1}
REPLICATED = ["gla_b_a", "gla_norm_w", "fox_b_f", "rel_bias", "lru_w_a", "lru_w_x"]
WEIGHTS = ["norm_w", "w_in_even", "gla_w_a_up", "gla_b_a", "gla_norm_w", "fox_b_f", "w_out_even", "w_in_odd",
           "rel_bias", "conv_w", "conv_b", "lru_w_a", "lru_b_a", "lru_w_x", "lru_b_x", "lru_lambda", "w_out_odd",
           "w_mlp_up", "w_mlp_down"]
MATRICES = ("w_in_even", "w_out_even", "w_in_odd", "w_out_odd", "w_mlp_up", "w_mlp_down")
TRANSPOSED = ("w_in_even", "w_in_odd")
VECTORS = tuple(n for n in SHARDED if n not in MATRICES)
MATRIX_BLOCKS = (("w_in_even", 0), ("w_out_even", 0), ("w_in_odd", 0), ("w_out_odd", 0),
                 ("w_mlp_up", 0), ("w_mlp_up", 1), ("w_mlp_down", 0), ("w_mlp_down", 1))


def _join_shards(blocks, axis):
    moved = jnp.moveaxis(blocks, 0, axis)
    shape = moved.shape
    return moved.reshape(shape[:axis] + (shape[axis] * shape[axis + 1],) + shape[axis + 2:])


def _split_shards(full, axis):
    shape = full.shape
    cut = full.reshape(shape[:axis] + (N_DEV, shape[axis] // N_DEV) + shape[axis + 1:])
    return jnp.moveaxis(cut, axis, 0)


EVEN_SPLITS = (0, 256, 512, 1024, 1536, 1552, 2064, 2576, 3088, 3096)


def _even_in_split(wt):
    c = [wt[EVEN_SPLITS[k]:EVEN_SPLITS[k + 1]] for k in range(9)]
    gq, gk, gv, gr, ga, fq, fk, fv, ff = c
    padrows = lambda a: jnp.pad(a, ((0, LANES - a.shape[0]), (0, 0)))
    return jnp.concatenate([gq, gk, gv, fq, fk, fv], axis=0), jnp.concatenate([gr, padrows(ga), padrows(ff)], axis=0)


def _even_in_merge(dmm, dele):
    return jnp.concatenate([dmm[:1024], dele[:512], dele[512:512 + GLA_RANK], dmm[1024:2560],
                            dele[640:640 + ATT_HEADS]], axis=0)


def _forward_backward(x, target, shard, vec_shard, w, place):
    w = dict(w)
    g, dnorm, sums, recv = {}, {}, {}, {}
    nrm = lambda l, k: w["norm_w"][l, k][None, :]
    gather = lambda *keys: _gather_plan([shard[k] for k in keys])
    blocks = lambda r, c: (N_DEV, r // N_DEV, c)

    def pair_sum(key):
        sums[key] = _pair_sum(g[key], got[key], place, f"rs_pair_sum_{key[0]}_{key[1]}")

    got = {}

    def mlp_fwd(xin, layer, ride_up, ride_down):
        up = _mm(xin, w["w_mlp_up"][layer], out_dtype=ACT_DTYPE, tm=TM_FWD, tn=D_FF // N_DEV, b_blocked=True,
                 a_norm=nrm(layer, 2), name=f"mlp_up_{layer}", ride=ride_up)
        (u, h), rode_up = up if ride_up is not None else (up, None)
        down = _mm(u, w["w_mlp_down"][layer], out_dtype=F32, tm=TM_DX // 2, tn=D_MODEL, a_sqrelu=True,
                   res_norm=(xin, nrm(layer, 3)), name=f"mlp_down_{layer}", ride=ride_down)
        (yv, xout), rode_down = down if ride_down is not None else (down, None)
        return xout, (xin, h, u, yv), rode_up, rode_down

    def mlp_bwd(dxout, saved, layer, ride):
        xin, h, u, yv = saved
        k_up, k_down = ("w_mlp_up", layer), ("w_mlp_down", layer)
        res = _mm(dxout, w["w_mlp_down"][layer], nt=True, out_dtype=ACT_DTYPE, tm=TM_DX, tn=TN, drelu_of=u,
                  a_norm_bwd=(yv, nrm(layer, 3)), name=f"mlp_down_dx_{layer}", ride=ride)
        (du, dy, dnorm[(layer, 3)]), rode = res if ride is not None else (res, None)
        g[k_down] = _mm(u, dy, ta=True, out_dtype=WIRE_DTYPE, tm=TM_DW, tn=TN, a_sqrelu=True,
                        name=f"mlp_down_dw_{layer}").reshape(blocks(D_FF, D_MODEL))
        g[k_up] = _mm(h, du, ta=True, out_dtype=WIRE_DTYPE, tm=TM_DW, tn=D_FF // N_DEV, out_blocked=True,
                      name=f"mlp_up_dw_{layer}")
        w_up = jnp.moveaxis(w["w_mlp_up"][layer], 0, 1).reshape(D_MODEL, D_FF)
        (dxin, dnorm[(layer, 2)]), (got[k_down], got[k_up]) = _mm(
            du, w_up, nt=True, out_dtype=F32, tm=TM_DX // 2, tn=D_MODEL, norm_bwd=(xin, nrm(layer, 2), dxout),
            name=f"mlp_up_dx_{layer}", ride=_sibling_plan([g[k_down], g[k_up]]))
        pair_sum(k_down)
        pair_sum(k_up)
        return dxin, rode

    first = _run_plan(_gather_plan([shard[("w_in_even", 0)]] + [vec_shard[n] for n in VECTORS]),
                      "weights_all_gather_first")
    w["w_in_even"] = first[0].reshape(-1, D_MODEL)
    for n, b in zip(VECTORS, first[1:]):
        w[n] = _join_shards(b, SHARDED[n])
    w["w_mlp_up"], w["w_mlp_down"] = [None] * DEPTH, [None] * DEPTH

    wmm_e, wel_e = _even_in_split(w["w_in_even"])
    w_up_pad = jnp.pad(w["gla_w_a_up"][0], ((0, LANES - GLA_RANK), (0, 0)))
    b_f_pad = jnp.pad(w["fox_b_f"], ((0, 0), (0, LANES - ATT_HEADS)))
    (pmm0, h0), (w_out_even,) = _mm(x, wmm_e, nt=True, out_dtype=ACT_DTYPE, tm=TM_FWD, tn=TN, a_norm=nrm(0, 0),
                                    name="in_even_mm", ride=gather(("w_out_even", 0)))
    pel0 = _mm(h0, wel_e, nt=True, out_dtype=F32, tm=TM_FWD, tn=768, name="in_even_el")
    (out_a, states), (w["w_mlp_up"][0],) = _gla_fwd(pmm0, pel0, w_up_pad, w["gla_b_a"], w["gla_norm_w"],
                                                    ride=gather(("w_mlp_up", 0)))
    cum, cum_t = _fox_gate_fwd(pel0, b_f_pad)
    (out_b, lse_b), (w_mlp_down0, w_in_odd) = _fox_fwd(pmm0, cum, cum_t,
                                                       ride=gather(("w_mlp_down", 0), ("w_in_odd", 0)))
    w["w_out_even"] = w_out_even.reshape(D_MODEL, D_MODEL)
    w["w_mlp_down"][0] = w_mlp_down0.reshape(D_FF, D_MODEL)
    mix_in0 = jnp.concatenate([out_a, out_b], axis=1)
    mix0, x1 = _mm(mix_in0, w["w_out_even"], out_dtype=F32, tm=TM_DX, tn=D_MODEL, res_norm=(x, nrm(0, 1)),
                   name="out_even")
    x2, mlp0, _, (w["w_mlp_up"][1],) = mlp_fwd(x1, 0, None, gather(("w_mlp_up", 1)))
    w["w_in_odd"] = w_in_odd.reshape(-1, D_MODEL)

    w_in_o = w["w_in_odd"]
    n_mm_o = 3 * GROUP_WIDTH
    wa_bd, wx_bd = _block_diag_pairs(w["lru_w_a"][0]), _block_diag_pairs(w["lru_w_x"][0])
    base = _ca_bias_base(w["rel_bias"][0])
    pmm1, h1 = _mm(x2, w_in_o[:n_mm_o], nt=True, out_dtype=ACT_DTYPE, tm=TM_FWD, tn=TN, a_norm=nrm(1, 0),
                   name="in_odd_mm")
    pel1 = _mm(h1, w_in_o[n_mm_o:], nt=True, out_dtype=F32, tm=TM_FWD, tn=TN, name="in_odd_el")
    kp = jnp.pad(pmm1[:, GROUP_WIDTH:2 * GROUP_WIDTH], ((CA_LEFT, 0), (0, 0)))
    vp = jnp.pad(pmm1[:, 2 * GROUP_WIDTH:], ((CA_LEFT, 0), (0, 0)))
    (out_c, lse_c), (w_mlp_down1,) = _ca_fwd(pmm1, kp, vp, base, ride=gather(("w_mlp_down", 1)))
    w["w_mlp_down"][1] = w_mlp_down1.reshape(D_FF, D_MODEL)
    lru_args = (pel1, w["conv_w"][0], w["conv_b"], wa_bd, w["lru_b_a"], wx_bd, w["lru_b_x"], w["lru_lambda"])
    out_d, (w_out_odd,) = _lru_fwd(*lru_args, ride=gather(("w_out_odd", 0)))
    w["w_out_odd"] = w_out_odd.reshape(D_MODEL, D_MODEL)
    mix_in1 = jnp.concatenate([out_c, out_d], axis=1)
    mix1, x3 = _mm(mix_in1, w["w_out_odd"], out_dtype=F32, tm=TM_DX, tn=D_MODEL, res_norm=(x2, nrm(1, 1)),
                   name="out_odd")
    x4, mlp1, _, _ = mlp_fwd(x3, 1, None, None)

    loss, dx4 = _loss_fwd_bwd(x4, target)

    k_oo, k_io, k_oe, k_ie = ("w_out_odd", 0), ("w_in_odd", 0), ("w_out_even", 0), ("w_in_even", 0)
    mlp_keys = lambda l: [("w_mlp_down", l), ("w_mlp_up", l)]
    dx3, _ = mlp_bwd(dx4, mlp1, 1, None)
    dmix_in1, dmix1, dnorm[(1, 1)] = _mm(dx3, w["w_out_odd"], nt=True, out_dtype=F32, tm=TM_DX, tn=TN,
                                         a_norm_bwd=(mix1, nrm(1, 1)), name="out_odd_dx")
    g[k_oo] = _mm(mix_in1, dmix1, ta=True, out_dtype=WIRE_DTYPE, tm=TM_DW, tn=TN, name="out_odd_dw").reshape(
        blocks(D_MODEL, D_MODEL))
    (dq_c, dkp, dvp, dbase), rode = _ca_bwd(
        pmm1, kp, vp, base, lse_c, dmix_in1,
        ride=_join_plans(_chip_plan([sums[k] for k in mlp_keys(1)]), _sibling_plan([g[k_oo]])))
    recv.update(zip(mlp_keys(1), rode[:2]))
    got[k_oo] = rode[2]
    pair_sum(k_oo)
    (dgate, dxin, g_conv_w, g_conv_b, dwa_bd, g_lru_b_a, dwx_bd, g_lru_b_x, g_lru_lambda), (recv[k_oo],) = _lru_bwd(
        *lru_args, dmix_in1, ride=_chip_plan([sums[k_oo]]))
    dp1 = jnp.concatenate([dq_c, dkp[CA_LEFT:].astype(ACT_DTYPE), dvp[CA_LEFT:].astype(ACT_DTYPE), dgate, dxin], axis=1)
    g[k_io] = _mm(dp1, h1, ta=True, out_dtype=WIRE_DTYPE, tm=dp1.shape[1] // 2, tn=TN, name="in_odd_dw").reshape(
        blocks(dp1.shape[1], D_MODEL))
    (dx2, dnorm[(1, 0)]), (got[k_io],) = _mm(dp1, w_in_o, out_dtype=F32, tm=TM_DX // 2, tn=D_MODEL,
                                             norm_bwd=(x2, nrm(1, 0), dx3), name="in_odd_dx",
                                             ride=_sibling_plan([g[k_io]]))
    pair_sum(k_io)
    g["rel_bias"] = _ca_bias_base_grad(dbase)[None]
    g["conv_w"], g["conv_b"] = g_conv_w[None], g_conv_b
    g["lru_w_a"], g["lru_w_x"] = _block_diag_pairs_grad(dwa_bd)[None], _block_diag_pairs_grad(dwx_bd)[None]
    g["lru_b_a"], g["lru_b_x"], g["lru_lambda"] = g_lru_b_a, g_lru_b_x, g_lru_lambda

    dx1, (recv[k_io],) = mlp_bwd(dx2, mlp0, 0, _chip_plan([sums[k_io]]))
    dmix_in0, dmix0, dnorm[(0, 1)] = _mm(dx1, w["w_out_even"], nt=True, out_dtype=F32, tm=TM_DX, tn=TN,
                                         a_norm_bwd=(mix0, nrm(0, 1)), name="out_even_dx")
    g[k_oe] = _mm(mix_in0, dmix0, ta=True, out_dtype=WIRE_DTYPE, tm=TM_DW, tn=TN, name="out_even_dw").reshape(
        blocks(D_MODEL, D_MODEL))
    k_md0, k_mu0 = mlp_keys(0)
    (dq_a, dk_a, dv_a, dr_a, da_a, dw_up_pad, g_gla_b_a, g_gla_norm_w), (got[k_oe],) = _gla_bwd(
        pmm0, pel0, w_up_pad, w["gla_b_a"], w["gla_norm_w"], states, dmix_in0, ride=_sibling_plan([g[k_oe]]))
    pair_sum(k_oe)
    (dq_b, dk_b, dv_b, dcum_t, dcum_q), (recv[k_md0], recv[k_mu0], recv[k_oe]) = _fox_bwd(
        pmm0, cum, cum_t, lse_b, dmix_in0, ride=_chip_plan([sums[k_md0], sums[k_mu0], sums[k_oe]]))
    df_b, db_f = _fox_gate_bwd(pel0, b_f_pad, dcum_t, dcum_q)
    g["gla_w_a_up"] = dw_up_pad[:GLA_RANK][None]
    g["gla_b_a"], g["gla_norm_w"], g["fox_b_f"] = g_gla_b_a, g_gla_norm_w, db_f[:, :ATT_HEADS]
    dp0 = jnp.concatenate([dq_a, dk_a, dv_a, dq_b, dk_b.astype(ACT_DTYPE), dv_b.astype(ACT_DTYPE), dr_a, da_a, df_b],
                          axis=1)
    w_perm = jnp.concatenate([wmm_e, wel_e], axis=0)
    n_mm_e = wmm_e.shape[0]
    dw_perm = _mm(dp0, h0, ta=True, out_dtype=WIRE_DTYPE, tm=dp0.shape[1] // 2, tn=TN, name="in_even_dw")
    dw_even = _even_in_merge(dw_perm[:n_mm_e], dw_perm[n_mm_e:])
    g[k_ie] = dw_even.reshape(blocks(dw_even.shape[0], D_MODEL))
    dh0, (got[k_ie], *repl_parts) = _mm(
        dp0, w_perm, out_dtype=F32, tm=TM_DX, tn=TN, name="in_even_dx",
        ride=_join_plans(_sibling_plan([g[k_ie]]), _gather_plan([g[n] for n in REPLICATED])))
    pair_sum(k_ie)
    dx0, dnorm[(0, 0)] = _norm_bwd(dh0, x, nrm(0, 0), out_dtype=F32, add=dx1, name="norm_in_bwd_0")

    g["norm_w"] = jnp.stack([jnp.concatenate([dnorm[(l, k)] for k in range(4)], axis=0) for l in range(DEPTH)])
    recv[k_ie], losses, *vec_parts = _run_plan(
        _join_plans(_chip_plan([sums[k_ie]]),
                    _gather_plan([loss] + [_split_shards(g[n], SHARDED[n]) for n in VECTORS])), "last_exchanges")
    return losses, dx0, sums, recv, repl_parts, vec_parts


def kernel(x, norm_w, w_in_even, gla_w_a_up, gla_b_a, gla_norm_w, fox_b_f, w_out_even, w_in_odd, rel_bias, conv_w, conv_b, lru_w_a, lru_b_a, lru_w_x, lru_b_x, lru_lambda, w_out_odd, w_mlp_up, w_mlp_down, loss_target, m_norm_w, m_w_in_even, m_gla_w_a_up, m_gla_b_a, m_gla_norm_w, m_fox_b_f, m_w_out_even, m_w_in_odd, m_rel_bias, m_conv_w, m_conv_b, m_lru_w_a, m_lru_b_a, m_lru_w_x, m_lru_b_x, m_lru_lambda, m_w_out_odd, m_w_mlp_up, m_w_mlp_down, v_norm_w, v_w_in_even, v_gla_w_a_up, v_gla_b_a, v_gla_norm_w, v_fox_b_f, v_w_out_even, v_w_in_odd, v_rel_bias, v_conv_w, v_conv_b, v_lru_w_a, v_lru_b_a, v_lru_w_x, v_lru_b_x, v_lru_lambda, v_w_out_odd, v_w_mlp_up, v_w_mlp_down):
    wts = dict(zip(WEIGHTS, (norm_w, w_in_even, gla_w_a_up, gla_b_a, gla_norm_w, fox_b_f, w_out_even, w_in_odd, rel_bias,
                             conv_w, conv_b, lru_w_a, lru_b_a, lru_w_x, lru_b_x, lru_lambda, w_out_odd, w_mlp_up,
                             w_mlp_down)))
    mom = dict(zip(WEIGHTS, (m_norm_w, m_w_in_even, m_gla_w_a_up, m_gla_b_a, m_gla_norm_w, m_fox_b_f, m_w_out_even,
                             m_w_in_odd, m_rel_bias, m_conv_w, m_conv_b, m_lru_w_a, m_lru_b_a, m_lru_w_x, m_lru_b_x,
                             m_lru_lambda, m_w_out_odd, m_w_mlp_up, m_w_mlp_down)))
    var = dict(zip(WEIGHTS, (v_norm_w, v_w_in_even, v_gla_w_a_up, v_gla_b_a, v_gla_norm_w, v_fox_b_f, v_w_out_even,
                             v_w_in_odd, v_rel_bias, v_conv_w, v_conv_b, v_lru_w_a, v_lru_b_a, v_lru_w_x, v_lru_b_x,
                             v_lru_lambda, v_w_out_odd, v_w_mlp_up, v_w_mlp_down)))
    ax, ay, ac = lax.axis_index("x"), lax.axis_index("y"), lax.axis_index("c")
    place = jnp.stack([ac, 2 * ax + ay, 4 * ax + 2 * ay + ac]).astype(jnp.int32)

    shard = {(n, l): (wts[n][l].T if n in TRANSPOSED else wts[n][l]).astype(WIRE_DTYPE) for n, l in MATRIX_BLOCKS}
    losses, dx, sums, recv, repl_parts, vec_parts = _forward_backward(
        x[0], loss_target[0], shard, {n: wts[n] for n in VECTORS}, {n: wts[n] for n in REPLICATED}, place)
    loss = jnp.sum(losses[:, 0, 0])

    view = lambda n, a: jnp.swapaxes(a, 1, 2) if n in TRANSPOSED else a
    upd = {n: [view(n, o) for o in _adamw_sharded(
        [(sums[(n, l)], [recv[(n, l)]]) for l in range(wts[n].shape[0])], view(n, wts[n]), view(n, mom[n]),
        view(n, var[n]), place, f"adamw_{n}")] for n in MATRICES}
    small = REPLICATED + list(VECTORS)
    upd.update(zip(small, _adamw_small(repl_parts, vec_parts, [wts[n] for n in small], [mom[n] for n in small],
                                       [var[n] for n in small], place)))
    return (loss, dx[None], *[upd[n][kind] for kind in range(4) for n in WEIGHTS])
```

```python
import functools
from typing import Callable, NamedTuple, Optional

import jax
import jax.numpy as jnp
from jax import lax
from jax.experimental import pallas as pl
from jax.experimental.pallas import tpu as pltpu

F32 = jnp.float32
MXU_DTYPE = jnp.bfloat16
ACT_DTYPE = jnp.bfloat16
WIRE_DTYPE = jnp.bfloat16

V7X_VMEM_BYTES = 64 * 1024 * 1024
VMEM_LIMIT = (V7X_VMEM_BYTES * 7) // 8
LANES = 128

D_MODEL = 1024
DEPTH = 2
CHUNK = 64
GROUP_WIDTH = D_MODEL // 2
D_FF = 4 * D_MODEL
NORM_EPS = 1e-6
GLA_HEADS = 4
GLA_DV = GROUP_WIDTH // GLA_HEADS
GLA_DK = GLA_DV // 2
GLA_KW = GLA_HEADS * GLA_DK
GLA_RANK = 16
GLA_GATE_TAU = 16.0
HEAD_DIM = 64
ATT_HEADS = GROUP_WIDTH // HEAD_DIM
CA_LEFT = 8 * CHUNK
REL_CLIP = 128
LRU_BLOCK_DIM = 64
CONV_WIDTH = 4
LRU_C = 8.0
N_DEV = 8

ADAM_LR = 0.001
ADAM_B1 = 0.9
ADAM_B2 = 0.999
ADAM_EPS = 1e-08
ADAM_WD = 0.01
ADAM_STEP = 10

NEG = float(jnp.finfo(jnp.float32).min)
MESH = pl.DeviceIdType.MESH


def _params(*sem):
    return pltpu.CompilerParams(dimension_semantics=sem, vmem_limit_bytes=VMEM_LIMIT)


def _dot(a, b, ca=1, cb=0):
    return lax.dot_general(a.astype(MXU_DTYPE), b.astype(MXU_DTYPE), (((ca,), (cb,)), ((), ())),
                           preferred_element_type=F32)


def _dot_exact(a, b):
    return lax.dot_general(a, b, (((1,), (0,)), ((), ())), precision=lax.Precision.HIGHEST,
                           preferred_element_type=F32)


def _log_sigmoid(x):
    return jnp.minimum(x, 0.0) - jnp.log1p(jnp.exp(-jnp.abs(x)))


def _iota(shape, axis):
    return lax.broadcasted_iota(jnp.int32, shape, axis)


ANY = pl.BlockSpec(memory_space=pl.ANY)
N_CHIPS = 4


class _Plan(NamedTuple):
    ins: list
    outs: list
    sems: list
    start: Callable
    finish: Callable
    relay: Optional[Callable] = None


def _place():
    x, y, c = lax.axis_index("x"), lax.axis_index("y"), lax.axis_index("c")
    return x, y, c, [(1 - x, y), (x, 1 - y), (1 - x, 1 - y)]


def _gather_plan(xs):
    n = len(xs)

    def parts(x_refs, out_refs, sems):
        send_sems, recv_sems, local_sems = sems
        x, y, c, chips = _place()
        me, sibling = (x, y, c), (x, y, 1 - c)

        def rows(a, px, py, pc):
            return out_refs[a].at[4 * px + 2 * py + pc]

        def copy(a, k, block, to, src=None):
            return pltpu.make_async_remote_copy(
                src_ref=rows(a, *block) if src is None else src, dst_ref=rows(a, *block),
                send_sem=send_sems.at[7 * a + k], recv_sem=recv_sems.at[7 * a + k], device_id=to, device_id_type=MESH)

        def own():
            mine = [pltpu.make_async_copy(x_refs[a], rows(a, *me), local_sems.at[a]) for a in range(n)]
            first = []
            for a in range(n):
                first.append(copy(a, 0, me, sibling, src=x_refs[a]))
                first += [copy(a, 1 + j, me, (*chip, c), src=x_refs[a]) for j, chip in enumerate(chips)]
            return mine, first

        return c, me, sibling, chips, copy, own

    def start(x_refs, out_refs, sems):
        mine, first = parts(x_refs, out_refs, sems)[-1]()
        for cp in first + mine:
            cp.start()

    def relay(x_refs, out_refs, sems):
        c, me, sibling, chips, copy, _ = parts(x_refs, out_refs, sems)
        for j, chip in enumerate(chips):
            for a in range(n):
                copy(a, 1 + j, (*chip, c), me).wait_recv()
                copy(a, 4 + j, (*chip, c), sibling).start()

    def finish(x_refs, out_refs, sems):
        c, me, sibling, chips, copy, own = parts(x_refs, out_refs, sems)
        mine, first = own()
        for a in range(n):
            copy(a, 0, sibling, me).wait_recv()
            for j, chip in enumerate(chips):
                copy(a, 4 + j, (*chip, 1 - c), me).wait_recv()
        for cp in first + [copy(a, 4 + j, (*chip, c), sibling) for j, chip in enumerate(chips) for a in range(n)]:
            cp.wait_send()
        for cp in mine:
            cp.wait()

    return _Plan(list(xs), [jax.ShapeDtypeStruct((N_DEV,) + x.shape, x.dtype) for x in xs],
                 [pltpu.SemaphoreType.DMA((7 * n,)), pltpu.SemaphoreType.DMA((7 * n,)), pltpu.SemaphoreType.DMA((n,))],
                 start, finish, relay)


def _exchange_plan(copies_of, ins, outs, per_array):
    n = len(ins)

    def start(in_refs, out_refs, sems):
        for cp in copies_of(in_refs, out_refs, sems):
            cp.start()

    def finish(in_refs, out_refs, sems):
        copies = copies_of(in_refs, out_refs, sems)
        for cp in copies:
            cp.wait_recv()
        for cp in copies:
            cp.wait_send()

    return _Plan(list(ins), outs, [pltpu.SemaphoreType.DMA((per_array * n,)), pltpu.SemaphoreType.DMA((per_array * n,))],
                 start, finish)


def _sibling_plan(gs):
    def copies_of(g_refs, got_refs, sems):
        x, y, c, _ = _place()
        return [pltpu.make_async_remote_copy(
            src_ref=g_refs[a].at[2 * k + (1 - c)], dst_ref=got_refs[a].at[k], send_sem=sems[0].at[N_CHIPS * a + k],
            recv_sem=sems[1].at[N_CHIPS * a + k], device_id=(x, y, 1 - c), device_id_type=MESH)
            for a in range(len(gs)) for k in range(N_CHIPS)]

    return _exchange_plan(copies_of, gs, [jax.ShapeDtypeStruct((N_CHIPS,) + g.shape[1:], g.dtype) for g in gs], N_CHIPS)


def _chip_plan(ss, relations=(0, 1, 2)):
    n_rel = len(relations)

    def copies_of(s_refs, out_refs, sems):
        x, y, c, chips = _place()
        return [pltpu.make_async_remote_copy(
            src_ref=s_refs[a].at[2 * chips[j][0] + chips[j][1]], dst_ref=out_refs[a].at[slot],
            send_sem=sems[0].at[n_rel * a + slot], recv_sem=sems[1].at[n_rel * a + slot],
            device_id=(*chips[j], c), device_id_type=MESH)
            for a in range(len(ss)) for slot, j in enumerate(relations)]

    return _exchange_plan(copies_of, ss, [jax.ShapeDtypeStruct((n_rel,) + s.shape[1:], s.dtype) for s in ss], n_rel)


def _join_plans(*plans):
    def cut(refs, counts):
        at = 0
        for n in counts:
            yield refs[at:at + n]
            at += n

    def each(in_refs, out_refs, sems):
        return zip(plans, cut(in_refs, [len(p.ins) for p in plans]), cut(out_refs, [len(p.outs) for p in plans]),
                   cut(sems, [len(p.sems) for p in plans]))

    def start(*refs):
        for p, i, o, s in each(*refs):
            p.start(i, o, s)

    def relay(*refs):
        for p, i, o, s in each(*refs):
            if p.relay is not None:
                p.relay(i, o, s)

    def finish(*refs):
        for p, i, o, s in each(*refs):
            p.finish(i, o, s)

    return _Plan([a for p in plans for a in p.ins], [a for p in plans for a in p.outs],
                 [a for p in plans for a in p.sems], start, finish, relay)


def _run_plan(plan, name):
    n_in, n_out = len(plan.ins), len(plan.outs)

    def body(*refs):
        args = refs[:n_in], refs[n_in:n_in + n_out], refs[n_in + n_out:]
        plan.start(*args)
        if plan.relay is not None:
            plan.relay(*args)
        plan.finish(*args)

    return pl.pallas_call(body, out_shape=plan.outs, in_specs=[ANY] * n_in, out_specs=[ANY] * n_out,
                          scratch_shapes=plan.sems, name=name)(*plan.ins)


def _pcall(body, ride, *, grid, in_specs, out_specs, out_shape, scratch_shapes=(), semantics, name, prefetch=False):
    n_pre = int(prefetch)

    def build(kernel, ins, outs, shapes, scratch, sem):
        if prefetch:
            return pl.pallas_call(
                kernel, grid_spec=pltpu.PrefetchScalarGridSpec(num_scalar_prefetch=1, grid=grid, in_specs=ins,
                                                               out_specs=outs, scratch_shapes=scratch),
                out_shape=shapes, compiler_params=_params(*sem), name=name)
        return pl.pallas_call(kernel, grid=grid, in_specs=ins, out_specs=outs, out_shape=shapes,
                              scratch_shapes=scratch, compiler_params=_params(*sem), name=name)

    if ride is None:
        return build(body, in_specs, out_specs, out_shape, list(scratch_shapes), semantics)
    single = not isinstance(out_shape, (list, tuple))
    out_specs_l, out_shape_l = ([out_specs], [out_shape]) if single else (list(out_specs), list(out_shape))
    n_in, n_out, n_scr = len(in_specs), len(out_shape_l), len(scratch_shapes)
    r_in, r_out = len(ride.ins), len(ride.outs)

    def riding(*refs):
        pre, refs = refs[:n_pre], refs[n_pre:]
        cuts = [n_in, r_in, n_out, r_out, n_scr]
        groups, at = [], 0
        for width in cuts:
            groups.append(refs[at:at + width])
            at += width
        ins, r_ins, outs, r_outs, scr = groups
        sems = refs[at:]
        first = functools.reduce(jnp.logical_and, [pl.program_id(d) == 0 for d in range(len(grid))])
        last = functools.reduce(jnp.logical_and, [pl.program_id(d) == grid[d] - 1 for d in range(len(grid))])

        @pl.when(first)
        def _():
            ride.start(r_ins, r_outs, sems)

        several_steps = any(n > 1 for n in grid)
        if ride.relay is not None and several_steps:
            @pl.when(last)
            def _():
                ride.relay(r_ins, r_outs, sems)

        body(*pre, *ins, *outs, *scr)

        @pl.when(last)
        def _():
            if ride.relay is not None and not several_steps:
                ride.relay(r_ins, r_outs, sems)
            ride.finish(r_ins, r_outs, sems)

    call = build(riding, list(in_specs) + [ANY] * r_in, out_specs_l + [ANY] * r_out, out_shape_l + list(ride.outs),
                 list(scratch_shapes) + list(ride.sems), ["arbitrary"] * len(grid))

    def run(*args):
        res = call(*args, *ride.ins)
        return (res[0] if single else list(res[:n_out])), list(res[n_out:])

    return run


def _rms(x):
    return x * lax.rsqrt(jnp.mean(x * x, axis=-1, keepdims=True) + NORM_EPS)


def _mm(a, b, *, nt=False, ta=False, out_dtype, tm, tn, a_sqrelu=False, drelu_of=None, b_blocked=False,
        out_blocked=False, a_norm=None, a_norm_bwd=None, res_norm=None, norm_bwd=None, name, ride=None):
    k, m = a.shape if ta else a.shape[::-1]
    if b_blocked:
        assert not nt and b.shape[1] == k and b.shape[2] == tn
        n = b.shape[0] * tn
    else:
        n = b.shape[0] if nt else b.shape[1]
        assert (b.shape[1] if nt else b.shape[0]) == k
    tm, tn = min(tm, m), min(tn, n)
    assert m % tm == 0 and n % tn == 0
    assert (res_norm is None and norm_bwd is None) or tn == n
    assert a_norm is None or a_norm_bwd is None
    n_in = (2 + (drelu_of is not None) + (a_norm is not None) + 2 * (a_norm_bwd is not None)
            + 2 * (res_norm is not None) + 3 * (norm_bwd is not None))

    def body(*refs):
        a_ref, b_ref = refs[0], refs[1]
        extra = list(refs[2:n_in])
        outs = list(refs[n_in:])
        o_ref = outs.pop(0)
        u_ref = extra.pop(0) if drelu_of is not None else None
        if a_norm is not None:
            wn_ref, h_ref, h_scr = extra.pop(0), outs.pop(0), outs.pop()

            @pl.when(pl.program_id(1) == 0)
            def _():
                h = (_rms(a_ref[...]) * wn_ref[...]).astype(ACT_DTYPE)
                h_scr[...] = h
                h_ref[...] = h

            av = h_scr[...]
        elif a_norm_bwd is not None:
            y_ref, wy_ref = extra.pop(0), extra.pop(0)
            dy_ref, dwy_ref, dy_scr = outs.pop(0), outs.pop(0), outs.pop()
            first_rows = pl.program_id(0) == 0

            @pl.when(pl.program_id(1) == 0)
            def _():
                yv, up = y_ref[...], a_ref[...]
                rstd = lax.rsqrt(jnp.mean(yv * yv, axis=-1, keepdims=True) + NORM_EPS)
                yhat = yv * rstd
                g = up * wy_ref[...]
                dy = (rstd * (g - yhat * jnp.mean(g * yhat, axis=-1, keepdims=True))).astype(ACT_DTYPE)
                dy_scr[...] = dy
                dy_ref[...] = dy

                @pl.when(first_rows)
                def _():
                    dwy_ref[...] = jnp.zeros_like(dwy_ref)

                dwy_ref[...] += jnp.sum(up * yhat, axis=0, keepdims=True)

            av = dy_scr[...]
        else:
            av = a_ref[...]
        if a_sqrelu:
            av = jnp.square(jnp.maximum(av.astype(F32), 0.0))
        acc = _dot(av, b_ref[...], 0 if ta else 1, 1 if nt else 0)
        if u_ref is not None:
            acc = acc * (2.0 * jnp.maximum(u_ref[...].astype(F32), 0.0))
        if norm_bwd is not None:
            x_ref, wb_ref, add_ref = extra
            dw_ref = outs[0]
            xv = x_ref[...]
            rstd = lax.rsqrt(jnp.mean(xv * xv, axis=-1, keepdims=True) + NORM_EPS)
            xhat = xv * rstd
            g = acc * wb_ref[...]
            o_ref[...] = rstd * (g - xhat * jnp.mean(g * xhat, axis=-1, keepdims=True)) + add_ref[...]

            @pl.when(pl.program_id(0) == 0)
            def _():
                dw_ref[...] = jnp.zeros_like(dw_ref)

            dw_ref[...] += jnp.sum(acc * xhat, axis=0, keepdims=True)
            return
        o_ref[...] = acc.astype(out_dtype)
        if res_norm is not None:
            res_ref, wr_ref = extra
            outs[0][...] = res_ref[...] + _rms(acc) * wr_ref[...]

    if b_blocked:
        b_spec = pl.BlockSpec((None, k, tn), lambda i, j: (j, 0, 0))
    elif nt:
        b_spec = pl.BlockSpec((tn, k), lambda i, j: (j, 0))
    else:
        b_spec = pl.BlockSpec((k, tn), lambda i, j: (0, j))
    a_spec = pl.BlockSpec((k, tm), lambda i, j: (0, i)) if ta else pl.BlockSpec((tm, k), lambda i, j: (i, 0))
    in_specs = [a_spec, b_spec]
    args = [a, b]
    if drelu_of is not None:
        in_specs.append(pl.BlockSpec((tm, tn), lambda i, j: (i, j)))
        args.append(drelu_of)
    if out_blocked:
        out_specs = [pl.BlockSpec((None, tm, tn), lambda i, j: (j, i, 0))]
        out_shape = [jax.ShapeDtypeStruct((n // tn, m, tn), out_dtype)]
    else:
        out_specs = [pl.BlockSpec((tm, tn), lambda i, j: (i, j))]
        out_shape = [jax.ShapeDtypeStruct((m, n), out_dtype)]
    scratch = []
    if a_norm is not None:
        assert not ta
        in_specs.append(pl.BlockSpec((1, k), lambda i, j: (0, 0)))
        args.append(a_norm)
        out_specs.append(pl.BlockSpec((tm, k), lambda i, j: (i, 0)))
        out_shape.append(jax.ShapeDtypeStruct((m, k), ACT_DTYPE))
        scratch.append(pltpu.VMEM((tm, k), ACT_DTYPE))
    if a_norm_bwd is not None:
        assert not ta
        in_specs += [pl.BlockSpec((tm, k), lambda i, j: (i, 0)), pl.BlockSpec((1, k), lambda i, j: (0, 0))]
        args += list(a_norm_bwd)
        out_specs += [pl.BlockSpec((tm, k), lambda i, j: (i, 0)), pl.BlockSpec((1, k), lambda i, j: (0, 0))]
        out_shape += [jax.ShapeDtypeStruct((m, k), ACT_DTYPE), jax.ShapeDtypeStruct((1, k), F32)]
        scratch.append(pltpu.VMEM((tm, k), ACT_DTYPE))
    if res_norm is not None:
        in_specs += [pl.BlockSpec((tm, n), lambda i, j: (i, 0)), pl.BlockSpec((1, n), lambda i, j: (0, 0))]
        args += list(res_norm)
        out_specs.append(pl.BlockSpec((tm, n), lambda i, j: (i, 0)))
        out_shape.append(jax.ShapeDtypeStruct((m, n), F32))
    if norm_bwd is not None:
        rows = pl.BlockSpec((tm, n), lambda i, j: (i, 0))
        in_specs += [rows, pl.BlockSpec((1, n), lambda i, j: (0, 0)), rows]
        args += list(norm_bwd)
        out_specs.append(pl.BlockSpec((1, n), lambda i, j: (0, 0)))
        out_shape.append(jax.ShapeDtypeStruct((1, n), F32))
    single = len(out_shape) == 1
    return _pcall(body, ride, grid=(m // tm, n // tn), in_specs=in_specs,
                  out_specs=out_specs[0] if single else out_specs, out_shape=out_shape[0] if single else out_shape,
                  scratch_shapes=scratch, semantics=("arbitrary", "arbitrary"), name=name)(*args)


ROW_TILE = 512
TM_FWD, TM_DX, TM_DW, TN = 2048, 1024, 1024, 512


def _norm_bwd(dy, x, w, *, out_dtype, add=None, name, ride=None):
    t, d = x.shape

    def body(*refs):
        dy_ref, x_ref, w_ref = refs[0], refs[1], refs[2]
        dx_ref, dw_ref = refs[-2], refs[-1]
        xv = x_ref[...]
        rstd = lax.rsqrt(jnp.mean(xv * xv, axis=-1, keepdims=True) + NORM_EPS)
        xhat = xv * rstd
        dyv = dy_ref[...].astype(F32)
        g = dyv * w_ref[...]
        dx = rstd * (g - xhat * jnp.mean(g * xhat, axis=-1, keepdims=True))
        if add is not None:
            dx = dx + refs[3][...]
        dx_ref[...] = dx.astype(out_dtype)

        @pl.when(pl.program_id(0) == 0)
        def _():
            dw_ref[...] = jnp.zeros_like(dw_ref)

        dw_ref[...] += jnp.sum(dyv * xhat, axis=0, keepdims=True)

    row = pl.BlockSpec((ROW_TILE, d), lambda i: (i, 0))
    vec = pl.BlockSpec((1, d), lambda i: (0, 0))
    in_specs = [row, row, vec] + ([row] if add is not None else [])
    args = [dy, x, w] + ([add] if add is not None else [])
    return _pcall(body, ride, grid=(t // ROW_TILE,), in_specs=in_specs, out_specs=[row, vec],
                  out_shape=[jax.ShapeDtypeStruct((t, d), out_dtype), jax.ShapeDtypeStruct((1, d), F32)],
                  semantics=("arbitrary",), name=name)(*args)


def _loss_fwd_bwd(y, target):
    t, d = y.shape

    def body(y_ref, t_ref, l_ref, dy_ref):
        diff = y_ref[...] - t_ref[...]
        dy_ref[...] = diff * (1.0 / d)

        @pl.when(pl.program_id(0) == 0)
        def _():
            l_ref[...] = jnp.zeros_like(l_ref)

        l_ref[...] += 0.5 * jnp.sum(jnp.mean(diff * diff, axis=-1, keepdims=True), axis=0, keepdims=True)

    row = pl.BlockSpec((ROW_TILE, d), lambda i: (i, 0))
    return pl.pallas_call(body, grid=(t // ROW_TILE,), in_specs=[row, row],
                          out_specs=[pl.BlockSpec((8, LANES), lambda i: (0, 0)), row],
                          out_shape=[jax.ShapeDtypeStruct((8, LANES), F32), jax.ShapeDtypeStruct((t, d), F32)],
                          compiler_params=_params("arbitrary"), name="loss")(y, target)


GLA_STATE = (GLA_HEADS * GLA_DV, GLA_KW)


def _gla_specs(chunk_of):
    rows = lambda width, col: pl.BlockSpec((CHUNK, width), lambda i: (chunk_of(i), col))
    const = lambda r, c: pl.BlockSpec((r, c), lambda i: (0, 0))
    return [rows(GLA_KW, 0),
            rows(GLA_KW, 1),
            rows(GROUP_WIDTH, 1),
            rows(GROUP_WIDTH, 0),
            rows(LANES, 4),
            const(LANES, GLA_KW),
            const(1, GLA_KW),
            const(1, GROUP_WIDTH)]


def _gla_chunk(q_ref, k_ref, v_ref, a_ref, wup_ref, ba_ref):
    z = _dot(a_ref[...], wup_ref[...]) + ba_ref[...]
    tri = (_iota((CHUNK, CHUNK), 1) <= _iota((CHUNK, CHUNK), 0)).astype(F32)
    cum = _dot_exact(tri, _log_sigmoid(z) * (1.0 / GLA_GATE_TAU))
    tot = cum[CHUNK - 1:CHUNK, :]
    e = jnp.exp(tot - cum)
    return (z, e, jnp.exp(tot), k_ref[...].astype(F32) * e, q_ref[...].astype(F32) * (GLA_DK ** -0.5),
            v_ref[...].astype(F32))


def _gla_head_mask():
    return _iota(GLA_STATE, 0) // GLA_DV == _iota(GLA_STATE, 1) // GLA_DK


def _gla_fwd(pmm, pel, w_up, b_a, gnorm_w, ride=None):
    t = pmm.shape[0]
    nc = t // CHUNK

    def body(q_ref, k_ref, v_ref, r_ref, a_ref, wup_ref, ba_ref, gw_ref, o_ref, st_ref, m_scr):
        @pl.when(pl.program_id(0) == 0)
        def _():
            m_scr[...] = jnp.zeros_like(m_scr)

        _, _, decay, kd, qs, vv = _gla_chunk(q_ref, k_ref, v_ref, a_ref, wup_ref, ba_ref)
        m = m_scr[...] * decay + jnp.where(_gla_head_mask(), _dot(vv, kd, 0, 0), 0.0)
        m_scr[...] = m
        st_ref[...] = m
        o = _dot(qs, m, 1, 1)
        rr = r_ref[...]
        gate = rr * jax.nn.sigmoid(rr) * gw_ref[...]
        for h in range(GLA_HEADS):
            vs = slice(h * GLA_DV, (h + 1) * GLA_DV)
            oh = o[:, vs]
            y = oh * lax.rsqrt(jnp.mean(oh * oh, axis=-1, keepdims=True) + NORM_EPS)
            o_ref[:, vs] = (y * gate[:, vs]).astype(o_ref.dtype)

    return _pcall(
        body, ride, grid=(nc,), in_specs=_gla_specs(lambda i: i),
        out_specs=[pl.BlockSpec((CHUNK, GROUP_WIDTH), lambda i: (i, 0)),
                   pl.BlockSpec((None,) + GLA_STATE, lambda i: (i, 0, 0))],
        out_shape=[jax.ShapeDtypeStruct((t, GROUP_WIDTH), ACT_DTYPE), jax.ShapeDtypeStruct((nc,) + GLA_STATE, F32)],
        scratch_shapes=[pltpu.VMEM(GLA_STATE, F32)],
        semantics=("arbitrary",), name="gla_fwd")(pmm, pmm, pmm, pel, pel, w_up, b_a, gnorm_w)


def _gla_bwd(pmm, pel, w_up, b_a, gnorm_w, states, dmix, ride=None):
    t = pmm.shape[0]
    nc = t // CHUNK
    scale = GLA_DK ** -0.5

    def body(q_ref, k_ref, v_ref, r_ref, a_ref, wup_ref, ba_ref, gw_ref, st_ref, prev_ref, do_ref,
             dq_ref, dk_ref, dv_ref, dr_ref, da_ref, dwup_ref, dba_ref, dgw_ref, dm_scr):
        step = pl.program_id(0)

        @pl.when(step == 0)
        def _():
            dm_scr[...] = jnp.zeros_like(dm_scr)
            dwup_ref[...] = jnp.zeros_like(dwup_ref)
            dba_ref[...] = jnp.zeros_like(dba_ref)
            dgw_ref[...] = jnp.zeros_like(dgw_ref)

        z, e, decay, kd, qs, vv = _gla_chunk(q_ref, k_ref, v_ref, a_ref, wup_ref, ba_ref)
        m = st_ref[...]
        m_prev = prev_ref[...] * (step < nc - 1).astype(F32)
        rr, dout, gw = r_ref[...], do_ref[...], gw_ref[...]
        sig = jax.nn.sigmoid(rr)
        silu = rr * sig
        dsilu = sig * (1.0 + rr * (1.0 - sig))
        o = _dot(qs, m, 1, 1)
        d_o, dgw = [], []
        for h in range(GLA_HEADS):
            vs = slice(h * GLA_DV, (h + 1) * GLA_DV)
            oh, dg = o[:, vs], dout[:, vs]
            rstd = lax.rsqrt(jnp.mean(oh * oh, axis=-1, keepdims=True) + NORM_EPS)
            y = oh * rstd
            dgw.append(jnp.sum(dg * y * silu[:, vs], axis=0, keepdims=True))
            dr_ref[:, vs] = (dg * y * gw[:, vs] * dsilu[:, vs]).astype(dr_ref.dtype)
            dy = dg * gw[:, vs] * silu[:, vs]
            d_o.append(rstd * (dy - y * jnp.mean(dy * y, axis=-1, keepdims=True)))
        d_o = jnp.concatenate(d_o, axis=1)
        dgw_ref[...] += jnp.concatenate(dgw, axis=1)
        dq_ref[...] = (_dot(d_o, m) * scale).astype(dq_ref.dtype)
        dm = dm_scr[...] + jnp.where(_gla_head_mask(), _dot(d_o, qs, 0, 0), 0.0)
        dv_ref[...] = _dot(kd, dm, 1, 1).astype(dv_ref.dtype)
        dkd = _dot(vv, dm)
        dk_ref[...] = (dkd * e).astype(dk_ref.dtype)
        dm_scr[...] = dm * decay
        tri_strict = (_iota((CHUNK, CHUNK), 1) < _iota((CHUNK, CHUNK), 0)).astype(F32)
        dla = jnp.sum(dm * m_prev, axis=0, keepdims=True) * decay + _dot_exact(tri_strict, dkd * kd)
        dz = dla * jax.nn.sigmoid(-z) * (1.0 / GLA_GATE_TAU)
        da_ref[...] = _dot(dz, wup_ref[...], 1, 1).astype(da_ref.dtype)
        dwup_ref[...] += _dot(a_ref[...], dz, 0, 0)
        dba_ref[...] += jnp.sum(dz, axis=0, keepdims=True)

    chunk_of = lambda i: nc - 1 - i
    in_specs = _gla_specs(chunk_of) + [
        pl.BlockSpec((None,) + GLA_STATE, lambda i: (chunk_of(i), 0, 0)),
        pl.BlockSpec((None,) + GLA_STATE, lambda i: (jnp.maximum(chunk_of(i) - 1, 0), 0, 0)),
        pl.BlockSpec((CHUNK, GROUP_WIDTH), lambda i: (chunk_of(i), 0))]
    rows = lambda width: pl.BlockSpec((CHUNK, width), lambda i: (chunk_of(i), 0))
    const = lambda r, c: pl.BlockSpec((r, c), lambda i: (0, 0))
    return _pcall(
        body, ride, grid=(nc,), in_specs=in_specs,
        out_specs=[rows(GLA_KW), rows(GLA_KW), rows(GROUP_WIDTH), rows(GROUP_WIDTH), rows(LANES),
                   const(LANES, GLA_KW), const(1, GLA_KW), const(1, GROUP_WIDTH)],
        out_shape=[jax.ShapeDtypeStruct((t, GLA_KW), ACT_DTYPE), jax.ShapeDtypeStruct((t, GLA_KW), ACT_DTYPE),
                   jax.ShapeDtypeStruct((t, GROUP_WIDTH), ACT_DTYPE), jax.ShapeDtypeStruct((t, GROUP_WIDTH), ACT_DTYPE),
                   jax.ShapeDtypeStruct((t, LANES), ACT_DTYPE), jax.ShapeDtypeStruct((LANES, GLA_KW), F32),
                   jax.ShapeDtypeStruct((1, GLA_KW), F32), jax.ShapeDtypeStruct((1, GROUP_WIDTH), F32)],
        scratch_shapes=[pltpu.VMEM(GLA_STATE, F32)],
        semantics=("arbitrary",), name="gla_bwd")(
            pmm, pmm, pmm, pel, pel, w_up, b_a, gnorm_w, states, states, dmix)


CUM_BLOCK = 256


def _fox_gate_fwd(pel, b_f):
    t = pel.shape[0]
    nb = t // CUM_BLOCK

    def body(f_ref, b_ref, cum_ref, cum_t_ref):
        tri = (_iota((CUM_BLOCK, CUM_BLOCK), 1) <= _iota((CUM_BLOCK, CUM_BLOCK), 0)).astype(F32)
        carry = jnp.zeros((1, LANES), F32)
        for blk in range(nb):
            rows = slice(blk * CUM_BLOCK, (blk + 1) * CUM_BLOCK)
            cum = _dot_exact(tri, _log_sigmoid(f_ref[rows, :] + b_ref[...])) + carry
            cum_ref[rows, :] = cum
            cum_t_ref[blk] = cum.T[:ATT_HEADS, :]
            carry = cum[CUM_BLOCK - 1:CUM_BLOCK, :]

    return pl.pallas_call(
        body, grid=(1,),
        in_specs=[pl.BlockSpec((t, LANES), lambda i: (0, 5)), pl.BlockSpec((1, LANES), lambda i: (0, 0))],
        out_specs=[pl.BlockSpec((t, LANES), lambda i: (0, 0)),
                   pl.BlockSpec((nb, ATT_HEADS, CUM_BLOCK), lambda i: (0, 0, 0))],
        out_shape=[jax.ShapeDtypeStruct((t, LANES), F32), jax.ShapeDtypeStruct((nb, ATT_HEADS, CUM_BLOCK), F32)],
        compiler_params=_params("arbitrary"), name="fox_gate_fwd")(pel, b_f)


def _fox_gate_bwd(pel, b_f, dcum_t, dcum_q):
    t = pel.shape[0]
    nb = t // CUM_BLOCK

    def body(f_ref, b_ref, dct_ref, dcq_ref, df_ref, db_ref):
        tri_up = (_iota((CUM_BLOCK, CUM_BLOCK), 1) >= _iota((CUM_BLOCK, CUM_BLOCK), 0)).astype(F32)
        carry = jnp.zeros((1, LANES), F32)
        db = jnp.zeros((1, LANES), F32)
        for blk in reversed(range(nb)):
            rows = slice(blk * CUM_BLOCK, (blk + 1) * CUM_BLOCK)
            query_side = sum(dcq_ref[pair, rows, :] for pair in range(dcq_ref.shape[0]))
            dls = _dot_exact(tri_up, dct_ref[blk].T + query_side) + carry
            carry = dls[0:1, :]
            df = dls * jax.nn.sigmoid(-(f_ref[rows, :] + b_ref[...]))
            df_ref[rows, :] = df.astype(df_ref.dtype)
            db = db + jnp.sum(df, axis=0, keepdims=True)
        db_ref[...] = db

    return pl.pallas_call(
        body, grid=(1,),
        in_specs=[pl.BlockSpec((t, LANES), lambda i: (0, 5)), pl.BlockSpec((1, LANES), lambda i: (0, 0)),
                  pl.BlockSpec((nb, LANES, CUM_BLOCK), lambda i: (0, 0, 0)),
                  pl.BlockSpec((dcum_q.shape[0], t, LANES), lambda i: (0, 0, 0))],
        out_specs=[pl.BlockSpec((t, LANES), lambda i: (0, 0)), pl.BlockSpec((1, LANES), lambda i: (0, 0))],
        out_shape=[jax.ShapeDtypeStruct((t, LANES), ACT_DTYPE), jax.ShapeDtypeStruct((1, LANES), F32)],
        compiler_params=_params("arbitrary"), name="fox_gate_bwd")(pel, b_f, dcum_t, dcum_q)


FOX_Q_BLOCK = 256


assert FOX_Q_BLOCK == CUM_BLOCK
FOX_KEY_STEP = 512


def _fox_scores(q_ref, k_ref, cum_ref, cum_t_ref, h, i):
    hs = slice(h * HEAD_DIM, (h + 1) * HEAD_DIM)
    nb = cum_t_ref.shape[0]
    key_gate = jnp.concatenate([cum_t_ref[kb, h:h + 1, :] for kb in range(nb)], axis=1)
    s = _dot(q_ref[:, hs], k_ref[:, hs], 1, 1) * (HEAD_DIM ** -0.5) + (cum_ref[:, h:h + 1] - key_gate)
    shape = (FOX_Q_BLOCK, nb * FOX_Q_BLOCK)
    return jnp.where(_iota(shape, 1) <= i * FOX_Q_BLOCK + _iota(shape, 0), s, NEG)


def _fox_specs(t):
    bq, nb = FOX_Q_BLOCK, t // FOX_Q_BLOCK
    return [pl.BlockSpec((bq, GROUP_WIDTH), lambda i: (i, 2)), pl.BlockSpec((t, GROUP_WIDTH), lambda i: (0, 3)),
            pl.BlockSpec((t, GROUP_WIDTH), lambda i: (0, 4)), pl.BlockSpec((bq, LANES), lambda i: (i, 0)),
            pl.BlockSpec((nb, ATT_HEADS, bq), lambda i: (0, 0, 0))]


def _fox_fwd(pmm, cum, cum_t, ride=None):
    t = pmm.shape[0]
    bq = FOX_Q_BLOCK

    def body(q_ref, k_ref, v_ref, cum_ref, cum_t_ref, o_ref, lse_ref):
        i = pl.program_id(0)
        lse_ref[...] = jnp.zeros_like(lse_ref)
        for h in range(ATT_HEADS):
            hs = slice(h * HEAD_DIM, (h + 1) * HEAD_DIM)
            s = _fox_scores(q_ref, k_ref, cum_ref, cum_t_ref, h, i)
            m = jnp.max(s, axis=-1, keepdims=True)
            p = jnp.exp(s - m)
            l = jnp.sum(p, axis=-1, keepdims=True)
            o_ref[:, hs] = (_dot(p, v_ref[:, hs]) / l).astype(o_ref.dtype)
            lse_ref[:, h:h + 1] = m + jnp.log(l)

    return _pcall(
        body, ride, grid=(t // bq,), in_specs=_fox_specs(t),
        out_specs=[pl.BlockSpec((bq, GROUP_WIDTH), lambda i: (i, 0)), pl.BlockSpec((bq, LANES), lambda i: (i, 0))],
        out_shape=[jax.ShapeDtypeStruct((t, GROUP_WIDTH), ACT_DTYPE), jax.ShapeDtypeStruct((t, LANES), F32)],
        semantics=("parallel",), name="fox_fwd")(pmm, pmm, pmm, cum, cum_t)


def _fox_bwd(pmm, cum, cum_t, lse, dmix, ride=None):
    t = pmm.shape[0]
    bq, nb = FOX_Q_BLOCK, t // FOX_Q_BLOCK
    pairs, per_pair = ATT_HEADS // 2, LANES // HEAD_DIM
    scale = HEAD_DIM ** -0.5

    def body(q_ref, k_ref, v_ref, cum_ref, cum_t_ref, lse_ref, do_ref, dq_ref, dk_ref, dv_ref, dct_ref, dcq_ref):
        g, i = pl.program_id(0), pl.program_id(1)

        @pl.when(i == 0)
        def _():
            dk_ref[...] = jnp.zeros_like(dk_ref)
            dv_ref[...] = jnp.zeros_like(dv_ref)

        @pl.when((i == 0) & (g == 0))
        def _():
            dct_ref[...] = jnp.zeros_like(dct_ref)

        lane = _iota((1, LANES), 1)

        def run(n):
            causal = _iota((bq, n), 1) <= i * bq + _iota((bq, n), 0)
            dcq = jnp.zeros((bq, LANES), F32)
            for hh in range(per_pair):
                h = per_pair * g + hh
                hs = slice(hh * HEAD_DIM, (hh + 1) * HEAD_DIM)
                pick = (lane == h).astype(F32)
                cq = jnp.sum(cum_ref[...] * pick, axis=1, keepdims=True)
                lse_h = jnp.sum(lse_ref[...] * pick, axis=1, keepdims=True)
                key_gate = jnp.concatenate([cum_t_ref[kb, pl.ds(h, 1), :] for kb in range(n // bq)], axis=1)
                s = _dot(q_ref[:, hs], k_ref[:n, hs], 1, 1) * scale + (cq - key_gate)
                p = jnp.exp(jnp.where(causal, s, NEG) - lse_h)
                do = do_ref[:, hs]
                dp = _dot(do, v_ref[:n, hs], 1, 1)
                ds = p * (dp - jnp.sum(p * dp, axis=-1, keepdims=True))
                dq_ref[:, hs] = (_dot(ds, k_ref[:n, hs]) * scale).astype(dq_ref.dtype)
                dk_ref[:n, hs] += _dot(ds, q_ref[:, hs], 0, 0) * scale
                dv_ref[:n, hs] += _dot(p, do, 0, 0)
                key_side = -jnp.sum(ds, axis=0, keepdims=True)
                for kb in range(n // bq):
                    dct_ref[kb, pl.ds(h, 1), :] += key_side[:, kb * bq:(kb + 1) * bq]
                dcq = dcq + jnp.sum(ds, axis=1, keepdims=True) * pick
            dcq_ref[...] = dcq

        for kx in range(t // FOX_KEY_STEP):
            pl.when(i // (FOX_KEY_STEP // bq) == kx)(functools.partial(run, (kx + 1) * FOX_KEY_STEP))

    cols = lambda first: pl.BlockSpec((bq, LANES), lambda g, i: (i, first + g))
    keys = lambda first: pl.BlockSpec((t, LANES), lambda g, i: (0, first + g))
    per_head = pl.BlockSpec((bq, LANES), lambda g, i: (i, 0))
    fox_q, fox_k, fox_v = (GROUP_WIDTH * n // LANES for n in (2, 3, 4))
    return _pcall(
        body, ride, grid=(pairs, t // bq),
        in_specs=[cols(fox_q), keys(fox_k), keys(fox_v), per_head,
                  pl.BlockSpec((nb, ATT_HEADS, bq), lambda g, i: (0, 0, 0)), per_head, cols(GROUP_WIDTH // LANES)],
        out_specs=[cols(0), keys(0), keys(0), pl.BlockSpec((nb, LANES, bq), lambda g, i: (0, 0, 0)),
                   pl.BlockSpec((None, bq, LANES), lambda g, i: (g, i, 0))],
        out_shape=[jax.ShapeDtypeStruct((t, GROUP_WIDTH), ACT_DTYPE), jax.ShapeDtypeStruct((t, GROUP_WIDTH), F32),
                   jax.ShapeDtypeStruct((t, GROUP_WIDTH), F32), jax.ShapeDtypeStruct((nb, LANES, bq), F32),
                   jax.ShapeDtypeStruct((pairs, t, LANES), F32)],
        semantics=("arbitrary", "arbitrary"), name="fox_bwd")(pmm, pmm, pmm, cum, cum_t, lse, dmix)


CA_Q_BLOCK = 4 * CHUNK
CA_WINDOW = CA_Q_BLOCK + CA_LEFT
CA_BASE = 1024


def _ca_bias_base(rel_bias):
    n = rel_bias.shape[0]
    flat = CA_Q_BLOCK + CA_LEFT - REL_CLIP
    tail = CA_BASE - flat - (2 * REL_CLIP + 1)
    return jnp.concatenate([jnp.broadcast_to(rel_bias[:, 2 * REL_CLIP:], (n, flat)), rel_bias[:, ::-1],
                            jnp.broadcast_to(rel_bias[:, :1], (n, tail))], axis=1)


def _ca_bias_base_grad(dbase):
    flat = CA_Q_BLOCK + CA_LEFT - REL_CLIP
    mid = dbase[:, flat:flat + 2 * REL_CLIP + 1][:, ::-1]
    lo = jnp.sum(dbase[:, flat + 2 * REL_CLIP + 1:], axis=1, keepdims=True)
    hi = jnp.sum(dbase[:, :flat], axis=1, keepdims=True)
    pad = jnp.zeros((dbase.shape[0], 2 * REL_CLIP - 1), F32)
    return mid + jnp.concatenate([lo, pad, hi], axis=1)


def _ca_mask(i):
    r, j = _iota((CA_Q_BLOCK, CA_WINDOW), 0), _iota((CA_Q_BLOCK, CA_WINDOW), 1)
    rc, jc = r // CHUNK, j // CHUNK
    return (jc >= rc) & (jc <= rc + CA_LEFT // CHUNK) & (i * CA_Q_BLOCK + j >= CA_LEFT)


def _ca_fill_bias(i, base_ref, bias_scr):
    @pl.when(i == 0)
    def _():
        for h in range(ATT_HEADS):
            rows = jnp.broadcast_to(base_ref[h:h + 1, :], (CA_Q_BLOCK, CA_BASE))
            bias_scr[h] = pltpu.roll(rows, CA_BASE - CA_Q_BLOCK, 1, stride=1, stride_axis=0)[:, :CA_WINDOW]


def _ca_scores(q_ref, kp_ref, bias_scr, win, h, mask):
    hs = slice(h * HEAD_DIM, (h + 1) * HEAD_DIM)
    s = _dot(q_ref[:, hs], kp_ref[win, hs], 1, 1) * (HEAD_DIM ** -0.5)
    return jnp.where(mask, s + bias_scr[h], NEG)


CA_BIAS_SCRATCH = pltpu.VMEM((ATT_HEADS, CA_Q_BLOCK, CA_WINDOW), F32)


def _ca_fwd(pmm, kp, vp, base, ride=None):
    t = pmm.shape[0]

    def body(q_ref, kp_ref, vp_ref, base_ref, o_ref, lse_ref, bias_scr):
        i = pl.program_id(0)
        _ca_fill_bias(i, base_ref, bias_scr)
        win = pl.ds(pl.multiple_of(i * CA_Q_BLOCK, CA_Q_BLOCK), CA_WINDOW)
        mask = _ca_mask(i)
        lse_ref[...] = jnp.zeros_like(lse_ref)
        for h in range(ATT_HEADS):
            hs = slice(h * HEAD_DIM, (h + 1) * HEAD_DIM)
            s = _ca_scores(q_ref, kp_ref, bias_scr, win, h, mask)
            m = jnp.max(s, axis=-1, keepdims=True)
            p = jnp.exp(s - m)
            l = jnp.sum(p, axis=-1, keepdims=True)
            o_ref[:, hs] = (_dot(p, vp_ref[win, hs]) / l).astype(o_ref.dtype)
            lse_ref[:, h:h + 1] = m + jnp.log(l)

    padded = pl.BlockSpec((t + CA_LEFT, GROUP_WIDTH), lambda i: (0, 0))
    return _pcall(
        body, ride, grid=(t // CA_Q_BLOCK,),
        in_specs=[pl.BlockSpec((CA_Q_BLOCK, GROUP_WIDTH), lambda i: (i, 0)), padded, padded,
                  pl.BlockSpec((ATT_HEADS, CA_BASE), lambda i: (0, 0))],
        out_specs=[pl.BlockSpec((CA_Q_BLOCK, GROUP_WIDTH), lambda i: (i, 0)),
                   pl.BlockSpec((CA_Q_BLOCK, LANES), lambda i: (i, 0))],
        out_shape=[jax.ShapeDtypeStruct((t, GROUP_WIDTH), ACT_DTYPE), jax.ShapeDtypeStruct((t, LANES), F32)],
        scratch_shapes=[CA_BIAS_SCRATCH], semantics=("arbitrary",), name="ca_fwd")(pmm, kp, vp, base)


def _ca_bwd(pmm, kp, vp, base, lse, dmix, ride=None):
    t = pmm.shape[0]
    scale = HEAD_DIM ** -0.5

    def body(q_ref, kp_ref, vp_ref, base_ref, lse_ref, do_ref, dq_ref, dkp_ref, dvp_ref, dbase_ref, bias_scr):
        i = pl.program_id(0)
        _ca_fill_bias(i, base_ref, bias_scr)

        @pl.when(i == 0)
        def _():
            dkp_ref[...] = jnp.zeros_like(dkp_ref)
            dvp_ref[...] = jnp.zeros_like(dvp_ref)
            dbase_ref[...] = jnp.zeros_like(dbase_ref)

        win = pl.ds(pl.multiple_of(i * CA_Q_BLOCK, CA_Q_BLOCK), CA_WINDOW)
        mask = _ca_mask(i)
        flip = (_iota((CA_Q_BLOCK, CA_Q_BLOCK), 0) + _iota((CA_Q_BLOCK, CA_Q_BLOCK), 1) == CA_Q_BLOCK - 1).astype(F32)
        for h in range(ATT_HEADS):
            hs = slice(h * HEAD_DIM, (h + 1) * HEAD_DIM)
            s = _ca_scores(q_ref, kp_ref, bias_scr, win, h, mask)
            p = jnp.exp(s - lse_ref[:, h:h + 1])
            do = do_ref[:, hs]
            dp = _dot(do, vp_ref[win, hs], 1, 1)
            ds = p * (dp - jnp.sum(p * dp, axis=-1, keepdims=True))
            dq_ref[:, hs] = (_dot(ds, kp_ref[win, hs]) * scale).astype(dq_ref.dtype)
            dkp_ref[win, hs] += _dot(ds, q_ref[:, hs], 0, 0) * scale
            dvp_ref[win, hs] += _dot(p, do, 0, 0)
            rev = jnp.concatenate([_dot(flip, ds), jnp.zeros((CA_Q_BLOCK, CA_BASE - CA_WINDOW), F32)], axis=1)
            lined = pltpu.roll(rev, 1, 1, stride=1, stride_axis=0)
            dbase_ref[h:h + 1, :] += jnp.sum(lined, axis=0, keepdims=True)

    padded = pl.BlockSpec((t + CA_LEFT, GROUP_WIDTH), lambda i: (0, 0))
    return _pcall(
        body, ride, grid=(t // CA_Q_BLOCK,),
        in_specs=[pl.BlockSpec((CA_Q_BLOCK, GROUP_WIDTH), lambda i: (i, 0)), padded, padded,
                  pl.BlockSpec((ATT_HEADS, CA_BASE), lambda i: (0, 0)),
                  pl.BlockSpec((CA_Q_BLOCK, LANES), lambda i: (i, 0)),
                  pl.BlockSpec((CA_Q_BLOCK, GROUP_WIDTH), lambda i: (i, 0))],
        out_specs=[pl.BlockSpec((CA_Q_BLOCK, GROUP_WIDTH), lambda i: (i, 0)), padded, padded,
                   pl.BlockSpec((ATT_HEADS, CA_BASE), lambda i: (0, 0))],
        out_shape=[jax.ShapeDtypeStruct((t, GROUP_WIDTH), ACT_DTYPE),
                   jax.ShapeDtypeStruct((t + CA_LEFT, GROUP_WIDTH), F32),
                   jax.ShapeDtypeStruct((t + CA_LEFT, GROUP_WIDTH), F32),
                   jax.ShapeDtypeStruct((ATT_HEADS, CA_BASE), F32)],
        scratch_shapes=[CA_BIAS_SCRATCH], semantics=("arbitrary",), name="ca_bwd")(pmm, kp, vp, base, lse, dmix)


GELU_C = 0.7978845608028654
GELU_A = 0.044715


def _shift_down(v, k, fill, period=None):
    rows = _iota(v.shape, 0)
    rows = rows if period is None else rows & (period - 1)
    return jnp.where(rows >= k, pltpu.roll(v, k, 0), fill)


def _shift_up(v, k, fill, period=None):
    t = v.shape[0]
    rows = _iota(v.shape, 0)
    rows, length = (rows, t) if period is None else (rows & (period - 1), period)
    return jnp.where(rows < length - k, pltpu.roll(v, t - k, 0), fill)


LRU_SCAN_BLOCK = 256


def _linear_scan(a, b, reverse=False):
    shift = _shift_up if reverse else _shift_down
    k = 1
    while k < LRU_SCAN_BLOCK:
        b = a * shift(b, k, 0.0, LRU_SCAN_BLOCK) + b
        a = a * shift(a, k, 1.0, LRU_SCAN_BLOCK)
        k *= 2
    nb = a.shape[0] // LRU_SCAN_BLOCK
    carry = jnp.zeros((1, a.shape[1]), F32)
    out = [None] * nb
    for blk in (reversed(range(nb)) if reverse else range(nb)):
        rows = slice(blk * LRU_SCAN_BLOCK, (blk + 1) * LRU_SCAN_BLOCK)
        h = b[rows] + a[rows] * carry
        out[blk] = h
        carry = h[0:1] if reverse else h[LRU_SCAN_BLOCK - 1:LRU_SCAN_BLOCK]
    return jnp.concatenate(out, axis=0)


def _neg_expm1(y):
    series = -y * (1.0 + y * (0.5 + y * (1.0 / 6.0 + y * (1.0 / 24.0 + y * (1.0 / 120.0)))))
    return jnp.where(y > -0.1, series, 1.0 - jnp.exp(y))


def _lru_forward(x, g_in, cw, cb, wa, ba, wx, bx, lam):
    xs = [_shift_down(x, CONV_WIDTH - 1 - j, 0.0) for j in range(CONV_WIDTH - 1)] + [x]
    xc = cb + sum(cw[j:j + 1, :] * xs[j] for j in range(CONV_WIDTH))
    r = jax.nn.sigmoid(_dot(xc, wa) + ba)
    i = jax.nn.sigmoid(_dot(xc, wx) + bx)
    lsl = _log_sigmoid(lam)
    la = LRU_C * r * lsl
    a = jnp.exp(la)
    s = jnp.sqrt(_neg_expm1(2.0 * la))
    h = _linear_scan(a, s * (i * xc))
    u = GELU_C * (g_in + GELU_A * g_in * g_in * g_in)
    th = jnp.tanh(u)
    gelu = 0.5 * g_in * (1.0 + th)
    return xs, xc, r, i, lsl, a, s, h, th, gelu


def _lru_specs(t):
    col = lambda off: pl.BlockSpec((t, LANES), lambda j: (0, j + off))
    vec = pl.BlockSpec((1, LANES), lambda j: (0, j))
    mat = pl.BlockSpec((None, LANES, LANES), lambda j: (j, 0, 0))
    return [col(0), col(GROUP_WIDTH // LANES), pl.BlockSpec((CONV_WIDTH, LANES), lambda j: (0, j)),
            vec, mat, vec, mat, vec, vec]


def _lru_fwd(pel, conv_w, conv_b, wa, ba, wx, bx, lam, ride=None):
    t = pel.shape[0]

    def body(g_ref, x_ref, cw_ref, cb_ref, wa_ref, ba_ref, wx_ref, bx_ref, lam_ref, o_ref):
        res = _lru_forward(x_ref[...], g_ref[...], cw_ref[...], cb_ref[...], wa_ref[...], ba_ref[...],
                           wx_ref[...], bx_ref[...], lam_ref[...])
        o_ref[...] = (res[7] * res[9]).astype(o_ref.dtype)

    return _pcall(
        body, ride, grid=(GROUP_WIDTH // LANES,), in_specs=_lru_specs(t),
        out_specs=pl.BlockSpec((t, LANES), lambda j: (0, j)),
        out_shape=jax.ShapeDtypeStruct((t, GROUP_WIDTH), ACT_DTYPE),
        semantics=("parallel",), name="lru_fwd")(pel, pel, conv_w, conv_b, wa, ba, wx, bx, lam)


def _lru_bwd(pel, conv_w, conv_b, wa, ba, wx, bx, lam, dmix, ride=None):
    t = pel.shape[0]

    def body(g_ref, x_ref, cw_ref, cb_ref, wa_ref, ba_ref, wx_ref, bx_ref, lam_ref, do_ref,
             dg_ref, dx_ref, dcw_ref, dcb_ref, dwa_ref, dba_ref, dwx_ref, dbx_ref, dlam_ref):
        g_in, cw, lam = g_ref[...], cw_ref[...], lam_ref[...]
        xs, xc, r, i, lsl, a, s, h, th, gelu = _lru_forward(
            x_ref[...], g_in, cw, cb_ref[...], wa_ref[...], ba_ref[...], wx_ref[...], bx_ref[...], lam)
        dout = do_ref[...]
        dgelu = 0.5 * (1.0 + th) + 0.5 * g_in * (1.0 - th * th) * GELU_C * (1.0 + 3.0 * GELU_A * g_in * g_in)
        dg_ref[...] = (dout * h * dgelu).astype(dg_ref.dtype)
        gsum = _linear_scan(_shift_up(a, 1, 0.0), dout * gelu, reverse=True)
        da = gsum * _shift_down(h, 1, 0.0)
        di = gsum * s * xc
        dla = da * a - gsum * (i * xc) * (a * a / s)
        dlam_ref[...] = jnp.sum(dla * (LRU_C * r), axis=0, keepdims=True) * jax.nn.sigmoid(-lam)
        dpr = dla * (LRU_C * lsl) * r * (1.0 - r)
        dpi = di * i * (1.0 - i)
        dxc = gsum * s * i + _dot(dpr, wa_ref[...], 1, 1) + _dot(dpi, wx_ref[...], 1, 1)
        xct = xc.T
        dwa_ref[...] = _dot(xct, dpr)
        dwx_ref[...] = _dot(xct, dpi)
        dba_ref[...] = jnp.sum(dpr, axis=0, keepdims=True)
        dbx_ref[...] = jnp.sum(dpi, axis=0, keepdims=True)
        dcb_ref[...] = jnp.sum(dxc, axis=0, keepdims=True)
        for j in range(CONV_WIDTH):
            dcw_ref[j:j + 1, :] = jnp.sum(dxc * xs[j], axis=0, keepdims=True)
        dx = cw[CONV_WIDTH - 1:CONV_WIDTH, :] * dxc
        for j in range(CONV_WIDTH - 1):
            dx = dx + cw[j:j + 1, :] * _shift_up(dxc, CONV_WIDTH - 1 - j, 0.0)
        dx_ref[...] = dx.astype(dx_ref.dtype)

    col = pl.BlockSpec((t, LANES), lambda j: (0, j))
    vec = pl.BlockSpec((1, LANES), lambda j: (0, j))
    mat = pl.BlockSpec((None, LANES, LANES), lambda j: (j, 0, 0))
    nb = GROUP_WIDTH // LANES
    vshape = jax.ShapeDtypeStruct((1, GROUP_WIDTH), F32)
    mshape = jax.ShapeDtypeStruct((nb, LANES, LANES), F32)
    return _pcall(
        body, ride, grid=(nb,),
        in_specs=_lru_specs(t) + [pl.BlockSpec((t, LANES), lambda j: (0, j + nb))],
        out_specs=[col, col, pl.BlockSpec((CONV_WIDTH, LANES), lambda j: (0, j)), vec, mat, vec, mat, vec, vec],
        out_shape=[jax.ShapeDtypeStruct((t, GROUP_WIDTH), ACT_DTYPE), jax.ShapeDtypeStruct((t, GROUP_WIDTH), ACT_DTYPE),
                   jax.ShapeDtypeStruct((CONV_WIDTH, GROUP_WIDTH), F32), vshape, mshape, vshape, mshape, vshape, vshape],
        semantics=("parallel",), name="lru_bwd")(
            pel, pel, conv_w, conv_b, wa, ba, wx, bx, lam, dmix)


def _block_diag_pairs(w):
    z = jnp.zeros((LRU_BLOCK_DIM, LRU_BLOCK_DIM), w.dtype)
    return jnp.stack([jnp.block([[w[2 * j], z], [z, w[2 * j + 1]]]) for j in range(w.shape[0] // 2)])


def _block_diag_pairs_grad(dw):
    b = LRU_BLOCK_DIM
    return jnp.stack([dw[n // 2, (n % 2) * b:(n % 2 + 1) * b, (n % 2) * b:(n % 2 + 1) * b] for n in range(2 * dw.shape[0])])


def _row_tile(r):
    return ROW_TILE if r % ROW_TILE == 0 else r


def _pair_sum(g, got, place, name):
    _, r, c = g.shape
    tile = r

    def body(place_ref, a_ref, b_ref, o_ref):
        o_ref[...] = (a_ref[...].astype(F32) + b_ref[...].astype(F32)).astype(o_ref.dtype)

    blk = pl.BlockSpec((1, tile, c), lambda k, i, place_ref: (k, i, 0))
    return pl.pallas_call(
        body,
        grid_spec=pltpu.PrefetchScalarGridSpec(
            num_scalar_prefetch=1, grid=(N_CHIPS, r // tile),
            in_specs=[pl.BlockSpec((1, tile, c), lambda k, i, place_ref: (2 * k + place_ref[0], i, 0)), blk],
            out_specs=blk),
        out_shape=jax.ShapeDtypeStruct(got.shape, got.dtype),
        compiler_params=_params("parallel", "parallel"), name=name)(place, g, got)


def _adamw_update(g, w_ref, m_ref, v_ref, g_ref, d_ref, nm_ref, nv_ref):
    nm = ADAM_B1 * m_ref[...] + (1.0 - ADAM_B1) * g
    nv = ADAM_B2 * v_ref[...] + (1.0 - ADAM_B2) * jnp.square(g)
    m_hat = nm / (1.0 - ADAM_B1 ** ADAM_STEP)
    v_hat = nv / (1.0 - ADAM_B2 ** ADAM_STEP)
    g_ref[...] = g
    d_ref[...] = -ADAM_LR * (m_hat / (jnp.sqrt(v_hat) + ADAM_EPS) + ADAM_WD * w_ref[...])
    nm_ref[...] = nm
    nv_ref[...] = nv


def _adamw_sharded(parts, w, m, v, place, name, ride=None):
    n_layers, r, c = w.shape
    tile = _row_tile(r)
    nb = r // tile
    counts = [1 + len(recvs) for _, recvs in parts]

    def body(place_ref, *refs):
        layer = pl.program_id(0)
        g, at = None, 0
        for l in range(n_layers):
            g_l = refs[at][0].astype(F32)
            for r_ref in refs[at + 1:at + counts[l]]:
                for k in range(r_ref.shape[0]):
                    g_l = g_l + r_ref[k].astype(F32)
            g = g_l if g is None else jnp.where(layer == l, g_l, g)
            at += counts[l]
        _adamw_update(g, *refs[at:])

    def part_specs(l, recvs):
        rows = lambda q, i: jnp.where(q < l, 0, jnp.where(q > l, nb - 1, i))
        return ([pl.BlockSpec((1, tile, c), lambda q, i, place_ref: (place_ref[1], rows(q, i), 0))] +
                [pl.BlockSpec((a.shape[0], tile, c), lambda q, i, place_ref: (0, rows(q, i), 0)) for a in recvs])

    in_specs, args = [], []
    for l, (s, recvs) in enumerate(parts):
        in_specs += part_specs(l, recvs)
        args += [s, *recvs]
    blk = pl.BlockSpec((None, tile, c), lambda q, i, place_ref: (q, i, 0))
    out = jax.ShapeDtypeStruct((n_layers, r, c), F32)
    return _pcall(body, ride, grid=(n_layers, nb), in_specs=in_specs + [blk, blk, blk], out_specs=[blk, blk, blk, blk],
                  out_shape=[out, out, out, out], semantics=("arbitrary", "arbitrary"), name=name, prefetch=True)(
                      place, *args, w, m, v)


def _adamw_small(repl_parts, vec_parts, w, m, v, place):
    n_r, n = len(repl_parts), len(w)
    shapes = [a.shape for a in w]

    def body(place_ref, *refs):
        parts, rest = refs[:n], refs[n:]
        for k in range(n):
            take = (lambda p: parts[k][p]) if k < n_r else (lambda p: parts[k][p, 0])
            g = take(0)
            for p in range(1, N_DEV):
                g = g + take(p)
            _adamw_update(g, rest[k], rest[n + k], rest[2 * n + k], *rest[3 * n + 4 * k:3 * n + 4 * k + 4])

    def whole(shape):
        return pl.BlockSpec(shape, lambda i, place_ref: (0,) * len(shape))

    def mine(shard):
        return pl.BlockSpec((N_DEV, 1) + shard, lambda i, place_ref: (0, place_ref[2]) + (0,) * len(shard))

    in_specs = [whole(a.shape) for a in repl_parts] + [mine(s) for s in shapes[n_r:]] + [whole(s) for s in shapes] * 3
    outs = pl.pallas_call(
        body,
        grid_spec=pltpu.PrefetchScalarGridSpec(
            num_scalar_prefetch=1, grid=(1,), in_specs=in_specs,
            out_specs=[whole(s) for s in shapes for _ in range(4)]),
        out_shape=[jax.ShapeDtypeStruct(s, F32) for s in shapes for _ in range(4)],
        compiler_params=_params("arbitrary"), name="adamw_small")(place, *repl_parts, *vec_parts, *w, *m, *v)
    return [outs[4 * k:4 * k + 4] for k in range(n)]


SHARDED = {"norm_w": 2, "w_in_even": 2, "gla_w_a_up": 2, "w_out_even": 1, "w_in_odd": 2, "conv_w": 2, "conv_b": 1,
           "lru_b_a": 1, "lru_b_x": 1, "lru_lambda": 1, "w_out_odd": 1, "w_mlp_up": 2, "w_mlp_down": 1}
REPLICATED_EVEN = ["gla_b_a", "gla_norm_w", "fox_b_f"]
REPLICATED_ODD = ["rel_bias", "lru_w_a", "lru_w_x"]
REPLICATED = REPLICATED_EVEN + REPLICATED_ODD
WEIGHTS = ["norm_w", "w_in_even", "gla_w_a_up", "gla_b_a", "gla_norm_w", "fox_b_f", "w_out_even", "w_in_odd",
           "rel_bias", "conv_w", "conv_b", "lru_w_a", "lru_b_a", "lru_w_x", "lru_b_x", "lru_lambda", "w_out_odd",
           "w_mlp_up", "w_mlp_down"]
MATRICES = ("w_in_even", "w_out_even", "w_in_odd", "w_out_odd", "w_mlp_up", "w_mlp_down")
TRANSPOSED = ("w_in_even", "w_in_odd")
VECTORS = tuple(n for n in SHARDED if n not in MATRICES)
MATRIX_BLOCKS = (("w_in_even", 0), ("w_out_even", 0), ("w_in_odd", 0), ("w_out_odd", 0),
                 ("w_mlp_up", 0), ("w_mlp_up", 1), ("w_mlp_down", 0), ("w_mlp_down", 1))


def _join_shards(blocks, axis):
    moved = jnp.moveaxis(blocks, 0, axis)
    shape = moved.shape
    return moved.reshape(shape[:axis] + (shape[axis] * shape[axis + 1],) + shape[axis + 2:])


def _split_shards(full, axis):
    shape = full.shape
    cut = full.reshape(shape[:axis] + (N_DEV, shape[axis] // N_DEV) + shape[axis + 1:])
    return jnp.moveaxis(cut, axis, 0)


EVEN_SPLITS = (0, 256, 512, 1024, 1536, 1552, 2064, 2576, 3088, 3096)


def _even_in_split(wt):
    c = [wt[EVEN_SPLITS[k]:EVEN_SPLITS[k + 1]] for k in range(9)]
    gq, gk, gv, gr, ga, fq, fk, fv, ff = c
    padrows = lambda a: jnp.pad(a, ((0, LANES - a.shape[0]), (0, 0)))
    return jnp.concatenate([gq, gk, gv, fq, fk, fv], axis=0), jnp.concatenate([gr, padrows(ga), padrows(ff)], axis=0)


def _even_in_merge(dmm, dele):
    return jnp.concatenate([dmm[:1024], dele[:512], dele[512:512 + GLA_RANK], dmm[1024:2560],
                            dele[640:640 + ATT_HEADS]], axis=0)


def _forward_backward(x, target, shard, vec_shard, w, place):
    w = dict(w)
    g, dnorm, sums, recv = {}, {}, {}, {}
    nrm = lambda l, k: w["norm_w"][l, k][None, :]
    gather = lambda *keys: _gather_plan([shard[k] for k in keys])
    blocks = lambda r, c: (N_DEV, r // N_DEV, c)

    def pair_sum(key):
        sums[key] = _pair_sum(g[key], got[key], place, f"rs_pair_sum_{key[0]}_{key[1]}")

    got = {}

    def mlp_fwd(xin, layer, ride_up, ride_down):
        up = _mm(xin, w["w_mlp_up"][layer], out_dtype=ACT_DTYPE, tm=TM_FWD, tn=D_FF // N_DEV, b_blocked=True,
                 a_norm=nrm(layer, 2), name=f"mlp_up_{layer}", ride=ride_up)
        (u, h), rode_up = up if ride_up is not None else (up, None)
        down = _mm(u, w["w_mlp_down"][layer], out_dtype=F32, tm=TM_DX // 2, tn=D_MODEL, a_sqrelu=True,
                   res_norm=(xin, nrm(layer, 3)), name=f"mlp_down_{layer}", ride=ride_down)
        (yv, xout), rode_down = down if ride_down is not None else (down, None)
        return xout, (xin, h, u, yv), rode_up, rode_down

    def mlp_bwd(dxout, saved, layer, ride):
        xin, h, u, yv = saved
        k_up, k_down = ("w_mlp_up", layer), ("w_mlp_down", layer)
        res = _mm(dxout, w["w_mlp_down"][layer], nt=True, out_dtype=ACT_DTYPE, tm=TM_DX, tn=TN, drelu_of=u,
                  a_norm_bwd=(yv, nrm(layer, 3)), name=f"mlp_down_dx_{layer}", ride=ride)
        (du, dy, dnorm[(layer, 3)]), rode = res if ride is not None else (res, None)
        g[k_down] = _mm(u, dy, ta=True, out_dtype=WIRE_DTYPE, tm=TM_DW, tn=TN, a_sqrelu=True,
                        name=f"mlp_down_dw_{layer}").reshape(blocks(D_FF, D_MODEL))
        g[k_up] = _mm(h, du, ta=True, out_dtype=WIRE_DTYPE, tm=TM_DW, tn=D_FF // N_DEV, out_blocked=True,
                      name=f"mlp_up_dw_{layer}")
        w_up = jnp.moveaxis(w["w_mlp_up"][layer], 0, 1).reshape(D_MODEL, D_FF)
        (dxin, dnorm[(layer, 2)]), (got[k_down], got[k_up]) = _mm(
            du, w_up, nt=True, out_dtype=F32, tm=TM_DX // 2, tn=D_MODEL, norm_bwd=(xin, nrm(layer, 2), dxout),
            name=f"mlp_up_dx_{layer}", ride=_sibling_plan([g[k_down], g[k_up]]))
        pair_sum(k_down)
        pair_sum(k_up)
        return dxin, rode

    first = _run_plan(_gather_plan([shard[("w_in_even", 0)]] + [vec_shard[n] for n in VECTORS]),
                      "weights_all_gather_first")
    w["w_in_even"] = first[0].reshape(-1, D_MODEL)
    for n, b in zip(VECTORS, first[1:]):
        w[n] = _join_shards(b, SHARDED[n])
    w["w_mlp_up"], w["w_mlp_down"] = [None] * DEPTH, [None] * DEPTH

    wmm_e, wel_e = _even_in_split(w["w_in_even"])
    w_up_pad = jnp.pad(w["gla_w_a_up"][0], ((0, LANES - GLA_RANK), (0, 0)))
    b_f_pad = jnp.pad(w["fox_b_f"], ((0, 0), (0, LANES - ATT_HEADS)))
    (pmm0, h0), (w_out_even,) = _mm(x, wmm_e, nt=True, out_dtype=ACT_DTYPE, tm=TM_FWD, tn=TN, a_norm=nrm(0, 0),
                                    name="in_even_mm", ride=gather(("w_out_even", 0)))
    pel0 = _mm(h0, wel_e, nt=True, out_dtype=F32, tm=TM_FWD, tn=768, name="in_even_el")
    (out_a, states), (w["w_mlp_up"][0],) = _gla_fwd(pmm0, pel0, w_up_pad, w["gla_b_a"], w["gla_norm_w"],
                                                    ride=gather(("w_mlp_up", 0)))
    cum, cum_t = _fox_gate_fwd(pel0, b_f_pad)
    (out_b, lse_b), (w_mlp_down0, w_in_odd) = _fox_fwd(pmm0, cum, cum_t,
                                                       ride=gather(("w_mlp_down", 0), ("w_in_odd", 0)))
    w["w_out_even"] = w_out_even.reshape(D_MODEL, D_MODEL)
    w["w_mlp_down"][0] = w_mlp_down0.reshape(D_FF, D_MODEL)
    mix_in0 = jnp.concatenate([out_a, out_b], axis=1)
    mix0, x1 = _mm(mix_in0, w["w_out_even"], out_dtype=F32, tm=TM_DX, tn=D_MODEL, res_norm=(x, nrm(0, 1)),
                   name="out_even")
    x2, mlp0, _, (w["w_mlp_up"][1],) = mlp_fwd(x1, 0, None, gather(("w_mlp_up", 1)))
    w["w_in_odd"] = w_in_odd.reshape(-1, D_MODEL)

    w_in_o = w["w_in_odd"]
    n_mm_o = 3 * GROUP_WIDTH
    wa_bd, wx_bd = _block_diag_pairs(w["lru_w_a"][0]), _block_diag_pairs(w["lru_w_x"][0])
    base = _ca_bias_base(w["rel_bias"][0])
    pmm1, h1 = _mm(x2, w_in_o[:n_mm_o], nt=True, out_dtype=ACT_DTYPE, tm=TM_FWD, tn=TN, a_norm=nrm(1, 0),
                   name="in_odd_mm")
    pel1 = _mm(h1, w_in_o[n_mm_o:], nt=True, out_dtype=F32, tm=TM_FWD, tn=TN, name="in_odd_el")
    kp = jnp.pad(pmm1[:, GROUP_WIDTH:2 * GROUP_WIDTH], ((CA_LEFT, 0), (0, 0)))
    vp = jnp.pad(pmm1[:, 2 * GROUP_WIDTH:], ((CA_LEFT, 0), (0, 0)))
    (out_c, lse_c), (w_mlp_down1,) = _ca_fwd(pmm1, kp, vp, base, ride=gather(("w_mlp_down", 1)))
    w["w_mlp_down"][1] = w_mlp_down1.reshape(D_FF, D_MODEL)
    lru_args = (pel1, w["conv_w"][0], w["conv_b"], wa_bd, w["lru_b_a"], wx_bd, w["lru_b_x"], w["lru_lambda"])
    out_d, (w_out_odd,) = _lru_fwd(*lru_args, ride=gather(("w_out_odd", 0)))
    w["w_out_odd"] = w_out_odd.reshape(D_MODEL, D_MODEL)
    mix_in1 = jnp.concatenate([out_c, out_d], axis=1)
    mix1, x3 = _mm(mix_in1, w["w_out_odd"], out_dtype=F32, tm=TM_DX, tn=D_MODEL, res_norm=(x2, nrm(1, 1)),
                   name="out_odd")
    x4, mlp1, _, _ = mlp_fwd(x3, 1, None, None)

    loss, dx4 = _loss_fwd_bwd(x4, target)

    k_oo, k_io, k_oe, k_ie = ("w_out_odd", 0), ("w_in_odd", 0), ("w_out_even", 0), ("w_in_even", 0)
    mlp_keys = lambda l: [("w_mlp_down", l), ("w_mlp_up", l)]
    dx3, _ = mlp_bwd(dx4, mlp1, 1, None)
    dmix_in1, dmix1, dnorm[(1, 1)] = _mm(dx3, w["w_out_odd"], nt=True, out_dtype=F32, tm=TM_DX, tn=TN,
                                         a_norm_bwd=(mix1, nrm(1, 1)), name="out_odd_dx")
    g[k_oo] = _mm(mix_in1, dmix1, ta=True, out_dtype=WIRE_DTYPE, tm=TM_DW, tn=TN, name="out_odd_dw").reshape(
        blocks(D_MODEL, D_MODEL))
    (dq_c, dkp, dvp, dbase), rode = _ca_bwd(
        pmm1, kp, vp, base, lse_c, dmix_in1,
        ride=_join_plans(_chip_plan([sums[k] for k in mlp_keys(1)]), _sibling_plan([g[k_oo]])))
    recv.update(zip(mlp_keys(1), rode[:2]))
    got[k_oo] = rode[2]
    pair_sum(k_oo)
    (dgate, dxin, g_conv_w, g_conv_b, dwa_bd, g_lru_b_a, dwx_bd, g_lru_b_x, g_lru_lambda), (recv[k_oo],) = _lru_bwd(
        *lru_args, dmix_in1, ride=_chip_plan([sums[k_oo]]))
    dp1 = jnp.concatenate([dq_c, dkp[CA_LEFT:].astype(ACT_DTYPE), dvp[CA_LEFT:].astype(ACT_DTYPE), dgate, dxin], axis=1)
    g[k_io] = _mm(dp1, h1, ta=True, out_dtype=WIRE_DTYPE, tm=dp1.shape[1] // 2, tn=TN, name="in_odd_dw").reshape(
        blocks(dp1.shape[1], D_MODEL))
    (dx2, dnorm[(1, 0)]), (got[k_io],) = _mm(dp1, w_in_o, out_dtype=F32, tm=TM_DX // 2, tn=D_MODEL,
                                             norm_bwd=(x2, nrm(1, 0), dx3), name="in_odd_dx",
                                             ride=_sibling_plan([g[k_io]]))
    pair_sum(k_io)
    g["rel_bias"] = _ca_bias_base_grad(dbase)[None]
    g["conv_w"], g["conv_b"] = g_conv_w[None], g_conv_b
    g["lru_w_a"], g["lru_w_x"] = _block_diag_pairs_grad(dwa_bd)[None], _block_diag_pairs_grad(dwx_bd)[None]
    g["lru_b_a"], g["lru_b_x"], g["lru_lambda"] = g_lru_b_a, g_lru_b_x, g_lru_lambda

    dx1, (recv[k_io],) = mlp_bwd(dx2, mlp0, 0, _chip_plan([sums[k_io]]))
    dmix_in0, dmix0, dnorm[(0, 1)] = _mm(dx1, w["w_out_even"], nt=True, out_dtype=F32, tm=TM_DX, tn=TN,
                                         a_norm_bwd=(mix0, nrm(0, 1)), name="out_even_dx")
    g[k_oe] = _mm(mix_in0, dmix0, ta=True, out_dtype=WIRE_DTYPE, tm=TM_DW, tn=TN, name="out_even_dw").reshape(
        blocks(D_MODEL, D_MODEL))
    k_md0, k_mu0 = mlp_keys(0)
    (dq_a, dk_a, dv_a, dr_a, da_a, dw_up_pad, g_gla_b_a, g_gla_norm_w), (got[k_oe], *repl_odd_parts) = _gla_bwd(
        pmm0, pel0, w_up_pad, w["gla_b_a"], w["gla_norm_w"], states, dmix_in0,
        ride=_join_plans(_sibling_plan([g[k_oe]]), _gather_plan([g[n] for n in REPLICATED_ODD])))
    pair_sum(k_oe)
    (dq_b, dk_b, dv_b, dcum_t, dcum_q), (recv[k_md0], recv[k_mu0], recv[k_oe]) = _fox_bwd(
        pmm0, cum, cum_t, lse_b, dmix_in0, ride=_chip_plan([sums[k_md0], sums[k_mu0], sums[k_oe]]))
    df_b, db_f = _fox_gate_bwd(pel0, b_f_pad, dcum_t, dcum_q)
    g["gla_w_a_up"] = dw_up_pad[:GLA_RANK][None]
    g["gla_b_a"], g["gla_norm_w"], g["fox_b_f"] = g_gla_b_a, g_gla_norm_w, db_f[:, :ATT_HEADS]
    dp0 = jnp.concatenate([dq_a, dk_a, dv_a, dq_b, dk_b.astype(ACT_DTYPE), dv_b.astype(ACT_DTYPE), dr_a, da_a, df_b],
                          axis=1)
    w_perm = jnp.concatenate([wmm_e, wel_e], axis=0)
    n_mm_e = wmm_e.shape[0]
    dw_perm = _mm(dp0, h0, ta=True, out_dtype=WIRE_DTYPE, tm=dp0.shape[1] // 2, tn=TN, name="in_even_dw")
    dw_even = _even_in_merge(dw_perm[:n_mm_e], dw_perm[n_mm_e:])
    g[k_ie] = dw_even.reshape(blocks(dw_even.shape[0], D_MODEL))
    dh0, (got[k_ie],) = _mm(dp0, w_perm, out_dtype=F32, tm=TM_DX, tn=TN, name="in_even_dx",
                            ride=_sibling_plan([g[k_ie]]))
    pair_sum(k_ie)
    dx0, dnorm[(0, 0)] = _norm_bwd(dh0, x, nrm(0, 0), out_dtype=F32, add=dx1, name="norm_in_bwd_0")

    g["norm_w"] = jnp.stack([jnp.concatenate([dnorm[(l, k)] for k in range(4)], axis=0) for l in range(DEPTH)])
    recv[k_ie], losses, *small_parts = _run_plan(
        _join_plans(_chip_plan([sums[k_ie]]),
                    _gather_plan([loss] + [g[n] for n in REPLICATED_EVEN]
                                 + [_split_shards(g[n], SHARDED[n]) for n in VECTORS])), "last_exchanges")
    repl_parts = small_parts[:len(REPLICATED_EVEN)] + repl_odd_parts
    return losses, dx0, sums, recv, repl_parts, small_parts[len(REPLICATED_EVEN):]


def kernel(x, norm_w, w_in_even, gla_w_a_up, gla_b_a, gla_norm_w, fox_b_f, w_out_even, w_in_odd, rel_bias, conv_w, conv_b, lru_w_a, lru_b_a, lru_w_x, lru_b_x, lru_lambda, w_out_odd, w_mlp_up, w_mlp_down, loss_target, m_norm_w, m_w_in_even, m_gla_w_a_up, m_gla_b_a, m_gla_norm_w, m_fox_b_f, m_w_out_even, m_w_in_odd, m_rel_bias, m_conv_w, m_conv_b, m_lru_w_a, m_lru_b_a, m_lru_w_x, m_lru_b_x, m_lru_lambda, m_w_out_odd, m_w_mlp_up, m_w_mlp_down, v_norm_w, v_w_in_even, v_gla_w_a_up, v_gla_b_a, v_gla_norm_w, v_fox_b_f, v_w_out_even, v_w_in_odd, v_rel_bias, v_conv_w, v_conv_b, v_lru_w_a, v_lru_b_a, v_lru_w_x, v_lru_b_x, v_lru_lambda, v_w_out_odd, v_w_mlp_up, v_w_mlp_down):
    wts = dict(zip(WEIGHTS, (norm_w, w_in_even, gla_w_a_up, gla_b_a, gla_norm_w, fox_b_f, w_out_even, w_in_odd, rel_bias,
                             conv_w, conv_b, lru_w_a, lru_b_a, lru_w_x, lru_b_x, lru_lambda, w_out_odd, w_mlp_up,
                             w_mlp_down)))
    mom = dict(zip(WEIGHTS, (m_norm_w, m_w_in_even, m_gla_w_a_up, m_gla_b_a, m_gla_norm_w, m_fox_b_f, m_w_out_even,
                             m_w_in_odd, m_rel_bias, m_conv_w, m_conv_b, m_lru_w_a, m_lru_b_a, m_lru_w_x, m_lru_b_x,
                             m_lru_lambda, m_w_out_odd, m_w_mlp_up, m_w_mlp_down)))
    var = dict(zip(WEIGHTS, (v_norm_w, v_w_in_even, v_gla_w_a_up, v_gla_b_a, v_gla_norm_w, v_fox_b_f, v_w_out_even,
                             v_w_in_odd, v_rel_bias, v_conv_w, v_conv_b, v_lru_w_a, v_lru_b_a, v_lru_w_x, v_lru_b_x,
                             v_lru_lambda, v_w_out_odd, v_w_mlp_up, v_w_mlp_down)))
    ax, ay, ac = lax.axis_index("x"), lax.axis_index("y"), lax.axis_index("c")
    place = jnp.stack([ac, 2 * ax + ay, 4 * ax + 2 * ay + ac]).astype(jnp.int32)

    shard = {(n, l): (wts[n][l].T if n in TRANSPOSED else wts[n][l]).astype(WIRE_DTYPE) for n, l in MATRIX_BLOCKS}
    losses, dx, sums, recv, repl_parts, vec_parts = _forward_backward(
        x[0], loss_target[0], shard, {n: wts[n] for n in VECTORS}, {n: wts[n] for n in REPLICATED}, place)
    loss = jnp.sum(losses[:, 0, 0])

    view = lambda n, a: jnp.swapaxes(a, 1, 2) if n in TRANSPOSED else a
    upd = {n: [view(n, o) for o in _adamw_sharded(
        [(sums[(n, l)], [recv[(n, l)]]) for l in range(wts[n].shape[0])], view(n, wts[n]), view(n, mom[n]),
        view(n, var[n]), place, f"adamw_{n}")] for n in MATRICES}
    small = REPLICATED + list(VECTORS)
    upd.update(zip(small, _adamw_small(repl_parts, vec_parts, [wts[n] for n in small], [mom[n] for n in small],
                                       [var[n] for n in small], place)))
    return (loss, dx[None], *[upd[n][kind] for kind in range(4) for n in WEIGHTS])
```

```python
import functools
from typing import Callable, NamedTuple, Optional

import jax
import jax.numpy as jnp
from jax import lax
from jax.experimental import pallas as pl
from jax.experimental.pallas import tpu as pltpu

F32 = jnp.float32
MXU_DTYPE = jnp.bfloat16
ACT_DTYPE = jnp.bfloat16
WIRE_DTYPE = jnp.bfloat16

V7X_VMEM_BYTES = 64 * 1024 * 1024
VMEM_LIMIT = (V7X_VMEM_BYTES * 7) // 8
LANES = 128

D_MODEL = 1024
DEPTH = 2
CHUNK = 64
GROUP_WIDTH = D_MODEL // 2
D_FF = 4 * D_MODEL
NORM_EPS = 1e-6
GLA_HEADS = 4
GLA_DV = GROUP_WIDTH // GLA_HEADS
GLA_DK = GLA_DV // 2
GLA_KW = GLA_HEADS * GLA_DK
GLA_RANK = 16
GLA_GATE_TAU = 16.0
HEAD_DIM = 64
ATT_HEADS = GROUP_WIDTH // HEAD_DIM
CA_LEFT = 8 * CHUNK
REL_CLIP = 128
LRU_BLOCK_DIM = 64
CONV_WIDTH = 4
LRU_C = 8.0
N_DEV = 8

ADAM_LR = 0.001
ADAM_B1 = 0.9
ADAM_B2 = 0.999
ADAM_EPS = 1e-08
ADAM_WD = 0.01
ADAM_STEP = 10

NEG = float(jnp.finfo(jnp.float32).min)
MESH = pl.DeviceIdType.MESH


def _params(*sem):
    return pltpu.CompilerParams(dimension_semantics=sem, vmem_limit_bytes=VMEM_LIMIT)


def _dot(a, b, ca=1, cb=0):
    return lax.dot_general(a.astype(MXU_DTYPE), b.astype(MXU_DTYPE), (((ca,), (cb,)), ((), ())),
                           preferred_element_type=F32)


def _dot_exact(a, b):
    return lax.dot_general(a, b, (((1,), (0,)), ((), ())), precision=lax.Precision.HIGHEST,
                           preferred_element_type=F32)


def _log_sigmoid(x):
    return jnp.minimum(x, 0.0) - jnp.log1p(jnp.exp(-jnp.abs(x)))


def _iota(shape, axis):
    return lax.broadcasted_iota(jnp.int32, shape, axis)


ANY = pl.BlockSpec(memory_space=pl.ANY)
N_CHIPS = 4


class _Plan(NamedTuple):
    ins: list
    outs: list
    sems: list
    start: Callable
    finish: Callable
    relay: Optional[Callable] = None


def _place():
    x, y, c = lax.axis_index("x"), lax.axis_index("y"), lax.axis_index("c")
    return x, y, c, [(1 - x, y), (x, 1 - y), (1 - x, 1 - y)]


def _gather_plan(xs):
    n = len(xs)

    def parts(x_refs, out_refs, sems):
        send_sems, recv_sems, local_sems = sems
        x, y, c, chips = _place()
        me, sibling = (x, y, c), (x, y, 1 - c)

        def rows(a, px, py, pc):
            return out_refs[a].at[4 * px + 2 * py + pc]

        def copy(a, k, block, to, src=None):
            return pltpu.make_async_remote_copy(
                src_ref=rows(a, *block) if src is None else src, dst_ref=rows(a, *block),
                send_sem=send_sems.at[7 * a + k], recv_sem=recv_sems.at[7 * a + k], device_id=to, device_id_type=MESH)

        def own():
            mine = [pltpu.make_async_copy(x_refs[a], rows(a, *me), local_sems.at[a]) for a in range(n)]
            first = []
            for a in range(n):
                first.append(copy(a, 0, me, sibling, src=x_refs[a]))
                first += [copy(a, 1 + j, me, (*chip, c), src=x_refs[a]) for j, chip in enumerate(chips)]
            return mine, first

        return c, me, sibling, chips, copy, own

    def start(x_refs, out_refs, sems):
        mine, first = parts(x_refs, out_refs, sems)[-1]()
        for cp in first + mine:
            cp.start()

    def relay(x_refs, out_refs, sems):
        c, me, sibling, chips, copy, _ = parts(x_refs, out_refs, sems)
        for j, chip in enumerate(chips):
            for a in range(n):
                copy(a, 1 + j, (*chip, c), me).wait_recv()
                copy(a, 4 + j, (*chip, c), sibling).start()

    def finish(x_refs, out_refs, sems):
        c, me, sibling, chips, copy, own = parts(x_refs, out_refs, sems)
        mine, first = own()
        for a in range(n):
            copy(a, 0, sibling, me).wait_recv()
            for j, chip in enumerate(chips):
                copy(a, 4 + j, (*chip, 1 - c), me).wait_recv()
        for cp in first + [copy(a, 4 + j, (*chip, c), sibling) for j, chip in enumerate(chips) for a in range(n)]:
            cp.wait_send()
        for cp in mine:
            cp.wait()

    return _Plan(list(xs), [jax.ShapeDtypeStruct((N_DEV,) + x.shape, x.dtype) for x in xs],
                 [pltpu.SemaphoreType.DMA((7 * n,)), pltpu.SemaphoreType.DMA((7 * n,)), pltpu.SemaphoreType.DMA((n,))],
                 start, finish, relay)


def _exchange_plan(copies_of, ins, outs, per_array):
    n = len(ins)

    def start(in_refs, out_refs, sems):
        for cp in copies_of(in_refs, out_refs, sems):
            cp.start()

    def finish(in_refs, out_refs, sems):
        copies = copies_of(in_refs, out_refs, sems)
        for cp in copies:
            cp.wait_recv()
        for cp in copies:
            cp.wait_send()

    return _Plan(list(ins), outs, [pltpu.SemaphoreType.DMA((per_array * n,)), pltpu.SemaphoreType.DMA((per_array * n,))],
                 start, finish)


def _sibling_plan(gs):
    def copies_of(g_refs, got_refs, sems):
        x, y, c, _ = _place()
        return [pltpu.make_async_remote_copy(
            src_ref=g_refs[a].at[2 * k + (1 - c)], dst_ref=got_refs[a].at[k], send_sem=sems[0].at[N_CHIPS * a + k],
            recv_sem=sems[1].at[N_CHIPS * a + k], device_id=(x, y, 1 - c), device_id_type=MESH)
            for a in range(len(gs)) for k in range(N_CHIPS)]

    return _exchange_plan(copies_of, gs, [jax.ShapeDtypeStruct((N_CHIPS,) + g.shape[1:], g.dtype) for g in gs], N_CHIPS)


def _chip_plan(ss, relations=(0, 1, 2)):
    n_rel = len(relations)

    def copies_of(s_refs, out_refs, sems):
        x, y, c, chips = _place()
        return [pltpu.make_async_remote_copy(
            src_ref=s_refs[a].at[2 * chips[j][0] + chips[j][1]], dst_ref=out_refs[a].at[slot],
            send_sem=sems[0].at[n_rel * a + slot], recv_sem=sems[1].at[n_rel * a + slot],
            device_id=(*chips[j], c), device_id_type=MESH)
            for a in range(len(ss)) for slot, j in enumerate(relations)]

    return _exchange_plan(copies_of, ss, [jax.ShapeDtypeStruct((n_rel,) + s.shape[1:], s.dtype) for s in ss], n_rel)


def _join_plans(*plans):
    def cut(refs, counts):
        at = 0
        for n in counts:
            yield refs[at:at + n]
            at += n

    def each(in_refs, out_refs, sems):
        return zip(plans, cut(in_refs, [len(p.ins) for p in plans]), cut(out_refs, [len(p.outs) for p in plans]),
                   cut(sems, [len(p.sems) for p in plans]))

    def start(*refs):
        for p, i, o, s in each(*refs):
            p.start(i, o, s)

    def relay(*refs):
        for p, i, o, s in each(*refs):
            if p.relay is not None:
                p.relay(i, o, s)

    def finish(*refs):
        for p, i, o, s in each(*refs):
            p.finish(i, o, s)

    return _Plan([a for p in plans for a in p.ins], [a for p in plans for a in p.outs],
                 [a for p in plans for a in p.sems], start, finish, relay)


def _run_plan(plan, name):
    n_in, n_out = len(plan.ins), len(plan.outs)

    def body(*refs):
        args = refs[:n_in], refs[n_in:n_in + n_out], refs[n_in + n_out:]
        plan.start(*args)
        if plan.relay is not None:
            plan.relay(*args)
        plan.finish(*args)

    return pl.pallas_call(body, out_shape=plan.outs, in_specs=[ANY] * n_in, out_specs=[ANY] * n_out,
                          scratch_shapes=plan.sems, name=name)(*plan.ins)


def _pcall(body, ride, *, grid, in_specs, out_specs, out_shape, scratch_shapes=(), semantics, name, prefetch=False):
    n_pre = int(prefetch)

    def build(kernel, ins, outs, shapes, scratch, sem):
        if prefetch:
            return pl.pallas_call(
                kernel, grid_spec=pltpu.PrefetchScalarGridSpec(num_scalar_prefetch=1, grid=grid, in_specs=ins,
                                                               out_specs=outs, scratch_shapes=scratch),
                out_shape=shapes, compiler_params=_params(*sem), name=name)
        return pl.pallas_call(kernel, grid=grid, in_specs=ins, out_specs=outs, out_shape=shapes,
                              scratch_shapes=scratch, compiler_params=_params(*sem), name=name)

    if ride is None:
        return build(body, in_specs, out_specs, out_shape, list(scratch_shapes), semantics)
    single = not isinstance(out_shape, (list, tuple))
    out_specs_l, out_shape_l = ([out_specs], [out_shape]) if single else (list(out_specs), list(out_shape))
    n_in, n_out, n_scr = len(in_specs), len(out_shape_l), len(scratch_shapes)
    r_in, r_out = len(ride.ins), len(ride.outs)

    def riding(*refs):
        pre, refs = refs[:n_pre], refs[n_pre:]
        cuts = [n_in, r_in, n_out, r_out, n_scr]
        groups, at = [], 0
        for width in cuts:
            groups.append(refs[at:at + width])
            at += width
        ins, r_ins, outs, r_outs, scr = groups
        sems = refs[at:]
        first = functools.reduce(jnp.logical_and, [pl.program_id(d) == 0 for d in range(len(grid))])
        last = functools.reduce(jnp.logical_and, [pl.program_id(d) == grid[d] - 1 for d in range(len(grid))])

        @pl.when(first)
        def _():
            ride.start(r_ins, r_outs, sems)

        several_steps = any(n > 1 for n in grid)
        if ride.relay is not None and several_steps:
            @pl.when(last)
            def _():
                ride.relay(r_ins, r_outs, sems)

        body(*pre, *ins, *outs, *scr)

        @pl.when(last)
        def _():
            if ride.relay is not None and not several_steps:
                ride.relay(r_ins, r_outs, sems)
            ride.finish(r_ins, r_outs, sems)

    call = build(riding, list(in_specs) + [ANY] * r_in, out_specs_l + [ANY] * r_out, out_shape_l + list(ride.outs),
                 list(scratch_shapes) + list(ride.sems), ["arbitrary"] * len(grid))

    def run(*args):
        res = call(*args, *ride.ins)
        return (res[0] if single else list(res[:n_out])), list(res[n_out:])

    return run


def _rms(x):
    return x * lax.rsqrt(jnp.mean(x * x, axis=-1, keepdims=True) + NORM_EPS)


def _mm(a, b, *, nt=False, ta=False, out_dtype, tm, tn, a_sqrelu=False, drelu_of=None, b_blocked=False,
        out_blocked=False, a_norm=None, a_norm_bwd=None, res_norm=None, norm_bwd=None, name, ride=None):
    k, m = a.shape if ta else a.shape[::-1]
    if b_blocked:
        assert not nt and b.shape[1] == k and b.shape[2] == tn
        n = b.shape[0] * tn
    else:
        n = b.shape[0] if nt else b.shape[1]
        assert (b.shape[1] if nt else b.shape[0]) == k
    tm, tn = min(tm, m), min(tn, n)
    assert m % tm == 0 and n % tn == 0
    assert (res_norm is None and norm_bwd is None) or tn == n
    assert a_norm is None or a_norm_bwd is None
    n_in = (2 + (drelu_of is not None) + (a_norm is not None) + 2 * (a_norm_bwd is not None)
            + 2 * (res_norm is not None) + 3 * (norm_bwd is not None))

    def body(*refs):
        a_ref, b_ref = refs[0], refs[1]
        extra = list(refs[2:n_in])
        outs = list(refs[n_in:])
        o_ref = outs.pop(0)
        u_ref = extra.pop(0) if drelu_of is not None else None
        if a_norm is not None:
            wn_ref, h_ref, h_scr = extra.pop(0), outs.pop(0), outs.pop()

            @pl.when(pl.program_id(1) == 0)
            def _():
                h = (_rms(a_ref[...]) * wn_ref[...]).astype(ACT_DTYPE)
                h_scr[...] = h
                h_ref[...] = h

            av = h_scr[...]
        elif a_norm_bwd is not None:
            y_ref, wy_ref = extra.pop(0), extra.pop(0)
            dy_ref, dwy_ref, dy_scr = outs.pop(0), outs.pop(0), outs.pop()
            first_rows = pl.program_id(0) == 0

            @pl.when(pl.program_id(1) == 0)
            def _():
                yv, up = y_ref[...], a_ref[...]
                rstd = lax.rsqrt(jnp.mean(yv * yv, axis=-1, keepdims=True) + NORM_EPS)
                yhat = yv * rstd
                g = up * wy_ref[...]
                dy = (rstd * (g - yhat * jnp.mean(g * yhat, axis=-1, keepdims=True))).astype(ACT_DTYPE)
                dy_scr[...] = dy
                dy_ref[...] = dy

                @pl.when(first_rows)
                def _():
                    dwy_ref[...] = jnp.zeros_like(dwy_ref)

                dwy_ref[...] += jnp.sum(up * yhat, axis=0, keepdims=True)

            av = dy_scr[...]
        else:
            av = a_ref[...]
        if a_sqrelu:
            av = jnp.square(jnp.maximum(av.astype(F32), 0.0))
        acc = _dot(av, b_ref[...], 0 if ta else 1, 1 if nt else 0)
        if u_ref is not None:
            acc = acc * (2.0 * jnp.maximum(u_ref[...].astype(F32), 0.0))
        if norm_bwd is not None:
            x_ref, wb_ref, add_ref = extra
            dw_ref = outs[0]
            xv = x_ref[...]
            rstd = lax.rsqrt(jnp.mean(xv * xv, axis=-1, keepdims=True) + NORM_EPS)
            xhat = xv * rstd
            g = acc * wb_ref[...]
            o_ref[...] = rstd * (g - xhat * jnp.mean(g * xhat, axis=-1, keepdims=True)) + add_ref[...]

            @pl.when(pl.program_id(0) == 0)
            def _():
                dw_ref[...] = jnp.zeros_like(dw_ref)

            dw_ref[...] += jnp.sum(acc * xhat, axis=0, keepdims=True)
            return
        o_ref[...] = acc.astype(out_dtype)
        if res_norm is not None:
            res_ref, wr_ref = extra
            outs[0][...] = res_ref[...] + _rms(acc) * wr_ref[...]

    if b_blocked:
        b_spec = pl.BlockSpec((None, k, tn), lambda i, j: (j, 0, 0))
    elif nt:
        b_spec = pl.BlockSpec((tn, k), lambda i, j: (j, 0))
    else:
        b_spec = pl.BlockSpec((k, tn), lambda i, j: (0, j))
    a_spec = pl.BlockSpec((k, tm), lambda i, j: (0, i)) if ta else pl.BlockSpec((tm, k), lambda i, j: (i, 0))
    in_specs = [a_spec, b_spec]
    args = [a, b]
    if drelu_of is not None:
        in_specs.append(pl.BlockSpec((tm, tn), lambda i, j: (i, j)))
        args.append(drelu_of)
    if out_blocked:
        out_specs = [pl.BlockSpec((None, tm, tn), lambda i, j: (j, i, 0))]
        out_shape = [jax.ShapeDtypeStruct((n // tn, m, tn), out_dtype)]
    else:
        out_specs = [pl.BlockSpec((tm, tn), lambda i, j: (i, j))]
        out_shape = [jax.ShapeDtypeStruct((m, n), out_dtype)]
    scratch = []
    if a_norm is not None:
        assert not ta
        in_specs.append(pl.BlockSpec((1, k), lambda i, j: (0, 0)))
        args.append(a_norm)
        out_specs.append(pl.BlockSpec((tm, k), lambda i, j: (i, 0)))
        out_shape.append(jax.ShapeDtypeStruct((m, k), ACT_DTYPE))
        scratch.append(pltpu.VMEM((tm, k), ACT_DTYPE))
    if a_norm_bwd is not None:
        assert not ta
        in_specs += [pl.BlockSpec((tm, k), lambda i, j: (i, 0)), pl.BlockSpec((1, k), lambda i, j: (0, 0))]
        args += list(a_norm_bwd)
        out_specs += [pl.BlockSpec((tm, k), lambda i, j: (i, 0)), pl.BlockSpec((1, k), lambda i, j: (0, 0))]
        out_shape += [jax.ShapeDtypeStruct((m, k), ACT_DTYPE), jax.ShapeDtypeStruct((1, k), F32)]
        scratch.append(pltpu.VMEM((tm, k), ACT_DTYPE))
    if res_norm is not None:
        in_specs += [pl.BlockSpec((tm, n), lambda i, j: (i, 0)), pl.BlockSpec((1, n), lambda i, j: (0, 0))]
        args += list(res_norm)
        out_specs.append(pl.BlockSpec((tm, n), lambda i, j: (i, 0)))
        out_shape.append(jax.ShapeDtypeStruct((m, n), F32))
    if norm_bwd is not None:
        rows = pl.BlockSpec((tm, n), lambda i, j: (i, 0))
        in_specs += [rows, pl.BlockSpec((1, n), lambda i, j: (0, 0)), rows]
        args += list(norm_bwd)
        out_specs.append(pl.BlockSpec((1, n), lambda i, j: (0, 0)))
        out_shape.append(jax.ShapeDtypeStruct((1, n), F32))
    single = len(out_shape) == 1
    return _pcall(body, ride, grid=(m // tm, n // tn), in_specs=in_specs,
                  out_specs=out_specs[0] if single else out_specs, out_shape=out_shape[0] if single else out_shape,
                  scratch_shapes=scratch, semantics=("arbitrary", "arbitrary"), name=name)(*args)


ROW_TILE = 512
TM_FWD, TM_DX, TM_DW, TN = 2048, 1024, 1024, 512


def _norm_bwd(dy, x, w, *, out_dtype, add=None, name, ride=None):
    t, d = x.shape

    def body(*refs):
        dy_ref, x_ref, w_ref = refs[0], refs[1], refs[2]
        dx_ref, dw_ref = refs[-2], refs[-1]
        xv = x_ref[...]
        rstd = lax.rsqrt(jnp.mean(xv * xv, axis=-1, keepdims=True) + NORM_EPS)
        xhat = xv * rstd
        dyv = dy_ref[...].astype(F32)
        g = dyv * w_ref[...]
        dx = rstd * (g - xhat * jnp.mean(g * xhat, axis=-1, keepdims=True))
        if add is not None:
            dx = dx + refs[3][...]
        dx_ref[...] = dx.astype(out_dtype)

        @pl.when(pl.program_id(0) == 0)
        def _():
            dw_ref[...] = jnp.zeros_like(dw_ref)

        dw_ref[...] += jnp.sum(dyv * xhat, axis=0, keepdims=True)

    row = pl.BlockSpec((ROW_TILE, d), lambda i: (i, 0))
    vec = pl.BlockSpec((1, d), lambda i: (0, 0))
    in_specs = [row, row, vec] + ([row] if add is not None else [])
    args = [dy, x, w] + ([add] if add is not None else [])
    return _pcall(body, ride, grid=(t // ROW_TILE,), in_specs=in_specs, out_specs=[row, vec],
                  out_shape=[jax.ShapeDtypeStruct((t, d), out_dtype), jax.ShapeDtypeStruct((1, d), F32)],
                  semantics=("arbitrary",), name=name)(*args)


def _loss_fwd_bwd(y, target):
    t, d = y.shape

    def body(y_ref, t_ref, l_ref, dy_ref):
        diff = y_ref[...] - t_ref[...]
        dy_ref[...] = diff * (1.0 / d)

        @pl.when(pl.program_id(0) == 0)
        def _():
            l_ref[...] = jnp.zeros_like(l_ref)

        l_ref[...] += 0.5 * jnp.sum(jnp.mean(diff * diff, axis=-1, keepdims=True), axis=0, keepdims=True)

    row = pl.BlockSpec((ROW_TILE, d), lambda i: (i, 0))
    return pl.pallas_call(body, grid=(t // ROW_TILE,), in_specs=[row, row],
                          out_specs=[pl.BlockSpec((8, LANES), lambda i: (0, 0)), row],
                          out_shape=[jax.ShapeDtypeStruct((8, LANES), F32), jax.ShapeDtypeStruct((t, d), F32)],
                          compiler_params=_params("arbitrary"), name="loss")(y, target)


GLA_STATE = (GLA_HEADS * GLA_DV, GLA_KW)


def _gla_specs(chunk_of):
    rows = lambda width, col: pl.BlockSpec((CHUNK, width), lambda i: (chunk_of(i), col))
    const = lambda r, c: pl.BlockSpec((r, c), lambda i: (0, 0))
    return [rows(GLA_KW, 0),
            rows(GLA_KW, 1),
            rows(GROUP_WIDTH, 1),
            rows(GROUP_WIDTH, 0),
            rows(LANES, 4),
            const(LANES, GLA_KW),
            const(1, GLA_KW),
            const(1, GROUP_WIDTH)]


def _gla_chunk(q_ref, k_ref, v_ref, a_ref, wup_ref, ba_ref):
    z = _dot(a_ref[...], wup_ref[...]) + ba_ref[...]
    tri = (_iota((CHUNK, CHUNK), 1) <= _iota((CHUNK, CHUNK), 0)).astype(F32)
    cum = _dot_exact(tri, _log_sigmoid(z) * (1.0 / GLA_GATE_TAU))
    tot = cum[CHUNK - 1:CHUNK, :]
    e = jnp.exp(tot - cum)
    return (z, e, jnp.exp(tot), k_ref[...].astype(F32) * e, q_ref[...].astype(F32) * (GLA_DK ** -0.5),
            v_ref[...].astype(F32))


def _gla_head_mask():
    return _iota(GLA_STATE, 0) // GLA_DV == _iota(GLA_STATE, 1) // GLA_DK


def _gla_fwd(pmm, pel, w_up, b_a, gnorm_w, ride=None):
    t = pmm.shape[0]
    nc = t // CHUNK

    def body(q_ref, k_ref, v_ref, r_ref, a_ref, wup_ref, ba_ref, gw_ref, o_ref, st_ref, m_scr):
        @pl.when(pl.program_id(0) == 0)
        def _():
            m_scr[...] = jnp.zeros_like(m_scr)

        _, _, decay, kd, qs, vv = _gla_chunk(q_ref, k_ref, v_ref, a_ref, wup_ref, ba_ref)
        m = m_scr[...] * decay + jnp.where(_gla_head_mask(), _dot(vv, kd, 0, 0), 0.0)
        m_scr[...] = m
        st_ref[...] = m
        o = _dot(qs, m, 1, 1)
        rr = r_ref[...]
        gate = rr * jax.nn.sigmoid(rr) * gw_ref[...]
        for h in range(GLA_HEADS):
            vs = slice(h * GLA_DV, (h + 1) * GLA_DV)
            oh = o[:, vs]
            y = oh * lax.rsqrt(jnp.mean(oh * oh, axis=-1, keepdims=True) + NORM_EPS)
            o_ref[:, vs] = (y * gate[:, vs]).astype(o_ref.dtype)

    return _pcall(
        body, ride, grid=(nc,), in_specs=_gla_specs(lambda i: i),
        out_specs=[pl.BlockSpec((CHUNK, GROUP_WIDTH), lambda i: (i, 0)),
                   pl.BlockSpec((None,) + GLA_STATE, lambda i: (i, 0, 0))],
        out_shape=[jax.ShapeDtypeStruct((t, GROUP_WIDTH), ACT_DTYPE), jax.ShapeDtypeStruct((nc,) + GLA_STATE, F32)],
        scratch_shapes=[pltpu.VMEM(GLA_STATE, F32)],
        semantics=("arbitrary",), name="gla_fwd")(pmm, pmm, pmm, pel, pel, w_up, b_a, gnorm_w)


def _gla_bwd(pmm, pel, w_up, b_a, gnorm_w, states, dmix, ride=None):
    t = pmm.shape[0]
    nc = t // CHUNK
    scale = GLA_DK ** -0.5

    def body(q_ref, k_ref, v_ref, r_ref, a_ref, wup_ref, ba_ref, gw_ref, st_ref, prev_ref, do_ref,
             dq_ref, dk_ref, dv_ref, dr_ref, da_ref, dwup_ref, dba_ref, dgw_ref, dm_scr):
        step = pl.program_id(0)

        @pl.when(step == 0)
        def _():
            dm_scr[...] = jnp.zeros_like(dm_scr)
            dwup_ref[...] = jnp.zeros_like(dwup_ref)
            dba_ref[...] = jnp.zeros_like(dba_ref)
            dgw_ref[...] = jnp.zeros_like(dgw_ref)

        z, e, decay, kd, qs, vv = _gla_chunk(q_ref, k_ref, v_ref, a_ref, wup_ref, ba_ref)
        m = st_ref[...]
        m_prev = prev_ref[...] * (step < nc - 1).astype(F32)
        rr, dout, gw = r_ref[...], do_ref[...], gw_ref[...]
        sig = jax.nn.sigmoid(rr)
        silu = rr * sig
        dsilu = sig * (1.0 + rr * (1.0 - sig))
        o = _dot(qs, m, 1, 1)
        d_o, dgw = [], []
        for h in range(GLA_HEADS):
            vs = slice(h * GLA_DV, (h + 1) * GLA_DV)
            oh, dg = o[:, vs], dout[:, vs]
            rstd = lax.rsqrt(jnp.mean(oh * oh, axis=-1, keepdims=True) + NORM_EPS)
            y = oh * rstd
            dgw.append(jnp.sum(dg * y * silu[:, vs], axis=0, keepdims=True))
            dr_ref[:, vs] = (dg * y * gw[:, vs] * dsilu[:, vs]).astype(dr_ref.dtype)
            dy = dg * gw[:, vs] * silu[:, vs]
            d_o.append(rstd * (dy - y * jnp.mean(dy * y, axis=-1, keepdims=True)))
        d_o = jnp.concatenate(d_o, axis=1)
        dgw_ref[...] += jnp.concatenate(dgw, axis=1)
        dq_ref[...] = (_dot(d_o, m) * scale).astype(dq_ref.dtype)
        dm = dm_scr[...] + jnp.where(_gla_head_mask(), _dot(d_o, qs, 0, 0), 0.0)
        dv_ref[...] = _dot(kd, dm, 1, 1).astype(dv_ref.dtype)
        dkd = _dot(vv, dm)
        dk_ref[...] = (dkd * e).astype(dk_ref.dtype)
        dm_scr[...] = dm * decay
        tri_strict = (_iota((CHUNK, CHUNK), 1) < _iota((CHUNK, CHUNK), 0)).astype(F32)
        dla = jnp.sum(dm * m_prev, axis=0, keepdims=True) * decay + _dot_exact(tri_strict, dkd * kd)
        dz = dla * jax.nn.sigmoid(-z) * (1.0 / GLA_GATE_TAU)
        da_ref[...] = _dot(dz, wup_ref[...], 1, 1).astype(da_ref.dtype)
        dwup_ref[...] += _dot(a_ref[...], dz, 0, 0)
        dba_ref[...] += jnp.sum(dz, axis=0, keepdims=True)

    chunk_of = lambda i: nc - 1 - i
    in_specs = _gla_specs(chunk_of) + [
        pl.BlockSpec((None,) + GLA_STATE, lambda i: (chunk_of(i), 0, 0)),
        pl.BlockSpec((None,) + GLA_STATE, lambda i: (jnp.maximum(chunk_of(i) - 1, 0), 0, 0)),
        pl.BlockSpec((CHUNK, GROUP_WIDTH), lambda i: (chunk_of(i), 0))]
    rows = lambda width: pl.BlockSpec((CHUNK, width), lambda i: (chunk_of(i), 0))
    const = lambda r, c: pl.BlockSpec((r, c), lambda i: (0, 0))
    return _pcall(
        body, ride, grid=(nc,), in_specs=in_specs,
        out_specs=[rows(GLA_KW), rows(GLA_KW), rows(GROUP_WIDTH), rows(GROUP_WIDTH), rows(LANES),
                   const(LANES, GLA_KW), const(1, GLA_KW), const(1, GROUP_WIDTH)],
        out_shape=[jax.ShapeDtypeStruct((t, GLA_KW), ACT_DTYPE), jax.ShapeDtypeStruct((t, GLA_KW), ACT_DTYPE),
                   jax.ShapeDtypeStruct((t, GROUP_WIDTH), ACT_DTYPE), jax.ShapeDtypeStruct((t, GROUP_WIDTH), ACT_DTYPE),
                   jax.ShapeDtypeStruct((t, LANES), ACT_DTYPE), jax.ShapeDtypeStruct((LANES, GLA_KW), F32),
                   jax.ShapeDtypeStruct((1, GLA_KW), F32), jax.ShapeDtypeStruct((1, GROUP_WIDTH), F32)],
        scratch_shapes=[pltpu.VMEM(GLA_STATE, F32)],
        semantics=("arbitrary",), name="gla_bwd")(
            pmm, pmm, pmm, pel, pel, w_up, b_a, gnorm_w, states, states, dmix)


CUM_BLOCK = 256


def _fox_gate_fwd(pel, b_f):
    t = pel.shape[0]
    nb = t // CUM_BLOCK

    def body(f_ref, b_ref, cum_ref, cum_t_ref):
        tri = (_iota((CUM_BLOCK, CUM_BLOCK), 1) <= _iota((CUM_BLOCK, CUM_BLOCK), 0)).astype(F32)
        carry = jnp.zeros((1, LANES), F32)
        for blk in range(nb):
            rows = slice(blk * CUM_BLOCK, (blk + 1) * CUM_BLOCK)
            cum = _dot_exact(tri, _log_sigmoid(f_ref[rows, :] + b_ref[...])) + carry
            cum_ref[rows, :] = cum
            cum_t_ref[blk] = cum.T[:ATT_HEADS, :]
            carry = cum[CUM_BLOCK - 1:CUM_BLOCK, :]

    return pl.pallas_call(
        body, grid=(1,),
        in_specs=[pl.BlockSpec((t, LANES), lambda i: (0, 5)), pl.BlockSpec((1, LANES), lambda i: (0, 0))],
        out_specs=[pl.BlockSpec((t, LANES), lambda i: (0, 0)),
                   pl.BlockSpec((nb, ATT_HEADS, CUM_BLOCK), lambda i: (0, 0, 0))],
        out_shape=[jax.ShapeDtypeStruct((t, LANES), F32), jax.ShapeDtypeStruct((nb, ATT_HEADS, CUM_BLOCK), F32)],
        compiler_params=_params("arbitrary"), name="fox_gate_fwd")(pel, b_f)


def _fox_gate_bwd(pel, b_f, dcum_t, dcum_q):
    t = pel.shape[0]
    nb = t // CUM_BLOCK

    def body(f_ref, b_ref, dct_ref, dcq_ref, df_ref, db_ref):
        tri_up = (_iota((CUM_BLOCK, CUM_BLOCK), 1) >= _iota((CUM_BLOCK, CUM_BLOCK), 0)).astype(F32)
        carry = jnp.zeros((1, LANES), F32)
        db = jnp.zeros((1, LANES), F32)
        for blk in reversed(range(nb)):
            rows = slice(blk * CUM_BLOCK, (blk + 1) * CUM_BLOCK)
            query_side = sum(dcq_ref[pair, rows, :] for pair in range(dcq_ref.shape[0]))
            dls = _dot_exact(tri_up, dct_ref[blk].T + query_side) + carry
            carry = dls[0:1, :]
            df = dls * jax.nn.sigmoid(-(f_ref[rows, :] + b_ref[...]))
            df_ref[rows, :] = df.astype(df_ref.dtype)
            db = db + jnp.sum(df, axis=0, keepdims=True)
        db_ref[...] = db

    return pl.pallas_call(
        body, grid=(1,),
        in_specs=[pl.BlockSpec((t, LANES), lambda i: (0, 5)), pl.BlockSpec((1, LANES), lambda i: (0, 0)),
                  pl.BlockSpec((nb, LANES, CUM_BLOCK), lambda i: (0, 0, 0)),
                  pl.BlockSpec((dcum_q.shape[0], t, LANES), lambda i: (0, 0, 0))],
        out_specs=[pl.BlockSpec((t, LANES), lambda i: (0, 0)), pl.BlockSpec((1, LANES), lambda i: (0, 0))],
        out_shape=[jax.ShapeDtypeStruct((t, LANES), ACT_DTYPE), jax.ShapeDtypeStruct((1, LANES), F32)],
        compiler_params=_params("arbitrary"), name="fox_gate_bwd")(pel, b_f, dcum_t, dcum_q)


FOX_Q_BLOCK = 256


assert FOX_Q_BLOCK == CUM_BLOCK
FOX_KEY_STEP = 512


def _fox_scores(q_ref, k_ref, cum_ref, cum_t_ref, h, i):
    hs = slice(h * HEAD_DIM, (h + 1) * HEAD_DIM)
    nb = cum_t_ref.shape[0]
    key_gate = jnp.concatenate([cum_t_ref[kb, h:h + 1, :] for kb in range(nb)], axis=1)
    s = _dot(q_ref[:, hs], k_ref[:, hs], 1, 1) * (HEAD_DIM ** -0.5) + (cum_ref[:, h:h + 1] - key_gate)
    shape = (FOX_Q_BLOCK, nb * FOX_Q_BLOCK)
    return jnp.where(_iota(shape, 1) <= i * FOX_Q_BLOCK + _iota(shape, 0), s, NEG)


def _fox_specs(t):
    bq, nb = FOX_Q_BLOCK, t // FOX_Q_BLOCK
    return [pl.BlockSpec((bq, GROUP_WIDTH), lambda i: (i, 2)), pl.BlockSpec((t, GROUP_WIDTH), lambda i: (0, 3)),
            pl.BlockSpec((t, GROUP_WIDTH), lambda i: (0, 4)), pl.BlockSpec((bq, LANES), lambda i: (i, 0)),
            pl.BlockSpec((nb, ATT_HEADS, bq), lambda i: (0, 0, 0))]


def _fox_fwd(pmm, cum, cum_t, ride=None):
    t = pmm.shape[0]
    bq = FOX_Q_BLOCK

    def body(q_ref, k_ref, v_ref, cum_ref, cum_t_ref, o_ref, lse_ref):
        i = pl.program_id(0)
        lse_ref[...] = jnp.zeros_like(lse_ref)
        for h in range(ATT_HEADS):
            hs = slice(h * HEAD_DIM, (h + 1) * HEAD_DIM)
            s = _fox_scores(q_ref, k_ref, cum_ref, cum_t_ref, h, i)
            m = jnp.max(s, axis=-1, keepdims=True)
            p = jnp.exp(s - m)
            l = jnp.sum(p, axis=-1, keepdims=True)
            o_ref[:, hs] = (_dot(p, v_ref[:, hs]) / l).astype(o_ref.dtype)
            lse_ref[:, h:h + 1] = m + jnp.log(l)

    return _pcall(
        body, ride, grid=(t // bq,), in_specs=_fox_specs(t),
        out_specs=[pl.BlockSpec((bq, GROUP_WIDTH), lambda i: (i, 0)), pl.BlockSpec((bq, LANES), lambda i: (i, 0))],
        out_shape=[jax.ShapeDtypeStruct((t, GROUP_WIDTH), ACT_DTYPE), jax.ShapeDtypeStruct((t, LANES), F32)],
        semantics=("parallel",), name="fox_fwd")(pmm, pmm, pmm, cum, cum_t)


def _fox_bwd(pmm, cum, cum_t, lse, dmix, ride=None):
    t = pmm.shape[0]
    bq, nb = FOX_Q_BLOCK, t // FOX_Q_BLOCK
    pairs, per_pair = ATT_HEADS // 2, LANES // HEAD_DIM
    scale = HEAD_DIM ** -0.5

    def body(q_ref, k_ref, v_ref, cum_ref, cum_t_ref, lse_ref, do_ref, dq_ref, dk_ref, dv_ref, dct_ref, dcq_ref):
        g, i = pl.program_id(0), pl.program_id(1)

        @pl.when(i == 0)
        def _():
            dk_ref[...] = jnp.zeros_like(dk_ref)
            dv_ref[...] = jnp.zeros_like(dv_ref)

        @pl.when((i == 0) & (g == 0))
        def _():
            dct_ref[...] = jnp.zeros_like(dct_ref)

        lane = _iota((1, LANES), 1)

        def run(n):
            causal = _iota((bq, n), 1) <= i * bq + _iota((bq, n), 0)
            dcq = jnp.zeros((bq, LANES), F32)
            for hh in range(per_pair):
                h = per_pair * g + hh
                hs = slice(hh * HEAD_DIM, (hh + 1) * HEAD_DIM)
                pick = (lane == h).astype(F32)
                cq = jnp.sum(cum_ref[...] * pick, axis=1, keepdims=True)
                lse_h = jnp.sum(lse_ref[...] * pick, axis=1, keepdims=True)
                key_gate = jnp.concatenate([cum_t_ref[kb, pl.ds(h, 1), :] for kb in range(n // bq)], axis=1)
                s = _dot(q_ref[:, hs], k_ref[:n, hs], 1, 1) * scale + (cq - key_gate)
                p = jnp.exp(jnp.where(causal, s, NEG) - lse_h)
                do = do_ref[:, hs]
                dp = _dot(do, v_ref[:n, hs], 1, 1)
                ds = p * (dp - jnp.sum(p * dp, axis=-1, keepdims=True))
                dq_ref[:, hs] = (_dot(ds, k_ref[:n, hs]) * scale).astype(dq_ref.dtype)
                dk_ref[:n, hs] += _dot(ds, q_ref[:, hs], 0, 0) * scale
                dv_ref[:n, hs] += _dot(p, do, 0, 0)
                key_side = -jnp.sum(ds, axis=0, keepdims=True)
                for kb in range(n // bq):
                    dct_ref[kb, pl.ds(h, 1), :] += key_side[:, kb * bq:(kb + 1) * bq]
                dcq = dcq + jnp.sum(ds, axis=1, keepdims=True) * pick
            dcq_ref[...] = dcq

        for kx in range(t // FOX_KEY_STEP):
            pl.when(i // (FOX_KEY_STEP // bq) == kx)(functools.partial(run, (kx + 1) * FOX_KEY_STEP))

    cols = lambda first: pl.BlockSpec((bq, LANES), lambda g, i: (i, first + g))
    keys = lambda first: pl.BlockSpec((t, LANES), lambda g, i: (0, first + g))
    per_head = pl.BlockSpec((bq, LANES), lambda g, i: (i, 0))
    fox_q, fox_k, fox_v = (GROUP_WIDTH * n // LANES for n in (2, 3, 4))
    return _pcall(
        body, ride, grid=(pairs, t // bq),
        in_specs=[cols(fox_q), keys(fox_k), keys(fox_v), per_head,
                  pl.BlockSpec((nb, ATT_HEADS, bq), lambda g, i: (0, 0, 0)), per_head, cols(GROUP_WIDTH // LANES)],
        out_specs=[cols(0), keys(0), keys(0), pl.BlockSpec((nb, LANES, bq), lambda g, i: (0, 0, 0)),
                   pl.BlockSpec((None, bq, LANES), lambda g, i: (g, i, 0))],
        out_shape=[jax.ShapeDtypeStruct((t, GROUP_WIDTH), ACT_DTYPE), jax.ShapeDtypeStruct((t, GROUP_WIDTH), F32),
                   jax.ShapeDtypeStruct((t, GROUP_WIDTH), F32), jax.ShapeDtypeStruct((nb, LANES, bq), F32),
                   jax.ShapeDtypeStruct((pairs, t, LANES), F32)],
        semantics=("arbitrary", "arbitrary"), name="fox_bwd")(pmm, pmm, pmm, cum, cum_t, lse, dmix)


CA_Q_BLOCK = 4 * CHUNK
CA_WINDOW = CA_Q_BLOCK + CA_LEFT
CA_BASE = 1024


def _ca_bias_base(rel_bias):
    n = rel_bias.shape[0]
    flat = CA_Q_BLOCK + CA_LEFT - REL_CLIP
    tail = CA_BASE - flat - (2 * REL_CLIP + 1)
    return jnp.concatenate([jnp.broadcast_to(rel_bias[:, 2 * REL_CLIP:], (n, flat)), rel_bias[:, ::-1],
                            jnp.broadcast_to(rel_bias[:, :1], (n, tail))], axis=1)


def _ca_bias_base_grad(dbase):
    flat = CA_Q_BLOCK + CA_LEFT - REL_CLIP
    mid = dbase[:, flat:flat + 2 * REL_CLIP + 1][:, ::-1]
    lo = jnp.sum(dbase[:, flat + 2 * REL_CLIP + 1:], axis=1, keepdims=True)
    hi = jnp.sum(dbase[:, :flat], axis=1, keepdims=True)
    pad = jnp.zeros((dbase.shape[0], 2 * REL_CLIP - 1), F32)
    return mid + jnp.concatenate([lo, pad, hi], axis=1)


def _ca_mask(i):
    r, j = _iota((CA_Q_BLOCK, CA_WINDOW), 0), _iota((CA_Q_BLOCK, CA_WINDOW), 1)
    rc, jc = r // CHUNK, j // CHUNK
    return (jc >= rc) & (jc <= rc + CA_LEFT // CHUNK) & (i * CA_Q_BLOCK + j >= CA_LEFT)


def _ca_fill_bias(i, base_ref, bias_scr):
    @pl.when(i == 0)
    def _():
        for h in range(ATT_HEADS):
            rows = jnp.broadcast_to(base_ref[h:h + 1, :], (CA_Q_BLOCK, CA_BASE))
            bias_scr[h] = pltpu.roll(rows, CA_BASE - CA_Q_BLOCK, 1, stride=1, stride_axis=0)[:, :CA_WINDOW]


def _ca_scores(q_ref, kp_ref, bias_scr, win, h, mask):
    hs = slice(h * HEAD_DIM, (h + 1) * HEAD_DIM)
    s = _dot(q_ref[:, hs], kp_ref[win, hs], 1, 1) * (HEAD_DIM ** -0.5)
    return jnp.where(mask, s + bias_scr[h], NEG)


CA_BIAS_SCRATCH = pltpu.VMEM((ATT_HEADS, CA_Q_BLOCK, CA_WINDOW), F32)


def _ca_fwd(pmm, kp, vp, base, ride=None):
    t = pmm.shape[0]

    def body(q_ref, kp_ref, vp_ref, base_ref, o_ref, lse_ref, bias_scr):
        i = pl.program_id(0)
        _ca_fill_bias(i, base_ref, bias_scr)
        win = pl.ds(pl.multiple_of(i * CA_Q_BLOCK, CA_Q_BLOCK), CA_WINDOW)
        mask = _ca_mask(i)
        lse_ref[...] = jnp.zeros_like(lse_ref)
        for h in range(ATT_HEADS):
            hs = slice(h * HEAD_DIM, (h + 1) * HEAD_DIM)
            s = _ca_scores(q_ref, kp_ref, bias_scr, win, h, mask)
            m = jnp.max(s, axis=-1, keepdims=True)
            p = jnp.exp(s - m)
            l = jnp.sum(p, axis=-1, keepdims=True)
            o_ref[:, hs] = (_dot(p, vp_ref[win, hs]) / l).astype(o_ref.dtype)
            lse_ref[:, h:h + 1] = m + jnp.log(l)

    padded = pl.BlockSpec((t + CA_LEFT, GROUP_WIDTH), lambda i: (0, 0))
    return _pcall(
        body, ride, grid=(t // CA_Q_BLOCK,),
        in_specs=[pl.BlockSpec((CA_Q_BLOCK, GROUP_WIDTH), lambda i: (i, 0)), padded, padded,
                  pl.BlockSpec((ATT_HEADS, CA_BASE), lambda i: (0, 0))],
        out_specs=[pl.BlockSpec((CA_Q_BLOCK, GROUP_WIDTH), lambda i: (i, 0)),
                   pl.BlockSpec((CA_Q_BLOCK, LANES), lambda i: (i, 0))],
        out_shape=[jax.ShapeDtypeStruct((t, GROUP_WIDTH), ACT_DTYPE), jax.ShapeDtypeStruct((t, LANES), F32)],
        scratch_shapes=[CA_BIAS_SCRATCH], semantics=("arbitrary",), name="ca_fwd")(pmm, kp, vp, base)


def _ca_bwd(pmm, kp, vp, base, lse, dmix, ride=None):
    t = pmm.shape[0]
    scale = HEAD_DIM ** -0.5

    def body(q_ref, kp_ref, vp_ref, base_ref, lse_ref, do_ref, dq_ref, dkp_ref, dvp_ref, dbase_ref, bias_scr):
        i = pl.program_id(0)
        _ca_fill_bias(i, base_ref, bias_scr)

        @pl.when(i == 0)
        def _():
            dkp_ref[...] = jnp.zeros_like(dkp_ref)
            dvp_ref[...] = jnp.zeros_like(dvp_ref)
            dbase_ref[...] = jnp.zeros_like(dbase_ref)

        win = pl.ds(pl.multiple_of(i * CA_Q_BLOCK, CA_Q_BLOCK), CA_WINDOW)
        mask = _ca_mask(i)
        flip = (_iota((CA_Q_BLOCK, CA_Q_BLOCK), 0) + _iota((CA_Q_BLOCK, CA_Q_BLOCK), 1) == CA_Q_BLOCK - 1).astype(F32)
        for h in range(ATT_HEADS):
            hs = slice(h * HEAD_DIM, (h + 1) * HEAD_DIM)
            s = _ca_scores(q_ref, kp_ref, bias_scr, win, h, mask)
            p = jnp.exp(s - lse_ref[:, h:h + 1])
            do = do_ref[:, hs]
            dp = _dot(do, vp_ref[win, hs], 1, 1)
            ds = p * (dp - jnp.sum(p * dp, axis=-1, keepdims=True))
            dq_ref[:, hs] = (_dot(ds, kp_ref[win, hs]) * scale).astype(dq_ref.dtype)
            dkp_ref[win, hs] += _dot(ds, q_ref[:, hs], 0, 0) * scale
            dvp_ref[win, hs] += _dot(p, do, 0, 0)
            rev = jnp.concatenate([_dot(flip, ds), jnp.zeros((CA_Q_BLOCK, CA_BASE - CA_WINDOW), F32)], axis=1)
            lined = pltpu.roll(rev, 1, 1, stride=1, stride_axis=0)
            dbase_ref[h:h + 1, :] += jnp.sum(lined, axis=0, keepdims=True)

    padded = pl.BlockSpec((t + CA_LEFT, GROUP_WIDTH), lambda i: (0, 0))
    return _pcall(
        body, ride, grid=(t // CA_Q_BLOCK,),
        in_specs=[pl.BlockSpec((CA_Q_BLOCK, GROUP_WIDTH), lambda i: (i, 0)), padded, padded,
                  pl.BlockSpec((ATT_HEADS, CA_BASE), lambda i: (0, 0)),
                  pl.BlockSpec((CA_Q_BLOCK, LANES), lambda i: (i, 0)),
                  pl.BlockSpec((CA_Q_BLOCK, GROUP_WIDTH), lambda i: (i, 0))],
        out_specs=[pl.BlockSpec((CA_Q_BLOCK, GROUP_WIDTH), lambda i: (i, 0)), padded, padded,
                   pl.BlockSpec((ATT_HEADS, CA_BASE), lambda i: (0, 0))],
        out_shape=[jax.ShapeDtypeStruct((t, GROUP_WIDTH), ACT_DTYPE),
                   jax.ShapeDtypeStruct((t + CA_LEFT, GROUP_WIDTH), F32),
                   jax.ShapeDtypeStruct((t + CA_LEFT, GROUP_WIDTH), F32),
                   jax.ShapeDtypeStruct((ATT_HEADS, CA_BASE), F32)],
        scratch_shapes=[CA_BIAS_SCRATCH], semantics=("arbitrary",), name="ca_bwd")(pmm, kp, vp, base, lse, dmix)


GELU_C = 0.7978845608028654
GELU_A = 0.044715


def _shift_down(v, k, fill, period=None):
    rows = _iota(v.shape, 0)
    rows = rows if period is None else rows & (period - 1)
    return jnp.where(rows >= k, pltpu.roll(v, k, 0), fill)


def _shift_up(v, k, fill, period=None):
    t = v.shape[0]
    rows = _iota(v.shape, 0)
    rows, length = (rows, t) if period is None else (rows & (period - 1), period)
    return jnp.where(rows < length - k, pltpu.roll(v, t - k, 0), fill)


LRU_SCAN_BLOCK = 256


def _linear_scan(a, b, reverse=False):
    shift = _shift_up if reverse else _shift_down
    k = 1
    while k < LRU_SCAN_BLOCK:
        b = a * shift(b, k, 0.0, LRU_SCAN_BLOCK) + b
        a = a * shift(a, k, 1.0, LRU_SCAN_BLOCK)
        k *= 2
    nb = a.shape[0] // LRU_SCAN_BLOCK
    carry = jnp.zeros((1, a.shape[1]), F32)
    out = [None] * nb
    for blk in (reversed(range(nb)) if reverse else range(nb)):
        rows = slice(blk * LRU_SCAN_BLOCK, (blk + 1) * LRU_SCAN_BLOCK)
        h = b[rows] + a[rows] * carry
        out[blk] = h
        carry = h[0:1] if reverse else h[LRU_SCAN_BLOCK - 1:LRU_SCAN_BLOCK]
    return jnp.concatenate(out, axis=0)


def _neg_expm1(y):
    series = -y * (1.0 + y * (0.5 + y * (1.0 / 6.0 + y * (1.0 / 24.0 + y * (1.0 / 120.0)))))
    return jnp.where(y > -0.1, series, 1.0 - jnp.exp(y))


def _lru_forward(x, g_in, cw, cb, wa, ba, wx, bx, lam):
    xs = [_shift_down(x, CONV_WIDTH - 1 - j, 0.0) for j in range(CONV_WIDTH - 1)] + [x]
    xc = cb + sum(cw[j:j + 1, :] * xs[j] for j in range(CONV_WIDTH))
    r = jax.nn.sigmoid(_dot(xc, wa) + ba)
    i = jax.nn.sigmoid(_dot(xc, wx) + bx)
    lsl = _log_sigmoid(lam)
    la = LRU_C * r * lsl
    a = jnp.exp(la)
    s = jnp.sqrt(_neg_expm1(2.0 * la))
    h = _linear_scan(a, s * (i * xc))
    u = GELU_C * (g_in + GELU_A * g_in * g_in * g_in)
    th = jnp.tanh(u)
    gelu = 0.5 * g_in * (1.0 + th)
    return xs, xc, r, i, lsl, a, s, h, th, gelu


def _lru_specs(t):
    col = lambda off: pl.BlockSpec((t, LANES), lambda j: (0, j + off))
    vec = pl.BlockSpec((1, LANES), lambda j: (0, j))
    mat = pl.BlockSpec((None, LANES, LANES), lambda j: (j, 0, 0))
    return [col(0), col(GROUP_WIDTH // LANES), pl.BlockSpec((CONV_WIDTH, LANES), lambda j: (0, j)),
            vec, mat, vec, mat, vec, vec]


def _lru_fwd(pel, conv_w, conv_b, wa, ba, wx, bx, lam, ride=None):
    t = pel.shape[0]

    def body(g_ref, x_ref, cw_ref, cb_ref, wa_ref, ba_ref, wx_ref, bx_ref, lam_ref, o_ref):
        res = _lru_forward(x_ref[...], g_ref[...], cw_ref[...], cb_ref[...], wa_ref[...], ba_ref[...],
                           wx_ref[...], bx_ref[...], lam_ref[...])
        o_ref[...] = (res[7] * res[9]).astype(o_ref.dtype)

    return _pcall(
        body, ride, grid=(GROUP_WIDTH // LANES,), in_specs=_lru_specs(t),
        out_specs=pl.BlockSpec((t, LANES), lambda j: (0, j)),
        out_shape=jax.ShapeDtypeStruct((t, GROUP_WIDTH), ACT_DTYPE),
        semantics=("parallel",), name="lru_fwd")(pel, pel, conv_w, conv_b, wa, ba, wx, bx, lam)


def _lru_bwd(pel, conv_w, conv_b, wa, ba, wx, bx, lam, dmix, ride=None):
    t = pel.shape[0]

    def body(g_ref, x_ref, cw_ref, cb_ref, wa_ref, ba_ref, wx_ref, bx_ref, lam_ref, do_ref,
             dg_ref, dx_ref, dcw_ref, dcb_ref, dwa_ref, dba_ref, dwx_ref, dbx_ref, dlam_ref):
        g_in, cw, lam = g_ref[...], cw_ref[...], lam_ref[...]
        xs, xc, r, i, lsl, a, s, h, th, gelu = _lru_forward(
            x_ref[...], g_in, cw, cb_ref[...], wa_ref[...], ba_ref[...], wx_ref[...], bx_ref[...], lam)
        dout = do_ref[...]
        dgelu = 0.5 * (1.0 + th) + 0.5 * g_in * (1.0 - th * th) * GELU_C * (1.0 + 3.0 * GELU_A * g_in * g_in)
        dg_ref[...] = (dout * h * dgelu).astype(dg_ref.dtype)
        gsum = _linear_scan(_shift_up(a, 1, 0.0), dout * gelu, reverse=True)
        da = gsum * _shift_down(h, 1, 0.0)
        di = gsum * s * xc
        dla = da * a - gsum * (i * xc) * (a * a / s)
        dlam_ref[...] = jnp.sum(dla * (LRU_C * r), axis=0, keepdims=True) * jax.nn.sigmoid(-lam)
        dpr = dla * (LRU_C * lsl) * r * (1.0 - r)
        dpi = di * i * (1.0 - i)
        dxc = gsum * s * i + _dot(dpr, wa_ref[...], 1, 1) + _dot(dpi, wx_ref[...], 1, 1)
        xct = xc.T
        dwa_ref[...] = _dot(xct, dpr)
        dwx_ref[...] = _dot(xct, dpi)
        dba_ref[...] = jnp.sum(dpr, axis=0, keepdims=True)
        dbx_ref[...] = jnp.sum(dpi, axis=0, keepdims=True)
        dcb_ref[...] = jnp.sum(dxc, axis=0, keepdims=True)
        for j in range(CONV_WIDTH):
            dcw_ref[j:j + 1, :] = jnp.sum(dxc * xs[j], axis=0, keepdims=True)
        dx = cw[CONV_WIDTH - 1:CONV_WIDTH, :] * dxc
        for j in range(CONV_WIDTH - 1):
            dx = dx + cw[j:j + 1, :] * _shift_up(dxc, CONV_WIDTH - 1 - j, 0.0)
        dx_ref[...] = dx.astype(dx_ref.dtype)

    col = pl.BlockSpec((t, LANES), lambda j: (0, j))
    vec = pl.BlockSpec((1, LANES), lambda j: (0, j))
    mat = pl.BlockSpec((None, LANES, LANES), lambda j: (j, 0, 0))
    nb = GROUP_WIDTH // LANES
    vshape = jax.ShapeDtypeStruct((1, GROUP_WIDTH), F32)
    mshape = jax.ShapeDtypeStruct((nb, LANES, LANES), F32)
    return _pcall(
        body, ride, grid=(nb,),
        in_specs=_lru_specs(t) + [pl.BlockSpec((t, LANES), lambda j: (0, j + nb))],
        out_specs=[col, col, pl.BlockSpec((CONV_WIDTH, LANES), lambda j: (0, j)), vec, mat, vec, mat, vec, vec],
        out_shape=[jax.ShapeDtypeStruct((t, GROUP_WIDTH), ACT_DTYPE), jax.ShapeDtypeStruct((t, GROUP_WIDTH), ACT_DTYPE),
                   jax.ShapeDtypeStruct((CONV_WIDTH, GROUP_WIDTH), F32), vshape, mshape, vshape, mshape, vshape, vshape],
        semantics=("parallel",), name="lru_bwd")(
            pel, pel, conv_w, conv_b, wa, ba, wx, bx, lam, dmix)


def _block_diag_pairs(w):
    z = jnp.zeros((LRU_BLOCK_DIM, LRU_BLOCK_DIM), w.dtype)
    return jnp.stack([jnp.block([[w[2 * j], z], [z, w[2 * j + 1]]]) for j in range(w.shape[0] // 2)])


def _block_diag_pairs_grad(dw):
    b = LRU_BLOCK_DIM
    return jnp.stack([dw[n // 2, (n % 2) * b:(n % 2 + 1) * b, (n % 2) * b:(n % 2 + 1) * b] for n in range(2 * dw.shape[0])])


def _row_tile(r):
    return ROW_TILE if r % ROW_TILE == 0 else r


def _pair_sum(g, got, place, name):
    _, r, c = g.shape
    tile = r

    def body(place_ref, a_ref, b_ref, o_ref):
        o_ref[...] = (a_ref[...].astype(F32) + b_ref[...].astype(F32)).astype(o_ref.dtype)

    blk = pl.BlockSpec((1, tile, c), lambda k, i, place_ref: (k, i, 0))
    return pl.pallas_call(
        body,
        grid_spec=pltpu.PrefetchScalarGridSpec(
            num_scalar_prefetch=1, grid=(N_CHIPS, r // tile),
            in_specs=[pl.BlockSpec((1, tile, c), lambda k, i, place_ref: (2 * k + place_ref[0], i, 0)), blk],
            out_specs=blk),
        out_shape=jax.ShapeDtypeStruct(got.shape, got.dtype),
        compiler_params=_params("parallel", "parallel"), name=name)(place, g, got)


def _adamw_update(g, w_ref, m_ref, v_ref, g_ref, d_ref, nm_ref, nv_ref):
    nm = ADAM_B1 * m_ref[...] + (1.0 - ADAM_B1) * g
    nv = ADAM_B2 * v_ref[...] + (1.0 - ADAM_B2) * jnp.square(g)
    m_hat = nm / (1.0 - ADAM_B1 ** ADAM_STEP)
    v_hat = nv / (1.0 - ADAM_B2 ** ADAM_STEP)
    g_ref[...] = g
    d_ref[...] = -ADAM_LR * (m_hat / (jnp.sqrt(v_hat) + ADAM_EPS) + ADAM_WD * w_ref[...])
    nm_ref[...] = nm
    nv_ref[...] = nv


def _adamw_sharded(parts, w, m, v, place, name, ride=None):
    n_layers, r, c = w.shape
    tile = _row_tile(r)
    nb = r // tile
    counts = [1 + len(recvs) for _, recvs in parts]

    def body(place_ref, *refs):
        layer = pl.program_id(0)
        g, at = None, 0
        for l in range(n_layers):
            g_l = refs[at][0].astype(F32)
            for r_ref in refs[at + 1:at + counts[l]]:
                for k in range(r_ref.shape[0]):
                    g_l = g_l + r_ref[k].astype(F32)
            g = g_l if g is None else jnp.where(layer == l, g_l, g)
            at += counts[l]
        _adamw_update(g, *refs[at:])

    def part_specs(l, recvs):
        rows = lambda q, i: jnp.where(q < l, 0, jnp.where(q > l, nb - 1, i))
        return ([pl.BlockSpec((1, tile, c), lambda q, i, place_ref: (place_ref[1], rows(q, i), 0))] +
                [pl.BlockSpec((a.shape[0], tile, c), lambda q, i, place_ref: (0, rows(q, i), 0)) for a in recvs])

    in_specs, args = [], []
    for l, (s, recvs) in enumerate(parts):
        in_specs += part_specs(l, recvs)
        args += [s, *recvs]
    blk = pl.BlockSpec((None, tile, c), lambda q, i, place_ref: (q, i, 0))
    out = jax.ShapeDtypeStruct((n_layers, r, c), F32)
    return _pcall(body, ride, grid=(n_layers, nb), in_specs=in_specs + [blk, blk, blk], out_specs=[blk, blk, blk, blk],
                  out_shape=[out, out, out, out], semantics=("arbitrary", "arbitrary"), name=name, prefetch=True)(
                      place, *args, w, m, v)


def _adamw_small(repl_parts, vec_parts, w, m, v, place):
    n_r, n = len(repl_parts), len(w)
    shapes = [a.shape for a in w]

    def body(place_ref, *refs):
        parts, rest = refs[:n], refs[n:]
        for k in range(n):
            take = (lambda p: parts[k][p]) if k < n_r else (lambda p: parts[k][p, 0])
            g = take(0)
            for p in range(1, N_DEV):
                g = g + take(p)
            _adamw_update(g, rest[k], rest[n + k], rest[2 * n + k], *rest[3 * n + 4 * k:3 * n + 4 * k + 4])

    def whole(shape):
        return pl.BlockSpec(shape, lambda i, place_ref: (0,) * len(shape))

    def mine(shard):
        return pl.BlockSpec((N_DEV, 1) + shard, lambda i, place_ref: (0, place_ref[2]) + (0,) * len(shard))

    in_specs = [whole(a.shape) for a in repl_parts] + [mine(s) for s in shapes[n_r:]] + [whole(s) for s in shapes] * 3
    outs = pl.pallas_call(
        body,
        grid_spec=pltpu.PrefetchScalarGridSpec(
            num_scalar_prefetch=1, grid=(1,), in_specs=in_specs,
            out_specs=[whole(s) for s in shapes for _ in range(4)]),
        out_shape=[jax.ShapeDtypeStruct(s, F32) for s in shapes for _ in range(4)],
        compiler_params=_params("arbitrary"), name="adamw_small")(place, *repl_parts, *vec_parts, *w, *m, *v)
    return [outs[4 * k:4 * k + 4] for k in range(n)]


SHARDED = {"norm_w": 2, "w_in_even": 2, "gla_w_a_up": 2, "w_out_even": 1, "w_in_odd": 2, "conv_w": 2, "conv_b": 1,
           "lru_b_a": 1, "lru_b_x": 1, "lru_lambda": 1, "w_out_odd": 1, "w_mlp_up": 2, "w_mlp_down": 1}
REPLICATED = ["gla_b_a", "gla_norm_w", "fox_b_f", "rel_bias", "lru_w_a", "lru_w_x"]
WEIGHTS = ["norm_w", "w_in_even", "gla_w_a_up", "gla_b_a", "gla_norm_w", "fox_b_f", "w_out_even", "w_in_odd",
           "rel_bias", "conv_w", "conv_b", "lru_w_a", "lru_b_a", "lru_w_x", "lru_b_x", "lru_lambda", "w_out_odd",
           "w_mlp_up", "w_mlp_down"]
MATRICES = ("w_in_even", "w_out_even", "w_in_odd", "w_out_odd", "w_mlp_up", "w_mlp_down")
TRANSPOSED = ("w_in_even", "w_in_odd")
VECTORS = tuple(n for n in SHARDED if n not in MATRICES)
MATRIX_BLOCKS = (("w_in_even", 0), ("w_out_even", 0), ("w_in_odd", 0), ("w_out_odd", 0),
                 ("w_mlp_up", 0), ("w_mlp_up", 1), ("w_mlp_down", 0), ("w_mlp_down", 1))


def _join_shards(blocks, axis):
    moved = jnp.moveaxis(blocks, 0, axis)
    shape = moved.shape
    return moved.reshape(shape[:axis] + (shape[axis] * shape[axis + 1],) + shape[axis + 2:])


def _split_shards(full, axis):
    shape = full.shape
    cut = full.reshape(shape[:axis] + (N_DEV, shape[axis] // N_DEV) + shape[axis + 1:])
    return jnp.moveaxis(cut, axis, 0)


EVEN_SPLITS = (0, 256, 512, 1024, 1536, 1552, 2064, 2576, 3088, 3096)


def _even_in_split(wt):
    c = [wt[EVEN_SPLITS[k]:EVEN_SPLITS[k + 1]] for k in range(9)]
    gq, gk, gv, gr, ga, fq, fk, fv, ff = c
    padrows = lambda a: jnp.pad(a, ((0, LANES - a.shape[0]), (0, 0)))
    return jnp.concatenate([gq, gk, gv, fq, fk, fv], axis=0), jnp.concatenate([gr, padrows(ga), padrows(ff)], axis=0)


def _even_in_merge(dmm, dele):
    return jnp.concatenate([dmm[:1024], dele[:512], dele[512:512 + GLA_RANK], dmm[1024:2560],
                            dele[640:640 + ATT_HEADS]], axis=0)


def _forward_backward(x, target, shard, vec_shard, w, place):
    w = dict(w)
    g, dnorm, sums, recv = {}, {}, {}, {}
    nrm = lambda l, k: w["norm_w"][l, k][None, :]
    gather = lambda *keys: _gather_plan([shard[k] for k in keys])
    blocks = lambda r, c: (N_DEV, r // N_DEV, c)

    def pair_sum(key):
        sums[key] = _pair_sum(g[key], got[key], place, f"rs_pair_sum_{key[0]}_{key[1]}")

    got = {}

    def mlp_fwd(xin, layer, ride_up, ride_down):
        up = _mm(xin, w["w_mlp_up"][layer], out_dtype=ACT_DTYPE, tm=TM_FWD, tn=D_FF // N_DEV, b_blocked=True,
                 a_norm=nrm(layer, 2), name=f"mlp_up_{layer}", ride=ride_up)
        (u, h), rode_up = up if ride_up is not None else (up, None)
        down = _mm(u, w["w_mlp_down"][layer], out_dtype=F32, tm=TM_DX // 2, tn=D_MODEL, a_sqrelu=True,
                   res_norm=(xin, nrm(layer, 3)), name=f"mlp_down_{layer}", ride=ride_down)
        (yv, xout), rode_down = down if ride_down is not None else (down, None)
        return xout, (xin, h, u, yv), rode_up, rode_down

    def mlp_bwd(dxout, saved, layer, ride):
        xin, h, u, yv = saved
        k_up, k_down = ("w_mlp_up", layer), ("w_mlp_down", layer)
        res = _mm(dxout, w["w_mlp_down"][layer], nt=True, out_dtype=ACT_DTYPE, tm=TM_DX, tn=TN, drelu_of=u,
                  a_norm_bwd=(yv, nrm(layer, 3)), name=f"mlp_down_dx_{layer}", ride=ride)
        (du, dy, dnorm[(layer, 3)]), rode = res if ride is not None else (res, None)
        g[k_down] = _mm(u, dy, ta=True, out_dtype=WIRE_DTYPE, tm=TM_DW, tn=TN, a_sqrelu=True,
                        name=f"mlp_down_dw_{layer}").reshape(blocks(D_FF, D_MODEL))
        g[k_up] = _mm(h, du, ta=True, out_dtype=WIRE_DTYPE, tm=TM_DW, tn=D_FF // N_DEV, out_blocked=True,
                      name=f"mlp_up_dw_{layer}")
        w_up = jnp.moveaxis(w["w_mlp_up"][layer], 0, 1).reshape(D_MODEL, D_FF)
        (dxin, dnorm[(layer, 2)]), (got[k_down], got[k_up]) = _mm(
            du, w_up, nt=True, out_dtype=F32, tm=TM_DX // 2, tn=D_MODEL, norm_bwd=(xin, nrm(layer, 2), dxout),
            name=f"mlp_up_dx_{layer}", ride=_sibling_plan([g[k_down], g[k_up]]))
        pair_sum(k_down)
        pair_sum(k_up)
        return dxin, rode

    first = _run_plan(_gather_plan([shard[("w_in_even", 0)], vec_shard["norm_w"]]), "weights_all_gather_first")
    w["w_in_even"] = first[0].reshape(-1, D_MODEL)
    w["norm_w"] = _join_shards(first[1], SHARDED["norm_w"])
    w["w_mlp_up"], w["w_mlp_down"] = [None] * DEPTH, [None] * DEPTH

    wmm_e, wel_e = _even_in_split(w["w_in_even"])
    later_vectors = [n for n in VECTORS if n != "norm_w"]
    (pmm0, h0), (w_out_even, *later) = _mm(
        x, wmm_e, nt=True, out_dtype=ACT_DTYPE, tm=TM_FWD, tn=TN, a_norm=nrm(0, 0), name="in_even_mm",
        ride=_gather_plan([shard[("w_out_even", 0)]] + [vec_shard[n] for n in later_vectors]))
    for n, b in zip(later_vectors, later):
        w[n] = _join_shards(b, SHARDED[n])
    w_up_pad = jnp.pad(w["gla_w_a_up"][0], ((0, LANES - GLA_RANK), (0, 0)))
    b_f_pad = jnp.pad(w["fox_b_f"], ((0, 0), (0, LANES - ATT_HEADS)))
    pel0 = _mm(h0, wel_e, nt=True, out_dtype=F32, tm=TM_FWD, tn=768, name="in_even_el")
    (out_a, states), (w["w_mlp_up"][0],) = _gla_fwd(pmm0, pel0, w_up_pad, w["gla_b_a"], w["gla_norm_w"],
                                                    ride=gather(("w_mlp_up", 0)))
    cum, cum_t = _fox_gate_fwd(pel0, b_f_pad)
    (out_b, lse_b), (w_mlp_down0, w_in_odd) = _fox_fwd(pmm0, cum, cum_t,
                                                       ride=gather(("w_mlp_down", 0), ("w_in_odd", 0)))
    w["w_out_even"] = w_out_even.reshape(D_MODEL, D_MODEL)
    w["w_mlp_down"][0] = w_mlp_down0.reshape(D_FF, D_MODEL)
    mix_in0 = jnp.concatenate([out_a, out_b], axis=1)
    mix0, x1 = _mm(mix_in0, w["w_out_even"], out_dtype=F32, tm=TM_DX, tn=D_MODEL, res_norm=(x, nrm(0, 1)),
                   name="out_even")
    x2, mlp0, _, (w["w_mlp_up"][1],) = mlp_fwd(x1, 0, None, gather(("w_mlp_up", 1)))
    w["w_in_odd"] = w_in_odd.reshape(-1, D_MODEL)

    w_in_o = w["w_in_odd"]
    n_mm_o = 3 * GROUP_WIDTH
    wa_bd, wx_bd = _block_diag_pairs(w["lru_w_a"][0]), _block_diag_pairs(w["lru_w_x"][0])
    base = _ca_bias_base(w["rel_bias"][0])
    pmm1, h1 = _mm(x2, w_in_o[:n_mm_o], nt=True, out_dtype=ACT_DTYPE, tm=TM_FWD, tn=TN, a_norm=nrm(1, 0),
                   name="in_odd_mm")
    pel1 = _mm(h1, w_in_o[n_mm_o:], nt=True, out_dtype=F32, tm=TM_FWD, tn=TN, name="in_odd_el")
    kp = jnp.pad(pmm1[:, GROUP_WIDTH:2 * GROUP_WIDTH], ((CA_LEFT, 0), (0, 0)))
    vp = jnp.pad(pmm1[:, 2 * GROUP_WIDTH:], ((CA_LEFT, 0), (0, 0)))
    (out_c, lse_c), (w_mlp_down1,) = _ca_fwd(pmm1, kp, vp, base, ride=gather(("w_mlp_down", 1)))
    w["w_mlp_down"][1] = w_mlp_down1.reshape(D_FF, D_MODEL)
    lru_args = (pel1, w["conv_w"][0], w["conv_b"], wa_bd, w["lru_b_a"], wx_bd, w["lru_b_x"], w["lru_lambda"])
    out_d, (w_out_odd,) = _lru_fwd(*lru_args, ride=gather(("w_out_odd", 0)))
    w["w_out_odd"] = w_out_odd.reshape(D_MODEL, D_MODEL)
    mix_in1 = jnp.concatenate([out_c, out_d], axis=1)
    mix1, x3 = _mm(mix_in1, w["w_out_odd"], out_dtype=F32, tm=TM_DX, tn=D_MODEL, res_norm=(x2, nrm(1, 1)),
                   name="out_odd")
    x4, mlp1, _, _ = mlp_fwd(x3, 1, None, None)

    loss, dx4 = _loss_fwd_bwd(x4, target)

    k_oo, k_io, k_oe, k_ie = ("w_out_odd", 0), ("w_in_odd", 0), ("w_out_even", 0), ("w_in_even", 0)
    mlp_keys = lambda l: [("w_mlp_down", l), ("w_mlp_up", l)]
    dx3, _ = mlp_bwd(dx4, mlp1, 1, None)
    dmix_in1, dmix1, dnorm[(1, 1)] = _mm(dx3, w["w_out_odd"], nt=True, out_dtype=F32, tm=TM_DX, tn=TN,
                                         a_norm_bwd=(mix1, nrm(1, 1)), name="out_odd_dx")
    g[k_oo] = _mm(mix_in1, dmix1, ta=True, out_dtype=WIRE_DTYPE, tm=TM_DW, tn=TN, name="out_odd_dw").reshape(
        blocks(D_MODEL, D_MODEL))
    (dq_c, dkp, dvp, dbase), rode = _ca_bwd(
        pmm1, kp, vp, base, lse_c, dmix_in1,
        ride=_join_plans(_chip_plan([sums[k] for k in mlp_keys(1)]), _sibling_plan([g[k_oo]])))
    recv.update(zip(mlp_keys(1), rode[:2]))
    got[k_oo] = rode[2]
    pair_sum(k_oo)
    (dgate, dxin, g_conv_w, g_conv_b, dwa_bd, g_lru_b_a, dwx_bd, g_lru_b_x, g_lru_lambda), (recv[k_oo],) = _lru_bwd(
        *lru_args, dmix_in1, ride=_chip_plan([sums[k_oo]]))
    dp1 = jnp.concatenate([dq_c, dkp[CA_LEFT:].astype(ACT_DTYPE), dvp[CA_LEFT:].astype(ACT_DTYPE), dgate, dxin], axis=1)
    g[k_io] = _mm(dp1, h1, ta=True, out_dtype=WIRE_DTYPE, tm=dp1.shape[1] // 2, tn=TN, name="in_odd_dw").reshape(
        blocks(dp1.shape[1], D_MODEL))
    (dx2, dnorm[(1, 0)]), (got[k_io],) = _mm(dp1, w_in_o, out_dtype=F32, tm=TM_DX // 2, tn=D_MODEL,
                                             norm_bwd=(x2, nrm(1, 0), dx3), name="in_odd_dx",
                                             ride=_sibling_plan([g[k_io]]))
    pair_sum(k_io)
    g["rel_bias"] = _ca_bias_base_grad(dbase)[None]
    g["conv_w"], g["conv_b"] = g_conv_w[None], g_conv_b
    g["lru_w_a"], g["lru_w_x"] = _block_diag_pairs_grad(dwa_bd)[None], _block_diag_pairs_grad(dwx_bd)[None]
    g["lru_b_a"], g["lru_b_x"], g["lru_lambda"] = g_lru_b_a, g_lru_b_x, g_lru_lambda

    dx1, (recv[k_io],) = mlp_bwd(dx2, mlp0, 0, _chip_plan([sums[k_io]]))
    dmix_in0, dmix0, dnorm[(0, 1)] = _mm(dx1, w["w_out_even"], nt=True, out_dtype=F32, tm=TM_DX, tn=TN,
                                         a_norm_bwd=(mix0, nrm(0, 1)), name="out_even_dx")
    g[k_oe] = _mm(mix_in0, dmix0, ta=True, out_dtype=WIRE_DTYPE, tm=TM_DW, tn=TN, name="out_even_dw").reshape(
        blocks(D_MODEL, D_MODEL))
    k_md0, k_mu0 = mlp_keys(0)
    (dq_a, dk_a, dv_a, dr_a, da_a, dw_up_pad, g_gla_b_a, g_gla_norm_w), (got[k_oe],) = _gla_bwd(
        pmm0, pel0, w_up_pad, w["gla_b_a"], w["gla_norm_w"], states, dmix_in0, ride=_sibling_plan([g[k_oe]]))
    pair_sum(k_oe)
    (dq_b, dk_b, dv_b, dcum_t, dcum_q), (recv[k_md0], recv[k_mu0], recv[k_oe]) = _fox_bwd(
        pmm0, cum, cum_t, lse_b, dmix_in0, ride=_chip_plan([sums[k_md0], sums[k_mu0], sums[k_oe]]))
    df_b, db_f = _fox_gate_bwd(pel0, b_f_pad, dcum_t, dcum_q)
    g["gla_w_a_up"] = dw_up_pad[:GLA_RANK][None]
    g["gla_b_a"], g["gla_norm_w"], g["fox_b_f"] = g_gla_b_a, g_gla_norm_w, db_f[:, :ATT_HEADS]
    dp0 = jnp.concatenate([dq_a, dk_a, dv_a, dq_b, dk_b.astype(ACT_DTYPE), dv_b.astype(ACT_DTYPE), dr_a, da_a, df_b],
                          axis=1)
    w_perm = jnp.concatenate([wmm_e, wel_e], axis=0)
    n_mm_e = wmm_e.shape[0]
    dw_perm = _mm(dp0, h0, ta=True, out_dtype=WIRE_DTYPE, tm=dp0.shape[1] // 2, tn=TN, name="in_even_dw")
    dw_even = _even_in_merge(dw_perm[:n_mm_e], dw_perm[n_mm_e:])
    g[k_ie] = dw_even.reshape(blocks(dw_even.shape[0], D_MODEL))
    dh0, (got[k_ie], *repl_parts) = _mm(
        dp0, w_perm, out_dtype=F32, tm=TM_DX, tn=TN, name="in_even_dx",
        ride=_join_plans(_sibling_plan([g[k_ie]]), _gather_plan([g[n] for n in REPLICATED])))
    pair_sum(k_ie)
    dx0, dnorm[(0, 0)] = _norm_bwd(dh0, x, nrm(0, 0), out_dtype=F32, add=dx1, name="norm_in_bwd_0")

    g["norm_w"] = jnp.stack([jnp.concatenate([dnorm[(l, k)] for k in range(4)], axis=0) for l in range(DEPTH)])
    recv[k_ie], losses, *vec_parts = _run_plan(
        _join_plans(_chip_plan([sums[k_ie]]),
                    _gather_plan([loss] + [_split_shards(g[n], SHARDED[n]) for n in VECTORS])), "last_exchanges")
    return losses, dx0, sums, recv, repl_parts, vec_parts


def kernel(x, norm_w, w_in_even, gla_w_a_up, gla_b_a, gla_norm_w, fox_b_f, w_out_even, w_in_odd, rel_bias, conv_w, conv_b, lru_w_a, lru_b_a, lru_w_x, lru_b_x, lru_lambda, w_out_odd, w_mlp_up, w_mlp_down, loss_target, m_norm_w, m_w_in_even, m_gla_w_a_up, m_gla_b_a, m_gla_norm_w, m_fox_b_f, m_w_out_even, m_w_in_odd, m_rel_bias, m_conv_w, m_conv_b, m_lru_w_a, m_lru_b_a, m_lru_w_x, m_lru_b_x, m_lru_lambda, m_w_out_odd, m_w_mlp_up, m_w_mlp_down, v_norm_w, v_w_in_even, v_gla_w_a_up, v_gla_b_a, v_gla_norm_w, v_fox_b_f, v_w_out_even, v_w_in_odd, v_rel_bias, v_conv_w, v_conv_b, v_lru_w_a, v_lru_b_a, v_lru_w_x, v_lru_b_x, v_lru_lambda, v_w_out_odd, v_w_mlp_up, v_w_mlp_down):
    wts = dict(zip(WEIGHTS, (norm_w, w_in_even, gla_w_a_up, gla_b_a, gla_norm_w, fox_b_f, w_out_even, w_in_odd, rel_bias,
                             conv_w, conv_b, lru_w_a, lru_b_a, lru_w_x, lru_b_x, lru_lambda, w_out_odd, w_mlp_up,
                             w_mlp_down)))
    mom = dict(zip(WEIGHTS, (m_norm_w, m_w_in_even, m_gla_w_a_up, m_gla_b_a, m_gla_norm_w, m_fox_b_f, m_w_out_even,
                             m_w_in_odd, m_rel_bias, m_conv_w, m_conv_b, m_lru_w_a, m_lru_b_a, m_lru_w_x, m_lru_b_x,
                             m_lru_lambda, m_w_out_odd, m_w_mlp_up, m_w_mlp_down)))
    var = dict(zip(WEIGHTS, (v_norm_w, v_w_in_even, v_gla_w_a_up, v_gla_b_a, v_gla_norm_w, v_fox_b_f, v_w_out_even,
                             v_w_in_odd, v_rel_bias, v_conv_w, v_conv_b, v_lru_w_a, v_lru_b_a, v_lru_w_x, v_lru_b_x,
                             v_lru_lambda, v_w_out_odd, v_w_mlp_up, v_w_mlp_down)))
    ax, ay, ac = lax.axis_index("x"), lax.axis_index("y"), lax.axis_index("c")
    place = jnp.stack([ac, 2 * ax + ay, 4 * ax + 2 * ay + ac]).astype(jnp.int32)

    shard = {(n, l): (wts[n][l].T if n in TRANSPOSED else wts[n][l]).astype(WIRE_DTYPE) for n, l in MATRIX_BLOCKS}
    losses, dx, sums, recv, repl_parts, vec_parts = _forward_backward(
        x[0], loss_target[0], shard, {n: wts[n] for n in VECTORS}, {n: wts[n] for n in REPLICATED}, place)
    loss = jnp.sum(losses[:, 0, 0])

    view = lambda n, a: jnp.swapaxes(a, 1, 2) if n in TRANSPOSED else a
    upd = {n: [view(n, o) for o in _adamw_sharded(
        [(sums[(n, l)], [recv[(n, l)]]) for l in range(wts[n].shape[0])], view(n, wts[n]), view(n, mom[n]),
        view(n, var[n]), place, f"adamw_{n}")] for n in MATRICES}
    small = REPLICATED + list(VECTORS)
    upd.update(zip(small, _adamw_small(repl_parts, vec_parts, [wts[n] for n in small], [mom[n] for n in small],
                                       [var[n] for n in small], place)))
    return (loss, dx[None], *[upd[n][kind] for kind in range(4) for n in WEIGHTS])
```

```python
import functools
from typing import Callable, NamedTuple, Optional

import jax
import jax.numpy as jnp
from jax import lax
from jax.experimental import pallas as pl
from jax.experimental.pallas import tpu as pltpu

F32 = jnp.float32
MXU_DTYPE = jnp.bfloat16
ACT_DTYPE = jnp.bfloat16
WIRE_DTYPE = jnp.bfloat16

V7X_VMEM_BYTES = 64 * 1024 * 1024
VMEM_LIMIT = (V7X_VMEM_BYTES * 7) // 8
LANES = 128

D_MODEL = 1024
DEPTH = 2
CHUNK = 64
GROUP_WIDTH = D_MODEL // 2
D_FF = 4 * D_MODEL
NORM_EPS = 1e-6
GLA_HEADS = 4
GLA_DV = GROUP_WIDTH // GLA_HEADS
GLA_DK = GLA_DV // 2
GLA_KW = GLA_HEADS * GLA_DK
GLA_RANK = 16
GLA_GATE_TAU = 16.0
HEAD_DIM = 64
ATT_HEADS = GROUP_WIDTH // HEAD_DIM
CA_LEFT = 8 * CHUNK
REL_CLIP = 128
LRU_BLOCK_DIM = 64
CONV_WIDTH = 4
LRU_C = 8.0
N_DEV = 8

ADAM_LR = 0.001
ADAM_B1 = 0.9
ADAM_B2 = 0.999
ADAM_EPS = 1e-08
ADAM_WD = 0.01
ADAM_STEP = 10

NEG = float(jnp.finfo(jnp.float32).min)
MESH = pl.DeviceIdType.MESH


def _params(*sem):
    return pltpu.CompilerParams(dimension_semantics=sem, vmem_limit_bytes=VMEM_LIMIT)


def _dot(a, b, ca=1, cb=0):
    return lax.dot_general(a.astype(MXU_DTYPE), b.astype(MXU_DTYPE), (((ca,), (cb,)), ((), ())),
                           preferred_element_type=F32)


def _dot_exact(a, b):
    return lax.dot_general(a, b, (((1,), (0,)), ((), ())), precision=lax.Precision.HIGHEST,
                           preferred_element_type=F32)


def _log_sigmoid(x):
    return jnp.minimum(x, 0.0) - jnp.log1p(jnp.exp(-jnp.abs(x)))


def _iota(shape, axis):
    return lax.broadcasted_iota(jnp.int32, shape, axis)


ANY = pl.BlockSpec(memory_space=pl.ANY)
N_CHIPS = 4


class _Plan(NamedTuple):
    ins: list
    outs: list
    sems: list
    start: Callable
    finish: Callable
    relay: Optional[Callable] = None


def _place():
    x, y, c = lax.axis_index("x"), lax.axis_index("y"), lax.axis_index("c")
    return x, y, c, [(1 - x, y), (x, 1 - y), (1 - x, 1 - y)]


def _gather_plan(xs):
    n = len(xs)

    def parts(x_refs, out_refs, sems):
        send_sems, recv_sems, local_sems = sems
        x, y, c, chips = _place()
        me, sibling = (x, y, c), (x, y, 1 - c)

        def rows(a, px, py, pc):
            return out_refs[a].at[4 * px + 2 * py + pc]

        def copy(a, k, block, to, src=None):
            return pltpu.make_async_remote_copy(
                src_ref=rows(a, *block) if src is None else src, dst_ref=rows(a, *block),
                send_sem=send_sems.at[7 * a + k], recv_sem=recv_sems.at[7 * a + k], device_id=to, device_id_type=MESH)

        def own():
            mine = [pltpu.make_async_copy(x_refs[a], rows(a, *me), local_sems.at[a]) for a in range(n)]
            first = []
            for a in range(n):
                first.append(copy(a, 0, me, sibling, src=x_refs[a]))
                first += [copy(a, 1 + j, me, (*chip, c), src=x_refs[a]) for j, chip in enumerate(chips)]
            return mine, first

        return c, me, sibling, chips, copy, own

    def start(x_refs, out_refs, sems):
        mine, first = parts(x_refs, out_refs, sems)[-1]()
        for cp in first + mine:
            cp.start()

    def relay(x_refs, out_refs, sems):
        c, me, sibling, chips, copy, _ = parts(x_refs, out_refs, sems)
        for j, chip in enumerate(chips):
            for a in range(n):
                copy(a, 1 + j, (*chip, c), me).wait_recv()
                copy(a, 4 + j, (*chip, c), sibling).start()

    def finish(x_refs, out_refs, sems):
        c, me, sibling, chips, copy, own = parts(x_refs, out_refs, sems)
        mine, first = own()
        for a in range(n):
            copy(a, 0, sibling, me).wait_recv()
            for j, chip in enumerate(chips):
                copy(a, 4 + j, (*chip, 1 - c), me).wait_recv()
        for cp in first + [copy(a, 4 + j, (*chip, c), sibling) for j, chip in enumerate(chips) for a in range(n)]:
            cp.wait_send()
        for cp in mine:
            cp.wait()

    return _Plan(list(xs), [jax.ShapeDtypeStruct((N_DEV,) + x.shape, x.dtype) for x in xs],
                 [pltpu.SemaphoreType.DMA((7 * n,)), pltpu.SemaphoreType.DMA((7 * n,)), pltpu.SemaphoreType.DMA((n,))],
                 start, finish, relay)


def _exchange_plan(copies_of, ins, outs, per_array):
    n = len(ins)

    def start(in_refs, out_refs, sems):
        for cp in copies_of(in_refs, out_refs, sems):
            cp.start()

    def finish(in_refs, out_refs, sems):
        copies = copies_of(in_refs, out_refs, sems)
        for cp in copies:
            cp.wait_recv()
        for cp in copies:
            cp.wait_send()

    return _Plan(list(ins), outs, [pltpu.SemaphoreType.DMA((per_array * n,)), pltpu.SemaphoreType.DMA((per_array * n,))],
                 start, finish)


def _sibling_plan(gs):
    def copies_of(g_refs, got_refs, sems):
        x, y, c, _ = _place()
        return [pltpu.make_async_remote_copy(
            src_ref=g_refs[a].at[2 * k + (1 - c)], dst_ref=got_refs[a].at[k], send_sem=sems[0].at[N_CHIPS * a + k],
            recv_sem=sems[1].at[N_CHIPS * a + k], device_id=(x, y, 1 - c), device_id_type=MESH)
            for a in range(len(gs)) for k in range(N_CHIPS)]

    return _exchange_plan(copies_of, gs, [jax.ShapeDtypeStruct((N_CHIPS,) + g.shape[1:], g.dtype) for g in gs], N_CHIPS)


def _chip_plan(ss, relations=(0, 1, 2)):
    n_rel = len(relations)

    def copies_of(s_refs, out_refs, sems):
        x, y, c, chips = _place()
        return [pltpu.make_async_remote_copy(
            src_ref=s_refs[a].at[2 * chips[j][0] + chips[j][1]], dst_ref=out_refs[a].at[slot],
            send_sem=sems[0].at[n_rel * a + slot], recv_sem=sems[1].at[n_rel * a + slot],
            device_id=(*chips[j], c), device_id_type=MESH)
            for a in range(len(ss)) for slot, j in enumerate(relations)]

    return _exchange_plan(copies_of, ss, [jax.ShapeDtypeStruct((n_rel,) + s.shape[1:], s.dtype) for s in ss], n_rel)


def _join_plans(*plans):
    def cut(refs, counts):
        at = 0
        for n in counts:
            yield refs[at:at + n]
            at += n

    def each(in_refs, out_refs, sems):
        return zip(plans, cut(in_refs, [len(p.ins) for p in plans]), cut(out_refs, [len(p.outs) for p in plans]),
                   cut(sems, [len(p.sems) for p in plans]))

    def start(*refs):
        for p, i, o, s in each(*refs):
            p.start(i, o, s)

    def relay(*refs):
        for p, i, o, s in each(*refs):
            if p.relay is not None:
                p.relay(i, o, s)

    def finish(*refs):
        for p, i, o, s in each(*refs):
            p.finish(i, o, s)

    return _Plan([a for p in plans for a in p.ins], [a for p in plans for a in p.outs],
                 [a for p in plans for a in p.sems], start, finish, relay)


def _run_plan(plan, name):
    n_in, n_out = len(plan.ins), len(plan.outs)

    def body(*refs):
        args = refs[:n_in], refs[n_in:n_in + n_out], refs[n_in + n_out:]
        plan.start(*args)
        if plan.relay is not None:
            plan.relay(*args)
        plan.finish(*args)

    return pl.pallas_call(body, out_shape=plan.outs, in_specs=[ANY] * n_in, out_specs=[ANY] * n_out,
                          scratch_shapes=plan.sems, name=name)(*plan.ins)


def _pcall(body, ride, *, grid, in_specs, out_specs, out_shape, scratch_shapes=(), semantics, name, prefetch=False):
    n_pre = int(prefetch)

    def build(kernel, ins, outs, shapes, scratch, sem):
        if prefetch:
            return pl.pallas_call(
                kernel, grid_spec=pltpu.PrefetchScalarGridSpec(num_scalar_prefetch=1, grid=grid, in_specs=ins,
                                                               out_specs=outs, scratch_shapes=scratch),
                out_shape=shapes, compiler_params=_params(*sem), name=name)
        return pl.pallas_call(kernel, grid=grid, in_specs=ins, out_specs=outs, out_shape=shapes,
                              scratch_shapes=scratch, compiler_params=_params(*sem), name=name)

    if ride is None:
        return build(body, in_specs, out_specs, out_shape, list(scratch_shapes), semantics)
    single = not isinstance(out_shape, (list, tuple))
    out_specs_l, out_shape_l = ([out_specs], [out_shape]) if single else (list(out_specs), list(out_shape))
    n_in, n_out, n_scr = len(in_specs), len(out_shape_l), len(scratch_shapes)
    r_in, r_out = len(ride.ins), len(ride.outs)

    def riding(*refs):
        pre, refs = refs[:n_pre], refs[n_pre:]
        cuts = [n_in, r_in, n_out, r_out, n_scr]
        groups, at = [], 0
        for width in cuts:
            groups.append(refs[at:at + width])
            at += width
        ins, r_ins, outs, r_outs, scr = groups
        sems = refs[at:]
        first = functools.reduce(jnp.logical_and, [pl.program_id(d) == 0 for d in range(len(grid))])
        last = functools.reduce(jnp.logical_and, [pl.program_id(d) == grid[d] - 1 for d in range(len(grid))])

        @pl.when(first)
        def _():
            ride.start(r_ins, r_outs, sems)

        several_steps = any(n > 1 for n in grid)
        if ride.relay is not None and several_steps:
            @pl.when(last)
            def _():
                ride.relay(r_ins, r_outs, sems)

        body(*pre, *ins, *outs, *scr)

        @pl.when(last)
        def _():
            if ride.relay is not None and not several_steps:
                ride.relay(r_ins, r_outs, sems)
            ride.finish(r_ins, r_outs, sems)

    call = build(riding, list(in_specs) + [ANY] * r_in, out_specs_l + [ANY] * r_out, out_shape_l + list(ride.outs),
                 list(scratch_shapes) + list(ride.sems), ["arbitrary"] * len(grid))

    def run(*args):
        res = call(*args, *ride.ins)
        return (res[0] if single else list(res[:n_out])), list(res[n_out:])

    return run


def _rms(x):
    return x * lax.rsqrt(jnp.mean(x * x, axis=-1, keepdims=True) + NORM_EPS)


def _mm(a, b, *, nt=False, ta=False, out_dtype, tm, tn, a_sqrelu=False, drelu_of=None, b_blocked=False,
        out_blocked=False, a_norm=None, a_norm_bwd=None, res_norm=None, loss_of=None, norm_bwd=None, name, ride=None):
    k, m = a.shape if ta else a.shape[::-1]
    if b_blocked:
        assert not nt and b.shape[1] == k and b.shape[2] == tn
        n = b.shape[0] * tn
    else:
        n = b.shape[0] if nt else b.shape[1]
        assert (b.shape[1] if nt else b.shape[0]) == k
    tm, tn = min(tm, m), min(tn, n)
    assert m % tm == 0 and n % tn == 0
    assert (res_norm is None and norm_bwd is None) or tn == n
    assert a_norm is None or a_norm_bwd is None
    assert loss_of is None or (res_norm is not None and norm_bwd is None)
    n_in = (2 + (drelu_of is not None) + (a_norm is not None) + 2 * (a_norm_bwd is not None)
            + 2 * (res_norm is not None) + (loss_of is not None) + 3 * (norm_bwd is not None))

    def body(*refs):
        a_ref, b_ref = refs[0], refs[1]
        extra = list(refs[2:n_in])
        outs = list(refs[n_in:])
        o_ref = outs.pop(0)
        u_ref = extra.pop(0) if drelu_of is not None else None
        if a_norm is not None:
            wn_ref, h_ref, h_scr = extra.pop(0), outs.pop(0), outs.pop()

            @pl.when(pl.program_id(1) == 0)
            def _():
                h = (_rms(a_ref[...]) * wn_ref[...]).astype(ACT_DTYPE)
                h_scr[...] = h
                h_ref[...] = h

            av = h_scr[...]
        elif a_norm_bwd is not None:
            y_ref, wy_ref = extra.pop(0), extra.pop(0)
            dy_ref, dwy_ref, dy_scr = outs.pop(0), outs.pop(0), outs.pop()
            first_rows = pl.program_id(0) == 0

            @pl.when(pl.program_id(1) == 0)
            def _():
                yv, up = y_ref[...], a_ref[...]
                rstd = lax.rsqrt(jnp.mean(yv * yv, axis=-1, keepdims=True) + NORM_EPS)
                yhat = yv * rstd
                g = up * wy_ref[...]
                dy = (rstd * (g - yhat * jnp.mean(g * yhat, axis=-1, keepdims=True))).astype(ACT_DTYPE)
                dy_scr[...] = dy
                dy_ref[...] = dy

                @pl.when(first_rows)
                def _():
                    dwy_ref[...] = jnp.zeros_like(dwy_ref)

                dwy_ref[...] += jnp.sum(up * yhat, axis=0, keepdims=True)

            av = dy_scr[...]
        else:
            av = a_ref[...]
        if a_sqrelu:
            av = jnp.square(jnp.maximum(av.astype(F32), 0.0))
        acc = _dot(av, b_ref[...], 0 if ta else 1, 1 if nt else 0)
        if u_ref is not None:
            acc = acc * (2.0 * jnp.maximum(u_ref[...].astype(F32), 0.0))
        if norm_bwd is not None:
            x_ref, wb_ref, add_ref = extra
            dw_ref = outs[0]
            xv = x_ref[...]
            rstd = lax.rsqrt(jnp.mean(xv * xv, axis=-1, keepdims=True) + NORM_EPS)
            xhat = xv * rstd
            g = acc * wb_ref[...]
            o_ref[...] = rstd * (g - xhat * jnp.mean(g * xhat, axis=-1, keepdims=True)) + add_ref[...]

            @pl.when(pl.program_id(0) == 0)
            def _():
                dw_ref[...] = jnp.zeros_like(dw_ref)

            dw_ref[...] += jnp.sum(acc * xhat, axis=0, keepdims=True)
            return
        o_ref[...] = acc.astype(out_dtype)
        if res_norm is not None:
            res_ref, wr_ref = extra[:2]
            z = res_ref[...] + _rms(acc) * wr_ref[...]
            if loss_of is None:
                outs[0][...] = z
                return
            diff = z - extra[2][...]
            outs[0][...] = diff * (1.0 / n)
            l_ref = outs[1]

            @pl.when(pl.program_id(0) == 0)
            def _():
                l_ref[...] = jnp.zeros_like(l_ref)

            l_ref[...] += 0.5 * jnp.sum(jnp.mean(diff * diff, axis=-1, keepdims=True), axis=0, keepdims=True)

    if b_blocked:
        b_spec = pl.BlockSpec((None, k, tn), lambda i, j: (j, 0, 0))
    elif nt:
        b_spec = pl.BlockSpec((tn, k), lambda i, j: (j, 0))
    else:
        b_spec = pl.BlockSpec((k, tn), lambda i, j: (0, j))
    a_spec = pl.BlockSpec((k, tm), lambda i, j: (0, i)) if ta else pl.BlockSpec((tm, k), lambda i, j: (i, 0))
    in_specs = [a_spec, b_spec]
    args = [a, b]
    if drelu_of is not None:
        in_specs.append(pl.BlockSpec((tm, tn), lambda i, j: (i, j)))
        args.append(drelu_of)
    if out_blocked:
        out_specs = [pl.BlockSpec((None, tm, tn), lambda i, j: (j, i, 0))]
        out_shape = [jax.ShapeDtypeStruct((n // tn, m, tn), out_dtype)]
    else:
        out_specs = [pl.BlockSpec((tm, tn), lambda i, j: (i, j))]
        out_shape = [jax.ShapeDtypeStruct((m, n), out_dtype)]
    scratch = []
    if a_norm is not None:
        assert not ta
        in_specs.append(pl.BlockSpec((1, k), lambda i, j: (0, 0)))
        args.append(a_norm)
        out_specs.append(pl.BlockSpec((tm, k), lambda i, j: (i, 0)))
        out_shape.append(jax.ShapeDtypeStruct((m, k), ACT_DTYPE))
        scratch.append(pltpu.VMEM((tm, k), ACT_DTYPE))
    if a_norm_bwd is not None:
        assert not ta
        in_specs += [pl.BlockSpec((tm, k), lambda i, j: (i, 0)), pl.BlockSpec((1, k), lambda i, j: (0, 0))]
        args += list(a_norm_bwd)
        out_specs += [pl.BlockSpec((tm, k), lambda i, j: (i, 0)), pl.BlockSpec((1, k), lambda i, j: (0, 0))]
        out_shape += [jax.ShapeDtypeStruct((m, k), ACT_DTYPE), jax.ShapeDtypeStruct((1, k), F32)]
        scratch.append(pltpu.VMEM((tm, k), ACT_DTYPE))
    if res_norm is not None:
        in_specs += [pl.BlockSpec((tm, n), lambda i, j: (i, 0)), pl.BlockSpec((1, n), lambda i, j: (0, 0))]
        args += list(res_norm)
        out_specs.append(pl.BlockSpec((tm, n), lambda i, j: (i, 0)))
        out_shape.append(jax.ShapeDtypeStruct((m, n), F32))
    if loss_of is not None:
        in_specs.append(pl.BlockSpec((tm, n), lambda i, j: (i, 0)))
        args.append(loss_of)
        out_specs.append(pl.BlockSpec((8, LANES), lambda i, j: (0, 0)))
        out_shape.append(jax.ShapeDtypeStruct((8, LANES), F32))
    if norm_bwd is not None:
        rows = pl.BlockSpec((tm, n), lambda i, j: (i, 0))
        in_specs += [rows, pl.BlockSpec((1, n), lambda i, j: (0, 0)), rows]
        args += list(norm_bwd)
        out_specs.append(pl.BlockSpec((1, n), lambda i, j: (0, 0)))
        out_shape.append(jax.ShapeDtypeStruct((1, n), F32))
    single = len(out_shape) == 1
    return _pcall(body, ride, grid=(m // tm, n // tn), in_specs=in_specs,
                  out_specs=out_specs[0] if single else out_specs, out_shape=out_shape[0] if single else out_shape,
                  scratch_shapes=scratch, semantics=("arbitrary", "arbitrary"), name=name)(*args)


ROW_TILE = 512
TM_FWD, TM_DX, TM_DW, TN = 2048, 1024, 1024, 512


def _norm_bwd(dy, x, w, *, out_dtype, add=None, name, ride=None):
    t, d = x.shape

    def body(*refs):
        dy_ref, x_ref, w_ref = refs[0], refs[1], refs[2]
        dx_ref, dw_ref = refs[-2], refs[-1]
        xv = x_ref[...]
        rstd = lax.rsqrt(jnp.mean(xv * xv, axis=-1, keepdims=True) + NORM_EPS)
        xhat = xv * rstd
        dyv = dy_ref[...].astype(F32)
        g = dyv * w_ref[...]
        dx = rstd * (g - xhat * jnp.mean(g * xhat, axis=-1, keepdims=True))
        if add is not None:
            dx = dx + refs[3][...]
        dx_ref[...] = dx.astype(out_dtype)

        @pl.when(pl.program_id(0) == 0)
        def _():
            dw_ref[...] = jnp.zeros_like(dw_ref)

        dw_ref[...] += jnp.sum(dyv * xhat, axis=0, keepdims=True)

    row = pl.BlockSpec((ROW_TILE, d), lambda i: (i, 0))
    vec = pl.BlockSpec((1, d), lambda i: (0, 0))
    in_specs = [row, row, vec] + ([row] if add is not None else [])
    args = [dy, x, w] + ([add] if add is not None else [])
    return _pcall(body, ride, grid=(t // ROW_TILE,), in_specs=in_specs, out_specs=[row, vec],
                  out_shape=[jax.ShapeDtypeStruct((t, d), out_dtype), jax.ShapeDtypeStruct((1, d), F32)],
                  semantics=("arbitrary",), name=name)(*args)


GLA_STATE = (GLA_HEADS * GLA_DV, GLA_KW)


def _gla_specs(chunk_of):
    rows = lambda width, col: pl.BlockSpec((CHUNK, width), lambda i: (chunk_of(i), col))
    const = lambda r, c: pl.BlockSpec((r, c), lambda i: (0, 0))
    return [rows(GLA_KW, 0),
            rows(GLA_KW, 1),
            rows(GROUP_WIDTH, 1),
            rows(GROUP_WIDTH, 0),
            rows(LANES, 4),
            const(LANES, GLA_KW),
            const(1, GLA_KW),
            const(1, GROUP_WIDTH)]


def _gla_chunk(q_ref, k_ref, v_ref, a_ref, wup_ref, ba_ref):
    z = _dot(a_ref[...], wup_ref[...]) + ba_ref[...]
    tri = (_iota((CHUNK, CHUNK), 1) <= _iota((CHUNK, CHUNK), 0)).astype(F32)
    cum = _dot_exact(tri, _log_sigmoid(z) * (1.0 / GLA_GATE_TAU))
    tot = cum[CHUNK - 1:CHUNK, :]
    e = jnp.exp(tot - cum)
    return (z, e, jnp.exp(tot), k_ref[...].astype(F32) * e, q_ref[...].astype(F32) * (GLA_DK ** -0.5),
            v_ref[...].astype(F32))


def _gla_head_mask():
    return _iota(GLA_STATE, 0) // GLA_DV == _iota(GLA_STATE, 1) // GLA_DK


def _gla_fwd(pmm, pel, w_up, b_a, gnorm_w, ride=None):
    t = pmm.shape[0]
    nc = t // CHUNK

    def body(q_ref, k_ref, v_ref, r_ref, a_ref, wup_ref, ba_ref, gw_ref, o_ref, st_ref, m_scr):
        @pl.when(pl.program_id(0) == 0)
        def _():
            m_scr[...] = jnp.zeros_like(m_scr)

        _, _, decay, kd, qs, vv = _gla_chunk(q_ref, k_ref, v_ref, a_ref, wup_ref, ba_ref)
        m = m_scr[...] * decay + jnp.where(_gla_head_mask(), _dot(vv, kd, 0, 0), 0.0)
        m_scr[...] = m
        st_ref[...] = m
        o = _dot(qs, m, 1, 1)
        rr = r_ref[...]
        gate = rr * jax.nn.sigmoid(rr) * gw_ref[...]
        for h in range(GLA_HEADS):
            vs = slice(h * GLA_DV, (h + 1) * GLA_DV)
            oh = o[:, vs]
            y = oh * lax.rsqrt(jnp.mean(oh * oh, axis=-1, keepdims=True) + NORM_EPS)
            o_ref[:, vs] = (y * gate[:, vs]).astype(o_ref.dtype)

    return _pcall(
        body, ride, grid=(nc,), in_specs=_gla_specs(lambda i: i),
        out_specs=[pl.BlockSpec((CHUNK, GROUP_WIDTH), lambda i: (i, 0)),
                   pl.BlockSpec((None,) + GLA_STATE, lambda i: (i, 0, 0))],
        out_shape=[jax.ShapeDtypeStruct((t, GROUP_WIDTH), ACT_DTYPE), jax.ShapeDtypeStruct((nc,) + GLA_STATE, F32)],
        scratch_shapes=[pltpu.VMEM(GLA_STATE, F32)],
        semantics=("arbitrary",), name="gla_fwd")(pmm, pmm, pmm, pel, pel, w_up, b_a, gnorm_w)


def _gla_bwd(pmm, pel, w_up, b_a, gnorm_w, states, dmix, ride=None):
    t = pmm.shape[0]
    nc = t // CHUNK
    scale = GLA_DK ** -0.5

    def body(q_ref, k_ref, v_ref, r_ref, a_ref, wup_ref, ba_ref, gw_ref, st_ref, prev_ref, do_ref,
             dq_ref, dk_ref, dv_ref, dr_ref, da_ref, dwup_ref, dba_ref, dgw_ref, dm_scr):
        step = pl.program_id(0)

        @pl.when(step == 0)
        def _():
            dm_scr[...] = jnp.zeros_like(dm_scr)
            dwup_ref[...] = jnp.zeros_like(dwup_ref)
            dba_ref[...] = jnp.zeros_like(dba_ref)
            dgw_ref[...] = jnp.zeros_like(dgw_ref)

        z, e, decay, kd, qs, vv = _gla_chunk(q_ref, k_ref, v_ref, a_ref, wup_ref, ba_ref)
        m = st_ref[...]
        m_prev = prev_ref[...] * (step < nc - 1).astype(F32)
        rr, dout, gw = r_ref[...], do_ref[...], gw_ref[...]
        sig = jax.nn.sigmoid(rr)
        silu = rr * sig
        dsilu = sig * (1.0 + rr * (1.0 - sig))
        o = _dot(qs, m, 1, 1)
        d_o, dgw = [], []
        for h in range(GLA_HEADS):
            vs = slice(h * GLA_DV, (h + 1) * GLA_DV)
            oh, dg = o[:, vs], dout[:, vs]
            rstd = lax.rsqrt(jnp.mean(oh * oh, axis=-1, keepdims=True) + NORM_EPS)
            y = oh * rstd
            dgw.append(jnp.sum(dg * y * silu[:, vs], axis=0, keepdims=True))
            dr_ref[:, vs] = (dg * y * gw[:, vs] * dsilu[:, vs]).astype(dr_ref.dtype)
            dy = dg * gw[:, vs] * silu[:, vs]
            d_o.append(rstd * (dy - y * jnp.mean(dy * y, axis=-1, keepdims=True)))
        d_o = jnp.concatenate(d_o, axis=1)
        dgw_ref[...] += jnp.concatenate(dgw, axis=1)
        dq_ref[...] = (_dot(d_o, m) * scale).astype(dq_ref.dtype)
        dm = dm_scr[...] + jnp.where(_gla_head_mask(), _dot(d_o, qs, 0, 0), 0.0)
        dv_ref[...] = _dot(kd, dm, 1, 1).astype(dv_ref.dtype)
        dkd = _dot(vv, dm)
        dk_ref[...] = (dkd * e).astype(dk_ref.dtype)
        dm_scr[...] = dm * decay
        tri_strict = (_iota((CHUNK, CHUNK), 1) < _iota((CHUNK, CHUNK), 0)).astype(F32)
        dla = jnp.sum(dm * m_prev, axis=0, keepdims=True) * decay + _dot_exact(tri_strict, dkd * kd)
        dz = dla * jax.nn.sigmoid(-z) * (1.0 / GLA_GATE_TAU)
        da_ref[...] = _dot(dz, wup_ref[...], 1, 1).astype(da_ref.dtype)
        dwup_ref[...] += _dot(a_ref[...], dz, 0, 0)
        dba_ref[...] += jnp.sum(dz, axis=0, keepdims=True)

    chunk_of = lambda i: nc - 1 - i
    in_specs = _gla_specs(chunk_of) + [
        pl.BlockSpec((None,) + GLA_STATE, lambda i: (chunk_of(i), 0, 0)),
        pl.BlockSpec((None,) + GLA_STATE, lambda i: (jnp.maximum(chunk_of(i) - 1, 0), 0, 0)),
        pl.BlockSpec((CHUNK, GROUP_WIDTH), lambda i: (chunk_of(i), 0))]
    rows = lambda width: pl.BlockSpec((CHUNK, width), lambda i: (chunk_of(i), 0))
    const = lambda r, c: pl.BlockSpec((r, c), lambda i: (0, 0))
    return _pcall(
        body, ride, grid=(nc,), in_specs=in_specs,
        out_specs=[rows(GLA_KW), rows(GLA_KW), rows(GROUP_WIDTH), rows(GROUP_WIDTH), rows(LANES),
                   const(LANES, GLA_KW), const(1, GLA_KW), const(1, GROUP_WIDTH)],
        out_shape=[jax.ShapeDtypeStruct((t, GLA_KW), ACT_DTYPE), jax.ShapeDtypeStruct((t, GLA_KW), ACT_DTYPE),
                   jax.ShapeDtypeStruct((t, GROUP_WIDTH), ACT_DTYPE), jax.ShapeDtypeStruct((t, GROUP_WIDTH), ACT_DTYPE),
                   jax.ShapeDtypeStruct((t, LANES), ACT_DTYPE), jax.ShapeDtypeStruct((LANES, GLA_KW), F32),
                   jax.ShapeDtypeStruct((1, GLA_KW), F32), jax.ShapeDtypeStruct((1, GROUP_WIDTH), F32)],
        scratch_shapes=[pltpu.VMEM(GLA_STATE, F32)],
        semantics=("arbitrary",), name="gla_bwd")(
            pmm, pmm, pmm, pel, pel, w_up, b_a, gnorm_w, states, states, dmix)


CUM_BLOCK = 256


def _fox_gate_fwd(pel, b_f):
    t = pel.shape[0]
    nb = t // CUM_BLOCK

    def body(f_ref, b_ref, cum_ref, cum_t_ref):
        tri = (_iota((CUM_BLOCK, CUM_BLOCK), 1) <= _iota((CUM_BLOCK, CUM_BLOCK), 0)).astype(F32)
        carry = jnp.zeros((1, LANES), F32)
        for blk in range(nb):
            rows = slice(blk * CUM_BLOCK, (blk + 1) * CUM_BLOCK)
            cum = _dot_exact(tri, _log_sigmoid(f_ref[rows, :] + b_ref[...])) + carry
            cum_ref[rows, :] = cum
            cum_t_ref[blk] = cum.T[:ATT_HEADS, :]
            carry = cum[CUM_BLOCK - 1:CUM_BLOCK, :]

    return pl.pallas_call(
        body, grid=(1,),
        in_specs=[pl.BlockSpec((t, LANES), lambda i: (0, 5)), pl.BlockSpec((1, LANES), lambda i: (0, 0))],
        out_specs=[pl.BlockSpec((t, LANES), lambda i: (0, 0)),
                   pl.BlockSpec((nb, ATT_HEADS, CUM_BLOCK), lambda i: (0, 0, 0))],
        out_shape=[jax.ShapeDtypeStruct((t, LANES), F32), jax.ShapeDtypeStruct((nb, ATT_HEADS, CUM_BLOCK), F32)],
        compiler_params=_params("arbitrary"), name="fox_gate_fwd")(pel, b_f)


def _fox_gate_bwd(pel, b_f, dcum_t, dcum_q):
    t = pel.shape[0]
    nb = t // CUM_BLOCK

    def body(f_ref, b_ref, dct_ref, dcq_ref, df_ref, db_ref):
        tri_up = (_iota((CUM_BLOCK, CUM_BLOCK), 1) >= _iota((CUM_BLOCK, CUM_BLOCK), 0)).astype(F32)
        carry = jnp.zeros((1, LANES), F32)
        db = jnp.zeros((1, LANES), F32)
        for blk in reversed(range(nb)):
            rows = slice(blk * CUM_BLOCK, (blk + 1) * CUM_BLOCK)
            query_side = sum(dcq_ref[pair, rows, :] for pair in range(dcq_ref.shape[0]))
            dls = _dot_exact(tri_up, dct_ref[blk].T + query_side) + carry
            carry = dls[0:1, :]
            df = dls * jax.nn.sigmoid(-(f_ref[rows, :] + b_ref[...]))
            df_ref[rows, :] = df.astype(df_ref.dtype)
            db = db + jnp.sum(df, axis=0, keepdims=True)
        db_ref[...] = db

    return pl.pallas_call(
        body, grid=(1,),
        in_specs=[pl.BlockSpec((t, LANES), lambda i: (0, 5)), pl.BlockSpec((1, LANES), lambda i: (0, 0)),
                  pl.BlockSpec((nb, LANES, CUM_BLOCK), lambda i: (0, 0, 0)),
                  pl.BlockSpec((dcum_q.shape[0], t, LANES), lambda i: (0, 0, 0))],
        out_specs=[pl.BlockSpec((t, LANES), lambda i: (0, 0)), pl.BlockSpec((1, LANES), lambda i: (0, 0))],
        out_shape=[jax.ShapeDtypeStruct((t, LANES), ACT_DTYPE), jax.ShapeDtypeStruct((1, LANES), F32)],
        compiler_params=_params("arbitrary"), name="fox_gate_bwd")(pel, b_f, dcum_t, dcum_q)


FOX_Q_BLOCK = 256


assert FOX_Q_BLOCK == CUM_BLOCK
FOX_KEY_STEP = 512


def _fox_scores(q_ref, k_ref, cum_ref, cum_t_ref, h, i):
    hs = slice(h * HEAD_DIM, (h + 1) * HEAD_DIM)
    nb = cum_t_ref.shape[0]
    key_gate = jnp.concatenate([cum_t_ref[kb, h:h + 1, :] for kb in range(nb)], axis=1)
    s = _dot(q_ref[:, hs], k_ref[:, hs], 1, 1) * (HEAD_DIM ** -0.5) + (cum_ref[:, h:h + 1] - key_gate)
    shape = (FOX_Q_BLOCK, nb * FOX_Q_BLOCK)
    return jnp.where(_iota(shape, 1) <= i * FOX_Q_BLOCK + _iota(shape, 0), s, NEG)


def _fox_specs(t):
    bq, nb = FOX_Q_BLOCK, t // FOX_Q_BLOCK
    return [pl.BlockSpec((bq, GROUP_WIDTH), lambda i: (i, 2)), pl.BlockSpec((t, GROUP_WIDTH), lambda i: (0, 3)),
            pl.BlockSpec((t, GROUP_WIDTH), lambda i: (0, 4)), pl.BlockSpec((bq, LANES), lambda i: (i, 0)),
            pl.BlockSpec((nb, ATT_HEADS, bq), lambda i: (0, 0, 0))]


def _fox_fwd(pmm, cum, cum_t, ride=None):
    t = pmm.shape[0]
    bq = FOX_Q_BLOCK

    def body(q_ref, k_ref, v_ref, cum_ref, cum_t_ref, o_ref, lse_ref):
        i = pl.program_id(0)
        lse_ref[...] = jnp.zeros_like(lse_ref)
        for h in range(ATT_HEADS):
            hs = slice(h * HEAD_DIM, (h + 1) * HEAD_DIM)
            s = _fox_scores(q_ref, k_ref, cum_ref, cum_t_ref, h, i)
            m = jnp.max(s, axis=-1, keepdims=True)
            p = jnp.exp(s - m)
            l = jnp.sum(p, axis=-1, keepdims=True)
            o_ref[:, hs] = (_dot(p, v_ref[:, hs]) / l).astype(o_ref.dtype)
            lse_ref[:, h:h + 1] = m + jnp.log(l)

    return _pcall(
        body, ride, grid=(t // bq,), in_specs=_fox_specs(t),
        out_specs=[pl.BlockSpec((bq, GROUP_WIDTH), lambda i: (i, 0)), pl.BlockSpec((bq, LANES), lambda i: (i, 0))],
        out_shape=[jax.ShapeDtypeStruct((t, GROUP_WIDTH), ACT_DTYPE), jax.ShapeDtypeStruct((t, LANES), F32)],
        semantics=("parallel",), name="fox_fwd")(pmm, pmm, pmm, cum, cum_t)


def _fox_bwd(pmm, cum, cum_t, lse, dmix, ride=None):
    t = pmm.shape[0]
    bq, nb = FOX_Q_BLOCK, t // FOX_Q_BLOCK
    pairs, per_pair = ATT_HEADS // 2, LANES // HEAD_DIM
    scale = HEAD_DIM ** -0.5

    def body(q_ref, k_ref, v_ref, cum_ref, cum_t_ref, lse_ref, do_ref, dq_ref, dk_ref, dv_ref, dct_ref, dcq_ref):
        g, i = pl.program_id(0), pl.program_id(1)

        @pl.when(i == 0)
        def _():
            dk_ref[...] = jnp.zeros_like(dk_ref)
            dv_ref[...] = jnp.zeros_like(dv_ref)

        @pl.when((i == 0) & (g == 0))
        def _():
            dct_ref[...] = jnp.zeros_like(dct_ref)

        lane = _iota((1, LANES), 1)

        def run(n):
            causal = _iota((bq, n), 1) <= i * bq + _iota((bq, n), 0)
            dcq = jnp.zeros((bq, LANES), F32)
            for hh in range(per_pair):
                h = per_pair * g + hh
                hs = slice(hh * HEAD_DIM, (hh + 1) * HEAD_DIM)
                pick = (lane == h).astype(F32)
                cq = jnp.sum(cum_ref[...] * pick, axis=1, keepdims=True)
                lse_h = jnp.sum(lse_ref[...] * pick, axis=1, keepdims=True)
                key_gate = jnp.concatenate([cum_t_ref[kb, pl.ds(h, 1), :] for kb in range(n // bq)], axis=1)
                s = _dot(q_ref[:, hs], k_ref[:n, hs], 1, 1) * scale + (cq - key_gate)
                p = jnp.exp(jnp.where(causal, s, NEG) - lse_h)
                do = do_ref[:, hs]
                dp = _dot(do, v_ref[:n, hs], 1, 1)
                ds = p * (dp - jnp.sum(p * dp, axis=-1, keepdims=True))
                dq_ref[:, hs] = (_dot(ds, k_ref[:n, hs]) * scale).astype(dq_ref.dtype)
                dk_ref[:n, hs] += _dot(ds, q_ref[:, hs], 0, 0) * scale
                dv_ref[:n, hs] += _dot(p, do, 0, 0)
                key_side = -jnp.sum(ds, axis=0, keepdims=True)
                for kb in range(n // bq):
                    dct_ref[kb, pl.ds(h, 1), :] += key_side[:, kb * bq:(kb + 1) * bq]
                dcq = dcq + jnp.sum(ds, axis=1, keepdims=True) * pick
            dcq_ref[...] = dcq

        for kx in range(t // FOX_KEY_STEP):
            pl.when(i // (FOX_KEY_STEP // bq) == kx)(functools.partial(run, (kx + 1) * FOX_KEY_STEP))

    cols = lambda first: pl.BlockSpec((bq, LANES), lambda g, i: (i, first + g))
    keys = lambda first: pl.BlockSpec((t, LANES), lambda g, i: (0, first + g))
    per_head = pl.BlockSpec((bq, LANES), lambda g, i: (i, 0))
    fox_q, fox_k, fox_v = (GROUP_WIDTH * n // LANES for n in (2, 3, 4))
    return _pcall(
        body, ride, grid=(pairs, t // bq),
        in_specs=[cols(fox_q), keys(fox_k), keys(fox_v), per_head,
                  pl.BlockSpec((nb, ATT_HEADS, bq), lambda g, i: (0, 0, 0)), per_head, cols(GROUP_WIDTH // LANES)],
        out_specs=[cols(0), keys(0), keys(0), pl.BlockSpec((nb, LANES, bq), lambda g, i: (0, 0, 0)),
                   pl.BlockSpec((None, bq, LANES), lambda g, i: (g, i, 0))],
        out_shape=[jax.ShapeDtypeStruct((t, GROUP_WIDTH), ACT_DTYPE), jax.ShapeDtypeStruct((t, GROUP_WIDTH), F32),
                   jax.ShapeDtypeStruct((t, GROUP_WIDTH), F32), jax.ShapeDtypeStruct((nb, LANES, bq), F32),
                   jax.ShapeDtypeStruct((pairs, t, LANES), F32)],
        semantics=("arbitrary", "arbitrary"), name="fox_bwd")(pmm, pmm, pmm, cum, cum_t, lse, dmix)


CA_Q_BLOCK = 4 * CHUNK
CA_WINDOW = CA_Q_BLOCK + CA_LEFT
CA_BASE = 1024


def _ca_bias_base(rel_bias):
    n = rel_bias.shape[0]
    flat = CA_Q_BLOCK + CA_LEFT - REL_CLIP
    tail = CA_BASE - flat - (2 * REL_CLIP + 1)
    return jnp.concatenate([jnp.broadcast_to(rel_bias[:, 2 * REL_CLIP:], (n, flat)), rel_bias[:, ::-1],
                            jnp.broadcast_to(rel_bias[:, :1], (n, tail))], axis=1)


def _ca_bias_base_grad(dbase):
    flat = CA_Q_BLOCK + CA_LEFT - REL_CLIP
    mid = dbase[:, flat:flat + 2 * REL_CLIP + 1][:, ::-1]
    lo = jnp.sum(dbase[:, flat + 2 * REL_CLIP + 1:], axis=1, keepdims=True)
    hi = jnp.sum(dbase[:, :flat], axis=1, keepdims=True)
    pad = jnp.zeros((dbase.shape[0], 2 * REL_CLIP - 1), F32)
    return mid + jnp.concatenate([lo, pad, hi], axis=1)


def _ca_mask(i):
    r, j = _iota((CA_Q_BLOCK, CA_WINDOW), 0), _iota((CA_Q_BLOCK, CA_WINDOW), 1)
    rc, jc = r // CHUNK, j // CHUNK
    return (jc >= rc) & (jc <= rc + CA_LEFT // CHUNK) & (i * CA_Q_BLOCK + j >= CA_LEFT)


def _ca_fill_bias(i, base_ref, bias_scr):
    @pl.when(i == 0)
    def _():
        for h in range(ATT_HEADS):
            rows = jnp.broadcast_to(base_ref[h:h + 1, :], (CA_Q_BLOCK, CA_BASE))
            bias_scr[h] = pltpu.roll(rows, CA_BASE - CA_Q_BLOCK, 1, stride=1, stride_axis=0)[:, :CA_WINDOW]


def _ca_scores(q_ref, kp_ref, bias_scr, win, h, mask):
    hs = slice(h * HEAD_DIM, (h + 1) * HEAD_DIM)
    s = _dot(q_ref[:, hs], kp_ref[win, hs], 1, 1) * (HEAD_DIM ** -0.5)
    return jnp.where(mask, s + bias_scr[h], NEG)


CA_BIAS_SCRATCH = pltpu.VMEM((ATT_HEADS, CA_Q_BLOCK, CA_WINDOW), F32)


def _ca_fwd(pmm, kp, vp, base, ride=None):
    t = pmm.shape[0]

    def body(q_ref, kp_ref, vp_ref, base_ref, o_ref, lse_ref, bias_scr):
        i = pl.program_id(0)
        _ca_fill_bias(i, base_ref, bias_scr)
        win = pl.ds(pl.multiple_of(i * CA_Q_BLOCK, CA_Q_BLOCK), CA_WINDOW)
        mask = _ca_mask(i)
        lse_ref[...] = jnp.zeros_like(lse_ref)
        for h in range(ATT_HEADS):
            hs = slice(h * HEAD_DIM, (h + 1) * HEAD_DIM)
            s = _ca_scores(q_ref, kp_ref, bias_scr, win, h, mask)
            m = jnp.max(s, axis=-1, keepdims=True)
            p = jnp.exp(s - m)
            l = jnp.sum(p, axis=-1, keepdims=True)
            o_ref[:, hs] = (_dot(p, vp_ref[win, hs]) / l).astype(o_ref.dtype)
            lse_ref[:, h:h + 1] = m + jnp.log(l)

    padded = pl.BlockSpec((t + CA_LEFT, GROUP_WIDTH), lambda i: (0, 0))
    return _pcall(
        body, ride, grid=(t // CA_Q_BLOCK,),
        in_specs=[pl.BlockSpec((CA_Q_BLOCK, GROUP_WIDTH), lambda i: (i, 0)), padded, padded,
                  pl.BlockSpec((ATT_HEADS, CA_BASE), lambda i: (0, 0))],
        out_specs=[pl.BlockSpec((CA_Q_BLOCK, GROUP_WIDTH), lambda i: (i, 0)),
                   pl.BlockSpec((CA_Q_BLOCK, LANES), lambda i: (i, 0))],
        out_shape=[jax.ShapeDtypeStruct((t, GROUP_WIDTH), ACT_DTYPE), jax.ShapeDtypeStruct((t, LANES), F32)],
        scratch_shapes=[CA_BIAS_SCRATCH], semantics=("arbitrary",), name="ca_fwd")(pmm, kp, vp, base)


def _ca_bwd(pmm, kp, vp, base, lse, dmix, ride=None):
    t = pmm.shape[0]
    scale = HEAD_DIM ** -0.5

    def body(q_ref, kp_ref, vp_ref, base_ref, lse_ref, do_ref, dq_ref, dkp_ref, dvp_ref, dbase_ref, bias_scr):
        i = pl.program_id(0)
        _ca_fill_bias(i, base_ref, bias_scr)

        @pl.when(i == 0)
        def _():
            dkp_ref[...] = jnp.zeros_like(dkp_ref)
            dvp_ref[...] = jnp.zeros_like(dvp_ref)
            dbase_ref[...] = jnp.zeros_like(dbase_ref)

        win = pl.ds(pl.multiple_of(i * CA_Q_BLOCK, CA_Q_BLOCK), CA_WINDOW)
        mask = _ca_mask(i)
        flip = (_iota((CA_Q_BLOCK, CA_Q_BLOCK), 0) + _iota((CA_Q_BLOCK, CA_Q_BLOCK), 1) == CA_Q_BLOCK - 1).astype(F32)
        for h in range(ATT_HEADS):
            hs = slice(h * HEAD_DIM, (h + 1) * HEAD_DIM)
            s = _ca_scores(q_ref, kp_ref, bias_scr, win, h, mask)
            p = jnp.exp(s - lse_ref[:, h:h + 1])
            do = do_ref[:, hs]
            dp = _dot(do, vp_ref[win, hs], 1, 1)
            ds = p * (dp - jnp.sum(p * dp, axis=-1, keepdims=True))
            dq_ref[:, hs] = (_dot(ds, kp_ref[win, hs]) * scale).astype(dq_ref.dtype)
            dkp_ref[win, hs] += _dot(ds, q_ref[:, hs], 0, 0) * scale
            dvp_ref[win, hs] += _dot(p, do, 0, 0)
            rev = jnp.concatenate([_dot(flip, ds), jnp.zeros((CA_Q_BLOCK, CA_BASE - CA_WINDOW), F32)], axis=1)
            lined = pltpu.roll(rev, 1, 1, stride=1, stride_axis=0)
            dbase_ref[h:h + 1, :] += jnp.sum(lined, axis=0, keepdims=True)

    padded = pl.BlockSpec((t + CA_LEFT, GROUP_WIDTH), lambda i: (0, 0))
    return _pcall(
        body, ride, grid=(t // CA_Q_BLOCK,),
        in_specs=[pl.BlockSpec((CA_Q_BLOCK, GROUP_WIDTH), lambda i: (i, 0)), padded, padded,
                  pl.BlockSpec((ATT_HEADS, CA_BASE), lambda i: (0, 0)),
                  pl.BlockSpec((CA_Q_BLOCK, LANES), lambda i: (i, 0)),
                  pl.BlockSpec((CA_Q_BLOCK, GROUP_WIDTH), lambda i: (i, 0))],
        out_specs=[pl.BlockSpec((CA_Q_BLOCK, GROUP_WIDTH), lambda i: (i, 0)), padded, padded,
                   pl.BlockSpec((ATT_HEADS, CA_BASE), lambda i: (0, 0))],
        out_shape=[jax.ShapeDtypeStruct((t, GROUP_WIDTH), ACT_DTYPE),
                   jax.ShapeDtypeStruct((t + CA_LEFT, GROUP_WIDTH), F32),
                   jax.ShapeDtypeStruct((t + CA_LEFT, GROUP_WIDTH), F32),
                   jax.ShapeDtypeStruct((ATT_HEADS, CA_BASE), F32)],
        scratch_shapes=[CA_BIAS_SCRATCH], semantics=("arbitrary",), name="ca_bwd")(pmm, kp, vp, base, lse, dmix)


GELU_C = 0.7978845608028654
GELU_A = 0.044715


def _shift_down(v, k, fill, period=None):
    rows = _iota(v.shape, 0)
    rows = rows if period is None else rows & (period - 1)
    return jnp.where(rows >= k, pltpu.roll(v, k, 0), fill)


def _shift_up(v, k, fill, period=None):
    t = v.shape[0]
    rows = _iota(v.shape, 0)
    rows, length = (rows, t) if period is None else (rows & (period - 1), period)
    return jnp.where(rows < length - k, pltpu.roll(v, t - k, 0), fill)


LRU_SCAN_BLOCK = 256


def _linear_scan(a, b, reverse=False):
    shift = _shift_up if reverse else _shift_down
    k = 1
    while k < LRU_SCAN_BLOCK:
        b = a * shift(b, k, 0.0, LRU_SCAN_BLOCK) + b
        a = a * shift(a, k, 1.0, LRU_SCAN_BLOCK)
        k *= 2
    nb = a.shape[0] // LRU_SCAN_BLOCK
    carry = jnp.zeros((1, a.shape[1]), F32)
    out = [None] * nb
    for blk in (reversed(range(nb)) if reverse else range(nb)):
        rows = slice(blk * LRU_SCAN_BLOCK, (blk + 1) * LRU_SCAN_BLOCK)
        h = b[rows] + a[rows] * carry
        out[blk] = h
        carry = h[0:1] if reverse else h[LRU_SCAN_BLOCK - 1:LRU_SCAN_BLOCK]
    return jnp.concatenate(out, axis=0)


def _neg_expm1(y):
    series = -y * (1.0 + y * (0.5 + y * (1.0 / 6.0 + y * (1.0 / 24.0 + y * (1.0 / 120.0)))))
    return jnp.where(y > -0.1, series, 1.0 - jnp.exp(y))


def _lru_forward(x, g_in, cw, cb, wa, ba, wx, bx, lam):
    xs = [_shift_down(x, CONV_WIDTH - 1 - j, 0.0) for j in range(CONV_WIDTH - 1)] + [x]
    xc = cb + sum(cw[j:j + 1, :] * xs[j] for j in range(CONV_WIDTH))
    r = jax.nn.sigmoid(_dot(xc, wa) + ba)
    i = jax.nn.sigmoid(_dot(xc, wx) + bx)
    lsl = _log_sigmoid(lam)
    la = LRU_C * r * lsl
    a = jnp.exp(la)
    s = jnp.sqrt(_neg_expm1(2.0 * la))
    h = _linear_scan(a, s * (i * xc))
    u = GELU_C * (g_in + GELU_A * g_in * g_in * g_in)
    th = jnp.tanh(u)
    gelu = 0.5 * g_in * (1.0 + th)
    return xs, xc, r, i, lsl, a, s, h, th, gelu


def _lru_specs(t):
    col = lambda off: pl.BlockSpec((t, LANES), lambda j: (0, j + off))
    vec = pl.BlockSpec((1, LANES), lambda j: (0, j))
    mat = pl.BlockSpec((None, LANES, LANES), lambda j: (j, 0, 0))
    return [col(0), col(GROUP_WIDTH // LANES), pl.BlockSpec((CONV_WIDTH, LANES), lambda j: (0, j)),
            vec, mat, vec, mat, vec, vec]


def _lru_fwd(pel, conv_w, conv_b, wa, ba, wx, bx, lam, ride=None):
    t = pel.shape[0]

    def body(g_ref, x_ref, cw_ref, cb_ref, wa_ref, ba_ref, wx_ref, bx_ref, lam_ref, o_ref):
        res = _lru_forward(x_ref[...], g_ref[...], cw_ref[...], cb_ref[...], wa_ref[...], ba_ref[...],
                           wx_ref[...], bx_ref[...], lam_ref[...])
        o_ref[...] = (res[7] * res[9]).astype(o_ref.dtype)

    return _pcall(
        body, ride, grid=(GROUP_WIDTH // LANES,), in_specs=_lru_specs(t),
        out_specs=pl.BlockSpec((t, LANES), lambda j: (0, j)),
        out_shape=jax.ShapeDtypeStruct((t, GROUP_WIDTH), ACT_DTYPE),
        semantics=("parallel",), name="lru_fwd")(pel, pel, conv_w, conv_b, wa, ba, wx, bx, lam)


def _lru_bwd(pel, conv_w, conv_b, wa, ba, wx, bx, lam, dmix, ride=None):
    t = pel.shape[0]

    def body(g_ref, x_ref, cw_ref, cb_ref, wa_ref, ba_ref, wx_ref, bx_ref, lam_ref, do_ref,
             dg_ref, dx_ref, dcw_ref, dcb_ref, dwa_ref, dba_ref, dwx_ref, dbx_ref, dlam_ref):
        g_in, cw, lam = g_ref[...], cw_ref[...], lam_ref[...]
        xs, xc, r, i, lsl, a, s, h, th, gelu = _lru_forward(
            x_ref[...], g_in, cw, cb_ref[...], wa_ref[...], ba_ref[...], wx_ref[...], bx_ref[...], lam)
        dout = do_ref[...]
        dgelu = 0.5 * (1.0 + th) + 0.5 * g_in * (1.0 - th * th) * GELU_C * (1.0 + 3.0 * GELU_A * g_in * g_in)
        dg_ref[...] = (dout * h * dgelu).astype(dg_ref.dtype)
        gsum = _linear_scan(_shift_up(a, 1, 0.0), dout * gelu, reverse=True)
        da = gsum * _shift_down(h, 1, 0.0)
        di = gsum * s * xc
        dla = da * a - gsum * (i * xc) * (a * a / s)
        dlam_ref[...] = jnp.sum(dla * (LRU_C * r), axis=0, keepdims=True) * jax.nn.sigmoid(-lam)
        dpr = dla * (LRU_C * lsl) * r * (1.0 - r)
        dpi = di * i * (1.0 - i)
        dxc = gsum * s * i + _dot(dpr, wa_ref[...], 1, 1) + _dot(dpi, wx_ref[...], 1, 1)
        xct = xc.T
        dwa_ref[...] = _dot(xct, dpr)
        dwx_ref[...] = _dot(xct, dpi)
        dba_ref[...] = jnp.sum(dpr, axis=0, keepdims=True)
        dbx_ref[...] = jnp.sum(dpi, axis=0, keepdims=True)
        dcb_ref[...] = jnp.sum(dxc, axis=0, keepdims=True)
        for j in range(CONV_WIDTH):
            dcw_ref[j:j + 1, :] = jnp.sum(dxc * xs[j], axis=0, keepdims=True)
        dx = cw[CONV_WIDTH - 1:CONV_WIDTH, :] * dxc
        for j in range(CONV_WIDTH - 1):
            dx = dx + cw[j:j + 1, :] * _shift_up(dxc, CONV_WIDTH - 1 - j, 0.0)
        dx_ref[...] = dx.astype(dx_ref.dtype)

    col = pl.BlockSpec((t, LANES), lambda j: (0, j))
    vec = pl.BlockSpec((1, LANES), lambda j: (0, j))
    mat = pl.BlockSpec((None, LANES, LANES), lambda j: (j, 0, 0))
    nb = GROUP_WIDTH // LANES
    vshape = jax.ShapeDtypeStruct((1, GROUP_WIDTH), F32)
    mshape = jax.ShapeDtypeStruct((nb, LANES, LANES), F32)
    return _pcall(
        body, ride, grid=(nb,),
        in_specs=_lru_specs(t) + [pl.BlockSpec((t, LANES), lambda j: (0, j + nb))],
        out_specs=[col, col, pl.BlockSpec((CONV_WIDTH, LANES), lambda j: (0, j)), vec, mat, vec, mat, vec, vec],
        out_shape=[jax.ShapeDtypeStruct((t, GROUP_WIDTH), ACT_DTYPE), jax.ShapeDtypeStruct((t, GROUP_WIDTH), ACT_DTYPE),
                   jax.ShapeDtypeStruct((CONV_WIDTH, GROUP_WIDTH), F32), vshape, mshape, vshape, mshape, vshape, vshape],
        semantics=("parallel",), name="lru_bwd")(
            pel, pel, conv_w, conv_b, wa, ba, wx, bx, lam, dmix)


def _block_diag_pairs(w):
    z = jnp.zeros((LRU_BLOCK_DIM, LRU_BLOCK_DIM), w.dtype)
    return jnp.stack([jnp.block([[w[2 * j], z], [z, w[2 * j + 1]]]) for j in range(w.shape[0] // 2)])


def _block_diag_pairs_grad(dw):
    b = LRU_BLOCK_DIM
    return jnp.stack([dw[n // 2, (n % 2) * b:(n % 2 + 1) * b, (n % 2) * b:(n % 2 + 1) * b] for n in range(2 * dw.shape[0])])


def _row_tile(r):
    return ROW_TILE if r % ROW_TILE == 0 else r


def _pair_sum(g, got, place, name):
    _, r, c = g.shape
    tile = r

    def body(place_ref, a_ref, b_ref, o_ref):
        o_ref[...] = (a_ref[...].astype(F32) + b_ref[...].astype(F32)).astype(o_ref.dtype)

    blk = pl.BlockSpec((1, tile, c), lambda k, i, place_ref: (k, i, 0))
    return pl.pallas_call(
        body,
        grid_spec=pltpu.PrefetchScalarGridSpec(
            num_scalar_prefetch=1, grid=(N_CHIPS, r // tile),
            in_specs=[pl.BlockSpec((1, tile, c), lambda k, i, place_ref: (2 * k + place_ref[0], i, 0)), blk],
            out_specs=blk),
        out_shape=jax.ShapeDtypeStruct(got.shape, got.dtype),
        compiler_params=_params("parallel", "parallel"), name=name)(place, g, got)


def _adamw_update(g, w_ref, m_ref, v_ref, g_ref, d_ref, nm_ref, nv_ref):
    nm = ADAM_B1 * m_ref[...] + (1.0 - ADAM_B1) * g
    nv = ADAM_B2 * v_ref[...] + (1.0 - ADAM_B2) * jnp.square(g)
    m_hat = nm / (1.0 - ADAM_B1 ** ADAM_STEP)
    v_hat = nv / (1.0 - ADAM_B2 ** ADAM_STEP)
    g_ref[...] = g
    d_ref[...] = -ADAM_LR * (m_hat / (jnp.sqrt(v_hat) + ADAM_EPS) + ADAM_WD * w_ref[...])
    nm_ref[...] = nm
    nv_ref[...] = nv


def _adamw_sharded(parts, w, m, v, place, name, ride=None):
    n_layers, r, c = w.shape
    tile = _row_tile(r)
    nb = r // tile
    counts = [1 + len(recvs) for _, recvs in parts]

    def body(place_ref, *refs):
        layer = pl.program_id(0)
        g, at = None, 0
        for l in range(n_layers):
            g_l = refs[at][0].astype(F32)
            for r_ref in refs[at + 1:at + counts[l]]:
                for k in range(r_ref.shape[0]):
                    g_l = g_l + r_ref[k].astype(F32)
            g = g_l if g is None else jnp.where(layer == l, g_l, g)
            at += counts[l]
        _adamw_update(g, *refs[at:])

    def part_specs(l, recvs):
        rows = lambda q, i: jnp.where(q < l, 0, jnp.where(q > l, nb - 1, i))
        return ([pl.BlockSpec((1, tile, c), lambda q, i, place_ref: (place_ref[1], rows(q, i), 0))] +
                [pl.BlockSpec((a.shape[0], tile, c), lambda q, i, place_ref: (0, rows(q, i), 0)) for a in recvs])

    in_specs, args = [], []
    for l, (s, recvs) in enumerate(parts):
        in_specs += part_specs(l, recvs)
        args += [s, *recvs]
    blk = pl.BlockSpec((None, tile, c), lambda q, i, place_ref: (q, i, 0))
    out = jax.ShapeDtypeStruct((n_layers, r, c), F32)
    return _pcall(body, ride, grid=(n_layers, nb), in_specs=in_specs + [blk, blk, blk], out_specs=[blk, blk, blk, blk],
                  out_shape=[out, out, out, out], semantics=("arbitrary", "arbitrary"), name=name, prefetch=True)(
                      place, *args, w, m, v)


def _adamw_small(repl_parts, vec_parts, w, m, v, place):
    n_r, n = len(repl_parts), len(w)
    shapes = [a.shape for a in w]

    def body(place_ref, *refs):
        parts, rest = refs[:n], refs[n:]
        for k in range(n):
            take = (lambda p: parts[k][p]) if k < n_r else (lambda p: parts[k][p, 0])
            g = take(0)
            for p in range(1, N_DEV):
                g = g + take(p)
            _adamw_update(g, rest[k], rest[n + k], rest[2 * n + k], *rest[3 * n + 4 * k:3 * n + 4 * k + 4])

    def whole(shape):
        return pl.BlockSpec(shape, lambda i, place_ref: (0,) * len(shape))

    def mine(shard):
        return pl.BlockSpec((N_DEV, 1) + shard, lambda i, place_ref: (0, place_ref[2]) + (0,) * len(shard))

    in_specs = [whole(a.shape) for a in repl_parts] + [mine(s) for s in shapes[n_r:]] + [whole(s) for s in shapes] * 3
    outs = pl.pallas_call(
        body,
        grid_spec=pltpu.PrefetchScalarGridSpec(
            num_scalar_prefetch=1, grid=(1,), in_specs=in_specs,
            out_specs=[whole(s) for s in shapes for _ in range(4)]),
        out_shape=[jax.ShapeDtypeStruct(s, F32) for s in shapes for _ in range(4)],
        compiler_params=_params("arbitrary"), name="adamw_small")(place, *repl_parts, *vec_parts, *w, *m, *v)
    return [outs[4 * k:4 * k + 4] for k in range(n)]


SHARDED = {"norm_w": 2, "w_in_even": 2, "gla_w_a_up": 2, "w_out_even": 1, "w_in_odd": 2, "conv_w": 2, "conv_b": 1,
           "lru_b_a": 1, "lru_b_x": 1, "lru_lambda": 1, "w_out_odd": 1, "w_mlp_up": 2, "w_mlp_down": 1}
REPLICATED = ["gla_b_a", "gla_norm_w", "fox_b_f", "rel_bias", "lru_w_a", "lru_w_x"]
WEIGHTS = ["norm_w", "w_in_even", "gla_w_a_up", "gla_b_a", "gla_norm_w", "fox_b_f", "w_out_even", "w_in_odd",
           "rel_bias", "conv_w", "conv_b", "lru_w_a", "lru_b_a", "lru_w_x", "lru_b_x", "lru_lambda", "w_out_odd",
           "w_mlp_up", "w_mlp_down"]
MATRICES = ("w_in_even", "w_out_even", "w_in_odd", "w_out_odd", "w_mlp_up", "w_mlp_down")
TRANSPOSED = ("w_in_even", "w_in_odd")
VECTORS = tuple(n for n in SHARDED if n not in MATRICES)
MATRIX_BLOCKS = (("w_in_even", 0), ("w_out_even", 0), ("w_in_odd", 0), ("w_out_odd", 0),
                 ("w_mlp_up", 0), ("w_mlp_up", 1), ("w_mlp_down", 0), ("w_mlp_down", 1))


def _join_shards(blocks, axis):
    moved = jnp.moveaxis(blocks, 0, axis)
    shape = moved.shape
    return moved.reshape(shape[:axis] + (shape[axis] * shape[axis + 1],) + shape[axis + 2:])


def _split_shards(full, axis):
    shape = full.shape
    cut = full.reshape(shape[:axis] + (N_DEV, shape[axis] // N_DEV) + shape[axis + 1:])
    return jnp.moveaxis(cut, axis, 0)


EVEN_SPLITS = (0, 256, 512, 1024, 1536, 1552, 2064, 2576, 3088, 3096)


def _even_in_split(wt):
    c = [wt[EVEN_SPLITS[k]:EVEN_SPLITS[k + 1]] for k in range(9)]
    gq, gk, gv, gr, ga, fq, fk, fv, ff = c
    padrows = lambda a: jnp.pad(a, ((0, LANES - a.shape[0]), (0, 0)))
    return jnp.concatenate([gq, gk, gv, fq, fk, fv], axis=0), jnp.concatenate([gr, padrows(ga), padrows(ff)], axis=0)


def _even_in_merge(dmm, dele):
    return jnp.concatenate([dmm[:1024], dele[:512], dele[512:512 + GLA_RANK], dmm[1024:2560],
                            dele[640:640 + ATT_HEADS]], axis=0)


def _forward_backward(x, target, shard, vec_shard, w, place):
    w = dict(w)
    g, dnorm, sums, recv = {}, {}, {}, {}
    nrm = lambda l, k: w["norm_w"][l, k][None, :]
    gather = lambda *keys: _gather_plan([shard[k] for k in keys])
    blocks = lambda r, c: (N_DEV, r // N_DEV, c)

    def pair_sum(key):
        sums[key] = _pair_sum(g[key], got[key], place, f"rs_pair_sum_{key[0]}_{key[1]}")

    got = {}

    def mlp_fwd(xin, layer, ride_up, ride_down, loss_of=None):
        up = _mm(xin, w["w_mlp_up"][layer], out_dtype=ACT_DTYPE, tm=TM_FWD, tn=D_FF // N_DEV, b_blocked=True,
                 a_norm=nrm(layer, 2), name=f"mlp_up_{layer}", ride=ride_up)
        (u, h), rode_up = up if ride_up is not None else (up, None)
        down = _mm(u, w["w_mlp_down"][layer], out_dtype=F32, tm=TM_DX // 2, tn=D_MODEL, a_sqrelu=True,
                   res_norm=(xin, nrm(layer, 3)), loss_of=loss_of, name=f"mlp_down_{layer}", ride=ride_down)
        (yv, *xout), rode_down = down if ride_down is not None else (down, None)
        return xout if loss_of is not None else xout[0], (xin, h, u, yv), rode_up, rode_down

    def mlp_bwd(dxout, saved, layer, ride):
        xin, h, u, yv = saved
        k_up, k_down = ("w_mlp_up", layer), ("w_mlp_down", layer)
        res = _mm(dxout, w["w_mlp_down"][layer], nt=True, out_dtype=ACT_DTYPE, tm=TM_DX, tn=TN, drelu_of=u,
                  a_norm_bwd=(yv, nrm(layer, 3)), name=f"mlp_down_dx_{layer}", ride=ride)
        (du, dy, dnorm[(layer, 3)]), rode = res if ride is not None else (res, None)
        g[k_down] = _mm(u, dy, ta=True, out_dtype=WIRE_DTYPE, tm=TM_DW, tn=TN, a_sqrelu=True,
                        name=f"mlp_down_dw_{layer}").reshape(blocks(D_FF, D_MODEL))
        g[k_up] = _mm(h, du, ta=True, out_dtype=WIRE_DTYPE, tm=TM_DW, tn=D_FF // N_DEV, out_blocked=True,
                      name=f"mlp_up_dw_{layer}")
        w_up = jnp.moveaxis(w["w_mlp_up"][layer], 0, 1).reshape(D_MODEL, D_FF)
        (dxin, dnorm[(layer, 2)]), (got[k_down], got[k_up]) = _mm(
            du, w_up, nt=True, out_dtype=F32, tm=TM_DX // 2, tn=D_MODEL, norm_bwd=(xin, nrm(layer, 2), dxout),
            name=f"mlp_up_dx_{layer}", ride=_sibling_plan([g[k_down], g[k_up]]))
        pair_sum(k_down)
        pair_sum(k_up)
        return dxin, rode

    first = _run_plan(_gather_plan([shard[("w_in_even", 0)]] + [vec_shard[n] for n in VECTORS]),
                      "weights_all_gather_first")
    w["w_in_even"] = first[0].reshape(-1, D_MODEL)
    for n, b in zip(VECTORS, first[1:]):
        w[n] = _join_shards(b, SHARDED[n])
    w["w_mlp_up"], w["w_mlp_down"] = [None] * DEPTH, [None] * DEPTH

    wmm_e, wel_e = _even_in_split(w["w_in_even"])
    w_up_pad = jnp.pad(w["gla_w_a_up"][0], ((0, LANES - GLA_RANK), (0, 0)))
    b_f_pad = jnp.pad(w["fox_b_f"], ((0, 0), (0, LANES - ATT_HEADS)))
    (pmm0, h0), (w_out_even,) = _mm(x, wmm_e, nt=True, out_dtype=ACT_DTYPE, tm=TM_FWD, tn=TN, a_norm=nrm(0, 0),
                                    name="in_even_mm", ride=gather(("w_out_even", 0)))
    pel0 = _mm(h0, wel_e, nt=True, out_dtype=F32, tm=TM_FWD, tn=768, name="in_even_el")
    (out_a, states), (w["w_mlp_up"][0],) = _gla_fwd(pmm0, pel0, w_up_pad, w["gla_b_a"], w["gla_norm_w"],
                                                    ride=gather(("w_mlp_up", 0)))
    cum, cum_t = _fox_gate_fwd(pel0, b_f_pad)
    (out_b, lse_b), (w_mlp_down0, w_in_odd) = _fox_fwd(pmm0, cum, cum_t,
                                                       ride=gather(("w_mlp_down", 0), ("w_in_odd", 0)))
    w["w_out_even"] = w_out_even.reshape(D_MODEL, D_MODEL)
    w["w_mlp_down"][0] = w_mlp_down0.reshape(D_FF, D_MODEL)
    mix_in0 = jnp.concatenate([out_a, out_b], axis=1)
    mix0, x1 = _mm(mix_in0, w["w_out_even"], out_dtype=F32, tm=TM_DX, tn=D_MODEL, res_norm=(x, nrm(0, 1)),
                   name="out_even")
    x2, mlp0, _, (w["w_mlp_up"][1],) = mlp_fwd(x1, 0, None, gather(("w_mlp_up", 1)))
    w["w_in_odd"] = w_in_odd.reshape(-1, D_MODEL)

    w_in_o = w["w_in_odd"]
    n_mm_o = 3 * GROUP_WIDTH
    wa_bd, wx_bd = _block_diag_pairs(w["lru_w_a"][0]), _block_diag_pairs(w["lru_w_x"][0])
    base = _ca_bias_base(w["rel_bias"][0])
    pmm1, h1 = _mm(x2, w_in_o[:n_mm_o], nt=True, out_dtype=ACT_DTYPE, tm=TM_FWD, tn=TN, a_norm=nrm(1, 0),
                   name="in_odd_mm")
    pel1 = _mm(h1, w_in_o[n_mm_o:], nt=True, out_dtype=F32, tm=TM_FWD, tn=TN, name="in_odd_el")
    kp = jnp.pad(pmm1[:, GROUP_WIDTH:2 * GROUP_WIDTH], ((CA_LEFT, 0), (0, 0)))
    vp = jnp.pad(pmm1[:, 2 * GROUP_WIDTH:], ((CA_LEFT, 0), (0, 0)))
    (out_c, lse_c), (w_mlp_down1,) = _ca_fwd(pmm1, kp, vp, base, ride=gather(("w_mlp_down", 1)))
    w["w_mlp_down"][1] = w_mlp_down1.reshape(D_FF, D_MODEL)
    lru_args = (pel1, w["conv_w"][0], w["conv_b"], wa_bd, w["lru_b_a"], wx_bd, w["lru_b_x"], w["lru_lambda"])
    out_d, (w_out_odd,) = _lru_fwd(*lru_args, ride=gather(("w_out_odd", 0)))
    w["w_out_odd"] = w_out_odd.reshape(D_MODEL, D_MODEL)
    mix_in1 = jnp.concatenate([out_c, out_d], axis=1)
    mix1, x3 = _mm(mix_in1, w["w_out_odd"], out_dtype=F32, tm=TM_DX, tn=D_MODEL, res_norm=(x2, nrm(1, 1)),
                   name="out_odd")
    (dx4, loss), mlp1, _, _ = mlp_fwd(x3, 1, None, None, loss_of=target)

    k_oo, k_io, k_oe, k_ie = ("w_out_odd", 0), ("w_in_odd", 0), ("w_out_even", 0), ("w_in_even", 0)
    mlp_keys = lambda l: [("w_mlp_down", l), ("w_mlp_up", l)]
    dx3, _ = mlp_bwd(dx4, mlp1, 1, None)
    dmix_in1, dmix1, dnorm[(1, 1)] = _mm(dx3, w["w_out_odd"], nt=True, out_dtype=F32, tm=TM_DX, tn=TN,
                                         a_norm_bwd=(mix1, nrm(1, 1)), name="out_odd_dx")
    g[k_oo] = _mm(mix_in1, dmix1, ta=True, out_dtype=WIRE_DTYPE, tm=TM_DW, tn=TN, name="out_odd_dw").reshape(
        blocks(D_MODEL, D_MODEL))
    (dq_c, dkp, dvp, dbase), rode = _ca_bwd(
        pmm1, kp, vp, base, lse_c, dmix_in1,
        ride=_join_plans(_chip_plan([sums[k] for k in mlp_keys(1)]), _sibling_plan([g[k_oo]])))
    recv.update(zip(mlp_keys(1), rode[:2]))
    got[k_oo] = rode[2]
    pair_sum(k_oo)
    (dgate, dxin, g_conv_w, g_conv_b, dwa_bd, g_lru_b_a, dwx_bd, g_lru_b_x, g_lru_lambda), (recv[k_oo],) = _lru_bwd(
        *lru_args, dmix_in1, ride=_chip_plan([sums[k_oo]]))
    dp1 = jnp.concatenate([dq_c, dkp[CA_LEFT:].astype(ACT_DTYPE), dvp[CA_LEFT:].astype(ACT_DTYPE), dgate, dxin], axis=1)
    g[k_io] = _mm(dp1, h1, ta=True, out_dtype=WIRE_DTYPE, tm=dp1.shape[1] // 2, tn=TN, name="in_odd_dw").reshape(
        blocks(dp1.shape[1], D_MODEL))
    (dx2, dnorm[(1, 0)]), (got[k_io],) = _mm(dp1, w_in_o, out_dtype=F32, tm=TM_DX // 2, tn=D_MODEL,
                                             norm_bwd=(x2, nrm(1, 0), dx3), name="in_odd_dx",
                                             ride=_sibling_plan([g[k_io]]))
    pair_sum(k_io)
    g["rel_bias"] = _ca_bias_base_grad(dbase)[None]
    g["conv_w"], g["conv_b"] = g_conv_w[None], g_conv_b
    g["lru_w_a"], g["lru_w_x"] = _block_diag_pairs_grad(dwa_bd)[None], _block_diag_pairs_grad(dwx_bd)[None]
    g["lru_b_a"], g["lru_b_x"], g["lru_lambda"] = g_lru_b_a, g_lru_b_x, g_lru_lambda

    dx1, (recv[k_io],) = mlp_bwd(dx2, mlp0, 0, _chip_plan([sums[k_io]]))
    dmix_in0, dmix0, dnorm[(0, 1)] = _mm(dx1, w["w_out_even"], nt=True, out_dtype=F32, tm=TM_DX, tn=TN,
                                         a_norm_bwd=(mix0, nrm(0, 1)), name="out_even_dx")
    g[k_oe] = _mm(mix_in0, dmix0, ta=True, out_dtype=WIRE_DTYPE, tm=TM_DW, tn=TN, name="out_even_dw").reshape(
        blocks(D_MODEL, D_MODEL))
    k_md0, k_mu0 = mlp_keys(0)
    (dq_a, dk_a, dv_a, dr_a, da_a, dw_up_pad, g_gla_b_a, g_gla_norm_w), (got[k_oe],) = _gla_bwd(
        pmm0, pel0, w_up_pad, w["gla_b_a"], w["gla_norm_w"], states, dmix_in0, ride=_sibling_plan([g[k_oe]]))
    pair_sum(k_oe)
    (dq_b, dk_b, dv_b, dcum_t, dcum_q), (recv[k_md0], recv[k_mu0], recv[k_oe]) = _fox_bwd(
        pmm0, cum, cum_t, lse_b, dmix_in0, ride=_chip_plan([sums[k_md0], sums[k_mu0], sums[k_oe]]))
    df_b, db_f = _fox_gate_bwd(pel0, b_f_pad, dcum_t, dcum_q)
    g["gla_w_a_up"] = dw_up_pad[:GLA_RANK][None]
    g["gla_b_a"], g["gla_norm_w"], g["fox_b_f"] = g_gla_b_a, g_gla_norm_w, db_f[:, :ATT_HEADS]
    dp0 = jnp.concatenate([dq_a, dk_a, dv_a, dq_b, dk_b.astype(ACT_DTYPE), dv_b.astype(ACT_DTYPE), dr_a, da_a, df_b],
                          axis=1)
    w_perm = jnp.concatenate([wmm_e, wel_e], axis=0)
    n_mm_e = wmm_e.shape[0]
    dw_perm = _mm(dp0, h0, ta=True, out_dtype=WIRE_DTYPE, tm=dp0.shape[1] // 2, tn=TN, name="in_even_dw")
    dw_even = _even_in_merge(dw_perm[:n_mm_e], dw_perm[n_mm_e:])
    g[k_ie] = dw_even.reshape(blocks(dw_even.shape[0], D_MODEL))
    dh0, (got[k_ie], *repl_parts) = _mm(
        dp0, w_perm, out_dtype=F32, tm=TM_DX, tn=TN, name="in_even_dx",
        ride=_join_plans(_sibling_plan([g[k_ie]]), _gather_plan([g[n] for n in REPLICATED])))
    pair_sum(k_ie)
    dx0, dnorm[(0, 0)] = _norm_bwd(dh0, x, nrm(0, 0), out_dtype=F32, add=dx1, name="norm_in_bwd_0")

    g["norm_w"] = jnp.stack([jnp.concatenate([dnorm[(l, k)] for k in range(4)], axis=0) for l in range(DEPTH)])
    recv[k_ie], losses, *vec_parts = _run_plan(
        _join_plans(_chip_plan([sums[k_ie]]),
                    _gather_plan([loss] + [_split_shards(g[n], SHARDED[n]) for n in VECTORS])), "last_exchanges")
    return losses, dx0, sums, recv, repl_parts, vec_parts


def kernel(x, norm_w, w_in_even, gla_w_a_up, gla_b_a, gla_norm_w, fox_b_f, w_out_even, w_in_odd, rel_bias, conv_w, conv_b, lru_w_a, lru_b_a, lru_w_x, lru_b_x, lru_lambda, w_out_odd, w_mlp_up, w_mlp_down, loss_target, m_norm_w, m_w_in_even, m_gla_w_a_up, m_gla_b_a, m_gla_norm_w, m_fox_b_f, m_w_out_even, m_w_in_odd, m_rel_bias, m_conv_w, m_conv_b, m_lru_w_a, m_lru_b_a, m_lru_w_x, m_lru_b_x, m_lru_lambda, m_w_out_odd, m_w_mlp_up, m_w_mlp_down, v_norm_w, v_w_in_even, v_gla_w_a_up, v_gla_b_a, v_gla_norm_w, v_fox_b_f, v_w_out_even, v_w_in_odd, v_rel_bias, v_conv_w, v_conv_b, v_lru_w_a, v_lru_b_a, v_lru_w_x, v_lru_b_x, v_lru_lambda, v_w_out_odd, v_w_mlp_up, v_w_mlp_down):
    wts = dict(zip(WEIGHTS, (norm_w, w_in_even, gla_w_a_up, gla_b_a, gla_norm_w, fox_b_f, w_out_even, w_in_odd, rel_bias,
                             conv_w, conv_b, lru_w_a, lru_b_a, lru_w_x, lru_b_x, lru_lambda, w_out_odd, w_mlp_up,
                             w_mlp_down)))
    mom = dict(zip(WEIGHTS, (m_norm_w, m_w_in_even, m_gla_w_a_up, m_gla_b_a, m_gla_norm_w, m_fox_b_f, m_w_out_even,
                             m_w_in_odd, m_rel_bias, m_conv_w, m_conv_b, m_lru_w_a, m_lru_b_a, m_lru_w_x, m_lru_b_x,
                             m_lru_lambda, m_w_out_odd, m_w_mlp_up, m_w_mlp_down)))
    var = dict(zip(WEIGHTS, (v_norm_w, v_w_in_even, v_gla_w_a_up, v_gla_b_a, v_gla_norm_w, v_fox_b_f, v_w_out_even,
                             v_w_in_odd, v_rel_bias, v_conv_w, v_conv_b, v_lru_w_a, v_lru_b_a, v_lru_w_x, v_lru_b_x,
                             v_lru_lambda, v_w_out_odd, v_w_mlp_up, v_w_mlp_down)))
    ax, ay, ac = lax.axis_index("x"), lax.axis_index("y"), lax.axis_index("c")
    place = jnp.stack([ac, 2 * ax + ay, 4 * ax + 2 * ay + ac]).astype(jnp.int32)

    shard = {(n, l): (wts[n][l].T if n in TRANSPOSED else wts[n][l]).astype(WIRE_DTYPE) for n, l in MATRIX_BLOCKS}
    losses, dx, sums, recv, repl_parts, vec_parts = _forward_backward(
        x[0], loss_target[0], shard, {n: wts[n] for n in VECTORS}, {n: wts[n] for n in REPLICATED}, place)
    loss = jnp.sum(losses[:, 0, 0])

    view = lambda n, a: jnp.swapaxes(a, 1, 2) if n in TRANSPOSED else a
    upd = {n: [view(n, o) for o in _adamw_sharded(
        [(sums[(n, l)], [recv[(n, l)]]) for l in range(wts[n].shape[0])], view(n, wts[n]), view(n, mom[n]),
        view(n, var[n]), place, f"adamw_{n}")] for n in MATRICES}
    small = REPLICATED + list(VECTORS)
    upd.update(zip(small, _adamw_small(repl_parts, vec_parts, [wts[n] for n in small], [mom[n] for n in small],
                                       [var[n] for n in small], place)))
    return (loss, dx[None], *[upd[n][kind] for kind in range(4) for n in WEIGHTS])
```

```python
import functools
from typing import Callable, NamedTuple, Optional

import jax
import jax.numpy as jnp
from jax import lax
from jax.experimental import pallas as pl
from jax.experimental.pallas import tpu as pltpu

F32 = jnp.float32
MXU_DTYPE = jnp.bfloat16
ACT_DTYPE = jnp.bfloat16
WIRE_DTYPE = jnp.bfloat16

V7X_VMEM_BYTES = 64 * 1024 * 1024
VMEM_LIMIT = (V7X_VMEM_BYTES * 7) // 8
LANES = 128

D_MODEL = 1024
DEPTH = 2
CHUNK = 64
GROUP_WIDTH = D_MODEL // 2
D_FF = 4 * D_MODEL
NORM_EPS = 1e-6
GLA_HEADS = 4
GLA_DV = GROUP_WIDTH // GLA_HEADS
GLA_DK = GLA_DV // 2
GLA_KW = GLA_HEADS * GLA_DK
GLA_RANK = 16
GLA_GATE_TAU = 16.0
HEAD_DIM = 64
ATT_HEADS = GROUP_WIDTH // HEAD_DIM
CA_LEFT = 8 * CHUNK
REL_CLIP = 128
LRU_BLOCK_DIM = 64
CONV_WIDTH = 4
LRU_C = 8.0
N_DEV = 8

ADAM_LR = 0.001
ADAM_B1 = 0.9
ADAM_B2 = 0.999
ADAM_EPS = 1e-08
ADAM_WD = 0.01
ADAM_STEP = 10

NEG = float(jnp.finfo(jnp.float32).min)
MESH = pl.DeviceIdType.MESH


def _params(*sem):
    return pltpu.CompilerParams(dimension_semantics=sem, vmem_limit_bytes=VMEM_LIMIT)


def _dot(a, b, ca=1, cb=0):
    return lax.dot_general(a.astype(MXU_DTYPE), b.astype(MXU_DTYPE), (((ca,), (cb,)), ((), ())),
                           preferred_element_type=F32)


def _dot_exact(a, b):
    return lax.dot_general(a, b, (((1,), (0,)), ((), ())), precision=lax.Precision.HIGHEST,
                           preferred_element_type=F32)


def _log_sigmoid(x):
    return jnp.minimum(x, 0.0) - jnp.log1p(jnp.exp(-jnp.abs(x)))


def _iota(shape, axis):
    return lax.broadcasted_iota(jnp.int32, shape, axis)


ANY = pl.BlockSpec(memory_space=pl.ANY)
N_CHIPS = 4


class _Plan(NamedTuple):
    ins: list
    outs: list
    sems: list
    start: Callable
    finish: Callable
    relay: Optional[Callable] = None


def _place():
    x, y, c = lax.axis_index("x"), lax.axis_index("y"), lax.axis_index("c")
    return x, y, c, [(1 - x, y), (x, 1 - y), (1 - x, 1 - y)]


def _gather_plan(xs):
    n = len(xs)

    def parts(x_refs, out_refs, sems):
        send_sems, recv_sems, local_sems = sems
        x, y, c, chips = _place()
        me, sibling = (x, y, c), (x, y, 1 - c)

        def rows(a, px, py, pc):
            return out_refs[a].at[4 * px + 2 * py + pc]

        def copy(a, k, block, to, src=None):
            return pltpu.make_async_remote_copy(
                src_ref=rows(a, *block) if src is None else src, dst_ref=rows(a, *block),
                send_sem=send_sems.at[7 * a + k], recv_sem=recv_sems.at[7 * a + k], device_id=to, device_id_type=MESH)

        def own():
            mine = [pltpu.make_async_copy(x_refs[a], rows(a, *me), local_sems.at[a]) for a in range(n)]
            first = []
            for a in range(n):
                first.append(copy(a, 0, me, sibling, src=x_refs[a]))
                first += [copy(a, 1 + j, me, (*chip, c), src=x_refs[a]) for j, chip in enumerate(chips)]
            return mine, first

        return c, me, sibling, chips, copy, own

    def start(x_refs, out_refs, sems):
        mine, first = parts(x_refs, out_refs, sems)[-1]()
        for cp in first + mine:
            cp.start()

    def relay(x_refs, out_refs, sems):
        c, me, sibling, chips, copy, _ = parts(x_refs, out_refs, sems)
        for j, chip in enumerate(chips):
            for a in range(n):
                copy(a, 1 + j, (*chip, c), me).wait_recv()
                copy(a, 4 + j, (*chip, c), sibling).start()

    def finish(x_refs, out_refs, sems):
        c, me, sibling, chips, copy, own = parts(x_refs, out_refs, sems)
        mine, first = own()
        for a in range(n):
            copy(a, 0, sibling, me).wait_recv()
            for j, chip in enumerate(chips):
                copy(a, 4 + j, (*chip, 1 - c), me).wait_recv()
        for cp in first + [copy(a, 4 + j, (*chip, c), sibling) for j, chip in enumerate(chips) for a in range(n)]:
            cp.wait_send()
        for cp in mine:
            cp.wait()

    return _Plan(list(xs), [jax.ShapeDtypeStruct((N_DEV,) + x.shape, x.dtype) for x in xs],
                 [pltpu.SemaphoreType.DMA((7 * n,)), pltpu.SemaphoreType.DMA((7 * n,)), pltpu.SemaphoreType.DMA((n,))],
                 start, finish, relay)


def _exchange_plan(copies_of, ins, outs, per_array):
    n = len(ins)

    def start(in_refs, out_refs, sems):
        for cp in copies_of(in_refs, out_refs, sems):
            cp.start()

    def finish(in_refs, out_refs, sems):
        copies = copies_of(in_refs, out_refs, sems)
        for cp in copies:
            cp.wait_recv()
        for cp in copies:
            cp.wait_send()

    return _Plan(list(ins), outs, [pltpu.SemaphoreType.DMA((per_array * n,)), pltpu.SemaphoreType.DMA((per_array * n,))],
                 start, finish)


def _sibling_plan(gs):
    def copies_of(g_refs, got_refs, sems):
        x, y, c, _ = _place()
        return [pltpu.make_async_remote_copy(
            src_ref=g_refs[a].at[2 * k + (1 - c)], dst_ref=got_refs[a].at[k], send_sem=sems[0].at[N_CHIPS * a + k],
            recv_sem=sems[1].at[N_CHIPS * a + k], device_id=(x, y, 1 - c), device_id_type=MESH)
            for a in range(len(gs)) for k in range(N_CHIPS)]

    return _exchange_plan(copies_of, gs, [jax.ShapeDtypeStruct((N_CHIPS,) + g.shape[1:], g.dtype) for g in gs], N_CHIPS)


def _chip_plan(ss, relations=(0, 1, 2)):
    n_rel = len(relations)

    def copies_of(s_refs, out_refs, sems):
        x, y, c, chips = _place()
        return [pltpu.make_async_remote_copy(
            src_ref=s_refs[a].at[2 * chips[j][0] + chips[j][1]], dst_ref=out_refs[a].at[slot],
            send_sem=sems[0].at[n_rel * a + slot], recv_sem=sems[1].at[n_rel * a + slot],
            device_id=(*chips[j], c), device_id_type=MESH)
            for a in range(len(ss)) for slot, j in enumerate(relations)]

    return _exchange_plan(copies_of, ss, [jax.ShapeDtypeStruct((n_rel,) + s.shape[1:], s.dtype) for s in ss], n_rel)


def _join_plans(*plans):
    def cut(refs, counts):
        at = 0
        for n in counts:
            yield refs[at:at + n]
            at += n

    def each(in_refs, out_refs, sems):
        return zip(plans, cut(in_refs, [len(p.ins) for p in plans]), cut(out_refs, [len(p.outs) for p in plans]),
                   cut(sems, [len(p.sems) for p in plans]))

    def start(*refs):
        for p, i, o, s in each(*refs):
            p.start(i, o, s)

    def relay(*refs):
        for p, i, o, s in each(*refs):
            if p.relay is not None:
                p.relay(i, o, s)

    def finish(*refs):
        for p, i, o, s in each(*refs):
            p.finish(i, o, s)

    return _Plan([a for p in plans for a in p.ins], [a for p in plans for a in p.outs],
                 [a for p in plans for a in p.sems], start, finish, relay)


def _run_plan(plan, name):
    n_in, n_out = len(plan.ins), len(plan.outs)

    def body(*refs):
        args = refs[:n_in], refs[n_in:n_in + n_out], refs[n_in + n_out:]
        plan.start(*args)
        if plan.relay is not None:
            plan.relay(*args)
        plan.finish(*args)

    return pl.pallas_call(body, out_shape=plan.outs, in_specs=[ANY] * n_in, out_specs=[ANY] * n_out,
                          scratch_shapes=plan.sems, name=name)(*plan.ins)


def _pcall(body, ride, *, grid, in_specs, out_specs, out_shape, scratch_shapes=(), semantics, name, prefetch=False):
    n_pre = int(prefetch)

    def build(kernel, ins, outs, shapes, scratch, sem):
        if prefetch:
            return pl.pallas_call(
                kernel, grid_spec=pltpu.PrefetchScalarGridSpec(num_scalar_prefetch=1, grid=grid, in_specs=ins,
                                                               out_specs=outs, scratch_shapes=scratch),
                out_shape=shapes, compiler_params=_params(*sem), name=name)
        return pl.pallas_call(kernel, grid=grid, in_specs=ins, out_specs=outs, out_shape=shapes,
                              scratch_shapes=scratch, compiler_params=_params(*sem), name=name)

    if ride is None:
        return build(body, in_specs, out_specs, out_shape, list(scratch_shapes), semantics)
    single = not isinstance(out_shape, (list, tuple))
    out_specs_l, out_shape_l = ([out_specs], [out_shape]) if single else (list(out_specs), list(out_shape))
    n_in, n_out, n_scr = len(in_specs), len(out_shape_l), len(scratch_shapes)
    r_in, r_out = len(ride.ins), len(ride.outs)

    def riding(*refs):
        pre, refs = refs[:n_pre], refs[n_pre:]
        cuts = [n_in, r_in, n_out, r_out, n_scr]
        groups, at = [], 0
        for width in cuts:
            groups.append(refs[at:at + width])
            at += width
        ins, r_ins, outs, r_outs, scr = groups
        sems = refs[at:]
        first = functools.reduce(jnp.logical_and, [pl.program_id(d) == 0 for d in range(len(grid))])
        last = functools.reduce(jnp.logical_and, [pl.program_id(d) == grid[d] - 1 for d in range(len(grid))])

        @pl.when(first)
        def _():
            ride.start(r_ins, r_outs, sems)

        several_steps = any(n > 1 for n in grid)
        if ride.relay is not None and several_steps:
            @pl.when(last)
            def _():
                ride.relay(r_ins, r_outs, sems)

        body(*pre, *ins, *outs, *scr)

        @pl.when(last)
        def _():
            if ride.relay is not None and not several_steps:
                ride.relay(r_ins, r_outs, sems)
            ride.finish(r_ins, r_outs, sems)

    call = build(riding, list(in_specs) + [ANY] * r_in, out_specs_l + [ANY] * r_out, out_shape_l + list(ride.outs),
                 list(scratch_shapes) + list(ride.sems), ["arbitrary"] * len(grid))

    def run(*args):
        res = call(*args, *ride.ins)
        return (res[0] if single else list(res[:n_out])), list(res[n_out:])

    return run


def _rms(x):
    return x * lax.rsqrt(jnp.mean(x * x, axis=-1, keepdims=True) + NORM_EPS)


def _mm(a, b, *, nt=False, ta=False, out_dtype, tm, tn, a_sqrelu=False, drelu_of=None, b_blocked=False,
        out_blocked=False, a_norm=None, a_norm_bwd=None, res_norm=None, loss_of=None, norm_bwd=None, name, ride=None):
    k, m = a.shape if ta else a.shape[::-1]
    if b_blocked:
        assert not nt and b.shape[1] == k and b.shape[2] == tn
        n = b.shape[0] * tn
    else:
        n = b.shape[0] if nt else b.shape[1]
        assert (b.shape[1] if nt else b.shape[0]) == k
    tm, tn = min(tm, m), min(tn, n)
    assert m % tm == 0 and n % tn == 0
    assert (res_norm is None and norm_bwd is None) or tn == n
    assert a_norm is None or a_norm_bwd is None
    assert loss_of is None or (res_norm is not None and norm_bwd is None)
    n_in = (2 + (drelu_of is not None) + (a_norm is not None) + 2 * (a_norm_bwd is not None)
            + 2 * (res_norm is not None) + (loss_of is not None) + 3 * (norm_bwd is not None))

    def body(*refs):
        a_ref, b_ref = refs[0], refs[1]
        extra = list(refs[2:n_in])
        outs = list(refs[n_in:])
        o_ref = outs.pop(0)
        u_ref = extra.pop(0) if drelu_of is not None else None
        if a_norm is not None:
            wn_ref, h_ref, h_scr = extra.pop(0), outs.pop(0), outs.pop()

            @pl.when(pl.program_id(1) == 0)
            def _():
                h = (_rms(a_ref[...]) * wn_ref[...]).astype(ACT_DTYPE)
                h_scr[...] = h
                h_ref[...] = h

            av = h_scr[...]
        elif a_norm_bwd is not None:
            y_ref, wy_ref = extra.pop(0), extra.pop(0)
            dy_ref, dwy_ref, dy_scr = outs.pop(0), outs.pop(0), outs.pop()
            first_rows = pl.program_id(0) == 0

            @pl.when(pl.program_id(1) == 0)
            def _():
                yv, up = y_ref[...], a_ref[...]
                rstd = lax.rsqrt(jnp.mean(yv * yv, axis=-1, keepdims=True) + NORM_EPS)
                yhat = yv * rstd
                g = up * wy_ref[...]
                dy = (rstd * (g - yhat * jnp.mean(g * yhat, axis=-1, keepdims=True))).astype(ACT_DTYPE)
                dy_scr[...] = dy
                dy_ref[...] = dy

                @pl.when(first_rows)
                def _():
                    dwy_ref[...] = jnp.zeros_like(dwy_ref)

                dwy_ref[...] += jnp.sum(up * yhat, axis=0, keepdims=True)

            av = dy_scr[...]
        else:
            av = a_ref[...]
        if a_sqrelu:
            av = jnp.square(jnp.maximum(av.astype(F32), 0.0))
        acc = _dot(av, b_ref[...], 0 if ta else 1, 1 if nt else 0)
        if u_ref is not None:
            acc = acc * (2.0 * jnp.maximum(u_ref[...].astype(F32), 0.0))
        if norm_bwd is not None:
            x_ref, wb_ref, add_ref = extra
            dw_ref = outs[0]
            xv = x_ref[...]
            rstd = lax.rsqrt(jnp.mean(xv * xv, axis=-1, keepdims=True) + NORM_EPS)
            xhat = xv * rstd
            g = acc * wb_ref[...]
            o_ref[...] = rstd * (g - xhat * jnp.mean(g * xhat, axis=-1, keepdims=True)) + add_ref[...]

            @pl.when(pl.program_id(0) == 0)
            def _():
                dw_ref[...] = jnp.zeros_like(dw_ref)

            dw_ref[...] += jnp.sum(acc * xhat, axis=0, keepdims=True)
            return
        o_ref[...] = acc.astype(out_dtype)
        if res_norm is not None:
            res_ref, wr_ref = extra[:2]
            z = res_ref[...] + _rms(acc) * wr_ref[...]
            if loss_of is None:
                outs[0][...] = z
                return
            diff = z - extra[2][...]
            outs[0][...] = diff * (1.0 / n)
            l_ref = outs[1]

            @pl.when(pl.program_id(0) == 0)
            def _():
                l_ref[...] = jnp.zeros_like(l_ref)

            l_ref[...] += 0.5 * jnp.sum(jnp.mean(diff * diff, axis=-1, keepdims=True), axis=0, keepdims=True)

    if b_blocked:
        b_spec = pl.BlockSpec((None, k, tn), lambda i, j: (j, 0, 0))
    elif nt:
        b_spec = pl.BlockSpec((tn, k), lambda i, j: (j, 0))
    else:
        b_spec = pl.BlockSpec((k, tn), lambda i, j: (0, j))
    a_spec = pl.BlockSpec((k, tm), lambda i, j: (0, i)) if ta else pl.BlockSpec((tm, k), lambda i, j: (i, 0))
    in_specs = [a_spec, b_spec]
    args = [a, b]
    if drelu_of is not None:
        in_specs.append(pl.BlockSpec((tm, tn), lambda i, j: (i, j)))
        args.append(drelu_of)
    if out_blocked:
        out_specs = [pl.BlockSpec((None, tm, tn), lambda i, j: (j, i, 0))]
        out_shape = [jax.ShapeDtypeStruct((n // tn, m, tn), out_dtype)]
    else:
        out_specs = [pl.BlockSpec((tm, tn), lambda i, j: (i, j))]
        out_shape = [jax.ShapeDtypeStruct((m, n), out_dtype)]
    scratch = []
    if a_norm is not None:
        assert not ta
        in_specs.append(pl.BlockSpec((1, k), lambda i, j: (0, 0)))
        args.append(a_norm)
        out_specs.append(pl.BlockSpec((tm, k), lambda i, j: (i, 0)))
        out_shape.append(jax.ShapeDtypeStruct((m, k), ACT_DTYPE))
        scratch.append(pltpu.VMEM((tm, k), ACT_DTYPE))
    if a_norm_bwd is not None:
        assert not ta
        in_specs += [pl.BlockSpec((tm, k), lambda i, j: (i, 0)), pl.BlockSpec((1, k), lambda i, j: (0, 0))]
        args += list(a_norm_bwd)
        out_specs += [pl.BlockSpec((tm, k), lambda i, j: (i, 0)), pl.BlockSpec((1, k), lambda i, j: (0, 0))]
        out_shape += [jax.ShapeDtypeStruct((m, k), ACT_DTYPE), jax.ShapeDtypeStruct((1, k), F32)]
        scratch.append(pltpu.VMEM((tm, k), ACT_DTYPE))
    if res_norm is not None:
        in_specs += [pl.BlockSpec((tm, n), lambda i, j: (i, 0)), pl.BlockSpec((1, n), lambda i, j: (0, 0))]
        args += list(res_norm)
        out_specs.append(pl.BlockSpec((tm, n), lambda i, j: (i, 0)))
        out_shape.append(jax.ShapeDtypeStruct((m, n), F32))
    if loss_of is not None:
        in_specs.append(pl.BlockSpec((tm, n), lambda i, j: (i, 0)))
        args.append(loss_of)
        out_specs.append(pl.BlockSpec((8, LANES), lambda i, j: (0, 0)))
        out_shape.append(jax.ShapeDtypeStruct((8, LANES), F32))
    if norm_bwd is not None:
        rows = pl.BlockSpec((tm, n), lambda i, j: (i, 0))
        in_specs += [rows, pl.BlockSpec((1, n), lambda i, j: (0, 0)), rows]
        args += list(norm_bwd)
        out_specs.append(pl.BlockSpec((1, n), lambda i, j: (0, 0)))
        out_shape.append(jax.ShapeDtypeStruct((1, n), F32))
    single = len(out_shape) == 1
    return _pcall(body, ride, grid=(m // tm, n // tn), in_specs=in_specs,
                  out_specs=out_specs[0] if single else out_specs, out_shape=out_shape[0] if single else out_shape,
                  scratch_shapes=scratch, semantics=("arbitrary", "arbitrary"), name=name)(*args)


ROW_TILE = 512
TM_FWD, TM_DX, TM_DW, TN = 2048, 1024, 1024, 512


def _norm_bwd(dy, x, w, *, out_dtype, add=None, name, ride=None):
    t, d = x.shape

    def body(*refs):
        dy_ref, x_ref, w_ref = refs[0], refs[1], refs[2]
        dx_ref, dw_ref = refs[-2], refs[-1]
        xv = x_ref[...]
        rstd = lax.rsqrt(jnp.mean(xv * xv, axis=-1, keepdims=True) + NORM_EPS)
        xhat = xv * rstd
        dyv = dy_ref[...].astype(F32)
        g = dyv * w_ref[...]
        dx = rstd * (g - xhat * jnp.mean(g * xhat, axis=-1, keepdims=True))
        if add is not None:
            dx = dx + refs[3][...]
        dx_ref[...] = dx.astype(out_dtype)

        @pl.when(pl.program_id(0) == 0)
        def _():
            dw_ref[...] = jnp.zeros_like(dw_ref)

        dw_ref[...] += jnp.sum(dyv * xhat, axis=0, keepdims=True)

    row = pl.BlockSpec((ROW_TILE, d), lambda i: (i, 0))
    vec = pl.BlockSpec((1, d), lambda i: (0, 0))
    in_specs = [row, row, vec] + ([row] if add is not None else [])
    args = [dy, x, w] + ([add] if add is not None else [])
    return _pcall(body, ride, grid=(t // ROW_TILE,), in_specs=in_specs, out_specs=[row, vec],
                  out_shape=[jax.ShapeDtypeStruct((t, d), out_dtype), jax.ShapeDtypeStruct((1, d), F32)],
                  semantics=("arbitrary",), name=name)(*args)


GLA_STATE = (GLA_HEADS * GLA_DV, GLA_KW)


def _gla_specs(chunk_of):
    rows = lambda width, col: pl.BlockSpec((CHUNK, width), lambda i: (chunk_of(i), col))
    const = lambda r, c: pl.BlockSpec((r, c), lambda i: (0, 0))
    return [rows(GLA_KW, 0),
            rows(GLA_KW, 1),
            rows(GROUP_WIDTH, 1),
            rows(GROUP_WIDTH, 0),
            rows(LANES, 4),
            const(LANES, GLA_KW),
            const(1, GLA_KW),
            const(1, GROUP_WIDTH)]


def _gla_chunk(q_ref, k_ref, v_ref, a_ref, wup_ref, ba_ref):
    z = _dot(a_ref[...], wup_ref[...]) + ba_ref[...]
    tri = (_iota((CHUNK, CHUNK), 1) <= _iota((CHUNK, CHUNK), 0)).astype(F32)
    cum = _dot_exact(tri, _log_sigmoid(z) * (1.0 / GLA_GATE_TAU))
    tot = cum[CHUNK - 1:CHUNK, :]
    e = jnp.exp(tot - cum)
    return (z, e, jnp.exp(tot), k_ref[...].astype(F32) * e, q_ref[...].astype(F32) * (GLA_DK ** -0.5),
            v_ref[...].astype(F32))


def _gla_head_mask():
    return _iota(GLA_STATE, 0) // GLA_DV == _iota(GLA_STATE, 1) // GLA_DK


def _gla_fwd(pmm, pel, w_up, b_a, gnorm_w, ride=None):
    t = pmm.shape[0]
    nc = t // CHUNK

    def body(q_ref, k_ref, v_ref, r_ref, a_ref, wup_ref, ba_ref, gw_ref, o_ref, st_ref, m_scr):
        @pl.when(pl.program_id(0) == 0)
        def _():
            m_scr[...] = jnp.zeros_like(m_scr)

        _, _, decay, kd, qs, vv = _gla_chunk(q_ref, k_ref, v_ref, a_ref, wup_ref, ba_ref)
        m = m_scr[...] * decay + jnp.where(_gla_head_mask(), _dot(vv, kd, 0, 0), 0.0)
        m_scr[...] = m
        st_ref[...] = m
        o = _dot(qs, m, 1, 1)
        rr = r_ref[...]
        gate = rr * jax.nn.sigmoid(rr) * gw_ref[...]
        for h in range(GLA_HEADS):
            vs = slice(h * GLA_DV, (h + 1) * GLA_DV)
            oh = o[:, vs]
            y = oh * lax.rsqrt(jnp.mean(oh * oh, axis=-1, keepdims=True) + NORM_EPS)
            o_ref[:, vs] = (y * gate[:, vs]).astype(o_ref.dtype)

    return _pcall(
        body, ride, grid=(nc,), in_specs=_gla_specs(lambda i: i),
        out_specs=[pl.BlockSpec((CHUNK, GROUP_WIDTH), lambda i: (i, 0)),
                   pl.BlockSpec((None,) + GLA_STATE, lambda i: (i, 0, 0))],
        out_shape=[jax.ShapeDtypeStruct((t, GROUP_WIDTH), ACT_DTYPE), jax.ShapeDtypeStruct((nc,) + GLA_STATE, F32)],
        scratch_shapes=[pltpu.VMEM(GLA_STATE, F32)],
        semantics=("arbitrary",), name="gla_fwd")(pmm, pmm, pmm, pel, pel, w_up, b_a, gnorm_w)


def _gla_bwd(pmm, pel, w_up, b_a, gnorm_w, states, dmix, ride=None):
    t = pmm.shape[0]
    nc = t // CHUNK
    scale = GLA_DK ** -0.5

    def body(q_ref, k_ref, v_ref, r_ref, a_ref, wup_ref, ba_ref, gw_ref, st_ref, prev_ref, do_ref,
             dq_ref, dk_ref, dv_ref, dr_ref, da_ref, dwup_ref, dba_ref, dgw_ref, dm_scr):
        step = pl.program_id(0)

        @pl.when(step == 0)
        def _():
            dm_scr[...] = jnp.zeros_like(dm_scr)
            dwup_ref[...] = jnp.zeros_like(dwup_ref)
            dba_ref[...] = jnp.zeros_like(dba_ref)
            dgw_ref[...] = jnp.zeros_like(dgw_ref)

        z, e, decay, kd, qs, vv = _gla_chunk(q_ref, k_ref, v_ref, a_ref, wup_ref, ba_ref)
        m = st_ref[...]
        m_prev = prev_ref[...] * (step < nc - 1).astype(F32)
        rr, dout, gw = r_ref[...], do_ref[...], gw_ref[...]
        sig = jax.nn.sigmoid(rr)
        silu = rr * sig
        dsilu = sig * (1.0 + rr * (1.0 - sig))
        o = _dot(qs, m, 1, 1)
        d_o, dgw = [], []
        for h in range(GLA_HEADS):
            vs = slice(h * GLA_DV, (h + 1) * GLA_DV)
            oh, dg = o[:, vs], dout[:, vs]
            rstd = lax.rsqrt(jnp.mean(oh * oh, axis=-1, keepdims=True) + NORM_EPS)
            y = oh * rstd
            dgw.append(jnp.sum(dg * y * silu[:, vs], axis=0, keepdims=True))
            dr_ref[:, vs] = (dg * y * gw[:, vs] * dsilu[:, vs]).astype(dr_ref.dtype)
            dy = dg * gw[:, vs] * silu[:, vs]
            d_o.append(rstd * (dy - y * jnp.mean(dy * y, axis=-1, keepdims=True)))
        d_o = jnp.concatenate(d_o, axis=1)
        dgw_ref[...] += jnp.concatenate(dgw, axis=1)
        dq_ref[...] = (_dot(d_o, m) * scale).astype(dq_ref.dtype)
        dm = dm_scr[...] + jnp.where(_gla_head_mask(), _dot(d_o, qs, 0, 0), 0.0)
        dv_ref[...] = _dot(kd, dm, 1, 1).astype(dv_ref.dtype)
        dkd = _dot(vv, dm)
        dk_ref[...] = (dkd * e).astype(dk_ref.dtype)
        dm_scr[...] = dm * decay
        tri_strict = (_iota((CHUNK, CHUNK), 1) < _iota((CHUNK, CHUNK), 0)).astype(F32)
        dla = jnp.sum(dm * m_prev, axis=0, keepdims=True) * decay + _dot_exact(tri_strict, dkd * kd)
        dz = dla * jax.nn.sigmoid(-z) * (1.0 / GLA_GATE_TAU)
        da_ref[...] = _dot(dz, wup_ref[...], 1, 1).astype(da_ref.dtype)
        dwup_ref[...] += _dot(a_ref[...], dz, 0, 0)
        dba_ref[...] += jnp.sum(dz, axis=0, keepdims=True)

    chunk_of = lambda i: nc - 1 - i
    in_specs = _gla_specs(chunk_of) + [
        pl.BlockSpec((None,) + GLA_STATE, lambda i: (chunk_of(i), 0, 0)),
        pl.BlockSpec((None,) + GLA_STATE, lambda i: (jnp.maximum(chunk_of(i) - 1, 0), 0, 0)),
        pl.BlockSpec((CHUNK, GROUP_WIDTH), lambda i: (chunk_of(i), 0))]
    rows = lambda width: pl.BlockSpec((CHUNK, width), lambda i: (chunk_of(i), 0))
    const = lambda r, c: pl.BlockSpec((r, c), lambda i: (0, 0))
    return _pcall(
        body, ride, grid=(nc,), in_specs=in_specs,
        out_specs=[rows(GLA_KW), rows(GLA_KW), rows(GROUP_WIDTH), rows(GROUP_WIDTH), rows(LANES),
                   const(LANES, GLA_KW), const(1, GLA_KW), const(1, GROUP_WIDTH)],
        out_shape=[jax.ShapeDtypeStruct((t, GLA_KW), ACT_DTYPE), jax.ShapeDtypeStruct((t, GLA_KW), ACT_DTYPE),
                   jax.ShapeDtypeStruct((t, GROUP_WIDTH), ACT_DTYPE), jax.ShapeDtypeStruct((t, GROUP_WIDTH), ACT_DTYPE),
                   jax.ShapeDtypeStruct((t, LANES), ACT_DTYPE), jax.ShapeDtypeStruct((LANES, GLA_KW), F32),
                   jax.ShapeDtypeStruct((1, GLA_KW), F32), jax.ShapeDtypeStruct((1, GROUP_WIDTH), F32)],
        scratch_shapes=[pltpu.VMEM(GLA_STATE, F32)],
        semantics=("arbitrary",), name="gla_bwd")(
            pmm, pmm, pmm, pel, pel, w_up, b_a, gnorm_w, states, states, dmix)


CUM_BLOCK = 256


def _fox_gate_fwd(pel, b_f):
    t = pel.shape[0]
    nb = t // CUM_BLOCK

    def body(f_ref, b_ref, cum_ref, cum_t_ref):
        tri = (_iota((CUM_BLOCK, CUM_BLOCK), 1) <= _iota((CUM_BLOCK, CUM_BLOCK), 0)).astype(F32)
        carry = jnp.zeros((1, LANES), F32)
        for blk in range(nb):
            rows = slice(blk * CUM_BLOCK, (blk + 1) * CUM_BLOCK)
            cum = _dot_exact(tri, _log_sigmoid(f_ref[rows, :] + b_ref[...])) + carry
            cum_ref[rows, :] = cum
            cum_t_ref[blk] = cum.T[:ATT_HEADS, :]
            carry = cum[CUM_BLOCK - 1:CUM_BLOCK, :]

    return pl.pallas_call(
        body, grid=(1,),
        in_specs=[pl.BlockSpec((t, LANES), lambda i: (0, 5)), pl.BlockSpec((1, LANES), lambda i: (0, 0))],
        out_specs=[pl.BlockSpec((t, LANES), lambda i: (0, 0)),
                   pl.BlockSpec((nb, ATT_HEADS, CUM_BLOCK), lambda i: (0, 0, 0))],
        out_shape=[jax.ShapeDtypeStruct((t, LANES), F32), jax.ShapeDtypeStruct((nb, ATT_HEADS, CUM_BLOCK), F32)],
        compiler_params=_params("arbitrary"), name="fox_gate_fwd")(pel, b_f)


def _fox_gate_bwd(pel, b_f, dcum_t, dcum_q):
    t = pel.shape[0]
    nb = t // CUM_BLOCK

    def body(f_ref, b_ref, dct_ref, dcq_ref, df_ref, db_ref):
        tri_up = (_iota((CUM_BLOCK, CUM_BLOCK), 1) >= _iota((CUM_BLOCK, CUM_BLOCK), 0)).astype(F32)
        carry = jnp.zeros((1, LANES), F32)
        db = jnp.zeros((1, LANES), F32)
        for blk in reversed(range(nb)):
            rows = slice(blk * CUM_BLOCK, (blk + 1) * CUM_BLOCK)
            query_side = sum(dcq_ref[pair, rows, :] for pair in range(dcq_ref.shape[0]))
            dls = _dot_exact(tri_up, dct_ref[blk].T + query_side) + carry
            carry = dls[0:1, :]
            df = dls * jax.nn.sigmoid(-(f_ref[rows, :] + b_ref[...]))
            df_ref[rows, :] = df.astype(df_ref.dtype)
            db = db + jnp.sum(df, axis=0, keepdims=True)
        db_ref[...] = db

    return pl.pallas_call(
        body, grid=(1,),
        in_specs=[pl.BlockSpec((t, LANES), lambda i: (0, 5)), pl.BlockSpec((1, LANES), lambda i: (0, 0)),
                  pl.BlockSpec((nb, LANES, CUM_BLOCK), lambda i: (0, 0, 0)),
                  pl.BlockSpec((dcum_q.shape[0], t, LANES), lambda i: (0, 0, 0))],
        out_specs=[pl.BlockSpec((t, LANES), lambda i: (0, 0)), pl.BlockSpec((1, LANES), lambda i: (0, 0))],
        out_shape=[jax.ShapeDtypeStruct((t, LANES), ACT_DTYPE), jax.ShapeDtypeStruct((1, LANES), F32)],
        compiler_params=_params("arbitrary"), name="fox_gate_bwd")(pel, b_f, dcum_t, dcum_q)


FOX_Q_BLOCK = 256


assert FOX_Q_BLOCK == CUM_BLOCK
FOX_KEY_STEP = 512


def _fox_scores(q_ref, k_ref, cum_ref, cum_t_ref, h, i):
    hs = slice(h * HEAD_DIM, (h + 1) * HEAD_DIM)
    nb = cum_t_ref.shape[0]
    key_gate = jnp.concatenate([cum_t_ref[kb, h:h + 1, :] for kb in range(nb)], axis=1)
    s = _dot(q_ref[:, hs], k_ref[:, hs], 1, 1) * (HEAD_DIM ** -0.5) + (cum_ref[:, h:h + 1] - key_gate)
    shape = (FOX_Q_BLOCK, nb * FOX_Q_BLOCK)
    return jnp.where(_iota(shape, 1) <= i * FOX_Q_BLOCK + _iota(shape, 0), s, NEG)


def _fox_specs(t):
    bq, nb = FOX_Q_BLOCK, t // FOX_Q_BLOCK
    return [pl.BlockSpec((bq, GROUP_WIDTH), lambda i: (i, 2)), pl.BlockSpec((t, GROUP_WIDTH), lambda i: (0, 3)),
            pl.BlockSpec((t, GROUP_WIDTH), lambda i: (0, 4)), pl.BlockSpec((bq, LANES), lambda i: (i, 0)),
            pl.BlockSpec((nb, ATT_HEADS, bq), lambda i: (0, 0, 0))]


def _fox_fwd(pmm, cum, cum_t, ride=None):
    t = pmm.shape[0]
    bq = FOX_Q_BLOCK

    def body(q_ref, k_ref, v_ref, cum_ref, cum_t_ref, o_ref, lse_ref):
        i = pl.program_id(0)
        lse_ref[...] = jnp.zeros_like(lse_ref)
        for h in range(ATT_HEADS):
            hs = slice(h * HEAD_DIM, (h + 1) * HEAD_DIM)
            s = _fox_scores(q_ref, k_ref, cum_ref, cum_t_ref, h, i)
            m = jnp.max(s, axis=-1, keepdims=True)
            p = jnp.exp(s - m)
            l = jnp.sum(p, axis=-1, keepdims=True)
            o_ref[:, hs] = (_dot(p, v_ref[:, hs]) / l).astype(o_ref.dtype)
            lse_ref[:, h:h + 1] = m + jnp.log(l)

    return _pcall(
        body, ride, grid=(t // bq,), in_specs=_fox_specs(t),
        out_specs=[pl.BlockSpec((bq, GROUP_WIDTH), lambda i: (i, 0)), pl.BlockSpec((bq, LANES), lambda i: (i, 0))],
        out_shape=[jax.ShapeDtypeStruct((t, GROUP_WIDTH), ACT_DTYPE), jax.ShapeDtypeStruct((t, LANES), F32)],
        semantics=("parallel",), name="fox_fwd")(pmm, pmm, pmm, cum, cum_t)


def _fox_bwd(pmm, cum, cum_t, lse, dmix, ride=None):
    t = pmm.shape[0]
    bq, nb = FOX_Q_BLOCK, t // FOX_Q_BLOCK
    pairs, per_pair = ATT_HEADS // 2, LANES // HEAD_DIM
    scale = HEAD_DIM ** -0.5

    def body(q_ref, k_ref, v_ref, cum_ref, cum_t_ref, lse_ref, do_ref, dq_ref, dk_ref, dv_ref, dct_ref, dcq_ref):
        g, i = pl.program_id(0), pl.program_id(1)

        @pl.when(i == 0)
        def _():
            dk_ref[...] = jnp.zeros_like(dk_ref)
            dv_ref[...] = jnp.zeros_like(dv_ref)

        @pl.when((i == 0) & (g == 0))
        def _():
            dct_ref[...] = jnp.zeros_like(dct_ref)

        lane = _iota((1, LANES), 1)

        def run(n):
            causal = _iota((bq, n), 1) <= i * bq + _iota((bq, n), 0)
            dcq = jnp.zeros((bq, LANES), F32)
            for hh in range(per_pair):
                h = per_pair * g + hh
                hs = slice(hh * HEAD_DIM, (hh + 1) * HEAD_DIM)
                pick = (lane == h).astype(F32)
                cq = jnp.sum(cum_ref[...] * pick, axis=1, keepdims=True)
                lse_h = jnp.sum(lse_ref[...] * pick, axis=1, keepdims=True)
                key_gate = jnp.concatenate([cum_t_ref[kb, pl.ds(h, 1), :] for kb in range(n // bq)], axis=1)
                s = _dot(q_ref[:, hs], k_ref[:n, hs], 1, 1) * scale + (cq - key_gate)
                p = jnp.exp(jnp.where(causal, s, NEG) - lse_h)
                do = do_ref[:, hs]
                dp = _dot(do, v_ref[:n, hs], 1, 1)
                ds = p * (dp - jnp.sum(p * dp, axis=-1, keepdims=True))
                dq_ref[:, hs] = (_dot(ds, k_ref[:n, hs]) * scale).astype(dq_ref.dtype)
                dk_ref[:n, hs] += _dot(ds, q_ref[:, hs], 0, 0) * scale
                dv_ref[:n, hs] += _dot(p, do, 0, 0)
                key_side = -jnp.sum(ds, axis=0, keepdims=True)
                for kb in range(n // bq):
                    dct_ref[kb, pl.ds(h, 1), :] += key_side[:, kb * bq:(kb + 1) * bq]
                dcq = dcq + jnp.sum(ds, axis=1, keepdims=True) * pick
            dcq_ref[...] = dcq

        for kx in range(t // FOX_KEY_STEP):
            pl.when(i // (FOX_KEY_STEP // bq) == kx)(functools.partial(run, (kx + 1) * FOX_KEY_STEP))

    cols = lambda first: pl.BlockSpec((bq, LANES), lambda g, i: (i, first + g))
    keys = lambda first: pl.BlockSpec((t, LANES), lambda g, i: (0, first + g))
    per_head = pl.BlockSpec((bq, LANES), lambda g, i: (i, 0))
    fox_q, fox_k, fox_v = (GROUP_WIDTH * n // LANES for n in (2, 3, 4))
    return _pcall(
        body, ride, grid=(pairs, t // bq),
        in_specs=[cols(fox_q), keys(fox_k), keys(fox_v), per_head,
                  pl.BlockSpec((nb, ATT_HEADS, bq), lambda g, i: (0, 0, 0)), per_head, cols(GROUP_WIDTH // LANES)],
        out_specs=[cols(0), keys(0), keys(0), pl.BlockSpec((nb, LANES, bq), lambda g, i: (0, 0, 0)),
                   pl.BlockSpec((None, bq, LANES), lambda g, i: (g, i, 0))],
        out_shape=[jax.ShapeDtypeStruct((t, GROUP_WIDTH), ACT_DTYPE), jax.ShapeDtypeStruct((t, GROUP_WIDTH), F32),
                   jax.ShapeDtypeStruct((t, GROUP_WIDTH), F32), jax.ShapeDtypeStruct((nb, LANES, bq), F32),
                   jax.ShapeDtypeStruct((pairs, t, LANES), F32)],
        semantics=("arbitrary", "arbitrary"), name="fox_bwd")(pmm, pmm, pmm, cum, cum_t, lse, dmix)


CA_Q_BLOCK = 4 * CHUNK
CA_WINDOW = CA_Q_BLOCK + CA_LEFT
CA_BASE = 1024


def _ca_bias_base(rel_bias):
    n = rel_bias.shape[0]
    flat = CA_Q_BLOCK + CA_LEFT - REL_CLIP
    tail = CA_BASE - flat - (2 * REL_CLIP + 1)
    return jnp.concatenate([jnp.broadcast_to(rel_bias[:, 2 * REL_CLIP:], (n, flat)), rel_bias[:, ::-1],
                            jnp.broadcast_to(rel_bias[:, :1], (n, tail))], axis=1)


def _ca_bias_base_grad(dbase):
    flat = CA_Q_BLOCK + CA_LEFT - REL_CLIP
    mid = dbase[:, flat:flat + 2 * REL_CLIP + 1][:, ::-1]
    lo = jnp.sum(dbase[:, flat + 2 * REL_CLIP + 1:], axis=1, keepdims=True)
    hi = jnp.sum(dbase[:, :flat], axis=1, keepdims=True)
    pad = jnp.zeros((dbase.shape[0], 2 * REL_CLIP - 1), F32)
    return mid + jnp.concatenate([lo, pad, hi], axis=1)


def _ca_mask(i):
    r, j = _iota((CA_Q_BLOCK, CA_WINDOW), 0), _iota((CA_Q_BLOCK, CA_WINDOW), 1)
    rc, jc = r // CHUNK, j // CHUNK
    return (jc >= rc) & (jc <= rc + CA_LEFT // CHUNK) & (i * CA_Q_BLOCK + j >= CA_LEFT)


def _ca_fill_bias(i, base_ref, bias_scr):
    @pl.when(i == 0)
    def _():
        for h in range(ATT_HEADS):
            rows = jnp.broadcast_to(base_ref[h:h + 1, :], (CA_Q_BLOCK, CA_BASE))
            bias_scr[h] = pltpu.roll(rows, CA_BASE - CA_Q_BLOCK, 1, stride=1, stride_axis=0)[:, :CA_WINDOW]


def _ca_scores(q_ref, kp_ref, bias_scr, win, h, mask):
    hs = slice(h * HEAD_DIM, (h + 1) * HEAD_DIM)
    s = _dot(q_ref[:, hs], kp_ref[win, hs], 1, 1) * (HEAD_DIM ** -0.5)
    return jnp.where(mask, s + bias_scr[h], NEG)


CA_BIAS_SCRATCH = pltpu.VMEM((ATT_HEADS, CA_Q_BLOCK, CA_WINDOW), F32)


def _ca_fwd(pmm, kp, vp, base, ride=None):
    t = pmm.shape[0]

    def body(q_ref, kp_ref, vp_ref, base_ref, o_ref, lse_ref, bias_scr):
        i = pl.program_id(0)
        _ca_fill_bias(i, base_ref, bias_scr)
        win = pl.ds(pl.multiple_of(i * CA_Q_BLOCK, CA_Q_BLOCK), CA_WINDOW)
        mask = _ca_mask(i)
        lse_ref[...] = jnp.zeros_like(lse_ref)
        for h in range(ATT_HEADS):
            hs = slice(h * HEAD_DIM, (h + 1) * HEAD_DIM)
            s = _ca_scores(q_ref, kp_ref, bias_scr, win, h, mask)
            m = jnp.max(s, axis=-1, keepdims=True)
            p = jnp.exp(s - m)
            l = jnp.sum(p, axis=-1, keepdims=True)
            o_ref[:, hs] = (_dot(p, vp_ref[win, hs]) / l).astype(o_ref.dtype)
            lse_ref[:, h:h + 1] = m + jnp.log(l)

    padded = pl.BlockSpec((t + CA_LEFT, GROUP_WIDTH), lambda i: (0, 0))
    return _pcall(
        body, ride, grid=(t // CA_Q_BLOCK,),
        in_specs=[pl.BlockSpec((CA_Q_BLOCK, GROUP_WIDTH), lambda i: (i, 0)), padded, padded,
                  pl.BlockSpec((ATT_HEADS, CA_BASE), lambda i: (0, 0))],
        out_specs=[pl.BlockSpec((CA_Q_BLOCK, GROUP_WIDTH), lambda i: (i, 0)),
                   pl.BlockSpec((CA_Q_BLOCK, LANES), lambda i: (i, 0))],
        out_shape=[jax.ShapeDtypeStruct((t, GROUP_WIDTH), ACT_DTYPE), jax.ShapeDtypeStruct((t, LANES), F32)],
        scratch_shapes=[CA_BIAS_SCRATCH], semantics=("arbitrary",), name="ca_fwd")(pmm, kp, vp, base)


def _ca_bwd(pmm, kp, vp, base, lse, dmix, ride=None):
    t = pmm.shape[0]
    scale = HEAD_DIM ** -0.5

    def body(q_ref, kp_ref, vp_ref, base_ref, lse_ref, do_ref, dq_ref, dkp_ref, dvp_ref, dbase_ref, bias_scr):
        i = pl.program_id(0)
        _ca_fill_bias(i, base_ref, bias_scr)

        @pl.when(i == 0)
        def _():
            dkp_ref[...] = jnp.zeros_like(dkp_ref)
            dvp_ref[...] = jnp.zeros_like(dvp_ref)
            dbase_ref[...] = jnp.zeros_like(dbase_ref)

        win = pl.ds(pl.multiple_of(i * CA_Q_BLOCK, CA_Q_BLOCK), CA_WINDOW)
        mask = _ca_mask(i)
        flip = (_iota((CA_Q_BLOCK, CA_Q_BLOCK), 0) + _iota((CA_Q_BLOCK, CA_Q_BLOCK), 1) == CA_Q_BLOCK - 1).astype(F32)
        for h in range(ATT_HEADS):
            hs = slice(h * HEAD_DIM, (h + 1) * HEAD_DIM)
            s = _ca_scores(q_ref, kp_ref, bias_scr, win, h, mask)
            p = jnp.exp(s - lse_ref[:, h:h + 1])
            do = do_ref[:, hs]
            dp = _dot(do, vp_ref[win, hs], 1, 1)
            ds = p * (dp - jnp.sum(p * dp, axis=-1, keepdims=True))
            dq_ref[:, hs] = (_dot(ds, kp_ref[win, hs]) * scale).astype(dq_ref.dtype)
            dkp_ref[win, hs] += _dot(ds, q_ref[:, hs], 0, 0) * scale
            dvp_ref[win, hs] += _dot(p, do, 0, 0)
            rev = jnp.concatenate([_dot(flip, ds), jnp.zeros((CA_Q_BLOCK, CA_BASE - CA_WINDOW), F32)], axis=1)
            lined = pltpu.roll(rev, 1, 1, stride=1, stride_axis=0)
            dbase_ref[h:h + 1, :] += jnp.sum(lined, axis=0, keepdims=True)

    padded = pl.BlockSpec((t + CA_LEFT, GROUP_WIDTH), lambda i: (0, 0))
    return _pcall(
        body, ride, grid=(t // CA_Q_BLOCK,),
        in_specs=[pl.BlockSpec((CA_Q_BLOCK, GROUP_WIDTH), lambda i: (i, 0)), padded, padded,
                  pl.BlockSpec((ATT_HEADS, CA_BASE), lambda i: (0, 0)),
                  pl.BlockSpec((CA_Q_BLOCK, LANES), lambda i: (i, 0)),
                  pl.BlockSpec((CA_Q_BLOCK, GROUP_WIDTH), lambda i: (i, 0))],
        out_specs=[pl.BlockSpec((CA_Q_BLOCK, GROUP_WIDTH), lambda i: (i, 0)), padded, padded,
                   pl.BlockSpec((ATT_HEADS, CA_BASE), lambda i: (0, 0))],
        out_shape=[jax.ShapeDtypeStruct((t, GROUP_WIDTH), ACT_DTYPE),
                   jax.ShapeDtypeStruct((t + CA_LEFT, GROUP_WIDTH), F32),
                   jax.ShapeDtypeStruct((t + CA_LEFT, GROUP_WIDTH), F32),
                   jax.ShapeDtypeStruct((ATT_HEADS, CA_BASE), F32)],
        scratch_shapes=[CA_BIAS_SCRATCH], semantics=("arbitrary",), name="ca_bwd")(pmm, kp, vp, base, lse, dmix)


GELU_C = 0.7978845608028654
GELU_A = 0.044715


def _shift_down(v, k, fill, period=None):
    rows = _iota(v.shape, 0)
    rows = rows if period is None else rows & (period - 1)
    return jnp.where(rows >= k, pltpu.roll(v, k, 0), fill)


def _shift_up(v, k, fill, period=None):
    t = v.shape[0]
    rows = _iota(v.shape, 0)
    rows, length = (rows, t) if period is None else (rows & (period - 1), period)
    return jnp.where(rows < length - k, pltpu.roll(v, t - k, 0), fill)


LRU_SCAN_BLOCK = 256


def _linear_scan(a, b, reverse=False):
    shift = _shift_up if reverse else _shift_down
    k = 1
    while k < LRU_SCAN_BLOCK:
        b = a * shift(b, k, 0.0, LRU_SCAN_BLOCK) + b
        a = a * shift(a, k, 1.0, LRU_SCAN_BLOCK)
        k *= 2
    nb = a.shape[0] // LRU_SCAN_BLOCK
    carry = jnp.zeros((1, a.shape[1]), F32)
    out = [None] * nb
    for blk in (reversed(range(nb)) if reverse else range(nb)):
        rows = slice(blk * LRU_SCAN_BLOCK, (blk + 1) * LRU_SCAN_BLOCK)
        h = b[rows] + a[rows] * carry
        out[blk] = h
        carry = h[0:1] if reverse else h[LRU_SCAN_BLOCK - 1:LRU_SCAN_BLOCK]
    return jnp.concatenate(out, axis=0)


def _neg_expm1(y):
    series = -y * (1.0 + y * (0.5 + y * (1.0 / 6.0 + y * (1.0 / 24.0 + y * (1.0 / 120.0)))))
    return jnp.where(y > -0.1, series, 1.0 - jnp.exp(y))


def _lru_forward(x, g_in, cw, cb, wa, ba, wx, bx, lam):
    xs = [_shift_down(x, CONV_WIDTH - 1 - j, 0.0) for j in range(CONV_WIDTH - 1)] + [x]
    xc = cb + sum(cw[j:j + 1, :] * xs[j] for j in range(CONV_WIDTH))
    r = jax.nn.sigmoid(_dot(xc, wa) + ba)
    i = jax.nn.sigmoid(_dot(xc, wx) + bx)
    lsl = _log_sigmoid(lam)
    la = LRU_C * r * lsl
    a = jnp.exp(la)
    s = jnp.sqrt(_neg_expm1(2.0 * la))
    h = _linear_scan(a, s * (i * xc))
    u = GELU_C * (g_in + GELU_A * g_in * g_in * g_in)
    th = jnp.tanh(u)
    gelu = 0.5 * g_in * (1.0 + th)
    return xs, xc, r, i, lsl, a, s, h, th, gelu


def _lru_specs(t):
    col = lambda off: pl.BlockSpec((t, LANES), lambda j: (0, j + off))
    vec = pl.BlockSpec((1, LANES), lambda j: (0, j))
    mat = pl.BlockSpec((None, LANES, LANES), lambda j: (j, 0, 0))
    return [col(0), col(GROUP_WIDTH // LANES), pl.BlockSpec((CONV_WIDTH, LANES), lambda j: (0, j)),
            vec, mat, vec, mat, vec, vec]


def _lru_fwd(pel, conv_w, conv_b, wa, ba, wx, bx, lam, ride=None):
    t = pel.shape[0]

    def body(g_ref, x_ref, cw_ref, cb_ref, wa_ref, ba_ref, wx_ref, bx_ref, lam_ref, o_ref):
        res = _lru_forward(x_ref[...], g_ref[...], cw_ref[...], cb_ref[...], wa_ref[...], ba_ref[...],
                           wx_ref[...], bx_ref[...], lam_ref[...])
        o_ref[...] = (res[7] * res[9]).astype(o_ref.dtype)

    return _pcall(
        body, ride, grid=(GROUP_WIDTH // LANES,), in_specs=_lru_specs(t),
        out_specs=pl.BlockSpec((t, LANES), lambda j: (0, j)),
        out_shape=jax.ShapeDtypeStruct((t, GROUP_WIDTH), ACT_DTYPE),
        semantics=("parallel",), name="lru_fwd")(pel, pel, conv_w, conv_b, wa, ba, wx, bx, lam)


def _lru_bwd(pel, conv_w, conv_b, wa, ba, wx, bx, lam, dmix, ride=None):
    t = pel.shape[0]

    def body(g_ref, x_ref, cw_ref, cb_ref, wa_ref, ba_ref, wx_ref, bx_ref, lam_ref, do_ref,
             dg_ref, dx_ref, dcw_ref, dcb_ref, dwa_ref, dba_ref, dwx_ref, dbx_ref, dlam_ref):
        g_in, cw, lam = g_ref[...], cw_ref[...], lam_ref[...]
        xs, xc, r, i, lsl, a, s, h, th, gelu = _lru_forward(
            x_ref[...], g_in, cw, cb_ref[...], wa_ref[...], ba_ref[...], wx_ref[...], bx_ref[...], lam)
        dout = do_ref[...]
        dgelu = 0.5 * (1.0 + th) + 0.5 * g_in * (1.0 - th * th) * GELU_C * (1.0 + 3.0 * GELU_A * g_in * g_in)
        dg_ref[...] = (dout * h * dgelu).astype(dg_ref.dtype)
        gsum = _linear_scan(_shift_up(a, 1, 0.0), dout * gelu, reverse=True)
        da = gsum * _shift_down(h, 1, 0.0)
        di = gsum * s * xc
        dla = da * a - gsum * (i * xc) * (a * a / s)
        dlam_ref[...] = jnp.sum(dla * (LRU_C * r), axis=0, keepdims=True) * jax.nn.sigmoid(-lam)
        dpr = dla * (LRU_C * lsl) * r * (1.0 - r)
        dpi = di * i * (1.0 - i)
        dxc = gsum * s * i + _dot(dpr, wa_ref[...], 1, 1) + _dot(dpi, wx_ref[...], 1, 1)
        xct = xc.T
        dwa_ref[...] = _dot(xct, dpr)
        dwx_ref[...] = _dot(xct, dpi)
        dba_ref[...] = jnp.sum(dpr, axis=0, keepdims=True)
        dbx_ref[...] = jnp.sum(dpi, axis=0, keepdims=True)
        dcb_ref[...] = jnp.sum(dxc, axis=0, keepdims=True)
        for j in range(CONV_WIDTH):
            dcw_ref[j:j + 1, :] = jnp.sum(dxc * xs[j], axis=0, keepdims=True)
        dx = cw[CONV_WIDTH - 1:CONV_WIDTH, :] * dxc
        for j in range(CONV_WIDTH - 1):
            dx = dx + cw[j:j + 1, :] * _shift_up(dxc, CONV_WIDTH - 1 - j, 0.0)
        dx_ref[...] = dx.astype(dx_ref.dtype)

    col = pl.BlockSpec((t, LANES), lambda j: (0, j))
    vec = pl.BlockSpec((1, LANES), lambda j: (0, j))
    mat = pl.BlockSpec((None, LANES, LANES), lambda j: (j, 0, 0))
    nb = GROUP_WIDTH // LANES
    vshape = jax.ShapeDtypeStruct((1, GROUP_WIDTH), F32)
    mshape = jax.ShapeDtypeStruct((nb, LANES, LANES), F32)
    return _pcall(
        body, ride, grid=(nb,),
        in_specs=_lru_specs(t) + [pl.BlockSpec((t, LANES), lambda j: (0, j + nb))],
        out_specs=[col, col, pl.BlockSpec((CONV_WIDTH, LANES), lambda j: (0, j)), vec, mat, vec, mat, vec, vec],
        out_shape=[jax.ShapeDtypeStruct((t, GROUP_WIDTH), ACT_DTYPE), jax.ShapeDtypeStruct((t, GROUP_WIDTH), ACT_DTYPE),
                   jax.ShapeDtypeStruct((CONV_WIDTH, GROUP_WIDTH), F32), vshape, mshape, vshape, mshape, vshape, vshape],
        semantics=("parallel",), name="lru_bwd")(
            pel, pel, conv_w, conv_b, wa, ba, wx, bx, lam, dmix)


def _block_diag_pairs(w):
    z = jnp.zeros((LRU_BLOCK_DIM, LRU_BLOCK_DIM), w.dtype)
    return jnp.stack([jnp.block([[w[2 * j], z], [z, w[2 * j + 1]]]) for j in range(w.shape[0] // 2)])


def _block_diag_pairs_grad(dw):
    b = LRU_BLOCK_DIM
    return jnp.stack([dw[n // 2, (n % 2) * b:(n % 2 + 1) * b, (n % 2) * b:(n % 2 + 1) * b] for n in range(2 * dw.shape[0])])


def _row_tile(r):
    return ROW_TILE if r % ROW_TILE == 0 else r


def _pair_sum(g, got, place, name):
    _, r, c = g.shape
    tile = r

    def body(place_ref, a_ref, b_ref, o_ref):
        o_ref[...] = (a_ref[...].astype(F32) + b_ref[...].astype(F32)).astype(o_ref.dtype)

    blk = pl.BlockSpec((1, tile, c), lambda k, i, place_ref: (k, i, 0))
    return pl.pallas_call(
        body,
        grid_spec=pltpu.PrefetchScalarGridSpec(
            num_scalar_prefetch=1, grid=(N_CHIPS, r // tile),
            in_specs=[pl.BlockSpec((1, tile, c), lambda k, i, place_ref: (2 * k + place_ref[0], i, 0)), blk],
            out_specs=blk),
        out_shape=jax.ShapeDtypeStruct(got.shape, got.dtype),
        compiler_params=_params("parallel", "parallel"), name=name)(place, g, got)


def _adamw_update(g, w_ref, m_ref, v_ref, g_ref, d_ref, nm_ref, nv_ref):
    nm = ADAM_B1 * m_ref[...] + (1.0 - ADAM_B1) * g
    nv = ADAM_B2 * v_ref[...] + (1.0 - ADAM_B2) * jnp.square(g)
    m_hat = nm / (1.0 - ADAM_B1 ** ADAM_STEP)
    v_hat = nv / (1.0 - ADAM_B2 ** ADAM_STEP)
    g_ref[...] = g
    d_ref[...] = -ADAM_LR * (m_hat / (jnp.sqrt(v_hat) + ADAM_EPS) + ADAM_WD * w_ref[...])
    nm_ref[...] = nm
    nv_ref[...] = nv


def _adamw_sharded(parts, w, m, v, place, name, ride=None):
    n_layers, r, c = w.shape
    tile = _row_tile(r)
    nb = r // tile
    counts = [1 + len(recvs) for _, recvs in parts]

    def body(place_ref, *refs):
        layer = pl.program_id(0)
        g, at = None, 0
        for l in range(n_layers):
            g_l = refs[at][0].astype(F32)
            for r_ref in refs[at + 1:at + counts[l]]:
                for k in range(r_ref.shape[0]):
                    g_l = g_l + r_ref[k].astype(F32)
            g = g_l if g is None else jnp.where(layer == l, g_l, g)
            at += counts[l]
        _adamw_update(g, *refs[at:])

    def part_specs(l, recvs):
        rows = lambda q, i: jnp.where(q < l, 0, jnp.where(q > l, nb - 1, i))
        return ([pl.BlockSpec((1, tile, c), lambda q, i, place_ref: (place_ref[1], rows(q, i), 0))] +
                [pl.BlockSpec((a.shape[0], tile, c), lambda q, i, place_ref: (0, rows(q, i), 0)) for a in recvs])

    in_specs, args = [], []
    for l, (s, recvs) in enumerate(parts):
        in_specs += part_specs(l, recvs)
        args += [s, *recvs]
    blk = pl.BlockSpec((None, tile, c), lambda q, i, place_ref: (q, i, 0))
    out = jax.ShapeDtypeStruct((n_layers, r, c), F32)
    return _pcall(body, ride, grid=(n_layers, nb), in_specs=in_specs + [blk, blk, blk], out_specs=[blk, blk, blk, blk],
                  out_shape=[out, out, out, out], semantics=("arbitrary", "arbitrary"), name=name, prefetch=True)(
                      place, *args, w, m, v)


def _adamw_small(repl_parts, vec_parts, w, m, v, place):
    n_r, n = len(repl_parts), len(w)
    shapes = [a.shape for a in w]

    def body(place_ref, *refs):
        parts, rest = refs[:n], refs[n:]
        for k in range(n):
            take = (lambda p: parts[k][p]) if k < n_r else (lambda p: parts[k][p, 0])
            g = take(0)
            for p in range(1, N_DEV):
                g = g + take(p)
            _adamw_update(g, rest[k], rest[n + k], rest[2 * n + k], *rest[3 * n + 4 * k:3 * n + 4 * k + 4])

    def whole(shape):
        return pl.BlockSpec(shape, lambda i, place_ref: (0,) * len(shape))

    def mine(shard):
        return pl.BlockSpec((N_DEV, 1) + shard, lambda i, place_ref: (0, place_ref[2]) + (0,) * len(shard))

    in_specs = [whole(a.shape) for a in repl_parts] + [mine(s) for s in shapes[n_r:]] + [whole(s) for s in shapes] * 3
    outs = pl.pallas_call(
        body,
        grid_spec=pltpu.PrefetchScalarGridSpec(
            num_scalar_prefetch=1, grid=(1,), in_specs=in_specs,
            out_specs=[whole(s) for s in shapes for _ in range(4)]),
        out_shape=[jax.ShapeDtypeStruct(s, F32) for s in shapes for _ in range(4)],
        compiler_params=_params("arbitrary"), name="adamw_small")(place, *repl_parts, *vec_parts, *w, *m, *v)
    return [outs[4 * k:4 * k + 4] for k in range(n)]


SHARDED = {"norm_w": 2, "w_in_even": 2, "gla_w_a_up": 2, "w_out_even": 1, "w_in_odd": 2, "conv_w": 2, "conv_b": 1,
           "lru_b_a": 1, "lru_b_x": 1, "lru_lambda": 1, "w_out_odd": 1, "w_mlp_up": 2, "w_mlp_down": 1}
REPLICATED = ["gla_b_a", "gla_norm_w", "fox_b_f", "rel_bias", "lru_w_a", "lru_w_x"]
WEIGHTS = ["norm_w", "w_in_even", "gla_w_a_up", "gla_b_a", "gla_norm_w", "fox_b_f", "w_out_even", "w_in_odd",
           "rel_bias", "conv_w", "conv_b", "lru_w_a", "lru_b_a", "lru_w_x", "lru_b_x", "lru_lambda", "w_out_odd",
           "w_mlp_up", "w_mlp_down"]
MATRICES = ("w_in_even", "w_out_even", "w_in_odd", "w_out_odd", "w_mlp_up", "w_mlp_down")
TRANSPOSED = ("w_in_even", "w_in_odd")
VECTORS = tuple(n for n in SHARDED if n not in MATRICES)
MATRIX_BLOCKS = (("w_in_even", 0), ("w_out_even", 0), ("w_in_odd", 0), ("w_out_odd", 0),
                 ("w_mlp_up", 0), ("w_mlp_up", 1), ("w_mlp_down", 0), ("w_mlp_down", 1))


def _join_shards(blocks, axis):
    moved = jnp.moveaxis(blocks, 0, axis)
    shape = moved.shape
    return moved.reshape(shape[:axis] + (shape[axis] * shape[axis + 1],) + shape[axis + 2:])


def _split_shards(full, axis):
    shape = full.shape
    cut = full.reshape(shape[:axis] + (N_DEV, shape[axis] // N_DEV) + shape[axis + 1:])
    return jnp.moveaxis(cut, axis, 0)


EVEN_SPLITS = (0, 256, 512, 1024, 1536, 1552, 2064, 2576, 3088, 3096)


def _even_in_split(wt):
    c = [wt[EVEN_SPLITS[k]:EVEN_SPLITS[k + 1]] for k in range(9)]
    gq, gk, gv, gr, ga, fq, fk, fv, ff = c
    padrows = lambda a: jnp.pad(a, ((0, LANES - a.shape[0]), (0, 0)))
    return jnp.concatenate([gq, gk, gv, fq, fk, fv], axis=0), jnp.concatenate([gr, padrows(ga), padrows(ff)], axis=0)


def _even_in_merge(dmm, dele):
    return jnp.concatenate([dmm[:1024], dele[:512], dele[512:512 + GLA_RANK], dmm[1024:2560],
                            dele[640:640 + ATT_HEADS]], axis=0)


def _forward_backward(x, target, shard, vec_shard, w, place):
    w = dict(w)
    g, dnorm, sums, recv = {}, {}, {}, {}
    nrm = lambda l, k: w["norm_w"][l, k][None, :]
    gather = lambda *keys: _gather_plan([shard[k] for k in keys])
    blocks = lambda r, c: (N_DEV, r // N_DEV, c)

    def pair_sum(key):
        sums[key] = _pair_sum(g[key], got[key], place, f"rs_pair_sum_{key[0]}_{key[1]}")

    got = {}

    def mlp_fwd(xin, layer, ride_up, ride_down, loss_of=None):
        up = _mm(xin, w["w_mlp_up"][layer], out_dtype=ACT_DTYPE, tm=TM_FWD, tn=D_FF // N_DEV, b_blocked=True,
                 a_norm=nrm(layer, 2), name=f"mlp_up_{layer}", ride=ride_up)
        (u, h), rode_up = up if ride_up is not None else (up, None)
        down = _mm(u, w["w_mlp_down"][layer], out_dtype=F32, tm=TM_DX // 2, tn=D_MODEL, a_sqrelu=True,
                   res_norm=(xin, nrm(layer, 3)), loss_of=loss_of, name=f"mlp_down_{layer}", ride=ride_down)
        (yv, *xout), rode_down = down if ride_down is not None else (down, None)
        return xout if loss_of is not None else xout[0], (xin, h, u, yv), rode_up, rode_down

    def mlp_bwd(dxout, saved, layer, ride):
        xin, h, u, yv = saved
        k_up, k_down = ("w_mlp_up", layer), ("w_mlp_down", layer)
        res = _mm(dxout, w["w_mlp_down"][layer], nt=True, out_dtype=ACT_DTYPE, tm=TM_DX, tn=TN, drelu_of=u,
                  a_norm_bwd=(yv, nrm(layer, 3)), name=f"mlp_down_dx_{layer}", ride=ride)
        (du, dy, dnorm[(layer, 3)]), rode = res if ride is not None else (res, None)
        g[k_down] = _mm(u, dy, ta=True, out_dtype=WIRE_DTYPE, tm=TM_DW, tn=TN, a_sqrelu=True,
                        name=f"mlp_down_dw_{layer}").reshape(blocks(D_FF, D_MODEL))
        g[k_up] = _mm(h, du, ta=True, out_dtype=WIRE_DTYPE, tm=TM_DW, tn=D_FF // N_DEV, out_blocked=True,
                      name=f"mlp_up_dw_{layer}")
        w_up = jnp.moveaxis(w["w_mlp_up"][layer], 0, 1).reshape(D_MODEL, D_FF)
        (dxin, dnorm[(layer, 2)]), (got[k_down], got[k_up]) = _mm(
            du, w_up, nt=True, out_dtype=F32, tm=TM_DX // 2, tn=D_MODEL, norm_bwd=(xin, nrm(layer, 2), dxout),
            name=f"mlp_up_dx_{layer}", ride=_sibling_plan([g[k_down], g[k_up]]))
        pair_sum(k_down)
        pair_sum(k_up)
        return dxin, rode

    first = _run_plan(_gather_plan([shard[("w_in_even", 0)]] + [vec_shard[n] for n in VECTORS]),
                      "weights_all_gather_first")
    w["w_in_even"] = first[0].reshape(-1, D_MODEL)
    for n, b in zip(VECTORS, first[1:]):
        w[n] = _join_shards(b, SHARDED[n])
    w["w_mlp_up"], w["w_mlp_down"] = [None] * DEPTH, [None] * DEPTH

    wmm_e, wel_e = _even_in_split(w["w_in_even"])
    w_up_pad = jnp.pad(w["gla_w_a_up"][0], ((0, LANES - GLA_RANK), (0, 0)))
    b_f_pad = jnp.pad(w["fox_b_f"], ((0, 0), (0, LANES - ATT_HEADS)))
    (pmm0, h0), (w_out_even,) = _mm(x, wmm_e, nt=True, out_dtype=ACT_DTYPE, tm=TM_FWD, tn=TN, a_norm=nrm(0, 0),
                                    name="in_even_mm", ride=gather(("w_out_even", 0)))
    pel0 = _mm(h0, wel_e, nt=True, out_dtype=F32, tm=TM_FWD, tn=768, name="in_even_el")
    (out_a, states), (w["w_mlp_up"][0],) = _gla_fwd(pmm0, pel0, w_up_pad, w["gla_b_a"], w["gla_norm_w"],
                                                    ride=gather(("w_mlp_up", 0)))
    cum, cum_t = _fox_gate_fwd(pel0, b_f_pad)
    (out_b, lse_b), (w_mlp_down0, w_in_odd) = _fox_fwd(pmm0, cum, cum_t,
                                                       ride=gather(("w_mlp_down", 0), ("w_in_odd", 0)))
    w["w_out_even"] = w_out_even.reshape(D_MODEL, D_MODEL)
    w["w_mlp_down"][0] = w_mlp_down0.reshape(D_FF, D_MODEL)
    mix_in0 = jnp.concatenate([out_a, out_b], axis=1)
    mix0, x1 = _mm(mix_in0, w["w_out_even"], out_dtype=F32, tm=TM_DX // 4, tn=D_MODEL, res_norm=(x, nrm(0, 1)),
                   name="out_even")
    x2, mlp0, _, (w["w_mlp_up"][1],) = mlp_fwd(x1, 0, None, gather(("w_mlp_up", 1)))
    w["w_in_odd"] = w_in_odd.reshape(-1, D_MODEL)

    w_in_o = w["w_in_odd"]
    n_mm_o = 3 * GROUP_WIDTH
    wa_bd, wx_bd = _block_diag_pairs(w["lru_w_a"][0]), _block_diag_pairs(w["lru_w_x"][0])
    base = _ca_bias_base(w["rel_bias"][0])
    pmm1, h1 = _mm(x2, w_in_o[:n_mm_o], nt=True, out_dtype=ACT_DTYPE, tm=TM_FWD, tn=TN, a_norm=nrm(1, 0),
                   name="in_odd_mm")
    pel1 = _mm(h1, w_in_o[n_mm_o:], nt=True, out_dtype=F32, tm=TM_FWD, tn=TN, name="in_odd_el")
    kp = jnp.pad(pmm1[:, GROUP_WIDTH:2 * GROUP_WIDTH], ((CA_LEFT, 0), (0, 0)))
    vp = jnp.pad(pmm1[:, 2 * GROUP_WIDTH:], ((CA_LEFT, 0), (0, 0)))
    (out_c, lse_c), (w_mlp_down1,) = _ca_fwd(pmm1, kp, vp, base, ride=gather(("w_mlp_down", 1)))
    w["w_mlp_down"][1] = w_mlp_down1.reshape(D_FF, D_MODEL)
    lru_args = (pel1, w["conv_w"][0], w["conv_b"], wa_bd, w["lru_b_a"], wx_bd, w["lru_b_x"], w["lru_lambda"])
    out_d, (w_out_odd,) = _lru_fwd(*lru_args, ride=gather(("w_out_odd", 0)))
    w["w_out_odd"] = w_out_odd.reshape(D_MODEL, D_MODEL)
    mix_in1 = jnp.concatenate([out_c, out_d], axis=1)
    mix1, x3 = _mm(mix_in1, w["w_out_odd"], out_dtype=F32, tm=TM_DX // 4, tn=D_MODEL, res_norm=(x2, nrm(1, 1)),
                   name="out_odd")
    (dx4, loss), mlp1, _, _ = mlp_fwd(x3, 1, None, None, loss_of=target)

    k_oo, k_io, k_oe, k_ie = ("w_out_odd", 0), ("w_in_odd", 0), ("w_out_even", 0), ("w_in_even", 0)
    mlp_keys = lambda l: [("w_mlp_down", l), ("w_mlp_up", l)]
    dx3, _ = mlp_bwd(dx4, mlp1, 1, None)
    dmix_in1, dmix1, dnorm[(1, 1)] = _mm(dx3, w["w_out_odd"], nt=True, out_dtype=F32, tm=TM_DX, tn=TN,
                                         a_norm_bwd=(mix1, nrm(1, 1)), name="out_odd_dx")
    g[k_oo] = _mm(mix_in1, dmix1, ta=True, out_dtype=WIRE_DTYPE, tm=TM_DW, tn=TN, name="out_odd_dw").reshape(
        blocks(D_MODEL, D_MODEL))
    (dq_c, dkp, dvp, dbase), rode = _ca_bwd(
        pmm1, kp, vp, base, lse_c, dmix_in1,
        ride=_join_plans(_chip_plan([sums[k] for k in mlp_keys(1)]), _sibling_plan([g[k_oo]])))
    recv.update(zip(mlp_keys(1), rode[:2]))
    got[k_oo] = rode[2]
    pair_sum(k_oo)
    (dgate, dxin, g_conv_w, g_conv_b, dwa_bd, g_lru_b_a, dwx_bd, g_lru_b_x, g_lru_lambda), (recv[k_oo],) = _lru_bwd(
        *lru_args, dmix_in1, ride=_chip_plan([sums[k_oo]]))
    dp1 = jnp.concatenate([dq_c, dkp[CA_LEFT:].astype(ACT_DTYPE), dvp[CA_LEFT:].astype(ACT_DTYPE), dgate, dxin], axis=1)
    g[k_io] = _mm(dp1, h1, ta=True, out_dtype=WIRE_DTYPE, tm=dp1.shape[1] // 2, tn=TN, name="in_odd_dw").reshape(
        blocks(dp1.shape[1], D_MODEL))
    (dx2, dnorm[(1, 0)]), (got[k_io],) = _mm(dp1, w_in_o, out_dtype=F32, tm=TM_DX // 2, tn=D_MODEL,
                                             norm_bwd=(x2, nrm(1, 0), dx3), name="in_odd_dx",
                                             ride=_sibling_plan([g[k_io]]))
    pair_sum(k_io)
    g["rel_bias"] = _ca_bias_base_grad(dbase)[None]
    g["conv_w"], g["conv_b"] = g_conv_w[None], g_conv_b
    g["lru_w_a"], g["lru_w_x"] = _block_diag_pairs_grad(dwa_bd)[None], _block_diag_pairs_grad(dwx_bd)[None]
    g["lru_b_a"], g["lru_b_x"], g["lru_lambda"] = g_lru_b_a, g_lru_b_x, g_lru_lambda

    dx1, (recv[k_io],) = mlp_bwd(dx2, mlp0, 0, _chip_plan([sums[k_io]]))
    dmix_in0, dmix0, dnorm[(0, 1)] = _mm(dx1, w["w_out_even"], nt=True, out_dtype=F32, tm=TM_DX, tn=TN,
                                         a_norm_bwd=(mix0, nrm(0, 1)), name="out_even_dx")
    g[k_oe] = _mm(mix_in0, dmix0, ta=True, out_dtype=WIRE_DTYPE, tm=TM_DW, tn=TN, name="out_even_dw").reshape(
        blocks(D_MODEL, D_MODEL))
    k_md0, k_mu0 = mlp_keys(0)
    (dq_a, dk_a, dv_a, dr_a, da_a, dw_up_pad, g_gla_b_a, g_gla_norm_w), (got[k_oe],) = _gla_bwd(
        pmm0, pel0, w_up_pad, w["gla_b_a"], w["gla_norm_w"], states, dmix_in0, ride=_sibling_plan([g[k_oe]]))
    pair_sum(k_oe)
    (dq_b, dk_b, dv_b, dcum_t, dcum_q), (recv[k_md0], recv[k_mu0], recv[k_oe]) = _fox_bwd(
        pmm0, cum, cum_t, lse_b, dmix_in0, ride=_chip_plan([sums[k_md0], sums[k_mu0], sums[k_oe]]))
    df_b, db_f = _fox_gate_bwd(pel0, b_f_pad, dcum_t, dcum_q)
    g["gla_w_a_up"] = dw_up_pad[:GLA_RANK][None]
    g["gla_b_a"], g["gla_norm_w"], g["fox_b_f"] = g_gla_b_a, g_gla_norm_w, db_f[:, :ATT_HEADS]
    dp0 = jnp.concatenate([dq_a, dk_a, dv_a, dq_b, dk_b.astype(ACT_DTYPE), dv_b.astype(ACT_DTYPE), dr_a, da_a, df_b],
                          axis=1)
    w_perm = jnp.concatenate([wmm_e, wel_e], axis=0)
    n_mm_e = wmm_e.shape[0]
    dw_perm = _mm(dp0, h0, ta=True, out_dtype=WIRE_DTYPE, tm=dp0.shape[1] // 2, tn=TN, name="in_even_dw")
    dw_even = _even_in_merge(dw_perm[:n_mm_e], dw_perm[n_mm_e:])
    g[k_ie] = dw_even.reshape(blocks(dw_even.shape[0], D_MODEL))
    dh0, (got[k_ie], *repl_parts) = _mm(
        dp0, w_perm, out_dtype=F32, tm=TM_DX, tn=TN, name="in_even_dx",
        ride=_join_plans(_sibling_plan([g[k_ie]]), _gather_plan([g[n] for n in REPLICATED])))
    pair_sum(k_ie)
    dx0, dnorm[(0, 0)] = _norm_bwd(dh0, x, nrm(0, 0), out_dtype=F32, add=dx1, name="norm_in_bwd_0")

    g["norm_w"] = jnp.stack([jnp.concatenate([dnorm[(l, k)] for k in range(4)], axis=0) for l in range(DEPTH)])
    recv[k_ie], losses, *vec_parts = _run_plan(
        _join_plans(_chip_plan([sums[k_ie]]),
                    _gather_plan([loss] + [_split_shards(g[n], SHARDED[n]) for n in VECTORS])), "last_exchanges")
    return losses, dx0, sums, recv, repl_parts, vec_parts


def kernel(x, norm_w, w_in_even, gla_w_a_up, gla_b_a, gla_norm_w, fox_b_f, w_out_even, w_in_odd, rel_bias, conv_w, conv_b, lru_w_a, lru_b_a, lru_w_x, lru_b_x, lru_lambda, w_out_odd, w_mlp_up, w_mlp_down, loss_target, m_norm_w, m_w_in_even, m_gla_w_a_up, m_gla_b_a, m_gla_norm_w, m_fox_b_f, m_w_out_even, m_w_in_odd, m_rel_bias, m_conv_w, m_conv_b, m_lru_w_a, m_lru_b_a, m_lru_w_x, m_lru_b_x, m_lru_lambda, m_w_out_odd, m_w_mlp_up, m_w_mlp_down, v_norm_w, v_w_in_even, v_gla_w_a_up, v_gla_b_a, v_gla_norm_w, v_fox_b_f, v_w_out_even, v_w_in_odd, v_rel_bias, v_conv_w, v_conv_b, v_lru_w_a, v_lru_b_a, v_lru_w_x, v_lru_b_x, v_lru_lambda, v_w_out_odd, v_w_mlp_up, v_w_mlp_down):
    wts = dict(zip(WEIGHTS, (norm_w, w_in_even, gla_w_a_up, gla_b_a, gla_norm_w, fox_b_f, w_out_even, w_in_odd, rel_bias,
                             conv_w, conv_b, lru_w_a, lru_b_a, lru_w_x, lru_b_x, lru_lambda, w_out_odd, w_mlp_up,
                             w_mlp_down)))
    mom = dict(zip(WEIGHTS, (m_norm_w, m_w_in_even, m_gla_w_a_up, m_gla_b_a, m_gla_norm_w, m_fox_b_f, m_w_out_even,
                             m_w_in_odd, m_rel_bias, m_conv_w, m_conv_b, m_lru_w_a, m_lru_b_a, m_lru_w_x, m_lru_b_x,
                             m_lru_lambda, m_w_out_odd, m_w_mlp_up, m_w_mlp_down)))
    var = dict(zip(WEIGHTS, (v_norm_w, v_w_in_even, v_gla_w_a_up, v_gla_b_a, v_gla_norm_w, v_fox_b_f, v_w_out_even,
                             v_w_in_odd, v_rel_bias, v_conv_w, v_conv_b, v_lru_w_a, v_lru_b_a, v_lru_w_x, v_lru_b_x,
                             v_lru_lambda, v_w_out_odd, v_w_mlp_up, v_w_mlp_down)))
    ax, ay, ac = lax.axis_index("x"), lax.axis_index("y"), lax.axis_index("c")
    place = jnp.stack([ac, 2 * ax + ay, 4 * ax + 2 * ay + ac]).astype(jnp.int32)

    shard = {(n, l): (wts[n][l].T if n in TRANSPOSED else wts[n][l]).astype(WIRE_DTYPE) for n, l in MATRIX_BLOCKS}
    losses, dx, sums, recv, repl_parts, vec_parts = _forward_backward(
        x[0], loss_target[0], shard, {n: wts[n] for n in VECTORS}, {n: wts[n] for n in REPLICATED}, place)
    loss = jnp.sum(losses[:, 0, 0])

    view = lambda n, a: jnp.swapaxes(a, 1, 2) if n in TRANSPOSED else a
    upd = {n: [view(n, o) for o in _adamw_sharded(
        [(sums[(n, l)], [recv[(n, l)]]) for l in range(wts[n].shape[0])], view(n, wts[n]), view(n, mom[n]),
        view(n, var[n]), place, f"adamw_{n}")] for n in MATRICES}
    small = REPLICATED + list(VECTORS)
    upd.update(zip(small, _adamw_small(repl_parts, vec_parts, [wts[n] for n in small], [mom[n] for n in small],
                                       [var[n] for n in small], place)))
    return (loss, dx[None], *[upd[n][kind] for kind in range(4) for n in WEIGHTS])
```
